```python
import jax, jax.numpy as jnp
from jax import lax
import numpy as np

D_MODEL = 1024
BATCH = 8
SEQ = 4096
DEPTH = 1

SSM_HEADS = 16
SSM_HEAD_DIM = 64
SSM_WIDTH = SSM_HEADS * SSM_HEAD_DIM
SSM_GROUPS = 2
SSM_STATE = 128
CONV_WIDTH = 4
CHUNK = 128
CONV_DIM = SSM_WIDTH + 2 * SSM_GROUPS * SSM_STATE
ATT_HEADS = 16
ATT_HEAD_DIM = 64
ATT_WIDTH = ATT_HEADS * ATT_HEAD_DIM
Q_BLOCK = 128
MIX_WIDTH = SSM_WIDTH + ATT_WIDTH
D_FF = 4 * D_MODEL
SPLITS = [SSM_WIDTH,
          SSM_WIDTH + CONV_DIM,
          SSM_WIDTH + CONV_DIM + SSM_HEADS,
          SSM_WIDTH + CONV_DIM + SSM_HEADS + ATT_WIDTH,
          SSM_WIDTH + CONV_DIM + SSM_HEADS + 2 * ATT_WIDTH,
          SSM_WIDTH + CONV_DIM + SSM_HEADS + 3 * ATT_WIDTH]
IN_COLS = SPLITS[-1] + ATT_HEADS
DEEPNORM_ALPHA = (2.0 * DEPTH) ** 0.25
DEEPNORM_BETA = (8.0 * DEPTH) ** -0.25
LN_EPS = 1e-5
RMS_EPS = 1e-5

kernel_name = "hymba_ssd_fox_deepnorm_adaln"


def layer_norm(x, g, b):
    xf = x.astype(jnp.float32)
    mu = jnp.mean(xf, axis=-1, keepdims=True)
    var = jnp.mean(jnp.square(xf - mu), axis=-1, keepdims=True)
    return ((xf - mu) * lax.rsqrt(var + LN_EPS) * g + b).astype(x.dtype)


def rms_norm(x, w):
    xf = x.astype(jnp.float32)
    return xf * lax.rsqrt(jnp.mean(xf * xf, axis=-1, keepdims=True) + RMS_EPS) * w


def causal_depthwise_conv(u, w, b):
    out = lax.conv_general_dilated(
        u, w[:, None, :].astype(u.dtype), window_strides=(1,),
        padding=[(CONV_WIDTH - 1, 0)], dimension_numbers=('NWC', 'WIO', 'NWC'),
        feature_group_count=u.shape[-1])
    return out + b


def segsum(a):
    cs = jnp.cumsum(a, axis=-1)
    diff = cs[..., :, None] - cs[..., None, :]
    t = a.shape[-1]
    mask = jnp.tril(jnp.ones((t, t), dtype=bool))
    return jnp.where(mask, diff, -jnp.inf)


def ssd_chunked(xh, dt, A, Bm, Cm):
    b, s, h, p = xh.shape
    g, n = Bm.shape[-2], Bm.shape[-1]
    e = h // g
    nc = s // CHUNK
    xc = (xh.astype(jnp.float32) * dt[..., None]).reshape(b, nc, CHUNK, g, e, p)
    Bc = Bm.astype(jnp.float32).reshape(b, nc, CHUNK, g, n)
    Cc = Cm.astype(jnp.float32).reshape(b, nc, CHUNK, g, n)
    a = (dt * A).reshape(b, nc, CHUNK, g, e).transpose(0, 3, 4, 1, 2)
    a_cs = jnp.cumsum(a, axis=-1)
    decay_in = jnp.exp(segsum(a))
    cb = jnp.einsum('bclgn,bcsgn->bgcls', Cc, Bc)
    scores = cb[:, :, None] * decay_in
    y_diag = jnp.einsum('bgecls,bcsgep->bclgep', scores, xc)
    decay_to_end = jnp.exp(a_cs[..., -1:] - a_cs).transpose(0, 3, 4, 1, 2)
    states = jnp.einsum('bclgn,bclgep->bcgepn', Bc, xc * decay_to_end[..., None])
    chunk_tot = jnp.pad(a_cs[..., -1], [(0, 0), (0, 0), (0, 0), (1, 0)])
    decay_chunk = jnp.exp(segsum(chunk_tot))
    states = jnp.concatenate([jnp.zeros_like(states[:, :1]), states], axis=1)
    new_states = jnp.einsum('bgezc,bcgepn->bzgepn', decay_chunk, states)
    prev_states = new_states[:, :-1]
    decay_out = jnp.exp(a_cs).transpose(0, 3, 4, 1, 2)
    y_off = jnp.einsum('bclgn,bcgepn->bclgep', Cc, prev_states) * decay_out[..., None]
    return (y_diag + y_off).reshape(b, s, h, p)


def forgetting_attention(q, k, v, log_f):
    s, d = q.shape[1], q.shape[-1]
    scale = d ** -0.5
    cum = jnp.cumsum(log_f, axis=1).transpose(0, 2, 1)
    outs = []
    for i in range(s // Q_BLOCK):
        q0, q1 = i * Q_BLOCK, (i + 1) * Q_BLOCK
        logits = jnp.einsum('bqhd,bkhd->bhqk', q[:, q0:q1], k[:, :q1],
                            preferred_element_type=jnp.float32) * scale
        logits = logits + (cum[:, :, q0:q1, None] - cum[:, :, None, :q1])
        mask = jnp.arange(q0, q1)[:, None] >= jnp.arange(q1)[None, :]
        logits = jnp.where(mask, logits, -jnp.inf)
        probs = jax.nn.softmax(logits, axis=-1)
        outs.append(jnp.einsum('bhqk,bkhd->bqhd', probs.astype(v.dtype), v[:, :q1]))
    return jnp.concatenate(outs, axis=1)


def hybrid_mixer(h, w_in, conv_w, conv_b, dt_bias, a_log, d_skip, ssm_norm_w, f_bias,
                 attn_norm_w, w_out):
    b, s, _ = h.shape
    proj = jnp.einsum('bsd,dk->bsk', h, w_in)
    z, xbc, dt_raw, q, k, v, f_raw = jnp.split(proj, SPLITS, axis=-1)
    xbc = jax.nn.silu(causal_depthwise_conv(xbc, conv_w, conv_b))
    xs, Bm, Cm = jnp.split(xbc, [SSM_WIDTH, SSM_WIDTH + SSM_GROUPS * SSM_STATE], axis=-1)
    xs = xs.reshape(b, s, SSM_HEADS, SSM_HEAD_DIM)
    Bm = Bm.reshape(b, s, SSM_GROUPS, SSM_STATE)
    Cm = Cm.reshape(b, s, SSM_GROUPS, SSM_STATE)
    dt = jax.nn.softplus(dt_raw.astype(jnp.float32) + dt_bias)
    A = -jnp.exp(a_log.astype(jnp.float32))
    y = ssd_chunked(xs, dt, A, Bm, Cm) + d_skip[:, None] * xs
    y_ssm = rms_norm(y.reshape(b, s, SSM_WIDTH) * jax.nn.silu(z), ssm_norm_w)
    q = q.reshape(b, s, ATT_HEADS, ATT_HEAD_DIM)
    k = k.reshape(b, s, ATT_HEADS, ATT_HEAD_DIM)
    v = v.reshape(b, s, ATT_HEADS, ATT_HEAD_DIM)
    log_f = jax.nn.log_sigmoid(f_raw.astype(jnp.float32) + f_bias)
    y_att = forgetting_attention(q, k, v, log_f).reshape(b, s, ATT_WIDTH)
    y_att = rms_norm(y_att, attn_norm_w)
    y_mix = jnp.concatenate([y_ssm, y_att.astype(y_ssm.dtype)], axis=-1)
    return jnp.einsum('bsk,kd->bsd', y_mix.astype(h.dtype), w_out)


def _fwd_setup_inputs(seed: int = 0) -> dict:
    key = jax.random.key(seed)
    ks = jax.random.split(key, 24)
    f32 = jnp.float32
    nrm = lambda k, shape, s: jax.random.normal(k, shape, f32) * s
    dt0 = jnp.exp(jax.random.uniform(ks[6], (DEPTH, SSM_HEADS), f32,
                                     np.log(1e-3).astype(np.float32), np.log(1e-1).astype(np.float32)))
    return {
        "x": nrm(ks[0], (BATCH, SEQ, D_MODEL), 1.0),
        "c": nrm(ks[1], (BATCH, D_MODEL), 1.0),
        "w_ada": nrm(ks[2], (DEPTH, D_MODEL, 6 * D_MODEL), 0.5 * D_MODEL ** -0.5),
        "b_ada": nrm(ks[3], (DEPTH, 6 * D_MODEL), 0.01),
        "w_in": nrm(ks[4], (DEPTH, D_MODEL, IN_COLS), D_MODEL ** -0.5),
        "conv_w": nrm(ks[5], (DEPTH, CONV_WIDTH, CONV_DIM), CONV_WIDTH ** -0.5),
        "conv_b": nrm(ks[7], (DEPTH, CONV_DIM), 0.01),
        "dt_bias": dt0 + jnp.log(-jnp.expm1(-dt0)),
        "a_log": jnp.log(jax.random.uniform(ks[8], (DEPTH, SSM_HEADS), f32, 1.0, 16.0)),
        "d_skip": 1.0 + nrm(ks[9], (DEPTH, SSM_HEADS), 0.1),
        "ssm_norm_w": 1.0 + nrm(ks[10], (DEPTH, SSM_WIDTH), 0.05),
        "f_bias": jax.random.uniform(ks[11], (DEPTH, ATT_HEADS), f32, 1.0, 4.0),
        "attn_norm_w": 1.0 + nrm(ks[12], (DEPTH, ATT_WIDTH), 0.05),
        "w_out": nrm(ks[13], (DEPTH, MIX_WIDTH, D_MODEL), DEEPNORM_BETA * MIX_WIDTH ** -0.5),
        "ln1_g": 1.0 + nrm(ks[14], (DEPTH, D_MODEL), 0.05),
        "ln1_b": nrm(ks[15], (DEPTH, D_MODEL), 0.01),
        "w_ff_in": nrm(ks[16], (DEPTH, D_MODEL, D_FF), D_MODEL ** -0.5),
        "w_ff_out": nrm(ks[17], (DEPTH, D_FF, D_MODEL), DEEPNORM_BETA * D_FF ** -0.5),
        "ln2_g": 1.0 + nrm(ks[18], (DEPTH, D_MODEL), 0.05),
        "ln2_b": nrm(ks[19], (DEPTH, D_MODEL), 0.01),
    }


def _fwd_reference(x, c, w_ada, b_ada, w_in, conv_w, conv_b, dt_bias, a_log, d_skip, ssm_norm_w,
              f_bias, attn_norm_w, w_out, ln1_g, ln1_b, w_ff_in, w_ff_out, ln2_g, ln2_b):
    c_act = jax.nn.silu(c)
    for l in range(DEPTH):
        mod = jnp.einsum('bd,de->be', c_act, w_ada[l]) + b_ada[l]
        sh1, sc1, g1, sh2, sc2, g2 = [m[:, None, :] for m in jnp.split(mod, 6, axis=-1)]
        h = x * (1.0 + sc1) + sh1
        y = hybrid_mixer(h, w_in[l], conv_w[l], conv_b[l], dt_bias[l], a_log[l], d_skip[l],
                         ssm_norm_w[l], f_bias[l], attn_norm_w[l], w_out[l])
        x = layer_norm(DEEPNORM_ALPHA * x + (1.0 + g1) * y, ln1_g[l], ln1_b[l])
        h = x * (1.0 + sc2) + sh2
        ff = jnp.einsum('bsf,fd->bsd',
                        jnp.square(jax.nn.relu(jnp.einsum('bsd,df->bsf', h, w_ff_in[l]))),
                        w_ff_out[l])
        x = layer_norm(DEEPNORM_ALPHA * x + (1.0 + g2) * ff, ln2_g[l], ln2_b[l])
    return x


import jax as _jax
import jax.numpy as _jnp

TWIN_FORMAT = 'train_step'
FWD_PARAMS = ['x', 'c', 'w_ada', 'b_ada', 'w_in', 'conv_w', 'conv_b', 'dt_bias', 'a_log', 'd_skip', 'ssm_norm_w', 'f_bias', 'attn_norm_w', 'w_out', 'ln1_g', 'ln1_b', 'w_ff_in', 'w_ff_out', 'ln2_g', 'ln2_b']
TWIN_WEIGHTS = ['w_ada', 'b_ada', 'w_in', 'conv_w', 'conv_b', 'dt_bias', 'a_log', 'd_skip', 'ssm_norm_w', 'f_bias', 'attn_norm_w', 'w_out', 'ln1_g', 'ln1_b', 'w_ff_in', 'w_ff_out', 'ln2_g', 'ln2_b']
TWIN_DIFF_INPUT = 'x'
TWIN_INPUTS = ['x', 'c', 'w_ada', 'b_ada', 'w_in', 'conv_w', 'conv_b', 'dt_bias', 'a_log', 'd_skip', 'ssm_norm_w', 'f_bias', 'attn_norm_w', 'w_out', 'ln1_g', 'ln1_b', 'w_ff_in', 'w_ff_out', 'ln2_g', 'ln2_b', 'loss_target', 'm_w_ada', 'm_b_ada', 'm_w_in', 'm_conv_w', 'm_conv_b', 'm_dt_bias', 'm_a_log', 'm_d_skip', 'm_ssm_norm_w', 'm_f_bias', 'm_attn_norm_w', 'm_w_out', 'm_ln1_g', 'm_ln1_b', 'm_w_ff_in', 'm_w_ff_out', 'm_ln2_g', 'm_ln2_b', 'v_w_ada', 'v_b_ada', 'v_w_in', 'v_conv_w', 'v_conv_b', 'v_dt_bias', 'v_a_log', 'v_d_skip', 'v_ssm_norm_w', 'v_f_bias', 'v_attn_norm_w', 'v_w_out', 'v_ln1_g', 'v_ln1_b', 'v_w_ff_in', 'v_w_ff_out', 'v_ln2_g', 'v_ln2_b']
TWIN_OUTPUTS = ['loss', 'grad_x', 'grad_w_ada', 'grad_b_ada', 'grad_w_in', 'grad_conv_w', 'grad_conv_b', 'grad_dt_bias', 'grad_a_log', 'grad_d_skip', 'grad_ssm_norm_w', 'grad_f_bias', 'grad_attn_norm_w', 'grad_w_out', 'grad_ln1_g', 'grad_ln1_b', 'grad_w_ff_in', 'grad_w_ff_out', 'grad_ln2_g', 'grad_ln2_b', 'delta_w_ada', 'delta_b_ada', 'delta_w_in', 'delta_conv_w', 'delta_conv_b', 'delta_dt_bias', 'delta_a_log', 'delta_d_skip', 'delta_ssm_norm_w', 'delta_f_bias', 'delta_attn_norm_w', 'delta_w_out', 'delta_ln1_g', 'delta_ln1_b', 'delta_w_ff_in', 'delta_w_ff_out', 'delta_ln2_g', 'delta_ln2_b', 'new_m_w_ada', 'new_m_b_ada', 'new_m_w_in', 'new_m_conv_w', 'new_m_conv_b', 'new_m_dt_bias', 'new_m_a_log', 'new_m_d_skip', 'new_m_ssm_norm_w', 'new_m_f_bias', 'new_m_attn_norm_w', 'new_m_w_out', 'new_m_ln1_g', 'new_m_ln1_b', 'new_m_w_ff_in', 'new_m_w_ff_out', 'new_m_ln2_g', 'new_m_ln2_b', 'new_v_w_ada', 'new_v_b_ada', 'new_v_w_in', 'new_v_conv_w', 'new_v_conv_b', 'new_v_dt_bias', 'new_v_a_log', 'new_v_d_skip', 'new_v_ssm_norm_w', 'new_v_f_bias', 'new_v_attn_norm_w', 'new_v_w_out', 'new_v_ln1_g', 'new_v_ln1_b', 'new_v_w_ff_in', 'new_v_w_ff_out', 'new_v_ln2_g', 'new_v_ln2_b']
TWIN_LEAF_KINDS = {'loss': 'loss', 'grad_x': 'grad_x', 'grad_w_ada': 'grad_w', 'grad_b_ada': 'grad_w', 'grad_w_in': 'grad_w', 'grad_conv_w': 'grad_w', 'grad_conv_b': 'grad_w', 'grad_dt_bias': 'grad_w', 'grad_a_log': 'grad_w', 'grad_d_skip': 'grad_w', 'grad_ssm_norm_w': 'grad_w', 'grad_f_bias': 'grad_w', 'grad_attn_norm_w': 'grad_w', 'grad_w_out': 'grad_w', 'grad_ln1_g': 'grad_w', 'grad_ln1_b': 'grad_w', 'grad_w_ff_in': 'grad_w', 'grad_w_ff_out': 'grad_w', 'grad_ln2_g': 'grad_w', 'grad_ln2_b': 'grad_w', 'delta_w_ada': 'delta_w', 'delta_b_ada': 'delta_w', 'delta_w_in': 'delta_w', 'delta_conv_w': 'delta_w', 'delta_conv_b': 'delta_w', 'delta_dt_bias': 'delta_w', 'delta_a_log': 'delta_w', 'delta_d_skip': 'delta_w', 'delta_ssm_norm_w': 'delta_w', 'delta_f_bias': 'delta_w', 'delta_attn_norm_w': 'delta_w', 'delta_w_out': 'delta_w', 'delta_ln1_g': 'delta_w', 'delta_ln1_b': 'delta_w', 'delta_w_ff_in': 'delta_w', 'delta_w_ff_out': 'delta_w', 'delta_ln2_g': 'delta_w', 'delta_ln2_b': 'delta_w', 'new_m_w_ada': 'new_m', 'new_m_b_ada': 'new_m', 'new_m_w_in': 'new_m', 'new_m_conv_w': 'new_m', 'new_m_conv_b': 'new_m', 'new_m_dt_bias': 'new_m', 'new_m_a_log': 'new_m', 'new_m_d_skip': 'new_m', 'new_m_ssm_norm_w': 'new_m', 'new_m_f_bias': 'new_m', 'new_m_attn_norm_w': 'new_m', 'new_m_w_out': 'new_m', 'new_m_ln1_g': 'new_m', 'new_m_ln1_b': 'new_m', 'new_m_w_ff_in': 'new_m', 'new_m_w_ff_out': 'new_m', 'new_m_ln2_g': 'new_m', 'new_m_ln2_b': 'new_m', 'new_v_w_ada': 'new_v', 'new_v_b_ada': 'new_v', 'new_v_w_in': 'new_v', 'new_v_conv_w': 'new_v', 'new_v_conv_b': 'new_v', 'new_v_dt_bias': 'new_v', 'new_v_a_log': 'new_v', 'new_v_d_skip': 'new_v', 'new_v_ssm_norm_w': 'new_v', 'new_v_f_bias': 'new_v', 'new_v_attn_norm_w': 'new_v', 'new_v_w_out': 'new_v', 'new_v_ln1_g': 'new_v', 'new_v_ln1_b': 'new_v', 'new_v_w_ff_in': 'new_v', 'new_v_w_ff_out': 'new_v', 'new_v_ln2_g': 'new_v', 'new_v_ln2_b': 'new_v'}


def _forward(args):
    return _fwd_reference(*[args[k] for k in FWD_PARAMS])


def _output_shape():
    def fwd():
        inp = _fwd_setup_inputs(0)
        return _fwd_reference(*[inp[k] for k in FWD_PARAMS])
    out = _jax.eval_shape(fwd)
    return out.shape, out.dtype

N_MICROBATCH = 1
ADAM_LR = 0.001
ADAM_B1 = 0.9
ADAM_B2 = 0.999
ADAM_EPS = 1e-08
ADAM_WD = 0.01
ADAM_STEP = 10
PER_EXAMPLE_BATCH_AXIS = {'x': 0, 'c': 0, 'loss_target': 0}
SHARED_INPUTS = []
_WEIGHT_DTYPES = {'w_ada': _jnp.float32, 'b_ada': _jnp.float32, 'w_in': _jnp.float32, 'conv_w': _jnp.float32, 'conv_b': _jnp.float32, 'dt_bias': _jnp.float32, 'a_log': _jnp.float32, 'd_skip': _jnp.float32, 'ssm_norm_w': _jnp.float32, 'f_bias': _jnp.float32, 'attn_norm_w': _jnp.float32, 'w_out': _jnp.float32, 'ln1_g': _jnp.float32, 'ln1_b': _jnp.float32, 'w_ff_in': _jnp.float32, 'w_ff_out': _jnp.float32, 'ln2_g': _jnp.float32, 'ln2_b': _jnp.float32}
MOMENT_SCALE = {'w_ada': 1.411707e-01, 'b_ada': 4.623986e-01, 'w_in': 5.941402e-02, 'conv_w': 5.528201e-02, 'conv_b': 7.793405e-02, 'dt_bias': 2.127115e-01, 'a_log': 2.136354e-01, 'd_skip': 2.555922e-01, 'ssm_norm_w': 6.427578e-02, 'f_bias': 2.149608e-01, 'attn_norm_w': 8.320334e-02, 'w_out': 1.910144e-01, 'ln1_g': 1.890348e+00, 'ln1_b': 6.803751e-01, 'w_ff_in': 6.841010e-02, 'w_ff_out': 3.470412e-01, 'ln2_g': 3.251295e+01, 'ln2_b': 8.572946e+00}


def _to_microbatches(a, axis):
    t = _jnp.moveaxis(a, axis, 0)
    t = t.reshape((N_MICROBATCH, t.shape[0] // N_MICROBATCH) + t.shape[1:])
    return _jnp.moveaxis(t, 1, axis + 1)


def setup_inputs(seed: int = 0) -> dict:
    inp = _fwd_setup_inputs(seed)
    key = _jax.random.fold_in(_jax.random.key(seed), 7919)
    shape, _ = _output_shape()
    out = dict(inp)
    out["loss_target"] = _jax.random.normal(_jax.random.fold_in(key, 0), shape, _jnp.float32)
    for i, name in enumerate(TWIN_WEIGHTS):
        w = inp[name].astype(_jnp.float32)
        if MOMENT_SCALE is None:
            s = _jnp.sqrt(_jnp.mean(_jnp.square(w)) + 1e-30)
        else:
            s = MOMENT_SCALE[name]
        km, kv = _jax.random.split(_jax.random.fold_in(key, i + 1))
        out[name] = w
        out["m_" + name] = s * _jax.random.normal(km, w.shape, _jnp.float32)
        out["v_" + name] = (s * s) * _jax.random.uniform(kv, w.shape, _jnp.float32, 0.5, 1.5)
    if N_MICROBATCH > 1:
        for name, axis in PER_EXAMPLE_BATCH_AXIS.items():
            out[name] = _to_microbatches(out[name], axis)
    return {'x': out['x'], 'c': out['c'], 'w_ada': out['w_ada'], 'b_ada': out['b_ada'], 'w_in': out['w_in'], 'conv_w': out['conv_w'], 'conv_b': out['conv_b'], 'dt_bias': out['dt_bias'], 'a_log': out['a_log'], 'd_skip': out['d_skip'], 'ssm_norm_w': out['ssm_norm_w'], 'f_bias': out['f_bias'], 'attn_norm_w': out['attn_norm_w'], 'w_out': out['w_out'], 'ln1_g': out['ln1_g'], 'ln1_b': out['ln1_b'], 'w_ff_in': out['w_ff_in'], 'w_ff_out': out['w_ff_out'], 'ln2_g': out['ln2_g'], 'ln2_b': out['ln2_b'], 'loss_target': out['loss_target'], 'm_w_ada': out['m_w_ada'], 'm_b_ada': out['m_b_ada'], 'm_w_in': out['m_w_in'], 'm_conv_w': out['m_conv_w'], 'm_conv_b': out['m_conv_b'], 'm_dt_bias': out['m_dt_bias'], 'm_a_log': out['m_a_log'], 'm_d_skip': out['m_d_skip'], 'm_ssm_norm_w': out['m_ssm_norm_w'], 'm_f_bias': out['m_f_bias'], 'm_attn_norm_w': out['m_attn_norm_w'], 'm_w_out': out['m_w_out'], 'm_ln1_g': out['m_ln1_g'], 'm_ln1_b': out['m_ln1_b'], 'm_w_ff_in': out['m_w_ff_in'], 'm_w_ff_out': out['m_w_ff_out'], 'm_ln2_g': out['m_ln2_g'], 'm_ln2_b': out['m_ln2_b'], 'v_w_ada': out['v_w_ada'], 'v_b_ada': out['v_b_ada'], 'v_w_in': out['v_w_in'], 'v_conv_w': out['v_conv_w'], 'v_conv_b': out['v_conv_b'], 'v_dt_bias': out['v_dt_bias'], 'v_a_log': out['v_a_log'], 'v_d_skip': out['v_d_skip'], 'v_ssm_norm_w': out['v_ssm_norm_w'], 'v_f_bias': out['v_f_bias'], 'v_attn_norm_w': out['v_attn_norm_w'], 'v_w_out': out['v_w_out'], 'v_ln1_g': out['v_ln1_g'], 'v_ln1_b': out['v_ln1_b'], 'v_w_ff_in': out['v_w_ff_in'], 'v_w_ff_out': out['v_w_ff_out'], 'v_ln2_g': out['v_ln2_g'], 'v_ln2_b': out['v_ln2_b']}


def _loss(weights, diff, rest, loss_target):
    with _jax.named_scope("forward"):
        args = {**rest, TWIN_DIFF_INPUT: diff, **{k: w.astype(_WEIGHT_DTYPES[k]) for k, w in weights.items()}}
        y = _forward(args)
    with _jax.named_scope("loss_head"):
        err = _jnp.square(y.astype(_jnp.float32) - loss_target)
        return 0.5 * _jnp.sum(_jnp.mean(err, axis=-1)) if err.ndim else 0.5 * err


def _adamw(w, g, m, v):
    m = ADAM_B1 * m + (1.0 - ADAM_B1) * g
    v = ADAM_B2 * v + (1.0 - ADAM_B2) * _jnp.square(g)
    m_hat = m / (1.0 - ADAM_B1 ** ADAM_STEP)
    v_hat = v / (1.0 - ADAM_B2 ** ADAM_STEP)
    delta = -ADAM_LR * (m_hat / (_jnp.sqrt(v_hat) + ADAM_EPS) + ADAM_WD * w)
    return delta, m, v


def reference(x, c, w_ada, b_ada, w_in, conv_w, conv_b, dt_bias, a_log, d_skip, ssm_norm_w, f_bias, attn_norm_w, w_out, ln1_g, ln1_b, w_ff_in, w_ff_out, ln2_g, ln2_b, loss_target, m_w_ada, m_b_ada, m_w_in, m_conv_w, m_conv_b, m_dt_bias, m_a_log, m_d_skip, m_ssm_norm_w, m_f_bias, m_attn_norm_w, m_w_out, m_ln1_g, m_ln1_b, m_w_ff_in, m_w_ff_out, m_ln2_g, m_ln2_b, v_w_ada, v_b_ada, v_w_in, v_conv_w, v_conv_b, v_dt_bias, v_a_log, v_d_skip, v_ssm_norm_w, v_f_bias, v_attn_norm_w, v_w_out, v_ln1_g, v_ln1_b, v_w_ff_in, v_w_ff_out, v_ln2_g, v_ln2_b):
    given = dict(x=x, c=c, w_ada=w_ada, b_ada=b_ada, w_in=w_in, conv_w=conv_w, conv_b=conv_b, dt_bias=dt_bias, a_log=a_log, d_skip=d_skip, ssm_norm_w=ssm_norm_w, f_bias=f_bias, attn_norm_w=attn_norm_w, w_out=w_out, ln1_g=ln1_g, ln1_b=ln1_b, w_ff_in=w_ff_in, w_ff_out=w_ff_out, ln2_g=ln2_g, ln2_b=ln2_b, loss_target=loss_target, m_w_ada=m_w_ada, m_b_ada=m_b_ada, m_w_in=m_w_in, m_conv_w=m_conv_w, m_conv_b=m_conv_b, m_dt_bias=m_dt_bias, m_a_log=m_a_log, m_d_skip=m_d_skip, m_ssm_norm_w=m_ssm_norm_w, m_f_bias=m_f_bias, m_attn_norm_w=m_attn_norm_w, m_w_out=m_w_out, m_ln1_g=m_ln1_g, m_ln1_b=m_ln1_b, m_w_ff_in=m_w_ff_in, m_w_ff_out=m_w_ff_out, m_ln2_g=m_ln2_g, m_ln2_b=m_ln2_b, v_w_ada=v_w_ada, v_b_ada=v_b_ada, v_w_in=v_w_in, v_conv_w=v_conv_w, v_conv_b=v_conv_b, v_dt_bias=v_dt_bias, v_a_log=v_a_log, v_d_skip=v_d_skip, v_ssm_norm_w=v_ssm_norm_w, v_f_bias=v_f_bias, v_attn_norm_w=v_attn_norm_w, v_w_out=v_w_out, v_ln1_g=v_ln1_g, v_ln1_b=v_ln1_b, v_w_ff_in=v_w_ff_in, v_w_ff_out=v_w_ff_out, v_ln2_g=v_ln2_g, v_ln2_b=v_ln2_b)
    weights = {n: given[n] for n in TWIN_WEIGHTS}
    shared = {n: given[n] for n in SHARED_INPUTS}
    per_example = {n: given[n] for n in ['x', 'c']}
    grad_fn = _jax.value_and_grad(_loss, argnums=(0, 1))

    def one_microbatch(ex, loss_target):
        ex = dict(ex)
        diff = ex.pop(TWIN_DIFF_INPUT)
        return grad_fn(weights, diff, {**shared, **ex}, loss_target)

    if N_MICROBATCH == 1:
        loss, (grad_w, grad_x) = one_microbatch(per_example, given["loss_target"])
    else:
        def body(carry, xs):
            loss_sum, grad_sum = carry
            l_k, (gw_k, gx_k) = one_microbatch(xs[0], xs[1])
            with _jax.named_scope("update"):
                return (loss_sum + l_k, _jax.tree.map(_jnp.add, grad_sum, gw_k)), gx_k

        init = (_jnp.zeros((), _jnp.float32), _jax.tree.map(_jnp.zeros_like, weights))
        (loss, grad_w), grad_x = _jax.lax.scan(body, init, (per_example, given["loss_target"]))
    with _jax.named_scope("update"):
        delta_w, new_m, new_v = {}, {}, {}
        for n in TWIN_WEIGHTS:
            delta_w[n], new_m[n], new_v[n] = _adamw(weights[n], grad_w[n], given["m_" + n], given["v_" + n])
    return (loss, grad_x, *[grad_w[n] for n in TWIN_WEIGHTS], *[delta_w[n] for n in TWIN_WEIGHTS],
            *[new_m[n] for n in TWIN_WEIGHTS], *[new_v[n] for n in TWIN_WEIGHTS])
```

```python
import functools

import jax
import jax.numpy as jnp
from jax import lax
from jax.experimental import pallas as pl
from jax.experimental.pallas import tpu as pltpu

F32, BF16 = jnp.float32, jnp.bfloat16

N_DEV = 8
D = 1024
NH, HD = 16, 64
NSTATE = 128
CHUNK = 128
HG = 8
DFF = 4096
ALPHA = 2.0 ** 0.25
EPS = 1e-5
ATT_SCALE = HD ** -0.5

OFF_Z, OFF_XS, OFF_Q, OFF_K, OFF_V, OFF_BC, OFF_DTF = 0, 1024, 2048, 3072, 4096, 5120, 5632
PCOLS = 5760
W_Z, W_XS, W_BC, W_DT, W_Q, W_K, W_V, W_F = 0, 1024, 2048, 2560, 2576, 3600, 4624, 5648
IN_COLS = 5664

ADAM_LR, ADAM_B1, ADAM_B2, ADAM_EPS, ADAM_WD, ADAM_STEP = 0.001, 0.9, 0.999, 1e-08, 0.01, 10

VMEM_LIMIT = 56 << 20

NN = (((1,), (0,)), ((), ()))
NT = (((1,), (1,)), ((), ()))
TN = (((0,), (0,)), ((), ()))


def _dot(a, b, dims=NN):
    return lax.dot_general(a, b, dims, preferred_element_type=F32)


def _bdot(a, b, dims=NN):
    return _dot(a.astype(BF16), b.astype(BF16), dims)


def _split3(v):
    parts, rest = [], v
    for _ in range(3):
        p = rest.astype(BF16)
        parts.append(p)
        rest = rest - p.astype(F32)
    return parts


def _sel_left(m01, v):
    return sum(_dot(m01, p) for p in _split3(v))


def _sel_right(v, m01, dims=NN):
    return sum(_dot(p, m01, dims) for p in _split3(v))


def _iota(shape, dim):
    return lax.broadcasted_iota(jnp.int32, shape, dim)


def _tri_lower(n):
    return (_iota((n, n), 1) <= _iota((n, n), 0)).astype(BF16)


def _tri_upper(n):
    return (_iota((n, n), 1) >= _iota((n, n), 0)).astype(BF16)


def _head_expand():
    return (lax.shift_right_logical(_iota((128, D), 1), 6) == _iota((128, D), 0)).astype(BF16)


def _head_reduce():
    return (lax.shift_right_logical(_iota((D, 128), 0), 6) == _iota((D, 128), 1)).astype(BF16)


def _sigmoid(x):
    return 1.0 / (1.0 + jnp.exp(-x))


def _silu(x):
    return x * _sigmoid(x)


def _dsilu(x):
    s = _sigmoid(x)
    return s * (1.0 + x * (1.0 - s))


def _softplus(x):
    return jnp.maximum(x, 0.0) + jnp.log(1.0 + jnp.exp(-jnp.abs(x)))


def _log_sigmoid(x):
    return jnp.minimum(x, 0.0) - jnp.log(1.0 + jnp.exp(-jnp.abs(x)))


def _params(sem):
    return pltpu.CompilerParams(dimension_semantics=sem, vmem_limit_bytes=VMEM_LIMIT)


def _mm_nn(name, a, b, *, tm, tn, tk, out_dtype, pro=None, aux=()):
    m, k_all = a.shape
    n = b.shape[1]
    nk = k_all // tk
    n_aux = len(aux)

    def body(a_ref, b_ref, *rest):
        aux_refs, o_ref = rest[:n_aux], rest[n_aux]
        at = a_ref[...]
        if pro is not None:
            at = pro(at, *[r[...] for r in aux_refs])
        part = _bdot(at, b_ref[...])
        if nk == 1:
            o_ref[...] = part.astype(out_dtype)
            return
        acc_ref = rest[n_aux + 1]
        kk = pl.program_id(2)

        @pl.when(kk == 0)
        def _():
            acc_ref[...] = part

        @pl.when(kk > 0)
        def _():
            acc_ref[...] += part

        @pl.when(kk == nk - 1)
        def _():
            o_ref[...] = acc_ref[...].astype(out_dtype)

    return pl.pallas_call(
        body, name=name,
        grid=(m // tm, n // tn, nk),
        in_specs=[pl.BlockSpec((tm, tk), lambda i, j, k: (i, k)),
                  pl.BlockSpec((tk, tn), lambda i, j, k: (k, j))]
        + [pl.BlockSpec((1, tk), lambda i, j, k: (0, k)) for _ in aux],
        out_specs=pl.BlockSpec((tm, tn), lambda i, j, k: (i, j)),
        out_shape=jax.ShapeDtypeStruct((m, n), out_dtype),
        scratch_shapes=[] if nk == 1 else [pltpu.VMEM((tm, tn), F32)],
        compiler_params=_params(("parallel", "parallel", "arbitrary")),
    )(a, b, *aux)


def _mm_nt(name, a_list, b_list, *, n, tm, tn, out_dtype, epi=None, epi_aux=()):
    m = a_list[0].shape[0]
    n_op = len(a_list)
    n_epi = len(epi_aux)

    def body(*refs):
        a_refs, b_refs = refs[:n_op], refs[n_op:2 * n_op]
        e_refs, o_ref = refs[2 * n_op:2 * n_op + n_epi], refs[2 * n_op + n_epi]
        acc = None
        for a_ref, b_ref in zip(a_refs, b_refs):
            part = _bdot(a_ref[...], b_ref[...], NT)
            acc = part if acc is None else acc + part
        if epi is not None:
            acc = epi(acc, *[r[...] for r in e_refs])
        o_ref[...] = acc.astype(out_dtype)

    in_specs = [pl.BlockSpec((tm, a.shape[1]), lambda i, j: (i, 0)) for a in a_list]
    in_specs += [pl.BlockSpec((tn, w), functools.partial(lambda i, j, cb: (j, cb), cb=cb)) for (_, w, cb) in b_list]
    in_specs += [pl.BlockSpec((tm, tn), lambda i, j: (i, j)) for _ in epi_aux]
    return pl.pallas_call(
        body, name=name,
        grid=(m // tm, n // tn),
        in_specs=in_specs,
        out_specs=pl.BlockSpec((tm, tn), lambda i, j: (i, j)),
        out_shape=jax.ShapeDtypeStruct((m, n), out_dtype),
        compiler_params=_params(("parallel", "parallel")),
    )(*a_list, *[b for (b, _, _) in b_list], *epi_aux)


def _mm_tn(name, a, b, *, tm, tn, ts, pro=None, aux=()):
    s_all, ka = a.shape
    nb = b.shape[1]
    n_aux = len(aux)

    def body(a_ref, b_ref, *rest):
        aux_refs, o_ref = rest[:n_aux], rest[n_aux]
        at = a_ref[...]
        if pro is not None:
            at = pro(at, *[r[...] for r in aux_refs])
        part = _bdot(at, b_ref[...], TN)
        ss = pl.program_id(2)

        @pl.when(ss == 0)
        def _():
            o_ref[...] = part

        @pl.when(ss > 0)
        def _():
            o_ref[...] += part

    return pl.pallas_call(
        body, name=name,
        grid=(ka // tm, nb // tn, s_all // ts),
        in_specs=[pl.BlockSpec((ts, tm), lambda i, j, s: (s, i)),
                  pl.BlockSpec((ts, tn), lambda i, j, s: (s, j))]
        + [pl.BlockSpec((1, tm), lambda i, j, s: (0, i)) for _ in aux],
        out_specs=pl.BlockSpec((tm, tn), lambda i, j, s: (i, j)),
        out_shape=jax.ShapeDtypeStruct((ka, nb), F32),
        compiler_params=_params(("parallel", "parallel", "arbitrary")),
    )(a, b, *aux)


def _rowk(name, fn, n_rows, tr, rows, fulls, outs, accs, reverse=False):
    n = n_rows // tr
    n_row, n_full, n_out, n_acc = len(rows), len(fulls), len(outs), len(accs)

    def pos(i):
        return (n - 1 - i) if reverse else i

    def body(*refs):
        row_refs = refs[:n_row]
        full_refs = refs[n_row:n_row + n_full]
        out_refs = refs[n_row + n_full:n_row + n_full + n_out]
        acc_refs = refs[n_row + n_full + n_out:]
        i = pl.program_id(0)

        @pl.when(i == 0)
        def _():
            for r in acc_refs:
                r[...] = jnp.zeros(r.shape, r.dtype)

        res = fn(pos(i), *[r[...] for r in row_refs], *[r[...] for r in full_refs], *[r[...] for r in acc_refs])
        for r, v in zip(out_refs + acc_refs, res):
            r[...] = v.astype(r.dtype)

    def row_map(i, cb, shift):
        return (jnp.clip(pos(i) + shift, 0, n - 1), cb)

    in_specs = [pl.BlockSpec((tr, w), functools.partial(row_map, cb=cb, shift=sh)) for (_, w, cb, sh) in rows]
    in_specs += [pl.BlockSpec(f.shape, functools.partial(lambda i, nd: (0,) * nd, nd=f.ndim)) for f in fulls]
    out_specs = [pl.BlockSpec((tr, w), lambda i: (pos(i), 0)) for (w, _) in outs]
    out_specs += [pl.BlockSpec((r, w), lambda i: (0, 0)) for (r, w) in accs]
    out_shape = [jax.ShapeDtypeStruct((n_rows, w), dt) for (w, dt) in outs]
    out_shape += [jax.ShapeDtypeStruct((r, w), F32) for (r, w) in accs]
    return pl.pallas_call(
        body, name=name, grid=(n,), in_specs=in_specs, out_specs=out_specs, out_shape=out_shape,
        compiler_params=_params(("arbitrary",)),
    )(*[a for (a, _, _, _) in rows], *fulls)


def _colsum(x):
    return jnp.sum(x, axis=0, keepdims=True)


def _mean(x):
    return jnp.mean(x, axis=-1, keepdims=True)


def _modulate(x, sc, sh):
    return x * (1.0 + sc) + sh


def _shift_down(cur, prev, j):
    row = _iota(cur.shape, 0)
    return jnp.where(row < j, pltpu.roll(prev, j, 0), pltpu.roll(cur, j, 0))


def _shift_up(cur, nxt, j):
    tr = cur.shape[0]
    row = _iota(cur.shape, 0)
    return jnp.where(row < tr - j, pltpu.roll(cur, tr - j, 0), pltpu.roll(nxt, tr - j, 0))


def _conv(cur, prev, w, b):
    out = cur * w[3:4] + b
    for j in (1, 2, 3):
        out = out + _shift_down(cur, prev, j) * w[3 - j:4 - j]
    return out


def _conv_fwd(p, w_xs, b_xs, w_bc, b_bc, s):
    def fn(pos, xs, xs_prev, bc, bc_prev, w_xs, b_xs, w_bc, b_bc):
        first = pos == 0
        xs_prev = jnp.where(first, 0.0, xs_prev)
        bc_prev = jnp.where(first, 0.0, bc_prev)
        return _silu(_conv(xs, xs_prev, w_xs, b_xs)), _silu(_conv(bc, bc_prev, w_bc, b_bc))

    return _rowk("conv_fwd", fn, s, 256,
                 [(p, D, OFF_XS // D, 0), (p, D, OFF_XS // D, -1), (p, 512, OFF_BC // 512, 0), (p, 512, OFF_BC // 512, -1)],
                 [w_xs, b_xs, w_bc, b_bc], [(D, F32), (512, F32)], [])


def _conv_bwd(dxs_a, dbc_a, p, w_xs, b_xs, w_bc, b_bc, s):
    tr = 256
    n = s // tr

    def fn(pos, da1, da1n, x1, x1p, x1n, da2, da2n, x2, x2p, x2n, w1, b1, w2, b2, aw1, ab1, aw2, ab2):
        dx1, dw1, db1 = _conv_bwd_fn(pos, n, da1, da1n, x1, x1p, x1n, w1, b1)
        dx2, dw2, db2 = _conv_bwd_fn(pos, n, da2, da2n, x2, x2p, x2n, w2, b2)
        return dx1, dx2, aw1 + dw1, ab1 + db1, aw2 + dw2, ab2 + db2

    cx, cb = OFF_XS // D, OFF_BC // 512
    return _rowk("conv_bwd", fn, s, tr,
                 [(dxs_a, D, 0, 0), (dxs_a, D, 0, 1), (p, D, cx, 0), (p, D, cx, -1), (p, D, cx, 1),
                  (dbc_a, 512, 0, 0), (dbc_a, 512, 0, 1), (p, 512, cb, 0), (p, 512, cb, -1), (p, 512, cb, 1)],
                 [w_xs, b_xs, w_bc, b_bc], [(D, BF16), (512, BF16)], [(8, D), (1, D), (8, 512), (1, 512)])


def _conv_bwd_fn(pos, n, da, da_next, x, x_prev, x_next, w, b):
    first, last = pos == 0, pos == n - 1
    x_prev = jnp.where(first, 0.0, x_prev)
    dc = da * _dsilu(_conv(x, x_prev, w, b))
    dc_next = jnp.where(last, 0.0, da_next * _dsilu(_conv(x_next, x, w, b)))
    dx = dc * w[3:4]
    dws = [None] * 4
    dws[3] = _colsum(dc * x)
    for j in (1, 2, 3):
        dx = dx + _shift_up(dc, dc_next, j) * w[3 - j:4 - j]
        dws[3 - j] = _colsum(dc * _shift_down(x, x_prev, j))
    row = _iota((8, x.shape[1]), 0)
    dw = jnp.zeros((8, x.shape[1]), F32)
    for k in range(4):
        dw = jnp.where(row == k, dws[k], dw)
    return dx, dw, _colsum(dc)


def _ssd_gates(dtf, bias, a_log):
    lane = _iota(dtf.shape, 1)
    head = lane < NH
    dt = jnp.where(head, _softplus(dtf + bias), 0.0)
    a_neg = jnp.where(_iota(a_log.shape, 1) < NH, -jnp.exp(a_log), 0.0)
    a = dt * a_neg
    cs = _sel_left(_tri_lower(CHUNK), a)
    return dt, a_neg, cs


def _decay_mask(cs_ref, cst_ref, h):
    diff = cs_ref[:, h:h + 1] - cst_ref[h:h + 1, :]
    low = _iota((CHUNK, CHUNK), 1) <= _iota((CHUNK, CHUNK), 0)
    return jnp.where(low, jnp.exp(jnp.minimum(diff, 0.0)), 0.0)


def _ssd_fwd(xs_a, bc_a, p, bias128, alog128, dskip_x, s):
    nc = s // CHUNK
    t = CHUNK

    def body(xs_ref, bc_ref, dtf_ref, bias_ref, alog_ref, dsk_ref, y_ref, st_ref,
             state, x_sc, xw_sc, cs_sc, cst_sc, yd_sc):
        c = pl.program_id(0)

        @pl.when(c == 0)
        def _():
            state[...] = jnp.zeros(state.shape, F32)

        dt, _, cs = _ssd_gates(dtf_ref[...], bias_ref[...], alog_ref[...])
        cs_sc[...] = cs
        cst_sc[...] = cs.T
        cs_last = cs[t - 1:t, :]
        expand = _head_expand()
        ex = _sel_right(jnp.concatenate([dt, jnp.exp(cs), jnp.exp(cs_last - cs)], axis=0), expand)
        dt_x, eo_x, we_x = ex[0:t], ex[t:2 * t], ex[2 * t:3 * t]
        g_x = _sel_right(jnp.broadcast_to(jnp.exp(cs_last), (8, 128)), expand)[0:1]
        xs = xs_ref[...]
        x = xs * dt_x
        x_sc[...] = x.astype(BF16)
        xw_sc[...] = (x * we_x).astype(BF16)
        prev = state[...]
        st_ref[0] = prev
        prev_b = prev.astype(BF16)
        for g in range(2):
            cols = slice(g * 512, (g + 1) * 512)
            b_g = bc_ref[:, g * 128:(g + 1) * 128].astype(BF16)
            c_g = bc_ref[:, 256 + g * 128:256 + (g + 1) * 128].astype(BF16)
            gmat = _dot(c_g, b_g, NT)
            y_off = _dot(c_g, prev_b[:, cols]) * eo_x[:, cols]
            s_loc = _dot(b_g, xw_sc[:, cols], TN)
            state[:, cols] = g_x[:, cols] * prev[:, cols] + s_loc
            for e in range(HG):
                h = g * HG + e
                m = gmat * _decay_mask(cs_sc, cst_sc, h)
                yd_sc[:, h * HD:(h + 1) * HD] = _dot(m.astype(BF16), x_sc[:, h * HD:(h + 1) * HD])
            y_ref[:, cols] = yd_sc[:, cols] + y_off + dsk_ref[:, cols] * xs[:, cols]

    return pl.pallas_call(
        body, name="ssd_fwd", grid=(nc,),
        in_specs=[pl.BlockSpec((t, D), lambda c: (c, 0)),
                  pl.BlockSpec((t, 512), lambda c: (c, 0)),
                  pl.BlockSpec((t, 128), lambda c: (c, OFF_DTF // 128)),
                  pl.BlockSpec((1, 128), lambda c: (0, 0)),
                  pl.BlockSpec((1, 128), lambda c: (0, 0)),
                  pl.BlockSpec((1, D), lambda c: (0, 0))],
        out_specs=[pl.BlockSpec((t, D), lambda c: (c, 0)),
                   pl.BlockSpec((1, NSTATE, D), lambda c: (c, 0, 0))],
        out_shape=[jax.ShapeDtypeStruct((s, D), F32), jax.ShapeDtypeStruct((nc, NSTATE, D), F32)],
        scratch_shapes=[pltpu.VMEM((NSTATE, D), F32), pltpu.VMEM((t, D), BF16), pltpu.VMEM((t, D), BF16),
                        pltpu.VMEM((t, 128), F32), pltpu.VMEM((128, t), F32), pltpu.VMEM((t, D), F32)],
        compiler_params=_params(("arbitrary",)),
    )(xs_a, bc_a, p, bias128, alog128, dskip_x)


def _ssd_bwd(dy, xs_a, bc_a, p, states, bias128, alog128, dskip_x, s):
    nc = s // CHUNK
    t = CHUNK

    def body(dy_ref, xs_ref, bc_ref, dtf_ref, st_ref, bias_ref, alog_ref, dsk_ref,
             dxs_ref, dbc_ref, ddt_ref, dalog_ref, dskip_ref,
             dstate, x_sc, dy_sc, dx_sc, deo_sc, dwe_sc, cs_sc, cst_sc, dcol_sc, drow_sc):
        i = pl.program_id(0)

        @pl.when(i == 0)
        def _():
            dstate[...] = jnp.zeros(dstate.shape, F32)
            dalog_ref[...] = jnp.zeros(dalog_ref.shape, F32)
            dskip_ref[...] = jnp.zeros(dskip_ref.shape, F32)

        dtf = dtf_ref[...]
        dt, a_neg, cs = _ssd_gates(dtf, bias_ref[...], alog_ref[...])
        cs_sc[...] = cs
        cst_sc[...] = cs.T
        cs_last = cs[t - 1:t, :]
        eo, we, g_end = jnp.exp(cs), jnp.exp(cs_last - cs), jnp.exp(cs_last)
        expand, reduce = _head_expand(), _head_reduce()
        ex = _sel_right(jnp.concatenate([dt, eo, we], axis=0), expand)
        dt_x, eo_x, we_x = ex[0:t], ex[t:2 * t], ex[2 * t:3 * t]
        g_x = _sel_right(jnp.broadcast_to(g_end, (8, 128)), expand)[0:1]
        xs = xs_ref[...]
        dyv = dy_ref[...]
        x = xs * dt_x
        x_sc[...] = x.astype(BF16)
        dy_sc[...] = dyv.astype(BF16)
        dyo_b = (dyv * eo_x).astype(BF16)
        xw_b = (x * we_x).astype(BF16)
        prev = st_ref[0]
        prev_b = prev.astype(BF16)
        dnext = dstate[...]
        dnext_b = dnext.astype(BF16)
        dcol_sc[...] = jnp.zeros(dcol_sc.shape, F32)
        drow_sc[...] = jnp.zeros(drow_sc.shape, F32)
        lane_row = _iota((1, 128), 1)
        sub_col = _iota((128, 1), 0)
        for g in range(2):
            cols = slice(g * 512, (g + 1) * 512)
            b_g = bc_ref[:, g * 128:(g + 1) * 128].astype(BF16)
            c_g = bc_ref[:, 256 + g * 128:256 + (g + 1) * 128].astype(BF16)
            gmat = _dot(c_g, b_g, NT)
            b_ds = _dot(b_g, dnext_b[:, cols])
            c_s = _dot(c_g, prev_b[:, cols])
            dx_sc[:, cols] = b_ds * we_x[:, cols]
            deo_sc[:, cols] = dyv[:, cols] * c_s
            dwe_sc[:, cols] = b_ds * x[:, cols]
            db = _dot(xw_b[:, cols], dnext_b[:, cols], NT)
            dc = _dot(dyo_b[:, cols], prev_b[:, cols], NT)
            dstate[:, cols] = g_x[:, cols] * dnext[:, cols] + _dot(c_g, dyo_b[:, cols], TN)
            dg = jnp.zeros((t, t), F32)
            for e in range(HG):
                h = g * HG + e
                hc = slice(h * HD, (h + 1) * HD)
                lmat = _decay_mask(cs_sc, cst_sc, h)
                m = gmat * lmat
                dx_sc[:, hc] += _dot(m.astype(BF16), dy_sc[:, hc], TN)
                dm = _dot(dy_sc[:, hc], x_sc[:, hc], NT)
                dg = dg + dm * lmat
                qm = dm * m
                dcol_sc[...] += jnp.sum(qm, axis=1, keepdims=True) * (lane_row == h).astype(F32)
                drow_sc[...] += (sub_col == h).astype(F32) * jnp.sum(qm, axis=0, keepdims=True)
            dg_b = dg.astype(BF16)
            dbc_ref[:, g * 128:(g + 1) * 128] = db + _dot(dg_b, c_g, TN)
            dbc_ref[:, 256 + g * 128:256 + (g + 1) * 128] = dc + _dot(dg_b, b_g)
        d_eo = _sel_right(deo_sc[...], reduce)
        d_we = _sel_right(dwe_sc[...], reduce)
        d_gend = _sel_right(jnp.broadcast_to(_colsum(dnext * prev), (8, D)), reduce)[0:1]
        d_cs = dcol_sc[...] - drow_sc[...].T + d_eo * eo - d_we * we
        extra = _colsum(d_we * we) + d_gend * g_end
        d_cs = d_cs + jnp.where(_iota((t, 128), 0) == t - 1, extra, 0.0)
        da = _sel_left(_tri_upper(t), d_cs)
        dx = dx_sc[...]
        ddt = _sel_right(dx * xs, reduce) + da * a_neg
        dxs_ref[...] = dx * dt_x + dsk_ref[...] * dyv
        ddt_ref[...] = jnp.where(_iota((t, 128), 1) < NH, ddt * _sigmoid(dtf + bias_ref[...]), 0.0)
        dalog_ref[...] += _colsum(da * dt) * a_neg
        dskip_ref[...] += _sel_right(jnp.broadcast_to(_colsum(dyv * xs), (8, D)), reduce)[0:1]

    rev = lambda i: nc - 1 - i
    return pl.pallas_call(
        body, name="ssd_bwd", grid=(nc,),
        in_specs=[pl.BlockSpec((t, D), lambda i: (rev(i), 0)),
                  pl.BlockSpec((t, D), lambda i: (rev(i), 0)),
                  pl.BlockSpec((t, 512), lambda i: (rev(i), 0)),
                  pl.BlockSpec((t, 128), lambda i: (rev(i), OFF_DTF // 128)),
                  pl.BlockSpec((1, NSTATE, D), lambda i: (rev(i), 0, 0)),
                  pl.BlockSpec((1, 128), lambda i: (0, 0)),
                  pl.BlockSpec((1, 128), lambda i: (0, 0)),
                  pl.BlockSpec((1, D), lambda i: (0, 0))],
        out_specs=[pl.BlockSpec((t, D), lambda i: (rev(i), 0)),
                   pl.BlockSpec((t, 512), lambda i: (rev(i), 0)),
                   pl.BlockSpec((t, 128), lambda i: (rev(i), 0)),
                   pl.BlockSpec((1, 128), lambda i: (0, 0)),
                   pl.BlockSpec((1, 128), lambda i: (0, 0))],
        out_shape=[jax.ShapeDtypeStruct((s, D), F32), jax.ShapeDtypeStruct((s, 512), F32),
                   jax.ShapeDtypeStruct((s, 128), F32), jax.ShapeDtypeStruct((1, 128), F32),
                   jax.ShapeDtypeStruct((1, 128), F32)],
        scratch_shapes=[pltpu.VMEM((NSTATE, D), F32), pltpu.VMEM((t, D), BF16), pltpu.VMEM((t, D), BF16),
                        pltpu.VMEM((t, D), F32), pltpu.VMEM((t, D), F32), pltpu.VMEM((t, D), F32),
                        pltpu.VMEM((t, 128), F32), pltpu.VMEM((128, t), F32),
                        pltpu.VMEM((t, 128), F32), pltpu.VMEM((128, t), F32)],
        compiler_params=_params(("arbitrary",)),
    )(dy, xs_a, bc_a, p, states, bias128, alog128, dskip_x)


def _gate_lanes(shape):
    lane = _iota(shape, 1)
    return (lane >= NH) & (lane < 2 * NH)


def _cum_fwd(p, bias128, s):
    tr = min(512, s)

    def body(dtf_ref, bias_ref, o_ref, carry):
        @pl.when(pl.program_id(0) == 0)
        def _():
            carry[...] = jnp.zeros(carry.shape, F32)

        lf = jnp.where(_gate_lanes((tr, 128)), _log_sigmoid(dtf_ref[...] + bias_ref[...]), 0.0)
        cum = _sel_left(_tri_lower(tr), lf) + carry[...]
        carry[...] = cum[tr - 1:tr, :]
        o_ref[...] = cum.T[NH:2 * NH, :]

    return pl.pallas_call(
        body, name="cum_fwd", grid=(s // tr,),
        in_specs=[pl.BlockSpec((tr, 128), lambda i: (i, OFF_DTF // 128)), pl.BlockSpec((1, 128), lambda i: (0, 0))],
        out_specs=pl.BlockSpec((NH, tr), lambda i: (0, i)),
        out_shape=jax.ShapeDtypeStruct((NH, s), F32),
        scratch_shapes=[pltpu.VMEM((1, 128), F32)],
        compiler_params=_params(("arbitrary",)),
    )(p, bias128)


def _cum_bwd(dcum, ddt_raw, p, bias128, s):
    tr = min(512, s)

    def fn(pos, dcum, ddt, dtf, bias, carry, acc):
        suffix = _sel_left(_tri_upper(tr), dcum) + carry
        dfr = jnp.where(_gate_lanes((tr, 128)), suffix * _sigmoid(-(dtf + bias)), 0.0)
        out = ddt + dfr
        return out, suffix[0:1, :], acc + _colsum(out)

    return _rowk("cum_bwd", fn, s, tr, [(dcum, 128, 0, 0), (ddt_raw, 128, 0, 0), (p, 128, OFF_DTF // 128, 0)],
                 [bias128], [(128, BF16)], [(1, 128), (1, 128)], reverse=True)


def _attn_masked_logits(q_h, k_h, crow, qi, ki, tq, tk):
    row = qi * tq + _iota((tq, tk), 0)
    col = ki * tk + _iota((tq, tk), 1)
    return jnp.where(col <= row, _dot(q_h, k_h, NT) - crow, -1e30)


def _attn_fwd(p, cum_t, s):
    tq = tk = min(512, s)
    nq = s // tq

    def body(q_ref, k_ref, v_ref, c_ref, o_ref, lse_ref, m_sc, l_sc, acc_sc):
        j, qi, ki = pl.program_id(0), pl.program_id(1), pl.program_id(2)

        @pl.when(ki == 0)
        def _():
            m_sc[...] = jnp.full(m_sc.shape, -1e30, F32)
            l_sc[...] = jnp.zeros(l_sc.shape, F32)
            acc_sc[...] = jnp.zeros(acc_sc.shape, F32)

        @pl.when(ki <= qi)
        def _():
            q = (q_ref[...] * ATT_SCALE).astype(BF16)
            k = k_ref[...].astype(BF16)
            v = v_ref[...].astype(BF16)
            for hh in range(2):
                hc = slice(hh * HD, (hh + 1) * HD)
                crow = c_ref[pl.ds(2 * j + hh, 1), :]
                sc = _attn_masked_logits(q[:, hc], k[:, hc], crow, qi, ki, tq, tk)
                m_prev = m_sc[hh]
                m_new = jnp.maximum(m_prev, jnp.max(sc, axis=1, keepdims=True))
                pr = jnp.exp(sc - m_new[:, 0:1])
                alpha = jnp.exp(m_prev - m_new)
                l_sc[hh] = alpha * l_sc[hh] + jnp.sum(pr, axis=1, keepdims=True)
                m_sc[hh] = m_new
                acc_sc[:, hc] = alpha[:, :HD] * acc_sc[:, hc] + _dot(pr.astype(BF16), v[:, hc])

        @pl.when(ki == qi)
        def _():
            for hh in range(2):
                hc = slice(hh * HD, (hh + 1) * HD)
                l = l_sc[hh]
                o_ref[:, hc] = acc_sc[:, hc] / l[:, :HD]
                lse_ref[:, hc] = (m_sc[hh] + jnp.log(l))[:, :HD]

    def kv_map(j, qi, ki, off):
        return (jnp.minimum(ki, qi), off + j)

    return pl.pallas_call(
        body, name="attn_fwd", grid=(NH // 2, nq, nq),
        in_specs=[pl.BlockSpec((tq, 128), lambda j, qi, ki: (qi, OFF_Q // 128 + j)),
                  pl.BlockSpec((tk, 128), functools.partial(kv_map, off=OFF_K // 128)),
                  pl.BlockSpec((tk, 128), functools.partial(kv_map, off=OFF_V // 128)),
                  pl.BlockSpec((NH, tk), lambda j, qi, ki: (0, jnp.minimum(ki, qi)))],
        out_specs=[pl.BlockSpec((tq, 128), lambda j, qi, ki: (qi, j)),
                   pl.BlockSpec((tq, 128), lambda j, qi, ki: (qi, j))],
        out_shape=[jax.ShapeDtypeStruct((s, D), F32), jax.ShapeDtypeStruct((s, D), F32)],
        scratch_shapes=[pltpu.VMEM((2, tq, 128), F32), pltpu.VMEM((2, tq, 128), F32), pltpu.VMEM((tq, 128), F32)],
        compiler_params=_params(("parallel", "parallel", "arbitrary")),
    )(p, p, p, cum_t)


def _attn_bwd(p, cum_t, o, lse, do, s):
    tq = tk = min(512, s)
    nq = s // tq

    def body(q_ref, k_ref, v_ref, c_ref, o_ref, lse_ref, do_ref, dq_ref, dk_ref, dv_ref, dc_ref, dr_ref, dk_sc, dv_sc, dc_sc):
        j, ki, qi = pl.program_id(0), pl.program_id(1), pl.program_id(2)

        @pl.when(qi == ki)
        def _():
            dk_sc[...] = jnp.zeros(dk_sc.shape, F32)
            dv_sc[...] = jnp.zeros(dv_sc.shape, F32)
            dc_sc[...] = jnp.zeros(dc_sc.shape, F32)

        @pl.when(qi >= ki)
        def _():
            q = (q_ref[...] * ATT_SCALE).astype(BF16)
            k = k_ref[...].astype(BF16)
            v = v_ref[...].astype(BF16)
            dov, ov, lse_v = do_ref[...], o_ref[...], lse_ref[...]
            rows = pl.ds(pl.multiple_of(qi * tq, tq), tq)
            for hh in range(2):
                hc = slice(hh * HD, (hh + 1) * HD)
                crow = c_ref[pl.ds(2 * j + hh, 1), :]
                sc = _attn_masked_logits(q[:, hc], k[:, hc], crow, qi, ki, tq, tk)
                pr = jnp.exp(sc - lse_v[:, hh * HD:hh * HD + 1])
                do_h = dov[:, hc]
                delta = jnp.sum(do_h * ov[:, hc], axis=1, keepdims=True)
                do_b = do_h.astype(BF16)
                dv_sc[:, hc] += _dot(pr.astype(BF16), do_b, TN)
                ds = pr * (_dot(do_b, v[:, hc], NT) - delta)
                dc_sc[hh:hh + 1, :] += _colsum(ds)
                ds_b = ds.astype(BF16)
                dk_sc[:, hc] += _dot(ds_b, q[:, hc], TN)
                dq_h = _dot(ds_b, k[:, hc]) * ATT_SCALE
                drow = jnp.broadcast_to(jnp.sum(ds, axis=1, keepdims=True), (tq, HD))

                @pl.when(ki == 0)
                def _():
                    dq_ref[rows, hc] = dq_h
                    dr_ref[rows, hc] = drow

                @pl.when(ki > 0)
                def _():
                    dq_ref[rows, hc] += dq_h
                    dr_ref[rows, hc] += drow

        @pl.when(qi == nq - 1)
        def _():
            dk_ref[...] = dk_sc[...].astype(BF16)
            dv_ref[...] = dv_sc[...].astype(BF16)
            dc_ref[0] = dc_sc[...]

    def q_map(j, ki, qi, off):
        return (jnp.maximum(qi, ki), off + j)

    return pl.pallas_call(
        body, name="attn_bwd", grid=(NH // 2, nq, nq),
        in_specs=[pl.BlockSpec((tq, 128), functools.partial(q_map, off=OFF_Q // 128)),
                  pl.BlockSpec((tk, 128), lambda j, ki, qi: (ki, OFF_K // 128 + j)),
                  pl.BlockSpec((tk, 128), lambda j, ki, qi: (ki, OFF_V // 128 + j)),
                  pl.BlockSpec((NH, tk), lambda j, ki, qi: (0, ki)),
                  pl.BlockSpec((tq, 128), functools.partial(q_map, off=0)),
                  pl.BlockSpec((tq, 128), functools.partial(q_map, off=0)),
                  pl.BlockSpec((tq, 128), functools.partial(q_map, off=0))],
        out_specs=[pl.BlockSpec((s, 128), lambda j, ki, qi: (0, j)),
                   pl.BlockSpec((tk, 128), lambda j, ki, qi: (ki, j)),
                   pl.BlockSpec((tk, 128), lambda j, ki, qi: (ki, j)),
                   pl.BlockSpec((1, 8, tk), lambda j, ki, qi: (j, 0, ki)),
                   pl.BlockSpec((s, 128), lambda j, ki, qi: (0, j))],
        out_shape=[jax.ShapeDtypeStruct((s, D), F32), jax.ShapeDtypeStruct((s, D), BF16),
                   jax.ShapeDtypeStruct((s, D), BF16), jax.ShapeDtypeStruct((NH // 2, 8, s), F32),
                   jax.ShapeDtypeStruct((s, D), F32)],
        scratch_shapes=[pltpu.VMEM((tk, 128), F32), pltpu.VMEM((tk, 128), F32), pltpu.VMEM((8, tk), F32)],
        compiler_params=_params(("parallel", "arbitrary", "arbitrary")),
    )(p, p, p, cum_t, o, lse, do)


def _ln_stats(u):
    mu = _mean(u)
    d = u - mu
    rstd = lax.rsqrt(_mean(d * d) + EPS)
    return d * rstd, rstd


def _ln_bwd(dx, xh, rstd, gam):
    dxh = dx * gam
    return rstd * (dxh - _mean(dxh) - xh * _mean(dxh * xh))


def _rms_bwd(d, xn, r, w):
    t = d * w
    return r * (t - xn * _mean(t * xn)), _colsum(d * xn)


def _mix_norm(y, p, att, w_ssm, w_att, s):
    def fn(pos, y, z, att, w1, w2):
        g = y * _silu(z)
        n1 = g * lax.rsqrt(_mean(g * g) + EPS) * w1
        n2 = att * lax.rsqrt(_mean(att * att) + EPS) * w2
        return (jnp.concatenate([n1, n2], axis=1),)

    return _rowk("mix_norm", fn, s, 256, [(y, D, 0, 0), (p, D, OFF_Z // D, 0), (att, D, 0, 0)],
                 [w_ssm, w_att], [(2 * D, BF16)], [])[0]


def _mix_norm_bwd(dmix, y, p, att, w_ssm, w_att, s):
    def fn(pos, dmix, y, z, att, w1, w2, a1, a2):
        sz = _silu(z)
        g = y * sz
        r1 = lax.rsqrt(_mean(g * g) + EPS)
        dg, dw1 = _rms_bwd(dmix[:, :D], g * r1, r1, w1)
        r2 = lax.rsqrt(_mean(att * att) + EPS)
        datt, dw2 = _rms_bwd(dmix[:, D:], att * r2, r2, w2)
        return dg * sz, dg * y * _dsilu(z), datt, a1 + dw1, a2 + dw2

    return _rowk("mix_norm_bwd", fn, s, 256, [(dmix, 2 * D, 0, 0), (y, D, 0, 0), (p, D, OFF_Z // D, 0), (att, D, 0, 0)],
                 [w_ssm, w_att], [(D, F32), (D, BF16), (D, F32)], [(1, D), (1, D)])


def _ln1(x0, y, g1, gam, bet, sc2, sh2, s):
    def fn(pos, x0, y, g1, gam, bet, sc2, sh2):
        xh, _ = _ln_stats(ALPHA * x0 + (1.0 + g1) * y)
        x1 = xh * gam + bet
        return x1, _modulate(x1, sc2, sh2)

    return _rowk("ln1", fn, s, 256, [(x0, D, 0, 0), (y, D, 0, 0)], [g1, gam, bet, sc2, sh2], [(D, F32), (D, BF16)], [])


def _ln2_loss(x1, ff, tgt, g2, gam, bet, s):
    def fn(pos, x1, ff, tgt, g2, gam, bet, a_loss, a_dgam, a_dbet, a_dg2):
        xh, rstd = _ln_stats(ALPHA * x1 + (1.0 + g2) * ff)
        err = xh * gam + bet - tgt
        dx2 = err * (1.0 / D)
        du = _ln_bwd(dx2, xh, rstd, gam)
        return (du, du * (1.0 + g2), a_loss + _colsum(err * err), a_dgam + _colsum(dx2 * xh),
                a_dbet + _colsum(dx2), a_dg2 + _colsum(du * ff))

    return _rowk("ln2_loss", fn, s, 256, [(x1, D, 0, 0), (ff, D, 0, 0), (tgt, D, 0, 0)], [g2, gam, bet],
                 [(D, F32), (D, BF16)], [(1, D)] * 4)


def _ln1_bwd(dh2, du2, x0, y, g1, gam, bet, sc2, s):
    def fn(pos, dh2, du2, x0, y, g1, gam, bet, sc2, a_sc, a_sh, a_gam, a_bet, a_g1):
        xh, rstd = _ln_stats(ALPHA * x0 + (1.0 + g1) * y)
        x1 = xh * gam + bet
        dx1 = ALPHA * du2 + dh2 * (1.0 + sc2)
        du1 = _ln_bwd(dx1, xh, rstd, gam)
        return (du1, du1 * (1.0 + g1), a_sc + _colsum(dh2 * x1), a_sh + _colsum(dh2), a_gam + _colsum(dx1 * xh),
                a_bet + _colsum(dx1), a_g1 + _colsum(du1 * y))

    return _rowk("ln1_bwd", fn, s, 256, [(dh2, D, 0, 0), (du2, D, 0, 0), (x0, D, 0, 0), (y, D, 0, 0)],
                 [g1, gam, bet, sc2], [(D, F32), (D, BF16)], [(1, D)] * 5)


def _input_grad(dh1, du1, x0, sc1, s):
    def fn(pos, dh1, du1, x0, sc1, a_sc, a_sh):
        return ALPHA * du1 + dh1 * (1.0 + sc1), a_sc + _colsum(dh1 * x0), a_sh + _colsum(dh1)

    return _rowk("input_grad", fn, s, 256, [(dh1, D, 0, 0), (du1, D, 0, 0), (x0, D, 0, 0)], [sc1],
                 [(D, F32)], [(1, D)] * 2)


def _adamw(name, w, g, m, v, *, tr, slots):
    r, c = w.shape

    def body(w_ref, g_ref, m_ref, v_ref, g_out, d_out, m_out, v_out):
        if slots:
            grad = g_ref[0].astype(F32)
            for k in range(1, N_DEV):
                grad = grad + g_ref[k].astype(F32)
        else:
            grad = g_ref[...]
        m_new = ADAM_B1 * m_ref[...] + (1.0 - ADAM_B1) * grad
        v_new = ADAM_B2 * v_ref[...] + (1.0 - ADAM_B2) * (grad * grad)
        m_hat = m_new / (1.0 - ADAM_B1 ** ADAM_STEP)
        v_hat = v_new / (1.0 - ADAM_B2 ** ADAM_STEP)
        g_out[...] = grad
        d_out[...] = -ADAM_LR * (m_hat / (jnp.sqrt(v_hat) + ADAM_EPS) + ADAM_WD * w_ref[...])
        m_out[...] = m_new
        v_out[...] = v_new

    tile = pl.BlockSpec((tr, c), lambda i: (i, 0))
    g_spec = pl.BlockSpec((N_DEV, tr, c), lambda i: (0, i, 0)) if slots else tile
    return pl.pallas_call(
        body, name=name, grid=(r // tr,),
        in_specs=[tile, g_spec, tile, tile], out_specs=[tile] * 4,
        out_shape=[jax.ShapeDtypeStruct((r, c), F32)] * 4,
        compiler_params=_params(("parallel",)),
    )(w, g, m, v)


def _dot_f32(a, b, dims=NN):
    a0, a1, a2 = _split3(a)
    b0, b1, b2 = _split3(b)
    acc = _dot(a0, b0, dims)
    for x, y in ((a0, b1), (a1, b0), (a1, b1), (a0, b2), (a2, b0)):
        acc = acc + _dot(x, y, dims)
    return acc


def _ada_mod(c_all, w_shard, b_shard):
    def body(c_ref, w_ref, b_ref, o_ref):
        act = _silu(c_ref[...])
        act16 = jnp.concatenate([act, jnp.zeros_like(act)], axis=0)
        o_ref[...] = _dot_f32(act16, w_ref[...])[0:N_DEV] + b_ref[...]

    return pl.pallas_call(
        body, name="ada_mod", out_shape=jax.ShapeDtypeStruct((N_DEV, w_shard.shape[1]), F32),
        compiler_params=_params(None),
    )(c_all, w_shard, b_shard)


def _ada_grad(c_all, dmod_cols, dmod_all):
    def body(c_ref, dc_ref, da_ref, gw_ref, gb_ref):
        act = _silu(c_ref[...])
        act16 = jnp.concatenate([act, jnp.zeros_like(act)], axis=0)
        dm = dc_ref[...]
        dm16 = jnp.concatenate([dm, jnp.zeros_like(dm)], axis=0)
        gw_ref[...] = _dot_f32(act16, dm16, TN)
        gb_ref[...] = _colsum(da_ref[...])

    return pl.pallas_call(
        body, name="ada_grad",
        out_shape=[jax.ShapeDtypeStruct((D, dmod_cols.shape[1]), F32), jax.ShapeDtypeStruct((1, 6 * D), F32)],
        compiler_params=_params(None),
    )(c_all, dmod_cols, dmod_all)


def _sum_slots(name, g):
    def body(g_ref, o_ref):
        acc = g_ref[0]
        for k in range(1, N_DEV):
            acc = acc + g_ref[k]
        o_ref[...] = acc

    return pl.pallas_call(body, name=name, out_shape=jax.ShapeDtypeStruct(g.shape[1:], F32),
                          compiler_params=_params(None))(g)


def _exchange(name, x, scatter):
    shape = x.shape[1:] if scatter else x.shape

    def body(x_ref, o_ref, send_sems, recv_sems, local_sem):
        mx, my, mc = lax.axis_index("x"), lax.axis_index("y"), lax.axis_index("c")
        me = 4 * mx + 2 * my + mc

        def src(slot):
            return x_ref.at[slot] if scatter else x_ref

        local = pltpu.make_async_copy(src(me), o_ref.at[me], local_sem)
        local.start()
        sends = []
        for d in range(1, N_DEV):
            px = 1 - mx if d & 4 else mx
            py = 1 - my if d & 2 else my
            pc = 1 - mc if d & 1 else mc
            peer = 4 * px + 2 * py + pc

            def copy(src_slot, dst_slot, d=d, to=(px, py, pc)):
                return pltpu.make_async_remote_copy(
                    src_ref=src(src_slot), dst_ref=o_ref.at[dst_slot],
                    send_sem=send_sems.at[d - 1], recv_sem=recv_sems.at[d - 1],
                    device_id=to, device_id_type=pl.DeviceIdType.MESH)

            out = copy(peer, me)
            out.start()
            sends.append((out, copy(me, peer)))
        for _, arrival in sends:
            arrival.wait_recv()
        for out, _ in sends:
            out.wait_send()
        local.wait()

    return pl.pallas_call(
        body, name=name,
        in_specs=[pl.BlockSpec(memory_space=pl.ANY)], out_specs=pl.BlockSpec(memory_space=pl.ANY),
        out_shape=jax.ShapeDtypeStruct((N_DEV,) + tuple(shape), x.dtype),
        scratch_shapes=[pltpu.SemaphoreType.DMA((N_DEV - 1,)), pltpu.SemaphoreType.DMA((N_DEV - 1,)),
                        pltpu.SemaphoreType.DMA(())],
        compiler_params=pltpu.CompilerParams(has_side_effects=True),
    )(x)


def _relu2(a):
    r = jnp.maximum(a, 0.0)
    return r * r


def _relu2_grad(acc, a):
    return acc * (2.0 * jnp.maximum(a, 0.0))


def _local_step(x0, tgt, mod, wcat, wout, w1, w2, conv_w, conv_b, dt_bias, a_log, d_skip, ssm_norm_w, f_bias,
                attn_norm_w, ln1_g, ln1_b, ln2_g, ln2_b):
    s = x0.shape[0]
    tm = min(512, s)
    ts = min(1024, s)
    sh1, sc1, g1, sh2, sc2, g2 = [mod[:, i * D:(i + 1) * D] for i in range(6)]
    zero = jnp.zeros((1, 128 - 2 * NH), F32)
    bias128 = jnp.concatenate([dt_bias, f_bias, zero], axis=1)
    alog128 = jnp.concatenate([a_log, jnp.zeros((1, 128 - NH), F32)], axis=1)
    dskip_x = jnp.repeat(d_skip, HD, axis=1)
    w_xs, w_bc, b_xs, b_bc = conv_w[:, :D], conv_w[:, D:], conv_b[:, :D], conv_b[:, D:]

    p = _mm_nn("in_proj", x0, wcat, tm=tm, tn=640, tk=D, out_dtype=F32, pro=_modulate, aux=(sc1, sh1))
    xs_a, bc_a = _conv_fwd(p, w_xs, b_xs, w_bc, b_bc, s)
    y_ssd, states = _ssd_fwd(xs_a, bc_a, p, bias128, alog128, dskip_x, s)
    cum_t = _cum_fwd(p, bias128, s)
    att, lse = _attn_fwd(p, cum_t, s)
    ymix = _mix_norm(y_ssd, p, att, ssm_norm_w, attn_norm_w, s)
    y = _mm_nn("out_proj", ymix, wout, tm=tm, tn=512, tk=2 * D, out_dtype=F32)
    x1, h2 = _ln1(x0, y, g1, ln1_g, ln1_b, sc2, sh2, s)
    a1 = _mm_nn("ff_in", h2, w1, tm=tm, tn=1024, tk=D, out_dtype=F32)
    ff = _mm_nn("ff_out", a1, w2, tm=tm, tn=512, tk=1024, out_dtype=F32, pro=_relu2)
    du2, dff, sq_err, d_ln2_g, d_ln2_b, d_g2 = _ln2_loss(x1, ff, tgt, g2, ln2_g, ln2_b, s)

    da1 = _mm_nt("d_ff_hidden", [dff], [(w2, D, 0)], n=DFF, tm=tm, tn=512, out_dtype=BF16, epi=_relu2_grad, epi_aux=(a1,))
    d_w2 = _mm_tn("d_w_ff_out", a1, dff, tm=1024, tn=512, ts=ts, pro=_relu2)
    d_w1 = _mm_tn("d_w_ff_in", h2, da1, tm=1024, tn=1024, ts=ts)
    dh2 = _mm_nt("d_ff_input", [da1], [(w1, DFF, 0)], n=D, tm=min(256, s), tn=512, out_dtype=F32)
    du1, dy, d_sc2, d_sh2, d_ln1_g, d_ln1_b, d_g1 = _ln1_bwd(dh2, du2, x0, y, g1, ln1_g, ln1_b, sc2, s)

    dmix = _mm_nt("d_mix", [dy], [(wout, D, 0)], n=2 * D, tm=tm, tn=512, out_dtype=F32)
    d_wout = _mm_tn("d_w_out", ymix, dy, tm=1024, tn=512, ts=ts)
    dy_ssd, dz, datt, d_ssm_w, d_attn_w = _mix_norm_bwd(dmix, y_ssd, p, att, ssm_norm_w, attn_norm_w, s)
    dq, dk, dv, dcs, drs = _attn_bwd(p, cum_t, att, lse, datt, s)
    dxs_a, dbc_a, ddt_raw, d_alog, d_dskip = _ssd_bwd(dy_ssd, xs_a, bc_a, p, states, bias128, alog128, dskip_x, s)
    dcum = jnp.pad(drs[:, ::HD] - dcs[:, :2, :].reshape(NH, s).T, ((0, 0), (NH, 128 - 2 * NH)))
    ddtf, _, d_bias = _cum_bwd(dcum, ddt_raw, p, bias128, s)
    dxs, dbc, d_wc_xs, d_bc_xs, d_wc_bc, d_bc_bc = _conv_bwd(dxs_a, dbc_a, p, w_xs, b_xs, w_bc, b_bc, s)

    segs = [(dz, OFF_Z, D), (dxs, OFF_XS, D), (dq, OFF_Q, D), (dk, OFF_K, D), (dv, OFF_V, D), (dbc, OFF_BC, 512),
            (ddtf, OFF_DTF, 128)]
    dh1 = _mm_nt("d_h1", [a for a, _, _ in segs], [(wcat, w, off // w) for _, off, w in segs], n=D, tm=min(256, s), tn=512,
                 out_dtype=F32)
    d_z, d_xs, d_q, d_k, d_v, d_bcw, d_dtf = [
        _mm_tn("d_w_in_%d" % i, x0, a, tm=1024, tn=min(w, 512), ts=ts, pro=_modulate, aux=(sc1, sh1))
        for i, (a, _, w) in enumerate(segs)]
    d_w_in = jnp.concatenate([d_z, d_xs, d_bcw, d_dtf[:, :NH], d_q, d_k, d_v, d_dtf[:, NH:2 * NH]], axis=1)
    grad_x, d_sc1, d_sh1 = _input_grad(dh1, du1, x0, sc1, s)

    return dict(
        loss=(0.5 / D) * jnp.sum(sq_err), grad_x=grad_x, d_w_in=d_w_in, d_w_out=d_wout, d_w_ff_in=d_w1, d_w_ff_out=d_w2,
        d_mod=jnp.concatenate([d_sh1, d_sc1, d_g1, d_sh2, d_sc2, d_g2], axis=1),
        d_conv_w=jnp.concatenate([d_wc_xs[:4], d_wc_bc[:4]], axis=1), d_conv_b=jnp.concatenate([d_bc_xs, d_bc_bc], axis=1),
        d_ssm_norm_w=d_ssm_w, d_attn_norm_w=d_attn_w, d_ln1_g=d_ln1_g, d_ln1_b=d_ln1_b, d_ln2_g=d_ln2_g, d_ln2_b=d_ln2_b,
        d_gate_bias=d_bias, d_a_log=d_alog, d_d_skip=d_dskip)


def _pack_w_in(w_in):
    seg = lambda off, w: w_in[:, off:off + w]
    return jnp.concatenate([seg(W_Z, D), seg(W_XS, D), seg(W_Q, D), seg(W_K, D), seg(W_V, D), seg(W_BC, 512),
                            seg(W_DT, NH), seg(W_F, NH), jnp.zeros((D, 128 - 2 * NH), w_in.dtype)], axis=1)


WEIGHTS = ['w_ada', 'b_ada', 'w_in', 'conv_w', 'conv_b', 'dt_bias', 'a_log', 'd_skip', 'ssm_norm_w', 'f_bias',
           'attn_norm_w', 'w_out', 'ln1_g', 'ln1_b', 'w_ff_in', 'w_ff_out', 'ln2_g', 'ln2_b']
BIG = ['w_in', 'w_out', 'w_ff_in', 'w_ff_out']
BIG_ROWS = [5664, 2048, 4096, 4096]
SMALL = ['b_ada', 'conv_b', 'ssm_norm_w', 'attn_norm_w', 'ln1_g', 'ln1_b', 'ln2_g', 'ln2_b', 'dt_bias', 'a_log', 'd_skip',
         'f_bias', 'conv_w']


def _pad_lanes(v, n=128):
    return jnp.pad(v, ((0, 0), (0, n - v.shape[1])))


def _small_block(vals):
    rows = [_pad_lanes(vals[n].reshape(1, -1), -(-vals[n].size // 128) * 128).reshape(-1, 128) for n in SMALL]
    block = jnp.concatenate(rows, axis=0)
    return jnp.pad(block, ((0, 120 - block.shape[0]), (0, 0)))


def _small_unblock(block, like):
    out, r = {}, 0
    for n in SMALL:
        size = like[n].size
        nr = -(-size // 128)
        out[n] = block[r:r + nr].reshape(-1)[:size].reshape(like[n].shape)
        r += nr
    return out


def kernel(x, c, w_ada, b_ada, w_in, conv_w, conv_b, dt_bias, a_log, d_skip, ssm_norm_w, f_bias, attn_norm_w, w_out, ln1_g, ln1_b, w_ff_in, w_ff_out, ln2_g, ln2_b, loss_target, m_w_ada, m_b_ada, m_w_in, m_conv_w, m_conv_b, m_dt_bias, m_a_log, m_d_skip, m_ssm_norm_w, m_f_bias, m_attn_norm_w, m_w_out, m_ln1_g, m_ln1_b, m_w_ff_in, m_w_ff_out, m_ln2_g, m_ln2_b, v_w_ada, v_b_ada, v_w_in, v_conv_w, v_conv_b, v_dt_bias, v_a_log, v_d_skip, v_ssm_norm_w, v_f_bias, v_attn_norm_w, v_w_out, v_ln1_g, v_ln1_b, v_w_ff_in, v_w_ff_out, v_ln2_g, v_ln2_b):
    args = dict(locals())
    w = {n: args[n] for n in WEIGHTS}
    m = {n: args['m_' + n] for n in WEIGHTS}
    v = {n: args['v_' + n] for n in WEIGHTS}
    me = 4 * lax.axis_index("x") + 2 * lax.axis_index("y") + lax.axis_index("c")
    ada_cols = 6 * D // N_DEV
    conv_cols = conv_w.shape[2]

    tiny = jnp.concatenate([c.reshape(-1, 128), conv_w.reshape(-1, 128)], axis=0)
    tiny_all = _exchange("gather_cond", tiny, False)
    c_all = tiny_all[:, :D // 128].reshape(N_DEV, D)
    conv_w_full = tiny_all[:, D // 128:].reshape(N_DEV, 4, conv_cols).transpose(1, 0, 2).reshape(4, N_DEV * conv_cols)
    b_shard = lax.dynamic_slice(b_ada, (0, me * ada_cols), (1, ada_cols))
    mod_all = _exchange("gather_mod", _ada_mod(c_all, w_ada[0], b_shard), False)
    mod = lax.dynamic_index_in_dim(mod_all, me, axis=1, keepdims=False).reshape(1, 6 * D)

    flat = jnp.concatenate([w[n].reshape(-1, 128) for n in BIG], axis=0).astype(BF16)
    w_all = _exchange("gather_weights", flat, False)
    r0, r1, r2 = BIG_ROWS[0], BIG_ROWS[0] + BIG_ROWS[1], BIG_ROWS[0] + BIG_ROWS[1] + BIG_ROWS[2]
    cols_of = lambda blk, width: blk.reshape(N_DEV, D, width // N_DEV).transpose(1, 0, 2).reshape(D, width)
    wcat = _pack_w_in(cols_of(w_all[:, :r0], IN_COLS))
    wout = w_all[:, r0:r1].reshape(2 * D, D)
    w1 = cols_of(w_all[:, r1:r2], DFF)
    w2 = w_all[:, r2:].reshape(DFF, D)

    out = _local_step(x[0], loss_target[0], mod, wcat, wout, w1, w2, conv_w_full, conv_b, dt_bias, a_log, d_skip,
                      ssm_norm_w, f_bias, attn_norm_w, ln1_g, ln1_b, ln2_g, ln2_b)

    to_cols = lambda g, width: g.reshape(D, N_DEV, width // N_DEV).transpose(1, 0, 2).reshape(N_DEV, -1, 128)
    g_flat = jnp.concatenate([to_cols(out['d_w_in'], IN_COLS), out['d_w_out'].reshape(N_DEV, -1, 128),
                              to_cols(out['d_w_ff_in'], DFF), out['d_w_ff_out'].reshape(N_DEV, -1, 128)], axis=1)
    g_parts = _exchange("scatter_grads", g_flat.astype(BF16), True)
    flat_of = lambda d: jnp.concatenate([d[n].reshape(-1, 128) for n in BIG], axis=0)
    big = _adamw("adamw_large", flat_of(w), g_parts, flat_of(m), flat_of(v), tr=sum(BIG_ROWS) // 7, slots=True)

    small = jnp.concatenate(
        [out['d_mod'], out['d_conv_w'].reshape(1, -1), out['d_conv_b'], out['d_ssm_norm_w'], out['d_attn_norm_w'],
         out['d_ln1_g'], out['d_ln1_b'], out['d_ln2_g'], out['d_ln2_b'], out['d_gate_bias'], out['d_a_log'],
         out['d_d_skip'], jnp.zeros((1, 128), F32)], axis=1).reshape(-1, 128)
    small_all = _exchange("gather_small", small, False)
    ssum = _sum_slots("sum_small", small_all)
    dmod_all = small_all[:, :6 * D // 128].reshape(N_DEV, 6 * D)
    g_w_ada, g_b_ada = _ada_grad(c_all, lax.dynamic_slice(dmod_all, (0, me * ada_cols), (N_DEV, ada_cols)), dmod_all)
    rows = lambda a, b: ssum[a:b].reshape(1, -1)
    g_conv_w = lax.dynamic_slice(ssum[48:96].reshape(4, N_DEV * conv_cols), (0, me * conv_cols), (4, conv_cols))
    g_small = dict(b_ada=g_b_ada, conv_w=g_conv_w[None], conv_b=rows(96, 108), ssm_norm_w=rows(108, 116),
                   attn_norm_w=rows(116, 124), ln1_g=rows(124, 132), ln1_b=rows(132, 140), ln2_g=rows(140, 148),
                   ln2_b=rows(148, 156), dt_bias=ssum[156:157, :NH], f_bias=ssum[156:157, NH:2 * NH],
                   a_log=ssum[157:158, :NH], d_skip=ssum[158:159, :NH])
    sm = _adamw("adamw_small", _small_block(w), _small_block(g_small), _small_block(m), _small_block(v), tr=120, slots=False)
    ada = _adamw("adamw_ada", w_ada[0], g_w_ada, m_w_ada[0], v_w_ada[0], tr=256, slots=False)

    results = []
    for k in range(4):
        vals = _small_unblock(sm[k], w)
        vals['w_ada'] = ada[k][None]
        r = 0
        for n, nr in zip(BIG, BIG_ROWS):
            vals[n] = big[k][r:r + nr].reshape(w[n].shape)
            r += nr
        results.append(vals)
    loss = lax.psum(out['loss'], ("x", "y", "c"))
    return (loss, out['grad_x'][None], *[res[n] for res in results for n in WEIGHTS])
```

```python
import functools

import jax
import jax.numpy as jnp
from jax import lax
from jax.experimental import pallas as pl
from jax.experimental.pallas import tpu as pltpu

F32, BF16 = jnp.float32, jnp.bfloat16

N_DEV = 8
D = 1024
NH, HD = 16, 64
NSTATE = 128
CHUNK = 128
HG = 8
DFF = 4096
ALPHA = 2.0 ** 0.25
EPS = 1e-5
ATT_SCALE = HD ** -0.5

OFF_Z, OFF_XS, OFF_Q, OFF_K, OFF_V, OFF_BC, OFF_DTF = 0, 1024, 2048, 3072, 4096, 5120, 5632
PCOLS = 5760
W_Z, W_XS, W_BC, W_DT, W_Q, W_K, W_V, W_F = 0, 1024, 2048, 2560, 2576, 3600, 4624, 5648
IN_COLS = 5664

ADAM_LR, ADAM_B1, ADAM_B2, ADAM_EPS, ADAM_WD, ADAM_STEP = 0.001, 0.9, 0.999, 1e-08, 0.01, 10

VMEM_LIMIT = 56 << 20

NN = (((1,), (0,)), ((), ()))
NT = (((1,), (1,)), ((), ()))
TN = (((0,), (0,)), ((), ()))


def _dot(a, b, dims=NN):
    return lax.dot_general(a, b, dims, preferred_element_type=F32)


def _bdot(a, b, dims=NN):
    return _dot(a.astype(BF16), b.astype(BF16), dims)


def _split3(v):
    parts, rest = [], v
    for _ in range(3):
        p = rest.astype(BF16)
        parts.append(p)
        rest = rest - p.astype(F32)
    return parts


def _sel_left(m01, v):
    return sum(_dot(m01, p) for p in _split3(v))


def _sel_right(v, m01, dims=NN):
    return sum(_dot(p, m01, dims) for p in _split3(v))


def _iota(shape, dim):
    return lax.broadcasted_iota(jnp.int32, shape, dim)


def _tri_lower(n):
    return (_iota((n, n), 1) <= _iota((n, n), 0)).astype(BF16)


def _tri_upper(n):
    return (_iota((n, n), 1) >= _iota((n, n), 0)).astype(BF16)


def _head_expand():
    return (lax.shift_right_logical(_iota((128, D), 1), 6) == _iota((128, D), 0)).astype(BF16)


def _head_reduce():
    return (lax.shift_right_logical(_iota((D, 128), 0), 6) == _iota((D, 128), 1)).astype(BF16)


def _sigmoid(x):
    return 1.0 / (1.0 + jnp.exp(-x))


def _silu(x):
    return x * _sigmoid(x)


def _dsilu(x):
    s = _sigmoid(x)
    return s * (1.0 + x * (1.0 - s))


def _softplus(x):
    return jnp.maximum(x, 0.0) + jnp.log(1.0 + jnp.exp(-jnp.abs(x)))


def _log_sigmoid(x):
    return jnp.minimum(x, 0.0) - jnp.log(1.0 + jnp.exp(-jnp.abs(x)))


def _params(sem):
    return pltpu.CompilerParams(dimension_semantics=sem, vmem_limit_bytes=VMEM_LIMIT)


def _mm_nn(name, a, b, *, tm, tn, tk, out_dtype, pro=None, aux=()):
    m, k_all = a.shape
    b_sharded = b.ndim == 3
    n = b.shape[0] * b.shape[2] if b_sharded else b.shape[1]
    assert not b_sharded or tn == b.shape[2]
    nk = k_all // tk
    n_aux = len(aux)
    b_spec = (pl.BlockSpec((None, tk, tn), lambda i, j, k: (j, k, 0)) if b_sharded
              else pl.BlockSpec((tk, tn), lambda i, j, k: (k, j)))

    def body(a_ref, b_ref, *rest):
        aux_refs, o_ref = rest[:n_aux], rest[n_aux]
        at = a_ref[...]
        if pro is not None:
            at = pro(at, *[r[...] for r in aux_refs])
        part = _bdot(at, b_ref[...])
        if nk == 1:
            o_ref[...] = part.astype(out_dtype)
            return
        acc_ref = rest[n_aux + 1]
        kk = pl.program_id(2)

        @pl.when(kk == 0)
        def _():
            acc_ref[...] = part

        @pl.when(kk > 0)
        def _():
            acc_ref[...] += part

        @pl.when(kk == nk - 1)
        def _():
            o_ref[...] = acc_ref[...].astype(out_dtype)

    return pl.pallas_call(
        body, name=name,
        grid=(m // tm, n // tn, nk),
        in_specs=[pl.BlockSpec((tm, tk), lambda i, j, k: (i, k)), b_spec]
        + [pl.BlockSpec((1, tk), lambda i, j, k: (0, k)) for _ in aux],
        out_specs=pl.BlockSpec((tm, tn), lambda i, j, k: (i, j)),
        out_shape=jax.ShapeDtypeStruct((m, n), out_dtype),
        scratch_shapes=[] if nk == 1 else [pltpu.VMEM((tm, tn), F32)],
        compiler_params=_params(("parallel", "parallel", "arbitrary")),
    )(a, b, *aux)


def _mm_nt(name, a_list, b_list, *, n, tm, tn, out_dtype, epi=None, epi_aux=()):
    m = a_list[0][0].shape[0]
    n_op = len(a_list)
    n_epi = len(epi_aux)

    def body(*refs):
        a_refs, b_refs = refs[:n_op], refs[n_op:2 * n_op]
        e_refs, o_ref = refs[2 * n_op:2 * n_op + n_epi], refs[2 * n_op + n_epi]
        acc = None
        for a_ref, b_ref in zip(a_refs, b_refs):
            part = _bdot(a_ref[...], b_ref[...], NT)
            acc = part if acc is None else acc + part
        if epi is not None:
            acc = epi(acc, *[r[...] for r in e_refs])
        o_ref[...] = acc.astype(out_dtype)

    in_specs = [pl.BlockSpec((tm, w), functools.partial(lambda i, j, cb: (i, cb), cb=cb)) for (_, w, cb) in a_list]
    for (b, w, cb) in b_list:
        if b.ndim == 3:
            in_specs.append(pl.BlockSpec((None, tn, w), functools.partial(lambda i, j, cb: (cb, j, 0), cb=cb)))
        else:
            in_specs.append(pl.BlockSpec((tn, w), functools.partial(lambda i, j, cb: (j, cb), cb=cb)))
    in_specs += [pl.BlockSpec((tm, tn), lambda i, j: (i, j)) for _ in epi_aux]
    return pl.pallas_call(
        body, name=name,
        grid=(m // tm, n // tn),
        in_specs=in_specs,
        out_specs=pl.BlockSpec((tm, tn), lambda i, j: (i, j)),
        out_shape=jax.ShapeDtypeStruct((m, n), out_dtype),
        compiler_params=_params(("parallel", "parallel")),
    )(*[a for (a, _, _) in a_list], *[b for (b, _, _) in b_list], *epi_aux)


def _mm_tn(name, a, b, *, tm, tn, ts, pro=None, aux=(), col_shards=False):
    s_all, ka = a.shape
    nb = b.shape[1]
    n_aux = len(aux)
    ns = s_all // ts
    assert not col_shards or tn == nb // N_DEV

    def body(a_ref, b_ref, *rest):
        aux_refs, o_ref, acc_ref = rest[:n_aux], rest[n_aux], rest[n_aux + 1]
        at = a_ref[...]
        if pro is not None:
            at = pro(at, *[r[...] for r in aux_refs])
        part = _bdot(at, b_ref[...], TN)
        ss = pl.program_id(2)

        @pl.when(ss == 0)
        def _():
            acc_ref[...] = part

        @pl.when(ss > 0)
        def _():
            acc_ref[...] += part

        @pl.when(ss == ns - 1)
        def _():
            o_ref[...] = acc_ref[...].astype(BF16)

    if col_shards:
        out_spec = pl.BlockSpec((None, tm, tn), lambda i, j, s: (j, i, 0))
        out_shape = jax.ShapeDtypeStruct((N_DEV, ka, tn), BF16)
    else:
        out_spec = pl.BlockSpec((tm, tn), lambda i, j, s: (i, j))
        out_shape = jax.ShapeDtypeStruct((ka, nb), BF16)
    return pl.pallas_call(
        body, name=name,
        grid=(ka // tm, nb // tn, ns),
        in_specs=[pl.BlockSpec((ts, tm), lambda i, j, s: (s, i)),
                  pl.BlockSpec((ts, tn), lambda i, j, s: (s, j))]
        + [pl.BlockSpec((1, tm), lambda i, j, s: (0, i)) for _ in aux],
        out_specs=out_spec, out_shape=out_shape,
        scratch_shapes=[pltpu.VMEM((tm, tn), F32)],
        compiler_params=_params(("parallel", "parallel", "arbitrary")),
    )(a, b, *aux)


def _rowk(name, fn, n_rows, tr, rows, fulls, outs, accs, reverse=False):
    n = n_rows // tr
    n_row, n_full, n_out, n_acc = len(rows), len(fulls), len(outs), len(accs)

    def pos(i):
        return (n - 1 - i) if reverse else i

    def body(*refs):
        row_refs = refs[:n_row]
        full_refs = refs[n_row:n_row + n_full]
        out_refs = refs[n_row + n_full:n_row + n_full + n_out]
        acc_refs = refs[n_row + n_full + n_out:]
        i = pl.program_id(0)

        @pl.when(i == 0)
        def _():
            for r in acc_refs:
                r[...] = jnp.zeros(r.shape, r.dtype)

        res = fn(pos(i), *[r[...] for r in row_refs], *[r[...] for r in full_refs], *[r[...] for r in acc_refs])
        for r, v in zip(out_refs + acc_refs, res):
            r[...] = v.astype(r.dtype)

    def row_map(i, cb, shift):
        return (jnp.clip(pos(i) + shift, 0, n - 1), cb)

    in_specs = [pl.BlockSpec((tr, w), functools.partial(row_map, cb=cb, shift=sh)) for (_, w, cb, sh) in rows]
    in_specs += [pl.BlockSpec(f.shape, functools.partial(lambda i, nd: (0,) * nd, nd=f.ndim)) for f in fulls]
    out_specs = [pl.BlockSpec((tr, w), lambda i: (pos(i), 0)) for (w, _) in outs]
    out_specs += [pl.BlockSpec((r, w), lambda i: (0, 0)) for (r, w) in accs]
    out_shape = [jax.ShapeDtypeStruct((n_rows, w), dt) for (w, dt) in outs]
    out_shape += [jax.ShapeDtypeStruct((r, w), F32) for (r, w) in accs]
    return pl.pallas_call(
        body, name=name, grid=(n,), in_specs=in_specs, out_specs=out_specs, out_shape=out_shape,
        compiler_params=_params(("arbitrary",)),
    )(*[a for (a, _, _, _) in rows], *fulls)


def _colsum(x):
    return jnp.sum(x, axis=0, keepdims=True)


def _mean(x):
    return jnp.mean(x, axis=-1, keepdims=True)


def _modulate(x, sc, sh):
    return x * (1.0 + sc) + sh


def _shift_down(cur, prev, j):
    row = _iota(cur.shape, 0)
    return jnp.where(row < j, pltpu.roll(prev, j, 0), pltpu.roll(cur, j, 0))


def _shift_up(cur, nxt, j):
    tr = cur.shape[0]
    row = _iota(cur.shape, 0)
    return jnp.where(row < tr - j, pltpu.roll(cur, tr - j, 0), pltpu.roll(nxt, tr - j, 0))


def _conv(cur, prev, w, b):
    out = cur * w[3:4] + b
    for j in (1, 2, 3):
        out = out + _shift_down(cur, prev, j) * w[3 - j:4 - j]
    return out


def _conv_fwd(p, w_xs, b_xs, w_bc, b_bc, s):
    def fn(pos, xs, xs_prev, bc, bc_prev, w_xs, b_xs, w_bc, b_bc):
        first = pos == 0
        xs_prev = jnp.where(first, 0.0, xs_prev)
        bc_prev = jnp.where(first, 0.0, bc_prev)
        return _silu(_conv(xs, xs_prev, w_xs, b_xs)), _silu(_conv(bc, bc_prev, w_bc, b_bc))

    return _rowk("conv_fwd", fn, s, 256,
                 [(p, D, OFF_XS // D, 0), (p, D, OFF_XS // D, -1), (p, 512, OFF_BC // 512, 0), (p, 512, OFF_BC // 512, -1)],
                 [w_xs, b_xs, w_bc, b_bc], [(D, F32), (512, F32)], [])


def _conv_bwd(dxs_a, dbc_a, p, w_xs, b_xs, w_bc, b_bc, s):
    tr = 256
    n = s // tr

    def fn(pos, da1, da1n, x1, x1p, x1n, da2, da2n, x2, x2p, x2n, w1, b1, w2, b2, aw1, ab1, aw2, ab2):
        dx1, dw1, db1 = _conv_bwd_fn(pos, n, da1, da1n, x1, x1p, x1n, w1, b1)
        dx2, dw2, db2 = _conv_bwd_fn(pos, n, da2, da2n, x2, x2p, x2n, w2, b2)
        return dx1, dx2, aw1 + dw1, ab1 + db1, aw2 + dw2, ab2 + db2

    cx, cb = OFF_XS // D, OFF_BC // 512
    return _rowk("conv_bwd", fn, s, tr,
                 [(dxs_a, D, 0, 0), (dxs_a, D, 0, 1), (p, D, cx, 0), (p, D, cx, -1), (p, D, cx, 1),
                  (dbc_a, 512, 0, 0), (dbc_a, 512, 0, 1), (p, 512, cb, 0), (p, 512, cb, -1), (p, 512, cb, 1)],
                 [w_xs, b_xs, w_bc, b_bc], [(D, BF16), (512, BF16)], [(8, D), (1, D), (8, 512), (1, 512)])


def _conv_bwd_fn(pos, n, da, da_next, x, x_prev, x_next, w, b):
    first, last = pos == 0, pos == n - 1
    x_prev = jnp.where(first, 0.0, x_prev)
    dc = da * _dsilu(_conv(x, x_prev, w, b))
    dc_next = jnp.where(last, 0.0, da_next * _dsilu(_conv(x_next, x, w, b)))
    dx = dc * w[3:4]
    dws = [None] * 4
    dws[3] = _colsum(dc * x)
    for j in (1, 2, 3):
        dx = dx + _shift_up(dc, dc_next, j) * w[3 - j:4 - j]
        dws[3 - j] = _colsum(dc * _shift_down(x, x_prev, j))
    row = _iota((8, x.shape[1]), 0)
    dw = jnp.zeros((8, x.shape[1]), F32)
    for k in range(4):
        dw = jnp.where(row == k, dws[k], dw)
    return dx, dw, _colsum(dc)


def _ssd_gates(dtf, bias, a_log):
    lane = _iota(dtf.shape, 1)
    head = lane < NH
    dt = jnp.where(head, _softplus(dtf + bias), 0.0)
    a_neg = jnp.where(_iota(a_log.shape, 1) < NH, -jnp.exp(a_log), 0.0)
    a = dt * a_neg
    cs = _sel_left(_tri_lower(CHUNK), a)
    return dt, a_neg, cs


def _decay_mask(cs_ref, cst_ref, h):
    diff = cs_ref[:, h:h + 1] - cst_ref[h:h + 1, :]
    low = _iota((CHUNK, CHUNK), 1) <= _iota((CHUNK, CHUNK), 0)
    return jnp.where(low, jnp.exp(jnp.minimum(diff, 0.0)), 0.0)


def _ssd_fwd(xs_a, bc_a, p, bias128, alog128, dskip_x, s):
    nc = s // CHUNK
    t = CHUNK

    def body(xs_ref, bc_ref, dtf_ref, bias_ref, alog_ref, dsk_ref, y_ref, st_ref,
             state, x_sc, xw_sc, cs_sc, cst_sc, yd_sc):
        c = pl.program_id(0)

        @pl.when(c == 0)
        def _():
            state[...] = jnp.zeros(state.shape, F32)

        dt, _, cs = _ssd_gates(dtf_ref[...], bias_ref[...], alog_ref[...])
        cs_sc[...] = cs
        cst_sc[...] = cs.T
        cs_last = cs[t - 1:t, :]
        expand = _head_expand()
        ex = _sel_right(jnp.concatenate([dt, jnp.exp(cs), jnp.exp(cs_last - cs)], axis=0), expand)
        dt_x, eo_x, we_x = ex[0:t], ex[t:2 * t], ex[2 * t:3 * t]
        g_x = _sel_right(jnp.broadcast_to(jnp.exp(cs_last), (8, 128)), expand)[0:1]
        xs = xs_ref[...]
        x = xs * dt_x
        x_sc[...] = x.astype(BF16)
        xw_sc[...] = (x * we_x).astype(BF16)
        prev = state[...]
        st_ref[0] = prev
        prev_b = prev.astype(BF16)
        for g in range(2):
            cols = slice(g * 512, (g + 1) * 512)
            b_g = bc_ref[:, g * 128:(g + 1) * 128].astype(BF16)
            c_g = bc_ref[:, 256 + g * 128:256 + (g + 1) * 128].astype(BF16)
            gmat = _dot(c_g, b_g, NT)
            y_off = _dot(c_g, prev_b[:, cols]) * eo_x[:, cols]
            s_loc = _dot(b_g, xw_sc[:, cols], TN)
            state[:, cols] = g_x[:, cols] * prev[:, cols] + s_loc
            for e in range(HG):
                h = g * HG + e
                m = gmat * _decay_mask(cs_sc, cst_sc, h)
                yd_sc[:, h * HD:(h + 1) * HD] = _dot(m.astype(BF16), x_sc[:, h * HD:(h + 1) * HD])
            y_ref[:, cols] = yd_sc[:, cols] + y_off + dsk_ref[:, cols] * xs[:, cols]

    return pl.pallas_call(
        body, name="ssd_fwd", grid=(nc,),
        in_specs=[pl.BlockSpec((t, D), lambda c: (c, 0)),
                  pl.BlockSpec((t, 512), lambda c: (c, 0)),
                  pl.BlockSpec((t, 128), lambda c: (c, OFF_DTF // 128)),
                  pl.BlockSpec((1, 128), lambda c: (0, 0)),
                  pl.BlockSpec((1, 128), lambda c: (0, 0)),
                  pl.BlockSpec((1, D), lambda c: (0, 0))],
        out_specs=[pl.BlockSpec((t, D), lambda c: (c, 0)),
                   pl.BlockSpec((1, NSTATE, D), lambda c: (c, 0, 0))],
        out_shape=[jax.ShapeDtypeStruct((s, D), F32), jax.ShapeDtypeStruct((nc, NSTATE, D), F32)],
        scratch_shapes=[pltpu.VMEM((NSTATE, D), F32), pltpu.VMEM((t, D), BF16), pltpu.VMEM((t, D), BF16),
                        pltpu.VMEM((t, 128), F32), pltpu.VMEM((128, t), F32), pltpu.VMEM((t, D), F32)],
        compiler_params=_params(("arbitrary",)),
    )(xs_a, bc_a, p, bias128, alog128, dskip_x)


def _ssd_bwd(dy, xs_a, bc_a, p, states, bias128, alog128, dskip_x, s):
    nc = s // CHUNK
    t = CHUNK

    def body(dy_ref, xs_ref, bc_ref, dtf_ref, st_ref, bias_ref, alog_ref, dsk_ref,
             dxs_ref, dbc_ref, ddt_ref, dalog_ref, dskip_ref,
             dstate, x_sc, dy_sc, dx_sc, deo_sc, dwe_sc, cs_sc, cst_sc, dcol_sc, drow_sc):
        i = pl.program_id(0)

        @pl.when(i == 0)
        def _():
            dstate[...] = jnp.zeros(dstate.shape, F32)
            dalog_ref[...] = jnp.zeros(dalog_ref.shape, F32)
            dskip_ref[...] = jnp.zeros(dskip_ref.shape, F32)

        dtf = dtf_ref[...]
        dt, a_neg, cs = _ssd_gates(dtf, bias_ref[...], alog_ref[...])
        cs_sc[...] = cs
        cst_sc[...] = cs.T
        cs_last = cs[t - 1:t, :]
        eo, we, g_end = jnp.exp(cs), jnp.exp(cs_last - cs), jnp.exp(cs_last)
        expand, reduce = _head_expand(), _head_reduce()
        ex = _sel_right(jnp.concatenate([dt, eo, we], axis=0), expand)
        dt_x, eo_x, we_x = ex[0:t], ex[t:2 * t], ex[2 * t:3 * t]
        g_x = _sel_right(jnp.broadcast_to(g_end, (8, 128)), expand)[0:1]
        xs = xs_ref[...]
        dyv = dy_ref[...]
        x = xs * dt_x
        x_sc[...] = x.astype(BF16)
        dy_sc[...] = dyv.astype(BF16)
        dyo_b = (dyv * eo_x).astype(BF16)
        xw_b = (x * we_x).astype(BF16)
        prev = st_ref[0]
        prev_b = prev.astype(BF16)
        dnext = dstate[...]
        dnext_b = dnext.astype(BF16)
        dcol_sc[...] = jnp.zeros(dcol_sc.shape, F32)
        drow_sc[...] = jnp.zeros(drow_sc.shape, F32)
        lane_row = _iota((1, 128), 1)
        sub_col = _iota((128, 1), 0)
        for g in range(2):
            cols = slice(g * 512, (g + 1) * 512)
            b_g = bc_ref[:, g * 128:(g + 1) * 128].astype(BF16)
            c_g = bc_ref[:, 256 + g * 128:256 + (g + 1) * 128].astype(BF16)
            gmat = _dot(c_g, b_g, NT)
            b_ds = _dot(b_g, dnext_b[:, cols])
            c_s = _dot(c_g, prev_b[:, cols])
            dx_sc[:, cols] = b_ds * we_x[:, cols]
            deo_sc[:, cols] = dyv[:, cols] * c_s
            dwe_sc[:, cols] = b_ds * x[:, cols]
            db = _dot(xw_b[:, cols], dnext_b[:, cols], NT)
            dc = _dot(dyo_b[:, cols], prev_b[:, cols], NT)
            dstate[:, cols] = g_x[:, cols] * dnext[:, cols] + _dot(c_g, dyo_b[:, cols], TN)
            dg = jnp.zeros((t, t), F32)
            for e in range(HG):
                h = g * HG + e
                hc = slice(h * HD, (h + 1) * HD)
                lmat = _decay_mask(cs_sc, cst_sc, h)
                m = gmat * lmat
                dx_sc[:, hc] += _dot(m.astype(BF16), dy_sc[:, hc], TN)
                dm = _dot(dy_sc[:, hc], x_sc[:, hc], NT)
                dg = dg + dm * lmat
                qm = dm * m
                dcol_sc[...] += jnp.sum(qm, axis=1, keepdims=True) * (lane_row == h).astype(F32)
                drow_sc[...] += (sub_col == h).astype(F32) * jnp.sum(qm, axis=0, keepdims=True)
            dg_b = dg.astype(BF16)
            dbc_ref[:, g * 128:(g + 1) * 128] = db + _dot(dg_b, c_g, TN)
            dbc_ref[:, 256 + g * 128:256 + (g + 1) * 128] = dc + _dot(dg_b, b_g)
        d_eo = _sel_right(deo_sc[...], reduce)
        d_we = _sel_right(dwe_sc[...], reduce)
        d_gend = _sel_right(jnp.broadcast_to(_colsum(dnext * prev), (8, D)), reduce)[0:1]
        d_cs = dcol_sc[...] - drow_sc[...].T + d_eo * eo - d_we * we
        extra = _colsum(d_we * we) + d_gend * g_end
        d_cs = d_cs + jnp.where(_iota((t, 128), 0) == t - 1, extra, 0.0)
        da = _sel_left(_tri_upper(t), d_cs)
        dx = dx_sc[...]
        ddt = _sel_right(dx * xs, reduce) + da * a_neg
        dxs_ref[...] = dx * dt_x + dsk_ref[...] * dyv
        ddt_ref[...] = jnp.where(_iota((t, 128), 1) < NH, ddt * _sigmoid(dtf + bias_ref[...]), 0.0)
        dalog_ref[...] += _colsum(da * dt) * a_neg
        dskip_ref[...] += _sel_right(jnp.broadcast_to(_colsum(dyv * xs), (8, D)), reduce)[0:1]

    rev = lambda i: nc - 1 - i
    return pl.pallas_call(
        body, name="ssd_bwd", grid=(nc,),
        in_specs=[pl.BlockSpec((t, D), lambda i: (rev(i), 0)),
                  pl.BlockSpec((t, D), lambda i: (rev(i), 0)),
                  pl.BlockSpec((t, 512), lambda i: (rev(i), 0)),
                  pl.BlockSpec((t, 128), lambda i: (rev(i), OFF_DTF // 128)),
                  pl.BlockSpec((1, NSTATE, D), lambda i: (rev(i), 0, 0)),
                  pl.BlockSpec((1, 128), lambda i: (0, 0)),
                  pl.BlockSpec((1, 128), lambda i: (0, 0)),
                  pl.BlockSpec((1, D), lambda i: (0, 0))],
        out_specs=[pl.BlockSpec((t, D), lambda i: (rev(i), 0)),
                   pl.BlockSpec((t, 512), lambda i: (rev(i), 0)),
                   pl.BlockSpec((t, 128), lambda i: (rev(i), 0)),
                   pl.BlockSpec((1, 128), lambda i: (0, 0)),
                   pl.BlockSpec((1, 128), lambda i: (0, 0))],
        out_shape=[jax.ShapeDtypeStruct((s, D), F32), jax.ShapeDtypeStruct((s, 512), F32),
                   jax.ShapeDtypeStruct((s, 128), F32), jax.ShapeDtypeStruct((1, 128), F32),
                   jax.ShapeDtypeStruct((1, 128), F32)],
        scratch_shapes=[pltpu.VMEM((NSTATE, D), F32), pltpu.VMEM((t, D), BF16), pltpu.VMEM((t, D), BF16),
                        pltpu.VMEM((t, D), F32), pltpu.VMEM((t, D), F32), pltpu.VMEM((t, D), F32),
                        pltpu.VMEM((t, 128), F32), pltpu.VMEM((128, t), F32),
                        pltpu.VMEM((t, 128), F32), pltpu.VMEM((128, t), F32)],
        compiler_params=_params(("arbitrary",)),
    )(dy, xs_a, bc_a, p, states, bias128, alog128, dskip_x)


def _gate_lanes(shape):
    lane = _iota(shape, 1)
    return (lane >= NH) & (lane < 2 * NH)


def _cum_fwd(p, bias128, s):
    tr = min(512, s)

    def body(dtf_ref, bias_ref, o_ref, carry):
        @pl.when(pl.program_id(0) == 0)
        def _():
            carry[...] = jnp.zeros(carry.shape, F32)

        lf = jnp.where(_gate_lanes((tr, 128)), _log_sigmoid(dtf_ref[...] + bias_ref[...]), 0.0)
        cum = _sel_left(_tri_lower(tr), lf) + carry[...]
        carry[...] = cum[tr - 1:tr, :]
        o_ref[...] = cum.T[NH:2 * NH, :]

    return pl.pallas_call(
        body, name="cum_fwd", grid=(s // tr,),
        in_specs=[pl.BlockSpec((tr, 128), lambda i: (i, OFF_DTF // 128)), pl.BlockSpec((1, 128), lambda i: (0, 0))],
        out_specs=pl.BlockSpec((NH, tr), lambda i: (0, i)),
        out_shape=jax.ShapeDtypeStruct((NH, s), F32),
        scratch_shapes=[pltpu.VMEM((1, 128), F32)],
        compiler_params=_params(("arbitrary",)),
    )(p, bias128)


def _cum_bwd(dcum, ddt_raw, p, bias128, s):
    tr = min(512, s)

    def fn(pos, dcum, ddt, dtf, bias, carry, acc):
        suffix = _sel_left(_tri_upper(tr), dcum) + carry
        dfr = jnp.where(_gate_lanes((tr, 128)), suffix * _sigmoid(-(dtf + bias)), 0.0)
        out = ddt + dfr
        return out, suffix[0:1, :], acc + _colsum(out)

    return _rowk("cum_bwd", fn, s, tr, [(dcum, 128, 0, 0), (ddt_raw, 128, 0, 0), (p, 128, OFF_DTF // 128, 0)],
                 [bias128], [(128, BF16)], [(1, 128), (1, 128)], reverse=True)


def _attn_masked_logits(q_h, k_h, crow, qi, ki, tq, tk):
    row = qi * tq + _iota((tq, tk), 0)
    col = ki * tk + _iota((tq, tk), 1)
    return jnp.where(col <= row, _dot(q_h, k_h, NT) - crow, -1e30)


def _attn_fwd(p, cum_t, s):
    tq = tk = min(512, s)
    nq = s // tq

    def body(q_ref, k_ref, v_ref, c_ref, o_ref, lse_ref, m_sc, l_sc, acc_sc):
        j, qi, ki = pl.program_id(0), pl.program_id(1), pl.program_id(2)

        @pl.when(ki == 0)
        def _():
            m_sc[...] = jnp.full(m_sc.shape, -1e30, F32)
            l_sc[...] = jnp.zeros(l_sc.shape, F32)
            acc_sc[...] = jnp.zeros(acc_sc.shape, F32)

        @pl.when(ki <= qi)
        def _():
            q = (q_ref[...] * ATT_SCALE).astype(BF16)
            k = k_ref[...].astype(BF16)
            v = v_ref[...].astype(BF16)
            for hh in range(2):
                hc = slice(hh * HD, (hh + 1) * HD)
                crow = c_ref[pl.ds(2 * j + hh, 1), :]
                sc = _attn_masked_logits(q[:, hc], k[:, hc], crow, qi, ki, tq, tk)
                m_prev = m_sc[hh]
                m_new = jnp.maximum(m_prev, jnp.max(sc, axis=1, keepdims=True))
                pr = jnp.exp(sc - m_new[:, 0:1])
                alpha = jnp.exp(m_prev - m_new)
                l_sc[hh] = alpha * l_sc[hh] + jnp.sum(pr, axis=1, keepdims=True)
                m_sc[hh] = m_new
                acc_sc[:, hc] = alpha[:, :HD] * acc_sc[:, hc] + _dot(pr.astype(BF16), v[:, hc])

        @pl.when(ki == qi)
        def _():
            for hh in range(2):
                hc = slice(hh * HD, (hh + 1) * HD)
                l = l_sc[hh]
                o_ref[:, hc] = acc_sc[:, hc] / l[:, :HD]
                lse_ref[:, hc] = (m_sc[hh] + jnp.log(l))[:, :HD]

    def kv_map(j, qi, ki, off):
        return (jnp.minimum(ki, qi), off + j)

    return pl.pallas_call(
        body, name="attn_fwd", grid=(NH // 2, nq, nq),
        in_specs=[pl.BlockSpec((tq, 128), lambda j, qi, ki: (qi, OFF_Q // 128 + j)),
                  pl.BlockSpec((tk, 128), functools.partial(kv_map, off=OFF_K // 128)),
                  pl.BlockSpec((tk, 128), functools.partial(kv_map, off=OFF_V // 128)),
                  pl.BlockSpec((NH, tk), lambda j, qi, ki: (0, jnp.minimum(ki, qi)))],
        out_specs=[pl.BlockSpec((tq, 128), lambda j, qi, ki: (qi, j)),
                   pl.BlockSpec((tq, 128), lambda j, qi, ki: (qi, j))],
        out_shape=[jax.ShapeDtypeStruct((s, D), F32), jax.ShapeDtypeStruct((s, D), F32)],
        scratch_shapes=[pltpu.VMEM((2, tq, 128), F32), pltpu.VMEM((2, tq, 128), F32), pltpu.VMEM((tq, 128), F32)],
        compiler_params=_params(("parallel", "parallel", "arbitrary")),
    )(p, p, p, cum_t)


def _attn_bwd(p, cum_t, o, lse, do, s):
    tq = tk = min(512, s)
    nq = s // tq

    def body(q_ref, k_ref, v_ref, c_ref, o_ref, lse_ref, do_ref, dq_ref, dk_ref, dv_ref, dc_ref, dr_ref, dk_sc, dv_sc, dc_sc):
        j, ki, qi = pl.program_id(0), pl.program_id(1), pl.program_id(2)

        @pl.when(qi == ki)
        def _():
            dk_sc[...] = jnp.zeros(dk_sc.shape, F32)
            dv_sc[...] = jnp.zeros(dv_sc.shape, F32)
            dc_sc[...] = jnp.zeros(dc_sc.shape, F32)

        @pl.when(qi >= ki)
        def _():
            q = (q_ref[...] * ATT_SCALE).astype(BF16)
            k = k_ref[...].astype(BF16)
            v = v_ref[...].astype(BF16)
            dov, ov, lse_v = do_ref[...], o_ref[...], lse_ref[...]
            rows = pl.ds(pl.multiple_of(qi * tq, tq), tq)
            for hh in range(2):
                hc = slice(hh * HD, (hh + 1) * HD)
                crow = c_ref[pl.ds(2 * j + hh, 1), :]
                sc = _attn_masked_logits(q[:, hc], k[:, hc], crow, qi, ki, tq, tk)
                pr = jnp.exp(sc - lse_v[:, hh * HD:hh * HD + 1])
                do_h = dov[:, hc]
                delta = jnp.sum(do_h * ov[:, hc], axis=1, keepdims=True)
                do_b = do_h.astype(BF16)
                dv_sc[:, hc] += _dot(pr.astype(BF16), do_b, TN)
                ds = pr * (_dot(do_b, v[:, hc], NT) - delta)
                dc_sc[hh:hh + 1, :] += _colsum(ds)
                ds_b = ds.astype(BF16)
                dk_sc[:, hc] += _dot(ds_b, q[:, hc], TN)
                dq_h = _dot(ds_b, k[:, hc]) * ATT_SCALE
                drow = jnp.broadcast_to(jnp.sum(ds, axis=1, keepdims=True), (tq, HD))

                @pl.when(ki == 0)
                def _():
                    dq_ref[rows, hc] = dq_h
                    dr_ref[rows, hc] = drow

                @pl.when(ki > 0)
                def _():
                    dq_ref[rows, hc] += dq_h
                    dr_ref[rows, hc] += drow

        @pl.when(qi == nq - 1)
        def _():
            dk_ref[...] = dk_sc[...].astype(BF16)
            dv_ref[...] = dv_sc[...].astype(BF16)
            dc_ref[0] = dc_sc[...]

    def q_map(j, ki, qi, off):
        return (jnp.maximum(qi, ki), off + j)

    return pl.pallas_call(
        body, name="attn_bwd", grid=(NH // 2, nq, nq),
        in_specs=[pl.BlockSpec((tq, 128), functools.partial(q_map, off=OFF_Q // 128)),
                  pl.BlockSpec((tk, 128), lambda j, ki, qi: (ki, OFF_K // 128 + j)),
                  pl.BlockSpec((tk, 128), lambda j, ki, qi: (ki, OFF_V // 128 + j)),
                  pl.BlockSpec((NH, tk), lambda j, ki, qi: (0, ki)),
                  pl.BlockSpec((tq, 128), functools.partial(q_map, off=0)),
                  pl.BlockSpec((tq, 128), functools.partial(q_map, off=0)),
                  pl.BlockSpec((tq, 128), functools.partial(q_map, off=0))],
        out_specs=[pl.BlockSpec((s, 128), lambda j, ki, qi: (0, j)),
                   pl.BlockSpec((tk, 128), lambda j, ki, qi: (ki, j)),
                   pl.BlockSpec((tk, 128), lambda j, ki, qi: (ki, j)),
                   pl.BlockSpec((1, 8, tk), lambda j, ki, qi: (j, 0, ki)),
                   pl.BlockSpec((s, 128), lambda j, ki, qi: (0, j))],
        out_shape=[jax.ShapeDtypeStruct((s, D), F32), jax.ShapeDtypeStruct((s, D), BF16),
                   jax.ShapeDtypeStruct((s, D), BF16), jax.ShapeDtypeStruct((NH // 2, 8, s), F32),
                   jax.ShapeDtypeStruct((s, D), F32)],
        scratch_shapes=[pltpu.VMEM((tk, 128), F32), pltpu.VMEM((tk, 128), F32), pltpu.VMEM((8, tk), F32)],
        compiler_params=_params(("parallel", "arbitrary", "arbitrary")),
    )(p, p, p, cum_t, o, lse, do)


def _ln_stats(u):
    mu = _mean(u)
    d = u - mu
    rstd = lax.rsqrt(_mean(d * d) + EPS)
    return d * rstd, rstd


def _ln_bwd(dx, xh, rstd, gam):
    dxh = dx * gam
    return rstd * (dxh - _mean(dxh) - xh * _mean(dxh * xh))


def _rms_bwd(d, xn, r, w):
    t = d * w
    return r * (t - xn * _mean(t * xn)), _colsum(d * xn)


def _mix_norm(y, p, att, w_ssm, w_att, s):
    def fn(pos, y, z, att, w1, w2):
        g = y * _silu(z)
        n1 = g * lax.rsqrt(_mean(g * g) + EPS) * w1
        n2 = att * lax.rsqrt(_mean(att * att) + EPS) * w2
        return (jnp.concatenate([n1, n2], axis=1),)

    return _rowk("mix_norm", fn, s, 256, [(y, D, 0, 0), (p, D, OFF_Z // D, 0), (att, D, 0, 0)],
                 [w_ssm, w_att], [(2 * D, BF16)], [])[0]


def _mix_norm_bwd(dmix, y, p, att, w_ssm, w_att, s):
    def fn(pos, dmix, y, z, att, w1, w2, a1, a2):
        sz = _silu(z)
        g = y * sz
        r1 = lax.rsqrt(_mean(g * g) + EPS)
        dg, dw1 = _rms_bwd(dmix[:, :D], g * r1, r1, w1)
        r2 = lax.rsqrt(_mean(att * att) + EPS)
        datt, dw2 = _rms_bwd(dmix[:, D:], att * r2, r2, w2)
        return dg * sz, dg * y * _dsilu(z), datt, a1 + dw1, a2 + dw2

    return _rowk("mix_norm_bwd", fn, s, 256, [(dmix, 2 * D, 0, 0), (y, D, 0, 0), (p, D, OFF_Z // D, 0), (att, D, 0, 0)],
                 [w_ssm, w_att], [(D, F32), (D, BF16), (D, F32)], [(1, D), (1, D)])


def _ln1(x0, y, g1, gam, bet, sc2, sh2, s):
    def fn(pos, x0, y, g1, gam, bet, sc2, sh2):
        xh, _ = _ln_stats(ALPHA * x0 + (1.0 + g1) * y)
        x1 = xh * gam + bet
        return x1, _modulate(x1, sc2, sh2)

    return _rowk("ln1", fn, s, 256, [(x0, D, 0, 0), (y, D, 0, 0)], [g1, gam, bet, sc2, sh2], [(D, F32), (D, BF16)], [])


def _ln2_loss(x1, ff, tgt, g2, gam, bet, s):
    def fn(pos, x1, ff, tgt, g2, gam, bet, a_loss, a_dgam, a_dbet, a_dg2):
        xh, rstd = _ln_stats(ALPHA * x1 + (1.0 + g2) * ff)
        err = xh * gam + bet - tgt
        dx2 = err * (1.0 / D)
        du = _ln_bwd(dx2, xh, rstd, gam)
        return (du, du * (1.0 + g2), a_loss + _colsum(err * err), a_dgam + _colsum(dx2 * xh),
                a_dbet + _colsum(dx2), a_dg2 + _colsum(du * ff))

    return _rowk("ln2_loss", fn, s, 256, [(x1, D, 0, 0), (ff, D, 0, 0), (tgt, D, 0, 0)], [g2, gam, bet],
                 [(D, F32), (D, BF16)], [(1, D)] * 4)


def _ln1_bwd(dh2, du2, x0, y, g1, gam, bet, sc2, s):
    def fn(pos, dh2, du2, x0, y, g1, gam, bet, sc2, a_sc, a_sh, a_gam, a_bet, a_g1):
        xh, rstd = _ln_stats(ALPHA * x0 + (1.0 + g1) * y)
        x1 = xh * gam + bet
        dx1 = ALPHA * du2 + dh2 * (1.0 + sc2)
        du1 = _ln_bwd(dx1, xh, rstd, gam)
        return (du1, du1 * (1.0 + g1), a_sc + _colsum(dh2 * x1), a_sh + _colsum(dh2), a_gam + _colsum(dx1 * xh),
                a_bet + _colsum(dx1), a_g1 + _colsum(du1 * y))

    return _rowk("ln1_bwd", fn, s, 256, [(dh2, D, 0, 0), (du2, D, 0, 0), (x0, D, 0, 0), (y, D, 0, 0)],
                 [g1, gam, bet, sc2], [(D, F32), (D, BF16)], [(1, D)] * 5)


def _input_grad(dh1, du1, x0, sc1, s):
    def fn(pos, dh1, du1, x0, sc1, a_sc, a_sh):
        return ALPHA * du1 + dh1 * (1.0 + sc1), a_sc + _colsum(dh1 * x0), a_sh + _colsum(dh1)

    return _rowk("input_grad", fn, s, 256, [(dh1, D, 0, 0), (du1, D, 0, 0), (x0, D, 0, 0)], [sc1],
                 [(D, F32)], [(1, D)] * 2)


def _adamw(name, w, g, m, v, *, tr, slots):
    r, c = w.shape

    def body(w_ref, g_ref, m_ref, v_ref, g_out, d_out, m_out, v_out):
        if slots:
            grad = g_ref[0].astype(F32)
            for k in range(1, N_DEV):
                grad = grad + g_ref[k].astype(F32)
        else:
            grad = g_ref[...]
        m_new = ADAM_B1 * m_ref[...] + (1.0 - ADAM_B1) * grad
        v_new = ADAM_B2 * v_ref[...] + (1.0 - ADAM_B2) * (grad * grad)
        m_hat = m_new / (1.0 - ADAM_B1 ** ADAM_STEP)
        v_hat = v_new / (1.0 - ADAM_B2 ** ADAM_STEP)
        g_out[...] = grad
        d_out[...] = -ADAM_LR * (m_hat / (jnp.sqrt(v_hat) + ADAM_EPS) + ADAM_WD * w_ref[...])
        m_out[...] = m_new
        v_out[...] = v_new

    tile = pl.BlockSpec((tr, c), lambda i: (i, 0))
    g_spec = pl.BlockSpec((N_DEV, tr, c), lambda i: (0, i, 0)) if slots else tile
    return pl.pallas_call(
        body, name=name, grid=(r // tr,),
        in_specs=[tile, g_spec, tile, tile], out_specs=[tile] * 4,
        out_shape=[jax.ShapeDtypeStruct((r, c), F32)] * 4,
        compiler_params=_params(("parallel",)),
    )(w, g, m, v)


def _dot_f32(a, b, dims=NN):
    a0, a1, a2 = _split3(a)
    b0, b1, b2 = _split3(b)
    acc = _dot(a0, b0, dims)
    for x, y in ((a0, b1), (a1, b0), (a1, b1), (a0, b2), (a2, b0)):
        acc = acc + _dot(x, y, dims)
    return acc


def _ada_mod(c_all, w_shard, b_shard):
    def body(c_ref, w_ref, b_ref, o_ref):
        act = _silu(c_ref[...])
        act16 = jnp.concatenate([act, jnp.zeros_like(act)], axis=0)
        o_ref[...] = _dot_f32(act16, w_ref[...])[0:N_DEV] + b_ref[...]

    return pl.pallas_call(
        body, name="ada_mod", out_shape=jax.ShapeDtypeStruct((N_DEV, w_shard.shape[1]), F32),
        compiler_params=_params(None),
    )(c_all, w_shard, b_shard)


def _ada_grad(c_all, dmod_cols, dmod_all):
    def body(c_ref, dc_ref, da_ref, gw_ref, gb_ref):
        act = _silu(c_ref[...])
        act16 = jnp.concatenate([act, jnp.zeros_like(act)], axis=0)
        dm = dc_ref[...]
        dm16 = jnp.concatenate([dm, jnp.zeros_like(dm)], axis=0)
        gw_ref[...] = _dot_f32(act16, dm16, TN)
        gb_ref[...] = _colsum(da_ref[...])

    return pl.pallas_call(
        body, name="ada_grad",
        out_shape=[jax.ShapeDtypeStruct((D, dmod_cols.shape[1]), F32), jax.ShapeDtypeStruct((1, 6 * D), F32)],
        compiler_params=_params(None),
    )(c_all, dmod_cols, dmod_all)


def _sum_slots(name, g):
    def body(g_ref, o_ref):
        acc = g_ref[0]
        for k in range(1, N_DEV):
            acc = acc + g_ref[k]
        o_ref[...] = acc

    return pl.pallas_call(body, name=name, out_shape=jax.ShapeDtypeStruct(g.shape[1:], F32),
                          compiler_params=_params(None))(g)


def _exchange(name, xs, scatter):
    n = len(xs)
    n_peer = N_DEV - 1

    def body(*refs):
        x_refs, o_refs = refs[:n], refs[n:2 * n]
        send_sems, recv_sems, local_sems = refs[2 * n:]
        mx, my, mc = lax.axis_index("x"), lax.axis_index("y"), lax.axis_index("c")
        me = 4 * mx + 2 * my + mc

        def src(a, slot):
            return x_refs[a].at[slot] if scatter else x_refs[a]

        own = [pltpu.make_async_copy(src(a, me), o_refs[a].at[me], local_sems.at[a]) for a in range(n)]
        for cp in own:
            cp.start()
        sends = []
        for d in range(1, N_DEV):
            px = 1 - mx if d & 4 else mx
            py = 1 - my if d & 2 else my
            pc = 1 - mc if d & 1 else mc
            peer = 4 * px + 2 * py + pc
            for a in range(n):
                def copy(src_slot, dst_slot, a=a, d=d, to=(px, py, pc)):
                    return pltpu.make_async_remote_copy(
                        src_ref=src(a, src_slot), dst_ref=o_refs[a].at[dst_slot],
                        send_sem=send_sems.at[a * n_peer + d - 1], recv_sem=recv_sems.at[a * n_peer + d - 1],
                        device_id=to, device_id_type=pl.DeviceIdType.MESH)

                out = copy(peer, me)
                out.start()
                sends.append((out, copy(me, peer)))
        for _, arrival in sends:
            arrival.wait_recv()
        for out, _ in sends:
            out.wait_send()
        for cp in own:
            cp.wait()

    shapes = [tuple(x.shape[1:] if scatter else x.shape) for x in xs]
    return pl.pallas_call(
        body, name=name,
        in_specs=[pl.BlockSpec(memory_space=pl.ANY)] * n, out_specs=[pl.BlockSpec(memory_space=pl.ANY)] * n,
        out_shape=[jax.ShapeDtypeStruct((N_DEV,) + sh, x.dtype) for sh, x in zip(shapes, xs)],
        scratch_shapes=[pltpu.SemaphoreType.DMA((n * n_peer,)), pltpu.SemaphoreType.DMA((n * n_peer,)),
                        pltpu.SemaphoreType.DMA((n,))],
        compiler_params=pltpu.CompilerParams(has_side_effects=True),
    )(*xs)


def _relu2(a):
    r = jnp.maximum(a, 0.0)
    return r * r


def _relu2_grad(acc, a):
    return acc * (2.0 * jnp.maximum(a, 0.0))


def _local_step(x0, tgt, mod, wcat, wout, w1s, w2, conv_w, conv_b, dt_bias, a_log, d_skip, ssm_norm_w, f_bias,
                attn_norm_w, ln1_g, ln1_b, ln2_g, ln2_b):
    ff_w = DFF // N_DEV
    s = x0.shape[0]
    tm = min(512, s)
    ts = min(1024, s)
    sh1, sc1, g1, sh2, sc2, g2 = [mod[:, i * D:(i + 1) * D] for i in range(6)]
    zero = jnp.zeros((1, 128 - 2 * NH), F32)
    bias128 = jnp.concatenate([dt_bias, f_bias, zero], axis=1)
    alog128 = jnp.concatenate([a_log, jnp.zeros((1, 128 - NH), F32)], axis=1)
    dskip_x = jnp.repeat(d_skip, HD, axis=1)
    w_xs, w_bc, b_xs, b_bc = conv_w[:, :D], conv_w[:, D:], conv_b[:, :D], conv_b[:, D:]

    p = _mm_nn("in_proj", x0, wcat, tm=tm, tn=640, tk=D, out_dtype=F32, pro=_modulate, aux=(sc1, sh1))
    xs_a, bc_a = _conv_fwd(p, w_xs, b_xs, w_bc, b_bc, s)
    y_ssd, states = _ssd_fwd(xs_a, bc_a, p, bias128, alog128, dskip_x, s)
    cum_t = _cum_fwd(p, bias128, s)
    att, lse = _attn_fwd(p, cum_t, s)
    ymix = _mix_norm(y_ssd, p, att, ssm_norm_w, attn_norm_w, s)
    y = _mm_nn("out_proj", ymix, wout, tm=tm, tn=512, tk=2 * D, out_dtype=F32)
    x1, h2 = _ln1(x0, y, g1, ln1_g, ln1_b, sc2, sh2, s)
    a1 = _mm_nn("ff_in", h2, w1s, tm=tm, tn=ff_w, tk=D, out_dtype=F32)
    ff = _mm_nn("ff_out", a1, w2, tm=tm, tn=512, tk=1024, out_dtype=F32, pro=_relu2)
    du2, dff, sq_err, d_ln2_g, d_ln2_b, d_g2 = _ln2_loss(x1, ff, tgt, g2, ln2_g, ln2_b, s)

    da1 = _mm_nt("d_ff_hidden", [(dff, D, 0)], [(w2, D, 0)], n=DFF, tm=tm, tn=512, out_dtype=BF16, epi=_relu2_grad,
                 epi_aux=(a1,))
    d_w2 = _mm_tn("d_w_ff_out", a1, dff, tm=1024, tn=512, ts=ts, pro=_relu2)
    d_w1s = _mm_tn("d_w_ff_in", h2, da1, tm=1024, tn=ff_w, ts=ts, col_shards=True)
    dh2 = _mm_nt("d_ff_input", [(da1, ff_w, k) for k in range(N_DEV)], [(w1s, ff_w, k) for k in range(N_DEV)], n=D,
                 tm=min(256, s), tn=512, out_dtype=F32)
    du1, dy, d_sc2, d_sh2, d_ln1_g, d_ln1_b, d_g1 = _ln1_bwd(dh2, du2, x0, y, g1, ln1_g, ln1_b, sc2, s)

    dmix = _mm_nt("d_mix", [(dy, D, 0)], [(wout, D, 0)], n=2 * D, tm=tm, tn=512, out_dtype=F32)
    d_wout = _mm_tn("d_w_out", ymix, dy, tm=1024, tn=512, ts=ts)
    dy_ssd, dz, datt, d_ssm_w, d_attn_w = _mix_norm_bwd(dmix, y_ssd, p, att, ssm_norm_w, attn_norm_w, s)
    dq, dk, dv, dcs, drs = _attn_bwd(p, cum_t, att, lse, datt, s)
    dxs_a, dbc_a, ddt_raw, d_alog, d_dskip = _ssd_bwd(dy_ssd, xs_a, bc_a, p, states, bias128, alog128, dskip_x, s)
    dcum = jnp.pad(drs[:, ::HD] - dcs[:, :2, :].reshape(NH, s).T, ((0, 0), (NH, 128 - 2 * NH)))
    ddtf, _, d_bias = _cum_bwd(dcum, ddt_raw, p, bias128, s)
    dxs, dbc, d_wc_xs, d_bc_xs, d_wc_bc, d_bc_bc = _conv_bwd(dxs_a, dbc_a, p, w_xs, b_xs, w_bc, b_bc, s)

    segs = [(dz, OFF_Z, D), (dxs, OFF_XS, D), (dq, OFF_Q, D), (dk, OFF_K, D), (dv, OFF_V, D), (dbc, OFF_BC, 512),
            (ddtf, OFF_DTF, 128)]
    dh1 = _mm_nt("d_h1", [(a, w, 0) for a, _, w in segs], [(wcat, w, off // w) for _, off, w in segs], n=D,
                 tm=min(256, s), tn=512, out_dtype=F32)
    d_z, d_xs, d_q, d_k, d_v, d_bcw, d_dtf = [
        _mm_tn("d_w_in_%d" % i, x0, a, tm=1024, tn=min(w, 512), ts=ts, pro=_modulate, aux=(sc1, sh1))
        for i, (a, _, w) in enumerate(segs)]
    d_w_in = dict(z=d_z, xs=d_xs, bc=d_bcw, dt=d_dtf[:, :NH], q=d_q, k=d_k, v=d_v, f=d_dtf[:, NH:2 * NH])
    grad_x, d_sc1, d_sh1 = _input_grad(dh1, du1, x0, sc1, s)

    return dict(
        loss=(0.5 / D) * jnp.sum(sq_err), grad_x=grad_x, d_w_in=d_w_in, d_w_out=d_wout, d_w_ff_in=d_w1s, d_w_ff_out=d_w2,
        d_mod=jnp.concatenate([d_sh1, d_sc1, d_g1, d_sh2, d_sc2, d_g2], axis=1),
        d_conv_w=jnp.concatenate([d_wc_xs[:4], d_wc_bc[:4]], axis=1), d_conv_b=jnp.concatenate([d_bc_xs, d_bc_bc], axis=1),
        d_ssm_norm_w=d_ssm_w, d_attn_norm_w=d_attn_w, d_ln1_g=d_ln1_g, d_ln1_b=d_ln1_b, d_ln2_g=d_ln2_g, d_ln2_b=d_ln2_b,
        d_gate_bias=d_bias, d_a_log=d_alog, d_d_skip=d_dskip)


W_IN_SEGS = [('z', W_Z, D), ('xs', W_XS, D), ('bc', W_BC, 512), ('dt', W_DT, NH), ('q', W_Q, D), ('k', W_K, D),
             ('v', W_V, D), ('f', W_F, NH)]
SHARD_W = IN_COLS // N_DEV


def _pack_w_in(shards):
    def cols(lo, hi):
        pieces = []
        while lo < hi:
            dev = lo // SHARD_W
            end = min(hi, (dev + 1) * SHARD_W)
            pieces.append(shards[dev][:, lo - dev * SHARD_W:end - dev * SHARD_W])
            lo = end
        return pieces

    seg = {n: cols(off, off + w) for n, off, w in W_IN_SEGS}
    pieces = seg['z'] + seg['xs'] + seg['q'] + seg['k'] + seg['v'] + seg['bc'] + seg['dt'] + seg['f']
    return jnp.concatenate(pieces + [jnp.zeros((D, 128 - 2 * NH), shards.dtype)], axis=1)


def _shard_w_in_grad(d_w_in):
    blocks = []
    for dev in range(N_DEV):
        lo, hi = dev * SHARD_W, (dev + 1) * SHARD_W
        pieces = [d_w_in[n][:, max(lo, off) - off:min(hi, off + w) - off] for n, off, w in W_IN_SEGS
                  if max(lo, off) < min(hi, off + w)]
        blocks.append(jnp.concatenate(pieces, axis=1))
    return jnp.stack(blocks, axis=0)


WEIGHTS = ['w_ada', 'b_ada', 'w_in', 'conv_w', 'conv_b', 'dt_bias', 'a_log', 'd_skip', 'ssm_norm_w', 'f_bias',
           'attn_norm_w', 'w_out', 'ln1_g', 'ln1_b', 'w_ff_in', 'w_ff_out', 'ln2_g', 'ln2_b']
BIG = ['w_in', 'w_out', 'w_ff_in', 'w_ff_out']
SMALL = ['b_ada', 'conv_b', 'ssm_norm_w', 'attn_norm_w', 'ln1_g', 'ln1_b', 'ln2_g', 'ln2_b', 'dt_bias', 'a_log', 'd_skip',
         'f_bias', 'conv_w']


def _pad_lanes(v, n=128):
    return jnp.pad(v, ((0, 0), (0, n - v.shape[1])))


def _small_block(vals):
    rows = [_pad_lanes(vals[n].reshape(1, -1), -(-vals[n].size // 128) * 128).reshape(-1, 128) for n in SMALL]
    block = jnp.concatenate(rows, axis=0)
    return jnp.pad(block, ((0, 120 - block.shape[0]), (0, 0)))


def _small_unblock(block, like):
    out, r = {}, 0
    for n in SMALL:
        size = like[n].size
        nr = -(-size // 128)
        out[n] = block[r:r + nr].reshape(-1)[:size].reshape(like[n].shape)
        r += nr
    return out


def kernel(x, c, w_ada, b_ada, w_in, conv_w, conv_b, dt_bias, a_log, d_skip, ssm_norm_w, f_bias, attn_norm_w, w_out, ln1_g, ln1_b, w_ff_in, w_ff_out, ln2_g, ln2_b, loss_target, m_w_ada, m_b_ada, m_w_in, m_conv_w, m_conv_b, m_dt_bias, m_a_log, m_d_skip, m_ssm_norm_w, m_f_bias, m_attn_norm_w, m_w_out, m_ln1_g, m_ln1_b, m_w_ff_in, m_w_ff_out, m_ln2_g, m_ln2_b, v_w_ada, v_b_ada, v_w_in, v_conv_w, v_conv_b, v_dt_bias, v_a_log, v_d_skip, v_ssm_norm_w, v_f_bias, v_attn_norm_w, v_w_out, v_ln1_g, v_ln1_b, v_w_ff_in, v_w_ff_out, v_ln2_g, v_ln2_b):
    args = dict(locals())
    w = {n: args[n] for n in WEIGHTS}
    m = {n: args['m_' + n] for n in WEIGHTS}
    v = {n: args['v_' + n] for n in WEIGHTS}
    me = 4 * lax.axis_index("x") + 2 * lax.axis_index("y") + lax.axis_index("c")
    ada_cols = 6 * D // N_DEV
    conv_cols = conv_w.shape[2]

    c_all, conv_all = _exchange("gather_cond", [c, conv_w[0]], False)
    c_all = c_all.reshape(N_DEV, D)
    conv_w_full = conv_all.transpose(1, 0, 2).reshape(4, N_DEV * conv_cols)
    b_shard = lax.dynamic_slice(b_ada, (0, me * ada_cols), (1, ada_cols))
    mod_all, = _exchange("gather_mod", [_ada_mod(c_all, w_ada[0], b_shard)], False)
    mod = lax.dynamic_index_in_dim(mod_all, me, axis=1, keepdims=False).reshape(1, 6 * D)

    win_s, wout_s, w1s, w2_s = _exchange("gather_weights", [w[n][0].astype(BF16) for n in BIG], False)
    out = _local_step(x[0], loss_target[0], mod, _pack_w_in(win_s), wout_s.reshape(2 * D, D), w1s, w2_s.reshape(DFF, D),
                      conv_w_full, conv_b, dt_bias, a_log, d_skip, ssm_norm_w, f_bias, attn_norm_w, ln1_g, ln1_b, ln2_g, ln2_b)

    g_blocks = [_shard_w_in_grad(out['d_w_in']), out['d_w_out'].reshape(N_DEV, -1, D), out['d_w_ff_in'],
                out['d_w_ff_out'].reshape(N_DEV, -1, D)]
    g_parts = _exchange("scatter_grads", g_blocks, True)
    big = {n: _adamw("adamw_" + n, w[n][0], g, m[n][0], v[n][0], tr=256, slots=True) for n, g in zip(BIG, g_parts)}

    small = jnp.concatenate(
        [out['d_mod'], out['d_conv_w'].reshape(1, -1), out['d_conv_b'], out['d_ssm_norm_w'], out['d_attn_norm_w'],
         out['d_ln1_g'], out['d_ln1_b'], out['d_ln2_g'], out['d_ln2_b'], out['d_gate_bias'], out['d_a_log'],
         out['d_d_skip'], jnp.zeros((1, 128), F32)], axis=1).reshape(-1, 128)
    small_all, = _exchange("gather_small", [small], False)
    ssum = _sum_slots("sum_small", small_all)
    dmod_all = small_all[:, :6 * D // 128].reshape(N_DEV, 6 * D)
    g_w_ada, g_b_ada = _ada_grad(c_all, lax.dynamic_slice(dmod_all, (0, me * ada_cols), (N_DEV, ada_cols)), dmod_all)
    rows = lambda a, b: ssum[a:b].reshape(1, -1)
    g_conv_w = lax.dynamic_slice(ssum[48:96].reshape(4, N_DEV * conv_cols), (0, me * conv_cols), (4, conv_cols))
    g_small = dict(b_ada=g_b_ada, conv_w=g_conv_w[None], conv_b=rows(96, 108), ssm_norm_w=rows(108, 116),
                   attn_norm_w=rows(116, 124), ln1_g=rows(124, 132), ln1_b=rows(132, 140), ln2_g=rows(140, 148),
                   ln2_b=rows(148, 156), dt_bias=ssum[156:157, :NH], f_bias=ssum[156:157, NH:2 * NH],
                   a_log=ssum[157:158, :NH], d_skip=ssum[158:159, :NH])
    sm = _adamw("adamw_small", _small_block(w), _small_block(g_small), _small_block(m), _small_block(v), tr=120, slots=False)
    ada = _adamw("adamw_ada", w_ada[0], g_w_ada, m_w_ada[0], v_w_ada[0], tr=256, slots=False)

    results = []
    for k in range(4):
        vals = _small_unblock(sm[k], w)
        vals['w_ada'] = ada[k][None]
        for n in BIG:
            vals[n] = big[n][k][None]
        results.append(vals)
    loss = lax.psum(out['loss'], ("x", "y", "c"))
    return (loss, out['grad_x'][None], *[res[n] for res in results for n in WEIGHTS])
```

```python
import functools

import jax
import jax.numpy as jnp
from jax import lax
from jax.experimental import pallas as pl
from jax.experimental.pallas import tpu as pltpu

F32, BF16 = jnp.float32, jnp.bfloat16

N_DEV = 8
D = 1024
NH, HD = 16, 64
NSTATE = 128
CHUNK = 128
HG = 8
DFF = 4096
ALPHA = 2.0 ** 0.25
EPS = 1e-5
ATT_SCALE = HD ** -0.5

OFF_Z, OFF_XS, OFF_Q, OFF_K, OFF_V, OFF_BC, OFF_DTF = 0, 1024, 2048, 3072, 4096, 5120, 5632
PCOLS = 5760
W_Z, W_XS, W_BC, W_DT, W_Q, W_K, W_V, W_F = 0, 1024, 2048, 2560, 2576, 3600, 4624, 5648
IN_COLS = 5664

ADAM_LR, ADAM_B1, ADAM_B2, ADAM_EPS, ADAM_WD, ADAM_STEP = 0.001, 0.9, 0.999, 1e-08, 0.01, 10

VMEM_LIMIT = 56 << 20

NN = (((1,), (0,)), ((), ()))
NT = (((1,), (1,)), ((), ()))
TN = (((0,), (0,)), ((), ()))


def _dot(a, b, dims=NN):
    return lax.dot_general(a, b, dims, preferred_element_type=F32)


def _bdot(a, b, dims=NN):
    return _dot(a.astype(BF16), b.astype(BF16), dims)


def _split3(v):
    parts, rest = [], v
    for _ in range(3):
        p = rest.astype(BF16)
        parts.append(p)
        rest = rest - p.astype(F32)
    return parts


def _sel_left(m01, v):
    return sum(_dot(m01, p) for p in _split3(v))


def _sel_right(v, m01, dims=NN):
    return sum(_dot(p, m01, dims) for p in _split3(v))


def _iota(shape, dim):
    return lax.broadcasted_iota(jnp.int32, shape, dim)


def _tri_lower(n):
    return (_iota((n, n), 1) <= _iota((n, n), 0)).astype(BF16)


def _tri_upper(n):
    return (_iota((n, n), 1) >= _iota((n, n), 0)).astype(BF16)


def _head_expand():
    return (lax.shift_right_logical(_iota((128, D), 1), 6) == _iota((128, D), 0)).astype(BF16)


def _head_reduce():
    return (lax.shift_right_logical(_iota((D, 128), 0), 6) == _iota((D, 128), 1)).astype(BF16)


def _sigmoid(x):
    return 1.0 / (1.0 + jnp.exp(-x))


def _silu(x):
    return x * _sigmoid(x)


def _dsilu(x):
    s = _sigmoid(x)
    return s * (1.0 + x * (1.0 - s))


def _softplus(x):
    return jnp.maximum(x, 0.0) + jnp.log(1.0 + jnp.exp(-jnp.abs(x)))


def _log_sigmoid(x):
    return jnp.minimum(x, 0.0) - jnp.log(1.0 + jnp.exp(-jnp.abs(x)))


def _params(sem):
    return pltpu.CompilerParams(dimension_semantics=sem, vmem_limit_bytes=VMEM_LIMIT)


def _mm_nn(name, a, b, *, tm, tn, tk, out_dtype, pro=None, aux=()):
    m, k_all = a.shape
    b_sharded = b.ndim == 3
    n = b.shape[0] * b.shape[2] if b_sharded else b.shape[1]
    assert not b_sharded or tn == b.shape[2]
    nk = k_all // tk
    n_aux = len(aux)
    b_spec = (pl.BlockSpec((None, tk, tn), lambda i, j, k: (j, k, 0)) if b_sharded
              else pl.BlockSpec((tk, tn), lambda i, j, k: (k, j)))

    def body(a_ref, b_ref, *rest):
        aux_refs, o_ref = rest[:n_aux], rest[n_aux]
        at = a_ref[...]
        if pro is not None:
            at = pro(at, *[r[...] for r in aux_refs])
        part = _bdot(at, b_ref[...])
        if nk == 1:
            o_ref[...] = part.astype(out_dtype)
            return
        acc_ref = rest[n_aux + 1]
        kk = pl.program_id(2)

        @pl.when(kk == 0)
        def _():
            acc_ref[...] = part

        @pl.when(kk > 0)
        def _():
            acc_ref[...] += part

        @pl.when(kk == nk - 1)
        def _():
            o_ref[...] = acc_ref[...].astype(out_dtype)

    return pl.pallas_call(
        body, name=name,
        grid=(m // tm, n // tn, nk),
        in_specs=[pl.BlockSpec((tm, tk), lambda i, j, k: (i, k)), b_spec]
        + [pl.BlockSpec((1, tk), lambda i, j, k: (0, k)) for _ in aux],
        out_specs=pl.BlockSpec((tm, tn), lambda i, j, k: (i, j)),
        out_shape=jax.ShapeDtypeStruct((m, n), out_dtype),
        scratch_shapes=[] if nk == 1 else [pltpu.VMEM((tm, tn), F32)],
        compiler_params=_params(("parallel", "parallel", "arbitrary")),
    )(a, b, *aux)


def _mm_nt(name, a_list, b_list, *, n, tm, tn, out_dtype, epi=None, epi_aux=()):
    m = a_list[0][0].shape[0]
    n_op = len(a_list)
    n_epi = len(epi_aux)

    def body(*refs):
        a_refs, b_refs = refs[:n_op], refs[n_op:2 * n_op]
        e_refs, o_ref = refs[2 * n_op:2 * n_op + n_epi], refs[2 * n_op + n_epi]
        acc = None
        for a_ref, b_ref in zip(a_refs, b_refs):
            part = _bdot(a_ref[...], b_ref[...], NT)
            acc = part if acc is None else acc + part
        if epi is not None:
            acc = epi(acc, *[r[...] for r in e_refs])
        o_ref[...] = acc.astype(out_dtype)

    in_specs = [pl.BlockSpec((tm, w), functools.partial(lambda i, j, cb: (i, cb), cb=cb)) for (_, w, cb) in a_list]
    for (b, w, cb) in b_list:
        if b.ndim == 3:
            in_specs.append(pl.BlockSpec((None, tn, w), functools.partial(lambda i, j, cb: (cb, j, 0), cb=cb)))
        else:
            in_specs.append(pl.BlockSpec((tn, w), functools.partial(lambda i, j, cb: (j, cb), cb=cb)))
    in_specs += [pl.BlockSpec((tm, tn), lambda i, j: (i, j)) for _ in epi_aux]
    return pl.pallas_call(
        body, name=name,
        grid=(m // tm, n // tn),
        in_specs=in_specs,
        out_specs=pl.BlockSpec((tm, tn), lambda i, j: (i, j)),
        out_shape=jax.ShapeDtypeStruct((m, n), out_dtype),
        compiler_params=_params(("parallel", "parallel")),
    )(*[a for (a, _, _) in a_list], *[b for (b, _, _) in b_list], *epi_aux)


def _mm_tn(name, a, b, *, tm, tn, ts, pro=None, aux=(), col_shards=False):
    s_all, ka = a.shape
    nb = b.shape[1]
    n_aux = len(aux)
    ns = s_all // ts
    assert not col_shards or tn == nb // N_DEV

    def body(a_ref, b_ref, *rest):
        aux_refs, o_ref, acc_ref = rest[:n_aux], rest[n_aux], rest[n_aux + 1]
        at = a_ref[...]
        if pro is not None:
            at = pro(at, *[r[...] for r in aux_refs])
        part = _bdot(at, b_ref[...], TN)
        ss = pl.program_id(2)

        @pl.when(ss == 0)
        def _():
            acc_ref[...] = part

        @pl.when(ss > 0)
        def _():
            acc_ref[...] += part

        @pl.when(ss == ns - 1)
        def _():
            o_ref[...] = acc_ref[...].astype(BF16)

    if col_shards:
        out_spec = pl.BlockSpec((None, tm, tn), lambda i, j, s: (j, i, 0))
        out_shape = jax.ShapeDtypeStruct((N_DEV, ka, tn), BF16)
    else:
        out_spec = pl.BlockSpec((tm, tn), lambda i, j, s: (i, j))
        out_shape = jax.ShapeDtypeStruct((ka, nb), BF16)
    return pl.pallas_call(
        body, name=name,
        grid=(ka // tm, nb // tn, ns),
        in_specs=[pl.BlockSpec((ts, tm), lambda i, j, s: (s, i)),
                  pl.BlockSpec((ts, tn), lambda i, j, s: (s, j))]
        + [pl.BlockSpec((1, tm), lambda i, j, s: (0, i)) for _ in aux],
        out_specs=out_spec, out_shape=out_shape,
        scratch_shapes=[pltpu.VMEM((tm, tn), F32)],
        compiler_params=_params(("parallel", "parallel", "arbitrary")),
    )(a, b, *aux)


def _rowk(name, fn, n_rows, tr, rows, fulls, outs, accs, reverse=False):
    n = n_rows // tr
    n_row, n_full, n_out, n_acc = len(rows), len(fulls), len(outs), len(accs)

    def pos(i):
        return (n - 1 - i) if reverse else i

    def body(*refs):
        row_refs = refs[:n_row]
        full_refs = refs[n_row:n_row + n_full]
        out_refs = refs[n_row + n_full:n_row + n_full + n_out]
        acc_refs = refs[n_row + n_full + n_out:]
        i = pl.program_id(0)

        @pl.when(i == 0)
        def _():
            for r in acc_refs:
                r[...] = jnp.zeros(r.shape, r.dtype)

        res = fn(pos(i), *[r[...] for r in row_refs], *[r[...] for r in full_refs], *[r[...] for r in acc_refs])
        for r, v in zip(out_refs + acc_refs, res):
            r[...] = v.astype(r.dtype)

    def row_map(i, cb, shift):
        return (jnp.clip(pos(i) + shift, 0, n - 1), cb)

    in_specs = [pl.BlockSpec((tr, w), functools.partial(row_map, cb=cb, shift=sh)) for (_, w, cb, sh) in rows]
    in_specs += [pl.BlockSpec(f.shape, functools.partial(lambda i, nd: (0,) * nd, nd=f.ndim)) for f in fulls]
    out_specs = [pl.BlockSpec((tr, w), lambda i: (pos(i), 0)) for (w, _) in outs]
    out_specs += [pl.BlockSpec((r, w), lambda i: (0, 0)) for (r, w) in accs]
    out_shape = [jax.ShapeDtypeStruct((n_rows, w), dt) for (w, dt) in outs]
    out_shape += [jax.ShapeDtypeStruct((r, w), F32) for (r, w) in accs]
    return pl.pallas_call(
        body, name=name, grid=(n,), in_specs=in_specs, out_specs=out_specs, out_shape=out_shape,
        compiler_params=_params(("arbitrary",)),
    )(*[a for (a, _, _, _) in rows], *fulls)


def _colsum(x):
    return jnp.sum(x, axis=0, keepdims=True)


def _mean(x):
    return jnp.mean(x, axis=-1, keepdims=True)


def _modulate(x, sc, sh):
    return x * (1.0 + sc) + sh


def _shift_down(cur, prev, j):
    row = _iota(cur.shape, 0)
    return jnp.where(row < j, pltpu.roll(prev, j, 0), pltpu.roll(cur, j, 0))


def _shift_up(cur, nxt, j):
    tr = cur.shape[0]
    row = _iota(cur.shape, 0)
    return jnp.where(row < tr - j, pltpu.roll(cur, tr - j, 0), pltpu.roll(nxt, tr - j, 0))


def _conv(cur, prev, w, b):
    out = cur * w[3:4] + b
    for j in (1, 2, 3):
        out = out + _shift_down(cur, prev, j) * w[3 - j:4 - j]
    return out


def _conv_fwd(p, w_xs, b_xs, w_bc, b_bc, s):
    def fn(pos, xs, xs_prev, bc, bc_prev, w_xs, b_xs, w_bc, b_bc):
        first = pos == 0
        xs_prev = jnp.where(first, 0.0, xs_prev)
        bc_prev = jnp.where(first, 0.0, bc_prev)
        return _silu(_conv(xs, xs_prev, w_xs, b_xs)), _silu(_conv(bc, bc_prev, w_bc, b_bc))

    return _rowk("conv_fwd", fn, s, 256,
                 [(p, D, OFF_XS // D, 0), (p, D, OFF_XS // D, -1), (p, 512, OFF_BC // 512, 0), (p, 512, OFF_BC // 512, -1)],
                 [w_xs, b_xs, w_bc, b_bc], [(D, F32), (512, F32)], [])


def _conv_bwd(dxs_a, dbc_a, p, w_xs, b_xs, w_bc, b_bc, s):
    tr = 256
    n = s // tr

    def fn(pos, da1, da1n, x1, x1p, x1n, da2, da2n, x2, x2p, x2n, w1, b1, w2, b2, aw1, ab1, aw2, ab2):
        dx1, dw1, db1 = _conv_bwd_fn(pos, n, da1, da1n, x1, x1p, x1n, w1, b1)
        dx2, dw2, db2 = _conv_bwd_fn(pos, n, da2, da2n, x2, x2p, x2n, w2, b2)
        return dx1, dx2, aw1 + dw1, ab1 + db1, aw2 + dw2, ab2 + db2

    cx, cb = OFF_XS // D, OFF_BC // 512
    return _rowk("conv_bwd", fn, s, tr,
                 [(dxs_a, D, 0, 0), (dxs_a, D, 0, 1), (p, D, cx, 0), (p, D, cx, -1), (p, D, cx, 1),
                  (dbc_a, 512, 0, 0), (dbc_a, 512, 0, 1), (p, 512, cb, 0), (p, 512, cb, -1), (p, 512, cb, 1)],
                 [w_xs, b_xs, w_bc, b_bc], [(D, BF16), (512, BF16)], [(8, D), (1, D), (8, 512), (1, 512)])


def _conv_bwd_fn(pos, n, da, da_next, x, x_prev, x_next, w, b):
    first, last = pos == 0, pos == n - 1
    x_prev = jnp.where(first, 0.0, x_prev)
    dc = da * _dsilu(_conv(x, x_prev, w, b))
    dc_next = jnp.where(last, 0.0, da_next * _dsilu(_conv(x_next, x, w, b)))
    dx = dc * w[3:4]
    dws = [None] * 4
    dws[3] = _colsum(dc * x)
    for j in (1, 2, 3):
        dx = dx + _shift_up(dc, dc_next, j) * w[3 - j:4 - j]
        dws[3 - j] = _colsum(dc * _shift_down(x, x_prev, j))
    row = _iota((8, x.shape[1]), 0)
    dw = jnp.zeros((8, x.shape[1]), F32)
    for k in range(4):
        dw = jnp.where(row == k, dws[k], dw)
    return dx, dw, _colsum(dc)


def _ssd_gates(dtf, bias, a_log):
    lane = _iota(dtf.shape, 1)
    head = lane < NH
    dt = jnp.where(head, _softplus(dtf + bias), 0.0)
    a_neg = jnp.where(_iota(a_log.shape, 1) < NH, -jnp.exp(a_log), 0.0)
    a = dt * a_neg
    cs = _sel_left(_tri_lower(CHUNK), a)
    return dt, a_neg, cs


def _decay_mask(cs_ref, cst_ref, h):
    diff = cs_ref[:, h:h + 1] - cst_ref[h:h + 1, :]
    low = _iota((CHUNK, CHUNK), 1) <= _iota((CHUNK, CHUNK), 0)
    return jnp.where(low, jnp.exp(jnp.minimum(diff, 0.0)), 0.0)


def _ssd_fwd(xs_a, bc_a, p, bias128, alog128, dskip_x, s):
    nc = s // CHUNK
    t = CHUNK

    def body(xs_ref, bc_ref, dtf_ref, bias_ref, alog_ref, dsk_ref, y_ref, st_ref,
             state, x_sc, xw_sc, cs_sc, cst_sc, yd_sc):
        c = pl.program_id(0)

        @pl.when(c == 0)
        def _():
            state[...] = jnp.zeros(state.shape, F32)

        dt, _, cs = _ssd_gates(dtf_ref[...], bias_ref[...], alog_ref[...])
        cs_sc[...] = cs
        cst_sc[...] = cs.T
        cs_last = cs[t - 1:t, :]
        expand = _head_expand()
        ex = _sel_right(jnp.concatenate([dt, jnp.exp(cs), jnp.exp(cs_last - cs)], axis=0), expand)
        dt_x, eo_x, we_x = ex[0:t], ex[t:2 * t], ex[2 * t:3 * t]
        g_x = _sel_right(jnp.broadcast_to(jnp.exp(cs_last), (8, 128)), expand)[0:1]
        xs = xs_ref[...]
        x = xs * dt_x
        x_sc[...] = x.astype(BF16)
        xw_sc[...] = (x * we_x).astype(BF16)
        prev = state[...]
        st_ref[0] = prev
        prev_b = prev.astype(BF16)
        for g in range(2):
            cols = slice(g * 512, (g + 1) * 512)
            b_g = bc_ref[:, g * 128:(g + 1) * 128].astype(BF16)
            c_g = bc_ref[:, 256 + g * 128:256 + (g + 1) * 128].astype(BF16)
            gmat = _dot(c_g, b_g, NT)
            y_off = _dot(c_g, prev_b[:, cols]) * eo_x[:, cols]
            s_loc = _dot(b_g, xw_sc[:, cols], TN)
            state[:, cols] = g_x[:, cols] * prev[:, cols] + s_loc
            for e in range(HG):
                h = g * HG + e
                m = gmat * _decay_mask(cs_sc, cst_sc, h)
                yd_sc[:, h * HD:(h + 1) * HD] = _dot(m.astype(BF16), x_sc[:, h * HD:(h + 1) * HD])
            y_ref[:, cols] = yd_sc[:, cols] + y_off + dsk_ref[:, cols] * xs[:, cols]

    return pl.pallas_call(
        body, name="ssd_fwd", grid=(nc,),
        in_specs=[pl.BlockSpec((t, D), lambda c: (c, 0)),
                  pl.BlockSpec((t, 512), lambda c: (c, 0)),
                  pl.BlockSpec((t, 128), lambda c: (c, OFF_DTF // 128)),
                  pl.BlockSpec((1, 128), lambda c: (0, 0)),
                  pl.BlockSpec((1, 128), lambda c: (0, 0)),
                  pl.BlockSpec((1, D), lambda c: (0, 0))],
        out_specs=[pl.BlockSpec((t, D), lambda c: (c, 0)),
                   pl.BlockSpec((1, NSTATE, D), lambda c: (c, 0, 0))],
        out_shape=[jax.ShapeDtypeStruct((s, D), F32), jax.ShapeDtypeStruct((nc, NSTATE, D), F32)],
        scratch_shapes=[pltpu.VMEM((NSTATE, D), F32), pltpu.VMEM((t, D), BF16), pltpu.VMEM((t, D), BF16),
                        pltpu.VMEM((t, 128), F32), pltpu.VMEM((128, t), F32), pltpu.VMEM((t, D), F32)],
        compiler_params=_params(("arbitrary",)),
    )(xs_a, bc_a, p, bias128, alog128, dskip_x)


def _ssd_bwd(dy, xs_a, bc_a, p, states, bias128, alog128, dskip_x, s):
    nc = s // CHUNK
    t = CHUNK

    def body(dy_ref, xs_ref, bc_ref, dtf_ref, st_ref, bias_ref, alog_ref, dsk_ref,
             dxs_ref, dbc_ref, ddt_ref, dalog_ref, dskip_ref,
             dstate, x_sc, dy_sc, dx_sc, deo_sc, dwe_sc, cs_sc, cst_sc, dcol_sc, drow_sc):
        i = pl.program_id(0)

        @pl.when(i == 0)
        def _():
            dstate[...] = jnp.zeros(dstate.shape, F32)
            dalog_ref[...] = jnp.zeros(dalog_ref.shape, F32)
            dskip_ref[...] = jnp.zeros(dskip_ref.shape, F32)

        dtf = dtf_ref[...]
        dt, a_neg, cs = _ssd_gates(dtf, bias_ref[...], alog_ref[...])
        cs_sc[...] = cs
        cst_sc[...] = cs.T
        cs_last = cs[t - 1:t, :]
        eo, we, g_end = jnp.exp(cs), jnp.exp(cs_last - cs), jnp.exp(cs_last)
        expand, reduce = _head_expand(), _head_reduce()
        ex = _sel_right(jnp.concatenate([dt, eo, we], axis=0), expand)
        dt_x, eo_x, we_x = ex[0:t], ex[t:2 * t], ex[2 * t:3 * t]
        g_x = _sel_right(jnp.broadcast_to(g_end, (8, 128)), expand)[0:1]
        xs = xs_ref[...]
        dyv = dy_ref[...]
        x = xs * dt_x
        x_sc[...] = x.astype(BF16)
        dy_sc[...] = dyv.astype(BF16)
        dyo_b = (dyv * eo_x).astype(BF16)
        xw_b = (x * we_x).astype(BF16)
        prev = st_ref[0]
        prev_b = prev.astype(BF16)
        dnext = dstate[...]
        dnext_b = dnext.astype(BF16)
        dcol_sc[...] = jnp.zeros(dcol_sc.shape, F32)
        drow_sc[...] = jnp.zeros(drow_sc.shape, F32)
        lane_row = _iota((1, 128), 1)
        sub_col = _iota((128, 1), 0)
        for g in range(2):
            cols = slice(g * 512, (g + 1) * 512)
            b_g = bc_ref[:, g * 128:(g + 1) * 128].astype(BF16)
            c_g = bc_ref[:, 256 + g * 128:256 + (g + 1) * 128].astype(BF16)
            gmat = _dot(c_g, b_g, NT)
            b_ds = _dot(b_g, dnext_b[:, cols])
            c_s = _dot(c_g, prev_b[:, cols])
            dx_sc[:, cols] = b_ds * we_x[:, cols]
            deo_sc[:, cols] = dyv[:, cols] * c_s
            dwe_sc[:, cols] = b_ds * x[:, cols]
            db = _dot(xw_b[:, cols], dnext_b[:, cols], NT)
            dc = _dot(dyo_b[:, cols], prev_b[:, cols], NT)
            dstate[:, cols] = g_x[:, cols] * dnext[:, cols] + _dot(c_g, dyo_b[:, cols], TN)
            dg = jnp.zeros((t, t), F32)
            for e in range(HG):
                h = g * HG + e
                hc = slice(h * HD, (h + 1) * HD)
                lmat = _decay_mask(cs_sc, cst_sc, h)
                m = gmat * lmat
                dx_sc[:, hc] += _dot(m.astype(BF16), dy_sc[:, hc], TN)
                dm = _dot(dy_sc[:, hc], x_sc[:, hc], NT)
                dg = dg + dm * lmat
                qm = dm * m
                dcol_sc[...] += jnp.sum(qm, axis=1, keepdims=True) * (lane_row == h).astype(F32)
                drow_sc[...] += (sub_col == h).astype(F32) * jnp.sum(qm, axis=0, keepdims=True)
            dg_b = dg.astype(BF16)
            dbc_ref[:, g * 128:(g + 1) * 128] = db + _dot(dg_b, c_g, TN)
            dbc_ref[:, 256 + g * 128:256 + (g + 1) * 128] = dc + _dot(dg_b, b_g)
        d_eo = _sel_right(deo_sc[...], reduce)
        d_we = _sel_right(dwe_sc[...], reduce)
        d_gend = _sel_right(jnp.broadcast_to(_colsum(dnext * prev), (8, D)), reduce)[0:1]
        d_cs = dcol_sc[...] - drow_sc[...].T + d_eo * eo - d_we * we
        extra = _colsum(d_we * we) + d_gend * g_end
        d_cs = d_cs + jnp.where(_iota((t, 128), 0) == t - 1, extra, 0.0)
        da = _sel_left(_tri_upper(t), d_cs)
        dx = dx_sc[...]
        ddt = _sel_right(dx * xs, reduce) + da * a_neg
        dxs_ref[...] = dx * dt_x + dsk_ref[...] * dyv
        ddt_ref[...] = jnp.where(_iota((t, 128), 1) < NH, ddt * _sigmoid(dtf + bias_ref[...]), 0.0)
        dalog_ref[...] += _colsum(da * dt) * a_neg
        dskip_ref[...] += _sel_right(jnp.broadcast_to(_colsum(dyv * xs), (8, D)), reduce)[0:1]

    rev = lambda i: nc - 1 - i
    return pl.pallas_call(
        body, name="ssd_bwd", grid=(nc,),
        in_specs=[pl.BlockSpec((t, D), lambda i: (rev(i), 0)),
                  pl.BlockSpec((t, D), lambda i: (rev(i), 0)),
                  pl.BlockSpec((t, 512), lambda i: (rev(i), 0)),
                  pl.BlockSpec((t, 128), lambda i: (rev(i), OFF_DTF // 128)),
                  pl.BlockSpec((1, NSTATE, D), lambda i: (rev(i), 0, 0)),
                  pl.BlockSpec((1, 128), lambda i: (0, 0)),
                  pl.BlockSpec((1, 128), lambda i: (0, 0)),
                  pl.BlockSpec((1, D), lambda i: (0, 0))],
        out_specs=[pl.BlockSpec((t, D), lambda i: (rev(i), 0)),
                   pl.BlockSpec((t, 512), lambda i: (rev(i), 0)),
                   pl.BlockSpec((t, 128), lambda i: (rev(i), 0)),
                   pl.BlockSpec((1, 128), lambda i: (0, 0)),
                   pl.BlockSpec((1, 128), lambda i: (0, 0))],
        out_shape=[jax.ShapeDtypeStruct((s, D), F32), jax.ShapeDtypeStruct((s, 512), F32),
                   jax.ShapeDtypeStruct((s, 128), F32), jax.ShapeDtypeStruct((1, 128), F32),
                   jax.ShapeDtypeStruct((1, 128), F32)],
        scratch_shapes=[pltpu.VMEM((NSTATE, D), F32), pltpu.VMEM((t, D), BF16), pltpu.VMEM((t, D), BF16),
                        pltpu.VMEM((t, D), F32), pltpu.VMEM((t, D), F32), pltpu.VMEM((t, D), F32),
                        pltpu.VMEM((t, 128), F32), pltpu.VMEM((128, t), F32),
                        pltpu.VMEM((t, 128), F32), pltpu.VMEM((128, t), F32)],
        compiler_params=_params(("arbitrary",)),
    )(dy, xs_a, bc_a, p, states, bias128, alog128, dskip_x)


def _gate_lanes(shape):
    lane = _iota(shape, 1)
    return (lane >= NH) & (lane < 2 * NH)


def _cum_fwd(p, bias128, s):
    tr = min(512, s)

    def body(dtf_ref, bias_ref, o_ref, carry):
        @pl.when(pl.program_id(0) == 0)
        def _():
            carry[...] = jnp.zeros(carry.shape, F32)

        lf = jnp.where(_gate_lanes((tr, 128)), _log_sigmoid(dtf_ref[...] + bias_ref[...]), 0.0)
        cum = _sel_left(_tri_lower(tr), lf) + carry[...]
        carry[...] = cum[tr - 1:tr, :]
        o_ref[...] = cum.T[NH:2 * NH, :]

    return pl.pallas_call(
        body, name="cum_fwd", grid=(s // tr,),
        in_specs=[pl.BlockSpec((tr, 128), lambda i: (i, OFF_DTF // 128)), pl.BlockSpec((1, 128), lambda i: (0, 0))],
        out_specs=pl.BlockSpec((NH, tr), lambda i: (0, i)),
        out_shape=jax.ShapeDtypeStruct((NH, s), F32),
        scratch_shapes=[pltpu.VMEM((1, 128), F32)],
        compiler_params=_params(("arbitrary",)),
    )(p, bias128)


def _cum_bwd(dcum, ddt_raw, p, bias128, s):
    tr = min(512, s)

    def fn(pos, dcum, ddt, dtf, bias, carry, acc):
        suffix = _sel_left(_tri_upper(tr), dcum) + carry
        dfr = jnp.where(_gate_lanes((tr, 128)), suffix * _sigmoid(-(dtf + bias)), 0.0)
        out = ddt + dfr
        return out, suffix[0:1, :], acc + _colsum(out)

    return _rowk("cum_bwd", fn, s, tr, [(dcum, 128, 0, 0), (ddt_raw, 128, 0, 0), (p, 128, OFF_DTF // 128, 0)],
                 [bias128], [(128, BF16)], [(1, 128), (1, 128)], reverse=True)


def _attn_masked_logits(q_h, k_h, crow, qi, ki, tq, tk):
    row = qi * tq + _iota((tq, tk), 0)
    col = ki * tk + _iota((tq, tk), 1)
    return jnp.where(col <= row, _dot(q_h, k_h, NT) - crow, -1e30)


def _attn_fwd(p, cum_t, s):
    tq = tk = min(512, s)
    nq = s // tq

    def body(q_ref, k_ref, v_ref, c_ref, o_ref, lse_ref, m_sc, l_sc, acc_sc):
        j, qi, ki = pl.program_id(0), pl.program_id(1), pl.program_id(2)

        @pl.when(ki == 0)
        def _():
            m_sc[...] = jnp.full(m_sc.shape, -1e30, F32)
            l_sc[...] = jnp.zeros(l_sc.shape, F32)
            acc_sc[...] = jnp.zeros(acc_sc.shape, F32)

        @pl.when(ki <= qi)
        def _():
            q = (q_ref[...] * ATT_SCALE).astype(BF16)
            k = k_ref[...].astype(BF16)
            v = v_ref[...].astype(BF16)
            for hh in range(2):
                hc = slice(hh * HD, (hh + 1) * HD)
                crow = c_ref[pl.ds(2 * j + hh, 1), :]
                sc = _attn_masked_logits(q[:, hc], k[:, hc], crow, qi, ki, tq, tk)
                m_prev = m_sc[hh]
                m_new = jnp.maximum(m_prev, jnp.max(sc, axis=1, keepdims=True))
                pr = jnp.exp(sc - m_new[:, 0:1])
                alpha = jnp.exp(m_prev - m_new)
                l_sc[hh] = alpha * l_sc[hh] + jnp.sum(pr, axis=1, keepdims=True)
                m_sc[hh] = m_new
                acc_sc[:, hc] = alpha[:, :HD] * acc_sc[:, hc] + _dot(pr.astype(BF16), v[:, hc])

        @pl.when(ki == qi)
        def _():
            for hh in range(2):
                hc = slice(hh * HD, (hh + 1) * HD)
                l = l_sc[hh]
                o_ref[:, hc] = acc_sc[:, hc] / l[:, :HD]
                lse_ref[:, hc] = (m_sc[hh] + jnp.log(l))[:, :HD]

    def kv_map(j, qi, ki, off):
        return (jnp.minimum(ki, qi), off + j)

    return pl.pallas_call(
        body, name="attn_fwd", grid=(NH // 2, nq, nq),
        in_specs=[pl.BlockSpec((tq, 128), lambda j, qi, ki: (qi, OFF_Q // 128 + j)),
                  pl.BlockSpec((tk, 128), functools.partial(kv_map, off=OFF_K // 128)),
                  pl.BlockSpec((tk, 128), functools.partial(kv_map, off=OFF_V // 128)),
                  pl.BlockSpec((NH, tk), lambda j, qi, ki: (0, jnp.minimum(ki, qi)))],
        out_specs=[pl.BlockSpec((tq, 128), lambda j, qi, ki: (qi, j)),
                   pl.BlockSpec((tq, 128), lambda j, qi, ki: (qi, j))],
        out_shape=[jax.ShapeDtypeStruct((s, D), F32), jax.ShapeDtypeStruct((s, D), F32)],
        scratch_shapes=[pltpu.VMEM((2, tq, 128), F32), pltpu.VMEM((2, tq, 128), F32), pltpu.VMEM((tq, 128), F32)],
        compiler_params=_params(("parallel", "parallel", "arbitrary")),
    )(p, p, p, cum_t)


def _attn_bwd(p, cum_t, o, lse, do, s):
    tq = tk = min(512, s)
    nq = s // tq

    def body(q_ref, k_ref, v_ref, c_ref, o_ref, lse_ref, do_ref, dq_ref, dk_ref, dv_ref, dc_ref, dr_ref, dk_sc, dv_sc, dc_sc):
        j, ki, qi = pl.program_id(0), pl.program_id(1), pl.program_id(2)

        @pl.when(qi == ki)
        def _():
            dk_sc[...] = jnp.zeros(dk_sc.shape, F32)
            dv_sc[...] = jnp.zeros(dv_sc.shape, F32)
            dc_sc[...] = jnp.zeros(dc_sc.shape, F32)

        @pl.when(qi >= ki)
        def _():
            q = (q_ref[...] * ATT_SCALE).astype(BF16)
            k = k_ref[...].astype(BF16)
            v = v_ref[...].astype(BF16)
            dov, ov, lse_v = do_ref[...], o_ref[...], lse_ref[...]
            rows = pl.ds(pl.multiple_of(qi * tq, tq), tq)
            for hh in range(2):
                hc = slice(hh * HD, (hh + 1) * HD)
                crow = c_ref[pl.ds(2 * j + hh, 1), :]
                sc = _attn_masked_logits(q[:, hc], k[:, hc], crow, qi, ki, tq, tk)
                pr = jnp.exp(sc - lse_v[:, hh * HD:hh * HD + 1])
                do_h = dov[:, hc]
                delta = jnp.sum(do_h * ov[:, hc], axis=1, keepdims=True)
                do_b = do_h.astype(BF16)
                dv_sc[:, hc] += _dot(pr.astype(BF16), do_b, TN)
                ds = pr * (_dot(do_b, v[:, hc], NT) - delta)
                dc_sc[hh:hh + 1, :] += _colsum(ds)
                ds_b = ds.astype(BF16)
                dk_sc[:, hc] += _dot(ds_b, q[:, hc], TN)
                dq_h = _dot(ds_b, k[:, hc]) * ATT_SCALE
                drow = jnp.broadcast_to(jnp.sum(ds, axis=1, keepdims=True), (tq, HD))

                @pl.when(ki == 0)
                def _():
                    dq_ref[rows, hc] = dq_h
                    dr_ref[rows, hc] = drow

                @pl.when(ki > 0)
                def _():
                    dq_ref[rows, hc] += dq_h
                    dr_ref[rows, hc] += drow

        @pl.when(qi == nq - 1)
        def _():
            dk_ref[...] = dk_sc[...].astype(BF16)
            dv_ref[...] = dv_sc[...].astype(BF16)
            dc_ref[0] = dc_sc[...]

    def q_map(j, ki, qi, off):
        return (jnp.maximum(qi, ki), off + j)

    return pl.pallas_call(
        body, name="attn_bwd", grid=(NH // 2, nq, nq),
        in_specs=[pl.BlockSpec((tq, 128), functools.partial(q_map, off=OFF_Q // 128)),
                  pl.BlockSpec((tk, 128), lambda j, ki, qi: (ki, OFF_K // 128 + j)),
                  pl.BlockSpec((tk, 128), lambda j, ki, qi: (ki, OFF_V // 128 + j)),
                  pl.BlockSpec((NH, tk), lambda j, ki, qi: (0, ki)),
                  pl.BlockSpec((tq, 128), functools.partial(q_map, off=0)),
                  pl.BlockSpec((tq, 128), functools.partial(q_map, off=0)),
                  pl.BlockSpec((tq, 128), functools.partial(q_map, off=0))],
        out_specs=[pl.BlockSpec((s, 128), lambda j, ki, qi: (0, j)),
                   pl.BlockSpec((tk, 128), lambda j, ki, qi: (ki, j)),
                   pl.BlockSpec((tk, 128), lambda j, ki, qi: (ki, j)),
                   pl.BlockSpec((1, 8, tk), lambda j, ki, qi: (j, 0, ki)),
                   pl.BlockSpec((s, 128), lambda j, ki, qi: (0, j))],
        out_shape=[jax.ShapeDtypeStruct((s, D), F32), jax.ShapeDtypeStruct((s, D), BF16),
                   jax.ShapeDtypeStruct((s, D), BF16), jax.ShapeDtypeStruct((NH // 2, 8, s), F32),
                   jax.ShapeDtypeStruct((s, D), F32)],
        scratch_shapes=[pltpu.VMEM((tk, 128), F32), pltpu.VMEM((tk, 128), F32), pltpu.VMEM((8, tk), F32)],
        compiler_params=_params(("parallel", "arbitrary", "arbitrary")),
    )(p, p, p, cum_t, o, lse, do)


def _ln_stats(u):
    mu = _mean(u)
    d = u - mu
    rstd = lax.rsqrt(_mean(d * d) + EPS)
    return d * rstd, rstd


def _ln_bwd(dx, xh, rstd, gam):
    dxh = dx * gam
    return rstd * (dxh - _mean(dxh) - xh * _mean(dxh * xh))


def _rms_bwd(d, xn, r, w):
    t = d * w
    return r * (t - xn * _mean(t * xn)), _colsum(d * xn)


def _mix_norm(y, p, att, w_ssm, w_att, s):
    def fn(pos, y, z, att, w1, w2):
        g = y * _silu(z)
        n1 = g * lax.rsqrt(_mean(g * g) + EPS) * w1
        n2 = att * lax.rsqrt(_mean(att * att) + EPS) * w2
        return (jnp.concatenate([n1, n2], axis=1),)

    return _rowk("mix_norm", fn, s, 256, [(y, D, 0, 0), (p, D, OFF_Z // D, 0), (att, D, 0, 0)],
                 [w_ssm, w_att], [(2 * D, BF16)], [])[0]


def _mix_norm_bwd(dmix, y, p, att, w_ssm, w_att, s):
    def fn(pos, dmix, y, z, att, w1, w2, a1, a2):
        sz = _silu(z)
        g = y * sz
        r1 = lax.rsqrt(_mean(g * g) + EPS)
        dg, dw1 = _rms_bwd(dmix[:, :D], g * r1, r1, w1)
        r2 = lax.rsqrt(_mean(att * att) + EPS)
        datt, dw2 = _rms_bwd(dmix[:, D:], att * r2, r2, w2)
        return dg * sz, dg * y * _dsilu(z), datt, a1 + dw1, a2 + dw2

    return _rowk("mix_norm_bwd", fn, s, 256, [(dmix, 2 * D, 0, 0), (y, D, 0, 0), (p, D, OFF_Z // D, 0), (att, D, 0, 0)],
                 [w_ssm, w_att], [(D, F32), (D, BF16), (D, F32)], [(1, D), (1, D)])


def _ln1(x0, y, g1, gam, bet, sc2, sh2, s):
    def fn(pos, x0, y, g1, gam, bet, sc2, sh2):
        xh, _ = _ln_stats(ALPHA * x0 + (1.0 + g1) * y)
        x1 = xh * gam + bet
        return x1, _modulate(x1, sc2, sh2)

    return _rowk("ln1", fn, s, 256, [(x0, D, 0, 0), (y, D, 0, 0)], [g1, gam, bet, sc2, sh2], [(D, F32), (D, BF16)], [])


def _ln2_loss(x1, ff, tgt, g2, gam, bet, s):
    def fn(pos, x1, ff, tgt, g2, gam, bet, a_loss, a_dgam, a_dbet, a_dg2):
        xh, rstd = _ln_stats(ALPHA * x1 + (1.0 + g2) * ff)
        err = xh * gam + bet - tgt
        dx2 = err * (1.0 / D)
        du = _ln_bwd(dx2, xh, rstd, gam)
        return (du, du * (1.0 + g2), a_loss + _colsum(err * err), a_dgam + _colsum(dx2 * xh),
                a_dbet + _colsum(dx2), a_dg2 + _colsum(du * ff))

    return _rowk("ln2_loss", fn, s, 256, [(x1, D, 0, 0), (ff, D, 0, 0), (tgt, D, 0, 0)], [g2, gam, bet],
                 [(D, F32), (D, BF16)], [(1, D)] * 4)


def _ln1_bwd(dh2, du2, x0, y, g1, gam, bet, sc2, s):
    def fn(pos, dh2, du2, x0, y, g1, gam, bet, sc2, a_sc, a_sh, a_gam, a_bet, a_g1):
        xh, rstd = _ln_stats(ALPHA * x0 + (1.0 + g1) * y)
        x1 = xh * gam + bet
        dx1 = ALPHA * du2 + dh2 * (1.0 + sc2)
        du1 = _ln_bwd(dx1, xh, rstd, gam)
        return (du1, du1 * (1.0 + g1), a_sc + _colsum(dh2 * x1), a_sh + _colsum(dh2), a_gam + _colsum(dx1 * xh),
                a_bet + _colsum(dx1), a_g1 + _colsum(du1 * y))

    return _rowk("ln1_bwd", fn, s, 256, [(dh2, D, 0, 0), (du2, D, 0, 0), (x0, D, 0, 0), (y, D, 0, 0)],
                 [g1, gam, bet, sc2], [(D, F32), (D, BF16)], [(1, D)] * 5)


def _input_grad(dh1, du1, x0, sc1, s):
    def fn(pos, dh1, du1, x0, sc1, a_sc, a_sh):
        return ALPHA * du1 + dh1 * (1.0 + sc1), a_sc + _colsum(dh1 * x0), a_sh + _colsum(dh1)

    return _rowk("input_grad", fn, s, 256, [(dh1, D, 0, 0), (du1, D, 0, 0), (x0, D, 0, 0)], [sc1],
                 [(D, F32)], [(1, D)] * 2)


def _adamw(name, w, g, m, v, *, tr, slots):
    r, c = w.shape

    def body(w_ref, g_ref, m_ref, v_ref, g_out, d_out, m_out, v_out):
        if slots:
            grad = g_ref[0].astype(F32)
            for k in range(1, N_DEV):
                grad = grad + g_ref[k].astype(F32)
        else:
            grad = g_ref[...]
        m_new = ADAM_B1 * m_ref[...] + (1.0 - ADAM_B1) * grad
        v_new = ADAM_B2 * v_ref[...] + (1.0 - ADAM_B2) * (grad * grad)
        m_hat = m_new / (1.0 - ADAM_B1 ** ADAM_STEP)
        v_hat = v_new / (1.0 - ADAM_B2 ** ADAM_STEP)
        g_out[...] = grad
        d_out[...] = -ADAM_LR * (m_hat / (jnp.sqrt(v_hat) + ADAM_EPS) + ADAM_WD * w_ref[...])
        m_out[...] = m_new
        v_out[...] = v_new

    tile = pl.BlockSpec((tr, c), lambda i: (i, 0))
    g_spec = pl.BlockSpec((N_DEV, tr, c), lambda i: (0, i, 0)) if slots else tile
    return pl.pallas_call(
        body, name=name, grid=(r // tr,),
        in_specs=[tile, g_spec, tile, tile], out_specs=[tile] * 4,
        out_shape=[jax.ShapeDtypeStruct((r, c), F32)] * 4,
        compiler_params=_params(("parallel",)),
    )(w, g, m, v)


def _dot_f32(a, b, dims=NN):
    a0, a1, a2 = _split3(a)
    b0, b1, b2 = _split3(b)
    acc = _dot(a0, b0, dims)
    for x, y in ((a0, b1), (a1, b0), (a1, b1), (a0, b2), (a2, b0)):
        acc = acc + _dot(x, y, dims)
    return acc


def _ada_mod(c_all, w_shard, b_shard):
    def body(c_ref, w_ref, b_ref, o_ref):
        act = _silu(c_ref[...])
        act16 = jnp.concatenate([act, jnp.zeros_like(act)], axis=0)
        o_ref[...] = _dot_f32(act16, w_ref[...])[0:N_DEV] + b_ref[...]

    return pl.pallas_call(
        body, name="ada_mod", out_shape=jax.ShapeDtypeStruct((N_DEV, w_shard.shape[1]), F32),
        compiler_params=_params(None),
    )(c_all, w_shard, b_shard)


def _ada_grad(c_all, dmod_cols, dmod_all):
    def body(c_ref, dc_ref, da_ref, gw_ref, gb_ref):
        act = _silu(c_ref[...])
        act16 = jnp.concatenate([act, jnp.zeros_like(act)], axis=0)
        dm = dc_ref[...]
        dm16 = jnp.concatenate([dm, jnp.zeros_like(dm)], axis=0)
        gw_ref[...] = _dot_f32(act16, dm16, TN)
        gb_ref[...] = _colsum(da_ref[...])

    return pl.pallas_call(
        body, name="ada_grad",
        out_shape=[jax.ShapeDtypeStruct((D, dmod_cols.shape[1]), F32), jax.ShapeDtypeStruct((1, 6 * D), F32)],
        compiler_params=_params(None),
    )(c_all, dmod_cols, dmod_all)


def _sum_slots(name, g):
    def body(g_ref, o_ref):
        acc = g_ref[0]
        for k in range(1, N_DEV):
            acc = acc + g_ref[k]
        o_ref[...] = acc

    return pl.pallas_call(body, name=name, out_shape=jax.ShapeDtypeStruct(g.shape[1:], F32),
                          compiler_params=_params(None))(g)


def _exchange(name, xs, scatter):
    n = len(xs)
    n_peer = N_DEV - 1

    def body(*refs):
        x_refs, o_refs = refs[:n], refs[n:2 * n]
        send_sems, recv_sems, local_sems = refs[2 * n:]
        mx, my, mc = lax.axis_index("x"), lax.axis_index("y"), lax.axis_index("c")
        me = 4 * mx + 2 * my + mc

        def src(a, slot):
            return x_refs[a].at[slot] if scatter else x_refs[a]

        own = [pltpu.make_async_copy(src(a, me), o_refs[a].at[me], local_sems.at[a]) for a in range(n)]
        for cp in own:
            cp.start()
        sends = []
        for d in range(1, N_DEV):
            px = 1 - mx if d & 4 else mx
            py = 1 - my if d & 2 else my
            pc = 1 - mc if d & 1 else mc
            peer = 4 * px + 2 * py + pc
            for a in range(n):
                def copy(src_slot, dst_slot, a=a, d=d, to=(px, py, pc)):
                    return pltpu.make_async_remote_copy(
                        src_ref=src(a, src_slot), dst_ref=o_refs[a].at[dst_slot],
                        send_sem=send_sems.at[a * n_peer + d - 1], recv_sem=recv_sems.at[a * n_peer + d - 1],
                        device_id=to, device_id_type=pl.DeviceIdType.MESH)

                out = copy(peer, me)
                out.start()
                sends.append((out, copy(me, peer)))
        for _, arrival in sends:
            arrival.wait_recv()
        for out, _ in sends:
            out.wait_send()
        for cp in own:
            cp.wait()

    shapes = [tuple(x.shape[1:] if scatter else x.shape) for x in xs]
    return pl.pallas_call(
        body, name=name,
        in_specs=[pl.BlockSpec(memory_space=pl.ANY)] * n, out_specs=[pl.BlockSpec(memory_space=pl.ANY)] * n,
        out_shape=[jax.ShapeDtypeStruct((N_DEV,) + sh, x.dtype) for sh, x in zip(shapes, xs)],
        scratch_shapes=[pltpu.SemaphoreType.DMA((n * n_peer,)), pltpu.SemaphoreType.DMA((n * n_peer,)),
                        pltpu.SemaphoreType.DMA((n,))],
        compiler_params=pltpu.CompilerParams(has_side_effects=True),
    )(*xs)


def _after(x, dep):
    return lax.optimization_barrier((x, dep))[0]


_HBM = pl.BlockSpec(memory_space=pltpu.HBM)
_SEM = pl.BlockSpec(memory_space=pltpu.SEMAPHORE)


def _exchange_copies(x_refs, land_refs, send_sems, recv_sems, scatter):
    n = len(x_refs)
    n_peer = N_DEV - 1
    mx, my, mc = lax.axis_index("x"), lax.axis_index("y"), lax.axis_index("c")
    me = 4 * mx + 2 * my + mc
    pairs = []
    for d in range(1, N_DEV):
        px = 1 - mx if d & 4 else mx
        py = 1 - my if d & 2 else my
        pc = 1 - mc if d & 1 else mc
        peer = 4 * px + 2 * py + pc
        for a in range(n):
            def copy(src_slot, dst_slot, a=a, d=d, to=(px, py, pc)):
                return pltpu.make_async_remote_copy(
                    src_ref=x_refs[a].at[src_slot] if scatter else x_refs[a], dst_ref=land_refs[a].at[dst_slot],
                    send_sem=send_sems.at[a * n_peer + d - 1], recv_sem=recv_sems.at[a * n_peer + d - 1],
                    device_id=to, device_id_type=pl.DeviceIdType.MESH)

            pairs.append((copy(peer, me), copy(me, peer)))
    return me, pairs


def _exchange_start(name, xs, scatter):
    n = len(xs)
    shapes = [tuple(x.shape[1:] if scatter else x.shape) for x in xs]

    def body(*refs):
        x_refs, land_refs = refs[:n], refs[n:2 * n]
        send_sems, recv_sems = refs[2 * n], refs[2 * n + 1]
        token, own_sems = refs[4 * n + 2], refs[4 * n + 3]
        me, pairs = _exchange_copies(x_refs, land_refs, send_sems, recv_sems, scatter)
        own = [pltpu.make_async_copy(x_refs[a].at[me] if scatter else x_refs[a], land_refs[a].at[me], own_sems.at[a])
               for a in range(n)]
        for cp in own:
            cp.start()
        for out, _ in pairs:
            out.start()
        for cp in own:
            cp.wait()
        token[...] = jnp.zeros(token.shape, token.dtype)

    lands = [pltpu.with_memory_space_constraint(lax.empty((N_DEV,) + sh, x.dtype), pltpu.HBM) for sh, x in zip(shapes, xs)]
    res = pl.pallas_call(
        body, name=name,
        out_shape=(pltpu.SemaphoreType.DMA((n * (N_DEV - 1),)), pltpu.SemaphoreType.DMA((n * (N_DEV - 1),)),
                   *[pltpu.HBM(x.shape, x.dtype) for x in xs], *[pltpu.HBM(l.shape, l.dtype) for l in lands],
                   jax.ShapeDtypeStruct((8, 128), F32)),
        in_specs=[_HBM] * (2 * n),
        out_specs=(_SEM, _SEM, *[_HBM] * (2 * n), pl.BlockSpec(memory_space=pltpu.VMEM)),
        input_output_aliases={i: 2 + i for i in range(2 * n)},
        scratch_shapes=[pltpu.SemaphoreType.DMA((n,))],
        compiler_params=pltpu.CompilerParams(has_side_effects=pltpu.SideEffectType.DATAFLOW_SIDE_EFFECTING),
    )(*[pltpu.with_memory_space_constraint(x, pltpu.HBM) for x in xs], *lands)
    return dict(send=res[0], recv=res[1], xs=list(res[2:2 + n]), lands=list(res[2 + n:2 + 2 * n]), token=res[2 + 2 * n])


def _exchange_wait(name, handle, after, scatter):
    n = len(handle['xs'])

    def body(*refs):
        x_refs, land_refs = refs[:n], refs[n:2 * n]
        send_sems, recv_sems = refs[2 * n], refs[2 * n + 1]
        _, pairs = _exchange_copies(x_refs, land_refs, send_sems, recv_sems, scatter)
        for out, arrival in pairs:
            out.wait_send()
            arrival.wait_recv()

    res = pl.pallas_call(
        body, name=name,
        out_shape=tuple(pltpu.HBM(a.shape, a.dtype) for a in handle['xs'] + handle['lands']),
        in_specs=[_HBM] * (2 * n) + [_SEM, _SEM, pl.BlockSpec(memory_space=pl.ANY)],
        out_specs=tuple([_HBM] * (2 * n)),
        input_output_aliases={i: i for i in range(2 * n)},
        compiler_params=pltpu.CompilerParams(has_side_effects=pltpu.SideEffectType.DATAFLOW_SIDE_EFFECTING),
    )(*handle['xs'], *handle['lands'], handle['send'], handle['recv'], after)
    return list(res[n:])


def _relu2(a):
    r = jnp.maximum(a, 0.0)
    return r * r


def _relu2_grad(acc, a):
    return acc * (2.0 * jnp.maximum(a, 0.0))


def _local_step(x0, tgt, mod, wcat, late_weights, send_grads, conv_w, conv_b, dt_bias, a_log, d_skip, ssm_norm_w, f_bias,
                attn_norm_w, ln1_g, ln1_b, ln2_g, ln2_b):
    ff_w = DFF // N_DEV
    s = x0.shape[0]
    tm = min(512, s)
    ts = min(1024, s)
    sh1, sc1, g1, sh2, sc2, g2 = [mod[:, i * D:(i + 1) * D] for i in range(6)]
    zero = jnp.zeros((1, 128 - 2 * NH), F32)
    bias128 = jnp.concatenate([dt_bias, f_bias, zero], axis=1)
    alog128 = jnp.concatenate([a_log, jnp.zeros((1, 128 - NH), F32)], axis=1)
    dskip_x = jnp.repeat(d_skip, HD, axis=1)
    w_xs, w_bc, b_xs, b_bc = conv_w[:, :D], conv_w[:, D:], conv_b[:, :D], conv_b[:, D:]

    p = _mm_nn("in_proj", x0, wcat, tm=tm, tn=640, tk=D, out_dtype=F32, pro=_modulate, aux=(sc1, sh1))
    xs_a, bc_a = _conv_fwd(p, w_xs, b_xs, w_bc, b_bc, s)
    y_ssd, states = _ssd_fwd(xs_a, bc_a, p, bias128, alog128, dskip_x, s)
    cum_t = _cum_fwd(p, bias128, s)
    att, lse = _attn_fwd(p, cum_t, s)
    wout, w1s, w2 = late_weights(lse)
    ymix = _mix_norm(y_ssd, p, att, ssm_norm_w, attn_norm_w, s)
    y = _mm_nn("out_proj", ymix, wout, tm=tm, tn=512, tk=2 * D, out_dtype=F32)
    x1, h2 = _ln1(x0, y, g1, ln1_g, ln1_b, sc2, sh2, s)
    a1 = _mm_nn("ff_in", h2, w1s, tm=tm, tn=ff_w, tk=D, out_dtype=F32)
    ff = _mm_nn("ff_out", a1, w2, tm=tm, tn=512, tk=1024, out_dtype=F32, pro=_relu2)
    du2, dff, sq_err, d_ln2_g, d_ln2_b, d_g2 = _ln2_loss(x1, ff, tgt, g2, ln2_g, ln2_b, s)

    da1 = _mm_nt("d_ff_hidden", [(dff, D, 0)], [(w2, D, 0)], n=DFF, tm=tm, tn=512, out_dtype=BF16, epi=_relu2_grad,
                 epi_aux=(a1,))
    d_w2 = _mm_tn("d_w_ff_out", a1, dff, tm=1024, tn=512, ts=ts, pro=_relu2)
    d_w1s = _mm_tn("d_w_ff_in", h2, da1, tm=1024, tn=ff_w, ts=ts, col_shards=True)
    dh2 = _mm_nt("d_ff_input", [(da1, ff_w, k) for k in range(N_DEV)], [(w1s, ff_w, k) for k in range(N_DEV)], n=D,
                 tm=min(256, s), tn=512, out_dtype=F32)
    sent = send_grads("ff", [d_w1s, d_w2.reshape(N_DEV, -1, D)])
    du1, dy, d_sc2, d_sh2, d_ln1_g, d_ln1_b, d_g1 = _ln1_bwd(dh2, du2, x0, y, g1, ln1_g, ln1_b, _after(sc2, sent), s)

    dmix = _mm_nt("d_mix", [(dy, D, 0)], [(wout, D, 0)], n=2 * D, tm=tm, tn=512, out_dtype=F32)
    d_wout = _mm_tn("d_w_out", ymix, dy, tm=1024, tn=512, ts=ts)
    sent = send_grads("out", [d_wout.reshape(N_DEV, -1, D)])
    dy_ssd, dz, datt, d_ssm_w, d_attn_w = _mix_norm_bwd(dmix, y_ssd, p, att, _after(ssm_norm_w, sent), attn_norm_w, s)
    dq, dk, dv, dcs, drs = _attn_bwd(p, cum_t, att, lse, datt, s)
    dxs_a, dbc_a, ddt_raw, d_alog, d_dskip = _ssd_bwd(dy_ssd, xs_a, bc_a, p, states, bias128, alog128, dskip_x, s)
    dcum = jnp.pad(drs[:, ::HD] - dcs[:, :2, :].reshape(NH, s).T, ((0, 0), (NH, 128 - 2 * NH)))
    ddtf, _, d_bias = _cum_bwd(dcum, ddt_raw, p, bias128, s)
    dxs, dbc, d_wc_xs, d_bc_xs, d_wc_bc, d_bc_bc = _conv_bwd(dxs_a, dbc_a, p, w_xs, b_xs, w_bc, b_bc, s)

    segs = [(dz, OFF_Z, D), (dxs, OFF_XS, D), (dq, OFF_Q, D), (dk, OFF_K, D), (dv, OFF_V, D), (dbc, OFF_BC, 512),
            (ddtf, OFF_DTF, 128)]
    d_z, d_xs, d_q, d_k, d_v, d_bcw, d_dtf = [
        _mm_tn("d_w_in_%d" % i, x0, a, tm=1024, tn=min(w, 512), ts=ts, pro=_modulate, aux=(sc1, sh1))
        for i, (a, _, w) in enumerate(segs)]
    d_w_in = dict(z=d_z, xs=d_xs, bc=d_bcw, dt=d_dtf[:, :NH], q=d_q, k=d_k, v=d_v, f=d_dtf[:, NH:2 * NH])
    sent = send_grads("in", [_shard_w_in_grad(d_w_in)])
    segs[-1] = (_after(ddtf, sent), OFF_DTF, 128)
    dh1 = _mm_nt("d_h1", [(a, w, 0) for a, _, w in segs], [(wcat, w, off // w) for _, off, w in segs], n=D,
                 tm=min(256, s), tn=512, out_dtype=F32)
    grad_x, d_sc1, d_sh1 = _input_grad(dh1, du1, x0, sc1, s)

    return dict(
        loss=(0.5 / D) * jnp.sum(sq_err), grad_x=grad_x,
        d_mod=jnp.concatenate([d_sh1, d_sc1, d_g1, d_sh2, d_sc2, d_g2], axis=1),
        d_conv_w=jnp.concatenate([d_wc_xs[:4], d_wc_bc[:4]], axis=1), d_conv_b=jnp.concatenate([d_bc_xs, d_bc_bc], axis=1),
        d_ssm_norm_w=d_ssm_w, d_attn_norm_w=d_attn_w, d_ln1_g=d_ln1_g, d_ln1_b=d_ln1_b, d_ln2_g=d_ln2_g, d_ln2_b=d_ln2_b,
        d_gate_bias=d_bias, d_a_log=d_alog, d_d_skip=d_dskip)


W_IN_SEGS = [('z', W_Z, D), ('xs', W_XS, D), ('bc', W_BC, 512), ('dt', W_DT, NH), ('q', W_Q, D), ('k', W_K, D),
             ('v', W_V, D), ('f', W_F, NH)]
SHARD_W = IN_COLS // N_DEV


def _pack_w_in(shards):
    def cols(lo, hi):
        pieces = []
        while lo < hi:
            dev = lo // SHARD_W
            end = min(hi, (dev + 1) * SHARD_W)
            pieces.append(shards[dev][:, lo - dev * SHARD_W:end - dev * SHARD_W])
            lo = end
        return pieces

    seg = {n: cols(off, off + w) for n, off, w in W_IN_SEGS}
    pieces = seg['z'] + seg['xs'] + seg['q'] + seg['k'] + seg['v'] + seg['bc'] + seg['dt'] + seg['f']
    return jnp.concatenate(pieces + [jnp.zeros((D, 128 - 2 * NH), shards.dtype)], axis=1)


def _shard_w_in_grad(d_w_in):
    blocks = []
    for dev in range(N_DEV):
        lo, hi = dev * SHARD_W, (dev + 1) * SHARD_W
        pieces = [d_w_in[n][:, max(lo, off) - off:min(hi, off + w) - off] for n, off, w in W_IN_SEGS
                  if max(lo, off) < min(hi, off + w)]
        blocks.append(jnp.concatenate(pieces, axis=1))
    return jnp.stack(blocks, axis=0)


WEIGHTS = ['w_ada', 'b_ada', 'w_in', 'conv_w', 'conv_b', 'dt_bias', 'a_log', 'd_skip', 'ssm_norm_w', 'f_bias',
           'attn_norm_w', 'w_out', 'ln1_g', 'ln1_b', 'w_ff_in', 'w_ff_out', 'ln2_g', 'ln2_b']
BIG = ['w_in', 'w_out', 'w_ff_in', 'w_ff_out']
SMALL = ['b_ada', 'conv_b', 'ssm_norm_w', 'attn_norm_w', 'ln1_g', 'ln1_b', 'ln2_g', 'ln2_b', 'dt_bias', 'a_log', 'd_skip',
         'f_bias', 'conv_w']


def _pad_lanes(v, n=128):
    return jnp.pad(v, ((0, 0), (0, n - v.shape[1])))


def _small_block(vals):
    rows = [_pad_lanes(vals[n].reshape(1, -1), -(-vals[n].size // 128) * 128).reshape(-1, 128) for n in SMALL]
    block = jnp.concatenate(rows, axis=0)
    return jnp.pad(block, ((0, 120 - block.shape[0]), (0, 0)))


def _small_unblock(block, like):
    out, r = {}, 0
    for n in SMALL:
        size = like[n].size
        nr = -(-size // 128)
        out[n] = block[r:r + nr].reshape(-1)[:size].reshape(like[n].shape)
        r += nr
    return out


def kernel(x, c, w_ada, b_ada, w_in, conv_w, conv_b, dt_bias, a_log, d_skip, ssm_norm_w, f_bias, attn_norm_w, w_out, ln1_g, ln1_b, w_ff_in, w_ff_out, ln2_g, ln2_b, loss_target, m_w_ada, m_b_ada, m_w_in, m_conv_w, m_conv_b, m_dt_bias, m_a_log, m_d_skip, m_ssm_norm_w, m_f_bias, m_attn_norm_w, m_w_out, m_ln1_g, m_ln1_b, m_w_ff_in, m_w_ff_out, m_ln2_g, m_ln2_b, v_w_ada, v_b_ada, v_w_in, v_conv_w, v_conv_b, v_dt_bias, v_a_log, v_d_skip, v_ssm_norm_w, v_f_bias, v_attn_norm_w, v_w_out, v_ln1_g, v_ln1_b, v_w_ff_in, v_w_ff_out, v_ln2_g, v_ln2_b):
    args = dict(locals())
    w = {n: args[n] for n in WEIGHTS}
    m = {n: args['m_' + n] for n in WEIGHTS}
    v = {n: args['v_' + n] for n in WEIGHTS}
    me = 4 * lax.axis_index("x") + 2 * lax.axis_index("y") + lax.axis_index("c")
    ada_cols = 6 * D // N_DEV
    conv_cols = conv_w.shape[2]

    c_all, conv_all = _exchange("gather_cond", [c, conv_w[0]], False)
    c_all = c_all.reshape(N_DEV, D)
    conv_w_full = conv_all.transpose(1, 0, 2).reshape(4, N_DEV * conv_cols)
    b_shard = lax.dynamic_slice(b_ada, (0, me * ada_cols), (1, ada_cols))
    mod_all, = _exchange("gather_mod", [_ada_mod(c_all, w_ada[0], b_shard)], False)
    mod = lax.dynamic_index_in_dim(mod_all, me, axis=1, keepdims=False).reshape(1, 6 * D)

    win_s, = _exchange("gather_w_in", [w_in[0].astype(BF16)], False)
    rest_src = _after([w[n][0].astype(BF16) for n in BIG[1:]], win_s)
    rest = _exchange_start("gather_rest_start", rest_src, False)

    def late_weights(after):
        wout_s, w1s, w2_s = _exchange_wait("gather_rest_wait", rest, after, False)
        return wout_s.reshape(2 * D, D), w1s, w2_s.reshape(DFF, D)

    sends = {}

    def send_grads(tag, blocks):
        sends[tag] = _exchange_start("scatter_%s_start" % tag, blocks, True)
        return sends[tag]['token']

    out = _local_step(x[0], loss_target[0], _after(mod, rest['token']), _pack_w_in(win_s), late_weights, send_grads,
                      conv_w_full, conv_b, dt_bias, a_log, d_skip, ssm_norm_w, f_bias, attn_norm_w, ln1_g, ln1_b, ln2_g, ln2_b)
    g_ff_in, g_ff_out = _exchange_wait("scatter_ff_wait", sends['ff'], out['grad_x'], True)
    g_out, = _exchange_wait("scatter_out_wait", sends['out'], out['grad_x'], True)
    g_in, = _exchange_wait("scatter_in_wait", sends['in'], out['grad_x'], True)
    g_parts = [g_in, g_out, g_ff_in, g_ff_out]
    big = {n: _adamw("adamw_" + n, w[n][0], g, m[n][0], v[n][0], tr=256, slots=True) for n, g in zip(BIG, g_parts)}

    small = jnp.concatenate(
        [out['d_mod'], out['d_conv_w'].reshape(1, -1), out['d_conv_b'], out['d_ssm_norm_w'], out['d_attn_norm_w'],
         out['d_ln1_g'], out['d_ln1_b'], out['d_ln2_g'], out['d_ln2_b'], out['d_gate_bias'], out['d_a_log'],
         out['d_d_skip'], jnp.zeros((1, 128), F32)], axis=1).reshape(-1, 128)
    small_all, = _exchange("gather_small", [small], False)
    ssum = _sum_slots("sum_small", small_all)
    dmod_all = small_all[:, :6 * D // 128].reshape(N_DEV, 6 * D)
    g_w_ada, g_b_ada = _ada_grad(c_all, lax.dynamic_slice(dmod_all, (0, me * ada_cols), (N_DEV, ada_cols)), dmod_all)
    rows = lambda a, b: ssum[a:b].reshape(1, -1)
    g_conv_w = lax.dynamic_slice(ssum[48:96].reshape(4, N_DEV * conv_cols), (0, me * conv_cols), (4, conv_cols))
    g_small = dict(b_ada=g_b_ada, conv_w=g_conv_w[None], conv_b=rows(96, 108), ssm_norm_w=rows(108, 116),
                   attn_norm_w=rows(116, 124), ln1_g=rows(124, 132), ln1_b=rows(132, 140), ln2_g=rows(140, 148),
                   ln2_b=rows(148, 156), dt_bias=ssum[156:157, :NH], f_bias=ssum[156:157, NH:2 * NH],
                   a_log=ssum[157:158, :NH], d_skip=ssum[158:159, :NH])
    sm = _adamw("adamw_small", _small_block(w), _small_block(g_small), _small_block(m), _small_block(v), tr=120, slots=False)
    ada = _adamw("adamw_ada", w_ada[0], g_w_ada, m_w_ada[0], v_w_ada[0], tr=256, slots=False)

    results = []
    for k in range(4):
        vals = _small_unblock(sm[k], w)
        vals['w_ada'] = ada[k][None]
        for n in BIG:
            vals[n] = big[n][k][None]
        results.append(vals)
    loss = lax.psum(out['loss'], ("x", "y", "c"))
    return (loss, out['grad_x'][None], *[res[n] for res in results for n in WEIGHTS])
```

```python
import functools

import jax
import jax.numpy as jnp
from jax import lax
from jax.experimental import pallas as pl
from jax.experimental.pallas import tpu as pltpu

F32, BF16 = jnp.float32, jnp.bfloat16

N_DEV = 8
D = 1024
NH, HD = 16, 64
NSTATE = 128
CHUNK = 128
HG = 8
DFF = 4096
ALPHA = 2.0 ** 0.25
EPS = 1e-5
ATT_SCALE = HD ** -0.5

OFF_Z, OFF_XS, OFF_Q, OFF_K, OFF_V, OFF_BC, OFF_DTF = 0, 1024, 2048, 3072, 4096, 5120, 5632
PCOLS = 5760
W_Z, W_XS, W_BC, W_DT, W_Q, W_K, W_V, W_F = 0, 1024, 2048, 2560, 2576, 3600, 4624, 5648
IN_COLS = 5664

ADAM_LR, ADAM_B1, ADAM_B2, ADAM_EPS, ADAM_WD, ADAM_STEP = 0.001, 0.9, 0.999, 1e-08, 0.01, 10

VMEM_LIMIT = 56 << 20

NN = (((1,), (0,)), ((), ()))
NT = (((1,), (1,)), ((), ()))
TN = (((0,), (0,)), ((), ()))


def _dot(a, b, dims=NN):
    return lax.dot_general(a, b, dims, preferred_element_type=F32)


def _bdot(a, b, dims=NN):
    return _dot(a.astype(BF16), b.astype(BF16), dims)


def _split3(v):
    parts, rest = [], v
    for _ in range(3):
        p = rest.astype(BF16)
        parts.append(p)
        rest = rest - p.astype(F32)
    return parts


def _sel_left(m01, v):
    return sum(_dot(m01, p) for p in _split3(v))


def _sel_right(v, m01, dims=NN):
    return sum(_dot(p, m01, dims) for p in _split3(v))


def _iota(shape, dim):
    return lax.broadcasted_iota(jnp.int32, shape, dim)


def _tri_lower(n):
    return (_iota((n, n), 1) <= _iota((n, n), 0)).astype(BF16)


def _tri_upper(n):
    return (_iota((n, n), 1) >= _iota((n, n), 0)).astype(BF16)


def _head_expand():
    return (lax.shift_right_logical(_iota((128, D), 1), 6) == _iota((128, D), 0)).astype(BF16)


def _head_reduce():
    return (lax.shift_right_logical(_iota((D, 128), 0), 6) == _iota((D, 128), 1)).astype(BF16)


def _sigmoid(x):
    return 1.0 / (1.0 + jnp.exp(-x))


def _silu(x):
    return x * _sigmoid(x)


def _dsilu(x):
    s = _sigmoid(x)
    return s * (1.0 + x * (1.0 - s))


def _softplus(x):
    return jnp.maximum(x, 0.0) + jnp.log(1.0 + jnp.exp(-jnp.abs(x)))


def _log_sigmoid(x):
    return jnp.minimum(x, 0.0) - jnp.log(1.0 + jnp.exp(-jnp.abs(x)))


def _params(sem):
    return pltpu.CompilerParams(dimension_semantics=sem, vmem_limit_bytes=VMEM_LIMIT)


def _mm_nn(name, a, b, *, tm, tn, tk, out_dtype, pro=None, aux=()):
    m, k_all = a.shape
    b_sharded = b.ndim == 3
    n = b.shape[0] * b.shape[2] if b_sharded else b.shape[1]
    assert not b_sharded or tn == b.shape[2]
    nk = k_all // tk
    n_aux = len(aux)
    b_spec = (pl.BlockSpec((None, tk, tn), lambda i, j, k: (j, k, 0)) if b_sharded
              else pl.BlockSpec((tk, tn), lambda i, j, k: (k, j)))

    def body(a_ref, b_ref, *rest):
        aux_refs, o_ref = rest[:n_aux], rest[n_aux]
        at = a_ref[...]
        if pro is not None:
            at = pro(at, *[r[...] for r in aux_refs])
        part = _bdot(at, b_ref[...])
        if nk == 1:
            o_ref[...] = part.astype(out_dtype)
            return
        acc_ref = rest[n_aux + 1]
        kk = pl.program_id(2)

        @pl.when(kk == 0)
        def _():
            acc_ref[...] = part

        @pl.when(kk > 0)
        def _():
            acc_ref[...] += part

        @pl.when(kk == nk - 1)
        def _():
            o_ref[...] = acc_ref[...].astype(out_dtype)

    return pl.pallas_call(
        body, name=name,
        grid=(m // tm, n // tn, nk),
        in_specs=[pl.BlockSpec((tm, tk), lambda i, j, k: (i, k)), b_spec]
        + [pl.BlockSpec((1, tk), lambda i, j, k: (0, k)) for _ in aux],
        out_specs=pl.BlockSpec((tm, tn), lambda i, j, k: (i, j)),
        out_shape=jax.ShapeDtypeStruct((m, n), out_dtype),
        scratch_shapes=[] if nk == 1 else [pltpu.VMEM((tm, tn), F32)],
        compiler_params=_params(("parallel", "parallel", "arbitrary")),
    )(a, b, *aux)


def _mm_nt(name, a_list, b_list, *, n, tm, tn, out_dtype, epi=None, epi_aux=()):
    m = a_list[0][0].shape[0]
    n_op = len(a_list)
    n_epi = len(epi_aux)

    def body(*refs):
        a_refs, b_refs = refs[:n_op], refs[n_op:2 * n_op]
        e_refs, o_ref = refs[2 * n_op:2 * n_op + n_epi], refs[2 * n_op + n_epi]
        acc = None
        for a_ref, b_ref in zip(a_refs, b_refs):
            part = _bdot(a_ref[...], b_ref[...], NT)
            acc = part if acc is None else acc + part
        if epi is not None:
            acc = epi(acc, *[r[...] for r in e_refs])
        o_ref[...] = acc.astype(out_dtype)

    in_specs = [pl.BlockSpec((tm, w), functools.partial(lambda i, j, cb: (i, cb), cb=cb)) for (_, w, cb) in a_list]
    for (b, w, cb) in b_list:
        if b.ndim == 3:
            in_specs.append(pl.BlockSpec((None, tn, w), functools.partial(lambda i, j, cb: (cb, j, 0), cb=cb)))
        else:
            in_specs.append(pl.BlockSpec((tn, w), functools.partial(lambda i, j, cb: (j, cb), cb=cb)))
    in_specs += [pl.BlockSpec((tm, tn), lambda i, j: (i, j)) for _ in epi_aux]
    return pl.pallas_call(
        body, name=name,
        grid=(m // tm, n // tn),
        in_specs=in_specs,
        out_specs=pl.BlockSpec((tm, tn), lambda i, j: (i, j)),
        out_shape=jax.ShapeDtypeStruct((m, n), out_dtype),
        compiler_params=_params(("parallel", "parallel")),
    )(*[a for (a, _, _) in a_list], *[b for (b, _, _) in b_list], *epi_aux)


def _mm_tn(name, a, b, *, tm, tn, ts, pro=None, aux=(), col_shards=False):
    s_all, ka = a.shape
    nb = b.shape[1]
    n_aux = len(aux)
    ns = s_all // ts
    assert not col_shards or tn == nb // N_DEV

    def body(a_ref, b_ref, *rest):
        aux_refs, o_ref, acc_ref = rest[:n_aux], rest[n_aux], rest[n_aux + 1]
        at = a_ref[...]
        if pro is not None:
            at = pro(at, *[r[...] for r in aux_refs])
        part = _bdot(at, b_ref[...], TN)
        ss = pl.program_id(2)

        @pl.when(ss == 0)
        def _():
            acc_ref[...] = part

        @pl.when(ss > 0)
        def _():
            acc_ref[...] += part

        @pl.when(ss == ns - 1)
        def _():
            o_ref[...] = acc_ref[...].astype(BF16)

    if col_shards:
        out_spec = pl.BlockSpec((None, tm, tn), lambda i, j, s: (j, i, 0))
        out_shape = jax.ShapeDtypeStruct((N_DEV, ka, tn), BF16)
    else:
        out_spec = pl.BlockSpec((tm, tn), lambda i, j, s: (i, j))
        out_shape = jax.ShapeDtypeStruct((ka, nb), BF16)
    return pl.pallas_call(
        body, name=name,
        grid=(ka // tm, nb // tn, ns),
        in_specs=[pl.BlockSpec((ts, tm), lambda i, j, s: (s, i)),
                  pl.BlockSpec((ts, tn), lambda i, j, s: (s, j))]
        + [pl.BlockSpec((1, tm), lambda i, j, s: (0, i)) for _ in aux],
        out_specs=out_spec, out_shape=out_shape,
        scratch_shapes=[pltpu.VMEM((tm, tn), F32)],
        compiler_params=_params(("parallel", "parallel", "arbitrary")),
    )(a, b, *aux)


def _rowk(name, fn, n_rows, tr, rows, fulls, outs, accs, reverse=False):
    n = n_rows // tr
    n_row, n_full, n_out, n_acc = len(rows), len(fulls), len(outs), len(accs)

    def pos(i):
        return (n - 1 - i) if reverse else i

    def body(*refs):
        row_refs = refs[:n_row]
        full_refs = refs[n_row:n_row + n_full]
        out_refs = refs[n_row + n_full:n_row + n_full + n_out]
        acc_refs = refs[n_row + n_full + n_out:]
        i = pl.program_id(0)

        @pl.when(i == 0)
        def _():
            for r in acc_refs:
                r[...] = jnp.zeros(r.shape, r.dtype)

        res = fn(pos(i), *[r[...] for r in row_refs], *[r[...] for r in full_refs], *[r[...] for r in acc_refs])
        for r, v in zip(out_refs + acc_refs, res):
            r[...] = v.astype(r.dtype)

    def row_map(i, cb, shift):
        return (jnp.clip(pos(i) + shift, 0, n - 1), cb)

    in_specs = [pl.BlockSpec((tr, w), functools.partial(row_map, cb=cb, shift=sh)) for (_, w, cb, sh) in rows]
    in_specs += [pl.BlockSpec(f.shape, functools.partial(lambda i, nd: (0,) * nd, nd=f.ndim)) for f in fulls]
    out_specs = [pl.BlockSpec((tr, w), lambda i: (pos(i), 0)) for (w, _) in outs]
    out_specs += [pl.BlockSpec((r, w), lambda i: (0, 0)) for (r, w) in accs]
    out_shape = [jax.ShapeDtypeStruct((n_rows, w), dt) for (w, dt) in outs]
    out_shape += [jax.ShapeDtypeStruct((r, w), F32) for (r, w) in accs]
    return pl.pallas_call(
        body, name=name, grid=(n,), in_specs=in_specs, out_specs=out_specs, out_shape=out_shape,
        compiler_params=_params(("arbitrary",)),
    )(*[a for (a, _, _, _) in rows], *fulls)


def _colsum(x):
    return jnp.sum(x, axis=0, keepdims=True)


def _mean(x):
    return jnp.mean(x, axis=-1, keepdims=True)


def _modulate(x, sc, sh):
    return x * (1.0 + sc) + sh


def _shift_down(cur, prev, j):
    row = _iota(cur.shape, 0)
    return jnp.where(row < j, pltpu.roll(prev, j, 0), pltpu.roll(cur, j, 0))


def _shift_up(cur, nxt, j):
    tr = cur.shape[0]
    row = _iota(cur.shape, 0)
    return jnp.where(row < tr - j, pltpu.roll(cur, tr - j, 0), pltpu.roll(nxt, tr - j, 0))


def _conv(cur, prev, w, b):
    out = cur * w[3:4] + b
    for j in (1, 2, 3):
        out = out + _shift_down(cur, prev, j) * w[3 - j:4 - j]
    return out


def _conv_fwd(p, w_xs, b_xs, w_bc, b_bc, s):
    def fn(pos, xs, xs_prev, bc, bc_prev, w_xs, b_xs, w_bc, b_bc):
        first = pos == 0
        xs_prev = jnp.where(first, 0.0, xs_prev)
        bc_prev = jnp.where(first, 0.0, bc_prev)
        return _silu(_conv(xs, xs_prev, w_xs, b_xs)), _silu(_conv(bc, bc_prev, w_bc, b_bc))

    return _rowk("conv_fwd", fn, s, 256,
                 [(p, D, OFF_XS // D, 0), (p, D, OFF_XS // D, -1), (p, 512, OFF_BC // 512, 0), (p, 512, OFF_BC // 512, -1)],
                 [w_xs, b_xs, w_bc, b_bc], [(D, F32), (512, F32)], [])


def _conv_bwd(dxs_a, dbc_a, p, w_xs, b_xs, w_bc, b_bc, s):
    tr = 256
    n = s // tr

    def fn(pos, da1, da1n, x1, x1p, x1n, da2, da2n, x2, x2p, x2n, w1, b1, w2, b2, aw1, ab1, aw2, ab2):
        dx1, dw1, db1 = _conv_bwd_fn(pos, n, da1, da1n, x1, x1p, x1n, w1, b1)
        dx2, dw2, db2 = _conv_bwd_fn(pos, n, da2, da2n, x2, x2p, x2n, w2, b2)
        return dx1, dx2, aw1 + dw1, ab1 + db1, aw2 + dw2, ab2 + db2

    cx, cb = OFF_XS // D, OFF_BC // 512
    return _rowk("conv_bwd", fn, s, tr,
                 [(dxs_a, D, 0, 0), (dxs_a, D, 0, 1), (p, D, cx, 0), (p, D, cx, -1), (p, D, cx, 1),
                  (dbc_a, 512, 0, 0), (dbc_a, 512, 0, 1), (p, 512, cb, 0), (p, 512, cb, -1), (p, 512, cb, 1)],
                 [w_xs, b_xs, w_bc, b_bc], [(D, BF16), (512, BF16)], [(8, D), (1, D), (8, 512), (1, 512)])


def _conv_bwd_fn(pos, n, da, da_next, x, x_prev, x_next, w, b):
    first, last = pos == 0, pos == n - 1
    x_prev = jnp.where(first, 0.0, x_prev)
    dc = da * _dsilu(_conv(x, x_prev, w, b))
    dc_next = jnp.where(last, 0.0, da_next * _dsilu(_conv(x_next, x, w, b)))
    dx = dc * w[3:4]
    dws = [None] * 4
    dws[3] = _colsum(dc * x)
    for j in (1, 2, 3):
        dx = dx + _shift_up(dc, dc_next, j) * w[3 - j:4 - j]
        dws[3 - j] = _colsum(dc * _shift_down(x, x_prev, j))
    row = _iota((8, x.shape[1]), 0)
    dw = jnp.zeros((8, x.shape[1]), F32)
    for k in range(4):
        dw = jnp.where(row == k, dws[k], dw)
    return dx, dw, _colsum(dc)


def _ssd_gates(dtf, bias, a_log):
    lane = _iota(dtf.shape, 1)
    head = lane < NH
    dt = jnp.where(head, _softplus(dtf + bias), 0.0)
    a_neg = jnp.where(_iota(a_log.shape, 1) < NH, -jnp.exp(a_log), 0.0)
    a = dt * a_neg
    cs = _sel_left(_tri_lower(CHUNK), a)
    return dt, a_neg, cs


def _decay_mask(cs_ref, cst_ref, h):
    diff = cs_ref[:, h:h + 1] - cst_ref[h:h + 1, :]
    low = _iota((CHUNK, CHUNK), 1) <= _iota((CHUNK, CHUNK), 0)
    return jnp.where(low, jnp.exp(jnp.minimum(diff, 0.0)), 0.0)


def _ssd_fwd(xs_a, bc_a, p, bias128, alog128, dskip_x, s):
    nc = s // CHUNK
    t = CHUNK

    def body(xs_ref, bc_ref, dtf_ref, bias_ref, alog_ref, dsk_ref, y_ref, st_ref,
             state, x_sc, xw_sc, cs_sc, cst_sc, yd_sc):
        c = pl.program_id(0)

        @pl.when(c == 0)
        def _():
            state[...] = jnp.zeros(state.shape, F32)

        dt, _, cs = _ssd_gates(dtf_ref[...], bias_ref[...], alog_ref[...])
        cs_sc[...] = cs
        cst_sc[...] = cs.T
        cs_last = cs[t - 1:t, :]
        expand = _head_expand()
        ex = _sel_right(jnp.concatenate([dt, jnp.exp(cs), jnp.exp(cs_last - cs)], axis=0), expand)
        dt_x, eo_x, we_x = ex[0:t], ex[t:2 * t], ex[2 * t:3 * t]
        g_x = _sel_right(jnp.broadcast_to(jnp.exp(cs_last), (8, 128)), expand)[0:1]
        xs = xs_ref[...]
        x = xs * dt_x
        x_sc[...] = x.astype(BF16)
        xw_sc[...] = (x * we_x).astype(BF16)
        prev = state[...]
        st_ref[0] = prev
        prev_b = prev.astype(BF16)
        for g in range(2):
            cols = slice(g * 512, (g + 1) * 512)
            b_g = bc_ref[:, g * 128:(g + 1) * 128].astype(BF16)
            c_g = bc_ref[:, 256 + g * 128:256 + (g + 1) * 128].astype(BF16)
            gmat = _dot(c_g, b_g, NT)
            y_off = _dot(c_g, prev_b[:, cols]) * eo_x[:, cols]
            s_loc = _dot(b_g, xw_sc[:, cols], TN)
            state[:, cols] = g_x[:, cols] * prev[:, cols] + s_loc
            for e in range(HG):
                h = g * HG + e
                m = gmat * _decay_mask(cs_sc, cst_sc, h)
                yd_sc[:, h * HD:(h + 1) * HD] = _dot(m.astype(BF16), x_sc[:, h * HD:(h + 1) * HD])
            y_ref[:, cols] = yd_sc[:, cols] + y_off + dsk_ref[:, cols] * xs[:, cols]

    return pl.pallas_call(
        body, name="ssd_fwd", grid=(nc,),
        in_specs=[pl.BlockSpec((t, D), lambda c: (c, 0)),
                  pl.BlockSpec((t, 512), lambda c: (c, 0)),
                  pl.BlockSpec((t, 128), lambda c: (c, OFF_DTF // 128)),
                  pl.BlockSpec((1, 128), lambda c: (0, 0)),
                  pl.BlockSpec((1, 128), lambda c: (0, 0)),
                  pl.BlockSpec((1, D), lambda c: (0, 0))],
        out_specs=[pl.BlockSpec((t, D), lambda c: (c, 0)),
                   pl.BlockSpec((1, NSTATE, D), lambda c: (c, 0, 0))],
        out_shape=[jax.ShapeDtypeStruct((s, D), F32), jax.ShapeDtypeStruct((nc, NSTATE, D), F32)],
        scratch_shapes=[pltpu.VMEM((NSTATE, D), F32), pltpu.VMEM((t, D), BF16), pltpu.VMEM((t, D), BF16),
                        pltpu.VMEM((t, 128), F32), pltpu.VMEM((128, t), F32), pltpu.VMEM((t, D), F32)],
        compiler_params=_params(("arbitrary",)),
    )(xs_a, bc_a, p, bias128, alog128, dskip_x)


def _ssd_bwd(dy, xs_a, bc_a, p, states, bias128, alog128, dskip_x, s):
    nc = s // CHUNK
    t = CHUNK

    def body(dy_ref, xs_ref, bc_ref, dtf_ref, st_ref, bias_ref, alog_ref, dsk_ref,
             dxs_ref, dbc_ref, ddt_ref, dalog_ref, dskip_ref,
             dstate, x_sc, dy_sc, dx_sc, deo_sc, dwe_sc, cs_sc, cst_sc, dcol_sc, drow_sc):
        i = pl.program_id(0)

        @pl.when(i == 0)
        def _():
            dstate[...] = jnp.zeros(dstate.shape, F32)
            dalog_ref[...] = jnp.zeros(dalog_ref.shape, F32)
            dskip_ref[...] = jnp.zeros(dskip_ref.shape, F32)

        dtf = dtf_ref[...]
        dt, a_neg, cs = _ssd_gates(dtf, bias_ref[...], alog_ref[...])
        cs_sc[...] = cs
        cst_sc[...] = cs.T
        cs_last = cs[t - 1:t, :]
        eo, we, g_end = jnp.exp(cs), jnp.exp(cs_last - cs), jnp.exp(cs_last)
        expand, reduce = _head_expand(), _head_reduce()
        ex = _sel_right(jnp.concatenate([dt, eo, we], axis=0), expand)
        dt_x, eo_x, we_x = ex[0:t], ex[t:2 * t], ex[2 * t:3 * t]
        g_x = _sel_right(jnp.broadcast_to(g_end, (8, 128)), expand)[0:1]
        xs = xs_ref[...]
        dyv = dy_ref[...]
        x = xs * dt_x
        x_sc[...] = x.astype(BF16)
        dy_sc[...] = dyv.astype(BF16)
        dyo_b = (dyv * eo_x).astype(BF16)
        xw_b = (x * we_x).astype(BF16)
        prev = st_ref[0]
        prev_b = prev.astype(BF16)
        dnext = dstate[...]
        dnext_b = dnext.astype(BF16)
        dcol_sc[...] = jnp.zeros(dcol_sc.shape, F32)
        drow_sc[...] = jnp.zeros(drow_sc.shape, F32)
        lane_row = _iota((1, 128), 1)
        sub_col = _iota((128, 1), 0)
        for g in range(2):
            cols = slice(g * 512, (g + 1) * 512)
            b_g = bc_ref[:, g * 128:(g + 1) * 128].astype(BF16)
            c_g = bc_ref[:, 256 + g * 128:256 + (g + 1) * 128].astype(BF16)
            gmat = _dot(c_g, b_g, NT)
            b_ds = _dot(b_g, dnext_b[:, cols])
            c_s = _dot(c_g, prev_b[:, cols])
            dx_sc[:, cols] = b_ds * we_x[:, cols]
            deo_sc[:, cols] = dyv[:, cols] * c_s
            dwe_sc[:, cols] = b_ds * x[:, cols]
            db = _dot(xw_b[:, cols], dnext_b[:, cols], NT)
            dc = _dot(dyo_b[:, cols], prev_b[:, cols], NT)
            dstate[:, cols] = g_x[:, cols] * dnext[:, cols] + _dot(c_g, dyo_b[:, cols], TN)
            dg = jnp.zeros((t, t), F32)
            for e in range(HG):
                h = g * HG + e
                hc = slice(h * HD, (h + 1) * HD)
                lmat = _decay_mask(cs_sc, cst_sc, h)
                m = gmat * lmat
                dx_sc[:, hc] += _dot(m.astype(BF16), dy_sc[:, hc], TN)
                dm = _dot(dy_sc[:, hc], x_sc[:, hc], NT)
                dg = dg + dm * lmat
                qm = dm * m
                dcol_sc[...] += jnp.sum(qm, axis=1, keepdims=True) * (lane_row == h).astype(F32)
                drow_sc[...] += (sub_col == h).astype(F32) * jnp.sum(qm, axis=0, keepdims=True)
            dg_b = dg.astype(BF16)
            dbc_ref[:, g * 128:(g + 1) * 128] = db + _dot(dg_b, c_g, TN)
            dbc_ref[:, 256 + g * 128:256 + (g + 1) * 128] = dc + _dot(dg_b, b_g)
        d_eo = _sel_right(deo_sc[...], reduce)
        d_we = _sel_right(dwe_sc[...], reduce)
        d_gend = _sel_right(jnp.broadcast_to(_colsum(dnext * prev), (8, D)), reduce)[0:1]
        d_cs = dcol_sc[...] - drow_sc[...].T + d_eo * eo - d_we * we
        extra = _colsum(d_we * we) + d_gend * g_end
        d_cs = d_cs + jnp.where(_iota((t, 128), 0) == t - 1, extra, 0.0)
        da = _sel_left(_tri_upper(t), d_cs)
        dx = dx_sc[...]
        ddt = _sel_right(dx * xs, reduce) + da * a_neg
        dxs_ref[...] = dx * dt_x + dsk_ref[...] * dyv
        ddt_ref[...] = jnp.where(_iota((t, 128), 1) < NH, ddt * _sigmoid(dtf + bias_ref[...]), 0.0)
        dalog_ref[...] += _colsum(da * dt) * a_neg
        dskip_ref[...] += _sel_right(jnp.broadcast_to(_colsum(dyv * xs), (8, D)), reduce)[0:1]

    rev = lambda i: nc - 1 - i
    return pl.pallas_call(
        body, name="ssd_bwd", grid=(nc,),
        in_specs=[pl.BlockSpec((t, D), lambda i: (rev(i), 0)),
                  pl.BlockSpec((t, D), lambda i: (rev(i), 0)),
                  pl.BlockSpec((t, 512), lambda i: (rev(i), 0)),
                  pl.BlockSpec((t, 128), lambda i: (rev(i), OFF_DTF // 128)),
                  pl.BlockSpec((1, NSTATE, D), lambda i: (rev(i), 0, 0)),
                  pl.BlockSpec((1, 128), lambda i: (0, 0)),
                  pl.BlockSpec((1, 128), lambda i: (0, 0)),
                  pl.BlockSpec((1, D), lambda i: (0, 0))],
        out_specs=[pl.BlockSpec((t, D), lambda i: (rev(i), 0)),
                   pl.BlockSpec((t, 512), lambda i: (rev(i), 0)),
                   pl.BlockSpec((t, 128), lambda i: (rev(i), 0)),
                   pl.BlockSpec((1, 128), lambda i: (0, 0)),
                   pl.BlockSpec((1, 128), lambda i: (0, 0))],
        out_shape=[jax.ShapeDtypeStruct((s, D), F32), jax.ShapeDtypeStruct((s, 512), F32),
                   jax.ShapeDtypeStruct((s, 128), F32), jax.ShapeDtypeStruct((1, 128), F32),
                   jax.ShapeDtypeStruct((1, 128), F32)],
        scratch_shapes=[pltpu.VMEM((NSTATE, D), F32), pltpu.VMEM((t, D), BF16), pltpu.VMEM((t, D), BF16),
                        pltpu.VMEM((t, D), F32), pltpu.VMEM((t, D), F32), pltpu.VMEM((t, D), F32),
                        pltpu.VMEM((t, 128), F32), pltpu.VMEM((128, t), F32),
                        pltpu.VMEM((t, 128), F32), pltpu.VMEM((128, t), F32)],
        compiler_params=_params(("arbitrary",)),
    )(dy, xs_a, bc_a, p, states, bias128, alog128, dskip_x)


def _gate_lanes(shape):
    lane = _iota(shape, 1)
    return (lane >= NH) & (lane < 2 * NH)


def _cum_fwd(p, bias128, s):
    tr = min(512, s)

    def body(dtf_ref, bias_ref, o_ref, carry):
        @pl.when(pl.program_id(0) == 0)
        def _():
            carry[...] = jnp.zeros(carry.shape, F32)

        lf = jnp.where(_gate_lanes((tr, 128)), _log_sigmoid(dtf_ref[...] + bias_ref[...]), 0.0)
        cum = _sel_left(_tri_lower(tr), lf) + carry[...]
        carry[...] = cum[tr - 1:tr, :]
        o_ref[...] = cum.T[NH:2 * NH, :]

    return pl.pallas_call(
        body, name="cum_fwd", grid=(s // tr,),
        in_specs=[pl.BlockSpec((tr, 128), lambda i: (i, OFF_DTF // 128)), pl.BlockSpec((1, 128), lambda i: (0, 0))],
        out_specs=pl.BlockSpec((NH, tr), lambda i: (0, i)),
        out_shape=jax.ShapeDtypeStruct((NH, s), F32),
        scratch_shapes=[pltpu.VMEM((1, 128), F32)],
        compiler_params=_params(("arbitrary",)),
    )(p, bias128)


def _cum_bwd(dcum, ddt_raw, p, bias128, s):
    tr = min(512, s)

    def fn(pos, dcum, ddt, dtf, bias, carry, acc):
        suffix = _sel_left(_tri_upper(tr), dcum) + carry
        dfr = jnp.where(_gate_lanes((tr, 128)), suffix * _sigmoid(-(dtf + bias)), 0.0)
        out = ddt + dfr
        return out, suffix[0:1, :], acc + _colsum(out)

    return _rowk("cum_bwd", fn, s, tr, [(dcum, 128, 0, 0), (ddt_raw, 128, 0, 0), (p, 128, OFF_DTF // 128, 0)],
                 [bias128], [(128, BF16)], [(1, 128), (1, 128)], reverse=True)


def _attn_masked_logits(q_h, k_h, crow, qi, ki, tq, tk):
    row = qi * tq + _iota((tq, tk), 0)
    col = ki * tk + _iota((tq, tk), 1)
    return jnp.where(col <= row, _dot(q_h, k_h, NT) - crow, -1e30)


def _attn_fwd(p, cum_t, s):
    tq = tk = min(512, s)
    nq = s // tq

    def body(q_ref, k_ref, v_ref, c_ref, o_ref, lse_ref, m_sc, l_sc, acc_sc):
        j, qi, ki = pl.program_id(0), pl.program_id(1), pl.program_id(2)

        @pl.when(ki == 0)
        def _():
            m_sc[...] = jnp.full(m_sc.shape, -1e30, F32)
            l_sc[...] = jnp.zeros(l_sc.shape, F32)
            acc_sc[...] = jnp.zeros(acc_sc.shape, F32)

        @pl.when(ki <= qi)
        def _():
            q = (q_ref[...] * ATT_SCALE).astype(BF16)
            k = k_ref[...].astype(BF16)
            v = v_ref[...].astype(BF16)
            for hh in range(2):
                hc = slice(hh * HD, (hh + 1) * HD)
                crow = c_ref[pl.ds(2 * j + hh, 1), :]
                sc = _attn_masked_logits(q[:, hc], k[:, hc], crow, qi, ki, tq, tk)
                m_prev = m_sc[hh]
                m_new = jnp.maximum(m_prev, jnp.max(sc, axis=1, keepdims=True))
                pr = jnp.exp(sc - m_new[:, 0:1])
                alpha = jnp.exp(m_prev - m_new)
                l_sc[hh] = alpha * l_sc[hh] + jnp.sum(pr, axis=1, keepdims=True)
                m_sc[hh] = m_new
                acc_sc[:, hc] = alpha[:, :HD] * acc_sc[:, hc] + _dot(pr.astype(BF16), v[:, hc])

        @pl.when(ki == qi)
        def _():
            for hh in range(2):
                hc = slice(hh * HD, (hh + 1) * HD)
                l = l_sc[hh]
                o_ref[:, hc] = acc_sc[:, hc] / l[:, :HD]
                lse_ref[:, hc] = (m_sc[hh] + jnp.log(l))[:, :HD]

    def kv_map(j, qi, ki, off):
        return (jnp.minimum(ki, qi), off + j)

    return pl.pallas_call(
        body, name="attn_fwd", grid=(NH // 2, nq, nq),
        in_specs=[pl.BlockSpec((tq, 128), lambda j, qi, ki: (qi, OFF_Q // 128 + j)),
                  pl.BlockSpec((tk, 128), functools.partial(kv_map, off=OFF_K // 128)),
                  pl.BlockSpec((tk, 128), functools.partial(kv_map, off=OFF_V // 128)),
                  pl.BlockSpec((NH, tk), lambda j, qi, ki: (0, jnp.minimum(ki, qi)))],
        out_specs=[pl.BlockSpec((tq, 128), lambda j, qi, ki: (qi, j)),
                   pl.BlockSpec((tq, 128), lambda j, qi, ki: (qi, j))],
        out_shape=[jax.ShapeDtypeStruct((s, D), F32), jax.ShapeDtypeStruct((s, D), F32)],
        scratch_shapes=[pltpu.VMEM((2, tq, 128), F32), pltpu.VMEM((2, tq, 128), F32), pltpu.VMEM((tq, 128), F32)],
        compiler_params=_params(("parallel", "parallel", "arbitrary")),
    )(p, p, p, cum_t)


def _attn_bwd(p, cum_t, o, lse, do, s):
    tq = tk = min(512, s)
    nq = s // tq

    def body(q_ref, k_ref, v_ref, c_ref, o_ref, lse_ref, do_ref, dq_ref, dk_ref, dv_ref, dc_ref, dr_ref, dk_sc, dv_sc, dc_sc):
        j, ki, qi = pl.program_id(0), pl.program_id(1), pl.program_id(2)

        @pl.when(qi == ki)
        def _():
            dk_sc[...] = jnp.zeros(dk_sc.shape, F32)
            dv_sc[...] = jnp.zeros(dv_sc.shape, F32)
            dc_sc[...] = jnp.zeros(dc_sc.shape, F32)

        @pl.when(qi >= ki)
        def _():
            q = (q_ref[...] * ATT_SCALE).astype(BF16)
            k = k_ref[...].astype(BF16)
            v = v_ref[...].astype(BF16)
            dov, ov, lse_v = do_ref[...], o_ref[...], lse_ref[...]
            rows = pl.ds(pl.multiple_of(qi * tq, tq), tq)
            for hh in range(2):
                hc = slice(hh * HD, (hh + 1) * HD)
                crow = c_ref[pl.ds(2 * j + hh, 1), :]
                sc = _attn_masked_logits(q[:, hc], k[:, hc], crow, qi, ki, tq, tk)
                pr = jnp.exp(sc - lse_v[:, hh * HD:hh * HD + 1])
                do_h = dov[:, hc]
                delta = jnp.sum(do_h * ov[:, hc], axis=1, keepdims=True)
                do_b = do_h.astype(BF16)
                dv_sc[:, hc] += _dot(pr.astype(BF16), do_b, TN)
                ds = pr * (_dot(do_b, v[:, hc], NT) - delta)
                dc_sc[hh:hh + 1, :] += _colsum(ds)
                ds_b = ds.astype(BF16)
                dk_sc[:, hc] += _dot(ds_b, q[:, hc], TN)
                dq_h = _dot(ds_b, k[:, hc]) * ATT_SCALE
                drow = jnp.broadcast_to(jnp.sum(ds, axis=1, keepdims=True), (tq, HD))

                @pl.when(ki == 0)
                def _():
                    dq_ref[rows, hc] = dq_h
                    dr_ref[rows, hc] = drow

                @pl.when(ki > 0)
                def _():
                    dq_ref[rows, hc] += dq_h
                    dr_ref[rows, hc] += drow

        @pl.when(qi == nq - 1)
        def _():
            dk_ref[...] = dk_sc[...].astype(BF16)
            dv_ref[...] = dv_sc[...].astype(BF16)
            dc_ref[0] = dc_sc[...]

    def q_map(j, ki, qi, off):
        return (jnp.maximum(qi, ki), off + j)

    return pl.pallas_call(
        body, name="attn_bwd", grid=(NH // 2, nq, nq),
        in_specs=[pl.BlockSpec((tq, 128), functools.partial(q_map, off=OFF_Q // 128)),
                  pl.BlockSpec((tk, 128), lambda j, ki, qi: (ki, OFF_K // 128 + j)),
                  pl.BlockSpec((tk, 128), lambda j, ki, qi: (ki, OFF_V // 128 + j)),
                  pl.BlockSpec((NH, tk), lambda j, ki, qi: (0, ki)),
                  pl.BlockSpec((tq, 128), functools.partial(q_map, off=0)),
                  pl.BlockSpec((tq, 128), functools.partial(q_map, off=0)),
                  pl.BlockSpec((tq, 128), functools.partial(q_map, off=0))],
        out_specs=[pl.BlockSpec((s, 128), lambda j, ki, qi: (0, j)),
                   pl.BlockSpec((tk, 128), lambda j, ki, qi: (ki, j)),
                   pl.BlockSpec((tk, 128), lambda j, ki, qi: (ki, j)),
                   pl.BlockSpec((1, 8, tk), lambda j, ki, qi: (j, 0, ki)),
                   pl.BlockSpec((s, 128), lambda j, ki, qi: (0, j))],
        out_shape=[jax.ShapeDtypeStruct((s, D), F32), jax.ShapeDtypeStruct((s, D), BF16),
                   jax.ShapeDtypeStruct((s, D), BF16), jax.ShapeDtypeStruct((NH // 2, 8, s), F32),
                   jax.ShapeDtypeStruct((s, D), F32)],
        scratch_shapes=[pltpu.VMEM((tk, 128), F32), pltpu.VMEM((tk, 128), F32), pltpu.VMEM((8, tk), F32)],
        compiler_params=_params(("parallel", "arbitrary", "arbitrary")),
    )(p, p, p, cum_t, o, lse, do)


def _ln_stats(u):
    mu = _mean(u)
    d = u - mu
    rstd = lax.rsqrt(_mean(d * d) + EPS)
    return d * rstd, rstd


def _ln_bwd(dx, xh, rstd, gam):
    dxh = dx * gam
    return rstd * (dxh - _mean(dxh) - xh * _mean(dxh * xh))


def _rms_bwd(d, xn, r, w):
    t = d * w
    return r * (t - xn * _mean(t * xn)), _colsum(d * xn)


def _mix_norm(y, p, att, w_ssm, w_att, s):
    def fn(pos, y, z, att, w1, w2):
        g = y * _silu(z)
        n1 = g * lax.rsqrt(_mean(g * g) + EPS) * w1
        n2 = att * lax.rsqrt(_mean(att * att) + EPS) * w2
        return (jnp.concatenate([n1, n2], axis=1),)

    return _rowk("mix_norm", fn, s, 256, [(y, D, 0, 0), (p, D, OFF_Z // D, 0), (att, D, 0, 0)],
                 [w_ssm, w_att], [(2 * D, BF16)], [])[0]


def _mix_norm_bwd(dmix, y, p, att, w_ssm, w_att, s):
    def fn(pos, dmix, y, z, att, w1, w2, a1, a2):
        sz = _silu(z)
        g = y * sz
        r1 = lax.rsqrt(_mean(g * g) + EPS)
        dg, dw1 = _rms_bwd(dmix[:, :D], g * r1, r1, w1)
        r2 = lax.rsqrt(_mean(att * att) + EPS)
        datt, dw2 = _rms_bwd(dmix[:, D:], att * r2, r2, w2)
        return dg * sz, dg * y * _dsilu(z), datt, a1 + dw1, a2 + dw2

    return _rowk("mix_norm_bwd", fn, s, 256, [(dmix, 2 * D, 0, 0), (y, D, 0, 0), (p, D, OFF_Z // D, 0), (att, D, 0, 0)],
                 [w_ssm, w_att], [(D, F32), (D, BF16), (D, F32)], [(1, D), (1, D)])


def _ln1(x0, y, g1, gam, bet, sc2, sh2, s):
    def fn(pos, x0, y, g1, gam, bet, sc2, sh2):
        xh, _ = _ln_stats(ALPHA * x0 + (1.0 + g1) * y)
        x1 = xh * gam + bet
        return x1, _modulate(x1, sc2, sh2)

    return _rowk("ln1", fn, s, 256, [(x0, D, 0, 0), (y, D, 0, 0)], [g1, gam, bet, sc2, sh2], [(D, F32), (D, BF16)], [])


def _ln2_loss(x1, ff, tgt, g2, gam, bet, s):
    def fn(pos, x1, ff, tgt, g2, gam, bet, a_loss, a_dgam, a_dbet, a_dg2):
        xh, rstd = _ln_stats(ALPHA * x1 + (1.0 + g2) * ff)
        err = xh * gam + bet - tgt
        dx2 = err * (1.0 / D)
        du = _ln_bwd(dx2, xh, rstd, gam)
        return (du, du * (1.0 + g2), a_loss + _colsum(err * err), a_dgam + _colsum(dx2 * xh),
                a_dbet + _colsum(dx2), a_dg2 + _colsum(du * ff))

    return _rowk("ln2_loss", fn, s, 256, [(x1, D, 0, 0), (ff, D, 0, 0), (tgt, D, 0, 0)], [g2, gam, bet],
                 [(D, F32), (D, BF16)], [(1, D)] * 4)


def _ln1_bwd(dh2, du2, x0, y, g1, gam, bet, sc2, s):
    def fn(pos, dh2, du2, x0, y, g1, gam, bet, sc2, a_sc, a_sh, a_gam, a_bet, a_g1):
        xh, rstd = _ln_stats(ALPHA * x0 + (1.0 + g1) * y)
        x1 = xh * gam + bet
        dx1 = ALPHA * du2 + dh2 * (1.0 + sc2)
        du1 = _ln_bwd(dx1, xh, rstd, gam)
        return (du1, du1 * (1.0 + g1), a_sc + _colsum(dh2 * x1), a_sh + _colsum(dh2), a_gam + _colsum(dx1 * xh),
                a_bet + _colsum(dx1), a_g1 + _colsum(du1 * y))

    return _rowk("ln1_bwd", fn, s, 256, [(dh2, D, 0, 0), (du2, D, 0, 0), (x0, D, 0, 0), (y, D, 0, 0)],
                 [g1, gam, bet, sc2], [(D, F32), (D, BF16)], [(1, D)] * 5)


def _input_grad(dh1, du1, x0, sc1, s):
    def fn(pos, dh1, du1, x0, sc1, a_sc, a_sh):
        return ALPHA * du1 + dh1 * (1.0 + sc1), a_sc + _colsum(dh1 * x0), a_sh + _colsum(dh1)

    return _rowk("input_grad", fn, s, 256, [(dh1, D, 0, 0), (du1, D, 0, 0), (x0, D, 0, 0)], [sc1],
                 [(D, F32)], [(1, D)] * 2)


def _adamw(name, w, g, m, v, *, tr, slots):
    r, c = w.shape

    def body(w_ref, g_ref, m_ref, v_ref, g_out, d_out, m_out, v_out):
        if slots:
            grad = g_ref[0].astype(F32)
            for k in range(1, N_DEV):
                grad = grad + g_ref[k].astype(F32)
        else:
            grad = g_ref[...]
        m_new = ADAM_B1 * m_ref[...] + (1.0 - ADAM_B1) * grad
        v_new = ADAM_B2 * v_ref[...] + (1.0 - ADAM_B2) * (grad * grad)
        m_hat = m_new / (1.0 - ADAM_B1 ** ADAM_STEP)
        v_hat = v_new / (1.0 - ADAM_B2 ** ADAM_STEP)
        g_out[...] = grad
        d_out[...] = -ADAM_LR * (m_hat / (jnp.sqrt(v_hat) + ADAM_EPS) + ADAM_WD * w_ref[...])
        m_out[...] = m_new
        v_out[...] = v_new

    tile = pl.BlockSpec((tr, c), lambda i: (i, 0))
    g_spec = pl.BlockSpec((N_DEV, tr, c), lambda i: (0, i, 0)) if slots else tile
    return pl.pallas_call(
        body, name=name, grid=(r // tr,),
        in_specs=[tile, g_spec, tile, tile], out_specs=[tile] * 4,
        out_shape=[jax.ShapeDtypeStruct((r, c), F32)] * 4,
        compiler_params=_params(("parallel",)),
    )(w, g, m, v)


def _dot_f32(a, b, dims=NN):
    a0, a1, a2 = _split3(a)
    b0, b1, b2 = _split3(b)
    acc = _dot(a0, b0, dims)
    for x, y in ((a0, b1), (a1, b0), (a1, b1), (a0, b2), (a2, b0)):
        acc = acc + _dot(x, y, dims)
    return acc


def _ada_mod(c_all, w_shard, b_shard):
    def body(c_ref, w_ref, b_ref, o_ref):
        act = _silu(c_ref[...])
        act16 = jnp.concatenate([act, jnp.zeros_like(act)], axis=0)
        o_ref[...] = _dot_f32(act16, w_ref[...])[0:N_DEV] + b_ref[...]

    return pl.pallas_call(
        body, name="ada_mod", out_shape=jax.ShapeDtypeStruct((N_DEV, w_shard.shape[1]), F32),
        compiler_params=_params(None),
    )(c_all, w_shard, b_shard)


def _ada_grad(c_all, dmod_cols, dmod_all):
    def body(c_ref, dc_ref, da_ref, gw_ref, gb_ref):
        act = _silu(c_ref[...])
        act16 = jnp.concatenate([act, jnp.zeros_like(act)], axis=0)
        dm = dc_ref[...]
        dm16 = jnp.concatenate([dm, jnp.zeros_like(dm)], axis=0)
        gw_ref[...] = _dot_f32(act16, dm16, TN)
        gb_ref[...] = _colsum(da_ref[...])

    return pl.pallas_call(
        body, name="ada_grad",
        out_shape=[jax.ShapeDtypeStruct((D, dmod_cols.shape[1]), F32), jax.ShapeDtypeStruct((1, 6 * D), F32)],
        compiler_params=_params(None),
    )(c_all, dmod_cols, dmod_all)


def _sum_slots(name, g):
    def body(g_ref, o_ref):
        acc = g_ref[0]
        for k in range(1, N_DEV):
            acc = acc + g_ref[k]
        o_ref[...] = acc

    return pl.pallas_call(body, name=name, out_shape=jax.ShapeDtypeStruct(g.shape[1:], F32),
                          compiler_params=_params(None))(g)


def _exchange(name, xs, scatter):
    n = len(xs)
    n_peer = N_DEV - 1

    def body(*refs):
        x_refs, o_refs = refs[:n], refs[n:2 * n]
        send_sems, recv_sems, local_sems = refs[2 * n:]
        mx, my, mc = lax.axis_index("x"), lax.axis_index("y"), lax.axis_index("c")
        me = 4 * mx + 2 * my + mc

        def src(a, slot):
            return x_refs[a].at[slot] if scatter else x_refs[a]

        own = [pltpu.make_async_copy(src(a, me), o_refs[a].at[me], local_sems.at[a]) for a in range(n)]
        for cp in own:
            cp.start()
        sends = []
        for d in range(1, N_DEV):
            px = 1 - mx if d & 4 else mx
            py = 1 - my if d & 2 else my
            pc = 1 - mc if d & 1 else mc
            peer = 4 * px + 2 * py + pc
            for a in range(n):
                def copy(src_slot, dst_slot, a=a, d=d, to=(px, py, pc)):
                    return pltpu.make_async_remote_copy(
                        src_ref=src(a, src_slot), dst_ref=o_refs[a].at[dst_slot],
                        send_sem=send_sems.at[a * n_peer + d - 1], recv_sem=recv_sems.at[a * n_peer + d - 1],
                        device_id=to, device_id_type=pl.DeviceIdType.MESH)

                out = copy(peer, me)
                out.start()
                sends.append((out, copy(me, peer)))
        for _, arrival in sends:
            arrival.wait_recv()
        for out, _ in sends:
            out.wait_send()
        for cp in own:
            cp.wait()

    shapes = [tuple(x.shape[1:] if scatter else x.shape) for x in xs]
    return pl.pallas_call(
        body, name=name,
        in_specs=[pl.BlockSpec(memory_space=pl.ANY)] * n, out_specs=[pl.BlockSpec(memory_space=pl.ANY)] * n,
        out_shape=[jax.ShapeDtypeStruct((N_DEV,) + sh, x.dtype) for sh, x in zip(shapes, xs)],
        scratch_shapes=[pltpu.SemaphoreType.DMA((n * n_peer,)), pltpu.SemaphoreType.DMA((n * n_peer,)),
                        pltpu.SemaphoreType.DMA((n,))],
        compiler_params=pltpu.CompilerParams(has_side_effects=True),
    )(*xs)


def _after(x, zero):
    return x + zero.reshape(-1)[0].astype(x.dtype)


_HBM = pl.BlockSpec(memory_space=pltpu.HBM)
_SEM = pl.BlockSpec(memory_space=pltpu.SEMAPHORE)


def _exchange_copies(x_refs, land_refs, send_sems, recv_sems, scatter):
    n = len(x_refs)
    n_peer = N_DEV - 1
    mx, my, mc = lax.axis_index("x"), lax.axis_index("y"), lax.axis_index("c")
    me = 4 * mx + 2 * my + mc
    pairs = []
    for d in range(1, N_DEV):
        px = 1 - mx if d & 4 else mx
        py = 1 - my if d & 2 else my
        pc = 1 - mc if d & 1 else mc
        peer = 4 * px + 2 * py + pc
        for a in range(n):
            def copy(src_slot, dst_slot, a=a, d=d, to=(px, py, pc)):
                return pltpu.make_async_remote_copy(
                    src_ref=x_refs[a].at[src_slot] if scatter else x_refs[a], dst_ref=land_refs[a].at[dst_slot],
                    send_sem=send_sems.at[a * n_peer + d - 1], recv_sem=recv_sems.at[a * n_peer + d - 1],
                    device_id=to, device_id_type=pl.DeviceIdType.MESH)

            pairs.append((copy(peer, me), copy(me, peer)))
    return me, pairs


def _exchange_start(name, xs, scatter):
    n = len(xs)
    shapes = [tuple(x.shape[1:] if scatter else x.shape) for x in xs]

    def body(*refs):
        x_refs, land_refs = refs[:n], refs[n:2 * n]
        send_sems, recv_sems = refs[2 * n], refs[2 * n + 1]
        token, own_sems = refs[4 * n + 2], refs[4 * n + 3]
        me, pairs = _exchange_copies(x_refs, land_refs, send_sems, recv_sems, scatter)
        own = [pltpu.make_async_copy(x_refs[a].at[me] if scatter else x_refs[a], land_refs[a].at[me], own_sems.at[a])
               for a in range(n)]
        for cp in own:
            cp.start()
        for out, _ in pairs:
            out.start()
        for cp in own:
            cp.wait()
        token[...] = jnp.zeros(token.shape, token.dtype)

    lands = [pltpu.with_memory_space_constraint(lax.empty((N_DEV,) + sh, x.dtype), pltpu.HBM) for sh, x in zip(shapes, xs)]
    res = pl.pallas_call(
        body, name=name,
        out_shape=(pltpu.SemaphoreType.DMA((n * (N_DEV - 1),)), pltpu.SemaphoreType.DMA((n * (N_DEV - 1),)),
                   *[pltpu.HBM(x.shape, x.dtype) for x in xs], *[pltpu.HBM(l.shape, l.dtype) for l in lands],
                   jax.ShapeDtypeStruct((8, 128), F32)),
        in_specs=[_HBM] * (2 * n),
        out_specs=(_SEM, _SEM, *[_HBM] * (2 * n), pl.BlockSpec(memory_space=pltpu.VMEM)),
        input_output_aliases={i: 2 + i for i in range(2 * n)},
        scratch_shapes=[pltpu.SemaphoreType.DMA((n,))],
        compiler_params=pltpu.CompilerParams(has_side_effects=pltpu.SideEffectType.DATAFLOW_SIDE_EFFECTING),
    )(*[pltpu.with_memory_space_constraint(x, pltpu.HBM) for x in xs], *lands)
    return dict(send=res[0], recv=res[1], xs=list(res[2:2 + n]), lands=list(res[2 + n:2 + 2 * n]), token=res[2 + 2 * n])


def _exchange_wait(name, handle, after, scatter):
    n = len(handle['xs'])

    def body(*refs):
        x_refs, land_refs = refs[:n], refs[n:2 * n]
        send_sems, recv_sems = refs[2 * n], refs[2 * n + 1]
        _, pairs = _exchange_copies(x_refs, land_refs, send_sems, recv_sems, scatter)
        for out, arrival in pairs:
            out.wait_send()
            arrival.wait_recv()

    res = pl.pallas_call(
        body, name=name,
        out_shape=tuple(pltpu.HBM(a.shape, a.dtype) for a in handle['xs'] + handle['lands']),
        in_specs=[_HBM] * (2 * n) + [_SEM, _SEM, pl.BlockSpec(memory_space=pl.ANY)],
        out_specs=tuple([_HBM] * (2 * n)),
        input_output_aliases={i: i for i in range(2 * n)},
        compiler_params=pltpu.CompilerParams(has_side_effects=pltpu.SideEffectType.DATAFLOW_SIDE_EFFECTING),
    )(*handle['xs'], *handle['lands'], handle['send'], handle['recv'], after)
    return list(res[n:])


def _relu2(a):
    r = jnp.maximum(a, 0.0)
    return r * r


def _relu2_grad(acc, a):
    return acc * (2.0 * jnp.maximum(a, 0.0))


def _local_step(x0, tgt, mod, wcat, late_weights, send_grads, conv_w, conv_b, dt_bias, a_log, d_skip, ssm_norm_w, f_bias,
                attn_norm_w, ln1_g, ln1_b, ln2_g, ln2_b):
    ff_w = DFF // N_DEV
    s = x0.shape[0]
    tm = min(512, s)
    ts = min(1024, s)
    sh1, sc1, g1, sh2, sc2, g2 = [mod[:, i * D:(i + 1) * D] for i in range(6)]
    zero = jnp.zeros((1, 128 - 2 * NH), F32)
    bias128 = jnp.concatenate([dt_bias, f_bias, zero], axis=1)
    alog128 = jnp.concatenate([a_log, jnp.zeros((1, 128 - NH), F32)], axis=1)
    dskip_x = jnp.repeat(d_skip, HD, axis=1)
    w_xs, w_bc, b_xs, b_bc = conv_w[:, :D], conv_w[:, D:], conv_b[:, :D], conv_b[:, D:]

    p = _mm_nn("in_proj", x0, wcat, tm=tm, tn=640, tk=D, out_dtype=F32, pro=_modulate, aux=(sc1, sh1))
    xs_a, bc_a = _conv_fwd(p, w_xs, b_xs, w_bc, b_bc, s)
    y_ssd, states = _ssd_fwd(xs_a, bc_a, p, bias128, alog128, dskip_x, s)
    cum_t = _cum_fwd(p, bias128, s)
    att, lse = _attn_fwd(p, cum_t, s)
    wout, w1s, w2 = late_weights(lse)
    ymix = _mix_norm(y_ssd, p, att, ssm_norm_w, attn_norm_w, s)
    y = _mm_nn("out_proj", ymix, wout, tm=tm, tn=512, tk=2 * D, out_dtype=F32)
    x1, h2 = _ln1(x0, y, g1, ln1_g, ln1_b, sc2, sh2, s)
    a1 = _mm_nn("ff_in", h2, w1s, tm=tm, tn=ff_w, tk=D, out_dtype=F32)
    ff = _mm_nn("ff_out", a1, w2, tm=tm, tn=512, tk=1024, out_dtype=F32, pro=_relu2)
    du2, dff, sq_err, d_ln2_g, d_ln2_b, d_g2 = _ln2_loss(x1, ff, tgt, g2, ln2_g, ln2_b, s)

    da1 = _mm_nt("d_ff_hidden", [(dff, D, 0)], [(w2, D, 0)], n=DFF, tm=tm, tn=512, out_dtype=BF16, epi=_relu2_grad,
                 epi_aux=(a1,))
    d_w2 = _mm_tn("d_w_ff_out", a1, dff, tm=1024, tn=512, ts=ts, pro=_relu2)
    d_w1s = _mm_tn("d_w_ff_in", h2, da1, tm=1024, tn=ff_w, ts=ts, col_shards=True)
    dh2 = _mm_nt("d_ff_input", [(da1, ff_w, k) for k in range(N_DEV)], [(w1s, ff_w, k) for k in range(N_DEV)], n=D,
                 tm=min(256, s), tn=512, out_dtype=F32)
    sent = send_grads("ff", [d_w1s, d_w2.reshape(N_DEV, -1, D)])
    du1, dy, d_sc2, d_sh2, d_ln1_g, d_ln1_b, d_g1 = _ln1_bwd(dh2, du2, x0, y, g1, ln1_g, ln1_b, _after(sc2, sent), s)

    dmix = _mm_nt("d_mix", [(dy, D, 0)], [(wout, D, 0)], n=2 * D, tm=tm, tn=512, out_dtype=F32)
    d_wout = _mm_tn("d_w_out", ymix, dy, tm=1024, tn=512, ts=ts)
    sent = send_grads("out", [d_wout.reshape(N_DEV, -1, D)])
    dy_ssd, dz, datt, d_ssm_w, d_attn_w = _mix_norm_bwd(dmix, y_ssd, p, att, _after(ssm_norm_w, sent), attn_norm_w, s)
    dq, dk, dv, dcs, drs = _attn_bwd(p, cum_t, att, lse, datt, s)
    dxs_a, dbc_a, ddt_raw, d_alog, d_dskip = _ssd_bwd(dy_ssd, xs_a, bc_a, p, states, bias128, alog128, dskip_x, s)
    dcum = jnp.pad(drs[:, ::HD] - dcs[:, :2, :].reshape(NH, s).T, ((0, 0), (NH, 128 - 2 * NH)))
    ddtf, _, d_bias = _cum_bwd(dcum, ddt_raw, p, bias128, s)
    dxs, dbc, d_wc_xs, d_bc_xs, d_wc_bc, d_bc_bc = _conv_bwd(dxs_a, dbc_a, p, w_xs, b_xs, w_bc, b_bc, s)

    segs = [(dz, OFF_Z, D), (dxs, OFF_XS, D), (dq, OFF_Q, D), (dk, OFF_K, D), (dv, OFF_V, D), (dbc, OFF_BC, 512),
            (ddtf, OFF_DTF, 128)]
    d_z, d_xs, d_q, d_k, d_v, d_bcw, d_dtf = [
        _mm_tn("d_w_in_%d" % i, x0, a, tm=1024, tn=min(w, 512), ts=ts, pro=_modulate, aux=(sc1, sh1))
        for i, (a, _, w) in enumerate(segs)]
    d_w_in = dict(z=d_z, xs=d_xs, bc=d_bcw, dt=d_dtf[:, :NH], q=d_q, k=d_k, v=d_v, f=d_dtf[:, NH:2 * NH])
    sent = send_grads("in", [_shard_w_in_grad(d_w_in)])
    segs[-1] = (_after(ddtf, sent), OFF_DTF, 128)
    dh1 = _mm_nt("d_h1", [(a, w, 0) for a, _, w in segs], [(wcat, w, off // w) for _, off, w in segs], n=D,
                 tm=min(256, s), tn=512, out_dtype=F32)
    grad_x, d_sc1, d_sh1 = _input_grad(dh1, du1, x0, sc1, s)

    return dict(
        loss=(0.5 / D) * jnp.sum(sq_err), grad_x=grad_x,
        d_mod=jnp.concatenate([d_sh1, d_sc1, d_g1, d_sh2, d_sc2, d_g2], axis=1),
        d_conv_w=jnp.concatenate([d_wc_xs[:4], d_wc_bc[:4]], axis=1), d_conv_b=jnp.concatenate([d_bc_xs, d_bc_bc], axis=1),
        d_ssm_norm_w=d_ssm_w, d_attn_norm_w=d_attn_w, d_ln1_g=d_ln1_g, d_ln1_b=d_ln1_b, d_ln2_g=d_ln2_g, d_ln2_b=d_ln2_b,
        d_gate_bias=d_bias, d_a_log=d_alog, d_d_skip=d_dskip)


W_IN_SEGS = [('z', W_Z, D), ('xs', W_XS, D), ('bc', W_BC, 512), ('dt', W_DT, NH), ('q', W_Q, D), ('k', W_K, D),
             ('v', W_V, D), ('f', W_F, NH)]
SHARD_W = IN_COLS // N_DEV


def _pack_w_in(shards):
    def cols(lo, hi):
        pieces = []
        while lo < hi:
            dev = lo // SHARD_W
            end = min(hi, (dev + 1) * SHARD_W)
            pieces.append(shards[dev][:, lo - dev * SHARD_W:end - dev * SHARD_W])
            lo = end
        return pieces

    seg = {n: cols(off, off + w) for n, off, w in W_IN_SEGS}
    pieces = seg['z'] + seg['xs'] + seg['q'] + seg['k'] + seg['v'] + seg['bc'] + seg['dt'] + seg['f']
    return jnp.concatenate(pieces + [jnp.zeros((D, 128 - 2 * NH), shards.dtype)], axis=1)


def _shard_w_in_grad(d_w_in):
    blocks = []
    for dev in range(N_DEV):
        lo, hi = dev * SHARD_W, (dev + 1) * SHARD_W
        pieces = [d_w_in[n][:, max(lo, off) - off:min(hi, off + w) - off] for n, off, w in W_IN_SEGS
                  if max(lo, off) < min(hi, off + w)]
        blocks.append(jnp.concatenate(pieces, axis=1))
    return jnp.stack(blocks, axis=0)


WEIGHTS = ['w_ada', 'b_ada', 'w_in', 'conv_w', 'conv_b', 'dt_bias', 'a_log', 'd_skip', 'ssm_norm_w', 'f_bias',
           'attn_norm_w', 'w_out', 'ln1_g', 'ln1_b', 'w_ff_in', 'w_ff_out', 'ln2_g', 'ln2_b']
BIG = ['w_in', 'w_out', 'w_ff_in', 'w_ff_out']
SMALL = ['b_ada', 'conv_b', 'ssm_norm_w', 'attn_norm_w', 'ln1_g', 'ln1_b', 'ln2_g', 'ln2_b', 'dt_bias', 'a_log', 'd_skip',
         'f_bias', 'conv_w']


def _pad_lanes(v, n=128):
    return jnp.pad(v, ((0, 0), (0, n - v.shape[1])))


def _small_block(vals):
    rows = [_pad_lanes(vals[n].reshape(1, -1), -(-vals[n].size // 128) * 128).reshape(-1, 128) for n in SMALL]
    block = jnp.concatenate(rows, axis=0)
    return jnp.pad(block, ((0, 120 - block.shape[0]), (0, 0)))


def _small_unblock(block, like):
    out, r = {}, 0
    for n in SMALL:
        size = like[n].size
        nr = -(-size // 128)
        out[n] = block[r:r + nr].reshape(-1)[:size].reshape(like[n].shape)
        r += nr
    return out


def kernel(x, c, w_ada, b_ada, w_in, conv_w, conv_b, dt_bias, a_log, d_skip, ssm_norm_w, f_bias, attn_norm_w, w_out, ln1_g, ln1_b, w_ff_in, w_ff_out, ln2_g, ln2_b, loss_target, m_w_ada, m_b_ada, m_w_in, m_conv_w, m_conv_b, m_dt_bias, m_a_log, m_d_skip, m_ssm_norm_w, m_f_bias, m_attn_norm_w, m_w_out, m_ln1_g, m_ln1_b, m_w_ff_in, m_w_ff_out, m_ln2_g, m_ln2_b, v_w_ada, v_b_ada, v_w_in, v_conv_w, v_conv_b, v_dt_bias, v_a_log, v_d_skip, v_ssm_norm_w, v_f_bias, v_attn_norm_w, v_w_out, v_ln1_g, v_ln1_b, v_w_ff_in, v_w_ff_out, v_ln2_g, v_ln2_b):
    args = dict(locals())
    w = {n: args[n] for n in WEIGHTS}
    m = {n: args['m_' + n] for n in WEIGHTS}
    v = {n: args['v_' + n] for n in WEIGHTS}
    me = 4 * lax.axis_index("x") + 2 * lax.axis_index("y") + lax.axis_index("c")
    ada_cols = 6 * D // N_DEV
    conv_cols = conv_w.shape[2]

    c_all, conv_all = _exchange("gather_cond", [c, conv_w[0]], False)
    c_all = c_all.reshape(N_DEV, D)
    conv_w_full = conv_all.transpose(1, 0, 2).reshape(4, N_DEV * conv_cols)
    b_shard = lax.dynamic_slice(b_ada, (0, me * ada_cols), (1, ada_cols))
    mod_all, = _exchange("gather_mod", [_ada_mod(c_all, w_ada[0], b_shard)], False)
    mod = lax.dynamic_index_in_dim(mod_all, me, axis=1, keepdims=False).reshape(1, 6 * D)

    win_s, = _exchange("gather_w_in", [w_in[0].astype(BF16)], False)
    first_done = win_s[0, 0:1, 0:1] * 0
    rest = _exchange_start("gather_rest_start", [_after(w[n][0].astype(BF16), first_done) for n in BIG[1:]], False)

    def late_weights(after):
        wout_s, w1s, w2_s = _exchange_wait("gather_rest_wait", rest, after, False)
        return wout_s.reshape(2 * D, D), w1s, w2_s.reshape(DFF, D)

    sends = {}

    def send_grads(tag, blocks):
        sends[tag] = _exchange_start("scatter_%s_start" % tag, blocks, True)
        return sends[tag]['token']

    out = _local_step(x[0], loss_target[0], _after(mod, rest['token']), _pack_w_in(win_s), late_weights, send_grads,
                      conv_w_full, conv_b, dt_bias, a_log, d_skip, ssm_norm_w, f_bias, attn_norm_w, ln1_g, ln1_b, ln2_g, ln2_b)
    g_ff_in, g_ff_out = _exchange_wait("scatter_ff_wait", sends['ff'], out['grad_x'], True)
    g_out, = _exchange_wait("scatter_out_wait", sends['out'], out['grad_x'], True)
    g_in, = _exchange_wait("scatter_in_wait", sends['in'], out['grad_x'], True)
    g_parts = [g_in, g_out, g_ff_in, g_ff_out]
    big = {n: _adamw("adamw_" + n, w[n][0], g, m[n][0], v[n][0], tr=256, slots=True) for n, g in zip(BIG, g_parts)}

    small = jnp.concatenate(
        [out['d_mod'], out['d_conv_w'].reshape(1, -1), out['d_conv_b'], out['d_ssm_norm_w'], out['d_attn_norm_w'],
         out['d_ln1_g'], out['d_ln1_b'], out['d_ln2_g'], out['d_ln2_b'], out['d_gate_bias'], out['d_a_log'],
         out['d_d_skip'], jnp.zeros((1, 128), F32)], axis=1).reshape(-1, 128)
    small_all, = _exchange("gather_small", [small], False)
    ssum = _sum_slots("sum_small", small_all)
    dmod_all = small_all[:, :6 * D // 128].reshape(N_DEV, 6 * D)
    g_w_ada, g_b_ada = _ada_grad(c_all, lax.dynamic_slice(dmod_all, (0, me * ada_cols), (N_DEV, ada_cols)), dmod_all)
    rows = lambda a, b: ssum[a:b].reshape(1, -1)
    g_conv_w = lax.dynamic_slice(ssum[48:96].reshape(4, N_DEV * conv_cols), (0, me * conv_cols), (4, conv_cols))
    g_small = dict(b_ada=g_b_ada, conv_w=g_conv_w[None], conv_b=rows(96, 108), ssm_norm_w=rows(108, 116),
                   attn_norm_w=rows(116, 124), ln1_g=rows(124, 132), ln1_b=rows(132, 140), ln2_g=rows(140, 148),
                   ln2_b=rows(148, 156), dt_bias=ssum[156:157, :NH], f_bias=ssum[156:157, NH:2 * NH],
                   a_log=ssum[157:158, :NH], d_skip=ssum[158:159, :NH])
    sm = _adamw("adamw_small", _small_block(w), _small_block(g_small), _small_block(m), _small_block(v), tr=120, slots=False)
    ada = _adamw("adamw_ada", w_ada[0], g_w_ada, m_w_ada[0], v_w_ada[0], tr=256, slots=False)

    results = []
    for k in range(4):
        vals = _small_unblock(sm[k], w)
        vals['w_ada'] = ada[k][None]
        for n in BIG:
            vals[n] = big[n][k][None]
        results.append(vals)
    loss = lax.psum(out['loss'], ("x", "y", "c"))
    return (loss, out['grad_x'][None], *[res[n] for res in results for n in WEIGHTS])
```

```python
import functools

import jax
import jax.numpy as jnp
from jax import lax
from jax.experimental import pallas as pl
from jax.experimental.pallas import tpu as pltpu
from jax.experimental.pallas import tpu_sc as plsc

F32, BF16 = jnp.float32, jnp.bfloat16

N_DEV = 8
D = 1024
NH, HD = 16, 64
NSTATE = 128
CHUNK = 128
HG = 8
DFF = 4096
ALPHA = 2.0 ** 0.25
EPS = 1e-5
ATT_SCALE = HD ** -0.5

OFF_Z, OFF_XS, OFF_Q, OFF_K, OFF_V, OFF_BC, OFF_DTF = 0, 1024, 2048, 3072, 4096, 5120, 5632
PCOLS = 5760
W_Z, W_XS, W_BC, W_DT, W_Q, W_K, W_V, W_F = 0, 1024, 2048, 2560, 2576, 3600, 4624, 5648
IN_COLS = 5664

ADAM_LR, ADAM_B1, ADAM_B2, ADAM_EPS, ADAM_WD, ADAM_STEP = 0.001, 0.9, 0.999, 1e-08, 0.01, 10

VMEM_LIMIT = 56 << 20

NN = (((1,), (0,)), ((), ()))
NT = (((1,), (1,)), ((), ()))
TN = (((0,), (0,)), ((), ()))


def _dot(a, b, dims=NN):
    return lax.dot_general(a, b, dims, preferred_element_type=F32)


def _bdot(a, b, dims=NN):
    return _dot(a.astype(BF16), b.astype(BF16), dims)


def _split3(v):
    parts, rest = [], v
    for _ in range(3):
        p = rest.astype(BF16)
        parts.append(p)
        rest = rest - p.astype(F32)
    return parts


def _sel_left(m01, v):
    return sum(_dot(m01, p) for p in _split3(v))


def _sel_right(v, m01, dims=NN):
    return sum(_dot(p, m01, dims) for p in _split3(v))


def _iota(shape, dim):
    return lax.broadcasted_iota(jnp.int32, shape, dim)


def _tri_lower(n):
    return (_iota((n, n), 1) <= _iota((n, n), 0)).astype(BF16)


def _tri_upper(n):
    return (_iota((n, n), 1) >= _iota((n, n), 0)).astype(BF16)


def _head_expand():
    return (lax.shift_right_logical(_iota((128, D), 1), 6) == _iota((128, D), 0)).astype(BF16)


def _head_reduce():
    return (lax.shift_right_logical(_iota((D, 128), 0), 6) == _iota((D, 128), 1)).astype(BF16)


def _sigmoid(x):
    return 1.0 / (1.0 + jnp.exp(-x))


def _silu(x):
    return x * _sigmoid(x)


def _dsilu(x):
    s = _sigmoid(x)
    return s * (1.0 + x * (1.0 - s))


def _softplus(x):
    return jnp.maximum(x, 0.0) + jnp.log(1.0 + jnp.exp(-jnp.abs(x)))


def _log_sigmoid(x):
    return jnp.minimum(x, 0.0) - jnp.log(1.0 + jnp.exp(-jnp.abs(x)))


def _params(sem):
    return pltpu.CompilerParams(dimension_semantics=sem, vmem_limit_bytes=VMEM_LIMIT)


def _mm_nn(name, a, b, *, tm, tn, tk, out_dtype, pro=None, aux=()):
    m, k_all = a.shape
    b_sharded = b.ndim == 3
    n = b.shape[0] * b.shape[2] if b_sharded else b.shape[1]
    assert not b_sharded or tn == b.shape[2]
    nk = k_all // tk
    n_aux = len(aux)
    b_spec = (pl.BlockSpec((None, tk, tn), lambda i, j, k: (j, k, 0)) if b_sharded
              else pl.BlockSpec((tk, tn), lambda i, j, k: (k, j)))

    def body(a_ref, b_ref, *rest):
        aux_refs, o_ref = rest[:n_aux], rest[n_aux]
        at = a_ref[...]
        if pro is not None:
            at = pro(at, *[r[...] for r in aux_refs])
        part = _bdot(at, b_ref[...])
        if nk == 1:
            o_ref[...] = part.astype(out_dtype)
            return
        acc_ref = rest[n_aux + 1]
        kk = pl.program_id(2)

        @pl.when(kk == 0)
        def _():
            acc_ref[...] = part

        @pl.when(kk > 0)
        def _():
            acc_ref[...] += part

        @pl.when(kk == nk - 1)
        def _():
            o_ref[...] = acc_ref[...].astype(out_dtype)

    return pl.pallas_call(
        body, name=name,
        grid=(m // tm, n // tn, nk),
        in_specs=[pl.BlockSpec((tm, tk), lambda i, j, k: (i, k)), b_spec]
        + [pl.BlockSpec((1, tk), lambda i, j, k: (0, k)) for _ in aux],
        out_specs=pl.BlockSpec((tm, tn), lambda i, j, k: (i, j)),
        out_shape=jax.ShapeDtypeStruct((m, n), out_dtype),
        scratch_shapes=[] if nk == 1 else [pltpu.VMEM((tm, tn), F32)],
        compiler_params=_params(("parallel", "parallel", "arbitrary")),
    )(a, b, *aux)


def _mm_nt(name, a_list, b_list, *, n, tm, tn, out_dtype, epi=None, epi_aux=()):
    m = a_list[0][0].shape[0]
    n_op = len(a_list)
    n_epi = len(epi_aux)

    def body(*refs):
        a_refs, b_refs = refs[:n_op], refs[n_op:2 * n_op]
        e_refs, o_ref = refs[2 * n_op:2 * n_op + n_epi], refs[2 * n_op + n_epi]
        acc = None
        for a_ref, b_ref in zip(a_refs, b_refs):
            part = _bdot(a_ref[...], b_ref[...], NT)
            acc = part if acc is None else acc + part
        if epi is not None:
            acc = epi(acc, *[r[...] for r in e_refs])
        o_ref[...] = acc.astype(out_dtype)

    in_specs = [pl.BlockSpec((tm, w), functools.partial(lambda i, j, cb: (i, cb), cb=cb)) for (_, w, cb) in a_list]
    for (b, w, cb) in b_list:
        if b.ndim == 3:
            in_specs.append(pl.BlockSpec((None, tn, w), functools.partial(lambda i, j, cb: (cb, j, 0), cb=cb)))
        else:
            in_specs.append(pl.BlockSpec((tn, w), functools.partial(lambda i, j, cb: (j, cb), cb=cb)))
    in_specs += [pl.BlockSpec((tm, tn), lambda i, j: (i, j)) for _ in epi_aux]
    return pl.pallas_call(
        body, name=name,
        grid=(m // tm, n // tn),
        in_specs=in_specs,
        out_specs=pl.BlockSpec((tm, tn), lambda i, j: (i, j)),
        out_shape=jax.ShapeDtypeStruct((m, n), out_dtype),
        compiler_params=_params(("parallel", "parallel")),
    )(*[a for (a, _, _) in a_list], *[b for (b, _, _) in b_list], *epi_aux)


def _mm_tn(name, a, b, *, tm, tn, ts, pro=None, aux=(), col_shards=False):
    s_all, ka = a.shape
    nb = b.shape[1]
    n_aux = len(aux)
    ns = s_all // ts
    assert not col_shards or tn == nb // N_DEV

    def body(a_ref, b_ref, *rest):
        aux_refs, o_ref, acc_ref = rest[:n_aux], rest[n_aux], rest[n_aux + 1]
        at = a_ref[...]
        if pro is not None:
            at = pro(at, *[r[...] for r in aux_refs])
        part = _bdot(at, b_ref[...], TN)
        ss = pl.program_id(2)

        @pl.when(ss == 0)
        def _():
            acc_ref[...] = part

        @pl.when(ss > 0)
        def _():
            acc_ref[...] += part

        @pl.when(ss == ns - 1)
        def _():
            o_ref[...] = acc_ref[...].astype(BF16)

    if col_shards:
        out_spec = pl.BlockSpec((None, tm, tn), lambda i, j, s: (j, i, 0))
        out_shape = jax.ShapeDtypeStruct((N_DEV, ka, tn), BF16)
    else:
        out_spec = pl.BlockSpec((tm, tn), lambda i, j, s: (i, j))
        out_shape = jax.ShapeDtypeStruct((ka, nb), BF16)
    return pl.pallas_call(
        body, name=name,
        grid=(ka // tm, nb // tn, ns),
        in_specs=[pl.BlockSpec((ts, tm), lambda i, j, s: (s, i)),
                  pl.BlockSpec((ts, tn), lambda i, j, s: (s, j))]
        + [pl.BlockSpec((1, tm), lambda i, j, s: (0, i)) for _ in aux],
        out_specs=out_spec, out_shape=out_shape,
        scratch_shapes=[pltpu.VMEM((tm, tn), F32)],
        compiler_params=_params(("parallel", "parallel", "arbitrary")),
    )(a, b, *aux)


def _rowk(name, fn, n_rows, tr, rows, fulls, outs, accs, reverse=False):
    n = n_rows // tr
    n_row, n_full, n_out, n_acc = len(rows), len(fulls), len(outs), len(accs)

    def pos(i):
        return (n - 1 - i) if reverse else i

    def body(*refs):
        row_refs = refs[:n_row]
        full_refs = refs[n_row:n_row + n_full]
        out_refs = refs[n_row + n_full:n_row + n_full + n_out]
        acc_refs = refs[n_row + n_full + n_out:]
        i = pl.program_id(0)

        @pl.when(i == 0)
        def _():
            for r in acc_refs:
                r[...] = jnp.zeros(r.shape, r.dtype)

        res = fn(pos(i), *[r[...] for r in row_refs], *[r[...] for r in full_refs], *[r[...] for r in acc_refs])
        for r, v in zip(out_refs + acc_refs, res):
            r[...] = v.astype(r.dtype)

    def row_map(i, cb, shift):
        return (jnp.clip(pos(i) + shift, 0, n - 1), cb)

    in_specs = [pl.BlockSpec((tr, w), functools.partial(row_map, cb=cb, shift=sh)) for (_, w, cb, sh) in rows]
    in_specs += [pl.BlockSpec(f.shape, functools.partial(lambda i, nd: (0,) * nd, nd=f.ndim)) for f in fulls]
    out_specs = [pl.BlockSpec((tr, w), lambda i: (pos(i), 0)) for (w, _) in outs]
    out_specs += [pl.BlockSpec((r, w), lambda i: (0, 0)) for (r, w) in accs]
    out_shape = [jax.ShapeDtypeStruct((n_rows, w), dt) for (w, dt) in outs]
    out_shape += [jax.ShapeDtypeStruct((r, w), F32) for (r, w) in accs]
    return pl.pallas_call(
        body, name=name, grid=(n,), in_specs=in_specs, out_specs=out_specs, out_shape=out_shape,
        compiler_params=_params(("arbitrary",)),
    )(*[a for (a, _, _, _) in rows], *fulls)


def _colsum(x):
    return jnp.sum(x, axis=0, keepdims=True)


def _mean(x):
    return jnp.mean(x, axis=-1, keepdims=True)


def _modulate(x, sc, sh):
    return x * (1.0 + sc) + sh


def _shift_down(cur, prev, j):
    row = _iota(cur.shape, 0)
    return jnp.where(row < j, pltpu.roll(prev, j, 0), pltpu.roll(cur, j, 0))


def _shift_up(cur, nxt, j):
    tr = cur.shape[0]
    row = _iota(cur.shape, 0)
    return jnp.where(row < tr - j, pltpu.roll(cur, tr - j, 0), pltpu.roll(nxt, tr - j, 0))


def _conv(cur, prev, w, b):
    out = cur * w[3:4] + b
    for j in (1, 2, 3):
        out = out + _shift_down(cur, prev, j) * w[3 - j:4 - j]
    return out


def _conv_fwd(p, w_xs, b_xs, w_bc, b_bc, s):
    def fn(pos, xs, xs_prev, bc, bc_prev, w_xs, b_xs, w_bc, b_bc):
        first = pos == 0
        xs_prev = jnp.where(first, 0.0, xs_prev)
        bc_prev = jnp.where(first, 0.0, bc_prev)
        return _silu(_conv(xs, xs_prev, w_xs, b_xs)), _silu(_conv(bc, bc_prev, w_bc, b_bc))

    return _rowk("conv_fwd", fn, s, 256,
                 [(p, D, OFF_XS // D, 0), (p, D, OFF_XS // D, -1), (p, 512, OFF_BC // 512, 0), (p, 512, OFF_BC // 512, -1)],
                 [w_xs, b_xs, w_bc, b_bc], [(D, F32), (512, F32)], [])


def _conv_bwd(dxs_a, dbc_a, p, w_xs, b_xs, w_bc, b_bc, s):
    tr = 256
    n = s // tr

    def fn(pos, da1, da1n, x1, x1p, x1n, da2, da2n, x2, x2p, x2n, w1, b1, w2, b2, aw1, ab1, aw2, ab2):
        dx1, dw1, db1 = _conv_bwd_fn(pos, n, da1, da1n, x1, x1p, x1n, w1, b1)
        dx2, dw2, db2 = _conv_bwd_fn(pos, n, da2, da2n, x2, x2p, x2n, w2, b2)
        return dx1, dx2, aw1 + dw1, ab1 + db1, aw2 + dw2, ab2 + db2

    cx, cb = OFF_XS // D, OFF_BC // 512
    return _rowk("conv_bwd", fn, s, tr,
                 [(dxs_a, D, 0, 0), (dxs_a, D, 0, 1), (p, D, cx, 0), (p, D, cx, -1), (p, D, cx, 1),
                  (dbc_a, 512, 0, 0), (dbc_a, 512, 0, 1), (p, 512, cb, 0), (p, 512, cb, -1), (p, 512, cb, 1)],
                 [w_xs, b_xs, w_bc, b_bc], [(D, BF16), (512, BF16)], [(8, D), (1, D), (8, 512), (1, 512)])


def _conv_bwd_fn(pos, n, da, da_next, x, x_prev, x_next, w, b):
    first, last = pos == 0, pos == n - 1
    x_prev = jnp.where(first, 0.0, x_prev)
    dc = da * _dsilu(_conv(x, x_prev, w, b))
    dc_next = jnp.where(last, 0.0, da_next * _dsilu(_conv(x_next, x, w, b)))
    dx = dc * w[3:4]
    dws = [None] * 4
    dws[3] = _colsum(dc * x)
    for j in (1, 2, 3):
        dx = dx + _shift_up(dc, dc_next, j) * w[3 - j:4 - j]
        dws[3 - j] = _colsum(dc * _shift_down(x, x_prev, j))
    row = _iota((8, x.shape[1]), 0)
    dw = jnp.zeros((8, x.shape[1]), F32)
    for k in range(4):
        dw = jnp.where(row == k, dws[k], dw)
    return dx, dw, _colsum(dc)


def _ssd_gates(dtf, bias, a_log):
    lane = _iota(dtf.shape, 1)
    head = lane < NH
    dt = jnp.where(head, _softplus(dtf + bias), 0.0)
    a_neg = jnp.where(_iota(a_log.shape, 1) < NH, -jnp.exp(a_log), 0.0)
    a = dt * a_neg
    cs = _sel_left(_tri_lower(CHUNK), a)
    return dt, a_neg, cs


def _decay_mask(cs_ref, cst_ref, h):
    diff = cs_ref[:, h:h + 1] - cst_ref[h:h + 1, :]
    low = _iota((CHUNK, CHUNK), 1) <= _iota((CHUNK, CHUNK), 0)
    return jnp.where(low, jnp.exp(jnp.minimum(diff, 0.0)), 0.0)


def _ssd_fwd(xs_a, bc_a, p, bias128, alog128, dskip_x, s):
    nc = s // CHUNK
    t = CHUNK

    def body(xs_ref, bc_ref, dtf_ref, bias_ref, alog_ref, dsk_ref, y_ref, st_ref,
             state, x_sc, xw_sc, cs_sc, cst_sc, yd_sc):
        c = pl.program_id(0)

        @pl.when(c == 0)
        def _():
            state[...] = jnp.zeros(state.shape, F32)

        dt, _, cs = _ssd_gates(dtf_ref[...], bias_ref[...], alog_ref[...])
        cs_sc[...] = cs
        cst_sc[...] = cs.T
        cs_last = cs[t - 1:t, :]
        expand = _head_expand()
        ex = _sel_right(jnp.concatenate([dt, jnp.exp(cs), jnp.exp(cs_last - cs)], axis=0), expand)
        dt_x, eo_x, we_x = ex[0:t], ex[t:2 * t], ex[2 * t:3 * t]
        g_x = _sel_right(jnp.broadcast_to(jnp.exp(cs_last), (8, 128)), expand)[0:1]
        xs = xs_ref[...]
        x = xs * dt_x
        x_sc[...] = x.astype(BF16)
        xw_sc[...] = (x * we_x).astype(BF16)
        prev = state[...]
        st_ref[0] = prev
        prev_b = prev.astype(BF16)
        for g in range(2):
            cols = slice(g * 512, (g + 1) * 512)
            b_g = bc_ref[:, g * 128:(g + 1) * 128].astype(BF16)
            c_g = bc_ref[:, 256 + g * 128:256 + (g + 1) * 128].astype(BF16)
            gmat = _dot(c_g, b_g, NT)
            y_off = _dot(c_g, prev_b[:, cols]) * eo_x[:, cols]
            s_loc = _dot(b_g, xw_sc[:, cols], TN)
            state[:, cols] = g_x[:, cols] * prev[:, cols] + s_loc
            for e in range(HG):
                h = g * HG + e
                m = gmat * _decay_mask(cs_sc, cst_sc, h)
                yd_sc[:, h * HD:(h + 1) * HD] = _dot(m.astype(BF16), x_sc[:, h * HD:(h + 1) * HD])
            y_ref[:, cols] = yd_sc[:, cols] + y_off + dsk_ref[:, cols] * xs[:, cols]

    return pl.pallas_call(
        body, name="ssd_fwd", grid=(nc,),
        in_specs=[pl.BlockSpec((t, D), lambda c: (c, 0)),
                  pl.BlockSpec((t, 512), lambda c: (c, 0)),
                  pl.BlockSpec((t, 128), lambda c: (c, OFF_DTF // 128)),
                  pl.BlockSpec((1, 128), lambda c: (0, 0)),
                  pl.BlockSpec((1, 128), lambda c: (0, 0)),
                  pl.BlockSpec((1, D), lambda c: (0, 0))],
        out_specs=[pl.BlockSpec((t, D), lambda c: (c, 0)),
                   pl.BlockSpec((1, NSTATE, D), lambda c: (c, 0, 0))],
        out_shape=[jax.ShapeDtypeStruct((s, D), F32), jax.ShapeDtypeStruct((nc, NSTATE, D), F32)],
        scratch_shapes=[pltpu.VMEM((NSTATE, D), F32), pltpu.VMEM((t, D), BF16), pltpu.VMEM((t, D), BF16),
                        pltpu.VMEM((t, 128), F32), pltpu.VMEM((128, t), F32), pltpu.VMEM((t, D), F32)],
        compiler_params=_params(("arbitrary",)),
    )(xs_a, bc_a, p, bias128, alog128, dskip_x)


def _ssd_bwd(dy, xs_a, bc_a, p, states, bias128, alog128, dskip_x, s):
    nc = s // CHUNK
    t = CHUNK

    def body(dy_ref, xs_ref, bc_ref, dtf_ref, st_ref, bias_ref, alog_ref, dsk_ref,
             dxs_ref, dbc_ref, ddt_ref, dalog_ref, dskip_ref,
             dstate, x_sc, dy_sc, dx_sc, deo_sc, dwe_sc, cs_sc, cst_sc, dcol_sc, drow_sc):
        i = pl.program_id(0)

        @pl.when(i == 0)
        def _():
            dstate[...] = jnp.zeros(dstate.shape, F32)
            dalog_ref[...] = jnp.zeros(dalog_ref.shape, F32)
            dskip_ref[...] = jnp.zeros(dskip_ref.shape, F32)

        dtf = dtf_ref[...]
        dt, a_neg, cs = _ssd_gates(dtf, bias_ref[...], alog_ref[...])
        cs_sc[...] = cs
        cst_sc[...] = cs.T
        cs_last = cs[t - 1:t, :]
        eo, we, g_end = jnp.exp(cs), jnp.exp(cs_last - cs), jnp.exp(cs_last)
        expand, reduce = _head_expand(), _head_reduce()
        ex = _sel_right(jnp.concatenate([dt, eo, we], axis=0), expand)
        dt_x, eo_x, we_x = ex[0:t], ex[t:2 * t], ex[2 * t:3 * t]
        g_x = _sel_right(jnp.broadcast_to(g_end, (8, 128)), expand)[0:1]
        xs = xs_ref[...]
        dyv = dy_ref[...]
        x = xs * dt_x
        x_sc[...] = x.astype(BF16)
        dy_sc[...] = dyv.astype(BF16)
        dyo_b = (dyv * eo_x).astype(BF16)
        xw_b = (x * we_x).astype(BF16)
        prev = st_ref[0]
        prev_b = prev.astype(BF16)
        dnext = dstate[...]
        dnext_b = dnext.astype(BF16)
        dcol_sc[...] = jnp.zeros(dcol_sc.shape, F32)
        drow_sc[...] = jnp.zeros(drow_sc.shape, F32)
        lane_row = _iota((1, 128), 1)
        sub_col = _iota((128, 1), 0)
        for g in range(2):
            cols = slice(g * 512, (g + 1) * 512)
            b_g = bc_ref[:, g * 128:(g + 1) * 128].astype(BF16)
            c_g = bc_ref[:, 256 + g * 128:256 + (g + 1) * 128].astype(BF16)
            gmat = _dot(c_g, b_g, NT)
            b_ds = _dot(b_g, dnext_b[:, cols])
            c_s = _dot(c_g, prev_b[:, cols])
            dx_sc[:, cols] = b_ds * we_x[:, cols]
            deo_sc[:, cols] = dyv[:, cols] * c_s
            dwe_sc[:, cols] = b_ds * x[:, cols]
            db = _dot(xw_b[:, cols], dnext_b[:, cols], NT)
            dc = _dot(dyo_b[:, cols], prev_b[:, cols], NT)
            dstate[:, cols] = g_x[:, cols] * dnext[:, cols] + _dot(c_g, dyo_b[:, cols], TN)
            dg = jnp.zeros((t, t), F32)
            for e in range(HG):
                h = g * HG + e
                hc = slice(h * HD, (h + 1) * HD)
                lmat = _decay_mask(cs_sc, cst_sc, h)
                m = gmat * lmat
                dx_sc[:, hc] += _dot(m.astype(BF16), dy_sc[:, hc], TN)
                dm = _dot(dy_sc[:, hc], x_sc[:, hc], NT)
                dg = dg + dm * lmat
                qm = dm * m
                dcol_sc[...] += jnp.sum(qm, axis=1, keepdims=True) * (lane_row == h).astype(F32)
                drow_sc[...] += (sub_col == h).astype(F32) * jnp.sum(qm, axis=0, keepdims=True)
            dg_b = dg.astype(BF16)
            dbc_ref[:, g * 128:(g + 1) * 128] = db + _dot(dg_b, c_g, TN)
            dbc_ref[:, 256 + g * 128:256 + (g + 1) * 128] = dc + _dot(dg_b, b_g)
        d_eo = _sel_right(deo_sc[...], reduce)
        d_we = _sel_right(dwe_sc[...], reduce)
        d_gend = _sel_right(jnp.broadcast_to(_colsum(dnext * prev), (8, D)), reduce)[0:1]
        d_cs = dcol_sc[...] - drow_sc[...].T + d_eo * eo - d_we * we
        extra = _colsum(d_we * we) + d_gend * g_end
        d_cs = d_cs + jnp.where(_iota((t, 128), 0) == t - 1, extra, 0.0)
        da = _sel_left(_tri_upper(t), d_cs)
        dx = dx_sc[...]
        ddt = _sel_right(dx * xs, reduce) + da * a_neg
        dxs_ref[...] = dx * dt_x + dsk_ref[...] * dyv
        ddt_ref[...] = jnp.where(_iota((t, 128), 1) < NH, ddt * _sigmoid(dtf + bias_ref[...]), 0.0)
        dalog_ref[...] += _colsum(da * dt) * a_neg
        dskip_ref[...] += _sel_right(jnp.broadcast_to(_colsum(dyv * xs), (8, D)), reduce)[0:1]

    rev = lambda i: nc - 1 - i
    return pl.pallas_call(
        body, name="ssd_bwd", grid=(nc,),
        in_specs=[pl.BlockSpec((t, D), lambda i: (rev(i), 0)),
                  pl.BlockSpec((t, D), lambda i: (rev(i), 0)),
                  pl.BlockSpec((t, 512), lambda i: (rev(i), 0)),
                  pl.BlockSpec((t, 128), lambda i: (rev(i), OFF_DTF // 128)),
                  pl.BlockSpec((1, NSTATE, D), lambda i: (rev(i), 0, 0)),
                  pl.BlockSpec((1, 128), lambda i: (0, 0)),
                  pl.BlockSpec((1, 128), lambda i: (0, 0)),
                  pl.BlockSpec((1, D), lambda i: (0, 0))],
        out_specs=[pl.BlockSpec((t, D), lambda i: (rev(i), 0)),
                   pl.BlockSpec((t, 512), lambda i: (rev(i), 0)),
                   pl.BlockSpec((t, 128), lambda i: (rev(i), 0)),
                   pl.BlockSpec((1, 128), lambda i: (0, 0)),
                   pl.BlockSpec((1, 128), lambda i: (0, 0))],
        out_shape=[jax.ShapeDtypeStruct((s, D), F32), jax.ShapeDtypeStruct((s, 512), F32),
                   jax.ShapeDtypeStruct((s, 128), F32), jax.ShapeDtypeStruct((1, 128), F32),
                   jax.ShapeDtypeStruct((1, 128), F32)],
        scratch_shapes=[pltpu.VMEM((NSTATE, D), F32), pltpu.VMEM((t, D), BF16), pltpu.VMEM((t, D), BF16),
                        pltpu.VMEM((t, D), F32), pltpu.VMEM((t, D), F32), pltpu.VMEM((t, D), F32),
                        pltpu.VMEM((t, 128), F32), pltpu.VMEM((128, t), F32),
                        pltpu.VMEM((t, 128), F32), pltpu.VMEM((128, t), F32)],
        compiler_params=_params(("arbitrary",)),
    )(dy, xs_a, bc_a, p, states, bias128, alog128, dskip_x)


def _gate_lanes(shape):
    lane = _iota(shape, 1)
    return (lane >= NH) & (lane < 2 * NH)


def _cum_fwd(p, bias128, s):
    tr = min(512, s)

    def body(dtf_ref, bias_ref, o_ref, carry):
        @pl.when(pl.program_id(0) == 0)
        def _():
            carry[...] = jnp.zeros(carry.shape, F32)

        lf = jnp.where(_gate_lanes((tr, 128)), _log_sigmoid(dtf_ref[...] + bias_ref[...]), 0.0)
        cum = _sel_left(_tri_lower(tr), lf) + carry[...]
        carry[...] = cum[tr - 1:tr, :]
        o_ref[...] = cum.T[NH:2 * NH, :]

    return pl.pallas_call(
        body, name="cum_fwd", grid=(s // tr,),
        in_specs=[pl.BlockSpec((tr, 128), lambda i: (i, OFF_DTF // 128)), pl.BlockSpec((1, 128), lambda i: (0, 0))],
        out_specs=pl.BlockSpec((NH, tr), lambda i: (0, i)),
        out_shape=jax.ShapeDtypeStruct((NH, s), F32),
        scratch_shapes=[pltpu.VMEM((1, 128), F32)],
        compiler_params=_params(("arbitrary",)),
    )(p, bias128)


def _cum_bwd(dcum, ddt_raw, p, bias128, s):
    tr = min(512, s)

    def fn(pos, dcum, ddt, dtf, bias, carry, acc):
        suffix = _sel_left(_tri_upper(tr), dcum) + carry
        dfr = jnp.where(_gate_lanes((tr, 128)), suffix * _sigmoid(-(dtf + bias)), 0.0)
        out = ddt + dfr
        return out, suffix[0:1, :], acc + _colsum(out)

    return _rowk("cum_bwd", fn, s, tr, [(dcum, 128, 0, 0), (ddt_raw, 128, 0, 0), (p, 128, OFF_DTF // 128, 0)],
                 [bias128], [(128, BF16)], [(1, 128), (1, 128)], reverse=True)


def _attn_masked_logits(q_h, k_h, crow, qi, ki, tq, tk):
    row = qi * tq + _iota((tq, tk), 0)
    col = ki * tk + _iota((tq, tk), 1)
    return jnp.where(col <= row, _dot(q_h, k_h, NT) - crow, -1e30)


def _attn_fwd(p, cum_t, s):
    tq = tk = min(512, s)
    nq = s // tq

    def body(q_ref, k_ref, v_ref, c_ref, o_ref, lse_ref, m_sc, l_sc, acc_sc):
        j, qi, ki = pl.program_id(0), pl.program_id(1), pl.program_id(2)

        @pl.when(ki == 0)
        def _():
            m_sc[...] = jnp.full(m_sc.shape, -1e30, F32)
            l_sc[...] = jnp.zeros(l_sc.shape, F32)
            acc_sc[...] = jnp.zeros(acc_sc.shape, F32)

        @pl.when(ki <= qi)
        def _():
            q = (q_ref[...] * ATT_SCALE).astype(BF16)
            k = k_ref[...].astype(BF16)
            v = v_ref[...].astype(BF16)
            for hh in range(2):
                hc = slice(hh * HD, (hh + 1) * HD)
                crow = c_ref[pl.ds(2 * j + hh, 1), :]
                sc = _attn_masked_logits(q[:, hc], k[:, hc], crow, qi, ki, tq, tk)
                m_prev = m_sc[hh]
                m_new = jnp.maximum(m_prev, jnp.max(sc, axis=1, keepdims=True))
                pr = jnp.exp(sc - m_new[:, 0:1])
                alpha = jnp.exp(m_prev - m_new)
                l_sc[hh] = alpha * l_sc[hh] + jnp.sum(pr, axis=1, keepdims=True)
                m_sc[hh] = m_new
                acc_sc[:, hc] = alpha[:, :HD] * acc_sc[:, hc] + _dot(pr.astype(BF16), v[:, hc])

        @pl.when(ki == qi)
        def _():
            for hh in range(2):
                hc = slice(hh * HD, (hh + 1) * HD)
                l = l_sc[hh]
                o_ref[:, hc] = acc_sc[:, hc] / l[:, :HD]
                lse_ref[:, hc] = (m_sc[hh] + jnp.log(l))[:, :HD]

    def kv_map(j, qi, ki, off):
        return (jnp.minimum(ki, qi), off + j)

    return pl.pallas_call(
        body, name="attn_fwd", grid=(NH // 2, nq, nq),
        in_specs=[pl.BlockSpec((tq, 128), lambda j, qi, ki: (qi, OFF_Q // 128 + j)),
                  pl.BlockSpec((tk, 128), functools.partial(kv_map, off=OFF_K // 128)),
                  pl.BlockSpec((tk, 128), functools.partial(kv_map, off=OFF_V // 128)),
                  pl.BlockSpec((NH, tk), lambda j, qi, ki: (0, jnp.minimum(ki, qi)))],
        out_specs=[pl.BlockSpec((tq, 128), lambda j, qi, ki: (qi, j)),
                   pl.BlockSpec((tq, 128), lambda j, qi, ki: (qi, j))],
        out_shape=[jax.ShapeDtypeStruct((s, D), F32), jax.ShapeDtypeStruct((s, D), F32)],
        scratch_shapes=[pltpu.VMEM((2, tq, 128), F32), pltpu.VMEM((2, tq, 128), F32), pltpu.VMEM((tq, 128), F32)],
        compiler_params=_params(("parallel", "parallel", "arbitrary")),
    )(p, p, p, cum_t)


def _attn_bwd(p, cum_t, o, lse, do, s):
    tq = tk = min(512, s)
    nq = s // tq

    def body(q_ref, k_ref, v_ref, c_ref, o_ref, lse_ref, do_ref, dq_ref, dk_ref, dv_ref, dc_ref, dr_ref, dk_sc, dv_sc, dc_sc):
        j, ki, qi = pl.program_id(0), pl.program_id(1), pl.program_id(2)

        @pl.when(qi == ki)
        def _():
            dk_sc[...] = jnp.zeros(dk_sc.shape, F32)
            dv_sc[...] = jnp.zeros(dv_sc.shape, F32)
            dc_sc[...] = jnp.zeros(dc_sc.shape, F32)

        @pl.when(qi >= ki)
        def _():
            q = (q_ref[...] * ATT_SCALE).astype(BF16)
            k = k_ref[...].astype(BF16)
            v = v_ref[...].astype(BF16)
            dov, ov, lse_v = do_ref[...], o_ref[...], lse_ref[...]
            rows = pl.ds(pl.multiple_of(qi * tq, tq), tq)
            for hh in range(2):
                hc = slice(hh * HD, (hh + 1) * HD)
                crow = c_ref[pl.ds(2 * j + hh, 1), :]
                sc = _attn_masked_logits(q[:, hc], k[:, hc], crow, qi, ki, tq, tk)
                pr = jnp.exp(sc - lse_v[:, hh * HD:hh * HD + 1])
                do_h = dov[:, hc]
                delta = jnp.sum(do_h * ov[:, hc], axis=1, keepdims=True)
                do_b = do_h.astype(BF16)
                dv_sc[:, hc] += _dot(pr.astype(BF16), do_b, TN)
                ds = pr * (_dot(do_b, v[:, hc], NT) - delta)
                dc_sc[hh:hh + 1, :] += _colsum(ds)
                ds_b = ds.astype(BF16)
                dk_sc[:, hc] += _dot(ds_b, q[:, hc], TN)
                dq_h = _dot(ds_b, k[:, hc]) * ATT_SCALE
                drow = jnp.broadcast_to(jnp.sum(ds, axis=1, keepdims=True), (tq, HD))

                @pl.when(ki == 0)
                def _():
                    dq_ref[rows, hc] = dq_h
                    dr_ref[rows, hc] = drow

                @pl.when(ki > 0)
                def _():
                    dq_ref[rows, hc] += dq_h
                    dr_ref[rows, hc] += drow

        @pl.when(qi == nq - 1)
        def _():
            dk_ref[...] = dk_sc[...].astype(BF16)
            dv_ref[...] = dv_sc[...].astype(BF16)
            dc_ref[0] = dc_sc[...]

    def q_map(j, ki, qi, off):
        return (jnp.maximum(qi, ki), off + j)

    return pl.pallas_call(
        body, name="attn_bwd", grid=(NH // 2, nq, nq),
        in_specs=[pl.BlockSpec((tq, 128), functools.partial(q_map, off=OFF_Q // 128)),
                  pl.BlockSpec((tk, 128), lambda j, ki, qi: (ki, OFF_K // 128 + j)),
                  pl.BlockSpec((tk, 128), lambda j, ki, qi: (ki, OFF_V // 128 + j)),
                  pl.BlockSpec((NH, tk), lambda j, ki, qi: (0, ki)),
                  pl.BlockSpec((tq, 128), functools.partial(q_map, off=0)),
                  pl.BlockSpec((tq, 128), functools.partial(q_map, off=0)),
                  pl.BlockSpec((tq, 128), functools.partial(q_map, off=0))],
        out_specs=[pl.BlockSpec((s, 128), lambda j, ki, qi: (0, j)),
                   pl.BlockSpec((tk, 128), lambda j, ki, qi: (ki, j)),
                   pl.BlockSpec((tk, 128), lambda j, ki, qi: (ki, j)),
                   pl.BlockSpec((1, 8, tk), lambda j, ki, qi: (j, 0, ki)),
                   pl.BlockSpec((s, 128), lambda j, ki, qi: (0, j))],
        out_shape=[jax.ShapeDtypeStruct((s, D), F32), jax.ShapeDtypeStruct((s, D), BF16),
                   jax.ShapeDtypeStruct((s, D), BF16), jax.ShapeDtypeStruct((NH // 2, 8, s), F32),
                   jax.ShapeDtypeStruct((s, D), F32)],
        scratch_shapes=[pltpu.VMEM((tk, 128), F32), pltpu.VMEM((tk, 128), F32), pltpu.VMEM((8, tk), F32)],
        compiler_params=_params(("parallel", "arbitrary", "arbitrary")),
    )(p, p, p, cum_t, o, lse, do)


def _ln_stats(u):
    mu = _mean(u)
    d = u - mu
    rstd = lax.rsqrt(_mean(d * d) + EPS)
    return d * rstd, rstd


def _ln_bwd(dx, xh, rstd, gam):
    dxh = dx * gam
    return rstd * (dxh - _mean(dxh) - xh * _mean(dxh * xh))


def _rms_bwd(d, xn, r, w):
    t = d * w
    return r * (t - xn * _mean(t * xn)), _colsum(d * xn)


def _mix_norm(y, p, att, w_ssm, w_att, s):
    def fn(pos, y, z, att, w1, w2):
        g = y * _silu(z)
        n1 = g * lax.rsqrt(_mean(g * g) + EPS) * w1
        n2 = att * lax.rsqrt(_mean(att * att) + EPS) * w2
        return (jnp.concatenate([n1, n2], axis=1),)

    return _rowk("mix_norm", fn, s, 256, [(y, D, 0, 0), (p, D, OFF_Z // D, 0), (att, D, 0, 0)],
                 [w_ssm, w_att], [(2 * D, BF16)], [])[0]


def _mix_norm_bwd(dmix, y, p, att, w_ssm, w_att, s):
    def fn(pos, dmix, y, z, att, w1, w2, a1, a2):
        sz = _silu(z)
        g = y * sz
        r1 = lax.rsqrt(_mean(g * g) + EPS)
        dg, dw1 = _rms_bwd(dmix[:, :D], g * r1, r1, w1)
        r2 = lax.rsqrt(_mean(att * att) + EPS)
        datt, dw2 = _rms_bwd(dmix[:, D:], att * r2, r2, w2)
        return dg * sz, dg * y * _dsilu(z), datt, a1 + dw1, a2 + dw2

    return _rowk("mix_norm_bwd", fn, s, 256, [(dmix, 2 * D, 0, 0), (y, D, 0, 0), (p, D, OFF_Z // D, 0), (att, D, 0, 0)],
                 [w_ssm, w_att], [(D, F32), (D, BF16), (D, F32)], [(1, D), (1, D)])


def _ln1(x0, y, g1, gam, bet, sc2, sh2, s):
    def fn(pos, x0, y, g1, gam, bet, sc2, sh2):
        xh, _ = _ln_stats(ALPHA * x0 + (1.0 + g1) * y)
        x1 = xh * gam + bet
        return x1, _modulate(x1, sc2, sh2)

    return _rowk("ln1", fn, s, 256, [(x0, D, 0, 0), (y, D, 0, 0)], [g1, gam, bet, sc2, sh2], [(D, F32), (D, BF16)], [])


def _ln2_loss(x1, ff, tgt, g2, gam, bet, s):
    def fn(pos, x1, ff, tgt, g2, gam, bet, a_loss, a_dgam, a_dbet, a_dg2):
        xh, rstd = _ln_stats(ALPHA * x1 + (1.0 + g2) * ff)
        err = xh * gam + bet - tgt
        dx2 = err * (1.0 / D)
        du = _ln_bwd(dx2, xh, rstd, gam)
        return (du, du * (1.0 + g2), a_loss + _colsum(err * err), a_dgam + _colsum(dx2 * xh),
                a_dbet + _colsum(dx2), a_dg2 + _colsum(du * ff))

    return _rowk("ln2_loss", fn, s, 256, [(x1, D, 0, 0), (ff, D, 0, 0), (tgt, D, 0, 0)], [g2, gam, bet],
                 [(D, F32), (D, BF16)], [(1, D)] * 4)


def _ln1_bwd(dh2, du2, x0, y, g1, gam, bet, sc2, s):
    def fn(pos, dh2, du2, x0, y, g1, gam, bet, sc2, a_sc, a_sh, a_gam, a_bet, a_g1):
        xh, rstd = _ln_stats(ALPHA * x0 + (1.0 + g1) * y)
        x1 = xh * gam + bet
        dx1 = ALPHA * du2 + dh2 * (1.0 + sc2)
        du1 = _ln_bwd(dx1, xh, rstd, gam)
        return (du1, du1 * (1.0 + g1), a_sc + _colsum(dh2 * x1), a_sh + _colsum(dh2), a_gam + _colsum(dx1 * xh),
                a_bet + _colsum(dx1), a_g1 + _colsum(du1 * y))

    return _rowk("ln1_bwd", fn, s, 256, [(dh2, D, 0, 0), (du2, D, 0, 0), (x0, D, 0, 0), (y, D, 0, 0)],
                 [g1, gam, bet, sc2], [(D, F32), (D, BF16)], [(1, D)] * 5)


def _input_grad(dh1, du1, x0, sc1, s):
    def fn(pos, dh1, du1, x0, sc1, a_sc, a_sh):
        return ALPHA * du1 + dh1 * (1.0 + sc1), a_sc + _colsum(dh1 * x0), a_sh + _colsum(dh1)

    return _rowk("input_grad", fn, s, 256, [(dh1, D, 0, 0), (du1, D, 0, 0), (x0, D, 0, 0)], [sc1],
                 [(D, F32)], [(1, D)] * 2)


def _adamw(name, w, g, m, v, *, tr, slots):
    r, c = w.shape

    def body(w_ref, g_ref, m_ref, v_ref, g_out, d_out, m_out, v_out):
        if slots:
            grad = g_ref[0].astype(F32)
            for k in range(1, N_DEV):
                grad = grad + g_ref[k].astype(F32)
        else:
            grad = g_ref[...]
        m_new = ADAM_B1 * m_ref[...] + (1.0 - ADAM_B1) * grad
        v_new = ADAM_B2 * v_ref[...] + (1.0 - ADAM_B2) * (grad * grad)
        m_hat = m_new / (1.0 - ADAM_B1 ** ADAM_STEP)
        v_hat = v_new / (1.0 - ADAM_B2 ** ADAM_STEP)
        g_out[...] = grad
        d_out[...] = -ADAM_LR * (m_hat / (jnp.sqrt(v_hat) + ADAM_EPS) + ADAM_WD * w_ref[...])
        m_out[...] = m_new
        v_out[...] = v_new

    tile = pl.BlockSpec((tr, c), lambda i: (i, 0))
    g_spec = pl.BlockSpec((N_DEV, tr, c), lambda i: (0, i, 0)) if slots else tile
    return pl.pallas_call(
        body, name=name, grid=(r // tr,),
        in_specs=[tile, g_spec, tile, tile], out_specs=[tile] * 4,
        out_shape=[jax.ShapeDtypeStruct((r, c), F32)] * 4,
        compiler_params=_params(("parallel",)),
    )(w, g, m, v)


def _dot_f32(a, b, dims=NN):
    a0, a1, a2 = _split3(a)
    b0, b1, b2 = _split3(b)
    acc = _dot(a0, b0, dims)
    for x, y in ((a0, b1), (a1, b0), (a1, b1), (a0, b2), (a2, b0)):
        acc = acc + _dot(x, y, dims)
    return acc


def _ada_mod(c_all, w_shard, b_shard):
    def body(c_ref, w_ref, b_ref, o_ref):
        act = _silu(c_ref[...])
        act16 = jnp.concatenate([act, jnp.zeros_like(act)], axis=0)
        o_ref[...] = _dot_f32(act16, w_ref[...])[0:N_DEV] + b_ref[...]

    return pl.pallas_call(
        body, name="ada_mod", out_shape=jax.ShapeDtypeStruct((N_DEV, w_shard.shape[1]), F32),
        compiler_params=_params(None),
    )(c_all, w_shard, b_shard)


def _ada_grad(c_all, dmod_cols, dmod_all):
    def body(c_ref, dc_ref, da_ref, gw_ref, gb_ref):
        act = _silu(c_ref[...])
        act16 = jnp.concatenate([act, jnp.zeros_like(act)], axis=0)
        dm = dc_ref[...]
        dm16 = jnp.concatenate([dm, jnp.zeros_like(dm)], axis=0)
        gw_ref[...] = _dot_f32(act16, dm16, TN)
        gb_ref[...] = _colsum(da_ref[...])

    return pl.pallas_call(
        body, name="ada_grad",
        out_shape=[jax.ShapeDtypeStruct((D, dmod_cols.shape[1]), F32), jax.ShapeDtypeStruct((1, 6 * D), F32)],
        compiler_params=_params(None),
    )(c_all, dmod_cols, dmod_all)


def _sum_slots(name, g):
    def body(g_ref, o_ref):
        acc = g_ref[0]
        for k in range(1, N_DEV):
            acc = acc + g_ref[k]
        o_ref[...] = acc

    return pl.pallas_call(body, name=name, out_shape=jax.ShapeDtypeStruct(g.shape[1:], F32),
                          compiler_params=_params(None))(g)


def _exchange(name, xs, scatter):
    n = len(xs)
    n_peer = N_DEV - 1

    def body(*refs):
        x_refs, o_refs = refs[:n], refs[n:2 * n]
        send_sems, recv_sems, local_sems = refs[2 * n:]
        mx, my, mc = lax.axis_index("x"), lax.axis_index("y"), lax.axis_index("c")
        me = 4 * mx + 2 * my + mc

        def src(a, slot):
            return x_refs[a].at[slot] if scatter else x_refs[a]

        own = [pltpu.make_async_copy(src(a, me), o_refs[a].at[me], local_sems.at[a]) for a in range(n)]
        for cp in own:
            cp.start()
        sends = []
        for d in range(1, N_DEV):
            px = 1 - mx if d & 4 else mx
            py = 1 - my if d & 2 else my
            pc = 1 - mc if d & 1 else mc
            peer = 4 * px + 2 * py + pc
            for a in range(n):
                def copy(src_slot, dst_slot, a=a, d=d, to=(px, py, pc)):
                    return pltpu.make_async_remote_copy(
                        src_ref=src(a, src_slot), dst_ref=o_refs[a].at[dst_slot],
                        send_sem=send_sems.at[a * n_peer + d - 1], recv_sem=recv_sems.at[a * n_peer + d - 1],
                        device_id=to, device_id_type=pl.DeviceIdType.MESH)

                out = copy(peer, me)
                out.start()
                sends.append((out, copy(me, peer)))
        for _, arrival in sends:
            arrival.wait_recv()
        for out, _ in sends:
            out.wait_send()
        for cp in own:
            cp.wait()

    shapes = [tuple(x.shape[1:] if scatter else x.shape) for x in xs]
    return pl.pallas_call(
        body, name=name,
        in_specs=[pl.BlockSpec(memory_space=pl.ANY)] * n, out_specs=[pl.BlockSpec(memory_space=pl.ANY)] * n,
        out_shape=[jax.ShapeDtypeStruct((N_DEV,) + sh, x.dtype) for sh, x in zip(shapes, xs)],
        scratch_shapes=[pltpu.SemaphoreType.DMA((n * n_peer,)), pltpu.SemaphoreType.DMA((n * n_peer,)),
                        pltpu.SemaphoreType.DMA((n,))],
        compiler_params=pltpu.CompilerParams(has_side_effects=True),
    )(*xs)


def _after(x, zero):
    return x if zero is None else x + zero.reshape(-1)[0].astype(x.dtype)


_HBM = pl.BlockSpec(memory_space=pltpu.HBM)
_SEM = pl.BlockSpec(memory_space=pltpu.SEMAPHORE)


def _exchange_copies(x_refs, land_refs, send_sems, recv_sems, scatter):
    n = len(x_refs)
    n_peer = N_DEV - 1
    mx, my, mc = lax.axis_index("x"), lax.axis_index("y"), lax.axis_index("c")
    me = 4 * mx + 2 * my + mc
    pairs = []
    for d in range(1, N_DEV):
        px = 1 - mx if d & 4 else mx
        py = 1 - my if d & 2 else my
        pc = 1 - mc if d & 1 else mc
        peer = 4 * px + 2 * py + pc
        for a in range(n):
            def copy(src_slot, dst_slot, a=a, d=d, to=(px, py, pc)):
                return pltpu.make_async_remote_copy(
                    src_ref=x_refs[a].at[src_slot] if scatter else x_refs[a], dst_ref=land_refs[a].at[dst_slot],
                    send_sem=send_sems.at[a * n_peer + d - 1], recv_sem=recv_sems.at[a * n_peer + d - 1],
                    device_id=to, device_id_type=pl.DeviceIdType.MESH)

            pairs.append((copy(peer, me), copy(me, peer)))
    return me, pairs


def _exchange_async(name, xs, scatter, collective_id):
    n = len(xs)
    shapes = [tuple(x.shape[1:] if scatter else x.shape) for x in xs]
    x_refs = [jax.new_ref(x, memory_space=pltpu.MemorySpace.HBM) for x in xs]
    land_refs = [jax.empty_ref(jax.ShapeDtypeStruct((N_DEV,) + sh, x.dtype), memory_space=pltpu.MemorySpace.HBM)
                 for sh, x in zip(shapes, xs)]

    @pl.kernel(mesh=plsc.ScalarSubcoreMesh(axis_name="sequencer", num_cores=1), name=name,
               scratch_types=(pltpu.SemaphoreType.DMA((n * (N_DEV - 1),)), pltpu.SemaphoreType.DMA((n * (N_DEV - 1),)),
                              pltpu.SemaphoreType.DMA((n,))),
               compiler_params=pltpu.CompilerParams(collective_id=collective_id))
    def launch(send_sems, recv_sems, own_sems):
        barrier = pltpu.get_barrier_semaphore()
        mx, my, mc = lax.axis_index("x"), lax.axis_index("y"), lax.axis_index("c")
        for d in range(1, N_DEV):
            peer = (1 - mx if d & 4 else mx, 1 - my if d & 2 else my, 1 - mc if d & 1 else mc)
            pl.semaphore_signal(barrier, inc=1, device_id=peer, device_id_type=pl.DeviceIdType.MESH)
        pl.semaphore_wait(barrier, N_DEV - 1)
        me, pairs = _exchange_copies(x_refs, land_refs, send_sems, recv_sems, scatter)
        own = [pltpu.make_async_copy(x_refs[a].at[me] if scatter else x_refs[a], land_refs[a].at[me], own_sems.at[a])
               for a in range(n)]
        for cp in own:
            cp.start()
        for out, _ in pairs:
            out.start()
        for out, arrival in pairs:
            arrival.wait_recv()
            out.wait_send()
        for cp in own:
            cp.wait()

    launch()
    return lambda: [r[...] for r in land_refs]


def _exchange_start(name, xs, scatter):
    n = len(xs)
    shapes = [tuple(x.shape[1:] if scatter else x.shape) for x in xs]

    def body(*refs):
        x_refs, land_refs = refs[:n], refs[n:2 * n]
        send_sems, recv_sems = refs[2 * n], refs[2 * n + 1]
        token, own_sems = refs[4 * n + 2], refs[4 * n + 3]
        me, pairs = _exchange_copies(x_refs, land_refs, send_sems, recv_sems, scatter)
        own = [pltpu.make_async_copy(x_refs[a].at[me] if scatter else x_refs[a], land_refs[a].at[me], own_sems.at[a])
               for a in range(n)]
        for cp in own:
            cp.start()
        for out, _ in pairs:
            out.start()
        for cp in own:
            cp.wait()
        token[...] = jnp.zeros(token.shape, token.dtype)

    lands = [pltpu.with_memory_space_constraint(lax.empty((N_DEV,) + sh, x.dtype), pltpu.HBM) for sh, x in zip(shapes, xs)]
    res = pl.pallas_call(
        body, name=name,
        out_shape=(pltpu.SemaphoreType.DMA((n * (N_DEV - 1),)), pltpu.SemaphoreType.DMA((n * (N_DEV - 1),)),
                   *[pltpu.HBM(x.shape, x.dtype) for x in xs], *[pltpu.HBM(l.shape, l.dtype) for l in lands],
                   jax.ShapeDtypeStruct((8, 128), F32)),
        in_specs=[_HBM] * (2 * n),
        out_specs=(_SEM, _SEM, *[_HBM] * (2 * n), pl.BlockSpec(memory_space=pltpu.VMEM)),
        input_output_aliases={i: 2 + i for i in range(2 * n)},
        scratch_shapes=[pltpu.SemaphoreType.DMA((n,))],
        compiler_params=pltpu.CompilerParams(has_side_effects=pltpu.SideEffectType.DATAFLOW_SIDE_EFFECTING),
    )(*[pltpu.with_memory_space_constraint(x, pltpu.HBM) for x in xs], *lands)
    return dict(send=res[0], recv=res[1], xs=list(res[2:2 + n]), lands=list(res[2 + n:2 + 2 * n]), token=res[2 + 2 * n])


def _exchange_wait(name, handle, after, scatter):
    n = len(handle['xs'])

    def body(*refs):
        x_refs, land_refs = refs[:n], refs[n:2 * n]
        send_sems, recv_sems = refs[2 * n], refs[2 * n + 1]
        _, pairs = _exchange_copies(x_refs, land_refs, send_sems, recv_sems, scatter)
        for out, arrival in pairs:
            out.wait_send()
            arrival.wait_recv()

    res = pl.pallas_call(
        body, name=name,
        out_shape=tuple(pltpu.HBM(a.shape, a.dtype) for a in handle['xs'] + handle['lands']),
        in_specs=[_HBM] * (2 * n) + [_SEM, _SEM, pl.BlockSpec(memory_space=pl.ANY)],
        out_specs=tuple([_HBM] * (2 * n)),
        input_output_aliases={i: i for i in range(2 * n)},
        compiler_params=pltpu.CompilerParams(has_side_effects=pltpu.SideEffectType.DATAFLOW_SIDE_EFFECTING),
    )(*handle['xs'], *handle['lands'], handle['send'], handle['recv'], after)
    return list(res[n:])


def _relu2(a):
    r = jnp.maximum(a, 0.0)
    return r * r


def _relu2_grad(acc, a):
    return acc * (2.0 * jnp.maximum(a, 0.0))


def _local_step(x0, tgt, mod, wcat, late_weights, send_grads, conv_w, conv_b, dt_bias, a_log, d_skip, ssm_norm_w, f_bias,
                attn_norm_w, ln1_g, ln1_b, ln2_g, ln2_b):
    ff_w = DFF // N_DEV
    s = x0.shape[0]
    tm = min(512, s)
    ts = min(1024, s)
    sh1, sc1, g1, sh2, sc2, g2 = [mod[:, i * D:(i + 1) * D] for i in range(6)]
    zero = jnp.zeros((1, 128 - 2 * NH), F32)
    bias128 = jnp.concatenate([dt_bias, f_bias, zero], axis=1)
    alog128 = jnp.concatenate([a_log, jnp.zeros((1, 128 - NH), F32)], axis=1)
    dskip_x = jnp.repeat(d_skip, HD, axis=1)
    w_xs, w_bc, b_xs, b_bc = conv_w[:, :D], conv_w[:, D:], conv_b[:, :D], conv_b[:, D:]

    p = _mm_nn("in_proj", x0, wcat, tm=tm, tn=640, tk=D, out_dtype=F32, pro=_modulate, aux=(sc1, sh1))
    xs_a, bc_a = _conv_fwd(p, w_xs, b_xs, w_bc, b_bc, s)
    y_ssd, states = _ssd_fwd(xs_a, bc_a, p, bias128, alog128, dskip_x, s)
    cum_t = _cum_fwd(p, bias128, s)
    att, lse = _attn_fwd(p, cum_t, s)
    wout, w1s, w2 = late_weights(lse)
    ymix = _mix_norm(y_ssd, p, att, ssm_norm_w, attn_norm_w, s)
    y = _mm_nn("out_proj", ymix, wout, tm=tm, tn=512, tk=2 * D, out_dtype=F32)
    x1, h2 = _ln1(x0, y, g1, ln1_g, ln1_b, sc2, sh2, s)
    a1 = _mm_nn("ff_in", h2, w1s, tm=tm, tn=ff_w, tk=D, out_dtype=F32)
    ff = _mm_nn("ff_out", a1, w2, tm=tm, tn=512, tk=1024, out_dtype=F32, pro=_relu2)
    du2, dff, sq_err, d_ln2_g, d_ln2_b, d_g2 = _ln2_loss(x1, ff, tgt, g2, ln2_g, ln2_b, s)

    da1 = _mm_nt("d_ff_hidden", [(dff, D, 0)], [(w2, D, 0)], n=DFF, tm=tm, tn=512, out_dtype=BF16, epi=_relu2_grad,
                 epi_aux=(a1,))
    d_w2 = _mm_tn("d_w_ff_out", a1, dff, tm=1024, tn=512, ts=ts, pro=_relu2)
    d_w1s = _mm_tn("d_w_ff_in", h2, da1, tm=1024, tn=ff_w, ts=ts, col_shards=True)
    dh2 = _mm_nt("d_ff_input", [(da1, ff_w, k) for k in range(N_DEV)], [(w1s, ff_w, k) for k in range(N_DEV)], n=D,
                 tm=min(256, s), tn=512, out_dtype=F32)
    sent = send_grads("ff", [d_w1s, d_w2.reshape(N_DEV, -1, D)])
    du1, dy, d_sc2, d_sh2, d_ln1_g, d_ln1_b, d_g1 = _ln1_bwd(dh2, du2, x0, y, g1, ln1_g, ln1_b, _after(sc2, sent), s)

    dmix = _mm_nt("d_mix", [(dy, D, 0)], [(wout, D, 0)], n=2 * D, tm=tm, tn=512, out_dtype=F32)
    d_wout = _mm_tn("d_w_out", ymix, dy, tm=1024, tn=512, ts=ts)
    sent = send_grads("out", [d_wout.reshape(N_DEV, -1, D)])
    dy_ssd, dz, datt, d_ssm_w, d_attn_w = _mix_norm_bwd(dmix, y_ssd, p, att, _after(ssm_norm_w, sent), attn_norm_w, s)
    dq, dk, dv, dcs, drs = _attn_bwd(p, cum_t, att, lse, datt, s)
    dxs_a, dbc_a, ddt_raw, d_alog, d_dskip = _ssd_bwd(dy_ssd, xs_a, bc_a, p, states, bias128, alog128, dskip_x, s)
    dcum = jnp.pad(drs[:, ::HD] - dcs[:, :2, :].reshape(NH, s).T, ((0, 0), (NH, 128 - 2 * NH)))
    ddtf, _, d_bias = _cum_bwd(dcum, ddt_raw, p, bias128, s)
    dxs, dbc, d_wc_xs, d_bc_xs, d_wc_bc, d_bc_bc = _conv_bwd(dxs_a, dbc_a, p, w_xs, b_xs, w_bc, b_bc, s)

    segs = [(dz, OFF_Z, D), (dxs, OFF_XS, D), (dq, OFF_Q, D), (dk, OFF_K, D), (dv, OFF_V, D), (dbc, OFF_BC, 512),
            (ddtf, OFF_DTF, 128)]
    d_z, d_xs, d_q, d_k, d_v, d_bcw, d_dtf = [
        _mm_tn("d_w_in_%d" % i, x0, a, tm=1024, tn=min(w, 512), ts=ts, pro=_modulate, aux=(sc1, sh1))
        for i, (a, _, w) in enumerate(segs)]
    d_w_in = dict(z=d_z, xs=d_xs, bc=d_bcw, dt=d_dtf[:, :NH], q=d_q, k=d_k, v=d_v, f=d_dtf[:, NH:2 * NH])
    sent = send_grads("in", [_shard_w_in_grad(d_w_in)])
    segs[-1] = (_after(ddtf, sent), OFF_DTF, 128)
    dh1 = _mm_nt("d_h1", [(a, w, 0) for a, _, w in segs], [(wcat, w, off // w) for _, off, w in segs], n=D,
                 tm=min(256, s), tn=512, out_dtype=F32)
    grad_x, d_sc1, d_sh1 = _input_grad(dh1, du1, x0, sc1, s)

    return dict(
        loss=(0.5 / D) * jnp.sum(sq_err), grad_x=grad_x,
        d_mod=jnp.concatenate([d_sh1, d_sc1, d_g1, d_sh2, d_sc2, d_g2], axis=1),
        d_conv_w=jnp.concatenate([d_wc_xs[:4], d_wc_bc[:4]], axis=1), d_conv_b=jnp.concatenate([d_bc_xs, d_bc_bc], axis=1),
        d_ssm_norm_w=d_ssm_w, d_attn_norm_w=d_attn_w, d_ln1_g=d_ln1_g, d_ln1_b=d_ln1_b, d_ln2_g=d_ln2_g, d_ln2_b=d_ln2_b,
        d_gate_bias=d_bias, d_a_log=d_alog, d_d_skip=d_dskip)


W_IN_SEGS = [('z', W_Z, D), ('xs', W_XS, D), ('bc', W_BC, 512), ('dt', W_DT, NH), ('q', W_Q, D), ('k', W_K, D),
             ('v', W_V, D), ('f', W_F, NH)]
SHARD_W = IN_COLS // N_DEV


def _pack_w_in(shards):
    def cols(lo, hi):
        pieces = []
        while lo < hi:
            dev = lo // SHARD_W
            end = min(hi, (dev + 1) * SHARD_W)
            pieces.append(shards[dev][:, lo - dev * SHARD_W:end - dev * SHARD_W])
            lo = end
        return pieces

    seg = {n: cols(off, off + w) for n, off, w in W_IN_SEGS}
    pieces = seg['z'] + seg['xs'] + seg['q'] + seg['k'] + seg['v'] + seg['bc'] + seg['dt'] + seg['f']
    return jnp.concatenate(pieces + [jnp.zeros((D, 128 - 2 * NH), shards.dtype)], axis=1)


def _shard_w_in_grad(d_w_in):
    blocks = []
    for dev in range(N_DEV):
        lo, hi = dev * SHARD_W, (dev + 1) * SHARD_W
        pieces = [d_w_in[n][:, max(lo, off) - off:min(hi, off + w) - off] for n, off, w in W_IN_SEGS
                  if max(lo, off) < min(hi, off + w)]
        blocks.append(jnp.concatenate(pieces, axis=1))
    return jnp.stack(blocks, axis=0)


WEIGHTS = ['w_ada', 'b_ada', 'w_in', 'conv_w', 'conv_b', 'dt_bias', 'a_log', 'd_skip', 'ssm_norm_w', 'f_bias',
           'attn_norm_w', 'w_out', 'ln1_g', 'ln1_b', 'w_ff_in', 'w_ff_out', 'ln2_g', 'ln2_b']
BIG = ['w_in', 'w_out', 'w_ff_in', 'w_ff_out']
SMALL = ['b_ada', 'conv_b', 'ssm_norm_w', 'attn_norm_w', 'ln1_g', 'ln1_b', 'ln2_g', 'ln2_b', 'dt_bias', 'a_log', 'd_skip',
         'f_bias', 'conv_w']


def _pad_lanes(v, n=128):
    return jnp.pad(v, ((0, 0), (0, n - v.shape[1])))


def _small_block(vals):
    rows = [_pad_lanes(vals[n].reshape(1, -1), -(-vals[n].size // 128) * 128).reshape(-1, 128) for n in SMALL]
    block = jnp.concatenate(rows, axis=0)
    return jnp.pad(block, ((0, 120 - block.shape[0]), (0, 0)))


def _small_unblock(block, like):
    out, r = {}, 0
    for n in SMALL:
        size = like[n].size
        nr = -(-size // 128)
        out[n] = block[r:r + nr].reshape(-1)[:size].reshape(like[n].shape)
        r += nr
    return out


def kernel(x, c, w_ada, b_ada, w_in, conv_w, conv_b, dt_bias, a_log, d_skip, ssm_norm_w, f_bias, attn_norm_w, w_out, ln1_g, ln1_b, w_ff_in, w_ff_out, ln2_g, ln2_b, loss_target, m_w_ada, m_b_ada, m_w_in, m_conv_w, m_conv_b, m_dt_bias, m_a_log, m_d_skip, m_ssm_norm_w, m_f_bias, m_attn_norm_w, m_w_out, m_ln1_g, m_ln1_b, m_w_ff_in, m_w_ff_out, m_ln2_g, m_ln2_b, v_w_ada, v_b_ada, v_w_in, v_conv_w, v_conv_b, v_dt_bias, v_a_log, v_d_skip, v_ssm_norm_w, v_f_bias, v_attn_norm_w, v_w_out, v_ln1_g, v_ln1_b, v_w_ff_in, v_w_ff_out, v_ln2_g, v_ln2_b):
    args = dict(locals())
    w = {n: args[n] for n in WEIGHTS}
    m = {n: args['m_' + n] for n in WEIGHTS}
    v = {n: args['v_' + n] for n in WEIGHTS}
    me = 4 * lax.axis_index("x") + 2 * lax.axis_index("y") + lax.axis_index("c")
    ada_cols = 6 * D // N_DEV
    conv_cols = conv_w.shape[2]

    c_all, conv_all = _exchange("gather_cond", [c, conv_w[0]], False)
    c_all = c_all.reshape(N_DEV, D)
    conv_w_full = conv_all.transpose(1, 0, 2).reshape(4, N_DEV * conv_cols)
    b_shard = lax.dynamic_slice(b_ada, (0, me * ada_cols), (1, ada_cols))
    mod_all, = _exchange("gather_mod", [_ada_mod(c_all, w_ada[0], b_shard)], False)
    mod = lax.dynamic_index_in_dim(mod_all, me, axis=1, keepdims=False).reshape(1, 6 * D)

    win_s, = _exchange("gather_w_in", [w_in[0].astype(BF16)], False)
    first_done = win_s[0, 0:1, 0:1] * 0
    rest = _exchange_async("gather_rest", [_after(w[n][0].astype(BF16), first_done) for n in BIG[1:]], False, 1)

    def late_weights(after):
        wout_s, w1s, w2_s = rest()
        return wout_s.reshape(2 * D, D), w1s, w2_s.reshape(DFF, D)

    sends = {}

    def send_grads(tag, blocks):
        sends[tag] = _exchange_async("scatter_" + tag, blocks, True, {'ff': 2, 'out': 3, 'in': 4}[tag])
        return sum(b.reshape(-1)[0].astype(F32) * 0 for b in blocks)

    out = _local_step(x[0], loss_target[0], mod, _pack_w_in(win_s), late_weights, send_grads,
                      conv_w_full, conv_b, dt_bias, a_log, d_skip, ssm_norm_w, f_bias, attn_norm_w, ln1_g, ln1_b, ln2_g, ln2_b)
    g_ff_in, g_ff_out = sends['ff']()
    g_out, = sends['out']()
    g_in, = sends['in']()
    g_parts = [g_in, g_out, g_ff_in, g_ff_out]
    big = {n: _adamw("adamw_" + n, w[n][0], g, m[n][0], v[n][0], tr=256, slots=True) for n, g in zip(BIG, g_parts)}

    small = jnp.concatenate(
        [out['d_mod'], out['d_conv_w'].reshape(1, -1), out['d_conv_b'], out['d_ssm_norm_w'], out['d_attn_norm_w'],
         out['d_ln1_g'], out['d_ln1_b'], out['d_ln2_g'], out['d_ln2_b'], out['d_gate_bias'], out['d_a_log'],
         out['d_d_skip'], jnp.zeros((1, 128), F32)], axis=1).reshape(-1, 128)
    small_all, = _exchange("gather_small", [small], False)
    ssum = _sum_slots("sum_small", small_all)
    dmod_all = small_all[:, :6 * D // 128].reshape(N_DEV, 6 * D)
    g_w_ada, g_b_ada = _ada_grad(c_all, lax.dynamic_slice(dmod_all, (0, me * ada_cols), (N_DEV, ada_cols)), dmod_all)
    rows = lambda a, b: ssum[a:b].reshape(1, -1)
    g_conv_w = lax.dynamic_slice(ssum[48:96].reshape(4, N_DEV * conv_cols), (0, me * conv_cols), (4, conv_cols))
    g_small = dict(b_ada=g_b_ada, conv_w=g_conv_w[None], conv_b=rows(96, 108), ssm_norm_w=rows(108, 116),
                   attn_norm_w=rows(116, 124), ln1_g=rows(124, 132), ln1_b=rows(132, 140), ln2_g=rows(140, 148),
                   ln2_b=rows(148, 156), dt_bias=ssum[156:157, :NH], f_bias=ssum[156:157, NH:2 * NH],
                   a_log=ssum[157:158, :NH], d_skip=ssum[158:159, :NH])
    sm = _adamw("adamw_small", _small_block(w), _small_block(g_small), _small_block(m), _small_block(v), tr=120, slots=False)
    ada = _adamw("adamw_ada", w_ada[0], g_w_ada, m_w_ada[0], v_w_ada[0], tr=256, slots=False)

    results = []
    for k in range(4):
        vals = _small_unblock(sm[k], w)
        vals['w_ada'] = ada[k][None]
        for n in BIG:
            vals[n] = big[n][k][None]
        results.append(vals)
    loss = lax.psum(out['loss'], ("x", "y", "c"))
    return (loss, out['grad_x'][None], *[res[n] for res in results for n in WEIGHTS])
```

```python
import functools

import jax
import jax.numpy as jnp
from jax import lax
from jax.experimental import pallas as pl
from jax.experimental.pallas import tpu as pltpu
from jax.experimental.pallas import tpu_sc as plsc

F32, BF16 = jnp.float32, jnp.bfloat16

N_DEV = 8
D = 1024
NH, HD = 16, 64
NSTATE = 128
CHUNK = 128
HG = 8
DFF = 4096
ALPHA = 2.0 ** 0.25
EPS = 1e-5
ATT_SCALE = HD ** -0.5

OFF_Z, OFF_XS, OFF_Q, OFF_K, OFF_V, OFF_BC, OFF_DTF = 0, 1024, 2048, 3072, 4096, 5120, 5632
PCOLS = 5760
W_Z, W_XS, W_BC, W_DT, W_Q, W_K, W_V, W_F = 0, 1024, 2048, 2560, 2576, 3600, 4624, 5648
IN_COLS = 5664

ADAM_LR, ADAM_B1, ADAM_B2, ADAM_EPS, ADAM_WD, ADAM_STEP = 0.001, 0.9, 0.999, 1e-08, 0.01, 10

VMEM_LIMIT = 56 << 20

NN = (((1,), (0,)), ((), ()))
NT = (((1,), (1,)), ((), ()))
TN = (((0,), (0,)), ((), ()))


def _dot(a, b, dims=NN):
    return lax.dot_general(a, b, dims, preferred_element_type=F32)


def _bdot(a, b, dims=NN):
    return _dot(a.astype(BF16), b.astype(BF16), dims)


def _split3(v):
    parts, rest = [], v
    for _ in range(3):
        p = rest.astype(BF16)
        parts.append(p)
        rest = rest - p.astype(F32)
    return parts


def _sel_left(m01, v):
    return sum(_dot(m01, p) for p in _split3(v))


def _sel_right(v, m01, dims=NN):
    return sum(_dot(p, m01, dims) for p in _split3(v))


def _iota(shape, dim):
    return lax.broadcasted_iota(jnp.int32, shape, dim)


def _tri_lower(n):
    return (_iota((n, n), 1) <= _iota((n, n), 0)).astype(BF16)


def _tri_upper(n):
    return (_iota((n, n), 1) >= _iota((n, n), 0)).astype(BF16)


def _head_expand():
    return (lax.shift_right_logical(_iota((128, D), 1), 6) == _iota((128, D), 0)).astype(BF16)


def _head_reduce():
    return (lax.shift_right_logical(_iota((D, 128), 0), 6) == _iota((D, 128), 1)).astype(BF16)


def _sigmoid(x):
    return 1.0 / (1.0 + jnp.exp(-x))


def _silu(x):
    return x * _sigmoid(x)


def _dsilu(x):
    s = _sigmoid(x)
    return s * (1.0 + x * (1.0 - s))


def _softplus(x):
    return jnp.maximum(x, 0.0) + jnp.log(1.0 + jnp.exp(-jnp.abs(x)))


def _log_sigmoid(x):
    return jnp.minimum(x, 0.0) - jnp.log(1.0 + jnp.exp(-jnp.abs(x)))


def _params(sem):
    return pltpu.CompilerParams(dimension_semantics=sem, vmem_limit_bytes=VMEM_LIMIT)


def _mm_nn(name, a, b, *, tm, tn, tk, out_dtype, pro=None, aux=()):
    m, k_all = a.shape
    b_sharded = b.ndim == 3
    n = b.shape[0] * b.shape[2] if b_sharded else b.shape[1]
    assert not b_sharded or tn == b.shape[2]
    nk = k_all // tk
    n_aux = len(aux)
    b_spec = (pl.BlockSpec((None, tk, tn), lambda i, j, k: (j, k, 0)) if b_sharded
              else pl.BlockSpec((tk, tn), lambda i, j, k: (k, j)))

    def body(a_ref, b_ref, *rest):
        aux_refs, o_ref = rest[:n_aux], rest[n_aux]
        at = a_ref[...]
        if pro is not None:
            at = pro(at, *[r[...] for r in aux_refs])
        part = _bdot(at, b_ref[...])
        if nk == 1:
            o_ref[...] = part.astype(out_dtype)
            return
        acc_ref = rest[n_aux + 1]
        kk = pl.program_id(2)

        @pl.when(kk == 0)
        def _():
            acc_ref[...] = part

        @pl.when(kk > 0)
        def _():
            acc_ref[...] += part

        @pl.when(kk == nk - 1)
        def _():
            o_ref[...] = acc_ref[...].astype(out_dtype)

    return pl.pallas_call(
        body, name=name,
        grid=(m // tm, n // tn, nk),
        in_specs=[pl.BlockSpec((tm, tk), lambda i, j, k: (i, k)), b_spec]
        + [pl.BlockSpec((1, tk), lambda i, j, k: (0, k)) for _ in aux],
        out_specs=pl.BlockSpec((tm, tn), lambda i, j, k: (i, j)),
        out_shape=jax.ShapeDtypeStruct((m, n), out_dtype),
        scratch_shapes=[] if nk == 1 else [pltpu.VMEM((tm, tn), F32)],
        compiler_params=_params(("parallel", "parallel", "arbitrary")),
    )(a, b, *aux)


def _mm_nt(name, a_list, b_list, *, n, tm, tn, out_dtype, epi=None, epi_aux=()):
    m = a_list[0][0].shape[0]
    n_op = len(a_list)
    n_epi = len(epi_aux)

    def body(*refs):
        a_refs, b_refs = refs[:n_op], refs[n_op:2 * n_op]
        e_refs, o_ref = refs[2 * n_op:2 * n_op + n_epi], refs[2 * n_op + n_epi]
        acc = None
        for a_ref, b_ref in zip(a_refs, b_refs):
            part = _bdot(a_ref[...], b_ref[...], NT)
            acc = part if acc is None else acc + part
        if epi is not None:
            acc = epi(acc, *[r[...] for r in e_refs])
        o_ref[...] = acc.astype(out_dtype)

    in_specs = [pl.BlockSpec((tm, w), functools.partial(lambda i, j, cb: (i, cb), cb=cb)) for (_, w, cb) in a_list]
    for (b, w, cb) in b_list:
        if b.ndim == 3:
            in_specs.append(pl.BlockSpec((None, tn, w), functools.partial(lambda i, j, cb: (cb, j, 0), cb=cb)))
        else:
            in_specs.append(pl.BlockSpec((tn, w), functools.partial(lambda i, j, cb: (j, cb), cb=cb)))
    in_specs += [pl.BlockSpec((tm, tn), lambda i, j: (i, j)) for _ in epi_aux]
    return pl.pallas_call(
        body, name=name,
        grid=(m // tm, n // tn),
        in_specs=in_specs,
        out_specs=pl.BlockSpec((tm, tn), lambda i, j: (i, j)),
        out_shape=jax.ShapeDtypeStruct((m, n), out_dtype),
        compiler_params=_params(("parallel", "parallel")),
    )(*[a for (a, _, _) in a_list], *[b for (b, _, _) in b_list], *epi_aux)


def _mm_tn(name, a, b, *, tm, tn, ts, pro=None, aux=(), col_shards=False):
    s_all, ka = a.shape
    nb = b.shape[1]
    n_aux = len(aux)
    ns = s_all // ts
    assert not col_shards or tn == nb // N_DEV

    def body(a_ref, b_ref, *rest):
        aux_refs, o_ref, acc_ref = rest[:n_aux], rest[n_aux], rest[n_aux + 1]
        at = a_ref[...]
        if pro is not None:
            at = pro(at, *[r[...] for r in aux_refs])
        part = _bdot(at, b_ref[...], TN)
        ss = pl.program_id(2)

        @pl.when(ss == 0)
        def _():
            acc_ref[...] = part

        @pl.when(ss > 0)
        def _():
            acc_ref[...] += part

        @pl.when(ss == ns - 1)
        def _():
            o_ref[...] = acc_ref[...].astype(BF16)

    if col_shards:
        out_spec = pl.BlockSpec((None, tm, tn), lambda i, j, s: (j, i, 0))
        out_shape = jax.ShapeDtypeStruct((N_DEV, ka, tn), BF16)
    else:
        out_spec = pl.BlockSpec((tm, tn), lambda i, j, s: (i, j))
        out_shape = jax.ShapeDtypeStruct((ka, nb), BF16)
    return pl.pallas_call(
        body, name=name,
        grid=(ka // tm, nb // tn, ns),
        in_specs=[pl.BlockSpec((ts, tm), lambda i, j, s: (s, i)),
                  pl.BlockSpec((ts, tn), lambda i, j, s: (s, j))]
        + [pl.BlockSpec((1, tm), lambda i, j, s: (0, i)) for _ in aux],
        out_specs=out_spec, out_shape=out_shape,
        scratch_shapes=[pltpu.VMEM((tm, tn), F32)],
        compiler_params=_params(("parallel", "parallel", "arbitrary")),
    )(a, b, *aux)


def _rowk(name, fn, n_rows, tr, rows, fulls, outs, accs, reverse=False):
    n = n_rows // tr
    n_row, n_full, n_out, n_acc = len(rows), len(fulls), len(outs), len(accs)

    def pos(i):
        return (n - 1 - i) if reverse else i

    def body(*refs):
        row_refs = refs[:n_row]
        full_refs = refs[n_row:n_row + n_full]
        out_refs = refs[n_row + n_full:n_row + n_full + n_out]
        acc_refs = refs[n_row + n_full + n_out:]
        i = pl.program_id(0)

        @pl.when(i == 0)
        def _():
            for r in acc_refs:
                r[...] = jnp.zeros(r.shape, r.dtype)

        res = fn(pos(i), *[r[...] for r in row_refs], *[r[...] for r in full_refs], *[r[...] for r in acc_refs])
        for r, v in zip(out_refs + acc_refs, res):
            r[...] = v.astype(r.dtype)

    def row_map(i, cb, shift):
        return (jnp.clip(pos(i) + shift, 0, n - 1), cb)

    in_specs = [pl.BlockSpec((tr, w), functools.partial(row_map, cb=cb, shift=sh)) for (_, w, cb, sh) in rows]
    in_specs += [pl.BlockSpec(f.shape, functools.partial(lambda i, nd: (0,) * nd, nd=f.ndim)) for f in fulls]
    out_specs = [pl.BlockSpec((tr, w), lambda i: (pos(i), 0)) for (w, _) in outs]
    out_specs += [pl.BlockSpec((r, w), lambda i: (0, 0)) for (r, w) in accs]
    out_shape = [jax.ShapeDtypeStruct((n_rows, w), dt) for (w, dt) in outs]
    out_shape += [jax.ShapeDtypeStruct((r, w), F32) for (r, w) in accs]
    return pl.pallas_call(
        body, name=name, grid=(n,), in_specs=in_specs, out_specs=out_specs, out_shape=out_shape,
        compiler_params=_params(("arbitrary",)),
    )(*[a for (a, _, _, _) in rows], *fulls)


def _colsum(x):
    return jnp.sum(x, axis=0, keepdims=True)


def _mean(x):
    return jnp.mean(x, axis=-1, keepdims=True)


def _modulate(x, sc, sh):
    return x * (1.0 + sc) + sh


def _shift_down(cur, prev, j):
    row = _iota(cur.shape, 0)
    return jnp.where(row < j, pltpu.roll(prev, j, 0), pltpu.roll(cur, j, 0))


def _shift_up(cur, nxt, j):
    tr = cur.shape[0]
    row = _iota(cur.shape, 0)
    return jnp.where(row < tr - j, pltpu.roll(cur, tr - j, 0), pltpu.roll(nxt, tr - j, 0))


def _conv(cur, prev, w, b):
    out = cur * w[3:4] + b
    for j in (1, 2, 3):
        out = out + _shift_down(cur, prev, j) * w[3 - j:4 - j]
    return out


def _conv_fwd(p, w_xs, b_xs, w_bc, b_bc, s):
    def fn(pos, xs, xs_prev, bc, bc_prev, w_xs, b_xs, w_bc, b_bc):
        first = pos == 0
        xs_prev = jnp.where(first, 0.0, xs_prev)
        bc_prev = jnp.where(first, 0.0, bc_prev)
        return _silu(_conv(xs, xs_prev, w_xs, b_xs)), _silu(_conv(bc, bc_prev, w_bc, b_bc))

    return _rowk("conv_fwd", fn, s, 256,
                 [(p, D, OFF_XS // D, 0), (p, D, OFF_XS // D, -1), (p, 512, OFF_BC // 512, 0), (p, 512, OFF_BC // 512, -1)],
                 [w_xs, b_xs, w_bc, b_bc], [(D, F32), (512, F32)], [])


def _conv_bwd(dxs_a, dbc_a, p, w_xs, b_xs, w_bc, b_bc, s):
    tr = 256
    n = s // tr

    def fn(pos, da1, da1n, x1, x1p, x1n, da2, da2n, x2, x2p, x2n, w1, b1, w2, b2, aw1, ab1, aw2, ab2):
        dx1, dw1, db1 = _conv_bwd_fn(pos, n, da1, da1n, x1, x1p, x1n, w1, b1)
        dx2, dw2, db2 = _conv_bwd_fn(pos, n, da2, da2n, x2, x2p, x2n, w2, b2)
        return dx1, dx2, aw1 + dw1, ab1 + db1, aw2 + dw2, ab2 + db2

    cx, cb = OFF_XS // D, OFF_BC // 512
    return _rowk("conv_bwd", fn, s, tr,
                 [(dxs_a, D, 0, 0), (dxs_a, D, 0, 1), (p, D, cx, 0), (p, D, cx, -1), (p, D, cx, 1),
                  (dbc_a, 512, 0, 0), (dbc_a, 512, 0, 1), (p, 512, cb, 0), (p, 512, cb, -1), (p, 512, cb, 1)],
                 [w_xs, b_xs, w_bc, b_bc], [(D, BF16), (512, BF16)], [(8, D), (1, D), (8, 512), (1, 512)])


def _conv_bwd_fn(pos, n, da, da_next, x, x_prev, x_next, w, b):
    first, last = pos == 0, pos == n - 1
    x_prev = jnp.where(first, 0.0, x_prev)
    dc = da * _dsilu(_conv(x, x_prev, w, b))
    dc_next = jnp.where(last, 0.0, da_next * _dsilu(_conv(x_next, x, w, b)))
    dx = dc * w[3:4]
    dws = [None] * 4
    dws[3] = _colsum(dc * x)
    for j in (1, 2, 3):
        dx = dx + _shift_up(dc, dc_next, j) * w[3 - j:4 - j]
        dws[3 - j] = _colsum(dc * _shift_down(x, x_prev, j))
    row = _iota((8, x.shape[1]), 0)
    dw = jnp.zeros((8, x.shape[1]), F32)
    for k in range(4):
        dw = jnp.where(row == k, dws[k], dw)
    return dx, dw, _colsum(dc)


def _ssd_gates(dtf, bias, a_log):
    lane = _iota(dtf.shape, 1)
    head = lane < NH
    dt = jnp.where(head, _softplus(dtf + bias), 0.0)
    a_neg = jnp.where(_iota(a_log.shape, 1) < NH, -jnp.exp(a_log), 0.0)
    a = dt * a_neg
    cs = _sel_left(_tri_lower(CHUNK), a)
    return dt, a_neg, cs


def _decay_mask(cs_ref, cst_ref, h):
    diff = cs_ref[:, h:h + 1] - cst_ref[h:h + 1, :]
    low = _iota((CHUNK, CHUNK), 1) <= _iota((CHUNK, CHUNK), 0)
    return jnp.where(low, jnp.exp(jnp.minimum(diff, 0.0)), 0.0)


def _ssd_fwd(xs_a, bc_a, p, bias128, alog128, dskip_x, s):
    nc = s // CHUNK
    t = CHUNK

    def body(xs_ref, bc_ref, dtf_ref, bias_ref, alog_ref, dsk_ref, y_ref, st_ref,
             state, x_sc, xw_sc, cs_sc, cst_sc, yd_sc):
        c = pl.program_id(0)

        @pl.when(c == 0)
        def _():
            state[...] = jnp.zeros(state.shape, F32)

        dt, _, cs = _ssd_gates(dtf_ref[...], bias_ref[...], alog_ref[...])
        cs_sc[...] = cs
        cst_sc[...] = cs.T
        cs_last = cs[t - 1:t, :]
        expand = _head_expand()
        ex = _sel_right(jnp.concatenate([dt, jnp.exp(cs), jnp.exp(cs_last - cs)], axis=0), expand)
        dt_x, eo_x, we_x = ex[0:t], ex[t:2 * t], ex[2 * t:3 * t]
        g_x = _sel_right(jnp.broadcast_to(jnp.exp(cs_last), (8, 128)), expand)[0:1]
        xs = xs_ref[...]
        x = xs * dt_x
        x_sc[...] = x.astype(BF16)
        xw_sc[...] = (x * we_x).astype(BF16)
        prev = state[...]
        st_ref[0] = prev
        prev_b = prev.astype(BF16)
        for g in range(2):
            cols = slice(g * 512, (g + 1) * 512)
            b_g = bc_ref[:, g * 128:(g + 1) * 128].astype(BF16)
            c_g = bc_ref[:, 256 + g * 128:256 + (g + 1) * 128].astype(BF16)
            gmat = _dot(c_g, b_g, NT)
            y_off = _dot(c_g, prev_b[:, cols]) * eo_x[:, cols]
            s_loc = _dot(b_g, xw_sc[:, cols], TN)
            state[:, cols] = g_x[:, cols] * prev[:, cols] + s_loc
            for e in range(HG):
                h = g * HG + e
                m = gmat * _decay_mask(cs_sc, cst_sc, h)
                yd_sc[:, h * HD:(h + 1) * HD] = _dot(m.astype(BF16), x_sc[:, h * HD:(h + 1) * HD])
            y_ref[:, cols] = yd_sc[:, cols] + y_off + dsk_ref[:, cols] * xs[:, cols]

    return pl.pallas_call(
        body, name="ssd_fwd", grid=(nc,),
        in_specs=[pl.BlockSpec((t, D), lambda c: (c, 0)),
                  pl.BlockSpec((t, 512), lambda c: (c, 0)),
                  pl.BlockSpec((t, 128), lambda c: (c, OFF_DTF // 128)),
                  pl.BlockSpec((1, 128), lambda c: (0, 0)),
                  pl.BlockSpec((1, 128), lambda c: (0, 0)),
                  pl.BlockSpec((1, D), lambda c: (0, 0))],
        out_specs=[pl.BlockSpec((t, D), lambda c: (c, 0)),
                   pl.BlockSpec((1, NSTATE, D), lambda c: (c, 0, 0))],
        out_shape=[jax.ShapeDtypeStruct((s, D), F32), jax.ShapeDtypeStruct((nc, NSTATE, D), F32)],
        scratch_shapes=[pltpu.VMEM((NSTATE, D), F32), pltpu.VMEM((t, D), BF16), pltpu.VMEM((t, D), BF16),
                        pltpu.VMEM((t, 128), F32), pltpu.VMEM((128, t), F32), pltpu.VMEM((t, D), F32)],
        compiler_params=_params(("arbitrary",)),
    )(xs_a, bc_a, p, bias128, alog128, dskip_x)


def _ssd_bwd(dy, xs_a, bc_a, p, states, bias128, alog128, dskip_x, s):
    nc = s // CHUNK
    t = CHUNK

    def body(dy_ref, xs_ref, bc_ref, dtf_ref, st_ref, bias_ref, alog_ref, dsk_ref,
             dxs_ref, dbc_ref, ddt_ref, dalog_ref, dskip_ref,
             dstate, x_sc, dy_sc, dx_sc, deo_sc, dwe_sc, cs_sc, cst_sc, dcol_sc, drow_sc):
        i = pl.program_id(0)

        @pl.when(i == 0)
        def _():
            dstate[...] = jnp.zeros(dstate.shape, F32)
            dalog_ref[...] = jnp.zeros(dalog_ref.shape, F32)
            dskip_ref[...] = jnp.zeros(dskip_ref.shape, F32)

        dtf = dtf_ref[...]
        dt, a_neg, cs = _ssd_gates(dtf, bias_ref[...], alog_ref[...])
        cs_sc[...] = cs
        cst_sc[...] = cs.T
        cs_last = cs[t - 1:t, :]
        eo, we, g_end = jnp.exp(cs), jnp.exp(cs_last - cs), jnp.exp(cs_last)
        expand, reduce = _head_expand(), _head_reduce()
        ex = _sel_right(jnp.concatenate([dt, eo, we], axis=0), expand)
        dt_x, eo_x, we_x = ex[0:t], ex[t:2 * t], ex[2 * t:3 * t]
        g_x = _sel_right(jnp.broadcast_to(g_end, (8, 128)), expand)[0:1]
        xs = xs_ref[...]
        dyv = dy_ref[...]
        x = xs * dt_x
        x_sc[...] = x.astype(BF16)
        dy_sc[...] = dyv.astype(BF16)
        dyo_b = (dyv * eo_x).astype(BF16)
        xw_b = (x * we_x).astype(BF16)
        prev = st_ref[0]
        prev_b = prev.astype(BF16)
        dnext = dstate[...]
        dnext_b = dnext.astype(BF16)
        dcol_sc[...] = jnp.zeros(dcol_sc.shape, F32)
        drow_sc[...] = jnp.zeros(drow_sc.shape, F32)
        lane_row = _iota((1, 128), 1)
        sub_col = _iota((128, 1), 0)
        for g in range(2):
            cols = slice(g * 512, (g + 1) * 512)
            b_g = bc_ref[:, g * 128:(g + 1) * 128].astype(BF16)
            c_g = bc_ref[:, 256 + g * 128:256 + (g + 1) * 128].astype(BF16)
            gmat = _dot(c_g, b_g, NT)
            b_ds = _dot(b_g, dnext_b[:, cols])
            c_s = _dot(c_g, prev_b[:, cols])
            dx_sc[:, cols] = b_ds * we_x[:, cols]
            deo_sc[:, cols] = dyv[:, cols] * c_s
            dwe_sc[:, cols] = b_ds * x[:, cols]
            db = _dot(xw_b[:, cols], dnext_b[:, cols], NT)
            dc = _dot(dyo_b[:, cols], prev_b[:, cols], NT)
            dstate[:, cols] = g_x[:, cols] * dnext[:, cols] + _dot(c_g, dyo_b[:, cols], TN)
            dg = jnp.zeros((t, t), F32)
            for e in range(HG):
                h = g * HG + e
                hc = slice(h * HD, (h + 1) * HD)
                lmat = _decay_mask(cs_sc, cst_sc, h)
                m = gmat * lmat
                dx_sc[:, hc] += _dot(m.astype(BF16), dy_sc[:, hc], TN)
                dm = _dot(dy_sc[:, hc], x_sc[:, hc], NT)
                dg = dg + dm * lmat
                qm = dm * m
                dcol_sc[...] += jnp.sum(qm, axis=1, keepdims=True) * (lane_row == h).astype(F32)
                drow_sc[...] += (sub_col == h).astype(F32) * jnp.sum(qm, axis=0, keepdims=True)
            dg_b = dg.astype(BF16)
            dbc_ref[:, g * 128:(g + 1) * 128] = db + _dot(dg_b, c_g, TN)
            dbc_ref[:, 256 + g * 128:256 + (g + 1) * 128] = dc + _dot(dg_b, b_g)
        d_eo = _sel_right(deo_sc[...], reduce)
        d_we = _sel_right(dwe_sc[...], reduce)
        d_gend = _sel_right(jnp.broadcast_to(_colsum(dnext * prev), (8, D)), reduce)[0:1]
        d_cs = dcol_sc[...] - drow_sc[...].T + d_eo * eo - d_we * we
        extra = _colsum(d_we * we) + d_gend * g_end
        d_cs = d_cs + jnp.where(_iota((t, 128), 0) == t - 1, extra, 0.0)
        da = _sel_left(_tri_upper(t), d_cs)
        dx = dx_sc[...]
        ddt = _sel_right(dx * xs, reduce) + da * a_neg
        dxs_ref[...] = dx * dt_x + dsk_ref[...] * dyv
        ddt_ref[...] = jnp.where(_iota((t, 128), 1) < NH, ddt * _sigmoid(dtf + bias_ref[...]), 0.0)
        dalog_ref[...] += _colsum(da * dt) * a_neg
        dskip_ref[...] += _sel_right(jnp.broadcast_to(_colsum(dyv * xs), (8, D)), reduce)[0:1]

    rev = lambda i: nc - 1 - i
    return pl.pallas_call(
        body, name="ssd_bwd", grid=(nc,),
        in_specs=[pl.BlockSpec((t, D), lambda i: (rev(i), 0)),
                  pl.BlockSpec((t, D), lambda i: (rev(i), 0)),
                  pl.BlockSpec((t, 512), lambda i: (rev(i), 0)),
                  pl.BlockSpec((t, 128), lambda i: (rev(i), OFF_DTF // 128)),
                  pl.BlockSpec((1, NSTATE, D), lambda i: (rev(i), 0, 0)),
                  pl.BlockSpec((1, 128), lambda i: (0, 0)),
                  pl.BlockSpec((1, 128), lambda i: (0, 0)),
                  pl.BlockSpec((1, D), lambda i: (0, 0))],
        out_specs=[pl.BlockSpec((t, D), lambda i: (rev(i), 0)),
                   pl.BlockSpec((t, 512), lambda i: (rev(i), 0)),
                   pl.BlockSpec((t, 128), lambda i: (rev(i), 0)),
                   pl.BlockSpec((1, 128), lambda i: (0, 0)),
                   pl.BlockSpec((1, 128), lambda i: (0, 0))],
        out_shape=[jax.ShapeDtypeStruct((s, D), F32), jax.ShapeDtypeStruct((s, 512), F32),
                   jax.ShapeDtypeStruct((s, 128), F32), jax.ShapeDtypeStruct((1, 128), F32),
                   jax.ShapeDtypeStruct((1, 128), F32)],
        scratch_shapes=[pltpu.VMEM((NSTATE, D), F32), pltpu.VMEM((t, D), BF16), pltpu.VMEM((t, D), BF16),
                        pltpu.VMEM((t, D), F32), pltpu.VMEM((t, D), F32), pltpu.VMEM((t, D), F32),
                        pltpu.VMEM((t, 128), F32), pltpu.VMEM((128, t), F32),
                        pltpu.VMEM((t, 128), F32), pltpu.VMEM((128, t), F32)],
        compiler_params=_params(("arbitrary",)),
    )(dy, xs_a, bc_a, p, states, bias128, alog128, dskip_x)


def _gate_lanes(shape):
    lane = _iota(shape, 1)
    return (lane >= NH) & (lane < 2 * NH)


def _cum_fwd(p, bias128, s):
    tr = min(512, s)

    def body(dtf_ref, bias_ref, o_ref, carry):
        @pl.when(pl.program_id(0) == 0)
        def _():
            carry[...] = jnp.zeros(carry.shape, F32)

        lf = jnp.where(_gate_lanes((tr, 128)), _log_sigmoid(dtf_ref[...] + bias_ref[...]), 0.0)
        cum = _sel_left(_tri_lower(tr), lf) + carry[...]
        carry[...] = cum[tr - 1:tr, :]
        o_ref[...] = cum

    return pl.pallas_call(
        body, name="cum_fwd", grid=(s // tr,),
        in_specs=[pl.BlockSpec((tr, 128), lambda i: (i, OFF_DTF // 128)), pl.BlockSpec((1, 128), lambda i: (0, 0))],
        out_specs=pl.BlockSpec((tr, 128), lambda i: (i, 0)),
        out_shape=jax.ShapeDtypeStruct((s, 128), F32),
        scratch_shapes=[pltpu.VMEM((1, 128), F32)],
        compiler_params=_params(("arbitrary",)),
    )(p, bias128)


def _cum_bwd(dcum, ddt_raw, p, bias128, s):
    tr = min(512, s)

    def fn(pos, dcum, ddt, dtf, bias, carry, acc):
        suffix = _sel_left(_tri_upper(tr), dcum) + carry
        dfr = jnp.where(_gate_lanes((tr, 128)), suffix * _sigmoid(-(dtf + bias)), 0.0)
        out = ddt + dfr
        return out, suffix[0:1, :], acc + _colsum(out)

    return _rowk("cum_bwd", fn, s, tr, [(dcum, 128, 0, 0), (ddt_raw, 128, 0, 0), (p, 128, OFF_DTF // 128, 0)],
                 [bias128], [(128, BF16)], [(1, 128), (1, 128)], reverse=True)


ATT_BLOCK = 512
ATT_STRIP = 32


def _head_part(shape, h, dim):
    i = _iota(shape, dim)
    return (i >= h * HD) & (i < (h + 1) * HD)


def _k_augmented(k_blk, cum_blk, j, h):
    tk = k_blk.shape[0]
    lane = _iota((tk, 128), 1)
    col = jnp.sum(jnp.where(lane == NH + 2 * j + h, cum_blk, 0.0), axis=1, keepdims=True)
    c0, c1, c2 = [c.astype(F32) for c in _split3(-col)]
    aug = jnp.where(lane == 0, c0, jnp.where(lane == 1, c1, jnp.where(lane == 2, c2, 0.0)))
    return jnp.concatenate([jnp.where(_head_part((tk, 128), h, 1), k_blk, 0.0), aug], axis=1).astype(BF16)


def _q_augmented_t(q_blk):
    tq = q_blk.shape[0]
    ones = (_iota((128, tq), 0) < 3).astype(BF16)
    return jnp.concatenate([(q_blk * ATT_SCALE).T.astype(BF16), ones], axis=0)


def _rows01(r0, r1):
    sub = _iota((8, r0.shape[1]), 0)
    return jnp.where(sub == 0, r0, jnp.where(sub == 1, r1, 0.0))


def _fold8(x, op, cur):
    for g in range(x.shape[0] // 8):
        cur = op(cur, x[8 * g:8 * (g + 1), :])
    return cur


def _attn_fwd(p, cum, s):
    t = min(ATT_BLOCK, s)
    nq = s // t
    r = ATT_STRIP

    def body(q_ref, k_ref, v_ref, c_ref, o_ref, lse_ref, kaug_sc, vt_sc, s_sc, p_sc, m_sc, l_sc, acc_sc):
        j, qi = pl.program_id(0), pl.program_id(1)

        @pl.when(qi == 0)
        def _():
            for c in range(nq):
                rows = slice(c * t, (c + 1) * t)
                k_blk, vt = k_ref[rows, :], v_ref[rows, :].T
                for h in range(2):
                    kaug_sc[h, rows, :] = _k_augmented(k_blk, c_ref[rows, :], j, h)
                    vt_sc[h, :, rows] = jnp.where(_head_part((128, t), h, 0), vt, 0.0).astype(BF16)

        qaug_t = _q_augmented_t(q_ref[...])
        m_sc[...] = jnp.full(m_sc.shape, -1e30, F32)
        l_sc[...] = jnp.zeros(l_sc.shape, F32)
        acc_sc[...] = jnp.zeros(acc_sc.shape, F32)
        top = _iota((128, t), 0) < HD

        def block(kb, diagonal):
            kv = pl.ds(pl.multiple_of(kb * t, t), t)
            for h in range(2):
                s_sc[h] = _dot(kaug_sc[h, kv, :], qaug_t)

            def strip(i):
                return pl.ds(pl.multiple_of(i * r, r), r)

            def pass_max(i, carry):
                out = []
                for h in range(2):
                    x = s_sc[h, strip(i), :]
                    if diagonal:
                        x = jnp.where(_iota((r, t), 1) >= i * r + _iota((r, t), 0), x, -1e30)
                        s_sc[h, strip(i), :] = x
                    out.append(_fold8(x, jnp.maximum, carry[h]))
                return tuple(out)

            low = jnp.full((8, t), -1e30, F32)
            tops = lax.fori_loop(0, t // r, pass_max, (low, low), unroll=2)
            m_new, alpha = [], []
            for h in range(2):
                m_prev = m_sc[h, 0:1, :]
                m_new.append(jnp.maximum(m_prev, jnp.max(tops[h], axis=0, keepdims=True)))
                alpha.append(jnp.exp(m_prev - m_new[h]))
                m_sc[h, 0:1, :] = m_new[h]

            def pass_exp(i, carry):
                out = []
                for h in range(2):
                    pr = jnp.exp(s_sc[h, strip(i), :] - m_new[h])
                    p_sc[h, strip(i), :] = pr.astype(BF16)
                    out.append(_fold8(pr, jnp.add, carry[h]))
                return tuple(out)

            zero = jnp.zeros((8, t), F32)
            sums = lax.fori_loop(0, t // r, pass_exp, (zero, zero), unroll=2)
            for h in range(2):
                l_sc[h, 0:1, :] = alpha[h] * l_sc[h, 0:1, :] + jnp.sum(sums[h], axis=0, keepdims=True)
            acc_sc[...] = (acc_sc[...] * jnp.where(top, alpha[0], alpha[1])
                           + _dot(vt_sc[0, :, kv], p_sc[0]) + _dot(vt_sc[1, :, kv], p_sc[1]))

        def earlier(kb, carry):
            block(kb, False)
            return carry

        lax.fori_loop(0, qi, earlier, 0)
        block(qi, True)
        l0, l1 = l_sc[0, 0:1, :], l_sc[1, 0:1, :]
        o_ref[...] = (acc_sc[...] / jnp.where(top, l0, l1)).T
        lse_ref[0] = _rows01(m_sc[0, 0:1, :] + jnp.log(l0), m_sc[1, 0:1, :] + jnp.log(l1))

    return pl.pallas_call(
        body, name="attn_fwd", grid=(NH // 2, nq),
        in_specs=[pl.BlockSpec((t, 128), lambda j, qi: (qi, OFF_Q // 128 + j)),
                  pl.BlockSpec((s, 128), lambda j, qi: (0, OFF_K // 128 + j)),
                  pl.BlockSpec((s, 128), lambda j, qi: (0, OFF_V // 128 + j)),
                  pl.BlockSpec((s, 128), lambda j, qi: (0, 0))],
        out_specs=[pl.BlockSpec((t, 128), lambda j, qi: (qi, j)),
                   pl.BlockSpec((1, 8, t), lambda j, qi: (j, 0, qi))],
        out_shape=[jax.ShapeDtypeStruct((s, D), F32), jax.ShapeDtypeStruct((NH // 2, 8, s), F32)],
        scratch_shapes=[pltpu.VMEM((2, s, 256), BF16), pltpu.VMEM((2, 128, s), BF16), pltpu.VMEM((2, t, t), F32),
                        pltpu.VMEM((2, t, t), BF16), pltpu.VMEM((2, 8, t), F32), pltpu.VMEM((2, 8, t), F32),
                        pltpu.VMEM((128, t), F32)],
        compiler_params=_params(("parallel", "arbitrary")),
    )(p, p, p, cum)


def _attn_bwd(p, cum, o, lse, do, s):
    t = min(ATT_BLOCK, s)
    nq = s // t
    r = ATT_STRIP

    def body(q_ref, k_ref, v_ref, c_ref, o_ref, lse_ref, do_ref, dq_ref, dk_ref, dv_ref, dc_ref, dr_ref,
             qaugt_sc, qh_sc, dot_sc, doh_sc, delta_sc, dqt_sc, dr_sc, kaug_sc, vh_sc, kt_sc, s_sc, dp_sc, p_sc, ds_sc,
             dk_sc, dv_sc, dc_sc):
        j, ki = pl.program_id(0), pl.program_id(1)

        @pl.when(ki == 0)
        def _():
            for c in range(nq):
                rows = slice(c * t, (c + 1) * t)
                q_blk, do_blk = q_ref[rows, :], do_ref[rows, :]
                qaugt_sc[:, rows] = _q_augmented_t(q_blk)
                dot_sc[:, rows] = do_blk.T.astype(BF16)
                prod_t = (do_blk * o_ref[rows, :]).T
                delta_sc[:, rows] = _rows01(jnp.sum(prod_t[0:HD], axis=0, keepdims=True),
                                            jnp.sum(prod_t[HD:], axis=0, keepdims=True))
                for h in range(2):
                    head = _head_part((t, 128), h, 1)
                    qh_sc[h, rows, :] = jnp.where(head, q_blk * ATT_SCALE, 0.0).astype(BF16)
                    doh_sc[h, rows, :] = jnp.where(head, do_blk, 0.0).astype(BF16)
            dqt_sc[...] = jnp.zeros(dqt_sc.shape, F32)
            dr_sc[...] = jnp.zeros(dr_sc.shape, F32)

        k_blk, v_blk = k_ref[...], v_ref[...]
        kt = k_blk.T
        for h in range(2):
            kaug_sc[h] = _k_augmented(k_blk, c_ref[...], j, h)
            vh_sc[h] = jnp.where(_head_part((t, 128), h, 1), v_blk, 0.0).astype(BF16)
            kt_sc[h] = jnp.where(_head_part((128, t), h, 0), kt, 0.0).astype(BF16)
        dk_sc[...] = jnp.zeros(dk_sc.shape, F32)
        dv_sc[...] = jnp.zeros(dv_sc.shape, F32)
        dc_sc[...] = jnp.zeros(dc_sc.shape, F32)

        def block(qb, diagonal):
            qs = pl.ds(pl.multiple_of(qb * t, t), t)
            for h in range(2):
                s_sc[h] = _dot(kaug_sc[h], qaugt_sc[:, qs])
                dp_sc[h] = _dot(vh_sc[h], dot_sc[:, qs])
            lse_row = [lse_ref[0, h:h + 1, qs] for h in range(2)]
            delta_row = [delta_sc[h:h + 1, qs] for h in range(2)]

            def strips(i, carry):
                rows = pl.ds(pl.multiple_of(i * r, r), r)
                out = []
                for h in range(2):
                    x = s_sc[h, rows, :]
                    if diagonal:
                        x = jnp.where(_iota((r, t), 1) >= i * r + _iota((r, t), 0), x, -1e30)
                    pr = jnp.exp(x - lse_row[h])
                    ds = pr * (dp_sc[h, rows, :] - delta_row[h])
                    p_sc[h, rows, :] = pr.astype(BF16)
                    ds_sc[h, rows, :] = ds.astype(BF16)
                    dc_sc[h, rows, :] += jnp.broadcast_to(jnp.sum(ds, axis=1, keepdims=True), (r, 128))
                    out.append(_fold8(ds, jnp.add, carry[h]))
                return tuple(out)

            zero = jnp.zeros((8, t), F32)
            dr = lax.fori_loop(0, t // r, strips, (zero, zero), unroll=2)
            for h in range(2):
                dr_sc[h, :, qs] += dr[h]
            dv_sc[...] += _dot(p_sc[0], doh_sc[0, qs, :]) + _dot(p_sc[1], doh_sc[1, qs, :])
            dk_sc[...] += _dot(ds_sc[0], qh_sc[0, qs, :]) + _dot(ds_sc[1], qh_sc[1, qs, :])
            dqt_sc[:, qs] += _dot(kt_sc[0], ds_sc[0]) + _dot(kt_sc[1], ds_sc[1])

        def later(qb, carry):
            block(qb, False)
            return carry

        block(ki, True)
        lax.fori_loop(ki + 1, nq, later, 0)
        dk_ref[...] = dk_sc[...].astype(BF16)
        dv_ref[...] = dv_sc[...].astype(BF16)
        dc_ref[...] = jnp.where(_iota((t, 128), 1) < HD, dc_sc[0], dc_sc[1])

        @pl.when(ki == nq - 1)
        def _():
            for c in range(nq):
                rows = slice(c * t, (c + 1) * t)
                dq_ref[rows, :] = dqt_sc[:, rows].T * ATT_SCALE
            dr_ref[0] = _rows01(jnp.sum(dr_sc[0], axis=0, keepdims=True), jnp.sum(dr_sc[1], axis=0, keepdims=True))

    whole = lambda off: pl.BlockSpec((s, 128), functools.partial(lambda j, ki, off: (0, off + j), off=off))
    return pl.pallas_call(
        body, name="attn_bwd", grid=(NH // 2, nq),
        in_specs=[whole(OFF_Q // 128),
                  pl.BlockSpec((t, 128), lambda j, ki: (ki, OFF_K // 128 + j)),
                  pl.BlockSpec((t, 128), lambda j, ki: (ki, OFF_V // 128 + j)),
                  pl.BlockSpec((t, 128), lambda j, ki: (ki, 0)),
                  whole(0),
                  pl.BlockSpec((1, 8, s), lambda j, ki: (j, 0, 0)),
                  whole(0)],
        out_specs=[whole(0),
                   pl.BlockSpec((t, 128), lambda j, ki: (ki, j)),
                   pl.BlockSpec((t, 128), lambda j, ki: (ki, j)),
                   pl.BlockSpec((t, 128), lambda j, ki: (ki, j)),
                   pl.BlockSpec((1, 8, s), lambda j, ki: (j, 0, 0))],
        out_shape=[jax.ShapeDtypeStruct((s, D), F32), jax.ShapeDtypeStruct((s, D), BF16), jax.ShapeDtypeStruct((s, D), BF16),
                   jax.ShapeDtypeStruct((s, D), F32), jax.ShapeDtypeStruct((NH // 2, 8, s), F32)],
        scratch_shapes=[pltpu.VMEM((256, s), BF16), pltpu.VMEM((2, s, 128), BF16), pltpu.VMEM((128, s), BF16),
                        pltpu.VMEM((2, s, 128), BF16), pltpu.VMEM((8, s), F32), pltpu.VMEM((128, s), F32),
                        pltpu.VMEM((2, 8, s), F32), pltpu.VMEM((2, t, 256), BF16), pltpu.VMEM((2, t, 128), BF16),
                        pltpu.VMEM((2, 128, t), BF16), pltpu.VMEM((2, t, t), F32), pltpu.VMEM((2, t, t), F32),
                        pltpu.VMEM((2, t, t), BF16), pltpu.VMEM((2, t, t), BF16), pltpu.VMEM((t, 128), F32),
                        pltpu.VMEM((t, 128), F32), pltpu.VMEM((2, t, 128), F32)],
        compiler_params=_params(("parallel", "arbitrary")),
    )(p, p, p, cum, o, lse, do)


def _ln_stats(u):
    mu = _mean(u)
    d = u - mu
    rstd = lax.rsqrt(_mean(d * d) + EPS)
    return d * rstd, rstd


def _ln_bwd(dx, xh, rstd, gam):
    dxh = dx * gam
    return rstd * (dxh - _mean(dxh) - xh * _mean(dxh * xh))


def _rms_bwd(d, xn, r, w):
    t = d * w
    return r * (t - xn * _mean(t * xn)), _colsum(d * xn)


def _mix_norm(y, p, att, w_ssm, w_att, s):
    def fn(pos, y, z, att, w1, w2):
        g = y * _silu(z)
        n1 = g * lax.rsqrt(_mean(g * g) + EPS) * w1
        n2 = att * lax.rsqrt(_mean(att * att) + EPS) * w2
        return (jnp.concatenate([n1, n2], axis=1),)

    return _rowk("mix_norm", fn, s, 256, [(y, D, 0, 0), (p, D, OFF_Z // D, 0), (att, D, 0, 0)],
                 [w_ssm, w_att], [(2 * D, BF16)], [])[0]


def _mix_norm_bwd(dmix, y, p, att, w_ssm, w_att, s):
    def fn(pos, dmix, y, z, att, w1, w2, a1, a2):
        sz = _silu(z)
        g = y * sz
        r1 = lax.rsqrt(_mean(g * g) + EPS)
        dg, dw1 = _rms_bwd(dmix[:, :D], g * r1, r1, w1)
        r2 = lax.rsqrt(_mean(att * att) + EPS)
        datt, dw2 = _rms_bwd(dmix[:, D:], att * r2, r2, w2)
        return dg * sz, dg * y * _dsilu(z), datt, a1 + dw1, a2 + dw2

    return _rowk("mix_norm_bwd", fn, s, 256, [(dmix, 2 * D, 0, 0), (y, D, 0, 0), (p, D, OFF_Z // D, 0), (att, D, 0, 0)],
                 [w_ssm, w_att], [(D, F32), (D, BF16), (D, F32)], [(1, D), (1, D)])


def _ln1(x0, y, g1, gam, bet, sc2, sh2, s):
    def fn(pos, x0, y, g1, gam, bet, sc2, sh2):
        xh, _ = _ln_stats(ALPHA * x0 + (1.0 + g1) * y)
        x1 = xh * gam + bet
        return x1, _modulate(x1, sc2, sh2)

    return _rowk("ln1", fn, s, 256, [(x0, D, 0, 0), (y, D, 0, 0)], [g1, gam, bet, sc2, sh2], [(D, F32), (D, BF16)], [])


def _ln2_loss(x1, ff, tgt, g2, gam, bet, s):
    def fn(pos, x1, ff, tgt, g2, gam, bet, a_loss, a_dgam, a_dbet, a_dg2):
        xh, rstd = _ln_stats(ALPHA * x1 + (1.0 + g2) * ff)
        err = xh * gam + bet - tgt
        dx2 = err * (1.0 / D)
        du = _ln_bwd(dx2, xh, rstd, gam)
        return (du, du * (1.0 + g2), a_loss + _colsum(err * err), a_dgam + _colsum(dx2 * xh),
                a_dbet + _colsum(dx2), a_dg2 + _colsum(du * ff))

    return _rowk("ln2_loss", fn, s, 256, [(x1, D, 0, 0), (ff, D, 0, 0), (tgt, D, 0, 0)], [g2, gam, bet],
                 [(D, F32), (D, BF16)], [(1, D)] * 4)


def _ln1_bwd(dh2, du2, x0, y, g1, gam, bet, sc2, s):
    def fn(pos, dh2, du2, x0, y, g1, gam, bet, sc2, a_sc, a_sh, a_gam, a_bet, a_g1):
        xh, rstd = _ln_stats(ALPHA * x0 + (1.0 + g1) * y)
        x1 = xh * gam + bet
        dx1 = ALPHA * du2 + dh2 * (1.0 + sc2)
        du1 = _ln_bwd(dx1, xh, rstd, gam)
        return (du1, du1 * (1.0 + g1), a_sc + _colsum(dh2 * x1), a_sh + _colsum(dh2), a_gam + _colsum(dx1 * xh),
                a_bet + _colsum(dx1), a_g1 + _colsum(du1 * y))

    return _rowk("ln1_bwd", fn, s, 256, [(dh2, D, 0, 0), (du2, D, 0, 0), (x0, D, 0, 0), (y, D, 0, 0)],
                 [g1, gam, bet, sc2], [(D, F32), (D, BF16)], [(1, D)] * 5)


def _input_grad(dh1, du1, x0, sc1, s):
    def fn(pos, dh1, du1, x0, sc1, a_sc, a_sh):
        return ALPHA * du1 + dh1 * (1.0 + sc1), a_sc + _colsum(dh1 * x0), a_sh + _colsum(dh1)

    return _rowk("input_grad", fn, s, 256, [(dh1, D, 0, 0), (du1, D, 0, 0), (x0, D, 0, 0)], [sc1],
                 [(D, F32)], [(1, D)] * 2)


def _adamw(name, w, g, m, v, *, tr, slots):
    r, c = w.shape

    def body(w_ref, g_ref, m_ref, v_ref, g_out, d_out, m_out, v_out):
        if slots:
            grad = g_ref[0].astype(F32)
            for k in range(1, N_DEV):
                grad = grad + g_ref[k].astype(F32)
        else:
            grad = g_ref[...]
        m_new = ADAM_B1 * m_ref[...] + (1.0 - ADAM_B1) * grad
        v_new = ADAM_B2 * v_ref[...] + (1.0 - ADAM_B2) * (grad * grad)
        m_hat = m_new / (1.0 - ADAM_B1 ** ADAM_STEP)
        v_hat = v_new / (1.0 - ADAM_B2 ** ADAM_STEP)
        g_out[...] = grad
        d_out[...] = -ADAM_LR * (m_hat / (jnp.sqrt(v_hat) + ADAM_EPS) + ADAM_WD * w_ref[...])
        m_out[...] = m_new
        v_out[...] = v_new

    tile = pl.BlockSpec((tr, c), lambda i: (i, 0))
    g_spec = pl.BlockSpec((N_DEV, tr, c), lambda i: (0, i, 0)) if slots else tile
    return pl.pallas_call(
        body, name=name, grid=(r // tr,),
        in_specs=[tile, g_spec, tile, tile], out_specs=[tile] * 4,
        out_shape=[jax.ShapeDtypeStruct((r, c), F32)] * 4,
        compiler_params=_params(("parallel",)),
    )(w, g, m, v)


def _dot_f32(a, b, dims=NN):
    a0, a1, a2 = _split3(a)
    b0, b1, b2 = _split3(b)
    acc = _dot(a0, b0, dims)
    for x, y in ((a0, b1), (a1, b0), (a1, b1), (a0, b2), (a2, b0)):
        acc = acc + _dot(x, y, dims)
    return acc


def _ada_mod(c_all, w_shard, b_shard):
    def body(c_ref, w_ref, b_ref, o_ref):
        act = _silu(c_ref[...])
        act16 = jnp.concatenate([act, jnp.zeros_like(act)], axis=0)
        o_ref[...] = _dot_f32(act16, w_ref[...])[0:N_DEV] + b_ref[...]

    return pl.pallas_call(
        body, name="ada_mod", out_shape=jax.ShapeDtypeStruct((N_DEV, w_shard.shape[1]), F32),
        compiler_params=_params(None),
    )(c_all, w_shard, b_shard)


def _ada_grad(c_all, dmod_cols, dmod_all):
    def body(c_ref, dc_ref, da_ref, gw_ref, gb_ref):
        act = _silu(c_ref[...])
        act16 = jnp.concatenate([act, jnp.zeros_like(act)], axis=0)
        dm = dc_ref[...]
        dm16 = jnp.concatenate([dm, jnp.zeros_like(dm)], axis=0)
        gw_ref[...] = _dot_f32(act16, dm16, TN)
        gb_ref[...] = _colsum(da_ref[...])

    return pl.pallas_call(
        body, name="ada_grad",
        out_shape=[jax.ShapeDtypeStruct((D, dmod_cols.shape[1]), F32), jax.ShapeDtypeStruct((1, 6 * D), F32)],
        compiler_params=_params(None),
    )(c_all, dmod_cols, dmod_all)


def _sum_slots(name, g):
    def body(g_ref, o_ref):
        acc = g_ref[0]
        for k in range(1, N_DEV):
            acc = acc + g_ref[k]
        o_ref[...] = acc

    return pl.pallas_call(body, name=name, out_shape=jax.ShapeDtypeStruct(g.shape[1:], F32),
                          compiler_params=_params(None))(g)


def _exchange(name, xs, scatter):
    n = len(xs)
    n_peer = N_DEV - 1

    def body(*refs):
        x_refs, o_refs = refs[:n], refs[n:2 * n]
        send_sems, recv_sems, local_sems = refs[2 * n:]
        mx, my, mc = lax.axis_index("x"), lax.axis_index("y"), lax.axis_index("c")
        me = 4 * mx + 2 * my + mc

        def src(a, slot):
            return x_refs[a].at[slot] if scatter else x_refs[a]

        own = [pltpu.make_async_copy(src(a, me), o_refs[a].at[me], local_sems.at[a]) for a in range(n)]
        for cp in own:
            cp.start()
        sends = []
        for d in range(1, N_DEV):
            px = 1 - mx if d & 4 else mx
            py = 1 - my if d & 2 else my
            pc = 1 - mc if d & 1 else mc
            peer = 4 * px + 2 * py + pc
            for a in range(n):
                def copy(src_slot, dst_slot, a=a, d=d, to=(px, py, pc)):
                    return pltpu.make_async_remote_copy(
                        src_ref=src(a, src_slot), dst_ref=o_refs[a].at[dst_slot],
                        send_sem=send_sems.at[a * n_peer + d - 1], recv_sem=recv_sems.at[a * n_peer + d - 1],
                        device_id=to, device_id_type=pl.DeviceIdType.MESH)

                out = copy(peer, me)
                out.start()
                sends.append((out, copy(me, peer)))
        for _, arrival in sends:
            arrival.wait_recv()
        for out, _ in sends:
            out.wait_send()
        for cp in own:
            cp.wait()

    shapes = [tuple(x.shape[1:] if scatter else x.shape) for x in xs]
    return pl.pallas_call(
        body, name=name,
        in_specs=[pl.BlockSpec(memory_space=pl.ANY)] * n, out_specs=[pl.BlockSpec(memory_space=pl.ANY)] * n,
        out_shape=[jax.ShapeDtypeStruct((N_DEV,) + sh, x.dtype) for sh, x in zip(shapes, xs)],
        scratch_shapes=[pltpu.SemaphoreType.DMA((n * n_peer,)), pltpu.SemaphoreType.DMA((n * n_peer,)),
                        pltpu.SemaphoreType.DMA((n,))],
        compiler_params=pltpu.CompilerParams(has_side_effects=True),
    )(*xs)


def _after(x, zero):
    return x if zero is None else x + zero.reshape(-1)[0].astype(x.dtype)


_HBM = pl.BlockSpec(memory_space=pltpu.HBM)
_SEM = pl.BlockSpec(memory_space=pltpu.SEMAPHORE)


def _exchange_copies(x_refs, land_refs, send_sems, recv_sems, scatter):
    n = len(x_refs)
    n_peer = N_DEV - 1
    mx, my, mc = lax.axis_index("x"), lax.axis_index("y"), lax.axis_index("c")
    me = 4 * mx + 2 * my + mc
    pairs = []
    for d in range(1, N_DEV):
        px = 1 - mx if d & 4 else mx
        py = 1 - my if d & 2 else my
        pc = 1 - mc if d & 1 else mc
        peer = 4 * px + 2 * py + pc
        for a in range(n):
            def copy(src_slot, dst_slot, a=a, d=d, to=(px, py, pc)):
                return pltpu.make_async_remote_copy(
                    src_ref=x_refs[a].at[src_slot] if scatter else x_refs[a], dst_ref=land_refs[a].at[dst_slot],
                    send_sem=send_sems.at[a * n_peer + d - 1], recv_sem=recv_sems.at[a * n_peer + d - 1],
                    device_id=to, device_id_type=pl.DeviceIdType.MESH)

            pairs.append((copy(peer, me), copy(me, peer)))
    return me, pairs


def _exchange_async(name, xs, scatter, collective_id):
    n = len(xs)
    shapes = [tuple(x.shape[1:] if scatter else x.shape) for x in xs]
    x_refs = [jax.new_ref(x, memory_space=pltpu.MemorySpace.HBM) for x in xs]
    land_refs = [jax.empty_ref(jax.ShapeDtypeStruct((N_DEV,) + sh, x.dtype), memory_space=pltpu.MemorySpace.HBM)
                 for sh, x in zip(shapes, xs)]

    @pl.kernel(mesh=plsc.ScalarSubcoreMesh(axis_name="sequencer", num_cores=1), name=name,
               scratch_types=(pltpu.SemaphoreType.DMA((n * (N_DEV - 1),)), pltpu.SemaphoreType.DMA((n * (N_DEV - 1),)),
                              pltpu.SemaphoreType.DMA((n,))),
               compiler_params=pltpu.CompilerParams(collective_id=collective_id))
    def launch(send_sems, recv_sems, own_sems):
        barrier = pltpu.get_barrier_semaphore()
        mx, my, mc = lax.axis_index("x"), lax.axis_index("y"), lax.axis_index("c")
        for d in range(1, N_DEV):
            peer = (1 - mx if d & 4 else mx, 1 - my if d & 2 else my, 1 - mc if d & 1 else mc)
            pl.semaphore_signal(barrier, inc=1, device_id=peer, device_id_type=pl.DeviceIdType.MESH)
        pl.semaphore_wait(barrier, N_DEV - 1)
        me, pairs = _exchange_copies(x_refs, land_refs, send_sems, recv_sems, scatter)
        own = [pltpu.make_async_copy(x_refs[a].at[me] if scatter else x_refs[a], land_refs[a].at[me], own_sems.at[a])
               for a in range(n)]
        for cp in own:
            cp.start()
        for out, _ in pairs:
            out.start()
        for out, arrival in pairs:
            arrival.wait_recv()
            out.wait_send()
        for cp in own:
            cp.wait()

    launch()
    return lambda: [r[...] for r in land_refs]


def _exchange_start(name, xs, scatter):
    n = len(xs)
    shapes = [tuple(x.shape[1:] if scatter else x.shape) for x in xs]

    def body(*refs):
        x_refs, land_refs = refs[:n], refs[n:2 * n]
        send_sems, recv_sems = refs[2 * n], refs[2 * n + 1]
        token, own_sems = refs[4 * n + 2], refs[4 * n + 3]
        me, pairs = _exchange_copies(x_refs, land_refs, send_sems, recv_sems, scatter)
        own = [pltpu.make_async_copy(x_refs[a].at[me] if scatter else x_refs[a], land_refs[a].at[me], own_sems.at[a])
               for a in range(n)]
        for cp in own:
            cp.start()
        for out, _ in pairs:
            out.start()
        for cp in own:
            cp.wait()
        token[...] = jnp.zeros(token.shape, token.dtype)

    lands = [pltpu.with_memory_space_constraint(lax.empty((N_DEV,) + sh, x.dtype), pltpu.HBM) for sh, x in zip(shapes, xs)]
    res = pl.pallas_call(
        body, name=name,
        out_shape=(pltpu.SemaphoreType.DMA((n * (N_DEV - 1),)), pltpu.SemaphoreType.DMA((n * (N_DEV - 1),)),
                   *[pltpu.HBM(x.shape, x.dtype) for x in xs], *[pltpu.HBM(l.shape, l.dtype) for l in lands],
                   jax.ShapeDtypeStruct((8, 128), F32)),
        in_specs=[_HBM] * (2 * n),
        out_specs=(_SEM, _SEM, *[_HBM] * (2 * n), pl.BlockSpec(memory_space=pltpu.VMEM)),
        input_output_aliases={i: 2 + i for i in range(2 * n)},
        scratch_shapes=[pltpu.SemaphoreType.DMA((n,))],
        compiler_params=pltpu.CompilerParams(has_side_effects=pltpu.SideEffectType.DATAFLOW_SIDE_EFFECTING),
    )(*[pltpu.with_memory_space_constraint(x, pltpu.HBM) for x in xs], *lands)
    return dict(send=res[0], recv=res[1], xs=list(res[2:2 + n]), lands=list(res[2 + n:2 + 2 * n]), token=res[2 + 2 * n])


def _exchange_wait(name, handle, after, scatter):
    n = len(handle['xs'])

    def body(*refs):
        x_refs, land_refs = refs[:n], refs[n:2 * n]
        send_sems, recv_sems = refs[2 * n], refs[2 * n + 1]
        _, pairs = _exchange_copies(x_refs, land_refs, send_sems, recv_sems, scatter)
        for out, arrival in pairs:
            out.wait_send()
            arrival.wait_recv()

    res = pl.pallas_call(
        body, name=name,
        out_shape=tuple(pltpu.HBM(a.shape, a.dtype) for a in handle['xs'] + handle['lands']),
        in_specs=[_HBM] * (2 * n) + [_SEM, _SEM, pl.BlockSpec(memory_space=pl.ANY)],
        out_specs=tuple([_HBM] * (2 * n)),
        input_output_aliases={i: i for i in range(2 * n)},
        compiler_params=pltpu.CompilerParams(has_side_effects=pltpu.SideEffectType.DATAFLOW_SIDE_EFFECTING),
    )(*handle['xs'], *handle['lands'], handle['send'], handle['recv'], after)
    return list(res[n:])


def _relu2(a):
    r = jnp.maximum(a, 0.0)
    return r * r


def _relu2_grad(acc, a):
    return acc * (2.0 * jnp.maximum(a, 0.0))


def _local_step(x0, tgt, mod, wcat, late_weights, send_grads, conv_w, conv_b, dt_bias, a_log, d_skip, ssm_norm_w, f_bias,
                attn_norm_w, ln1_g, ln1_b, ln2_g, ln2_b):
    ff_w = DFF // N_DEV
    s = x0.shape[0]
    tm = min(512, s)
    ts = min(1024, s)
    sh1, sc1, g1, sh2, sc2, g2 = [mod[:, i * D:(i + 1) * D] for i in range(6)]
    zero = jnp.zeros((1, 128 - 2 * NH), F32)
    bias128 = jnp.concatenate([dt_bias, f_bias, zero], axis=1)
    alog128 = jnp.concatenate([a_log, jnp.zeros((1, 128 - NH), F32)], axis=1)
    dskip_x = jnp.repeat(d_skip, HD, axis=1)
    w_xs, w_bc, b_xs, b_bc = conv_w[:, :D], conv_w[:, D:], conv_b[:, :D], conv_b[:, D:]

    p = _mm_nn("in_proj", x0, wcat, tm=tm, tn=640, tk=D, out_dtype=F32, pro=_modulate, aux=(sc1, sh1))
    xs_a, bc_a = _conv_fwd(p, w_xs, b_xs, w_bc, b_bc, s)
    y_ssd, states = _ssd_fwd(xs_a, bc_a, p, bias128, alog128, dskip_x, s)
    cum = _cum_fwd(p, bias128, s)
    att, lse = _attn_fwd(p, cum, s)
    wout, w1s, w2 = late_weights(lse)
    ymix = _mix_norm(y_ssd, p, att, ssm_norm_w, attn_norm_w, s)
    y = _mm_nn("out_proj", ymix, wout, tm=tm, tn=512, tk=2 * D, out_dtype=F32)
    x1, h2 = _ln1(x0, y, g1, ln1_g, ln1_b, sc2, sh2, s)
    a1 = _mm_nn("ff_in", h2, w1s, tm=tm, tn=ff_w, tk=D, out_dtype=F32)
    ff = _mm_nn("ff_out", a1, w2, tm=tm, tn=512, tk=1024, out_dtype=F32, pro=_relu2)
    du2, dff, sq_err, d_ln2_g, d_ln2_b, d_g2 = _ln2_loss(x1, ff, tgt, g2, ln2_g, ln2_b, s)

    da1 = _mm_nt("d_ff_hidden", [(dff, D, 0)], [(w2, D, 0)], n=DFF, tm=tm, tn=512, out_dtype=BF16, epi=_relu2_grad,
                 epi_aux=(a1,))
    d_w2 = _mm_tn("d_w_ff_out", a1, dff, tm=1024, tn=512, ts=ts, pro=_relu2)
    d_w1s = _mm_tn("d_w_ff_in", h2, da1, tm=1024, tn=ff_w, ts=ts, col_shards=True)
    dh2 = _mm_nt("d_ff_input", [(da1, ff_w, k) for k in range(N_DEV)], [(w1s, ff_w, k) for k in range(N_DEV)], n=D,
                 tm=min(256, s), tn=512, out_dtype=F32)
    sent = send_grads("ff", [d_w1s, d_w2.reshape(N_DEV, -1, D)])
    du1, dy, d_sc2, d_sh2, d_ln1_g, d_ln1_b, d_g1 = _ln1_bwd(dh2, du2, x0, y, g1, ln1_g, ln1_b, _after(sc2, sent), s)

    dmix = _mm_nt("d_mix", [(dy, D, 0)], [(wout, D, 0)], n=2 * D, tm=tm, tn=512, out_dtype=F32)
    d_wout = _mm_tn("d_w_out", ymix, dy, tm=1024, tn=512, ts=ts)
    sent = send_grads("out", [d_wout.reshape(N_DEV, -1, D)])
    dy_ssd, dz, datt, d_ssm_w, d_attn_w = _mix_norm_bwd(dmix, y_ssd, p, att, _after(ssm_norm_w, sent), attn_norm_w, s)
    dq, dk, dv, dcs, drs = _attn_bwd(p, cum, att, lse, datt, s)
    dxs_a, dbc_a, ddt_raw, d_alog, d_dskip = _ssd_bwd(dy_ssd, xs_a, bc_a, p, states, bias128, alog128, dskip_x, s)
    dcum = jnp.pad(drs[:, :2, :].reshape(NH, s).T - dcs[:, ::HD], ((0, 0), (NH, 128 - 2 * NH)))
    ddtf, _, d_bias = _cum_bwd(dcum, ddt_raw, p, bias128, s)
    dxs, dbc, d_wc_xs, d_bc_xs, d_wc_bc, d_bc_bc = _conv_bwd(dxs_a, dbc_a, p, w_xs, b_xs, w_bc, b_bc, s)

    segs = [(dz, OFF_Z, D), (dxs, OFF_XS, D), (dq, OFF_Q, D), (dk, OFF_K, D), (dv, OFF_V, D), (dbc, OFF_BC, 512),
            (ddtf, OFF_DTF, 128)]
    d_z, d_xs, d_q, d_k, d_v, d_bcw, d_dtf = [
        _mm_tn("d_w_in_%d" % i, x0, a, tm=1024, tn=min(w, 512), ts=ts, pro=_modulate, aux=(sc1, sh1))
        for i, (a, _, w) in enumerate(segs)]
    d_w_in = dict(z=d_z, xs=d_xs, bc=d_bcw, dt=d_dtf[:, :NH], q=d_q, k=d_k, v=d_v, f=d_dtf[:, NH:2 * NH])
    sent = send_grads("in", [_shard_w_in_grad(d_w_in)])
    segs[-1] = (_after(ddtf, sent), OFF_DTF, 128)
    dh1 = _mm_nt("d_h1", [(a, w, 0) for a, _, w in segs], [(wcat, w, off // w) for _, off, w in segs], n=D,
                 tm=min(256, s), tn=512, out_dtype=F32)
    grad_x, d_sc1, d_sh1 = _input_grad(dh1, du1, x0, sc1, s)

    return dict(
        loss=(0.5 / D) * jnp.sum(sq_err), grad_x=grad_x,
        d_mod=jnp.concatenate([d_sh1, d_sc1, d_g1, d_sh2, d_sc2, d_g2], axis=1),
        d_conv_w=jnp.concatenate([d_wc_xs[:4], d_wc_bc[:4]], axis=1), d_conv_b=jnp.concatenate([d_bc_xs, d_bc_bc], axis=1),
        d_ssm_norm_w=d_ssm_w, d_attn_norm_w=d_attn_w, d_ln1_g=d_ln1_g, d_ln1_b=d_ln1_b, d_ln2_g=d_ln2_g, d_ln2_b=d_ln2_b,
        d_gate_bias=d_bias, d_a_log=d_alog, d_d_skip=d_dskip)


W_IN_SEGS = [('z', W_Z, D), ('xs', W_XS, D), ('bc', W_BC, 512), ('dt', W_DT, NH), ('q', W_Q, D), ('k', W_K, D),
             ('v', W_V, D), ('f', W_F, NH)]
SHARD_W = IN_COLS // N_DEV


def _pack_w_in(shards):
    def cols(lo, hi):
        pieces = []
        while lo < hi:
            dev = lo // SHARD_W
            end = min(hi, (dev + 1) * SHARD_W)
            pieces.append(shards[dev][:, lo - dev * SHARD_W:end - dev * SHARD_W])
            lo = end
        return pieces

    seg = {n: cols(off, off + w) for n, off, w in W_IN_SEGS}
    pieces = seg['z'] + seg['xs'] + seg['q'] + seg['k'] + seg['v'] + seg['bc'] + seg['dt'] + seg['f']
    return jnp.concatenate(pieces + [jnp.zeros((D, 128 - 2 * NH), shards.dtype)], axis=1)


def _shard_w_in_grad(d_w_in):
    blocks = []
    for dev in range(N_DEV):
        lo, hi = dev * SHARD_W, (dev + 1) * SHARD_W
        pieces = [d_w_in[n][:, max(lo, off) - off:min(hi, off + w) - off] for n, off, w in W_IN_SEGS
                  if max(lo, off) < min(hi, off + w)]
        blocks.append(jnp.concatenate(pieces, axis=1))
    return jnp.stack(blocks, axis=0)


WEIGHTS = ['w_ada', 'b_ada', 'w_in', 'conv_w', 'conv_b', 'dt_bias', 'a_log', 'd_skip', 'ssm_norm_w', 'f_bias',
           'attn_norm_w', 'w_out', 'ln1_g', 'ln1_b', 'w_ff_in', 'w_ff_out', 'ln2_g', 'ln2_b']
BIG = ['w_in', 'w_out', 'w_ff_in', 'w_ff_out']
SMALL = ['b_ada', 'conv_b', 'ssm_norm_w', 'attn_norm_w', 'ln1_g', 'ln1_b', 'ln2_g', 'ln2_b', 'dt_bias', 'a_log', 'd_skip',
         'f_bias', 'conv_w']


def _pad_lanes(v, n=128):
    return jnp.pad(v, ((0, 0), (0, n - v.shape[1])))


def _small_block(vals):
    rows = [_pad_lanes(vals[n].reshape(1, -1), -(-vals[n].size // 128) * 128).reshape(-1, 128) for n in SMALL]
    block = jnp.concatenate(rows, axis=0)
    return jnp.pad(block, ((0, 120 - block.shape[0]), (0, 0)))


def _small_unblock(block, like):
    out, r = {}, 0
    for n in SMALL:
        size = like[n].size
        nr = -(-size // 128)
        out[n] = block[r:r + nr].reshape(-1)[:size].reshape(like[n].shape)
        r += nr
    return out


def kernel(x, c, w_ada, b_ada, w_in, conv_w, conv_b, dt_bias, a_log, d_skip, ssm_norm_w, f_bias, attn_norm_w, w_out, ln1_g, ln1_b, w_ff_in, w_ff_out, ln2_g, ln2_b, loss_target, m_w_ada, m_b_ada, m_w_in, m_conv_w, m_conv_b, m_dt_bias, m_a_log, m_d_skip, m_ssm_norm_w, m_f_bias, m_attn_norm_w, m_w_out, m_ln1_g, m_ln1_b, m_w_ff_in, m_w_ff_out, m_ln2_g, m_ln2_b, v_w_ada, v_b_ada, v_w_in, v_conv_w, v_conv_b, v_dt_bias, v_a_log, v_d_skip, v_ssm_norm_w, v_f_bias, v_attn_norm_w, v_w_out, v_ln1_g, v_ln1_b, v_w_ff_in, v_w_ff_out, v_ln2_g, v_ln2_b):
    args = dict(locals())
    w = {n: args[n] for n in WEIGHTS}
    m = {n: args['m_' + n] for n in WEIGHTS}
    v = {n: args['v_' + n] for n in WEIGHTS}
    me = 4 * lax.axis_index("x") + 2 * lax.axis_index("y") + lax.axis_index("c")
    ada_cols = 6 * D // N_DEV
    conv_cols = conv_w.shape[2]

    c_all, conv_all = _exchange("gather_cond", [c, conv_w[0]], False)
    c_all = c_all.reshape(N_DEV, D)
    conv_w_full = conv_all.transpose(1, 0, 2).reshape(4, N_DEV * conv_cols)
    b_shard = lax.dynamic_slice(b_ada, (0, me * ada_cols), (1, ada_cols))
    mod_all, = _exchange("gather_mod", [_ada_mod(c_all, w_ada[0], b_shard)], False)
    mod = lax.dynamic_index_in_dim(mod_all, me, axis=1, keepdims=False).reshape(1, 6 * D)

    win_s, = _exchange("gather_w_in", [_after(w_in[0].astype(BF16), mod * 0)], False)
    first_done = win_s[0, 0:1, 0:1] * 0
    rest = _exchange_async("gather_rest", [_after(w[n][0].astype(BF16), first_done) for n in BIG[1:]], False, 1)

    def late_weights(after):
        wout_s, w1s, w2_s = rest()
        return wout_s.reshape(2 * D, D), w1s, w2_s.reshape(DFF, D)

    sends = {}

    def send_grads(tag, blocks):
        sends[tag] = _exchange_async("scatter_" + tag, blocks, True, {'ff': 2, 'out': 3, 'in': 4}[tag])
        return sum(b.reshape(-1)[0].astype(F32) * 0 for b in blocks)

    out = _local_step(x[0], loss_target[0], mod, _pack_w_in(win_s), late_weights, send_grads,
                      conv_w_full, conv_b, dt_bias, a_log, d_skip, ssm_norm_w, f_bias, attn_norm_w, ln1_g, ln1_b, ln2_g, ln2_b)
    g_ff_in, g_ff_out = sends['ff']()
    g_out, = sends['out']()
    g_in, = sends['in']()
    g_parts = [g_in, g_out, g_ff_in, g_ff_out]
    big = {n: _adamw("adamw_" + n, w[n][0], g, m[n][0], v[n][0], tr=256, slots=True) for n, g in zip(BIG, g_parts)}

    small = jnp.concatenate(
        [out['d_mod'], out['d_conv_w'].reshape(1, -1), out['d_conv_b'], out['d_ssm_norm_w'], out['d_attn_norm_w'],
         out['d_ln1_g'], out['d_ln1_b'], out['d_ln2_g'], out['d_ln2_b'], out['d_gate_bias'], out['d_a_log'],
         out['d_d_skip'], jnp.zeros((1, 128), F32)], axis=1).reshape(-1, 128)
    small_all, = _exchange("gather_small", [small], False)
    ssum = _sum_slots("sum_small", small_all)
    dmod_all = small_all[:, :6 * D // 128].reshape(N_DEV, 6 * D)
    g_w_ada, g_b_ada = _ada_grad(c_all, lax.dynamic_slice(dmod_all, (0, me * ada_cols), (N_DEV, ada_cols)), dmod_all)
    rows = lambda a, b: ssum[a:b].reshape(1, -1)
    g_conv_w = lax.dynamic_slice(ssum[48:96].reshape(4, N_DEV * conv_cols), (0, me * conv_cols), (4, conv_cols))
    g_small = dict(b_ada=g_b_ada, conv_w=g_conv_w[None], conv_b=rows(96, 108), ssm_norm_w=rows(108, 116),
                   attn_norm_w=rows(116, 124), ln1_g=rows(124, 132), ln1_b=rows(132, 140), ln2_g=rows(140, 148),
                   ln2_b=rows(148, 156), dt_bias=ssum[156:157, :NH], f_bias=ssum[156:157, NH:2 * NH],
                   a_log=ssum[157:158, :NH], d_skip=ssum[158:159, :NH])
    sm = _adamw("adamw_small", _small_block(w), _small_block(g_small), _small_block(m), _small_block(v), tr=120, slots=False)
    ada = _adamw("adamw_ada", w_ada[0], g_w_ada, m_w_ada[0], v_w_ada[0], tr=256, slots=False)

    results = []
    for k in range(4):
        vals = _small_unblock(sm[k], w)
        vals['w_ada'] = ada[k][None]
        for n in BIG:
            vals[n] = big[n][k][None]
        results.append(vals)
    loss = lax.psum(out['loss'], ("x", "y", "c"))
    return (loss, out['grad_x'][None], *[res[n] for res in results for n in WEIGHTS])
```

```python
import functools

import jax
import jax.numpy as jnp
from jax import lax
from jax.experimental import pallas as pl
from jax.experimental.pallas import tpu as pltpu
from jax.experimental.pallas import tpu_sc as plsc

F32, BF16 = jnp.float32, jnp.bfloat16

N_DEV = 8
D = 1024
NH, HD = 16, 64
NSTATE = 128
CHUNK = 128
HG = 8
DFF = 4096
ALPHA = 2.0 ** 0.25
EPS = 1e-5
ATT_SCALE = HD ** -0.5

OFF_Z, OFF_XS, OFF_Q, OFF_K, OFF_V, OFF_BC, OFF_DTF = 0, 1024, 2048, 3072, 4096, 5120, 5632
PCOLS = 5760
W_Z, W_XS, W_BC, W_DT, W_Q, W_K, W_V, W_F = 0, 1024, 2048, 2560, 2576, 3600, 4624, 5648
IN_COLS = 5664

ADAM_LR, ADAM_B1, ADAM_B2, ADAM_EPS, ADAM_WD, ADAM_STEP = 0.001, 0.9, 0.999, 1e-08, 0.01, 10

VMEM_LIMIT = 56 << 20

NN = (((1,), (0,)), ((), ()))
NT = (((1,), (1,)), ((), ()))
TN = (((0,), (0,)), ((), ()))


def _dot(a, b, dims=NN):
    return lax.dot_general(a, b, dims, preferred_element_type=F32)


def _bdot(a, b, dims=NN):
    return _dot(a.astype(BF16), b.astype(BF16), dims)


def _split3(v):
    parts, rest = [], v
    for _ in range(3):
        p = rest.astype(BF16)
        parts.append(p)
        rest = rest - p.astype(F32)
    return parts


def _sel_left(m01, v):
    return sum(_dot(m01, p) for p in _split3(v))


def _sel_right(v, m01, dims=NN):
    return sum(_dot(p, m01, dims) for p in _split3(v))


def _iota(shape, dim):
    return lax.broadcasted_iota(jnp.int32, shape, dim)


def _tri_lower(n):
    return (_iota((n, n), 1) <= _iota((n, n), 0)).astype(BF16)


def _tri_upper(n):
    return (_iota((n, n), 1) >= _iota((n, n), 0)).astype(BF16)


def _head_expand():
    return (lax.shift_right_logical(_iota((128, D), 1), 6) == _iota((128, D), 0)).astype(BF16)


def _head_reduce():
    return (lax.shift_right_logical(_iota((D, 128), 0), 6) == _iota((D, 128), 1)).astype(BF16)


def _sigmoid(x):
    return 1.0 / (1.0 + jnp.exp(-x))


def _silu(x):
    return x * _sigmoid(x)


def _dsilu(x):
    s = _sigmoid(x)
    return s * (1.0 + x * (1.0 - s))


def _softplus(x):
    return jnp.maximum(x, 0.0) + jnp.log(1.0 + jnp.exp(-jnp.abs(x)))


def _log_sigmoid(x):
    return jnp.minimum(x, 0.0) - jnp.log(1.0 + jnp.exp(-jnp.abs(x)))


def _params(sem):
    return pltpu.CompilerParams(dimension_semantics=sem, vmem_limit_bytes=VMEM_LIMIT)


def _mm_nn(name, a, b, *, tm, tn, tk, out_dtype, pro=None, aux=()):
    m, k_all = a.shape
    b_sharded = b.ndim == 3
    n = b.shape[0] * b.shape[2] if b_sharded else b.shape[1]
    assert not b_sharded or tn == b.shape[2]
    nk = k_all // tk
    n_aux = len(aux)
    b_spec = (pl.BlockSpec((None, tk, tn), lambda i, j, k: (j, k, 0)) if b_sharded
              else pl.BlockSpec((tk, tn), lambda i, j, k: (k, j)))

    def body(a_ref, b_ref, *rest):
        aux_refs, o_ref = rest[:n_aux], rest[n_aux]
        at = a_ref[...]
        if pro is not None:
            at = pro(at, *[r[...] for r in aux_refs])
        part = _bdot(at, b_ref[...])
        if nk == 1:
            o_ref[...] = part.astype(out_dtype)
            return
        acc_ref = rest[n_aux + 1]
        kk = pl.program_id(2)

        @pl.when(kk == 0)
        def _():
            acc_ref[...] = part

        @pl.when(kk > 0)
        def _():
            acc_ref[...] += part

        @pl.when(kk == nk - 1)
        def _():
            o_ref[...] = acc_ref[...].astype(out_dtype)

    return pl.pallas_call(
        body, name=name,
        grid=(m // tm, n // tn, nk),
        in_specs=[pl.BlockSpec((tm, tk), lambda i, j, k: (i, k)), b_spec]
        + [pl.BlockSpec((1, tk), lambda i, j, k: (0, k)) for _ in aux],
        out_specs=pl.BlockSpec((tm, tn), lambda i, j, k: (i, j)),
        out_shape=jax.ShapeDtypeStruct((m, n), out_dtype),
        scratch_shapes=[] if nk == 1 else [pltpu.VMEM((tm, tn), F32)],
        compiler_params=_params(("parallel", "parallel", "arbitrary")),
    )(a, b, *aux)


def _mm_nt(name, a_list, b_list, *, n, tm, tn, out_dtype, epi=None, epi_aux=()):
    m = a_list[0][0].shape[0]
    n_op = len(a_list)
    n_epi = len(epi_aux)

    def body(*refs):
        a_refs, b_refs = refs[:n_op], refs[n_op:2 * n_op]
        e_refs, o_ref = refs[2 * n_op:2 * n_op + n_epi], refs[2 * n_op + n_epi]
        acc = None
        for a_ref, b_ref in zip(a_refs, b_refs):
            part = _bdot(a_ref[...], b_ref[...], NT)
            acc = part if acc is None else acc + part
        if epi is not None:
            acc = epi(acc, *[r[...] for r in e_refs])
        o_ref[...] = acc.astype(out_dtype)

    in_specs = [pl.BlockSpec((tm, w), functools.partial(lambda i, j, cb: (i, cb), cb=cb)) for (_, w, cb) in a_list]
    for (b, w, cb) in b_list:
        if b.ndim == 3:
            in_specs.append(pl.BlockSpec((None, tn, w), functools.partial(lambda i, j, cb: (cb, j, 0), cb=cb)))
        else:
            in_specs.append(pl.BlockSpec((tn, w), functools.partial(lambda i, j, cb: (j, cb), cb=cb)))
    in_specs += [pl.BlockSpec((tm, tn), lambda i, j: (i, j)) for _ in epi_aux]
    return pl.pallas_call(
        body, name=name,
        grid=(m // tm, n // tn),
        in_specs=in_specs,
        out_specs=pl.BlockSpec((tm, tn), lambda i, j: (i, j)),
        out_shape=jax.ShapeDtypeStruct((m, n), out_dtype),
        compiler_params=_params(("parallel", "parallel")),
    )(*[a for (a, _, _) in a_list], *[b for (b, _, _) in b_list], *epi_aux)


def _mm_tn(name, a, b, *, tm, tn, ts, pro=None, aux=(), col_shards=False):
    s_all, ka = a.shape
    nb = b.shape[1]
    n_aux = len(aux)
    ns = s_all // ts
    assert not col_shards or tn == nb // N_DEV

    def body(a_ref, b_ref, *rest):
        aux_refs, o_ref, acc_ref = rest[:n_aux], rest[n_aux], rest[n_aux + 1]
        at = a_ref[...]
        if pro is not None:
            at = pro(at, *[r[...] for r in aux_refs])
        part = _bdot(at, b_ref[...], TN)
        ss = pl.program_id(2)

        @pl.when(ss == 0)
        def _():
            acc_ref[...] = part

        @pl.when(ss > 0)
        def _():
            acc_ref[...] += part

        @pl.when(ss == ns - 1)
        def _():
            o_ref[...] = acc_ref[...].astype(BF16)

    if col_shards:
        out_spec = pl.BlockSpec((None, tm, tn), lambda i, j, s: (j, i, 0))
        out_shape = jax.ShapeDtypeStruct((N_DEV, ka, tn), BF16)
    else:
        out_spec = pl.BlockSpec((tm, tn), lambda i, j, s: (i, j))
        out_shape = jax.ShapeDtypeStruct((ka, nb), BF16)
    return pl.pallas_call(
        body, name=name,
        grid=(ka // tm, nb // tn, ns),
        in_specs=[pl.BlockSpec((ts, tm), lambda i, j, s: (s, i)),
                  pl.BlockSpec((ts, tn), lambda i, j, s: (s, j))]
        + [pl.BlockSpec((1, tm), lambda i, j, s: (0, i)) for _ in aux],
        out_specs=out_spec, out_shape=out_shape,
        scratch_shapes=[pltpu.VMEM((tm, tn), F32)],
        compiler_params=_params(("parallel", "parallel", "arbitrary")),
    )(a, b, *aux)


def _rowk(name, fn, n_rows, tr, rows, fulls, outs, accs, reverse=False):
    n = n_rows // tr
    n_row, n_full, n_out, n_acc = len(rows), len(fulls), len(outs), len(accs)

    def pos(i):
        return (n - 1 - i) if reverse else i

    def body(*refs):
        row_refs = refs[:n_row]
        full_refs = refs[n_row:n_row + n_full]
        out_refs = refs[n_row + n_full:n_row + n_full + n_out]
        acc_refs = refs[n_row + n_full + n_out:]
        i = pl.program_id(0)

        @pl.when(i == 0)
        def _():
            for r in acc_refs:
                r[...] = jnp.zeros(r.shape, r.dtype)

        res = fn(pos(i), *[r[...] for r in row_refs], *[r[...] for r in full_refs], *[r[...] for r in acc_refs])
        for r, v in zip(out_refs + acc_refs, res):
            r[...] = v.astype(r.dtype)

    def row_map(i, cb, shift):
        return (jnp.clip(pos(i) + shift, 0, n - 1), cb)

    in_specs = [pl.BlockSpec((tr, w), functools.partial(row_map, cb=cb, shift=sh)) for (_, w, cb, sh) in rows]
    in_specs += [pl.BlockSpec(f.shape, functools.partial(lambda i, nd: (0,) * nd, nd=f.ndim)) for f in fulls]
    out_specs = [pl.BlockSpec((tr, w), lambda i: (pos(i), 0)) for (w, _) in outs]
    out_specs += [pl.BlockSpec((r, w), lambda i: (0, 0)) for (r, w) in accs]
    out_shape = [jax.ShapeDtypeStruct((n_rows, w), dt) for (w, dt) in outs]
    out_shape += [jax.ShapeDtypeStruct((r, w), F32) for (r, w) in accs]
    return pl.pallas_call(
        body, name=name, grid=(n,), in_specs=in_specs, out_specs=out_specs, out_shape=out_shape,
        compiler_params=_params(("arbitrary",)),
    )(*[a for (a, _, _, _) in rows], *fulls)


def _colsum(x):
    return jnp.sum(x, axis=0, keepdims=True)


def _mean(x):
    return jnp.mean(x, axis=-1, keepdims=True)


def _modulate(x, sc, sh):
    return x * (1.0 + sc) + sh


def _shift_down(cur, prev, j):
    row = _iota(cur.shape, 0)
    return jnp.where(row < j, pltpu.roll(prev, j, 0), pltpu.roll(cur, j, 0))


def _shift_up(cur, nxt, j):
    tr = cur.shape[0]
    row = _iota(cur.shape, 0)
    return jnp.where(row < tr - j, pltpu.roll(cur, tr - j, 0), pltpu.roll(nxt, tr - j, 0))


def _conv(cur, prev, w, b):
    out = cur * w[3:4] + b
    for j in (1, 2, 3):
        out = out + _shift_down(cur, prev, j) * w[3 - j:4 - j]
    return out


def _conv_fwd(p, w_xs, b_xs, w_bc, b_bc, s):
    def fn(pos, xs, xs_prev, bc, bc_prev, w_xs, b_xs, w_bc, b_bc):
        first = pos == 0
        xs_prev = jnp.where(first, 0.0, xs_prev)
        bc_prev = jnp.where(first, 0.0, bc_prev)
        return _silu(_conv(xs, xs_prev, w_xs, b_xs)), _silu(_conv(bc, bc_prev, w_bc, b_bc))

    return _rowk("conv_fwd", fn, s, 256,
                 [(p, D, OFF_XS // D, 0), (p, D, OFF_XS // D, -1), (p, 512, OFF_BC // 512, 0), (p, 512, OFF_BC // 512, -1)],
                 [w_xs, b_xs, w_bc, b_bc], [(D, F32), (512, F32)], [])


def _conv_bwd(dxs_a, dbc_a, p, w_xs, b_xs, w_bc, b_bc, s):
    tr = 256
    n = s // tr

    def fn(pos, da1, da1n, x1, x1p, x1n, da2, da2n, x2, x2p, x2n, w1, b1, w2, b2, aw1, ab1, aw2, ab2):
        dx1, dw1, db1 = _conv_bwd_fn(pos, n, da1, da1n, x1, x1p, x1n, w1, b1)
        dx2, dw2, db2 = _conv_bwd_fn(pos, n, da2, da2n, x2, x2p, x2n, w2, b2)
        return dx1, dx2, aw1 + dw1, ab1 + db1, aw2 + dw2, ab2 + db2

    cx, cb = OFF_XS // D, OFF_BC // 512
    return _rowk("conv_bwd", fn, s, tr,
                 [(dxs_a, D, 0, 0), (dxs_a, D, 0, 1), (p, D, cx, 0), (p, D, cx, -1), (p, D, cx, 1),
                  (dbc_a, 512, 0, 0), (dbc_a, 512, 0, 1), (p, 512, cb, 0), (p, 512, cb, -1), (p, 512, cb, 1)],
                 [w_xs, b_xs, w_bc, b_bc], [(D, BF16), (512, BF16)], [(8, D), (1, D), (8, 512), (1, 512)])


def _conv_bwd_fn(pos, n, da, da_next, x, x_prev, x_next, w, b):
    first, last = pos == 0, pos == n - 1
    x_prev = jnp.where(first, 0.0, x_prev)
    dc = da * _dsilu(_conv(x, x_prev, w, b))
    dc_next = jnp.where(last, 0.0, da_next * _dsilu(_conv(x_next, x, w, b)))
    dx = dc * w[3:4]
    dws = [None] * 4
    dws[3] = _colsum(dc * x)
    for j in (1, 2, 3):
        dx = dx + _shift_up(dc, dc_next, j) * w[3 - j:4 - j]
        dws[3 - j] = _colsum(dc * _shift_down(x, x_prev, j))
    row = _iota((8, x.shape[1]), 0)
    dw = jnp.zeros((8, x.shape[1]), F32)
    for k in range(4):
        dw = jnp.where(row == k, dws[k], dw)
    return dx, dw, _colsum(dc)


def _ssd_gates(dtf, bias, a_log):
    lane = _iota(dtf.shape, 1)
    head = lane < NH
    dt = jnp.where(head, _softplus(dtf + bias), 0.0)
    a_neg = jnp.where(_iota(a_log.shape, 1) < NH, -jnp.exp(a_log), 0.0)
    a = dt * a_neg
    cs = _sel_left(_tri_lower(CHUNK), a)
    return dt, a_neg, cs


def _decay_mask(cs_ref, cst_ref, h):
    diff = cs_ref[:, h:h + 1] - cst_ref[h:h + 1, :]
    low = _iota((CHUNK, CHUNK), 1) <= _iota((CHUNK, CHUNK), 0)
    return jnp.where(low, jnp.exp(jnp.minimum(diff, 0.0)), 0.0)


def _ssd_fwd(xs_a, bc_a, p, bias128, alog128, dskip_x, s):
    nc = s // CHUNK
    t = CHUNK

    def body(xs_ref, bc_ref, dtf_ref, bias_ref, alog_ref, dsk_ref, y_ref, st_ref,
             state, x_sc, xw_sc, cs_sc, cst_sc, yd_sc):
        c = pl.program_id(0)

        @pl.when(c == 0)
        def _():
            state[...] = jnp.zeros(state.shape, F32)

        dt, _, cs = _ssd_gates(dtf_ref[...], bias_ref[...], alog_ref[...])
        cs_sc[...] = cs
        cst_sc[...] = cs.T
        cs_last = cs[t - 1:t, :]
        expand = _head_expand()
        ex = _sel_right(jnp.concatenate([dt, jnp.exp(cs), jnp.exp(cs_last - cs)], axis=0), expand)
        dt_x, eo_x, we_x = ex[0:t], ex[t:2 * t], ex[2 * t:3 * t]
        g_x = _sel_right(jnp.broadcast_to(jnp.exp(cs_last), (8, 128)), expand)[0:1]
        xs = xs_ref[...]
        x = xs * dt_x
        x_sc[...] = x.astype(BF16)
        xw_sc[...] = (x * we_x).astype(BF16)
        prev = state[...]
        st_ref[0] = prev
        prev_b = prev.astype(BF16)
        for g in range(2):
            cols = slice(g * 512, (g + 1) * 512)
            b_g = bc_ref[:, g * 128:(g + 1) * 128].astype(BF16)
            c_g = bc_ref[:, 256 + g * 128:256 + (g + 1) * 128].astype(BF16)
            gmat = _dot(c_g, b_g, NT)
            y_off = _dot(c_g, prev_b[:, cols]) * eo_x[:, cols]
            s_loc = _dot(b_g, xw_sc[:, cols], TN)
            state[:, cols] = g_x[:, cols] * prev[:, cols] + s_loc
            for e in range(HG):
                h = g * HG + e
                m = gmat * _decay_mask(cs_sc, cst_sc, h)
                yd_sc[:, h * HD:(h + 1) * HD] = _dot(m.astype(BF16), x_sc[:, h * HD:(h + 1) * HD])
            y_ref[:, cols] = yd_sc[:, cols] + y_off + dsk_ref[:, cols] * xs[:, cols]

    return pl.pallas_call(
        body, name="ssd_fwd", grid=(nc,),
        in_specs=[pl.BlockSpec((t, D), lambda c: (c, 0)),
                  pl.BlockSpec((t, 512), lambda c: (c, 0)),
                  pl.BlockSpec((t, 128), lambda c: (c, OFF_DTF // 128)),
                  pl.BlockSpec((1, 128), lambda c: (0, 0)),
                  pl.BlockSpec((1, 128), lambda c: (0, 0)),
                  pl.BlockSpec((1, D), lambda c: (0, 0))],
        out_specs=[pl.BlockSpec((t, D), lambda c: (c, 0)),
                   pl.BlockSpec((1, NSTATE, D), lambda c: (c, 0, 0))],
        out_shape=[jax.ShapeDtypeStruct((s, D), F32), jax.ShapeDtypeStruct((nc, NSTATE, D), F32)],
        scratch_shapes=[pltpu.VMEM((NSTATE, D), F32), pltpu.VMEM((t, D), BF16), pltpu.VMEM((t, D), BF16),
                        pltpu.VMEM((t, 128), F32), pltpu.VMEM((128, t), F32), pltpu.VMEM((t, D), F32)],
        compiler_params=_params(("arbitrary",)),
    )(xs_a, bc_a, p, bias128, alog128, dskip_x)


def _ssd_bwd(dy, xs_a, bc_a, p, states, bias128, alog128, dskip_x, s):
    nc = s // CHUNK
    t = CHUNK

    def body(dy_ref, xs_ref, bc_ref, dtf_ref, st_ref, bias_ref, alog_ref, dsk_ref,
             dxs_ref, dbc_ref, ddt_ref, dalog_ref, dskip_ref,
             dstate, x_sc, dy_sc, dx_sc, deo_sc, dwe_sc, cs_sc, cst_sc, dcol_sc, drow_sc):
        i = pl.program_id(0)

        @pl.when(i == 0)
        def _():
            dstate[...] = jnp.zeros(dstate.shape, F32)
            dalog_ref[...] = jnp.zeros(dalog_ref.shape, F32)
            dskip_ref[...] = jnp.zeros(dskip_ref.shape, F32)

        dtf = dtf_ref[...]
        dt, a_neg, cs = _ssd_gates(dtf, bias_ref[...], alog_ref[...])
        cs_sc[...] = cs
        cst_sc[...] = cs.T
        cs_last = cs[t - 1:t, :]
        eo, we, g_end = jnp.exp(cs), jnp.exp(cs_last - cs), jnp.exp(cs_last)
        expand, reduce = _head_expand(), _head_reduce()
        ex = _sel_right(jnp.concatenate([dt, eo, we], axis=0), expand)
        dt_x, eo_x, we_x = ex[0:t], ex[t:2 * t], ex[2 * t:3 * t]
        g_x = _sel_right(jnp.broadcast_to(g_end, (8, 128)), expand)[0:1]
        xs = xs_ref[...]
        dyv = dy_ref[...]
        x = xs * dt_x
        x_sc[...] = x.astype(BF16)
        dy_sc[...] = dyv.astype(BF16)
        dyo_b = (dyv * eo_x).astype(BF16)
        xw_b = (x * we_x).astype(BF16)
        prev = st_ref[0]
        prev_b = prev.astype(BF16)
        dnext = dstate[...]
        dnext_b = dnext.astype(BF16)
        dcol_sc[...] = jnp.zeros(dcol_sc.shape, F32)
        drow_sc[...] = jnp.zeros(drow_sc.shape, F32)
        lane_row = _iota((1, 128), 1)
        sub_col = _iota((128, 1), 0)
        for g in range(2):
            cols = slice(g * 512, (g + 1) * 512)
            b_g = bc_ref[:, g * 128:(g + 1) * 128].astype(BF16)
            c_g = bc_ref[:, 256 + g * 128:256 + (g + 1) * 128].astype(BF16)
            gmat = _dot(c_g, b_g, NT)
            b_ds = _dot(b_g, dnext_b[:, cols])
            c_s = _dot(c_g, prev_b[:, cols])
            dx_sc[:, cols] = b_ds * we_x[:, cols]
            deo_sc[:, cols] = dyv[:, cols] * c_s
            dwe_sc[:, cols] = b_ds * x[:, cols]
            db = _dot(xw_b[:, cols], dnext_b[:, cols], NT)
            dc = _dot(dyo_b[:, cols], prev_b[:, cols], NT)
            dstate[:, cols] = g_x[:, cols] * dnext[:, cols] + _dot(c_g, dyo_b[:, cols], TN)
            dg = jnp.zeros((t, t), F32)
            for e in range(HG):
                h = g * HG + e
                hc = slice(h * HD, (h + 1) * HD)
                lmat = _decay_mask(cs_sc, cst_sc, h)
                m = gmat * lmat
                dx_sc[:, hc] += _dot(m.astype(BF16), dy_sc[:, hc], TN)
                dm = _dot(dy_sc[:, hc], x_sc[:, hc], NT)
                dg = dg + dm * lmat
                qm = dm * m
                dcol_sc[...] += jnp.sum(qm, axis=1, keepdims=True) * (lane_row == h).astype(F32)
                drow_sc[...] += (sub_col == h).astype(F32) * jnp.sum(qm, axis=0, keepdims=True)
            dg_b = dg.astype(BF16)
            dbc_ref[:, g * 128:(g + 1) * 128] = db + _dot(dg_b, c_g, TN)
            dbc_ref[:, 256 + g * 128:256 + (g + 1) * 128] = dc + _dot(dg_b, b_g)
        d_eo = _sel_right(deo_sc[...], reduce)
        d_we = _sel_right(dwe_sc[...], reduce)
        d_gend = _sel_right(jnp.broadcast_to(_colsum(dnext * prev), (8, D)), reduce)[0:1]
        d_cs = dcol_sc[...] - drow_sc[...].T + d_eo * eo - d_we * we
        extra = _colsum(d_we * we) + d_gend * g_end
        d_cs = d_cs + jnp.where(_iota((t, 128), 0) == t - 1, extra, 0.0)
        da = _sel_left(_tri_upper(t), d_cs)
        dx = dx_sc[...]
        ddt = _sel_right(dx * xs, reduce) + da * a_neg
        dxs_ref[...] = dx * dt_x + dsk_ref[...] * dyv
        ddt_ref[...] = jnp.where(_iota((t, 128), 1) < NH, ddt * _sigmoid(dtf + bias_ref[...]), 0.0)
        dalog_ref[...] += _colsum(da * dt) * a_neg
        dskip_ref[...] += _sel_right(jnp.broadcast_to(_colsum(dyv * xs), (8, D)), reduce)[0:1]

    rev = lambda i: nc - 1 - i
    return pl.pallas_call(
        body, name="ssd_bwd", grid=(nc,),
        in_specs=[pl.BlockSpec((t, D), lambda i: (rev(i), 0)),
                  pl.BlockSpec((t, D), lambda i: (rev(i), 0)),
                  pl.BlockSpec((t, 512), lambda i: (rev(i), 0)),
                  pl.BlockSpec((t, 128), lambda i: (rev(i), OFF_DTF // 128)),
                  pl.BlockSpec((1, NSTATE, D), lambda i: (rev(i), 0, 0)),
                  pl.BlockSpec((1, 128), lambda i: (0, 0)),
                  pl.BlockSpec((1, 128), lambda i: (0, 0)),
                  pl.BlockSpec((1, D), lambda i: (0, 0))],
        out_specs=[pl.BlockSpec((t, D), lambda i: (rev(i), 0)),
                   pl.BlockSpec((t, 512), lambda i: (rev(i), 0)),
                   pl.BlockSpec((t, 128), lambda i: (rev(i), 0)),
                   pl.BlockSpec((1, 128), lambda i: (0, 0)),
                   pl.BlockSpec((1, 128), lambda i: (0, 0))],
        out_shape=[jax.ShapeDtypeStruct((s, D), F32), jax.ShapeDtypeStruct((s, 512), F32),
                   jax.ShapeDtypeStruct((s, 128), F32), jax.ShapeDtypeStruct((1, 128), F32),
                   jax.ShapeDtypeStruct((1, 128), F32)],
        scratch_shapes=[pltpu.VMEM((NSTATE, D), F32), pltpu.VMEM((t, D), BF16), pltpu.VMEM((t, D), BF16),
                        pltpu.VMEM((t, D), F32), pltpu.VMEM((t, D), F32), pltpu.VMEM((t, D), F32),
                        pltpu.VMEM((t, 128), F32), pltpu.VMEM((128, t), F32),
                        pltpu.VMEM((t, 128), F32), pltpu.VMEM((128, t), F32)],
        compiler_params=_params(("arbitrary",)),
    )(dy, xs_a, bc_a, p, states, bias128, alog128, dskip_x)


def _gate_lanes(shape):
    lane = _iota(shape, 1)
    return (lane >= NH) & (lane < 2 * NH)


def _cum_fwd(p, bias128, s):
    tr = min(512, s)

    def body(dtf_ref, bias_ref, o_ref, carry):
        @pl.when(pl.program_id(0) == 0)
        def _():
            carry[...] = jnp.zeros(carry.shape, F32)

        lf = jnp.where(_gate_lanes((tr, 128)), _log_sigmoid(dtf_ref[...] + bias_ref[...]), 0.0)
        cum = _sel_left(_tri_lower(tr), lf) + carry[...]
        carry[...] = cum[tr - 1:tr, :]
        o_ref[...] = cum

    return pl.pallas_call(
        body, name="cum_fwd", grid=(s // tr,),
        in_specs=[pl.BlockSpec((tr, 128), lambda i: (i, OFF_DTF // 128)), pl.BlockSpec((1, 128), lambda i: (0, 0))],
        out_specs=pl.BlockSpec((tr, 128), lambda i: (i, 0)),
        out_shape=jax.ShapeDtypeStruct((s, 128), F32),
        scratch_shapes=[pltpu.VMEM((1, 128), F32)],
        compiler_params=_params(("arbitrary",)),
    )(p, bias128)


def _cum_bwd(dcum, ddt_raw, p, bias128, s):
    tr = min(512, s)

    def fn(pos, dcum, ddt, dtf, bias, carry, acc):
        suffix = _sel_left(_tri_upper(tr), dcum) + carry
        dfr = jnp.where(_gate_lanes((tr, 128)), suffix * _sigmoid(-(dtf + bias)), 0.0)
        out = ddt + dfr
        return out, suffix[0:1, :], acc + _colsum(out)

    return _rowk("cum_bwd", fn, s, tr, [(dcum, 128, 0, 0), (ddt_raw, 128, 0, 0), (p, 128, OFF_DTF // 128, 0)],
                 [bias128], [(128, BF16)], [(1, 128), (1, 128)], reverse=True)


ATT_BLOCK = 512
ATT_STRIP = 32


def _head_part(shape, h, dim):
    i = _iota(shape, dim)
    return (i >= h * HD) & (i < (h + 1) * HD)


def _k_augmented(k_blk, cum_blk, j, h):
    tk = k_blk.shape[0]
    lane = _iota((tk, 128), 1)
    col = jnp.sum(jnp.where(lane == NH + 2 * j + h, cum_blk, 0.0), axis=1, keepdims=True)
    c0, c1, c2 = [c.astype(F32) for c in _split3(-col)]
    aug = jnp.where(lane == 0, c0, jnp.where(lane == 1, c1, jnp.where(lane == 2, c2, 0.0)))
    return jnp.concatenate([jnp.where(_head_part((tk, 128), h, 1), k_blk, 0.0), aug], axis=1).astype(BF16)


def _q_augmented_t(q_blk):
    tq = q_blk.shape[0]
    ones = (_iota((128, tq), 0) < 3).astype(BF16)
    return jnp.concatenate([(q_blk * ATT_SCALE).T.astype(BF16), ones], axis=0)


def _rows01(r0, r1):
    sub = _iota((8, r0.shape[1]), 0)
    return jnp.where(sub == 0, r0, jnp.where(sub == 1, r1, 0.0))


def _fold8(x, op, cur):
    for g in range(x.shape[0] // 8):
        cur = op(cur, x[8 * g:8 * (g + 1), :])
    return cur


def _attn_fwd(p, cum, s):
    t = min(ATT_BLOCK, s)
    nq = s // t
    r = ATT_STRIP

    def body(q_ref, k_ref, v_ref, c_ref, o_ref, lse_ref, kaug_sc, vt_sc, s_sc, p_sc, m_sc, l_sc, acc_sc):
        j, qi = pl.program_id(0), pl.program_id(1)

        @pl.when(qi == 0)
        def _():
            for c in range(nq):
                rows = slice(c * t, (c + 1) * t)
                k_blk, vt = k_ref[rows, :], v_ref[rows, :].T
                for h in range(2):
                    kaug_sc[h, rows, :] = _k_augmented(k_blk, c_ref[rows, :], j, h)
                    vt_sc[h, :, rows] = jnp.where(_head_part((128, t), h, 0), vt, 0.0).astype(BF16)

        qaug_t = _q_augmented_t(q_ref[...])
        m_sc[...] = jnp.full(m_sc.shape, -1e30, F32)
        l_sc[...] = jnp.zeros(l_sc.shape, F32)
        acc_sc[...] = jnp.zeros(acc_sc.shape, F32)
        top = _iota((128, t), 0) < HD

        def block(kb, diagonal):
            kv = pl.ds(pl.multiple_of(kb * t, t), t)
            for h in range(2):
                s_sc[h] = _dot(kaug_sc[h, kv, :], qaug_t)

            def strip(i):
                return pl.ds(pl.multiple_of(i * r, r), r)

            def pass_max(i, carry):
                out = []
                for h in range(2):
                    x = s_sc[h, strip(i), :]
                    if diagonal:
                        x = jnp.where(_iota((r, t), 1) >= i * r + _iota((r, t), 0), x, -1e30)
                        s_sc[h, strip(i), :] = x
                    out.append(_fold8(x, jnp.maximum, carry[h]))
                return tuple(out)

            low = jnp.full((8, t), -1e30, F32)
            tops = lax.fori_loop(0, t // r, pass_max, (low, low), unroll=2)
            m_new, alpha = [], []
            for h in range(2):
                m_prev = m_sc[h, 0:1, :]
                m_new.append(jnp.maximum(m_prev, jnp.max(tops[h], axis=0, keepdims=True)))
                alpha.append(jnp.exp(m_prev - m_new[h]))
                m_sc[h, 0:1, :] = m_new[h]

            def pass_exp(i, carry):
                out = []
                for h in range(2):
                    pr = jnp.exp(s_sc[h, strip(i), :] - m_new[h])
                    p_sc[h, strip(i), :] = pr.astype(BF16)
                    out.append(_fold8(pr, jnp.add, carry[h]))
                return tuple(out)

            zero = jnp.zeros((8, t), F32)
            sums = lax.fori_loop(0, t // r, pass_exp, (zero, zero), unroll=2)
            for h in range(2):
                l_sc[h, 0:1, :] = alpha[h] * l_sc[h, 0:1, :] + jnp.sum(sums[h], axis=0, keepdims=True)
            acc_sc[...] = (acc_sc[...] * jnp.where(top, alpha[0], alpha[1])
                           + _dot(vt_sc[0, :, kv], p_sc[0]) + _dot(vt_sc[1, :, kv], p_sc[1]))

        def earlier(kb, carry):
            block(kb, False)
            return carry

        lax.fori_loop(0, qi, earlier, 0)
        block(qi, True)
        l0, l1 = l_sc[0, 0:1, :], l_sc[1, 0:1, :]
        o_ref[...] = (acc_sc[...] / jnp.where(top, l0, l1)).T
        lse_ref[0] = _rows01(m_sc[0, 0:1, :] + jnp.log(l0), m_sc[1, 0:1, :] + jnp.log(l1))

    return pl.pallas_call(
        body, name="attn_fwd", grid=(NH // 2, nq),
        in_specs=[pl.BlockSpec((t, 128), lambda j, qi: (qi, OFF_Q // 128 + j)),
                  pl.BlockSpec((s, 128), lambda j, qi: (0, OFF_K // 128 + j)),
                  pl.BlockSpec((s, 128), lambda j, qi: (0, OFF_V // 128 + j)),
                  pl.BlockSpec((s, 128), lambda j, qi: (0, 0))],
        out_specs=[pl.BlockSpec((t, 128), lambda j, qi: (qi, j)),
                   pl.BlockSpec((1, 8, t), lambda j, qi: (j, 0, qi))],
        out_shape=[jax.ShapeDtypeStruct((s, D), F32), jax.ShapeDtypeStruct((NH // 2, 8, s), F32)],
        scratch_shapes=[pltpu.VMEM((2, s, 256), BF16), pltpu.VMEM((2, 128, s), BF16), pltpu.VMEM((2, t, t), F32),
                        pltpu.VMEM((2, t, t), BF16), pltpu.VMEM((2, 8, t), F32), pltpu.VMEM((2, 8, t), F32),
                        pltpu.VMEM((128, t), F32)],
        compiler_params=_params(("parallel", "arbitrary")),
    )(p, p, p, cum)


def _attn_bwd(p, cum, o, lse, do, s):
    t = min(ATT_BLOCK, s)
    nq = s // t
    r = ATT_STRIP

    def body(q_ref, k_ref, v_ref, c_ref, o_ref, lse_ref, do_ref, dq_ref, dk_ref, dv_ref, dc_ref, dr_ref,
             qaugt_sc, qh_sc, dot_sc, doh_sc, delta_sc, dqt_sc, dr_sc, kaug_sc, vh_sc, kt_sc, s_sc, dp_sc, p_sc, ds_sc,
             dk_sc, dv_sc, dc_sc):
        j, ki = pl.program_id(0), pl.program_id(1)

        @pl.when(ki == 0)
        def _():
            for c in range(nq):
                rows = slice(c * t, (c + 1) * t)
                q_blk, do_blk = q_ref[rows, :], do_ref[rows, :]
                qaugt_sc[:, rows] = _q_augmented_t(q_blk)
                dot_sc[:, rows] = do_blk.T.astype(BF16)
                prod_t = (do_blk * o_ref[rows, :]).T
                delta_sc[:, rows] = _rows01(jnp.sum(prod_t[0:HD], axis=0, keepdims=True),
                                            jnp.sum(prod_t[HD:], axis=0, keepdims=True))
                for h in range(2):
                    head = _head_part((t, 128), h, 1)
                    qh_sc[h, rows, :] = jnp.where(head, q_blk * ATT_SCALE, 0.0).astype(BF16)
                    doh_sc[h, rows, :] = jnp.where(head, do_blk, 0.0).astype(BF16)
            dqt_sc[...] = jnp.zeros(dqt_sc.shape, F32)
            dr_sc[...] = jnp.zeros(dr_sc.shape, F32)

        k_blk, v_blk = k_ref[...], v_ref[...]
        kt = k_blk.T
        for h in range(2):
            kaug_sc[h] = _k_augmented(k_blk, c_ref[...], j, h)
            vh_sc[h] = jnp.where(_head_part((t, 128), h, 1), v_blk, 0.0).astype(BF16)
            kt_sc[h] = jnp.where(_head_part((128, t), h, 0), kt, 0.0).astype(BF16)
        dk_sc[...] = jnp.zeros(dk_sc.shape, F32)
        dv_sc[...] = jnp.zeros(dv_sc.shape, F32)
        dc_sc[...] = jnp.zeros(dc_sc.shape, F32)

        def block(qb, diagonal):
            qs = pl.ds(pl.multiple_of(qb * t, t), t)
            for h in range(2):
                s_sc[h] = _dot(kaug_sc[h], qaugt_sc[:, qs])
                dp_sc[h] = _dot(vh_sc[h], dot_sc[:, qs])
            lse_row = [lse_ref[0, h:h + 1, qs] for h in range(2)]
            delta_row = [delta_sc[h:h + 1, qs] for h in range(2)]

            def strips(i, carry):
                rows = pl.ds(pl.multiple_of(i * r, r), r)
                out = []
                for h in range(2):
                    x = s_sc[h, rows, :]
                    if diagonal:
                        x = jnp.where(_iota((r, t), 1) >= i * r + _iota((r, t), 0), x, -1e30)
                    pr = jnp.exp(x - lse_row[h])
                    ds = pr * (dp_sc[h, rows, :] - delta_row[h])
                    p_sc[h, rows, :] = pr.astype(BF16)
                    ds_sc[h, rows, :] = ds.astype(BF16)
                    dc_sc[h, rows, :] += jnp.broadcast_to(jnp.sum(ds, axis=1, keepdims=True), (r, 128))
                    out.append(_fold8(ds, jnp.add, carry[h]))
                return tuple(out)

            zero = jnp.zeros((8, t), F32)
            dr = lax.fori_loop(0, t // r, strips, (zero, zero), unroll=2)
            for h in range(2):
                dr_sc[h, :, qs] += dr[h]
            dv_sc[...] += _dot(p_sc[0], doh_sc[0, qs, :]) + _dot(p_sc[1], doh_sc[1, qs, :])
            dk_sc[...] += _dot(ds_sc[0], qh_sc[0, qs, :]) + _dot(ds_sc[1], qh_sc[1, qs, :])
            dqt_sc[:, qs] += _dot(kt_sc[0], ds_sc[0]) + _dot(kt_sc[1], ds_sc[1])

        def later(qb, carry):
            block(qb, False)
            return carry

        block(ki, True)
        lax.fori_loop(ki + 1, nq, later, 0)
        dk_ref[...] = dk_sc[...].astype(BF16)
        dv_ref[...] = dv_sc[...].astype(BF16)
        dc_ref[...] = jnp.where(_iota((t, 128), 1) < HD, dc_sc[0], dc_sc[1])

        @pl.when(ki == nq - 1)
        def _():
            for c in range(nq):
                rows = slice(c * t, (c + 1) * t)
                dq_ref[rows, :] = dqt_sc[:, rows].T * ATT_SCALE
            dr_ref[0] = _rows01(jnp.sum(dr_sc[0], axis=0, keepdims=True), jnp.sum(dr_sc[1], axis=0, keepdims=True))

    whole = lambda off: pl.BlockSpec((s, 128), functools.partial(lambda j, ki, off: (0, off + j), off=off))
    return pl.pallas_call(
        body, name="attn_bwd", grid=(NH // 2, nq),
        in_specs=[whole(OFF_Q // 128),
                  pl.BlockSpec((t, 128), lambda j, ki: (ki, OFF_K // 128 + j)),
                  pl.BlockSpec((t, 128), lambda j, ki: (ki, OFF_V // 128 + j)),
                  pl.BlockSpec((t, 128), lambda j, ki: (ki, 0)),
                  whole(0),
                  pl.BlockSpec((1, 8, s), lambda j, ki: (j, 0, 0)),
                  whole(0)],
        out_specs=[whole(0),
                   pl.BlockSpec((t, 128), lambda j, ki: (ki, j)),
                   pl.BlockSpec((t, 128), lambda j, ki: (ki, j)),
                   pl.BlockSpec((t, 128), lambda j, ki: (ki, j)),
                   pl.BlockSpec((1, 8, s), lambda j, ki: (j, 0, 0))],
        out_shape=[jax.ShapeDtypeStruct((s, D), F32), jax.ShapeDtypeStruct((s, D), BF16), jax.ShapeDtypeStruct((s, D), BF16),
                   jax.ShapeDtypeStruct((s, D), F32), jax.ShapeDtypeStruct((NH // 2, 8, s), F32)],
        scratch_shapes=[pltpu.VMEM((256, s), BF16), pltpu.VMEM((2, s, 128), BF16), pltpu.VMEM((128, s), BF16),
                        pltpu.VMEM((2, s, 128), BF16), pltpu.VMEM((8, s), F32), pltpu.VMEM((128, s), F32),
                        pltpu.VMEM((2, 8, s), F32), pltpu.VMEM((2, t, 256), BF16), pltpu.VMEM((2, t, 128), BF16),
                        pltpu.VMEM((2, 128, t), BF16), pltpu.VMEM((2, t, t), F32), pltpu.VMEM((2, t, t), F32),
                        pltpu.VMEM((2, t, t), BF16), pltpu.VMEM((2, t, t), BF16), pltpu.VMEM((t, 128), F32),
                        pltpu.VMEM((t, 128), F32), pltpu.VMEM((2, t, 128), F32)],
        compiler_params=_params(("parallel", "arbitrary")),
    )(p, p, p, cum, o, lse, do)


def _ln_stats(u):
    mu = _mean(u)
    d = u - mu
    rstd = lax.rsqrt(_mean(d * d) + EPS)
    return d * rstd, rstd


def _ln_bwd(dx, xh, rstd, gam):
    dxh = dx * gam
    return rstd * (dxh - _mean(dxh) - xh * _mean(dxh * xh))


def _rms_bwd(d, xn, r, w):
    t = d * w
    return r * (t - xn * _mean(t * xn)), _colsum(d * xn)


def _mix_norm(y, p, att, w_ssm, w_att, s):
    def fn(pos, y, z, att, w1, w2):
        g = y * _silu(z)
        n1 = g * lax.rsqrt(_mean(g * g) + EPS) * w1
        n2 = att * lax.rsqrt(_mean(att * att) + EPS) * w2
        return (jnp.concatenate([n1, n2], axis=1),)

    return _rowk("mix_norm", fn, s, 256, [(y, D, 0, 0), (p, D, OFF_Z // D, 0), (att, D, 0, 0)],
                 [w_ssm, w_att], [(2 * D, BF16)], [])[0]


def _mix_norm_bwd(dmix, y, p, att, w_ssm, w_att, s):
    def fn(pos, dmix, y, z, att, w1, w2, a1, a2):
        sz = _silu(z)
        g = y * sz
        r1 = lax.rsqrt(_mean(g * g) + EPS)
        dg, dw1 = _rms_bwd(dmix[:, :D], g * r1, r1, w1)
        r2 = lax.rsqrt(_mean(att * att) + EPS)
        datt, dw2 = _rms_bwd(dmix[:, D:], att * r2, r2, w2)
        return dg * sz, dg * y * _dsilu(z), datt, a1 + dw1, a2 + dw2

    return _rowk("mix_norm_bwd", fn, s, 256, [(dmix, 2 * D, 0, 0), (y, D, 0, 0), (p, D, OFF_Z // D, 0), (att, D, 0, 0)],
                 [w_ssm, w_att], [(D, F32), (D, BF16), (D, F32)], [(1, D), (1, D)])


def _ln1(x0, y, g1, gam, bet, sc2, sh2, s):
    def fn(pos, x0, y, g1, gam, bet, sc2, sh2):
        xh, _ = _ln_stats(ALPHA * x0 + (1.0 + g1) * y)
        x1 = xh * gam + bet
        return x1, _modulate(x1, sc2, sh2)

    return _rowk("ln1", fn, s, 256, [(x0, D, 0, 0), (y, D, 0, 0)], [g1, gam, bet, sc2, sh2], [(D, F32), (D, BF16)], [])


def _ln2_loss(x1, ff, tgt, g2, gam, bet, s):
    def fn(pos, x1, ff, tgt, g2, gam, bet, a_loss, a_dgam, a_dbet, a_dg2):
        xh, rstd = _ln_stats(ALPHA * x1 + (1.0 + g2) * ff)
        err = xh * gam + bet - tgt
        dx2 = err * (1.0 / D)
        du = _ln_bwd(dx2, xh, rstd, gam)
        return (du, du * (1.0 + g2), a_loss + _colsum(err * err), a_dgam + _colsum(dx2 * xh),
                a_dbet + _colsum(dx2), a_dg2 + _colsum(du * ff))

    return _rowk("ln2_loss", fn, s, 256, [(x1, D, 0, 0), (ff, D, 0, 0), (tgt, D, 0, 0)], [g2, gam, bet],
                 [(D, F32), (D, BF16)], [(1, D)] * 4)


def _ln1_bwd(dh2, du2, x0, y, g1, gam, bet, sc2, s):
    def fn(pos, dh2, du2, x0, y, g1, gam, bet, sc2, a_sc, a_sh, a_gam, a_bet, a_g1):
        xh, rstd = _ln_stats(ALPHA * x0 + (1.0 + g1) * y)
        x1 = xh * gam + bet
        dx1 = ALPHA * du2 + dh2 * (1.0 + sc2)
        du1 = _ln_bwd(dx1, xh, rstd, gam)
        return (du1, du1 * (1.0 + g1), a_sc + _colsum(dh2 * x1), a_sh + _colsum(dh2), a_gam + _colsum(dx1 * xh),
                a_bet + _colsum(dx1), a_g1 + _colsum(du1 * y))

    return _rowk("ln1_bwd", fn, s, 256, [(dh2, D, 0, 0), (du2, D, 0, 0), (x0, D, 0, 0), (y, D, 0, 0)],
                 [g1, gam, bet, sc2], [(D, F32), (D, BF16)], [(1, D)] * 5)


def _input_grad(dh1, du1, x0, sc1, s):
    def fn(pos, dh1, du1, x0, sc1, a_sc, a_sh):
        return ALPHA * du1 + dh1 * (1.0 + sc1), a_sc + _colsum(dh1 * x0), a_sh + _colsum(dh1)

    return _rowk("input_grad", fn, s, 256, [(dh1, D, 0, 0), (du1, D, 0, 0), (x0, D, 0, 0)], [sc1],
                 [(D, F32)], [(1, D)] * 2)


def _adamw(name, w, g, m, v, *, tr, slots):
    r, c = w.shape

    def body(w_ref, g_ref, m_ref, v_ref, g_out, d_out, m_out, v_out):
        if slots:
            grad = g_ref[0].astype(F32)
            for k in range(1, N_DEV):
                grad = grad + g_ref[k].astype(F32)
        else:
            grad = g_ref[...]
        m_new = ADAM_B1 * m_ref[...] + (1.0 - ADAM_B1) * grad
        v_new = ADAM_B2 * v_ref[...] + (1.0 - ADAM_B2) * (grad * grad)
        m_hat = m_new / (1.0 - ADAM_B1 ** ADAM_STEP)
        v_hat = v_new / (1.0 - ADAM_B2 ** ADAM_STEP)
        g_out[...] = grad
        d_out[...] = -ADAM_LR * (m_hat / (jnp.sqrt(v_hat) + ADAM_EPS) + ADAM_WD * w_ref[...])
        m_out[...] = m_new
        v_out[...] = v_new

    tile = pl.BlockSpec((tr, c), lambda i: (i, 0))
    g_spec = pl.BlockSpec((N_DEV, tr, c), lambda i: (0, i, 0)) if slots else tile
    return pl.pallas_call(
        body, name=name, grid=(r // tr,),
        in_specs=[tile, g_spec, tile, tile], out_specs=[tile] * 4,
        out_shape=[jax.ShapeDtypeStruct((r, c), F32)] * 4,
        compiler_params=_params(("parallel",)),
    )(w, g, m, v)


def _dot_f32(a, b, dims=NN):
    a0, a1, a2 = _split3(a)
    b0, b1, b2 = _split3(b)
    acc = _dot(a0, b0, dims)
    for x, y in ((a0, b1), (a1, b0), (a1, b1), (a0, b2), (a2, b0)):
        acc = acc + _dot(x, y, dims)
    return acc


def _ada_mod(c_all, w_shard, b_shard):
    def body(c_ref, w_ref, b_ref, o_ref):
        act = _silu(c_ref[...])
        act16 = jnp.concatenate([act, jnp.zeros_like(act)], axis=0)
        o_ref[...] = _dot_f32(act16, w_ref[...])[0:N_DEV] + b_ref[...]

    return pl.pallas_call(
        body, name="ada_mod", out_shape=jax.ShapeDtypeStruct((N_DEV, w_shard.shape[1]), F32),
        compiler_params=_params(None),
    )(c_all, w_shard, b_shard)


def _ada_grad(c_all, dmod_cols, dmod_all):
    def body(c_ref, dc_ref, da_ref, gw_ref, gb_ref):
        act = _silu(c_ref[...])
        act16 = jnp.concatenate([act, jnp.zeros_like(act)], axis=0)
        dm = dc_ref[...]
        dm16 = jnp.concatenate([dm, jnp.zeros_like(dm)], axis=0)
        gw_ref[...] = _dot_f32(act16, dm16, TN)
        gb_ref[...] = _colsum(da_ref[...])

    return pl.pallas_call(
        body, name="ada_grad",
        out_shape=[jax.ShapeDtypeStruct((D, dmod_cols.shape[1]), F32), jax.ShapeDtypeStruct((1, 6 * D), F32)],
        compiler_params=_params(None),
    )(c_all, dmod_cols, dmod_all)


def _sum_slots(name, g):
    def body(g_ref, o_ref):
        acc = g_ref[0]
        for k in range(1, N_DEV):
            acc = acc + g_ref[k]
        o_ref[...] = acc

    return pl.pallas_call(body, name=name, out_shape=jax.ShapeDtypeStruct(g.shape[1:], F32),
                          compiler_params=_params(None))(g)


def _exchange(name, xs, scatter):
    n = len(xs)
    n_peer = N_DEV - 1

    def body(*refs):
        x_refs, o_refs = refs[:n], refs[n:2 * n]
        send_sems, recv_sems, local_sems = refs[2 * n:]
        mx, my, mc = lax.axis_index("x"), lax.axis_index("y"), lax.axis_index("c")
        me = 4 * mx + 2 * my + mc

        def src(a, slot):
            return x_refs[a].at[slot] if scatter else x_refs[a]

        own = [pltpu.make_async_copy(src(a, me), o_refs[a].at[me], local_sems.at[a]) for a in range(n)]
        for cp in own:
            cp.start()
        sends = []
        for d in range(1, N_DEV):
            px = 1 - mx if d & 4 else mx
            py = 1 - my if d & 2 else my
            pc = 1 - mc if d & 1 else mc
            peer = 4 * px + 2 * py + pc
            for a in range(n):
                def copy(src_slot, dst_slot, a=a, d=d, to=(px, py, pc)):
                    return pltpu.make_async_remote_copy(
                        src_ref=src(a, src_slot), dst_ref=o_refs[a].at[dst_slot],
                        send_sem=send_sems.at[a * n_peer + d - 1], recv_sem=recv_sems.at[a * n_peer + d - 1],
                        device_id=to, device_id_type=pl.DeviceIdType.MESH)

                out = copy(peer, me)
                out.start()
                sends.append((out, copy(me, peer)))
        for _, arrival in sends:
            arrival.wait_recv()
        for out, _ in sends:
            out.wait_send()
        for cp in own:
            cp.wait()

    shapes = [tuple(x.shape[1:] if scatter else x.shape) for x in xs]
    return pl.pallas_call(
        body, name=name,
        in_specs=[pl.BlockSpec(memory_space=pl.ANY)] * n, out_specs=[pl.BlockSpec(memory_space=pl.ANY)] * n,
        out_shape=[jax.ShapeDtypeStruct((N_DEV,) + sh, x.dtype) for sh, x in zip(shapes, xs)],
        scratch_shapes=[pltpu.SemaphoreType.DMA((n * n_peer,)), pltpu.SemaphoreType.DMA((n * n_peer,)),
                        pltpu.SemaphoreType.DMA((n,))],
        compiler_params=pltpu.CompilerParams(has_side_effects=True),
    )(*xs)


def _after(x, zero):
    return x if zero is None else x + zero.reshape(-1)[0].astype(x.dtype)


_HBM = pl.BlockSpec(memory_space=pltpu.HBM)
_SEM = pl.BlockSpec(memory_space=pltpu.SEMAPHORE)


def _exchange_copies(x_refs, land_refs, send_sems, recv_sems, scatter):
    n = len(x_refs)
    n_peer = N_DEV - 1
    mx, my, mc = lax.axis_index("x"), lax.axis_index("y"), lax.axis_index("c")
    me = 4 * mx + 2 * my + mc
    pairs = []
    for d in range(1, N_DEV):
        px = 1 - mx if d & 4 else mx
        py = 1 - my if d & 2 else my
        pc = 1 - mc if d & 1 else mc
        peer = 4 * px + 2 * py + pc
        for a in range(n):
            def copy(src_slot, dst_slot, a=a, d=d, to=(px, py, pc)):
                return pltpu.make_async_remote_copy(
                    src_ref=x_refs[a].at[src_slot] if scatter else x_refs[a], dst_ref=land_refs[a].at[dst_slot],
                    send_sem=send_sems.at[a * n_peer + d - 1], recv_sem=recv_sems.at[a * n_peer + d - 1],
                    device_id=to, device_id_type=pl.DeviceIdType.MESH)

            pairs.append((copy(peer, me), copy(me, peer)))
    return me, pairs


def _exchange_async(name, xs, scatter, collective_id):
    n = len(xs)
    shapes = [tuple(x.shape[1:] if scatter else x.shape) for x in xs]
    x_refs = [jax.new_ref(x, memory_space=pltpu.MemorySpace.HBM) for x in xs]
    land_refs = [jax.empty_ref(jax.ShapeDtypeStruct((N_DEV,) + sh, x.dtype), memory_space=pltpu.MemorySpace.HBM)
                 for sh, x in zip(shapes, xs)]

    @pl.kernel(mesh=plsc.ScalarSubcoreMesh(axis_name="sequencer", num_cores=1), name=name,
               scratch_types=(pltpu.SemaphoreType.DMA((n * (N_DEV - 1),)), pltpu.SemaphoreType.DMA((n * (N_DEV - 1),)),
                              pltpu.SemaphoreType.DMA((n,))),
               compiler_params=pltpu.CompilerParams(collective_id=collective_id))
    def launch(send_sems, recv_sems, own_sems):
        barrier = pltpu.get_barrier_semaphore()
        mx, my, mc = lax.axis_index("x"), lax.axis_index("y"), lax.axis_index("c")
        for d in range(1, N_DEV):
            peer = (1 - mx if d & 4 else mx, 1 - my if d & 2 else my, 1 - mc if d & 1 else mc)
            pl.semaphore_signal(barrier, inc=1, device_id=peer, device_id_type=pl.DeviceIdType.MESH)
        pl.semaphore_wait(barrier, N_DEV - 1)
        me, pairs = _exchange_copies(x_refs, land_refs, send_sems, recv_sems, scatter)
        own = [pltpu.make_async_copy(x_refs[a].at[me] if scatter else x_refs[a], land_refs[a].at[me], own_sems.at[a])
               for a in range(n)]
        for cp in own:
            cp.start()
        for out, _ in pairs:
            out.start()
        for out, arrival in pairs:
            arrival.wait_recv()
            out.wait_send()
        for cp in own:
            cp.wait()

    launch()
    return lambda: [r[...] for r in land_refs]


def _exchange_start(name, xs, scatter):
    n = len(xs)
    shapes = [tuple(x.shape[1:] if scatter else x.shape) for x in xs]

    def body(*refs):
        x_refs, land_refs = refs[:n], refs[n:2 * n]
        send_sems, recv_sems = refs[2 * n], refs[2 * n + 1]
        token, own_sems = refs[4 * n + 2], refs[4 * n + 3]
        me, pairs = _exchange_copies(x_refs, land_refs, send_sems, recv_sems, scatter)
        own = [pltpu.make_async_copy(x_refs[a].at[me] if scatter else x_refs[a], land_refs[a].at[me], own_sems.at[a])
               for a in range(n)]
        for cp in own:
            cp.start()
        for out, _ in pairs:
            out.start()
        for cp in own:
            cp.wait()
        token[...] = jnp.zeros(token.shape, token.dtype)

    lands = [pltpu.with_memory_space_constraint(lax.empty((N_DEV,) + sh, x.dtype), pltpu.HBM) for sh, x in zip(shapes, xs)]
    res = pl.pallas_call(
        body, name=name,
        out_shape=(pltpu.SemaphoreType.DMA((n * (N_DEV - 1),)), pltpu.SemaphoreType.DMA((n * (N_DEV - 1),)),
                   *[pltpu.HBM(x.shape, x.dtype) for x in xs], *[pltpu.HBM(l.shape, l.dtype) for l in lands],
                   jax.ShapeDtypeStruct((8, 128), F32)),
        in_specs=[_HBM] * (2 * n),
        out_specs=(_SEM, _SEM, *[_HBM] * (2 * n), pl.BlockSpec(memory_space=pltpu.VMEM)),
        input_output_aliases={i: 2 + i for i in range(2 * n)},
        scratch_shapes=[pltpu.SemaphoreType.DMA((n,))],
        compiler_params=pltpu.CompilerParams(has_side_effects=pltpu.SideEffectType.DATAFLOW_SIDE_EFFECTING),
    )(*[pltpu.with_memory_space_constraint(x, pltpu.HBM) for x in xs], *lands)
    return dict(send=res[0], recv=res[1], xs=list(res[2:2 + n]), lands=list(res[2 + n:2 + 2 * n]), token=res[2 + 2 * n])


def _exchange_wait(name, handle, after, scatter):
    n = len(handle['xs'])

    def body(*refs):
        x_refs, land_refs = refs[:n], refs[n:2 * n]
        send_sems, recv_sems = refs[2 * n], refs[2 * n + 1]
        _, pairs = _exchange_copies(x_refs, land_refs, send_sems, recv_sems, scatter)
        for out, arrival in pairs:
            out.wait_send()
            arrival.wait_recv()

    res = pl.pallas_call(
        body, name=name,
        out_shape=tuple(pltpu.HBM(a.shape, a.dtype) for a in handle['xs'] + handle['lands']),
        in_specs=[_HBM] * (2 * n) + [_SEM, _SEM, pl.BlockSpec(memory_space=pl.ANY)],
        out_specs=tuple([_HBM] * (2 * n)),
        input_output_aliases={i: i for i in range(2 * n)},
        compiler_params=pltpu.CompilerParams(has_side_effects=pltpu.SideEffectType.DATAFLOW_SIDE_EFFECTING),
    )(*handle['xs'], *handle['lands'], handle['send'], handle['recv'], after)
    return list(res[n:])


def _relu2(a):
    r = jnp.maximum(a, 0.0)
    return r * r


def _relu2_grad(acc, a):
    return acc * (2.0 * jnp.maximum(a, 0.0))


def _local_step(x0, tgt, mod, wcat, late_weights, send_grads, conv_w, conv_b, dt_bias, a_log, d_skip, ssm_norm_w, f_bias,
                attn_norm_w, ln1_g, ln1_b, ln2_g, ln2_b):
    ff_w = DFF // N_DEV
    s = x0.shape[0]
    tm = min(1024, s)
    ts = min(1024, s)
    sh1, sc1, g1, sh2, sc2, g2 = [mod[:, i * D:(i + 1) * D] for i in range(6)]
    zero = jnp.zeros((1, 128 - 2 * NH), F32)
    bias128 = jnp.concatenate([dt_bias, f_bias, zero], axis=1)
    alog128 = jnp.concatenate([a_log, jnp.zeros((1, 128 - NH), F32)], axis=1)
    dskip_x = jnp.repeat(d_skip, HD, axis=1)
    w_xs, w_bc, b_xs, b_bc = conv_w[:, :D], conv_w[:, D:], conv_b[:, :D], conv_b[:, D:]

    p = _mm_nn("in_proj", x0, wcat, tm=tm, tn=1152, tk=D, out_dtype=F32, pro=_modulate, aux=(sc1, sh1))
    xs_a, bc_a = _conv_fwd(p, w_xs, b_xs, w_bc, b_bc, s)
    y_ssd, states = _ssd_fwd(xs_a, bc_a, p, bias128, alog128, dskip_x, s)
    cum = _cum_fwd(p, bias128, s)
    att, lse = _attn_fwd(p, cum, s)
    wout, w1s, w2 = late_weights(lse)
    ymix = _mix_norm(y_ssd, p, att, ssm_norm_w, attn_norm_w, s)
    y = _mm_nn("out_proj", ymix, wout, tm=tm, tn=1024, tk=2 * D, out_dtype=F32)
    x1, h2 = _ln1(x0, y, g1, ln1_g, ln1_b, sc2, sh2, s)
    a1 = _mm_nn("ff_in", h2, w1s, tm=tm, tn=ff_w, tk=D, out_dtype=F32)
    ff = _mm_nn("ff_out", a1, w2, tm=tm, tn=1024, tk=1024, out_dtype=F32, pro=_relu2)
    du2, dff, sq_err, d_ln2_g, d_ln2_b, d_g2 = _ln2_loss(x1, ff, tgt, g2, ln2_g, ln2_b, s)

    da1 = _mm_nt("d_ff_hidden", [(dff, D, 0)], [(w2, D, 0)], n=DFF, tm=tm, tn=1024, out_dtype=BF16, epi=_relu2_grad,
                 epi_aux=(a1,))
    d_w2 = _mm_tn("d_w_ff_out", a1, dff, tm=1024, tn=1024, ts=ts, pro=_relu2)
    d_w1s = _mm_tn("d_w_ff_in", h2, da1, tm=1024, tn=ff_w, ts=ts, col_shards=True)
    dh2 = _mm_nt("d_ff_input", [(da1, ff_w, k) for k in range(N_DEV)], [(w1s, ff_w, k) for k in range(N_DEV)], n=D,
                 tm=min(512, s), tn=1024, out_dtype=F32)
    sent = send_grads("ff", [d_w1s, d_w2.reshape(N_DEV, -1, D)])
    du1, dy, d_sc2, d_sh2, d_ln1_g, d_ln1_b, d_g1 = _ln1_bwd(dh2, du2, x0, y, g1, ln1_g, ln1_b, _after(sc2, sent), s)

    dmix = _mm_nt("d_mix", [(dy, D, 0)], [(wout, D, 0)], n=2 * D, tm=tm, tn=1024, out_dtype=F32)
    d_wout = _mm_tn("d_w_out", ymix, dy, tm=1024, tn=1024, ts=ts)
    sent = send_grads("out", [d_wout.reshape(N_DEV, -1, D)])
    dy_ssd, dz, datt, d_ssm_w, d_attn_w = _mix_norm_bwd(dmix, y_ssd, p, att, _after(ssm_norm_w, sent), attn_norm_w, s)
    dq, dk, dv, dcs, drs = _attn_bwd(p, cum, att, lse, datt, s)
    dxs_a, dbc_a, ddt_raw, d_alog, d_dskip = _ssd_bwd(dy_ssd, xs_a, bc_a, p, states, bias128, alog128, dskip_x, s)
    dcum = jnp.pad(drs[:, :2, :].reshape(NH, s).T - dcs[:, ::HD], ((0, 0), (NH, 128 - 2 * NH)))
    ddtf, _, d_bias = _cum_bwd(dcum, ddt_raw, p, bias128, s)
    dxs, dbc, d_wc_xs, d_bc_xs, d_wc_bc, d_bc_bc = _conv_bwd(dxs_a, dbc_a, p, w_xs, b_xs, w_bc, b_bc, s)

    segs = [(dz, OFF_Z, D), (dxs, OFF_XS, D), (dq, OFF_Q, D), (dk, OFF_K, D), (dv, OFF_V, D), (dbc, OFF_BC, 512),
            (ddtf, OFF_DTF, 128)]
    d_z, d_xs, d_q, d_k, d_v, d_bcw, d_dtf = [
        _mm_tn("d_w_in_%d" % i, x0, a, tm=1024, tn=min(w, 1024), ts=ts, pro=_modulate, aux=(sc1, sh1))
        for i, (a, _, w) in enumerate(segs)]
    d_w_in = dict(z=d_z, xs=d_xs, bc=d_bcw, dt=d_dtf[:, :NH], q=d_q, k=d_k, v=d_v, f=d_dtf[:, NH:2 * NH])
    sent = send_grads("in", [_shard_w_in_grad(d_w_in)])
    segs[-1] = (_after(ddtf, sent), OFF_DTF, 128)
    dh1 = _mm_nt("d_h1", [(a, w, 0) for a, _, w in segs], [(wcat, w, off // w) for _, off, w in segs], n=D,
                 tm=min(512, s), tn=1024, out_dtype=F32)
    grad_x, d_sc1, d_sh1 = _input_grad(dh1, du1, x0, sc1, s)

    return dict(
        loss=(0.5 / D) * jnp.sum(sq_err), grad_x=grad_x,
        d_mod=jnp.concatenate([d_sh1, d_sc1, d_g1, d_sh2, d_sc2, d_g2], axis=1),
        d_conv_w=jnp.concatenate([d_wc_xs[:4], d_wc_bc[:4]], axis=1), d_conv_b=jnp.concatenate([d_bc_xs, d_bc_bc], axis=1),
        d_ssm_norm_w=d_ssm_w, d_attn_norm_w=d_attn_w, d_ln1_g=d_ln1_g, d_ln1_b=d_ln1_b, d_ln2_g=d_ln2_g, d_ln2_b=d_ln2_b,
        d_gate_bias=d_bias, d_a_log=d_alog, d_d_skip=d_dskip)


W_IN_SEGS = [('z', W_Z, D), ('xs', W_XS, D), ('bc', W_BC, 512), ('dt', W_DT, NH), ('q', W_Q, D), ('k', W_K, D),
             ('v', W_V, D), ('f', W_F, NH)]
SHARD_W = IN_COLS // N_DEV


def _pack_w_in(shards):
    def cols(lo, hi):
        pieces = []
        while lo < hi:
            dev = lo // SHARD_W
            end = min(hi, (dev + 1) * SHARD_W)
            pieces.append(shards[dev][:, lo - dev * SHARD_W:end - dev * SHARD_W])
            lo = end
        return pieces

    seg = {n: cols(off, off + w) for n, off, w in W_IN_SEGS}
    pieces = seg['z'] + seg['xs'] + seg['q'] + seg['k'] + seg['v'] + seg['bc'] + seg['dt'] + seg['f']
    return jnp.concatenate(pieces + [jnp.zeros((D, 128 - 2 * NH), shards.dtype)], axis=1)


def _shard_w_in_grad(d_w_in):
    blocks = []
    for dev in range(N_DEV):
        lo, hi = dev * SHARD_W, (dev + 1) * SHARD_W
        pieces = [d_w_in[n][:, max(lo, off) - off:min(hi, off + w) - off] for n, off, w in W_IN_SEGS
                  if max(lo, off) < min(hi, off + w)]
        blocks.append(jnp.concatenate(pieces, axis=1))
    return jnp.stack(blocks, axis=0)


WEIGHTS = ['w_ada', 'b_ada', 'w_in', 'conv_w', 'conv_b', 'dt_bias', 'a_log', 'd_skip', 'ssm_norm_w', 'f_bias',
           'attn_norm_w', 'w_out', 'ln1_g', 'ln1_b', 'w_ff_in', 'w_ff_out', 'ln2_g', 'ln2_b']
BIG = ['w_in', 'w_out', 'w_ff_in', 'w_ff_out']
SMALL = ['b_ada', 'conv_b', 'ssm_norm_w', 'attn_norm_w', 'ln1_g', 'ln1_b', 'ln2_g', 'ln2_b', 'dt_bias', 'a_log', 'd_skip',
         'f_bias', 'conv_w']


def _pad_lanes(v, n=128):
    return jnp.pad(v, ((0, 0), (0, n - v.shape[1])))


def _small_block(vals):
    rows = [_pad_lanes(vals[n].reshape(1, -1), -(-vals[n].size // 128) * 128).reshape(-1, 128) for n in SMALL]
    block = jnp.concatenate(rows, axis=0)
    return jnp.pad(block, ((0, 120 - block.shape[0]), (0, 0)))


def _small_unblock(block, like):
    out, r = {}, 0
    for n in SMALL:
        size = like[n].size
        nr = -(-size // 128)
        out[n] = block[r:r + nr].reshape(-1)[:size].reshape(like[n].shape)
        r += nr
    return out


def kernel(x, c, w_ada, b_ada, w_in, conv_w, conv_b, dt_bias, a_log, d_skip, ssm_norm_w, f_bias, attn_norm_w, w_out, ln1_g, ln1_b, w_ff_in, w_ff_out, ln2_g, ln2_b, loss_target, m_w_ada, m_b_ada, m_w_in, m_conv_w, m_conv_b, m_dt_bias, m_a_log, m_d_skip, m_ssm_norm_w, m_f_bias, m_attn_norm_w, m_w_out, m_ln1_g, m_ln1_b, m_w_ff_in, m_w_ff_out, m_ln2_g, m_ln2_b, v_w_ada, v_b_ada, v_w_in, v_conv_w, v_conv_b, v_dt_bias, v_a_log, v_d_skip, v_ssm_norm_w, v_f_bias, v_attn_norm_w, v_w_out, v_ln1_g, v_ln1_b, v_w_ff_in, v_w_ff_out, v_ln2_g, v_ln2_b):
    args = dict(locals())
    w = {n: args[n] for n in WEIGHTS}
    m = {n: args['m_' + n] for n in WEIGHTS}
    v = {n: args['v_' + n] for n in WEIGHTS}
    me = 4 * lax.axis_index("x") + 2 * lax.axis_index("y") + lax.axis_index("c")
    ada_cols = 6 * D // N_DEV
    conv_cols = conv_w.shape[2]

    c_all, conv_all = _exchange("gather_cond", [c, conv_w[0]], False)
    c_all = c_all.reshape(N_DEV, D)
    conv_w_full = conv_all.transpose(1, 0, 2).reshape(4, N_DEV * conv_cols)
    b_shard = lax.dynamic_slice(b_ada, (0, me * ada_cols), (1, ada_cols))
    mod_all, = _exchange("gather_mod", [_ada_mod(c_all, w_ada[0], b_shard)], False)
    mod = lax.dynamic_index_in_dim(mod_all, me, axis=1, keepdims=False).reshape(1, 6 * D)

    win_s, = _exchange("gather_w_in", [_after(w_in[0].astype(BF16), mod * 0)], False)
    first_done = win_s[0, 0:1, 0:1] * 0
    rest = _exchange_async("gather_rest", [_after(w[n][0].astype(BF16), first_done) for n in BIG[1:]], False, 1)

    def late_weights(after):
        wout_s, w1s, w2_s = rest()
        return wout_s.reshape(2 * D, D), w1s, w2_s.reshape(DFF, D)

    sends = {}

    def send_grads(tag, blocks):
        sends[tag] = _exchange_async("scatter_" + tag, blocks, True, {'ff': 2, 'out': 3, 'in': 4}[tag])
        return sum(b.reshape(-1)[0].astype(F32) * 0 for b in blocks)

    out = _local_step(x[0], loss_target[0], mod, _pack_w_in(win_s), late_weights, send_grads,
                      conv_w_full, conv_b, dt_bias, a_log, d_skip, ssm_norm_w, f_bias, attn_norm_w, ln1_g, ln1_b, ln2_g, ln2_b)
    g_ff_in, g_ff_out = sends['ff']()
    g_out, = sends['out']()
    g_in, = sends['in']()
    g_parts = [g_in, g_out, g_ff_in, g_ff_out]
    big = {n: _adamw("adamw_" + n, w[n][0], g, m[n][0], v[n][0], tr=256, slots=True) for n, g in zip(BIG, g_parts)}

    small = jnp.concatenate(
        [out['d_mod'], out['d_conv_w'].reshape(1, -1), out['d_conv_b'], out['d_ssm_norm_w'], out['d_attn_norm_w'],
         out['d_ln1_g'], out['d_ln1_b'], out['d_ln2_g'], out['d_ln2_b'], out['d_gate_bias'], out['d_a_log'],
         out['d_d_skip'], jnp.zeros((1, 128), F32)], axis=1).reshape(-1, 128)
    small_all, = _exchange("gather_small", [small], False)
    ssum = _sum_slots("sum_small", small_all)
    dmod_all = small_all[:, :6 * D // 128].reshape(N_DEV, 6 * D)
    g_w_ada, g_b_ada = _ada_grad(c_all, lax.dynamic_slice(dmod_all, (0, me * ada_cols), (N_DEV, ada_cols)), dmod_all)
    rows = lambda a, b: ssum[a:b].reshape(1, -1)
    g_conv_w = lax.dynamic_slice(ssum[48:96].reshape(4, N_DEV * conv_cols), (0, me * conv_cols), (4, conv_cols))
    g_small = dict(b_ada=g_b_ada, conv_w=g_conv_w[None], conv_b=rows(96, 108), ssm_norm_w=rows(108, 116),
                   attn_norm_w=rows(116, 124), ln1_g=rows(124, 132), ln1_b=rows(132, 140), ln2_g=rows(140, 148),
                   ln2_b=rows(148, 156), dt_bias=ssum[156:157, :NH], f_bias=ssum[156:157, NH:2 * NH],
                   a_log=ssum[157:158, :NH], d_skip=ssum[158:159, :NH])
    sm = _adamw("adamw_small", _small_block(w), _small_block(g_small), _small_block(m), _small_block(v), tr=120, slots=False)
    ada = _adamw("adamw_ada", w_ada[0], g_w_ada, m_w_ada[0], v_w_ada[0], tr=256, slots=False)

    results = []
    for k in range(4):
        vals = _small_unblock(sm[k], w)
        vals['w_ada'] = ada[k][None]
        for n in BIG:
            vals[n] = big[n][k][None]
        results.append(vals)
    loss = lax.psum(out['loss'], ("x", "y", "c"))
    return (loss, out['grad_x'][None], *[res[n] for res in results for n in WEIGHTS])
```

```python
import functools

import jax
import jax.numpy as jnp
from jax import lax
from jax.experimental import pallas as pl
from jax.experimental.pallas import tpu as pltpu
from jax.experimental.pallas import tpu_sc as plsc

F32, BF16 = jnp.float32, jnp.bfloat16

N_DEV = 8
D = 1024
NH, HD = 16, 64
NSTATE = 128
CHUNK = 128
HG = 8
DFF = 4096
ALPHA = 2.0 ** 0.25
EPS = 1e-5
ATT_SCALE = HD ** -0.5

OFF_Z, OFF_XS, OFF_Q, OFF_K, OFF_V, OFF_BC, OFF_DTF = 0, 1024, 2048, 3072, 4096, 5120, 5632
PCOLS = 5760
W_Z, W_XS, W_BC, W_DT, W_Q, W_K, W_V, W_F = 0, 1024, 2048, 2560, 2576, 3600, 4624, 5648
IN_COLS = 5664

ADAM_LR, ADAM_B1, ADAM_B2, ADAM_EPS, ADAM_WD, ADAM_STEP = 0.001, 0.9, 0.999, 1e-08, 0.01, 10

VMEM_LIMIT = 56 << 20

NN = (((1,), (0,)), ((), ()))
NT = (((1,), (1,)), ((), ()))
TN = (((0,), (0,)), ((), ()))


def _dot(a, b, dims=NN):
    return lax.dot_general(a, b, dims, preferred_element_type=F32)


def _bdot(a, b, dims=NN):
    return _dot(a.astype(BF16), b.astype(BF16), dims)


def _split3(v):
    parts, rest = [], v
    for _ in range(3):
        p = rest.astype(BF16)
        parts.append(p)
        rest = rest - p.astype(F32)
    return parts


def _sel_left(m01, v):
    return sum(_dot(m01, p) for p in _split3(v))


def _sel_right(v, m01, dims=NN):
    return sum(_dot(p, m01, dims) for p in _split3(v))


def _iota(shape, dim):
    return lax.broadcasted_iota(jnp.int32, shape, dim)


def _tri_lower(n):
    return (_iota((n, n), 1) <= _iota((n, n), 0)).astype(BF16)


def _tri_upper(n):
    return (_iota((n, n), 1) >= _iota((n, n), 0)).astype(BF16)


def _head_expand():
    return (lax.shift_right_logical(_iota((128, D), 1), 6) == _iota((128, D), 0)).astype(BF16)


def _head_reduce():
    return (lax.shift_right_logical(_iota((D, 128), 0), 6) == _iota((D, 128), 1)).astype(BF16)


def _sigmoid(x):
    return 1.0 / (1.0 + jnp.exp(-x))


def _silu(x):
    return x * _sigmoid(x)


def _dsilu(x):
    s = _sigmoid(x)
    return s * (1.0 + x * (1.0 - s))


def _softplus(x):
    return jnp.maximum(x, 0.0) + jnp.log(1.0 + jnp.exp(-jnp.abs(x)))


def _log_sigmoid(x):
    return jnp.minimum(x, 0.0) - jnp.log(1.0 + jnp.exp(-jnp.abs(x)))


def _params(sem):
    return pltpu.CompilerParams(dimension_semantics=sem, vmem_limit_bytes=VMEM_LIMIT)


def _mm_nn(name, a, b, *, tm, tn, tk, out_dtype, pro=None, aux=()):
    m, k_all = a.shape
    b_sharded = b.ndim == 3
    n = b.shape[0] * b.shape[2] if b_sharded else b.shape[1]
    assert not b_sharded or tn == b.shape[2]
    nk = k_all // tk
    n_aux = len(aux)
    b_spec = (pl.BlockSpec((None, tk, tn), lambda i, j, k: (j, k, 0)) if b_sharded
              else pl.BlockSpec((tk, tn), lambda i, j, k: (k, j)))

    def body(a_ref, b_ref, *rest):
        aux_refs, o_ref = rest[:n_aux], rest[n_aux]
        at = a_ref[...]
        if pro is not None:
            at = pro(at, *[r[...] for r in aux_refs])
        part = _bdot(at, b_ref[...])
        if nk == 1:
            o_ref[...] = part.astype(out_dtype)
            return
        acc_ref = rest[n_aux + 1]
        kk = pl.program_id(2)

        @pl.when(kk == 0)
        def _():
            acc_ref[...] = part

        @pl.when(kk > 0)
        def _():
            acc_ref[...] += part

        @pl.when(kk == nk - 1)
        def _():
            o_ref[...] = acc_ref[...].astype(out_dtype)

    return pl.pallas_call(
        body, name=name,
        grid=(m // tm, n // tn, nk),
        in_specs=[pl.BlockSpec((tm, tk), lambda i, j, k: (i, k)), b_spec]
        + [pl.BlockSpec((1, tk), lambda i, j, k: (0, k)) for _ in aux],
        out_specs=pl.BlockSpec((tm, tn), lambda i, j, k: (i, j)),
        out_shape=jax.ShapeDtypeStruct((m, n), out_dtype),
        scratch_shapes=[] if nk == 1 else [pltpu.VMEM((tm, tn), F32)],
        compiler_params=_params(("parallel", "parallel", "arbitrary")),
    )(a, b, *aux)


def _mm_nt(name, a_list, b_list, *, n, tm, tn, out_dtype, epi=None, epi_aux=()):
    m = a_list[0][0].shape[0]
    n_op = len(a_list)
    n_epi = len(epi_aux)

    def body(*refs):
        a_refs, b_refs = refs[:n_op], refs[n_op:2 * n_op]
        e_refs, o_ref = refs[2 * n_op:2 * n_op + n_epi], refs[2 * n_op + n_epi]
        acc = None
        for a_ref, b_ref in zip(a_refs, b_refs):
            part = _bdot(a_ref[...], b_ref[...], NT)
            acc = part if acc is None else acc + part
        if epi is not None:
            acc = epi(acc, *[r[...] for r in e_refs])
        o_ref[...] = acc.astype(out_dtype)

    in_specs = [pl.BlockSpec((tm, w), functools.partial(lambda i, j, cb: (i, cb), cb=cb)) for (_, w, cb) in a_list]
    for (b, w, cb) in b_list:
        if b.ndim == 3:
            in_specs.append(pl.BlockSpec((None, tn, w), functools.partial(lambda i, j, cb: (cb, j, 0), cb=cb)))
        else:
            in_specs.append(pl.BlockSpec((tn, w), functools.partial(lambda i, j, cb: (j, cb), cb=cb)))
    in_specs += [pl.BlockSpec((tm, tn), lambda i, j: (i, j)) for _ in epi_aux]
    return pl.pallas_call(
        body, name=name,
        grid=(m // tm, n // tn),
        in_specs=in_specs,
        out_specs=pl.BlockSpec((tm, tn), lambda i, j: (i, j)),
        out_shape=jax.ShapeDtypeStruct((m, n), out_dtype),
        compiler_params=_params(("parallel", "parallel")),
    )(*[a for (a, _, _) in a_list], *[b for (b, _, _) in b_list], *epi_aux)


def _mm_tn(name, a, b, *, tm, tn, ts, pro=None, aux=(), col_shards=False):
    s_all, ka = a.shape
    nb = b.shape[1]
    n_aux = len(aux)
    ns = s_all // ts
    assert not col_shards or tn == nb // N_DEV

    def body(a_ref, b_ref, *rest):
        aux_refs, o_ref, acc_ref = rest[:n_aux], rest[n_aux], rest[n_aux + 1]
        at = a_ref[...]
        if pro is not None:
            at = pro(at, *[r[...] for r in aux_refs])
        part = _bdot(at, b_ref[...], TN)
        ss = pl.program_id(2)

        @pl.when(ss == 0)
        def _():
            acc_ref[...] = part

        @pl.when(ss > 0)
        def _():
            acc_ref[...] += part

        @pl.when(ss == ns - 1)
        def _():
            o_ref[...] = acc_ref[...].astype(BF16)

    if col_shards:
        out_spec = pl.BlockSpec((None, tm, tn), lambda i, j, s: (j, i, 0))
        out_shape = jax.ShapeDtypeStruct((N_DEV, ka, tn), BF16)
    else:
        out_spec = pl.BlockSpec((tm, tn), lambda i, j, s: (i, j))
        out_shape = jax.ShapeDtypeStruct((ka, nb), BF16)
    return pl.pallas_call(
        body, name=name,
        grid=(ka // tm, nb // tn, ns),
        in_specs=[pl.BlockSpec((ts, tm), lambda i, j, s: (s, i)),
                  pl.BlockSpec((ts, tn), lambda i, j, s: (s, j))]
        + [pl.BlockSpec((1, tm), lambda i, j, s: (0, i)) for _ in aux],
        out_specs=out_spec, out_shape=out_shape,
        scratch_shapes=[pltpu.VMEM((tm, tn), F32)],
        compiler_params=_params(("parallel", "parallel", "arbitrary")),
    )(a, b, *aux)


def _rowk(name, fn, n_rows, tr, rows, fulls, outs, accs, reverse=False):
    n = n_rows // tr
    n_row, n_full, n_out, n_acc = len(rows), len(fulls), len(outs), len(accs)

    def pos(i):
        return (n - 1 - i) if reverse else i

    def body(*refs):
        row_refs = refs[:n_row]
        full_refs = refs[n_row:n_row + n_full]
        out_refs = refs[n_row + n_full:n_row + n_full + n_out]
        acc_refs = refs[n_row + n_full + n_out:]
        i = pl.program_id(0)

        @pl.when(i == 0)
        def _():
            for r in acc_refs:
                r[...] = jnp.zeros(r.shape, r.dtype)

        res = fn(pos(i), *[r[...] for r in row_refs], *[r[...] for r in full_refs], *[r[...] for r in acc_refs])
        for r, v in zip(out_refs + acc_refs, res):
            r[...] = v.astype(r.dtype)

    def row_map(i, cb, shift):
        return (jnp.clip(pos(i) + shift, 0, n - 1), cb)

    in_specs = [pl.BlockSpec((tr, w), functools.partial(row_map, cb=cb, shift=sh)) for (_, w, cb, sh) in rows]
    in_specs += [pl.BlockSpec(f.shape, functools.partial(lambda i, nd: (0,) * nd, nd=f.ndim)) for f in fulls]
    out_specs = [pl.BlockSpec((tr, w), lambda i: (pos(i), 0)) for (w, _) in outs]
    out_specs += [pl.BlockSpec((r, w), lambda i: (0, 0)) for (r, w) in accs]
    out_shape = [jax.ShapeDtypeStruct((n_rows, w), dt) for (w, dt) in outs]
    out_shape += [jax.ShapeDtypeStruct((r, w), F32) for (r, w) in accs]
    return pl.pallas_call(
        body, name=name, grid=(n,), in_specs=in_specs, out_specs=out_specs, out_shape=out_shape,
        compiler_params=_params(("arbitrary",)),
    )(*[a for (a, _, _, _) in rows], *fulls)


def _colsum(x):
    return jnp.sum(x, axis=0, keepdims=True)


def _mean(x):
    return jnp.mean(x, axis=-1, keepdims=True)


def _modulate(x, sc, sh):
    return x * (1.0 + sc) + sh


def _shift_down(cur, prev, j):
    row = _iota(cur.shape, 0)
    return jnp.where(row < j, pltpu.roll(prev, j, 0), pltpu.roll(cur, j, 0))


def _shift_up(cur, nxt, j):
    tr = cur.shape[0]
    row = _iota(cur.shape, 0)
    return jnp.where(row < tr - j, pltpu.roll(cur, tr - j, 0), pltpu.roll(nxt, tr - j, 0))


def _conv(cur, prev, w, b):
    out = cur * w[3:4] + b
    for j in (1, 2, 3):
        out = out + _shift_down(cur, prev, j) * w[3 - j:4 - j]
    return out


def _conv_fwd(p, w_xs, b_xs, w_bc, b_bc, s):
    def fn(pos, xs, xs_prev, bc, bc_prev, w_xs, b_xs, w_bc, b_bc):
        first = pos == 0
        xs_prev = jnp.where(first, 0.0, xs_prev)
        bc_prev = jnp.where(first, 0.0, bc_prev)
        return _silu(_conv(xs, xs_prev, w_xs, b_xs)), _silu(_conv(bc, bc_prev, w_bc, b_bc))

    return _rowk("conv_fwd", fn, s, 256,
                 [(p, D, OFF_XS // D, 0), (p, D, OFF_XS // D, -1), (p, 512, OFF_BC // 512, 0), (p, 512, OFF_BC // 512, -1)],
                 [w_xs, b_xs, w_bc, b_bc], [(D, F32), (512, F32)], [])


def _conv_bwd(dxs_a, dbc_a, p, w_xs, b_xs, w_bc, b_bc, s):
    tr = 256
    n = s // tr

    def fn(pos, da1, da1n, x1, x1p, x1n, da2, da2n, x2, x2p, x2n, w1, b1, w2, b2, aw1, ab1, aw2, ab2):
        dx1, dw1, db1 = _conv_bwd_fn(pos, n, da1, da1n, x1, x1p, x1n, w1, b1)
        dx2, dw2, db2 = _conv_bwd_fn(pos, n, da2, da2n, x2, x2p, x2n, w2, b2)
        return dx1, dx2, aw1 + dw1, ab1 + db1, aw2 + dw2, ab2 + db2

    cx, cb = OFF_XS // D, OFF_BC // 512
    return _rowk("conv_bwd", fn, s, tr,
                 [(dxs_a, D, 0, 0), (dxs_a, D, 0, 1), (p, D, cx, 0), (p, D, cx, -1), (p, D, cx, 1),
                  (dbc_a, 512, 0, 0), (dbc_a, 512, 0, 1), (p, 512, cb, 0), (p, 512, cb, -1), (p, 512, cb, 1)],
                 [w_xs, b_xs, w_bc, b_bc], [(D, BF16), (512, BF16)], [(8, D), (1, D), (8, 512), (1, 512)])


def _conv_bwd_fn(pos, n, da, da_next, x, x_prev, x_next, w, b):
    first, last = pos == 0, pos == n - 1
    x_prev = jnp.where(first, 0.0, x_prev)
    dc = da * _dsilu(_conv(x, x_prev, w, b))
    dc_next = jnp.where(last, 0.0, da_next * _dsilu(_conv(x_next, x, w, b)))
    dx = dc * w[3:4]
    dws = [None] * 4
    dws[3] = _colsum(dc * x)
    for j in (1, 2, 3):
        dx = dx + _shift_up(dc, dc_next, j) * w[3 - j:4 - j]
        dws[3 - j] = _colsum(dc * _shift_down(x, x_prev, j))
    row = _iota((8, x.shape[1]), 0)
    dw = jnp.zeros((8, x.shape[1]), F32)
    for k in range(4):
        dw = jnp.where(row == k, dws[k], dw)
    return dx, dw, _colsum(dc)


def _ssd_gates(dtf, bias, a_log):
    lane = _iota(dtf.shape, 1)
    head = lane < NH
    dt = jnp.where(head, _softplus(dtf + bias), 0.0)
    a_neg = jnp.where(_iota(a_log.shape, 1) < NH, -jnp.exp(a_log), 0.0)
    a = dt * a_neg
    cs = _sel_left(_tri_lower(CHUNK), a)
    return dt, a_neg, cs


def _decay_mask(cs_ref, cst_ref, h):
    diff = cs_ref[:, h:h + 1] - cst_ref[h:h + 1, :]
    low = _iota((CHUNK, CHUNK), 1) <= _iota((CHUNK, CHUNK), 0)
    return jnp.where(low, jnp.exp(jnp.minimum(diff, 0.0)), 0.0)


def _ssd_fwd(xs_a, bc_a, p, bias128, alog128, dskip_x, s):
    nc = s // CHUNK
    t = CHUNK

    def body(xs_ref, bc_ref, dtf_ref, bias_ref, alog_ref, dsk_ref, y_ref, st_ref,
             state, x_sc, xw_sc, cs_sc, cst_sc, yd_sc):
        c = pl.program_id(0)

        @pl.when(c == 0)
        def _():
            state[...] = jnp.zeros(state.shape, F32)

        dt, _, cs = _ssd_gates(dtf_ref[...], bias_ref[...], alog_ref[...])
        cs_sc[...] = cs
        cst_sc[...] = cs.T
        cs_last = cs[t - 1:t, :]
        expand = _head_expand()
        ex = _sel_right(jnp.concatenate([dt, jnp.exp(cs), jnp.exp(cs_last - cs)], axis=0), expand)
        dt_x, eo_x, we_x = ex[0:t], ex[t:2 * t], ex[2 * t:3 * t]
        g_x = _sel_right(jnp.broadcast_to(jnp.exp(cs_last), (8, 128)), expand)[0:1]
        xs = xs_ref[...]
        x = xs * dt_x
        x_sc[...] = x.astype(BF16)
        xw_sc[...] = (x * we_x).astype(BF16)
        prev = state[...]
        st_ref[0] = prev
        prev_b = prev.astype(BF16)
        for g in range(2):
            cols = slice(g * 512, (g + 1) * 512)
            b_g = bc_ref[:, g * 128:(g + 1) * 128].astype(BF16)
            c_g = bc_ref[:, 256 + g * 128:256 + (g + 1) * 128].astype(BF16)
            gmat = _dot(c_g, b_g, NT)
            y_off = _dot(c_g, prev_b[:, cols]) * eo_x[:, cols]
            s_loc = _dot(b_g, xw_sc[:, cols], TN)
            state[:, cols] = g_x[:, cols] * prev[:, cols] + s_loc
            for e in range(HG):
                h = g * HG + e
                m = gmat * _decay_mask(cs_sc, cst_sc, h)
                yd_sc[:, h * HD:(h + 1) * HD] = _dot(m.astype(BF16), x_sc[:, h * HD:(h + 1) * HD])
            y_ref[:, cols] = yd_sc[:, cols] + y_off + dsk_ref[:, cols] * xs[:, cols]

    return pl.pallas_call(
        body, name="ssd_fwd", grid=(nc,),
        in_specs=[pl.BlockSpec((t, D), lambda c: (c, 0)),
                  pl.BlockSpec((t, 512), lambda c: (c, 0)),
                  pl.BlockSpec((t, 128), lambda c: (c, OFF_DTF // 128)),
                  pl.BlockSpec((1, 128), lambda c: (0, 0)),
                  pl.BlockSpec((1, 128), lambda c: (0, 0)),
                  pl.BlockSpec((1, D), lambda c: (0, 0))],
        out_specs=[pl.BlockSpec((t, D), lambda c: (c, 0)),
                   pl.BlockSpec((1, NSTATE, D), lambda c: (c, 0, 0))],
        out_shape=[jax.ShapeDtypeStruct((s, D), F32), jax.ShapeDtypeStruct((nc, NSTATE, D), F32)],
        scratch_shapes=[pltpu.VMEM((NSTATE, D), F32), pltpu.VMEM((t, D), BF16), pltpu.VMEM((t, D), BF16),
                        pltpu.VMEM((t, 128), F32), pltpu.VMEM((128, t), F32), pltpu.VMEM((t, D), F32)],
        compiler_params=_params(("arbitrary",)),
    )(xs_a, bc_a, p, bias128, alog128, dskip_x)


def _ssd_bwd(dy, xs_a, bc_a, p, states, bias128, alog128, dskip_x, s):
    nc = s // CHUNK
    t = CHUNK

    def body(dy_ref, xs_ref, bc_ref, dtf_ref, st_ref, bias_ref, alog_ref, dsk_ref,
             dxs_ref, dbc_ref, ddt_ref, dalog_ref, dskip_ref,
             dstate, x_sc, dy_sc, dx_sc, deo_sc, dwe_sc, cs_sc, cst_sc, dcol_sc, drow_sc):
        i = pl.program_id(0)

        @pl.when(i == 0)
        def _():
            dstate[...] = jnp.zeros(dstate.shape, F32)
            dalog_ref[...] = jnp.zeros(dalog_ref.shape, F32)
            dskip_ref[...] = jnp.zeros(dskip_ref.shape, F32)

        dtf = dtf_ref[...]
        dt, a_neg, cs = _ssd_gates(dtf, bias_ref[...], alog_ref[...])
        cs_sc[...] = cs
        cst_sc[...] = cs.T
        cs_last = cs[t - 1:t, :]
        eo, we, g_end = jnp.exp(cs), jnp.exp(cs_last - cs), jnp.exp(cs_last)
        expand, reduce = _head_expand(), _head_reduce()
        ex = _sel_right(jnp.concatenate([dt, eo, we], axis=0), expand)
        dt_x, eo_x, we_x = ex[0:t], ex[t:2 * t], ex[2 * t:3 * t]
        g_x = _sel_right(jnp.broadcast_to(g_end, (8, 128)), expand)[0:1]
        xs = xs_ref[...]
        dyv = dy_ref[...]
        x = xs * dt_x
        x_sc[...] = x.astype(BF16)
        dy_sc[...] = dyv.astype(BF16)
        dyo_b = (dyv * eo_x).astype(BF16)
        xw_b = (x * we_x).astype(BF16)
        prev = st_ref[0]
        prev_b = prev.astype(BF16)
        dnext = dstate[...]
        dnext_b = dnext.astype(BF16)
        dcol_sc[...] = jnp.zeros(dcol_sc.shape, F32)
        drow_sc[...] = jnp.zeros(drow_sc.shape, F32)
        lane_row = _iota((1, 128), 1)
        sub_col = _iota((128, 1), 0)
        for g in range(2):
            cols = slice(g * 512, (g + 1) * 512)
            b_g = bc_ref[:, g * 128:(g + 1) * 128].astype(BF16)
            c_g = bc_ref[:, 256 + g * 128:256 + (g + 1) * 128].astype(BF16)
            gmat = _dot(c_g, b_g, NT)
            b_ds = _dot(b_g, dnext_b[:, cols])
            c_s = _dot(c_g, prev_b[:, cols])
            dx_sc[:, cols] = b_ds * we_x[:, cols]
            deo_sc[:, cols] = dyv[:, cols] * c_s
            dwe_sc[:, cols] = b_ds * x[:, cols]
            db = _dot(xw_b[:, cols], dnext_b[:, cols], NT)
            dc = _dot(dyo_b[:, cols], prev_b[:, cols], NT)
            dstate[:, cols] = g_x[:, cols] * dnext[:, cols] + _dot(c_g, dyo_b[:, cols], TN)
            dg = jnp.zeros((t, t), F32)
            for e in range(HG):
                h = g * HG + e
                hc = slice(h * HD, (h + 1) * HD)
                lmat = _decay_mask(cs_sc, cst_sc, h)
                m = gmat * lmat
                dx_sc[:, hc] += _dot(m.astype(BF16), dy_sc[:, hc], TN)
                dm = _dot(dy_sc[:, hc], x_sc[:, hc], NT)
                dg = dg + dm * lmat
                qm = dm * m
                dcol_sc[...] += jnp.sum(qm, axis=1, keepdims=True) * (lane_row == h).astype(F32)
                drow_sc[...] += (sub_col == h).astype(F32) * jnp.sum(qm, axis=0, keepdims=True)
            dg_b = dg.astype(BF16)
            dbc_ref[:, g * 128:(g + 1) * 128] = db + _dot(dg_b, c_g, TN)
            dbc_ref[:, 256 + g * 128:256 + (g + 1) * 128] = dc + _dot(dg_b, b_g)
        d_eo = _sel_right(deo_sc[...], reduce)
        d_we = _sel_right(dwe_sc[...], reduce)
        d_gend = _sel_right(jnp.broadcast_to(_colsum(dnext * prev), (8, D)), reduce)[0:1]
        d_cs = dcol_sc[...] - drow_sc[...].T + d_eo * eo - d_we * we
        extra = _colsum(d_we * we) + d_gend * g_end
        d_cs = d_cs + jnp.where(_iota((t, 128), 0) == t - 1, extra, 0.0)
        da = _sel_left(_tri_upper(t), d_cs)
        dx = dx_sc[...]
        ddt = _sel_right(dx * xs, reduce) + da * a_neg
        dxs_ref[...] = dx * dt_x + dsk_ref[...] * dyv
        ddt_ref[...] = jnp.where(_iota((t, 128), 1) < NH, ddt * _sigmoid(dtf + bias_ref[...]), 0.0)
        dalog_ref[...] += _colsum(da * dt) * a_neg
        dskip_ref[...] += _sel_right(jnp.broadcast_to(_colsum(dyv * xs), (8, D)), reduce)[0:1]

    rev = lambda i: nc - 1 - i
    return pl.pallas_call(
        body, name="ssd_bwd", grid=(nc,),
        in_specs=[pl.BlockSpec((t, D), lambda i: (rev(i), 0)),
                  pl.BlockSpec((t, D), lambda i: (rev(i), 0)),
                  pl.BlockSpec((t, 512), lambda i: (rev(i), 0)),
                  pl.BlockSpec((t, 128), lambda i: (rev(i), OFF_DTF // 128)),
                  pl.BlockSpec((1, NSTATE, D), lambda i: (rev(i), 0, 0)),
                  pl.BlockSpec((1, 128), lambda i: (0, 0)),
                  pl.BlockSpec((1, 128), lambda i: (0, 0)),
                  pl.BlockSpec((1, D), lambda i: (0, 0))],
        out_specs=[pl.BlockSpec((t, D), lambda i: (rev(i), 0)),
                   pl.BlockSpec((t, 512), lambda i: (rev(i), 0)),
                   pl.BlockSpec((t, 128), lambda i: (rev(i), 0)),
                   pl.BlockSpec((1, 128), lambda i: (0, 0)),
                   pl.BlockSpec((1, 128), lambda i: (0, 0))],
        out_shape=[jax.ShapeDtypeStruct((s, D), F32), jax.ShapeDtypeStruct((s, 512), F32),
                   jax.ShapeDtypeStruct((s, 128), F32), jax.ShapeDtypeStruct((1, 128), F32),
                   jax.ShapeDtypeStruct((1, 128), F32)],
        scratch_shapes=[pltpu.VMEM((NSTATE, D), F32), pltpu.VMEM((t, D), BF16), pltpu.VMEM((t, D), BF16),
                        pltpu.VMEM((t, D), F32), pltpu.VMEM((t, D), F32), pltpu.VMEM((t, D), F32),
                        pltpu.VMEM((t, 128), F32), pltpu.VMEM((128, t), F32),
                        pltpu.VMEM((t, 128), F32), pltpu.VMEM((128, t), F32)],
        compiler_params=_params(("arbitrary",)),
    )(dy, xs_a, bc_a, p, states, bias128, alog128, dskip_x)


def _gate_lanes(shape):
    lane = _iota(shape, 1)
    return (lane >= NH) & (lane < 2 * NH)


def _cum_fwd(p, bias128, s):
    tr = min(512, s)

    def body(dtf_ref, bias_ref, o_ref, carry):
        @pl.when(pl.program_id(0) == 0)
        def _():
            carry[...] = jnp.zeros(carry.shape, F32)

        lf = jnp.where(_gate_lanes((tr, 128)), _log_sigmoid(dtf_ref[...] + bias_ref[...]), 0.0)
        cum = _sel_left(_tri_lower(tr), lf) + carry[...]
        carry[...] = cum[tr - 1:tr, :]
        o_ref[...] = cum

    return pl.pallas_call(
        body, name="cum_fwd", grid=(s // tr,),
        in_specs=[pl.BlockSpec((tr, 128), lambda i: (i, OFF_DTF // 128)), pl.BlockSpec((1, 128), lambda i: (0, 0))],
        out_specs=pl.BlockSpec((tr, 128), lambda i: (i, 0)),
        out_shape=jax.ShapeDtypeStruct((s, 128), F32),
        scratch_shapes=[pltpu.VMEM((1, 128), F32)],
        compiler_params=_params(("arbitrary",)),
    )(p, bias128)


def _cum_bwd(dcum, ddt_raw, p, bias128, s):
    tr = min(512, s)

    def fn(pos, dcum, ddt, dtf, bias, carry, acc):
        suffix = _sel_left(_tri_upper(tr), dcum) + carry
        dfr = jnp.where(_gate_lanes((tr, 128)), suffix * _sigmoid(-(dtf + bias)), 0.0)
        out = ddt + dfr
        return out, suffix[0:1, :], acc + _colsum(out)

    return _rowk("cum_bwd", fn, s, tr, [(dcum, 128, 0, 0), (ddt_raw, 128, 0, 0), (p, 128, OFF_DTF // 128, 0)],
                 [bias128], [(128, BF16)], [(1, 128), (1, 128)], reverse=True)


ATT_BLOCK = 512
ATT_STRIP = 32


def _head_part(shape, h, dim):
    i = _iota(shape, dim)
    return (i >= h * HD) & (i < (h + 1) * HD)


def _k_augmented(k_blk, cum_blk, j, h):
    tk = k_blk.shape[0]
    lane = _iota((tk, 128), 1)
    col = jnp.sum(jnp.where(lane == NH + 2 * j + h, cum_blk, 0.0), axis=1, keepdims=True)
    c0, c1, c2 = [c.astype(F32) for c in _split3(-col)]
    aug = jnp.where(lane == 0, c0, jnp.where(lane == 1, c1, jnp.where(lane == 2, c2, 0.0)))
    return jnp.concatenate([jnp.where(_head_part((tk, 128), h, 1), k_blk, 0.0), aug], axis=1).astype(BF16)


def _q_augmented_t(q_blk):
    tq = q_blk.shape[0]
    ones = (_iota((128, tq), 0) < 3).astype(BF16)
    return jnp.concatenate([(q_blk * ATT_SCALE).T.astype(BF16), ones], axis=0)


def _rows01(r0, r1):
    sub = _iota((8, r0.shape[1]), 0)
    return jnp.where(sub == 0, r0, jnp.where(sub == 1, r1, 0.0))


def _fold8(x, op, cur):
    for g in range(x.shape[0] // 8):
        cur = op(cur, x[8 * g:8 * (g + 1), :])
    return cur


def _attn_fwd(p, cum, s):
    t = min(ATT_BLOCK, s)
    nq = s // t
    r = ATT_STRIP

    def body(q_ref, k_ref, v_ref, c_ref, o_ref, lse_ref, kaug_sc, vt_sc, s_sc, p_sc, m_sc, l_sc, acc_sc):
        j, qi = pl.program_id(0), pl.program_id(1)

        @pl.when(qi == 0)
        def _():
            for c in range(nq):
                rows = slice(c * t, (c + 1) * t)
                k_blk, vt = k_ref[rows, :], v_ref[rows, :].T
                for h in range(2):
                    kaug_sc[h, rows, :] = _k_augmented(k_blk, c_ref[rows, :], j, h)
                    vt_sc[h, :, rows] = jnp.where(_head_part((128, t), h, 0), vt, 0.0).astype(BF16)

        qaug_t = _q_augmented_t(q_ref[...])
        m_sc[...] = jnp.full(m_sc.shape, -1e30, F32)
        l_sc[...] = jnp.zeros(l_sc.shape, F32)
        acc_sc[...] = jnp.zeros(acc_sc.shape, F32)
        top = _iota((128, t), 0) < HD

        def block(kb, diagonal):
            kv = pl.ds(pl.multiple_of(kb * t, t), t)
            for h in range(2):
                s_sc[h] = _dot(kaug_sc[h, kv, :], qaug_t)

            def strip(i):
                return pl.ds(pl.multiple_of(i * r, r), r)

            def pass_max(i, carry):
                out = []
                for h in range(2):
                    x = s_sc[h, strip(i), :]
                    if diagonal:
                        x = jnp.where(_iota((r, t), 1) >= i * r + _iota((r, t), 0), x, -1e30)
                        s_sc[h, strip(i), :] = x
                    out.append(_fold8(x, jnp.maximum, carry[h]))
                return tuple(out)

            low = jnp.full((8, t), -1e30, F32)
            tops = lax.fori_loop(0, t // r, pass_max, (low, low), unroll=2)
            m_new, alpha = [], []
            for h in range(2):
                m_prev = m_sc[h, 0:1, :]
                m_new.append(jnp.maximum(m_prev, jnp.max(tops[h], axis=0, keepdims=True)))
                alpha.append(jnp.exp(m_prev - m_new[h]))
                m_sc[h, 0:1, :] = m_new[h]

            def pass_exp(i, carry):
                out = []
                for h in range(2):
                    pr = jnp.exp(s_sc[h, strip(i), :] - m_new[h])
                    p_sc[h, strip(i), :] = pr.astype(BF16)
                    out.append(_fold8(pr, jnp.add, carry[h]))
                return tuple(out)

            zero = jnp.zeros((8, t), F32)
            sums = lax.fori_loop(0, t // r, pass_exp, (zero, zero), unroll=2)
            for h in range(2):
                l_sc[h, 0:1, :] = alpha[h] * l_sc[h, 0:1, :] + jnp.sum(sums[h], axis=0, keepdims=True)
            acc_sc[...] = (acc_sc[...] * jnp.where(top, alpha[0], alpha[1])
                           + _dot(vt_sc[0, :, kv], p_sc[0]) + _dot(vt_sc[1, :, kv], p_sc[1]))

        def earlier(kb, carry):
            block(kb, False)
            return carry

        lax.fori_loop(0, qi, earlier, 0)
        block(qi, True)
        l0, l1 = l_sc[0, 0:1, :], l_sc[1, 0:1, :]
        o_ref[...] = (acc_sc[...] / jnp.where(top, l0, l1)).T
        lse_ref[0] = _rows01(m_sc[0, 0:1, :] + jnp.log(l0), m_sc[1, 0:1, :] + jnp.log(l1))

    return pl.pallas_call(
        body, name="attn_fwd", grid=(NH // 2, nq),
        in_specs=[pl.BlockSpec((t, 128), lambda j, qi: (qi, OFF_Q // 128 + j)),
                  pl.BlockSpec((s, 128), lambda j, qi: (0, OFF_K // 128 + j)),
                  pl.BlockSpec((s, 128), lambda j, qi: (0, OFF_V // 128 + j)),
                  pl.BlockSpec((s, 128), lambda j, qi: (0, 0))],
        out_specs=[pl.BlockSpec((t, 128), lambda j, qi: (qi, j)),
                   pl.BlockSpec((1, 8, t), lambda j, qi: (j, 0, qi))],
        out_shape=[jax.ShapeDtypeStruct((s, D), F32), jax.ShapeDtypeStruct((NH // 2, 8, s), F32)],
        scratch_shapes=[pltpu.VMEM((2, s, 256), BF16), pltpu.VMEM((2, 128, s), BF16), pltpu.VMEM((2, t, t), F32),
                        pltpu.VMEM((2, t, t), BF16), pltpu.VMEM((2, 8, t), F32), pltpu.VMEM((2, 8, t), F32),
                        pltpu.VMEM((128, t), F32)],
        compiler_params=_params(("parallel", "arbitrary")),
    )(p, p, p, cum)


def _attn_bwd(p, cum, o, lse, do, s):
    t = min(ATT_BLOCK, s)
    nq = s // t
    r = ATT_STRIP

    def body(q_ref, k_ref, v_ref, c_ref, o_ref, lse_ref, do_ref, dq_ref, dk_ref, dv_ref, dc_ref, dr_ref,
             qaugt_sc, qh_sc, dot_sc, doh_sc, delta_sc, dqt_sc, dr_sc, kaug_sc, vh_sc, kt_sc, s_sc, dp_sc, p_sc, ds_sc,
             dk_sc, dv_sc, dc_sc):
        j, ki = pl.program_id(0), pl.program_id(1)

        @pl.when(ki == 0)
        def _():
            for c in range(nq):
                rows = slice(c * t, (c + 1) * t)
                q_blk, do_blk = q_ref[rows, :], do_ref[rows, :]
                qaugt_sc[:, rows] = _q_augmented_t(q_blk)
                dot_sc[:, rows] = do_blk.T.astype(BF16)
                prod_t = (do_blk * o_ref[rows, :]).T
                delta_sc[:, rows] = _rows01(jnp.sum(prod_t[0:HD], axis=0, keepdims=True),
                                            jnp.sum(prod_t[HD:], axis=0, keepdims=True))
                for h in range(2):
                    head = _head_part((t, 128), h, 1)
                    qh_sc[h, rows, :] = jnp.where(head, q_blk * ATT_SCALE, 0.0).astype(BF16)
                    doh_sc[h, rows, :] = jnp.where(head, do_blk, 0.0).astype(BF16)
            dqt_sc[...] = jnp.zeros(dqt_sc.shape, F32)
            dr_sc[...] = jnp.zeros(dr_sc.shape, F32)

        k_blk, v_blk = k_ref[...], v_ref[...]
        kt = k_blk.T
        for h in range(2):
            kaug_sc[h] = _k_augmented(k_blk, c_ref[...], j, h)
            vh_sc[h] = jnp.where(_head_part((t, 128), h, 1), v_blk, 0.0).astype(BF16)
            kt_sc[h] = jnp.where(_head_part((128, t), h, 0), kt, 0.0).astype(BF16)
        dk_sc[...] = jnp.zeros(dk_sc.shape, F32)
        dv_sc[...] = jnp.zeros(dv_sc.shape, F32)
        dc_sc[...] = jnp.zeros(dc_sc.shape, F32)

        def block(qb, diagonal):
            qs = pl.ds(pl.multiple_of(qb * t, t), t)
            for h in range(2):
                s_sc[h] = _dot(kaug_sc[h], qaugt_sc[:, qs])
                dp_sc[h] = _dot(vh_sc[h], dot_sc[:, qs])
            lse_row = [lse_ref[0, h:h + 1, qs] for h in range(2)]
            delta_row = [delta_sc[h:h + 1, qs] for h in range(2)]

            def strips(i, carry):
                rows = pl.ds(pl.multiple_of(i * r, r), r)
                out = []
                for h in range(2):
                    x = s_sc[h, rows, :]
                    if diagonal:
                        x = jnp.where(_iota((r, t), 1) >= i * r + _iota((r, t), 0), x, -1e30)
                    pr = jnp.exp(x - lse_row[h])
                    ds = pr * (dp_sc[h, rows, :] - delta_row[h])
                    p_sc[h, rows, :] = pr.astype(BF16)
                    ds_sc[h, rows, :] = ds.astype(BF16)
                    dc_sc[h, rows, :] += sum(ds[:, 128 * g:128 * (g + 1)] for g in range(t // 128))
                    out.append(_fold8(ds, jnp.add, carry[h]))
                return tuple(out)

            zero = jnp.zeros((8, t), F32)
            dr = lax.fori_loop(0, t // r, strips, (zero, zero), unroll=2)
            for h in range(2):
                dr_sc[h, :, qs] += dr[h]
            dv_sc[...] += _dot(p_sc[0], doh_sc[0, qs, :]) + _dot(p_sc[1], doh_sc[1, qs, :])
            dk_sc[...] += _dot(ds_sc[0], qh_sc[0, qs, :]) + _dot(ds_sc[1], qh_sc[1, qs, :])
            dqt_sc[:, qs] += _dot(kt_sc[0], ds_sc[0]) + _dot(kt_sc[1], ds_sc[1])

        def later(qb, carry):
            block(qb, False)
            return carry

        block(ki, True)
        lax.fori_loop(ki + 1, nq, later, 0)
        dk_ref[...] = dk_sc[...].astype(BF16)
        dv_ref[...] = dv_sc[...].astype(BF16)
        dc_ref[...] = jnp.where(_iota((t, 128), 1) < HD, jnp.sum(dc_sc[0], axis=1, keepdims=True),
                                jnp.sum(dc_sc[1], axis=1, keepdims=True))

        @pl.when(ki == nq - 1)
        def _():
            for c in range(nq):
                rows = slice(c * t, (c + 1) * t)
                dq_ref[rows, :] = dqt_sc[:, rows].T * ATT_SCALE
            dr_ref[0] = _rows01(jnp.sum(dr_sc[0], axis=0, keepdims=True), jnp.sum(dr_sc[1], axis=0, keepdims=True))

    whole = lambda off: pl.BlockSpec((s, 128), functools.partial(lambda j, ki, off: (0, off + j), off=off))
    return pl.pallas_call(
        body, name="attn_bwd", grid=(NH // 2, nq),
        in_specs=[whole(OFF_Q // 128),
                  pl.BlockSpec((t, 128), lambda j, ki: (ki, OFF_K // 128 + j)),
                  pl.BlockSpec((t, 128), lambda j, ki: (ki, OFF_V // 128 + j)),
                  pl.BlockSpec((t, 128), lambda j, ki: (ki, 0)),
                  whole(0),
                  pl.BlockSpec((1, 8, s), lambda j, ki: (j, 0, 0)),
                  whole(0)],
        out_specs=[whole(0),
                   pl.BlockSpec((t, 128), lambda j, ki: (ki, j)),
                   pl.BlockSpec((t, 128), lambda j, ki: (ki, j)),
                   pl.BlockSpec((t, 128), lambda j, ki: (ki, j)),
                   pl.BlockSpec((1, 8, s), lambda j, ki: (j, 0, 0))],
        out_shape=[jax.ShapeDtypeStruct((s, D), F32), jax.ShapeDtypeStruct((s, D), BF16), jax.ShapeDtypeStruct((s, D), BF16),
                   jax.ShapeDtypeStruct((s, D), F32), jax.ShapeDtypeStruct((NH // 2, 8, s), F32)],
        scratch_shapes=[pltpu.VMEM((256, s), BF16), pltpu.VMEM((2, s, 128), BF16), pltpu.VMEM((128, s), BF16),
                        pltpu.VMEM((2, s, 128), BF16), pltpu.VMEM((8, s), F32), pltpu.VMEM((128, s), F32),
                        pltpu.VMEM((2, 8, s), F32), pltpu.VMEM((2, t, 256), BF16), pltpu.VMEM((2, t, 128), BF16),
                        pltpu.VMEM((2, 128, t), BF16), pltpu.VMEM((2, t, t), F32), pltpu.VMEM((2, t, t), F32),
                        pltpu.VMEM((2, t, t), BF16), pltpu.VMEM((2, t, t), BF16), pltpu.VMEM((t, 128), F32),
                        pltpu.VMEM((t, 128), F32), pltpu.VMEM((2, t, 128), F32)],
        compiler_params=_params(("parallel", "arbitrary")),
    )(p, p, p, cum, o, lse, do)


def _ln_stats(u):
    mu = _mean(u)
    d = u - mu
    rstd = lax.rsqrt(_mean(d * d) + EPS)
    return d * rstd, rstd


def _ln_bwd(dx, xh, rstd, gam):
    dxh = dx * gam
    return rstd * (dxh - _mean(dxh) - xh * _mean(dxh * xh))


def _rms_bwd(d, xn, r, w):
    t = d * w
    return r * (t - xn * _mean(t * xn)), _colsum(d * xn)


def _mix_norm(y, p, att, w_ssm, w_att, s):
    def fn(pos, y, z, att, w1, w2):
        g = y * _silu(z)
        n1 = g * lax.rsqrt(_mean(g * g) + EPS) * w1
        n2 = att * lax.rsqrt(_mean(att * att) + EPS) * w2
        return (jnp.concatenate([n1, n2], axis=1),)

    return _rowk("mix_norm", fn, s, 256, [(y, D, 0, 0), (p, D, OFF_Z // D, 0), (att, D, 0, 0)],
                 [w_ssm, w_att], [(2 * D, BF16)], [])[0]


def _mix_norm_bwd(dmix, y, p, att, w_ssm, w_att, s):
    def fn(pos, dmix, y, z, att, w1, w2, a1, a2):
        sz = _silu(z)
        g = y * sz
        r1 = lax.rsqrt(_mean(g * g) + EPS)
        dg, dw1 = _rms_bwd(dmix[:, :D], g * r1, r1, w1)
        r2 = lax.rsqrt(_mean(att * att) + EPS)
        datt, dw2 = _rms_bwd(dmix[:, D:], att * r2, r2, w2)
        return dg * sz, dg * y * _dsilu(z), datt, a1 + dw1, a2 + dw2

    return _rowk("mix_norm_bwd", fn, s, 256, [(dmix, 2 * D, 0, 0), (y, D, 0, 0), (p, D, OFF_Z // D, 0), (att, D, 0, 0)],
                 [w_ssm, w_att], [(D, F32), (D, BF16), (D, F32)], [(1, D), (1, D)])


def _ln1(x0, y, g1, gam, bet, sc2, sh2, s):
    def fn(pos, x0, y, g1, gam, bet, sc2, sh2):
        xh, _ = _ln_stats(ALPHA * x0 + (1.0 + g1) * y)
        x1 = xh * gam + bet
        return x1, _modulate(x1, sc2, sh2)

    return _rowk("ln1", fn, s, 256, [(x0, D, 0, 0), (y, D, 0, 0)], [g1, gam, bet, sc2, sh2], [(D, F32), (D, BF16)], [])


def _ln2_loss(x1, ff, tgt, g2, gam, bet, s):
    def fn(pos, x1, ff, tgt, g2, gam, bet, a_loss, a_dgam, a_dbet, a_dg2):
        xh, rstd = _ln_stats(ALPHA * x1 + (1.0 + g2) * ff)
        err = xh * gam + bet - tgt
        dx2 = err * (1.0 / D)
        du = _ln_bwd(dx2, xh, rstd, gam)
        return (du, du * (1.0 + g2), a_loss + _colsum(err * err), a_dgam + _colsum(dx2 * xh),
                a_dbet + _colsum(dx2), a_dg2 + _colsum(du * ff))

    return _rowk("ln2_loss", fn, s, 256, [(x1, D, 0, 0), (ff, D, 0, 0), (tgt, D, 0, 0)], [g2, gam, bet],
                 [(D, F32), (D, BF16)], [(1, D)] * 4)


def _ln1_bwd(dh2, du2, x0, y, g1, gam, bet, sc2, s):
    def fn(pos, dh2, du2, x0, y, g1, gam, bet, sc2, a_sc, a_sh, a_gam, a_bet, a_g1):
        xh, rstd = _ln_stats(ALPHA * x0 + (1.0 + g1) * y)
        x1 = xh * gam + bet
        dx1 = ALPHA * du2 + dh2 * (1.0 + sc2)
        du1 = _ln_bwd(dx1, xh, rstd, gam)
        return (du1, du1 * (1.0 + g1), a_sc + _colsum(dh2 * x1), a_sh + _colsum(dh2), a_gam + _colsum(dx1 * xh),
                a_bet + _colsum(dx1), a_g1 + _colsum(du1 * y))

    return _rowk("ln1_bwd", fn, s, 256, [(dh2, D, 0, 0), (du2, D, 0, 0), (x0, D, 0, 0), (y, D, 0, 0)],
                 [g1, gam, bet, sc2], [(D, F32), (D, BF16)], [(1, D)] * 5)


def _input_grad(dh1, du1, x0, sc1, s):
    def fn(pos, dh1, du1, x0, sc1, a_sc, a_sh):
        return ALPHA * du1 + dh1 * (1.0 + sc1), a_sc + _colsum(dh1 * x0), a_sh + _colsum(dh1)

    return _rowk("input_grad", fn, s, 256, [(dh1, D, 0, 0), (du1, D, 0, 0), (x0, D, 0, 0)], [sc1],
                 [(D, F32)], [(1, D)] * 2)


def _adamw(name, w, g, m, v, *, tr, slots):
    r, c = w.shape

    def body(w_ref, g_ref, m_ref, v_ref, g_out, d_out, m_out, v_out):
        if slots:
            grad = g_ref[0].astype(F32)
            for k in range(1, N_DEV):
                grad = grad + g_ref[k].astype(F32)
        else:
            grad = g_ref[...]
        m_new = ADAM_B1 * m_ref[...] + (1.0 - ADAM_B1) * grad
        v_new = ADAM_B2 * v_ref[...] + (1.0 - ADAM_B2) * (grad * grad)
        m_hat = m_new / (1.0 - ADAM_B1 ** ADAM_STEP)
        v_hat = v_new / (1.0 - ADAM_B2 ** ADAM_STEP)
        g_out[...] = grad
        d_out[...] = -ADAM_LR * (m_hat / (jnp.sqrt(v_hat) + ADAM_EPS) + ADAM_WD * w_ref[...])
        m_out[...] = m_new
        v_out[...] = v_new

    tile = pl.BlockSpec((tr, c), lambda i: (i, 0))
    g_spec = pl.BlockSpec((N_DEV, tr, c), lambda i: (0, i, 0)) if slots else tile
    return pl.pallas_call(
        body, name=name, grid=(r // tr,),
        in_specs=[tile, g_spec, tile, tile], out_specs=[tile] * 4,
        out_shape=[jax.ShapeDtypeStruct((r, c), F32)] * 4,
        compiler_params=_params(("parallel",)),
    )(w, g, m, v)


def _dot_f32(a, b, dims=NN):
    a0, a1, a2 = _split3(a)
    b0, b1, b2 = _split3(b)
    acc = _dot(a0, b0, dims)
    for x, y in ((a0, b1), (a1, b0), (a1, b1), (a0, b2), (a2, b0)):
        acc = acc + _dot(x, y, dims)
    return acc


def _ada_mod(c_all, w_shard, b_shard):
    def body(c_ref, w_ref, b_ref, o_ref):
        act = _silu(c_ref[...])
        act16 = jnp.concatenate([act, jnp.zeros_like(act)], axis=0)
        o_ref[...] = _dot_f32(act16, w_ref[...])[0:N_DEV] + b_ref[...]

    return pl.pallas_call(
        body, name="ada_mod", out_shape=jax.ShapeDtypeStruct((N_DEV, w_shard.shape[1]), F32),
        compiler_params=_params(None),
    )(c_all, w_shard, b_shard)


def _ada_grad(c_all, dmod_cols, dmod_all):
    def body(c_ref, dc_ref, da_ref, gw_ref, gb_ref):
        act = _silu(c_ref[...])
        act16 = jnp.concatenate([act, jnp.zeros_like(act)], axis=0)
        dm = dc_ref[...]
        dm16 = jnp.concatenate([dm, jnp.zeros_like(dm)], axis=0)
        gw_ref[...] = _dot_f32(act16, dm16, TN)
        gb_ref[...] = _colsum(da_ref[...])

    return pl.pallas_call(
        body, name="ada_grad",
        out_shape=[jax.ShapeDtypeStruct((D, dmod_cols.shape[1]), F32), jax.ShapeDtypeStruct((1, 6 * D), F32)],
        compiler_params=_params(None),
    )(c_all, dmod_cols, dmod_all)


def _sum_slots(name, g):
    def body(g_ref, o_ref):
        acc = g_ref[0]
        for k in range(1, N_DEV):
            acc = acc + g_ref[k]
        o_ref[...] = acc

    return pl.pallas_call(body, name=name, out_shape=jax.ShapeDtypeStruct(g.shape[1:], F32),
                          compiler_params=_params(None))(g)


def _exchange(name, xs, scatter):
    n = len(xs)
    n_peer = N_DEV - 1

    def body(*refs):
        x_refs, o_refs = refs[:n], refs[n:2 * n]
        send_sems, recv_sems, local_sems = refs[2 * n:]
        mx, my, mc = lax.axis_index("x"), lax.axis_index("y"), lax.axis_index("c")
        me = 4 * mx + 2 * my + mc

        def src(a, slot):
            return x_refs[a].at[slot] if scatter else x_refs[a]

        own = [pltpu.make_async_copy(src(a, me), o_refs[a].at[me], local_sems.at[a]) for a in range(n)]
        for cp in own:
            cp.start()
        sends = []
        for d in range(1, N_DEV):
            px = 1 - mx if d & 4 else mx
            py = 1 - my if d & 2 else my
            pc = 1 - mc if d & 1 else mc
            peer = 4 * px + 2 * py + pc
            for a in range(n):
                def copy(src_slot, dst_slot, a=a, d=d, to=(px, py, pc)):
                    return pltpu.make_async_remote_copy(
                        src_ref=src(a, src_slot), dst_ref=o_refs[a].at[dst_slot],
                        send_sem=send_sems.at[a * n_peer + d - 1], recv_sem=recv_sems.at[a * n_peer + d - 1],
                        device_id=to, device_id_type=pl.DeviceIdType.MESH)

                out = copy(peer, me)
                out.start()
                sends.append((out, copy(me, peer)))
        for _, arrival in sends:
            arrival.wait_recv()
        for out, _ in sends:
            out.wait_send()
        for cp in own:
            cp.wait()

    shapes = [tuple(x.shape[1:] if scatter else x.shape) for x in xs]
    return pl.pallas_call(
        body, name=name,
        in_specs=[pl.BlockSpec(memory_space=pl.ANY)] * n, out_specs=[pl.BlockSpec(memory_space=pl.ANY)] * n,
        out_shape=[jax.ShapeDtypeStruct((N_DEV,) + sh, x.dtype) for sh, x in zip(shapes, xs)],
        scratch_shapes=[pltpu.SemaphoreType.DMA((n * n_peer,)), pltpu.SemaphoreType.DMA((n * n_peer,)),
                        pltpu.SemaphoreType.DMA((n,))],
        compiler_params=pltpu.CompilerParams(has_side_effects=True),
    )(*xs)


def _after(x, zero):
    return x if zero is None else x + zero.reshape(-1)[0].astype(x.dtype)


_HBM = pl.BlockSpec(memory_space=pltpu.HBM)
_SEM = pl.BlockSpec(memory_space=pltpu.SEMAPHORE)


def _exchange_copies(x_refs, land_refs, send_sems, recv_sems, scatter):
    n = len(x_refs)
    n_peer = N_DEV - 1
    mx, my, mc = lax.axis_index("x"), lax.axis_index("y"), lax.axis_index("c")
    me = 4 * mx + 2 * my + mc
    pairs = []
    for d in range(1, N_DEV):
        px = 1 - mx if d & 4 else mx
        py = 1 - my if d & 2 else my
        pc = 1 - mc if d & 1 else mc
        peer = 4 * px + 2 * py + pc
        for a in range(n):
            def copy(src_slot, dst_slot, a=a, d=d, to=(px, py, pc)):
                return pltpu.make_async_remote_copy(
                    src_ref=x_refs[a].at[src_slot] if scatter else x_refs[a], dst_ref=land_refs[a].at[dst_slot],
                    send_sem=send_sems.at[a * n_peer + d - 1], recv_sem=recv_sems.at[a * n_peer + d - 1],
                    device_id=to, device_id_type=pl.DeviceIdType.MESH)

            pairs.append((copy(peer, me), copy(me, peer)))
    return me, pairs


def _exchange_async(name, xs, scatter, collective_id):
    n = len(xs)
    shapes = [tuple(x.shape[1:] if scatter else x.shape) for x in xs]
    x_refs = [jax.new_ref(x, memory_space=pltpu.MemorySpace.HBM) for x in xs]
    land_refs = [jax.empty_ref(jax.ShapeDtypeStruct((N_DEV,) + sh, x.dtype), memory_space=pltpu.MemorySpace.HBM)
                 for sh, x in zip(shapes, xs)]

    @pl.kernel(mesh=plsc.ScalarSubcoreMesh(axis_name="sequencer", num_cores=1), name=name,
               scratch_types=(pltpu.SemaphoreType.DMA((n * (N_DEV - 1),)), pltpu.SemaphoreType.DMA((n * (N_DEV - 1),)),
                              pltpu.SemaphoreType.DMA((n,))),
               compiler_params=pltpu.CompilerParams(collective_id=collective_id))
    def launch(send_sems, recv_sems, own_sems):
        barrier = pltpu.get_barrier_semaphore()
        mx, my, mc = lax.axis_index("x"), lax.axis_index("y"), lax.axis_index("c")
        for d in range(1, N_DEV):
            peer = (1 - mx if d & 4 else mx, 1 - my if d & 2 else my, 1 - mc if d & 1 else mc)
            pl.semaphore_signal(barrier, inc=1, device_id=peer, device_id_type=pl.DeviceIdType.MESH)
        pl.semaphore_wait(barrier, N_DEV - 1)
        me, pairs = _exchange_copies(x_refs, land_refs, send_sems, recv_sems, scatter)
        own = [pltpu.make_async_copy(x_refs[a].at[me] if scatter else x_refs[a], land_refs[a].at[me], own_sems.at[a])
               for a in range(n)]
        for cp in own:
            cp.start()
        for out, _ in pairs:
            out.start()
        for out, arrival in pairs:
            arrival.wait_recv()
            out.wait_send()
        for cp in own:
            cp.wait()

    launch()
    return lambda: [r[...] for r in land_refs]


def _exchange_start(name, xs, scatter):
    n = len(xs)
    shapes = [tuple(x.shape[1:] if scatter else x.shape) for x in xs]

    def body(*refs):
        x_refs, land_refs = refs[:n], refs[n:2 * n]
        send_sems, recv_sems = refs[2 * n], refs[2 * n + 1]
        token, own_sems = refs[4 * n + 2], refs[4 * n + 3]
        me, pairs = _exchange_copies(x_refs, land_refs, send_sems, recv_sems, scatter)
        own = [pltpu.make_async_copy(x_refs[a].at[me] if scatter else x_refs[a], land_refs[a].at[me], own_sems.at[a])
               for a in range(n)]
        for cp in own:
            cp.start()
        for out, _ in pairs:
            out.start()
        for cp in own:
            cp.wait()
        token[...] = jnp.zeros(token.shape, token.dtype)

    lands = [pltpu.with_memory_space_constraint(lax.empty((N_DEV,) + sh, x.dtype), pltpu.HBM) for sh, x in zip(shapes, xs)]
    res = pl.pallas_call(
        body, name=name,
        out_shape=(pltpu.SemaphoreType.DMA((n * (N_DEV - 1),)), pltpu.SemaphoreType.DMA((n * (N_DEV - 1),)),
                   *[pltpu.HBM(x.shape, x.dtype) for x in xs], *[pltpu.HBM(l.shape, l.dtype) for l in lands],
                   jax.ShapeDtypeStruct((8, 128), F32)),
        in_specs=[_HBM] * (2 * n),
        out_specs=(_SEM, _SEM, *[_HBM] * (2 * n), pl.BlockSpec(memory_space=pltpu.VMEM)),
        input_output_aliases={i: 2 + i for i in range(2 * n)},
        scratch_shapes=[pltpu.SemaphoreType.DMA((n,))],
        compiler_params=pltpu.CompilerParams(has_side_effects=pltpu.SideEffectType.DATAFLOW_SIDE_EFFECTING),
    )(*[pltpu.with_memory_space_constraint(x, pltpu.HBM) for x in xs], *lands)
    return dict(send=res[0], recv=res[1], xs=list(res[2:2 + n]), lands=list(res[2 + n:2 + 2 * n]), token=res[2 + 2 * n])


def _exchange_wait(name, handle, after, scatter):
    n = len(handle['xs'])

    def body(*refs):
        x_refs, land_refs = refs[:n], refs[n:2 * n]
        send_sems, recv_sems = refs[2 * n], refs[2 * n + 1]
        _, pairs = _exchange_copies(x_refs, land_refs, send_sems, recv_sems, scatter)
        for out, arrival in pairs:
            out.wait_send()
            arrival.wait_recv()

    res = pl.pallas_call(
        body, name=name,
        out_shape=tuple(pltpu.HBM(a.shape, a.dtype) for a in handle['xs'] + handle['lands']),
        in_specs=[_HBM] * (2 * n) + [_SEM, _SEM, pl.BlockSpec(memory_space=pl.ANY)],
        out_specs=tuple([_HBM] * (2 * n)),
        input_output_aliases={i: i for i in range(2 * n)},
        compiler_params=pltpu.CompilerParams(has_side_effects=pltpu.SideEffectType.DATAFLOW_SIDE_EFFECTING),
    )(*handle['xs'], *handle['lands'], handle['send'], handle['recv'], after)
    return list(res[n:])


def _relu2(a):
    r = jnp.maximum(a, 0.0)
    return r * r


def _relu2_grad(acc, a):
    return acc * (2.0 * jnp.maximum(a, 0.0))


def _local_step(x0, tgt, mod, wcat, late_weights, send_grads, conv_w, conv_b, dt_bias, a_log, d_skip, ssm_norm_w, f_bias,
                attn_norm_w, ln1_g, ln1_b, ln2_g, ln2_b):
    ff_w = DFF // N_DEV
    s = x0.shape[0]
    tm = min(1024, s)
    ts = min(1024, s)
    sh1, sc1, g1, sh2, sc2, g2 = [mod[:, i * D:(i + 1) * D] for i in range(6)]
    zero = jnp.zeros((1, 128 - 2 * NH), F32)
    bias128 = jnp.concatenate([dt_bias, f_bias, zero], axis=1)
    alog128 = jnp.concatenate([a_log, jnp.zeros((1, 128 - NH), F32)], axis=1)
    dskip_x = jnp.repeat(d_skip, HD, axis=1)
    w_xs, w_bc, b_xs, b_bc = conv_w[:, :D], conv_w[:, D:], conv_b[:, :D], conv_b[:, D:]

    p = _mm_nn("in_proj", x0, wcat, tm=tm, tn=1152, tk=D, out_dtype=F32, pro=_modulate, aux=(sc1, sh1))
    xs_a, bc_a = _conv_fwd(p, w_xs, b_xs, w_bc, b_bc, s)
    y_ssd, states = _ssd_fwd(xs_a, bc_a, p, bias128, alog128, dskip_x, s)
    cum = _cum_fwd(p, bias128, s)
    att, lse = _attn_fwd(p, cum, s)
    wout, w1s, w2 = late_weights(lse)
    ymix = _mix_norm(y_ssd, p, att, ssm_norm_w, attn_norm_w, s)
    y = _mm_nn("out_proj", ymix, wout, tm=tm, tn=1024, tk=2 * D, out_dtype=F32)
    x1, h2 = _ln1(x0, y, g1, ln1_g, ln1_b, sc2, sh2, s)
    a1 = _mm_nn("ff_in", h2, w1s, tm=tm, tn=ff_w, tk=D, out_dtype=F32)
    ff = _mm_nn("ff_out", a1, w2, tm=tm, tn=1024, tk=1024, out_dtype=F32, pro=_relu2)
    du2, dff, sq_err, d_ln2_g, d_ln2_b, d_g2 = _ln2_loss(x1, ff, tgt, g2, ln2_g, ln2_b, s)

    da1 = _mm_nt("d_ff_hidden", [(dff, D, 0)], [(w2, D, 0)], n=DFF, tm=tm, tn=1024, out_dtype=BF16, epi=_relu2_grad,
                 epi_aux=(a1,))
    d_w2 = _mm_tn("d_w_ff_out", a1, dff, tm=1024, tn=1024, ts=ts, pro=_relu2)
    d_w1s = _mm_tn("d_w_ff_in", h2, da1, tm=1024, tn=ff_w, ts=ts, col_shards=True)
    dh2 = _mm_nt("d_ff_input", [(da1, ff_w, k) for k in range(N_DEV)], [(w1s, ff_w, k) for k in range(N_DEV)], n=D,
                 tm=min(512, s), tn=1024, out_dtype=F32)
    sent = send_grads("ff", [d_w1s, d_w2.reshape(N_DEV, -1, D)])
    du1, dy, d_sc2, d_sh2, d_ln1_g, d_ln1_b, d_g1 = _ln1_bwd(dh2, du2, x0, y, g1, ln1_g, ln1_b, _after(sc2, sent), s)

    dmix = _mm_nt("d_mix", [(dy, D, 0)], [(wout, D, 0)], n=2 * D, tm=tm, tn=1024, out_dtype=F32)
    d_wout = _mm_tn("d_w_out", ymix, dy, tm=1024, tn=1024, ts=ts)
    sent = send_grads("out", [d_wout.reshape(N_DEV, -1, D)])
    dy_ssd, dz, datt, d_ssm_w, d_attn_w = _mix_norm_bwd(dmix, y_ssd, p, att, _after(ssm_norm_w, sent), attn_norm_w, s)
    dq, dk, dv, dcs, drs = _attn_bwd(p, cum, att, lse, datt, s)
    dxs_a, dbc_a, ddt_raw, d_alog, d_dskip = _ssd_bwd(dy_ssd, xs_a, bc_a, p, states, bias128, alog128, dskip_x, s)
    dcum = jnp.pad(drs[:, :2, :].reshape(NH, s).T - dcs[:, ::HD], ((0, 0), (NH, 128 - 2 * NH)))
    ddtf, _, d_bias = _cum_bwd(dcum, ddt_raw, p, bias128, s)
    dxs, dbc, d_wc_xs, d_bc_xs, d_wc_bc, d_bc_bc = _conv_bwd(dxs_a, dbc_a, p, w_xs, b_xs, w_bc, b_bc, s)

    segs = [(dz, OFF_Z, D), (dxs, OFF_XS, D), (dq, OFF_Q, D), (dk, OFF_K, D), (dv, OFF_V, D), (dbc, OFF_BC, 512),
            (ddtf, OFF_DTF, 128)]
    d_z, d_xs, d_q, d_k, d_v, d_bcw, d_dtf = [
        _mm_tn("d_w_in_%d" % i, x0, a, tm=1024, tn=min(w, 1024), ts=ts, pro=_modulate, aux=(sc1, sh1))
        for i, (a, _, w) in enumerate(segs)]
    d_w_in = dict(z=d_z, xs=d_xs, bc=d_bcw, dt=d_dtf[:, :NH], q=d_q, k=d_k, v=d_v, f=d_dtf[:, NH:2 * NH])
    sent = send_grads("in", [_shard_w_in_grad(d_w_in)])
    segs[-1] = (_after(ddtf, sent), OFF_DTF, 128)
    dh1 = _mm_nt("d_h1", [(a, w, 0) for a, _, w in segs], [(wcat, w, off // w) for _, off, w in segs], n=D,
                 tm=min(512, s), tn=1024, out_dtype=F32)
    grad_x, d_sc1, d_sh1 = _input_grad(dh1, du1, x0, sc1, s)

    return dict(
        loss=(0.5 / D) * jnp.sum(sq_err), grad_x=grad_x,
        d_mod=jnp.concatenate([d_sh1, d_sc1, d_g1, d_sh2, d_sc2, d_g2], axis=1),
        d_conv_w=jnp.concatenate([d_wc_xs[:4], d_wc_bc[:4]], axis=1), d_conv_b=jnp.concatenate([d_bc_xs, d_bc_bc], axis=1),
        d_ssm_norm_w=d_ssm_w, d_attn_norm_w=d_attn_w, d_ln1_g=d_ln1_g, d_ln1_b=d_ln1_b, d_ln2_g=d_ln2_g, d_ln2_b=d_ln2_b,
        d_gate_bias=d_bias, d_a_log=d_alog, d_d_skip=d_dskip)


W_IN_SEGS = [('z', W_Z, D), ('xs', W_XS, D), ('bc', W_BC, 512), ('dt', W_DT, NH), ('q', W_Q, D), ('k', W_K, D),
             ('v', W_V, D), ('f', W_F, NH)]
SHARD_W = IN_COLS // N_DEV


def _pack_w_in(shards):
    def cols(lo, hi):
        pieces = []
        while lo < hi:
            dev = lo // SHARD_W
            end = min(hi, (dev + 1) * SHARD_W)
            pieces.append(shards[dev][:, lo - dev * SHARD_W:end - dev * SHARD_W])
            lo = end
        return pieces

    seg = {n: cols(off, off + w) for n, off, w in W_IN_SEGS}
    pieces = seg['z'] + seg['xs'] + seg['q'] + seg['k'] + seg['v'] + seg['bc'] + seg['dt'] + seg['f']
    return jnp.concatenate(pieces + [jnp.zeros((D, 128 - 2 * NH), shards.dtype)], axis=1)


def _shard_w_in_grad(d_w_in):
    blocks = []
    for dev in range(N_DEV):
        lo, hi = dev * SHARD_W, (dev + 1) * SHARD_W
        pieces = [d_w_in[n][:, max(lo, off) - off:min(hi, off + w) - off] for n, off, w in W_IN_SEGS
                  if max(lo, off) < min(hi, off + w)]
        blocks.append(jnp.concatenate(pieces, axis=1))
    return jnp.stack(blocks, axis=0)


WEIGHTS = ['w_ada', 'b_ada', 'w_in', 'conv_w', 'conv_b', 'dt_bias', 'a_log', 'd_skip', 'ssm_norm_w', 'f_bias',
           'attn_norm_w', 'w_out', 'ln1_g', 'ln1_b', 'w_ff_in', 'w_ff_out', 'ln2_g', 'ln2_b']
BIG = ['w_in', 'w_out', 'w_ff_in', 'w_ff_out']
SMALL = ['b_ada', 'conv_b', 'ssm_norm_w', 'attn_norm_w', 'ln1_g', 'ln1_b', 'ln2_g', 'ln2_b', 'dt_bias', 'a_log', 'd_skip',
         'f_bias', 'conv_w']


def _pad_lanes(v, n=128):
    return jnp.pad(v, ((0, 0), (0, n - v.shape[1])))


def _small_block(vals):
    rows = [_pad_lanes(vals[n].reshape(1, -1), -(-vals[n].size // 128) * 128).reshape(-1, 128) for n in SMALL]
    block = jnp.concatenate(rows, axis=0)
    return jnp.pad(block, ((0, 120 - block.shape[0]), (0, 0)))


def _small_unblock(block, like):
    out, r = {}, 0
    for n in SMALL:
        size = like[n].size
        nr = -(-size // 128)
        out[n] = block[r:r + nr].reshape(-1)[:size].reshape(like[n].shape)
        r += nr
    return out


def kernel(x, c, w_ada, b_ada, w_in, conv_w, conv_b, dt_bias, a_log, d_skip, ssm_norm_w, f_bias, attn_norm_w, w_out, ln1_g, ln1_b, w_ff_in, w_ff_out, ln2_g, ln2_b, loss_target, m_w_ada, m_b_ada, m_w_in, m_conv_w, m_conv_b, m_dt_bias, m_a_log, m_d_skip, m_ssm_norm_w, m_f_bias, m_attn_norm_w, m_w_out, m_ln1_g, m_ln1_b, m_w_ff_in, m_w_ff_out, m_ln2_g, m_ln2_b, v_w_ada, v_b_ada, v_w_in, v_conv_w, v_conv_b, v_dt_bias, v_a_log, v_d_skip, v_ssm_norm_w, v_f_bias, v_attn_norm_w, v_w_out, v_ln1_g, v_ln1_b, v_w_ff_in, v_w_ff_out, v_ln2_g, v_ln2_b):
    args = dict(locals())
    w = {n: args[n] for n in WEIGHTS}
    m = {n: args['m_' + n] for n in WEIGHTS}
    v = {n: args['v_' + n] for n in WEIGHTS}
    me = 4 * lax.axis_index("x") + 2 * lax.axis_index("y") + lax.axis_index("c")
    ada_cols = 6 * D // N_DEV
    conv_cols = conv_w.shape[2]

    c_all, conv_all = _exchange("gather_cond", [c, conv_w[0]], False)
    c_all = c_all.reshape(N_DEV, D)
    conv_w_full = conv_all.transpose(1, 0, 2).reshape(4, N_DEV * conv_cols)
    b_shard = lax.dynamic_slice(b_ada, (0, me * ada_cols), (1, ada_cols))
    mod_all, = _exchange("gather_mod", [_ada_mod(c_all, w_ada[0], b_shard)], False)
    mod = lax.dynamic_index_in_dim(mod_all, me, axis=1, keepdims=False).reshape(1, 6 * D)

    win_s, = _exchange("gather_w_in", [_after(w_in[0].astype(BF16), mod * 0)], False)
    first_done = win_s[0, 0:1, 0:1] * 0
    rest = _exchange_async("gather_rest", [_after(w[n][0].astype(BF16), first_done) for n in BIG[1:]], False, 1)

    def late_weights(after):
        wout_s, w1s, w2_s = rest()
        return wout_s.reshape(2 * D, D), w1s, w2_s.reshape(DFF, D)

    sends = {}

    def send_grads(tag, blocks):
        sends[tag] = _exchange_async("scatter_" + tag, blocks, True, {'ff': 2, 'out': 3, 'in': 4}[tag])
        return sum(b.reshape(-1)[0].astype(F32) * 0 for b in blocks)

    out = _local_step(x[0], loss_target[0], mod, _pack_w_in(win_s), late_weights, send_grads,
                      conv_w_full, conv_b, dt_bias, a_log, d_skip, ssm_norm_w, f_bias, attn_norm_w, ln1_g, ln1_b, ln2_g, ln2_b)
    g_ff_in, g_ff_out = sends['ff']()
    g_out, = sends['out']()
    g_in, = sends['in']()
    g_parts = [g_in, g_out, g_ff_in, g_ff_out]
    big = {n: _adamw("adamw_" + n, w[n][0], g, m[n][0], v[n][0], tr=256, slots=True) for n, g in zip(BIG, g_parts)}

    small = jnp.concatenate(
        [out['d_mod'], out['d_conv_w'].reshape(1, -1), out['d_conv_b'], out['d_ssm_norm_w'], out['d_attn_norm_w'],
         out['d_ln1_g'], out['d_ln1_b'], out['d_ln2_g'], out['d_ln2_b'], out['d_gate_bias'], out['d_a_log'],
         out['d_d_skip'], jnp.zeros((1, 128), F32)], axis=1).reshape(-1, 128)
    small_all, = _exchange("gather_small", [small], False)
    ssum = _sum_slots("sum_small", small_all)
    dmod_all = small_all[:, :6 * D // 128].reshape(N_DEV, 6 * D)
    g_w_ada, g_b_ada = _ada_grad(c_all, lax.dynamic_slice(dmod_all, (0, me * ada_cols), (N_DEV, ada_cols)), dmod_all)
    rows = lambda a, b: ssum[a:b].reshape(1, -1)
    g_conv_w = lax.dynamic_slice(ssum[48:96].reshape(4, N_DEV * conv_cols), (0, me * conv_cols), (4, conv_cols))
    g_small = dict(b_ada=g_b_ada, conv_w=g_conv_w[None], conv_b=rows(96, 108), ssm_norm_w=rows(108, 116),
                   attn_norm_w=rows(116, 124), ln1_g=rows(124, 132), ln1_b=rows(132, 140), ln2_g=rows(140, 148),
                   ln2_b=rows(148, 156), dt_bias=ssum[156:157, :NH], f_bias=ssum[156:157, NH:2 * NH],
                   a_log=ssum[157:158, :NH], d_skip=ssum[158:159, :NH])
    sm = _adamw("adamw_small", _small_block(w), _small_block(g_small), _small_block(m), _small_block(v), tr=120, slots=False)
    ada = _adamw("adamw_ada", w_ada[0], g_w_ada, m_w_ada[0], v_w_ada[0], tr=256, slots=False)

    results = []
    for k in range(4):
        vals = _small_unblock(sm[k], w)
        vals['w_ada'] = ada[k][None]
        for n in BIG:
            vals[n] = big[n][k][None]
        results.append(vals)
    loss = lax.psum(out['loss'], ("x", "y", "c"))
    return (loss, out['grad_x'][None], *[res[n] for res in results for n in WEIGHTS])
```

```python
import functools

import jax
import jax.numpy as jnp
from jax import lax
from jax.experimental import pallas as pl
from jax.experimental.pallas import tpu as pltpu
from jax.experimental.pallas import tpu_sc as plsc

F32, BF16 = jnp.float32, jnp.bfloat16

N_DEV = 8
D = 1024
NH, HD = 16, 64
NSTATE = 128
CHUNK = 128
HG = 8
DFF = 4096
ALPHA = 2.0 ** 0.25
EPS = 1e-5
ATT_SCALE = HD ** -0.5

OFF_Z, OFF_XS, OFF_Q, OFF_K, OFF_V, OFF_BC, OFF_DTF = 0, 1024, 2048, 3072, 4096, 5120, 5632
PCOLS = 5760
W_Z, W_XS, W_BC, W_DT, W_Q, W_K, W_V, W_F = 0, 1024, 2048, 2560, 2576, 3600, 4624, 5648
IN_COLS = 5664

ADAM_LR, ADAM_B1, ADAM_B2, ADAM_EPS, ADAM_WD, ADAM_STEP = 0.001, 0.9, 0.999, 1e-08, 0.01, 10

VMEM_LIMIT = 56 << 20

NN = (((1,), (0,)), ((), ()))
NT = (((1,), (1,)), ((), ()))
TN = (((0,), (0,)), ((), ()))


def _dot(a, b, dims=NN):
    return lax.dot_general(a, b, dims, preferred_element_type=F32)


def _bdot(a, b, dims=NN):
    return _dot(a.astype(BF16), b.astype(BF16), dims)


def _split3(v):
    parts, rest = [], v
    for _ in range(3):
        p = rest.astype(BF16)
        parts.append(p)
        rest = rest - p.astype(F32)
    return parts


def _sel_left(m01, v):
    return sum(_dot(m01, p) for p in _split3(v))


def _sel_right(v, m01, dims=NN):
    return sum(_dot(p, m01, dims) for p in _split3(v))


def _iota(shape, dim):
    return lax.broadcasted_iota(jnp.int32, shape, dim)


def _tri_lower(n):
    return (_iota((n, n), 1) <= _iota((n, n), 0)).astype(BF16)


def _tri_upper(n):
    return (_iota((n, n), 1) >= _iota((n, n), 0)).astype(BF16)


def _head_expand():
    return (lax.shift_right_logical(_iota((128, D), 1), 6) == _iota((128, D), 0)).astype(BF16)


def _head_reduce():
    return (lax.shift_right_logical(_iota((D, 128), 0), 6) == _iota((D, 128), 1)).astype(BF16)


def _sigmoid(x):
    return 1.0 / (1.0 + jnp.exp(-x))


def _silu(x):
    return x * _sigmoid(x)


def _dsilu(x):
    s = _sigmoid(x)
    return s * (1.0 + x * (1.0 - s))


def _softplus(x):
    return jnp.maximum(x, 0.0) + jnp.log(1.0 + jnp.exp(-jnp.abs(x)))


def _log_sigmoid(x):
    return jnp.minimum(x, 0.0) - jnp.log(1.0 + jnp.exp(-jnp.abs(x)))


def _params(sem):
    return pltpu.CompilerParams(dimension_semantics=sem, vmem_limit_bytes=VMEM_LIMIT)


def _mm_nn(name, a, b, *, tm, tn, tk, out_dtype, pro=None, aux=()):
    m, k_all = a.shape
    b_sharded = b.ndim == 3
    n = b.shape[0] * b.shape[2] if b_sharded else b.shape[1]
    assert not b_sharded or tn == b.shape[2]
    nk = k_all // tk
    n_aux = len(aux)
    b_spec = (pl.BlockSpec((None, tk, tn), lambda i, j, k: (j, k, 0)) if b_sharded
              else pl.BlockSpec((tk, tn), lambda i, j, k: (k, j)))

    def body(a_ref, b_ref, *rest):
        aux_refs, o_ref = rest[:n_aux], rest[n_aux]
        at = a_ref[...]
        if pro is not None:
            at = pro(at, *[r[...] for r in aux_refs])
        part = _bdot(at, b_ref[...])
        if nk == 1:
            o_ref[...] = part.astype(out_dtype)
            return
        acc_ref = rest[n_aux + 1]
        kk = pl.program_id(2)

        @pl.when(kk == 0)
        def _():
            acc_ref[...] = part

        @pl.when(kk > 0)
        def _():
            acc_ref[...] += part

        @pl.when(kk == nk - 1)
        def _():
            o_ref[...] = acc_ref[...].astype(out_dtype)

    return pl.pallas_call(
        body, name=name,
        grid=(m // tm, n // tn, nk),
        in_specs=[pl.BlockSpec((tm, tk), lambda i, j, k: (i, k)), b_spec]
        + [pl.BlockSpec((1, tk), lambda i, j, k: (0, k)) for _ in aux],
        out_specs=pl.BlockSpec((tm, tn), lambda i, j, k: (i, j)),
        out_shape=jax.ShapeDtypeStruct((m, n), out_dtype),
        scratch_shapes=[] if nk == 1 else [pltpu.VMEM((tm, tn), F32)],
        compiler_params=_params(("parallel", "parallel", "arbitrary")),
    )(a, b, *aux)


def _mm_nt(name, a_list, b_list, *, n, tm, tn, out_dtype, epi=None, epi_aux=()):
    m = a_list[0][0].shape[0]
    n_op = len(a_list)
    n_epi = len(epi_aux)

    def body(*refs):
        a_refs, b_refs = refs[:n_op], refs[n_op:2 * n_op]
        e_refs, o_ref = refs[2 * n_op:2 * n_op + n_epi], refs[2 * n_op + n_epi]
        acc = None
        for a_ref, b_ref in zip(a_refs, b_refs):
            part = _bdot(a_ref[...], b_ref[...], NT)
            acc = part if acc is None else acc + part
        if epi is not None:
            acc = epi(acc, *[r[...] for r in e_refs])
        o_ref[...] = acc.astype(out_dtype)

    in_specs = [pl.BlockSpec((tm, w), functools.partial(lambda i, j, cb: (i, cb), cb=cb)) for (_, w, cb) in a_list]
    for (b, w, cb) in b_list:
        if b.ndim == 3:
            in_specs.append(pl.BlockSpec((None, tn, w), functools.partial(lambda i, j, cb: (cb, j, 0), cb=cb)))
        else:
            in_specs.append(pl.BlockSpec((tn, w), functools.partial(lambda i, j, cb: (j, cb), cb=cb)))
    in_specs += [pl.BlockSpec((tm, tn), lambda i, j: (i, j)) for _ in epi_aux]
    return pl.pallas_call(
        body, name=name,
        grid=(m // tm, n // tn),
        in_specs=in_specs,
        out_specs=pl.BlockSpec((tm, tn), lambda i, j: (i, j)),
        out_shape=jax.ShapeDtypeStruct((m, n), out_dtype),
        compiler_params=_params(("parallel", "parallel")),
    )(*[a for (a, _, _) in a_list], *[b for (b, _, _) in b_list], *epi_aux)


def _mm_tn(name, a, b, *, tm, tn, ts, pro=None, aux=(), col_shards=False):
    s_all, ka = a.shape
    nb = b.shape[1]
    n_aux = len(aux)
    ns = s_all // ts
    assert not col_shards or tn == nb // N_DEV

    def body(a_ref, b_ref, *rest):
        aux_refs, o_ref, acc_ref = rest[:n_aux], rest[n_aux], rest[n_aux + 1]
        at = a_ref[...]
        if pro is not None:
            at = pro(at, *[r[...] for r in aux_refs])
        part = _bdot(at, b_ref[...], TN)
        ss = pl.program_id(2)

        @pl.when(ss == 0)
        def _():
            acc_ref[...] = part

        @pl.when(ss > 0)
        def _():
            acc_ref[...] += part

        @pl.when(ss == ns - 1)
        def _():
            o_ref[...] = acc_ref[...].astype(BF16)

    if col_shards:
        out_spec = pl.BlockSpec((None, tm, tn), lambda i, j, s: (j, i, 0))
        out_shape = jax.ShapeDtypeStruct((N_DEV, ka, tn), BF16)
    else:
        out_spec = pl.BlockSpec((tm, tn), lambda i, j, s: (i, j))
        out_shape = jax.ShapeDtypeStruct((ka, nb), BF16)
    return pl.pallas_call(
        body, name=name,
        grid=(ka // tm, nb // tn, ns),
        in_specs=[pl.BlockSpec((ts, tm), lambda i, j, s: (s, i)),
                  pl.BlockSpec((ts, tn), lambda i, j, s: (s, j))]
        + [pl.BlockSpec((1, tm), lambda i, j, s: (0, i)) for _ in aux],
        out_specs=out_spec, out_shape=out_shape,
        scratch_shapes=[pltpu.VMEM((tm, tn), F32)],
        compiler_params=_params(("parallel", "parallel", "arbitrary")),
    )(a, b, *aux)


def _rowk(name, fn, n_rows, tr, rows, fulls, outs, accs, reverse=False):
    n = n_rows // tr
    n_row, n_full, n_out, n_acc = len(rows), len(fulls), len(outs), len(accs)

    def pos(i):
        return (n - 1 - i) if reverse else i

    def body(*refs):
        row_refs = refs[:n_row]
        full_refs = refs[n_row:n_row + n_full]
        out_refs = refs[n_row + n_full:n_row + n_full + n_out]
        acc_refs = refs[n_row + n_full + n_out:]
        i = pl.program_id(0)

        @pl.when(i == 0)
        def _():
            for r in acc_refs:
                r[...] = jnp.zeros(r.shape, r.dtype)

        res = fn(pos(i), *[r[...] for r in row_refs], *[r[...] for r in full_refs], *[r[...] for r in acc_refs])
        for r, v in zip(out_refs + acc_refs, res):
            r[...] = v.astype(r.dtype)

    def row_map(i, cb, shift):
        return (jnp.clip(pos(i) + shift, 0, n - 1), cb)

    in_specs = [pl.BlockSpec((tr, w), functools.partial(row_map, cb=cb, shift=sh)) for (_, w, cb, sh) in rows]
    in_specs += [pl.BlockSpec(f.shape, functools.partial(lambda i, nd: (0,) * nd, nd=f.ndim)) for f in fulls]
    out_specs = [pl.BlockSpec((tr, w), lambda i: (pos(i), 0)) for (w, _) in outs]
    out_specs += [pl.BlockSpec((r, w), lambda i: (0, 0)) for (r, w) in accs]
    out_shape = [jax.ShapeDtypeStruct((n_rows, w), dt) for (w, dt) in outs]
    out_shape += [jax.ShapeDtypeStruct((r, w), F32) for (r, w) in accs]
    return pl.pallas_call(
        body, name=name, grid=(n,), in_specs=in_specs, out_specs=out_specs, out_shape=out_shape,
        compiler_params=_params(("arbitrary",)),
    )(*[a for (a, _, _, _) in rows], *fulls)


def _colsum(x):
    return jnp.sum(x, axis=0, keepdims=True)


def _mean(x):
    return jnp.mean(x, axis=-1, keepdims=True)


def _modulate(x, sc, sh):
    return x * (1.0 + sc) + sh


def _shift_down(cur, prev, j):
    row = _iota(cur.shape, 0)
    return jnp.where(row < j, pltpu.roll(prev, j, 0), pltpu.roll(cur, j, 0))


def _shift_up(cur, nxt, j):
    tr = cur.shape[0]
    row = _iota(cur.shape, 0)
    return jnp.where(row < tr - j, pltpu.roll(cur, tr - j, 0), pltpu.roll(nxt, tr - j, 0))


def _conv(cur, prev, w, b):
    out = cur * w[3:4] + b
    for j in (1, 2, 3):
        out = out + _shift_down(cur, prev, j) * w[3 - j:4 - j]
    return out


def _conv_fwd(p, w_xs, b_xs, w_bc, b_bc, s):
    def fn(pos, xs, xs_prev, bc, bc_prev, w_xs, b_xs, w_bc, b_bc):
        first = pos == 0
        xs_prev = jnp.where(first, 0.0, xs_prev)
        bc_prev = jnp.where(first, 0.0, bc_prev)
        return _silu(_conv(xs, xs_prev, w_xs, b_xs)), _silu(_conv(bc, bc_prev, w_bc, b_bc))

    return _rowk("conv_fwd", fn, s, 256,
                 [(p, D, OFF_XS // D, 0), (p, D, OFF_XS // D, -1), (p, 512, OFF_BC // 512, 0), (p, 512, OFF_BC // 512, -1)],
                 [w_xs, b_xs, w_bc, b_bc], [(D, F32), (512, F32)], [])


def _conv_bwd(dxs_a, dbc_a, p, w_xs, b_xs, w_bc, b_bc, s):
    tr = 256
    n = s // tr

    def fn(pos, da1, da1n, x1, x1p, x1n, da2, da2n, x2, x2p, x2n, w1, b1, w2, b2, aw1, ab1, aw2, ab2):
        dx1, dw1, db1 = _conv_bwd_fn(pos, n, da1, da1n, x1, x1p, x1n, w1, b1)
        dx2, dw2, db2 = _conv_bwd_fn(pos, n, da2, da2n, x2, x2p, x2n, w2, b2)
        return dx1, dx2, aw1 + dw1, ab1 + db1, aw2 + dw2, ab2 + db2

    cx, cb = OFF_XS // D, OFF_BC // 512
    return _rowk("conv_bwd", fn, s, tr,
                 [(dxs_a, D, 0, 0), (dxs_a, D, 0, 1), (p, D, cx, 0), (p, D, cx, -1), (p, D, cx, 1),
                  (dbc_a, 512, 0, 0), (dbc_a, 512, 0, 1), (p, 512, cb, 0), (p, 512, cb, -1), (p, 512, cb, 1)],
                 [w_xs, b_xs, w_bc, b_bc], [(D, BF16), (512, BF16)], [(8, D), (1, D), (8, 512), (1, 512)])


def _conv_bwd_fn(pos, n, da, da_next, x, x_prev, x_next, w, b):
    first, last = pos == 0, pos == n - 1
    x_prev = jnp.where(first, 0.0, x_prev)
    dc = da * _dsilu(_conv(x, x_prev, w, b))
    dc_next = jnp.where(last, 0.0, da_next * _dsilu(_conv(x_next, x, w, b)))
    dx = dc * w[3:4]
    dws = [None] * 4
    dws[3] = _colsum(dc * x)
    for j in (1, 2, 3):
        dx = dx + _shift_up(dc, dc_next, j) * w[3 - j:4 - j]
        dws[3 - j] = _colsum(dc * _shift_down(x, x_prev, j))
    row = _iota((8, x.shape[1]), 0)
    dw = jnp.zeros((8, x.shape[1]), F32)
    for k in range(4):
        dw = jnp.where(row == k, dws[k], dw)
    return dx, dw, _colsum(dc)


def _ssd_gates(dtf, bias, a_log):
    lane = _iota(dtf.shape, 1)
    head = lane < NH
    dt = jnp.where(head, _softplus(dtf + bias), 0.0)
    a_neg = jnp.where(_iota(a_log.shape, 1) < NH, -jnp.exp(a_log), 0.0)
    a = dt * a_neg
    cs = _sel_left(_tri_lower(CHUNK), a)
    return dt, a_neg, cs


def _decay_mask(cs_ref, cst_ref, h):
    diff = cs_ref[:, h:h + 1] - cst_ref[h:h + 1, :]
    low = _iota((CHUNK, CHUNK), 1) <= _iota((CHUNK, CHUNK), 0)
    return jnp.where(low, jnp.exp(jnp.minimum(diff, 0.0)), 0.0)


def _ssd_fwd(xs_a, bc_a, p, bias128, alog128, dskip_x, s):
    nc = s // CHUNK
    t = CHUNK

    def body(xs_ref, bc_ref, dtf_ref, bias_ref, alog_ref, dsk_ref, y_ref, st_ref,
             state, x_sc, xw_sc, cs_sc, cst_sc, yd_sc):
        c = pl.program_id(0)

        @pl.when(c == 0)
        def _():
            state[...] = jnp.zeros(state.shape, F32)

        dt, _, cs = _ssd_gates(dtf_ref[...], bias_ref[...], alog_ref[...])
        cs_sc[...] = cs
        cst_sc[...] = cs.T
        cs_last = cs[t - 1:t, :]
        expand = _head_expand()
        ex = _sel_right(jnp.concatenate([dt, jnp.exp(cs), jnp.exp(cs_last - cs)], axis=0), expand)
        dt_x, eo_x, we_x = ex[0:t], ex[t:2 * t], ex[2 * t:3 * t]
        g_x = _sel_right(jnp.broadcast_to(jnp.exp(cs_last), (8, 128)), expand)[0:1]
        xs = xs_ref[...]
        x = xs * dt_x
        x_sc[...] = x.astype(BF16)
        xw_sc[...] = (x * we_x).astype(BF16)
        prev = state[...]
        st_ref[0] = prev
        prev_b = prev.astype(BF16)
        for g in range(2):
            cols = slice(g * 512, (g + 1) * 512)
            b_g = bc_ref[:, g * 128:(g + 1) * 128].astype(BF16)
            c_g = bc_ref[:, 256 + g * 128:256 + (g + 1) * 128].astype(BF16)
            gmat = _dot(c_g, b_g, NT)
            y_off = _dot(c_g, prev_b[:, cols]) * eo_x[:, cols]
            s_loc = _dot(b_g, xw_sc[:, cols], TN)
            state[:, cols] = g_x[:, cols] * prev[:, cols] + s_loc
            for e in range(HG):
                h = g * HG + e
                m = gmat * _decay_mask(cs_sc, cst_sc, h)
                yd_sc[:, h * HD:(h + 1) * HD] = _dot(m.astype(BF16), x_sc[:, h * HD:(h + 1) * HD])
            y_ref[:, cols] = yd_sc[:, cols] + y_off + dsk_ref[:, cols] * xs[:, cols]

    return pl.pallas_call(
        body, name="ssd_fwd", grid=(nc,),
        in_specs=[pl.BlockSpec((t, D), lambda c: (c, 0)),
                  pl.BlockSpec((t, 512), lambda c: (c, 0)),
                  pl.BlockSpec((t, 128), lambda c: (c, OFF_DTF // 128)),
                  pl.BlockSpec((1, 128), lambda c: (0, 0)),
                  pl.BlockSpec((1, 128), lambda c: (0, 0)),
                  pl.BlockSpec((1, D), lambda c: (0, 0))],
        out_specs=[pl.BlockSpec((t, D), lambda c: (c, 0)),
                   pl.BlockSpec((1, NSTATE, D), lambda c: (c, 0, 0))],
        out_shape=[jax.ShapeDtypeStruct((s, D), F32), jax.ShapeDtypeStruct((nc, NSTATE, D), F32)],
        scratch_shapes=[pltpu.VMEM((NSTATE, D), F32), pltpu.VMEM((t, D), BF16), pltpu.VMEM((t, D), BF16),
                        pltpu.VMEM((t, 128), F32), pltpu.VMEM((128, t), F32), pltpu.VMEM((t, D), F32)],
        compiler_params=_params(("arbitrary",)),
    )(xs_a, bc_a, p, bias128, alog128, dskip_x)


def _ssd_bwd(dy, xs_a, bc_a, p, states, bias128, alog128, dskip_x, s):
    nc = s // CHUNK
    t = CHUNK

    def body(dy_ref, xs_ref, bc_ref, dtf_ref, st_ref, bias_ref, alog_ref, dsk_ref,
             dxs_ref, dbc_ref, ddt_ref, dalog_ref, dskip_ref,
             dstate, x_sc, dy_sc, dx_sc, deo_sc, dwe_sc, cs_sc, cst_sc, dcol_sc, drow_sc):
        i = pl.program_id(0)

        @pl.when(i == 0)
        def _():
            dstate[...] = jnp.zeros(dstate.shape, F32)
            dalog_ref[...] = jnp.zeros(dalog_ref.shape, F32)
            dskip_ref[...] = jnp.zeros(dskip_ref.shape, F32)

        dtf = dtf_ref[...]
        dt, a_neg, cs = _ssd_gates(dtf, bias_ref[...], alog_ref[...])
        cs_sc[...] = cs
        cst_sc[...] = cs.T
        cs_last = cs[t - 1:t, :]
        eo, we, g_end = jnp.exp(cs), jnp.exp(cs_last - cs), jnp.exp(cs_last)
        expand, reduce = _head_expand(), _head_reduce()
        ex = _sel_right(jnp.concatenate([dt, eo, we], axis=0), expand)
        dt_x, eo_x, we_x = ex[0:t], ex[t:2 * t], ex[2 * t:3 * t]
        g_x = _sel_right(jnp.broadcast_to(g_end, (8, 128)), expand)[0:1]
        xs = xs_ref[...]
        dyv = dy_ref[...]
        x = xs * dt_x
        x_sc[...] = x.astype(BF16)
        dy_sc[...] = dyv.astype(BF16)
        dyo_b = (dyv * eo_x).astype(BF16)
        xw_b = (x * we_x).astype(BF16)
        prev = st_ref[0]
        prev_b = prev.astype(BF16)
        dnext = dstate[...]
        dnext_b = dnext.astype(BF16)
        dcol_sc[...] = jnp.zeros(dcol_sc.shape, F32)
        drow_sc[...] = jnp.zeros(drow_sc.shape, F32)
        lane_row = _iota((1, 128), 1)
        sub_col = _iota((128, 1), 0)
        for g in range(2):
            cols = slice(g * 512, (g + 1) * 512)
            b_g = bc_ref[:, g * 128:(g + 1) * 128].astype(BF16)
            c_g = bc_ref[:, 256 + g * 128:256 + (g + 1) * 128].astype(BF16)
            gmat = _dot(c_g, b_g, NT)
            b_ds = _dot(b_g, dnext_b[:, cols])
            c_s = _dot(c_g, prev_b[:, cols])
            dx_sc[:, cols] = b_ds * we_x[:, cols]
            deo_sc[:, cols] = dyv[:, cols] * c_s
            dwe_sc[:, cols] = b_ds * x[:, cols]
            db = _dot(xw_b[:, cols], dnext_b[:, cols], NT)
            dc = _dot(dyo_b[:, cols], prev_b[:, cols], NT)
            dstate[:, cols] = g_x[:, cols] * dnext[:, cols] + _dot(c_g, dyo_b[:, cols], TN)
            dg = jnp.zeros((t, t), F32)
            for e in range(HG):
                h = g * HG + e
                hc = slice(h * HD, (h + 1) * HD)
                lmat = _decay_mask(cs_sc, cst_sc, h)
                m = gmat * lmat
                dx_sc[:, hc] += _dot(m.astype(BF16), dy_sc[:, hc], TN)
                dm = _dot(dy_sc[:, hc], x_sc[:, hc], NT)
                dg = dg + dm * lmat
                qm = dm * m
                dcol_sc[...] += jnp.sum(qm, axis=1, keepdims=True) * (lane_row == h).astype(F32)
                drow_sc[...] += (sub_col == h).astype(F32) * jnp.sum(qm, axis=0, keepdims=True)
            dg_b = dg.astype(BF16)
            dbc_ref[:, g * 128:(g + 1) * 128] = db + _dot(dg_b, c_g, TN)
            dbc_ref[:, 256 + g * 128:256 + (g + 1) * 128] = dc + _dot(dg_b, b_g)
        d_eo = _sel_right(deo_sc[...], reduce)
        d_we = _sel_right(dwe_sc[...], reduce)
        d_gend = _sel_right(jnp.broadcast_to(_colsum(dnext * prev), (8, D)), reduce)[0:1]
        d_cs = dcol_sc[...] - drow_sc[...].T + d_eo * eo - d_we * we
        extra = _colsum(d_we * we) + d_gend * g_end
        d_cs = d_cs + jnp.where(_iota((t, 128), 0) == t - 1, extra, 0.0)
        da = _sel_left(_tri_upper(t), d_cs)
        dx = dx_sc[...]
        ddt = _sel_right(dx * xs, reduce) + da * a_neg
        dxs_ref[...] = dx * dt_x + dsk_ref[...] * dyv
        ddt_ref[...] = jnp.where(_iota((t, 128), 1) < NH, ddt * _sigmoid(dtf + bias_ref[...]), 0.0)
        dalog_ref[...] += _colsum(da * dt) * a_neg
        dskip_ref[...] += _sel_right(jnp.broadcast_to(_colsum(dyv * xs), (8, D)), reduce)[0:1]

    rev = lambda i: nc - 1 - i
    return pl.pallas_call(
        body, name="ssd_bwd", grid=(nc,),
        in_specs=[pl.BlockSpec((t, D), lambda i: (rev(i), 0)),
                  pl.BlockSpec((t, D), lambda i: (rev(i), 0)),
                  pl.BlockSpec((t, 512), lambda i: (rev(i), 0)),
                  pl.BlockSpec((t, 128), lambda i: (rev(i), OFF_DTF // 128)),
                  pl.BlockSpec((1, NSTATE, D), lambda i: (rev(i), 0, 0)),
                  pl.BlockSpec((1, 128), lambda i: (0, 0)),
                  pl.BlockSpec((1, 128), lambda i: (0, 0)),
                  pl.BlockSpec((1, D), lambda i: (0, 0))],
        out_specs=[pl.BlockSpec((t, D), lambda i: (rev(i), 0)),
                   pl.BlockSpec((t, 512), lambda i: (rev(i), 0)),
                   pl.BlockSpec((t, 128), lambda i: (rev(i), 0)),
                   pl.BlockSpec((1, 128), lambda i: (0, 0)),
                   pl.BlockSpec((1, 128), lambda i: (0, 0))],
        out_shape=[jax.ShapeDtypeStruct((s, D), F32), jax.ShapeDtypeStruct((s, 512), F32),
                   jax.ShapeDtypeStruct((s, 128), F32), jax.ShapeDtypeStruct((1, 128), F32),
                   jax.ShapeDtypeStruct((1, 128), F32)],
        scratch_shapes=[pltpu.VMEM((NSTATE, D), F32), pltpu.VMEM((t, D), BF16), pltpu.VMEM((t, D), BF16),
                        pltpu.VMEM((t, D), F32), pltpu.VMEM((t, D), F32), pltpu.VMEM((t, D), F32),
                        pltpu.VMEM((t, 128), F32), pltpu.VMEM((128, t), F32),
                        pltpu.VMEM((t, 128), F32), pltpu.VMEM((128, t), F32)],
        compiler_params=_params(("arbitrary",)),
    )(dy, xs_a, bc_a, p, states, bias128, alog128, dskip_x)


def _gate_lanes(shape):
    lane = _iota(shape, 1)
    return (lane >= NH) & (lane < 2 * NH)


def _cum_fwd(p, bias128, s):
    tr = min(512, s)

    def body(dtf_ref, bias_ref, o_ref, carry):
        @pl.when(pl.program_id(0) == 0)
        def _():
            carry[...] = jnp.zeros(carry.shape, F32)

        lf = jnp.where(_gate_lanes((tr, 128)), _log_sigmoid(dtf_ref[...] + bias_ref[...]), 0.0)
        cum = _sel_left(_tri_lower(tr), lf) + carry[...]
        carry[...] = cum[tr - 1:tr, :]
        o_ref[...] = cum

    return pl.pallas_call(
        body, name="cum_fwd", grid=(s // tr,),
        in_specs=[pl.BlockSpec((tr, 128), lambda i: (i, OFF_DTF // 128)), pl.BlockSpec((1, 128), lambda i: (0, 0))],
        out_specs=pl.BlockSpec((tr, 128), lambda i: (i, 0)),
        out_shape=jax.ShapeDtypeStruct((s, 128), F32),
        scratch_shapes=[pltpu.VMEM((1, 128), F32)],
        compiler_params=_params(("arbitrary",)),
    )(p, bias128)


def _cum_bwd(dcum, ddt_raw, p, bias128, s):
    tr = min(512, s)

    def fn(pos, dcum, ddt, dtf, bias, carry, acc):
        suffix = _sel_left(_tri_upper(tr), dcum) + carry
        dfr = jnp.where(_gate_lanes((tr, 128)), suffix * _sigmoid(-(dtf + bias)), 0.0)
        out = ddt + dfr
        return out, suffix[0:1, :], acc + _colsum(out)

    return _rowk("cum_bwd", fn, s, tr, [(dcum, 128, 0, 0), (ddt_raw, 128, 0, 0), (p, 128, OFF_DTF // 128, 0)],
                 [bias128], [(128, BF16)], [(1, 128), (1, 128)], reverse=True)


ATT_BLOCK = 512
ATT_STRIP = 32


def _head_part(shape, h, dim):
    i = _iota(shape, dim)
    return (i >= h * HD) & (i < (h + 1) * HD)


def _k_augmented(k_blk, cum_blk, j, h):
    tk = k_blk.shape[0]
    lane = _iota((tk, 128), 1)
    col = jnp.sum(jnp.where(lane == NH + 2 * j + h, cum_blk, 0.0), axis=1, keepdims=True)
    c0, c1, c2 = [c.astype(F32) for c in _split3(-col)]
    aug = jnp.where(lane == 0, c0, jnp.where(lane == 1, c1, jnp.where(lane == 2, c2, 0.0)))
    return jnp.concatenate([jnp.where(_head_part((tk, 128), h, 1), k_blk, 0.0), aug], axis=1).astype(BF16)


def _q_augmented_t(q_blk):
    tq = q_blk.shape[0]
    ones = (_iota((128, tq), 0) < 3).astype(BF16)
    return jnp.concatenate([(q_blk * ATT_SCALE).T.astype(BF16), ones], axis=0)


def _rows01(r0, r1):
    sub = _iota((8, r0.shape[1]), 0)
    return jnp.where(sub == 0, r0, jnp.where(sub == 1, r1, 0.0))


def _fold8(x, op, cur):
    for g in range(x.shape[0] // 8):
        cur = op(cur, x[8 * g:8 * (g + 1), :])
    return cur


def _attn_fwd(p, cum, s):
    t = min(ATT_BLOCK, s)
    nq = s // t
    r = ATT_STRIP

    def body(q_ref, k_ref, v_ref, c_ref, o_ref, lse_ref, kaug_sc, vt_sc, s_sc, p_sc, m_sc, l_sc, acc_sc):
        j, qi = pl.program_id(0), pl.program_id(1)

        @pl.when(qi == 0)
        def _():
            for c in range(nq):
                rows = slice(c * t, (c + 1) * t)
                k_blk, vt = k_ref[rows, :], v_ref[rows, :].T
                for h in range(2):
                    kaug_sc[h, rows, :] = _k_augmented(k_blk, c_ref[rows, :], j, h)
                    vt_sc[h, :, rows] = jnp.where(_head_part((128, t), h, 0), vt, 0.0).astype(BF16)

        qaug_t = _q_augmented_t(q_ref[...])
        m_sc[...] = jnp.full(m_sc.shape, -1e30, F32)
        l_sc[...] = jnp.zeros(l_sc.shape, F32)
        acc_sc[...] = jnp.zeros(acc_sc.shape, F32)
        top = _iota((128, t), 0) < HD

        def logits(kb, buf):
            kv = pl.ds(pl.multiple_of(kb * t, t), t)
            for h in range(2):
                s_sc[buf, h] = _dot(kaug_sc[h, kv, :], qaug_t)

        def softmax(buf, diagonal):
            alphas = []
            for h in range(2):
                cur = jnp.full((8, t), -1e30, F32)
                for i in range(t // r):
                    rows = slice(i * r, (i + 1) * r)
                    x = s_sc[buf, h, rows, :]
                    if diagonal:
                        x = jnp.where(_iota((r, t), 1) >= i * r + _iota((r, t), 0), x, -1e30)
                        s_sc[buf, h, rows, :] = x
                    cur = _fold8(x, jnp.maximum, cur)
                m_prev = m_sc[h, 0:1, :]
                m_new = jnp.maximum(m_prev, jnp.max(cur, axis=0, keepdims=True))
                alpha = jnp.exp(m_prev - m_new)
                m_sc[h, 0:1, :] = m_new
                alphas.append(alpha)
                tot = jnp.zeros((8, t), F32)
                for i in range(t // r):
                    rows = slice(i * r, (i + 1) * r)
                    pr = jnp.exp(s_sc[buf, h, rows, :] - m_new)
                    p_sc[buf, h, rows, :] = pr.astype(BF16)
                    tot = _fold8(pr, jnp.add, tot)
                l_sc[h, 0:1, :] = alpha * l_sc[h, 0:1, :] + jnp.sum(tot, axis=0, keepdims=True)
            return alphas

        def accumulate(kb, buf, alphas):
            kv = pl.ds(pl.multiple_of(kb * t, t), t)
            acc_sc[...] = (acc_sc[...] * jnp.where(top, alphas[0], alphas[1])
                           + _dot(vt_sc[0, :, kv], p_sc[buf, 0]) + _dot(vt_sc[1, :, kv], p_sc[buf, 1]))

        def pair(a, b, b_diagonal):
            logits(a, 0)
            logits(b, 1)
            accumulate(a, 0, softmax(0, False))
            accumulate(b, 1, softmax(1, b_diagonal))

        def earlier(u, carry):
            pair(2 * u, 2 * u + 1, False)
            return carry

        lax.fori_loop(0, qi // 2, earlier, 0)

        @pl.when(qi % 2 == 1)
        def _():
            pair(qi - 1, qi, True)

        @pl.when(qi % 2 == 0)
        def _():
            logits(qi, 0)
            accumulate(qi, 0, softmax(0, True))

        l0, l1 = l_sc[0, 0:1, :], l_sc[1, 0:1, :]
        o_ref[...] = (acc_sc[...] / jnp.where(top, l0, l1)).T
        lse_ref[0] = _rows01(m_sc[0, 0:1, :] + jnp.log(l0), m_sc[1, 0:1, :] + jnp.log(l1))

    return pl.pallas_call(
        body, name="attn_fwd", grid=(NH // 2, nq),
        in_specs=[pl.BlockSpec((t, 128), lambda j, qi: (qi, OFF_Q // 128 + j)),
                  pl.BlockSpec((s, 128), lambda j, qi: (0, OFF_K // 128 + j)),
                  pl.BlockSpec((s, 128), lambda j, qi: (0, OFF_V // 128 + j)),
                  pl.BlockSpec((s, 128), lambda j, qi: (0, 0))],
        out_specs=[pl.BlockSpec((t, 128), lambda j, qi: (qi, j)),
                   pl.BlockSpec((1, 8, t), lambda j, qi: (j, 0, qi))],
        out_shape=[jax.ShapeDtypeStruct((s, D), F32), jax.ShapeDtypeStruct((NH // 2, 8, s), F32)],
        scratch_shapes=[pltpu.VMEM((2, s, 256), BF16), pltpu.VMEM((2, 128, s), BF16), pltpu.VMEM((2, 2, t, t), F32),
                        pltpu.VMEM((2, 2, t, t), BF16), pltpu.VMEM((2, 8, t), F32), pltpu.VMEM((2, 8, t), F32),
                        pltpu.VMEM((128, t), F32)],
        compiler_params=_params(("parallel", "arbitrary")),
    )(p, p, p, cum)


def _attn_bwd(p, cum, o, lse, do, s):
    t = min(ATT_BLOCK, s)
    nq = s // t
    r = ATT_STRIP

    def body(q_ref, k_ref, v_ref, c_ref, o_ref, lse_ref, do_ref, dq_ref, dk_ref, dv_ref, dc_ref, dr_ref,
             qaugt_sc, qh_sc, dot_sc, doh_sc, delta_sc, dqt_sc, dr_sc, kaug_sc, vh_sc, kt_sc, s_sc, dp_sc, p_sc, ds_sc,
             dk_sc, dv_sc, dc_sc):
        j, ki = pl.program_id(0), pl.program_id(1)

        @pl.when(ki == 0)
        def _():
            for c in range(nq):
                rows = slice(c * t, (c + 1) * t)
                q_blk, do_blk = q_ref[rows, :], do_ref[rows, :]
                qaugt_sc[:, rows] = _q_augmented_t(q_blk)
                dot_sc[:, rows] = do_blk.T.astype(BF16)
                prod_t = (do_blk * o_ref[rows, :]).T
                delta_sc[:, rows] = _rows01(jnp.sum(prod_t[0:HD], axis=0, keepdims=True),
                                            jnp.sum(prod_t[HD:], axis=0, keepdims=True))
                for h in range(2):
                    head = _head_part((t, 128), h, 1)
                    qh_sc[h, rows, :] = jnp.where(head, q_blk * ATT_SCALE, 0.0).astype(BF16)
                    doh_sc[h, rows, :] = jnp.where(head, do_blk, 0.0).astype(BF16)
            dqt_sc[...] = jnp.zeros(dqt_sc.shape, F32)
            dr_sc[...] = jnp.zeros(dr_sc.shape, F32)

        k_blk, v_blk = k_ref[...], v_ref[...]
        kt = k_blk.T
        for h in range(2):
            kaug_sc[h] = _k_augmented(k_blk, c_ref[...], j, h)
            vh_sc[h] = jnp.where(_head_part((t, 128), h, 1), v_blk, 0.0).astype(BF16)
            kt_sc[h] = jnp.where(_head_part((128, t), h, 0), kt, 0.0).astype(BF16)
        dk_sc[...] = jnp.zeros(dk_sc.shape, F32)
        dv_sc[...] = jnp.zeros(dv_sc.shape, F32)
        dc_sc[...] = jnp.zeros(dc_sc.shape, F32)

        def inputs(qb, buf):
            qs = pl.ds(pl.multiple_of(qb * t, t), t)
            for h in range(2):
                s_sc[buf, h] = _dot(kaug_sc[h], qaugt_sc[:, qs])
                dp_sc[buf, h] = _dot(vh_sc[h], dot_sc[:, qs])

        def elementwise(qb, buf, diagonal):
            qs = pl.ds(pl.multiple_of(qb * t, t), t)
            for h in range(2):
                lse_row, delta_row = lse_ref[0, h:h + 1, qs], delta_sc[h:h + 1, qs]
                tot = jnp.zeros((8, t), F32)
                for i in range(t // r):
                    rows = slice(i * r, (i + 1) * r)
                    x = s_sc[buf, h, rows, :]
                    if diagonal:
                        x = jnp.where(_iota((r, t), 1) >= i * r + _iota((r, t), 0), x, -1e30)
                    pr = jnp.exp(x - lse_row)
                    ds = pr * (dp_sc[buf, h, rows, :] - delta_row)
                    p_sc[buf, h, rows, :] = pr.astype(BF16)
                    ds_sc[buf, h, rows, :] = ds.astype(BF16)
                    dc_sc[h, rows, :] += sum(ds[:, 128 * g:128 * (g + 1)] for g in range(t // 128))
                    tot = _fold8(ds, jnp.add, tot)
                dr_sc[h, :, qs] += tot

        def outputs(qb, buf):
            qs = pl.ds(pl.multiple_of(qb * t, t), t)
            dv_sc[...] += _dot(p_sc[buf, 0], doh_sc[0, qs, :]) + _dot(p_sc[buf, 1], doh_sc[1, qs, :])
            dk_sc[...] += _dot(ds_sc[buf, 0], qh_sc[0, qs, :]) + _dot(ds_sc[buf, 1], qh_sc[1, qs, :])
            dqt_sc[:, qs] += _dot(kt_sc[0], ds_sc[buf, 0]) + _dot(kt_sc[1], ds_sc[buf, 1])

        def pair(a, b, a_diagonal):
            inputs(a, 0)
            inputs(b, 1)
            elementwise(a, 0, a_diagonal)
            outputs(a, 0)
            elementwise(b, 1, False)
            outputs(b, 1)

        def later(u, carry):
            pair(ki + 1 + 2 * u, ki + 2 + 2 * u, False)
            return carry

        n_later = nq - 1 - ki
        lax.fori_loop(0, n_later // 2, later, 0)

        @pl.when(n_later % 2 == 1)
        def _():
            pair(ki, nq - 1, True)

        @pl.when(n_later % 2 == 0)
        def _():
            inputs(ki, 0)
            elementwise(ki, 0, True)
            outputs(ki, 0)

        dk_ref[...] = dk_sc[...].astype(BF16)
        dv_ref[...] = dv_sc[...].astype(BF16)
        dc_ref[...] = jnp.where(_iota((t, 128), 1) < HD, jnp.sum(dc_sc[0], axis=1, keepdims=True),
                                jnp.sum(dc_sc[1], axis=1, keepdims=True))

        @pl.when(ki == nq - 1)
        def _():
            for c in range(nq):
                rows = slice(c * t, (c + 1) * t)
                dq_ref[rows, :] = dqt_sc[:, rows].T * ATT_SCALE
            dr_ref[0] = _rows01(jnp.sum(dr_sc[0], axis=0, keepdims=True), jnp.sum(dr_sc[1], axis=0, keepdims=True))

    whole = lambda off: pl.BlockSpec((s, 128), functools.partial(lambda j, ki, off: (0, off + j), off=off))
    return pl.pallas_call(
        body, name="attn_bwd", grid=(NH // 2, nq),
        in_specs=[whole(OFF_Q // 128),
                  pl.BlockSpec((t, 128), lambda j, ki: (ki, OFF_K // 128 + j)),
                  pl.BlockSpec((t, 128), lambda j, ki: (ki, OFF_V // 128 + j)),
                  pl.BlockSpec((t, 128), lambda j, ki: (ki, 0)),
                  whole(0),
                  pl.BlockSpec((1, 8, s), lambda j, ki: (j, 0, 0)),
                  whole(0)],
        out_specs=[whole(0),
                   pl.BlockSpec((t, 128), lambda j, ki: (ki, j)),
                   pl.BlockSpec((t, 128), lambda j, ki: (ki, j)),
                   pl.BlockSpec((t, 128), lambda j, ki: (ki, j)),
                   pl.BlockSpec((1, 8, s), lambda j, ki: (j, 0, 0))],
        out_shape=[jax.ShapeDtypeStruct((s, D), F32), jax.ShapeDtypeStruct((s, D), BF16), jax.ShapeDtypeStruct((s, D), BF16),
                   jax.ShapeDtypeStruct((s, D), F32), jax.ShapeDtypeStruct((NH // 2, 8, s), F32)],
        scratch_shapes=[pltpu.VMEM((256, s), BF16), pltpu.VMEM((2, s, 128), BF16), pltpu.VMEM((128, s), BF16),
                        pltpu.VMEM((2, s, 128), BF16), pltpu.VMEM((8, s), F32), pltpu.VMEM((128, s), F32),
                        pltpu.VMEM((2, 8, s), F32), pltpu.VMEM((2, t, 256), BF16), pltpu.VMEM((2, t, 128), BF16),
                        pltpu.VMEM((2, 128, t), BF16), pltpu.VMEM((2, 2, t, t), F32), pltpu.VMEM((2, 2, t, t), F32),
                        pltpu.VMEM((2, 2, t, t), BF16), pltpu.VMEM((2, 2, t, t), BF16), pltpu.VMEM((t, 128), F32),
                        pltpu.VMEM((t, 128), F32), pltpu.VMEM((2, t, 128), F32)],
        compiler_params=_params(("parallel", "arbitrary")),
    )(p, p, p, cum, o, lse, do)


def _ln_stats(u):
    mu = _mean(u)
    d = u - mu
    rstd = lax.rsqrt(_mean(d * d) + EPS)
    return d * rstd, rstd


def _ln_bwd(dx, xh, rstd, gam):
    dxh = dx * gam
    return rstd * (dxh - _mean(dxh) - xh * _mean(dxh * xh))


def _rms_bwd(d, xn, r, w):
    t = d * w
    return r * (t - xn * _mean(t * xn)), _colsum(d * xn)


def _mix_norm(y, p, att, w_ssm, w_att, s):
    def fn(pos, y, z, att, w1, w2):
        g = y * _silu(z)
        n1 = g * lax.rsqrt(_mean(g * g) + EPS) * w1
        n2 = att * lax.rsqrt(_mean(att * att) + EPS) * w2
        return (jnp.concatenate([n1, n2], axis=1),)

    return _rowk("mix_norm", fn, s, 256, [(y, D, 0, 0), (p, D, OFF_Z // D, 0), (att, D, 0, 0)],
                 [w_ssm, w_att], [(2 * D, BF16)], [])[0]


def _mix_norm_bwd(dmix, y, p, att, w_ssm, w_att, s):
    def fn(pos, dmix, y, z, att, w1, w2, a1, a2):
        sz = _silu(z)
        g = y * sz
        r1 = lax.rsqrt(_mean(g * g) + EPS)
        dg, dw1 = _rms_bwd(dmix[:, :D], g * r1, r1, w1)
        r2 = lax.rsqrt(_mean(att * att) + EPS)
        datt, dw2 = _rms_bwd(dmix[:, D:], att * r2, r2, w2)
        return dg * sz, dg * y * _dsilu(z), datt, a1 + dw1, a2 + dw2

    return _rowk("mix_norm_bwd", fn, s, 256, [(dmix, 2 * D, 0, 0), (y, D, 0, 0), (p, D, OFF_Z // D, 0), (att, D, 0, 0)],
                 [w_ssm, w_att], [(D, F32), (D, BF16), (D, F32)], [(1, D), (1, D)])


def _ln1(x0, y, g1, gam, bet, sc2, sh2, s):
    def fn(pos, x0, y, g1, gam, bet, sc2, sh2):
        xh, _ = _ln_stats(ALPHA * x0 + (1.0 + g1) * y)
        x1 = xh * gam + bet
        return x1, _modulate(x1, sc2, sh2)

    return _rowk("ln1", fn, s, 256, [(x0, D, 0, 0), (y, D, 0, 0)], [g1, gam, bet, sc2, sh2], [(D, F32), (D, BF16)], [])


def _ln2_loss(x1, ff, tgt, g2, gam, bet, s):
    def fn(pos, x1, ff, tgt, g2, gam, bet, a_loss, a_dgam, a_dbet, a_dg2):
        xh, rstd = _ln_stats(ALPHA * x1 + (1.0 + g2) * ff)
        err = xh * gam + bet - tgt
        dx2 = err * (1.0 / D)
        du = _ln_bwd(dx2, xh, rstd, gam)
        return (du, du * (1.0 + g2), a_loss + _colsum(err * err), a_dgam + _colsum(dx2 * xh),
                a_dbet + _colsum(dx2), a_dg2 + _colsum(du * ff))

    return _rowk("ln2_loss", fn, s, 256, [(x1, D, 0, 0), (ff, D, 0, 0), (tgt, D, 0, 0)], [g2, gam, bet],
                 [(D, F32), (D, BF16)], [(1, D)] * 4)


def _ln1_bwd(dh2, du2, x0, y, g1, gam, bet, sc2, s):
    def fn(pos, dh2, du2, x0, y, g1, gam, bet, sc2, a_sc, a_sh, a_gam, a_bet, a_g1):
        xh, rstd = _ln_stats(ALPHA * x0 + (1.0 + g1) * y)
        x1 = xh * gam + bet
        dx1 = ALPHA * du2 + dh2 * (1.0 + sc2)
        du1 = _ln_bwd(dx1, xh, rstd, gam)
        return (du1, du1 * (1.0 + g1), a_sc + _colsum(dh2 * x1), a_sh + _colsum(dh2), a_gam + _colsum(dx1 * xh),
                a_bet + _colsum(dx1), a_g1 + _colsum(du1 * y))

    return _rowk("ln1_bwd", fn, s, 256, [(dh2, D, 0, 0), (du2, D, 0, 0), (x0, D, 0, 0), (y, D, 0, 0)],
                 [g1, gam, bet, sc2], [(D, F32), (D, BF16)], [(1, D)] * 5)


def _input_grad(dh1, du1, x0, sc1, s):
    def fn(pos, dh1, du1, x0, sc1, a_sc, a_sh):
        return ALPHA * du1 + dh1 * (1.0 + sc1), a_sc + _colsum(dh1 * x0), a_sh + _colsum(dh1)

    return _rowk("input_grad", fn, s, 256, [(dh1, D, 0, 0), (du1, D, 0, 0), (x0, D, 0, 0)], [sc1],
                 [(D, F32)], [(1, D)] * 2)


def _adamw(name, w, g, m, v, *, tr, slots):
    r, c = w.shape

    def body(w_ref, g_ref, m_ref, v_ref, g_out, d_out, m_out, v_out):
        if slots:
            grad = g_ref[0].astype(F32)
            for k in range(1, N_DEV):
                grad = grad + g_ref[k].astype(F32)
        else:
            grad = g_ref[...]
        m_new = ADAM_B1 * m_ref[...] + (1.0 - ADAM_B1) * grad
        v_new = ADAM_B2 * v_ref[...] + (1.0 - ADAM_B2) * (grad * grad)
        m_hat = m_new / (1.0 - ADAM_B1 ** ADAM_STEP)
        v_hat = v_new / (1.0 - ADAM_B2 ** ADAM_STEP)
        g_out[...] = grad
        d_out[...] = -ADAM_LR * (m_hat / (jnp.sqrt(v_hat) + ADAM_EPS) + ADAM_WD * w_ref[...])
        m_out[...] = m_new
        v_out[...] = v_new

    tile = pl.BlockSpec((tr, c), lambda i: (i, 0))
    g_spec = pl.BlockSpec((N_DEV, tr, c), lambda i: (0, i, 0)) if slots else tile
    return pl.pallas_call(
        body, name=name, grid=(r // tr,),
        in_specs=[tile, g_spec, tile, tile], out_specs=[tile] * 4,
        out_shape=[jax.ShapeDtypeStruct((r, c), F32)] * 4,
        compiler_params=_params(("parallel",)),
    )(w, g, m, v)


def _dot_f32(a, b, dims=NN):
    a0, a1, a2 = _split3(a)
    b0, b1, b2 = _split3(b)
    acc = _dot(a0, b0, dims)
    for x, y in ((a0, b1), (a1, b0), (a1, b1), (a0, b2), (a2, b0)):
        acc = acc + _dot(x, y, dims)
    return acc


def _ada_mod(c_all, w_shard, b_shard):
    def body(c_ref, w_ref, b_ref, o_ref):
        act = _silu(c_ref[...])
        act16 = jnp.concatenate([act, jnp.zeros_like(act)], axis=0)
        o_ref[...] = _dot_f32(act16, w_ref[...])[0:N_DEV] + b_ref[...]

    return pl.pallas_call(
        body, name="ada_mod", out_shape=jax.ShapeDtypeStruct((N_DEV, w_shard.shape[1]), F32),
        compiler_params=_params(None),
    )(c_all, w_shard, b_shard)


def _ada_grad(c_all, dmod_cols, dmod_all):
    def body(c_ref, dc_ref, da_ref, gw_ref, gb_ref):
        act = _silu(c_ref[...])
        act16 = jnp.concatenate([act, jnp.zeros_like(act)], axis=0)
        dm = dc_ref[...]
        dm16 = jnp.concatenate([dm, jnp.zeros_like(dm)], axis=0)
        gw_ref[...] = _dot_f32(act16, dm16, TN)
        gb_ref[...] = _colsum(da_ref[...])

    return pl.pallas_call(
        body, name="ada_grad",
        out_shape=[jax.ShapeDtypeStruct((D, dmod_cols.shape[1]), F32), jax.ShapeDtypeStruct((1, 6 * D), F32)],
        compiler_params=_params(None),
    )(c_all, dmod_cols, dmod_all)


def _sum_slots(name, g):
    def body(g_ref, o_ref):
        acc = g_ref[0]
        for k in range(1, N_DEV):
            acc = acc + g_ref[k]
        o_ref[...] = acc

    return pl.pallas_call(body, name=name, out_shape=jax.ShapeDtypeStruct(g.shape[1:], F32),
                          compiler_params=_params(None))(g)


def _exchange(name, xs, scatter):
    n = len(xs)
    n_peer = N_DEV - 1

    def body(*refs):
        x_refs, o_refs = refs[:n], refs[n:2 * n]
        send_sems, recv_sems, local_sems = refs[2 * n:]
        mx, my, mc = lax.axis_index("x"), lax.axis_index("y"), lax.axis_index("c")
        me = 4 * mx + 2 * my + mc

        def src(a, slot):
            return x_refs[a].at[slot] if scatter else x_refs[a]

        own = [pltpu.make_async_copy(src(a, me), o_refs[a].at[me], local_sems.at[a]) for a in range(n)]
        for cp in own:
            cp.start()
        sends = []
        for d in range(1, N_DEV):
            px = 1 - mx if d & 4 else mx
            py = 1 - my if d & 2 else my
            pc = 1 - mc if d & 1 else mc
            peer = 4 * px + 2 * py + pc
            for a in range(n):
                def copy(src_slot, dst_slot, a=a, d=d, to=(px, py, pc)):
                    return pltpu.make_async_remote_copy(
                        src_ref=src(a, src_slot), dst_ref=o_refs[a].at[dst_slot],
                        send_sem=send_sems.at[a * n_peer + d - 1], recv_sem=recv_sems.at[a * n_peer + d - 1],
                        device_id=to, device_id_type=pl.DeviceIdType.MESH)

                out = copy(peer, me)
                out.start()
                sends.append((out, copy(me, peer)))
        for _, arrival in sends:
            arrival.wait_recv()
        for out, _ in sends:
            out.wait_send()
        for cp in own:
            cp.wait()

    shapes = [tuple(x.shape[1:] if scatter else x.shape) for x in xs]
    return pl.pallas_call(
        body, name=name,
        in_specs=[pl.BlockSpec(memory_space=pl.ANY)] * n, out_specs=[pl.BlockSpec(memory_space=pl.ANY)] * n,
        out_shape=[jax.ShapeDtypeStruct((N_DEV,) + sh, x.dtype) for sh, x in zip(shapes, xs)],
        scratch_shapes=[pltpu.SemaphoreType.DMA((n * n_peer,)), pltpu.SemaphoreType.DMA((n * n_peer,)),
                        pltpu.SemaphoreType.DMA((n,))],
        compiler_params=pltpu.CompilerParams(has_side_effects=True),
    )(*xs)


def _after(x, zero):
    return x if zero is None else x + zero.reshape(-1)[0].astype(x.dtype)


_HBM = pl.BlockSpec(memory_space=pltpu.HBM)
_SEM = pl.BlockSpec(memory_space=pltpu.SEMAPHORE)


def _exchange_copies(x_refs, land_refs, send_sems, recv_sems, scatter):
    n = len(x_refs)
    n_peer = N_DEV - 1
    mx, my, mc = lax.axis_index("x"), lax.axis_index("y"), lax.axis_index("c")
    me = 4 * mx + 2 * my + mc
    pairs = []
    for d in range(1, N_DEV):
        px = 1 - mx if d & 4 else mx
        py = 1 - my if d & 2 else my
        pc = 1 - mc if d & 1 else mc
        peer = 4 * px + 2 * py + pc
        for a in range(n):
            def copy(src_slot, dst_slot, a=a, d=d, to=(px, py, pc)):
                return pltpu.make_async_remote_copy(
                    src_ref=x_refs[a].at[src_slot] if scatter else x_refs[a], dst_ref=land_refs[a].at[dst_slot],
                    send_sem=send_sems.at[a * n_peer + d - 1], recv_sem=recv_sems.at[a * n_peer + d - 1],
                    device_id=to, device_id_type=pl.DeviceIdType.MESH)

            pairs.append((copy(peer, me), copy(me, peer)))
    return me, pairs


def _exchange_async(name, xs, scatter, collective_id):
    n = len(xs)
    shapes = [tuple(x.shape[1:] if scatter else x.shape) for x in xs]
    x_refs = [jax.new_ref(x, memory_space=pltpu.MemorySpace.HBM) for x in xs]
    land_refs = [jax.empty_ref(jax.ShapeDtypeStruct((N_DEV,) + sh, x.dtype), memory_space=pltpu.MemorySpace.HBM)
                 for sh, x in zip(shapes, xs)]

    @pl.kernel(mesh=plsc.ScalarSubcoreMesh(axis_name="sequencer", num_cores=1), name=name,
               scratch_types=(pltpu.SemaphoreType.DMA((n * (N_DEV - 1),)), pltpu.SemaphoreType.DMA((n * (N_DEV - 1),)),
                              pltpu.SemaphoreType.DMA((n,))),
               compiler_params=pltpu.CompilerParams(collective_id=collective_id))
    def launch(send_sems, recv_sems, own_sems):
        barrier = pltpu.get_barrier_semaphore()
        mx, my, mc = lax.axis_index("x"), lax.axis_index("y"), lax.axis_index("c")
        for d in range(1, N_DEV):
            peer = (1 - mx if d & 4 else mx, 1 - my if d & 2 else my, 1 - mc if d & 1 else mc)
            pl.semaphore_signal(barrier, inc=1, device_id=peer, device_id_type=pl.DeviceIdType.MESH)
        pl.semaphore_wait(barrier, N_DEV - 1)
        me, pairs = _exchange_copies(x_refs, land_refs, send_sems, recv_sems, scatter)
        own = [pltpu.make_async_copy(x_refs[a].at[me] if scatter else x_refs[a], land_refs[a].at[me], own_sems.at[a])
               for a in range(n)]
        for cp in own:
            cp.start()
        for out, _ in pairs:
            out.start()
        for out, arrival in pairs:
            arrival.wait_recv()
            out.wait_send()
        for cp in own:
            cp.wait()

    launch()
    return lambda: [r[...] for r in land_refs]


def _exchange_start(name, xs, scatter):
    n = len(xs)
    shapes = [tuple(x.shape[1:] if scatter else x.shape) for x in xs]

    def body(*refs):
        x_refs, land_refs = refs[:n], refs[n:2 * n]
        send_sems, recv_sems = refs[2 * n], refs[2 * n + 1]
        token, own_sems = refs[4 * n + 2], refs[4 * n + 3]
        me, pairs = _exchange_copies(x_refs, land_refs, send_sems, recv_sems, scatter)
        own = [pltpu.make_async_copy(x_refs[a].at[me] if scatter else x_refs[a], land_refs[a].at[me], own_sems.at[a])
               for a in range(n)]
        for cp in own:
            cp.start()
        for out, _ in pairs:
            out.start()
        for cp in own:
            cp.wait()
        token[...] = jnp.zeros(token.shape, token.dtype)

    lands = [pltpu.with_memory_space_constraint(lax.empty((N_DEV,) + sh, x.dtype), pltpu.HBM) for sh, x in zip(shapes, xs)]
    res = pl.pallas_call(
        body, name=name,
        out_shape=(pltpu.SemaphoreType.DMA((n * (N_DEV - 1),)), pltpu.SemaphoreType.DMA((n * (N_DEV - 1),)),
                   *[pltpu.HBM(x.shape, x.dtype) for x in xs], *[pltpu.HBM(l.shape, l.dtype) for l in lands],
                   jax.ShapeDtypeStruct((8, 128), F32)),
        in_specs=[_HBM] * (2 * n),
        out_specs=(_SEM, _SEM, *[_HBM] * (2 * n), pl.BlockSpec(memory_space=pltpu.VMEM)),
        input_output_aliases={i: 2 + i for i in range(2 * n)},
        scratch_shapes=[pltpu.SemaphoreType.DMA((n,))],
        compiler_params=pltpu.CompilerParams(has_side_effects=pltpu.SideEffectType.DATAFLOW_SIDE_EFFECTING),
    )(*[pltpu.with_memory_space_constraint(x, pltpu.HBM) for x in xs], *lands)
    return dict(send=res[0], recv=res[1], xs=list(res[2:2 + n]), lands=list(res[2 + n:2 + 2 * n]), token=res[2 + 2 * n])


def _exchange_wait(name, handle, after, scatter):
    n = len(handle['xs'])

    def body(*refs):
        x_refs, land_refs = refs[:n], refs[n:2 * n]
        send_sems, recv_sems = refs[2 * n], refs[2 * n + 1]
        _, pairs = _exchange_copies(x_refs, land_refs, send_sems, recv_sems, scatter)
        for out, arrival in pairs:
            out.wait_send()
            arrival.wait_recv()

    res = pl.pallas_call(
        body, name=name,
        out_shape=tuple(pltpu.HBM(a.shape, a.dtype) for a in handle['xs'] + handle['lands']),
        in_specs=[_HBM] * (2 * n) + [_SEM, _SEM, pl.BlockSpec(memory_space=pl.ANY)],
        out_specs=tuple([_HBM] * (2 * n)),
        input_output_aliases={i: i for i in range(2 * n)},
        compiler_params=pltpu.CompilerParams(has_side_effects=pltpu.SideEffectType.DATAFLOW_SIDE_EFFECTING),
    )(*handle['xs'], *handle['lands'], handle['send'], handle['recv'], after)
    return list(res[n:])


def _relu2(a):
    r = jnp.maximum(a, 0.0)
    return r * r


def _relu2_grad(acc, a):
    return acc * (2.0 * jnp.maximum(a, 0.0))


def _local_step(x0, tgt, mod, wcat, late_weights, send_grads, conv_w, conv_b, dt_bias, a_log, d_skip, ssm_norm_w, f_bias,
                attn_norm_w, ln1_g, ln1_b, ln2_g, ln2_b):
    ff_w = DFF // N_DEV
    s = x0.shape[0]
    tm = min(1024, s)
    ts = min(1024, s)
    sh1, sc1, g1, sh2, sc2, g2 = [mod[:, i * D:(i + 1) * D] for i in range(6)]
    zero = jnp.zeros((1, 128 - 2 * NH), F32)
    bias128 = jnp.concatenate([dt_bias, f_bias, zero], axis=1)
    alog128 = jnp.concatenate([a_log, jnp.zeros((1, 128 - NH), F32)], axis=1)
    dskip_x = jnp.repeat(d_skip, HD, axis=1)
    w_xs, w_bc, b_xs, b_bc = conv_w[:, :D], conv_w[:, D:], conv_b[:, :D], conv_b[:, D:]

    p = _mm_nn("in_proj", x0, wcat, tm=tm, tn=1152, tk=D, out_dtype=F32, pro=_modulate, aux=(sc1, sh1))
    xs_a, bc_a = _conv_fwd(p, w_xs, b_xs, w_bc, b_bc, s)
    y_ssd, states = _ssd_fwd(xs_a, bc_a, p, bias128, alog128, dskip_x, s)
    cum = _cum_fwd(p, bias128, s)
    att, lse = _attn_fwd(p, cum, s)
    wout, w1s, w2 = late_weights(lse)
    ymix = _mix_norm(y_ssd, p, att, ssm_norm_w, attn_norm_w, s)
    y = _mm_nn("out_proj", ymix, wout, tm=tm, tn=1024, tk=2 * D, out_dtype=F32)
    x1, h2 = _ln1(x0, y, g1, ln1_g, ln1_b, sc2, sh2, s)
    a1 = _mm_nn("ff_in", h2, w1s, tm=tm, tn=ff_w, tk=D, out_dtype=F32)
    ff = _mm_nn("ff_out", a1, w2, tm=tm, tn=1024, tk=1024, out_dtype=F32, pro=_relu2)
    du2, dff, sq_err, d_ln2_g, d_ln2_b, d_g2 = _ln2_loss(x1, ff, tgt, g2, ln2_g, ln2_b, s)

    da1 = _mm_nt("d_ff_hidden", [(dff, D, 0)], [(w2, D, 0)], n=DFF, tm=tm, tn=1024, out_dtype=BF16, epi=_relu2_grad,
                 epi_aux=(a1,))
    d_w2 = _mm_tn("d_w_ff_out", a1, dff, tm=1024, tn=1024, ts=ts, pro=_relu2)
    d_w1s = _mm_tn("d_w_ff_in", h2, da1, tm=1024, tn=ff_w, ts=ts, col_shards=True)
    dh2 = _mm_nt("d_ff_input", [(da1, ff_w, k) for k in range(N_DEV)], [(w1s, ff_w, k) for k in range(N_DEV)], n=D,
                 tm=min(512, s), tn=1024, out_dtype=F32)
    sent = send_grads("ff", [d_w1s, d_w2.reshape(N_DEV, -1, D)])
    du1, dy, d_sc2, d_sh2, d_ln1_g, d_ln1_b, d_g1 = _ln1_bwd(dh2, du2, x0, y, g1, ln1_g, ln1_b, _after(sc2, sent), s)

    dmix = _mm_nt("d_mix", [(dy, D, 0)], [(wout, D, 0)], n=2 * D, tm=tm, tn=1024, out_dtype=F32)
    d_wout = _mm_tn("d_w_out", ymix, dy, tm=1024, tn=1024, ts=ts)
    sent = send_grads("out", [d_wout.reshape(N_DEV, -1, D)])
    dy_ssd, dz, datt, d_ssm_w, d_attn_w = _mix_norm_bwd(dmix, y_ssd, p, att, _after(ssm_norm_w, sent), attn_norm_w, s)
    dq, dk, dv, dcs, drs = _attn_bwd(p, cum, att, lse, datt, s)
    dxs_a, dbc_a, ddt_raw, d_alog, d_dskip = _ssd_bwd(dy_ssd, xs_a, bc_a, p, states, bias128, alog128, dskip_x, s)
    dcum = jnp.pad(drs[:, :2, :].reshape(NH, s).T - dcs[:, ::HD], ((0, 0), (NH, 128 - 2 * NH)))
    ddtf, _, d_bias = _cum_bwd(dcum, ddt_raw, p, bias128, s)
    dxs, dbc, d_wc_xs, d_bc_xs, d_wc_bc, d_bc_bc = _conv_bwd(dxs_a, dbc_a, p, w_xs, b_xs, w_bc, b_bc, s)

    segs = [(dz, OFF_Z, D), (dxs, OFF_XS, D), (dq, OFF_Q, D), (dk, OFF_K, D), (dv, OFF_V, D), (dbc, OFF_BC, 512),
            (ddtf, OFF_DTF, 128)]
    d_z, d_xs, d_q, d_k, d_v, d_bcw, d_dtf = [
        _mm_tn("d_w_in_%d" % i, x0, a, tm=1024, tn=min(w, 1024), ts=ts, pro=_modulate, aux=(sc1, sh1))
        for i, (a, _, w) in enumerate(segs)]
    d_w_in = dict(z=d_z, xs=d_xs, bc=d_bcw, dt=d_dtf[:, :NH], q=d_q, k=d_k, v=d_v, f=d_dtf[:, NH:2 * NH])
    sent = send_grads("in", [_shard_w_in_grad(d_w_in)])
    segs[-1] = (_after(ddtf, sent), OFF_DTF, 128)
    dh1 = _mm_nt("d_h1", [(a, w, 0) for a, _, w in segs], [(wcat, w, off // w) for _, off, w in segs], n=D,
                 tm=min(512, s), tn=1024, out_dtype=F32)
    grad_x, d_sc1, d_sh1 = _input_grad(dh1, du1, x0, sc1, s)

    return dict(
        loss=(0.5 / D) * jnp.sum(sq_err), grad_x=grad_x,
        d_mod=jnp.concatenate([d_sh1, d_sc1, d_g1, d_sh2, d_sc2, d_g2], axis=1),
        d_conv_w=jnp.concatenate([d_wc_xs[:4], d_wc_bc[:4]], axis=1), d_conv_b=jnp.concatenate([d_bc_xs, d_bc_bc], axis=1),
        d_ssm_norm_w=d_ssm_w, d_attn_norm_w=d_attn_w, d_ln1_g=d_ln1_g, d_ln1_b=d_ln1_b, d_ln2_g=d_ln2_g, d_ln2_b=d_ln2_b,
        d_gate_bias=d_bias, d_a_log=d_alog, d_d_skip=d_dskip)


W_IN_SEGS = [('z', W_Z, D), ('xs', W_XS, D), ('bc', W_BC, 512), ('dt', W_DT, NH), ('q', W_Q, D), ('k', W_K, D),
             ('v', W_V, D), ('f', W_F, NH)]
SHARD_W = IN_COLS // N_DEV


def _pack_w_in(shards):
    def cols(lo, hi):
        pieces = []
        while lo < hi:
            dev = lo // SHARD_W
            end = min(hi, (dev + 1) * SHARD_W)
            pieces.append(shards[dev][:, lo - dev * SHARD_W:end - dev * SHARD_W])
            lo = end
        return pieces

    seg = {n: cols(off, off + w) for n, off, w in W_IN_SEGS}
    pieces = seg['z'] + seg['xs'] + seg['q'] + seg['k'] + seg['v'] + seg['bc'] + seg['dt'] + seg['f']
    return jnp.concatenate(pieces + [jnp.zeros((D, 128 - 2 * NH), shards.dtype)], axis=1)


def _shard_w_in_grad(d_w_in):
    blocks = []
    for dev in range(N_DEV):
        lo, hi = dev * SHARD_W, (dev + 1) * SHARD_W
        pieces = [d_w_in[n][:, max(lo, off) - off:min(hi, off + w) - off] for n, off, w in W_IN_SEGS
                  if max(lo, off) < min(hi, off + w)]
        blocks.append(jnp.concatenate(pieces, axis=1))
    return jnp.stack(blocks, axis=0)


WEIGHTS = ['w_ada', 'b_ada', 'w_in', 'conv_w', 'conv_b', 'dt_bias', 'a_log', 'd_skip', 'ssm_norm_w', 'f_bias',
           'attn_norm_w', 'w_out', 'ln1_g', 'ln1_b', 'w_ff_in', 'w_ff_out', 'ln2_g', 'ln2_b']
BIG = ['w_in', 'w_out', 'w_ff_in', 'w_ff_out']
SMALL = ['b_ada', 'conv_b', 'ssm_norm_w', 'attn_norm_w', 'ln1_g', 'ln1_b', 'ln2_g', 'ln2_b', 'dt_bias', 'a_log', 'd_skip',
         'f_bias', 'conv_w']


def _pad_lanes(v, n=128):
    return jnp.pad(v, ((0, 0), (0, n - v.shape[1])))


def _small_block(vals):
    rows = [_pad_lanes(vals[n].reshape(1, -1), -(-vals[n].size // 128) * 128).reshape(-1, 128) for n in SMALL]
    block = jnp.concatenate(rows, axis=0)
    return jnp.pad(block, ((0, 120 - block.shape[0]), (0, 0)))


def _small_unblock(block, like):
    out, r = {}, 0
    for n in SMALL:
        size = like[n].size
        nr = -(-size // 128)
        out[n] = block[r:r + nr].reshape(-1)[:size].reshape(like[n].shape)
        r += nr
    return out


def kernel(x, c, w_ada, b_ada, w_in, conv_w, conv_b, dt_bias, a_log, d_skip, ssm_norm_w, f_bias, attn_norm_w, w_out, ln1_g, ln1_b, w_ff_in, w_ff_out, ln2_g, ln2_b, loss_target, m_w_ada, m_b_ada, m_w_in, m_conv_w, m_conv_b, m_dt_bias, m_a_log, m_d_skip, m_ssm_norm_w, m_f_bias, m_attn_norm_w, m_w_out, m_ln1_g, m_ln1_b, m_w_ff_in, m_w_ff_out, m_ln2_g, m_ln2_b, v_w_ada, v_b_ada, v_w_in, v_conv_w, v_conv_b, v_dt_bias, v_a_log, v_d_skip, v_ssm_norm_w, v_f_bias, v_attn_norm_w, v_w_out, v_ln1_g, v_ln1_b, v_w_ff_in, v_w_ff_out, v_ln2_g, v_ln2_b):
    args = dict(locals())
    w = {n: args[n] for n in WEIGHTS}
    m = {n: args['m_' + n] for n in WEIGHTS}
    v = {n: args['v_' + n] for n in WEIGHTS}
    me = 4 * lax.axis_index("x") + 2 * lax.axis_index("y") + lax.axis_index("c")
    ada_cols = 6 * D // N_DEV
    conv_cols = conv_w.shape[2]

    c_all, conv_all = _exchange("gather_cond", [c, conv_w[0]], False)
    c_all = c_all.reshape(N_DEV, D)
    conv_w_full = conv_all.transpose(1, 0, 2).reshape(4, N_DEV * conv_cols)
    b_shard = lax.dynamic_slice(b_ada, (0, me * ada_cols), (1, ada_cols))
    mod_all, = _exchange("gather_mod", [_ada_mod(c_all, w_ada[0], b_shard)], False)
    mod = lax.dynamic_index_in_dim(mod_all, me, axis=1, keepdims=False).reshape(1, 6 * D)

    win_s, = _exchange("gather_w_in", [_after(w_in[0].astype(BF16), mod * 0)], False)
    first_done = win_s[0, 0:1, 0:1] * 0
    rest = _exchange_async("gather_rest", [_after(w[n][0].astype(BF16), first_done) for n in BIG[1:]], False, 1)

    def late_weights(after):
        wout_s, w1s, w2_s = rest()
        return wout_s.reshape(2 * D, D), w1s, w2_s.reshape(DFF, D)

    sends = {}

    def send_grads(tag, blocks):
        sends[tag] = _exchange_async("scatter_" + tag, blocks, True, {'ff': 2, 'out': 3, 'in': 4}[tag])
        return sum(b.reshape(-1)[0].astype(F32) * 0 for b in blocks)

    out = _local_step(x[0], loss_target[0], mod, _pack_w_in(win_s), late_weights, send_grads,
                      conv_w_full, conv_b, dt_bias, a_log, d_skip, ssm_norm_w, f_bias, attn_norm_w, ln1_g, ln1_b, ln2_g, ln2_b)
    g_ff_in, g_ff_out = sends['ff']()
    g_out, = sends['out']()
    g_in, = sends['in']()
    g_parts = [g_in, g_out, g_ff_in, g_ff_out]
    big = {n: _adamw("adamw_" + n, w[n][0], g, m[n][0], v[n][0], tr=256, slots=True) for n, g in zip(BIG, g_parts)}

    small = jnp.concatenate(
        [out['d_mod'], out['d_conv_w'].reshape(1, -1), out['d_conv_b'], out['d_ssm_norm_w'], out['d_attn_norm_w'],
         out['d_ln1_g'], out['d_ln1_b'], out['d_ln2_g'], out['d_ln2_b'], out['d_gate_bias'], out['d_a_log'],
         out['d_d_skip'], jnp.zeros((1, 128), F32)], axis=1).reshape(-1, 128)
    small_all, = _exchange("gather_small", [small], False)
    ssum = _sum_slots("sum_small", small_all)
    dmod_all = small_all[:, :6 * D // 128].reshape(N_DEV, 6 * D)
    g_w_ada, g_b_ada = _ada_grad(c_all, lax.dynamic_slice(dmod_all, (0, me * ada_cols), (N_DEV, ada_cols)), dmod_all)
    rows = lambda a, b: ssum[a:b].reshape(1, -1)
    g_conv_w = lax.dynamic_slice(ssum[48:96].reshape(4, N_DEV * conv_cols), (0, me * conv_cols), (4, conv_cols))
    g_small = dict(b_ada=g_b_ada, conv_w=g_conv_w[None], conv_b=rows(96, 108), ssm_norm_w=rows(108, 116),
                   attn_norm_w=rows(116, 124), ln1_g=rows(124, 132), ln1_b=rows(132, 140), ln2_g=rows(140, 148),
                   ln2_b=rows(148, 156), dt_bias=ssum[156:157, :NH], f_bias=ssum[156:157, NH:2 * NH],
                   a_log=ssum[157:158, :NH], d_skip=ssum[158:159, :NH])
    sm = _adamw("adamw_small", _small_block(w), _small_block(g_small), _small_block(m), _small_block(v), tr=120, slots=False)
    ada = _adamw("adamw_ada", w_ada[0], g_w_ada, m_w_ada[0], v_w_ada[0], tr=256, slots=False)

    results = []
    for k in range(4):
        vals = _small_unblock(sm[k], w)
        vals['w_ada'] = ada[k][None]
        for n in BIG:
            vals[n] = big[n][k][None]
        results.append(vals)
    loss = lax.psum(out['loss'], ("x", "y", "c"))
    return (loss, out['grad_x'][None], *[res[n] for res in results for n in WEIGHTS])
```

```python
import functools

import jax
import jax.numpy as jnp
from jax import lax
from jax.experimental import pallas as pl
from jax.experimental.pallas import tpu as pltpu
from jax.experimental.pallas import tpu_sc as plsc

F32, BF16 = jnp.float32, jnp.bfloat16

N_DEV = 8
D = 1024
NH, HD = 16, 64
NSTATE = 128
CHUNK = 128
HG = 8
DFF = 4096
ALPHA = 2.0 ** 0.25
EPS = 1e-5
ATT_SCALE = HD ** -0.5

OFF_Z, OFF_XS, OFF_Q, OFF_K, OFF_V, OFF_BC, OFF_DTF = 0, 1024, 2048, 3072, 4096, 5120, 5632
PCOLS = 5760
W_Z, W_XS, W_BC, W_DT, W_Q, W_K, W_V, W_F = 0, 1024, 2048, 2560, 2576, 3600, 4624, 5648
IN_COLS = 5664

ADAM_LR, ADAM_B1, ADAM_B2, ADAM_EPS, ADAM_WD, ADAM_STEP = 0.001, 0.9, 0.999, 1e-08, 0.01, 10

VMEM_LIMIT = 56 << 20

NN = (((1,), (0,)), ((), ()))
NT = (((1,), (1,)), ((), ()))
TN = (((0,), (0,)), ((), ()))


def _dot(a, b, dims=NN):
    return lax.dot_general(a, b, dims, preferred_element_type=F32)


def _bdot(a, b, dims=NN):
    return _dot(a.astype(BF16), b.astype(BF16), dims)


def _split3(v):
    parts, rest = [], v
    for _ in range(3):
        p = rest.astype(BF16)
        parts.append(p)
        rest = rest - p.astype(F32)
    return parts


def _sel_left(m01, v):
    return sum(_dot(m01, p) for p in _split3(v))


def _sel_right(v, m01, dims=NN):
    return sum(_dot(p, m01, dims) for p in _split3(v))


def _iota(shape, dim):
    return lax.broadcasted_iota(jnp.int32, shape, dim)


def _tri_lower(n):
    return (_iota((n, n), 1) <= _iota((n, n), 0)).astype(BF16)


def _tri_upper(n):
    return (_iota((n, n), 1) >= _iota((n, n), 0)).astype(BF16)


def _head_expand():
    return (lax.shift_right_logical(_iota((128, D), 1), 6) == _iota((128, D), 0)).astype(BF16)


def _head_reduce():
    return (lax.shift_right_logical(_iota((D, 128), 0), 6) == _iota((D, 128), 1)).astype(BF16)


def _sigmoid(x):
    return 1.0 / (1.0 + jnp.exp(-x))


def _silu(x):
    return x * _sigmoid(x)


def _dsilu(x):
    s = _sigmoid(x)
    return s * (1.0 + x * (1.0 - s))


def _softplus(x):
    return jnp.maximum(x, 0.0) + jnp.log(1.0 + jnp.exp(-jnp.abs(x)))


def _log_sigmoid(x):
    return jnp.minimum(x, 0.0) - jnp.log(1.0 + jnp.exp(-jnp.abs(x)))


def _params(sem):
    return pltpu.CompilerParams(dimension_semantics=sem, vmem_limit_bytes=VMEM_LIMIT)


def _mm_nn(name, a, b, *, tm, tn, tk, out_dtype, pro=None, aux=()):
    m, k_all = a.shape
    b_sharded = b.ndim == 3
    n = b.shape[0] * b.shape[2] if b_sharded else b.shape[1]
    assert not b_sharded or tn == b.shape[2]
    nk = k_all // tk
    n_aux = len(aux)
    b_spec = (pl.BlockSpec((None, tk, tn), lambda i, j, k: (j, k, 0)) if b_sharded
              else pl.BlockSpec((tk, tn), lambda i, j, k: (k, j)))

    def body(a_ref, b_ref, *rest):
        aux_refs, o_ref = rest[:n_aux], rest[n_aux]
        at = a_ref[...]
        if pro is not None:
            at = pro(at, *[r[...] for r in aux_refs])
        part = _bdot(at, b_ref[...])
        if nk == 1:
            o_ref[...] = part.astype(out_dtype)
            return
        acc_ref = rest[n_aux + 1]
        kk = pl.program_id(2)

        @pl.when(kk == 0)
        def _():
            acc_ref[...] = part

        @pl.when(kk > 0)
        def _():
            acc_ref[...] += part

        @pl.when(kk == nk - 1)
        def _():
            o_ref[...] = acc_ref[...].astype(out_dtype)

    return pl.pallas_call(
        body, name=name,
        grid=(m // tm, n // tn, nk),
        in_specs=[pl.BlockSpec((tm, tk), lambda i, j, k: (i, k)), b_spec]
        + [pl.BlockSpec((1, tk), lambda i, j, k: (0, k)) for _ in aux],
        out_specs=pl.BlockSpec((tm, tn), lambda i, j, k: (i, j)),
        out_shape=jax.ShapeDtypeStruct((m, n), out_dtype),
        scratch_shapes=[] if nk == 1 else [pltpu.VMEM((tm, tn), F32)],
        compiler_params=_params(("parallel", "parallel", "arbitrary")),
    )(a, b, *aux)


def _mm_nt(name, a_list, b_list, *, n, tm, tn, out_dtype, epi=None, epi_aux=()):
    m = a_list[0][0].shape[0]
    n_op = len(a_list)
    n_epi = len(epi_aux)

    def body(*refs):
        a_refs, b_refs = refs[:n_op], refs[n_op:2 * n_op]
        e_refs, o_ref = refs[2 * n_op:2 * n_op + n_epi], refs[2 * n_op + n_epi]
        acc = None
        for a_ref, b_ref in zip(a_refs, b_refs):
            part = _bdot(a_ref[...], b_ref[...], NT)
            acc = part if acc is None else acc + part
        if epi is not None:
            acc = epi(acc, *[r[...] for r in e_refs])
        o_ref[...] = acc.astype(out_dtype)

    in_specs = [pl.BlockSpec((tm, w), functools.partial(lambda i, j, cb: (i, cb), cb=cb)) for (_, w, cb) in a_list]
    for (b, w, cb) in b_list:
        if b.ndim == 3:
            in_specs.append(pl.BlockSpec((None, tn, w), functools.partial(lambda i, j, cb: (cb, j, 0), cb=cb)))
        else:
            in_specs.append(pl.BlockSpec((tn, w), functools.partial(lambda i, j, cb: (j, cb), cb=cb)))
    in_specs += [pl.BlockSpec((tm, tn), lambda i, j: (i, j)) for _ in epi_aux]
    return pl.pallas_call(
        body, name=name,
        grid=(m // tm, n // tn),
        in_specs=in_specs,
        out_specs=pl.BlockSpec((tm, tn), lambda i, j: (i, j)),
        out_shape=jax.ShapeDtypeStruct((m, n), out_dtype),
        compiler_params=_params(("parallel", "parallel")),
    )(*[a for (a, _, _) in a_list], *[b for (b, _, _) in b_list], *epi_aux)


def _mm_tn(name, a, b, *, tm, tn, ts, pro=None, aux=(), col_shards=False):
    s_all, ka = a.shape
    nb = b.shape[1]
    n_aux = len(aux)
    ns = s_all // ts
    assert not col_shards or tn == nb // N_DEV

    def body(a_ref, b_ref, *rest):
        aux_refs, o_ref, acc_ref = rest[:n_aux], rest[n_aux], rest[n_aux + 1]
        at = a_ref[...]
        if pro is not None:
            at = pro(at, *[r[...] for r in aux_refs])
        part = _bdot(at, b_ref[...], TN)
        ss = pl.program_id(2)

        @pl.when(ss == 0)
        def _():
            acc_ref[...] = part

        @pl.when(ss > 0)
        def _():
            acc_ref[...] += part

        @pl.when(ss == ns - 1)
        def _():
            o_ref[...] = acc_ref[...].astype(BF16)

    if col_shards:
        out_spec = pl.BlockSpec((None, tm, tn), lambda i, j, s: (j, i, 0))
        out_shape = jax.ShapeDtypeStruct((N_DEV, ka, tn), BF16)
    else:
        out_spec = pl.BlockSpec((tm, tn), lambda i, j, s: (i, j))
        out_shape = jax.ShapeDtypeStruct((ka, nb), BF16)
    return pl.pallas_call(
        body, name=name,
        grid=(ka // tm, nb // tn, ns),
        in_specs=[pl.BlockSpec((ts, tm), lambda i, j, s: (s, i)),
                  pl.BlockSpec((ts, tn), lambda i, j, s: (s, j))]
        + [pl.BlockSpec((1, tm), lambda i, j, s: (0, i)) for _ in aux],
        out_specs=out_spec, out_shape=out_shape,
        scratch_shapes=[pltpu.VMEM((tm, tn), F32)],
        compiler_params=_params(("parallel", "parallel", "arbitrary")),
    )(a, b, *aux)


def _rowk(name, fn, n_rows, tr, rows, fulls, outs, accs, reverse=False):
    n = n_rows // tr
    n_row, n_full, n_out, n_acc = len(rows), len(fulls), len(outs), len(accs)

    def pos(i):
        return (n - 1 - i) if reverse else i

    def body(*refs):
        row_refs = refs[:n_row]
        full_refs = refs[n_row:n_row + n_full]
        out_refs = refs[n_row + n_full:n_row + n_full + n_out]
        acc_refs = refs[n_row + n_full + n_out:]
        i = pl.program_id(0)

        @pl.when(i == 0)
        def _():
            for r in acc_refs:
                r[...] = jnp.zeros(r.shape, r.dtype)

        res = fn(pos(i), *[r[...] for r in row_refs], *[r[...] for r in full_refs], *[r[...] for r in acc_refs])
        for r, v in zip(out_refs + acc_refs, res):
            r[...] = v.astype(r.dtype)

    def row_map(i, cb, shift):
        return (jnp.clip(pos(i) + shift, 0, n - 1), cb)

    def halo_map(i, cb, shift):
        tile = jnp.clip(pos(i) + shift, 0, n - 1)
        return (tile * (tr // 8) + (tr // 8 - 1 if shift < 0 else 0), cb)

    in_specs = [pl.BlockSpec((tr, w), functools.partial(row_map, cb=cb, shift=sh)) if sh == 0 else
                pl.BlockSpec((8, w), functools.partial(halo_map, cb=cb, shift=sh)) for (_, w, cb, sh) in rows]
    in_specs += [pl.BlockSpec(f.shape, functools.partial(lambda i, nd: (0,) * nd, nd=f.ndim)) for f in fulls]
    out_specs = [pl.BlockSpec((tr, w), lambda i: (pos(i), 0)) for (w, _) in outs]
    out_specs += [pl.BlockSpec((r, w), lambda i: (0, 0)) for (r, w) in accs]
    out_shape = [jax.ShapeDtypeStruct((n_rows, w), dt) for (w, dt) in outs]
    out_shape += [jax.ShapeDtypeStruct((r, w), F32) for (r, w) in accs]
    return pl.pallas_call(
        body, name=name, grid=(n,), in_specs=in_specs, out_specs=out_specs, out_shape=out_shape,
        compiler_params=_params(("arbitrary",)),
    )(*[a for (a, _, _, _) in rows], *fulls)


def _colsum(x):
    return jnp.sum(x, axis=0, keepdims=True)


def _mean(x):
    return jnp.mean(x, axis=-1, keepdims=True)


def _modulate(x, sc, sh):
    return x * (1.0 + sc) + sh


def _shift_down(cur, prev8, j):
    tr = cur.shape[0]
    row8 = _iota(prev8.shape, 0)
    head = jnp.where(row8 < j, pltpu.roll(prev8, j, 0), pltpu.roll(cur[0:8], j, 0))
    return head if tr == 8 else jnp.concatenate([head, pltpu.roll(cur, j, 0)[8:]], axis=0)


def _shift_up(cur, next8, j):
    tr = cur.shape[0]
    row8 = _iota(next8.shape, 0)
    tail = jnp.where(row8 < 8 - j, pltpu.roll(cur[tr - 8:], 8 - j, 0), pltpu.roll(next8, 8 - j, 0))
    return jnp.concatenate([pltpu.roll(cur, tr - j, 0)[:tr - 8], tail], axis=0)


def _conv(cur, prev, w, b):
    out = cur * w[3:4] + b
    for j in (1, 2, 3):
        out = out + _shift_down(cur, prev, j) * w[3 - j:4 - j]
    return out


def _conv_fwd(p, w_xs, b_xs, w_bc, b_bc, s):
    def fn(pos, xs, xs_prev, bc, bc_prev, w_xs, b_xs, w_bc, b_bc):
        first = pos == 0
        xs_prev = jnp.where(first, 0.0, xs_prev)
        bc_prev = jnp.where(first, 0.0, bc_prev)
        return _silu(_conv(xs, xs_prev, w_xs, b_xs)), _silu(_conv(bc, bc_prev, w_bc, b_bc))

    return _rowk("conv_fwd", fn, s, 256,
                 [(p, D, OFF_XS // D, 0), (p, D, OFF_XS // D, -1), (p, 512, OFF_BC // 512, 0), (p, 512, OFF_BC // 512, -1)],
                 [w_xs, b_xs, w_bc, b_bc], [(D, F32), (512, F32)], [])


def _conv_bwd(dxs_a, dbc_a, p, w_xs, b_xs, w_bc, b_bc, s):
    tr = 256
    n = s // tr

    def fn(pos, da1, da1n, x1, x1p, x1n, da2, da2n, x2, x2p, x2n, w1, b1, w2, b2, aw1, ab1, aw2, ab2):
        dx1, dw1, db1 = _conv_bwd_fn(pos, n, da1, da1n, x1, x1p, x1n, w1, b1)
        dx2, dw2, db2 = _conv_bwd_fn(pos, n, da2, da2n, x2, x2p, x2n, w2, b2)
        return dx1, dx2, aw1 + dw1, ab1 + db1, aw2 + dw2, ab2 + db2

    cx, cb = OFF_XS // D, OFF_BC // 512
    return _rowk("conv_bwd", fn, s, tr,
                 [(dxs_a, D, 0, 0), (dxs_a, D, 0, 1), (p, D, cx, 0), (p, D, cx, -1), (p, D, cx, 1),
                  (dbc_a, 512, 0, 0), (dbc_a, 512, 0, 1), (p, 512, cb, 0), (p, 512, cb, -1), (p, 512, cb, 1)],
                 [w_xs, b_xs, w_bc, b_bc], [(D, BF16), (512, BF16)], [(8, D), (1, D), (8, 512), (1, 512)])


def _conv_bwd_fn(pos, n, da, da_next, x, x_prev, x_next, w, b):
    first, last = pos == 0, pos == n - 1
    x_prev = jnp.where(first, 0.0, x_prev)
    dc = da * _dsilu(_conv(x, x_prev, w, b))
    dc_next = jnp.where(last, 0.0, da_next * _dsilu(_conv(x_next, x[x.shape[0] - 8:], w, b)))
    dx = dc * w[3:4]
    dws = [None] * 4
    dws[3] = _colsum(dc * x)
    for j in (1, 2, 3):
        dx = dx + _shift_up(dc, dc_next, j) * w[3 - j:4 - j]
        dws[3 - j] = _colsum(dc * _shift_down(x, x_prev, j))
    row = _iota((8, x.shape[1]), 0)
    dw = jnp.zeros((8, x.shape[1]), F32)
    for k in range(4):
        dw = jnp.where(row == k, dws[k], dw)
    return dx, dw, _colsum(dc)


def _ssd_gates(dtf, bias, a_log):
    lane = _iota(dtf.shape, 1)
    head = lane < NH
    dt = jnp.where(head, _softplus(dtf + bias), 0.0)
    a_neg = jnp.where(_iota(a_log.shape, 1) < NH, -jnp.exp(a_log), 0.0)
    a = dt * a_neg
    cs = _sel_left(_tri_lower(CHUNK), a)
    return dt, a_neg, cs


def _decay_mask(cs_ref, cst_ref, h):
    diff = cs_ref[:, h:h + 1] - cst_ref[h:h + 1, :]
    low = _iota((CHUNK, CHUNK), 1) <= _iota((CHUNK, CHUNK), 0)
    return jnp.where(low, jnp.exp(jnp.minimum(diff, 0.0)), 0.0)


def _ssd_fwd(xs_a, bc_a, p, bias128, alog128, dskip_x, s):
    nc = s // CHUNK
    t = CHUNK

    def body(xs_ref, bc_ref, dtf_ref, bias_ref, alog_ref, dsk_ref, y_ref, st_ref,
             state, x_sc, xw_sc, cs_sc, cst_sc, yd_sc):
        c = pl.program_id(0)

        @pl.when(c == 0)
        def _():
            state[...] = jnp.zeros(state.shape, F32)

        dt, _, cs = _ssd_gates(dtf_ref[...], bias_ref[...], alog_ref[...])
        cs_sc[...] = cs
        cst_sc[...] = cs.T
        cs_last = cs[t - 1:t, :]
        expand = _head_expand()
        ex = _sel_right(jnp.concatenate([dt, jnp.exp(cs), jnp.exp(cs_last - cs)], axis=0), expand)
        dt_x, eo_x, we_x = ex[0:t], ex[t:2 * t], ex[2 * t:3 * t]
        g_x = _sel_right(jnp.broadcast_to(jnp.exp(cs_last), (8, 128)), expand)[0:1]
        xs = xs_ref[...]
        x = xs * dt_x
        x_sc[...] = x.astype(BF16)
        xw_sc[...] = (x * we_x).astype(BF16)
        prev = state[...]
        st_ref[0] = prev
        prev_b = prev.astype(BF16)
        for g in range(2):
            cols = slice(g * 512, (g + 1) * 512)
            b_g = bc_ref[:, g * 128:(g + 1) * 128].astype(BF16)
            c_g = bc_ref[:, 256 + g * 128:256 + (g + 1) * 128].astype(BF16)
            gmat = _dot(c_g, b_g, NT)
            y_off = _dot(c_g, prev_b[:, cols]) * eo_x[:, cols]
            s_loc = _dot(b_g, xw_sc[:, cols], TN)
            state[:, cols] = g_x[:, cols] * prev[:, cols] + s_loc
            for e in range(HG):
                h = g * HG + e
                m = gmat * _decay_mask(cs_sc, cst_sc, h)
                yd_sc[:, h * HD:(h + 1) * HD] = _dot(m.astype(BF16), x_sc[:, h * HD:(h + 1) * HD])
            y_ref[:, cols] = yd_sc[:, cols] + y_off + dsk_ref[:, cols] * xs[:, cols]

    return pl.pallas_call(
        body, name="ssd_fwd", grid=(nc,),
        in_specs=[pl.BlockSpec((t, D), lambda c: (c, 0)),
                  pl.BlockSpec((t, 512), lambda c: (c, 0)),
                  pl.BlockSpec((t, 128), lambda c: (c, OFF_DTF // 128)),
                  pl.BlockSpec((1, 128), lambda c: (0, 0)),
                  pl.BlockSpec((1, 128), lambda c: (0, 0)),
                  pl.BlockSpec((1, D), lambda c: (0, 0))],
        out_specs=[pl.BlockSpec((t, D), lambda c: (c, 0)),
                   pl.BlockSpec((1, NSTATE, D), lambda c: (c, 0, 0))],
        out_shape=[jax.ShapeDtypeStruct((s, D), F32), jax.ShapeDtypeStruct((nc, NSTATE, D), F32)],
        scratch_shapes=[pltpu.VMEM((NSTATE, D), F32), pltpu.VMEM((t, D), BF16), pltpu.VMEM((t, D), BF16),
                        pltpu.VMEM((t, 128), F32), pltpu.VMEM((128, t), F32), pltpu.VMEM((t, D), F32)],
        compiler_params=_params(("arbitrary",)),
    )(xs_a, bc_a, p, bias128, alog128, dskip_x)


def _ssd_bwd(dy, xs_a, bc_a, p, states, bias128, alog128, dskip_x, s):
    nc = s // CHUNK
    t = CHUNK

    def body(dy_ref, xs_ref, bc_ref, dtf_ref, st_ref, bias_ref, alog_ref, dsk_ref,
             dxs_ref, dbc_ref, ddt_ref, dalog_ref, dskip_ref,
             dstate, x_sc, dy_sc, dx_sc, deo_sc, dwe_sc, cs_sc, cst_sc, dcol_sc, drow_sc):
        i = pl.program_id(0)

        @pl.when(i == 0)
        def _():
            dstate[...] = jnp.zeros(dstate.shape, F32)
            dalog_ref[...] = jnp.zeros(dalog_ref.shape, F32)
            dskip_ref[...] = jnp.zeros(dskip_ref.shape, F32)

        dtf = dtf_ref[...]
        dt, a_neg, cs = _ssd_gates(dtf, bias_ref[...], alog_ref[...])
        cs_sc[...] = cs
        cst_sc[...] = cs.T
        cs_last = cs[t - 1:t, :]
        eo, we, g_end = jnp.exp(cs), jnp.exp(cs_last - cs), jnp.exp(cs_last)
        expand, reduce = _head_expand(), _head_reduce()
        ex = _sel_right(jnp.concatenate([dt, eo, we], axis=0), expand)
        dt_x, eo_x, we_x = ex[0:t], ex[t:2 * t], ex[2 * t:3 * t]
        g_x = _sel_right(jnp.broadcast_to(g_end, (8, 128)), expand)[0:1]
        xs = xs_ref[...]
        dyv = dy_ref[...]
        x = xs * dt_x
        x_sc[...] = x.astype(BF16)
        dy_sc[...] = dyv.astype(BF16)
        dyo_b = (dyv * eo_x).astype(BF16)
        xw_b = (x * we_x).astype(BF16)
        prev = st_ref[0]
        prev_b = prev.astype(BF16)
        dnext = dstate[...]
        dnext_b = dnext.astype(BF16)
        dcol_sc[...] = jnp.zeros(dcol_sc.shape, F32)
        drow_sc[...] = jnp.zeros(drow_sc.shape, F32)
        lane_row = _iota((1, 128), 1)
        sub_col = _iota((128, 1), 0)
        for g in range(2):
            cols = slice(g * 512, (g + 1) * 512)
            b_g = bc_ref[:, g * 128:(g + 1) * 128].astype(BF16)
            c_g = bc_ref[:, 256 + g * 128:256 + (g + 1) * 128].astype(BF16)
            gmat = _dot(c_g, b_g, NT)
            b_ds = _dot(b_g, dnext_b[:, cols])
            c_s = _dot(c_g, prev_b[:, cols])
            dx_sc[:, cols] = b_ds * we_x[:, cols]
            deo_sc[:, cols] = dyv[:, cols] * c_s
            dwe_sc[:, cols] = b_ds * x[:, cols]
            db = _dot(xw_b[:, cols], dnext_b[:, cols], NT)
            dc = _dot(dyo_b[:, cols], prev_b[:, cols], NT)
            dstate[:, cols] = g_x[:, cols] * dnext[:, cols] + _dot(c_g, dyo_b[:, cols], TN)
            dg = jnp.zeros((t, t), F32)
            for e in range(HG):
                h = g * HG + e
                hc = slice(h * HD, (h + 1) * HD)
                lmat = _decay_mask(cs_sc, cst_sc, h)
                m = gmat * lmat
                dx_sc[:, hc] += _dot(m.astype(BF16), dy_sc[:, hc], TN)
                dm = _dot(dy_sc[:, hc], x_sc[:, hc], NT)
                dg = dg + dm * lmat
                qm = dm * m
                dcol_sc[...] += jnp.sum(qm, axis=1, keepdims=True) * (lane_row == h).astype(F32)
                drow_sc[...] += (sub_col == h).astype(F32) * jnp.sum(qm, axis=0, keepdims=True)
            dg_b = dg.astype(BF16)
            dbc_ref[:, g * 128:(g + 1) * 128] = db + _dot(dg_b, c_g, TN)
            dbc_ref[:, 256 + g * 128:256 + (g + 1) * 128] = dc + _dot(dg_b, b_g)
        d_eo = _sel_right(deo_sc[...], reduce)
        d_we = _sel_right(dwe_sc[...], reduce)
        d_gend = _sel_right(jnp.broadcast_to(_colsum(dnext * prev), (8, D)), reduce)[0:1]
        d_cs = dcol_sc[...] - drow_sc[...].T + d_eo * eo - d_we * we
        extra = _colsum(d_we * we) + d_gend * g_end
        d_cs = d_cs + jnp.where(_iota((t, 128), 0) == t - 1, extra, 0.0)
        da = _sel_left(_tri_upper(t), d_cs)
        dx = dx_sc[...]
        ddt = _sel_right(dx * xs, reduce) + da * a_neg
        dxs_ref[...] = dx * dt_x + dsk_ref[...] * dyv
        ddt_ref[...] = jnp.where(_iota((t, 128), 1) < NH, ddt * _sigmoid(dtf + bias_ref[...]), 0.0)
        dalog_ref[...] += _colsum(da * dt) * a_neg
        dskip_ref[...] += _sel_right(jnp.broadcast_to(_colsum(dyv * xs), (8, D)), reduce)[0:1]

    rev = lambda i: nc - 1 - i
    return pl.pallas_call(
        body, name="ssd_bwd", grid=(nc,),
        in_specs=[pl.BlockSpec((t, D), lambda i: (rev(i), 0)),
                  pl.BlockSpec((t, D), lambda i: (rev(i), 0)),
                  pl.BlockSpec((t, 512), lambda i: (rev(i), 0)),
                  pl.BlockSpec((t, 128), lambda i: (rev(i), OFF_DTF // 128)),
                  pl.BlockSpec((1, NSTATE, D), lambda i: (rev(i), 0, 0)),
                  pl.BlockSpec((1, 128), lambda i: (0, 0)),
                  pl.BlockSpec((1, 128), lambda i: (0, 0)),
                  pl.BlockSpec((1, D), lambda i: (0, 0))],
        out_specs=[pl.BlockSpec((t, D), lambda i: (rev(i), 0)),
                   pl.BlockSpec((t, 512), lambda i: (rev(i), 0)),
                   pl.BlockSpec((t, 128), lambda i: (rev(i), 0)),
                   pl.BlockSpec((1, 128), lambda i: (0, 0)),
                   pl.BlockSpec((1, 128), lambda i: (0, 0))],
        out_shape=[jax.ShapeDtypeStruct((s, D), F32), jax.ShapeDtypeStruct((s, 512), F32),
                   jax.ShapeDtypeStruct((s, 128), F32), jax.ShapeDtypeStruct((1, 128), F32),
                   jax.ShapeDtypeStruct((1, 128), F32)],
        scratch_shapes=[pltpu.VMEM((NSTATE, D), F32), pltpu.VMEM((t, D), BF16), pltpu.VMEM((t, D), BF16),
                        pltpu.VMEM((t, D), F32), pltpu.VMEM((t, D), F32), pltpu.VMEM((t, D), F32),
                        pltpu.VMEM((t, 128), F32), pltpu.VMEM((128, t), F32),
                        pltpu.VMEM((t, 128), F32), pltpu.VMEM((128, t), F32)],
        compiler_params=_params(("arbitrary",)),
    )(dy, xs_a, bc_a, p, states, bias128, alog128, dskip_x)


def _gate_lanes(shape):
    lane = _iota(shape, 1)
    return (lane >= NH) & (lane < 2 * NH)


def _cum_fwd(p, bias128, s):
    tr = min(512, s)

    def body(dtf_ref, bias_ref, o_ref, carry):
        @pl.when(pl.program_id(0) == 0)
        def _():
            carry[...] = jnp.zeros(carry.shape, F32)

        lf = jnp.where(_gate_lanes((tr, 128)), _log_sigmoid(dtf_ref[...] + bias_ref[...]), 0.0)
        cum = _sel_left(_tri_lower(tr), lf) + carry[...]
        carry[...] = cum[tr - 1:tr, :]
        o_ref[...] = cum

    return pl.pallas_call(
        body, name="cum_fwd", grid=(s // tr,),
        in_specs=[pl.BlockSpec((tr, 128), lambda i: (i, OFF_DTF // 128)), pl.BlockSpec((1, 128), lambda i: (0, 0))],
        out_specs=pl.BlockSpec((tr, 128), lambda i: (i, 0)),
        out_shape=jax.ShapeDtypeStruct((s, 128), F32),
        scratch_shapes=[pltpu.VMEM((1, 128), F32)],
        compiler_params=_params(("arbitrary",)),
    )(p, bias128)


def _cum_bwd(dr_col, dcs, ddt_raw, p, bias128, s):
    tr = min(512, s)

    def fn(pos, dr, dc, ddt, dtf, bias, carry, acc):
        pick = (_iota((D, 128), 0) == (_iota((D, 128), 1) - NH) * HD).astype(BF16)
        dcum = dr - _sel_right(dc, pick)
        suffix = _sel_left(_tri_upper(tr), dcum) + carry
        dfr = jnp.where(_gate_lanes((tr, 128)), suffix * _sigmoid(-(dtf + bias)), 0.0)
        out = ddt + dfr
        return out, suffix[0:1, :], acc + _colsum(out)

    return _rowk("cum_bwd", fn, s, tr,
                 [(dr_col, 128, 0, 0), (dcs, D, 0, 0), (ddt_raw, 128, 0, 0), (p, 128, OFF_DTF // 128, 0)],
                 [bias128], [(128, BF16)], [(1, 128), (1, 128)], reverse=True)


ATT_BLOCK = 512
ATT_STRIP = 32


def _head_part(shape, h, dim):
    i = _iota(shape, dim)
    return (i >= h * HD) & (i < (h + 1) * HD)


def _k_augmented(k_blk, cum_blk, j, h):
    tk = k_blk.shape[0]
    lane = _iota((tk, 128), 1)
    col = jnp.sum(jnp.where(lane == NH + 2 * j + h, cum_blk, 0.0), axis=1, keepdims=True)
    c0, c1, c2 = [c.astype(F32) for c in _split3(-col)]
    aug = jnp.where(lane == 0, c0, jnp.where(lane == 1, c1, jnp.where(lane == 2, c2, 0.0)))
    return jnp.concatenate([jnp.where(_head_part((tk, 128), h, 1), k_blk, 0.0), aug], axis=1).astype(BF16)


def _q_augmented_t(q_blk):
    tq = q_blk.shape[0]
    ones = (_iota((128, tq), 0) < 3).astype(BF16)
    return jnp.concatenate([(q_blk * ATT_SCALE).T.astype(BF16), ones], axis=0)


def _rows01(r0, r1):
    sub = _iota((8, r0.shape[1]), 0)
    return jnp.where(sub == 0, r0, jnp.where(sub == 1, r1, 0.0))


def _fold8(x, op, cur):
    for g in range(x.shape[0] // 8):
        cur = op(cur, x[8 * g:8 * (g + 1), :])
    return cur


def _attn_fwd(p, cum, s):
    t = min(ATT_BLOCK, s)
    nq = s // t
    r = ATT_STRIP

    def body(q_ref, k_ref, v_ref, c_ref, o_ref, lse_ref, kaug_sc, vt_sc, s_sc, p_sc, m_sc, l_sc, acc_sc):
        j, qi = pl.program_id(0), pl.program_id(1)

        @pl.when(qi == 0)
        def _():
            for c in range(nq):
                rows = slice(c * t, (c + 1) * t)
                k_blk, vt = k_ref[rows, :], v_ref[rows, :].T
                for h in range(2):
                    kaug_sc[h, rows, :] = _k_augmented(k_blk, c_ref[rows, :], j, h)
                    vt_sc[h, :, rows] = jnp.where(_head_part((128, t), h, 0), vt, 0.0).astype(BF16)

        qaug_t = _q_augmented_t(q_ref[...])
        m_sc[...] = jnp.full(m_sc.shape, -1e30, F32)
        l_sc[...] = jnp.zeros(l_sc.shape, F32)
        acc_sc[...] = jnp.zeros(acc_sc.shape, F32)
        top = _iota((128, t), 0) < HD

        def logits(kb, buf):
            kv = pl.ds(pl.multiple_of(kb * t, t), t)
            for h in range(2):
                s_sc[buf, h] = _dot(kaug_sc[h, kv, :], qaug_t)

        def softmax(buf, diagonal):
            alphas = []
            for h in range(2):
                cur = jnp.full((8, t), -1e30, F32)
                for i in range(t // r):
                    rows = slice(i * r, (i + 1) * r)
                    x = s_sc[buf, h, rows, :]
                    if diagonal:
                        x = jnp.where(_iota((r, t), 1) >= i * r + _iota((r, t), 0), x, -1e30)
                        s_sc[buf, h, rows, :] = x
                    cur = _fold8(x, jnp.maximum, cur)
                m_prev = m_sc[h, 0:1, :]
                m_new = jnp.maximum(m_prev, jnp.max(cur, axis=0, keepdims=True))
                alpha = jnp.exp(m_prev - m_new)
                m_sc[h, 0:1, :] = m_new
                alphas.append(alpha)
                tot = jnp.zeros((8, t), F32)
                for i in range(t // r):
                    rows = slice(i * r, (i + 1) * r)
                    pr = jnp.exp(s_sc[buf, h, rows, :] - m_new)
                    p_sc[buf, h, rows, :] = pr.astype(BF16)
                    tot = _fold8(pr, jnp.add, tot)
                l_sc[h, 0:1, :] = alpha * l_sc[h, 0:1, :] + jnp.sum(tot, axis=0, keepdims=True)
            return alphas

        def accumulate(kb, buf, alphas):
            kv = pl.ds(pl.multiple_of(kb * t, t), t)
            acc_sc[...] = (acc_sc[...] * jnp.where(top, alphas[0], alphas[1])
                           + _dot(vt_sc[0, :, kv], p_sc[buf, 0]) + _dot(vt_sc[1, :, kv], p_sc[buf, 1]))

        def pair(a, b, b_diagonal):
            logits(a, 0)
            logits(b, 1)
            accumulate(a, 0, softmax(0, False))
            accumulate(b, 1, softmax(1, b_diagonal))

        def earlier(u, carry):
            pair(2 * u, 2 * u + 1, False)
            return carry

        lax.fori_loop(0, qi // 2, earlier, 0)

        @pl.when(qi % 2 == 1)
        def _():
            pair(qi - 1, qi, True)

        @pl.when(qi % 2 == 0)
        def _():
            logits(qi, 0)
            accumulate(qi, 0, softmax(0, True))

        l0, l1 = l_sc[0, 0:1, :], l_sc[1, 0:1, :]
        o_ref[...] = (acc_sc[...] / jnp.where(top, l0, l1)).T
        lse_ref[0] = _rows01(m_sc[0, 0:1, :] + jnp.log(l0), m_sc[1, 0:1, :] + jnp.log(l1))

    return pl.pallas_call(
        body, name="attn_fwd", grid=(NH // 2, nq),
        in_specs=[pl.BlockSpec((t, 128), lambda j, qi: (qi, OFF_Q // 128 + j)),
                  pl.BlockSpec((s, 128), lambda j, qi: (0, OFF_K // 128 + j)),
                  pl.BlockSpec((s, 128), lambda j, qi: (0, OFF_V // 128 + j)),
                  pl.BlockSpec((s, 128), lambda j, qi: (0, 0))],
        out_specs=[pl.BlockSpec((t, 128), lambda j, qi: (qi, j)),
                   pl.BlockSpec((1, 8, t), lambda j, qi: (j, 0, qi))],
        out_shape=[jax.ShapeDtypeStruct((s, D), F32), jax.ShapeDtypeStruct((NH // 2, 8, s), F32)],
        scratch_shapes=[pltpu.VMEM((2, s, 256), BF16), pltpu.VMEM((2, 128, s), BF16), pltpu.VMEM((2, 2, t, t), F32),
                        pltpu.VMEM((2, 2, t, t), BF16), pltpu.VMEM((2, 8, t), F32), pltpu.VMEM((2, 8, t), F32),
                        pltpu.VMEM((128, t), F32)],
        compiler_params=_params(("parallel", "arbitrary")),
    )(p, p, p, cum)


def _attn_bwd(p, cum, o, lse, do, s):
    t = min(ATT_BLOCK, s)
    nq = s // t
    r = ATT_STRIP

    def body(q_ref, k_ref, v_ref, c_ref, o_ref, lse_ref, do_ref, dq_ref, dk_ref, dv_ref, dc_ref, dr_ref,
             qaugt_sc, qh_sc, dot_sc, doh_sc, delta_sc, dqt_sc, dr_sc, kaug_sc, vh_sc, kt_sc, s_sc, dp_sc, p_sc, ds_sc,
             dk_sc, dv_sc, dc_sc):
        j, ki = pl.program_id(0), pl.program_id(1)

        @pl.when(ki == 0)
        def _():
            for c in range(nq):
                rows = slice(c * t, (c + 1) * t)
                q_blk, do_blk = q_ref[rows, :], do_ref[rows, :]
                qaugt_sc[:, rows] = _q_augmented_t(q_blk)
                dot_sc[:, rows] = do_blk.T.astype(BF16)
                prod_t = (do_blk * o_ref[rows, :]).T
                delta_sc[:, rows] = _rows01(jnp.sum(prod_t[0:HD], axis=0, keepdims=True),
                                            jnp.sum(prod_t[HD:], axis=0, keepdims=True))
                for h in range(2):
                    head = _head_part((t, 128), h, 1)
                    qh_sc[h, rows, :] = jnp.where(head, q_blk * ATT_SCALE, 0.0).astype(BF16)
                    doh_sc[h, rows, :] = jnp.where(head, do_blk, 0.0).astype(BF16)
            dqt_sc[...] = jnp.zeros(dqt_sc.shape, F32)
            dr_sc[...] = jnp.zeros(dr_sc.shape, F32)

        k_blk, v_blk = k_ref[...], v_ref[...]
        kt = k_blk.T
        for h in range(2):
            kaug_sc[h] = _k_augmented(k_blk, c_ref[...], j, h)
            vh_sc[h] = jnp.where(_head_part((t, 128), h, 1), v_blk, 0.0).astype(BF16)
            kt_sc[h] = jnp.where(_head_part((128, t), h, 0), kt, 0.0).astype(BF16)
        dk_sc[...] = jnp.zeros(dk_sc.shape, F32)
        dv_sc[...] = jnp.zeros(dv_sc.shape, F32)
        dc_sc[...] = jnp.zeros(dc_sc.shape, F32)

        def inputs(qb, buf):
            qs = pl.ds(pl.multiple_of(qb * t, t), t)
            for h in range(2):
                s_sc[buf, h] = _dot(kaug_sc[h], qaugt_sc[:, qs])
                dp_sc[buf, h] = _dot(vh_sc[h], dot_sc[:, qs])

        def elementwise(qb, buf, diagonal):
            qs = pl.ds(pl.multiple_of(qb * t, t), t)
            for h in range(2):
                lse_row, delta_row = lse_ref[0, h:h + 1, qs], delta_sc[h:h + 1, qs]
                tot = jnp.zeros((8, t), F32)
                for i in range(t // r):
                    rows = slice(i * r, (i + 1) * r)
                    x = s_sc[buf, h, rows, :]
                    if diagonal:
                        x = jnp.where(_iota((r, t), 1) >= i * r + _iota((r, t), 0), x, -1e30)
                    pr = jnp.exp(x - lse_row)
                    ds = pr * (dp_sc[buf, h, rows, :] - delta_row)
                    p_sc[buf, h, rows, :] = pr.astype(BF16)
                    ds_sc[buf, h, rows, :] = ds.astype(BF16)
                    dc_sc[h, rows, :] += sum(ds[:, 128 * g:128 * (g + 1)] for g in range(t // 128))
                    tot = _fold8(ds, jnp.add, tot)
                dr_sc[h, :, qs] += tot

        def outputs(qb, buf):
            qs = pl.ds(pl.multiple_of(qb * t, t), t)
            dv_sc[...] += _dot(p_sc[buf, 0], doh_sc[0, qs, :]) + _dot(p_sc[buf, 1], doh_sc[1, qs, :])
            dk_sc[...] += _dot(ds_sc[buf, 0], qh_sc[0, qs, :]) + _dot(ds_sc[buf, 1], qh_sc[1, qs, :])
            dqt_sc[:, qs] += _dot(kt_sc[0], ds_sc[buf, 0]) + _dot(kt_sc[1], ds_sc[buf, 1])

        def pair(a, b, a_diagonal):
            inputs(a, 0)
            inputs(b, 1)
            elementwise(a, 0, a_diagonal)
            outputs(a, 0)
            elementwise(b, 1, False)
            outputs(b, 1)

        def later(u, carry):
            pair(ki + 1 + 2 * u, ki + 2 + 2 * u, False)
            return carry

        n_later = nq - 1 - ki
        lax.fori_loop(0, n_later // 2, later, 0)

        @pl.when(n_later % 2 == 1)
        def _():
            pair(ki, nq - 1, True)

        @pl.when(n_later % 2 == 0)
        def _():
            inputs(ki, 0)
            elementwise(ki, 0, True)
            outputs(ki, 0)

        dk_ref[...] = dk_sc[...].astype(BF16)
        dv_ref[...] = dv_sc[...].astype(BF16)
        dc_ref[...] = jnp.where(_iota((t, 128), 1) < HD, jnp.sum(dc_sc[0], axis=1, keepdims=True),
                                jnp.sum(dc_sc[1], axis=1, keepdims=True))

        @pl.when(ki == nq - 1)
        def _():
            for c in range(nq):
                rows = slice(c * t, (c + 1) * t)
                dq_ref[rows, :] = dqt_sc[:, rows].T * ATT_SCALE
            dr_ref[0] = _rows01(jnp.sum(dr_sc[0], axis=0, keepdims=True), jnp.sum(dr_sc[1], axis=0, keepdims=True))

    whole = lambda off: pl.BlockSpec((s, 128), functools.partial(lambda j, ki, off: (0, off + j), off=off))
    return pl.pallas_call(
        body, name="attn_bwd", grid=(NH // 2, nq),
        in_specs=[whole(OFF_Q // 128),
                  pl.BlockSpec((t, 128), lambda j, ki: (ki, OFF_K // 128 + j)),
                  pl.BlockSpec((t, 128), lambda j, ki: (ki, OFF_V // 128 + j)),
                  pl.BlockSpec((t, 128), lambda j, ki: (ki, 0)),
                  whole(0),
                  pl.BlockSpec((1, 8, s), lambda j, ki: (j, 0, 0)),
                  whole(0)],
        out_specs=[whole(0),
                   pl.BlockSpec((t, 128), lambda j, ki: (ki, j)),
                   pl.BlockSpec((t, 128), lambda j, ki: (ki, j)),
                   pl.BlockSpec((t, 128), lambda j, ki: (ki, j)),
                   pl.BlockSpec((1, 8, s), lambda j, ki: (j, 0, 0))],
        out_shape=[jax.ShapeDtypeStruct((s, D), F32), jax.ShapeDtypeStruct((s, D), BF16), jax.ShapeDtypeStruct((s, D), BF16),
                   jax.ShapeDtypeStruct((s, D), F32), jax.ShapeDtypeStruct((NH // 2, 8, s), F32)],
        scratch_shapes=[pltpu.VMEM((256, s), BF16), pltpu.VMEM((2, s, 128), BF16), pltpu.VMEM((128, s), BF16),
                        pltpu.VMEM((2, s, 128), BF16), pltpu.VMEM((8, s), F32), pltpu.VMEM((128, s), F32),
                        pltpu.VMEM((2, 8, s), F32), pltpu.VMEM((2, t, 256), BF16), pltpu.VMEM((2, t, 128), BF16),
                        pltpu.VMEM((2, 128, t), BF16), pltpu.VMEM((2, 2, t, t), F32), pltpu.VMEM((2, 2, t, t), F32),
                        pltpu.VMEM((2, 2, t, t), BF16), pltpu.VMEM((2, 2, t, t), BF16), pltpu.VMEM((t, 128), F32),
                        pltpu.VMEM((t, 128), F32), pltpu.VMEM((2, t, 128), F32)],
        compiler_params=_params(("parallel", "arbitrary")),
    )(p, p, p, cum, o, lse, do)


def _ln_stats(u):
    mu = _mean(u)
    d = u - mu
    rstd = lax.rsqrt(_mean(d * d) + EPS)
    return d * rstd, rstd


def _ln_bwd(dx, xh, rstd, gam):
    dxh = dx * gam
    return rstd * (dxh - _mean(dxh) - xh * _mean(dxh * xh))


def _rms_bwd(d, xn, r, w):
    t = d * w
    return r * (t - xn * _mean(t * xn)), _colsum(d * xn)


def _mix_norm(y, p, att, w_ssm, w_att, s):
    def fn(pos, y, z, att, w1, w2):
        g = y * _silu(z)
        n1 = g * lax.rsqrt(_mean(g * g) + EPS) * w1
        n2 = att * lax.rsqrt(_mean(att * att) + EPS) * w2
        return (jnp.concatenate([n1, n2], axis=1),)

    return _rowk("mix_norm", fn, s, 256, [(y, D, 0, 0), (p, D, OFF_Z // D, 0), (att, D, 0, 0)],
                 [w_ssm, w_att], [(2 * D, BF16)], [])[0]


def _mix_norm_bwd(dmix, y, p, att, w_ssm, w_att, s):
    def fn(pos, dmix, y, z, att, w1, w2, a1, a2):
        sz = _silu(z)
        g = y * sz
        r1 = lax.rsqrt(_mean(g * g) + EPS)
        dg, dw1 = _rms_bwd(dmix[:, :D], g * r1, r1, w1)
        r2 = lax.rsqrt(_mean(att * att) + EPS)
        datt, dw2 = _rms_bwd(dmix[:, D:], att * r2, r2, w2)
        return dg * sz, dg * y * _dsilu(z), datt, a1 + dw1, a2 + dw2

    return _rowk("mix_norm_bwd", fn, s, 256, [(dmix, 2 * D, 0, 0), (y, D, 0, 0), (p, D, OFF_Z // D, 0), (att, D, 0, 0)],
                 [w_ssm, w_att], [(D, F32), (D, BF16), (D, F32)], [(1, D), (1, D)])


def _ln1(x0, y, g1, gam, bet, sc2, sh2, s):
    def fn(pos, x0, y, g1, gam, bet, sc2, sh2):
        xh, _ = _ln_stats(ALPHA * x0 + (1.0 + g1) * y)
        x1 = xh * gam + bet
        return x1, _modulate(x1, sc2, sh2)

    return _rowk("ln1", fn, s, 256, [(x0, D, 0, 0), (y, D, 0, 0)], [g1, gam, bet, sc2, sh2], [(D, F32), (D, BF16)], [])


def _ln2_loss(x1, ff, tgt, g2, gam, bet, s):
    def fn(pos, x1, ff, tgt, g2, gam, bet, a_loss, a_dgam, a_dbet, a_dg2):
        xh, rstd = _ln_stats(ALPHA * x1 + (1.0 + g2) * ff)
        err = xh * gam + bet - tgt
        dx2 = err * (1.0 / D)
        du = _ln_bwd(dx2, xh, rstd, gam)
        return (du, du * (1.0 + g2), a_loss + _colsum(err * err), a_dgam + _colsum(dx2 * xh),
                a_dbet + _colsum(dx2), a_dg2 + _colsum(du * ff))

    return _rowk("ln2_loss", fn, s, 256, [(x1, D, 0, 0), (ff, D, 0, 0), (tgt, D, 0, 0)], [g2, gam, bet],
                 [(D, F32), (D, BF16)], [(1, D)] * 4)


def _ln1_bwd(dh2, du2, x0, y, g1, gam, bet, sc2, s):
    def fn(pos, dh2, du2, x0, y, g1, gam, bet, sc2, a_sc, a_sh, a_gam, a_bet, a_g1):
        xh, rstd = _ln_stats(ALPHA * x0 + (1.0 + g1) * y)
        x1 = xh * gam + bet
        dx1 = ALPHA * du2 + dh2 * (1.0 + sc2)
        du1 = _ln_bwd(dx1, xh, rstd, gam)
        return (du1, du1 * (1.0 + g1), a_sc + _colsum(dh2 * x1), a_sh + _colsum(dh2), a_gam + _colsum(dx1 * xh),
                a_bet + _colsum(dx1), a_g1 + _colsum(du1 * y))

    return _rowk("ln1_bwd", fn, s, 256, [(dh2, D, 0, 0), (du2, D, 0, 0), (x0, D, 0, 0), (y, D, 0, 0)],
                 [g1, gam, bet, sc2], [(D, F32), (D, BF16)], [(1, D)] * 5)


def _input_grad(dh1, du1, x0, sc1, s):
    def fn(pos, dh1, du1, x0, sc1, a_sc, a_sh):
        return ALPHA * du1 + dh1 * (1.0 + sc1), a_sc + _colsum(dh1 * x0), a_sh + _colsum(dh1)

    return _rowk("input_grad", fn, s, 256, [(dh1, D, 0, 0), (du1, D, 0, 0), (x0, D, 0, 0)], [sc1],
                 [(D, F32)], [(1, D)] * 2)


def _adamw(name, w, g, m, v, *, tr, slots):
    r, c = w.shape

    def body(w_ref, g_ref, m_ref, v_ref, g_out, d_out, m_out, v_out):
        if slots:
            grad = g_ref[0][:, :c].astype(F32)
            for k in range(1, N_DEV):
                grad = grad + g_ref[k][:, :c].astype(F32)
        else:
            grad = g_ref[...]
        m_new = ADAM_B1 * m_ref[...] + (1.0 - ADAM_B1) * grad
        v_new = ADAM_B2 * v_ref[...] + (1.0 - ADAM_B2) * (grad * grad)
        m_hat = m_new / (1.0 - ADAM_B1 ** ADAM_STEP)
        v_hat = v_new / (1.0 - ADAM_B2 ** ADAM_STEP)
        g_out[...] = grad
        d_out[...] = -ADAM_LR * (m_hat / (jnp.sqrt(v_hat) + ADAM_EPS) + ADAM_WD * w_ref[...])
        m_out[...] = m_new
        v_out[...] = v_new

    tile = pl.BlockSpec((tr, c), lambda i: (i, 0))
    g_spec = pl.BlockSpec((N_DEV, tr, g.shape[-1]), lambda i: (0, i, 0)) if slots else tile
    return pl.pallas_call(
        body, name=name, grid=(r // tr,),
        in_specs=[tile, g_spec, tile, tile], out_specs=[tile] * 4,
        out_shape=[jax.ShapeDtypeStruct((r, c), F32)] * 4,
        compiler_params=_params(("parallel",)),
    )(w, g, m, v)


def _dot_f32(a, b, dims=NN):
    a0, a1, a2 = _split3(a)
    b0, b1, b2 = _split3(b)
    acc = _dot(a0, b0, dims)
    for x, y in ((a0, b1), (a1, b0), (a1, b1), (a0, b2), (a2, b0)):
        acc = acc + _dot(x, y, dims)
    return acc


def _ada_mod(c_all, w_shard, b_shard):
    def body(c_ref, w_ref, b_ref, o_ref):
        act = _silu(c_ref[...])
        act16 = jnp.concatenate([act, jnp.zeros_like(act)], axis=0)
        o_ref[...] = _dot_f32(act16, w_ref[...])[0:N_DEV] + b_ref[...]

    return pl.pallas_call(
        body, name="ada_mod", out_shape=jax.ShapeDtypeStruct((N_DEV, w_shard.shape[1]), F32),
        compiler_params=_params(None),
    )(c_all, w_shard, b_shard)


def _ada_grad(c_all, dmod_cols, dmod_all):
    def body(c_ref, dc_ref, da_ref, gw_ref, gb_ref):
        act = _silu(c_ref[...])
        act16 = jnp.concatenate([act, jnp.zeros_like(act)], axis=0)
        dm = dc_ref[...]
        dm16 = jnp.concatenate([dm, jnp.zeros_like(dm)], axis=0)
        gw_ref[...] = _dot_f32(act16, dm16, TN)
        gb_ref[...] = _colsum(da_ref[...])

    return pl.pallas_call(
        body, name="ada_grad",
        out_shape=[jax.ShapeDtypeStruct((D, dmod_cols.shape[1]), F32), jax.ShapeDtypeStruct((1, 6 * D), F32)],
        compiler_params=_params(None),
    )(c_all, dmod_cols, dmod_all)


def _sum_slots(name, g):
    def body(g_ref, o_ref):
        acc = g_ref[0]
        for k in range(1, N_DEV):
            acc = acc + g_ref[k]
        o_ref[...] = acc

    return pl.pallas_call(body, name=name, out_shape=jax.ShapeDtypeStruct(g.shape[1:], F32),
                          compiler_params=_params(None))(g)


def _exchange(name, xs, scatter):
    n = len(xs)
    n_peer = N_DEV - 1

    def body(*refs):
        x_refs, o_refs = refs[:n], refs[n:2 * n]
        send_sems, recv_sems, local_sems = refs[2 * n:]
        mx, my, mc = lax.axis_index("x"), lax.axis_index("y"), lax.axis_index("c")
        me = 4 * mx + 2 * my + mc

        def src(a, slot):
            return x_refs[a].at[slot] if scatter else x_refs[a]

        own = [pltpu.make_async_copy(src(a, me), o_refs[a].at[me], local_sems.at[a]) for a in range(n)]
        for cp in own:
            cp.start()
        sends = []
        for d in range(1, N_DEV):
            px = 1 - mx if d & 4 else mx
            py = 1 - my if d & 2 else my
            pc = 1 - mc if d & 1 else mc
            peer = 4 * px + 2 * py + pc
            for a in range(n):
                def copy(src_slot, dst_slot, a=a, d=d, to=(px, py, pc)):
                    return pltpu.make_async_remote_copy(
                        src_ref=src(a, src_slot), dst_ref=o_refs[a].at[dst_slot],
                        send_sem=send_sems.at[a * n_peer + d - 1], recv_sem=recv_sems.at[a * n_peer + d - 1],
                        device_id=to, device_id_type=pl.DeviceIdType.MESH)

                out = copy(peer, me)
                out.start()
                sends.append((out, copy(me, peer)))
        for _, arrival in sends:
            arrival.wait_recv()
        for out, _ in sends:
            out.wait_send()
        for cp in own:
            cp.wait()

    shapes = [tuple(x.shape[1:] if scatter else x.shape) for x in xs]
    return pl.pallas_call(
        body, name=name,
        in_specs=[pl.BlockSpec(memory_space=pl.ANY)] * n, out_specs=[pl.BlockSpec(memory_space=pl.ANY)] * n,
        out_shape=[jax.ShapeDtypeStruct((N_DEV,) + sh, x.dtype) for sh, x in zip(shapes, xs)],
        scratch_shapes=[pltpu.SemaphoreType.DMA((n * n_peer,)), pltpu.SemaphoreType.DMA((n * n_peer,)),
                        pltpu.SemaphoreType.DMA((n,))],
        compiler_params=pltpu.CompilerParams(has_side_effects=True),
    )(*xs)


def _gather_two_level(name, x):
    def body(x_ref, o_ref, send_sems, recv_sems, local_sem):
        mx, my, mc = lax.axis_index("x"), lax.axis_index("y"), lax.axis_index("c")
        me, sibling = (mx, my, mc), (mx, my, 1 - mc)
        chips = [(1 - mx, my), (mx, 1 - my), (1 - mx, 1 - my)]

        def slot(px, py, pc):
            return o_ref.at[4 * px + 2 * py + pc]

        def copy(k, block, to, src=None):
            return pltpu.make_async_remote_copy(
                src_ref=slot(*block) if src is None else src, dst_ref=slot(*block),
                send_sem=send_sems.at[k], recv_sem=recv_sems.at[k], device_id=to, device_id_type=pl.DeviceIdType.MESH)

        mine = pltpu.make_async_copy(x_ref, slot(*me), local_sem)
        mine.start()
        first = [copy(0, me, sibling, src=x_ref)] + [copy(1 + i, me, (*chip, mc), src=x_ref) for i, chip in enumerate(chips)]
        for cp in first:
            cp.start()
        passed = [copy(4 + i, (*chip, mc), sibling) for i, chip in enumerate(chips)]
        for i, chip in enumerate(chips):
            copy(1 + i, (*chip, mc), me).wait_recv()
            passed[i].start()
        copy(0, sibling, me).wait_recv()
        for i, chip in enumerate(chips):
            copy(4 + i, (*chip, 1 - mc), me).wait_recv()
        for cp in first + passed:
            cp.wait_send()
        mine.wait()

    return pl.pallas_call(
        body, name=name,
        in_specs=[pl.BlockSpec(memory_space=pl.ANY)], out_specs=pl.BlockSpec(memory_space=pl.ANY),
        out_shape=jax.ShapeDtypeStruct((N_DEV,) + tuple(x.shape), x.dtype),
        scratch_shapes=[pltpu.SemaphoreType.DMA((7,)), pltpu.SemaphoreType.DMA((7,)), pltpu.SemaphoreType.DMA(())],
        compiler_params=pltpu.CompilerParams(has_side_effects=True),
    )(x)


def _after(x, zero):
    return x if zero is None else x + zero.reshape(-1)[0].astype(x.dtype)


_HBM = pl.BlockSpec(memory_space=pltpu.HBM)
_SEM = pl.BlockSpec(memory_space=pltpu.SEMAPHORE)


def _exchange_copies(x_refs, land_refs, send_sems, recv_sems, scatter):
    n = len(x_refs)
    n_peer = N_DEV - 1
    mx, my, mc = lax.axis_index("x"), lax.axis_index("y"), lax.axis_index("c")
    me = 4 * mx + 2 * my + mc
    pairs = []
    for d in range(1, N_DEV):
        px = 1 - mx if d & 4 else mx
        py = 1 - my if d & 2 else my
        pc = 1 - mc if d & 1 else mc
        peer = 4 * px + 2 * py + pc
        for a in range(n):
            def copy(src_slot, dst_slot, a=a, d=d, to=(px, py, pc)):
                return pltpu.make_async_remote_copy(
                    src_ref=x_refs[a].at[src_slot] if scatter else x_refs[a], dst_ref=land_refs[a].at[dst_slot],
                    send_sem=send_sems.at[a * n_peer + d - 1], recv_sem=recv_sems.at[a * n_peer + d - 1],
                    device_id=to, device_id_type=pl.DeviceIdType.MESH)

            pairs.append((copy(peer, me), copy(me, peer)))
    return me, pairs


def _exchange_async(name, xs, scatter, collective_id):
    n = len(xs)
    shapes = [tuple(x.shape[1:] if scatter else x.shape) for x in xs]
    x_refs = [jax.new_ref(x, memory_space=pltpu.MemorySpace.HBM) for x in xs]
    land_refs = [jax.empty_ref(jax.ShapeDtypeStruct((N_DEV,) + sh, x.dtype), memory_space=pltpu.MemorySpace.HBM)
                 for sh, x in zip(shapes, xs)]

    @pl.kernel(mesh=plsc.ScalarSubcoreMesh(axis_name="sequencer", num_cores=1), name=name,
               scratch_types=(pltpu.SemaphoreType.DMA((n * (N_DEV - 1),)), pltpu.SemaphoreType.DMA((n * (N_DEV - 1),)),
                              pltpu.SemaphoreType.DMA((n,))),
               compiler_params=pltpu.CompilerParams(collective_id=collective_id))
    def launch(send_sems, recv_sems, own_sems):
        barrier = pltpu.get_barrier_semaphore()
        mx, my, mc = lax.axis_index("x"), lax.axis_index("y"), lax.axis_index("c")
        for d in range(1, N_DEV):
            peer = (1 - mx if d & 4 else mx, 1 - my if d & 2 else my, 1 - mc if d & 1 else mc)
            pl.semaphore_signal(barrier, inc=1, device_id=peer, device_id_type=pl.DeviceIdType.MESH)
        pl.semaphore_wait(barrier, N_DEV - 1)
        me, pairs = _exchange_copies(x_refs, land_refs, send_sems, recv_sems, scatter)
        own = [pltpu.make_async_copy(x_refs[a].at[me] if scatter else x_refs[a], land_refs[a].at[me], own_sems.at[a])
               for a in range(n)]
        for cp in own:
            cp.start()
        for out, _ in pairs:
            out.start()
        for out, arrival in pairs:
            arrival.wait_recv()
            out.wait_send()
        for cp in own:
            cp.wait()

    launch()
    return lambda: [r[...] for r in land_refs]


def _exchange_start(name, xs, scatter):
    n = len(xs)
    shapes = [tuple(x.shape[1:] if scatter else x.shape) for x in xs]

    def body(*refs):
        x_refs, land_refs = refs[:n], refs[n:2 * n]
        send_sems, recv_sems = refs[2 * n], refs[2 * n + 1]
        token, own_sems = refs[4 * n + 2], refs[4 * n + 3]
        me, pairs = _exchange_copies(x_refs, land_refs, send_sems, recv_sems, scatter)
        own = [pltpu.make_async_copy(x_refs[a].at[me] if scatter else x_refs[a], land_refs[a].at[me], own_sems.at[a])
               for a in range(n)]
        for cp in own:
            cp.start()
        for out, _ in pairs:
            out.start()
        for cp in own:
            cp.wait()
        token[...] = jnp.zeros(token.shape, token.dtype)

    lands = [pltpu.with_memory_space_constraint(lax.empty((N_DEV,) + sh, x.dtype), pltpu.HBM) for sh, x in zip(shapes, xs)]
    res = pl.pallas_call(
        body, name=name,
        out_shape=(pltpu.SemaphoreType.DMA((n * (N_DEV - 1),)), pltpu.SemaphoreType.DMA((n * (N_DEV - 1),)),
                   *[pltpu.HBM(x.shape, x.dtype) for x in xs], *[pltpu.HBM(l.shape, l.dtype) for l in lands],
                   jax.ShapeDtypeStruct((8, 128), F32)),
        in_specs=[_HBM] * (2 * n),
        out_specs=(_SEM, _SEM, *[_HBM] * (2 * n), pl.BlockSpec(memory_space=pltpu.VMEM)),
        input_output_aliases={i: 2 + i for i in range(2 * n)},
        scratch_shapes=[pltpu.SemaphoreType.DMA((n,))],
        compiler_params=pltpu.CompilerParams(has_side_effects=pltpu.SideEffectType.DATAFLOW_SIDE_EFFECTING),
    )(*[pltpu.with_memory_space_constraint(x, pltpu.HBM) for x in xs], *lands)
    return dict(send=res[0], recv=res[1], xs=list(res[2:2 + n]), lands=list(res[2 + n:2 + 2 * n]), token=res[2 + 2 * n])


def _exchange_wait(name, handle, after, scatter):
    n = len(handle['xs'])

    def body(*refs):
        x_refs, land_refs = refs[:n], refs[n:2 * n]
        send_sems, recv_sems = refs[2 * n], refs[2 * n + 1]
        _, pairs = _exchange_copies(x_refs, land_refs, send_sems, recv_sems, scatter)
        for out, arrival in pairs:
            out.wait_send()
            arrival.wait_recv()

    res = pl.pallas_call(
        body, name=name,
        out_shape=tuple(pltpu.HBM(a.shape, a.dtype) for a in handle['xs'] + handle['lands']),
        in_specs=[_HBM] * (2 * n) + [_SEM, _SEM, pl.BlockSpec(memory_space=pl.ANY)],
        out_specs=tuple([_HBM] * (2 * n)),
        input_output_aliases={i: i for i in range(2 * n)},
        compiler_params=pltpu.CompilerParams(has_side_effects=pltpu.SideEffectType.DATAFLOW_SIDE_EFFECTING),
    )(*handle['xs'], *handle['lands'], handle['send'], handle['recv'], after)
    return list(res[n:])


def _relu2(a):
    r = jnp.maximum(a, 0.0)
    return r * r


def _relu2_grad(acc, a):
    return acc * (2.0 * jnp.maximum(a, 0.0))


def _local_step(x0, tgt, mod, wcat, late_weights, send_grads, conv_w, conv_b, dt_bias, a_log, d_skip, ssm_norm_w, f_bias,
                attn_norm_w, ln1_g, ln1_b, ln2_g, ln2_b):
    ff_w = DFF // N_DEV
    s = x0.shape[0]
    tm = min(1024, s)
    ts = min(1024, s)
    sh1, sc1, g1, sh2, sc2, g2 = [mod[:, i * D:(i + 1) * D] for i in range(6)]
    zero = jnp.zeros((1, 128 - 2 * NH), F32)
    bias128 = jnp.concatenate([dt_bias, f_bias, zero], axis=1)
    alog128 = jnp.concatenate([a_log, jnp.zeros((1, 128 - NH), F32)], axis=1)
    dskip_x = jnp.repeat(d_skip, HD, axis=1)
    w_xs, w_bc, b_xs, b_bc = conv_w[:, :D], conv_w[:, D:], conv_b[:, :D], conv_b[:, D:]

    p = _mm_nn("in_proj", x0, wcat, tm=tm, tn=1152, tk=D, out_dtype=F32, pro=_modulate, aux=(sc1, sh1))
    xs_a, bc_a = _conv_fwd(p, w_xs, b_xs, w_bc, b_bc, s)
    y_ssd, states = _ssd_fwd(xs_a, bc_a, p, bias128, alog128, dskip_x, s)
    cum = _cum_fwd(p, bias128, s)
    att, lse = _attn_fwd(p, cum, s)
    wout, w1s, w2 = late_weights(lse)
    ymix = _mix_norm(y_ssd, p, att, ssm_norm_w, attn_norm_w, s)
    y = _mm_nn("out_proj", ymix, wout, tm=tm, tn=1024, tk=2 * D, out_dtype=F32)
    x1, h2 = _ln1(x0, y, g1, ln1_g, ln1_b, sc2, sh2, s)
    a1 = _mm_nn("ff_in", h2, w1s, tm=tm, tn=ff_w, tk=D, out_dtype=F32)
    ff = _mm_nn("ff_out", a1, w2, tm=tm, tn=1024, tk=1024, out_dtype=F32, pro=_relu2)
    du2, dff, sq_err, d_ln2_g, d_ln2_b, d_g2 = _ln2_loss(x1, ff, tgt, g2, ln2_g, ln2_b, s)

    da1 = _mm_nt("d_ff_hidden", [(dff, D, 0)], [(w2, D, 0)], n=DFF, tm=tm, tn=1024, out_dtype=BF16, epi=_relu2_grad,
                 epi_aux=(a1,))
    d_w2 = _mm_tn("d_w_ff_out", a1, dff, tm=1024, tn=1024, ts=ts, pro=_relu2)
    d_w1s = _mm_tn("d_w_ff_in", h2, da1, tm=1024, tn=ff_w, ts=ts, col_shards=True)
    dh2 = _mm_nt("d_ff_input", [(da1, ff_w, k) for k in range(N_DEV)], [(w1s, ff_w, k) for k in range(N_DEV)], n=D,
                 tm=min(512, s), tn=1024, out_dtype=F32)
    sent = send_grads("ff", [d_w1s, d_w2.reshape(N_DEV, -1, D)])
    du1, dy, d_sc2, d_sh2, d_ln1_g, d_ln1_b, d_g1 = _ln1_bwd(dh2, du2, x0, y, g1, ln1_g, ln1_b, _after(sc2, sent), s)

    dmix = _mm_nt("d_mix", [(dy, D, 0)], [(wout, D, 0)], n=2 * D, tm=tm, tn=1024, out_dtype=F32)
    d_wout = _mm_tn("d_w_out", ymix, dy, tm=1024, tn=1024, ts=ts)
    sent = send_grads("out", [d_wout.reshape(N_DEV, -1, D)])
    dy_ssd, dz, datt, d_ssm_w, d_attn_w = _mix_norm_bwd(dmix, y_ssd, p, att, _after(ssm_norm_w, sent), attn_norm_w, s)
    dq, dk, dv, dcs, drs = _attn_bwd(p, cum, att, lse, datt, s)
    dxs_a, dbc_a, ddt_raw, d_alog, d_dskip = _ssd_bwd(dy_ssd, xs_a, bc_a, p, states, bias128, alog128, dskip_x, s)
    dr_col = jnp.pad(drs[:, :2, :].reshape(NH, s).T, ((0, 0), (NH, 128 - 2 * NH)))
    ddtf, _, d_bias = _cum_bwd(dr_col, dcs, ddt_raw, p, bias128, s)
    dxs, dbc, d_wc_xs, d_bc_xs, d_wc_bc, d_bc_bc = _conv_bwd(dxs_a, dbc_a, p, w_xs, b_xs, w_bc, b_bc, s)

    segs = [(dz, OFF_Z, D), (dxs, OFF_XS, D), (dq, OFF_Q, D), (dk, OFF_K, D), (dv, OFF_V, D), (dbc, OFF_BC, 512),
            (ddtf, OFF_DTF, 128)]
    d_z, d_xs, d_q, d_k, d_v, d_bcw, d_dtf = [
        _mm_tn("d_w_in_%d" % i, x0, a, tm=1024, tn=min(w, 1024), ts=ts, pro=_modulate, aux=(sc1, sh1))
        for i, (a, _, w) in enumerate(segs)]
    d_w_in = dict(z=d_z, xs=d_xs, bc=d_bcw, dt=d_dtf[:, :NH], q=d_q, k=d_k, v=d_v, f=d_dtf[:, NH:2 * NH])
    sent = send_grads("in", [_shard_w_in_grad(d_w_in)])
    segs[-1] = (_after(ddtf, sent), OFF_DTF, 128)
    dh1 = _mm_nt("d_h1", [(a, w, 0) for a, _, w in segs], [(wcat, w, off // w) for _, off, w in segs], n=D,
                 tm=min(512, s), tn=1024, out_dtype=F32)
    grad_x, d_sc1, d_sh1 = _input_grad(dh1, du1, x0, sc1, s)

    return dict(
        loss=(0.5 / D) * jnp.sum(sq_err), grad_x=grad_x,
        d_mod=jnp.concatenate([d_sh1, d_sc1, d_g1, d_sh2, d_sc2, d_g2], axis=1),
        d_conv_w=jnp.concatenate([d_wc_xs[:4], d_wc_bc[:4]], axis=1), d_conv_b=jnp.concatenate([d_bc_xs, d_bc_bc], axis=1),
        d_ssm_norm_w=d_ssm_w, d_attn_norm_w=d_attn_w, d_ln1_g=d_ln1_g, d_ln1_b=d_ln1_b, d_ln2_g=d_ln2_g, d_ln2_b=d_ln2_b,
        d_gate_bias=d_bias, d_a_log=d_alog, d_d_skip=d_dskip)


W_IN_SEGS = [('z', W_Z, D), ('xs', W_XS, D), ('bc', W_BC, 512), ('dt', W_DT, NH), ('q', W_Q, D), ('k', W_K, D),
             ('v', W_V, D), ('f', W_F, NH)]
SHARD_W = IN_COLS // N_DEV


def _pack_w_in(shards):
    def cols(lo, hi):
        pieces = []
        while lo < hi:
            dev = lo // SHARD_W
            end = min(hi, (dev + 1) * SHARD_W)
            pieces.append(shards[dev][:, lo - dev * SHARD_W:end - dev * SHARD_W])
            lo = end
        return pieces

    seg = {n: cols(off, off + w) for n, off, w in W_IN_SEGS}
    pieces = seg['z'] + seg['xs'] + seg['q'] + seg['k'] + seg['v'] + seg['bc'] + seg['dt'] + seg['f']
    return jnp.concatenate(pieces + [jnp.zeros((D, 128 - 2 * NH), shards.dtype)], axis=1)


def _shard_w_in_grad(d_w_in):
    blocks = []
    for dev in range(N_DEV):
        lo, hi = dev * SHARD_W, (dev + 1) * SHARD_W
        pieces = [d_w_in[n][:, max(lo, off) - off:min(hi, off + w) - off] for n, off, w in W_IN_SEGS
                  if max(lo, off) < min(hi, off + w)]
        pieces.append(jnp.zeros((D, -SHARD_W % 128), pieces[0].dtype))
        blocks.append(jnp.concatenate(pieces, axis=1))
    return jnp.stack(blocks, axis=0)


WEIGHTS = ['w_ada', 'b_ada', 'w_in', 'conv_w', 'conv_b', 'dt_bias', 'a_log', 'd_skip', 'ssm_norm_w', 'f_bias',
           'attn_norm_w', 'w_out', 'ln1_g', 'ln1_b', 'w_ff_in', 'w_ff_out', 'ln2_g', 'ln2_b']
BIG = ['w_in', 'w_out', 'w_ff_in', 'w_ff_out']
SMALL = ['b_ada', 'conv_b', 'ssm_norm_w', 'attn_norm_w', 'ln1_g', 'ln1_b', 'ln2_g', 'ln2_b', 'dt_bias', 'a_log', 'd_skip',
         'f_bias', 'conv_w']


def _pad_lanes(v, n=128):
    return jnp.pad(v, ((0, 0), (0, n - v.shape[1])))


def _small_block(vals):
    rows = [_pad_lanes(vals[n].reshape(1, -1), -(-vals[n].size // 128) * 128).reshape(-1, 128) for n in SMALL]
    block = jnp.concatenate(rows, axis=0)
    return jnp.pad(block, ((0, 120 - block.shape[0]), (0, 0)))


def _small_unblock(block, like):
    out, r = {}, 0
    for n in SMALL:
        size = like[n].size
        nr = -(-size // 128)
        out[n] = block[r:r + nr].reshape(-1)[:size].reshape(like[n].shape)
        r += nr
    return out


def kernel(x, c, w_ada, b_ada, w_in, conv_w, conv_b, dt_bias, a_log, d_skip, ssm_norm_w, f_bias, attn_norm_w, w_out, ln1_g, ln1_b, w_ff_in, w_ff_out, ln2_g, ln2_b, loss_target, m_w_ada, m_b_ada, m_w_in, m_conv_w, m_conv_b, m_dt_bias, m_a_log, m_d_skip, m_ssm_norm_w, m_f_bias, m_attn_norm_w, m_w_out, m_ln1_g, m_ln1_b, m_w_ff_in, m_w_ff_out, m_ln2_g, m_ln2_b, v_w_ada, v_b_ada, v_w_in, v_conv_w, v_conv_b, v_dt_bias, v_a_log, v_d_skip, v_ssm_norm_w, v_f_bias, v_attn_norm_w, v_w_out, v_ln1_g, v_ln1_b, v_w_ff_in, v_w_ff_out, v_ln2_g, v_ln2_b):
    args = dict(locals())
    w = {n: args[n] for n in WEIGHTS}
    m = {n: args['m_' + n] for n in WEIGHTS}
    v = {n: args['v_' + n] for n in WEIGHTS}
    me = 4 * lax.axis_index("x") + 2 * lax.axis_index("y") + lax.axis_index("c")
    ada_cols = 6 * D // N_DEV
    conv_cols = conv_w.shape[2]

    c_all, conv_all = _exchange("gather_cond", [c, conv_w[0]], False)
    c_all = c_all.reshape(N_DEV, D)
    conv_w_full = conv_all.transpose(1, 0, 2).reshape(4, N_DEV * conv_cols)
    b_shard = lax.dynamic_slice(b_ada, (0, me * ada_cols), (1, ada_cols))
    mod_all, = _exchange("gather_mod", [_ada_mod(c_all, w_ada[0], b_shard)], False)
    mod = lax.dynamic_index_in_dim(mod_all, me, axis=1, keepdims=False).reshape(1, 6 * D)

    win_s = _gather_two_level("gather_w_in", _after(w_in[0].astype(BF16), mod * 0))
    first_done = win_s[0, 0:1, 0:1] * 0
    rest = _exchange_async("gather_rest", [_after(w[n][0].astype(BF16), first_done) for n in BIG[1:]], False, 1)

    def late_weights(after):
        wout_s, w1s, w2_s = rest()
        return wout_s.reshape(2 * D, D), w1s, w2_s.reshape(DFF, D)

    sends = {}

    def send_grads(tag, blocks):
        sends[tag] = _exchange_async("scatter_" + tag, blocks, True, {'ff': 2, 'out': 3, 'in': 4}[tag])
        return sum(b.reshape(-1)[0].astype(F32) * 0 for b in blocks)

    out = _local_step(x[0], loss_target[0], mod, _pack_w_in(win_s), late_weights, send_grads,
                      conv_w_full, conv_b, dt_bias, a_log, d_skip, ssm_norm_w, f_bias, attn_norm_w, ln1_g, ln1_b, ln2_g, ln2_b)

    small = jnp.concatenate(
        [out['d_mod'], out['d_conv_w'].reshape(1, -1), out['d_conv_b'], out['d_ssm_norm_w'], out['d_attn_norm_w'],
         out['d_ln1_g'], out['d_ln1_b'], out['d_ln2_g'], out['d_ln2_b'], out['d_gate_bias'], out['d_a_log'],
         out['d_d_skip'], _pad_lanes(out['loss'].reshape(1, 1))], axis=1).reshape(-1, 128)
    small_landed = _exchange_async("gather_small", [small], False, 5)
    (g_ff_in, g_ff_out), (g_out,), (g_in,) = sends['ff'](), sends['out'](), sends['in']()
    g_parts = dict(w_ff_in=g_ff_in, w_ff_out=g_ff_out, w_out=g_out, w_in=g_in)
    big = {n: _adamw("adamw_" + n, w[n][0], g_parts[n], m[n][0], v[n][0], tr=256, slots=True) for n in BIG}
    big_done = sum(big[n][1][0:1, 0:1] * 0 for n in BIG)
    small_all = _after(small_landed()[0], big_done)
    ssum = _sum_slots("sum_small", small_all)
    dmod_all = small_all[:, :6 * D // 128].reshape(N_DEV, 6 * D)
    g_w_ada, g_b_ada = _ada_grad(c_all, lax.dynamic_slice(dmod_all, (0, me * ada_cols), (N_DEV, ada_cols)), dmod_all)
    rows = lambda a, b: ssum[a:b].reshape(1, -1)
    g_conv_w = lax.dynamic_slice(ssum[48:96].reshape(4, N_DEV * conv_cols), (0, me * conv_cols), (4, conv_cols))
    g_small = dict(b_ada=g_b_ada, conv_w=g_conv_w[None], conv_b=rows(96, 108), ssm_norm_w=rows(108, 116),
                   attn_norm_w=rows(116, 124), ln1_g=rows(124, 132), ln1_b=rows(132, 140), ln2_g=rows(140, 148),
                   ln2_b=rows(148, 156), dt_bias=ssum[156:157, :NH], f_bias=ssum[156:157, NH:2 * NH],
                   a_log=ssum[157:158, :NH], d_skip=ssum[158:159, :NH])
    sm = _adamw("adamw_small", _small_block(w), _small_block(g_small), _small_block(m), _small_block(v), tr=120, slots=False)
    ada = _adamw("adamw_ada", w_ada[0], g_w_ada, m_w_ada[0], v_w_ada[0], tr=256, slots=False)

    results = []
    for k in range(4):
        vals = _small_unblock(sm[k], w)
        vals['w_ada'] = ada[k][None]
        for n in BIG:
            vals[n] = big[n][k][None]
        results.append(vals)
    return (ssum[159, 0], out['grad_x'][None], *[res[n] for res in results for n in WEIGHTS])
```

```python
import functools

import jax
import jax.numpy as jnp
from jax import lax
from jax.experimental import pallas as pl
from jax.experimental.pallas import tpu as pltpu
from jax.experimental.pallas import tpu_sc as plsc

F32, BF16 = jnp.float32, jnp.bfloat16

N_DEV = 8
D = 1024
NH, HD = 16, 64
NSTATE = 128
CHUNK = 128
HG = 8
DFF = 4096
ALPHA = 2.0 ** 0.25
EPS = 1e-5
ATT_SCALE = HD ** -0.5

OFF_Z, OFF_XS, OFF_Q, OFF_K, OFF_V, OFF_BC, OFF_DTF = 0, 1024, 2048, 3072, 4096, 5120, 5632
PCOLS = 5760
W_Z, W_XS, W_BC, W_DT, W_Q, W_K, W_V, W_F = 0, 1024, 2048, 2560, 2576, 3600, 4624, 5648
IN_COLS = 5664

ADAM_LR, ADAM_B1, ADAM_B2, ADAM_EPS, ADAM_WD, ADAM_STEP = 0.001, 0.9, 0.999, 1e-08, 0.01, 10

VMEM_LIMIT = 56 << 20

NN = (((1,), (0,)), ((), ()))
NT = (((1,), (1,)), ((), ()))
TN = (((0,), (0,)), ((), ()))


def _dot(a, b, dims=NN):
    return lax.dot_general(a, b, dims, preferred_element_type=F32)


def _bdot(a, b, dims=NN):
    return _dot(a.astype(BF16), b.astype(BF16), dims)


def _split3(v, terms=3):
    parts, rest = [], v
    for _ in range(terms):
        p = rest.astype(BF16)
        parts.append(p)
        rest = rest - p.astype(F32)
    return parts


def _sel_left(m01, v):
    return sum(_dot(m01, p) for p in _split3(v))


def _sel_right(v, m01, dims=NN, terms=3):
    return sum(_dot(p, m01, dims) for p in _split3(v, terms))


def _iota(shape, dim):
    return lax.broadcasted_iota(jnp.int32, shape, dim)


def _tri_lower(n):
    return (_iota((n, n), 1) <= _iota((n, n), 0)).astype(BF16)


def _tri_upper(n):
    return (_iota((n, n), 1) >= _iota((n, n), 0)).astype(BF16)


def _head_expand():
    return (lax.shift_right_logical(_iota((128, D), 1), 6) == _iota((128, D), 0)).astype(BF16)


def _head_reduce():
    return (lax.shift_right_logical(_iota((D, 128), 0), 6) == _iota((D, 128), 1)).astype(BF16)


def _sigmoid(x):
    return 1.0 / (1.0 + jnp.exp(-x))


def _silu(x):
    return x * _sigmoid(x)


def _dsilu(x):
    s = _sigmoid(x)
    return s * (1.0 + x * (1.0 - s))


def _softplus(x):
    return jnp.maximum(x, 0.0) + jnp.log(1.0 + jnp.exp(-jnp.abs(x)))


def _log_sigmoid(x):
    return jnp.minimum(x, 0.0) - jnp.log(1.0 + jnp.exp(-jnp.abs(x)))


def _params(sem):
    return pltpu.CompilerParams(dimension_semantics=sem, vmem_limit_bytes=VMEM_LIMIT)


def _mm_nn(name, a, b, *, tm, tn, tk, out_dtype, pro=None, aux=()):
    m, k_all = a.shape
    b_sharded = b.ndim == 3
    n = b.shape[0] * b.shape[2] if b_sharded else b.shape[1]
    assert not b_sharded or tn == b.shape[2]
    nk = k_all // tk
    n_aux = len(aux)
    b_spec = (pl.BlockSpec((None, tk, tn), lambda i, j, k: (j, k, 0)) if b_sharded
              else pl.BlockSpec((tk, tn), lambda i, j, k: (k, j)))

    def body(a_ref, b_ref, *rest):
        aux_refs, o_ref = rest[:n_aux], rest[n_aux]
        at = a_ref[...]
        if pro is not None:
            at = pro(at, *[r[...] for r in aux_refs])
        part = _bdot(at, b_ref[...])
        if nk == 1:
            o_ref[...] = part.astype(out_dtype)
            return
        acc_ref = rest[n_aux + 1]
        kk = pl.program_id(2)

        @pl.when(kk == 0)
        def _():
            acc_ref[...] = part

        @pl.when(kk > 0)
        def _():
            acc_ref[...] += part

        @pl.when(kk == nk - 1)
        def _():
            o_ref[...] = acc_ref[...].astype(out_dtype)

    return pl.pallas_call(
        body, name=name,
        grid=(m // tm, n // tn, nk),
        in_specs=[pl.BlockSpec((tm, tk), lambda i, j, k: (i, k)), b_spec]
        + [pl.BlockSpec((1, tk), lambda i, j, k: (0, k)) for _ in aux],
        out_specs=pl.BlockSpec((tm, tn), lambda i, j, k: (i, j)),
        out_shape=jax.ShapeDtypeStruct((m, n), out_dtype),
        scratch_shapes=[] if nk == 1 else [pltpu.VMEM((tm, tn), F32)],
        compiler_params=_params(("parallel", "parallel", "arbitrary")),
    )(a, b, *aux)


def _mm_nt(name, a_list, b_list, *, n, tm, tn, out_dtype, epi=None, epi_aux=()):
    m = a_list[0][0].shape[0]
    n_op = len(a_list)
    n_epi = len(epi_aux)

    def body(*refs):
        a_refs, b_refs = refs[:n_op], refs[n_op:2 * n_op]
        e_refs, o_ref = refs[2 * n_op:2 * n_op + n_epi], refs[2 * n_op + n_epi]
        acc = None
        for a_ref, b_ref in zip(a_refs, b_refs):
            part = _bdot(a_ref[...], b_ref[...], NT)
            acc = part if acc is None else acc + part
        if epi is not None:
            acc = epi(acc, *[r[...] for r in e_refs])
        o_ref[...] = acc.astype(out_dtype)

    in_specs = [pl.BlockSpec((tm, w), functools.partial(lambda i, j, cb: (i, cb), cb=cb)) for (_, w, cb) in a_list]
    for (b, w, cb) in b_list:
        if b.ndim == 3:
            in_specs.append(pl.BlockSpec((None, tn, w), functools.partial(lambda i, j, cb: (cb, j, 0), cb=cb)))
        else:
            in_specs.append(pl.BlockSpec((tn, w), functools.partial(lambda i, j, cb: (j, cb), cb=cb)))
    in_specs += [pl.BlockSpec((tm, tn), lambda i, j: (i, j)) for _ in epi_aux]
    return pl.pallas_call(
        body, name=name,
        grid=(m // tm, n // tn),
        in_specs=in_specs,
        out_specs=pl.BlockSpec((tm, tn), lambda i, j: (i, j)),
        out_shape=jax.ShapeDtypeStruct((m, n), out_dtype),
        compiler_params=_params(("parallel", "parallel")),
    )(*[a for (a, _, _) in a_list], *[b for (b, _, _) in b_list], *epi_aux)


def _mm_tn(name, a, b, *, tm, tn, ts, pro=None, aux=(), col_shards=False):
    s_all, ka = a.shape
    nb = b.shape[1]
    n_aux = len(aux)
    ns = s_all // ts
    assert not col_shards or tn == nb // N_DEV

    def body(a_ref, b_ref, *rest):
        aux_refs, o_ref, acc_ref = rest[:n_aux], rest[n_aux], rest[n_aux + 1]
        at = a_ref[...]
        if pro is not None:
            at = pro(at, *[r[...] for r in aux_refs])
        part = _bdot(at, b_ref[...], TN)
        ss = pl.program_id(2)

        @pl.when(ss == 0)
        def _():
            acc_ref[...] = part

        @pl.when(ss > 0)
        def _():
            acc_ref[...] += part

        @pl.when(ss == ns - 1)
        def _():
            o_ref[...] = acc_ref[...].astype(BF16)

    if col_shards:
        out_spec = pl.BlockSpec((None, tm, tn), lambda i, j, s: (j, i, 0))
        out_shape = jax.ShapeDtypeStruct((N_DEV, ka, tn), BF16)
    else:
        out_spec = pl.BlockSpec((tm, tn), lambda i, j, s: (i, j))
        out_shape = jax.ShapeDtypeStruct((ka, nb), BF16)
    return pl.pallas_call(
        body, name=name,
        grid=(ka // tm, nb // tn, ns),
        in_specs=[pl.BlockSpec((ts, tm), lambda i, j, s: (s, i)),
                  pl.BlockSpec((ts, tn), lambda i, j, s: (s, j))]
        + [pl.BlockSpec((1, tm), lambda i, j, s: (0, i)) for _ in aux],
        out_specs=out_spec, out_shape=out_shape,
        scratch_shapes=[pltpu.VMEM((tm, tn), F32)],
        compiler_params=_params(("parallel", "parallel", "arbitrary")),
    )(a, b, *aux)


def _rowk(name, fn, n_rows, tr, rows, fulls, outs, accs, reverse=False):
    n = n_rows // tr
    n_row, n_full, n_out, n_acc = len(rows), len(fulls), len(outs), len(accs)

    def pos(i):
        return (n - 1 - i) if reverse else i

    def body(*refs):
        row_refs = refs[:n_row]
        full_refs = refs[n_row:n_row + n_full]
        out_refs = refs[n_row + n_full:n_row + n_full + n_out]
        acc_refs = refs[n_row + n_full + n_out:]
        i = pl.program_id(0)

        @pl.when(i == 0)
        def _():
            for r in acc_refs:
                r[...] = jnp.zeros(r.shape, r.dtype)

        res = fn(pos(i), *[r[...] for r in row_refs], *[r[...] for r in full_refs], *[r[...] for r in acc_refs])
        for r, v in zip(out_refs + acc_refs, res):
            r[...] = v.astype(r.dtype)

    def row_map(i, cb, shift):
        return (jnp.clip(pos(i) + shift, 0, n - 1), cb)

    def halo_map(i, cb, shift):
        tile = jnp.clip(pos(i) + shift, 0, n - 1)
        return (tile * (tr // 8) + (tr // 8 - 1 if shift < 0 else 0), cb)

    in_specs = [pl.BlockSpec((tr, w), functools.partial(row_map, cb=cb, shift=sh)) if sh == 0 else
                pl.BlockSpec((8, w), functools.partial(halo_map, cb=cb, shift=sh)) for (_, w, cb, sh) in rows]
    in_specs += [pl.BlockSpec(f.shape, functools.partial(lambda i, nd: (0,) * nd, nd=f.ndim)) for f in fulls]
    out_specs = [pl.BlockSpec((tr, w), lambda i: (pos(i), 0)) for (w, _) in outs]
    out_specs += [pl.BlockSpec((r, w), lambda i: (0, 0)) for (r, w) in accs]
    out_shape = [jax.ShapeDtypeStruct((n_rows, w), dt) for (w, dt) in outs]
    out_shape += [jax.ShapeDtypeStruct((r, w), F32) for (r, w) in accs]
    return pl.pallas_call(
        body, name=name, grid=(n,), in_specs=in_specs, out_specs=out_specs, out_shape=out_shape,
        compiler_params=_params(("arbitrary",)),
    )(*[a for (a, _, _, _) in rows], *fulls)


def _colsum(x):
    return jnp.sum(x, axis=0, keepdims=True)


def _mean(x):
    return jnp.mean(x, axis=-1, keepdims=True)


def _modulate(x, sc, sh):
    return x * (1.0 + sc) + sh


def _shift_down(cur, prev8, j):
    tr = cur.shape[0]
    row8 = _iota(prev8.shape, 0)
    head = jnp.where(row8 < j, pltpu.roll(prev8, j, 0), pltpu.roll(cur[0:8], j, 0))
    return head if tr == 8 else jnp.concatenate([head, pltpu.roll(cur, j, 0)[8:]], axis=0)


def _shift_up(cur, next8, j):
    tr = cur.shape[0]
    row8 = _iota(next8.shape, 0)
    tail = jnp.where(row8 < 8 - j, pltpu.roll(cur[tr - 8:], 8 - j, 0), pltpu.roll(next8, 8 - j, 0))
    return jnp.concatenate([pltpu.roll(cur, tr - j, 0)[:tr - 8], tail], axis=0)


def _conv(cur, prev, w, b):
    out = cur * w[3:4] + b
    for j in (1, 2, 3):
        out = out + _shift_down(cur, prev, j) * w[3 - j:4 - j]
    return out


def _conv_fwd(p, w_xs, b_xs, w_bc, b_bc, s):
    def fn(pos, xs, xs_prev, bc, bc_prev, w_xs, b_xs, w_bc, b_bc):
        first = pos == 0
        xs_prev = jnp.where(first, 0.0, xs_prev)
        bc_prev = jnp.where(first, 0.0, bc_prev)
        return _silu(_conv(xs, xs_prev, w_xs, b_xs)), _silu(_conv(bc, bc_prev, w_bc, b_bc))

    return _rowk("conv_fwd", fn, s, 256,
                 [(p, D, OFF_XS // D, 0), (p, D, OFF_XS // D, -1), (p, 512, OFF_BC // 512, 0), (p, 512, OFF_BC // 512, -1)],
                 [w_xs, b_xs, w_bc, b_bc], [(D, F32), (512, F32)], [])


def _conv_bwd(dxs_a, dbc_a, p, w_xs, b_xs, w_bc, b_bc, s):
    tr = 256
    n = s // tr

    def fn(pos, da1, da1n, x1, x1p, x1n, da2, da2n, x2, x2p, x2n, w1, b1, w2, b2, aw1, ab1, aw2, ab2):
        dx1, dw1, db1 = _conv_bwd_fn(pos, n, da1, da1n, x1, x1p, x1n, w1, b1)
        dx2, dw2, db2 = _conv_bwd_fn(pos, n, da2, da2n, x2, x2p, x2n, w2, b2)
        return dx1, dx2, aw1 + dw1, ab1 + db1, aw2 + dw2, ab2 + db2

    cx, cb = OFF_XS // D, OFF_BC // 512
    return _rowk("conv_bwd", fn, s, tr,
                 [(dxs_a, D, 0, 0), (dxs_a, D, 0, 1), (p, D, cx, 0), (p, D, cx, -1), (p, D, cx, 1),
                  (dbc_a, 512, 0, 0), (dbc_a, 512, 0, 1), (p, 512, cb, 0), (p, 512, cb, -1), (p, 512, cb, 1)],
                 [w_xs, b_xs, w_bc, b_bc], [(D, BF16), (512, BF16)], [(8, D), (1, D), (8, 512), (1, 512)])


def _conv_bwd_fn(pos, n, da, da_next, x, x_prev, x_next, w, b):
    first, last = pos == 0, pos == n - 1
    x_prev = jnp.where(first, 0.0, x_prev)
    dc = da * _dsilu(_conv(x, x_prev, w, b))
    dc_next = jnp.where(last, 0.0, da_next * _dsilu(_conv(x_next, x[x.shape[0] - 8:], w, b)))
    dx = dc * w[3:4]
    dws = [None] * 4
    dws[3] = _colsum(dc * x)
    for j in (1, 2, 3):
        dx = dx + _shift_up(dc, dc_next, j) * w[3 - j:4 - j]
        dws[3 - j] = _colsum(dc * _shift_down(x, x_prev, j))
    row = _iota((8, x.shape[1]), 0)
    dw = jnp.zeros((8, x.shape[1]), F32)
    for k in range(4):
        dw = jnp.where(row == k, dws[k], dw)
    return dx, dw, _colsum(dc)


def _ssd_gates(dtf, bias, a_log):
    lane = _iota(dtf.shape, 1)
    head = lane < NH
    dt = jnp.where(head, _softplus(dtf + bias), 0.0)
    a_neg = jnp.where(_iota(a_log.shape, 1) < NH, -jnp.exp(a_log), 0.0)
    a = dt * a_neg
    cs = _sel_left(_tri_lower(CHUNK), a)
    return dt, a_neg, cs


def _decay_mask(cs_ref, cst_ref, h):
    diff = cs_ref[:, h:h + 1] - cst_ref[h:h + 1, :]
    low = _iota((CHUNK, CHUNK), 1) <= _iota((CHUNK, CHUNK), 0)
    return jnp.where(low, jnp.exp(jnp.minimum(diff, 0.0)), 0.0)


def _ssd_fwd(xs_a, bc_a, p, bias128, alog128, dskip_x, s):
    nc = s // CHUNK
    t = CHUNK

    def body(xs_ref, bc_ref, dtf_ref, bias_ref, alog_ref, dsk_ref, y_ref, st_ref,
             state, x_sc, xw_sc, cs_sc, cst_sc, yd_sc):
        c = pl.program_id(0)

        @pl.when(c == 0)
        def _():
            state[...] = jnp.zeros(state.shape, F32)

        dt, _, cs = _ssd_gates(dtf_ref[...], bias_ref[...], alog_ref[...])
        cs_sc[...] = cs
        cst_sc[...] = cs.T
        cs_last = cs[t - 1:t, :]
        expand = _head_expand()
        ex = _sel_right(jnp.concatenate([dt, jnp.exp(cs), jnp.exp(cs_last - cs)], axis=0), expand, terms=2)
        dt_x, eo_x, we_x = ex[0:t], ex[t:2 * t], ex[2 * t:3 * t]
        g_x = _sel_right(jnp.broadcast_to(jnp.exp(cs_last), (8, 128)), expand)[0:1]
        xs = xs_ref[...]
        x = xs * dt_x
        x_sc[...] = x.astype(BF16)
        xw_sc[...] = (x * we_x).astype(BF16)
        prev = state[...]
        st_ref[0] = prev
        prev_b = prev.astype(BF16)
        for g in range(2):
            cols = slice(g * 512, (g + 1) * 512)
            b_g = bc_ref[:, g * 128:(g + 1) * 128].astype(BF16)
            c_g = bc_ref[:, 256 + g * 128:256 + (g + 1) * 128].astype(BF16)
            gmat = _dot(c_g, b_g, NT)
            y_off = _dot(c_g, prev_b[:, cols]) * eo_x[:, cols]
            s_loc = _dot(b_g, xw_sc[:, cols], TN)
            state[:, cols] = g_x[:, cols] * prev[:, cols] + s_loc
            for e in range(HG):
                h = g * HG + e
                m = gmat * _decay_mask(cs_sc, cst_sc, h)
                yd_sc[:, h * HD:(h + 1) * HD] = _dot(m.astype(BF16), x_sc[:, h * HD:(h + 1) * HD])
            y_ref[:, cols] = yd_sc[:, cols] + y_off + dsk_ref[:, cols] * xs[:, cols]

    return pl.pallas_call(
        body, name="ssd_fwd", grid=(nc,),
        in_specs=[pl.BlockSpec((t, D), lambda c: (c, 0)),
                  pl.BlockSpec((t, 512), lambda c: (c, 0)),
                  pl.BlockSpec((t, 128), lambda c: (c, OFF_DTF // 128)),
                  pl.BlockSpec((1, 128), lambda c: (0, 0)),
                  pl.BlockSpec((1, 128), lambda c: (0, 0)),
                  pl.BlockSpec((1, D), lambda c: (0, 0))],
        out_specs=[pl.BlockSpec((t, D), lambda c: (c, 0)),
                   pl.BlockSpec((1, NSTATE, D), lambda c: (c, 0, 0))],
        out_shape=[jax.ShapeDtypeStruct((s, D), F32), jax.ShapeDtypeStruct((nc, NSTATE, D), F32)],
        scratch_shapes=[pltpu.VMEM((NSTATE, D), F32), pltpu.VMEM((t, D), BF16), pltpu.VMEM((t, D), BF16),
                        pltpu.VMEM((t, 128), F32), pltpu.VMEM((128, t), F32), pltpu.VMEM((t, D), F32)],
        compiler_params=_params(("arbitrary",)),
    )(xs_a, bc_a, p, bias128, alog128, dskip_x)


def _ssd_bwd(dy, xs_a, bc_a, p, states, bias128, alog128, dskip_x, s):
    nc = s // CHUNK
    t = CHUNK

    def body(dy_ref, xs_ref, bc_ref, dtf_ref, st_ref, bias_ref, alog_ref, dsk_ref,
             dxs_ref, dbc_ref, ddt_ref, dalog_ref, dskip_ref,
             dstate, x_sc, dy_sc, dx_sc, deo_sc, dwe_sc, cs_sc, cst_sc, dcol_sc, drow_sc):
        i = pl.program_id(0)

        @pl.when(i == 0)
        def _():
            dstate[...] = jnp.zeros(dstate.shape, F32)
            dalog_ref[...] = jnp.zeros(dalog_ref.shape, F32)
            dskip_ref[...] = jnp.zeros(dskip_ref.shape, F32)

        dtf = dtf_ref[...]
        dt, a_neg, cs = _ssd_gates(dtf, bias_ref[...], alog_ref[...])
        cs_sc[...] = cs
        cst_sc[...] = cs.T
        cs_last = cs[t - 1:t, :]
        eo, we, g_end = jnp.exp(cs), jnp.exp(cs_last - cs), jnp.exp(cs_last)
        expand, reduce = _head_expand(), _head_reduce()
        ex = _sel_right(jnp.concatenate([dt, eo, we], axis=0), expand, terms=2)
        dt_x, eo_x, we_x = ex[0:t], ex[t:2 * t], ex[2 * t:3 * t]
        g_x = _sel_right(jnp.broadcast_to(g_end, (8, 128)), expand)[0:1]
        xs = xs_ref[...]
        dyv = dy_ref[...]
        x = xs * dt_x
        x_sc[...] = x.astype(BF16)
        dy_sc[...] = dyv.astype(BF16)
        dyo_b = (dyv * eo_x).astype(BF16)
        xw_b = (x * we_x).astype(BF16)
        prev = st_ref[0]
        prev_b = prev.astype(BF16)
        dnext = dstate[...]
        dnext_b = dnext.astype(BF16)
        dcol_sc[...] = jnp.zeros(dcol_sc.shape, F32)
        drow_sc[...] = jnp.zeros(drow_sc.shape, F32)
        lane_row = _iota((1, 128), 1)
        sub_col = _iota((128, 1), 0)
        for g in range(2):
            cols = slice(g * 512, (g + 1) * 512)
            b_g = bc_ref[:, g * 128:(g + 1) * 128].astype(BF16)
            c_g = bc_ref[:, 256 + g * 128:256 + (g + 1) * 128].astype(BF16)
            gmat = _dot(c_g, b_g, NT)
            b_ds = _dot(b_g, dnext_b[:, cols])
            c_s = _dot(c_g, prev_b[:, cols])
            dx_sc[:, cols] = b_ds * we_x[:, cols]
            deo_sc[:, cols] = dyv[:, cols] * c_s
            dwe_sc[:, cols] = b_ds * x[:, cols]
            db = _dot(xw_b[:, cols], dnext_b[:, cols], NT)
            dc = _dot(dyo_b[:, cols], prev_b[:, cols], NT)
            dstate[:, cols] = g_x[:, cols] * dnext[:, cols] + _dot(c_g, dyo_b[:, cols], TN)
            dg = jnp.zeros((t, t), F32)
            for e in range(HG):
                h = g * HG + e
                hc = slice(h * HD, (h + 1) * HD)
                lmat = _decay_mask(cs_sc, cst_sc, h)
                m = gmat * lmat
                dx_sc[:, hc] += _dot(m.astype(BF16), dy_sc[:, hc], TN)
                dm = _dot(dy_sc[:, hc], x_sc[:, hc], NT)
                dg = dg + dm * lmat
                qm = dm * m
                dcol_sc[...] += jnp.sum(qm, axis=1, keepdims=True) * (lane_row == h).astype(F32)
                drow_sc[...] += (sub_col == h).astype(F32) * jnp.sum(qm, axis=0, keepdims=True)
            dg_b = dg.astype(BF16)
            dbc_ref[:, g * 128:(g + 1) * 128] = db + _dot(dg_b, c_g, TN)
            dbc_ref[:, 256 + g * 128:256 + (g + 1) * 128] = dc + _dot(dg_b, b_g)
        d_eo = _sel_right(deo_sc[...], reduce, terms=2)
        d_we = _sel_right(dwe_sc[...], reduce, terms=2)
        d_gend = _sel_right(jnp.broadcast_to(_colsum(dnext * prev), (8, D)), reduce)[0:1]
        d_cs = dcol_sc[...] - drow_sc[...].T + d_eo * eo - d_we * we
        extra = _colsum(d_we * we) + d_gend * g_end
        d_cs = d_cs + jnp.where(_iota((t, 128), 0) == t - 1, extra, 0.0)
        da = _sel_left(_tri_upper(t), d_cs)
        dx = dx_sc[...]
        ddt = _sel_right(dx * xs, reduce, terms=2) + da * a_neg
        dxs_ref[...] = dx * dt_x + dsk_ref[...] * dyv
        ddt_ref[...] = jnp.where(_iota((t, 128), 1) < NH, ddt * _sigmoid(dtf + bias_ref[...]), 0.0)
        dalog_ref[...] += _colsum(da * dt) * a_neg
        dskip_ref[...] += _sel_right(jnp.broadcast_to(_colsum(dyv * xs), (8, D)), reduce)[0:1]

    rev = lambda i: nc - 1 - i
    return pl.pallas_call(
        body, name="ssd_bwd", grid=(nc,),
        in_specs=[pl.BlockSpec((t, D), lambda i: (rev(i), 0)),
                  pl.BlockSpec((t, D), lambda i: (rev(i), 0)),
                  pl.BlockSpec((t, 512), lambda i: (rev(i), 0)),
                  pl.BlockSpec((t, 128), lambda i: (rev(i), OFF_DTF // 128)),
                  pl.BlockSpec((1, NSTATE, D), lambda i: (rev(i), 0, 0)),
                  pl.BlockSpec((1, 128), lambda i: (0, 0)),
                  pl.BlockSpec((1, 128), lambda i: (0, 0)),
                  pl.BlockSpec((1, D), lambda i: (0, 0))],
        out_specs=[pl.BlockSpec((t, D), lambda i: (rev(i), 0)),
                   pl.BlockSpec((t, 512), lambda i: (rev(i), 0)),
                   pl.BlockSpec((t, 128), lambda i: (rev(i), 0)),
                   pl.BlockSpec((1, 128), lambda i: (0, 0)),
                   pl.BlockSpec((1, 128), lambda i: (0, 0))],
        out_shape=[jax.ShapeDtypeStruct((s, D), F32), jax.ShapeDtypeStruct((s, 512), F32),
                   jax.ShapeDtypeStruct((s, 128), F32), jax.ShapeDtypeStruct((1, 128), F32),
                   jax.ShapeDtypeStruct((1, 128), F32)],
        scratch_shapes=[pltpu.VMEM((NSTATE, D), F32), pltpu.VMEM((t, D), BF16), pltpu.VMEM((t, D), BF16),
                        pltpu.VMEM((t, D), F32), pltpu.VMEM((t, D), F32), pltpu.VMEM((t, D), F32),
                        pltpu.VMEM((t, 128), F32), pltpu.VMEM((128, t), F32),
                        pltpu.VMEM((t, 128), F32), pltpu.VMEM((128, t), F32)],
        compiler_params=_params(("arbitrary",)),
    )(dy, xs_a, bc_a, p, states, bias128, alog128, dskip_x)


def _gate_lanes(shape):
    lane = _iota(shape, 1)
    return (lane >= NH) & (lane < 2 * NH)


def _cum_fwd(p, bias128, s):
    tr = min(512, s)

    def body(dtf_ref, bias_ref, o_ref, carry):
        @pl.when(pl.program_id(0) == 0)
        def _():
            carry[...] = jnp.zeros(carry.shape, F32)

        lf = jnp.where(_gate_lanes((tr, 128)), _log_sigmoid(dtf_ref[...] + bias_ref[...]), 0.0)
        cum = _sel_left(_tri_lower(tr), lf) + carry[...]
        carry[...] = cum[tr - 1:tr, :]
        o_ref[...] = cum

    return pl.pallas_call(
        body, name="cum_fwd", grid=(s // tr,),
        in_specs=[pl.BlockSpec((tr, 128), lambda i: (i, OFF_DTF // 128)), pl.BlockSpec((1, 128), lambda i: (0, 0))],
        out_specs=pl.BlockSpec((tr, 128), lambda i: (i, 0)),
        out_shape=jax.ShapeDtypeStruct((s, 128), F32),
        scratch_shapes=[pltpu.VMEM((1, 128), F32)],
        compiler_params=_params(("arbitrary",)),
    )(p, bias128)


def _cum_bwd(dr_col, dcs, ddt_raw, p, bias128, s):
    tr = min(512, s)

    def fn(pos, dr, dc, ddt, dtf, bias, carry, acc):
        pick = (_iota((D, 128), 0) == (_iota((D, 128), 1) - NH) * HD).astype(BF16)
        dcum = dr - _sel_right(dc, pick)
        suffix = _sel_left(_tri_upper(tr), dcum) + carry
        dfr = jnp.where(_gate_lanes((tr, 128)), suffix * _sigmoid(-(dtf + bias)), 0.0)
        out = ddt + dfr
        return out, suffix[0:1, :], acc + _colsum(out)

    return _rowk("cum_bwd", fn, s, tr,
                 [(dr_col, 128, 0, 0), (dcs, D, 0, 0), (ddt_raw, 128, 0, 0), (p, 128, OFF_DTF // 128, 0)],
                 [bias128], [(128, BF16)], [(1, 128), (1, 128)], reverse=True)


ATT_BLOCK = 512
ATT_STRIP = 32


def _head_part(shape, h, dim):
    i = _iota(shape, dim)
    return (i >= h * HD) & (i < (h + 1) * HD)


def _k_augmented(k_blk, cum_blk, j, h):
    tk = k_blk.shape[0]
    lane = _iota((tk, 128), 1)
    col = jnp.sum(jnp.where(lane == NH + 2 * j + h, cum_blk, 0.0), axis=1, keepdims=True)
    c0, c1, c2 = [c.astype(F32) for c in _split3(-col)]
    aug = jnp.where(lane == 0, c0, jnp.where(lane == 1, c1, jnp.where(lane == 2, c2, 0.0)))
    return jnp.concatenate([jnp.where(_head_part((tk, 128), h, 1), k_blk, 0.0), aug], axis=1).astype(BF16)


def _q_augmented_t(q_blk):
    tq = q_blk.shape[0]
    ones = (_iota((128, tq), 0) < 3).astype(BF16)
    return jnp.concatenate([(q_blk * ATT_SCALE).T.astype(BF16), ones], axis=0)


def _rows01(r0, r1):
    sub = _iota((8, r0.shape[1]), 0)
    return jnp.where(sub == 0, r0, jnp.where(sub == 1, r1, 0.0))


def _fold8(x, op, cur):
    for g in range(x.shape[0] // 8):
        cur = op(cur, x[8 * g:8 * (g + 1), :])
    return cur


def _attn_fwd(p, cum, s):
    t = min(ATT_BLOCK, s)
    nq = s // t
    r = ATT_STRIP

    def body(q_ref, k_ref, v_ref, c_ref, o_ref, lse_ref, kaug_sc, vt_sc, s_sc, p_sc, m_sc, l_sc, acc_sc):
        j, qi = pl.program_id(0), pl.program_id(1)

        @pl.when(qi == 0)
        def _():
            for c in range(nq):
                rows = slice(c * t, (c + 1) * t)
                k_blk, vt = k_ref[rows, :], v_ref[rows, :].T
                for h in range(2):
                    kaug_sc[h, rows, :] = _k_augmented(k_blk, c_ref[rows, :], j, h)
                    vt_sc[h, :, rows] = jnp.where(_head_part((128, t), h, 0), vt, 0.0).astype(BF16)

        qaug_t = _q_augmented_t(q_ref[...])
        m_sc[...] = jnp.full(m_sc.shape, -1e30, F32)
        l_sc[...] = jnp.zeros(l_sc.shape, F32)
        acc_sc[...] = jnp.zeros(acc_sc.shape, F32)
        top = _iota((128, t), 0) < HD

        def logits(kb, buf):
            kv = pl.ds(pl.multiple_of(kb * t, t), t)
            for h in range(2):
                s_sc[buf, h] = _dot(kaug_sc[h, kv, :], qaug_t)

        def softmax(buf, diagonal):
            alphas = []
            for h in range(2):
                cur = jnp.full((8, t), -1e30, F32)
                for i in range(t // r):
                    rows = slice(i * r, (i + 1) * r)
                    x = s_sc[buf, h, rows, :]
                    if diagonal:
                        x = jnp.where(_iota((r, t), 1) >= i * r + _iota((r, t), 0), x, -1e30)
                        s_sc[buf, h, rows, :] = x
                    cur = _fold8(x, jnp.maximum, cur)
                m_prev = m_sc[h, 0:1, :]
                m_new = jnp.maximum(m_prev, jnp.max(cur, axis=0, keepdims=True))
                alpha = jnp.exp(m_prev - m_new)
                m_sc[h, 0:1, :] = m_new
                alphas.append(alpha)
                tot = jnp.zeros((8, t), F32)
                for i in range(t // r):
                    rows = slice(i * r, (i + 1) * r)
                    pr = jnp.exp(s_sc[buf, h, rows, :] - m_new)
                    p_sc[buf, h, rows, :] = pr.astype(BF16)
                    tot = _fold8(pr, jnp.add, tot)
                l_sc[h, 0:1, :] = alpha * l_sc[h, 0:1, :] + jnp.sum(tot, axis=0, keepdims=True)
            return alphas

        def accumulate(kb, buf, alphas):
            kv = pl.ds(pl.multiple_of(kb * t, t), t)
            acc_sc[...] = (acc_sc[...] * jnp.where(top, alphas[0], alphas[1])
                           + _dot(vt_sc[0, :, kv], p_sc[buf, 0]) + _dot(vt_sc[1, :, kv], p_sc[buf, 1]))

        def pair(a, b, b_diagonal):
            logits(a, 0)
            logits(b, 1)
            accumulate(a, 0, softmax(0, False))
            accumulate(b, 1, softmax(1, b_diagonal))

        def earlier(u, carry):
            pair(2 * u, 2 * u + 1, False)
            return carry

        lax.fori_loop(0, qi // 2, earlier, 0)

        @pl.when(qi % 2 == 1)
        def _():
            pair(qi - 1, qi, True)

        @pl.when(qi % 2 == 0)
        def _():
            logits(qi, 0)
            accumulate(qi, 0, softmax(0, True))

        l0, l1 = l_sc[0, 0:1, :], l_sc[1, 0:1, :]
        o_ref[...] = (acc_sc[...] / jnp.where(top, l0, l1)).T
        lse_ref[0] = _rows01(m_sc[0, 0:1, :] + jnp.log(l0), m_sc[1, 0:1, :] + jnp.log(l1))

    return pl.pallas_call(
        body, name="attn_fwd", grid=(NH // 2, nq),
        in_specs=[pl.BlockSpec((t, 128), lambda j, qi: (qi, OFF_Q // 128 + j)),
                  pl.BlockSpec((s, 128), lambda j, qi: (0, OFF_K // 128 + j)),
                  pl.BlockSpec((s, 128), lambda j, qi: (0, OFF_V // 128 + j)),
                  pl.BlockSpec((s, 128), lambda j, qi: (0, 0))],
        out_specs=[pl.BlockSpec((t, 128), lambda j, qi: (qi, j)),
                   pl.BlockSpec((1, 8, t), lambda j, qi: (j, 0, qi))],
        out_shape=[jax.ShapeDtypeStruct((s, D), F32), jax.ShapeDtypeStruct((NH // 2, 8, s), F32)],
        scratch_shapes=[pltpu.VMEM((2, s, 256), BF16), pltpu.VMEM((2, 128, s), BF16), pltpu.VMEM((2, 2, t, t), F32),
                        pltpu.VMEM((2, 2, t, t), BF16), pltpu.VMEM((2, 8, t), F32), pltpu.VMEM((2, 8, t), F32),
                        pltpu.VMEM((128, t), F32)],
        compiler_params=_params(("parallel", "arbitrary")),
    )(p, p, p, cum)


def _attn_bwd(p, cum, o, lse, do, s):
    t = min(ATT_BLOCK, s)
    nq = s // t
    r = ATT_STRIP

    def body(q_ref, k_ref, v_ref, c_ref, o_ref, lse_ref, do_ref, dq_ref, dk_ref, dv_ref, dc_ref, dr_ref,
             qaugt_sc, qh_sc, dot_sc, doh_sc, delta_sc, dqt_sc, dr_sc, kaug_sc, vh_sc, kt_sc, s_sc, dp_sc, p_sc, ds_sc,
             dk_sc, dv_sc, dc_sc):
        j, ki = pl.program_id(0), pl.program_id(1)

        @pl.when(ki == 0)
        def _():
            for c in range(nq):
                rows = slice(c * t, (c + 1) * t)
                q_blk, do_blk = q_ref[rows, :], do_ref[rows, :]
                qaugt_sc[:, rows] = _q_augmented_t(q_blk)
                dot_sc[:, rows] = do_blk.T.astype(BF16)
                prod_t = (do_blk * o_ref[rows, :]).T
                delta_sc[:, rows] = _rows01(jnp.sum(prod_t[0:HD], axis=0, keepdims=True),
                                            jnp.sum(prod_t[HD:], axis=0, keepdims=True))
                for h in range(2):
                    head = _head_part((t, 128), h, 1)
                    qh_sc[h, rows, :] = jnp.where(head, q_blk * ATT_SCALE, 0.0).astype(BF16)
                    doh_sc[h, rows, :] = jnp.where(head, do_blk, 0.0).astype(BF16)
            dqt_sc[...] = jnp.zeros(dqt_sc.shape, F32)
            dr_sc[...] = jnp.zeros(dr_sc.shape, F32)

        k_blk, v_blk = k_ref[...], v_ref[...]
        kt = k_blk.T
        for h in range(2):
            kaug_sc[h] = _k_augmented(k_blk, c_ref[...], j, h)
            vh_sc[h] = jnp.where(_head_part((t, 128), h, 1), v_blk, 0.0).astype(BF16)
            kt_sc[h] = jnp.where(_head_part((128, t), h, 0), kt, 0.0).astype(BF16)
        dk_sc[...] = jnp.zeros(dk_sc.shape, F32)
        dv_sc[...] = jnp.zeros(dv_sc.shape, F32)
        dc_sc[...] = jnp.zeros(dc_sc.shape, F32)

        def inputs(qb, buf):
            qs = pl.ds(pl.multiple_of(qb * t, t), t)
            for h in range(2):
                s_sc[buf, h] = _dot(kaug_sc[h], qaugt_sc[:, qs])
                dp_sc[buf, h] = _dot(vh_sc[h], dot_sc[:, qs])

        def elementwise(qb, buf, diagonal):
            qs = pl.ds(pl.multiple_of(qb * t, t), t)
            for h in range(2):
                lse_row, delta_row = lse_ref[0, h:h + 1, qs], delta_sc[h:h + 1, qs]
                tot = jnp.zeros((8, t), F32)
                for i in range(t // r):
                    rows = slice(i * r, (i + 1) * r)
                    x = s_sc[buf, h, rows, :]
                    if diagonal:
                        x = jnp.where(_iota((r, t), 1) >= i * r + _iota((r, t), 0), x, -1e30)
                    pr = jnp.exp(x - lse_row)
                    ds = pr * (dp_sc[buf, h, rows, :] - delta_row)
                    p_sc[buf, h, rows, :] = pr.astype(BF16)
                    ds_sc[buf, h, rows, :] = ds.astype(BF16)
                    dc_sc[h, rows, :] += sum(ds[:, 128 * g:128 * (g + 1)] for g in range(t // 128))
                    tot = _fold8(ds, jnp.add, tot)
                dr_sc[h, :, qs] += tot

        def outputs(qb, buf):
            qs = pl.ds(pl.multiple_of(qb * t, t), t)
            dv_sc[...] += _dot(p_sc[buf, 0], doh_sc[0, qs, :]) + _dot(p_sc[buf, 1], doh_sc[1, qs, :])
            dk_sc[...] += _dot(ds_sc[buf, 0], qh_sc[0, qs, :]) + _dot(ds_sc[buf, 1], qh_sc[1, qs, :])
            dqt_sc[:, qs] += _dot(kt_sc[0], ds_sc[buf, 0]) + _dot(kt_sc[1], ds_sc[buf, 1])

        def pair(a, b, a_diagonal):
            inputs(a, 0)
            inputs(b, 1)
            elementwise(a, 0, a_diagonal)
            outputs(a, 0)
            elementwise(b, 1, False)
            outputs(b, 1)

        def later(u, carry):
            pair(ki + 1 + 2 * u, ki + 2 + 2 * u, False)
            return carry

        n_later = nq - 1 - ki
        lax.fori_loop(0, n_later // 2, later, 0)

        @pl.when(n_later % 2 == 1)
        def _():
            pair(ki, nq - 1, True)

        @pl.when(n_later % 2 == 0)
        def _():
            inputs(ki, 0)
            elementwise(ki, 0, True)
            outputs(ki, 0)

        dk_ref[...] = dk_sc[...].astype(BF16)
        dv_ref[...] = dv_sc[...].astype(BF16)
        dc_ref[...] = jnp.where(_iota((t, 128), 1) < HD, jnp.sum(dc_sc[0], axis=1, keepdims=True),
                                jnp.sum(dc_sc[1], axis=1, keepdims=True))

        @pl.when(ki == nq - 1)
        def _():
            for c in range(nq):
                rows = slice(c * t, (c + 1) * t)
                dq_ref[rows, :] = dqt_sc[:, rows].T * ATT_SCALE
            dr_ref[0] = _rows01(jnp.sum(dr_sc[0], axis=0, keepdims=True), jnp.sum(dr_sc[1], axis=0, keepdims=True))

    whole = lambda off: pl.BlockSpec((s, 128), functools.partial(lambda j, ki, off: (0, off + j), off=off))
    return pl.pallas_call(
        body, name="attn_bwd", grid=(NH // 2, nq),
        in_specs=[whole(OFF_Q // 128),
                  pl.BlockSpec((t, 128), lambda j, ki: (ki, OFF_K // 128 + j)),
                  pl.BlockSpec((t, 128), lambda j, ki: (ki, OFF_V // 128 + j)),
                  pl.BlockSpec((t, 128), lambda j, ki: (ki, 0)),
                  whole(0),
                  pl.BlockSpec((1, 8, s), lambda j, ki: (j, 0, 0)),
                  whole(0)],
        out_specs=[whole(0),
                   pl.BlockSpec((t, 128), lambda j, ki: (ki, j)),
                   pl.BlockSpec((t, 128), lambda j, ki: (ki, j)),
                   pl.BlockSpec((t, 128), lambda j, ki: (ki, j)),
                   pl.BlockSpec((1, 8, s), lambda j, ki: (j, 0, 0))],
        out_shape=[jax.ShapeDtypeStruct((s, D), F32), jax.ShapeDtypeStruct((s, D), BF16), jax.ShapeDtypeStruct((s, D), BF16),
                   jax.ShapeDtypeStruct((s, D), F32), jax.ShapeDtypeStruct((NH // 2, 8, s), F32)],
        scratch_shapes=[pltpu.VMEM((256, s), BF16), pltpu.VMEM((2, s, 128), BF16), pltpu.VMEM((128, s), BF16),
                        pltpu.VMEM((2, s, 128), BF16), pltpu.VMEM((8, s), F32), pltpu.VMEM((128, s), F32),
                        pltpu.VMEM((2, 8, s), F32), pltpu.VMEM((2, t, 256), BF16), pltpu.VMEM((2, t, 128), BF16),
                        pltpu.VMEM((2, 128, t), BF16), pltpu.VMEM((2, 2, t, t), F32), pltpu.VMEM((2, 2, t, t), F32),
                        pltpu.VMEM((2, 2, t, t), BF16), pltpu.VMEM((2, 2, t, t), BF16), pltpu.VMEM((t, 128), F32),
                        pltpu.VMEM((t, 128), F32), pltpu.VMEM((2, t, 128), F32)],
        compiler_params=_params(("parallel", "arbitrary")),
    )(p, p, p, cum, o, lse, do)


def _ln_stats(u):
    mu = _mean(u)
    d = u - mu
    rstd = lax.rsqrt(_mean(d * d) + EPS)
    return d * rstd, rstd


def _ln_bwd(dx, xh, rstd, gam):
    dxh = dx * gam
    return rstd * (dxh - _mean(dxh) - xh * _mean(dxh * xh))


def _rms_bwd(d, xn, r, w):
    t = d * w
    return r * (t - xn * _mean(t * xn)), _colsum(d * xn)


def _mix_norm(y, p, att, w_ssm, w_att, s):
    def fn(pos, y, z, att, w1, w2):
        g = y * _silu(z)
        n1 = g * lax.rsqrt(_mean(g * g) + EPS) * w1
        n2 = att * lax.rsqrt(_mean(att * att) + EPS) * w2
        return (jnp.concatenate([n1, n2], axis=1),)

    return _rowk("mix_norm", fn, s, 256, [(y, D, 0, 0), (p, D, OFF_Z // D, 0), (att, D, 0, 0)],
                 [w_ssm, w_att], [(2 * D, BF16)], [])[0]


def _mix_norm_bwd(dmix, y, p, att, w_ssm, w_att, s):
    def fn(pos, dmix, y, z, att, w1, w2, a1, a2):
        sz = _silu(z)
        g = y * sz
        r1 = lax.rsqrt(_mean(g * g) + EPS)
        dg, dw1 = _rms_bwd(dmix[:, :D], g * r1, r1, w1)
        r2 = lax.rsqrt(_mean(att * att) + EPS)
        datt, dw2 = _rms_bwd(dmix[:, D:], att * r2, r2, w2)
        return dg * sz, dg * y * _dsilu(z), datt, a1 + dw1, a2 + dw2

    return _rowk("mix_norm_bwd", fn, s, 256, [(dmix, 2 * D, 0, 0), (y, D, 0, 0), (p, D, OFF_Z // D, 0), (att, D, 0, 0)],
                 [w_ssm, w_att], [(D, F32), (D, BF16), (D, F32)], [(1, D), (1, D)])


def _ln1(x0, y, g1, gam, bet, sc2, sh2, s):
    def fn(pos, x0, y, g1, gam, bet, sc2, sh2):
        xh, _ = _ln_stats(ALPHA * x0 + (1.0 + g1) * y)
        x1 = xh * gam + bet
        return x1, _modulate(x1, sc2, sh2)

    return _rowk("ln1", fn, s, 256, [(x0, D, 0, 0), (y, D, 0, 0)], [g1, gam, bet, sc2, sh2], [(D, F32), (D, BF16)], [])


def _ln2_loss(x1, ff, tgt, g2, gam, bet, s):
    def fn(pos, x1, ff, tgt, g2, gam, bet, a_loss, a_dgam, a_dbet, a_dg2):
        xh, rstd = _ln_stats(ALPHA * x1 + (1.0 + g2) * ff)
        err = xh * gam + bet - tgt
        dx2 = err * (1.0 / D)
        du = _ln_bwd(dx2, xh, rstd, gam)
        return (du, du * (1.0 + g2), a_loss + _colsum(err * err), a_dgam + _colsum(dx2 * xh),
                a_dbet + _colsum(dx2), a_dg2 + _colsum(du * ff))

    return _rowk("ln2_loss", fn, s, 256, [(x1, D, 0, 0), (ff, D, 0, 0), (tgt, D, 0, 0)], [g2, gam, bet],
                 [(D, F32), (D, BF16)], [(1, D)] * 4)


def _ln1_bwd(dh2, du2, x0, y, g1, gam, bet, sc2, s):
    def fn(pos, dh2, du2, x0, y, g1, gam, bet, sc2, a_sc, a_sh, a_gam, a_bet, a_g1):
        xh, rstd = _ln_stats(ALPHA * x0 + (1.0 + g1) * y)
        x1 = xh * gam + bet
        dx1 = ALPHA * du2 + dh2 * (1.0 + sc2)
        du1 = _ln_bwd(dx1, xh, rstd, gam)
        return (du1, du1 * (1.0 + g1), a_sc + _colsum(dh2 * x1), a_sh + _colsum(dh2), a_gam + _colsum(dx1 * xh),
                a_bet + _colsum(dx1), a_g1 + _colsum(du1 * y))

    return _rowk("ln1_bwd", fn, s, 256, [(dh2, D, 0, 0), (du2, D, 0, 0), (x0, D, 0, 0), (y, D, 0, 0)],
                 [g1, gam, bet, sc2], [(D, F32), (D, BF16)], [(1, D)] * 5)


def _input_grad(dh1, du1, x0, sc1, s):
    def fn(pos, dh1, du1, x0, sc1, a_sc, a_sh):
        return ALPHA * du1 + dh1 * (1.0 + sc1), a_sc + _colsum(dh1 * x0), a_sh + _colsum(dh1)

    return _rowk("input_grad", fn, s, 256, [(dh1, D, 0, 0), (du1, D, 0, 0), (x0, D, 0, 0)], [sc1],
                 [(D, F32)], [(1, D)] * 2)


def _adamw(name, w, g, m, v, *, tr, slots):
    r, c = w.shape

    def body(w_ref, g_ref, m_ref, v_ref, g_out, d_out, m_out, v_out):
        if slots:
            grad = g_ref[0][:, :c].astype(F32)
            for k in range(1, N_DEV):
                grad = grad + g_ref[k][:, :c].astype(F32)
        else:
            grad = g_ref[...]
        m_new = ADAM_B1 * m_ref[...] + (1.0 - ADAM_B1) * grad
        v_new = ADAM_B2 * v_ref[...] + (1.0 - ADAM_B2) * (grad * grad)
        m_hat = m_new / (1.0 - ADAM_B1 ** ADAM_STEP)
        v_hat = v_new / (1.0 - ADAM_B2 ** ADAM_STEP)
        g_out[...] = grad
        d_out[...] = -ADAM_LR * (m_hat / (jnp.sqrt(v_hat) + ADAM_EPS) + ADAM_WD * w_ref[...])
        m_out[...] = m_new
        v_out[...] = v_new

    tile = pl.BlockSpec((tr, c), lambda i: (i, 0))
    g_spec = pl.BlockSpec((N_DEV, tr, g.shape[-1]), lambda i: (0, i, 0)) if slots else tile
    return pl.pallas_call(
        body, name=name, grid=(r // tr,),
        in_specs=[tile, g_spec, tile, tile], out_specs=[tile] * 4,
        out_shape=[jax.ShapeDtypeStruct((r, c), F32)] * 4,
        compiler_params=_params(("parallel",)),
    )(w, g, m, v)


def _dot_f32(a, b, dims=NN):
    a0, a1, a2 = _split3(a)
    b0, b1, b2 = _split3(b)
    acc = _dot(a0, b0, dims)
    for x, y in ((a0, b1), (a1, b0), (a1, b1), (a0, b2), (a2, b0)):
        acc = acc + _dot(x, y, dims)
    return acc


def _ada_mod(c_all, w_shard, b_shard):
    def body(c_ref, w_ref, b_ref, o_ref):
        act = _silu(c_ref[...])
        act16 = jnp.concatenate([act, jnp.zeros_like(act)], axis=0)
        o_ref[...] = _dot_f32(act16, w_ref[...])[0:N_DEV] + b_ref[...]

    return pl.pallas_call(
        body, name="ada_mod", out_shape=jax.ShapeDtypeStruct((N_DEV, w_shard.shape[1]), F32),
        compiler_params=_params(None),
    )(c_all, w_shard, b_shard)


def _ada_grad(c_all, dmod_cols, dmod_all):
    def body(c_ref, dc_ref, da_ref, gw_ref, gb_ref):
        act = _silu(c_ref[...])
        act16 = jnp.concatenate([act, jnp.zeros_like(act)], axis=0)
        dm = dc_ref[...]
        dm16 = jnp.concatenate([dm, jnp.zeros_like(dm)], axis=0)
        gw_ref[...] = _dot_f32(act16, dm16, TN)
        gb_ref[...] = _colsum(da_ref[...])

    return pl.pallas_call(
        body, name="ada_grad",
        out_shape=[jax.ShapeDtypeStruct((D, dmod_cols.shape[1]), F32), jax.ShapeDtypeStruct((1, 6 * D), F32)],
        compiler_params=_params(None),
    )(c_all, dmod_cols, dmod_all)


def _sum_slots(name, g):
    def body(g_ref, o_ref):
        acc = g_ref[0]
        for k in range(1, N_DEV):
            acc = acc + g_ref[k]
        o_ref[...] = acc

    return pl.pallas_call(body, name=name, out_shape=jax.ShapeDtypeStruct(g.shape[1:], F32),
                          compiler_params=_params(None))(g)


def _exchange(name, xs, scatter):
    n = len(xs)
    n_peer = N_DEV - 1

    def body(*refs):
        x_refs, o_refs = refs[:n], refs[n:2 * n]
        send_sems, recv_sems, local_sems = refs[2 * n:]
        mx, my, mc = lax.axis_index("x"), lax.axis_index("y"), lax.axis_index("c")
        me = 4 * mx + 2 * my + mc

        def src(a, slot):
            return x_refs[a].at[slot] if scatter else x_refs[a]

        own = [pltpu.make_async_copy(src(a, me), o_refs[a].at[me], local_sems.at[a]) for a in range(n)]
        for cp in own:
            cp.start()
        sends = []
        for d in range(1, N_DEV):
            px = 1 - mx if d & 4 else mx
            py = 1 - my if d & 2 else my
            pc = 1 - mc if d & 1 else mc
            peer = 4 * px + 2 * py + pc
            for a in range(n):
                def copy(src_slot, dst_slot, a=a, d=d, to=(px, py, pc)):
                    return pltpu.make_async_remote_copy(
                        src_ref=src(a, src_slot), dst_ref=o_refs[a].at[dst_slot],
                        send_sem=send_sems.at[a * n_peer + d - 1], recv_sem=recv_sems.at[a * n_peer + d - 1],
                        device_id=to, device_id_type=pl.DeviceIdType.MESH)

                out = copy(peer, me)
                out.start()
                sends.append((out, copy(me, peer)))
        for _, arrival in sends:
            arrival.wait_recv()
        for out, _ in sends:
            out.wait_send()
        for cp in own:
            cp.wait()

    shapes = [tuple(x.shape[1:] if scatter else x.shape) for x in xs]
    return pl.pallas_call(
        body, name=name,
        in_specs=[pl.BlockSpec(memory_space=pl.ANY)] * n, out_specs=[pl.BlockSpec(memory_space=pl.ANY)] * n,
        out_shape=[jax.ShapeDtypeStruct((N_DEV,) + sh, x.dtype) for sh, x in zip(shapes, xs)],
        scratch_shapes=[pltpu.SemaphoreType.DMA((n * n_peer,)), pltpu.SemaphoreType.DMA((n * n_peer,)),
                        pltpu.SemaphoreType.DMA((n,))],
        compiler_params=pltpu.CompilerParams(has_side_effects=True),
    )(*xs)


def _gather_two_level(name, x):
    def body(x_ref, o_ref, send_sems, recv_sems, local_sem):
        mx, my, mc = lax.axis_index("x"), lax.axis_index("y"), lax.axis_index("c")
        me, sibling = (mx, my, mc), (mx, my, 1 - mc)
        chips = [(1 - mx, my), (mx, 1 - my), (1 - mx, 1 - my)]

        def slot(px, py, pc):
            return o_ref.at[4 * px + 2 * py + pc]

        def copy(k, block, to, src=None):
            return pltpu.make_async_remote_copy(
                src_ref=slot(*block) if src is None else src, dst_ref=slot(*block),
                send_sem=send_sems.at[k], recv_sem=recv_sems.at[k], device_id=to, device_id_type=pl.DeviceIdType.MESH)

        mine = pltpu.make_async_copy(x_ref, slot(*me), local_sem)
        mine.start()
        first = [copy(0, me, sibling, src=x_ref)] + [copy(1 + i, me, (*chip, mc), src=x_ref) for i, chip in enumerate(chips)]
        for cp in first:
            cp.start()
        passed = [copy(4 + i, (*chip, mc), sibling) for i, chip in enumerate(chips)]
        for i, chip in enumerate(chips):
            copy(1 + i, (*chip, mc), me).wait_recv()
            passed[i].start()
        copy(0, sibling, me).wait_recv()
        for i, chip in enumerate(chips):
            copy(4 + i, (*chip, 1 - mc), me).wait_recv()
        for cp in first + passed:
            cp.wait_send()
        mine.wait()

    return pl.pallas_call(
        body, name=name,
        in_specs=[pl.BlockSpec(memory_space=pl.ANY)], out_specs=pl.BlockSpec(memory_space=pl.ANY),
        out_shape=jax.ShapeDtypeStruct((N_DEV,) + tuple(x.shape), x.dtype),
        scratch_shapes=[pltpu.SemaphoreType.DMA((7,)), pltpu.SemaphoreType.DMA((7,)), pltpu.SemaphoreType.DMA(())],
        compiler_params=pltpu.CompilerParams(has_side_effects=True),
    )(x)


def _after(x, zero):
    return x if zero is None else x + zero.reshape(-1)[0].astype(x.dtype)


_HBM = pl.BlockSpec(memory_space=pltpu.HBM)
_SEM = pl.BlockSpec(memory_space=pltpu.SEMAPHORE)


def _exchange_copies(x_refs, land_refs, send_sems, recv_sems, scatter):
    n = len(x_refs)
    n_peer = N_DEV - 1
    mx, my, mc = lax.axis_index("x"), lax.axis_index("y"), lax.axis_index("c")
    me = 4 * mx + 2 * my + mc
    pairs = []
    for d in range(1, N_DEV):
        px = 1 - mx if d & 4 else mx
        py = 1 - my if d & 2 else my
        pc = 1 - mc if d & 1 else mc
        peer = 4 * px + 2 * py + pc
        for a in range(n):
            def copy(src_slot, dst_slot, a=a, d=d, to=(px, py, pc)):
                return pltpu.make_async_remote_copy(
                    src_ref=x_refs[a].at[src_slot] if scatter else x_refs[a], dst_ref=land_refs[a].at[dst_slot],
                    send_sem=send_sems.at[a * n_peer + d - 1], recv_sem=recv_sems.at[a * n_peer + d - 1],
                    device_id=to, device_id_type=pl.DeviceIdType.MESH)

            pairs.append((copy(peer, me), copy(me, peer)))
    return me, pairs


def _exchange_async(name, xs, scatter, collective_id):
    n = len(xs)
    shapes = [tuple(x.shape[1:] if scatter else x.shape) for x in xs]
    x_refs = [jax.new_ref(x, memory_space=pltpu.MemorySpace.HBM) for x in xs]
    land_refs = [jax.empty_ref(jax.ShapeDtypeStruct((N_DEV,) + sh, x.dtype), memory_space=pltpu.MemorySpace.HBM)
                 for sh, x in zip(shapes, xs)]

    @pl.kernel(mesh=plsc.ScalarSubcoreMesh(axis_name="sequencer", num_cores=1), name=name,
               scratch_types=(pltpu.SemaphoreType.DMA((n * (N_DEV - 1),)), pltpu.SemaphoreType.DMA((n * (N_DEV - 1),)),
                              pltpu.SemaphoreType.DMA((n,))),
               compiler_params=pltpu.CompilerParams(collective_id=collective_id))
    def launch(send_sems, recv_sems, own_sems):
        barrier = pltpu.get_barrier_semaphore()
        mx, my, mc = lax.axis_index("x"), lax.axis_index("y"), lax.axis_index("c")
        for d in range(1, N_DEV):
            peer = (1 - mx if d & 4 else mx, 1 - my if d & 2 else my, 1 - mc if d & 1 else mc)
            pl.semaphore_signal(barrier, inc=1, device_id=peer, device_id_type=pl.DeviceIdType.MESH)
        pl.semaphore_wait(barrier, N_DEV - 1)
        me, pairs = _exchange_copies(x_refs, land_refs, send_sems, recv_sems, scatter)
        own = [pltpu.make_async_copy(x_refs[a].at[me] if scatter else x_refs[a], land_refs[a].at[me], own_sems.at[a])
               for a in range(n)]
        for cp in own:
            cp.start()
        for out, _ in pairs:
            out.start()
        for out, arrival in pairs:
            arrival.wait_recv()
            out.wait_send()
        for cp in own:
            cp.wait()

    launch()
    return lambda: [r[...] for r in land_refs]


def _exchange_start(name, xs, scatter):
    n = len(xs)
    shapes = [tuple(x.shape[1:] if scatter else x.shape) for x in xs]

    def body(*refs):
        x_refs, land_refs = refs[:n], refs[n:2 * n]
        send_sems, recv_sems = refs[2 * n], refs[2 * n + 1]
        token, own_sems = refs[4 * n + 2], refs[4 * n + 3]
        me, pairs = _exchange_copies(x_refs, land_refs, send_sems, recv_sems, scatter)
        own = [pltpu.make_async_copy(x_refs[a].at[me] if scatter else x_refs[a], land_refs[a].at[me], own_sems.at[a])
               for a in range(n)]
        for cp in own:
            cp.start()
        for out, _ in pairs:
            out.start()
        for cp in own:
            cp.wait()
        token[...] = jnp.zeros(token.shape, token.dtype)

    lands = [pltpu.with_memory_space_constraint(lax.empty((N_DEV,) + sh, x.dtype), pltpu.HBM) for sh, x in zip(shapes, xs)]
    res = pl.pallas_call(
        body, name=name,
        out_shape=(pltpu.SemaphoreType.DMA((n * (N_DEV - 1),)), pltpu.SemaphoreType.DMA((n * (N_DEV - 1),)),
                   *[pltpu.HBM(x.shape, x.dtype) for x in xs], *[pltpu.HBM(l.shape, l.dtype) for l in lands],
                   jax.ShapeDtypeStruct((8, 128), F32)),
        in_specs=[_HBM] * (2 * n),
        out_specs=(_SEM, _SEM, *[_HBM] * (2 * n), pl.BlockSpec(memory_space=pltpu.VMEM)),
        input_output_aliases={i: 2 + i for i in range(2 * n)},
        scratch_shapes=[pltpu.SemaphoreType.DMA((n,))],
        compiler_params=pltpu.CompilerParams(has_side_effects=pltpu.SideEffectType.DATAFLOW_SIDE_EFFECTING),
    )(*[pltpu.with_memory_space_constraint(x, pltpu.HBM) for x in xs], *lands)
    return dict(send=res[0], recv=res[1], xs=list(res[2:2 + n]), lands=list(res[2 + n:2 + 2 * n]), token=res[2 + 2 * n])


def _exchange_wait(name, handle, after, scatter):
    n = len(handle['xs'])

    def body(*refs):
        x_refs, land_refs = refs[:n], refs[n:2 * n]
        send_sems, recv_sems = refs[2 * n], refs[2 * n + 1]
        _, pairs = _exchange_copies(x_refs, land_refs, send_sems, recv_sems, scatter)
        for out, arrival in pairs:
            out.wait_send()
            arrival.wait_recv()

    res = pl.pallas_call(
        body, name=name,
        out_shape=tuple(pltpu.HBM(a.shape, a.dtype) for a in handle['xs'] + handle['lands']),
        in_specs=[_HBM] * (2 * n) + [_SEM, _SEM, pl.BlockSpec(memory_space=pl.ANY)],
        out_specs=tuple([_HBM] * (2 * n)),
        input_output_aliases={i: i for i in range(2 * n)},
        compiler_params=pltpu.CompilerParams(has_side_effects=pltpu.SideEffectType.DATAFLOW_SIDE_EFFECTING),
    )(*handle['xs'], *handle['lands'], handle['send'], handle['recv'], after)
    return list(res[n:])


def _relu2(a):
    r = jnp.maximum(a, 0.0)
    return r * r


def _relu2_grad(acc, a):
    return acc * (2.0 * jnp.maximum(a, 0.0))


def _local_step(x0, tgt, mod, wcat, late_weights, send_grads, conv_w, conv_b, dt_bias, a_log, d_skip, ssm_norm_w, f_bias,
                attn_norm_w, ln1_g, ln1_b, ln2_g, ln2_b):
    ff_w = DFF // N_DEV
    s = x0.shape[0]
    tm = min(1024, s)
    ts = min(1024, s)
    sh1, sc1, g1, sh2, sc2, g2 = [mod[:, i * D:(i + 1) * D] for i in range(6)]
    zero = jnp.zeros((1, 128 - 2 * NH), F32)
    bias128 = jnp.concatenate([dt_bias, f_bias, zero], axis=1)
    alog128 = jnp.concatenate([a_log, jnp.zeros((1, 128 - NH), F32)], axis=1)
    dskip_x = jnp.repeat(d_skip, HD, axis=1)
    w_xs, w_bc, b_xs, b_bc = conv_w[:, :D], conv_w[:, D:], conv_b[:, :D], conv_b[:, D:]

    p = _mm_nn("in_proj", x0, wcat, tm=tm, tn=1152, tk=D, out_dtype=F32, pro=_modulate, aux=(sc1, sh1))
    xs_a, bc_a = _conv_fwd(p, w_xs, b_xs, w_bc, b_bc, s)
    y_ssd, states = _ssd_fwd(xs_a, bc_a, p, bias128, alog128, dskip_x, s)
    cum = _cum_fwd(p, bias128, s)
    att, lse = _attn_fwd(p, cum, s)
    wout, w1s, w2 = late_weights(lse)
    ymix = _mix_norm(y_ssd, p, att, ssm_norm_w, attn_norm_w, s)
    y = _mm_nn("out_proj", ymix, wout, tm=tm, tn=1024, tk=2 * D, out_dtype=F32)
    x1, h2 = _ln1(x0, y, g1, ln1_g, ln1_b, sc2, sh2, s)
    a1 = _mm_nn("ff_in", h2, w1s, tm=tm, tn=ff_w, tk=D, out_dtype=F32)
    ff = _mm_nn("ff_out", a1, w2, tm=tm, tn=1024, tk=1024, out_dtype=F32, pro=_relu2)
    du2, dff, sq_err, d_ln2_g, d_ln2_b, d_g2 = _ln2_loss(x1, ff, tgt, g2, ln2_g, ln2_b, s)

    da1 = _mm_nt("d_ff_hidden", [(dff, D, 0)], [(w2, D, 0)], n=DFF, tm=tm, tn=1024, out_dtype=BF16, epi=_relu2_grad,
                 epi_aux=(a1,))
    d_w2 = _mm_tn("d_w_ff_out", a1, dff, tm=1024, tn=1024, ts=ts, pro=_relu2)
    d_w1s = _mm_tn("d_w_ff_in", h2, da1, tm=1024, tn=ff_w, ts=ts, col_shards=True)
    dh2 = _mm_nt("d_ff_input", [(da1, ff_w, k) for k in range(N_DEV)], [(w1s, ff_w, k) for k in range(N_DEV)], n=D,
                 tm=min(512, s), tn=1024, out_dtype=F32)
    du1, dy, d_sc2, d_sh2, d_ln1_g, d_ln1_b, d_g1 = _ln1_bwd(dh2, du2, x0, y, g1, ln1_g, ln1_b, sc2, s)

    dmix = _mm_nt("d_mix", [(dy, D, 0)], [(wout, D, 0)], n=2 * D, tm=tm, tn=1024, out_dtype=F32)
    d_wout = _mm_tn("d_w_out", ymix, dy, tm=1024, tn=1024, ts=ts)
    sent = send_grads("late", [d_w1s, d_w2.reshape(N_DEV, -1, D), d_wout.reshape(N_DEV, -1, D)])
    dy_ssd, dz, datt, d_ssm_w, d_attn_w = _mix_norm_bwd(dmix, y_ssd, p, att, _after(ssm_norm_w, sent), attn_norm_w, s)
    dq, dk, dv, dcs, drs = _attn_bwd(p, cum, att, lse, datt, s)
    dxs_a, dbc_a, ddt_raw, d_alog, d_dskip = _ssd_bwd(dy_ssd, xs_a, bc_a, p, states, bias128, alog128, dskip_x, s)
    dr_col = jnp.pad(drs[:, :2, :].reshape(NH, s).T, ((0, 0), (NH, 128 - 2 * NH)))
    ddtf, _, d_bias = _cum_bwd(dr_col, dcs, ddt_raw, p, bias128, s)
    dxs, dbc, d_wc_xs, d_bc_xs, d_wc_bc, d_bc_bc = _conv_bwd(dxs_a, dbc_a, p, w_xs, b_xs, w_bc, b_bc, s)

    segs = [(dz, OFF_Z, D), (dxs, OFF_XS, D), (dq, OFF_Q, D), (dk, OFF_K, D), (dv, OFF_V, D), (dbc, OFF_BC, 512),
            (ddtf, OFF_DTF, 128)]
    d_z, d_xs, d_q, d_k, d_v, d_bcw, d_dtf = [
        _mm_tn("d_w_in_%d" % i, x0, a, tm=1024, tn=min(w, 1024), ts=ts, pro=_modulate, aux=(sc1, sh1))
        for i, (a, _, w) in enumerate(segs)]
    d_w_in = dict(z=d_z, xs=d_xs, bc=d_bcw, dt=d_dtf[:, :NH], q=d_q, k=d_k, v=d_v, f=d_dtf[:, NH:2 * NH])
    sent = send_grads("in", [_shard_w_in_grad(d_w_in)])
    segs[-1] = (_after(ddtf, sent), OFF_DTF, 128)
    dh1 = _mm_nt("d_h1", [(a, w, 0) for a, _, w in segs], [(wcat, w, off // w) for _, off, w in segs], n=D,
                 tm=min(512, s), tn=1024, out_dtype=F32)
    grad_x, d_sc1, d_sh1 = _input_grad(dh1, du1, x0, sc1, s)

    return dict(
        loss=(0.5 / D) * jnp.sum(sq_err), grad_x=grad_x,
        d_mod=jnp.concatenate([d_sh1, d_sc1, d_g1, d_sh2, d_sc2, d_g2], axis=1),
        d_conv_w=jnp.concatenate([d_wc_xs[:4], d_wc_bc[:4]], axis=1), d_conv_b=jnp.concatenate([d_bc_xs, d_bc_bc], axis=1),
        d_ssm_norm_w=d_ssm_w, d_attn_norm_w=d_attn_w, d_ln1_g=d_ln1_g, d_ln1_b=d_ln1_b, d_ln2_g=d_ln2_g, d_ln2_b=d_ln2_b,
        d_gate_bias=d_bias, d_a_log=d_alog, d_d_skip=d_dskip)


W_IN_SEGS = [('z', W_Z, D), ('xs', W_XS, D), ('bc', W_BC, 512), ('dt', W_DT, NH), ('q', W_Q, D), ('k', W_K, D),
             ('v', W_V, D), ('f', W_F, NH)]
SHARD_W = IN_COLS // N_DEV


def _pack_w_in(shards):
    def cols(lo, hi):
        pieces = []
        while lo < hi:
            dev = lo // SHARD_W
            end = min(hi, (dev + 1) * SHARD_W)
            pieces.append(shards[dev][:, lo - dev * SHARD_W:end - dev * SHARD_W])
            lo = end
        return pieces

    seg = {n: cols(off, off + w) for n, off, w in W_IN_SEGS}
    pieces = seg['z'] + seg['xs'] + seg['q'] + seg['k'] + seg['v'] + seg['bc'] + seg['dt'] + seg['f']
    return jnp.concatenate(pieces + [jnp.zeros((D, 128 - 2 * NH), shards.dtype)], axis=1)


def _shard_w_in_grad(d_w_in):
    blocks = []
    for dev in range(N_DEV):
        lo, hi = dev * SHARD_W, (dev + 1) * SHARD_W
        pieces = [d_w_in[n][:, max(lo, off) - off:min(hi, off + w) - off] for n, off, w in W_IN_SEGS
                  if max(lo, off) < min(hi, off + w)]
        pieces.append(jnp.zeros((D, -SHARD_W % 128), pieces[0].dtype))
        blocks.append(jnp.concatenate(pieces, axis=1))
    return jnp.stack(blocks, axis=0)


WEIGHTS = ['w_ada', 'b_ada', 'w_in', 'conv_w', 'conv_b', 'dt_bias', 'a_log', 'd_skip', 'ssm_norm_w', 'f_bias',
           'attn_norm_w', 'w_out', 'ln1_g', 'ln1_b', 'w_ff_in', 'w_ff_out', 'ln2_g', 'ln2_b']
BIG = ['w_in', 'w_out', 'w_ff_in', 'w_ff_out']
SMALL = ['b_ada', 'conv_b', 'ssm_norm_w', 'attn_norm_w', 'ln1_g', 'ln1_b', 'ln2_g', 'ln2_b', 'dt_bias', 'a_log', 'd_skip',
         'f_bias', 'conv_w']


def _pad_lanes(v, n=128):
    return jnp.pad(v, ((0, 0), (0, n - v.shape[1])))


def _small_block(vals):
    rows = [_pad_lanes(vals[n].reshape(1, -1), -(-vals[n].size // 128) * 128).reshape(-1, 128) for n in SMALL]
    block = jnp.concatenate(rows, axis=0)
    return jnp.pad(block, ((0, 120 - block.shape[0]), (0, 0)))


def _small_unblock(block, like):
    out, r = {}, 0
    for n in SMALL:
        size = like[n].size
        nr = -(-size // 128)
        out[n] = block[r:r + nr].reshape(-1)[:size].reshape(like[n].shape)
        r += nr
    return out


def kernel(x, c, w_ada, b_ada, w_in, conv_w, conv_b, dt_bias, a_log, d_skip, ssm_norm_w, f_bias, attn_norm_w, w_out, ln1_g, ln1_b, w_ff_in, w_ff_out, ln2_g, ln2_b, loss_target, m_w_ada, m_b_ada, m_w_in, m_conv_w, m_conv_b, m_dt_bias, m_a_log, m_d_skip, m_ssm_norm_w, m_f_bias, m_attn_norm_w, m_w_out, m_ln1_g, m_ln1_b, m_w_ff_in, m_w_ff_out, m_ln2_g, m_ln2_b, v_w_ada, v_b_ada, v_w_in, v_conv_w, v_conv_b, v_dt_bias, v_a_log, v_d_skip, v_ssm_norm_w, v_f_bias, v_attn_norm_w, v_w_out, v_ln1_g, v_ln1_b, v_w_ff_in, v_w_ff_out, v_ln2_g, v_ln2_b):
    args = dict(locals())
    w = {n: args[n] for n in WEIGHTS}
    m = {n: args['m_' + n] for n in WEIGHTS}
    v = {n: args['v_' + n] for n in WEIGHTS}
    me = 4 * lax.axis_index("x") + 2 * lax.axis_index("y") + lax.axis_index("c")
    ada_cols = 6 * D // N_DEV
    conv_cols = conv_w.shape[2]

    c_all, conv_all = _exchange("gather_cond", [c, conv_w[0]], False)
    c_all = c_all.reshape(N_DEV, D)
    conv_w_full = conv_all.transpose(1, 0, 2).reshape(4, N_DEV * conv_cols)
    b_shard = lax.dynamic_slice(b_ada, (0, me * ada_cols), (1, ada_cols))
    mod_all, = _exchange("gather_mod", [_ada_mod(c_all, w_ada[0], b_shard)], False)
    mod = lax.dynamic_index_in_dim(mod_all, me, axis=1, keepdims=False).reshape(1, 6 * D)

    win_s = _gather_two_level("gather_w_in", _after(w_in[0].astype(BF16), mod * 0))
    first_done = win_s[0, 0:1, 0:1] * 0
    rest = _exchange_async("gather_rest", [_after(w[n][0].astype(BF16), first_done) for n in BIG[1:]], False, 1)

    def late_weights(after):
        wout_s, w1s, w2_s = rest()
        return wout_s.reshape(2 * D, D), w1s, w2_s.reshape(DFF, D)

    sends = {}

    def send_grads(tag, blocks):
        sends[tag] = _exchange_async("scatter_" + tag, blocks, True, {'late': 2, 'in': 3}[tag])
        return sum(b.reshape(-1)[0].astype(F32) * 0 for b in blocks)

    out = _local_step(x[0], loss_target[0], mod, _pack_w_in(win_s), late_weights, send_grads,
                      conv_w_full, conv_b, dt_bias, a_log, d_skip, ssm_norm_w, f_bias, attn_norm_w, ln1_g, ln1_b, ln2_g, ln2_b)

    small = jnp.concatenate(
        [out['d_mod'], out['d_conv_w'].reshape(1, -1), out['d_conv_b'], out['d_ssm_norm_w'], out['d_attn_norm_w'],
         out['d_ln1_g'], out['d_ln1_b'], out['d_ln2_g'], out['d_ln2_b'], out['d_gate_bias'], out['d_a_log'],
         out['d_d_skip'], _pad_lanes(out['loss'].reshape(1, 1))], axis=1).reshape(-1, 128)
    small_landed = _exchange_async("gather_small", [small], False, 4)
    (g_ff_in, g_ff_out, g_out), (g_in,) = sends['late'](), sends['in']()
    g_parts = dict(w_ff_in=g_ff_in, w_ff_out=g_ff_out, w_out=g_out, w_in=g_in)
    big = {n: _adamw("adamw_" + n, w[n][0], g_parts[n], m[n][0], v[n][0], tr=256, slots=True) for n in BIG}
    big_done = sum(big[n][1][0:1, 0:1] * 0 for n in BIG)
    small_all = _after(small_landed()[0], big_done)
    ssum = _sum_slots("sum_small", small_all)
    dmod_all = small_all[:, :6 * D // 128].reshape(N_DEV, 6 * D)
    g_w_ada, g_b_ada = _ada_grad(c_all, lax.dynamic_slice(dmod_all, (0, me * ada_cols), (N_DEV, ada_cols)), dmod_all)
    rows = lambda a, b: ssum[a:b].reshape(1, -1)
    g_conv_w = lax.dynamic_slice(ssum[48:96].reshape(4, N_DEV * conv_cols), (0, me * conv_cols), (4, conv_cols))
    g_small = dict(b_ada=g_b_ada, conv_w=g_conv_w[None], conv_b=rows(96, 108), ssm_norm_w=rows(108, 116),
                   attn_norm_w=rows(116, 124), ln1_g=rows(124, 132), ln1_b=rows(132, 140), ln2_g=rows(140, 148),
                   ln2_b=rows(148, 156), dt_bias=ssum[156:157, :NH], f_bias=ssum[156:157, NH:2 * NH],
                   a_log=ssum[157:158, :NH], d_skip=ssum[158:159, :NH])
    sm = _adamw("adamw_small", _small_block(w), _small_block(g_small), _small_block(m), _small_block(v), tr=120, slots=False)
    ada = _adamw("adamw_ada", w_ada[0], g_w_ada, m_w_ada[0], v_w_ada[0], tr=256, slots=False)

    results = []
    for k in range(4):
        vals = _small_unblock(sm[k], w)
        vals['w_ada'] = ada[k][None]
        for n in BIG:
            vals[n] = big[n][k][None]
        results.append(vals)
    return (ssum[159, 0], out['grad_x'][None], *[res[n] for res in results for n in WEIGHTS])
```

```python
import functools

import jax
import jax.numpy as jnp
from jax import lax
from jax.experimental import pallas as pl
from jax.experimental.pallas import tpu as pltpu
from jax.experimental.pallas import tpu_sc as plsc

F32, BF16 = jnp.float32, jnp.bfloat16

N_DEV = 8
D = 1024
NH, HD = 16, 64
NSTATE = 128
CHUNK = 128
HG = 8
DFF = 4096
ALPHA = 2.0 ** 0.25
EPS = 1e-5
ATT_SCALE = HD ** -0.5

OFF_Z, OFF_XS, OFF_Q, OFF_K, OFF_V, OFF_BC, OFF_DTF = 0, 1024, 2048, 3072, 4096, 5120, 5632
PCOLS = 5760
W_Z, W_XS, W_BC, W_DT, W_Q, W_K, W_V, W_F = 0, 1024, 2048, 2560, 2576, 3600, 4624, 5648
IN_COLS = 5664

ADAM_LR, ADAM_B1, ADAM_B2, ADAM_EPS, ADAM_WD, ADAM_STEP = 0.001, 0.9, 0.999, 1e-08, 0.01, 10

VMEM_LIMIT = 56 << 20

NN = (((1,), (0,)), ((), ()))
NT = (((1,), (1,)), ((), ()))
TN = (((0,), (0,)), ((), ()))


def _dot(a, b, dims=NN):
    return lax.dot_general(a, b, dims, preferred_element_type=F32)


def _bdot(a, b, dims=NN):
    return _dot(a.astype(BF16), b.astype(BF16), dims)


def _split3(v, terms=3):
    parts, rest = [], v
    for _ in range(terms):
        p = rest.astype(BF16)
        parts.append(p)
        rest = rest - p.astype(F32)
    return parts


def _sel_left(m01, v):
    return sum(_dot(m01, p) for p in _split3(v))


def _sel_right(v, m01, dims=NN, terms=3):
    return sum(_dot(p, m01, dims) for p in _split3(v, terms))


def _iota(shape, dim):
    return lax.broadcasted_iota(jnp.int32, shape, dim)


def _tri_lower(n):
    return (_iota((n, n), 1) <= _iota((n, n), 0)).astype(BF16)


def _tri_upper(n):
    return (_iota((n, n), 1) >= _iota((n, n), 0)).astype(BF16)


def _head_expand():
    return (lax.shift_right_logical(_iota((128, D), 1), 6) == _iota((128, D), 0)).astype(BF16)


def _head_reduce():
    return (lax.shift_right_logical(_iota((D, 128), 0), 6) == _iota((D, 128), 1)).astype(BF16)


def _sigmoid(x):
    return 0.5 * jnp.tanh(0.5 * x) + 0.5


def _silu(x):
    return x * _sigmoid(x)


def _dsilu(x):
    s = _sigmoid(x)
    return s * (1.0 + x * (1.0 - s))


def _softplus(x):
    return jnp.maximum(x, 0.0) + jnp.log(1.0 + jnp.exp(-jnp.abs(x)))


def _log_sigmoid(x):
    return jnp.minimum(x, 0.0) - jnp.log(1.0 + jnp.exp(-jnp.abs(x)))


def _params(sem):
    return pltpu.CompilerParams(dimension_semantics=sem, vmem_limit_bytes=VMEM_LIMIT)


def _mm_nn(name, a, b, *, tm, tn, tk, out_dtype, pro=None, aux=()):
    m, k_all = a.shape
    b_sharded = b.ndim == 3
    n = b.shape[0] * b.shape[2] if b_sharded else b.shape[1]
    assert not b_sharded or tn == b.shape[2]
    nk = k_all // tk
    n_aux = len(aux)
    b_spec = (pl.BlockSpec((None, tk, tn), lambda i, j, k: (j, k, 0)) if b_sharded
              else pl.BlockSpec((tk, tn), lambda i, j, k: (k, j)))

    def body(a_ref, b_ref, *rest):
        aux_refs, o_ref = rest[:n_aux], rest[n_aux]
        at = a_ref[...]
        if pro is not None:
            at = pro(at, *[r[...] for r in aux_refs])
        part = _bdot(at, b_ref[...])
        if nk == 1:
            o_ref[...] = part.astype(out_dtype)
            return
        acc_ref = rest[n_aux + 1]
        kk = pl.program_id(2)

        @pl.when(kk == 0)
        def _():
            acc_ref[...] = part

        @pl.when(kk > 0)
        def _():
            acc_ref[...] += part

        @pl.when(kk == nk - 1)
        def _():
            o_ref[...] = acc_ref[...].astype(out_dtype)

    return pl.pallas_call(
        body, name=name,
        grid=(m // tm, n // tn, nk),
        in_specs=[pl.BlockSpec((tm, tk), lambda i, j, k: (i, k)), b_spec]
        + [pl.BlockSpec((1, tk), lambda i, j, k: (0, k)) for _ in aux],
        out_specs=pl.BlockSpec((tm, tn), lambda i, j, k: (i, j)),
        out_shape=jax.ShapeDtypeStruct((m, n), out_dtype),
        scratch_shapes=[] if nk == 1 else [pltpu.VMEM((tm, tn), F32)],
        compiler_params=_params(("parallel", "parallel", "arbitrary")),
    )(a, b, *aux)


def _mm_nt(name, a_list, b_list, *, n, tm, tn, out_dtype, epi=None, epi_aux=()):
    m = a_list[0][0].shape[0]
    n_op = len(a_list)
    n_epi = len(epi_aux)

    def body(*refs):
        a_refs, b_refs = refs[:n_op], refs[n_op:2 * n_op]
        e_refs, o_ref = refs[2 * n_op:2 * n_op + n_epi], refs[2 * n_op + n_epi]
        acc = None
        for a_ref, b_ref in zip(a_refs, b_refs):
            part = _bdot(a_ref[...], b_ref[...], NT)
            acc = part if acc is None else acc + part
        if epi is not None:
            acc = epi(acc, *[r[...] for r in e_refs])
        o_ref[...] = acc.astype(out_dtype)

    in_specs = [pl.BlockSpec((tm, w), functools.partial(lambda i, j, cb: (i, cb), cb=cb)) for (_, w, cb) in a_list]
    for (b, w, cb) in b_list:
        if b.ndim == 3:
            in_specs.append(pl.BlockSpec((None, tn, w), functools.partial(lambda i, j, cb: (cb, j, 0), cb=cb)))
        else:
            in_specs.append(pl.BlockSpec((tn, w), functools.partial(lambda i, j, cb: (j, cb), cb=cb)))
    in_specs += [pl.BlockSpec((tm, tn), lambda i, j: (i, j)) for _ in epi_aux]
    return pl.pallas_call(
        body, name=name,
        grid=(m // tm, n // tn),
        in_specs=in_specs,
        out_specs=pl.BlockSpec((tm, tn), lambda i, j: (i, j)),
        out_shape=jax.ShapeDtypeStruct((m, n), out_dtype),
        compiler_params=_params(("parallel", "parallel")),
    )(*[a for (a, _, _) in a_list], *[b for (b, _, _) in b_list], *epi_aux)


def _mm_tn(name, a, b, *, tm, tn, ts, pro=None, aux=(), col_shards=False):
    s_all, ka = a.shape
    nb = b.shape[1]
    n_aux = len(aux)
    ns = s_all // ts
    assert not col_shards or tn == nb // N_DEV

    def body(a_ref, b_ref, *rest):
        aux_refs, o_ref, acc_ref = rest[:n_aux], rest[n_aux], rest[n_aux + 1]
        at = a_ref[...]
        if pro is not None:
            at = pro(at, *[r[...] for r in aux_refs])
        part = _bdot(at, b_ref[...], TN)
        ss = pl.program_id(2)

        @pl.when(ss == 0)
        def _():
            acc_ref[...] = part

        @pl.when(ss > 0)
        def _():
            acc_ref[...] += part

        @pl.when(ss == ns - 1)
        def _():
            o_ref[...] = acc_ref[...].astype(BF16)

    if col_shards:
        out_spec = pl.BlockSpec((None, tm, tn), lambda i, j, s: (j, i, 0))
        out_shape = jax.ShapeDtypeStruct((N_DEV, ka, tn), BF16)
    else:
        out_spec = pl.BlockSpec((tm, tn), lambda i, j, s: (i, j))
        out_shape = jax.ShapeDtypeStruct((ka, nb), BF16)
    return pl.pallas_call(
        body, name=name,
        grid=(ka // tm, nb // tn, ns),
        in_specs=[pl.BlockSpec((ts, tm), lambda i, j, s: (s, i)),
                  pl.BlockSpec((ts, tn), lambda i, j, s: (s, j))]
        + [pl.BlockSpec((1, tm), lambda i, j, s: (0, i)) for _ in aux],
        out_specs=out_spec, out_shape=out_shape,
        scratch_shapes=[pltpu.VMEM((tm, tn), F32)],
        compiler_params=_params(("parallel", "parallel", "arbitrary")),
    )(a, b, *aux)


def _rowk(name, fn, n_rows, tr, rows, fulls, outs, accs, reverse=False):
    n = n_rows // tr
    n_row, n_full, n_out, n_acc = len(rows), len(fulls), len(outs), len(accs)

    def pos(i):
        return (n - 1 - i) if reverse else i

    def body(*refs):
        row_refs = refs[:n_row]
        full_refs = refs[n_row:n_row + n_full]
        out_refs = refs[n_row + n_full:n_row + n_full + n_out]
        acc_refs = refs[n_row + n_full + n_out:]
        i = pl.program_id(0)

        @pl.when(i == 0)
        def _():
            for r in acc_refs:
                r[...] = jnp.zeros(r.shape, r.dtype)

        res = fn(pos(i), *[r[...] for r in row_refs], *[r[...] for r in full_refs], *[r[...] for r in acc_refs])
        for r, v in zip(out_refs + acc_refs, res):
            r[...] = v.astype(r.dtype)

    def row_map(i, cb, shift):
        return (jnp.clip(pos(i) + shift, 0, n - 1), cb)

    def halo_map(i, cb, shift):
        tile = jnp.clip(pos(i) + shift, 0, n - 1)
        return (tile * (tr // 8) + (tr // 8 - 1 if shift < 0 else 0), cb)

    in_specs = [pl.BlockSpec((tr, w), functools.partial(row_map, cb=cb, shift=sh)) if sh == 0 else
                pl.BlockSpec((8, w), functools.partial(halo_map, cb=cb, shift=sh)) for (_, w, cb, sh) in rows]
    in_specs += [pl.BlockSpec(f.shape, functools.partial(lambda i, nd: (0,) * nd, nd=f.ndim)) for f in fulls]
    out_specs = [pl.BlockSpec((tr, w), lambda i: (pos(i), 0)) for (w, _) in outs]
    out_specs += [pl.BlockSpec((r, w), lambda i: (0, 0)) for (r, w) in accs]
    out_shape = [jax.ShapeDtypeStruct((n_rows, w), dt) for (w, dt) in outs]
    out_shape += [jax.ShapeDtypeStruct((r, w), F32) for (r, w) in accs]
    return pl.pallas_call(
        body, name=name, grid=(n,), in_specs=in_specs, out_specs=out_specs, out_shape=out_shape,
        compiler_params=_params(("arbitrary",)),
    )(*[a for (a, _, _, _) in rows], *fulls)


def _colsum(x):
    return jnp.sum(x, axis=0, keepdims=True)


def _mean(x):
    return jnp.mean(x, axis=-1, keepdims=True)


def _modulate(x, sc, sh):
    return x * (1.0 + sc) + sh


def _shift_down(cur, prev8, j):
    tr = cur.shape[0]
    row8 = _iota(prev8.shape, 0)
    head = jnp.where(row8 < j, pltpu.roll(prev8, j, 0), pltpu.roll(cur[0:8], j, 0))
    return head if tr == 8 else jnp.concatenate([head, pltpu.roll(cur, j, 0)[8:]], axis=0)


def _shift_up(cur, next8, j):
    tr = cur.shape[0]
    row8 = _iota(next8.shape, 0)
    tail = jnp.where(row8 < 8 - j, pltpu.roll(cur[tr - 8:], 8 - j, 0), pltpu.roll(next8, 8 - j, 0))
    return jnp.concatenate([pltpu.roll(cur, tr - j, 0)[:tr - 8], tail], axis=0)


def _conv(cur, prev, w, b):
    out = cur * w[3:4] + b
    for j in (1, 2, 3):
        out = out + _shift_down(cur, prev, j) * w[3 - j:4 - j]
    return out


def _conv_fwd(p, w_xs, b_xs, w_bc, b_bc, s):
    def fn(pos, xs, xs_prev, bc, bc_prev, w_xs, b_xs, w_bc, b_bc):
        first = pos == 0
        xs_prev = jnp.where(first, 0.0, xs_prev)
        bc_prev = jnp.where(first, 0.0, bc_prev)
        return _silu(_conv(xs, xs_prev, w_xs, b_xs)), _silu(_conv(bc, bc_prev, w_bc, b_bc))

    return _rowk("conv_fwd", fn, s, 256,
                 [(p, D, OFF_XS // D, 0), (p, D, OFF_XS // D, -1), (p, 512, OFF_BC // 512, 0), (p, 512, OFF_BC // 512, -1)],
                 [w_xs, b_xs, w_bc, b_bc], [(D, F32), (512, F32)], [])


def _conv_bwd(dxs_a, dbc_a, p, w_xs, b_xs, w_bc, b_bc, s):
    tr = 256
    n = s // tr

    def fn(pos, da1, da1n, x1, x1p, x1n, da2, da2n, x2, x2p, x2n, w1, b1, w2, b2, aw1, ab1, aw2, ab2):
        dx1, dw1, db1 = _conv_bwd_fn(pos, n, da1, da1n, x1, x1p, x1n, w1, b1)
        dx2, dw2, db2 = _conv_bwd_fn(pos, n, da2, da2n, x2, x2p, x2n, w2, b2)
        return dx1, dx2, aw1 + dw1, ab1 + db1, aw2 + dw2, ab2 + db2

    cx, cb = OFF_XS // D, OFF_BC // 512
    return _rowk("conv_bwd", fn, s, tr,
                 [(dxs_a, D, 0, 0), (dxs_a, D, 0, 1), (p, D, cx, 0), (p, D, cx, -1), (p, D, cx, 1),
                  (dbc_a, 512, 0, 0), (dbc_a, 512, 0, 1), (p, 512, cb, 0), (p, 512, cb, -1), (p, 512, cb, 1)],
                 [w_xs, b_xs, w_bc, b_bc], [(D, BF16), (512, BF16)], [(8, D), (1, D), (8, 512), (1, 512)])


def _conv_bwd_fn(pos, n, da, da_next, x, x_prev, x_next, w, b):
    first, last = pos == 0, pos == n - 1
    x_prev = jnp.where(first, 0.0, x_prev)
    dc = da * _dsilu(_conv(x, x_prev, w, b))
    dc_next = jnp.where(last, 0.0, da_next * _dsilu(_conv(x_next, x[x.shape[0] - 8:], w, b)))
    dx = dc * w[3:4]
    dws = [None] * 4
    dws[3] = _colsum(dc * x)
    for j in (1, 2, 3):
        dx = dx + _shift_up(dc, dc_next, j) * w[3 - j:4 - j]
        dws[3 - j] = _colsum(dc * _shift_down(x, x_prev, j))
    row = _iota((8, x.shape[1]), 0)
    dw = jnp.zeros((8, x.shape[1]), F32)
    for k in range(4):
        dw = jnp.where(row == k, dws[k], dw)
    return dx, dw, _colsum(dc)


def _ssd_gates(dtf, bias, a_log):
    lane = _iota(dtf.shape, 1)
    head = lane < NH
    dt = jnp.where(head, _softplus(dtf + bias), 0.0)
    a_neg = jnp.where(_iota(a_log.shape, 1) < NH, -jnp.exp(a_log), 0.0)
    a = dt * a_neg
    cs = _sel_left(_tri_lower(CHUNK), a)
    return dt, a_neg, cs


def _decay_mask(cs_ref, cst_ref, h):
    diff = cs_ref[:, h:h + 1] - cst_ref[h:h + 1, :]
    low = _iota((CHUNK, CHUNK), 1) <= _iota((CHUNK, CHUNK), 0)
    return jnp.where(low, jnp.exp(jnp.minimum(diff, 0.0)), 0.0)


def _ssd_fwd(xs_a, bc_a, p, bias128, alog128, dskip_x, s):
    nc = s // CHUNK
    t = CHUNK

    def body(xs_ref, bc_ref, dtf_ref, bias_ref, alog_ref, dsk_ref, y_ref, st_ref,
             state, x_sc, xw_sc, cs_sc, cst_sc, yd_sc):
        c = pl.program_id(0)

        @pl.when(c == 0)
        def _():
            state[...] = jnp.zeros(state.shape, F32)

        dt, _, cs = _ssd_gates(dtf_ref[...], bias_ref[...], alog_ref[...])
        cs_sc[...] = cs
        cst_sc[...] = cs.T
        cs_last = cs[t - 1:t, :]
        expand = _head_expand()
        ex = _sel_right(jnp.concatenate([dt, jnp.exp(cs), jnp.exp(cs_last - cs)], axis=0), expand, terms=2)
        dt_x, eo_x, we_x = ex[0:t], ex[t:2 * t], ex[2 * t:3 * t]
        g_x = _sel_right(jnp.broadcast_to(jnp.exp(cs_last), (8, 128)), expand)[0:1]
        xs = xs_ref[...]
        x = xs * dt_x
        x_sc[...] = x.astype(BF16)
        xw_sc[...] = (x * we_x).astype(BF16)
        prev = state[...]
        st_ref[0] = prev
        prev_b = prev.astype(BF16)
        for g in range(2):
            cols = slice(g * 512, (g + 1) * 512)
            b_g = bc_ref[:, g * 128:(g + 1) * 128].astype(BF16)
            c_g = bc_ref[:, 256 + g * 128:256 + (g + 1) * 128].astype(BF16)
            gmat = _dot(c_g, b_g, NT)
            y_off = _dot(c_g, prev_b[:, cols]) * eo_x[:, cols]
            s_loc = _dot(b_g, xw_sc[:, cols], TN)
            state[:, cols] = g_x[:, cols] * prev[:, cols] + s_loc
            for e in range(HG):
                h = g * HG + e
                m = gmat * _decay_mask(cs_sc, cst_sc, h)
                yd_sc[:, h * HD:(h + 1) * HD] = _dot(m.astype(BF16), x_sc[:, h * HD:(h + 1) * HD])
            y_ref[:, cols] = yd_sc[:, cols] + y_off + dsk_ref[:, cols] * xs[:, cols]

    return pl.pallas_call(
        body, name="ssd_fwd", grid=(nc,),
        in_specs=[pl.BlockSpec((t, D), lambda c: (c, 0)),
                  pl.BlockSpec((t, 512), lambda c: (c, 0)),
                  pl.BlockSpec((t, 128), lambda c: (c, OFF_DTF // 128)),
                  pl.BlockSpec((1, 128), lambda c: (0, 0)),
                  pl.BlockSpec((1, 128), lambda c: (0, 0)),
                  pl.BlockSpec((1, D), lambda c: (0, 0))],
        out_specs=[pl.BlockSpec((t, D), lambda c: (c, 0)),
                   pl.BlockSpec((1, NSTATE, D), lambda c: (c, 0, 0))],
        out_shape=[jax.ShapeDtypeStruct((s, D), F32), jax.ShapeDtypeStruct((nc, NSTATE, D), F32)],
        scratch_shapes=[pltpu.VMEM((NSTATE, D), F32), pltpu.VMEM((t, D), BF16), pltpu.VMEM((t, D), BF16),
                        pltpu.VMEM((t, 128), F32), pltpu.VMEM((128, t), F32), pltpu.VMEM((t, D), F32)],
        compiler_params=_params(("arbitrary",)),
    )(xs_a, bc_a, p, bias128, alog128, dskip_x)


def _ssd_bwd(dy, xs_a, bc_a, p, states, bias128, alog128, dskip_x, s):
    nc = s // CHUNK
    t = CHUNK

    def body(dy_ref, xs_ref, bc_ref, dtf_ref, st_ref, bias_ref, alog_ref, dsk_ref,
             dxs_ref, dbc_ref, ddt_ref, dalog_ref, dskip_ref,
             dstate, x_sc, dy_sc, dx_sc, deo_sc, dwe_sc, cs_sc, cst_sc, dcol_sc, drow_sc):
        i = pl.program_id(0)

        @pl.when(i == 0)
        def _():
            dstate[...] = jnp.zeros(dstate.shape, F32)
            dalog_ref[...] = jnp.zeros(dalog_ref.shape, F32)
            dskip_ref[...] = jnp.zeros(dskip_ref.shape, F32)

        dtf = dtf_ref[...]
        dt, a_neg, cs = _ssd_gates(dtf, bias_ref[...], alog_ref[...])
        cs_sc[...] = cs
        cst_sc[...] = cs.T
        cs_last = cs[t - 1:t, :]
        eo, we, g_end = jnp.exp(cs), jnp.exp(cs_last - cs), jnp.exp(cs_last)
        expand, reduce = _head_expand(), _head_reduce()
        ex = _sel_right(jnp.concatenate([dt, eo, we], axis=0), expand, terms=2)
        dt_x, eo_x, we_x = ex[0:t], ex[t:2 * t], ex[2 * t:3 * t]
        g_x = _sel_right(jnp.broadcast_to(g_end, (8, 128)), expand)[0:1]
        xs = xs_ref[...]
        dyv = dy_ref[...]
        x = xs * dt_x
        x_sc[...] = x.astype(BF16)
        dy_sc[...] = dyv.astype(BF16)
        dyo_b = (dyv * eo_x).astype(BF16)
        xw_b = (x * we_x).astype(BF16)
        prev = st_ref[0]
        prev_b = prev.astype(BF16)
        dnext = dstate[...]
        dnext_b = dnext.astype(BF16)
        dcol_sc[...] = jnp.zeros(dcol_sc.shape, F32)
        drow_sc[...] = jnp.zeros(drow_sc.shape, F32)
        lane_row = _iota((1, 128), 1)
        sub_col = _iota((128, 1), 0)
        for g in range(2):
            cols = slice(g * 512, (g + 1) * 512)
            b_g = bc_ref[:, g * 128:(g + 1) * 128].astype(BF16)
            c_g = bc_ref[:, 256 + g * 128:256 + (g + 1) * 128].astype(BF16)
            gmat = _dot(c_g, b_g, NT)
            b_ds = _dot(b_g, dnext_b[:, cols])
            c_s = _dot(c_g, prev_b[:, cols])
            dx_sc[:, cols] = b_ds * we_x[:, cols]
            deo_sc[:, cols] = dyv[:, cols] * c_s
            dwe_sc[:, cols] = b_ds * x[:, cols]
            db = _dot(xw_b[:, cols], dnext_b[:, cols], NT)
            dc = _dot(dyo_b[:, cols], prev_b[:, cols], NT)
            dstate[:, cols] = g_x[:, cols] * dnext[:, cols] + _dot(c_g, dyo_b[:, cols], TN)
            dg = jnp.zeros((t, t), F32)
            for e in range(HG):
                h = g * HG + e
                hc = slice(h * HD, (h + 1) * HD)
                lmat = _decay_mask(cs_sc, cst_sc, h)
                m = gmat * lmat
                dx_sc[:, hc] += _dot(m.astype(BF16), dy_sc[:, hc], TN)
                dm = _dot(dy_sc[:, hc], x_sc[:, hc], NT)
                dg = dg + dm * lmat
                qm = dm * m
                dcol_sc[...] += jnp.sum(qm, axis=1, keepdims=True) * (lane_row == h).astype(F32)
                drow_sc[...] += (sub_col == h).astype(F32) * jnp.sum(qm, axis=0, keepdims=True)
            dg_b = dg.astype(BF16)
            dbc_ref[:, g * 128:(g + 1) * 128] = db + _dot(dg_b, c_g, TN)
            dbc_ref[:, 256 + g * 128:256 + (g + 1) * 128] = dc + _dot(dg_b, b_g)
        d_eo = _sel_right(deo_sc[...], reduce, terms=2)
        d_we = _sel_right(dwe_sc[...], reduce, terms=2)
        d_gend = _sel_right(jnp.broadcast_to(_colsum(dnext * prev), (8, D)), reduce)[0:1]
        d_cs = dcol_sc[...] - drow_sc[...].T + d_eo * eo - d_we * we
        extra = _colsum(d_we * we) + d_gend * g_end
        d_cs = d_cs + jnp.where(_iota((t, 128), 0) == t - 1, extra, 0.0)
        da = _sel_left(_tri_upper(t), d_cs)
        dx = dx_sc[...]
        ddt = _sel_right(dx * xs, reduce, terms=2) + da * a_neg
        dxs_ref[...] = dx * dt_x + dsk_ref[...] * dyv
        ddt_ref[...] = jnp.where(_iota((t, 128), 1) < NH, ddt * _sigmoid(dtf + bias_ref[...]), 0.0)
        dalog_ref[...] += _colsum(da * dt) * a_neg
        dskip_ref[...] += _sel_right(jnp.broadcast_to(_colsum(dyv * xs), (8, D)), reduce)[0:1]

    rev = lambda i: nc - 1 - i
    return pl.pallas_call(
        body, name="ssd_bwd", grid=(nc,),
        in_specs=[pl.BlockSpec((t, D), lambda i: (rev(i), 0)),
                  pl.BlockSpec((t, D), lambda i: (rev(i), 0)),
                  pl.BlockSpec((t, 512), lambda i: (rev(i), 0)),
                  pl.BlockSpec((t, 128), lambda i: (rev(i), OFF_DTF // 128)),
                  pl.BlockSpec((1, NSTATE, D), lambda i: (rev(i), 0, 0)),
                  pl.BlockSpec((1, 128), lambda i: (0, 0)),
                  pl.BlockSpec((1, 128), lambda i: (0, 0)),
                  pl.BlockSpec((1, D), lambda i: (0, 0))],
        out_specs=[pl.BlockSpec((t, D), lambda i: (rev(i), 0)),
                   pl.BlockSpec((t, 512), lambda i: (rev(i), 0)),
                   pl.BlockSpec((t, 128), lambda i: (rev(i), 0)),
                   pl.BlockSpec((1, 128), lambda i: (0, 0)),
                   pl.BlockSpec((1, 128), lambda i: (0, 0))],
        out_shape=[jax.ShapeDtypeStruct((s, D), F32), jax.ShapeDtypeStruct((s, 512), F32),
                   jax.ShapeDtypeStruct((s, 128), F32), jax.ShapeDtypeStruct((1, 128), F32),
                   jax.ShapeDtypeStruct((1, 128), F32)],
        scratch_shapes=[pltpu.VMEM((NSTATE, D), F32), pltpu.VMEM((t, D), BF16), pltpu.VMEM((t, D), BF16),
                        pltpu.VMEM((t, D), F32), pltpu.VMEM((t, D), F32), pltpu.VMEM((t, D), F32),
                        pltpu.VMEM((t, 128), F32), pltpu.VMEM((128, t), F32),
                        pltpu.VMEM((t, 128), F32), pltpu.VMEM((128, t), F32)],
        compiler_params=_params(("arbitrary",)),
    )(dy, xs_a, bc_a, p, states, bias128, alog128, dskip_x)


def _gate_lanes(shape):
    lane = _iota(shape, 1)
    return (lane >= NH) & (lane < 2 * NH)


def _cum_fwd(p, bias128, s):
    tr = min(512, s)

    def body(dtf_ref, bias_ref, o_ref, carry):
        @pl.when(pl.program_id(0) == 0)
        def _():
            carry[...] = jnp.zeros(carry.shape, F32)

        lf = jnp.where(_gate_lanes((tr, 128)), _log_sigmoid(dtf_ref[...] + bias_ref[...]), 0.0)
        cum = _sel_left(_tri_lower(tr), lf) + carry[...]
        carry[...] = cum[tr - 1:tr, :]
        o_ref[...] = cum

    return pl.pallas_call(
        body, name="cum_fwd", grid=(s // tr,),
        in_specs=[pl.BlockSpec((tr, 128), lambda i: (i, OFF_DTF // 128)), pl.BlockSpec((1, 128), lambda i: (0, 0))],
        out_specs=pl.BlockSpec((tr, 128), lambda i: (i, 0)),
        out_shape=jax.ShapeDtypeStruct((s, 128), F32),
        scratch_shapes=[pltpu.VMEM((1, 128), F32)],
        compiler_params=_params(("arbitrary",)),
    )(p, bias128)


def _cum_bwd(dr_col, dcs, ddt_raw, p, bias128, s):
    tr = min(512, s)

    def fn(pos, dr, dc, ddt, dtf, bias, carry, acc):
        pick = (_iota((D, 128), 0) == (_iota((D, 128), 1) - NH) * HD).astype(BF16)
        dcum = dr - _sel_right(dc, pick)
        suffix = _sel_left(_tri_upper(tr), dcum) + carry
        dfr = jnp.where(_gate_lanes((tr, 128)), suffix * _sigmoid(-(dtf + bias)), 0.0)
        out = ddt + dfr
        return out, suffix[0:1, :], acc + _colsum(out)

    return _rowk("cum_bwd", fn, s, tr,
                 [(dr_col, 128, 0, 0), (dcs, D, 0, 0), (ddt_raw, 128, 0, 0), (p, 128, OFF_DTF // 128, 0)],
                 [bias128], [(128, BF16)], [(1, 128), (1, 128)], reverse=True)


ATT_BLOCK = 512
ATT_STRIP = 32


def _head_part(shape, h, dim):
    i = _iota(shape, dim)
    return (i >= h * HD) & (i < (h + 1) * HD)


def _k_augmented(k_blk, cum_blk, j, h):
    tk = k_blk.shape[0]
    lane = _iota((tk, 128), 1)
    col = jnp.sum(jnp.where(lane == NH + 2 * j + h, cum_blk, 0.0), axis=1, keepdims=True)
    c0, c1, c2 = [c.astype(F32) for c in _split3(-col)]
    aug = jnp.where(lane == 0, c0, jnp.where(lane == 1, c1, jnp.where(lane == 2, c2, 0.0)))
    return jnp.concatenate([jnp.where(_head_part((tk, 128), h, 1), k_blk, 0.0), aug], axis=1).astype(BF16)


def _q_augmented_t(q_blk):
    tq = q_blk.shape[0]
    ones = (_iota((128, tq), 0) < 3).astype(BF16)
    return jnp.concatenate([(q_blk * ATT_SCALE).T.astype(BF16), ones], axis=0)


def _rows01(r0, r1):
    sub = _iota((8, r0.shape[1]), 0)
    return jnp.where(sub == 0, r0, jnp.where(sub == 1, r1, 0.0))


def _fold8(x, op, cur):
    for g in range(x.shape[0] // 8):
        cur = op(cur, x[8 * g:8 * (g + 1), :])
    return cur


def _attn_fwd(p, cum, s):
    t = min(ATT_BLOCK, s)
    nq = s // t
    r = ATT_STRIP

    def body(q_ref, k_ref, v_ref, c_ref, o_ref, lse_ref, kaug_sc, vt_sc, s_sc, p_sc, m_sc, l_sc, acc_sc):
        j, qi = pl.program_id(0), pl.program_id(1)

        @pl.when(qi == 0)
        def _():
            for c in range(nq):
                rows = slice(c * t, (c + 1) * t)
                k_blk, vt = k_ref[rows, :], v_ref[rows, :].T
                for h in range(2):
                    kaug_sc[h, rows, :] = _k_augmented(k_blk, c_ref[rows, :], j, h)
                    vt_sc[h, :, rows] = jnp.where(_head_part((128, t), h, 0), vt, 0.0).astype(BF16)

        qaug_t = _q_augmented_t(q_ref[...])
        m_sc[...] = jnp.full(m_sc.shape, -1e30, F32)
        l_sc[...] = jnp.zeros(l_sc.shape, F32)
        acc_sc[...] = jnp.zeros(acc_sc.shape, F32)
        top = _iota((128, t), 0) < HD

        def logits(kb, buf):
            kv = pl.ds(pl.multiple_of(kb * t, t), t)
            for h in range(2):
                s_sc[buf, h] = _dot(kaug_sc[h, kv, :], qaug_t)

        def softmax(buf, diagonal):
            alphas = []
            for h in range(2):
                cur = jnp.full((8, t), -1e30, F32)
                for i in range(t // r):
                    rows = slice(i * r, (i + 1) * r)
                    x = s_sc[buf, h, rows, :]
                    if diagonal:
                        x = jnp.where(_iota((r, t), 1) >= i * r + _iota((r, t), 0), x, -1e30)
                        s_sc[buf, h, rows, :] = x
                    cur = _fold8(x, jnp.maximum, cur)
                m_prev = m_sc[h, 0:1, :]
                m_new = jnp.maximum(m_prev, jnp.max(cur, axis=0, keepdims=True))
                alpha = jnp.exp(m_prev - m_new)
                m_sc[h, 0:1, :] = m_new
                alphas.append(alpha)
                tot = jnp.zeros((8, t), F32)
                for i in range(t // r):
                    rows = slice(i * r, (i + 1) * r)
                    pr = jnp.exp(s_sc[buf, h, rows, :] - m_new)
                    p_sc[buf, h, rows, :] = pr.astype(BF16)
                    tot = _fold8(pr, jnp.add, tot)
                l_sc[h, 0:1, :] = alpha * l_sc[h, 0:1, :] + jnp.sum(tot, axis=0, keepdims=True)
            return alphas

        def accumulate(kb, buf, alphas):
            kv = pl.ds(pl.multiple_of(kb * t, t), t)
            acc_sc[...] = (acc_sc[...] * jnp.where(top, alphas[0], alphas[1])
                           + _dot(vt_sc[0, :, kv], p_sc[buf, 0]) + _dot(vt_sc[1, :, kv], p_sc[buf, 1]))

        def pair(a, b, b_diagonal):
            logits(a, 0)
            logits(b, 1)
            accumulate(a, 0, softmax(0, False))
            accumulate(b, 1, softmax(1, b_diagonal))

        def earlier(u, carry):
            pair(2 * u, 2 * u + 1, False)
            return carry

        lax.fori_loop(0, qi // 2, earlier, 0)

        @pl.when(qi % 2 == 1)
        def _():
            pair(qi - 1, qi, True)

        @pl.when(qi % 2 == 0)
        def _():
            logits(qi, 0)
            accumulate(qi, 0, softmax(0, True))

        l0, l1 = l_sc[0, 0:1, :], l_sc[1, 0:1, :]
        o_ref[...] = (acc_sc[...] / jnp.where(top, l0, l1)).T
        lse_ref[0] = _rows01(m_sc[0, 0:1, :] + jnp.log(l0), m_sc[1, 0:1, :] + jnp.log(l1))

    return pl.pallas_call(
        body, name="attn_fwd", grid=(NH // 2, nq),
        in_specs=[pl.BlockSpec((t, 128), lambda j, qi: (qi, OFF_Q // 128 + j)),
                  pl.BlockSpec((s, 128), lambda j, qi: (0, OFF_K // 128 + j)),
                  pl.BlockSpec((s, 128), lambda j, qi: (0, OFF_V // 128 + j)),
                  pl.BlockSpec((s, 128), lambda j, qi: (0, 0))],
        out_specs=[pl.BlockSpec((t, 128), lambda j, qi: (qi, j)),
                   pl.BlockSpec((1, 8, t), lambda j, qi: (j, 0, qi))],
        out_shape=[jax.ShapeDtypeStruct((s, D), F32), jax.ShapeDtypeStruct((NH // 2, 8, s), F32)],
        scratch_shapes=[pltpu.VMEM((2, s, 256), BF16), pltpu.VMEM((2, 128, s), BF16), pltpu.VMEM((2, 2, t, t), F32),
                        pltpu.VMEM((2, 2, t, t), BF16), pltpu.VMEM((2, 8, t), F32), pltpu.VMEM((2, 8, t), F32),
                        pltpu.VMEM((128, t), F32)],
        compiler_params=_params(("parallel", "arbitrary")),
    )(p, p, p, cum)


def _attn_bwd(p, cum, o, lse, do, s):
    t = min(ATT_BLOCK, s)
    nq = s // t
    r = ATT_STRIP

    def body(q_ref, k_ref, v_ref, c_ref, o_ref, lse_ref, do_ref, dq_ref, dk_ref, dv_ref, dc_ref, dr_ref,
             qaugt_sc, qh_sc, dot_sc, doh_sc, delta_sc, dqt_sc, dr_sc, kaug_sc, vh_sc, kt_sc, s_sc, dp_sc, p_sc, ds_sc,
             dk_sc, dv_sc, dc_sc):
        j, ki = pl.program_id(0), pl.program_id(1)

        @pl.when(ki == 0)
        def _():
            for c in range(nq):
                rows = slice(c * t, (c + 1) * t)
                q_blk, do_blk = q_ref[rows, :], do_ref[rows, :]
                qaugt_sc[:, rows] = _q_augmented_t(q_blk)
                dot_sc[:, rows] = do_blk.T.astype(BF16)
                prod_t = (do_blk * o_ref[rows, :]).T
                delta_sc[:, rows] = _rows01(jnp.sum(prod_t[0:HD], axis=0, keepdims=True),
                                            jnp.sum(prod_t[HD:], axis=0, keepdims=True))
                for h in range(2):
                    head = _head_part((t, 128), h, 1)
                    qh_sc[h, rows, :] = jnp.where(head, q_blk * ATT_SCALE, 0.0).astype(BF16)
                    doh_sc[h, rows, :] = jnp.where(head, do_blk, 0.0).astype(BF16)
            dqt_sc[...] = jnp.zeros(dqt_sc.shape, F32)
            dr_sc[...] = jnp.zeros(dr_sc.shape, F32)

        k_blk, v_blk = k_ref[...], v_ref[...]
        kt = k_blk.T
        for h in range(2):
            kaug_sc[h] = _k_augmented(k_blk, c_ref[...], j, h)
            vh_sc[h] = jnp.where(_head_part((t, 128), h, 1), v_blk, 0.0).astype(BF16)
            kt_sc[h] = jnp.where(_head_part((128, t), h, 0), kt, 0.0).astype(BF16)
        dk_sc[...] = jnp.zeros(dk_sc.shape, F32)
        dv_sc[...] = jnp.zeros(dv_sc.shape, F32)
        dc_sc[...] = jnp.zeros(dc_sc.shape, F32)

        def inputs(qb, buf):
            qs = pl.ds(pl.multiple_of(qb * t, t), t)
            for h in range(2):
                s_sc[buf, h] = _dot(kaug_sc[h], qaugt_sc[:, qs])
                dp_sc[buf, h] = _dot(vh_sc[h], dot_sc[:, qs])

        def elementwise(qb, buf, diagonal):
            qs = pl.ds(pl.multiple_of(qb * t, t), t)
            for h in range(2):
                lse_row, delta_row = lse_ref[0, h:h + 1, qs], delta_sc[h:h + 1, qs]
                tot = jnp.zeros((8, t), F32)
                for i in range(t // r):
                    rows = slice(i * r, (i + 1) * r)
                    x = s_sc[buf, h, rows, :]
                    if diagonal:
                        x = jnp.where(_iota((r, t), 1) >= i * r + _iota((r, t), 0), x, -1e30)
                    pr = jnp.exp(x - lse_row)
                    ds = pr * (dp_sc[buf, h, rows, :] - delta_row)
                    p_sc[buf, h, rows, :] = pr.astype(BF16)
                    ds_sc[buf, h, rows, :] = ds.astype(BF16)
                    dc_sc[h, rows, :] += sum(ds[:, 128 * g:128 * (g + 1)] for g in range(t // 128))
                    tot = _fold8(ds, jnp.add, tot)
                dr_sc[h, :, qs] += tot

        def outputs(qb, buf):
            qs = pl.ds(pl.multiple_of(qb * t, t), t)
            dv_sc[...] += _dot(p_sc[buf, 0], doh_sc[0, qs, :]) + _dot(p_sc[buf, 1], doh_sc[1, qs, :])
            dk_sc[...] += _dot(ds_sc[buf, 0], qh_sc[0, qs, :]) + _dot(ds_sc[buf, 1], qh_sc[1, qs, :])
            dqt_sc[:, qs] += _dot(kt_sc[0], ds_sc[buf, 0]) + _dot(kt_sc[1], ds_sc[buf, 1])

        def pair(a, b, a_diagonal):
            inputs(a, 0)
            inputs(b, 1)
            elementwise(a, 0, a_diagonal)
            outputs(a, 0)
            elementwise(b, 1, False)
            outputs(b, 1)

        def later(u, carry):
            pair(ki + 1 + 2 * u, ki + 2 + 2 * u, False)
            return carry

        n_later = nq - 1 - ki
        lax.fori_loop(0, n_later // 2, later, 0)

        @pl.when(n_later % 2 == 1)
        def _():
            pair(ki, nq - 1, True)

        @pl.when(n_later % 2 == 0)
        def _():
            inputs(ki, 0)
            elementwise(ki, 0, True)
            outputs(ki, 0)

        dk_ref[...] = dk_sc[...].astype(BF16)
        dv_ref[...] = dv_sc[...].astype(BF16)
        dc_ref[...] = jnp.where(_iota((t, 128), 1) < HD, jnp.sum(dc_sc[0], axis=1, keepdims=True),
                                jnp.sum(dc_sc[1], axis=1, keepdims=True))

        @pl.when(ki == nq - 1)
        def _():
            for c in range(nq):
                rows = slice(c * t, (c + 1) * t)
                dq_ref[rows, :] = dqt_sc[:, rows].T * ATT_SCALE
            dr_ref[0] = _rows01(jnp.sum(dr_sc[0], axis=0, keepdims=True), jnp.sum(dr_sc[1], axis=0, keepdims=True))

    whole = lambda off: pl.BlockSpec((s, 128), functools.partial(lambda j, ki, off: (0, off + j), off=off))
    return pl.pallas_call(
        body, name="attn_bwd", grid=(NH // 2, nq),
        in_specs=[whole(OFF_Q // 128),
                  pl.BlockSpec((t, 128), lambda j, ki: (ki, OFF_K // 128 + j)),
                  pl.BlockSpec((t, 128), lambda j, ki: (ki, OFF_V // 128 + j)),
                  pl.BlockSpec((t, 128), lambda j, ki: (ki, 0)),
                  whole(0),
                  pl.BlockSpec((1, 8, s), lambda j, ki: (j, 0, 0)),
                  whole(0)],
        out_specs=[whole(0),
                   pl.BlockSpec((t, 128), lambda j, ki: (ki, j)),
                   pl.BlockSpec((t, 128), lambda j, ki: (ki, j)),
                   pl.BlockSpec((t, 128), lambda j, ki: (ki, j)),
                   pl.BlockSpec((1, 8, s), lambda j, ki: (j, 0, 0))],
        out_shape=[jax.ShapeDtypeStruct((s, D), F32), jax.ShapeDtypeStruct((s, D), BF16), jax.ShapeDtypeStruct((s, D), BF16),
                   jax.ShapeDtypeStruct((s, D), F32), jax.ShapeDtypeStruct((NH // 2, 8, s), F32)],
        scratch_shapes=[pltpu.VMEM((256, s), BF16), pltpu.VMEM((2, s, 128), BF16), pltpu.VMEM((128, s), BF16),
                        pltpu.VMEM((2, s, 128), BF16), pltpu.VMEM((8, s), F32), pltpu.VMEM((128, s), F32),
                        pltpu.VMEM((2, 8, s), F32), pltpu.VMEM((2, t, 256), BF16), pltpu.VMEM((2, t, 128), BF16),
                        pltpu.VMEM((2, 128, t), BF16), pltpu.VMEM((2, 2, t, t), F32), pltpu.VMEM((2, 2, t, t), F32),
                        pltpu.VMEM((2, 2, t, t), BF16), pltpu.VMEM((2, 2, t, t), BF16), pltpu.VMEM((t, 128), F32),
                        pltpu.VMEM((t, 128), F32), pltpu.VMEM((2, t, 128), F32)],
        compiler_params=_params(("parallel", "arbitrary")),
    )(p, p, p, cum, o, lse, do)


def _ln_stats(u):
    mu = _mean(u)
    d = u - mu
    rstd = lax.rsqrt(_mean(d * d) + EPS)
    return d * rstd, rstd


def _ln_bwd(dx, xh, rstd, gam):
    dxh = dx * gam
    return rstd * (dxh - _mean(dxh) - xh * _mean(dxh * xh))


def _rms_bwd(d, xn, r, w):
    t = d * w
    return r * (t - xn * _mean(t * xn)), _colsum(d * xn)


def _mix_norm(y, p, att, w_ssm, w_att, s):
    def fn(pos, y, z, att, w1, w2):
        g = y * _silu(z)
        n1 = g * lax.rsqrt(_mean(g * g) + EPS) * w1
        n2 = att * lax.rsqrt(_mean(att * att) + EPS) * w2
        return (jnp.concatenate([n1, n2], axis=1),)

    return _rowk("mix_norm", fn, s, 256, [(y, D, 0, 0), (p, D, OFF_Z // D, 0), (att, D, 0, 0)],
                 [w_ssm, w_att], [(2 * D, BF16)], [])[0]


def _mix_norm_bwd(dmix, y, p, att, w_ssm, w_att, s):
    def fn(pos, dmix, y, z, att, w1, w2, a1, a2):
        sz = _silu(z)
        g = y * sz
        r1 = lax.rsqrt(_mean(g * g) + EPS)
        dg, dw1 = _rms_bwd(dmix[:, :D], g * r1, r1, w1)
        r2 = lax.rsqrt(_mean(att * att) + EPS)
        datt, dw2 = _rms_bwd(dmix[:, D:], att * r2, r2, w2)
        return dg * sz, dg * y * _dsilu(z), datt, a1 + dw1, a2 + dw2

    return _rowk("mix_norm_bwd", fn, s, 256, [(dmix, 2 * D, 0, 0), (y, D, 0, 0), (p, D, OFF_Z // D, 0), (att, D, 0, 0)],
                 [w_ssm, w_att], [(D, F32), (D, BF16), (D, F32)], [(1, D), (1, D)])


def _ln1(x0, y, g1, gam, bet, sc2, sh2, s):
    def fn(pos, x0, y, g1, gam, bet, sc2, sh2):
        xh, _ = _ln_stats(ALPHA * x0 + (1.0 + g1) * y)
        x1 = xh * gam + bet
        return x1, _modulate(x1, sc2, sh2)

    return _rowk("ln1", fn, s, 256, [(x0, D, 0, 0), (y, D, 0, 0)], [g1, gam, bet, sc2, sh2], [(D, F32), (D, BF16)], [])


def _ln2_loss(x1, ff, tgt, g2, gam, bet, s):
    def fn(pos, x1, ff, tgt, g2, gam, bet, a_loss, a_dgam, a_dbet, a_dg2):
        xh, rstd = _ln_stats(ALPHA * x1 + (1.0 + g2) * ff)
        err = xh * gam + bet - tgt
        dx2 = err * (1.0 / D)
        du = _ln_bwd(dx2, xh, rstd, gam)
        return (du, du * (1.0 + g2), a_loss + _colsum(err * err), a_dgam + _colsum(dx2 * xh),
                a_dbet + _colsum(dx2), a_dg2 + _colsum(du * ff))

    return _rowk("ln2_loss", fn, s, 256, [(x1, D, 0, 0), (ff, D, 0, 0), (tgt, D, 0, 0)], [g2, gam, bet],
                 [(D, F32), (D, BF16)], [(1, D)] * 4)


def _ln1_bwd(dh2, du2, x0, y, g1, gam, bet, sc2, s):
    def fn(pos, dh2, du2, x0, y, g1, gam, bet, sc2, a_sc, a_sh, a_gam, a_bet, a_g1):
        xh, rstd = _ln_stats(ALPHA * x0 + (1.0 + g1) * y)
        x1 = xh * gam + bet
        dx1 = ALPHA * du2 + dh2 * (1.0 + sc2)
        du1 = _ln_bwd(dx1, xh, rstd, gam)
        return (du1, du1 * (1.0 + g1), a_sc + _colsum(dh2 * x1), a_sh + _colsum(dh2), a_gam + _colsum(dx1 * xh),
                a_bet + _colsum(dx1), a_g1 + _colsum(du1 * y))

    return _rowk("ln1_bwd", fn, s, 256, [(dh2, D, 0, 0), (du2, D, 0, 0), (x0, D, 0, 0), (y, D, 0, 0)],
                 [g1, gam, bet, sc2], [(D, F32), (D, BF16)], [(1, D)] * 5)


def _input_grad(dh1, du1, x0, sc1, s):
    def fn(pos, dh1, du1, x0, sc1, a_sc, a_sh):
        return ALPHA * du1 + dh1 * (1.0 + sc1), a_sc + _colsum(dh1 * x0), a_sh + _colsum(dh1)

    return _rowk("input_grad", fn, s, 256, [(dh1, D, 0, 0), (du1, D, 0, 0), (x0, D, 0, 0)], [sc1],
                 [(D, F32)], [(1, D)] * 2)


def _adamw(name, w, g, m, v, *, tr, slots):
    r, c = w.shape

    def body(w_ref, g_ref, m_ref, v_ref, g_out, d_out, m_out, v_out):
        if slots:
            grad = g_ref[0][:, :c].astype(F32)
            for k in range(1, N_DEV):
                grad = grad + g_ref[k][:, :c].astype(F32)
        else:
            grad = g_ref[...]
        m_new = ADAM_B1 * m_ref[...] + (1.0 - ADAM_B1) * grad
        v_new = ADAM_B2 * v_ref[...] + (1.0 - ADAM_B2) * (grad * grad)
        m_hat = m_new / (1.0 - ADAM_B1 ** ADAM_STEP)
        v_hat = v_new / (1.0 - ADAM_B2 ** ADAM_STEP)
        g_out[...] = grad
        d_out[...] = -ADAM_LR * (m_hat / (jnp.sqrt(v_hat) + ADAM_EPS) + ADAM_WD * w_ref[...])
        m_out[...] = m_new
        v_out[...] = v_new

    tile = pl.BlockSpec((tr, c), lambda i: (i, 0))
    g_spec = pl.BlockSpec((N_DEV, tr, g.shape[-1]), lambda i: (0, i, 0)) if slots else tile
    return pl.pallas_call(
        body, name=name, grid=(r // tr,),
        in_specs=[tile, g_spec, tile, tile], out_specs=[tile] * 4,
        out_shape=[jax.ShapeDtypeStruct((r, c), F32)] * 4,
        compiler_params=_params(("parallel",)),
    )(w, g, m, v)


def _dot_f32(a, b, dims=NN):
    a0, a1, a2 = _split3(a)
    b0, b1, b2 = _split3(b)
    acc = _dot(a0, b0, dims)
    for x, y in ((a0, b1), (a1, b0), (a1, b1), (a0, b2), (a2, b0)):
        acc = acc + _dot(x, y, dims)
    return acc


def _ada_mod(c_all, w_shard, b_shard):
    def body(c_ref, w_ref, b_ref, o_ref):
        act = _silu(c_ref[...])
        act16 = jnp.concatenate([act, jnp.zeros_like(act)], axis=0)
        o_ref[...] = _dot_f32(act16, w_ref[...])[0:N_DEV] + b_ref[...]

    return pl.pallas_call(
        body, name="ada_mod", out_shape=jax.ShapeDtypeStruct((N_DEV, w_shard.shape[1]), F32),
        compiler_params=_params(None),
    )(c_all, w_shard, b_shard)


def _ada_grad(c_all, dmod_cols, dmod_all):
    def body(c_ref, dc_ref, da_ref, gw_ref, gb_ref):
        act = _silu(c_ref[...])
        act16 = jnp.concatenate([act, jnp.zeros_like(act)], axis=0)
        dm = dc_ref[...]
        dm16 = jnp.concatenate([dm, jnp.zeros_like(dm)], axis=0)
        gw_ref[...] = _dot_f32(act16, dm16, TN)
        gb_ref[...] = _colsum(da_ref[...])

    return pl.pallas_call(
        body, name="ada_grad",
        out_shape=[jax.ShapeDtypeStruct((D, dmod_cols.shape[1]), F32), jax.ShapeDtypeStruct((1, 6 * D), F32)],
        compiler_params=_params(None),
    )(c_all, dmod_cols, dmod_all)


def _sum_slots(name, g):
    def body(g_ref, o_ref):
        acc = g_ref[0]
        for k in range(1, N_DEV):
            acc = acc + g_ref[k]
        o_ref[...] = acc

    return pl.pallas_call(body, name=name, out_shape=jax.ShapeDtypeStruct(g.shape[1:], F32),
                          compiler_params=_params(None))(g)


def _exchange(name, xs, scatter):
    n = len(xs)
    n_peer = N_DEV - 1

    def body(*refs):
        x_refs, o_refs = refs[:n], refs[n:2 * n]
        send_sems, recv_sems, local_sems = refs[2 * n:]
        mx, my, mc = lax.axis_index("x"), lax.axis_index("y"), lax.axis_index("c")
        me = 4 * mx + 2 * my + mc

        def src(a, slot):
            return x_refs[a].at[slot] if scatter else x_refs[a]

        own = [pltpu.make_async_copy(src(a, me), o_refs[a].at[me], local_sems.at[a]) for a in range(n)]
        for cp in own:
            cp.start()
        sends = []
        for d in range(1, N_DEV):
            px = 1 - mx if d & 4 else mx
            py = 1 - my if d & 2 else my
            pc = 1 - mc if d & 1 else mc
            peer = 4 * px + 2 * py + pc
            for a in range(n):
                def copy(src_slot, dst_slot, a=a, d=d, to=(px, py, pc)):
                    return pltpu.make_async_remote_copy(
                        src_ref=src(a, src_slot), dst_ref=o_refs[a].at[dst_slot],
                        send_sem=send_sems.at[a * n_peer + d - 1], recv_sem=recv_sems.at[a * n_peer + d - 1],
                        device_id=to, device_id_type=pl.DeviceIdType.MESH)

                out = copy(peer, me)
                out.start()
                sends.append((out, copy(me, peer)))
        for _, arrival in sends:
            arrival.wait_recv()
        for out, _ in sends:
            out.wait_send()
        for cp in own:
            cp.wait()

    shapes = [tuple(x.shape[1:] if scatter else x.shape) for x in xs]
    return pl.pallas_call(
        body, name=name,
        in_specs=[pl.BlockSpec(memory_space=pl.ANY)] * n, out_specs=[pl.BlockSpec(memory_space=pl.ANY)] * n,
        out_shape=[jax.ShapeDtypeStruct((N_DEV,) + sh, x.dtype) for sh, x in zip(shapes, xs)],
        scratch_shapes=[pltpu.SemaphoreType.DMA((n * n_peer,)), pltpu.SemaphoreType.DMA((n * n_peer,)),
                        pltpu.SemaphoreType.DMA((n,))],
        compiler_params=pltpu.CompilerParams(has_side_effects=True),
    )(*xs)


def _gather_two_level(name, x):
    def body(x_ref, o_ref, send_sems, recv_sems, local_sem):
        mx, my, mc = lax.axis_index("x"), lax.axis_index("y"), lax.axis_index("c")
        me, sibling = (mx, my, mc), (mx, my, 1 - mc)
        chips = [(1 - mx, my), (mx, 1 - my), (1 - mx, 1 - my)]

        def slot(px, py, pc):
            return o_ref.at[4 * px + 2 * py + pc]

        def copy(k, block, to, src=None):
            return pltpu.make_async_remote_copy(
                src_ref=slot(*block) if src is None else src, dst_ref=slot(*block),
                send_sem=send_sems.at[k], recv_sem=recv_sems.at[k], device_id=to, device_id_type=pl.DeviceIdType.MESH)

        mine = pltpu.make_async_copy(x_ref, slot(*me), local_sem)
        mine.start()
        first = [copy(0, me, sibling, src=x_ref)] + [copy(1 + i, me, (*chip, mc), src=x_ref) for i, chip in enumerate(chips)]
        for cp in first:
            cp.start()
        passed = [copy(4 + i, (*chip, mc), sibling) for i, chip in enumerate(chips)]
        for i, chip in enumerate(chips):
            copy(1 + i, (*chip, mc), me).wait_recv()
            passed[i].start()
        copy(0, sibling, me).wait_recv()
        for i, chip in enumerate(chips):
            copy(4 + i, (*chip, 1 - mc), me).wait_recv()
        for cp in first + passed:
            cp.wait_send()
        mine.wait()

    return pl.pallas_call(
        body, name=name,
        in_specs=[pl.BlockSpec(memory_space=pl.ANY)], out_specs=pl.BlockSpec(memory_space=pl.ANY),
        out_shape=jax.ShapeDtypeStruct((N_DEV,) + tuple(x.shape), x.dtype),
        scratch_shapes=[pltpu.SemaphoreType.DMA((7,)), pltpu.SemaphoreType.DMA((7,)), pltpu.SemaphoreType.DMA(())],
        compiler_params=pltpu.CompilerParams(has_side_effects=True),
    )(x)


def _after(x, zero):
    return x if zero is None else x + zero.reshape(-1)[0].astype(x.dtype)


_HBM = pl.BlockSpec(memory_space=pltpu.HBM)
_SEM = pl.BlockSpec(memory_space=pltpu.SEMAPHORE)


def _exchange_copies(x_refs, land_refs, send_sems, recv_sems, scatter):
    n = len(x_refs)
    n_peer = N_DEV - 1
    mx, my, mc = lax.axis_index("x"), lax.axis_index("y"), lax.axis_index("c")
    me = 4 * mx + 2 * my + mc
    pairs = []
    for d in range(1, N_DEV):
        px = 1 - mx if d & 4 else mx
        py = 1 - my if d & 2 else my
        pc = 1 - mc if d & 1 else mc
        peer = 4 * px + 2 * py + pc
        for a in range(n):
            def copy(src_slot, dst_slot, a=a, d=d, to=(px, py, pc)):
                return pltpu.make_async_remote_copy(
                    src_ref=x_refs[a].at[src_slot] if scatter else x_refs[a], dst_ref=land_refs[a].at[dst_slot],
                    send_sem=send_sems.at[a * n_peer + d - 1], recv_sem=recv_sems.at[a * n_peer + d - 1],
                    device_id=to, device_id_type=pl.DeviceIdType.MESH)

            pairs.append((copy(peer, me), copy(me, peer)))
    return me, pairs


def _exchange_async(name, xs, scatter, collective_id):
    n = len(xs)
    shapes = [tuple(x.shape[1:] if scatter else x.shape) for x in xs]
    x_refs = [jax.new_ref(x, memory_space=pltpu.MemorySpace.HBM) for x in xs]
    land_refs = [jax.empty_ref(jax.ShapeDtypeStruct((N_DEV,) + sh, x.dtype), memory_space=pltpu.MemorySpace.HBM)
                 for sh, x in zip(shapes, xs)]

    @pl.kernel(mesh=plsc.ScalarSubcoreMesh(axis_name="sequencer", num_cores=1), name=name,
               scratch_types=(pltpu.SemaphoreType.DMA((n * (N_DEV - 1),)), pltpu.SemaphoreType.DMA((n * (N_DEV - 1),)),
                              pltpu.SemaphoreType.DMA((n,))),
               compiler_params=pltpu.CompilerParams(collective_id=collective_id))
    def launch(send_sems, recv_sems, own_sems):
        barrier = pltpu.get_barrier_semaphore()
        mx, my, mc = lax.axis_index("x"), lax.axis_index("y"), lax.axis_index("c")
        for d in range(1, N_DEV):
            peer = (1 - mx if d & 4 else mx, 1 - my if d & 2 else my, 1 - mc if d & 1 else mc)
            pl.semaphore_signal(barrier, inc=1, device_id=peer, device_id_type=pl.DeviceIdType.MESH)
        pl.semaphore_wait(barrier, N_DEV - 1)
        me, pairs = _exchange_copies(x_refs, land_refs, send_sems, recv_sems, scatter)
        own = [pltpu.make_async_copy(x_refs[a].at[me] if scatter else x_refs[a], land_refs[a].at[me], own_sems.at[a])
               for a in range(n)]
        for cp in own:
            cp.start()
        for out, _ in pairs:
            out.start()
        for out, arrival in pairs:
            arrival.wait_recv()
            out.wait_send()
        for cp in own:
            cp.wait()

    launch()
    return lambda: [r[...] for r in land_refs]


def _exchange_start(name, xs, scatter):
    n = len(xs)
    shapes = [tuple(x.shape[1:] if scatter else x.shape) for x in xs]

    def body(*refs):
        x_refs, land_refs = refs[:n], refs[n:2 * n]
        send_sems, recv_sems = refs[2 * n], refs[2 * n + 1]
        token, own_sems = refs[4 * n + 2], refs[4 * n + 3]
        me, pairs = _exchange_copies(x_refs, land_refs, send_sems, recv_sems, scatter)
        own = [pltpu.make_async_copy(x_refs[a].at[me] if scatter else x_refs[a], land_refs[a].at[me], own_sems.at[a])
               for a in range(n)]
        for cp in own:
            cp.start()
        for out, _ in pairs:
            out.start()
        for cp in own:
            cp.wait()
        token[...] = jnp.zeros(token.shape, token.dtype)

    lands = [pltpu.with_memory_space_constraint(lax.empty((N_DEV,) + sh, x.dtype), pltpu.HBM) for sh, x in zip(shapes, xs)]
    res = pl.pallas_call(
        body, name=name,
        out_shape=(pltpu.SemaphoreType.DMA((n * (N_DEV - 1),)), pltpu.SemaphoreType.DMA((n * (N_DEV - 1),)),
                   *[pltpu.HBM(x.shape, x.dtype) for x in xs], *[pltpu.HBM(l.shape, l.dtype) for l in lands],
                   jax.ShapeDtypeStruct((8, 128), F32)),
        in_specs=[_HBM] * (2 * n),
        out_specs=(_SEM, _SEM, *[_HBM] * (2 * n), pl.BlockSpec(memory_space=pltpu.VMEM)),
        input_output_aliases={i: 2 + i for i in range(2 * n)},
        scratch_shapes=[pltpu.SemaphoreType.DMA((n,))],
        compiler_params=pltpu.CompilerParams(has_side_effects=pltpu.SideEffectType.DATAFLOW_SIDE_EFFECTING),
    )(*[pltpu.with_memory_space_constraint(x, pltpu.HBM) for x in xs], *lands)
    return dict(send=res[0], recv=res[1], xs=list(res[2:2 + n]), lands=list(res[2 + n:2 + 2 * n]), token=res[2 + 2 * n])


def _exchange_wait(name, handle, after, scatter):
    n = len(handle['xs'])

    def body(*refs):
        x_refs, land_refs = refs[:n], refs[n:2 * n]
        send_sems, recv_sems = refs[2 * n], refs[2 * n + 1]
        _, pairs = _exchange_copies(x_refs, land_refs, send_sems, recv_sems, scatter)
        for out, arrival in pairs:
            out.wait_send()
            arrival.wait_recv()

    res = pl.pallas_call(
        body, name=name,
        out_shape=tuple(pltpu.HBM(a.shape, a.dtype) for a in handle['xs'] + handle['lands']),
        in_specs=[_HBM] * (2 * n) + [_SEM, _SEM, pl.BlockSpec(memory_space=pl.ANY)],
        out_specs=tuple([_HBM] * (2 * n)),
        input_output_aliases={i: i for i in range(2 * n)},
        compiler_params=pltpu.CompilerParams(has_side_effects=pltpu.SideEffectType.DATAFLOW_SIDE_EFFECTING),
    )(*handle['xs'], *handle['lands'], handle['send'], handle['recv'], after)
    return list(res[n:])


def _relu2(a):
    r = jnp.maximum(a, 0.0)
    return r * r


def _relu2_grad(acc, a):
    return acc * (2.0 * jnp.maximum(a, 0.0))


def _local_step(x0, tgt, mod, wcat, late_weights, send_grads, conv_w, conv_b, dt_bias, a_log, d_skip, ssm_norm_w, f_bias,
                attn_norm_w, ln1_g, ln1_b, ln2_g, ln2_b):
    ff_w = DFF // N_DEV
    s = x0.shape[0]
    tm = min(1024, s)
    ts = min(1024, s)
    sh1, sc1, g1, sh2, sc2, g2 = [mod[:, i * D:(i + 1) * D] for i in range(6)]
    zero = jnp.zeros((1, 128 - 2 * NH), F32)
    bias128 = jnp.concatenate([dt_bias, f_bias, zero], axis=1)
    alog128 = jnp.concatenate([a_log, jnp.zeros((1, 128 - NH), F32)], axis=1)
    dskip_x = jnp.repeat(d_skip, HD, axis=1)
    w_xs, w_bc, b_xs, b_bc = conv_w[:, :D], conv_w[:, D:], conv_b[:, :D], conv_b[:, D:]

    p = _mm_nn("in_proj", x0, wcat, tm=tm, tn=1152, tk=D, out_dtype=F32, pro=_modulate, aux=(sc1, sh1))
    xs_a, bc_a = _conv_fwd(p, w_xs, b_xs, w_bc, b_bc, s)
    y_ssd, states = _ssd_fwd(xs_a, bc_a, p, bias128, alog128, dskip_x, s)
    cum = _cum_fwd(p, bias128, s)
    att, lse = _attn_fwd(p, cum, s)
    wout, w1s, w2 = late_weights(lse)
    ymix = _mix_norm(y_ssd, p, att, ssm_norm_w, attn_norm_w, s)
    y = _mm_nn("out_proj", ymix, wout, tm=tm, tn=1024, tk=2 * D, out_dtype=F32)
    x1, h2 = _ln1(x0, y, g1, ln1_g, ln1_b, sc2, sh2, s)
    a1 = _mm_nn("ff_in", h2, w1s, tm=tm, tn=ff_w, tk=D, out_dtype=F32)
    ff = _mm_nn("ff_out", a1, w2, tm=tm, tn=1024, tk=1024, out_dtype=F32, pro=_relu2)
    du2, dff, sq_err, d_ln2_g, d_ln2_b, d_g2 = _ln2_loss(x1, ff, tgt, g2, ln2_g, ln2_b, s)

    da1 = _mm_nt("d_ff_hidden", [(dff, D, 0)], [(w2, D, 0)], n=DFF, tm=tm, tn=1024, out_dtype=BF16, epi=_relu2_grad,
                 epi_aux=(a1,))
    d_w2 = _mm_tn("d_w_ff_out", a1, dff, tm=1024, tn=1024, ts=ts, pro=_relu2)
    d_w1s = _mm_tn("d_w_ff_in", h2, da1, tm=1024, tn=ff_w, ts=ts, col_shards=True)
    dh2 = _mm_nt("d_ff_input", [(da1, ff_w, k) for k in range(N_DEV)], [(w1s, ff_w, k) for k in range(N_DEV)], n=D,
                 tm=min(512, s), tn=1024, out_dtype=F32)
    du1, dy, d_sc2, d_sh2, d_ln1_g, d_ln1_b, d_g1 = _ln1_bwd(dh2, du2, x0, y, g1, ln1_g, ln1_b, sc2, s)

    dmix = _mm_nt("d_mix", [(dy, D, 0)], [(wout, D, 0)], n=2 * D, tm=tm, tn=1024, out_dtype=F32)
    d_wout = _mm_tn("d_w_out", ymix, dy, tm=1024, tn=1024, ts=ts)
    sent = send_grads("late", [d_w1s, d_w2.reshape(N_DEV, -1, D), d_wout.reshape(N_DEV, -1, D)])
    dy_ssd, dz, datt, d_ssm_w, d_attn_w = _mix_norm_bwd(dmix, y_ssd, p, att, _after(ssm_norm_w, sent), attn_norm_w, s)
    dq, dk, dv, dcs, drs = _attn_bwd(p, cum, att, lse, datt, s)
    dxs_a, dbc_a, ddt_raw, d_alog, d_dskip = _ssd_bwd(dy_ssd, xs_a, bc_a, p, states, bias128, alog128, dskip_x, s)
    dr_col = jnp.pad(drs[:, :2, :].reshape(NH, s).T, ((0, 0), (NH, 128 - 2 * NH)))
    ddtf, _, d_bias = _cum_bwd(dr_col, dcs, ddt_raw, p, bias128, s)
    dxs, dbc, d_wc_xs, d_bc_xs, d_wc_bc, d_bc_bc = _conv_bwd(dxs_a, dbc_a, p, w_xs, b_xs, w_bc, b_bc, s)

    segs = [(dz, OFF_Z, D), (dxs, OFF_XS, D), (dq, OFF_Q, D), (dk, OFF_K, D), (dv, OFF_V, D), (dbc, OFF_BC, 512),
            (ddtf, OFF_DTF, 128)]
    d_z, d_xs, d_q, d_k, d_v, d_bcw, d_dtf = [
        _mm_tn("d_w_in_%d" % i, x0, a, tm=1024, tn=min(w, 1024), ts=ts, pro=_modulate, aux=(sc1, sh1))
        for i, (a, _, w) in enumerate(segs)]
    d_w_in = dict(z=d_z, xs=d_xs, bc=d_bcw, dt=d_dtf[:, :NH], q=d_q, k=d_k, v=d_v, f=d_dtf[:, NH:2 * NH])
    sent = send_grads("in", [_shard_w_in_grad(d_w_in)])
    segs[-1] = (_after(ddtf, sent), OFF_DTF, 128)
    dh1 = _mm_nt("d_h1", [(a, w, 0) for a, _, w in segs], [(wcat, w, off // w) for _, off, w in segs], n=D,
                 tm=min(512, s), tn=1024, out_dtype=F32)
    grad_x, d_sc1, d_sh1 = _input_grad(dh1, du1, x0, sc1, s)

    return dict(
        loss=(0.5 / D) * jnp.sum(sq_err), grad_x=grad_x,
        d_mod=jnp.concatenate([d_sh1, d_sc1, d_g1, d_sh2, d_sc2, d_g2], axis=1),
        d_conv_w=jnp.concatenate([d_wc_xs[:4], d_wc_bc[:4]], axis=1), d_conv_b=jnp.concatenate([d_bc_xs, d_bc_bc], axis=1),
        d_ssm_norm_w=d_ssm_w, d_attn_norm_w=d_attn_w, d_ln1_g=d_ln1_g, d_ln1_b=d_ln1_b, d_ln2_g=d_ln2_g, d_ln2_b=d_ln2_b,
        d_gate_bias=d_bias, d_a_log=d_alog, d_d_skip=d_dskip)


W_IN_SEGS = [('z', W_Z, D), ('xs', W_XS, D), ('bc', W_BC, 512), ('dt', W_DT, NH), ('q', W_Q, D), ('k', W_K, D),
             ('v', W_V, D), ('f', W_F, NH)]
SHARD_W = IN_COLS // N_DEV


def _pack_w_in(shards):
    def cols(lo, hi):
        pieces = []
        while lo < hi:
            dev = lo // SHARD_W
            end = min(hi, (dev + 1) * SHARD_W)
            pieces.append(shards[dev][:, lo - dev * SHARD_W:end - dev * SHARD_W])
            lo = end
        return pieces

    seg = {n: cols(off, off + w) for n, off, w in W_IN_SEGS}
    pieces = seg['z'] + seg['xs'] + seg['q'] + seg['k'] + seg['v'] + seg['bc'] + seg['dt'] + seg['f']
    return jnp.concatenate(pieces + [jnp.zeros((D, 128 - 2 * NH), shards.dtype)], axis=1)


def _shard_w_in_grad(d_w_in):
    blocks = []
    for dev in range(N_DEV):
        lo, hi = dev * SHARD_W, (dev + 1) * SHARD_W
        pieces = [d_w_in[n][:, max(lo, off) - off:min(hi, off + w) - off] for n, off, w in W_IN_SEGS
                  if max(lo, off) < min(hi, off + w)]
        pieces.append(jnp.zeros((D, -SHARD_W % 128), pieces[0].dtype))
        blocks.append(jnp.concatenate(pieces, axis=1))
    return jnp.stack(blocks, axis=0)


WEIGHTS = ['w_ada', 'b_ada', 'w_in', 'conv_w', 'conv_b', 'dt_bias', 'a_log', 'd_skip', 'ssm_norm_w', 'f_bias',
           'attn_norm_w', 'w_out', 'ln1_g', 'ln1_b', 'w_ff_in', 'w_ff_out', 'ln2_g', 'ln2_b']
BIG = ['w_in', 'w_out', 'w_ff_in', 'w_ff_out']
SMALL = ['b_ada', 'conv_b', 'ssm_norm_w', 'attn_norm_w', 'ln1_g', 'ln1_b', 'ln2_g', 'ln2_b', 'dt_bias', 'a_log', 'd_skip',
         'f_bias', 'conv_w']


def _pad_lanes(v, n=128):
    return jnp.pad(v, ((0, 0), (0, n - v.shape[1])))


def _small_block(vals):
    rows = [_pad_lanes(vals[n].reshape(1, -1), -(-vals[n].size // 128) * 128).reshape(-1, 128) for n in SMALL]
    block = jnp.concatenate(rows, axis=0)
    return jnp.pad(block, ((0, 120 - block.shape[0]), (0, 0)))


def _small_unblock(block, like):
    out, r = {}, 0
    for n in SMALL:
        size = like[n].size
        nr = -(-size // 128)
        out[n] = block[r:r + nr].reshape(-1)[:size].reshape(like[n].shape)
        r += nr
    return out


def kernel(x, c, w_ada, b_ada, w_in, conv_w, conv_b, dt_bias, a_log, d_skip, ssm_norm_w, f_bias, attn_norm_w, w_out, ln1_g, ln1_b, w_ff_in, w_ff_out, ln2_g, ln2_b, loss_target, m_w_ada, m_b_ada, m_w_in, m_conv_w, m_conv_b, m_dt_bias, m_a_log, m_d_skip, m_ssm_norm_w, m_f_bias, m_attn_norm_w, m_w_out, m_ln1_g, m_ln1_b, m_w_ff_in, m_w_ff_out, m_ln2_g, m_ln2_b, v_w_ada, v_b_ada, v_w_in, v_conv_w, v_conv_b, v_dt_bias, v_a_log, v_d_skip, v_ssm_norm_w, v_f_bias, v_attn_norm_w, v_w_out, v_ln1_g, v_ln1_b, v_w_ff_in, v_w_ff_out, v_ln2_g, v_ln2_b):
    args = dict(locals())
    w = {n: args[n] for n in WEIGHTS}
    m = {n: args['m_' + n] for n in WEIGHTS}
    v = {n: args['v_' + n] for n in WEIGHTS}
    me = 4 * lax.axis_index("x") + 2 * lax.axis_index("y") + lax.axis_index("c")
    ada_cols = 6 * D // N_DEV
    conv_cols = conv_w.shape[2]

    c_all, conv_all = _exchange("gather_cond", [c, conv_w[0]], False)
    c_all = c_all.reshape(N_DEV, D)
    conv_w_full = conv_all.transpose(1, 0, 2).reshape(4, N_DEV * conv_cols)
    b_shard = lax.dynamic_slice(b_ada, (0, me * ada_cols), (1, ada_cols))
    mod_all, = _exchange("gather_mod", [_ada_mod(c_all, w_ada[0], b_shard)], False)
    mod = lax.dynamic_index_in_dim(mod_all, me, axis=1, keepdims=False).reshape(1, 6 * D)

    win_s = _gather_two_level("gather_w_in", _after(w_in[0].astype(BF16), mod * 0))
    first_done = win_s[0, 0:1, 0:1] * 0
    rest = _exchange_async("gather_rest", [_after(w[n][0].astype(BF16), first_done) for n in BIG[1:]], False, 1)

    def late_weights(after):
        wout_s, w1s, w2_s = rest()
        return wout_s.reshape(2 * D, D), w1s, w2_s.reshape(DFF, D)

    sends = {}

    def send_grads(tag, blocks):
        sends[tag] = _exchange_async("scatter_" + tag, blocks, True, {'late': 2, 'in': 3}[tag])
        return sum(b.reshape(-1)[0].astype(F32) * 0 for b in blocks)

    out = _local_step(x[0], loss_target[0], mod, _pack_w_in(win_s), late_weights, send_grads,
                      conv_w_full, conv_b, dt_bias, a_log, d_skip, ssm_norm_w, f_bias, attn_norm_w, ln1_g, ln1_b, ln2_g, ln2_b)

    small = jnp.concatenate(
        [out['d_mod'], out['d_conv_w'].reshape(1, -1), out['d_conv_b'], out['d_ssm_norm_w'], out['d_attn_norm_w'],
         out['d_ln1_g'], out['d_ln1_b'], out['d_ln2_g'], out['d_ln2_b'], out['d_gate_bias'], out['d_a_log'],
         out['d_d_skip'], _pad_lanes(out['loss'].reshape(1, 1))], axis=1).reshape(-1, 128)
    small_landed = _exchange_async("gather_small", [small], False, 4)
    (g_ff_in, g_ff_out, g_out), (g_in,) = sends['late'](), sends['in']()
    g_parts = dict(w_ff_in=g_ff_in, w_ff_out=g_ff_out, w_out=g_out, w_in=g_in)
    big = {n: _adamw("adamw_" + n, w[n][0], g_parts[n], m[n][0], v[n][0], tr=256, slots=True) for n in BIG}
    big_done = sum(big[n][1][0:1, 0:1] * 0 for n in BIG)
    small_all = _after(small_landed()[0], big_done)
    ssum = _sum_slots("sum_small", small_all)
    dmod_all = small_all[:, :6 * D // 128].reshape(N_DEV, 6 * D)
    g_w_ada, g_b_ada = _ada_grad(c_all, lax.dynamic_slice(dmod_all, (0, me * ada_cols), (N_DEV, ada_cols)), dmod_all)
    rows = lambda a, b: ssum[a:b].reshape(1, -1)
    g_conv_w = lax.dynamic_slice(ssum[48:96].reshape(4, N_DEV * conv_cols), (0, me * conv_cols), (4, conv_cols))
    g_small = dict(b_ada=g_b_ada, conv_w=g_conv_w[None], conv_b=rows(96, 108), ssm_norm_w=rows(108, 116),
                   attn_norm_w=rows(116, 124), ln1_g=rows(124, 132), ln1_b=rows(132, 140), ln2_g=rows(140, 148),
                   ln2_b=rows(148, 156), dt_bias=ssum[156:157, :NH], f_bias=ssum[156:157, NH:2 * NH],
                   a_log=ssum[157:158, :NH], d_skip=ssum[158:159, :NH])
    sm = _adamw("adamw_small", _small_block(w), _small_block(g_small), _small_block(m), _small_block(v), tr=120, slots=False)
    ada = _adamw("adamw_ada", w_ada[0], g_w_ada, m_w_ada[0], v_w_ada[0], tr=256, slots=False)

    results = []
    for k in range(4):
        vals = _small_unblock(sm[k], w)
        vals['w_ada'] = ada[k][None]
        for n in BIG:
            vals[n] = big[n][k][None]
        results.append(vals)
    return (ssum[159, 0], out['grad_x'][None], *[res[n] for res in results for n in WEIGHTS])
```

```python
import functools

import jax
import jax.numpy as jnp
from jax import lax
from jax.experimental import pallas as pl
from jax.experimental.pallas import tpu as pltpu
from jax.experimental.pallas import tpu_sc as plsc

F32, BF16 = jnp.float32, jnp.bfloat16

N_DEV = 8
D = 1024
NH, HD = 16, 64
NSTATE = 128
CHUNK = 128
HG = 8
DFF = 4096
ALPHA = 2.0 ** 0.25
EPS = 1e-5
ATT_SCALE = HD ** -0.5

OFF_Z, OFF_XS, OFF_Q, OFF_K, OFF_V, OFF_BC, OFF_DTF = 0, 1024, 2048, 3072, 4096, 5120, 5632
PCOLS = 5760
W_Z, W_XS, W_BC, W_DT, W_Q, W_K, W_V, W_F = 0, 1024, 2048, 2560, 2576, 3600, 4624, 5648
IN_COLS = 5664

ADAM_LR, ADAM_B1, ADAM_B2, ADAM_EPS, ADAM_WD, ADAM_STEP = 0.001, 0.9, 0.999, 1e-08, 0.01, 10

VMEM_LIMIT = 56 << 20

NN = (((1,), (0,)), ((), ()))
NT = (((1,), (1,)), ((), ()))
TN = (((0,), (0,)), ((), ()))


def _dot(a, b, dims=NN):
    return lax.dot_general(a, b, dims, preferred_element_type=F32)


def _bdot(a, b, dims=NN):
    return _dot(a.astype(BF16), b.astype(BF16), dims)


def _split3(v, terms=3):
    parts, rest = [], v
    for _ in range(terms):
        p = rest.astype(BF16)
        parts.append(p)
        rest = rest - p.astype(F32)
    return parts


def _sel_left(m01, v):
    return sum(_dot(m01, p) for p in _split3(v))


def _sel_right(v, m01, dims=NN, terms=3):
    return sum(_dot(p, m01, dims) for p in _split3(v, terms))


def _iota(shape, dim):
    return lax.broadcasted_iota(jnp.int32, shape, dim)


def _tri_lower(n):
    return (_iota((n, n), 1) <= _iota((n, n), 0)).astype(BF16)


def _tri_upper(n):
    return (_iota((n, n), 1) >= _iota((n, n), 0)).astype(BF16)


def _head_expand():
    return (lax.shift_right_logical(_iota((128, D), 1), 6) == _iota((128, D), 0)).astype(BF16)


def _head_reduce():
    return (lax.shift_right_logical(_iota((D, 128), 0), 6) == _iota((D, 128), 1)).astype(BF16)


def _sigmoid(x):
    return 0.5 * jnp.tanh(0.5 * x) + 0.5


def _silu(x):
    return x * _sigmoid(x)


def _dsilu(x):
    s = _sigmoid(x)
    return s * (1.0 + x * (1.0 - s))


def _softplus(x):
    return jnp.maximum(x, 0.0) + jnp.log(1.0 + jnp.exp(-jnp.abs(x)))


def _log_sigmoid(x):
    return jnp.minimum(x, 0.0) - jnp.log(1.0 + jnp.exp(-jnp.abs(x)))


def _params(sem):
    return pltpu.CompilerParams(dimension_semantics=sem, vmem_limit_bytes=VMEM_LIMIT)


def _mm_nn(name, a, b, *, tm, tn, tk, out_dtype, pro=None, aux=()):
    m, k_all = a.shape
    b_sharded = b.ndim == 3
    n = b.shape[0] * b.shape[2] if b_sharded else b.shape[1]
    assert not b_sharded or tn == b.shape[2]
    nk = k_all // tk
    n_aux = len(aux)
    b_spec = (pl.BlockSpec((None, tk, tn), lambda i, j, k: (j, k, 0)) if b_sharded
              else pl.BlockSpec((tk, tn), lambda i, j, k: (k, j)))

    def body(a_ref, b_ref, *rest):
        aux_refs, o_ref = rest[:n_aux], rest[n_aux]
        at = a_ref[...]
        if pro is not None:
            at = pro(at, *[r[...] for r in aux_refs])
        part = _bdot(at, b_ref[...])
        if nk == 1:
            o_ref[...] = part.astype(out_dtype)
            return
        acc_ref = rest[n_aux + 1]
        kk = pl.program_id(2)

        @pl.when(kk == 0)
        def _():
            acc_ref[...] = part

        @pl.when(kk > 0)
        def _():
            acc_ref[...] += part

        @pl.when(kk == nk - 1)
        def _():
            o_ref[...] = acc_ref[...].astype(out_dtype)

    return pl.pallas_call(
        body, name=name,
        grid=(m // tm, n // tn, nk),
        in_specs=[pl.BlockSpec((tm, tk), lambda i, j, k: (i, k)), b_spec]
        + [pl.BlockSpec((1, tk), lambda i, j, k: (0, k)) for _ in aux],
        out_specs=pl.BlockSpec((tm, tn), lambda i, j, k: (i, j)),
        out_shape=jax.ShapeDtypeStruct((m, n), out_dtype),
        scratch_shapes=[] if nk == 1 else [pltpu.VMEM((tm, tn), F32)],
        compiler_params=_params(("parallel", "parallel", "arbitrary")),
    )(a, b, *aux)


def _mm_nt(name, a_list, b_list, *, n, tm, tn, out_dtype, epi=None, epi_aux=()):
    m = a_list[0][0].shape[0]
    n_op = len(a_list)
    n_epi = len(epi_aux)

    def body(*refs):
        a_refs, b_refs = refs[:n_op], refs[n_op:2 * n_op]
        e_refs, o_ref = refs[2 * n_op:2 * n_op + n_epi], refs[2 * n_op + n_epi]
        acc = None
        for a_ref, b_ref in zip(a_refs, b_refs):
            part = _bdot(a_ref[...], b_ref[...], NT)
            acc = part if acc is None else acc + part
        if epi is not None:
            acc = epi(acc, *[r[...] for r in e_refs])
        o_ref[...] = acc.astype(out_dtype)

    in_specs = [pl.BlockSpec((tm, w), functools.partial(lambda i, j, cb: (i, cb), cb=cb)) for (_, w, cb) in a_list]
    for (b, w, cb) in b_list:
        if b.ndim == 3:
            in_specs.append(pl.BlockSpec((None, tn, w), functools.partial(lambda i, j, cb: (cb, j, 0), cb=cb)))
        else:
            in_specs.append(pl.BlockSpec((tn, w), functools.partial(lambda i, j, cb: (j, cb), cb=cb)))
    in_specs += [pl.BlockSpec((tm, tn), lambda i, j: (i, j)) for _ in epi_aux]
    return pl.pallas_call(
        body, name=name,
        grid=(m // tm, n // tn),
        in_specs=in_specs,
        out_specs=pl.BlockSpec((tm, tn), lambda i, j: (i, j)),
        out_shape=jax.ShapeDtypeStruct((m, n), out_dtype),
        compiler_params=_params(("parallel", "parallel")),
    )(*[a for (a, _, _) in a_list], *[b for (b, _, _) in b_list], *epi_aux)


def _mm_tn(name, a, b, *, tm, tn, ts, pro=None, aux=(), col_shards=False):
    s_all, ka = a.shape
    nb = b.shape[1]
    n_aux = len(aux)
    ns = s_all // ts
    assert not col_shards or tn == nb // N_DEV

    def body(a_ref, b_ref, *rest):
        aux_refs, o_ref, acc_ref = rest[:n_aux], rest[n_aux], rest[n_aux + 1]
        at = a_ref[...]
        if pro is not None:
            at = pro(at, *[r[...] for r in aux_refs])
        part = _bdot(at, b_ref[...], TN)
        ss = pl.program_id(2)

        @pl.when(ss == 0)
        def _():
            acc_ref[...] = part

        @pl.when(ss > 0)
        def _():
            acc_ref[...] += part

        @pl.when(ss == ns - 1)
        def _():
            o_ref[...] = acc_ref[...].astype(BF16)

    if col_shards:
        out_spec = pl.BlockSpec((None, tm, tn), lambda i, j, s: (j, i, 0))
        out_shape = jax.ShapeDtypeStruct((N_DEV, ka, tn), BF16)
    else:
        out_spec = pl.BlockSpec((tm, tn), lambda i, j, s: (i, j))
        out_shape = jax.ShapeDtypeStruct((ka, nb), BF16)
    return pl.pallas_call(
        body, name=name,
        grid=(ka // tm, nb // tn, ns),
        in_specs=[pl.BlockSpec((ts, tm), lambda i, j, s: (s, i)),
                  pl.BlockSpec((ts, tn), lambda i, j, s: (s, j))]
        + [pl.BlockSpec((1, tm), lambda i, j, s: (0, i)) for _ in aux],
        out_specs=out_spec, out_shape=out_shape,
        scratch_shapes=[pltpu.VMEM((tm, tn), F32)],
        compiler_params=_params(("parallel", "parallel", "arbitrary")),
    )(a, b, *aux)


def _rowk(name, fn, n_rows, tr, rows, fulls, outs, accs, reverse=False):
    n = n_rows // tr
    n_row, n_full, n_out, n_acc = len(rows), len(fulls), len(outs), len(accs)

    def pos(i):
        return (n - 1 - i) if reverse else i

    def body(*refs):
        row_refs = refs[:n_row]
        full_refs = refs[n_row:n_row + n_full]
        out_refs = refs[n_row + n_full:n_row + n_full + n_out]
        acc_refs = refs[n_row + n_full + n_out:]
        i = pl.program_id(0)

        @pl.when(i == 0)
        def _():
            for r in acc_refs:
                r[...] = jnp.zeros(r.shape, r.dtype)

        res = fn(pos(i), *[r[...] for r in row_refs], *[r[...] for r in full_refs], *[r[...] for r in acc_refs])
        for r, v in zip(out_refs + acc_refs, res):
            r[...] = v.astype(r.dtype)

    def row_map(i, cb, shift):
        return (jnp.clip(pos(i) + shift, 0, n - 1), cb)

    def halo_map(i, cb, shift):
        tile = jnp.clip(pos(i) + shift, 0, n - 1)
        return (tile * (tr // 8) + (tr // 8 - 1 if shift < 0 else 0), cb)

    in_specs = [pl.BlockSpec((tr, w), functools.partial(row_map, cb=cb, shift=sh)) if sh == 0 else
                pl.BlockSpec((8, w), functools.partial(halo_map, cb=cb, shift=sh)) for (_, w, cb, sh) in rows]
    in_specs += [pl.BlockSpec(f.shape, functools.partial(lambda i, nd: (0,) * nd, nd=f.ndim)) for f in fulls]
    out_specs = [pl.BlockSpec((tr, w), lambda i: (pos(i), 0)) for (w, _) in outs]
    out_specs += [pl.BlockSpec((r, w), lambda i: (0, 0)) for (r, w) in accs]
    out_shape = [jax.ShapeDtypeStruct((n_rows, w), dt) for (w, dt) in outs]
    out_shape += [jax.ShapeDtypeStruct((r, w), F32) for (r, w) in accs]
    return pl.pallas_call(
        body, name=name, grid=(n,), in_specs=in_specs, out_specs=out_specs, out_shape=out_shape,
        compiler_params=_params(("arbitrary",)),
    )(*[a for (a, _, _, _) in rows], *fulls)


def _colsum(x):
    return jnp.sum(x, axis=0, keepdims=True)


def _mean(x):
    return jnp.mean(x, axis=-1, keepdims=True)


def _modulate(x, sc, sh):
    return x * (1.0 + sc) + sh


def _shift_down(cur, prev8, j):
    tr = cur.shape[0]
    row8 = _iota(prev8.shape, 0)
    head = jnp.where(row8 < j, pltpu.roll(prev8, j, 0), pltpu.roll(cur[0:8], j, 0))
    return head if tr == 8 else jnp.concatenate([head, pltpu.roll(cur, j, 0)[8:]], axis=0)


def _shift_up(cur, next8, j):
    tr = cur.shape[0]
    row8 = _iota(next8.shape, 0)
    tail = jnp.where(row8 < 8 - j, pltpu.roll(cur[tr - 8:], 8 - j, 0), pltpu.roll(next8, 8 - j, 0))
    return jnp.concatenate([pltpu.roll(cur, tr - j, 0)[:tr - 8], tail], axis=0)


def _conv(cur, prev, w, b):
    out = cur * w[3:4] + b
    for j in (1, 2, 3):
        out = out + _shift_down(cur, prev, j) * w[3 - j:4 - j]
    return out


def _conv_fwd(p, w_xs, b_xs, w_bc, b_bc, s):
    def fn(pos, xs, xs_prev, bc, bc_prev, w_xs, b_xs, w_bc, b_bc):
        first = pos == 0
        xs_prev = jnp.where(first, 0.0, xs_prev)
        bc_prev = jnp.where(first, 0.0, bc_prev)
        return _silu(_conv(xs, xs_prev, w_xs, b_xs)), _silu(_conv(bc, bc_prev, w_bc, b_bc))

    return _rowk("conv_fwd", fn, s, 256,
                 [(p, D, OFF_XS // D, 0), (p, D, OFF_XS // D, -1), (p, 512, OFF_BC // 512, 0), (p, 512, OFF_BC // 512, -1)],
                 [w_xs, b_xs, w_bc, b_bc], [(D, F32), (512, F32)], [])


def _conv_bwd(dxs_a, dbc_a, p, w_xs, b_xs, w_bc, b_bc, s):
    tr = 256
    n = s // tr

    def fn(pos, da1, da1n, x1, x1p, x1n, da2, da2n, x2, x2p, x2n, w1, b1, w2, b2, aw1, ab1, aw2, ab2):
        dx1, dw1, db1 = _conv_bwd_fn(pos, n, da1, da1n, x1, x1p, x1n, w1, b1)
        dx2, dw2, db2 = _conv_bwd_fn(pos, n, da2, da2n, x2, x2p, x2n, w2, b2)
        return dx1, dx2, aw1 + dw1, ab1 + db1, aw2 + dw2, ab2 + db2

    cx, cb = OFF_XS // D, OFF_BC // 512
    return _rowk("conv_bwd", fn, s, tr,
                 [(dxs_a, D, 0, 0), (dxs_a, D, 0, 1), (p, D, cx, 0), (p, D, cx, -1), (p, D, cx, 1),
                  (dbc_a, 512, 0, 0), (dbc_a, 512, 0, 1), (p, 512, cb, 0), (p, 512, cb, -1), (p, 512, cb, 1)],
                 [w_xs, b_xs, w_bc, b_bc], [(D, BF16), (512, BF16)], [(8, D), (1, D), (8, 512), (1, 512)])


def _conv_bwd_fn(pos, n, da, da_next, x, x_prev, x_next, w, b):
    first, last = pos == 0, pos == n - 1
    x_prev = jnp.where(first, 0.0, x_prev)
    shifted = {j: _shift_down(x, x_prev, j) for j in (1, 2, 3)}
    conv = x * w[3:4] + b
    for j in (1, 2, 3):
        conv = conv + shifted[j] * w[3 - j:4 - j]
    dc = da * _dsilu(conv)
    dc_next = jnp.where(last, 0.0, da_next * _dsilu(_conv(x_next, x[x.shape[0] - 8:], w, b)))
    dx = dc * w[3:4]
    dws = [None] * 4
    dws[3] = _colsum(dc * x)
    for j in (1, 2, 3):
        dx = dx + _shift_up(dc, dc_next, j) * w[3 - j:4 - j]
        dws[3 - j] = _colsum(dc * shifted[j])
    row = _iota((8, x.shape[1]), 0)
    dw = jnp.zeros((8, x.shape[1]), F32)
    for k in range(4):
        dw = jnp.where(row == k, dws[k], dw)
    return dx, dw, _colsum(dc)


def _ssd_gates(dtf, bias, a_log):
    lane = _iota(dtf.shape, 1)
    head = lane < NH
    dt = jnp.where(head, _softplus(dtf + bias), 0.0)
    a_neg = jnp.where(_iota(a_log.shape, 1) < NH, -jnp.exp(a_log), 0.0)
    a = dt * a_neg
    cs = _sel_left(_tri_lower(CHUNK), a)
    return dt, a_neg, cs


def _decay_mask(cs_ref, cst_ref, h):
    diff = cs_ref[:, h:h + 1] - cst_ref[h:h + 1, :]
    low = _iota((CHUNK, CHUNK), 1) <= _iota((CHUNK, CHUNK), 0)
    return jnp.where(low, jnp.exp(jnp.minimum(diff, 0.0)), 0.0)


def _ssd_fwd(xs_a, bc_a, p, bias128, alog128, dskip_x, s):
    nc = s // CHUNK
    t = CHUNK

    def body(xs_ref, bc_ref, dtf_ref, bias_ref, alog_ref, dsk_ref, y_ref, st_ref,
             state, x_sc, xw_sc, cs_sc, cst_sc, yd_sc):
        c = pl.program_id(0)

        @pl.when(c == 0)
        def _():
            state[...] = jnp.zeros(state.shape, F32)

        dt, _, cs = _ssd_gates(dtf_ref[...], bias_ref[...], alog_ref[...])
        cs_sc[...] = cs
        cst_sc[...] = cs.T
        cs_last = cs[t - 1:t, :]
        expand = _head_expand()
        ex = _sel_right(jnp.concatenate([dt, jnp.exp(cs), jnp.exp(cs_last - cs)], axis=0), expand, terms=2)
        dt_x, eo_x, we_x = ex[0:t], ex[t:2 * t], ex[2 * t:3 * t]
        g_x = _sel_right(jnp.broadcast_to(jnp.exp(cs_last), (8, 128)), expand)[0:1]
        xs = xs_ref[...]
        x = xs * dt_x
        x_sc[...] = x.astype(BF16)
        xw_sc[...] = (x * we_x).astype(BF16)
        prev = state[...]
        st_ref[0] = prev
        prev_b = prev.astype(BF16)
        for g in range(2):
            cols = slice(g * 512, (g + 1) * 512)
            b_g = bc_ref[:, g * 128:(g + 1) * 128].astype(BF16)
            c_g = bc_ref[:, 256 + g * 128:256 + (g + 1) * 128].astype(BF16)
            gmat = _dot(c_g, b_g, NT)
            y_off = _dot(c_g, prev_b[:, cols]) * eo_x[:, cols]
            s_loc = _dot(b_g, xw_sc[:, cols], TN)
            state[:, cols] = g_x[:, cols] * prev[:, cols] + s_loc
            for e in range(HG):
                h = g * HG + e
                m = gmat * _decay_mask(cs_sc, cst_sc, h)
                yd_sc[:, h * HD:(h + 1) * HD] = _dot(m.astype(BF16), x_sc[:, h * HD:(h + 1) * HD])
            y_ref[:, cols] = yd_sc[:, cols] + y_off + dsk_ref[:, cols] * xs[:, cols]

    return pl.pallas_call(
        body, name="ssd_fwd", grid=(nc,),
        in_specs=[pl.BlockSpec((t, D), lambda c: (c, 0)),
                  pl.BlockSpec((t, 512), lambda c: (c, 0)),
                  pl.BlockSpec((t, 128), lambda c: (c, OFF_DTF // 128)),
                  pl.BlockSpec((1, 128), lambda c: (0, 0)),
                  pl.BlockSpec((1, 128), lambda c: (0, 0)),
                  pl.BlockSpec((1, D), lambda c: (0, 0))],
        out_specs=[pl.BlockSpec((t, D), lambda c: (c, 0)),
                   pl.BlockSpec((1, NSTATE, D), lambda c: (c, 0, 0))],
        out_shape=[jax.ShapeDtypeStruct((s, D), F32), jax.ShapeDtypeStruct((nc, NSTATE, D), F32)],
        scratch_shapes=[pltpu.VMEM((NSTATE, D), F32), pltpu.VMEM((t, D), BF16), pltpu.VMEM((t, D), BF16),
                        pltpu.VMEM((t, 128), F32), pltpu.VMEM((128, t), F32), pltpu.VMEM((t, D), F32)],
        compiler_params=_params(("arbitrary",)),
    )(xs_a, bc_a, p, bias128, alog128, dskip_x)


def _ssd_bwd(dy, xs_a, bc_a, p, states, bias128, alog128, dskip_x, s):
    nc = s // CHUNK
    t = CHUNK

    def body(dy_ref, xs_ref, bc_ref, dtf_ref, st_ref, bias_ref, alog_ref, dsk_ref,
             dxs_ref, dbc_ref, ddt_ref, dalog_ref, dskip_ref,
             dstate, x_sc, dy_sc, dx_sc, deo_sc, dwe_sc, cs_sc, cst_sc, dcol_sc, drow_sc):
        i = pl.program_id(0)

        @pl.when(i == 0)
        def _():
            dstate[...] = jnp.zeros(dstate.shape, F32)
            dalog_ref[...] = jnp.zeros(dalog_ref.shape, F32)
            dskip_ref[...] = jnp.zeros(dskip_ref.shape, F32)

        dtf = dtf_ref[...]
        dt, a_neg, cs = _ssd_gates(dtf, bias_ref[...], alog_ref[...])
        cs_sc[...] = cs
        cst_sc[...] = cs.T
        cs_last = cs[t - 1:t, :]
        eo, we, g_end = jnp.exp(cs), jnp.exp(cs_last - cs), jnp.exp(cs_last)
        expand, reduce = _head_expand(), _head_reduce()
        ex = _sel_right(jnp.concatenate([dt, eo, we], axis=0), expand, terms=2)
        dt_x, eo_x, we_x = ex[0:t], ex[t:2 * t], ex[2 * t:3 * t]
        g_x = _sel_right(jnp.broadcast_to(g_end, (8, 128)), expand)[0:1]
        xs = xs_ref[...]
        dyv = dy_ref[...]
        x = xs * dt_x
        x_sc[...] = x.astype(BF16)
        dy_sc[...] = dyv.astype(BF16)
        dyo_b = (dyv * eo_x).astype(BF16)
        xw_b = (x * we_x).astype(BF16)
        prev = st_ref[0]
        prev_b = prev.astype(BF16)
        dnext = dstate[...]
        dnext_b = dnext.astype(BF16)
        dcol_sc[...] = jnp.zeros(dcol_sc.shape, F32)
        drow_sc[...] = jnp.zeros(drow_sc.shape, F32)
        lane_row = _iota((1, 128), 1)
        sub_col = _iota((128, 1), 0)
        for g in range(2):
            cols = slice(g * 512, (g + 1) * 512)
            b_g = bc_ref[:, g * 128:(g + 1) * 128].astype(BF16)
            c_g = bc_ref[:, 256 + g * 128:256 + (g + 1) * 128].astype(BF16)
            gmat = _dot(c_g, b_g, NT)
            b_ds = _dot(b_g, dnext_b[:, cols])
            c_s = _dot(c_g, prev_b[:, cols])
            dx_sc[:, cols] = b_ds * we_x[:, cols]
            deo_sc[:, cols] = dyv[:, cols] * c_s
            dwe_sc[:, cols] = b_ds * x[:, cols]
            db = _dot(xw_b[:, cols], dnext_b[:, cols], NT)
            dc = _dot(dyo_b[:, cols], prev_b[:, cols], NT)
            dstate[:, cols] = g_x[:, cols] * dnext[:, cols] + _dot(c_g, dyo_b[:, cols], TN)
            dg = jnp.zeros((t, t), F32)
            for e in range(HG):
                h = g * HG + e
                hc = slice(h * HD, (h + 1) * HD)
                lmat = _decay_mask(cs_sc, cst_sc, h)
                m = gmat * lmat
                dx_sc[:, hc] += _dot(m.astype(BF16), dy_sc[:, hc], TN)
                dm = _dot(dy_sc[:, hc], x_sc[:, hc], NT)
                dg = dg + dm * lmat
                qm = dm * m
                dcol_sc[...] += jnp.sum(qm, axis=1, keepdims=True) * (lane_row == h).astype(F32)
                drow_sc[...] += (sub_col == h).astype(F32) * jnp.sum(qm, axis=0, keepdims=True)
            dg_b = dg.astype(BF16)
            dbc_ref[:, g * 128:(g + 1) * 128] = db + _dot(dg_b, c_g, TN)
            dbc_ref[:, 256 + g * 128:256 + (g + 1) * 128] = dc + _dot(dg_b, b_g)
        d_eo = _sel_right(deo_sc[...], reduce, terms=2)
        d_we = _sel_right(dwe_sc[...], reduce, terms=2)
        d_gend = _sel_right(jnp.broadcast_to(_colsum(dnext * prev), (8, D)), reduce)[0:1]
        d_cs = dcol_sc[...] - drow_sc[...].T + d_eo * eo - d_we * we
        extra = _colsum(d_we * we) + d_gend * g_end
        d_cs = d_cs + jnp.where(_iota((t, 128), 0) == t - 1, extra, 0.0)
        da = _sel_left(_tri_upper(t), d_cs)
        dx = dx_sc[...]
        ddt = _sel_right(dx * xs, reduce, terms=2) + da * a_neg
        dxs_ref[...] = dx * dt_x + dsk_ref[...] * dyv
        ddt_ref[...] = jnp.where(_iota((t, 128), 1) < NH, ddt * _sigmoid(dtf + bias_ref[...]), 0.0)
        dalog_ref[...] += _colsum(da * dt) * a_neg
        dskip_ref[...] += _sel_right(jnp.broadcast_to(_colsum(dyv * xs), (8, D)), reduce)[0:1]

    rev = lambda i: nc - 1 - i
    return pl.pallas_call(
        body, name="ssd_bwd", grid=(nc,),
        in_specs=[pl.BlockSpec((t, D), lambda i: (rev(i), 0)),
                  pl.BlockSpec((t, D), lambda i: (rev(i), 0)),
                  pl.BlockSpec((t, 512), lambda i: (rev(i), 0)),
                  pl.BlockSpec((t, 128), lambda i: (rev(i), OFF_DTF // 128)),
                  pl.BlockSpec((1, NSTATE, D), lambda i: (rev(i), 0, 0)),
                  pl.BlockSpec((1, 128), lambda i: (0, 0)),
                  pl.BlockSpec((1, 128), lambda i: (0, 0)),
                  pl.BlockSpec((1, D), lambda i: (0, 0))],
        out_specs=[pl.BlockSpec((t, D), lambda i: (rev(i), 0)),
                   pl.BlockSpec((t, 512), lambda i: (rev(i), 0)),
                   pl.BlockSpec((t, 128), lambda i: (rev(i), 0)),
                   pl.BlockSpec((1, 128), lambda i: (0, 0)),
                   pl.BlockSpec((1, 128), lambda i: (0, 0))],
        out_shape=[jax.ShapeDtypeStruct((s, D), F32), jax.ShapeDtypeStruct((s, 512), F32),
                   jax.ShapeDtypeStruct((s, 128), F32), jax.ShapeDtypeStruct((1, 128), F32),
                   jax.ShapeDtypeStruct((1, 128), F32)],
        scratch_shapes=[pltpu.VMEM((NSTATE, D), F32), pltpu.VMEM((t, D), BF16), pltpu.VMEM((t, D), BF16),
                        pltpu.VMEM((t, D), F32), pltpu.VMEM((t, D), F32), pltpu.VMEM((t, D), F32),
                        pltpu.VMEM((t, 128), F32), pltpu.VMEM((128, t), F32),
                        pltpu.VMEM((t, 128), F32), pltpu.VMEM((128, t), F32)],
        compiler_params=_params(("arbitrary",)),
    )(dy, xs_a, bc_a, p, states, bias128, alog128, dskip_x)


def _gate_lanes(shape):
    lane = _iota(shape, 1)
    return (lane >= NH) & (lane < 2 * NH)


def _cum_fwd(p, bias128, s):
    tr = min(512, s)

    def body(dtf_ref, bias_ref, o_ref, carry):
        @pl.when(pl.program_id(0) == 0)
        def _():
            carry[...] = jnp.zeros(carry.shape, F32)

        lf = jnp.where(_gate_lanes((tr, 128)), _log_sigmoid(dtf_ref[...] + bias_ref[...]), 0.0)
        cum = _sel_left(_tri_lower(tr), lf) + carry[...]
        carry[...] = cum[tr - 1:tr, :]
        o_ref[...] = cum

    return pl.pallas_call(
        body, name="cum_fwd", grid=(s // tr,),
        in_specs=[pl.BlockSpec((tr, 128), lambda i: (i, OFF_DTF // 128)), pl.BlockSpec((1, 128), lambda i: (0, 0))],
        out_specs=pl.BlockSpec((tr, 128), lambda i: (i, 0)),
        out_shape=jax.ShapeDtypeStruct((s, 128), F32),
        scratch_shapes=[pltpu.VMEM((1, 128), F32)],
        compiler_params=_params(("arbitrary",)),
    )(p, bias128)


def _cum_bwd(dr_col, dcs, ddt_raw, p, bias128, s):
    tr = min(512, s)

    def fn(pos, dr, dc, ddt, dtf, bias, carry, acc):
        pick = (_iota((D, 128), 0) == (_iota((D, 128), 1) - NH) * HD).astype(BF16)
        dcum = dr - _sel_right(dc, pick)
        suffix = _sel_left(_tri_upper(tr), dcum) + carry
        dfr = jnp.where(_gate_lanes((tr, 128)), suffix * _sigmoid(-(dtf + bias)), 0.0)
        out = ddt + dfr
        return out, suffix[0:1, :], acc + _colsum(out)

    return _rowk("cum_bwd", fn, s, tr,
                 [(dr_col, 128, 0, 0), (dcs, D, 0, 0), (ddt_raw, 128, 0, 0), (p, 128, OFF_DTF // 128, 0)],
                 [bias128], [(128, BF16)], [(1, 128), (1, 128)], reverse=True)


ATT_BLOCK = 512
ATT_STRIP = 32


def _head_part(shape, h, dim):
    i = _iota(shape, dim)
    return (i >= h * HD) & (i < (h + 1) * HD)


def _k_augmented(k_blk, cum_blk, j, h):
    tk = k_blk.shape[0]
    lane = _iota((tk, 128), 1)
    col = jnp.sum(jnp.where(lane == NH + 2 * j + h, cum_blk, 0.0), axis=1, keepdims=True)
    c0, c1, c2 = [c.astype(F32) for c in _split3(-col)]
    aug = jnp.where(lane == 0, c0, jnp.where(lane == 1, c1, jnp.where(lane == 2, c2, 0.0)))
    return jnp.concatenate([jnp.where(_head_part((tk, 128), h, 1), k_blk, 0.0), aug], axis=1).astype(BF16)


def _q_augmented_t(q_blk):
    tq = q_blk.shape[0]
    ones = (_iota((128, tq), 0) < 3).astype(BF16)
    return jnp.concatenate([(q_blk * ATT_SCALE).T.astype(BF16), ones], axis=0)


def _rows01(r0, r1):
    sub = _iota((8, r0.shape[1]), 0)
    return jnp.where(sub == 0, r0, jnp.where(sub == 1, r1, 0.0))


def _fold8(x, op, cur):
    for g in range(x.shape[0] // 8):
        cur = op(cur, x[8 * g:8 * (g + 1), :])
    return cur


def _attn_fwd(p, cum, s):
    t = min(ATT_BLOCK, s)
    nq = s // t
    r = ATT_STRIP

    def body(q_ref, k_ref, v_ref, c_ref, o_ref, lse_ref, kaug_sc, vt_sc, s_sc, p_sc, m_sc, l_sc, acc_sc):
        j, qi = pl.program_id(0), pl.program_id(1)

        @pl.when(qi == 0)
        def _():
            for c in range(nq):
                rows = slice(c * t, (c + 1) * t)
                k_blk, vt = k_ref[rows, :], v_ref[rows, :].T
                for h in range(2):
                    kaug_sc[h, rows, :] = _k_augmented(k_blk, c_ref[rows, :], j, h)
                    vt_sc[h, :, rows] = jnp.where(_head_part((128, t), h, 0), vt, 0.0).astype(BF16)

        qaug_t = _q_augmented_t(q_ref[...])
        m_sc[...] = jnp.full(m_sc.shape, -1e30, F32)
        l_sc[...] = jnp.zeros(l_sc.shape, F32)
        acc_sc[...] = jnp.zeros(acc_sc.shape, F32)
        top = _iota((128, t), 0) < HD

        def logits(kb, buf):
            kv = pl.ds(pl.multiple_of(kb * t, t), t)
            for h in range(2):
                s_sc[buf, h] = _dot(kaug_sc[h, kv, :], qaug_t)

        def softmax(buf, diagonal):
            alphas = []
            for h in range(2):
                cur = jnp.full((8, t), -1e30, F32)
                for i in range(t // r):
                    rows = slice(i * r, (i + 1) * r)
                    x = s_sc[buf, h, rows, :]
                    if diagonal:
                        x = jnp.where(_iota((r, t), 1) >= i * r + _iota((r, t), 0), x, -1e30)
                        s_sc[buf, h, rows, :] = x
                    cur = _fold8(x, jnp.maximum, cur)
                m_prev = m_sc[h, 0:1, :]
                m_new = jnp.maximum(m_prev, jnp.max(cur, axis=0, keepdims=True))
                alpha = jnp.exp(m_prev - m_new)
                m_sc[h, 0:1, :] = m_new
                alphas.append(alpha)
                tot = jnp.zeros((8, t), F32)
                for i in range(t // r):
                    rows = slice(i * r, (i + 1) * r)
                    pr = jnp.exp(s_sc[buf, h, rows, :] - m_new)
                    p_sc[buf, h, rows, :] = pr.astype(BF16)
                    tot = _fold8(pr, jnp.add, tot)
                l_sc[h, 0:1, :] = alpha * l_sc[h, 0:1, :] + jnp.sum(tot, axis=0, keepdims=True)
            return alphas

        def accumulate(kb, buf, alphas):
            kv = pl.ds(pl.multiple_of(kb * t, t), t)
            acc_sc[...] = (acc_sc[...] * jnp.where(top, alphas[0], alphas[1])
                           + _dot(vt_sc[0, :, kv], p_sc[buf, 0]) + _dot(vt_sc[1, :, kv], p_sc[buf, 1]))

        def pair(a, b, b_diagonal):
            logits(a, 0)
            logits(b, 1)
            accumulate(a, 0, softmax(0, False))
            accumulate(b, 1, softmax(1, b_diagonal))

        def earlier(u, carry):
            pair(2 * u, 2 * u + 1, False)
            return carry

        lax.fori_loop(0, qi // 2, earlier, 0)

        @pl.when(qi % 2 == 1)
        def _():
            pair(qi - 1, qi, True)

        @pl.when(qi % 2 == 0)
        def _():
            logits(qi, 0)
            accumulate(qi, 0, softmax(0, True))

        l0, l1 = l_sc[0, 0:1, :], l_sc[1, 0:1, :]
        o_ref[...] = (acc_sc[...] / jnp.where(top, l0, l1)).T
        lse_ref[0] = _rows01(m_sc[0, 0:1, :] + jnp.log(l0), m_sc[1, 0:1, :] + jnp.log(l1))

    return pl.pallas_call(
        body, name="attn_fwd", grid=(NH // 2, nq),
        in_specs=[pl.BlockSpec((t, 128), lambda j, qi: (qi, OFF_Q // 128 + j)),
                  pl.BlockSpec((s, 128), lambda j, qi: (0, OFF_K // 128 + j)),
                  pl.BlockSpec((s, 128), lambda j, qi: (0, OFF_V // 128 + j)),
                  pl.BlockSpec((s, 128), lambda j, qi: (0, 0))],
        out_specs=[pl.BlockSpec((t, 128), lambda j, qi: (qi, j)),
                   pl.BlockSpec((1, 8, t), lambda j, qi: (j, 0, qi))],
        out_shape=[jax.ShapeDtypeStruct((s, D), F32), jax.ShapeDtypeStruct((NH // 2, 8, s), F32)],
        scratch_shapes=[pltpu.VMEM((2, s, 256), BF16), pltpu.VMEM((2, 128, s), BF16), pltpu.VMEM((2, 2, t, t), F32),
                        pltpu.VMEM((2, 2, t, t), BF16), pltpu.VMEM((2, 8, t), F32), pltpu.VMEM((2, 8, t), F32),
                        pltpu.VMEM((128, t), F32)],
        compiler_params=_params(("parallel", "arbitrary")),
    )(p, p, p, cum)


def _attn_bwd(p, cum, o, lse, do, s):
    t = min(ATT_BLOCK, s)
    nq = s // t
    r = ATT_STRIP

    def body(q_ref, k_ref, v_ref, c_ref, o_ref, lse_ref, do_ref, dq_ref, dk_ref, dv_ref, dc_ref, dr_ref,
             qaugt_sc, qh_sc, dot_sc, doh_sc, delta_sc, dqt_sc, dr_sc, kaug_sc, vh_sc, kt_sc, s_sc, dp_sc, p_sc, ds_sc,
             dk_sc, dv_sc, dc_sc):
        j, ki = pl.program_id(0), pl.program_id(1)

        @pl.when(ki == 0)
        def _():
            for c in range(nq):
                rows = slice(c * t, (c + 1) * t)
                q_blk, do_blk = q_ref[rows, :], do_ref[rows, :]
                qaugt_sc[:, rows] = _q_augmented_t(q_blk)
                dot_sc[:, rows] = do_blk.T.astype(BF16)
                prod_t = (do_blk * o_ref[rows, :]).T
                delta_sc[:, rows] = _rows01(jnp.sum(prod_t[0:HD], axis=0, keepdims=True),
                                            jnp.sum(prod_t[HD:], axis=0, keepdims=True))
                for h in range(2):
                    head = _head_part((t, 128), h, 1)
                    qh_sc[h, rows, :] = jnp.where(head, q_blk * ATT_SCALE, 0.0).astype(BF16)
                    doh_sc[h, rows, :] = jnp.where(head, do_blk, 0.0).astype(BF16)
            dqt_sc[...] = jnp.zeros(dqt_sc.shape, F32)
            dr_sc[...] = jnp.zeros(dr_sc.shape, F32)

        k_blk, v_blk = k_ref[...], v_ref[...]
        kt = k_blk.T
        for h in range(2):
            kaug_sc[h] = _k_augmented(k_blk, c_ref[...], j, h)
            vh_sc[h] = jnp.where(_head_part((t, 128), h, 1), v_blk, 0.0).astype(BF16)
            kt_sc[h] = jnp.where(_head_part((128, t), h, 0), kt, 0.0).astype(BF16)
        dk_sc[...] = jnp.zeros(dk_sc.shape, F32)
        dv_sc[...] = jnp.zeros(dv_sc.shape, F32)
        dc_sc[...] = jnp.zeros(dc_sc.shape, F32)

        def inputs(qb, buf):
            qs = pl.ds(pl.multiple_of(qb * t, t), t)
            for h in range(2):
                s_sc[buf, h] = _dot(kaug_sc[h], qaugt_sc[:, qs])
                dp_sc[buf, h] = _dot(vh_sc[h], dot_sc[:, qs])

        def elementwise(qb, buf, diagonal):
            qs = pl.ds(pl.multiple_of(qb * t, t), t)
            for h in range(2):
                lse_row, delta_row = lse_ref[0, h:h + 1, qs], delta_sc[h:h + 1, qs]
                tot = jnp.zeros((8, t), F32)
                for i in range(t // r):
                    rows = slice(i * r, (i + 1) * r)
                    x = s_sc[buf, h, rows, :]
                    if diagonal:
                        x = jnp.where(_iota((r, t), 1) >= i * r + _iota((r, t), 0), x, -1e30)
                    pr = jnp.exp(x - lse_row)
                    ds = pr * (dp_sc[buf, h, rows, :] - delta_row)
                    p_sc[buf, h, rows, :] = pr.astype(BF16)
                    ds_sc[buf, h, rows, :] = ds.astype(BF16)
                    dc_sc[h, rows, :] += sum(ds[:, 128 * g:128 * (g + 1)] for g in range(t // 128))
                    tot = _fold8(ds, jnp.add, tot)
                dr_sc[h, :, qs] += tot

        def outputs(qb, buf):
            qs = pl.ds(pl.multiple_of(qb * t, t), t)
            dv_sc[...] += _dot(p_sc[buf, 0], doh_sc[0, qs, :]) + _dot(p_sc[buf, 1], doh_sc[1, qs, :])
            dk_sc[...] += _dot(ds_sc[buf, 0], qh_sc[0, qs, :]) + _dot(ds_sc[buf, 1], qh_sc[1, qs, :])
            dqt_sc[:, qs] += _dot(kt_sc[0], ds_sc[buf, 0]) + _dot(kt_sc[1], ds_sc[buf, 1])

        def pair(a, b, a_diagonal):
            inputs(a, 0)
            inputs(b, 1)
            elementwise(a, 0, a_diagonal)
            outputs(a, 0)
            elementwise(b, 1, False)
            outputs(b, 1)

        def later(u, carry):
            pair(ki + 1 + 2 * u, ki + 2 + 2 * u, False)
            return carry

        n_later = nq - 1 - ki
        lax.fori_loop(0, n_later // 2, later, 0)

        @pl.when(n_later % 2 == 1)
        def _():
            pair(ki, nq - 1, True)

        @pl.when(n_later % 2 == 0)
        def _():
            inputs(ki, 0)
            elementwise(ki, 0, True)
            outputs(ki, 0)

        dk_ref[...] = dk_sc[...].astype(BF16)
        dv_ref[...] = dv_sc[...].astype(BF16)
        dc_ref[...] = jnp.where(_iota((t, 128), 1) < HD, jnp.sum(dc_sc[0], axis=1, keepdims=True),
                                jnp.sum(dc_sc[1], axis=1, keepdims=True))

        @pl.when(ki == nq - 1)
        def _():
            for c in range(nq):
                rows = slice(c * t, (c + 1) * t)
                dq_ref[rows, :] = dqt_sc[:, rows].T * ATT_SCALE
            dr_ref[0] = _rows01(jnp.sum(dr_sc[0], axis=0, keepdims=True), jnp.sum(dr_sc[1], axis=0, keepdims=True))

    whole = lambda off: pl.BlockSpec((s, 128), functools.partial(lambda j, ki, off: (0, off + j), off=off))
    return pl.pallas_call(
        body, name="attn_bwd", grid=(NH // 2, nq),
        in_specs=[whole(OFF_Q // 128),
                  pl.BlockSpec((t, 128), lambda j, ki: (ki, OFF_K // 128 + j)),
                  pl.BlockSpec((t, 128), lambda j, ki: (ki, OFF_V // 128 + j)),
                  pl.BlockSpec((t, 128), lambda j, ki: (ki, 0)),
                  whole(0),
                  pl.BlockSpec((1, 8, s), lambda j, ki: (j, 0, 0)),
                  whole(0)],
        out_specs=[whole(0),
                   pl.BlockSpec((t, 128), lambda j, ki: (ki, j)),
                   pl.BlockSpec((t, 128), lambda j, ki: (ki, j)),
                   pl.BlockSpec((t, 128), lambda j, ki: (ki, j)),
                   pl.BlockSpec((1, 8, s), lambda j, ki: (j, 0, 0))],
        out_shape=[jax.ShapeDtypeStruct((s, D), F32), jax.ShapeDtypeStruct((s, D), BF16), jax.ShapeDtypeStruct((s, D), BF16),
                   jax.ShapeDtypeStruct((s, D), F32), jax.ShapeDtypeStruct((NH // 2, 8, s), F32)],
        scratch_shapes=[pltpu.VMEM((256, s), BF16), pltpu.VMEM((2, s, 128), BF16), pltpu.VMEM((128, s), BF16),
                        pltpu.VMEM((2, s, 128), BF16), pltpu.VMEM((8, s), F32), pltpu.VMEM((128, s), F32),
                        pltpu.VMEM((2, 8, s), F32), pltpu.VMEM((2, t, 256), BF16), pltpu.VMEM((2, t, 128), BF16),
                        pltpu.VMEM((2, 128, t), BF16), pltpu.VMEM((2, 2, t, t), F32), pltpu.VMEM((2, 2, t, t), F32),
                        pltpu.VMEM((2, 2, t, t), BF16), pltpu.VMEM((2, 2, t, t), BF16), pltpu.VMEM((t, 128), F32),
                        pltpu.VMEM((t, 128), F32), pltpu.VMEM((2, t, 128), F32)],
        compiler_params=_params(("parallel", "arbitrary")),
    )(p, p, p, cum, o, lse, do)


def _ln_stats(u):
    mu = _mean(u)
    d = u - mu
    rstd = lax.rsqrt(_mean(d * d) + EPS)
    return d * rstd, rstd


def _ln_bwd(dx, xh, rstd, gam):
    dxh = dx * gam
    return rstd * (dxh - _mean(dxh) - xh * _mean(dxh * xh))


def _rms_bwd(d, xn, r, w):
    t = d * w
    return r * (t - xn * _mean(t * xn)), _colsum(d * xn)


def _mix_norm(y, p, att, w_ssm, w_att, s):
    def fn(pos, y, z, att, w1, w2):
        g = y * _silu(z)
        n1 = g * lax.rsqrt(_mean(g * g) + EPS) * w1
        n2 = att * lax.rsqrt(_mean(att * att) + EPS) * w2
        return (jnp.concatenate([n1, n2], axis=1),)

    return _rowk("mix_norm", fn, s, 256, [(y, D, 0, 0), (p, D, OFF_Z // D, 0), (att, D, 0, 0)],
                 [w_ssm, w_att], [(2 * D, BF16)], [])[0]


def _mix_norm_bwd(dmix, y, p, att, w_ssm, w_att, s):
    def fn(pos, dmix, y, z, att, w1, w2, a1, a2):
        sz = _silu(z)
        g = y * sz
        r1 = lax.rsqrt(_mean(g * g) + EPS)
        dg, dw1 = _rms_bwd(dmix[:, :D], g * r1, r1, w1)
        r2 = lax.rsqrt(_mean(att * att) + EPS)
        datt, dw2 = _rms_bwd(dmix[:, D:], att * r2, r2, w2)
        return dg * sz, dg * y * _dsilu(z), datt, a1 + dw1, a2 + dw2

    return _rowk("mix_norm_bwd", fn, s, 256, [(dmix, 2 * D, 0, 0), (y, D, 0, 0), (p, D, OFF_Z // D, 0), (att, D, 0, 0)],
                 [w_ssm, w_att], [(D, F32), (D, BF16), (D, F32)], [(1, D), (1, D)])


def _ln1(x0, y, g1, gam, bet, sc2, sh2, s):
    def fn(pos, x0, y, g1, gam, bet, sc2, sh2):
        xh, _ = _ln_stats(ALPHA * x0 + (1.0 + g1) * y)
        x1 = xh * gam + bet
        return x1, _modulate(x1, sc2, sh2)

    return _rowk("ln1", fn, s, 256, [(x0, D, 0, 0), (y, D, 0, 0)], [g1, gam, bet, sc2, sh2], [(D, F32), (D, BF16)], [])


def _ln2_loss(x1, ff, tgt, g2, gam, bet, s):
    def fn(pos, x1, ff, tgt, g2, gam, bet, a_loss, a_dgam, a_dbet, a_dg2):
        xh, rstd = _ln_stats(ALPHA * x1 + (1.0 + g2) * ff)
        err = xh * gam + bet - tgt
        dx2 = err * (1.0 / D)
        du = _ln_bwd(dx2, xh, rstd, gam)
        return (du, du * (1.0 + g2), a_loss + _colsum(err * err), a_dgam + _colsum(dx2 * xh),
                a_dbet + _colsum(dx2), a_dg2 + _colsum(du * ff))

    return _rowk("ln2_loss", fn, s, 256, [(x1, D, 0, 0), (ff, D, 0, 0), (tgt, D, 0, 0)], [g2, gam, bet],
                 [(D, F32), (D, BF16)], [(1, D)] * 4)


def _ln1_bwd(dh2, du2, x0, y, g1, gam, bet, sc2, s):
    def fn(pos, dh2, du2, x0, y, g1, gam, bet, sc2, a_sc, a_sh, a_gam, a_bet, a_g1):
        xh, rstd = _ln_stats(ALPHA * x0 + (1.0 + g1) * y)
        x1 = xh * gam + bet
        dx1 = ALPHA * du2 + dh2 * (1.0 + sc2)
        du1 = _ln_bwd(dx1, xh, rstd, gam)
        return (du1, du1 * (1.0 + g1), a_sc + _colsum(dh2 * x1), a_sh + _colsum(dh2), a_gam + _colsum(dx1 * xh),
                a_bet + _colsum(dx1), a_g1 + _colsum(du1 * y))

    return _rowk("ln1_bwd", fn, s, 256, [(dh2, D, 0, 0), (du2, D, 0, 0), (x0, D, 0, 0), (y, D, 0, 0)],
                 [g1, gam, bet, sc2], [(D, F32), (D, BF16)], [(1, D)] * 5)


def _input_grad(dh1, du1, x0, sc1, s):
    def fn(pos, dh1, du1, x0, sc1, a_sc, a_sh):
        return ALPHA * du1 + dh1 * (1.0 + sc1), a_sc + _colsum(dh1 * x0), a_sh + _colsum(dh1)

    return _rowk("input_grad", fn, s, 256, [(dh1, D, 0, 0), (du1, D, 0, 0), (x0, D, 0, 0)], [sc1],
                 [(D, F32)], [(1, D)] * 2)


def _adamw_math(w, grad, m, v):
    m_new = ADAM_B1 * m + (1.0 - ADAM_B1) * grad
    v_new = ADAM_B2 * v + (1.0 - ADAM_B2) * (grad * grad)
    m_hat = m_new / (1.0 - ADAM_B1 ** ADAM_STEP)
    v_hat = v_new / (1.0 - ADAM_B2 ** ADAM_STEP)
    return -ADAM_LR * (m_hat / (jnp.sqrt(v_hat) + ADAM_EPS) + ADAM_WD * w), m_new, v_new


def _small_update(small_all, layout, w, m, v):
    names = [n for n, _, _ in layout]

    def body(*refs):
        all_ref = refs[0]
        w_refs, m_refs, v_refs = [refs[1 + k * len(names):1 + (k + 1) * len(names)] for k in range(3)]
        sum_ref = refs[1 + 3 * len(names)]
        outs = refs[2 + 3 * len(names):]
        total = all_ref[0]
        for k in range(1, N_DEV):
            total = total + all_ref[k]
        sum_ref[...] = total
        for i, (_, off, size) in enumerate(layout):
            grad = total[:, off:off + size]
            delta, m_new, v_new = _adamw_math(w_refs[i][...], grad, m_refs[i][...], v_refs[i][...])
            for o, val in zip(outs[4 * i:4 * i + 4], (grad, delta, m_new, v_new)):
                o[...] = val

    res = pl.pallas_call(
        body, name="small_update",
        out_shape=[jax.ShapeDtypeStruct(small_all.shape[1:], F32)]
        + [jax.ShapeDtypeStruct(w[n].shape, F32) for n in names for _ in range(4)],
        compiler_params=_params(None),
    )(small_all, *[w[n] for n in names], *[m[n] for n in names], *[v[n] for n in names])
    return res[0], {n: res[1 + 4 * i:5 + 4 * i] for i, n in enumerate(names)}


def _adamw(name, w, g, m, v, *, tr, slots):
    r, c = w.shape

    def body(w_ref, g_ref, m_ref, v_ref, g_out, d_out, m_out, v_out):
        if slots:
            grad = g_ref[0][:, :c].astype(F32)
            for k in range(1, N_DEV):
                grad = grad + g_ref[k][:, :c].astype(F32)
        else:
            grad = g_ref[...]
        g_out[...] = grad
        d_out[...], m_out[...], v_out[...] = _adamw_math(w_ref[...], grad, m_ref[...], v_ref[...])

    tile = pl.BlockSpec((tr, c), lambda i: (i, 0))
    g_spec = pl.BlockSpec((N_DEV, tr, g.shape[-1]), lambda i: (0, i, 0)) if slots else tile
    return pl.pallas_call(
        body, name=name, grid=(r // tr,),
        in_specs=[tile, g_spec, tile, tile], out_specs=[tile] * 4,
        out_shape=[jax.ShapeDtypeStruct((r, c), F32)] * 4,
        compiler_params=_params(("parallel",)),
    )(w, g, m, v)


def _dot_f32(a, b, dims=NN):
    a0, a1, a2 = _split3(a)
    b0, b1, b2 = _split3(b)
    acc = _dot(a0, b0, dims)
    for x, y in ((a0, b1), (a1, b0), (a1, b1), (a0, b2), (a2, b0)):
        acc = acc + _dot(x, y, dims)
    return acc


def _ada_mod(c_all, w_shard, b_shard):
    def body(c_ref, w_ref, b_ref, o_ref):
        act = _silu(c_ref[...])
        act16 = jnp.concatenate([act, jnp.zeros_like(act)], axis=0)
        o_ref[...] = _dot_f32(act16, w_ref[...])[0:N_DEV] + b_ref[...]

    return pl.pallas_call(
        body, name="ada_mod", out_shape=jax.ShapeDtypeStruct((N_DEV, w_shard.shape[1]), F32),
        compiler_params=_params(None),
    )(c_all, w_shard, b_shard)


def _ada_grad(c_all, dmod_cols):
    def body(c_ref, dc_ref, gw_ref):
        act = _silu(c_ref[...])
        act16 = jnp.concatenate([act, jnp.zeros_like(act)], axis=0)
        dm = dc_ref[...]
        dm16 = jnp.concatenate([dm, jnp.zeros_like(dm)], axis=0)
        gw_ref[...] = _dot_f32(act16, dm16, TN)

    return pl.pallas_call(
        body, name="ada_grad", out_shape=jax.ShapeDtypeStruct((D, dmod_cols.shape[1]), F32),
        compiler_params=_params(None),
    )(c_all, dmod_cols)


def _exchange(name, xs, scatter):
    n = len(xs)
    n_peer = N_DEV - 1

    def body(*refs):
        x_refs, o_refs = refs[:n], refs[n:2 * n]
        send_sems, recv_sems, local_sems = refs[2 * n:]
        mx, my, mc = lax.axis_index("x"), lax.axis_index("y"), lax.axis_index("c")
        me = 4 * mx + 2 * my + mc

        def src(a, slot):
            return x_refs[a].at[slot] if scatter else x_refs[a]

        own = [pltpu.make_async_copy(src(a, me), o_refs[a].at[me], local_sems.at[a]) for a in range(n)]
        for cp in own:
            cp.start()
        sends = []
        for d in range(1, N_DEV):
            px = 1 - mx if d & 4 else mx
            py = 1 - my if d & 2 else my
            pc = 1 - mc if d & 1 else mc
            peer = 4 * px + 2 * py + pc
            for a in range(n):
                def copy(src_slot, dst_slot, a=a, d=d, to=(px, py, pc)):
                    return pltpu.make_async_remote_copy(
                        src_ref=src(a, src_slot), dst_ref=o_refs[a].at[dst_slot],
                        send_sem=send_sems.at[a * n_peer + d - 1], recv_sem=recv_sems.at[a * n_peer + d - 1],
                        device_id=to, device_id_type=pl.DeviceIdType.MESH)

                out = copy(peer, me)
                out.start()
                sends.append((out, copy(me, peer)))
        for _, arrival in sends:
            arrival.wait_recv()
        for out, _ in sends:
            out.wait_send()
        for cp in own:
            cp.wait()

    shapes = [tuple(x.shape[1:] if scatter else x.shape) for x in xs]
    return pl.pallas_call(
        body, name=name,
        in_specs=[pl.BlockSpec(memory_space=pl.ANY)] * n, out_specs=[pl.BlockSpec(memory_space=pl.ANY)] * n,
        out_shape=[jax.ShapeDtypeStruct((N_DEV,) + sh, x.dtype) for sh, x in zip(shapes, xs)],
        scratch_shapes=[pltpu.SemaphoreType.DMA((n * n_peer,)), pltpu.SemaphoreType.DMA((n * n_peer,)),
                        pltpu.SemaphoreType.DMA((n,))],
        compiler_params=pltpu.CompilerParams(has_side_effects=True),
    )(*xs)


def _gather_two_level(name, x):
    def body(x_ref, o_ref, send_sems, recv_sems, local_sem):
        mx, my, mc = lax.axis_index("x"), lax.axis_index("y"), lax.axis_index("c")
        me, sibling = (mx, my, mc), (mx, my, 1 - mc)
        chips = [(1 - mx, my), (mx, 1 - my), (1 - mx, 1 - my)]

        def slot(px, py, pc):
            return o_ref.at[4 * px + 2 * py + pc]

        def copy(k, block, to, src=None):
            return pltpu.make_async_remote_copy(
                src_ref=slot(*block) if src is None else src, dst_ref=slot(*block),
                send_sem=send_sems.at[k], recv_sem=recv_sems.at[k], device_id=to, device_id_type=pl.DeviceIdType.MESH)

        mine = pltpu.make_async_copy(x_ref, slot(*me), local_sem)
        mine.start()
        first = [copy(0, me, sibling, src=x_ref)] + [copy(1 + i, me, (*chip, mc), src=x_ref) for i, chip in enumerate(chips)]
        for cp in first:
            cp.start()
        passed = [copy(4 + i, (*chip, mc), sibling) for i, chip in enumerate(chips)]
        for i, chip in enumerate(chips):
            copy(1 + i, (*chip, mc), me).wait_recv()
            passed[i].start()
        copy(0, sibling, me).wait_recv()
        for i, chip in enumerate(chips):
            copy(4 + i, (*chip, 1 - mc), me).wait_recv()
        for cp in first + passed:
            cp.wait_send()
        mine.wait()

    return pl.pallas_call(
        body, name=name,
        in_specs=[pl.BlockSpec(memory_space=pl.ANY)], out_specs=pl.BlockSpec(memory_space=pl.ANY),
        out_shape=jax.ShapeDtypeStruct((N_DEV,) + tuple(x.shape), x.dtype),
        scratch_shapes=[pltpu.SemaphoreType.DMA((7,)), pltpu.SemaphoreType.DMA((7,)), pltpu.SemaphoreType.DMA(())],
        compiler_params=pltpu.CompilerParams(has_side_effects=True),
    )(x)


def _after(x, zero):
    return x if zero is None else x + zero.reshape(-1)[0].astype(x.dtype)


_HBM = pl.BlockSpec(memory_space=pltpu.HBM)
_SEM = pl.BlockSpec(memory_space=pltpu.SEMAPHORE)


def _exchange_copies(x_refs, land_refs, send_sems, recv_sems, scatter):
    n = len(x_refs)
    n_peer = N_DEV - 1
    mx, my, mc = lax.axis_index("x"), lax.axis_index("y"), lax.axis_index("c")
    me = 4 * mx + 2 * my + mc
    pairs = []
    for d in range(1, N_DEV):
        px = 1 - mx if d & 4 else mx
        py = 1 - my if d & 2 else my
        pc = 1 - mc if d & 1 else mc
        peer = 4 * px + 2 * py + pc
        for a in range(n):
            def copy(src_slot, dst_slot, a=a, d=d, to=(px, py, pc)):
                return pltpu.make_async_remote_copy(
                    src_ref=x_refs[a].at[src_slot] if scatter else x_refs[a], dst_ref=land_refs[a].at[dst_slot],
                    send_sem=send_sems.at[a * n_peer + d - 1], recv_sem=recv_sems.at[a * n_peer + d - 1],
                    device_id=to, device_id_type=pl.DeviceIdType.MESH)

            pairs.append((copy(peer, me), copy(me, peer)))
    return me, pairs


def _exchange_async(name, xs, scatter, collective_id):
    n = len(xs)
    shapes = [tuple(x.shape[1:] if scatter else x.shape) for x in xs]
    x_refs = [jax.new_ref(x, memory_space=pltpu.MemorySpace.HBM) for x in xs]
    land_refs = [jax.empty_ref(jax.ShapeDtypeStruct((N_DEV,) + sh, x.dtype), memory_space=pltpu.MemorySpace.HBM)
                 for sh, x in zip(shapes, xs)]

    @pl.kernel(mesh=plsc.ScalarSubcoreMesh(axis_name="sequencer", num_cores=1), name=name,
               scratch_types=(pltpu.SemaphoreType.DMA((n * (N_DEV - 1),)), pltpu.SemaphoreType.DMA((n * (N_DEV - 1),)),
                              pltpu.SemaphoreType.DMA((n,))),
               compiler_params=pltpu.CompilerParams(collective_id=collective_id))
    def launch(send_sems, recv_sems, own_sems):
        barrier = pltpu.get_barrier_semaphore()
        mx, my, mc = lax.axis_index("x"), lax.axis_index("y"), lax.axis_index("c")
        for d in range(1, N_DEV):
            peer = (1 - mx if d & 4 else mx, 1 - my if d & 2 else my, 1 - mc if d & 1 else mc)
            pl.semaphore_signal(barrier, inc=1, device_id=peer, device_id_type=pl.DeviceIdType.MESH)
        pl.semaphore_wait(barrier, N_DEV - 1)
        me, pairs = _exchange_copies(x_refs, land_refs, send_sems, recv_sems, scatter)
        own = [pltpu.make_async_copy(x_refs[a].at[me] if scatter else x_refs[a], land_refs[a].at[me], own_sems.at[a])
               for a in range(n)]
        for cp in own:
            cp.start()
        for out, _ in pairs:
            out.start()
        for out, arrival in pairs:
            arrival.wait_recv()
            out.wait_send()
        for cp in own:
            cp.wait()

    launch()
    return lambda: [r[...] for r in land_refs]


def _exchange_start(name, xs, scatter):
    n = len(xs)
    shapes = [tuple(x.shape[1:] if scatter else x.shape) for x in xs]

    def body(*refs):
        x_refs, land_refs = refs[:n], refs[n:2 * n]
        send_sems, recv_sems = refs[2 * n], refs[2 * n + 1]
        token, own_sems = refs[4 * n + 2], refs[4 * n + 3]
        me, pairs = _exchange_copies(x_refs, land_refs, send_sems, recv_sems, scatter)
        own = [pltpu.make_async_copy(x_refs[a].at[me] if scatter else x_refs[a], land_refs[a].at[me], own_sems.at[a])
               for a in range(n)]
        for cp in own:
            cp.start()
        for out, _ in pairs:
            out.start()
        for cp in own:
            cp.wait()
        token[...] = jnp.zeros(token.shape, token.dtype)

    lands = [pltpu.with_memory_space_constraint(lax.empty((N_DEV,) + sh, x.dtype), pltpu.HBM) for sh, x in zip(shapes, xs)]
    res = pl.pallas_call(
        body, name=name,
        out_shape=(pltpu.SemaphoreType.DMA((n * (N_DEV - 1),)), pltpu.SemaphoreType.DMA((n * (N_DEV - 1),)),
                   *[pltpu.HBM(x.shape, x.dtype) for x in xs], *[pltpu.HBM(l.shape, l.dtype) for l in lands],
                   jax.ShapeDtypeStruct((8, 128), F32)),
        in_specs=[_HBM] * (2 * n),
        out_specs=(_SEM, _SEM, *[_HBM] * (2 * n), pl.BlockSpec(memory_space=pltpu.VMEM)),
        input_output_aliases={i: 2 + i for i in range(2 * n)},
        scratch_shapes=[pltpu.SemaphoreType.DMA((n,))],
        compiler_params=pltpu.CompilerParams(has_side_effects=pltpu.SideEffectType.DATAFLOW_SIDE_EFFECTING),
    )(*[pltpu.with_memory_space_constraint(x, pltpu.HBM) for x in xs], *lands)
    return dict(send=res[0], recv=res[1], xs=list(res[2:2 + n]), lands=list(res[2 + n:2 + 2 * n]), token=res[2 + 2 * n])


def _exchange_wait(name, handle, after, scatter):
    n = len(handle['xs'])

    def body(*refs):
        x_refs, land_refs = refs[:n], refs[n:2 * n]
        send_sems, recv_sems = refs[2 * n], refs[2 * n + 1]
        _, pairs = _exchange_copies(x_refs, land_refs, send_sems, recv_sems, scatter)
        for out, arrival in pairs:
            out.wait_send()
            arrival.wait_recv()

    res = pl.pallas_call(
        body, name=name,
        out_shape=tuple(pltpu.HBM(a.shape, a.dtype) for a in handle['xs'] + handle['lands']),
        in_specs=[_HBM] * (2 * n) + [_SEM, _SEM, pl.BlockSpec(memory_space=pl.ANY)],
        out_specs=tuple([_HBM] * (2 * n)),
        input_output_aliases={i: i for i in range(2 * n)},
        compiler_params=pltpu.CompilerParams(has_side_effects=pltpu.SideEffectType.DATAFLOW_SIDE_EFFECTING),
    )(*handle['xs'], *handle['lands'], handle['send'], handle['recv'], after)
    return list(res[n:])


def _relu2(a):
    r = jnp.maximum(a, 0.0)
    return r * r


def _relu2_grad(acc, a):
    return acc * (2.0 * jnp.maximum(a, 0.0))


def _local_step(x0, tgt, mod, wcat, late_weights, send_grads, conv_w, conv_b, dt_bias, a_log, d_skip, ssm_norm_w, f_bias,
                attn_norm_w, ln1_g, ln1_b, ln2_g, ln2_b):
    ff_w = DFF // N_DEV
    s = x0.shape[0]
    tm = min(1024, s)
    ts = min(1024, s)
    sh1, sc1, g1, sh2, sc2, g2 = [mod[:, i * D:(i + 1) * D] for i in range(6)]
    zero = jnp.zeros((1, 128 - 2 * NH), F32)
    bias128 = jnp.concatenate([dt_bias, f_bias, zero], axis=1)
    alog128 = jnp.concatenate([a_log, jnp.zeros((1, 128 - NH), F32)], axis=1)
    dskip_x = jnp.repeat(d_skip, HD, axis=1)
    w_xs, w_bc, b_xs, b_bc = conv_w[:, :D], conv_w[:, D:], conv_b[:, :D], conv_b[:, D:]

    p = _mm_nn("in_proj", x0, wcat, tm=tm, tn=1152, tk=D, out_dtype=F32, pro=_modulate, aux=(sc1, sh1))
    xs_a, bc_a = _conv_fwd(p, w_xs, b_xs, w_bc, b_bc, s)
    y_ssd, states = _ssd_fwd(xs_a, bc_a, p, bias128, alog128, dskip_x, s)
    cum = _cum_fwd(p, bias128, s)
    att, lse = _attn_fwd(p, cum, s)
    wout, w1s, w2 = late_weights(lse)
    ymix = _mix_norm(y_ssd, p, att, ssm_norm_w, attn_norm_w, s)
    y = _mm_nn("out_proj", ymix, wout, tm=tm, tn=1024, tk=2 * D, out_dtype=F32)
    x1, h2 = _ln1(x0, y, g1, ln1_g, ln1_b, sc2, sh2, s)
    a1 = _mm_nn("ff_in", h2, w1s, tm=tm, tn=ff_w, tk=D, out_dtype=F32)
    ff = _mm_nn("ff_out", a1, w2, tm=tm, tn=1024, tk=1024, out_dtype=F32, pro=_relu2)
    du2, dff, sq_err, d_ln2_g, d_ln2_b, d_g2 = _ln2_loss(x1, ff, tgt, g2, ln2_g, ln2_b, s)

    da1 = _mm_nt("d_ff_hidden", [(dff, D, 0)], [(w2, D, 0)], n=DFF, tm=tm, tn=1024, out_dtype=BF16, epi=_relu2_grad,
                 epi_aux=(a1,))
    d_w2 = _mm_tn("d_w_ff_out", a1, dff, tm=1024, tn=1024, ts=ts, pro=_relu2)
    d_w1s = _mm_tn("d_w_ff_in", h2, da1, tm=1024, tn=ff_w, ts=ts, col_shards=True)
    dh2 = _mm_nt("d_ff_input", [(da1, ff_w, k) for k in range(N_DEV)], [(w1s, ff_w, k) for k in range(N_DEV)], n=D,
                 tm=min(512, s), tn=1024, out_dtype=F32)
    du1, dy, d_sc2, d_sh2, d_ln1_g, d_ln1_b, d_g1 = _ln1_bwd(dh2, du2, x0, y, g1, ln1_g, ln1_b, sc2, s)

    dmix = _mm_nt("d_mix", [(dy, D, 0)], [(wout, D, 0)], n=2 * D, tm=tm, tn=1024, out_dtype=F32)
    d_wout = _mm_tn("d_w_out", ymix, dy, tm=1024, tn=1024, ts=ts)
    sent = send_grads("late", [d_w1s, d_w2.reshape(N_DEV, -1, D), d_wout.reshape(N_DEV, -1, D)])
    dy_ssd, dz, datt, d_ssm_w, d_attn_w = _mix_norm_bwd(dmix, y_ssd, p, att, _after(ssm_norm_w, sent), attn_norm_w, s)
    dq, dk, dv, dcs, drs = _attn_bwd(p, cum, att, lse, datt, s)
    dxs_a, dbc_a, ddt_raw, d_alog, d_dskip = _ssd_bwd(dy_ssd, xs_a, bc_a, p, states, bias128, alog128, dskip_x, s)
    dr_col = jnp.pad(drs[:, :2, :].reshape(NH, s).T, ((0, 0), (NH, 128 - 2 * NH)))
    ddtf, _, d_bias = _cum_bwd(dr_col, dcs, ddt_raw, p, bias128, s)
    dxs, dbc, d_wc_xs, d_bc_xs, d_wc_bc, d_bc_bc = _conv_bwd(dxs_a, dbc_a, p, w_xs, b_xs, w_bc, b_bc, s)

    segs = [(dz, OFF_Z, D), (dxs, OFF_XS, D), (dq, OFF_Q, D), (dk, OFF_K, D), (dv, OFF_V, D), (dbc, OFF_BC, 512),
            (ddtf, OFF_DTF, 128)]
    d_z, d_xs, d_q, d_k, d_v, d_bcw, d_dtf = [
        _mm_tn("d_w_in_%d" % i, x0, a, tm=1024, tn=min(w, 1024), ts=ts, pro=_modulate, aux=(sc1, sh1))
        for i, (a, _, w) in enumerate(segs)]
    d_w_in = dict(z=d_z, xs=d_xs, bc=d_bcw, dt=d_dtf[:, :NH], q=d_q, k=d_k, v=d_v, f=d_dtf[:, NH:2 * NH])
    sent = send_grads("in", [_shard_w_in_grad(d_w_in)])
    segs[-1] = (_after(ddtf, sent), OFF_DTF, 128)
    dh1 = _mm_nt("d_h1", [(a, w, 0) for a, _, w in segs], [(wcat, w, off // w) for _, off, w in segs], n=D,
                 tm=min(512, s), tn=1024, out_dtype=F32)
    grad_x, d_sc1, d_sh1 = _input_grad(dh1, du1, x0, sc1, s)

    return dict(
        loss=(0.5 / D) * jnp.sum(sq_err), grad_x=grad_x,
        d_mod=jnp.concatenate([d_sh1, d_sc1, d_g1, d_sh2, d_sc2, d_g2], axis=1),
        d_conv_w=jnp.concatenate([d_wc_xs[:4], d_wc_bc[:4]], axis=1), d_conv_b=jnp.concatenate([d_bc_xs, d_bc_bc], axis=1),
        d_ssm_norm_w=d_ssm_w, d_attn_norm_w=d_attn_w, d_ln1_g=d_ln1_g, d_ln1_b=d_ln1_b, d_ln2_g=d_ln2_g, d_ln2_b=d_ln2_b,
        d_gate_bias=d_bias, d_a_log=d_alog, d_d_skip=d_dskip)


W_IN_SEGS = [('z', W_Z, D), ('xs', W_XS, D), ('bc', W_BC, 512), ('dt', W_DT, NH), ('q', W_Q, D), ('k', W_K, D),
             ('v', W_V, D), ('f', W_F, NH)]
SHARD_W = IN_COLS // N_DEV


def _pack_w_in(shards):
    def cols(lo, hi):
        pieces = []
        while lo < hi:
            dev = lo // SHARD_W
            end = min(hi, (dev + 1) * SHARD_W)
            pieces.append(shards[dev][:, lo - dev * SHARD_W:end - dev * SHARD_W])
            lo = end
        return pieces

    seg = {n: cols(off, off + w) for n, off, w in W_IN_SEGS}
    pieces = seg['z'] + seg['xs'] + seg['q'] + seg['k'] + seg['v'] + seg['bc'] + seg['dt'] + seg['f']
    return jnp.concatenate(pieces + [jnp.zeros((D, 128 - 2 * NH), shards.dtype)], axis=1)


def _shard_w_in_grad(d_w_in):
    blocks = []
    for dev in range(N_DEV):
        lo, hi = dev * SHARD_W, (dev + 1) * SHARD_W
        pieces = [d_w_in[n][:, max(lo, off) - off:min(hi, off + w) - off] for n, off, w in W_IN_SEGS
                  if max(lo, off) < min(hi, off + w)]
        pieces.append(jnp.zeros((D, -SHARD_W % 128), pieces[0].dtype))
        blocks.append(jnp.concatenate(pieces, axis=1))
    return jnp.stack(blocks, axis=0)


WEIGHTS = ['w_ada', 'b_ada', 'w_in', 'conv_w', 'conv_b', 'dt_bias', 'a_log', 'd_skip', 'ssm_norm_w', 'f_bias',
           'attn_norm_w', 'w_out', 'ln1_g', 'ln1_b', 'w_ff_in', 'w_ff_out', 'ln2_g', 'ln2_b']
BIG = ['w_in', 'w_out', 'w_ff_in', 'w_ff_out']
SMALL_LAYOUT = [('b_ada', 0, 6 * D), ('conv_b', 12288, 1536), ('ssm_norm_w', 13824, D), ('attn_norm_w', 14848, D),
                ('ln1_g', 15872, D), ('ln1_b', 16896, D), ('ln2_g', 17920, D), ('ln2_b', 18944, D),
                ('dt_bias', 19968, NH), ('f_bias', 19968 + NH, NH), ('a_log', 20096, NH), ('d_skip', 20224, NH)]
SMALL_LOSS_LANE = 20352


def _pad_lanes(v, n=128):
    return jnp.pad(v, ((0, 0), (0, n - v.shape[1])))


def kernel(x, c, w_ada, b_ada, w_in, conv_w, conv_b, dt_bias, a_log, d_skip, ssm_norm_w, f_bias, attn_norm_w, w_out, ln1_g, ln1_b, w_ff_in, w_ff_out, ln2_g, ln2_b, loss_target, m_w_ada, m_b_ada, m_w_in, m_conv_w, m_conv_b, m_dt_bias, m_a_log, m_d_skip, m_ssm_norm_w, m_f_bias, m_attn_norm_w, m_w_out, m_ln1_g, m_ln1_b, m_w_ff_in, m_w_ff_out, m_ln2_g, m_ln2_b, v_w_ada, v_b_ada, v_w_in, v_conv_w, v_conv_b, v_dt_bias, v_a_log, v_d_skip, v_ssm_norm_w, v_f_bias, v_attn_norm_w, v_w_out, v_ln1_g, v_ln1_b, v_w_ff_in, v_w_ff_out, v_ln2_g, v_ln2_b):
    args = dict(locals())
    w = {n: args[n] for n in WEIGHTS}
    m = {n: args['m_' + n] for n in WEIGHTS}
    v = {n: args['v_' + n] for n in WEIGHTS}
    me = 4 * lax.axis_index("x") + 2 * lax.axis_index("y") + lax.axis_index("c")
    ada_cols = 6 * D // N_DEV
    conv_cols = conv_w.shape[2]

    c_all, conv_all = _exchange("gather_cond", [c, conv_w[0]], False)
    c_all = c_all.reshape(N_DEV, D)
    conv_w_full = conv_all.transpose(1, 0, 2).reshape(4, N_DEV * conv_cols)
    b_shard = lax.dynamic_slice(b_ada, (0, me * ada_cols), (1, ada_cols))
    mod_all, = _exchange("gather_mod", [_ada_mod(c_all, w_ada[0], b_shard)], False)
    mod = lax.dynamic_index_in_dim(mod_all, me, axis=1, keepdims=False).reshape(1, 6 * D)

    win_s = _gather_two_level("gather_w_in", _after(w_in[0].astype(BF16), mod * 0))
    first_done = win_s[0, 0:1, 0:1] * 0
    rest = _exchange_async("gather_rest", [_after(w[n][0].astype(BF16), first_done) for n in BIG[1:]], False, 1)

    def late_weights(after):
        wout_s, w1s, w2_s = rest()
        return wout_s.reshape(2 * D, D), w1s, w2_s.reshape(DFF, D)

    sends = {}

    def send_grads(tag, blocks):
        sends[tag] = _exchange_async("scatter_" + tag, blocks, True, {'late': 2, 'in': 3}[tag])
        return sum(b.reshape(-1)[0].astype(F32) * 0 for b in blocks)

    out = _local_step(x[0], loss_target[0], mod, _pack_w_in(win_s), late_weights, send_grads,
                      conv_w_full, conv_b, dt_bias, a_log, d_skip, ssm_norm_w, f_bias, attn_norm_w, ln1_g, ln1_b, ln2_g, ln2_b)

    small = jnp.concatenate(
        [out['d_mod'], out['d_conv_w'].reshape(1, -1), out['d_conv_b'], out['d_ssm_norm_w'], out['d_attn_norm_w'],
         out['d_ln1_g'], out['d_ln1_b'], out['d_ln2_g'], out['d_ln2_b'], out['d_gate_bias'], out['d_a_log'],
         out['d_d_skip'], _pad_lanes(out['loss'].reshape(1, 1))], axis=1)
    small_landed = _exchange_async("gather_small", [small], False, 4)
    (g_ff_in, g_ff_out, g_out), (g_in,) = sends['late'](), sends['in']()
    g_parts = dict(w_ff_in=g_ff_in, w_ff_out=g_ff_out, w_out=g_out, w_in=g_in)
    big = {n: _adamw("adamw_" + n, w[n][0], g_parts[n], m[n][0], v[n][0], tr=256, slots=True) for n in BIG}
    big_done = sum(big[n][1][0:1, 0:1] * 0 for n in BIG)
    small_all = _after(small_landed()[0], big_done)
    ssum, small_res = _small_update(small_all, SMALL_LAYOUT, w, m, v)
    dmod_all = small_all[:, 0, :6 * D]
    g_w_ada = _ada_grad(c_all, lax.dynamic_slice(dmod_all, (0, me * ada_cols), (N_DEV, ada_cols)))
    ada = _adamw("adamw_ada", w_ada[0], g_w_ada, m_w_ada[0], v_w_ada[0], tr=256, slots=False)
    g_conv_w = lax.dynamic_slice(ssum[:, 6 * D:6 * D + 4 * N_DEV * conv_cols].reshape(4, N_DEV * conv_cols),
                                 (0, me * conv_cols), (4, conv_cols))
    conv = _adamw("adamw_conv_w", conv_w[0], g_conv_w, m_conv_w[0], v_conv_w[0], tr=4, slots=False)

    results = []
    for k in range(4):
        vals = {n: small_res[n][k] for n in small_res}
        vals['w_ada'], vals['conv_w'] = ada[k][None], conv[k][None]
        for n in BIG:
            vals[n] = big[n][k][None]
        results.append(vals)
    return (ssum[0, SMALL_LOSS_LANE], out['grad_x'][None], *[res[n] for res in results for n in WEIGHTS])
```

```python
import functools

import jax
import jax.numpy as jnp
from jax import lax
from jax.experimental import pallas as pl
from jax.experimental.pallas import tpu as pltpu
from jax.experimental.pallas import tpu_sc as plsc

F32, BF16 = jnp.float32, jnp.bfloat16

N_DEV = 8
D = 1024
NH, HD = 16, 64
NSTATE = 128
CHUNK = 128
HG = 8
DFF = 4096
ALPHA = 2.0 ** 0.25
EPS = 1e-5
ATT_SCALE = HD ** -0.5

OFF_Z, OFF_XS, OFF_Q, OFF_K, OFF_V, OFF_BC, OFF_DTF = 0, 1024, 2048, 3072, 4096, 5120, 5632
PCOLS = 5760
W_Z, W_XS, W_BC, W_DT, W_Q, W_K, W_V, W_F = 0, 1024, 2048, 2560, 2576, 3600, 4624, 5648
IN_COLS = 5664

ADAM_LR, ADAM_B1, ADAM_B2, ADAM_EPS, ADAM_WD, ADAM_STEP = 0.001, 0.9, 0.999, 1e-08, 0.01, 10

VMEM_LIMIT = 56 << 20

NN = (((1,), (0,)), ((), ()))
NT = (((1,), (1,)), ((), ()))
TN = (((0,), (0,)), ((), ()))


def _dot(a, b, dims=NN):
    return lax.dot_general(a, b, dims, preferred_element_type=F32)


def _bdot(a, b, dims=NN):
    return _dot(a.astype(BF16), b.astype(BF16), dims)


def _split3(v, terms=3):
    parts, rest = [], v
    for _ in range(terms):
        p = rest.astype(BF16)
        parts.append(p)
        rest = rest - p.astype(F32)
    return parts


def _sel_left(m01, v):
    return sum(_dot(m01, p) for p in _split3(v))


def _sel_right(v, m01, dims=NN, terms=3):
    return sum(_dot(p, m01, dims) for p in _split3(v, terms))


def _iota(shape, dim):
    return lax.broadcasted_iota(jnp.int32, shape, dim)


def _tri_lower(n):
    return (_iota((n, n), 1) <= _iota((n, n), 0)).astype(BF16)


def _tri_upper(n):
    return (_iota((n, n), 1) >= _iota((n, n), 0)).astype(BF16)


def _head_expand():
    return (lax.shift_right_logical(_iota((128, D), 1), 6) == _iota((128, D), 0)).astype(BF16)


def _head_reduce():
    return (lax.shift_right_logical(_iota((D, 128), 0), 6) == _iota((D, 128), 1)).astype(BF16)


def _sigmoid(x):
    return 1.0 / (1.0 + jnp.exp(-x))


def _silu(x):
    return x * _sigmoid(x)


def _dsilu(x):
    s = _sigmoid(x)
    return s * (1.0 + x * (1.0 - s))


def _softplus(x):
    return jnp.maximum(x, 0.0) + jnp.log(1.0 + jnp.exp(-jnp.abs(x)))


def _log_sigmoid(x):
    return jnp.minimum(x, 0.0) - jnp.log(1.0 + jnp.exp(-jnp.abs(x)))


def _params(sem):
    return pltpu.CompilerParams(dimension_semantics=sem, vmem_limit_bytes=VMEM_LIMIT)


def _mm_nn(name, a, b, *, tm, tn, tk, out_dtype, pro=None, aux=()):
    m, k_all = a.shape
    b_sharded = b.ndim == 3
    n = b.shape[0] * b.shape[2] if b_sharded else b.shape[1]
    assert not b_sharded or tn == b.shape[2]
    nk = k_all // tk
    n_aux = len(aux)
    b_spec = (pl.BlockSpec((None, tk, tn), lambda i, j, k: (j, k, 0)) if b_sharded
              else pl.BlockSpec((tk, tn), lambda i, j, k: (k, j)))

    def body(a_ref, b_ref, *rest):
        aux_refs, o_ref = rest[:n_aux], rest[n_aux]
        at = a_ref[...]
        if pro is not None:
            at = pro(at, *[r[...] for r in aux_refs])
        part = _bdot(at, b_ref[...])
        if nk == 1:
            o_ref[...] = part.astype(out_dtype)
            return
        acc_ref = rest[n_aux + 1]
        kk = pl.program_id(2)

        @pl.when(kk == 0)
        def _():
            acc_ref[...] = part

        @pl.when(kk > 0)
        def _():
            acc_ref[...] += part

        @pl.when(kk == nk - 1)
        def _():
            o_ref[...] = acc_ref[...].astype(out_dtype)

    return pl.pallas_call(
        body, name=name,
        grid=(m // tm, n // tn, nk),
        in_specs=[pl.BlockSpec((tm, tk), lambda i, j, k: (i, k)), b_spec]
        + [pl.BlockSpec((1, tk), lambda i, j, k: (0, k)) for _ in aux],
        out_specs=pl.BlockSpec((tm, tn), lambda i, j, k: (i, j)),
        out_shape=jax.ShapeDtypeStruct((m, n), out_dtype),
        scratch_shapes=[] if nk == 1 else [pltpu.VMEM((tm, tn), F32)],
        compiler_params=_params(("parallel", "parallel", "arbitrary")),
    )(a, b, *aux)


def _mm_nt(name, a_list, b_list, *, n, tm, tn, out_dtype, epi=None, epi_aux=()):
    m = a_list[0][0].shape[0]
    n_op = len(a_list)
    n_epi = len(epi_aux)

    def body(*refs):
        a_refs, b_refs = refs[:n_op], refs[n_op:2 * n_op]
        e_refs, o_ref = refs[2 * n_op:2 * n_op + n_epi], refs[2 * n_op + n_epi]
        acc = None
        for a_ref, b_ref in zip(a_refs, b_refs):
            part = _bdot(a_ref[...], b_ref[...], NT)
            acc = part if acc is None else acc + part
        if epi is not None:
            acc = epi(acc, *[r[...] for r in e_refs])
        o_ref[...] = acc.astype(out_dtype)

    in_specs = [pl.BlockSpec((tm, w), functools.partial(lambda i, j, cb: (i, cb), cb=cb)) for (_, w, cb) in a_list]
    for (b, w, cb) in b_list:
        if b.ndim == 3:
            in_specs.append(pl.BlockSpec((None, tn, w), functools.partial(lambda i, j, cb: (cb, j, 0), cb=cb)))
        else:
            in_specs.append(pl.BlockSpec((tn, w), functools.partial(lambda i, j, cb: (j, cb), cb=cb)))
    in_specs += [pl.BlockSpec((tm, tn), lambda i, j: (i, j)) for _ in epi_aux]
    return pl.pallas_call(
        body, name=name,
        grid=(m // tm, n // tn),
        in_specs=in_specs,
        out_specs=pl.BlockSpec((tm, tn), lambda i, j: (i, j)),
        out_shape=jax.ShapeDtypeStruct((m, n), out_dtype),
        compiler_params=_params(("parallel", "parallel")),
    )(*[a for (a, _, _) in a_list], *[b for (b, _, _) in b_list], *epi_aux)


def _mm_tn(name, a, b, *, tm, tn, ts, pro=None, aux=(), col_shards=False):
    s_all, ka = a.shape
    nb = b.shape[1]
    n_aux = len(aux)
    ns = s_all // ts
    assert not col_shards or tn == nb // N_DEV

    def body(a_ref, b_ref, *rest):
        aux_refs, o_ref, acc_ref = rest[:n_aux], rest[n_aux], rest[n_aux + 1]
        at = a_ref[...]
        if pro is not None:
            at = pro(at, *[r[...] for r in aux_refs])
        part = _bdot(at, b_ref[...], TN)
        ss = pl.program_id(2)

        @pl.when(ss == 0)
        def _():
            acc_ref[...] = part

        @pl.when(ss > 0)
        def _():
            acc_ref[...] += part

        @pl.when(ss == ns - 1)
        def _():
            o_ref[...] = acc_ref[...].astype(BF16)

    if col_shards:
        out_spec = pl.BlockSpec((None, tm, tn), lambda i, j, s: (j, i, 0))
        out_shape = jax.ShapeDtypeStruct((N_DEV, ka, tn), BF16)
    else:
        out_spec = pl.BlockSpec((tm, tn), lambda i, j, s: (i, j))
        out_shape = jax.ShapeDtypeStruct((ka, nb), BF16)
    return pl.pallas_call(
        body, name=name,
        grid=(ka // tm, nb // tn, ns),
        in_specs=[pl.BlockSpec((ts, tm), lambda i, j, s: (s, i)),
                  pl.BlockSpec((ts, tn), lambda i, j, s: (s, j))]
        + [pl.BlockSpec((1, tm), lambda i, j, s: (0, i)) for _ in aux],
        out_specs=out_spec, out_shape=out_shape,
        scratch_shapes=[pltpu.VMEM((tm, tn), F32)],
        compiler_params=_params(("parallel", "parallel", "arbitrary")),
    )(a, b, *aux)


def _rowk(name, fn, n_rows, tr, rows, fulls, outs, accs, reverse=False):
    n = n_rows // tr
    n_row, n_full, n_out, n_acc = len(rows), len(fulls), len(outs), len(accs)

    def pos(i):
        return (n - 1 - i) if reverse else i

    def body(*refs):
        row_refs = refs[:n_row]
        full_refs = refs[n_row:n_row + n_full]
        out_refs = refs[n_row + n_full:n_row + n_full + n_out]
        acc_refs = refs[n_row + n_full + n_out:]
        i = pl.program_id(0)

        @pl.when(i == 0)
        def _():
            for r in acc_refs:
                r[...] = jnp.zeros(r.shape, r.dtype)

        res = fn(pos(i), *[r[...] for r in row_refs], *[r[...] for r in full_refs], *[r[...] for r in acc_refs])
        for r, v in zip(out_refs + acc_refs, res):
            r[...] = v.astype(r.dtype)

    def row_map(i, cb, shift):
        return (jnp.clip(pos(i) + shift, 0, n - 1), cb)

    def halo_map(i, cb, shift):
        tile = jnp.clip(pos(i) + shift, 0, n - 1)
        return (tile * (tr // 8) + (tr // 8 - 1 if shift < 0 else 0), cb)

    in_specs = [pl.BlockSpec((tr, w), functools.partial(row_map, cb=cb, shift=sh)) if sh == 0 else
                pl.BlockSpec((8, w), functools.partial(halo_map, cb=cb, shift=sh)) for (_, w, cb, sh) in rows]
    in_specs += [pl.BlockSpec(f.shape, functools.partial(lambda i, nd: (0,) * nd, nd=f.ndim)) for f in fulls]
    out_specs = [pl.BlockSpec((tr, w), lambda i: (pos(i), 0)) for (w, _) in outs]
    out_specs += [pl.BlockSpec((r, w), lambda i: (0, 0)) for (r, w) in accs]
    out_shape = [jax.ShapeDtypeStruct((n_rows, w), dt) for (w, dt) in outs]
    out_shape += [jax.ShapeDtypeStruct((r, w), F32) for (r, w) in accs]
    return pl.pallas_call(
        body, name=name, grid=(n,), in_specs=in_specs, out_specs=out_specs, out_shape=out_shape,
        compiler_params=_params(("arbitrary",)),
    )(*[a for (a, _, _, _) in rows], *fulls)


def _colsum(x):
    return jnp.sum(x, axis=0, keepdims=True)


def _mean(x):
    return jnp.mean(x, axis=-1, keepdims=True)


def _modulate(x, sc, sh):
    return x * (1.0 + sc) + sh


def _shift_down(cur, prev8, j):
    tr = cur.shape[0]
    row8 = _iota(prev8.shape, 0)
    head = jnp.where(row8 < j, pltpu.roll(prev8, j, 0), pltpu.roll(cur[0:8], j, 0))
    return head if tr == 8 else jnp.concatenate([head, pltpu.roll(cur, j, 0)[8:]], axis=0)


def _shift_up(cur, next8, j):
    tr = cur.shape[0]
    row8 = _iota(next8.shape, 0)
    tail = jnp.where(row8 < 8 - j, pltpu.roll(cur[tr - 8:], 8 - j, 0), pltpu.roll(next8, 8 - j, 0))
    return jnp.concatenate([pltpu.roll(cur, tr - j, 0)[:tr - 8], tail], axis=0)


def _conv(cur, prev, w, b):
    out = cur * w[3:4] + b
    for j in (1, 2, 3):
        out = out + _shift_down(cur, prev, j) * w[3 - j:4 - j]
    return out


def _conv_fwd(p, w_xs, b_xs, w_bc, b_bc, s):
    def fn(pos, xs, xs_prev, bc, bc_prev, w_xs, b_xs, w_bc, b_bc):
        first = pos == 0
        xs_prev = jnp.where(first, 0.0, xs_prev)
        bc_prev = jnp.where(first, 0.0, bc_prev)
        return _silu(_conv(xs, xs_prev, w_xs, b_xs)), _silu(_conv(bc, bc_prev, w_bc, b_bc))

    return _rowk("conv_fwd", fn, s, 256,
                 [(p, D, OFF_XS // D, 0), (p, D, OFF_XS // D, -1), (p, 512, OFF_BC // 512, 0), (p, 512, OFF_BC // 512, -1)],
                 [w_xs, b_xs, w_bc, b_bc], [(D, F32), (512, F32)], [])


def _conv_bwd(dxs_a, dbc_a, p, w_xs, b_xs, w_bc, b_bc, s):
    tr = 256
    n = s // tr

    def fn(pos, da1, da1n, x1, x1p, x1n, da2, da2n, x2, x2p, x2n, w1, b1, w2, b2, aw1, ab1, aw2, ab2):
        dx1, dw1, db1 = _conv_bwd_fn(pos, n, da1, da1n, x1, x1p, x1n, w1, b1)
        dx2, dw2, db2 = _conv_bwd_fn(pos, n, da2, da2n, x2, x2p, x2n, w2, b2)
        return dx1, dx2, aw1 + dw1, ab1 + db1, aw2 + dw2, ab2 + db2

    cx, cb = OFF_XS // D, OFF_BC // 512
    return _rowk("conv_bwd", fn, s, tr,
                 [(dxs_a, D, 0, 0), (dxs_a, D, 0, 1), (p, D, cx, 0), (p, D, cx, -1), (p, D, cx, 1),
                  (dbc_a, 512, 0, 0), (dbc_a, 512, 0, 1), (p, 512, cb, 0), (p, 512, cb, -1), (p, 512, cb, 1)],
                 [w_xs, b_xs, w_bc, b_bc], [(D, BF16), (512, BF16)], [(8, D), (1, D), (8, 512), (1, 512)])


def _conv_bwd_fn(pos, n, da, da_next, x, x_prev, x_next, w, b):
    first, last = pos == 0, pos == n - 1
    x_prev = jnp.where(first, 0.0, x_prev)
    shifted = {j: _shift_down(x, x_prev, j) for j in (1, 2, 3)}
    conv = x * w[3:4] + b
    for j in (1, 2, 3):
        conv = conv + shifted[j] * w[3 - j:4 - j]
    dc = da * _dsilu(conv)
    dc_next = jnp.where(last, 0.0, da_next * _dsilu(_conv(x_next, x[x.shape[0] - 8:], w, b)))
    dx = dc * w[3:4]
    dws = [None] * 4
    dws[3] = _colsum(dc * x)
    for j in (1, 2, 3):
        dx = dx + _shift_up(dc, dc_next, j) * w[3 - j:4 - j]
        dws[3 - j] = _colsum(dc * shifted[j])
    row = _iota((8, x.shape[1]), 0)
    dw = jnp.zeros((8, x.shape[1]), F32)
    for k in range(4):
        dw = jnp.where(row == k, dws[k], dw)
    return dx, dw, _colsum(dc)


def _ssd_gates(dtf, bias, a_log):
    lane = _iota(dtf.shape, 1)
    head = lane < NH
    dt = jnp.where(head, _softplus(dtf + bias), 0.0)
    a_neg = jnp.where(_iota(a_log.shape, 1) < NH, -jnp.exp(a_log), 0.0)
    a = dt * a_neg
    cs = _sel_left(_tri_lower(CHUNK), a)
    return dt, a_neg, cs


def _decay_mask(cs_ref, cst_ref, h):
    diff = cs_ref[:, h:h + 1] - cst_ref[h:h + 1, :]
    low = _iota((CHUNK, CHUNK), 1) <= _iota((CHUNK, CHUNK), 0)
    return jnp.where(low, jnp.exp(jnp.minimum(diff, 0.0)), 0.0)


def _ssd_fwd(xs_a, bc_a, p, bias128, alog128, dskip_x, s):
    nc = s // CHUNK
    t = CHUNK

    def body(xs_ref, bc_ref, dtf_ref, bias_ref, alog_ref, dsk_ref, y_ref, st_ref,
             state, x_sc, xw_sc, cs_sc, cst_sc, yd_sc):
        c = pl.program_id(0)

        @pl.when(c == 0)
        def _():
            state[...] = jnp.zeros(state.shape, F32)

        dt, _, cs = _ssd_gates(dtf_ref[...], bias_ref[...], alog_ref[...])
        cs_sc[...] = cs
        cst_sc[...] = cs.T
        cs_last = cs[t - 1:t, :]
        expand = _head_expand()
        ex = _sel_right(jnp.concatenate([dt, jnp.exp(cs), jnp.exp(cs_last - cs)], axis=0), expand, terms=2)
        dt_x, eo_x, we_x = ex[0:t], ex[t:2 * t], ex[2 * t:3 * t]
        g_x = _sel_right(jnp.broadcast_to(jnp.exp(cs_last), (8, 128)), expand)[0:1]
        xs = xs_ref[...]
        x = xs * dt_x
        x_sc[...] = x.astype(BF16)
        xw_sc[...] = (x * we_x).astype(BF16)
        prev = state[...]
        st_ref[0] = prev
        prev_b = prev.astype(BF16)
        for g in range(2):
            cols = slice(g * 512, (g + 1) * 512)
            b_g = bc_ref[:, g * 128:(g + 1) * 128].astype(BF16)
            c_g = bc_ref[:, 256 + g * 128:256 + (g + 1) * 128].astype(BF16)
            gmat = _dot(c_g, b_g, NT)
            y_off = _dot(c_g, prev_b[:, cols]) * eo_x[:, cols]
            s_loc = _dot(b_g, xw_sc[:, cols], TN)
            state[:, cols] = g_x[:, cols] * prev[:, cols] + s_loc
            for e in range(HG):
                h = g * HG + e
                m = gmat * _decay_mask(cs_sc, cst_sc, h)
                yd_sc[:, h * HD:(h + 1) * HD] = _dot(m.astype(BF16), x_sc[:, h * HD:(h + 1) * HD])
            y_ref[:, cols] = yd_sc[:, cols] + y_off + dsk_ref[:, cols] * xs[:, cols]

    return pl.pallas_call(
        body, name="ssd_fwd", grid=(nc,),
        in_specs=[pl.BlockSpec((t, D), lambda c: (c, 0)),
                  pl.BlockSpec((t, 512), lambda c: (c, 0)),
                  pl.BlockSpec((t, 128), lambda c: (c, OFF_DTF // 128)),
                  pl.BlockSpec((1, 128), lambda c: (0, 0)),
                  pl.BlockSpec((1, 128), lambda c: (0, 0)),
                  pl.BlockSpec((1, D), lambda c: (0, 0))],
        out_specs=[pl.BlockSpec((t, D), lambda c: (c, 0)),
                   pl.BlockSpec((1, NSTATE, D), lambda c: (c, 0, 0))],
        out_shape=[jax.ShapeDtypeStruct((s, D), F32), jax.ShapeDtypeStruct((nc, NSTATE, D), F32)],
        scratch_shapes=[pltpu.VMEM((NSTATE, D), F32), pltpu.VMEM((t, D), BF16), pltpu.VMEM((t, D), BF16),
                        pltpu.VMEM((t, 128), F32), pltpu.VMEM((128, t), F32), pltpu.VMEM((t, D), F32)],
        compiler_params=_params(("arbitrary",)),
    )(xs_a, bc_a, p, bias128, alog128, dskip_x)


def _ssd_bwd(dy, xs_a, bc_a, p, states, bias128, alog128, dskip_x, s):
    nc = s // CHUNK
    t = CHUNK

    def body(dy_ref, xs_ref, bc_ref, dtf_ref, st_ref, bias_ref, alog_ref, dsk_ref,
             dxs_ref, dbc_ref, ddt_ref, dalog_ref, dskip_ref,
             dstate, x_sc, dy_sc, dx_sc, deo_sc, dwe_sc, cs_sc, cst_sc, dcol_sc, drow_sc):
        i = pl.program_id(0)

        @pl.when(i == 0)
        def _():
            dstate[...] = jnp.zeros(dstate.shape, F32)
            dalog_ref[...] = jnp.zeros(dalog_ref.shape, F32)
            dskip_ref[...] = jnp.zeros(dskip_ref.shape, F32)

        dtf = dtf_ref[...]
        dt, a_neg, cs = _ssd_gates(dtf, bias_ref[...], alog_ref[...])
        cs_sc[...] = cs
        cst_sc[...] = cs.T
        cs_last = cs[t - 1:t, :]
        eo, we, g_end = jnp.exp(cs), jnp.exp(cs_last - cs), jnp.exp(cs_last)
        expand, reduce = _head_expand(), _head_reduce()
        ex = _sel_right(jnp.concatenate([dt, eo, we], axis=0), expand, terms=2)
        dt_x, eo_x, we_x = ex[0:t], ex[t:2 * t], ex[2 * t:3 * t]
        g_x = _sel_right(jnp.broadcast_to(g_end, (8, 128)), expand)[0:1]
        xs = xs_ref[...]
        dyv = dy_ref[...]
        x = xs * dt_x
        x_sc[...] = x.astype(BF16)
        dy_sc[...] = dyv.astype(BF16)
        dyo_b = (dyv * eo_x).astype(BF16)
        xw_b = (x * we_x).astype(BF16)
        prev = st_ref[0]
        prev_b = prev.astype(BF16)
        dnext = dstate[...]
        dnext_b = dnext.astype(BF16)
        dcol_sc[...] = jnp.zeros(dcol_sc.shape, F32)
        drow_sc[...] = jnp.zeros(drow_sc.shape, F32)
        lane_row = _iota((1, 128), 1)
        sub_col = _iota((128, 1), 0)
        for g in range(2):
            cols = slice(g * 512, (g + 1) * 512)
            b_g = bc_ref[:, g * 128:(g + 1) * 128].astype(BF16)
            c_g = bc_ref[:, 256 + g * 128:256 + (g + 1) * 128].astype(BF16)
            gmat = _dot(c_g, b_g, NT)
            b_ds = _dot(b_g, dnext_b[:, cols])
            c_s = _dot(c_g, prev_b[:, cols])
            dx_sc[:, cols] = b_ds * we_x[:, cols]
            deo_sc[:, cols] = dyv[:, cols] * c_s
            dwe_sc[:, cols] = b_ds * x[:, cols]
            db = _dot(xw_b[:, cols], dnext_b[:, cols], NT)
            dc = _dot(dyo_b[:, cols], prev_b[:, cols], NT)
            dstate[:, cols] = g_x[:, cols] * dnext[:, cols] + _dot(c_g, dyo_b[:, cols], TN)
            dg = jnp.zeros((t, t), F32)
            for e in range(HG):
                h = g * HG + e
                hc = slice(h * HD, (h + 1) * HD)
                lmat = _decay_mask(cs_sc, cst_sc, h)
                m = gmat * lmat
                dx_sc[:, hc] += _dot(m.astype(BF16), dy_sc[:, hc], TN)
                dm = _dot(dy_sc[:, hc], x_sc[:, hc], NT)
                dg = dg + dm * lmat
                qm = dm * m
                dcol_sc[...] += jnp.sum(qm, axis=1, keepdims=True) * (lane_row == h).astype(F32)
                drow_sc[...] += (sub_col == h).astype(F32) * jnp.sum(qm, axis=0, keepdims=True)
            dg_b = dg.astype(BF16)
            dbc_ref[:, g * 128:(g + 1) * 128] = db + _dot(dg_b, c_g, TN)
            dbc_ref[:, 256 + g * 128:256 + (g + 1) * 128] = dc + _dot(dg_b, b_g)
        d_eo = _sel_right(deo_sc[...], reduce, terms=2)
        d_we = _sel_right(dwe_sc[...], reduce, terms=2)
        d_gend = _sel_right(jnp.broadcast_to(_colsum(dnext * prev), (8, D)), reduce)[0:1]
        d_cs = dcol_sc[...] - drow_sc[...].T + d_eo * eo - d_we * we
        extra = _colsum(d_we * we) + d_gend * g_end
        d_cs = d_cs + jnp.where(_iota((t, 128), 0) == t - 1, extra, 0.0)
        da = _sel_left(_tri_upper(t), d_cs)
        dx = dx_sc[...]
        ddt = _sel_right(dx * xs, reduce, terms=2) + da * a_neg
        dxs_ref[...] = dx * dt_x + dsk_ref[...] * dyv
        ddt_ref[...] = jnp.where(_iota((t, 128), 1) < NH, ddt * _sigmoid(dtf + bias_ref[...]), 0.0)
        dalog_ref[...] += _colsum(da * dt) * a_neg
        dskip_ref[...] += _sel_right(jnp.broadcast_to(_colsum(dyv * xs), (8, D)), reduce)[0:1]

    rev = lambda i: nc - 1 - i
    return pl.pallas_call(
        body, name="ssd_bwd", grid=(nc,),
        in_specs=[pl.BlockSpec((t, D), lambda i: (rev(i), 0)),
                  pl.BlockSpec((t, D), lambda i: (rev(i), 0)),
                  pl.BlockSpec((t, 512), lambda i: (rev(i), 0)),
                  pl.BlockSpec((t, 128), lambda i: (rev(i), OFF_DTF // 128)),
                  pl.BlockSpec((1, NSTATE, D), lambda i: (rev(i), 0, 0)),
                  pl.BlockSpec((1, 128), lambda i: (0, 0)),
                  pl.BlockSpec((1, 128), lambda i: (0, 0)),
                  pl.BlockSpec((1, D), lambda i: (0, 0))],
        out_specs=[pl.BlockSpec((t, D), lambda i: (rev(i), 0)),
                   pl.BlockSpec((t, 512), lambda i: (rev(i), 0)),
                   pl.BlockSpec((t, 128), lambda i: (rev(i), 0)),
                   pl.BlockSpec((1, 128), lambda i: (0, 0)),
                   pl.BlockSpec((1, 128), lambda i: (0, 0))],
        out_shape=[jax.ShapeDtypeStruct((s, D), F32), jax.ShapeDtypeStruct((s, 512), F32),
                   jax.ShapeDtypeStruct((s, 128), F32), jax.ShapeDtypeStruct((1, 128), F32),
                   jax.ShapeDtypeStruct((1, 128), F32)],
        scratch_shapes=[pltpu.VMEM((NSTATE, D), F32), pltpu.VMEM((t, D), BF16), pltpu.VMEM((t, D), BF16),
                        pltpu.VMEM((t, D), F32), pltpu.VMEM((t, D), F32), pltpu.VMEM((t, D), F32),
                        pltpu.VMEM((t, 128), F32), pltpu.VMEM((128, t), F32),
                        pltpu.VMEM((t, 128), F32), pltpu.VMEM((128, t), F32)],
        compiler_params=_params(("arbitrary",)),
    )(dy, xs_a, bc_a, p, states, bias128, alog128, dskip_x)


def _gate_lanes(shape):
    lane = _iota(shape, 1)
    return (lane >= NH) & (lane < 2 * NH)


def _cum_fwd(p, bias128, s):
    tr = min(512, s)

    def body(dtf_ref, bias_ref, o_ref, carry):
        @pl.when(pl.program_id(0) == 0)
        def _():
            carry[...] = jnp.zeros(carry.shape, F32)

        lf = jnp.where(_gate_lanes((tr, 128)), _log_sigmoid(dtf_ref[...] + bias_ref[...]), 0.0)
        cum = _sel_left(_tri_lower(tr), lf) + carry[...]
        carry[...] = cum[tr - 1:tr, :]
        o_ref[...] = cum

    return pl.pallas_call(
        body, name="cum_fwd", grid=(s // tr,),
        in_specs=[pl.BlockSpec((tr, 128), lambda i: (i, OFF_DTF // 128)), pl.BlockSpec((1, 128), lambda i: (0, 0))],
        out_specs=pl.BlockSpec((tr, 128), lambda i: (i, 0)),
        out_shape=jax.ShapeDtypeStruct((s, 128), F32),
        scratch_shapes=[pltpu.VMEM((1, 128), F32)],
        compiler_params=_params(("arbitrary",)),
    )(p, bias128)


def _cum_bwd(dcum, ddt_raw, p, bias128, s):
    tr = min(512, s)

    def fn(pos, dcum, ddt, dtf, bias, carry, acc):
        suffix = _sel_left(_tri_upper(tr), dcum) + carry
        dfr = jnp.where(_gate_lanes((tr, 128)), suffix * _sigmoid(-(dtf + bias)), 0.0)
        out = ddt + dfr
        return out, suffix[0:1, :], acc + _colsum(out)

    return _rowk("cum_bwd", fn, s, tr, [(dcum, 128, 0, 0), (ddt_raw, 128, 0, 0), (p, 128, OFF_DTF // 128, 0)],
                 [bias128], [(128, BF16)], [(1, 128), (1, 128)], reverse=True)


ATT_BLOCK = 512
ATT_STRIP = 32


def _head_part(shape, h, dim):
    i = _iota(shape, dim)
    return (i >= h * HD) & (i < (h + 1) * HD)


def _k_augmented(k_blk, cum_blk, j, h):
    tk = k_blk.shape[0]
    lane = _iota((tk, 128), 1)
    col = jnp.sum(jnp.where(lane == NH + 2 * j + h, cum_blk, 0.0), axis=1, keepdims=True)
    c0, c1, c2 = [c.astype(F32) for c in _split3(-col)]
    k_h = k_blk if h == 0 else pltpu.roll(k_blk, HD, 1)
    aug = jnp.where(lane == HD, c0, jnp.where(lane == HD + 1, c1, jnp.where(lane == HD + 2, c2, 0.0)))
    return jnp.where(lane < HD, k_h, aug).astype(BF16)


def _q_augmented_t(q_blk):
    tq = q_blk.shape[0]
    q_t = (q_blk * ATT_SCALE).T.astype(BF16)
    ones = (_iota((HD, tq), 0) < 3).astype(BF16)
    return [jnp.concatenate([q_t[h * HD:(h + 1) * HD], ones], axis=0) for h in range(2)]


def _rows01(r0, r1):
    sub = _iota((8, r0.shape[1]), 0)
    return jnp.where(sub == 0, r0, jnp.where(sub == 1, r1, 0.0))


def _fold8(x, op, cur):
    for g in range(x.shape[0] // 8):
        cur = op(cur, x[8 * g:8 * (g + 1), :])
    return cur


def _attn_fwd(p, cum, s):
    t = min(ATT_BLOCK, s)
    nq = s // t
    r = ATT_STRIP

    def body(q_ref, k_ref, v_ref, c_ref, o_ref, lse_ref, kaug_sc, vt_sc, s0_sc, s1_sc, p0_sc, p1_sc, m_sc, l_sc, acc_sc):
        j, qi = pl.program_id(0), pl.program_id(1)
        s_sc, p_sc = (s0_sc, s1_sc), (p0_sc, p1_sc)

        @pl.when(qi == 0)
        def _():
            for c in range(nq):
                rows = slice(c * t, (c + 1) * t)
                k_blk, vt = k_ref[rows, :], v_ref[rows, :].T
                for h in range(2):
                    kaug_sc[h, rows, :] = _k_augmented(k_blk, c_ref[rows, :], j, h)
                    vt_sc[h, :, rows] = vt[h * HD:(h + 1) * HD].astype(BF16)

        qaug_t = _q_augmented_t(q_ref[...])
        m_sc[...] = jnp.full(m_sc.shape, -1e30, F32)
        l_sc[...] = jnp.zeros(l_sc.shape, F32)
        acc_sc[...] = jnp.zeros(acc_sc.shape, F32)
        top = _iota((128, t), 0) < HD

        def logits(kb, buf):
            kv = pl.ds(pl.multiple_of(kb * t, t), t)
            for h in range(2):
                s_sc[buf][h] = _dot(kaug_sc[h, kv, :], qaug_t[h])

        def softmax(buf, diagonal):
            alphas = []
            for h in range(2):
                cur = jnp.full((8, t), -1e30, F32)
                for i in range(t // r):
                    rows = slice(i * r, (i + 1) * r)
                    x = s_sc[buf][h, rows, :]
                    if diagonal:
                        x = jnp.where(_iota((r, t), 1) >= i * r + _iota((r, t), 0), x, -1e30)
                        s_sc[buf][h, rows, :] = x
                    cur = _fold8(x, jnp.maximum, cur)
                m_prev = m_sc[h, 0:1, :]
                m_new = jnp.maximum(m_prev, jnp.max(cur, axis=0, keepdims=True))
                alpha = jnp.exp(m_prev - m_new)
                m_sc[h, 0:1, :] = m_new
                alphas.append(alpha)
                tot = jnp.zeros((8, t), F32)
                for i in range(t // r):
                    rows = slice(i * r, (i + 1) * r)
                    pr = jnp.exp(s_sc[buf][h, rows, :] - m_new)
                    p_sc[buf][h, rows, :] = pr.astype(BF16)
                    tot = _fold8(pr, jnp.add, tot)
                l_sc[h, 0:1, :] = alpha * l_sc[h, 0:1, :] + jnp.sum(tot, axis=0, keepdims=True)
            return alphas

        def accumulate(kb, buf, alphas):
            kv = pl.ds(pl.multiple_of(kb * t, t), t)
            for h in range(2):
                part = slice(h * HD, (h + 1) * HD)
                acc_sc[part, :] = acc_sc[part, :] * alphas[h] + _dot(vt_sc[h, :, kv], p_sc[buf][h])

        def steady(u, alphas_b):
            a, b = 2 * u, 2 * u + 1
            accumulate(jnp.maximum(b - 2, 0), 1, alphas_b)
            logits(b, 1)
            accumulate(a, 0, softmax(0, False))
            logits(a + 2, 0)
            return tuple(softmax(1, False))

        ones = jnp.ones((1, t), F32)
        p_sc[1][...] = jnp.zeros(p_sc[1].shape, BF16)
        logits(0, 0)
        alphas_b = lax.fori_loop(0, qi // 2, steady, (ones, ones))
        last_b = jnp.maximum(2 * (qi // 2) - 1, 0)

        @pl.when(qi % 2 == 0)
        def _():
            accumulate(last_b, 1, alphas_b)
            accumulate(qi, 0, softmax(0, True))

        @pl.when(qi % 2 == 1)
        def _():
            accumulate(last_b, 1, alphas_b)
            logits(qi, 1)
            accumulate(qi - 1, 0, softmax(0, False))
            accumulate(qi, 1, softmax(1, True))

        l0, l1 = l_sc[0, 0:1, :], l_sc[1, 0:1, :]
        o_ref[...] = (acc_sc[...] / jnp.where(top, l0, l1)).T
        lse_ref[0] = _rows01(m_sc[0, 0:1, :] + jnp.log(l0), m_sc[1, 0:1, :] + jnp.log(l1))

    return pl.pallas_call(
        body, name="attn_fwd", grid=(NH // 2, nq),
        in_specs=[pl.BlockSpec((t, 128), lambda j, qi: (qi, OFF_Q // 128 + j)),
                  pl.BlockSpec((s, 128), lambda j, qi: (0, OFF_K // 128 + j)),
                  pl.BlockSpec((s, 128), lambda j, qi: (0, OFF_V // 128 + j)),
                  pl.BlockSpec((s, 128), lambda j, qi: (0, 0))],
        out_specs=[pl.BlockSpec((t, 128), lambda j, qi: (qi, j)),
                   pl.BlockSpec((1, 8, t), lambda j, qi: (j, 0, qi))],
        out_shape=[jax.ShapeDtypeStruct((s, D), F32), jax.ShapeDtypeStruct((NH // 2, 8, s), F32)],
        scratch_shapes=[pltpu.VMEM((2, s, 128), BF16), pltpu.VMEM((2, HD, s), BF16), pltpu.VMEM((2, t, t), F32),
                        pltpu.VMEM((2, t, t), F32), pltpu.VMEM((2, t, t), BF16), pltpu.VMEM((2, t, t), BF16),
                        pltpu.VMEM((2, 8, t), F32), pltpu.VMEM((2, 8, t), F32), pltpu.VMEM((128, t), F32)],
        compiler_params=_params(("parallel", "arbitrary")),
    )(p, p, p, cum)


def _attn_bwd(p, cum, o, lse, do, s):
    t = min(ATT_BLOCK, s)
    nq = s // t
    r = ATT_STRIP

    def body(q_ref, k_ref, v_ref, c_ref, o_ref, lse_ref, do_ref, dq_ref, dk_ref, dv_ref, dc_ref, dr_ref,
             qaugt_sc, qh_sc, dot_sc, doh_sc, delta_sc, dqt_sc, dr_sc, kaug_sc, vh_sc, kt_sc,
             s0_sc, s1_sc, dp0_sc, dp1_sc, p0_sc, p1_sc, ds0_sc, ds1_sc, dk_sc, dv_sc, dc_sc):
        j, ki = pl.program_id(0), pl.program_id(1)
        s_sc, dp_sc, p_sc, ds_sc = (s0_sc, s1_sc), (dp0_sc, dp1_sc), (p0_sc, p1_sc), (ds0_sc, ds1_sc)

        @pl.when(ki == 0)
        def _():
            for c in range(nq):
                rows = slice(c * t, (c + 1) * t)
                q_blk, do_blk = q_ref[rows, :], do_ref[rows, :]
                qaugt_sc[0, :, rows], qaugt_sc[1, :, rows] = _q_augmented_t(q_blk)
                dot_sc[:, rows] = do_blk.T.astype(BF16)
                prod_t = (do_blk * o_ref[rows, :]).T
                delta_sc[:, rows] = _rows01(jnp.sum(prod_t[0:HD], axis=0, keepdims=True),
                                            jnp.sum(prod_t[HD:], axis=0, keepdims=True))
                for h in range(2):
                    head = _head_part((t, 128), h, 1)
                    qh_sc[h, rows, :] = jnp.where(head, q_blk * ATT_SCALE, 0.0).astype(BF16)
                    doh_sc[h, rows, :] = jnp.where(head, do_blk, 0.0).astype(BF16)
            dqt_sc[...] = jnp.zeros(dqt_sc.shape, F32)
            dr_sc[...] = jnp.zeros(dr_sc.shape, F32)

        k_blk, v_blk = k_ref[...], v_ref[...]
        kt = k_blk.T
        for h in range(2):
            kaug_sc[h] = _k_augmented(k_blk, c_ref[...], j, h)
            vh_sc[h] = jnp.where(_head_part((t, 128), h, 1), v_blk, 0.0).astype(BF16)
            kt_sc[h] = kt[h * HD:(h + 1) * HD].astype(BF16)
        dk_sc[...] = jnp.zeros(dk_sc.shape, F32)
        dv_sc[...] = jnp.zeros(dv_sc.shape, F32)
        dc_sc[...] = jnp.zeros(dc_sc.shape, F32)

        def inputs(qb, buf):
            qs = pl.ds(pl.multiple_of(qb * t, t), t)
            for h in range(2):
                s_sc[buf][h] = _dot(kaug_sc[h], qaugt_sc[h, :, qs])
                dp_sc[buf][h] = _dot(vh_sc[h], dot_sc[:, qs])

        def elementwise(qb, buf, diagonal):
            qs = pl.ds(pl.multiple_of(qb * t, t), t)
            for h in range(2):
                lse_row, delta_row = lse_ref[0, h:h + 1, qs], delta_sc[h:h + 1, qs]
                tot = jnp.zeros((8, t), F32)
                for i in range(t // r):
                    rows = slice(i * r, (i + 1) * r)
                    x = s_sc[buf][h, rows, :]
                    if diagonal:
                        x = jnp.where(_iota((r, t), 1) >= i * r + _iota((r, t), 0), x, -1e30)
                    pr = jnp.exp(x - lse_row)
                    ds = pr * (dp_sc[buf][h, rows, :] - delta_row)
                    p_sc[buf][h, rows, :] = pr.astype(BF16)
                    ds_sc[buf][h, rows, :] = ds.astype(BF16)
                    dc_sc[h, rows, :] += sum(ds[:, 128 * g:128 * (g + 1)] for g in range(t // 128))
                    tot = _fold8(ds, jnp.add, tot)
                dr_sc[h, :, qs] += tot

        def outputs(qb, buf):
            qs = pl.ds(pl.multiple_of(qb * t, t), t)
            dv_sc[...] += _dot(p_sc[buf][0], doh_sc[0, qs, :]) + _dot(p_sc[buf][1], doh_sc[1, qs, :])
            dk_sc[...] += _dot(ds_sc[buf][0], qh_sc[0, qs, :]) + _dot(ds_sc[buf][1], qh_sc[1, qs, :])
            for h in range(2):
                dqt_sc[h * HD:(h + 1) * HD, qs] += _dot(kt_sc[h], ds_sc[buf][h])

        def pair(a, b, a_diagonal):
            inputs(a, 0)
            inputs(b, 1)
            elementwise(a, 0, a_diagonal)
            outputs(a, 0)
            elementwise(b, 1, False)
            outputs(b, 1)

        def later(u, carry):
            pair(ki + 1 + 2 * u, ki + 2 + 2 * u, False)
            return carry

        n_later = nq - 1 - ki
        lax.fori_loop(0, n_later // 2, later, 0)

        @pl.when(n_later % 2 == 1)
        def _():
            pair(ki, nq - 1, True)

        @pl.when(n_later % 2 == 0)
        def _():
            inputs(ki, 0)
            elementwise(ki, 0, True)
            outputs(ki, 0)

        dk_ref[...] = dk_sc[...].astype(BF16)
        dv_ref[...] = dv_sc[...].astype(BF16)
        lane = _iota((t, 128), 1)
        cols = jnp.where(lane == 0, jnp.sum(dc_sc[0], axis=1, keepdims=True),
                         jnp.where(lane == 1, jnp.sum(dc_sc[1], axis=1, keepdims=True), 0.0))
        dc_ref[0] = cols.T[0:8, :]

        @pl.when(ki == nq - 1)
        def _():
            for c in range(nq):
                rows = slice(c * t, (c + 1) * t)
                dq_ref[rows, :] = dqt_sc[:, rows].T * ATT_SCALE
            dr_ref[0] = _rows01(jnp.sum(dr_sc[0], axis=0, keepdims=True), jnp.sum(dr_sc[1], axis=0, keepdims=True))

    whole = lambda off: pl.BlockSpec((s, 128), functools.partial(lambda j, ki, off: (0, off + j), off=off))
    return pl.pallas_call(
        body, name="attn_bwd", grid=(NH // 2, nq),
        in_specs=[whole(OFF_Q // 128),
                  pl.BlockSpec((t, 128), lambda j, ki: (ki, OFF_K // 128 + j)),
                  pl.BlockSpec((t, 128), lambda j, ki: (ki, OFF_V // 128 + j)),
                  pl.BlockSpec((t, 128), lambda j, ki: (ki, 0)),
                  whole(0),
                  pl.BlockSpec((1, 8, s), lambda j, ki: (j, 0, 0)),
                  whole(0)],
        out_specs=[whole(0),
                   pl.BlockSpec((t, 128), lambda j, ki: (ki, j)),
                   pl.BlockSpec((t, 128), lambda j, ki: (ki, j)),
                   pl.BlockSpec((1, 8, t), lambda j, ki: (j, 0, ki)),
                   pl.BlockSpec((1, 8, s), lambda j, ki: (j, 0, 0))],
        out_shape=[jax.ShapeDtypeStruct((s, D), F32), jax.ShapeDtypeStruct((s, D), BF16), jax.ShapeDtypeStruct((s, D), BF16),
                   jax.ShapeDtypeStruct((NH // 2, 8, s), F32), jax.ShapeDtypeStruct((NH // 2, 8, s), F32)],
        scratch_shapes=[pltpu.VMEM((2, 128, s), BF16), pltpu.VMEM((2, s, 128), BF16), pltpu.VMEM((128, s), BF16),
                        pltpu.VMEM((2, s, 128), BF16), pltpu.VMEM((8, s), F32), pltpu.VMEM((128, s), F32),
                        pltpu.VMEM((2, 8, s), F32), pltpu.VMEM((2, t, 128), BF16), pltpu.VMEM((2, t, 128), BF16),
                        pltpu.VMEM((2, HD, t), BF16)]
        + [pltpu.VMEM((2, t, t), F32)] * 4 + [pltpu.VMEM((2, t, t), BF16)] * 4
        + [pltpu.VMEM((t, 128), F32), pltpu.VMEM((t, 128), F32), pltpu.VMEM((2, t, 128), F32)],
        compiler_params=_params(("parallel", "arbitrary")),
    )(p, p, p, cum, o, lse, do)


def _ln_stats(u):
    mu = _mean(u)
    d = u - mu
    rstd = lax.rsqrt(_mean(d * d) + EPS)
    return d * rstd, rstd


def _ln_bwd(dx, xh, rstd, gam):
    dxh = dx * gam
    return rstd * (dxh - _mean(dxh) - xh * _mean(dxh * xh))


def _rms_bwd(d, xn, r, w):
    t = d * w
    return r * (t - xn * _mean(t * xn)), _colsum(d * xn)


def _mix_norm(y, p, att, w_ssm, w_att, s):
    def fn(pos, y, z, att, w1, w2):
        g = y * _silu(z)
        n1 = g * lax.rsqrt(_mean(g * g) + EPS) * w1
        n2 = att * lax.rsqrt(_mean(att * att) + EPS) * w2
        return (jnp.concatenate([n1, n2], axis=1),)

    return _rowk("mix_norm", fn, s, 256, [(y, D, 0, 0), (p, D, OFF_Z // D, 0), (att, D, 0, 0)],
                 [w_ssm, w_att], [(2 * D, BF16)], [])[0]


def _mix_norm_bwd(dmix, y, p, att, w_ssm, w_att, s):
    def fn(pos, dmix, y, z, att, w1, w2, a1, a2):
        sz = _silu(z)
        g = y * sz
        r1 = lax.rsqrt(_mean(g * g) + EPS)
        dg, dw1 = _rms_bwd(dmix[:, :D], g * r1, r1, w1)
        r2 = lax.rsqrt(_mean(att * att) + EPS)
        datt, dw2 = _rms_bwd(dmix[:, D:], att * r2, r2, w2)
        return dg * sz, dg * y * _dsilu(z), datt, a1 + dw1, a2 + dw2

    return _rowk("mix_norm_bwd", fn, s, 256, [(dmix, 2 * D, 0, 0), (y, D, 0, 0), (p, D, OFF_Z // D, 0), (att, D, 0, 0)],
                 [w_ssm, w_att], [(D, F32), (D, BF16), (D, F32)], [(1, D), (1, D)])


def _ln1(x0, y, g1, gam, bet, sc2, sh2, s):
    def fn(pos, x0, y, g1, gam, bet, sc2, sh2):
        xh, _ = _ln_stats(ALPHA * x0 + (1.0 + g1) * y)
        x1 = xh * gam + bet
        return x1, _modulate(x1, sc2, sh2)

    return _rowk("ln1", fn, s, 256, [(x0, D, 0, 0), (y, D, 0, 0)], [g1, gam, bet, sc2, sh2], [(D, F32), (D, BF16)], [])


def _ln2_loss(x1, ff, tgt, g2, gam, bet, s):
    def fn(pos, x1, ff, tgt, g2, gam, bet, a_loss, a_dgam, a_dbet, a_dg2):
        xh, rstd = _ln_stats(ALPHA * x1 + (1.0 + g2) * ff)
        err = xh * gam + bet - tgt
        dx2 = err * (1.0 / D)
        du = _ln_bwd(dx2, xh, rstd, gam)
        return (du, du * (1.0 + g2), a_loss + _colsum(err * err), a_dgam + _colsum(dx2 * xh),
                a_dbet + _colsum(dx2), a_dg2 + _colsum(du * ff))

    return _rowk("ln2_loss", fn, s, 256, [(x1, D, 0, 0), (ff, D, 0, 0), (tgt, D, 0, 0)], [g2, gam, bet],
                 [(D, F32), (D, BF16)], [(1, D)] * 4)


def _ln1_bwd(dh2, du2, x0, y, g1, gam, bet, sc2, s):
    def fn(pos, dh2, du2, x0, y, g1, gam, bet, sc2, a_sc, a_sh, a_gam, a_bet, a_g1):
        xh, rstd = _ln_stats(ALPHA * x0 + (1.0 + g1) * y)
        x1 = xh * gam + bet
        dx1 = ALPHA * du2 + dh2 * (1.0 + sc2)
        du1 = _ln_bwd(dx1, xh, rstd, gam)
        return (du1, du1 * (1.0 + g1), a_sc + _colsum(dh2 * x1), a_sh + _colsum(dh2), a_gam + _colsum(dx1 * xh),
                a_bet + _colsum(dx1), a_g1 + _colsum(du1 * y))

    return _rowk("ln1_bwd", fn, s, 256, [(dh2, D, 0, 0), (du2, D, 0, 0), (x0, D, 0, 0), (y, D, 0, 0)],
                 [g1, gam, bet, sc2], [(D, F32), (D, BF16)], [(1, D)] * 5)


def _input_grad(dh1, du1, x0, sc1, s):
    def fn(pos, dh1, du1, x0, sc1, a_sc, a_sh):
        return ALPHA * du1 + dh1 * (1.0 + sc1), a_sc + _colsum(dh1 * x0), a_sh + _colsum(dh1)

    return _rowk("input_grad", fn, s, 256, [(dh1, D, 0, 0), (du1, D, 0, 0), (x0, D, 0, 0)], [sc1],
                 [(D, F32)], [(1, D)] * 2)


def _adamw_math(w, grad, m, v):
    m_new = ADAM_B1 * m + (1.0 - ADAM_B1) * grad
    v_new = ADAM_B2 * v + (1.0 - ADAM_B2) * (grad * grad)
    m_hat = m_new / (1.0 - ADAM_B1 ** ADAM_STEP)
    v_hat = v_new / (1.0 - ADAM_B2 ** ADAM_STEP)
    return -ADAM_LR * (m_hat / (jnp.sqrt(v_hat) + ADAM_EPS) + ADAM_WD * w), m_new, v_new


def _small_update(small_all, layout, w, m, v):
    names = [n for n, _, _ in layout]

    def body(*refs):
        all_ref = refs[0]
        w_refs, m_refs, v_refs = [refs[1 + k * len(names):1 + (k + 1) * len(names)] for k in range(3)]
        sum_ref = refs[1 + 3 * len(names)]
        outs = refs[2 + 3 * len(names):]
        total = all_ref[0]
        for k in range(1, N_DEV):
            total = total + all_ref[k]
        sum_ref[...] = total
        for i, (_, off, size) in enumerate(layout):
            grad = total[:, off:off + size]
            delta, m_new, v_new = _adamw_math(w_refs[i][...], grad, m_refs[i][...], v_refs[i][...])
            for o, val in zip(outs[4 * i:4 * i + 4], (grad, delta, m_new, v_new)):
                o[...] = val

    res = pl.pallas_call(
        body, name="small_update",
        out_shape=[jax.ShapeDtypeStruct(small_all.shape[1:], F32)]
        + [jax.ShapeDtypeStruct(w[n].shape, F32) for n in names for _ in range(4)],
        compiler_params=_params(None),
    )(small_all, *[w[n] for n in names], *[m[n] for n in names], *[v[n] for n in names])
    return res[0], {n: res[1 + 4 * i:5 + 4 * i] for i, n in enumerate(names)}


def _adamw(name, w, g, m, v, *, tr, slots):
    r, c = w.shape

    def body(w_ref, g_ref, m_ref, v_ref, g_out, d_out, m_out, v_out):
        if slots:
            grad = g_ref[0][:, :c].astype(F32)
            for k in range(1, N_DEV):
                grad = grad + g_ref[k][:, :c].astype(F32)
        else:
            grad = g_ref[...]
        g_out[...] = grad
        d_out[...], m_out[...], v_out[...] = _adamw_math(w_ref[...], grad, m_ref[...], v_ref[...])

    tile = pl.BlockSpec((tr, c), lambda i: (i, 0))
    g_spec = pl.BlockSpec((N_DEV, tr, g.shape[-1]), lambda i: (0, i, 0)) if slots else tile
    return pl.pallas_call(
        body, name=name, grid=(r // tr,),
        in_specs=[tile, g_spec, tile, tile], out_specs=[tile] * 4,
        out_shape=[jax.ShapeDtypeStruct((r, c), F32)] * 4,
        compiler_params=_params(("parallel",)),
    )(w, g, m, v)


def _dot_f32(a, b, dims=NN):
    a0, a1, a2 = _split3(a)
    b0, b1, b2 = _split3(b)
    acc = _dot(a0, b0, dims)
    for x, y in ((a0, b1), (a1, b0), (a1, b1), (a0, b2), (a2, b0)):
        acc = acc + _dot(x, y, dims)
    return acc


def _ada_mod(c_all, w_shard, b_shard):
    def body(c_ref, w_ref, b_ref, o_ref):
        act = _silu(c_ref[...])
        act16 = jnp.concatenate([act, jnp.zeros_like(act)], axis=0)
        o_ref[...] = _dot_f32(act16, w_ref[...])[0:N_DEV] + b_ref[...]

    return pl.pallas_call(
        body, name="ada_mod", out_shape=jax.ShapeDtypeStruct((N_DEV, w_shard.shape[1]), F32),
        compiler_params=_params(None),
    )(c_all, w_shard, b_shard)


def _ada_grad(c_all, dmod_cols):
    def body(c_ref, dc_ref, gw_ref):
        act = _silu(c_ref[...])
        act16 = jnp.concatenate([act, jnp.zeros_like(act)], axis=0)
        dm = dc_ref[...]
        dm16 = jnp.concatenate([dm, jnp.zeros_like(dm)], axis=0)
        gw_ref[...] = _dot_f32(act16, dm16, TN)

    return pl.pallas_call(
        body, name="ada_grad", out_shape=jax.ShapeDtypeStruct((D, dmod_cols.shape[1]), F32),
        compiler_params=_params(None),
    )(c_all, dmod_cols)


def _exchange(name, xs, scatter):
    n = len(xs)
    n_peer = N_DEV - 1

    def body(*refs):
        x_refs, o_refs = refs[:n], refs[n:2 * n]
        send_sems, recv_sems, local_sems = refs[2 * n:]
        mx, my, mc = lax.axis_index("x"), lax.axis_index("y"), lax.axis_index("c")
        me = 4 * mx + 2 * my + mc

        def src(a, slot):
            return x_refs[a].at[slot] if scatter else x_refs[a]

        own = [pltpu.make_async_copy(src(a, me), o_refs[a].at[me], local_sems.at[a]) for a in range(n)]
        for cp in own:
            cp.start()
        sends = []
        for d in range(1, N_DEV):
            px = 1 - mx if d & 4 else mx
            py = 1 - my if d & 2 else my
            pc = 1 - mc if d & 1 else mc
            peer = 4 * px + 2 * py + pc
            for a in range(n):
                def copy(src_slot, dst_slot, a=a, d=d, to=(px, py, pc)):
                    return pltpu.make_async_remote_copy(
                        src_ref=src(a, src_slot), dst_ref=o_refs[a].at[dst_slot],
                        send_sem=send_sems.at[a * n_peer + d - 1], recv_sem=recv_sems.at[a * n_peer + d - 1],
                        device_id=to, device_id_type=pl.DeviceIdType.MESH)

                out = copy(peer, me)
                out.start()
                sends.append((out, copy(me, peer)))
        for _, arrival in sends:
            arrival.wait_recv()
        for out, _ in sends:
            out.wait_send()
        for cp in own:
            cp.wait()

    shapes = [tuple(x.shape[1:] if scatter else x.shape) for x in xs]
    return pl.pallas_call(
        body, name=name,
        in_specs=[pl.BlockSpec(memory_space=pl.ANY)] * n, out_specs=[pl.BlockSpec(memory_space=pl.ANY)] * n,
        out_shape=[jax.ShapeDtypeStruct((N_DEV,) + sh, x.dtype) for sh, x in zip(shapes, xs)],
        scratch_shapes=[pltpu.SemaphoreType.DMA((n * n_peer,)), pltpu.SemaphoreType.DMA((n * n_peer,)),
                        pltpu.SemaphoreType.DMA((n,))],
        compiler_params=pltpu.CompilerParams(has_side_effects=True),
    )(*xs)


def _gather_two_level(name, x):
    def body(x_ref, o_ref, send_sems, recv_sems, local_sem):
        mx, my, mc = lax.axis_index("x"), lax.axis_index("y"), lax.axis_index("c")
        me, sibling = (mx, my, mc), (mx, my, 1 - mc)
        chips = [(1 - mx, my), (mx, 1 - my), (1 - mx, 1 - my)]

        def slot(px, py, pc):
            return o_ref.at[4 * px + 2 * py + pc]

        def copy(k, block, to, src=None):
            return pltpu.make_async_remote_copy(
                src_ref=slot(*block) if src is None else src, dst_ref=slot(*block),
                send_sem=send_sems.at[k], recv_sem=recv_sems.at[k], device_id=to, device_id_type=pl.DeviceIdType.MESH)

        mine = pltpu.make_async_copy(x_ref, slot(*me), local_sem)
        mine.start()
        first = [copy(0, me, sibling, src=x_ref)] + [copy(1 + i, me, (*chip, mc), src=x_ref) for i, chip in enumerate(chips)]
        for cp in first:
            cp.start()
        passed = [copy(4 + i, (*chip, mc), sibling) for i, chip in enumerate(chips)]
        for i, chip in enumerate(chips):
            copy(1 + i, (*chip, mc), me).wait_recv()
            passed[i].start()
        copy(0, sibling, me).wait_recv()
        for i, chip in enumerate(chips):
            copy(4 + i, (*chip, 1 - mc), me).wait_recv()
        for cp in first + passed:
            cp.wait_send()
        mine.wait()

    return pl.pallas_call(
        body, name=name,
        in_specs=[pl.BlockSpec(memory_space=pl.ANY)], out_specs=pl.BlockSpec(memory_space=pl.ANY),
        out_shape=jax.ShapeDtypeStruct((N_DEV,) + tuple(x.shape), x.dtype),
        scratch_shapes=[pltpu.SemaphoreType.DMA((7,)), pltpu.SemaphoreType.DMA((7,)), pltpu.SemaphoreType.DMA(())],
        compiler_params=pltpu.CompilerParams(has_side_effects=True),
    )(x)


def _after(x, zero):
    return x if zero is None else x + zero.reshape(-1)[0].astype(x.dtype)


_HBM = pl.BlockSpec(memory_space=pltpu.HBM)
_SEM = pl.BlockSpec(memory_space=pltpu.SEMAPHORE)


def _exchange_copies(x_refs, land_refs, send_sems, recv_sems, scatter):
    n = len(x_refs)
    n_peer = N_DEV - 1
    mx, my, mc = lax.axis_index("x"), lax.axis_index("y"), lax.axis_index("c")
    me = 4 * mx + 2 * my + mc
    pairs = []
    for d in range(1, N_DEV):
        px = 1 - mx if d & 4 else mx
        py = 1 - my if d & 2 else my
        pc = 1 - mc if d & 1 else mc
        peer = 4 * px + 2 * py + pc
        for a in range(n):
            def copy(src_slot, dst_slot, a=a, d=d, to=(px, py, pc)):
                return pltpu.make_async_remote_copy(
                    src_ref=x_refs[a].at[src_slot] if scatter else x_refs[a], dst_ref=land_refs[a].at[dst_slot],
                    send_sem=send_sems.at[a * n_peer + d - 1], recv_sem=recv_sems.at[a * n_peer + d - 1],
                    device_id=to, device_id_type=pl.DeviceIdType.MESH)

            pairs.append((copy(peer, me), copy(me, peer)))
    return me, pairs


def _exchange_async(name, xs, scatter, collective_id):
    n = len(xs)
    shapes = [tuple(x.shape[1:] if scatter else x.shape) for x in xs]
    x_refs = [jax.new_ref(x, memory_space=pltpu.MemorySpace.HBM) for x in xs]
    land_refs = [jax.empty_ref(jax.ShapeDtypeStruct((N_DEV,) + sh, x.dtype), memory_space=pltpu.MemorySpace.HBM)
                 for sh, x in zip(shapes, xs)]

    @pl.kernel(mesh=plsc.ScalarSubcoreMesh(axis_name="sequencer", num_cores=1), name=name,
               scratch_types=(pltpu.SemaphoreType.DMA((n * (N_DEV - 1),)), pltpu.SemaphoreType.DMA((n * (N_DEV - 1),)),
                              pltpu.SemaphoreType.DMA((n,))),
               compiler_params=pltpu.CompilerParams(collective_id=collective_id))
    def launch(send_sems, recv_sems, own_sems):
        barrier = pltpu.get_barrier_semaphore()
        mx, my, mc = lax.axis_index("x"), lax.axis_index("y"), lax.axis_index("c")
        for d in range(1, N_DEV):
            peer = (1 - mx if d & 4 else mx, 1 - my if d & 2 else my, 1 - mc if d & 1 else mc)
            pl.semaphore_signal(barrier, inc=1, device_id=peer, device_id_type=pl.DeviceIdType.MESH)
        pl.semaphore_wait(barrier, N_DEV - 1)
        me, pairs = _exchange_copies(x_refs, land_refs, send_sems, recv_sems, scatter)
        own = [pltpu.make_async_copy(x_refs[a].at[me] if scatter else x_refs[a], land_refs[a].at[me], own_sems.at[a])
               for a in range(n)]
        for cp in own:
            cp.start()
        for out, _ in pairs:
            out.start()
        for out, arrival in pairs:
            arrival.wait_recv()
            out.wait_send()
        for cp in own:
            cp.wait()

    launch()
    return lambda: [r[...] for r in land_refs]


def _exchange_start(name, xs, scatter):
    n = len(xs)
    shapes = [tuple(x.shape[1:] if scatter else x.shape) for x in xs]

    def body(*refs):
        x_refs, land_refs = refs[:n], refs[n:2 * n]
        send_sems, recv_sems = refs[2 * n], refs[2 * n + 1]
        token, own_sems = refs[4 * n + 2], refs[4 * n + 3]
        me, pairs = _exchange_copies(x_refs, land_refs, send_sems, recv_sems, scatter)
        own = [pltpu.make_async_copy(x_refs[a].at[me] if scatter else x_refs[a], land_refs[a].at[me], own_sems.at[a])
               for a in range(n)]
        for cp in own:
            cp.start()
        for out, _ in pairs:
            out.start()
        for cp in own:
            cp.wait()
        token[...] = jnp.zeros(token.shape, token.dtype)

    lands = [pltpu.with_memory_space_constraint(lax.empty((N_DEV,) + sh, x.dtype), pltpu.HBM) for sh, x in zip(shapes, xs)]
    res = pl.pallas_call(
        body, name=name,
        out_shape=(pltpu.SemaphoreType.DMA((n * (N_DEV - 1),)), pltpu.SemaphoreType.DMA((n * (N_DEV - 1),)),
                   *[pltpu.HBM(x.shape, x.dtype) for x in xs], *[pltpu.HBM(l.shape, l.dtype) for l in lands],
                   jax.ShapeDtypeStruct((8, 128), F32)),
        in_specs=[_HBM] * (2 * n),
        out_specs=(_SEM, _SEM, *[_HBM] * (2 * n), pl.BlockSpec(memory_space=pltpu.VMEM)),
        input_output_aliases={i: 2 + i for i in range(2 * n)},
        scratch_shapes=[pltpu.SemaphoreType.DMA((n,))],
        compiler_params=pltpu.CompilerParams(has_side_effects=pltpu.SideEffectType.DATAFLOW_SIDE_EFFECTING),
    )(*[pltpu.with_memory_space_constraint(x, pltpu.HBM) for x in xs], *lands)
    return dict(send=res[0], recv=res[1], xs=list(res[2:2 + n]), lands=list(res[2 + n:2 + 2 * n]), token=res[2 + 2 * n])


def _exchange_wait(name, handle, after, scatter):
    n = len(handle['xs'])

    def body(*refs):
        x_refs, land_refs = refs[:n], refs[n:2 * n]
        send_sems, recv_sems = refs[2 * n], refs[2 * n + 1]
        _, pairs = _exchange_copies(x_refs, land_refs, send_sems, recv_sems, scatter)
        for out, arrival in pairs:
            out.wait_send()
            arrival.wait_recv()

    res = pl.pallas_call(
        body, name=name,
        out_shape=tuple(pltpu.HBM(a.shape, a.dtype) for a in handle['xs'] + handle['lands']),
        in_specs=[_HBM] * (2 * n) + [_SEM, _SEM, pl.BlockSpec(memory_space=pl.ANY)],
        out_specs=tuple([_HBM] * (2 * n)),
        input_output_aliases={i: i for i in range(2 * n)},
        compiler_params=pltpu.CompilerParams(has_side_effects=pltpu.SideEffectType.DATAFLOW_SIDE_EFFECTING),
    )(*handle['xs'], *handle['lands'], handle['send'], handle['recv'], after)
    return list(res[n:])


def _relu2(a):
    r = jnp.maximum(a, 0.0)
    return r * r


def _relu2_grad(acc, a):
    return acc * (2.0 * jnp.maximum(a, 0.0))


def _local_step(x0, tgt, mod, wcat, late_weights, send_grads, conv_w, conv_b, dt_bias, a_log, d_skip, ssm_norm_w, f_bias,
                attn_norm_w, ln1_g, ln1_b, ln2_g, ln2_b):
    ff_w = DFF // N_DEV
    s = x0.shape[0]
    tm = min(1024, s)
    ts = min(1024, s)
    sh1, sc1, g1, sh2, sc2, g2 = [mod[:, i * D:(i + 1) * D] for i in range(6)]
    zero = jnp.zeros((1, 128 - 2 * NH), F32)
    bias128 = jnp.concatenate([dt_bias, f_bias, zero], axis=1)
    alog128 = jnp.concatenate([a_log, jnp.zeros((1, 128 - NH), F32)], axis=1)
    dskip_x = jnp.repeat(d_skip, HD, axis=1)
    w_xs, w_bc, b_xs, b_bc = conv_w[:, :D], conv_w[:, D:], conv_b[:, :D], conv_b[:, D:]

    p = _mm_nn("in_proj", x0, wcat, tm=tm, tn=1152, tk=D, out_dtype=F32, pro=_modulate, aux=(sc1, sh1))
    xs_a, bc_a = _conv_fwd(p, w_xs, b_xs, w_bc, b_bc, s)
    y_ssd, states = _ssd_fwd(xs_a, bc_a, p, bias128, alog128, dskip_x, s)
    cum = _cum_fwd(p, bias128, s)
    att, lse = _attn_fwd(p, cum, s)
    wout, w1s, w2 = late_weights(lse)
    ymix = _mix_norm(y_ssd, p, att, ssm_norm_w, attn_norm_w, s)
    y = _mm_nn("out_proj", ymix, wout, tm=tm, tn=1024, tk=2 * D, out_dtype=F32)
    x1, h2 = _ln1(x0, y, g1, ln1_g, ln1_b, sc2, sh2, s)
    a1 = _mm_nn("ff_in", h2, w1s, tm=tm, tn=ff_w, tk=D, out_dtype=F32)
    ff = _mm_nn("ff_out", a1, w2, tm=tm, tn=1024, tk=1024, out_dtype=F32, pro=_relu2)
    du2, dff, sq_err, d_ln2_g, d_ln2_b, d_g2 = _ln2_loss(x1, ff, tgt, g2, ln2_g, ln2_b, s)

    da1 = _mm_nt("d_ff_hidden", [(dff, D, 0)], [(w2, D, 0)], n=DFF, tm=tm, tn=1024, out_dtype=BF16, epi=_relu2_grad,
                 epi_aux=(a1,))
    d_w2 = _mm_tn("d_w_ff_out", a1, dff, tm=1024, tn=1024, ts=ts, pro=_relu2)
    d_w1s = _mm_tn("d_w_ff_in", h2, da1, tm=1024, tn=ff_w, ts=ts, col_shards=True)
    dh2 = _mm_nt("d_ff_input", [(da1, ff_w, k) for k in range(N_DEV)], [(w1s, ff_w, k) for k in range(N_DEV)], n=D,
                 tm=min(512, s), tn=1024, out_dtype=F32)
    du1, dy, d_sc2, d_sh2, d_ln1_g, d_ln1_b, d_g1 = _ln1_bwd(dh2, du2, x0, y, g1, ln1_g, ln1_b, sc2, s)

    dmix = _mm_nt("d_mix", [(dy, D, 0)], [(wout, D, 0)], n=2 * D, tm=tm, tn=1024, out_dtype=F32)
    d_wout = _mm_tn("d_w_out", ymix, dy, tm=1024, tn=1024, ts=ts)
    sent = send_grads("late", [d_w1s, d_w2.reshape(N_DEV, -1, D), d_wout.reshape(N_DEV, -1, D)])
    dy_ssd, dz, datt, d_ssm_w, d_attn_w = _mix_norm_bwd(dmix, y_ssd, p, att, _after(ssm_norm_w, sent), attn_norm_w, s)
    dq, dk, dv, dcs, drs = _attn_bwd(p, cum, att, lse, datt, s)
    dxs_a, dbc_a, ddt_raw, d_alog, d_dskip = _ssd_bwd(dy_ssd, xs_a, bc_a, p, states, bias128, alog128, dskip_x, s)
    dcum = jnp.pad((drs - dcs)[:, :2, :].reshape(NH, s).T, ((0, 0), (NH, 128 - 2 * NH)))
    ddtf, _, d_bias = _cum_bwd(dcum, ddt_raw, p, bias128, s)
    dxs, dbc, d_wc_xs, d_bc_xs, d_wc_bc, d_bc_bc = _conv_bwd(dxs_a, dbc_a, p, w_xs, b_xs, w_bc, b_bc, s)

    segs = [(dz, OFF_Z, D), (dxs, OFF_XS, D), (dq, OFF_Q, D), (dk, OFF_K, D), (dv, OFF_V, D), (dbc, OFF_BC, 512),
            (ddtf, OFF_DTF, 128)]
    d_z, d_xs, d_q, d_k, d_v, d_bcw, d_dtf = [
        _mm_tn("d_w_in_%d" % i, x0, a, tm=1024, tn=min(w, 1024), ts=ts, pro=_modulate, aux=(sc1, sh1))
        for i, (a, _, w) in enumerate(segs)]
    d_w_in = dict(z=d_z, xs=d_xs, bc=d_bcw, dt=d_dtf[:, :NH], q=d_q, k=d_k, v=d_v, f=d_dtf[:, NH:2 * NH])
    sent = send_grads("in", [_shard_w_in_grad(d_w_in)])
    segs[-1] = (_after(ddtf, sent), OFF_DTF, 128)
    dh1 = _mm_nt("d_h1", [(a, w, 0) for a, _, w in segs], [(wcat, w, off // w) for _, off, w in segs], n=D,
                 tm=min(512, s), tn=1024, out_dtype=F32)
    grad_x, d_sc1, d_sh1 = _input_grad(dh1, du1, x0, sc1, s)

    return dict(
        loss=(0.5 / D) * jnp.sum(sq_err), grad_x=grad_x,
        d_mod=jnp.concatenate([d_sh1, d_sc1, d_g1, d_sh2, d_sc2, d_g2], axis=1),
        d_conv_w=jnp.concatenate([d_wc_xs[:4], d_wc_bc[:4]], axis=1), d_conv_b=jnp.concatenate([d_bc_xs, d_bc_bc], axis=1),
        d_ssm_norm_w=d_ssm_w, d_attn_norm_w=d_attn_w, d_ln1_g=d_ln1_g, d_ln1_b=d_ln1_b, d_ln2_g=d_ln2_g, d_ln2_b=d_ln2_b,
        d_gate_bias=d_bias, d_a_log=d_alog, d_d_skip=d_dskip)


W_IN_SEGS = [('z', W_Z, D), ('xs', W_XS, D), ('bc', W_BC, 512), ('dt', W_DT, NH), ('q', W_Q, D), ('k', W_K, D),
             ('v', W_V, D), ('f', W_F, NH)]
SHARD_W = IN_COLS // N_DEV


def _pack_w_in(shards):
    def cols(lo, hi):
        pieces = []
        while lo < hi:
            dev = lo // SHARD_W
            end = min(hi, (dev + 1) * SHARD_W)
            pieces.append(shards[dev][:, lo - dev * SHARD_W:end - dev * SHARD_W])
            lo = end
        return pieces

    seg = {n: cols(off, off + w) for n, off, w in W_IN_SEGS}
    pieces = seg['z'] + seg['xs'] + seg['q'] + seg['k'] + seg['v'] + seg['bc'] + seg['dt'] + seg['f']
    return jnp.concatenate(pieces + [jnp.zeros((D, 128 - 2 * NH), shards.dtype)], axis=1)


def _shard_w_in_grad(d_w_in):
    blocks = []
    for dev in range(N_DEV):
        lo, hi = dev * SHARD_W, (dev + 1) * SHARD_W
        pieces = [d_w_in[n][:, max(lo, off) - off:min(hi, off + w) - off] for n, off, w in W_IN_SEGS
                  if max(lo, off) < min(hi, off + w)]
        pieces.append(jnp.zeros((D, -SHARD_W % 128), pieces[0].dtype))
        blocks.append(jnp.concatenate(pieces, axis=1))
    return jnp.stack(blocks, axis=0)


WEIGHTS = ['w_ada', 'b_ada', 'w_in', 'conv_w', 'conv_b', 'dt_bias', 'a_log', 'd_skip', 'ssm_norm_w', 'f_bias',
           'attn_norm_w', 'w_out', 'ln1_g', 'ln1_b', 'w_ff_in', 'w_ff_out', 'ln2_g', 'ln2_b']
BIG = ['w_in', 'w_out', 'w_ff_in', 'w_ff_out']
SMALL_LAYOUT = [('b_ada', 0, 6 * D), ('conv_b', 12288, 1536), ('ssm_norm_w', 13824, D), ('attn_norm_w', 14848, D),
                ('ln1_g', 15872, D), ('ln1_b', 16896, D), ('ln2_g', 17920, D), ('ln2_b', 18944, D),
                ('dt_bias', 19968, NH), ('f_bias', 19968 + NH, NH), ('a_log', 20096, NH), ('d_skip', 20224, NH)]
SMALL_LOSS_LANE = 20352


def _pad_lanes(v, n=128):
    return jnp.pad(v, ((0, 0), (0, n - v.shape[1])))


def kernel(x, c, w_ada, b_ada, w_in, conv_w, conv_b, dt_bias, a_log, d_skip, ssm_norm_w, f_bias, attn_norm_w, w_out, ln1_g, ln1_b, w_ff_in, w_ff_out, ln2_g, ln2_b, loss_target, m_w_ada, m_b_ada, m_w_in, m_conv_w, m_conv_b, m_dt_bias, m_a_log, m_d_skip, m_ssm_norm_w, m_f_bias, m_attn_norm_w, m_w_out, m_ln1_g, m_ln1_b, m_w_ff_in, m_w_ff_out, m_ln2_g, m_ln2_b, v_w_ada, v_b_ada, v_w_in, v_conv_w, v_conv_b, v_dt_bias, v_a_log, v_d_skip, v_ssm_norm_w, v_f_bias, v_attn_norm_w, v_w_out, v_ln1_g, v_ln1_b, v_w_ff_in, v_w_ff_out, v_ln2_g, v_ln2_b):
    args = dict(locals())
    w = {n: args[n] for n in WEIGHTS}
    m = {n: args['m_' + n] for n in WEIGHTS}
    v = {n: args['v_' + n] for n in WEIGHTS}
    me = 4 * lax.axis_index("x") + 2 * lax.axis_index("y") + lax.axis_index("c")
    ada_cols = 6 * D // N_DEV
    conv_cols = conv_w.shape[2]

    c_all, conv_all = _exchange("gather_cond", [c, conv_w[0]], False)
    c_all = c_all.reshape(N_DEV, D)
    conv_w_full = conv_all.transpose(1, 0, 2).reshape(4, N_DEV * conv_cols)
    b_shard = lax.dynamic_slice(b_ada, (0, me * ada_cols), (1, ada_cols))
    mod_all, = _exchange("gather_mod", [_ada_mod(c_all, w_ada[0], b_shard)], False)
    mod = lax.dynamic_index_in_dim(mod_all, me, axis=1, keepdims=False).reshape(1, 6 * D)

    win_s = _gather_two_level("gather_w_in", _after(w_in[0].astype(BF16), mod * 0))
    first_done = win_s[0, 0:1, 0:1] * 0
    rest = _exchange_async("gather_rest", [_after(w[n][0].astype(BF16), first_done) for n in BIG[1:]], False, 1)

    def late_weights(after):
        wout_s, w1s, w2_s = rest()
        return wout_s.reshape(2 * D, D), w1s, w2_s.reshape(DFF, D)

    sends = {}

    def send_grads(tag, blocks):
        sends[tag] = _exchange_async("scatter_" + tag, blocks, True, {'late': 2, 'in': 3}[tag])
        return sum(b.reshape(-1)[0].astype(F32) * 0 for b in blocks)

    out = _local_step(x[0], loss_target[0], mod, _pack_w_in(win_s), late_weights, send_grads,
                      conv_w_full, conv_b, dt_bias, a_log, d_skip, ssm_norm_w, f_bias, attn_norm_w, ln1_g, ln1_b, ln2_g, ln2_b)

    small = jnp.concatenate(
        [out['d_mod'], out['d_conv_w'].reshape(1, -1), out['d_conv_b'], out['d_ssm_norm_w'], out['d_attn_norm_w'],
         out['d_ln1_g'], out['d_ln1_b'], out['d_ln2_g'], out['d_ln2_b'], out['d_gate_bias'], out['d_a_log'],
         out['d_d_skip'], _pad_lanes(out['loss'].reshape(1, 1))], axis=1)
    small_landed = _exchange_async("gather_small", [small], False, 4)
    (g_ff_in, g_ff_out, g_out), (g_in,) = sends['late'](), sends['in']()
    g_parts = dict(w_ff_in=g_ff_in, w_ff_out=g_ff_out, w_out=g_out, w_in=g_in)
    big = {n: _adamw("adamw_" + n, w[n][0], g_parts[n], m[n][0], v[n][0], tr=256, slots=True) for n in BIG}
    big_done = sum(big[n][1][0:1, 0:1] * 0 for n in BIG)
    small_all = _after(small_landed()[0], big_done)
    ssum, small_res = _small_update(small_all, SMALL_LAYOUT, w, m, v)
    dmod_all = small_all[:, 0, :6 * D]
    g_w_ada = _ada_grad(c_all, lax.dynamic_slice(dmod_all, (0, me * ada_cols), (N_DEV, ada_cols)))
    ada = _adamw("adamw_ada", w_ada[0], g_w_ada, m_w_ada[0], v_w_ada[0], tr=256, slots=False)
    g_conv_w = lax.dynamic_slice(ssum[:, 6 * D:6 * D + 4 * N_DEV * conv_cols].reshape(4, N_DEV * conv_cols),
                                 (0, me * conv_cols), (4, conv_cols))
    conv = _adamw("adamw_conv_w", conv_w[0], g_conv_w, m_conv_w[0], v_conv_w[0], tr=4, slots=False)

    results = []
    for k in range(4):
        vals = {n: small_res[n][k] for n in small_res}
        vals['w_ada'], vals['conv_w'] = ada[k][None], conv[k][None]
        for n in BIG:
            vals[n] = big[n][k][None]
        results.append(vals)
    return (ssum[0, SMALL_LOSS_LANE], out['grad_x'][None], *[res[n] for res in results for n in WEIGHTS])
```

```python
import functools

import jax
import jax.numpy as jnp
from jax import lax
from jax.experimental import pallas as pl
from jax.experimental.pallas import tpu as pltpu
from jax.experimental.pallas import tpu_sc as plsc

F32, BF16 = jnp.float32, jnp.bfloat16

N_DEV = 8
D = 1024
NH, HD = 16, 64
NSTATE = 128
CHUNK = 128
HG = 8
DFF = 4096
ALPHA = 2.0 ** 0.25
EPS = 1e-5
ATT_SCALE = HD ** -0.5

OFF_Z, OFF_XS, OFF_Q, OFF_K, OFF_V, OFF_BC, OFF_DTF = 0, 1024, 2048, 3072, 4096, 5120, 5632
PCOLS = 5760
W_Z, W_XS, W_BC, W_DT, W_Q, W_K, W_V, W_F = 0, 1024, 2048, 2560, 2576, 3600, 4624, 5648
IN_COLS = 5664

ADAM_LR, ADAM_B1, ADAM_B2, ADAM_EPS, ADAM_WD, ADAM_STEP = 0.001, 0.9, 0.999, 1e-08, 0.01, 10

VMEM_LIMIT = 56 << 20

NN = (((1,), (0,)), ((), ()))
NT = (((1,), (1,)), ((), ()))
TN = (((0,), (0,)), ((), ()))


def _dot(a, b, dims=NN):
    return lax.dot_general(a, b, dims, preferred_element_type=F32)


def _bdot(a, b, dims=NN):
    return _dot(a.astype(BF16), b.astype(BF16), dims)


def _split3(v, terms=3):
    parts, rest = [], v
    for _ in range(terms):
        p = rest.astype(BF16)
        parts.append(p)
        rest = rest - p.astype(F32)
    return parts


def _sel_left(m01, v):
    return sum(_dot(m01, p) for p in _split3(v))


def _sel_right(v, m01, dims=NN, terms=3):
    return sum(_dot(p, m01, dims) for p in _split3(v, terms))


def _iota(shape, dim):
    return lax.broadcasted_iota(jnp.int32, shape, dim)


def _tri_lower(n):
    return (_iota((n, n), 1) <= _iota((n, n), 0)).astype(BF16)


def _tri_upper(n):
    return (_iota((n, n), 1) >= _iota((n, n), 0)).astype(BF16)


def _head_expand():
    return (lax.shift_right_logical(_iota((128, D), 1), 6) == _iota((128, D), 0)).astype(BF16)


def _head_reduce():
    return (lax.shift_right_logical(_iota((D, 128), 0), 6) == _iota((D, 128), 1)).astype(BF16)


def _sigmoid(x):
    return 1.0 / (1.0 + jnp.exp(-x))


def _silu(x):
    return x * _sigmoid(x)


def _dsilu(x):
    s = _sigmoid(x)
    return s * (1.0 + x * (1.0 - s))


def _softplus(x):
    return jnp.maximum(x, 0.0) + jnp.log(1.0 + jnp.exp(-jnp.abs(x)))


def _log_sigmoid(x):
    return jnp.minimum(x, 0.0) - jnp.log(1.0 + jnp.exp(-jnp.abs(x)))


def _params(sem):
    return pltpu.CompilerParams(dimension_semantics=sem, vmem_limit_bytes=VMEM_LIMIT)


def _mm_nn(name, a, b, *, tm, tn, tk, out_dtype, pro=None, aux=()):
    m, k_all = a.shape
    b_sharded = b.ndim == 3
    n = b.shape[0] * b.shape[2] if b_sharded else b.shape[1]
    assert not b_sharded or tn == b.shape[2]
    nk = k_all // tk
    n_aux = len(aux)
    b_spec = (pl.BlockSpec((None, tk, tn), lambda i, j, k: (j, k, 0)) if b_sharded
              else pl.BlockSpec((tk, tn), lambda i, j, k: (k, j)))

    def body(a_ref, b_ref, *rest):
        aux_refs, o_ref = rest[:n_aux], rest[n_aux]
        at = a_ref[...]
        if pro is not None:
            at = pro(at, *[r[...] for r in aux_refs])
        part = _bdot(at, b_ref[...])
        if nk == 1:
            o_ref[...] = part.astype(out_dtype)
            return
        acc_ref = rest[n_aux + 1]
        kk = pl.program_id(2)

        @pl.when(kk == 0)
        def _():
            acc_ref[...] = part

        @pl.when(kk > 0)
        def _():
            acc_ref[...] += part

        @pl.when(kk == nk - 1)
        def _():
            o_ref[...] = acc_ref[...].astype(out_dtype)

    return pl.pallas_call(
        body, name=name,
        grid=(m // tm, n // tn, nk),
        in_specs=[pl.BlockSpec((tm, tk), lambda i, j, k: (i, k)), b_spec]
        + [pl.BlockSpec((1, tk), lambda i, j, k: (0, k)) for _ in aux],
        out_specs=pl.BlockSpec((tm, tn), lambda i, j, k: (i, j)),
        out_shape=jax.ShapeDtypeStruct((m, n), out_dtype),
        scratch_shapes=[] if nk == 1 else [pltpu.VMEM((tm, tn), F32)],
        compiler_params=_params(("parallel", "parallel", "arbitrary")),
    )(a, b, *aux)


def _mm_nt(name, a_list, b_list, *, n, tm, tn, out_dtype, epi=None, epi_aux=()):
    m = a_list[0][0].shape[0]
    n_op = len(a_list)
    n_epi = len(epi_aux)

    def body(*refs):
        a_refs, b_refs = refs[:n_op], refs[n_op:2 * n_op]
        e_refs, o_ref = refs[2 * n_op:2 * n_op + n_epi], refs[2 * n_op + n_epi]
        acc = None
        for a_ref, b_ref in zip(a_refs, b_refs):
            part = _bdot(a_ref[...], b_ref[...], NT)
            acc = part if acc is None else acc + part
        if epi is not None:
            acc = epi(acc, *[r[...] for r in e_refs])
        o_ref[...] = acc.astype(out_dtype)

    in_specs = [pl.BlockSpec((tm, w), functools.partial(lambda i, j, cb: (i, cb), cb=cb)) for (_, w, cb) in a_list]
    for (b, w, cb) in b_list:
        if b.ndim == 3:
            in_specs.append(pl.BlockSpec((None, tn, w), functools.partial(lambda i, j, cb: (cb, j, 0), cb=cb)))
        else:
            in_specs.append(pl.BlockSpec((tn, w), functools.partial(lambda i, j, cb: (j, cb), cb=cb)))
    in_specs += [pl.BlockSpec((tm, tn), lambda i, j: (i, j)) for _ in epi_aux]
    return pl.pallas_call(
        body, name=name,
        grid=(m // tm, n // tn),
        in_specs=in_specs,
        out_specs=pl.BlockSpec((tm, tn), lambda i, j: (i, j)),
        out_shape=jax.ShapeDtypeStruct((m, n), out_dtype),
        compiler_params=_params(("parallel", "parallel")),
    )(*[a for (a, _, _) in a_list], *[b for (b, _, _) in b_list], *epi_aux)


def _mm_tn(name, a, b, *, tm, tn, ts, pro=None, aux=(), col_shards=False):
    s_all, ka = a.shape
    nb = b.shape[1]
    n_aux = len(aux)
    ns = s_all // ts
    assert not col_shards or tn == nb // N_DEV

    def body(a_ref, b_ref, *rest):
        aux_refs, o_ref, acc_ref = rest[:n_aux], rest[n_aux], rest[n_aux + 1]
        at = a_ref[...]
        if pro is not None:
            at = pro(at, *[r[...] for r in aux_refs])
        part = _bdot(at, b_ref[...], TN)
        ss = pl.program_id(2)

        @pl.when(ss == 0)
        def _():
            acc_ref[...] = part

        @pl.when(ss > 0)
        def _():
            acc_ref[...] += part

        @pl.when(ss == ns - 1)
        def _():
            o_ref[...] = acc_ref[...].astype(BF16)

    if col_shards:
        out_spec = pl.BlockSpec((None, tm, tn), lambda i, j, s: (j, i, 0))
        out_shape = jax.ShapeDtypeStruct((N_DEV, ka, tn), BF16)
    else:
        out_spec = pl.BlockSpec((tm, tn), lambda i, j, s: (i, j))
        out_shape = jax.ShapeDtypeStruct((ka, nb), BF16)
    return pl.pallas_call(
        body, name=name,
        grid=(ka // tm, nb // tn, ns),
        in_specs=[pl.BlockSpec((ts, tm), lambda i, j, s: (s, i)),
                  pl.BlockSpec((ts, tn), lambda i, j, s: (s, j))]
        + [pl.BlockSpec((1, tm), lambda i, j, s: (0, i)) for _ in aux],
        out_specs=out_spec, out_shape=out_shape,
        scratch_shapes=[pltpu.VMEM((tm, tn), F32)],
        compiler_params=_params(("parallel", "parallel", "arbitrary")),
    )(a, b, *aux)


def _rowk(name, fn, n_rows, tr, rows, fulls, outs, accs, reverse=False):
    n = n_rows // tr
    n_row, n_full, n_out, n_acc = len(rows), len(fulls), len(outs), len(accs)

    def pos(i):
        return (n - 1 - i) if reverse else i

    def body(*refs):
        row_refs = refs[:n_row]
        full_refs = refs[n_row:n_row + n_full]
        out_refs = refs[n_row + n_full:n_row + n_full + n_out]
        acc_refs = refs[n_row + n_full + n_out:]
        i = pl.program_id(0)

        @pl.when(i == 0)
        def _():
            for r in acc_refs:
                r[...] = jnp.zeros(r.shape, r.dtype)

        res = fn(pos(i), *[r[...] for r in row_refs], *[r[...] for r in full_refs], *[r[...] for r in acc_refs])
        for r, v in zip(out_refs + acc_refs, res):
            r[...] = v.astype(r.dtype)

    def row_map(i, cb, shift):
        return (jnp.clip(pos(i) + shift, 0, n - 1), cb)

    def halo_map(i, cb, shift):
        tile = jnp.clip(pos(i) + shift, 0, n - 1)
        return (tile * (tr // 8) + (tr // 8 - 1 if shift < 0 else 0), cb)

    in_specs = [pl.BlockSpec((tr, w), functools.partial(row_map, cb=cb, shift=sh)) if sh == 0 else
                pl.BlockSpec((8, w), functools.partial(halo_map, cb=cb, shift=sh)) for (_, w, cb, sh) in rows]
    in_specs += [pl.BlockSpec(f.shape, functools.partial(lambda i, nd: (0,) * nd, nd=f.ndim)) for f in fulls]
    out_specs = [pl.BlockSpec((tr, w), lambda i: (pos(i), 0)) for (w, _) in outs]
    out_specs += [pl.BlockSpec((r, w), lambda i: (0, 0)) for (r, w) in accs]
    out_shape = [jax.ShapeDtypeStruct((n_rows, w), dt) for (w, dt) in outs]
    out_shape += [jax.ShapeDtypeStruct((r, w), F32) for (r, w) in accs]
    return pl.pallas_call(
        body, name=name, grid=(n,), in_specs=in_specs, out_specs=out_specs, out_shape=out_shape,
        compiler_params=_params(("arbitrary",)),
    )(*[a for (a, _, _, _) in rows], *fulls)


def _colsum(x):
    return jnp.sum(x, axis=0, keepdims=True)


def _mean(x):
    return jnp.mean(x, axis=-1, keepdims=True)


def _modulate(x, sc, sh):
    return x * (1.0 + sc) + sh


def _shift_down(cur, prev8, j):
    tr = cur.shape[0]
    row8 = _iota(prev8.shape, 0)
    head = jnp.where(row8 < j, pltpu.roll(prev8, j, 0), pltpu.roll(cur[0:8], j, 0))
    return head if tr == 8 else jnp.concatenate([head, pltpu.roll(cur, j, 0)[8:]], axis=0)


def _shift_up(cur, next8, j):
    tr = cur.shape[0]
    row8 = _iota(next8.shape, 0)
    tail = jnp.where(row8 < 8 - j, pltpu.roll(cur[tr - 8:], 8 - j, 0), pltpu.roll(next8, 8 - j, 0))
    return jnp.concatenate([pltpu.roll(cur, tr - j, 0)[:tr - 8], tail], axis=0)


def _conv(cur, prev, w, b):
    out = cur * w[3:4] + b
    for j in (1, 2, 3):
        out = out + _shift_down(cur, prev, j) * w[3 - j:4 - j]
    return out


def _conv_fwd(p, w_xs, b_xs, w_bc, b_bc, s):
    def fn(pos, xs, xs_prev, bc, bc_prev, w_xs, b_xs, w_bc, b_bc):
        first = pos == 0
        xs_prev = jnp.where(first, 0.0, xs_prev)
        bc_prev = jnp.where(first, 0.0, bc_prev)
        return _silu(_conv(xs, xs_prev, w_xs, b_xs)), _silu(_conv(bc, bc_prev, w_bc, b_bc))

    return _rowk("conv_fwd", fn, s, 256,
                 [(p, D, OFF_XS // D, 0), (p, D, OFF_XS // D, -1), (p, 512, OFF_BC // 512, 0), (p, 512, OFF_BC // 512, -1)],
                 [w_xs, b_xs, w_bc, b_bc], [(D, F32), (512, F32)], [])


def _conv_bwd(dxs_a, dbc_a, p, w_xs, b_xs, w_bc, b_bc, s):
    tr = 256
    n = s // tr

    def fn(pos, da1, da1n, x1, x1p, x1n, da2, da2n, x2, x2p, x2n, w1, b1, w2, b2, aw1, ab1, aw2, ab2):
        dx1, dw1, db1 = _conv_bwd_fn(pos, n, da1, da1n, x1, x1p, x1n, w1, b1)
        dx2, dw2, db2 = _conv_bwd_fn(pos, n, da2, da2n, x2, x2p, x2n, w2, b2)
        return dx1, dx2, aw1 + dw1, ab1 + db1, aw2 + dw2, ab2 + db2

    cx, cb = OFF_XS // D, OFF_BC // 512
    return _rowk("conv_bwd", fn, s, tr,
                 [(dxs_a, D, 0, 0), (dxs_a, D, 0, 1), (p, D, cx, 0), (p, D, cx, -1), (p, D, cx, 1),
                  (dbc_a, 512, 0, 0), (dbc_a, 512, 0, 1), (p, 512, cb, 0), (p, 512, cb, -1), (p, 512, cb, 1)],
                 [w_xs, b_xs, w_bc, b_bc], [(D, BF16), (512, BF16)], [(8, D), (1, D), (8, 512), (1, 512)])


def _conv_bwd_fn(pos, n, da, da_next, x, x_prev, x_next, w, b):
    first, last = pos == 0, pos == n - 1
    x_prev = jnp.where(first, 0.0, x_prev)
    shifted = {j: _shift_down(x, x_prev, j) for j in (1, 2, 3)}
    conv = x * w[3:4] + b
    for j in (1, 2, 3):
        conv = conv + shifted[j] * w[3 - j:4 - j]
    dc = da * _dsilu(conv)
    dc_next = jnp.where(last, 0.0, da_next * _dsilu(_conv(x_next, x[x.shape[0] - 8:], w, b)))
    dx = dc * w[3:4]
    dws = [None] * 4
    dws[3] = _colsum(dc * x)
    for j in (1, 2, 3):
        dx = dx + _shift_up(dc, dc_next, j) * w[3 - j:4 - j]
        dws[3 - j] = _colsum(dc * shifted[j])
    row = _iota((8, x.shape[1]), 0)
    dw = jnp.zeros((8, x.shape[1]), F32)
    for k in range(4):
        dw = jnp.where(row == k, dws[k], dw)
    return dx, dw, _colsum(dc)


def _ssd_gates(dtf, bias, a_log):
    lane = _iota(dtf.shape, 1)
    head = lane < NH
    dt = jnp.where(head, _softplus(dtf + bias), 0.0)
    a_neg = jnp.where(_iota(a_log.shape, 1) < NH, -jnp.exp(a_log), 0.0)
    a = dt * a_neg
    cs = _sel_left(_tri_lower(CHUNK), a)
    return dt, a_neg, cs


def _decay_mask(cs_ref, cst_ref, h):
    diff = cs_ref[:, h:h + 1] - cst_ref[h:h + 1, :]
    low = _iota((CHUNK, CHUNK), 1) <= _iota((CHUNK, CHUNK), 0)
    return jnp.where(low, jnp.exp(jnp.minimum(diff, 0.0)), 0.0)


def _ssd_fwd(xs_a, bc_a, p, bias128, alog128, dskip_x, s):
    nc = s // CHUNK
    t = CHUNK

    def body(xs_ref, bc_ref, dtf_ref, bias_ref, alog_ref, dsk_ref, y_ref, st_ref,
             state, x_sc, xw_sc, cs_sc, cst_sc, yd_sc):
        c = pl.program_id(0)

        @pl.when(c == 0)
        def _():
            state[...] = jnp.zeros(state.shape, F32)

        dt, _, cs = _ssd_gates(dtf_ref[...], bias_ref[...], alog_ref[...])
        cs_sc[...] = cs
        cst_sc[...] = cs.T
        cs_last = cs[t - 1:t, :]
        expand = _head_expand()
        ex = _sel_right(jnp.concatenate([dt, jnp.exp(cs), jnp.exp(cs_last - cs)], axis=0), expand, terms=2)
        dt_x, eo_x, we_x = ex[0:t], ex[t:2 * t], ex[2 * t:3 * t]
        g_x = _sel_right(jnp.broadcast_to(jnp.exp(cs_last), (8, 128)), expand)[0:1]
        xs = xs_ref[...]
        x = xs * dt_x
        x_sc[...] = x.astype(BF16)
        xw_sc[...] = (x * we_x).astype(BF16)
        prev = state[...]
        st_ref[0] = prev
        prev_b = prev.astype(BF16)
        for g in range(2):
            cols = slice(g * 512, (g + 1) * 512)
            b_g = bc_ref[:, g * 128:(g + 1) * 128].astype(BF16)
            c_g = bc_ref[:, 256 + g * 128:256 + (g + 1) * 128].astype(BF16)
            gmat = _dot(c_g, b_g, NT)
            y_off = _dot(c_g, prev_b[:, cols]) * eo_x[:, cols]
            s_loc = _dot(b_g, xw_sc[:, cols], TN)
            state[:, cols] = g_x[:, cols] * prev[:, cols] + s_loc
            for e in range(HG):
                h = g * HG + e
                m = gmat * _decay_mask(cs_sc, cst_sc, h)
                yd_sc[:, h * HD:(h + 1) * HD] = _dot(m.astype(BF16), x_sc[:, h * HD:(h + 1) * HD])
            y_ref[:, cols] = yd_sc[:, cols] + y_off + dsk_ref[:, cols] * xs[:, cols]

    return pl.pallas_call(
        body, name="ssd_fwd", grid=(nc,),
        in_specs=[pl.BlockSpec((t, D), lambda c: (c, 0)),
                  pl.BlockSpec((t, 512), lambda c: (c, 0)),
                  pl.BlockSpec((t, 128), lambda c: (c, OFF_DTF // 128)),
                  pl.BlockSpec((1, 128), lambda c: (0, 0)),
                  pl.BlockSpec((1, 128), lambda c: (0, 0)),
                  pl.BlockSpec((1, D), lambda c: (0, 0))],
        out_specs=[pl.BlockSpec((t, D), lambda c: (c, 0)),
                   pl.BlockSpec((1, NSTATE, D), lambda c: (c, 0, 0))],
        out_shape=[jax.ShapeDtypeStruct((s, D), F32), jax.ShapeDtypeStruct((nc, NSTATE, D), F32)],
        scratch_shapes=[pltpu.VMEM((NSTATE, D), F32), pltpu.VMEM((t, D), BF16), pltpu.VMEM((t, D), BF16),
                        pltpu.VMEM((t, 128), F32), pltpu.VMEM((128, t), F32), pltpu.VMEM((t, D), F32)],
        compiler_params=_params(("arbitrary",)),
    )(xs_a, bc_a, p, bias128, alog128, dskip_x)


def _ssd_bwd(dy, xs_a, bc_a, p, states, bias128, alog128, dskip_x, s):
    nc = s // CHUNK
    t = CHUNK

    def body(dy_ref, xs_ref, bc_ref, dtf_ref, st_ref, bias_ref, alog_ref, dsk_ref,
             dxs_ref, dbc_ref, ddt_ref, dalog_ref, dskip_ref,
             dstate, x_sc, dy_sc, dx_sc, deo_sc, dwe_sc, cs_sc, cst_sc, dcol_sc, drow_sc):
        i = pl.program_id(0)

        @pl.when(i == 0)
        def _():
            dstate[...] = jnp.zeros(dstate.shape, F32)
            dalog_ref[...] = jnp.zeros(dalog_ref.shape, F32)
            dskip_ref[...] = jnp.zeros(dskip_ref.shape, F32)

        dtf = dtf_ref[...]
        dt, a_neg, cs = _ssd_gates(dtf, bias_ref[...], alog_ref[...])
        cs_sc[...] = cs
        cst_sc[...] = cs.T
        cs_last = cs[t - 1:t, :]
        eo, we, g_end = jnp.exp(cs), jnp.exp(cs_last - cs), jnp.exp(cs_last)
        expand, reduce = _head_expand(), _head_reduce()
        ex = _sel_right(jnp.concatenate([dt, eo, we], axis=0), expand, terms=2)
        dt_x, eo_x, we_x = ex[0:t], ex[t:2 * t], ex[2 * t:3 * t]
        g_x = _sel_right(jnp.broadcast_to(g_end, (8, 128)), expand)[0:1]
        xs = xs_ref[...]
        dyv = dy_ref[...]
        x = xs * dt_x
        x_sc[...] = x.astype(BF16)
        dy_sc[...] = dyv.astype(BF16)
        dyo_b = (dyv * eo_x).astype(BF16)
        xw_b = (x * we_x).astype(BF16)
        prev = st_ref[0]
        prev_b = prev.astype(BF16)
        dnext = dstate[...]
        dnext_b = dnext.astype(BF16)
        dcol_sc[...] = jnp.zeros(dcol_sc.shape, F32)
        drow_sc[...] = jnp.zeros(drow_sc.shape, F32)
        lane_row = _iota((1, 128), 1)
        sub_col = _iota((128, 1), 0)
        for g in range(2):
            cols = slice(g * 512, (g + 1) * 512)
            b_g = bc_ref[:, g * 128:(g + 1) * 128].astype(BF16)
            c_g = bc_ref[:, 256 + g * 128:256 + (g + 1) * 128].astype(BF16)
            gmat = _dot(c_g, b_g, NT)
            b_ds = _dot(b_g, dnext_b[:, cols])
            c_s = _dot(c_g, prev_b[:, cols])
            dx_sc[:, cols] = b_ds * we_x[:, cols]
            deo_sc[:, cols] = dyv[:, cols] * c_s
            dwe_sc[:, cols] = b_ds * x[:, cols]
            db = _dot(xw_b[:, cols], dnext_b[:, cols], NT)
            dc = _dot(dyo_b[:, cols], prev_b[:, cols], NT)
            dstate[:, cols] = g_x[:, cols] * dnext[:, cols] + _dot(c_g, dyo_b[:, cols], TN)
            dg = jnp.zeros((t, t), F32)
            for e in range(HG):
                h = g * HG + e
                hc = slice(h * HD, (h + 1) * HD)
                lmat = _decay_mask(cs_sc, cst_sc, h)
                m = gmat * lmat
                dx_sc[:, hc] += _dot(m.astype(BF16), dy_sc[:, hc], TN)
                dm = _dot(dy_sc[:, hc], x_sc[:, hc], NT)
                dg = dg + dm * lmat
                qm = dm * m
                dcol_sc[...] += jnp.sum(qm, axis=1, keepdims=True) * (lane_row == h).astype(F32)
                drow_sc[...] += (sub_col == h).astype(F32) * jnp.sum(qm, axis=0, keepdims=True)
            dg_b = dg.astype(BF16)
            dbc_ref[:, g * 128:(g + 1) * 128] = db + _dot(dg_b, c_g, TN)
            dbc_ref[:, 256 + g * 128:256 + (g + 1) * 128] = dc + _dot(dg_b, b_g)
        d_eo = _sel_right(deo_sc[...], reduce, terms=2)
        d_we = _sel_right(dwe_sc[...], reduce, terms=2)
        d_gend = _sel_right(jnp.broadcast_to(_colsum(dnext * prev), (8, D)), reduce)[0:1]
        d_cs = dcol_sc[...] - drow_sc[...].T + d_eo * eo - d_we * we
        extra = _colsum(d_we * we) + d_gend * g_end
        d_cs = d_cs + jnp.where(_iota((t, 128), 0) == t - 1, extra, 0.0)
        da = _sel_left(_tri_upper(t), d_cs)
        dx = dx_sc[...]
        ddt = _sel_right(dx * xs, reduce, terms=2) + da * a_neg
        dxs_ref[...] = dx * dt_x + dsk_ref[...] * dyv
        ddt_ref[...] = jnp.where(_iota((t, 128), 1) < NH, ddt * _sigmoid(dtf + bias_ref[...]), 0.0)
        dalog_ref[...] += _colsum(da * dt) * a_neg
        dskip_ref[...] += _sel_right(jnp.broadcast_to(_colsum(dyv * xs), (8, D)), reduce)[0:1]

    rev = lambda i: nc - 1 - i
    return pl.pallas_call(
        body, name="ssd_bwd", grid=(nc,),
        in_specs=[pl.BlockSpec((t, D), lambda i: (rev(i), 0)),
                  pl.BlockSpec((t, D), lambda i: (rev(i), 0)),
                  pl.BlockSpec((t, 512), lambda i: (rev(i), 0)),
                  pl.BlockSpec((t, 128), lambda i: (rev(i), OFF_DTF // 128)),
                  pl.BlockSpec((1, NSTATE, D), lambda i: (rev(i), 0, 0)),
                  pl.BlockSpec((1, 128), lambda i: (0, 0)),
                  pl.BlockSpec((1, 128), lambda i: (0, 0)),
                  pl.BlockSpec((1, D), lambda i: (0, 0))],
        out_specs=[pl.BlockSpec((t, D), lambda i: (rev(i), 0)),
                   pl.BlockSpec((t, 512), lambda i: (rev(i), 0)),
                   pl.BlockSpec((t, 128), lambda i: (rev(i), 0)),
                   pl.BlockSpec((1, 128), lambda i: (0, 0)),
                   pl.BlockSpec((1, 128), lambda i: (0, 0))],
        out_shape=[jax.ShapeDtypeStruct((s, D), F32), jax.ShapeDtypeStruct((s, 512), F32),
                   jax.ShapeDtypeStruct((s, 128), F32), jax.ShapeDtypeStruct((1, 128), F32),
                   jax.ShapeDtypeStruct((1, 128), F32)],
        scratch_shapes=[pltpu.VMEM((NSTATE, D), F32), pltpu.VMEM((t, D), BF16), pltpu.VMEM((t, D), BF16),
                        pltpu.VMEM((t, D), F32), pltpu.VMEM((t, D), F32), pltpu.VMEM((t, D), F32),
                        pltpu.VMEM((t, 128), F32), pltpu.VMEM((128, t), F32),
                        pltpu.VMEM((t, 128), F32), pltpu.VMEM((128, t), F32)],
        compiler_params=_params(("arbitrary",)),
    )(dy, xs_a, bc_a, p, states, bias128, alog128, dskip_x)


def _gate_lanes(shape):
    lane = _iota(shape, 1)
    return (lane >= NH) & (lane < 2 * NH)


def _cum_fwd(p, bias128, s):
    tr = min(512, s)

    def body(dtf_ref, bias_ref, o_ref, carry):
        @pl.when(pl.program_id(0) == 0)
        def _():
            carry[...] = jnp.zeros(carry.shape, F32)

        lf = jnp.where(_gate_lanes((tr, 128)), _log_sigmoid(dtf_ref[...] + bias_ref[...]), 0.0)
        cum = _sel_left(_tri_lower(tr), lf) + carry[...]
        carry[...] = cum[tr - 1:tr, :]
        o_ref[...] = cum

    return pl.pallas_call(
        body, name="cum_fwd", grid=(s // tr,),
        in_specs=[pl.BlockSpec((tr, 128), lambda i: (i, OFF_DTF // 128)), pl.BlockSpec((1, 128), lambda i: (0, 0))],
        out_specs=pl.BlockSpec((tr, 128), lambda i: (i, 0)),
        out_shape=jax.ShapeDtypeStruct((s, 128), F32),
        scratch_shapes=[pltpu.VMEM((1, 128), F32)],
        compiler_params=_params(("arbitrary",)),
    )(p, bias128)


def _cum_bwd(dcum, ddt_raw, p, bias128, s):
    tr = min(512, s)

    def fn(pos, dcum, ddt, dtf, bias, carry, acc):
        suffix = _sel_left(_tri_upper(tr), dcum) + carry
        dfr = jnp.where(_gate_lanes((tr, 128)), suffix * _sigmoid(-(dtf + bias)), 0.0)
        out = ddt + dfr
        return out, suffix[0:1, :], acc + _colsum(out)

    return _rowk("cum_bwd", fn, s, tr, [(dcum, 128, 0, 0), (ddt_raw, 128, 0, 0), (p, 128, OFF_DTF // 128, 0)],
                 [bias128], [(128, BF16)], [(1, 128), (1, 128)], reverse=True)


ATT_BLOCK = 512
ATT_STRIP = 32


def _head_part(shape, h, dim):
    i = _iota(shape, dim)
    return (i >= h * HD) & (i < (h + 1) * HD)


def _k_augmented(k_blk, cum_blk, j, h):
    tk = k_blk.shape[0]
    lane = _iota((tk, 128), 1)
    col = jnp.sum(jnp.where(lane == NH + 2 * j + h, cum_blk, 0.0), axis=1, keepdims=True)
    c0, c1, c2 = [c.astype(F32) for c in _split3(-col)]
    k_h = k_blk if h == 0 else pltpu.roll(k_blk, HD, 1)
    aug = jnp.where(lane == HD, c0, jnp.where(lane == HD + 1, c1, jnp.where(lane == HD + 2, c2, 0.0)))
    return jnp.where(lane < HD, k_h, aug).astype(BF16)


def _q_augmented_t(q_blk):
    tq = q_blk.shape[0]
    q_t = (q_blk * ATT_SCALE).T.astype(BF16)
    ones = (_iota((HD, tq), 0) < 3).astype(BF16)
    return [jnp.concatenate([q_t[h * HD:(h + 1) * HD], ones], axis=0) for h in range(2)]


def _rows01(r0, r1):
    sub = _iota((8, r0.shape[1]), 0)
    return jnp.where(sub == 0, r0, jnp.where(sub == 1, r1, 0.0))


def _fold8(x, op, cur):
    for g in range(x.shape[0] // 8):
        cur = op(cur, x[8 * g:8 * (g + 1), :])
    return cur


def _attn_fwd(p, cum, s):
    t = min(ATT_BLOCK, s)
    nq = s // t
    r = ATT_STRIP

    def body(q_ref, k_ref, v_ref, c_ref, o_ref, lse_ref, kaug_sc, vt_sc, s0_sc, s1_sc, p0_sc, p1_sc, m_sc, l_sc, acc_sc):
        j, qi = pl.program_id(0), pl.program_id(1)
        s_sc, p_sc = (s0_sc, s1_sc), (p0_sc, p1_sc)

        @pl.when(qi == 0)
        def _():
            for c in range(nq):
                rows = slice(c * t, (c + 1) * t)
                k_blk, vt = k_ref[rows, :], v_ref[rows, :].T
                for h in range(2):
                    kaug_sc[h, rows, :] = _k_augmented(k_blk, c_ref[rows, :], j, h)
                    vt_sc[h, :, rows] = vt[h * HD:(h + 1) * HD].astype(BF16)

        qaug_t = _q_augmented_t(q_ref[...])
        m_sc[...] = jnp.full(m_sc.shape, -1e30, F32)
        l_sc[...] = jnp.zeros(l_sc.shape, F32)
        acc_sc[...] = jnp.zeros(acc_sc.shape, F32)
        top = _iota((128, t), 0) < HD

        def logits(kb, buf):
            kv = pl.ds(pl.multiple_of(kb * t, t), t)
            for h in range(2):
                s_sc[buf][h] = _dot(kaug_sc[h, kv, :], qaug_t[h])

        def softmax(buf, diagonal):
            alphas = []
            for h in range(2):
                cur = jnp.full((8, t), -1e30, F32)
                for i in range(t // r):
                    rows = slice(i * r, (i + 1) * r)
                    x = s_sc[buf][h, rows, :]
                    if diagonal:
                        x = jnp.where(_iota((r, t), 1) >= i * r + _iota((r, t), 0), x, -1e30)
                        s_sc[buf][h, rows, :] = x
                    cur = _fold8(x, jnp.maximum, cur)
                m_prev = m_sc[h, 0:1, :]
                m_new = jnp.maximum(m_prev, jnp.max(cur, axis=0, keepdims=True))
                alpha = jnp.exp(m_prev - m_new)
                m_sc[h, 0:1, :] = m_new
                alphas.append(alpha)
                tot = jnp.zeros((8, t), F32)
                for i in range(t // r):
                    rows = slice(i * r, (i + 1) * r)
                    pr = jnp.exp(s_sc[buf][h, rows, :] - m_new)
                    p_sc[buf][h, rows, :] = pr.astype(BF16)
                    tot = _fold8(pr, jnp.add, tot)
                l_sc[h, 0:1, :] = alpha * l_sc[h, 0:1, :] + jnp.sum(tot, axis=0, keepdims=True)
            return alphas

        def accumulate(kb, buf, alphas):
            kv = pl.ds(pl.multiple_of(kb * t, t), t)
            for h in range(2):
                part = slice(h * HD, (h + 1) * HD)
                acc_sc[part, :] = acc_sc[part, :] * alphas[h] + _dot(vt_sc[h, :, kv], p_sc[buf][h])

        def first_trip():
            logits(0, 1)
            accumulate(qi, 0, softmax(0, True))
            logits(jnp.minimum(1, qi - 1), 0)
            return tuple(softmax(1, False))

        def only_diagonal():
            accumulate(qi, 0, softmax(0, True))
            return (jnp.ones((1, t), F32),) * 2

        def steady(u, alphas_b):
            accumulate(2 * u - 2, 1, alphas_b)
            logits(2 * u, 1)
            accumulate(2 * u - 1, 0, softmax(0, False))
            logits(jnp.minimum(2 * u + 1, qi - 1), 0)
            return tuple(softmax(1, False))

        logits(qi, 0)
        n_blocks = qi + 1
        alphas_b = lax.cond(qi >= 1, first_trip, only_diagonal)
        alphas_b = lax.fori_loop(1, n_blocks // 2, steady, alphas_b)
        last_b = 2 * (n_blocks // 2) - 2

        @pl.when((qi >= 1) & (n_blocks % 2 == 0))
        def _():
            accumulate(last_b, 1, alphas_b)

        @pl.when((qi >= 2) & (n_blocks % 2 == 1))
        def _():
            accumulate(last_b, 1, alphas_b)
            accumulate(qi - 1, 0, softmax(0, False))

        l0, l1 = l_sc[0, 0:1, :], l_sc[1, 0:1, :]
        o_ref[...] = (acc_sc[...] / jnp.where(top, l0, l1)).T
        lse_ref[0] = _rows01(m_sc[0, 0:1, :] + jnp.log(l0), m_sc[1, 0:1, :] + jnp.log(l1))

    return pl.pallas_call(
        body, name="attn_fwd", grid=(NH // 2, nq),
        in_specs=[pl.BlockSpec((t, 128), lambda j, qi: (qi, OFF_Q // 128 + j)),
                  pl.BlockSpec((s, 128), lambda j, qi: (0, OFF_K // 128 + j)),
                  pl.BlockSpec((s, 128), lambda j, qi: (0, OFF_V // 128 + j)),
                  pl.BlockSpec((s, 128), lambda j, qi: (0, 0))],
        out_specs=[pl.BlockSpec((t, 128), lambda j, qi: (qi, j)),
                   pl.BlockSpec((1, 8, t), lambda j, qi: (j, 0, qi))],
        out_shape=[jax.ShapeDtypeStruct((s, D), F32), jax.ShapeDtypeStruct((NH // 2, 8, s), F32)],
        scratch_shapes=[pltpu.VMEM((2, s, 128), BF16), pltpu.VMEM((2, HD, s), BF16), pltpu.VMEM((2, t, t), F32),
                        pltpu.VMEM((2, t, t), F32), pltpu.VMEM((2, t, t), BF16), pltpu.VMEM((2, t, t), BF16),
                        pltpu.VMEM((2, 8, t), F32), pltpu.VMEM((2, 8, t), F32), pltpu.VMEM((128, t), F32)],
        compiler_params=_params(("parallel", "arbitrary")),
    )(p, p, p, cum)


def _attn_bwd(p, cum, o, lse, do, s):
    t = min(ATT_BLOCK, s)
    nq = s // t
    r = ATT_STRIP

    def body(q_ref, k_ref, v_ref, c_ref, o_ref, lse_ref, do_ref, dq_ref, dk_ref, dv_ref, dc_ref, dr_ref,
             qaugt_sc, qh_sc, dot_sc, doh_sc, delta_sc, dqt_sc, dr_sc, kaug_sc, vh_sc, kt_sc,
             s0_sc, s1_sc, dp0_sc, dp1_sc, p0_sc, p1_sc, ds0_sc, ds1_sc, dk_sc, dv_sc, dc_sc):
        j, ki = pl.program_id(0), pl.program_id(1)
        s_sc, dp_sc, p_sc, ds_sc = (s0_sc, s1_sc), (dp0_sc, dp1_sc), (p0_sc, p1_sc), (ds0_sc, ds1_sc)

        @pl.when(ki == 0)
        def _():
            for c in range(nq):
                rows = slice(c * t, (c + 1) * t)
                q_blk, do_blk = q_ref[rows, :], do_ref[rows, :]
                qaugt_sc[0, :, rows], qaugt_sc[1, :, rows] = _q_augmented_t(q_blk)
                dot_sc[:, rows] = do_blk.T.astype(BF16)
                prod_t = (do_blk * o_ref[rows, :]).T
                delta_sc[:, rows] = _rows01(jnp.sum(prod_t[0:HD], axis=0, keepdims=True),
                                            jnp.sum(prod_t[HD:], axis=0, keepdims=True))
                for h in range(2):
                    head = _head_part((t, 128), h, 1)
                    qh_sc[h, rows, :] = jnp.where(head, q_blk * ATT_SCALE, 0.0).astype(BF16)
                    doh_sc[h, rows, :] = jnp.where(head, do_blk, 0.0).astype(BF16)
            dqt_sc[...] = jnp.zeros(dqt_sc.shape, F32)
            dr_sc[...] = jnp.zeros(dr_sc.shape, F32)

        k_blk, v_blk = k_ref[...], v_ref[...]
        kt = k_blk.T
        for h in range(2):
            kaug_sc[h] = _k_augmented(k_blk, c_ref[...], j, h)
            vh_sc[h] = jnp.where(_head_part((t, 128), h, 1), v_blk, 0.0).astype(BF16)
            kt_sc[h] = kt[h * HD:(h + 1) * HD].astype(BF16)
        dk_sc[...] = jnp.zeros(dk_sc.shape, F32)
        dv_sc[...] = jnp.zeros(dv_sc.shape, F32)
        dc_sc[...] = jnp.zeros(dc_sc.shape, F32)

        def inputs(qb, buf):
            qs = pl.ds(pl.multiple_of(qb * t, t), t)
            for h in range(2):
                s_sc[buf][h] = _dot(kaug_sc[h], qaugt_sc[h, :, qs])
                dp_sc[buf][h] = _dot(vh_sc[h], dot_sc[:, qs])

        def elementwise(qb, buf, diagonal):
            qs = pl.ds(pl.multiple_of(qb * t, t), t)
            for h in range(2):
                lse_row, delta_row = lse_ref[0, h:h + 1, qs], delta_sc[h:h + 1, qs]
                tot = jnp.zeros((8, t), F32)
                for i in range(t // r):
                    rows = slice(i * r, (i + 1) * r)
                    x = s_sc[buf][h, rows, :]
                    if diagonal:
                        x = jnp.where(_iota((r, t), 1) >= i * r + _iota((r, t), 0), x, -1e30)
                    pr = jnp.exp(x - lse_row)
                    ds = pr * (dp_sc[buf][h, rows, :] - delta_row)
                    p_sc[buf][h, rows, :] = pr.astype(BF16)
                    ds_sc[buf][h, rows, :] = ds.astype(BF16)
                    dc_sc[h, rows, :] += sum(ds[:, 128 * g:128 * (g + 1)] for g in range(t // 128))
                    tot = _fold8(ds, jnp.add, tot)
                dr_sc[h, :, qs] += tot

        def outputs(qb, buf):
            qs = pl.ds(pl.multiple_of(qb * t, t), t)
            dv_sc[...] += _dot(p_sc[buf][0], doh_sc[0, qs, :]) + _dot(p_sc[buf][1], doh_sc[1, qs, :])
            dk_sc[...] += _dot(ds_sc[buf][0], qh_sc[0, qs, :]) + _dot(ds_sc[buf][1], qh_sc[1, qs, :])
            for h in range(2):
                dqt_sc[h * HD:(h + 1) * HD, qs] += _dot(kt_sc[h], ds_sc[buf][h])

        def pair(a, b, a_diagonal):
            inputs(a, 0)
            inputs(b, 1)
            elementwise(a, 0, a_diagonal)
            outputs(a, 0)
            elementwise(b, 1, False)
            outputs(b, 1)

        def later(u, carry):
            pair(ki + 1 + 2 * u, ki + 2 + 2 * u, False)
            return carry

        n_later = nq - 1 - ki
        lax.fori_loop(0, n_later // 2, later, 0)

        @pl.when(n_later % 2 == 1)
        def _():
            pair(ki, nq - 1, True)

        @pl.when(n_later % 2 == 0)
        def _():
            inputs(ki, 0)
            elementwise(ki, 0, True)
            outputs(ki, 0)

        dk_ref[...] = dk_sc[...].astype(BF16)
        dv_ref[...] = dv_sc[...].astype(BF16)
        lane = _iota((t, 128), 1)
        cols = jnp.where(lane == 0, jnp.sum(dc_sc[0], axis=1, keepdims=True),
                         jnp.where(lane == 1, jnp.sum(dc_sc[1], axis=1, keepdims=True), 0.0))
        dc_ref[0] = cols.T[0:8, :]

        @pl.when(ki == nq - 1)
        def _():
            for c in range(nq):
                rows = slice(c * t, (c + 1) * t)
                dq_ref[rows, :] = dqt_sc[:, rows].T * ATT_SCALE
            dr_ref[0] = _rows01(jnp.sum(dr_sc[0], axis=0, keepdims=True), jnp.sum(dr_sc[1], axis=0, keepdims=True))

    whole = lambda off: pl.BlockSpec((s, 128), functools.partial(lambda j, ki, off: (0, off + j), off=off))
    return pl.pallas_call(
        body, name="attn_bwd", grid=(NH // 2, nq),
        in_specs=[whole(OFF_Q // 128),
                  pl.BlockSpec((t, 128), lambda j, ki: (ki, OFF_K // 128 + j)),
                  pl.BlockSpec((t, 128), lambda j, ki: (ki, OFF_V // 128 + j)),
                  pl.BlockSpec((t, 128), lambda j, ki: (ki, 0)),
                  whole(0),
                  pl.BlockSpec((1, 8, s), lambda j, ki: (j, 0, 0)),
                  whole(0)],
        out_specs=[whole(0),
                   pl.BlockSpec((t, 128), lambda j, ki: (ki, j)),
                   pl.BlockSpec((t, 128), lambda j, ki: (ki, j)),
                   pl.BlockSpec((1, 8, t), lambda j, ki: (j, 0, ki)),
                   pl.BlockSpec((1, 8, s), lambda j, ki: (j, 0, 0))],
        out_shape=[jax.ShapeDtypeStruct((s, D), F32), jax.ShapeDtypeStruct((s, D), BF16), jax.ShapeDtypeStruct((s, D), BF16),
                   jax.ShapeDtypeStruct((NH // 2, 8, s), F32), jax.ShapeDtypeStruct((NH // 2, 8, s), F32)],
        scratch_shapes=[pltpu.VMEM((2, 128, s), BF16), pltpu.VMEM((2, s, 128), BF16), pltpu.VMEM((128, s), BF16),
                        pltpu.VMEM((2, s, 128), BF16), pltpu.VMEM((8, s), F32), pltpu.VMEM((128, s), F32),
                        pltpu.VMEM((2, 8, s), F32), pltpu.VMEM((2, t, 128), BF16), pltpu.VMEM((2, t, 128), BF16),
                        pltpu.VMEM((2, HD, t), BF16)]
        + [pltpu.VMEM((2, t, t), F32)] * 4 + [pltpu.VMEM((2, t, t), BF16)] * 4
        + [pltpu.VMEM((t, 128), F32), pltpu.VMEM((t, 128), F32), pltpu.VMEM((2, t, 128), F32)],
        compiler_params=_params(("parallel", "arbitrary")),
    )(p, p, p, cum, o, lse, do)


def _ln_stats(u):
    mu = _mean(u)
    d = u - mu
    rstd = lax.rsqrt(_mean(d * d) + EPS)
    return d * rstd, rstd


def _ln_bwd(dx, xh, rstd, gam):
    dxh = dx * gam
    return rstd * (dxh - _mean(dxh) - xh * _mean(dxh * xh))


def _rms_bwd(d, xn, r, w):
    t = d * w
    return r * (t - xn * _mean(t * xn)), _colsum(d * xn)


def _mix_norm(y, p, att, w_ssm, w_att, s):
    def fn(pos, y, z, att, w1, w2):
        g = y * _silu(z)
        n1 = g * lax.rsqrt(_mean(g * g) + EPS) * w1
        n2 = att * lax.rsqrt(_mean(att * att) + EPS) * w2
        return (jnp.concatenate([n1, n2], axis=1),)

    return _rowk("mix_norm", fn, s, 256, [(y, D, 0, 0), (p, D, OFF_Z // D, 0), (att, D, 0, 0)],
                 [w_ssm, w_att], [(2 * D, BF16)], [])[0]


def _mix_norm_bwd(dmix, y, p, att, w_ssm, w_att, s):
    def fn(pos, dmix, y, z, att, w1, w2, a1, a2):
        sz = _silu(z)
        g = y * sz
        r1 = lax.rsqrt(_mean(g * g) + EPS)
        dg, dw1 = _rms_bwd(dmix[:, :D], g * r1, r1, w1)
        r2 = lax.rsqrt(_mean(att * att) + EPS)
        datt, dw2 = _rms_bwd(dmix[:, D:], att * r2, r2, w2)
        return dg * sz, dg * y * _dsilu(z), datt, a1 + dw1, a2 + dw2

    return _rowk("mix_norm_bwd", fn, s, 256, [(dmix, 2 * D, 0, 0), (y, D, 0, 0), (p, D, OFF_Z // D, 0), (att, D, 0, 0)],
                 [w_ssm, w_att], [(D, F32), (D, BF16), (D, F32)], [(1, D), (1, D)])


def _ln1(x0, y, g1, gam, bet, sc2, sh2, s):
    def fn(pos, x0, y, g1, gam, bet, sc2, sh2):
        xh, _ = _ln_stats(ALPHA * x0 + (1.0 + g1) * y)
        x1 = xh * gam + bet
        return x1, _modulate(x1, sc2, sh2)

    return _rowk("ln1", fn, s, 256, [(x0, D, 0, 0), (y, D, 0, 0)], [g1, gam, bet, sc2, sh2], [(D, F32), (D, BF16)], [])


def _ln2_loss(x1, ff, tgt, g2, gam, bet, s):
    def fn(pos, x1, ff, tgt, g2, gam, bet, a_loss, a_dgam, a_dbet, a_dg2):
        xh, rstd = _ln_stats(ALPHA * x1 + (1.0 + g2) * ff)
        err = xh * gam + bet - tgt
        dx2 = err * (1.0 / D)
        du = _ln_bwd(dx2, xh, rstd, gam)
        return (du, du * (1.0 + g2), a_loss + _colsum(err * err), a_dgam + _colsum(dx2 * xh),
                a_dbet + _colsum(dx2), a_dg2 + _colsum(du * ff))

    return _rowk("ln2_loss", fn, s, 256, [(x1, D, 0, 0), (ff, D, 0, 0), (tgt, D, 0, 0)], [g2, gam, bet],
                 [(D, F32), (D, BF16)], [(1, D)] * 4)


def _ln1_bwd(dh2, du2, x0, y, g1, gam, bet, sc2, s):
    def fn(pos, dh2, du2, x0, y, g1, gam, bet, sc2, a_sc, a_sh, a_gam, a_bet, a_g1):
        xh, rstd = _ln_stats(ALPHA * x0 + (1.0 + g1) * y)
        x1 = xh * gam + bet
        dx1 = ALPHA * du2 + dh2 * (1.0 + sc2)
        du1 = _ln_bwd(dx1, xh, rstd, gam)
        return (du1, du1 * (1.0 + g1), a_sc + _colsum(dh2 * x1), a_sh + _colsum(dh2), a_gam + _colsum(dx1 * xh),
                a_bet + _colsum(dx1), a_g1 + _colsum(du1 * y))

    return _rowk("ln1_bwd", fn, s, 256, [(dh2, D, 0, 0), (du2, D, 0, 0), (x0, D, 0, 0), (y, D, 0, 0)],
                 [g1, gam, bet, sc2], [(D, F32), (D, BF16)], [(1, D)] * 5)


def _input_grad(dh1, du1, x0, sc1, s):
    def fn(pos, dh1, du1, x0, sc1, a_sc, a_sh):
        return ALPHA * du1 + dh1 * (1.0 + sc1), a_sc + _colsum(dh1 * x0), a_sh + _colsum(dh1)

    return _rowk("input_grad", fn, s, 256, [(dh1, D, 0, 0), (du1, D, 0, 0), (x0, D, 0, 0)], [sc1],
                 [(D, F32)], [(1, D)] * 2)


def _adamw_math(w, grad, m, v):
    m_new = ADAM_B1 * m + (1.0 - ADAM_B1) * grad
    v_new = ADAM_B2 * v + (1.0 - ADAM_B2) * (grad * grad)
    m_hat = m_new / (1.0 - ADAM_B1 ** ADAM_STEP)
    v_hat = v_new / (1.0 - ADAM_B2 ** ADAM_STEP)
    return -ADAM_LR * (m_hat / (jnp.sqrt(v_hat) + ADAM_EPS) + ADAM_WD * w), m_new, v_new


def _small_update(small_all, layout, w, m, v):
    names = [n for n, _, _ in layout]

    def body(*refs):
        all_ref = refs[0]
        w_refs, m_refs, v_refs = [refs[1 + k * len(names):1 + (k + 1) * len(names)] for k in range(3)]
        sum_ref = refs[1 + 3 * len(names)]
        outs = refs[2 + 3 * len(names):]
        total = all_ref[0]
        for k in range(1, N_DEV):
            total = total + all_ref[k]
        sum_ref[...] = total
        for i, (_, off, size) in enumerate(layout):
            grad = total[:, off:off + size]
            delta, m_new, v_new = _adamw_math(w_refs[i][...], grad, m_refs[i][...], v_refs[i][...])
            for o, val in zip(outs[4 * i:4 * i + 4], (grad, delta, m_new, v_new)):
                o[...] = val

    res = pl.pallas_call(
        body, name="small_update",
        out_shape=[jax.ShapeDtypeStruct(small_all.shape[1:], F32)]
        + [jax.ShapeDtypeStruct(w[n].shape, F32) for n in names for _ in range(4)],
        compiler_params=_params(None),
    )(small_all, *[w[n] for n in names], *[m[n] for n in names], *[v[n] for n in names])
    return res[0], {n: res[1 + 4 * i:5 + 4 * i] for i, n in enumerate(names)}


def _adamw(name, w, g, m, v, *, tr, slots):
    r, c = w.shape

    def body(w_ref, g_ref, m_ref, v_ref, g_out, d_out, m_out, v_out):
        if slots:
            grad = g_ref[0][:, :c].astype(F32)
            for k in range(1, N_DEV):
                grad = grad + g_ref[k][:, :c].astype(F32)
        else:
            grad = g_ref[...]
        g_out[...] = grad
        d_out[...], m_out[...], v_out[...] = _adamw_math(w_ref[...], grad, m_ref[...], v_ref[...])

    tile = pl.BlockSpec((tr, c), lambda i: (i, 0))
    g_spec = pl.BlockSpec((N_DEV, tr, g.shape[-1]), lambda i: (0, i, 0)) if slots else tile
    return pl.pallas_call(
        body, name=name, grid=(r // tr,),
        in_specs=[tile, g_spec, tile, tile], out_specs=[tile] * 4,
        out_shape=[jax.ShapeDtypeStruct((r, c), F32)] * 4,
        compiler_params=_params(("parallel",)),
    )(w, g, m, v)


def _dot_f32(a, b, dims=NN):
    a0, a1, a2 = _split3(a)
    b0, b1, b2 = _split3(b)
    acc = _dot(a0, b0, dims)
    for x, y in ((a0, b1), (a1, b0), (a1, b1), (a0, b2), (a2, b0)):
        acc = acc + _dot(x, y, dims)
    return acc


def _ada_mod(c_all, w_shard, b_shard):
    def body(c_ref, w_ref, b_ref, o_ref):
        act = _silu(c_ref[...])
        act16 = jnp.concatenate([act, jnp.zeros_like(act)], axis=0)
        o_ref[...] = _dot_f32(act16, w_ref[...])[0:N_DEV] + b_ref[...]

    return pl.pallas_call(
        body, name="ada_mod", out_shape=jax.ShapeDtypeStruct((N_DEV, w_shard.shape[1]), F32),
        compiler_params=_params(None),
    )(c_all, w_shard, b_shard)


def _ada_grad(c_all, dmod_cols):
    def body(c_ref, dc_ref, gw_ref):
        act = _silu(c_ref[...])
        act16 = jnp.concatenate([act, jnp.zeros_like(act)], axis=0)
        dm = dc_ref[...]
        dm16 = jnp.concatenate([dm, jnp.zeros_like(dm)], axis=0)
        gw_ref[...] = _dot_f32(act16, dm16, TN)

    return pl.pallas_call(
        body, name="ada_grad", out_shape=jax.ShapeDtypeStruct((D, dmod_cols.shape[1]), F32),
        compiler_params=_params(None),
    )(c_all, dmod_cols)


def _exchange(name, xs, scatter):
    n = len(xs)
    n_peer = N_DEV - 1

    def body(*refs):
        x_refs, o_refs = refs[:n], refs[n:2 * n]
        send_sems, recv_sems, local_sems = refs[2 * n:]
        mx, my, mc = lax.axis_index("x"), lax.axis_index("y"), lax.axis_index("c")
        me = 4 * mx + 2 * my + mc

        def src(a, slot):
            return x_refs[a].at[slot] if scatter else x_refs[a]

        own = [pltpu.make_async_copy(src(a, me), o_refs[a].at[me], local_sems.at[a]) for a in range(n)]
        for cp in own:
            cp.start()
        sends = []
        for d in range(1, N_DEV):
            px = 1 - mx if d & 4 else mx
            py = 1 - my if d & 2 else my
            pc = 1 - mc if d & 1 else mc
            peer = 4 * px + 2 * py + pc
            for a in range(n):
                def copy(src_slot, dst_slot, a=a, d=d, to=(px, py, pc)):
                    return pltpu.make_async_remote_copy(
                        src_ref=src(a, src_slot), dst_ref=o_refs[a].at[dst_slot],
                        send_sem=send_sems.at[a * n_peer + d - 1], recv_sem=recv_sems.at[a * n_peer + d - 1],
                        device_id=to, device_id_type=pl.DeviceIdType.MESH)

                out = copy(peer, me)
                out.start()
                sends.append((out, copy(me, peer)))
        for _, arrival in sends:
            arrival.wait_recv()
        for out, _ in sends:
            out.wait_send()
        for cp in own:
            cp.wait()

    shapes = [tuple(x.shape[1:] if scatter else x.shape) for x in xs]
    return pl.pallas_call(
        body, name=name,
        in_specs=[pl.BlockSpec(memory_space=pl.ANY)] * n, out_specs=[pl.BlockSpec(memory_space=pl.ANY)] * n,
        out_shape=[jax.ShapeDtypeStruct((N_DEV,) + sh, x.dtype) for sh, x in zip(shapes, xs)],
        scratch_shapes=[pltpu.SemaphoreType.DMA((n * n_peer,)), pltpu.SemaphoreType.DMA((n * n_peer,)),
                        pltpu.SemaphoreType.DMA((n,))],
        compiler_params=pltpu.CompilerParams(has_side_effects=True),
    )(*xs)


def _gather_two_level(name, x):
    def body(x_ref, o_ref, send_sems, recv_sems, local_sem):
        mx, my, mc = lax.axis_index("x"), lax.axis_index("y"), lax.axis_index("c")
        me, sibling = (mx, my, mc), (mx, my, 1 - mc)
        chips = [(1 - mx, my), (mx, 1 - my), (1 - mx, 1 - my)]

        def slot(px, py, pc):
            return o_ref.at[4 * px + 2 * py + pc]

        def copy(k, block, to, src=None):
            return pltpu.make_async_remote_copy(
                src_ref=slot(*block) if src is None else src, dst_ref=slot(*block),
                send_sem=send_sems.at[k], recv_sem=recv_sems.at[k], device_id=to, device_id_type=pl.DeviceIdType.MESH)

        mine = pltpu.make_async_copy(x_ref, slot(*me), local_sem)
        mine.start()
        first = [copy(0, me, sibling, src=x_ref)] + [copy(1 + i, me, (*chip, mc), src=x_ref) for i, chip in enumerate(chips)]
        for cp in first:
            cp.start()
        passed = [copy(4 + i, (*chip, mc), sibling) for i, chip in enumerate(chips)]
        for i, chip in enumerate(chips):
            copy(1 + i, (*chip, mc), me).wait_recv()
            passed[i].start()
        copy(0, sibling, me).wait_recv()
        for i, chip in enumerate(chips):
            copy(4 + i, (*chip, 1 - mc), me).wait_recv()
        for cp in first + passed:
            cp.wait_send()
        mine.wait()

    return pl.pallas_call(
        body, name=name,
        in_specs=[pl.BlockSpec(memory_space=pl.ANY)], out_specs=pl.BlockSpec(memory_space=pl.ANY),
        out_shape=jax.ShapeDtypeStruct((N_DEV,) + tuple(x.shape), x.dtype),
        scratch_shapes=[pltpu.SemaphoreType.DMA((7,)), pltpu.SemaphoreType.DMA((7,)), pltpu.SemaphoreType.DMA(())],
        compiler_params=pltpu.CompilerParams(has_side_effects=True),
    )(x)


def _after(x, zero):
    return x if zero is None else x + zero.reshape(-1)[0].astype(x.dtype)


def _exchange_copies(x_refs, land_refs, send_sems, recv_sems, scatter):
    n = len(x_refs)
    n_peer = N_DEV - 1
    mx, my, mc = lax.axis_index("x"), lax.axis_index("y"), lax.axis_index("c")
    me = 4 * mx + 2 * my + mc
    pairs = []
    for d in range(1, N_DEV):
        px = 1 - mx if d & 4 else mx
        py = 1 - my if d & 2 else my
        pc = 1 - mc if d & 1 else mc
        peer = 4 * px + 2 * py + pc
        for a in range(n):
            def copy(src_slot, dst_slot, a=a, d=d, to=(px, py, pc)):
                return pltpu.make_async_remote_copy(
                    src_ref=x_refs[a].at[src_slot] if scatter else x_refs[a], dst_ref=land_refs[a].at[dst_slot],
                    send_sem=send_sems.at[a * n_peer + d - 1], recv_sem=recv_sems.at[a * n_peer + d - 1],
                    device_id=to, device_id_type=pl.DeviceIdType.MESH)

            pairs.append((copy(peer, me), copy(me, peer)))
    return me, pairs


def _exchange_async(name, xs, scatter, collective_id):
    n = len(xs)
    shapes = [tuple(x.shape[1:] if scatter else x.shape) for x in xs]
    x_refs = [jax.new_ref(x, memory_space=pltpu.MemorySpace.HBM) for x in xs]
    land_refs = [jax.empty_ref(jax.ShapeDtypeStruct((N_DEV,) + sh, x.dtype), memory_space=pltpu.MemorySpace.HBM)
                 for sh, x in zip(shapes, xs)]

    @pl.kernel(mesh=plsc.ScalarSubcoreMesh(axis_name="sequencer", num_cores=1), name=name,
               scratch_types=(pltpu.SemaphoreType.DMA((n * (N_DEV - 1),)), pltpu.SemaphoreType.DMA((n * (N_DEV - 1),)),
                              pltpu.SemaphoreType.DMA((n,))),
               compiler_params=pltpu.CompilerParams(collective_id=collective_id))
    def launch(send_sems, recv_sems, own_sems):
        barrier = pltpu.get_barrier_semaphore()
        mx, my, mc = lax.axis_index("x"), lax.axis_index("y"), lax.axis_index("c")
        for d in range(1, N_DEV):
            peer = (1 - mx if d & 4 else mx, 1 - my if d & 2 else my, 1 - mc if d & 1 else mc)
            pl.semaphore_signal(barrier, inc=1, device_id=peer, device_id_type=pl.DeviceIdType.MESH)
        pl.semaphore_wait(barrier, N_DEV - 1)
        me, pairs = _exchange_copies(x_refs, land_refs, send_sems, recv_sems, scatter)
        own = [pltpu.make_async_copy(x_refs[a].at[me] if scatter else x_refs[a], land_refs[a].at[me], own_sems.at[a])
               for a in range(n)]
        for cp in own:
            cp.start()
        for out, _ in pairs:
            out.start()
        for out, arrival in pairs:
            arrival.wait_recv()
            out.wait_send()
        for cp in own:
            cp.wait()

    launch()
    return lambda: [r[...] for r in land_refs]


def _relu2(a):
    r = jnp.maximum(a, 0.0)
    return r * r


def _relu2_grad(acc, a):
    return acc * (2.0 * jnp.maximum(a, 0.0))


def _local_step(x0, tgt, mod, wcat, late_weights, send_grads, conv_w, conv_b, dt_bias, a_log, d_skip, ssm_norm_w, f_bias,
                attn_norm_w, ln1_g, ln1_b, ln2_g, ln2_b):
    ff_w = DFF // N_DEV
    s = x0.shape[0]
    tm = min(1024, s)
    ts = min(1024, s)
    sh1, sc1, g1, sh2, sc2, g2 = [mod[:, i * D:(i + 1) * D] for i in range(6)]
    zero = jnp.zeros((1, 128 - 2 * NH), F32)
    bias128 = jnp.concatenate([dt_bias, f_bias, zero], axis=1)
    alog128 = jnp.concatenate([a_log, jnp.zeros((1, 128 - NH), F32)], axis=1)
    dskip_x = jnp.repeat(d_skip, HD, axis=1)
    w_xs, w_bc, b_xs, b_bc = conv_w[:, :D], conv_w[:, D:], conv_b[:, :D], conv_b[:, D:]

    p = _mm_nn("in_proj", x0, wcat, tm=tm, tn=1152, tk=D, out_dtype=F32, pro=_modulate, aux=(sc1, sh1))
    xs_a, bc_a = _conv_fwd(p, w_xs, b_xs, w_bc, b_bc, s)
    y_ssd, states = _ssd_fwd(xs_a, bc_a, p, bias128, alog128, dskip_x, s)
    cum = _cum_fwd(p, bias128, s)
    att, lse = _attn_fwd(p, cum, s)
    wout, w1s, w2 = late_weights(lse)
    ymix = _mix_norm(y_ssd, p, att, ssm_norm_w, attn_norm_w, s)
    y = _mm_nn("out_proj", ymix, wout, tm=tm, tn=1024, tk=2 * D, out_dtype=F32)
    x1, h2 = _ln1(x0, y, g1, ln1_g, ln1_b, sc2, sh2, s)
    a1 = _mm_nn("ff_in", h2, w1s, tm=tm, tn=ff_w, tk=D, out_dtype=F32)
    ff = _mm_nn("ff_out", a1, w2, tm=tm, tn=1024, tk=1024, out_dtype=F32, pro=_relu2)
    du2, dff, sq_err, d_ln2_g, d_ln2_b, d_g2 = _ln2_loss(x1, ff, tgt, g2, ln2_g, ln2_b, s)

    da1 = _mm_nt("d_ff_hidden", [(dff, D, 0)], [(w2, D, 0)], n=DFF, tm=tm, tn=1024, out_dtype=BF16, epi=_relu2_grad,
                 epi_aux=(a1,))
    d_w2 = _mm_tn("d_w_ff_out", a1, dff, tm=1024, tn=1024, ts=ts, pro=_relu2)
    d_w1s = _mm_tn("d_w_ff_in", h2, da1, tm=1024, tn=ff_w, ts=ts, col_shards=True)
    dh2 = _mm_nt("d_ff_input", [(da1, ff_w, k) for k in range(N_DEV)], [(w1s, ff_w, k) for k in range(N_DEV)], n=D,
                 tm=min(512, s), tn=1024, out_dtype=F32)
    du1, dy, d_sc2, d_sh2, d_ln1_g, d_ln1_b, d_g1 = _ln1_bwd(dh2, du2, x0, y, g1, ln1_g, ln1_b, sc2, s)

    dmix = _mm_nt("d_mix", [(dy, D, 0)], [(wout, D, 0)], n=2 * D, tm=tm, tn=1024, out_dtype=F32)
    d_wout = _mm_tn("d_w_out", ymix, dy, tm=1024, tn=1024, ts=ts)
    sent = send_grads("late", [d_w1s, d_w2.reshape(N_DEV, -1, D), d_wout.reshape(N_DEV, -1, D)])
    dy_ssd, dz, datt, d_ssm_w, d_attn_w = _mix_norm_bwd(dmix, y_ssd, p, att, _after(ssm_norm_w, sent), attn_norm_w, s)
    dq, dk, dv, dcs, drs = _attn_bwd(p, cum, att, lse, datt, s)
    dxs_a, dbc_a, ddt_raw, d_alog, d_dskip = _ssd_bwd(dy_ssd, xs_a, bc_a, p, states, bias128, alog128, dskip_x, s)
    dcum = jnp.pad((drs - dcs)[:, :2, :].reshape(NH, s).T, ((0, 0), (NH, 128 - 2 * NH)))
    ddtf, _, d_bias = _cum_bwd(dcum, ddt_raw, p, bias128, s)
    dxs, dbc, d_wc_xs, d_bc_xs, d_wc_bc, d_bc_bc = _conv_bwd(dxs_a, dbc_a, p, w_xs, b_xs, w_bc, b_bc, s)

    segs = [(dz, OFF_Z, D), (dxs, OFF_XS, D), (dq, OFF_Q, D), (dk, OFF_K, D), (dv, OFF_V, D), (dbc, OFF_BC, 512),
            (ddtf, OFF_DTF, 128)]
    d_z, d_xs, d_q, d_k, d_v, d_bcw, d_dtf = [
        _mm_tn("d_w_in_%d" % i, x0, a, tm=1024, tn=min(w, 1024), ts=ts, pro=_modulate, aux=(sc1, sh1))
        for i, (a, _, w) in enumerate(segs)]
    d_w_in = dict(z=d_z, xs=d_xs, bc=d_bcw, dt=d_dtf[:, :NH], q=d_q, k=d_k, v=d_v, f=d_dtf[:, NH:2 * NH])
    sent = send_grads("in", [_shard_w_in_grad(d_w_in)])
    segs[-1] = (_after(ddtf, sent), OFF_DTF, 128)
    dh1 = _mm_nt("d_h1", [(a, w, 0) for a, _, w in segs], [(wcat, w, off // w) for _, off, w in segs], n=D,
                 tm=min(512, s), tn=1024, out_dtype=F32)
    grad_x, d_sc1, d_sh1 = _input_grad(dh1, du1, x0, sc1, s)

    return dict(
        loss=(0.5 / D) * jnp.sum(sq_err), grad_x=grad_x,
        d_mod=jnp.concatenate([d_sh1, d_sc1, d_g1, d_sh2, d_sc2, d_g2], axis=1),
        d_conv_w=jnp.concatenate([d_wc_xs[:4], d_wc_bc[:4]], axis=1), d_conv_b=jnp.concatenate([d_bc_xs, d_bc_bc], axis=1),
        d_ssm_norm_w=d_ssm_w, d_attn_norm_w=d_attn_w, d_ln1_g=d_ln1_g, d_ln1_b=d_ln1_b, d_ln2_g=d_ln2_g, d_ln2_b=d_ln2_b,
        d_gate_bias=d_bias, d_a_log=d_alog, d_d_skip=d_dskip)


W_IN_SEGS = [('z', W_Z, D), ('xs', W_XS, D), ('bc', W_BC, 512), ('dt', W_DT, NH), ('q', W_Q, D), ('k', W_K, D),
             ('v', W_V, D), ('f', W_F, NH)]
SHARD_W = IN_COLS // N_DEV


def _pack_w_in(shards):
    def cols(lo, hi):
        pieces = []
        while lo < hi:
            dev = lo // SHARD_W
            end = min(hi, (dev + 1) * SHARD_W)
            pieces.append(shards[dev][:, lo - dev * SHARD_W:end - dev * SHARD_W])
            lo = end
        return pieces

    seg = {n: cols(off, off + w) for n, off, w in W_IN_SEGS}
    pieces = seg['z'] + seg['xs'] + seg['q'] + seg['k'] + seg['v'] + seg['bc'] + seg['dt'] + seg['f']
    return jnp.concatenate(pieces + [jnp.zeros((D, 128 - 2 * NH), shards.dtype)], axis=1)


def _shard_w_in_grad(d_w_in):
    blocks = []
    for dev in range(N_DEV):
        lo, hi = dev * SHARD_W, (dev + 1) * SHARD_W
        pieces = [d_w_in[n][:, max(lo, off) - off:min(hi, off + w) - off] for n, off, w in W_IN_SEGS
                  if max(lo, off) < min(hi, off + w)]
        pieces.append(jnp.zeros((D, -SHARD_W % 128), pieces[0].dtype))
        blocks.append(jnp.concatenate(pieces, axis=1))
    return jnp.stack(blocks, axis=0)


WEIGHTS = ['w_ada', 'b_ada', 'w_in', 'conv_w', 'conv_b', 'dt_bias', 'a_log', 'd_skip', 'ssm_norm_w', 'f_bias',
           'attn_norm_w', 'w_out', 'ln1_g', 'ln1_b', 'w_ff_in', 'w_ff_out', 'ln2_g', 'ln2_b']
BIG = ['w_in', 'w_out', 'w_ff_in', 'w_ff_out']
SMALL_LAYOUT = [('b_ada', 0, 6 * D), ('conv_b', 12288, 1536), ('ssm_norm_w', 13824, D), ('attn_norm_w', 14848, D),
                ('ln1_g', 15872, D), ('ln1_b', 16896, D), ('ln2_g', 17920, D), ('ln2_b', 18944, D),
                ('dt_bias', 19968, NH), ('f_bias', 19968 + NH, NH), ('a_log', 20096, NH), ('d_skip', 20224, NH)]
SMALL_LOSS_LANE = 20352


def _pad_lanes(v, n=128):
    return jnp.pad(v, ((0, 0), (0, n - v.shape[1])))


def kernel(x, c, w_ada, b_ada, w_in, conv_w, conv_b, dt_bias, a_log, d_skip, ssm_norm_w, f_bias, attn_norm_w, w_out, ln1_g, ln1_b, w_ff_in, w_ff_out, ln2_g, ln2_b, loss_target, m_w_ada, m_b_ada, m_w_in, m_conv_w, m_conv_b, m_dt_bias, m_a_log, m_d_skip, m_ssm_norm_w, m_f_bias, m_attn_norm_w, m_w_out, m_ln1_g, m_ln1_b, m_w_ff_in, m_w_ff_out, m_ln2_g, m_ln2_b, v_w_ada, v_b_ada, v_w_in, v_conv_w, v_conv_b, v_dt_bias, v_a_log, v_d_skip, v_ssm_norm_w, v_f_bias, v_attn_norm_w, v_w_out, v_ln1_g, v_ln1_b, v_w_ff_in, v_w_ff_out, v_ln2_g, v_ln2_b):
    args = dict(locals())
    w = {n: args[n] for n in WEIGHTS}
    m = {n: args['m_' + n] for n in WEIGHTS}
    v = {n: args['v_' + n] for n in WEIGHTS}
    me = 4 * lax.axis_index("x") + 2 * lax.axis_index("y") + lax.axis_index("c")
    ada_cols = 6 * D // N_DEV
    conv_cols = conv_w.shape[2]

    c_all, conv_all = _exchange("gather_cond", [c, conv_w[0]], False)
    c_all = c_all.reshape(N_DEV, D)
    conv_w_full = conv_all.transpose(1, 0, 2).reshape(4, N_DEV * conv_cols)
    b_shard = lax.dynamic_slice(b_ada, (0, me * ada_cols), (1, ada_cols))
    mod_all, = _exchange("gather_mod", [_ada_mod(c_all, w_ada[0], b_shard)], False)
    mod = lax.dynamic_index_in_dim(mod_all, me, axis=1, keepdims=False).reshape(1, 6 * D)

    win_s = _gather_two_level("gather_w_in", _after(w_in[0].astype(BF16), mod * 0))
    first_done = win_s[0, 0:1, 0:1] * 0
    rest = _exchange_async("gather_rest", [_after(w[n][0].astype(BF16), first_done) for n in BIG[1:]], False, 1)

    def late_weights(after):
        wout_s, w1s, w2_s = rest()
        return wout_s.reshape(2 * D, D), w1s, w2_s.reshape(DFF, D)

    sends = {}

    def send_grads(tag, blocks):
        sends[tag] = _exchange_async("scatter_" + tag, blocks, True, {'late': 2, 'in': 3}[tag])
        return sum(b.reshape(-1)[0].astype(F32) * 0 for b in blocks)

    out = _local_step(x[0], loss_target[0], mod, _pack_w_in(win_s), late_weights, send_grads,
                      conv_w_full, conv_b, dt_bias, a_log, d_skip, ssm_norm_w, f_bias, attn_norm_w, ln1_g, ln1_b, ln2_g, ln2_b)

    small = jnp.concatenate(
        [out['d_mod'], out['d_conv_w'].reshape(1, -1), out['d_conv_b'], out['d_ssm_norm_w'], out['d_attn_norm_w'],
         out['d_ln1_g'], out['d_ln1_b'], out['d_ln2_g'], out['d_ln2_b'], out['d_gate_bias'], out['d_a_log'],
         out['d_d_skip'], _pad_lanes(out['loss'].reshape(1, 1))], axis=1)
    small_landed = _exchange_async("gather_small", [small], False, 4)
    (g_ff_in, g_ff_out, g_out), (g_in,) = sends['late'](), sends['in']()
    g_parts = dict(w_ff_in=g_ff_in, w_ff_out=g_ff_out, w_out=g_out, w_in=g_in)
    big = {n: _adamw("adamw_" + n, w[n][0], g_parts[n], m[n][0], v[n][0], tr=256, slots=True) for n in BIG}
    big_done = sum(big[n][1][0:1, 0:1] * 0 for n in BIG)
    small_all = _after(small_landed()[0], big_done)
    ssum, small_res = _small_update(small_all, SMALL_LAYOUT, w, m, v)
    dmod_all = small_all[:, 0, :6 * D]
    g_w_ada = _ada_grad(c_all, lax.dynamic_slice(dmod_all, (0, me * ada_cols), (N_DEV, ada_cols)))
    ada = _adamw("adamw_ada", w_ada[0], g_w_ada, m_w_ada[0], v_w_ada[0], tr=256, slots=False)
    g_conv_w = lax.dynamic_slice(ssum[:, 6 * D:6 * D + 4 * N_DEV * conv_cols].reshape(4, N_DEV * conv_cols),
                                 (0, me * conv_cols), (4, conv_cols))
    conv = _adamw("adamw_conv_w", conv_w[0], g_conv_w, m_conv_w[0], v_conv_w[0], tr=4, slots=False)

    results = []
    for k in range(4):
        vals = {n: small_res[n][k] for n in small_res}
        vals['w_ada'], vals['conv_w'] = ada[k][None], conv[k][None]
        for n in BIG:
            vals[n] = big[n][k][None]
        results.append(vals)
    return (ssum[0, SMALL_LOSS_LANE], out['grad_x'][None], *[res[n] for res in results for n in WEIGHTS])
```

```python
import functools

import jax
import jax.numpy as jnp
from jax import lax
from jax.experimental import pallas as pl
from jax.experimental.pallas import tpu as pltpu
from jax.experimental.pallas import tpu_sc as plsc

F32, BF16 = jnp.float32, jnp.bfloat16

N_DEV = 8
D = 1024
NH, HD = 16, 64
NSTATE = 128
CHUNK = 128
HG = 8
DFF = 4096
ALPHA = 2.0 ** 0.25
EPS = 1e-5
ATT_SCALE = HD ** -0.5

OFF_Z, OFF_XS, OFF_Q, OFF_K, OFF_V, OFF_BC, OFF_DTF = 0, 1024, 2048, 3072, 4096, 5120, 5632
PCOLS = 5760
W_Z, W_XS, W_BC, W_DT, W_Q, W_K, W_V, W_F = 0, 1024, 2048, 2560, 2576, 3600, 4624, 5648
IN_COLS = 5664

ADAM_LR, ADAM_B1, ADAM_B2, ADAM_EPS, ADAM_WD, ADAM_STEP = 0.001, 0.9, 0.999, 1e-08, 0.01, 10

VMEM_LIMIT = 56 << 20

NN = (((1,), (0,)), ((), ()))
NT = (((1,), (1,)), ((), ()))
TN = (((0,), (0,)), ((), ()))


def _dot(a, b, dims=NN):
    return lax.dot_general(a, b, dims, preferred_element_type=F32)


def _bdot(a, b, dims=NN):
    return _dot(a.astype(BF16), b.astype(BF16), dims)


def _split3(v, terms=3):
    parts, rest = [], v
    for _ in range(terms):
        p = rest.astype(BF16)
        parts.append(p)
        rest = rest - p.astype(F32)
    return parts


def _sel_left(m01, v):
    return sum(_dot(m01, p) for p in _split3(v))


def _sel_right(v, m01, dims=NN, terms=3):
    return sum(_dot(p, m01, dims) for p in _split3(v, terms))


def _iota(shape, dim):
    return lax.broadcasted_iota(jnp.int32, shape, dim)


def _tri_lower(n):
    return (_iota((n, n), 1) <= _iota((n, n), 0)).astype(BF16)


def _tri_upper(n):
    return (_iota((n, n), 1) >= _iota((n, n), 0)).astype(BF16)


def _head_expand():
    return (lax.shift_right_logical(_iota((128, D), 1), 6) == _iota((128, D), 0)).astype(BF16)


def _head_reduce():
    return (lax.shift_right_logical(_iota((D, 128), 0), 6) == _iota((D, 128), 1)).astype(BF16)


def _sigmoid(x):
    return 1.0 / (1.0 + jnp.exp(-x))


def _silu(x):
    return x * _sigmoid(x)


def _dsilu(x):
    s = _sigmoid(x)
    return s * (1.0 + x * (1.0 - s))


def _softplus(x):
    return jnp.maximum(x, 0.0) + jnp.log(1.0 + jnp.exp(-jnp.abs(x)))


def _log_sigmoid(x):
    return jnp.minimum(x, 0.0) - jnp.log(1.0 + jnp.exp(-jnp.abs(x)))


def _params(sem):
    return pltpu.CompilerParams(dimension_semantics=sem, vmem_limit_bytes=VMEM_LIMIT)


def _mm_nn(name, a, b, *, tm, tn, tk, out_dtype, pro=None, aux=(), epi=None):
    m, k_all = a.shape
    b_sharded = b.ndim == 3
    n = b.shape[0] * b.shape[2] if b_sharded else b.shape[1]
    assert not b_sharded or tn == b.shape[2]
    nk = k_all // tk
    n_aux = len(aux)
    b_spec = (pl.BlockSpec((None, tk, tn), lambda i, j, k: (j, k, 0)) if b_sharded
              else pl.BlockSpec((tk, tn), lambda i, j, k: (k, j)))

    def body(a_ref, b_ref, *rest):
        aux_refs, o_ref = rest[:n_aux], rest[n_aux]
        at = a_ref[...]
        if pro is not None:
            at = pro(at, *[r[...] for r in aux_refs])
        part = _bdot(at, b_ref[...])
        if nk == 1:
            o_ref[...] = (part if epi is None else epi(part)).astype(out_dtype)
            return
        assert epi is None
        acc_ref = rest[n_aux + 1]
        kk = pl.program_id(2)

        @pl.when(kk == 0)
        def _():
            acc_ref[...] = part

        @pl.when(kk > 0)
        def _():
            acc_ref[...] += part

        @pl.when(kk == nk - 1)
        def _():
            o_ref[...] = acc_ref[...].astype(out_dtype)

    return pl.pallas_call(
        body, name=name,
        grid=(m // tm, n // tn, nk),
        in_specs=[pl.BlockSpec((tm, tk), lambda i, j, k: (i, k)), b_spec]
        + [pl.BlockSpec((1, tk), lambda i, j, k: (0, k)) for _ in aux],
        out_specs=pl.BlockSpec((tm, tn), lambda i, j, k: (i, j)),
        out_shape=jax.ShapeDtypeStruct((m, n), out_dtype),
        scratch_shapes=[] if nk == 1 else [pltpu.VMEM((tm, tn), F32)],
        compiler_params=_params(("parallel", "parallel", "arbitrary")),
    )(a, b, *aux)


def _mm_nt(name, a_list, b_list, *, n, tm, tn, out_dtype, epi=None, epi_aux=()):
    m = a_list[0][0].shape[0]
    n_op = len(a_list)
    n_epi = len(epi_aux)

    def body(*refs):
        a_refs, b_refs = refs[:n_op], refs[n_op:2 * n_op]
        e_refs, o_ref = refs[2 * n_op:2 * n_op + n_epi], refs[2 * n_op + n_epi]
        acc = None
        for a_ref, b_ref in zip(a_refs, b_refs):
            part = _bdot(a_ref[...], b_ref[...], NT)
            acc = part if acc is None else acc + part
        if epi is not None:
            acc = epi(acc, *[r[...] for r in e_refs])
        o_ref[...] = acc.astype(out_dtype)

    in_specs = [pl.BlockSpec((tm, w), functools.partial(lambda i, j, cb: (i, cb), cb=cb)) for (_, w, cb) in a_list]
    for (b, w, cb) in b_list:
        if b.ndim == 3:
            in_specs.append(pl.BlockSpec((None, tn, w), functools.partial(lambda i, j, cb: (cb, j, 0), cb=cb)))
        else:
            in_specs.append(pl.BlockSpec((tn, w), functools.partial(lambda i, j, cb: (j, cb), cb=cb)))
    in_specs += [pl.BlockSpec((tm, tn), lambda i, j: (i, j)) for _ in epi_aux]
    return pl.pallas_call(
        body, name=name,
        grid=(m // tm, n // tn),
        in_specs=in_specs,
        out_specs=pl.BlockSpec((tm, tn), lambda i, j: (i, j)),
        out_shape=jax.ShapeDtypeStruct((m, n), out_dtype),
        compiler_params=_params(("parallel", "parallel")),
    )(*[a for (a, _, _) in a_list], *[b for (b, _, _) in b_list], *epi_aux)


def _mm_tn(name, a, b, *, tm, tn, ts, pro=None, aux=(), col_shards=False):
    s_all, ka = a.shape
    nb = b.shape[1]
    n_aux = len(aux)
    ns = s_all // ts
    assert not col_shards or tn == nb // N_DEV

    def body(a_ref, b_ref, *rest):
        aux_refs, o_ref, acc_ref = rest[:n_aux], rest[n_aux], rest[n_aux + 1]
        at = a_ref[...]
        if pro is not None:
            at = pro(at, *[r[...] for r in aux_refs])
        part = _bdot(at, b_ref[...], TN)
        ss = pl.program_id(2)

        @pl.when(ss == 0)
        def _():
            acc_ref[...] = part

        @pl.when(ss > 0)
        def _():
            acc_ref[...] += part

        @pl.when(ss == ns - 1)
        def _():
            o_ref[...] = acc_ref[...].astype(BF16)

    if col_shards:
        out_spec = pl.BlockSpec((None, tm, tn), lambda i, j, s: (j, i, 0))
        out_shape = jax.ShapeDtypeStruct((N_DEV, ka, tn), BF16)
    else:
        out_spec = pl.BlockSpec((tm, tn), lambda i, j, s: (i, j))
        out_shape = jax.ShapeDtypeStruct((ka, nb), BF16)
    return pl.pallas_call(
        body, name=name,
        grid=(ka // tm, nb // tn, ns),
        in_specs=[pl.BlockSpec((ts, tm), lambda i, j, s: (s, i)),
                  pl.BlockSpec((ts, tn), lambda i, j, s: (s, j))]
        + [pl.BlockSpec((1, tm), lambda i, j, s: (0, i)) for _ in aux],
        out_specs=out_spec, out_shape=out_shape,
        scratch_shapes=[pltpu.VMEM((tm, tn), F32)],
        compiler_params=_params(("parallel", "parallel", "arbitrary")),
    )(a, b, *aux)


def _rowk(name, fn, n_rows, tr, rows, fulls, outs, accs, reverse=False):
    n = n_rows // tr
    n_row, n_full, n_out, n_acc = len(rows), len(fulls), len(outs), len(accs)

    def pos(i):
        return (n - 1 - i) if reverse else i

    def body(*refs):
        row_refs = refs[:n_row]
        full_refs = refs[n_row:n_row + n_full]
        out_refs = refs[n_row + n_full:n_row + n_full + n_out]
        acc_refs = refs[n_row + n_full + n_out:]
        i = pl.program_id(0)

        @pl.when(i == 0)
        def _():
            for r in acc_refs:
                r[...] = jnp.zeros(r.shape, r.dtype)

        res = fn(pos(i), *[r[...] for r in row_refs], *[r[...] for r in full_refs], *[r[...] for r in acc_refs])
        for r, v in zip(out_refs + acc_refs, res):
            r[...] = v.astype(r.dtype)

    def row_map(i, cb, shift):
        return (jnp.clip(pos(i) + shift, 0, n - 1), cb)

    def halo_map(i, cb, shift):
        tile = jnp.clip(pos(i) + shift, 0, n - 1)
        return (tile * (tr // 8) + (tr // 8 - 1 if shift < 0 else 0), cb)

    in_specs = [pl.BlockSpec((tr, w), functools.partial(row_map, cb=cb, shift=sh)) if sh == 0 else
                pl.BlockSpec((8, w), functools.partial(halo_map, cb=cb, shift=sh)) for (_, w, cb, sh) in rows]
    in_specs += [pl.BlockSpec(f.shape, functools.partial(lambda i, nd: (0,) * nd, nd=f.ndim)) for f in fulls]
    out_specs = [pl.BlockSpec((tr, w), lambda i: (pos(i), 0)) for (w, _) in outs]
    out_specs += [pl.BlockSpec((r, w), lambda i: (0, 0)) for (r, w) in accs]
    out_shape = [jax.ShapeDtypeStruct((n_rows, w), dt) for (w, dt) in outs]
    out_shape += [jax.ShapeDtypeStruct((r, w), F32) for (r, w) in accs]
    return pl.pallas_call(
        body, name=name, grid=(n,), in_specs=in_specs, out_specs=out_specs, out_shape=out_shape,
        compiler_params=_params(("arbitrary",)),
    )(*[a for (a, _, _, _) in rows], *fulls)


def _colsum(x):
    return jnp.sum(x, axis=0, keepdims=True)


def _mean(x):
    return jnp.mean(x, axis=-1, keepdims=True)


def _modulate(x, sc, sh):
    return x * (1.0 + sc) + sh


def _shift_down(cur, prev8, j):
    tr = cur.shape[0]
    row8 = _iota(prev8.shape, 0)
    head = jnp.where(row8 < j, pltpu.roll(prev8, j, 0), pltpu.roll(cur[0:8], j, 0))
    return head if tr == 8 else jnp.concatenate([head, pltpu.roll(cur, j, 0)[8:]], axis=0)


def _shift_up(cur, next8, j):
    tr = cur.shape[0]
    row8 = _iota(next8.shape, 0)
    tail = jnp.where(row8 < 8 - j, pltpu.roll(cur[tr - 8:], 8 - j, 0), pltpu.roll(next8, 8 - j, 0))
    return jnp.concatenate([pltpu.roll(cur, tr - j, 0)[:tr - 8], tail], axis=0)


def _conv(cur, prev, w, b):
    out = cur * w[3:4] + b
    for j in (1, 2, 3):
        out = out + _shift_down(cur, prev, j) * w[3 - j:4 - j]
    return out


def _conv_fwd(p, w_xs, b_xs, w_bc, b_bc, s):
    def fn(pos, xs, xs_prev, bc, bc_prev, w_xs, b_xs, w_bc, b_bc):
        first = pos == 0
        xs_prev = jnp.where(first, 0.0, xs_prev)
        bc_prev = jnp.where(first, 0.0, bc_prev)
        return _silu(_conv(xs, xs_prev, w_xs, b_xs)), _silu(_conv(bc, bc_prev, w_bc, b_bc))

    return _rowk("conv_fwd", fn, s, 256,
                 [(p, D, OFF_XS // D, 0), (p, D, OFF_XS // D, -1), (p, 512, OFF_BC // 512, 0), (p, 512, OFF_BC // 512, -1)],
                 [w_xs, b_xs, w_bc, b_bc], [(D, F32), (512, F32)], [])


def _conv_bwd(dxs_a, dbc_a, p, w_xs, b_xs, w_bc, b_bc, s):
    tr = 256
    n = s // tr

    def fn(pos, da1, da1n, x1, x1p, x1n, da2, da2n, x2, x2p, x2n, w1, b1, w2, b2, aw1, ab1, aw2, ab2):
        dx1, dw1, db1 = _conv_bwd_fn(pos, n, da1, da1n, x1, x1p, x1n, w1, b1)
        dx2, dw2, db2 = _conv_bwd_fn(pos, n, da2, da2n, x2, x2p, x2n, w2, b2)
        return dx1, dx2, aw1 + dw1, ab1 + db1, aw2 + dw2, ab2 + db2

    cx, cb = OFF_XS // D, OFF_BC // 512
    return _rowk("conv_bwd", fn, s, tr,
                 [(dxs_a, D, 0, 0), (dxs_a, D, 0, 1), (p, D, cx, 0), (p, D, cx, -1), (p, D, cx, 1),
                  (dbc_a, 512, 0, 0), (dbc_a, 512, 0, 1), (p, 512, cb, 0), (p, 512, cb, -1), (p, 512, cb, 1)],
                 [w_xs, b_xs, w_bc, b_bc], [(D, BF16), (512, BF16)], [(8, D), (1, D), (8, 512), (1, 512)])


def _conv_bwd_fn(pos, n, da, da_next, x, x_prev, x_next, w, b):
    first, last = pos == 0, pos == n - 1
    x_prev = jnp.where(first, 0.0, x_prev)
    shifted = {j: _shift_down(x, x_prev, j) for j in (1, 2, 3)}
    conv = x * w[3:4] + b
    for j in (1, 2, 3):
        conv = conv + shifted[j] * w[3 - j:4 - j]
    dc = da * _dsilu(conv)
    dc_next = jnp.where(last, 0.0, da_next * _dsilu(_conv(x_next, x[x.shape[0] - 8:], w, b)))
    dx = dc * w[3:4]
    dws = [None] * 4
    dws[3] = _colsum(dc * x)
    for j in (1, 2, 3):
        dx = dx + _shift_up(dc, dc_next, j) * w[3 - j:4 - j]
        dws[3 - j] = _colsum(dc * shifted[j])
    row = _iota((8, x.shape[1]), 0)
    dw = jnp.zeros((8, x.shape[1]), F32)
    for k in range(4):
        dw = jnp.where(row == k, dws[k], dw)
    return dx, dw, _colsum(dc)


def _ssd_gates(dtf, bias, a_log):
    lane = _iota(dtf.shape, 1)
    head = lane < NH
    dt = jnp.where(head, _softplus(dtf + bias), 0.0)
    a_neg = jnp.where(_iota(a_log.shape, 1) < NH, -jnp.exp(a_log), 0.0)
    a = dt * a_neg
    cs = _sel_left(_tri_lower(CHUNK), a)
    return dt, a_neg, cs


def _decay_mask(cs_ref, cst_ref, h):
    diff = cs_ref[:, h:h + 1] - cst_ref[h:h + 1, :]
    low = _iota((CHUNK, CHUNK), 1) <= _iota((CHUNK, CHUNK), 0)
    return jnp.where(low, jnp.exp(jnp.minimum(diff, 0.0)), 0.0)


def _ssd_fwd(xs_a, bc_a, p, bias128, alog128, dskip_x, s):
    nc = s // CHUNK
    t = CHUNK

    def body(xs_ref, bc_ref, dtf_ref, bias_ref, alog_ref, dsk_ref, y_ref, st_ref,
             state, x_sc, xw_sc, cs_sc, cst_sc, yd_sc):
        c = pl.program_id(0)

        @pl.when(c == 0)
        def _():
            state[...] = jnp.zeros(state.shape, F32)

        dt, _, cs = _ssd_gates(dtf_ref[...], bias_ref[...], alog_ref[...])
        cs_sc[...] = cs
        cst_sc[...] = cs.T
        cs_last = cs[t - 1:t, :]
        expand = _head_expand()
        ex = _sel_right(jnp.concatenate([dt, jnp.exp(cs), jnp.exp(cs_last - cs)], axis=0), expand, terms=2)
        dt_x, eo_x, we_x = ex[0:t], ex[t:2 * t], ex[2 * t:3 * t]
        g_x = _sel_right(jnp.broadcast_to(jnp.exp(cs_last), (8, 128)), expand)[0:1]
        xs = xs_ref[...]
        x = xs * dt_x
        x_sc[...] = x.astype(BF16)
        xw_sc[...] = (x * we_x).astype(BF16)
        prev = state[...]
        st_ref[0] = prev
        prev_b = prev.astype(BF16)
        for g in range(2):
            cols = slice(g * 512, (g + 1) * 512)
            b_g = bc_ref[:, g * 128:(g + 1) * 128].astype(BF16)
            c_g = bc_ref[:, 256 + g * 128:256 + (g + 1) * 128].astype(BF16)
            gmat = _dot(c_g, b_g, NT)
            y_off = _dot(c_g, prev_b[:, cols]) * eo_x[:, cols]
            s_loc = _dot(b_g, xw_sc[:, cols], TN)
            state[:, cols] = g_x[:, cols] * prev[:, cols] + s_loc
            for e in range(HG):
                h = g * HG + e
                m = gmat * _decay_mask(cs_sc, cst_sc, h)
                yd_sc[:, h * HD:(h + 1) * HD] = _dot(m.astype(BF16), x_sc[:, h * HD:(h + 1) * HD])
            y_ref[:, cols] = yd_sc[:, cols] + y_off + dsk_ref[:, cols] * xs[:, cols]

    return pl.pallas_call(
        body, name="ssd_fwd", grid=(nc,),
        in_specs=[pl.BlockSpec((t, D), lambda c: (c, 0)),
                  pl.BlockSpec((t, 512), lambda c: (c, 0)),
                  pl.BlockSpec((t, 128), lambda c: (c, OFF_DTF // 128)),
                  pl.BlockSpec((1, 128), lambda c: (0, 0)),
                  pl.BlockSpec((1, 128), lambda c: (0, 0)),
                  pl.BlockSpec((1, D), lambda c: (0, 0))],
        out_specs=[pl.BlockSpec((t, D), lambda c: (c, 0)),
                   pl.BlockSpec((1, NSTATE, D), lambda c: (c, 0, 0))],
        out_shape=[jax.ShapeDtypeStruct((s, D), F32), jax.ShapeDtypeStruct((nc, NSTATE, D), F32)],
        scratch_shapes=[pltpu.VMEM((NSTATE, D), F32), pltpu.VMEM((t, D), BF16), pltpu.VMEM((t, D), BF16),
                        pltpu.VMEM((t, 128), F32), pltpu.VMEM((128, t), F32), pltpu.VMEM((t, D), F32)],
        compiler_params=_params(("arbitrary",)),
    )(xs_a, bc_a, p, bias128, alog128, dskip_x)


def _ssd_bwd(dy, xs_a, bc_a, p, states, bias128, alog128, dskip_x, s):
    nc = s // CHUNK
    t = CHUNK

    def body(dy_ref, xs_ref, bc_ref, dtf_ref, st_ref, bias_ref, alog_ref, dsk_ref,
             dxs_ref, dbc_ref, ddt_ref, dalog_ref, dskip_ref,
             dstate, x_sc, dy_sc, dx_sc, deo_sc, dwe_sc, cs_sc, cst_sc, dcol_sc, drow_sc):
        i = pl.program_id(0)

        @pl.when(i == 0)
        def _():
            dstate[...] = jnp.zeros(dstate.shape, F32)
            dalog_ref[...] = jnp.zeros(dalog_ref.shape, F32)
            dskip_ref[...] = jnp.zeros(dskip_ref.shape, F32)

        dtf = dtf_ref[...]
        dt, a_neg, cs = _ssd_gates(dtf, bias_ref[...], alog_ref[...])
        cs_sc[...] = cs
        cst_sc[...] = cs.T
        cs_last = cs[t - 1:t, :]
        eo, we, g_end = jnp.exp(cs), jnp.exp(cs_last - cs), jnp.exp(cs_last)
        expand, reduce = _head_expand(), _head_reduce()
        ex = _sel_right(jnp.concatenate([dt, eo, we], axis=0), expand, terms=2)
        dt_x, eo_x, we_x = ex[0:t], ex[t:2 * t], ex[2 * t:3 * t]
        g_x = _sel_right(jnp.broadcast_to(g_end, (8, 128)), expand)[0:1]
        xs = xs_ref[...]
        dyv = dy_ref[...]
        x = xs * dt_x
        x_sc[...] = x.astype(BF16)
        dy_sc[...] = dyv.astype(BF16)
        dyo_b = (dyv * eo_x).astype(BF16)
        xw_b = (x * we_x).astype(BF16)
        prev = st_ref[0]
        prev_b = prev.astype(BF16)
        dnext = dstate[...]
        dnext_b = dnext.astype(BF16)
        dcol_sc[...] = jnp.zeros(dcol_sc.shape, F32)
        drow_sc[...] = jnp.zeros(drow_sc.shape, F32)
        lane_row = _iota((1, 128), 1)
        sub_col = _iota((128, 1), 0)
        for g in range(2):
            cols = slice(g * 512, (g + 1) * 512)
            b_g = bc_ref[:, g * 128:(g + 1) * 128].astype(BF16)
            c_g = bc_ref[:, 256 + g * 128:256 + (g + 1) * 128].astype(BF16)
            gmat = _dot(c_g, b_g, NT)
            b_ds = _dot(b_g, dnext_b[:, cols])
            c_s = _dot(c_g, prev_b[:, cols])
            dx_sc[:, cols] = b_ds * we_x[:, cols]
            deo_sc[:, cols] = dyv[:, cols] * c_s
            dwe_sc[:, cols] = b_ds * x[:, cols]
            db = _dot(xw_b[:, cols], dnext_b[:, cols], NT)
            dc = _dot(dyo_b[:, cols], prev_b[:, cols], NT)
            dstate[:, cols] = g_x[:, cols] * dnext[:, cols] + _dot(c_g, dyo_b[:, cols], TN)
            dg = jnp.zeros((t, t), F32)
            for e in range(HG):
                h = g * HG + e
                hc = slice(h * HD, (h + 1) * HD)
                lmat = _decay_mask(cs_sc, cst_sc, h)
                m = gmat * lmat
                dx_sc[:, hc] += _dot(m.astype(BF16), dy_sc[:, hc], TN)
                dm = _dot(dy_sc[:, hc], x_sc[:, hc], NT)
                dg = dg + dm * lmat
                qm = dm * m
                dcol_sc[...] += jnp.sum(qm, axis=1, keepdims=True) * (lane_row == h).astype(F32)
                drow_sc[...] += (sub_col == h).astype(F32) * jnp.sum(qm, axis=0, keepdims=True)
            dg_b = dg.astype(BF16)
            dbc_ref[:, g * 128:(g + 1) * 128] = db + _dot(dg_b, c_g, TN)
            dbc_ref[:, 256 + g * 128:256 + (g + 1) * 128] = dc + _dot(dg_b, b_g)
        d_eo = _sel_right(deo_sc[...], reduce, terms=2)
        d_we = _sel_right(dwe_sc[...], reduce, terms=2)
        d_gend = _sel_right(jnp.broadcast_to(_colsum(dnext * prev), (8, D)), reduce)[0:1]
        d_cs = dcol_sc[...] - drow_sc[...].T + d_eo * eo - d_we * we
        extra = _colsum(d_we * we) + d_gend * g_end
        d_cs = d_cs + jnp.where(_iota((t, 128), 0) == t - 1, extra, 0.0)
        da = _sel_left(_tri_upper(t), d_cs)
        dx = dx_sc[...]
        ddt = _sel_right(dx * xs, reduce, terms=2) + da * a_neg
        dxs_ref[...] = dx * dt_x + dsk_ref[...] * dyv
        ddt_ref[...] = jnp.where(_iota((t, 128), 1) < NH, ddt * _sigmoid(dtf + bias_ref[...]), 0.0)
        dalog_ref[...] += _colsum(da * dt) * a_neg
        dskip_ref[...] += _sel_right(jnp.broadcast_to(_colsum(dyv * xs), (8, D)), reduce)[0:1]

    rev = lambda i: nc - 1 - i
    return pl.pallas_call(
        body, name="ssd_bwd", grid=(nc,),
        in_specs=[pl.BlockSpec((t, D), lambda i: (rev(i), 0)),
                  pl.BlockSpec((t, D), lambda i: (rev(i), 0)),
                  pl.BlockSpec((t, 512), lambda i: (rev(i), 0)),
                  pl.BlockSpec((t, 128), lambda i: (rev(i), OFF_DTF // 128)),
                  pl.BlockSpec((1, NSTATE, D), lambda i: (rev(i), 0, 0)),
                  pl.BlockSpec((1, 128), lambda i: (0, 0)),
                  pl.BlockSpec((1, 128), lambda i: (0, 0)),
                  pl.BlockSpec((1, D), lambda i: (0, 0))],
        out_specs=[pl.BlockSpec((t, D), lambda i: (rev(i), 0)),
                   pl.BlockSpec((t, 512), lambda i: (rev(i), 0)),
                   pl.BlockSpec((t, 128), lambda i: (rev(i), 0)),
                   pl.BlockSpec((1, 128), lambda i: (0, 0)),
                   pl.BlockSpec((1, 128), lambda i: (0, 0))],
        out_shape=[jax.ShapeDtypeStruct((s, D), F32), jax.ShapeDtypeStruct((s, 512), F32),
                   jax.ShapeDtypeStruct((s, 128), F32), jax.ShapeDtypeStruct((1, 128), F32),
                   jax.ShapeDtypeStruct((1, 128), F32)],
        scratch_shapes=[pltpu.VMEM((NSTATE, D), F32), pltpu.VMEM((t, D), BF16), pltpu.VMEM((t, D), BF16),
                        pltpu.VMEM((t, D), F32), pltpu.VMEM((t, D), F32), pltpu.VMEM((t, D), F32),
                        pltpu.VMEM((t, 128), F32), pltpu.VMEM((128, t), F32),
                        pltpu.VMEM((t, 128), F32), pltpu.VMEM((128, t), F32)],
        compiler_params=_params(("arbitrary",)),
    )(dy, xs_a, bc_a, p, states, bias128, alog128, dskip_x)


def _gate_lanes(shape):
    lane = _iota(shape, 1)
    return (lane >= NH) & (lane < 2 * NH)


def _cum_fwd(p, bias128, s):
    tr = min(512, s)

    def body(dtf_ref, bias_ref, o_ref, carry):
        @pl.when(pl.program_id(0) == 0)
        def _():
            carry[...] = jnp.zeros(carry.shape, F32)

        lf = jnp.where(_gate_lanes((tr, 128)), _log_sigmoid(dtf_ref[...] + bias_ref[...]), 0.0)
        cum = _sel_left(_tri_lower(tr), lf) + carry[...]
        carry[...] = cum[tr - 1:tr, :]
        o_ref[...] = cum

    return pl.pallas_call(
        body, name="cum_fwd", grid=(s // tr,),
        in_specs=[pl.BlockSpec((tr, 128), lambda i: (i, OFF_DTF // 128)), pl.BlockSpec((1, 128), lambda i: (0, 0))],
        out_specs=pl.BlockSpec((tr, 128), lambda i: (i, 0)),
        out_shape=jax.ShapeDtypeStruct((s, 128), F32),
        scratch_shapes=[pltpu.VMEM((1, 128), F32)],
        compiler_params=_params(("arbitrary",)),
    )(p, bias128)


def _cum_bwd(dcum, ddt_raw, p, bias128, s):
    tr = min(512, s)

    def fn(pos, dcum, ddt, dtf, bias, carry, acc):
        suffix = _sel_left(_tri_upper(tr), dcum) + carry
        dfr = jnp.where(_gate_lanes((tr, 128)), suffix * _sigmoid(-(dtf + bias)), 0.0)
        out = ddt + dfr
        return out, suffix[0:1, :], acc + _colsum(out)

    return _rowk("cum_bwd", fn, s, tr, [(dcum, 128, 0, 0), (ddt_raw, 128, 0, 0), (p, 128, OFF_DTF // 128, 0)],
                 [bias128], [(128, BF16)], [(1, 128), (1, 128)], reverse=True)


ATT_BLOCK = 512
ATT_STRIP = 32


def _head_part(shape, h, dim):
    i = _iota(shape, dim)
    return (i >= h * HD) & (i < (h + 1) * HD)


def _k_augmented(k_blk, cum_blk, j, h):
    tk = k_blk.shape[0]
    lane = _iota((tk, 128), 1)
    col = jnp.sum(jnp.where(lane == NH + 2 * j + h, cum_blk, 0.0), axis=1, keepdims=True)
    c0, c1, c2 = [c.astype(F32) for c in _split3(-col)]
    k_h = k_blk if h == 0 else pltpu.roll(k_blk, HD, 1)
    aug = jnp.where(lane == HD, c0, jnp.where(lane == HD + 1, c1, jnp.where(lane == HD + 2, c2, 0.0)))
    return jnp.where(lane < HD, k_h, aug).astype(BF16)


def _q_augmented_t(q_blk):
    tq = q_blk.shape[0]
    q_t = (q_blk * ATT_SCALE).T.astype(BF16)
    ones = (_iota((HD, tq), 0) < 3).astype(BF16)
    return [jnp.concatenate([q_t[h * HD:(h + 1) * HD], ones], axis=0) for h in range(2)]


def _rows01(r0, r1):
    sub = _iota((8, r0.shape[1]), 0)
    return jnp.where(sub == 0, r0, jnp.where(sub == 1, r1, 0.0))


def _fold8(x, op, cur):
    for g in range(x.shape[0] // 8):
        cur = op(cur, x[8 * g:8 * (g + 1), :])
    return cur


def _attn_fwd(p, cum, s):
    t = min(ATT_BLOCK, s)
    nq = s // t
    r = ATT_STRIP

    def body(q_ref, k_ref, v_ref, c_ref, o_ref, lse_ref, kaug_sc, vt_sc, s0_sc, s1_sc, p0_sc, p1_sc, m_sc, l_sc, acc_sc):
        j, qi = pl.program_id(0), pl.program_id(1)
        s_sc, p_sc = (s0_sc, s1_sc), (p0_sc, p1_sc)

        @pl.when(qi == 0)
        def _():
            for c in range(nq):
                rows = slice(c * t, (c + 1) * t)
                k_blk, vt = k_ref[rows, :], v_ref[rows, :].T
                for h in range(2):
                    kaug_sc[h, rows, :] = _k_augmented(k_blk, c_ref[rows, :], j, h)
                    vt_sc[h, :, rows] = vt[h * HD:(h + 1) * HD].astype(BF16)

        qaug_t = _q_augmented_t(q_ref[...])
        m_sc[...] = jnp.full(m_sc.shape, -1e30, F32)
        l_sc[...] = jnp.zeros(l_sc.shape, F32)
        acc_sc[...] = jnp.zeros(acc_sc.shape, F32)
        top = _iota((128, t), 0) < HD

        def logits(kb, buf):
            kv = pl.ds(pl.multiple_of(kb * t, t), t)
            for h in range(2):
                s_sc[buf][h] = _dot(kaug_sc[h, kv, :], qaug_t[h])

        def softmax(buf, diagonal):
            alphas = []
            for h in range(2):
                cur = jnp.full((8, t), -1e30, F32)
                for i in range(t // r):
                    rows = slice(i * r, (i + 1) * r)
                    x = s_sc[buf][h, rows, :]
                    if diagonal:
                        x = jnp.where(_iota((r, t), 1) >= i * r + _iota((r, t), 0), x, -1e30)
                        s_sc[buf][h, rows, :] = x
                    cur = _fold8(x, jnp.maximum, cur)
                m_prev = m_sc[h, 0:1, :]
                m_new = jnp.maximum(m_prev, jnp.max(cur, axis=0, keepdims=True))
                alpha = jnp.exp(m_prev - m_new)
                m_sc[h, 0:1, :] = m_new
                alphas.append(alpha)
                tot = jnp.zeros((8, t), F32)
                for i in range(t // r):
                    rows = slice(i * r, (i + 1) * r)
                    pr = jnp.exp(s_sc[buf][h, rows, :] - m_new)
                    p_sc[buf][h, rows, :] = pr.astype(BF16)
                    tot = _fold8(pr, jnp.add, tot)
                l_sc[h, 0:1, :] = alpha * l_sc[h, 0:1, :] + jnp.sum(tot, axis=0, keepdims=True)
            return alphas

        def accumulate(kb, buf, alphas):
            kv = pl.ds(pl.multiple_of(kb * t, t), t)
            for h in range(2):
                part = slice(h * HD, (h + 1) * HD)
                acc_sc[part, :] = acc_sc[part, :] * alphas[h] + _dot(vt_sc[h, :, kv], p_sc[buf][h])

        def first_trip():
            logits(0, 1)
            accumulate(qi, 0, softmax(0, True))
            logits(jnp.minimum(1, qi - 1), 0)
            return tuple(softmax(1, False))

        def only_diagonal():
            accumulate(qi, 0, softmax(0, True))
            return (jnp.ones((1, t), F32),) * 2

        def steady(u, alphas_b):
            accumulate(2 * u - 2, 1, alphas_b)
            logits(2 * u, 1)
            accumulate(2 * u - 1, 0, softmax(0, False))
            logits(jnp.minimum(2 * u + 1, qi - 1), 0)
            return tuple(softmax(1, False))

        logits(qi, 0)
        n_blocks = qi + 1
        alphas_b = lax.cond(qi >= 1, first_trip, only_diagonal)
        alphas_b = lax.fori_loop(1, n_blocks // 2, steady, alphas_b)
        last_b = 2 * (n_blocks // 2) - 2

        @pl.when((qi >= 1) & (n_blocks % 2 == 0))
        def _():
            accumulate(last_b, 1, alphas_b)

        @pl.when((qi >= 2) & (n_blocks % 2 == 1))
        def _():
            accumulate(last_b, 1, alphas_b)
            accumulate(qi - 1, 0, softmax(0, False))

        l0, l1 = l_sc[0, 0:1, :], l_sc[1, 0:1, :]
        o_ref[...] = (acc_sc[...] / jnp.where(top, l0, l1)).T
        lse_ref[0] = _rows01(m_sc[0, 0:1, :] + jnp.log(l0), m_sc[1, 0:1, :] + jnp.log(l1))

    return pl.pallas_call(
        body, name="attn_fwd", grid=(NH // 2, nq),
        in_specs=[pl.BlockSpec((t, 128), lambda j, qi: (qi, OFF_Q // 128 + j)),
                  pl.BlockSpec((s, 128), lambda j, qi: (0, OFF_K // 128 + j)),
                  pl.BlockSpec((s, 128), lambda j, qi: (0, OFF_V // 128 + j)),
                  pl.BlockSpec((s, 128), lambda j, qi: (0, 0))],
        out_specs=[pl.BlockSpec((t, 128), lambda j, qi: (qi, j)),
                   pl.BlockSpec((1, 8, t), lambda j, qi: (j, 0, qi))],
        out_shape=[jax.ShapeDtypeStruct((s, D), F32), jax.ShapeDtypeStruct((NH // 2, 8, s), F32)],
        scratch_shapes=[pltpu.VMEM((2, s, 128), BF16), pltpu.VMEM((2, HD, s), BF16), pltpu.VMEM((2, t, t), F32),
                        pltpu.VMEM((2, t, t), F32), pltpu.VMEM((2, t, t), BF16), pltpu.VMEM((2, t, t), BF16),
                        pltpu.VMEM((2, 8, t), F32), pltpu.VMEM((2, 8, t), F32), pltpu.VMEM((128, t), F32)],
        compiler_params=_params(("parallel", "arbitrary")),
    )(p, p, p, cum)


def _attn_bwd(p, cum, o, lse, do, s):
    t = min(ATT_BLOCK, s)
    nq = s // t
    r = ATT_STRIP

    def body(q_ref, k_ref, v_ref, c_ref, o_ref, lse_ref, do_ref, dq_ref, dk_ref, dv_ref, dc_ref, dr_ref,
             qaugt_sc, qh_sc, dot_sc, doh_sc, delta_sc, dqt_sc, dr_sc, kaug_sc, vh_sc, kt_sc,
             s0_sc, s1_sc, dp0_sc, dp1_sc, p0_sc, p1_sc, ds0_sc, ds1_sc, dk_sc, dv_sc, dc_sc):
        j, ki = pl.program_id(0), pl.program_id(1)
        s_sc, dp_sc, p_sc, ds_sc = (s0_sc, s1_sc), (dp0_sc, dp1_sc), (p0_sc, p1_sc), (ds0_sc, ds1_sc)

        @pl.when(ki == 0)
        def _():
            for c in range(nq):
                rows = slice(c * t, (c + 1) * t)
                q_blk, do_blk = q_ref[rows, :], do_ref[rows, :]
                qaugt_sc[0, :, rows], qaugt_sc[1, :, rows] = _q_augmented_t(q_blk)
                dot_sc[:, rows] = do_blk.T.astype(BF16)
                prod_t = (do_blk * o_ref[rows, :]).T
                delta_sc[:, rows] = _rows01(jnp.sum(prod_t[0:HD], axis=0, keepdims=True),
                                            jnp.sum(prod_t[HD:], axis=0, keepdims=True))
                for h in range(2):
                    head = _head_part((t, 128), h, 1)
                    qh_sc[h, rows, :] = jnp.where(head, q_blk * ATT_SCALE, 0.0).astype(BF16)
                    doh_sc[h, rows, :] = jnp.where(head, do_blk, 0.0).astype(BF16)
            dqt_sc[...] = jnp.zeros(dqt_sc.shape, F32)
            dr_sc[...] = jnp.zeros(dr_sc.shape, F32)

        k_blk, v_blk = k_ref[...], v_ref[...]
        kt = k_blk.T
        for h in range(2):
            kaug_sc[h] = _k_augmented(k_blk, c_ref[...], j, h)
            vh_sc[h] = jnp.where(_head_part((t, 128), h, 1), v_blk, 0.0).astype(BF16)
            kt_sc[h] = kt[h * HD:(h + 1) * HD].astype(BF16)
        dk_sc[...] = jnp.zeros(dk_sc.shape, F32)
        dv_sc[...] = jnp.zeros(dv_sc.shape, F32)
        dc_sc[...] = jnp.zeros(dc_sc.shape, F32)

        def inputs(qb, buf):
            qs = pl.ds(pl.multiple_of(qb * t, t), t)
            for h in range(2):
                s_sc[buf][h] = _dot(kaug_sc[h], qaugt_sc[h, :, qs])
                dp_sc[buf][h] = _dot(vh_sc[h], dot_sc[:, qs])

        def elementwise(qb, buf, diagonal):
            qs = pl.ds(pl.multiple_of(qb * t, t), t)
            for h in range(2):
                lse_row, delta_row = lse_ref[0, h:h + 1, qs], delta_sc[h:h + 1, qs]
                tot = jnp.zeros((8, t), F32)
                for i in range(t // r):
                    rows = slice(i * r, (i + 1) * r)
                    x = s_sc[buf][h, rows, :]
                    if diagonal:
                        x = jnp.where(_iota((r, t), 1) >= i * r + _iota((r, t), 0), x, -1e30)
                    pr = jnp.exp(x - lse_row)
                    ds = pr * (dp_sc[buf][h, rows, :] - delta_row)
                    p_sc[buf][h, rows, :] = pr.astype(BF16)
                    ds_sc[buf][h, rows, :] = ds.astype(BF16)
                    dc_sc[h, rows, :] += sum(ds[:, 128 * g:128 * (g + 1)] for g in range(t // 128))
                    tot = _fold8(ds, jnp.add, tot)
                dr_sc[h, :, qs] += tot

        def outputs(qb, buf):
            qs = pl.ds(pl.multiple_of(qb * t, t), t)
            dv_sc[...] += _dot(p_sc[buf][0], doh_sc[0, qs, :]) + _dot(p_sc[buf][1], doh_sc[1, qs, :])
            dk_sc[...] += _dot(ds_sc[buf][0], qh_sc[0, qs, :]) + _dot(ds_sc[buf][1], qh_sc[1, qs, :])
            for h in range(2):
                dqt_sc[h * HD:(h + 1) * HD, qs] += _dot(kt_sc[h], ds_sc[buf][h])

        def pair(a, b, a_diagonal):
            inputs(a, 0)
            inputs(b, 1)
            elementwise(a, 0, a_diagonal)
            outputs(a, 0)
            elementwise(b, 1, False)
            outputs(b, 1)

        def later(u, carry):
            pair(ki + 1 + 2 * u, ki + 2 + 2 * u, False)
            return carry

        n_later = nq - 1 - ki
        lax.fori_loop(0, n_later // 2, later, 0)

        @pl.when(n_later % 2 == 1)
        def _():
            pair(ki, nq - 1, True)

        @pl.when(n_later % 2 == 0)
        def _():
            inputs(ki, 0)
            elementwise(ki, 0, True)
            outputs(ki, 0)

        dk_ref[...] = dk_sc[...].astype(BF16)
        dv_ref[...] = dv_sc[...].astype(BF16)
        lane = _iota((t, 128), 1)
        cols = jnp.where(lane == 0, jnp.sum(dc_sc[0], axis=1, keepdims=True),
                         jnp.where(lane == 1, jnp.sum(dc_sc[1], axis=1, keepdims=True), 0.0))
        dc_ref[0] = cols.T[0:8, :]

        @pl.when(ki == nq - 1)
        def _():
            for c in range(nq):
                rows = slice(c * t, (c + 1) * t)
                dq_ref[rows, :] = dqt_sc[:, rows].T * ATT_SCALE
            dr_ref[0] = _rows01(jnp.sum(dr_sc[0], axis=0, keepdims=True), jnp.sum(dr_sc[1], axis=0, keepdims=True))

    whole = lambda off: pl.BlockSpec((s, 128), functools.partial(lambda j, ki, off: (0, off + j), off=off))
    return pl.pallas_call(
        body, name="attn_bwd", grid=(NH // 2, nq),
        in_specs=[whole(OFF_Q // 128),
                  pl.BlockSpec((t, 128), lambda j, ki: (ki, OFF_K // 128 + j)),
                  pl.BlockSpec((t, 128), lambda j, ki: (ki, OFF_V // 128 + j)),
                  pl.BlockSpec((t, 128), lambda j, ki: (ki, 0)),
                  whole(0),
                  pl.BlockSpec((1, 8, s), lambda j, ki: (j, 0, 0)),
                  whole(0)],
        out_specs=[whole(0),
                   pl.BlockSpec((t, 128), lambda j, ki: (ki, j)),
                   pl.BlockSpec((t, 128), lambda j, ki: (ki, j)),
                   pl.BlockSpec((1, 8, t), lambda j, ki: (j, 0, ki)),
                   pl.BlockSpec((1, 8, s), lambda j, ki: (j, 0, 0))],
        out_shape=[jax.ShapeDtypeStruct((s, D), F32), jax.ShapeDtypeStruct((s, D), BF16), jax.ShapeDtypeStruct((s, D), BF16),
                   jax.ShapeDtypeStruct((NH // 2, 8, s), F32), jax.ShapeDtypeStruct((NH // 2, 8, s), F32)],
        scratch_shapes=[pltpu.VMEM((2, 128, s), BF16), pltpu.VMEM((2, s, 128), BF16), pltpu.VMEM((128, s), BF16),
                        pltpu.VMEM((2, s, 128), BF16), pltpu.VMEM((8, s), F32), pltpu.VMEM((128, s), F32),
                        pltpu.VMEM((2, 8, s), F32), pltpu.VMEM((2, t, 128), BF16), pltpu.VMEM((2, t, 128), BF16),
                        pltpu.VMEM((2, HD, t), BF16)]
        + [pltpu.VMEM((2, t, t), F32)] * 4 + [pltpu.VMEM((2, t, t), BF16)] * 4
        + [pltpu.VMEM((t, 128), F32), pltpu.VMEM((t, 128), F32), pltpu.VMEM((2, t, 128), F32)],
        compiler_params=_params(("parallel", "arbitrary")),
    )(p, p, p, cum, o, lse, do)


def _ln_stats(u):
    mu = _mean(u)
    d = u - mu
    rstd = lax.rsqrt(_mean(d * d) + EPS)
    return d * rstd, rstd


def _ln_bwd(dx, xh, rstd, gam):
    dxh = dx * gam
    return rstd * (dxh - _mean(dxh) - xh * _mean(dxh * xh))


def _rms_bwd(d, xn, r, w):
    t = d * w
    return r * (t - xn * _mean(t * xn)), _colsum(d * xn)


def _mix_norm(y, p, att, w_ssm, w_att, s):
    def fn(pos, y, z, att, w1, w2):
        g = y * _silu(z)
        n1 = g * lax.rsqrt(_mean(g * g) + EPS) * w1
        n2 = att * lax.rsqrt(_mean(att * att) + EPS) * w2
        return (jnp.concatenate([n1, n2], axis=1),)

    return _rowk("mix_norm", fn, s, 256, [(y, D, 0, 0), (p, D, OFF_Z // D, 0), (att, D, 0, 0)],
                 [w_ssm, w_att], [(2 * D, BF16)], [])[0]


def _mix_norm_bwd(dmix, y, p, att, w_ssm, w_att, s):
    def fn(pos, dmix, y, z, att, w1, w2, a1, a2):
        sz = _silu(z)
        g = y * sz
        r1 = lax.rsqrt(_mean(g * g) + EPS)
        dg, dw1 = _rms_bwd(dmix[:, :D], g * r1, r1, w1)
        r2 = lax.rsqrt(_mean(att * att) + EPS)
        datt, dw2 = _rms_bwd(dmix[:, D:], att * r2, r2, w2)
        return dg * sz, dg * y * _dsilu(z), datt, a1 + dw1, a2 + dw2

    return _rowk("mix_norm_bwd", fn, s, 256, [(dmix, 2 * D, 0, 0), (y, D, 0, 0), (p, D, OFF_Z // D, 0), (att, D, 0, 0)],
                 [w_ssm, w_att], [(D, F32), (D, BF16), (D, F32)], [(1, D), (1, D)])


def _ln1(x0, y, g1, gam, bet, sc2, sh2, s):
    def fn(pos, x0, y, g1, gam, bet, sc2, sh2):
        xh, _ = _ln_stats(ALPHA * x0 + (1.0 + g1) * y)
        x1 = xh * gam + bet
        return x1, _modulate(x1, sc2, sh2)

    return _rowk("ln1", fn, s, 256, [(x0, D, 0, 0), (y, D, 0, 0)], [g1, gam, bet, sc2, sh2], [(D, F32), (D, BF16)], [])


def _ln2_loss(x1, ff, tgt, g2, gam, bet, s):
    def fn(pos, x1, ff, tgt, g2, gam, bet, a_loss, a_dgam, a_dbet, a_dg2):
        xh, rstd = _ln_stats(ALPHA * x1 + (1.0 + g2) * ff)
        err = xh * gam + bet - tgt
        dx2 = err * (1.0 / D)
        du = _ln_bwd(dx2, xh, rstd, gam)
        return (du, du * (1.0 + g2), a_loss + _colsum(err * err), a_dgam + _colsum(dx2 * xh),
                a_dbet + _colsum(dx2), a_dg2 + _colsum(du * ff))

    return _rowk("ln2_loss", fn, s, 256, [(x1, D, 0, 0), (ff, D, 0, 0), (tgt, D, 0, 0)], [g2, gam, bet],
                 [(D, F32), (D, BF16)], [(1, D)] * 4)


def _ln1_bwd(dh2, du2, x0, y, g1, gam, bet, sc2, s):
    def fn(pos, dh2, du2, x0, y, g1, gam, bet, sc2, a_sc, a_sh, a_gam, a_bet, a_g1):
        xh, rstd = _ln_stats(ALPHA * x0 + (1.0 + g1) * y)
        x1 = xh * gam + bet
        dx1 = ALPHA * du2 + dh2 * (1.0 + sc2)
        du1 = _ln_bwd(dx1, xh, rstd, gam)
        return (du1, du1 * (1.0 + g1), a_sc + _colsum(dh2 * x1), a_sh + _colsum(dh2), a_gam + _colsum(dx1 * xh),
                a_bet + _colsum(dx1), a_g1 + _colsum(du1 * y))

    return _rowk("ln1_bwd", fn, s, 256, [(dh2, D, 0, 0), (du2, D, 0, 0), (x0, D, 0, 0), (y, D, 0, 0)],
                 [g1, gam, bet, sc2], [(D, F32), (D, BF16)], [(1, D)] * 5)


def _input_grad(dh1, du1, x0, sc1, s):
    def fn(pos, dh1, du1, x0, sc1, a_sc, a_sh):
        return ALPHA * du1 + dh1 * (1.0 + sc1), a_sc + _colsum(dh1 * x0), a_sh + _colsum(dh1)

    return _rowk("input_grad", fn, s, 256, [(dh1, D, 0, 0), (du1, D, 0, 0), (x0, D, 0, 0)], [sc1],
                 [(D, F32)], [(1, D)] * 2)


def _adamw_math(w, grad, m, v):
    m_new = ADAM_B1 * m + (1.0 - ADAM_B1) * grad
    v_new = ADAM_B2 * v + (1.0 - ADAM_B2) * (grad * grad)
    m_hat = m_new / (1.0 - ADAM_B1 ** ADAM_STEP)
    v_hat = v_new / (1.0 - ADAM_B2 ** ADAM_STEP)
    return -ADAM_LR * (m_hat / (jnp.sqrt(v_hat) + ADAM_EPS) + ADAM_WD * w), m_new, v_new


def _small_update(small_all, layout, w, m, v):
    names = [n for n, _, _ in layout]

    def body(*refs):
        all_ref = refs[0]
        w_refs, m_refs, v_refs = [refs[1 + k * len(names):1 + (k + 1) * len(names)] for k in range(3)]
        sum_ref = refs[1 + 3 * len(names)]
        outs = refs[2 + 3 * len(names):]
        total = all_ref[0]
        for k in range(1, N_DEV):
            total = total + all_ref[k]
        sum_ref[...] = total
        for i, (_, off, size) in enumerate(layout):
            grad = total[:, off:off + size]
            delta, m_new, v_new = _adamw_math(w_refs[i][...], grad, m_refs[i][...], v_refs[i][...])
            for o, val in zip(outs[4 * i:4 * i + 4], (grad, delta, m_new, v_new)):
                o[...] = val

    res = pl.pallas_call(
        body, name="small_update",
        out_shape=[jax.ShapeDtypeStruct(small_all.shape[1:], F32)]
        + [jax.ShapeDtypeStruct(w[n].shape, F32) for n in names for _ in range(4)],
        compiler_params=_params(None),
    )(small_all, *[w[n] for n in names], *[m[n] for n in names], *[v[n] for n in names])
    return res[0], {n: res[1 + 4 * i:5 + 4 * i] for i, n in enumerate(names)}


def _adamw(name, w, g, m, v, *, tr, slots):
    r, c = w.shape

    def body(w_ref, g_ref, m_ref, v_ref, g_out, d_out, m_out, v_out):
        if slots:
            grad = g_ref[0][:, :c].astype(F32)
            for k in range(1, N_DEV):
                grad = grad + g_ref[k][:, :c].astype(F32)
        else:
            grad = g_ref[...]
        g_out[...] = grad
        d_out[...], m_out[...], v_out[...] = _adamw_math(w_ref[...], grad, m_ref[...], v_ref[...])

    tile = pl.BlockSpec((tr, c), lambda i: (i, 0))
    g_spec = pl.BlockSpec((N_DEV, tr, g.shape[-1]), lambda i: (0, i, 0)) if slots else tile
    return pl.pallas_call(
        body, name=name, grid=(r // tr,),
        in_specs=[tile, g_spec, tile, tile], out_specs=[tile] * 4,
        out_shape=[jax.ShapeDtypeStruct((r, c), F32)] * 4,
        compiler_params=_params(("parallel",)),
    )(w, g, m, v)


def _dot_f32(a, b, dims=NN):
    a0, a1, a2 = _split3(a)
    b0, b1, b2 = _split3(b)
    acc = _dot(a0, b0, dims)
    for x, y in ((a0, b1), (a1, b0), (a1, b1), (a0, b2), (a2, b0)):
        acc = acc + _dot(x, y, dims)
    return acc


def _ada_mod(c_all, w_shard, b_shard):
    def body(c_ref, w_ref, b_ref, o_ref):
        act = _silu(c_ref[...])
        act16 = jnp.concatenate([act, jnp.zeros_like(act)], axis=0)
        o_ref[...] = _dot_f32(act16, w_ref[...])[0:N_DEV] + b_ref[...]

    return pl.pallas_call(
        body, name="ada_mod", out_shape=jax.ShapeDtypeStruct((N_DEV, w_shard.shape[1]), F32),
        compiler_params=_params(None),
    )(c_all, w_shard, b_shard)


def _ada_grad(c_all, dmod_cols):
    def body(c_ref, dc_ref, gw_ref):
        act = _silu(c_ref[...])
        act16 = jnp.concatenate([act, jnp.zeros_like(act)], axis=0)
        dm = dc_ref[...]
        dm16 = jnp.concatenate([dm, jnp.zeros_like(dm)], axis=0)
        gw_ref[...] = _dot_f32(act16, dm16, TN)

    return pl.pallas_call(
        body, name="ada_grad", out_shape=jax.ShapeDtypeStruct((D, dmod_cols.shape[1]), F32),
        compiler_params=_params(None),
    )(c_all, dmod_cols)


def _exchange(name, xs, scatter):
    n = len(xs)
    n_peer = N_DEV - 1

    def body(*refs):
        x_refs, o_refs = refs[:n], refs[n:2 * n]
        send_sems, recv_sems, local_sems = refs[2 * n:]
        mx, my, mc = lax.axis_index("x"), lax.axis_index("y"), lax.axis_index("c")
        me = 4 * mx + 2 * my + mc

        def src(a, slot):
            return x_refs[a].at[slot] if scatter else x_refs[a]

        own = [pltpu.make_async_copy(src(a, me), o_refs[a].at[me], local_sems.at[a]) for a in range(n)]
        for cp in own:
            cp.start()
        sends = []
        for d in range(1, N_DEV):
            px = 1 - mx if d & 4 else mx
            py = 1 - my if d & 2 else my
            pc = 1 - mc if d & 1 else mc
            peer = 4 * px + 2 * py + pc
            for a in range(n):
                def copy(src_slot, dst_slot, a=a, d=d, to=(px, py, pc)):
                    return pltpu.make_async_remote_copy(
                        src_ref=src(a, src_slot), dst_ref=o_refs[a].at[dst_slot],
                        send_sem=send_sems.at[a * n_peer + d - 1], recv_sem=recv_sems.at[a * n_peer + d - 1],
                        device_id=to, device_id_type=pl.DeviceIdType.MESH)

                out = copy(peer, me)
                out.start()
                sends.append((out, copy(me, peer)))
        for _, arrival in sends:
            arrival.wait_recv()
        for out, _ in sends:
            out.wait_send()
        for cp in own:
            cp.wait()

    shapes = [tuple(x.shape[1:] if scatter else x.shape) for x in xs]
    return pl.pallas_call(
        body, name=name,
        in_specs=[pl.BlockSpec(memory_space=pl.ANY)] * n, out_specs=[pl.BlockSpec(memory_space=pl.ANY)] * n,
        out_shape=[jax.ShapeDtypeStruct((N_DEV,) + sh, x.dtype) for sh, x in zip(shapes, xs)],
        scratch_shapes=[pltpu.SemaphoreType.DMA((n * n_peer,)), pltpu.SemaphoreType.DMA((n * n_peer,)),
                        pltpu.SemaphoreType.DMA((n,))],
        compiler_params=pltpu.CompilerParams(has_side_effects=True),
    )(*xs)


def _gather_two_level(name, x):
    def body(x_ref, o_ref, send_sems, recv_sems, local_sem):
        mx, my, mc = lax.axis_index("x"), lax.axis_index("y"), lax.axis_index("c")
        me, sibling = (mx, my, mc), (mx, my, 1 - mc)
        chips = [(1 - mx, my), (mx, 1 - my), (1 - mx, 1 - my)]

        def slot(px, py, pc):
            return o_ref.at[4 * px + 2 * py + pc]

        def copy(k, block, to, src=None):
            return pltpu.make_async_remote_copy(
                src_ref=slot(*block) if src is None else src, dst_ref=slot(*block),
                send_sem=send_sems.at[k], recv_sem=recv_sems.at[k], device_id=to, device_id_type=pl.DeviceIdType.MESH)

        mine = pltpu.make_async_copy(x_ref, slot(*me), local_sem)
        mine.start()
        first = [copy(0, me, sibling, src=x_ref)] + [copy(1 + i, me, (*chip, mc), src=x_ref) for i, chip in enumerate(chips)]
        for cp in first:
            cp.start()
        passed = [copy(4 + i, (*chip, mc), sibling) for i, chip in enumerate(chips)]
        for i, chip in enumerate(chips):
            copy(1 + i, (*chip, mc), me).wait_recv()
            passed[i].start()
        copy(0, sibling, me).wait_recv()
        for i, chip in enumerate(chips):
            copy(4 + i, (*chip, 1 - mc), me).wait_recv()
        for cp in first + passed:
            cp.wait_send()
        mine.wait()

    return pl.pallas_call(
        body, name=name,
        in_specs=[pl.BlockSpec(memory_space=pl.ANY)], out_specs=pl.BlockSpec(memory_space=pl.ANY),
        out_shape=jax.ShapeDtypeStruct((N_DEV,) + tuple(x.shape), x.dtype),
        scratch_shapes=[pltpu.SemaphoreType.DMA((7,)), pltpu.SemaphoreType.DMA((7,)), pltpu.SemaphoreType.DMA(())],
        compiler_params=pltpu.CompilerParams(has_side_effects=True),
    )(x)


def _after(x, zero):
    return x if zero is None else x + zero.reshape(-1)[0].astype(x.dtype)


def _exchange_copies(x_refs, land_refs, send_sems, recv_sems, scatter):
    n = len(x_refs)
    n_peer = N_DEV - 1
    mx, my, mc = lax.axis_index("x"), lax.axis_index("y"), lax.axis_index("c")
    me = 4 * mx + 2 * my + mc
    pairs = []
    for d in range(1, N_DEV):
        px = 1 - mx if d & 4 else mx
        py = 1 - my if d & 2 else my
        pc = 1 - mc if d & 1 else mc
        peer = 4 * px + 2 * py + pc
        for a in range(n):
            def copy(src_slot, dst_slot, a=a, d=d, to=(px, py, pc)):
                return pltpu.make_async_remote_copy(
                    src_ref=x_refs[a].at[src_slot] if scatter else x_refs[a], dst_ref=land_refs[a].at[dst_slot],
                    send_sem=send_sems.at[a * n_peer + d - 1], recv_sem=recv_sems.at[a * n_peer + d - 1],
                    device_id=to, device_id_type=pl.DeviceIdType.MESH)

            pairs.append((copy(peer, me), copy(me, peer)))
    return me, pairs


def _exchange_async(name, xs, scatter, collective_id):
    n = len(xs)
    shapes = [tuple(x.shape[1:] if scatter else x.shape) for x in xs]
    x_refs = [jax.new_ref(x, memory_space=pltpu.MemorySpace.HBM) for x in xs]
    land_refs = [jax.empty_ref(jax.ShapeDtypeStruct((N_DEV,) + sh, x.dtype), memory_space=pltpu.MemorySpace.HBM)
                 for sh, x in zip(shapes, xs)]

    @pl.kernel(mesh=plsc.ScalarSubcoreMesh(axis_name="sequencer", num_cores=1), name=name,
               scratch_types=(pltpu.SemaphoreType.DMA((n * (N_DEV - 1),)), pltpu.SemaphoreType.DMA((n * (N_DEV - 1),)),
                              pltpu.SemaphoreType.DMA((n,))),
               compiler_params=pltpu.CompilerParams(collective_id=collective_id))
    def launch(send_sems, recv_sems, own_sems):
        barrier = pltpu.get_barrier_semaphore()
        mx, my, mc = lax.axis_index("x"), lax.axis_index("y"), lax.axis_index("c")
        for d in range(1, N_DEV):
            peer = (1 - mx if d & 4 else mx, 1 - my if d & 2 else my, 1 - mc if d & 1 else mc)
            pl.semaphore_signal(barrier, inc=1, device_id=peer, device_id_type=pl.DeviceIdType.MESH)
        pl.semaphore_wait(barrier, N_DEV - 1)
        me, pairs = _exchange_copies(x_refs, land_refs, send_sems, recv_sems, scatter)
        own = [pltpu.make_async_copy(x_refs[a].at[me] if scatter else x_refs[a], land_refs[a].at[me], own_sems.at[a])
               for a in range(n)]
        for cp in own:
            cp.start()
        for out, _ in pairs:
            out.start()
        for out, arrival in pairs:
            arrival.wait_recv()
            out.wait_send()
        for cp in own:
            cp.wait()

    launch()
    return lambda: [r[...] for r in land_refs]


def _relu2(a):
    r = jnp.maximum(a, 0.0)
    return r * r


def _relu2_grad(acc, r):
    return acc * (2.0 * jnp.sqrt(r.astype(F32)))


def _local_step(x0, tgt, mod, wcat, late_weights, send_grads, conv_w, conv_b, dt_bias, a_log, d_skip, ssm_norm_w, f_bias,
                attn_norm_w, ln1_g, ln1_b, ln2_g, ln2_b):
    ff_w = DFF // N_DEV
    s = x0.shape[0]
    tm = min(1024, s)
    ts = min(1024, s)
    sh1, sc1, g1, sh2, sc2, g2 = [mod[:, i * D:(i + 1) * D] for i in range(6)]
    zero = jnp.zeros((1, 128 - 2 * NH), F32)
    bias128 = jnp.concatenate([dt_bias, f_bias, zero], axis=1)
    alog128 = jnp.concatenate([a_log, jnp.zeros((1, 128 - NH), F32)], axis=1)
    dskip_x = jnp.repeat(d_skip, HD, axis=1)
    w_xs, w_bc, b_xs, b_bc = conv_w[:, :D], conv_w[:, D:], conv_b[:, :D], conv_b[:, D:]

    p = _mm_nn("in_proj", x0, wcat, tm=tm, tn=1152, tk=D, out_dtype=F32, pro=_modulate, aux=(sc1, sh1))
    xs_a, bc_a = _conv_fwd(p, w_xs, b_xs, w_bc, b_bc, s)
    y_ssd, states = _ssd_fwd(xs_a, bc_a, p, bias128, alog128, dskip_x, s)
    cum = _cum_fwd(p, bias128, s)
    att, lse = _attn_fwd(p, cum, s)
    wout, w1s, w2 = late_weights(lse)
    ymix = _mix_norm(y_ssd, p, att, ssm_norm_w, attn_norm_w, s)
    y = _mm_nn("out_proj", ymix, wout, tm=tm, tn=1024, tk=2 * D, out_dtype=F32)
    x1, h2 = _ln1(x0, y, g1, ln1_g, ln1_b, sc2, sh2, s)
    r = _mm_nn("ff_in", h2, w1s, tm=tm, tn=ff_w, tk=D, out_dtype=BF16, epi=_relu2)
    ff = _mm_nn("ff_out", r, w2, tm=tm, tn=1024, tk=1024, out_dtype=F32)
    du2, dff, sq_err, d_ln2_g, d_ln2_b, d_g2 = _ln2_loss(x1, ff, tgt, g2, ln2_g, ln2_b, s)

    da1 = _mm_nt("d_ff_hidden", [(dff, D, 0)], [(w2, D, 0)], n=DFF, tm=tm, tn=1024, out_dtype=BF16, epi=_relu2_grad,
                 epi_aux=(r,))
    d_w2 = _mm_tn("d_w_ff_out", r, dff, tm=1024, tn=1024, ts=ts)
    d_w1s = _mm_tn("d_w_ff_in", h2, da1, tm=1024, tn=ff_w, ts=ts, col_shards=True)
    dh2 = _mm_nt("d_ff_input", [(da1, ff_w, k) for k in range(N_DEV)], [(w1s, ff_w, k) for k in range(N_DEV)], n=D,
                 tm=min(512, s), tn=1024, out_dtype=F32)
    du1, dy, d_sc2, d_sh2, d_ln1_g, d_ln1_b, d_g1 = _ln1_bwd(dh2, du2, x0, y, g1, ln1_g, ln1_b, sc2, s)

    dmix = _mm_nt("d_mix", [(dy, D, 0)], [(wout, D, 0)], n=2 * D, tm=tm, tn=1024, out_dtype=F32)
    d_wout = _mm_tn("d_w_out", ymix, dy, tm=1024, tn=1024, ts=ts)
    sent = send_grads("late", [d_w1s, d_w2.reshape(N_DEV, -1, D), d_wout.reshape(N_DEV, -1, D)])
    dy_ssd, dz, datt, d_ssm_w, d_attn_w = _mix_norm_bwd(dmix, y_ssd, p, att, _after(ssm_norm_w, sent), attn_norm_w, s)
    dq, dk, dv, dcs, drs = _attn_bwd(p, cum, att, lse, datt, s)
    dxs_a, dbc_a, ddt_raw, d_alog, d_dskip = _ssd_bwd(dy_ssd, xs_a, bc_a, p, states, bias128, alog128, dskip_x, s)
    dcum = jnp.pad((drs - dcs)[:, :2, :].reshape(NH, s).T, ((0, 0), (NH, 128 - 2 * NH)))
    ddtf, _, d_bias = _cum_bwd(dcum, ddt_raw, p, bias128, s)
    dxs, dbc, d_wc_xs, d_bc_xs, d_wc_bc, d_bc_bc = _conv_bwd(dxs_a, dbc_a, p, w_xs, b_xs, w_bc, b_bc, s)

    segs = [(dz, OFF_Z, D), (dxs, OFF_XS, D), (dq, OFF_Q, D), (dk, OFF_K, D), (dv, OFF_V, D), (dbc, OFF_BC, 512),
            (ddtf, OFF_DTF, 128)]
    d_z, d_xs, d_q, d_k, d_v, d_bcw, d_dtf = [
        _mm_tn("d_w_in_%d" % i, x0, a, tm=1024, tn=min(w, 1024), ts=ts, pro=_modulate, aux=(sc1, sh1))
        for i, (a, _, w) in enumerate(segs)]
    d_w_in = dict(z=d_z, xs=d_xs, bc=d_bcw, dt=d_dtf[:, :NH], q=d_q, k=d_k, v=d_v, f=d_dtf[:, NH:2 * NH])
    sent = send_grads("in", [_shard_w_in_grad(d_w_in)])
    segs[-1] = (_after(ddtf, sent), OFF_DTF, 128)
    dh1 = _mm_nt("d_h1", [(a, w, 0) for a, _, w in segs], [(wcat, w, off // w) for _, off, w in segs], n=D,
                 tm=min(512, s), tn=1024, out_dtype=F32)
    grad_x, d_sc1, d_sh1 = _input_grad(dh1, du1, x0, sc1, s)

    return dict(
        loss=(0.5 / D) * jnp.sum(sq_err), grad_x=grad_x,
        d_mod=jnp.concatenate([d_sh1, d_sc1, d_g1, d_sh2, d_sc2, d_g2], axis=1),
        d_conv_w=jnp.concatenate([d_wc_xs[:4], d_wc_bc[:4]], axis=1), d_conv_b=jnp.concatenate([d_bc_xs, d_bc_bc], axis=1),
        d_ssm_norm_w=d_ssm_w, d_attn_norm_w=d_attn_w, d_ln1_g=d_ln1_g, d_ln1_b=d_ln1_b, d_ln2_g=d_ln2_g, d_ln2_b=d_ln2_b,
        d_gate_bias=d_bias, d_a_log=d_alog, d_d_skip=d_dskip)


W_IN_SEGS = [('z', W_Z, D), ('xs', W_XS, D), ('bc', W_BC, 512), ('dt', W_DT, NH), ('q', W_Q, D), ('k', W_K, D),
             ('v', W_V, D), ('f', W_F, NH)]
SHARD_W = IN_COLS // N_DEV


def _pack_w_in(shards):
    def cols(lo, hi):
        pieces = []
        while lo < hi:
            dev = lo // SHARD_W
            end = min(hi, (dev + 1) * SHARD_W)
            pieces.append(shards[dev][:, lo - dev * SHARD_W:end - dev * SHARD_W])
            lo = end
        return pieces

    seg = {n: cols(off, off + w) for n, off, w in W_IN_SEGS}
    pieces = seg['z'] + seg['xs'] + seg['q'] + seg['k'] + seg['v'] + seg['bc'] + seg['dt'] + seg['f']
    return jnp.concatenate(pieces + [jnp.zeros((D, 128 - 2 * NH), shards.dtype)], axis=1)


def _shard_w_in_grad(d_w_in):
    blocks = []
    for dev in range(N_DEV):
        lo, hi = dev * SHARD_W, (dev + 1) * SHARD_W
        pieces = [d_w_in[n][:, max(lo, off) - off:min(hi, off + w) - off] for n, off, w in W_IN_SEGS
                  if max(lo, off) < min(hi, off + w)]
        pieces.append(jnp.zeros((D, -SHARD_W % 128), pieces[0].dtype))
        blocks.append(jnp.concatenate(pieces, axis=1))
    return jnp.stack(blocks, axis=0)


WEIGHTS = ['w_ada', 'b_ada', 'w_in', 'conv_w', 'conv_b', 'dt_bias', 'a_log', 'd_skip', 'ssm_norm_w', 'f_bias',
           'attn_norm_w', 'w_out', 'ln1_g', 'ln1_b', 'w_ff_in', 'w_ff_out', 'ln2_g', 'ln2_b']
BIG = ['w_in', 'w_out', 'w_ff_in', 'w_ff_out']
SMALL_LAYOUT = [('b_ada', 0, 6 * D), ('conv_b', 12288, 1536), ('ssm_norm_w', 13824, D), ('attn_norm_w', 14848, D),
                ('ln1_g', 15872, D), ('ln1_b', 16896, D), ('ln2_g', 17920, D), ('ln2_b', 18944, D),
                ('dt_bias', 19968, NH), ('f_bias', 19968 + NH, NH), ('a_log', 20096, NH), ('d_skip', 20224, NH)]
SMALL_LOSS_LANE = 20352


def _pad_lanes(v, n=128):
    return jnp.pad(v, ((0, 0), (0, n - v.shape[1])))


def kernel(x, c, w_ada, b_ada, w_in, conv_w, conv_b, dt_bias, a_log, d_skip, ssm_norm_w, f_bias, attn_norm_w, w_out, ln1_g, ln1_b, w_ff_in, w_ff_out, ln2_g, ln2_b, loss_target, m_w_ada, m_b_ada, m_w_in, m_conv_w, m_conv_b, m_dt_bias, m_a_log, m_d_skip, m_ssm_norm_w, m_f_bias, m_attn_norm_w, m_w_out, m_ln1_g, m_ln1_b, m_w_ff_in, m_w_ff_out, m_ln2_g, m_ln2_b, v_w_ada, v_b_ada, v_w_in, v_conv_w, v_conv_b, v_dt_bias, v_a_log, v_d_skip, v_ssm_norm_w, v_f_bias, v_attn_norm_w, v_w_out, v_ln1_g, v_ln1_b, v_w_ff_in, v_w_ff_out, v_ln2_g, v_ln2_b):
    args = dict(locals())
    w = {n: args[n] for n in WEIGHTS}
    m = {n: args['m_' + n] for n in WEIGHTS}
    v = {n: args['v_' + n] for n in WEIGHTS}
    me = 4 * lax.axis_index("x") + 2 * lax.axis_index("y") + lax.axis_index("c")
    ada_cols = 6 * D // N_DEV
    conv_cols = conv_w.shape[2]

    c_all, conv_all = _exchange("gather_cond", [c, conv_w[0]], False)
    c_all = c_all.reshape(N_DEV, D)
    conv_w_full = conv_all.transpose(1, 0, 2).reshape(4, N_DEV * conv_cols)
    b_shard = lax.dynamic_slice(b_ada, (0, me * ada_cols), (1, ada_cols))
    mod_all, = _exchange("gather_mod", [_ada_mod(c_all, w_ada[0], b_shard)], False)
    mod = lax.dynamic_index_in_dim(mod_all, me, axis=1, keepdims=False).reshape(1, 6 * D)

    win_s = _gather_two_level("gather_w_in", _after(w_in[0].astype(BF16), mod * 0))
    first_done = win_s[0, 0:1, 0:1] * 0
    rest = _exchange_async("gather_rest", [_after(w[n][0].astype(BF16), first_done) for n in BIG[1:]], False, 1)

    def late_weights(after):
        wout_s, w1s, w2_s = rest()
        return wout_s.reshape(2 * D, D), w1s, w2_s.reshape(DFF, D)

    sends = {}

    def send_grads(tag, blocks):
        sends[tag] = _exchange_async("scatter_" + tag, blocks, True, {'late': 2, 'in': 3}[tag])
        return sum(b.reshape(-1)[0].astype(F32) * 0 for b in blocks)

    out = _local_step(x[0], loss_target[0], mod, _pack_w_in(win_s), late_weights, send_grads,
                      conv_w_full, conv_b, dt_bias, a_log, d_skip, ssm_norm_w, f_bias, attn_norm_w, ln1_g, ln1_b, ln2_g, ln2_b)

    small = jnp.concatenate(
        [out['d_mod'], out['d_conv_w'].reshape(1, -1), out['d_conv_b'], out['d_ssm_norm_w'], out['d_attn_norm_w'],
         out['d_ln1_g'], out['d_ln1_b'], out['d_ln2_g'], out['d_ln2_b'], out['d_gate_bias'], out['d_a_log'],
         out['d_d_skip'], _pad_lanes(out['loss'].reshape(1, 1))], axis=1)
    small_landed = _exchange_async("gather_small", [small], False, 4)
    (g_ff_in, g_ff_out, g_out), (g_in,) = sends['late'](), sends['in']()
    g_parts = dict(w_ff_in=g_ff_in, w_ff_out=g_ff_out, w_out=g_out, w_in=g_in)
    big = {n: _adamw("adamw_" + n, w[n][0], g_parts[n], m[n][0], v[n][0], tr=256, slots=True) for n in BIG}
    big_done = sum(big[n][1][0:1, 0:1] * 0 for n in BIG)
    small_all = _after(small_landed()[0], big_done)
    ssum, small_res = _small_update(small_all, SMALL_LAYOUT, w, m, v)
    dmod_all = small_all[:, 0, :6 * D]
    g_w_ada = _ada_grad(c_all, lax.dynamic_slice(dmod_all, (0, me * ada_cols), (N_DEV, ada_cols)))
    ada = _adamw("adamw_ada", w_ada[0], g_w_ada, m_w_ada[0], v_w_ada[0], tr=256, slots=False)
    g_conv_w = lax.dynamic_slice(ssum[:, 6 * D:6 * D + 4 * N_DEV * conv_cols].reshape(4, N_DEV * conv_cols),
                                 (0, me * conv_cols), (4, conv_cols))
    conv = _adamw("adamw_conv_w", conv_w[0], g_conv_w, m_conv_w[0], v_conv_w[0], tr=4, slots=False)

    results = []
    for k in range(4):
        vals = {n: small_res[n][k] for n in small_res}
        vals['w_ada'], vals['conv_w'] = ada[k][None], conv[k][None]
        for n in BIG:
            vals[n] = big[n][k][None]
        results.append(vals)
    return (ssum[0, SMALL_LOSS_LANE], out['grad_x'][None], *[res[n] for res in results for n in WEIGHTS])
```

```python
import functools

import jax
import jax.numpy as jnp
from jax import lax
from jax.experimental import pallas as pl
from jax.experimental.pallas import tpu as pltpu
from jax.experimental.pallas import tpu_sc as plsc

F32, BF16 = jnp.float32, jnp.bfloat16

N_DEV = 8
D = 1024
NH, HD = 16, 64
NSTATE = 128
CHUNK = 128
HG = 8
DFF = 4096
ALPHA = 2.0 ** 0.25
EPS = 1e-5
ATT_SCALE = HD ** -0.5

OFF_Z, OFF_XS, OFF_Q, OFF_K, OFF_V, OFF_BC, OFF_DTF = 0, 1024, 2048, 3072, 4096, 5120, 5632
PCOLS = 5760
W_Z, W_XS, W_BC, W_DT, W_Q, W_K, W_V, W_F = 0, 1024, 2048, 2560, 2576, 3600, 4624, 5648
IN_COLS = 5664

ADAM_LR, ADAM_B1, ADAM_B2, ADAM_EPS, ADAM_WD, ADAM_STEP = 0.001, 0.9, 0.999, 1e-08, 0.01, 10

VMEM_LIMIT = 56 << 20

NN = (((1,), (0,)), ((), ()))
NT = (((1,), (1,)), ((), ()))
TN = (((0,), (0,)), ((), ()))


def _dot(a, b, dims=NN):
    return lax.dot_general(a, b, dims, preferred_element_type=F32)


def _bdot(a, b, dims=NN):
    return _dot(a.astype(BF16), b.astype(BF16), dims)


def _split3(v, terms=3):
    parts, rest = [], v
    for _ in range(terms):
        p = rest.astype(BF16)
        parts.append(p)
        rest = rest - p.astype(F32)
    return parts


def _sel_left(m01, v):
    return sum(_dot(m01, p) for p in _split3(v))


def _sel_right(v, m01, dims=NN, terms=3):
    return sum(_dot(p, m01, dims) for p in _split3(v, terms))


def _iota(shape, dim):
    return lax.broadcasted_iota(jnp.int32, shape, dim)


def _tri_lower(n):
    return (_iota((n, n), 1) <= _iota((n, n), 0)).astype(BF16)


def _tri_upper(n):
    return (_iota((n, n), 1) >= _iota((n, n), 0)).astype(BF16)


def _head_expand():
    return (lax.shift_right_logical(_iota((128, D), 1), 6) == _iota((128, D), 0)).astype(BF16)


def _head_reduce():
    return (lax.shift_right_logical(_iota((D, 128), 0), 6) == _iota((D, 128), 1)).astype(BF16)


def _sigmoid(x):
    return 1.0 / (1.0 + jnp.exp(-x))


def _silu(x):
    return x * _sigmoid(x)


def _dsilu(x):
    s = _sigmoid(x)
    return s * (1.0 + x * (1.0 - s))


def _softplus(x):
    return jnp.maximum(x, 0.0) + jnp.log(1.0 + jnp.exp(-jnp.abs(x)))


def _log_sigmoid(x):
    return jnp.minimum(x, 0.0) - jnp.log(1.0 + jnp.exp(-jnp.abs(x)))


def _params(sem):
    return pltpu.CompilerParams(dimension_semantics=sem, vmem_limit_bytes=VMEM_LIMIT)


def _mm_nn(name, a, b, *, tm, tn, tk, out_dtype, pro=None, aux=(), epi=None):
    m, k_all = a.shape
    b_sharded = b.ndim == 3
    n = b.shape[0] * b.shape[2] if b_sharded else b.shape[1]
    assert not b_sharded or tn == b.shape[2]
    nk = k_all // tk
    n_aux = len(aux)
    b_spec = (pl.BlockSpec((None, tk, tn), lambda i, j, k: (j, k, 0)) if b_sharded
              else pl.BlockSpec((tk, tn), lambda i, j, k: (k, j)))

    def body(a_ref, b_ref, *rest):
        aux_refs, o_ref = rest[:n_aux], rest[n_aux]
        at = a_ref[...]
        if pro is not None:
            at = pro(at, *[r[...] for r in aux_refs])
        part = _bdot(at, b_ref[...])
        if nk == 1:
            o_ref[...] = (part if epi is None else epi(part)).astype(out_dtype)
            return
        assert epi is None
        acc_ref = rest[n_aux + 1]
        kk = pl.program_id(2)

        @pl.when(kk == 0)
        def _():
            acc_ref[...] = part

        @pl.when(kk > 0)
        def _():
            acc_ref[...] += part

        @pl.when(kk == nk - 1)
        def _():
            o_ref[...] = acc_ref[...].astype(out_dtype)

    return pl.pallas_call(
        body, name=name,
        grid=(m // tm, n // tn, nk),
        in_specs=[pl.BlockSpec((tm, tk), lambda i, j, k: (i, k)), b_spec]
        + [pl.BlockSpec((1, tk), lambda i, j, k: (0, k)) for _ in aux],
        out_specs=pl.BlockSpec((tm, tn), lambda i, j, k: (i, j)),
        out_shape=jax.ShapeDtypeStruct((m, n), out_dtype),
        scratch_shapes=[] if nk == 1 else [pltpu.VMEM((tm, tn), F32)],
        compiler_params=_params(("parallel", "parallel", "arbitrary")),
    )(a, b, *aux)


def _mm_nt(name, a_list, b_list, *, n, tm, tn, out_dtype, epi=None, epi_aux=()):
    m = a_list[0][0].shape[0]
    n_op = len(a_list)
    n_epi = len(epi_aux)

    def body(*refs):
        a_refs, b_refs = refs[:n_op], refs[n_op:2 * n_op]
        e_refs, o_ref = refs[2 * n_op:2 * n_op + n_epi], refs[2 * n_op + n_epi]
        acc = None
        for a_ref, b_ref in zip(a_refs, b_refs):
            part = _bdot(a_ref[...], b_ref[...], NT)
            acc = part if acc is None else acc + part
        if epi is not None:
            acc = epi(acc, *[r[...] for r in e_refs])
        o_ref[...] = acc.astype(out_dtype)

    in_specs = [pl.BlockSpec((tm, w), functools.partial(lambda i, j, cb: (i, cb), cb=cb)) for (_, w, cb) in a_list]
    for (b, w, cb) in b_list:
        if b.ndim == 3:
            in_specs.append(pl.BlockSpec((None, tn, w), functools.partial(lambda i, j, cb: (cb, j, 0), cb=cb)))
        else:
            in_specs.append(pl.BlockSpec((tn, w), functools.partial(lambda i, j, cb: (j, cb), cb=cb)))
    in_specs += [pl.BlockSpec((tm, tn), lambda i, j: (i, j)) for _ in epi_aux]
    return pl.pallas_call(
        body, name=name,
        grid=(m // tm, n // tn),
        in_specs=in_specs,
        out_specs=pl.BlockSpec((tm, tn), lambda i, j: (i, j)),
        out_shape=jax.ShapeDtypeStruct((m, n), out_dtype),
        compiler_params=_params(("parallel", "parallel")),
    )(*[a for (a, _, _) in a_list], *[b for (b, _, _) in b_list], *epi_aux)


def _mm_tn(name, a, b, *, tm, tn, ts, pro=None, aux=(), col_shards=False):
    s_all, ka = a.shape
    nb = b.shape[1]
    n_aux = len(aux)
    ns = s_all // ts
    assert not col_shards or tn == nb // N_DEV

    def body(a_ref, b_ref, *rest):
        aux_refs, o_ref, acc_ref = rest[:n_aux], rest[n_aux], rest[n_aux + 1]
        at = a_ref[...]
        if pro is not None:
            at = pro(at, *[r[...] for r in aux_refs])
        part = _bdot(at, b_ref[...], TN)
        ss = pl.program_id(2)

        @pl.when(ss == 0)
        def _():
            acc_ref[...] = part

        @pl.when(ss > 0)
        def _():
            acc_ref[...] += part

        @pl.when(ss == ns - 1)
        def _():
            o_ref[...] = acc_ref[...].astype(BF16)

    if col_shards:
        out_spec = pl.BlockSpec((None, tm, tn), lambda i, j, s: (j, i, 0))
        out_shape = jax.ShapeDtypeStruct((N_DEV, ka, tn), BF16)
    else:
        out_spec = pl.BlockSpec((tm, tn), lambda i, j, s: (i, j))
        out_shape = jax.ShapeDtypeStruct((ka, nb), BF16)
    return pl.pallas_call(
        body, name=name,
        grid=(ka // tm, nb // tn, ns),
        in_specs=[pl.BlockSpec((ts, tm), lambda i, j, s: (s, i)),
                  pl.BlockSpec((ts, tn), lambda i, j, s: (s, j))]
        + [pl.BlockSpec((1, tm), lambda i, j, s: (0, i)) for _ in aux],
        out_specs=out_spec, out_shape=out_shape,
        scratch_shapes=[pltpu.VMEM((tm, tn), F32)],
        compiler_params=_params(("parallel", "parallel", "arbitrary")),
    )(a, b, *aux)


def _rowk(name, fn, n_rows, tr, rows, fulls, outs, accs, reverse=False):
    n = n_rows // tr
    n_row, n_full, n_out, n_acc = len(rows), len(fulls), len(outs), len(accs)

    def pos(i):
        return (n - 1 - i) if reverse else i

    def body(*refs):
        row_refs = refs[:n_row]
        full_refs = refs[n_row:n_row + n_full]
        out_refs = refs[n_row + n_full:n_row + n_full + n_out]
        acc_refs = refs[n_row + n_full + n_out:]
        i = pl.program_id(0)

        @pl.when(i == 0)
        def _():
            for r in acc_refs:
                r[...] = jnp.zeros(r.shape, r.dtype)

        res = fn(pos(i), *[r[...] for r in row_refs], *[r[...] for r in full_refs], *[r[...] for r in acc_refs])
        for r, v in zip(out_refs + acc_refs, res):
            r[...] = v.astype(r.dtype)

    def row_map(i, cb, shift):
        return (jnp.clip(pos(i) + shift, 0, n - 1), cb)

    def halo_map(i, cb, shift):
        tile = jnp.clip(pos(i) + shift, 0, n - 1)
        return (tile * (tr // 8) + (tr // 8 - 1 if shift < 0 else 0), cb)

    in_specs = [pl.BlockSpec((tr, w), functools.partial(row_map, cb=cb, shift=sh)) if sh == 0 else
                pl.BlockSpec((8, w), functools.partial(halo_map, cb=cb, shift=sh)) for (_, w, cb, sh) in rows]
    in_specs += [pl.BlockSpec(f.shape, functools.partial(lambda i, nd: (0,) * nd, nd=f.ndim)) for f in fulls]
    out_specs = [pl.BlockSpec((tr, w), lambda i: (pos(i), 0)) for (w, _) in outs]
    out_specs += [pl.BlockSpec((r, w), lambda i: (0, 0)) for (r, w) in accs]
    out_shape = [jax.ShapeDtypeStruct((n_rows, w), dt) for (w, dt) in outs]
    out_shape += [jax.ShapeDtypeStruct((r, w), F32) for (r, w) in accs]
    return pl.pallas_call(
        body, name=name, grid=(n,), in_specs=in_specs, out_specs=out_specs, out_shape=out_shape,
        compiler_params=_params(("arbitrary",)),
    )(*[a for (a, _, _, _) in rows], *fulls)


def _colsum(x):
    return jnp.sum(x, axis=0, keepdims=True)


def _mean(x):
    return jnp.mean(x, axis=-1, keepdims=True)


def _modulate(x, sc, sh):
    return x * (1.0 + sc) + sh


def _shift_down(cur, prev8, j):
    tr = cur.shape[0]
    row8 = _iota(prev8.shape, 0)
    head = jnp.where(row8 < j, pltpu.roll(prev8, j, 0), pltpu.roll(cur[0:8], j, 0))
    return head if tr == 8 else jnp.concatenate([head, pltpu.roll(cur, j, 0)[8:]], axis=0)


def _shift_up(cur, next8, j):
    tr = cur.shape[0]
    row8 = _iota(next8.shape, 0)
    tail = jnp.where(row8 < 8 - j, pltpu.roll(cur[tr - 8:], 8 - j, 0), pltpu.roll(next8, 8 - j, 0))
    return jnp.concatenate([pltpu.roll(cur, tr - j, 0)[:tr - 8], tail], axis=0)


def _conv(cur, prev, w, b):
    out = cur * w[3:4] + b
    for j in (1, 2, 3):
        out = out + _shift_down(cur, prev, j) * w[3 - j:4 - j]
    return out


def _conv_fwd(p, w_xs, b_xs, w_bc, b_bc, s):
    def fn(pos, xs, xs_prev, bc, bc_prev, w_xs, b_xs, w_bc, b_bc):
        first = pos == 0
        xs_prev = jnp.where(first, 0.0, xs_prev)
        bc_prev = jnp.where(first, 0.0, bc_prev)
        return _silu(_conv(xs, xs_prev, w_xs, b_xs)), _silu(_conv(bc, bc_prev, w_bc, b_bc))

    return _rowk("conv_fwd", fn, s, 256,
                 [(p, D, OFF_XS // D, 0), (p, D, OFF_XS // D, -1), (p, 512, OFF_BC // 512, 0), (p, 512, OFF_BC // 512, -1)],
                 [w_xs, b_xs, w_bc, b_bc], [(D, F32), (512, F32)], [])


def _conv_bwd(dxs_a, dbc_a, p, w_xs, b_xs, w_bc, b_bc, s):
    tr = 256
    n = s // tr

    def fn(pos, da1, da1n, x1, x1p, x1n, da2, da2n, x2, x2p, x2n, w1, b1, w2, b2, aw1, ab1, aw2, ab2):
        dx1, dw1, db1 = _conv_bwd_fn(pos, n, da1, da1n, x1, x1p, x1n, w1, b1)
        dx2, dw2, db2 = _conv_bwd_fn(pos, n, da2, da2n, x2, x2p, x2n, w2, b2)
        return dx1, dx2, aw1 + dw1, ab1 + db1, aw2 + dw2, ab2 + db2

    cx, cb = OFF_XS // D, OFF_BC // 512
    return _rowk("conv_bwd", fn, s, tr,
                 [(dxs_a, D, 0, 0), (dxs_a, D, 0, 1), (p, D, cx, 0), (p, D, cx, -1), (p, D, cx, 1),
                  (dbc_a, 512, 0, 0), (dbc_a, 512, 0, 1), (p, 512, cb, 0), (p, 512, cb, -1), (p, 512, cb, 1)],
                 [w_xs, b_xs, w_bc, b_bc], [(D, BF16), (512, BF16)], [(8, D), (1, D), (8, 512), (1, 512)])


def _conv_bwd_fn(pos, n, da, da_next, x, x_prev, x_next, w, b):
    first, last = pos == 0, pos == n - 1
    x_prev = jnp.where(first, 0.0, x_prev)
    shifted = {j: _shift_down(x, x_prev, j) for j in (1, 2, 3)}
    conv = x * w[3:4] + b
    for j in (1, 2, 3):
        conv = conv + shifted[j] * w[3 - j:4 - j]
    dc = da * _dsilu(conv)
    dc_next = jnp.where(last, 0.0, da_next * _dsilu(_conv(x_next, x[x.shape[0] - 8:], w, b)))
    dx = dc * w[3:4]
    dws = [None] * 4
    dws[3] = _colsum(dc * x)
    for j in (1, 2, 3):
        dx = dx + _shift_up(dc, dc_next, j) * w[3 - j:4 - j]
        dws[3 - j] = _colsum(dc * shifted[j])
    row = _iota((8, x.shape[1]), 0)
    dw = jnp.zeros((8, x.shape[1]), F32)
    for k in range(4):
        dw = jnp.where(row == k, dws[k], dw)
    return dx, dw, _colsum(dc)


def _ssd_gates(dtf, bias, a_log):
    lane = _iota(dtf.shape, 1)
    head = lane < NH
    dt = jnp.where(head, _softplus(dtf + bias), 0.0)
    a_neg = jnp.where(_iota(a_log.shape, 1) < NH, -jnp.exp(a_log), 0.0)
    a = dt * a_neg
    cs = _sel_left(_tri_lower(CHUNK), a)
    return dt, a_neg, cs


def _decay_mask(cs_ref, cst_ref, h):
    diff = cs_ref[:, h:h + 1] - cst_ref[h:h + 1, :]
    low = _iota((CHUNK, CHUNK), 1) <= _iota((CHUNK, CHUNK), 0)
    return jnp.where(low, jnp.exp(jnp.minimum(diff, 0.0)), 0.0)


def _ssd_fwd(xs_a, bc_a, p, bias128, alog128, dskip_x, s):
    nc = s // CHUNK
    t = CHUNK

    def body(xs_ref, bc_ref, dtf_ref, bias_ref, alog_ref, dsk_ref, y_ref, st_ref,
             state, x_sc, xw_sc, cs_sc, cst_sc, yd_sc):
        c = pl.program_id(0)

        @pl.when(c == 0)
        def _():
            state[...] = jnp.zeros(state.shape, F32)

        dt, _, cs = _ssd_gates(dtf_ref[...], bias_ref[...], alog_ref[...])
        cs_sc[...] = cs
        cst_sc[...] = cs.T
        cs_last = cs[t - 1:t, :]
        expand = _head_expand()
        ex = _sel_right(jnp.concatenate([dt, jnp.exp(cs), jnp.exp(cs_last - cs)], axis=0), expand, terms=2)
        dt_x, eo_x, we_x = ex[0:t], ex[t:2 * t], ex[2 * t:3 * t]
        g_x = _sel_right(jnp.broadcast_to(jnp.exp(cs_last), (8, 128)), expand)[0:1]
        xs = xs_ref[...]
        x = xs * dt_x
        x_sc[...] = x.astype(BF16)
        xw_sc[...] = (x * we_x).astype(BF16)
        prev = state[...]
        st_ref[0] = prev
        prev_b = prev.astype(BF16)
        for g in range(2):
            cols = slice(g * 512, (g + 1) * 512)
            b_g = bc_ref[:, g * 128:(g + 1) * 128].astype(BF16)
            c_g = bc_ref[:, 256 + g * 128:256 + (g + 1) * 128].astype(BF16)
            gmat = _dot(c_g, b_g, NT)
            y_off = _dot(c_g, prev_b[:, cols]) * eo_x[:, cols]
            s_loc = _dot(b_g, xw_sc[:, cols], TN)
            state[:, cols] = g_x[:, cols] * prev[:, cols] + s_loc
            for e in range(HG):
                h = g * HG + e
                m = gmat * _decay_mask(cs_sc, cst_sc, h)
                yd_sc[:, h * HD:(h + 1) * HD] = _dot(m.astype(BF16), x_sc[:, h * HD:(h + 1) * HD])
            y_ref[:, cols] = yd_sc[:, cols] + y_off + dsk_ref[:, cols] * xs[:, cols]

    return pl.pallas_call(
        body, name="ssd_fwd", grid=(nc,),
        in_specs=[pl.BlockSpec((t, D), lambda c: (c, 0)),
                  pl.BlockSpec((t, 512), lambda c: (c, 0)),
                  pl.BlockSpec((t, 128), lambda c: (c, OFF_DTF // 128)),
                  pl.BlockSpec((1, 128), lambda c: (0, 0)),
                  pl.BlockSpec((1, 128), lambda c: (0, 0)),
                  pl.BlockSpec((1, D), lambda c: (0, 0))],
        out_specs=[pl.BlockSpec((t, D), lambda c: (c, 0)),
                   pl.BlockSpec((1, NSTATE, D), lambda c: (c, 0, 0))],
        out_shape=[jax.ShapeDtypeStruct((s, D), F32), jax.ShapeDtypeStruct((nc, NSTATE, D), F32)],
        scratch_shapes=[pltpu.VMEM((NSTATE, D), F32), pltpu.VMEM((t, D), BF16), pltpu.VMEM((t, D), BF16),
                        pltpu.VMEM((t, 128), F32), pltpu.VMEM((128, t), F32), pltpu.VMEM((t, D), F32)],
        compiler_params=_params(("arbitrary",)),
    )(xs_a, bc_a, p, bias128, alog128, dskip_x)


def _ssd_bwd(dy, xs_a, bc_a, p, states, bias128, alog128, dskip_x, s):
    nc = s // CHUNK
    t = CHUNK

    def body(dy_ref, xs_ref, bc_ref, dtf_ref, st_ref, bias_ref, alog_ref, dsk_ref,
             dxs_ref, dbc_ref, ddt_ref, dalog_ref, dskip_ref,
             dstate, x_sc, dy_sc, dx_sc, deo_sc, dwe_sc, cs_sc, cst_sc, dcol_sc, drow_sc):
        i = pl.program_id(0)

        @pl.when(i == 0)
        def _():
            dstate[...] = jnp.zeros(dstate.shape, F32)
            dalog_ref[...] = jnp.zeros(dalog_ref.shape, F32)
            dskip_ref[...] = jnp.zeros(dskip_ref.shape, F32)

        dtf = dtf_ref[...]
        dt, a_neg, cs = _ssd_gates(dtf, bias_ref[...], alog_ref[...])
        cs_sc[...] = cs
        cst_sc[...] = cs.T
        cs_last = cs[t - 1:t, :]
        eo, we, g_end = jnp.exp(cs), jnp.exp(cs_last - cs), jnp.exp(cs_last)
        expand, reduce = _head_expand(), _head_reduce()
        ex = _sel_right(jnp.concatenate([dt, eo, we], axis=0), expand, terms=2)
        dt_x, eo_x, we_x = ex[0:t], ex[t:2 * t], ex[2 * t:3 * t]
        g_x = _sel_right(jnp.broadcast_to(g_end, (8, 128)), expand)[0:1]
        xs = xs_ref[...]
        dyv = dy_ref[...]
        x = xs * dt_x
        x_sc[...] = x.astype(BF16)
        dy_sc[...] = dyv.astype(BF16)
        dyo_b = (dyv * eo_x).astype(BF16)
        xw_b = (x * we_x).astype(BF16)
        prev = st_ref[0]
        prev_b = prev.astype(BF16)
        dnext = dstate[...]
        dnext_b = dnext.astype(BF16)
        dcol_sc[...] = jnp.zeros(dcol_sc.shape, F32)
        drow_sc[...] = jnp.zeros(drow_sc.shape, F32)
        lane_row = _iota((1, 128), 1)
        sub_col = _iota((128, 1), 0)
        for g in range(2):
            cols = slice(g * 512, (g + 1) * 512)
            b_g = bc_ref[:, g * 128:(g + 1) * 128].astype(BF16)
            c_g = bc_ref[:, 256 + g * 128:256 + (g + 1) * 128].astype(BF16)
            gmat = _dot(c_g, b_g, NT)
            b_ds = _dot(b_g, dnext_b[:, cols])
            c_s = _dot(c_g, prev_b[:, cols])
            dx_sc[:, cols] = b_ds * we_x[:, cols]
            deo_sc[:, cols] = dyv[:, cols] * c_s
            dwe_sc[:, cols] = b_ds * x[:, cols]
            db = _dot(xw_b[:, cols], dnext_b[:, cols], NT)
            dc = _dot(dyo_b[:, cols], prev_b[:, cols], NT)
            dstate[:, cols] = g_x[:, cols] * dnext[:, cols] + _dot(c_g, dyo_b[:, cols], TN)
            dg = jnp.zeros((t, t), F32)
            for e in range(HG):
                h = g * HG + e
                hc = slice(h * HD, (h + 1) * HD)
                lmat = _decay_mask(cs_sc, cst_sc, h)
                m = gmat * lmat
                dx_sc[:, hc] += _dot(m.astype(BF16), dy_sc[:, hc], TN)
                dm = _dot(dy_sc[:, hc], x_sc[:, hc], NT)
                dg = dg + dm * lmat
                qm = dm * m
                dcol_sc[...] += jnp.sum(qm, axis=1, keepdims=True) * (lane_row == h).astype(F32)
                drow_sc[...] += (sub_col == h).astype(F32) * jnp.sum(qm, axis=0, keepdims=True)
            dg_b = dg.astype(BF16)
            dbc_ref[:, g * 128:(g + 1) * 128] = db + _dot(dg_b, c_g, TN)
            dbc_ref[:, 256 + g * 128:256 + (g + 1) * 128] = dc + _dot(dg_b, b_g)
        d_eo = _sel_right(deo_sc[...], reduce, terms=2)
        d_we = _sel_right(dwe_sc[...], reduce, terms=2)
        d_gend = _sel_right(jnp.broadcast_to(_colsum(dnext * prev), (8, D)), reduce)[0:1]
        d_cs = dcol_sc[...] - drow_sc[...].T + d_eo * eo - d_we * we
        extra = _colsum(d_we * we) + d_gend * g_end
        d_cs = d_cs + jnp.where(_iota((t, 128), 0) == t - 1, extra, 0.0)
        da = _sel_left(_tri_upper(t), d_cs)
        dx = dx_sc[...]
        ddt = _sel_right(dx * xs, reduce, terms=2) + da * a_neg
        dxs_ref[...] = dx * dt_x + dsk_ref[...] * dyv
        ddt_ref[...] = jnp.where(_iota((t, 128), 1) < NH, ddt * _sigmoid(dtf + bias_ref[...]), 0.0)
        dalog_ref[...] += _colsum(da * dt) * a_neg
        dskip_ref[...] += _sel_right(jnp.broadcast_to(_colsum(dyv * xs), (8, D)), reduce)[0:1]

    rev = lambda i: nc - 1 - i
    return pl.pallas_call(
        body, name="ssd_bwd", grid=(nc,),
        in_specs=[pl.BlockSpec((t, D), lambda i: (rev(i), 0)),
                  pl.BlockSpec((t, D), lambda i: (rev(i), 0)),
                  pl.BlockSpec((t, 512), lambda i: (rev(i), 0)),
                  pl.BlockSpec((t, 128), lambda i: (rev(i), OFF_DTF // 128)),
                  pl.BlockSpec((1, NSTATE, D), lambda i: (rev(i), 0, 0)),
                  pl.BlockSpec((1, 128), lambda i: (0, 0)),
                  pl.BlockSpec((1, 128), lambda i: (0, 0)),
                  pl.BlockSpec((1, D), lambda i: (0, 0))],
        out_specs=[pl.BlockSpec((t, D), lambda i: (rev(i), 0)),
                   pl.BlockSpec((t, 512), lambda i: (rev(i), 0)),
                   pl.BlockSpec((t, 128), lambda i: (rev(i), 0)),
                   pl.BlockSpec((1, 128), lambda i: (0, 0)),
                   pl.BlockSpec((1, 128), lambda i: (0, 0))],
        out_shape=[jax.ShapeDtypeStruct((s, D), F32), jax.ShapeDtypeStruct((s, 512), F32),
                   jax.ShapeDtypeStruct((s, 128), F32), jax.ShapeDtypeStruct((1, 128), F32),
                   jax.ShapeDtypeStruct((1, 128), F32)],
        scratch_shapes=[pltpu.VMEM((NSTATE, D), F32), pltpu.VMEM((t, D), BF16), pltpu.VMEM((t, D), BF16),
                        pltpu.VMEM((t, D), F32), pltpu.VMEM((t, D), F32), pltpu.VMEM((t, D), F32),
                        pltpu.VMEM((t, 128), F32), pltpu.VMEM((128, t), F32),
                        pltpu.VMEM((t, 128), F32), pltpu.VMEM((128, t), F32)],
        compiler_params=_params(("arbitrary",)),
    )(dy, xs_a, bc_a, p, states, bias128, alog128, dskip_x)


def _gate_lanes(shape):
    lane = _iota(shape, 1)
    return (lane >= NH) & (lane < 2 * NH)


def _cum_fwd(p, bias128, s):
    tr = min(512, s)

    def body(dtf_ref, bias_ref, o_ref, carry):
        @pl.when(pl.program_id(0) == 0)
        def _():
            carry[...] = jnp.zeros(carry.shape, F32)

        lf = jnp.where(_gate_lanes((tr, 128)), _log_sigmoid(dtf_ref[...] + bias_ref[...]), 0.0)
        cum = _sel_left(_tri_lower(tr), lf) + carry[...]
        carry[...] = cum[tr - 1:tr, :]
        o_ref[...] = cum

    return pl.pallas_call(
        body, name="cum_fwd", grid=(s // tr,),
        in_specs=[pl.BlockSpec((tr, 128), lambda i: (i, OFF_DTF // 128)), pl.BlockSpec((1, 128), lambda i: (0, 0))],
        out_specs=pl.BlockSpec((tr, 128), lambda i: (i, 0)),
        out_shape=jax.ShapeDtypeStruct((s, 128), F32),
        scratch_shapes=[pltpu.VMEM((1, 128), F32)],
        compiler_params=_params(("arbitrary",)),
    )(p, bias128)


def _cum_bwd(dcum, ddt_raw, p, bias128, s):
    tr = min(512, s)

    def fn(pos, dcum, ddt, dtf, bias, carry, acc):
        suffix = _sel_left(_tri_upper(tr), dcum) + carry
        dfr = jnp.where(_gate_lanes((tr, 128)), suffix * _sigmoid(-(dtf + bias)), 0.0)
        out = ddt + dfr
        return out, suffix[0:1, :], acc + _colsum(out)

    return _rowk("cum_bwd", fn, s, tr, [(dcum, 128, 0, 0), (ddt_raw, 128, 0, 0), (p, 128, OFF_DTF // 128, 0)],
                 [bias128], [(128, BF16)], [(1, 128), (1, 128)], reverse=True)


ATT_BLOCK = 512
ATT_STRIP = 32


def _head_part(shape, h, dim):
    i = _iota(shape, dim)
    return (i >= h * HD) & (i < (h + 1) * HD)


def _k_augmented(k_blk, cum_blk, j, h):
    tk = k_blk.shape[0]
    lane = _iota((tk, 128), 1)
    col = jnp.sum(jnp.where(lane == NH + 2 * j + h, cum_blk, 0.0), axis=1, keepdims=True)
    c0, c1, c2 = [c.astype(F32) for c in _split3(-col)]
    k_h = k_blk if h == 0 else pltpu.roll(k_blk, HD, 1)
    aug = jnp.where(lane == HD, c0, jnp.where(lane == HD + 1, c1, jnp.where(lane == HD + 2, c2, 0.0)))
    return jnp.where(lane < HD, k_h, aug).astype(BF16)


def _q_augmented_t(q_blk):
    tq = q_blk.shape[0]
    q_t = (q_blk * ATT_SCALE).T.astype(BF16)
    ones = (_iota((HD, tq), 0) < 3).astype(BF16)
    return [jnp.concatenate([q_t[h * HD:(h + 1) * HD], ones], axis=0) for h in range(2)]


def _rows01(r0, r1):
    sub = _iota((8, r0.shape[1]), 0)
    return jnp.where(sub == 0, r0, jnp.where(sub == 1, r1, 0.0))


def _fold8(x, op, cur):
    for g in range(x.shape[0] // 8):
        cur = op(cur, x[8 * g:8 * (g + 1), :])
    return cur


def _attn_fwd(p, cum, s):
    t = min(ATT_BLOCK, s)
    nq = s // t
    r = ATT_STRIP

    def body(q_ref, k_ref, v_ref, c_ref, o_ref, lse_ref, kaug_sc, vt_sc, s0_sc, s1_sc, p0_sc, p1_sc, m_sc, l_sc, acc_sc):
        j, qi = pl.program_id(0), pl.program_id(1)
        s_sc, p_sc = (s0_sc, s1_sc), (p0_sc, p1_sc)

        @pl.when(qi == 0)
        def _():
            for c in range(nq):
                rows = slice(c * t, (c + 1) * t)
                k_blk, vt = k_ref[rows, :], v_ref[rows, :].T
                for h in range(2):
                    kaug_sc[h, rows, :] = _k_augmented(k_blk, c_ref[rows, :], j, h)
                    vt_sc[h, :, rows] = vt[h * HD:(h + 1) * HD].astype(BF16)

        qaug_t = _q_augmented_t(q_ref[...])
        m_sc[...] = jnp.full(m_sc.shape, -1e30, F32)
        l_sc[...] = jnp.zeros(l_sc.shape, F32)
        acc_sc[...] = jnp.zeros(acc_sc.shape, F32)
        top = _iota((128, t), 0) < HD

        def logits(kb, buf):
            kv = pl.ds(pl.multiple_of(kb * t, t), t)
            for h in range(2):
                s_sc[buf][h] = _dot(kaug_sc[h, kv, :], qaug_t[h])

        def softmax(buf, diagonal):
            alphas = []
            for h in range(2):
                cur = jnp.full((8, t), -1e30, F32)
                for i in range(t // r):
                    rows = slice(i * r, (i + 1) * r)
                    x = s_sc[buf][h, rows, :]
                    if diagonal:
                        x = jnp.where(_iota((r, t), 1) >= i * r + _iota((r, t), 0), x, -1e30)
                        s_sc[buf][h, rows, :] = x
                    cur = _fold8(x, jnp.maximum, cur)
                m_prev = m_sc[h, 0:1, :]
                m_new = jnp.maximum(m_prev, jnp.max(cur, axis=0, keepdims=True))
                alpha = jnp.exp(m_prev - m_new)
                m_sc[h, 0:1, :] = m_new
                alphas.append(alpha)
                tot = jnp.zeros((8, t), F32)
                for i in range(t // r):
                    rows = slice(i * r, (i + 1) * r)
                    pr = jnp.exp(s_sc[buf][h, rows, :] - m_new)
                    p_sc[buf][h, rows, :] = pr.astype(BF16)
                    tot = _fold8(pr, jnp.add, tot)
                l_sc[h, 0:1, :] = alpha * l_sc[h, 0:1, :] + jnp.sum(tot, axis=0, keepdims=True)
            return alphas

        def accumulate(kb, buf, alphas):
            kv = pl.ds(pl.multiple_of(kb * t, t), t)
            for h in range(2):
                part = slice(h * HD, (h + 1) * HD)
                acc_sc[part, :] = acc_sc[part, :] * alphas[h] + _dot(vt_sc[h, :, kv], p_sc[buf][h])

        def first_trip():
            logits(0, 1)
            accumulate(qi, 0, softmax(0, True))
            logits(jnp.minimum(1, qi - 1), 0)
            return tuple(softmax(1, False))

        def only_diagonal():
            accumulate(qi, 0, softmax(0, True))
            return (jnp.ones((1, t), F32),) * 2

        def steady(u, alphas_b):
            accumulate(2 * u - 2, 1, alphas_b)
            logits(2 * u, 1)
            accumulate(2 * u - 1, 0, softmax(0, False))
            logits(jnp.minimum(2 * u + 1, qi - 1), 0)
            return tuple(softmax(1, False))

        logits(qi, 0)
        n_blocks = qi + 1
        alphas_b = lax.cond(qi >= 1, first_trip, only_diagonal)
        alphas_b = lax.fori_loop(1, n_blocks // 2, steady, alphas_b)
        last_b = 2 * (n_blocks // 2) - 2

        @pl.when((qi >= 1) & (n_blocks % 2 == 0))
        def _():
            accumulate(last_b, 1, alphas_b)

        @pl.when((qi >= 2) & (n_blocks % 2 == 1))
        def _():
            accumulate(last_b, 1, alphas_b)
            accumulate(qi - 1, 0, softmax(0, False))

        l0, l1 = l_sc[0, 0:1, :], l_sc[1, 0:1, :]
        o_ref[...] = (acc_sc[...] / jnp.where(top, l0, l1)).T
        lse_ref[0] = _rows01(m_sc[0, 0:1, :] + jnp.log(l0), m_sc[1, 0:1, :] + jnp.log(l1))

    return pl.pallas_call(
        body, name="attn_fwd", grid=(NH // 2, nq),
        in_specs=[pl.BlockSpec((t, 128), lambda j, qi: (qi, OFF_Q // 128 + j)),
                  pl.BlockSpec((s, 128), lambda j, qi: (0, OFF_K // 128 + j)),
                  pl.BlockSpec((s, 128), lambda j, qi: (0, OFF_V // 128 + j)),
                  pl.BlockSpec((s, 128), lambda j, qi: (0, 0))],
        out_specs=[pl.BlockSpec((t, 128), lambda j, qi: (qi, j)),
                   pl.BlockSpec((1, 8, t), lambda j, qi: (j, 0, qi))],
        out_shape=[jax.ShapeDtypeStruct((s, D), F32), jax.ShapeDtypeStruct((NH // 2, 8, s), F32)],
        scratch_shapes=[pltpu.VMEM((2, s, 128), BF16), pltpu.VMEM((2, HD, s), BF16), pltpu.VMEM((2, t, t), F32),
                        pltpu.VMEM((2, t, t), F32), pltpu.VMEM((2, t, t), BF16), pltpu.VMEM((2, t, t), BF16),
                        pltpu.VMEM((2, 8, t), F32), pltpu.VMEM((2, 8, t), F32), pltpu.VMEM((128, t), F32)],
        compiler_params=_params(("parallel", "arbitrary")),
    )(p, p, p, cum)


def _attn_bwd(p, cum, o, lse, do, s):
    t = min(ATT_BLOCK, s)
    nq = s // t
    r = ATT_STRIP

    def body(q_ref, k_ref, v_ref, c_ref, o_ref, lse_ref, do_ref, dq_ref, dk_ref, dv_ref, dc_ref, dr_ref,
             qaugt_sc, qh_sc, dot_sc, doh_sc, delta_sc, dqt_sc, dr_sc, kaug_sc, vh_sc, kt_sc,
             s0_sc, s1_sc, dp0_sc, dp1_sc, p0_sc, p1_sc, ds0_sc, ds1_sc, dk_sc, dv_sc, dc_sc):
        j, ki = pl.program_id(0), pl.program_id(1)
        s_sc, dp_sc, p_sc, ds_sc = (s0_sc, s1_sc), (dp0_sc, dp1_sc), (p0_sc, p1_sc), (ds0_sc, ds1_sc)

        @pl.when(ki == 0)
        def _():
            for c in range(nq):
                rows = slice(c * t, (c + 1) * t)
                q_blk, do_blk = q_ref[rows, :], do_ref[rows, :]
                qaugt_sc[0, :, rows], qaugt_sc[1, :, rows] = _q_augmented_t(q_blk)
                dot_sc[:, rows] = do_blk.T.astype(BF16)
                prod_t = (do_blk * o_ref[rows, :]).T
                delta_sc[:, rows] = _rows01(jnp.sum(prod_t[0:HD], axis=0, keepdims=True),
                                            jnp.sum(prod_t[HD:], axis=0, keepdims=True))
                for h in range(2):
                    head = _head_part((t, 128), h, 1)
                    qh_sc[h, rows, :] = jnp.where(head, q_blk * ATT_SCALE, 0.0).astype(BF16)
                    doh_sc[h, rows, :] = jnp.where(head, do_blk, 0.0).astype(BF16)
            dqt_sc[...] = jnp.zeros(dqt_sc.shape, F32)
            dr_sc[...] = jnp.zeros(dr_sc.shape, F32)

        k_blk, v_blk = k_ref[...], v_ref[...]
        kt = k_blk.T
        for h in range(2):
            kaug_sc[h] = _k_augmented(k_blk, c_ref[...], j, h)
            vh_sc[h] = jnp.where(_head_part((t, 128), h, 1), v_blk, 0.0).astype(BF16)
            kt_sc[h] = kt[h * HD:(h + 1) * HD].astype(BF16)
        dk_sc[...] = jnp.zeros(dk_sc.shape, F32)
        dv_sc[...] = jnp.zeros(dv_sc.shape, F32)
        dc_sc[...] = jnp.zeros(dc_sc.shape, F32)

        def inputs(qb, buf):
            qs = pl.ds(pl.multiple_of(qb * t, t), t)
            for h in range(2):
                s_sc[buf][h] = _dot(kaug_sc[h], qaugt_sc[h, :, qs])
                dp_sc[buf][h] = _dot(vh_sc[h], dot_sc[:, qs])

        def elementwise(qb, buf, diagonal):
            qs = pl.ds(pl.multiple_of(qb * t, t), t)
            for h in range(2):
                lse_row, delta_row = lse_ref[0, h:h + 1, qs], delta_sc[h:h + 1, qs]
                tot = jnp.zeros((8, t), F32)
                for i in range(t // r):
                    rows = slice(i * r, (i + 1) * r)
                    x = s_sc[buf][h, rows, :]
                    if diagonal:
                        x = jnp.where(_iota((r, t), 1) >= i * r + _iota((r, t), 0), x, -1e30)
                    pr = jnp.exp(x - lse_row)
                    ds = pr * (dp_sc[buf][h, rows, :] - delta_row)
                    p_sc[buf][h, rows, :] = pr.astype(BF16)
                    ds_sc[buf][h, rows, :] = ds.astype(BF16)
                    dc_sc[h, rows, :] += sum(ds[:, 128 * g:128 * (g + 1)] for g in range(t // 128))
                    tot = _fold8(ds, jnp.add, tot)
                dr_sc[h, :, qs] += tot

        def outputs(qb, buf):
            qs = pl.ds(pl.multiple_of(qb * t, t), t)
            dv_sc[...] += _dot(p_sc[buf][0], doh_sc[0, qs, :]) + _dot(p_sc[buf][1], doh_sc[1, qs, :])
            dk_sc[...] += _dot(ds_sc[buf][0], qh_sc[0, qs, :]) + _dot(ds_sc[buf][1], qh_sc[1, qs, :])
            for h in range(2):
                dqt_sc[h * HD:(h + 1) * HD, qs] += _dot(kt_sc[h], ds_sc[buf][h])

        def pair(a, b, a_diagonal):
            inputs(a, 0)
            inputs(b, 1)
            elementwise(a, 0, a_diagonal)
            outputs(a, 0)
            elementwise(b, 1, False)
            outputs(b, 1)

        def later(u, carry):
            pair(ki + 1 + 2 * u, ki + 2 + 2 * u, False)
            return carry

        n_later = nq - 1 - ki
        lax.fori_loop(0, n_later // 2, later, 0)

        @pl.when(n_later % 2 == 1)
        def _():
            pair(ki, nq - 1, True)

        @pl.when(n_later % 2 == 0)
        def _():
            inputs(ki, 0)
            elementwise(ki, 0, True)
            outputs(ki, 0)

        dk_ref[...] = dk_sc[...].astype(BF16)
        dv_ref[...] = dv_sc[...].astype(BF16)
        lane = _iota((t, 128), 1)
        cols = jnp.where(lane == 0, jnp.sum(dc_sc[0], axis=1, keepdims=True),
                         jnp.where(lane == 1, jnp.sum(dc_sc[1], axis=1, keepdims=True), 0.0))
        dc_ref[0] = cols.T[0:8, :]

        @pl.when(ki == nq - 1)
        def _():
            for c in range(nq):
                rows = slice(c * t, (c + 1) * t)
                dq_ref[rows, :] = dqt_sc[:, rows].T * ATT_SCALE
            dr_ref[0] = _rows01(jnp.sum(dr_sc[0], axis=0, keepdims=True), jnp.sum(dr_sc[1], axis=0, keepdims=True))

    whole = lambda off: pl.BlockSpec((s, 128), functools.partial(lambda j, ki, off: (0, off + j), off=off))
    return pl.pallas_call(
        body, name="attn_bwd", grid=(NH // 2, nq),
        in_specs=[whole(OFF_Q // 128),
                  pl.BlockSpec((t, 128), lambda j, ki: (ki, OFF_K // 128 + j)),
                  pl.BlockSpec((t, 128), lambda j, ki: (ki, OFF_V // 128 + j)),
                  pl.BlockSpec((t, 128), lambda j, ki: (ki, 0)),
                  whole(0),
                  pl.BlockSpec((1, 8, s), lambda j, ki: (j, 0, 0)),
                  whole(0)],
        out_specs=[whole(0),
                   pl.BlockSpec((t, 128), lambda j, ki: (ki, j)),
                   pl.BlockSpec((t, 128), lambda j, ki: (ki, j)),
                   pl.BlockSpec((1, 8, t), lambda j, ki: (j, 0, ki)),
                   pl.BlockSpec((1, 8, s), lambda j, ki: (j, 0, 0))],
        out_shape=[jax.ShapeDtypeStruct((s, D), F32), jax.ShapeDtypeStruct((s, D), BF16), jax.ShapeDtypeStruct((s, D), BF16),
                   jax.ShapeDtypeStruct((NH // 2, 8, s), F32), jax.ShapeDtypeStruct((NH // 2, 8, s), F32)],
        scratch_shapes=[pltpu.VMEM((2, 128, s), BF16), pltpu.VMEM((2, s, 128), BF16), pltpu.VMEM((128, s), BF16),
                        pltpu.VMEM((2, s, 128), BF16), pltpu.VMEM((8, s), F32), pltpu.VMEM((128, s), F32),
                        pltpu.VMEM((2, 8, s), F32), pltpu.VMEM((2, t, 128), BF16), pltpu.VMEM((2, t, 128), BF16),
                        pltpu.VMEM((2, HD, t), BF16)]
        + [pltpu.VMEM((2, t, t), F32)] * 4 + [pltpu.VMEM((2, t, t), BF16)] * 4
        + [pltpu.VMEM((t, 128), F32), pltpu.VMEM((t, 128), F32), pltpu.VMEM((2, t, 128), F32)],
        compiler_params=_params(("parallel", "arbitrary")),
    )(p, p, p, cum, o, lse, do)


def _ln_stats(u):
    mu = _mean(u)
    d = u - mu
    rstd = lax.rsqrt(_mean(d * d) + EPS)
    return d * rstd, rstd


def _ln_bwd(dx, xh, rstd, gam):
    dxh = dx * gam
    return rstd * (dxh - _mean(dxh) - xh * _mean(dxh * xh))


def _rms_bwd(d, xn, r, w):
    t = d * w
    return r * (t - xn * _mean(t * xn)), _colsum(d * xn)


def _mix_norm(y, p, att, w_ssm, w_att, s):
    def fn(pos, y, z, att, w1, w2):
        g = y * _silu(z)
        n1 = g * lax.rsqrt(_mean(g * g) + EPS) * w1
        n2 = att * lax.rsqrt(_mean(att * att) + EPS) * w2
        return (jnp.concatenate([n1, n2], axis=1),)

    return _rowk("mix_norm", fn, s, 512, [(y, D, 0, 0), (p, D, OFF_Z // D, 0), (att, D, 0, 0)],
                 [w_ssm, w_att], [(2 * D, BF16)], [])[0]


def _mix_norm_bwd(dmix, y, p, att, w_ssm, w_att, s):
    def fn(pos, dmix, y, z, att, w1, w2, a1, a2):
        sz = _silu(z)
        g = y * sz
        r1 = lax.rsqrt(_mean(g * g) + EPS)
        dg, dw1 = _rms_bwd(dmix[:, :D], g * r1, r1, w1)
        r2 = lax.rsqrt(_mean(att * att) + EPS)
        datt, dw2 = _rms_bwd(dmix[:, D:], att * r2, r2, w2)
        return dg * sz, dg * y * _dsilu(z), datt, a1 + dw1, a2 + dw2

    return _rowk("mix_norm_bwd", fn, s, 256, [(dmix, 2 * D, 0, 0), (y, D, 0, 0), (p, D, OFF_Z // D, 0), (att, D, 0, 0)],
                 [w_ssm, w_att], [(D, F32), (D, BF16), (D, F32)], [(1, D), (1, D)])


def _ln1(x0, y, g1, gam, bet, sc2, sh2, s):
    def fn(pos, x0, y, g1, gam, bet, sc2, sh2):
        xh, _ = _ln_stats(ALPHA * x0 + (1.0 + g1) * y)
        x1 = xh * gam + bet
        return x1, _modulate(x1, sc2, sh2)

    return _rowk("ln1", fn, s, 512, [(x0, D, 0, 0), (y, D, 0, 0)], [g1, gam, bet, sc2, sh2], [(D, F32), (D, BF16)], [])


def _ln2_loss(x1, ff, tgt, g2, gam, bet, s):
    def fn(pos, x1, ff, tgt, g2, gam, bet, a_loss, a_dgam, a_dbet, a_dg2):
        xh, rstd = _ln_stats(ALPHA * x1 + (1.0 + g2) * ff)
        err = xh * gam + bet - tgt
        dx2 = err * (1.0 / D)
        du = _ln_bwd(dx2, xh, rstd, gam)
        return (du, du * (1.0 + g2), a_loss + _colsum(err * err), a_dgam + _colsum(dx2 * xh),
                a_dbet + _colsum(dx2), a_dg2 + _colsum(du * ff))

    return _rowk("ln2_loss", fn, s, 512, [(x1, D, 0, 0), (ff, D, 0, 0), (tgt, D, 0, 0)], [g2, gam, bet],
                 [(D, F32), (D, BF16)], [(1, D)] * 4)


def _ln1_bwd(dh2, du2, x0, y, g1, gam, bet, sc2, s):
    def fn(pos, dh2, du2, x0, y, g1, gam, bet, sc2, a_sc, a_sh, a_gam, a_bet, a_g1):
        xh, rstd = _ln_stats(ALPHA * x0 + (1.0 + g1) * y)
        x1 = xh * gam + bet
        dx1 = ALPHA * du2 + dh2 * (1.0 + sc2)
        du1 = _ln_bwd(dx1, xh, rstd, gam)
        return (du1, du1 * (1.0 + g1), a_sc + _colsum(dh2 * x1), a_sh + _colsum(dh2), a_gam + _colsum(dx1 * xh),
                a_bet + _colsum(dx1), a_g1 + _colsum(du1 * y))

    return _rowk("ln1_bwd", fn, s, 512, [(dh2, D, 0, 0), (du2, D, 0, 0), (x0, D, 0, 0), (y, D, 0, 0)],
                 [g1, gam, bet, sc2], [(D, F32), (D, BF16)], [(1, D)] * 5)


def _input_grad(dh1, du1, x0, sc1, s):
    def fn(pos, dh1, du1, x0, sc1, a_sc, a_sh):
        return ALPHA * du1 + dh1 * (1.0 + sc1), a_sc + _colsum(dh1 * x0), a_sh + _colsum(dh1)

    return _rowk("input_grad", fn, s, 512, [(dh1, D, 0, 0), (du1, D, 0, 0), (x0, D, 0, 0)], [sc1],
                 [(D, F32)], [(1, D)] * 2)


def _adamw_math(w, grad, m, v):
    m_new = ADAM_B1 * m + (1.0 - ADAM_B1) * grad
    v_new = ADAM_B2 * v + (1.0 - ADAM_B2) * (grad * grad)
    m_hat = m_new / (1.0 - ADAM_B1 ** ADAM_STEP)
    v_hat = v_new / (1.0 - ADAM_B2 ** ADAM_STEP)
    return -ADAM_LR * (m_hat / (jnp.sqrt(v_hat) + ADAM_EPS) + ADAM_WD * w), m_new, v_new


def _small_update(small_all, layout, w, m, v):
    names = [n for n, _, _ in layout]

    def body(*refs):
        all_ref = refs[0]
        w_refs, m_refs, v_refs = [refs[1 + k * len(names):1 + (k + 1) * len(names)] for k in range(3)]
        sum_ref = refs[1 + 3 * len(names)]
        outs = refs[2 + 3 * len(names):]
        total = all_ref[0]
        for k in range(1, N_DEV):
            total = total + all_ref[k]
        sum_ref[...] = total
        for i, (_, off, size) in enumerate(layout):
            grad = total[:, off:off + size]
            delta, m_new, v_new = _adamw_math(w_refs[i][...], grad, m_refs[i][...], v_refs[i][...])
            for o, val in zip(outs[4 * i:4 * i + 4], (grad, delta, m_new, v_new)):
                o[...] = val

    res = pl.pallas_call(
        body, name="small_update",
        out_shape=[jax.ShapeDtypeStruct(small_all.shape[1:], F32)]
        + [jax.ShapeDtypeStruct(w[n].shape, F32) for n in names for _ in range(4)],
        compiler_params=_params(None),
    )(small_all, *[w[n] for n in names], *[m[n] for n in names], *[v[n] for n in names])
    return res[0], {n: res[1 + 4 * i:5 + 4 * i] for i, n in enumerate(names)}


def _adamw(name, w, g, m, v, *, tr, slots):
    r, c = w.shape

    def body(w_ref, g_ref, m_ref, v_ref, g_out, d_out, m_out, v_out):
        if slots:
            grad = g_ref[0][:, :c].astype(F32)
            for k in range(1, N_DEV):
                grad = grad + g_ref[k][:, :c].astype(F32)
        else:
            grad = g_ref[...]
        g_out[...] = grad
        d_out[...], m_out[...], v_out[...] = _adamw_math(w_ref[...], grad, m_ref[...], v_ref[...])

    tile = pl.BlockSpec((tr, c), lambda i: (i, 0))
    g_spec = pl.BlockSpec((N_DEV, tr, g.shape[-1]), lambda i: (0, i, 0)) if slots else tile
    return pl.pallas_call(
        body, name=name, grid=(r // tr,),
        in_specs=[tile, g_spec, tile, tile], out_specs=[tile] * 4,
        out_shape=[jax.ShapeDtypeStruct((r, c), F32)] * 4,
        compiler_params=_params(("parallel",)),
    )(w, g, m, v)


def _dot_f32(a, b, dims=NN):
    a0, a1, a2 = _split3(a)
    b0, b1, b2 = _split3(b)
    acc = _dot(a0, b0, dims)
    for x, y in ((a0, b1), (a1, b0), (a1, b1), (a0, b2), (a2, b0)):
        acc = acc + _dot(x, y, dims)
    return acc


def _ada_mod(c_all, w_shard, b_shard):
    def body(c_ref, w_ref, b_ref, o_ref):
        act = _silu(c_ref[...])
        act16 = jnp.concatenate([act, jnp.zeros_like(act)], axis=0)
        o_ref[...] = _dot_f32(act16, w_ref[...])[0:N_DEV] + b_ref[...]

    return pl.pallas_call(
        body, name="ada_mod", out_shape=jax.ShapeDtypeStruct((N_DEV, w_shard.shape[1]), F32),
        compiler_params=_params(None),
    )(c_all, w_shard, b_shard)


def _ada_grad(c_all, dmod_cols):
    def body(c_ref, dc_ref, gw_ref):
        act = _silu(c_ref[...])
        act16 = jnp.concatenate([act, jnp.zeros_like(act)], axis=0)
        dm = dc_ref[...]
        dm16 = jnp.concatenate([dm, jnp.zeros_like(dm)], axis=0)
        gw_ref[...] = _dot_f32(act16, dm16, TN)

    return pl.pallas_call(
        body, name="ada_grad", out_shape=jax.ShapeDtypeStruct((D, dmod_cols.shape[1]), F32),
        compiler_params=_params(None),
    )(c_all, dmod_cols)


def _exchange(name, xs, scatter):
    n = len(xs)
    n_peer = N_DEV - 1

    def body(*refs):
        x_refs, o_refs = refs[:n], refs[n:2 * n]
        send_sems, recv_sems, local_sems = refs[2 * n:]
        mx, my, mc = lax.axis_index("x"), lax.axis_index("y"), lax.axis_index("c")
        me = 4 * mx + 2 * my + mc

        def src(a, slot):
            return x_refs[a].at[slot] if scatter else x_refs[a]

        own = [pltpu.make_async_copy(src(a, me), o_refs[a].at[me], local_sems.at[a]) for a in range(n)]
        for cp in own:
            cp.start()
        sends = []
        for d in range(1, N_DEV):
            px = 1 - mx if d & 4 else mx
            py = 1 - my if d & 2 else my
            pc = 1 - mc if d & 1 else mc
            peer = 4 * px + 2 * py + pc
            for a in range(n):
                def copy(src_slot, dst_slot, a=a, d=d, to=(px, py, pc)):
                    return pltpu.make_async_remote_copy(
                        src_ref=src(a, src_slot), dst_ref=o_refs[a].at[dst_slot],
                        send_sem=send_sems.at[a * n_peer + d - 1], recv_sem=recv_sems.at[a * n_peer + d - 1],
                        device_id=to, device_id_type=pl.DeviceIdType.MESH)

                out = copy(peer, me)
                out.start()
                sends.append((out, copy(me, peer)))
        for _, arrival in sends:
            arrival.wait_recv()
        for out, _ in sends:
            out.wait_send()
        for cp in own:
            cp.wait()

    shapes = [tuple(x.shape[1:] if scatter else x.shape) for x in xs]
    return pl.pallas_call(
        body, name=name,
        in_specs=[pl.BlockSpec(memory_space=pl.ANY)] * n, out_specs=[pl.BlockSpec(memory_space=pl.ANY)] * n,
        out_shape=[jax.ShapeDtypeStruct((N_DEV,) + sh, x.dtype) for sh, x in zip(shapes, xs)],
        scratch_shapes=[pltpu.SemaphoreType.DMA((n * n_peer,)), pltpu.SemaphoreType.DMA((n * n_peer,)),
                        pltpu.SemaphoreType.DMA((n,))],
        compiler_params=pltpu.CompilerParams(has_side_effects=True),
    )(*xs)


def _gather_two_level(name, x):
    def body(x_ref, o_ref, send_sems, recv_sems, local_sem):
        mx, my, mc = lax.axis_index("x"), lax.axis_index("y"), lax.axis_index("c")
        me, sibling = (mx, my, mc), (mx, my, 1 - mc)
        chips = [(1 - mx, my), (mx, 1 - my), (1 - mx, 1 - my)]

        def slot(px, py, pc):
            return o_ref.at[4 * px + 2 * py + pc]

        def copy(k, block, to, src=None):
            return pltpu.make_async_remote_copy(
                src_ref=slot(*block) if src is None else src, dst_ref=slot(*block),
                send_sem=send_sems.at[k], recv_sem=recv_sems.at[k], device_id=to, device_id_type=pl.DeviceIdType.MESH)

        mine = pltpu.make_async_copy(x_ref, slot(*me), local_sem)
        mine.start()
        first = [copy(0, me, sibling, src=x_ref)] + [copy(1 + i, me, (*chip, mc), src=x_ref) for i, chip in enumerate(chips)]
        for cp in first:
            cp.start()
        passed = [copy(4 + i, (*chip, mc), sibling) for i, chip in enumerate(chips)]
        for i, chip in enumerate(chips):
            copy(1 + i, (*chip, mc), me).wait_recv()
            passed[i].start()
        copy(0, sibling, me).wait_recv()
        for i, chip in enumerate(chips):
            copy(4 + i, (*chip, 1 - mc), me).wait_recv()
        for cp in first + passed:
            cp.wait_send()
        mine.wait()

    return pl.pallas_call(
        body, name=name,
        in_specs=[pl.BlockSpec(memory_space=pl.ANY)], out_specs=pl.BlockSpec(memory_space=pl.ANY),
        out_shape=jax.ShapeDtypeStruct((N_DEV,) + tuple(x.shape), x.dtype),
        scratch_shapes=[pltpu.SemaphoreType.DMA((7,)), pltpu.SemaphoreType.DMA((7,)), pltpu.SemaphoreType.DMA(())],
        compiler_params=pltpu.CompilerParams(has_side_effects=True),
    )(x)


def _after(x, zero):
    return x if zero is None else x + zero.reshape(-1)[0].astype(x.dtype)


def _exchange_copies(x_refs, land_refs, send_sems, recv_sems, scatter):
    n = len(x_refs)
    n_peer = N_DEV - 1
    mx, my, mc = lax.axis_index("x"), lax.axis_index("y"), lax.axis_index("c")
    me = 4 * mx + 2 * my + mc
    pairs = []
    for d in range(1, N_DEV):
        px = 1 - mx if d & 4 else mx
        py = 1 - my if d & 2 else my
        pc = 1 - mc if d & 1 else mc
        peer = 4 * px + 2 * py + pc
        for a in range(n):
            def copy(src_slot, dst_slot, a=a, d=d, to=(px, py, pc)):
                return pltpu.make_async_remote_copy(
                    src_ref=x_refs[a].at[src_slot] if scatter else x_refs[a], dst_ref=land_refs[a].at[dst_slot],
                    send_sem=send_sems.at[a * n_peer + d - 1], recv_sem=recv_sems.at[a * n_peer + d - 1],
                    device_id=to, device_id_type=pl.DeviceIdType.MESH)

            pairs.append((copy(peer, me), copy(me, peer)))
    return me, pairs


def _exchange_async(name, xs, scatter, collective_id):
    n = len(xs)
    shapes = [tuple(x.shape[1:] if scatter else x.shape) for x in xs]
    x_refs = [jax.new_ref(x, memory_space=pltpu.MemorySpace.HBM) for x in xs]
    land_refs = [jax.empty_ref(jax.ShapeDtypeStruct((N_DEV,) + sh, x.dtype), memory_space=pltpu.MemorySpace.HBM)
                 for sh, x in zip(shapes, xs)]

    @pl.kernel(mesh=plsc.ScalarSubcoreMesh(axis_name="sequencer", num_cores=1), name=name,
               scratch_types=(pltpu.SemaphoreType.DMA((n * (N_DEV - 1),)), pltpu.SemaphoreType.DMA((n * (N_DEV - 1),)),
                              pltpu.SemaphoreType.DMA((n,))),
               compiler_params=pltpu.CompilerParams(collective_id=collective_id))
    def launch(send_sems, recv_sems, own_sems):
        barrier = pltpu.get_barrier_semaphore()
        mx, my, mc = lax.axis_index("x"), lax.axis_index("y"), lax.axis_index("c")
        for d in range(1, N_DEV):
            peer = (1 - mx if d & 4 else mx, 1 - my if d & 2 else my, 1 - mc if d & 1 else mc)
            pl.semaphore_signal(barrier, inc=1, device_id=peer, device_id_type=pl.DeviceIdType.MESH)
        pl.semaphore_wait(barrier, N_DEV - 1)
        me, pairs = _exchange_copies(x_refs, land_refs, send_sems, recv_sems, scatter)
        own = [pltpu.make_async_copy(x_refs[a].at[me] if scatter else x_refs[a], land_refs[a].at[me], own_sems.at[a])
               for a in range(n)]
        for cp in own:
            cp.start()
        for out, _ in pairs:
            out.start()
        for out, arrival in pairs:
            arrival.wait_recv()
            out.wait_send()
        for cp in own:
            cp.wait()

    launch()
    return lambda: [r[...] for r in land_refs]


def _relu2(a):
    r = jnp.maximum(a, 0.0)
    return r * r


def _relu2_grad(acc, r):
    return acc * (2.0 * jnp.sqrt(r.astype(F32)))


def _local_step(x0, tgt, mod, wcat, late_weights, send_grads, conv_w, conv_b, dt_bias, a_log, d_skip, ssm_norm_w, f_bias,
                attn_norm_w, ln1_g, ln1_b, ln2_g, ln2_b):
    ff_w = DFF // N_DEV
    s = x0.shape[0]
    tm = min(1024, s)
    ts = min(1024, s)
    sh1, sc1, g1, sh2, sc2, g2 = [mod[:, i * D:(i + 1) * D] for i in range(6)]
    zero = jnp.zeros((1, 128 - 2 * NH), F32)
    bias128 = jnp.concatenate([dt_bias, f_bias, zero], axis=1)
    alog128 = jnp.concatenate([a_log, jnp.zeros((1, 128 - NH), F32)], axis=1)
    dskip_x = jnp.repeat(d_skip, HD, axis=1)
    w_xs, w_bc, b_xs, b_bc = conv_w[:, :D], conv_w[:, D:], conv_b[:, :D], conv_b[:, D:]

    p = _mm_nn("in_proj", x0, wcat, tm=tm, tn=1152, tk=D, out_dtype=F32, pro=_modulate, aux=(sc1, sh1))
    xs_a, bc_a = _conv_fwd(p, w_xs, b_xs, w_bc, b_bc, s)
    y_ssd, states = _ssd_fwd(xs_a, bc_a, p, bias128, alog128, dskip_x, s)
    cum = _cum_fwd(p, bias128, s)
    att, lse = _attn_fwd(p, cum, s)
    wout, w1s, w2 = late_weights()
    ymix = _mix_norm(y_ssd, p, att, ssm_norm_w, attn_norm_w, s)
    y = _mm_nn("out_proj", ymix, wout, tm=tm, tn=1024, tk=2 * D, out_dtype=F32)
    x1, h2 = _ln1(x0, y, g1, ln1_g, ln1_b, sc2, sh2, s)
    r = _mm_nn("ff_in", h2, w1s, tm=tm, tn=ff_w, tk=D, out_dtype=BF16, epi=_relu2)
    ff = _mm_nn("ff_out", r, w2, tm=tm, tn=1024, tk=1024, out_dtype=F32)
    du2, dff, sq_err, d_ln2_g, d_ln2_b, d_g2 = _ln2_loss(x1, ff, tgt, g2, ln2_g, ln2_b, s)

    da1 = _mm_nt("d_ff_hidden", [(dff, D, 0)], [(w2, D, 0)], n=DFF, tm=tm, tn=1024, out_dtype=BF16, epi=_relu2_grad,
                 epi_aux=(r,))
    d_w2 = _mm_tn("d_w_ff_out", r, dff, tm=1024, tn=1024, ts=ts)
    d_w1s = _mm_tn("d_w_ff_in", h2, da1, tm=1024, tn=ff_w, ts=ts, col_shards=True)
    dh2 = _mm_nt("d_ff_input", [(da1, ff_w, k) for k in range(N_DEV)], [(w1s, ff_w, k) for k in range(N_DEV)], n=D,
                 tm=min(512, s), tn=1024, out_dtype=F32)
    du1, dy, d_sc2, d_sh2, d_ln1_g, d_ln1_b, d_g1 = _ln1_bwd(dh2, du2, x0, y, g1, ln1_g, ln1_b, sc2, s)

    dmix = _mm_nt("d_mix", [(dy, D, 0)], [(wout, D, 0)], n=2 * D, tm=tm, tn=1024, out_dtype=F32)
    d_wout = _mm_tn("d_w_out", ymix, dy, tm=1024, tn=1024, ts=ts)
    sent = send_grads("late", [d_w1s, d_w2.reshape(N_DEV, -1, D), d_wout.reshape(N_DEV, -1, D)])
    dy_ssd, dz, datt, d_ssm_w, d_attn_w = _mix_norm_bwd(dmix, y_ssd, p, att, _after(ssm_norm_w, sent), attn_norm_w, s)
    dq, dk, dv, dcs, drs = _attn_bwd(p, cum, att, lse, datt, s)
    dxs_a, dbc_a, ddt_raw, d_alog, d_dskip = _ssd_bwd(dy_ssd, xs_a, bc_a, p, states, bias128, alog128, dskip_x, s)
    dcum = jnp.pad((drs - dcs)[:, :2, :].reshape(NH, s).T, ((0, 0), (NH, 128 - 2 * NH)))
    ddtf, _, d_bias = _cum_bwd(dcum, ddt_raw, p, bias128, s)
    dxs, dbc, d_wc_xs, d_bc_xs, d_wc_bc, d_bc_bc = _conv_bwd(dxs_a, dbc_a, p, w_xs, b_xs, w_bc, b_bc, s)

    segs = [(dz, OFF_Z, D), (dxs, OFF_XS, D), (dq, OFF_Q, D), (dk, OFF_K, D), (dv, OFF_V, D), (dbc, OFF_BC, 512),
            (ddtf, OFF_DTF, 128)]
    d_z, d_xs, d_q, d_k, d_v, d_bcw, d_dtf = [
        _mm_tn("d_w_in_%d" % i, x0, a, tm=1024, tn=min(w, 1024), ts=ts, pro=_modulate, aux=(sc1, sh1))
        for i, (a, _, w) in enumerate(segs)]
    d_w_in = dict(z=d_z, xs=d_xs, bc=d_bcw, dt=d_dtf[:, :NH], q=d_q, k=d_k, v=d_v, f=d_dtf[:, NH:2 * NH])
    sent = send_grads("in", [_shard_w_in_grad(d_w_in)])
    segs[-1] = (_after(ddtf, sent), OFF_DTF, 128)
    dh1 = _mm_nt("d_h1", [(a, w, 0) for a, _, w in segs], [(wcat, w, off // w) for _, off, w in segs], n=D,
                 tm=min(512, s), tn=1024, out_dtype=F32)
    grad_x, d_sc1, d_sh1 = _input_grad(dh1, du1, x0, sc1, s)

    return dict(
        loss=(0.5 / D) * jnp.sum(sq_err), grad_x=grad_x,
        d_mod=jnp.concatenate([d_sh1, d_sc1, d_g1, d_sh2, d_sc2, d_g2], axis=1),
        d_conv_w=jnp.concatenate([d_wc_xs[:4], d_wc_bc[:4]], axis=1), d_conv_b=jnp.concatenate([d_bc_xs, d_bc_bc], axis=1),
        d_ssm_norm_w=d_ssm_w, d_attn_norm_w=d_attn_w, d_ln1_g=d_ln1_g, d_ln1_b=d_ln1_b, d_ln2_g=d_ln2_g, d_ln2_b=d_ln2_b,
        d_gate_bias=d_bias, d_a_log=d_alog, d_d_skip=d_dskip)


W_IN_SEGS = [('z', W_Z, D), ('xs', W_XS, D), ('bc', W_BC, 512), ('dt', W_DT, NH), ('q', W_Q, D), ('k', W_K, D),
             ('v', W_V, D), ('f', W_F, NH)]
SHARD_W = IN_COLS // N_DEV


def _pack_w_in(shards):
    def cols(lo, hi):
        pieces = []
        while lo < hi:
            dev = lo // SHARD_W
            end = min(hi, (dev + 1) * SHARD_W)
            pieces.append(shards[dev][:, lo - dev * SHARD_W:end - dev * SHARD_W])
            lo = end
        return pieces

    seg = {n: cols(off, off + w) for n, off, w in W_IN_SEGS}
    pieces = seg['z'] + seg['xs'] + seg['q'] + seg['k'] + seg['v'] + seg['bc'] + seg['dt'] + seg['f']
    return jnp.concatenate(pieces + [jnp.zeros((D, 128 - 2 * NH), shards.dtype)], axis=1)


def _shard_w_in_grad(d_w_in):
    blocks = []
    for dev in range(N_DEV):
        lo, hi = dev * SHARD_W, (dev + 1) * SHARD_W
        pieces = [d_w_in[n][:, max(lo, off) - off:min(hi, off + w) - off] for n, off, w in W_IN_SEGS
                  if max(lo, off) < min(hi, off + w)]
        pieces.append(jnp.zeros((D, -SHARD_W % 128), pieces[0].dtype))
        blocks.append(jnp.concatenate(pieces, axis=1))
    return jnp.stack(blocks, axis=0)


WEIGHTS = ['w_ada', 'b_ada', 'w_in', 'conv_w', 'conv_b', 'dt_bias', 'a_log', 'd_skip', 'ssm_norm_w', 'f_bias',
           'attn_norm_w', 'w_out', 'ln1_g', 'ln1_b', 'w_ff_in', 'w_ff_out', 'ln2_g', 'ln2_b']
BIG = ['w_in', 'w_out', 'w_ff_in', 'w_ff_out']
SMALL_LAYOUT = [('b_ada', 0, 6 * D), ('conv_b', 12288, 1536), ('ssm_norm_w', 13824, D), ('attn_norm_w', 14848, D),
                ('ln1_g', 15872, D), ('ln1_b', 16896, D), ('ln2_g', 17920, D), ('ln2_b', 18944, D),
                ('dt_bias', 19968, NH), ('f_bias', 19968 + NH, NH), ('a_log', 20096, NH), ('d_skip', 20224, NH)]
SMALL_LOSS_LANE = 20352


def _pad_lanes(v, n=128):
    return jnp.pad(v, ((0, 0), (0, n - v.shape[1])))


def kernel(x, c, w_ada, b_ada, w_in, conv_w, conv_b, dt_bias, a_log, d_skip, ssm_norm_w, f_bias, attn_norm_w, w_out, ln1_g, ln1_b, w_ff_in, w_ff_out, ln2_g, ln2_b, loss_target, m_w_ada, m_b_ada, m_w_in, m_conv_w, m_conv_b, m_dt_bias, m_a_log, m_d_skip, m_ssm_norm_w, m_f_bias, m_attn_norm_w, m_w_out, m_ln1_g, m_ln1_b, m_w_ff_in, m_w_ff_out, m_ln2_g, m_ln2_b, v_w_ada, v_b_ada, v_w_in, v_conv_w, v_conv_b, v_dt_bias, v_a_log, v_d_skip, v_ssm_norm_w, v_f_bias, v_attn_norm_w, v_w_out, v_ln1_g, v_ln1_b, v_w_ff_in, v_w_ff_out, v_ln2_g, v_ln2_b):
    args = dict(locals())
    w = {n: args[n] for n in WEIGHTS}
    m = {n: args['m_' + n] for n in WEIGHTS}
    v = {n: args['v_' + n] for n in WEIGHTS}
    me = 4 * lax.axis_index("x") + 2 * lax.axis_index("y") + lax.axis_index("c")
    ada_cols = 6 * D // N_DEV
    conv_cols = conv_w.shape[2]

    c_all, conv_all = _exchange("gather_cond", [c, conv_w[0]], False)
    c_all = c_all.reshape(N_DEV, D)
    conv_w_full = conv_all.transpose(1, 0, 2).reshape(4, N_DEV * conv_cols)
    b_shard = lax.dynamic_slice(b_ada, (0, me * ada_cols), (1, ada_cols))
    mod_all, = _exchange("gather_mod", [_ada_mod(c_all, w_ada[0], b_shard)], False)
    mod = lax.dynamic_index_in_dim(mod_all, me, axis=1, keepdims=False).reshape(1, 6 * D)

    win_s = _gather_two_level("gather_w_in", _after(w_in[0].astype(BF16), mod * 0))
    first_done = win_s[0, 0:1, 0:1] * 0
    rest = _exchange_async("gather_rest", [_after(w[n][0].astype(BF16), first_done) for n in BIG[1:]], False, 1)

    def late_weights():
        wout_s, w1s, w2_s = rest()
        return wout_s.reshape(2 * D, D), w1s, w2_s.reshape(DFF, D)

    sends = {}

    def send_grads(tag, blocks):
        sends[tag] = _exchange_async("scatter_" + tag, blocks, True, {'late': 2, 'in': 3}[tag])
        return sum(b.reshape(-1)[0].astype(F32) * 0 for b in blocks)

    out = _local_step(x[0], loss_target[0], mod, _pack_w_in(win_s), late_weights, send_grads,
                      conv_w_full, conv_b, dt_bias, a_log, d_skip, ssm_norm_w, f_bias, attn_norm_w, ln1_g, ln1_b, ln2_g, ln2_b)

    small = jnp.concatenate(
        [out['d_mod'], out['d_conv_w'].reshape(1, -1), out['d_conv_b'], out['d_ssm_norm_w'], out['d_attn_norm_w'],
         out['d_ln1_g'], out['d_ln1_b'], out['d_ln2_g'], out['d_ln2_b'], out['d_gate_bias'], out['d_a_log'],
         out['d_d_skip'], _pad_lanes(out['loss'].reshape(1, 1))], axis=1)
    small_landed = _exchange_async("gather_small", [small], False, 4)
    (g_ff_in, g_ff_out, g_out), (g_in,) = sends['late'](), sends['in']()
    g_parts = dict(w_ff_in=g_ff_in, w_ff_out=g_ff_out, w_out=g_out, w_in=g_in)
    big = {n: _adamw("adamw_" + n, w[n][0], g_parts[n], m[n][0], v[n][0], tr=256, slots=True) for n in BIG}
    big_done = sum(big[n][1][0:1, 0:1] * 0 for n in BIG)
    small_all = _after(small_landed()[0], big_done)
    ssum, small_res = _small_update(small_all, SMALL_LAYOUT, w, m, v)
    dmod_all = small_all[:, 0, :6 * D]
    g_w_ada = _ada_grad(c_all, lax.dynamic_slice(dmod_all, (0, me * ada_cols), (N_DEV, ada_cols)))
    ada = _adamw("adamw_ada", w_ada[0], g_w_ada, m_w_ada[0], v_w_ada[0], tr=256, slots=False)
    g_conv_w = lax.dynamic_slice(ssum[:, 6 * D:6 * D + 4 * N_DEV * conv_cols].reshape(4, N_DEV * conv_cols),
                                 (0, me * conv_cols), (4, conv_cols))
    conv = _adamw("adamw_conv_w", conv_w[0], g_conv_w, m_conv_w[0], v_conv_w[0], tr=4, slots=False)

    results = []
    for k in range(4):
        vals = {n: small_res[n][k] for n in small_res}
        vals['w_ada'], vals['conv_w'] = ada[k][None], conv[k][None]
        for n in BIG:
            vals[n] = big[n][k][None]
        results.append(vals)
    return (ssum[0, SMALL_LOSS_LANE], out['grad_x'][None], *[res[n] for res in results for n in WEIGHTS])
```

```python
import functools

import jax
import jax.numpy as jnp
from jax import lax
from jax.experimental import pallas as pl
from jax.experimental.pallas import tpu as pltpu
from jax.experimental.pallas import tpu_sc as plsc

F32, BF16 = jnp.float32, jnp.bfloat16

N_DEV = 8
D = 1024
NH, HD = 16, 64
NSTATE = 128
CHUNK = 128
HG = 8
DFF = 4096
ALPHA = 2.0 ** 0.25
EPS = 1e-5
ATT_SCALE = HD ** -0.5

OFF_Z, OFF_XS, OFF_Q, OFF_K, OFF_V, OFF_BC, OFF_DTF = 0, 1024, 2048, 3072, 4096, 5120, 5632
PCOLS = 5760
W_Z, W_XS, W_BC, W_DT, W_Q, W_K, W_V, W_F = 0, 1024, 2048, 2560, 2576, 3600, 4624, 5648
IN_COLS = 5664

ADAM_LR, ADAM_B1, ADAM_B2, ADAM_EPS, ADAM_WD, ADAM_STEP = 0.001, 0.9, 0.999, 1e-08, 0.01, 10

VMEM_LIMIT = 56 << 20

NN = (((1,), (0,)), ((), ()))
NT = (((1,), (1,)), ((), ()))
TN = (((0,), (0,)), ((), ()))


def _dot(a, b, dims=NN):
    return lax.dot_general(a, b, dims, preferred_element_type=F32)


def _bdot(a, b, dims=NN):
    return _dot(a.astype(BF16), b.astype(BF16), dims)


def _split3(v, terms=3):
    parts, rest = [], v
    for _ in range(terms):
        p = rest.astype(BF16)
        parts.append(p)
        rest = rest - p.astype(F32)
    return parts


def _sel_left(m01, v):
    return sum(_dot(m01, p) for p in _split3(v))


def _sel_right(v, m01, dims=NN, terms=3):
    return sum(_dot(p, m01, dims) for p in _split3(v, terms))


def _iota(shape, dim):
    return lax.broadcasted_iota(jnp.int32, shape, dim)


def _tri_lower(n):
    return (_iota((n, n), 1) <= _iota((n, n), 0)).astype(BF16)


def _tri_upper(n):
    return (_iota((n, n), 1) >= _iota((n, n), 0)).astype(BF16)


def _head_expand():
    return (lax.shift_right_logical(_iota((128, D), 1), 6) == _iota((128, D), 0)).astype(BF16)


def _head_reduce():
    return (lax.shift_right_logical(_iota((D, 128), 0), 6) == _iota((D, 128), 1)).astype(BF16)


def _sigmoid(x):
    return 1.0 / (1.0 + jnp.exp(-x))


def _silu(x):
    return x * _sigmoid(x)


def _dsilu(x):
    s = _sigmoid(x)
    return s * (1.0 + x * (1.0 - s))


def _softplus(x):
    return jnp.maximum(x, 0.0) + jnp.log(1.0 + jnp.exp(-jnp.abs(x)))


def _log_sigmoid(x):
    return jnp.minimum(x, 0.0) - jnp.log(1.0 + jnp.exp(-jnp.abs(x)))


def _params(sem):
    return pltpu.CompilerParams(dimension_semantics=sem, vmem_limit_bytes=VMEM_LIMIT)


def _mm_nn(name, a, b, *, tm, tn, tk, out_dtype, pro=None, aux=(), epi=None):
    m, k_all = a.shape
    b_sharded = b.ndim == 3
    n = b.shape[0] * b.shape[2] if b_sharded else b.shape[1]
    assert not b_sharded or tn == b.shape[2]
    nk = k_all // tk
    n_aux = len(aux)
    b_spec = (pl.BlockSpec((None, tk, tn), lambda i, j, k: (j, k, 0)) if b_sharded
              else pl.BlockSpec((tk, tn), lambda i, j, k: (k, j)))

    def body(a_ref, b_ref, *rest):
        aux_refs, o_ref = rest[:n_aux], rest[n_aux]
        at = a_ref[...]
        if pro is not None:
            at = pro(at, *[r[...] for r in aux_refs])
        part = _bdot(at, b_ref[...])
        if nk == 1:
            o_ref[...] = (part if epi is None else epi(part)).astype(out_dtype)
            return
        assert epi is None
        acc_ref = rest[n_aux + 1]
        kk = pl.program_id(2)

        @pl.when(kk == 0)
        def _():
            acc_ref[...] = part

        @pl.when(kk > 0)
        def _():
            acc_ref[...] += part

        @pl.when(kk == nk - 1)
        def _():
            o_ref[...] = acc_ref[...].astype(out_dtype)

    return pl.pallas_call(
        body, name=name,
        grid=(m // tm, n // tn, nk),
        in_specs=[pl.BlockSpec((tm, tk), lambda i, j, k: (i, k)), b_spec]
        + [pl.BlockSpec((1, tk), lambda i, j, k: (0, k)) for _ in aux],
        out_specs=pl.BlockSpec((tm, tn), lambda i, j, k: (i, j)),
        out_shape=jax.ShapeDtypeStruct((m, n), out_dtype),
        scratch_shapes=[] if nk == 1 else [pltpu.VMEM((tm, tn), F32)],
        compiler_params=_params(("parallel", "parallel", "arbitrary")),
    )(a, b, *aux)


def _mm_nt(name, a_list, b_list, *, n, tm, tn, out_dtype, epi=None, epi_aux=()):
    m = a_list[0][0].shape[0]
    n_op = len(a_list)
    n_epi = len(epi_aux)

    def body(*refs):
        a_refs, b_refs = refs[:n_op], refs[n_op:2 * n_op]
        e_refs, o_ref = refs[2 * n_op:2 * n_op + n_epi], refs[2 * n_op + n_epi]
        acc = None
        for a_ref, b_ref in zip(a_refs, b_refs):
            part = _bdot(a_ref[...], b_ref[...], NT)
            acc = part if acc is None else acc + part
        if epi is not None:
            acc = epi(acc, *[r[...] for r in e_refs])
        o_ref[...] = acc.astype(out_dtype)

    in_specs = [pl.BlockSpec((tm, w), functools.partial(lambda i, j, cb: (i, cb), cb=cb)) for (_, w, cb) in a_list]
    for (b, w, cb) in b_list:
        if b.ndim == 3:
            in_specs.append(pl.BlockSpec((None, tn, w), functools.partial(lambda i, j, cb: (cb, j, 0), cb=cb)))
        else:
            in_specs.append(pl.BlockSpec((tn, w), functools.partial(lambda i, j, cb: (j, cb), cb=cb)))
    in_specs += [pl.BlockSpec((tm, tn), lambda i, j: (i, j)) for _ in epi_aux]
    return pl.pallas_call(
        body, name=name,
        grid=(m // tm, n // tn),
        in_specs=in_specs,
        out_specs=pl.BlockSpec((tm, tn), lambda i, j: (i, j)),
        out_shape=jax.ShapeDtypeStruct((m, n), out_dtype),
        compiler_params=_params(("parallel", "parallel")),
    )(*[a for (a, _, _) in a_list], *[b for (b, _, _) in b_list], *epi_aux)


def _mm_tn(name, a, b, *, tm, tn, ts, pro=None, aux=(), col_shards=False):
    s_all, ka = a.shape
    nb = b.shape[1]
    n_aux = len(aux)
    ns = s_all // ts
    assert not col_shards or tn == nb // N_DEV

    def body(a_ref, b_ref, *rest):
        aux_refs, o_ref, acc_ref = rest[:n_aux], rest[n_aux], rest[n_aux + 1]
        at = a_ref[...]
        if pro is not None:
            at = pro(at, *[r[...] for r in aux_refs])
        part = _bdot(at, b_ref[...], TN)
        ss = pl.program_id(2)

        @pl.when(ss == 0)
        def _():
            acc_ref[...] = part

        @pl.when(ss > 0)
        def _():
            acc_ref[...] += part

        @pl.when(ss == ns - 1)
        def _():
            o_ref[...] = acc_ref[...].astype(BF16)

    if col_shards:
        out_spec = pl.BlockSpec((None, tm, tn), lambda i, j, s: (j, i, 0))
        out_shape = jax.ShapeDtypeStruct((N_DEV, ka, tn), BF16)
    else:
        out_spec = pl.BlockSpec((tm, tn), lambda i, j, s: (i, j))
        out_shape = jax.ShapeDtypeStruct((ka, nb), BF16)
    return pl.pallas_call(
        body, name=name,
        grid=(ka // tm, nb // tn, ns),
        in_specs=[pl.BlockSpec((ts, tm), lambda i, j, s: (s, i)),
                  pl.BlockSpec((ts, tn), lambda i, j, s: (s, j))]
        + [pl.BlockSpec((1, tm), lambda i, j, s: (0, i)) for _ in aux],
        out_specs=out_spec, out_shape=out_shape,
        scratch_shapes=[pltpu.VMEM((tm, tn), F32)],
        compiler_params=_params(("parallel", "parallel", "arbitrary")),
    )(a, b, *aux)


def _rowk(name, fn, n_rows, tr, rows, fulls, outs, accs, reverse=False):
    n = n_rows // tr
    n_row, n_full, n_out, n_acc = len(rows), len(fulls), len(outs), len(accs)

    def pos(i):
        return (n - 1 - i) if reverse else i

    def body(*refs):
        row_refs = refs[:n_row]
        full_refs = refs[n_row:n_row + n_full]
        out_refs = refs[n_row + n_full:n_row + n_full + n_out]
        acc_refs = refs[n_row + n_full + n_out:]
        i = pl.program_id(0)

        @pl.when(i == 0)
        def _():
            for r in acc_refs:
                r[...] = jnp.zeros(r.shape, r.dtype)

        res = fn(pos(i), *[r[...] for r in row_refs], *[r[...] for r in full_refs], *[r[...] for r in acc_refs])
        for r, v in zip(out_refs + acc_refs, res):
            r[...] = v.astype(r.dtype)

    def row_map(i, cb, shift):
        return (jnp.clip(pos(i) + shift, 0, n - 1), cb)

    def halo_map(i, cb, shift):
        tile = jnp.clip(pos(i) + shift, 0, n - 1)
        return (tile * (tr // 8) + (tr // 8 - 1 if shift < 0 else 0), cb)

    in_specs = [pl.BlockSpec((tr, w), functools.partial(row_map, cb=cb, shift=sh)) if sh == 0 else
                pl.BlockSpec((8, w), functools.partial(halo_map, cb=cb, shift=sh)) for (_, w, cb, sh) in rows]
    in_specs += [pl.BlockSpec(f.shape, functools.partial(lambda i, nd: (0,) * nd, nd=f.ndim)) for f in fulls]
    out_specs = [pl.BlockSpec((tr, w), lambda i: (pos(i), 0)) for (w, _) in outs]
    out_specs += [pl.BlockSpec((r, w), lambda i: (0, 0)) for (r, w) in accs]
    out_shape = [jax.ShapeDtypeStruct((n_rows, w), dt) for (w, dt) in outs]
    out_shape += [jax.ShapeDtypeStruct((r, w), F32) for (r, w) in accs]
    return pl.pallas_call(
        body, name=name, grid=(n,), in_specs=in_specs, out_specs=out_specs, out_shape=out_shape,
        compiler_params=_params(("arbitrary",)),
    )(*[a for (a, _, _, _) in rows], *fulls)


def _colsum(x):
    return jnp.sum(x, axis=0, keepdims=True)


def _mean(x):
    return jnp.mean(x, axis=-1, keepdims=True)


def _modulate(x, sc, sh):
    return x * (1.0 + sc) + sh


def _shift_down(cur, prev8, j):
    tr = cur.shape[0]
    row8 = _iota(prev8.shape, 0)
    head = jnp.where(row8 < j, pltpu.roll(prev8, j, 0), pltpu.roll(cur[0:8], j, 0))
    return head if tr == 8 else jnp.concatenate([head, pltpu.roll(cur, j, 0)[8:]], axis=0)


def _shift_up(cur, next8, j):
    tr = cur.shape[0]
    row8 = _iota(next8.shape, 0)
    tail = jnp.where(row8 < 8 - j, pltpu.roll(cur[tr - 8:], 8 - j, 0), pltpu.roll(next8, 8 - j, 0))
    return jnp.concatenate([pltpu.roll(cur, tr - j, 0)[:tr - 8], tail], axis=0)


def _conv(cur, prev, w, b):
    out = cur * w[3:4] + b
    for j in (1, 2, 3):
        out = out + _shift_down(cur, prev, j) * w[3 - j:4 - j]
    return out


def _conv_fwd(p, w_xs, b_xs, w_bc, b_bc, s):
    def fn(pos, xs, xs_prev, bc, bc_prev, w_xs, b_xs, w_bc, b_bc):
        first = pos == 0
        xs_prev = jnp.where(first, 0.0, xs_prev)
        bc_prev = jnp.where(first, 0.0, bc_prev)
        return _silu(_conv(xs, xs_prev, w_xs, b_xs)), _silu(_conv(bc, bc_prev, w_bc, b_bc))

    return _rowk("conv_fwd", fn, s, 512,
                 [(p, D, OFF_XS // D, 0), (p, D, OFF_XS // D, -1), (p, 512, OFF_BC // 512, 0), (p, 512, OFF_BC // 512, -1)],
                 [w_xs, b_xs, w_bc, b_bc], [(D, F32), (512, F32)], [])


def _conv_bwd(dxs_a, dbc_a, p, w_xs, b_xs, w_bc, b_bc, s):
    tr = 512
    n = s // tr

    def fn(pos, da1, da1n, x1, x1p, x1n, da2, da2n, x2, x2p, x2n, w1, b1, w2, b2, aw1, ab1, aw2, ab2):
        dx1, dw1, db1 = _conv_bwd_fn(pos, n, da1, da1n, x1, x1p, x1n, w1, b1)
        dx2, dw2, db2 = _conv_bwd_fn(pos, n, da2, da2n, x2, x2p, x2n, w2, b2)
        return dx1, dx2, aw1 + dw1, ab1 + db1, aw2 + dw2, ab2 + db2

    cx, cb = OFF_XS // D, OFF_BC // 512
    return _rowk("conv_bwd", fn, s, tr,
                 [(dxs_a, D, 0, 0), (dxs_a, D, 0, 1), (p, D, cx, 0), (p, D, cx, -1), (p, D, cx, 1),
                  (dbc_a, 512, 0, 0), (dbc_a, 512, 0, 1), (p, 512, cb, 0), (p, 512, cb, -1), (p, 512, cb, 1)],
                 [w_xs, b_xs, w_bc, b_bc], [(D, BF16), (512, BF16)], [(8, D), (1, D), (8, 512), (1, 512)])


def _conv_bwd_fn(pos, n, da, da_next, x, x_prev, x_next, w, b):
    first, last = pos == 0, pos == n - 1
    x_prev = jnp.where(first, 0.0, x_prev)
    shifted = {j: _shift_down(x, x_prev, j) for j in (1, 2, 3)}
    conv = x * w[3:4] + b
    for j in (1, 2, 3):
        conv = conv + shifted[j] * w[3 - j:4 - j]
    dc = da * _dsilu(conv)
    dc_next = jnp.where(last, 0.0, da_next * _dsilu(_conv(x_next, x[x.shape[0] - 8:], w, b)))
    dx = dc * w[3:4]
    dws = [None] * 4
    dws[3] = _colsum(dc * x)
    for j in (1, 2, 3):
        dx = dx + _shift_up(dc, dc_next, j) * w[3 - j:4 - j]
        dws[3 - j] = _colsum(dc * shifted[j])
    row = _iota((8, x.shape[1]), 0)
    dw = jnp.zeros((8, x.shape[1]), F32)
    for k in range(4):
        dw = jnp.where(row == k, dws[k], dw)
    return dx, dw, _colsum(dc)


def _ssd_gates(dtf, bias, a_log):
    lane = _iota(dtf.shape, 1)
    head = lane < NH
    dt = jnp.where(head, _softplus(dtf + bias), 0.0)
    a_neg = jnp.where(_iota(a_log.shape, 1) < NH, -jnp.exp(a_log), 0.0)
    a = dt * a_neg
    cs = _sel_left(_tri_lower(CHUNK), a)
    return dt, a_neg, cs


def _decay_mask(cs_ref, cst_ref, h):
    diff = cs_ref[:, h:h + 1] - cst_ref[h:h + 1, :]
    low = _iota((CHUNK, CHUNK), 1) <= _iota((CHUNK, CHUNK), 0)
    return jnp.where(low, jnp.exp(jnp.minimum(diff, 0.0)), 0.0)


def _ssd_fwd(xs_a, bc_a, p, bias128, alog128, dskip_x, s):
    nc = s // CHUNK
    t = CHUNK

    def body(xs_ref, bc_ref, dtf_ref, bias_ref, alog_ref, dsk_ref, y_ref, st_ref,
             state, x_sc, xw_sc, cs_sc, cst_sc, yd_sc):
        c = pl.program_id(0)

        @pl.when(c == 0)
        def _():
            state[...] = jnp.zeros(state.shape, F32)

        dt, _, cs = _ssd_gates(dtf_ref[...], bias_ref[...], alog_ref[...])
        cs_sc[...] = cs
        cst_sc[...] = cs.T
        cs_last = cs[t - 1:t, :]
        expand = _head_expand()
        ex = _sel_right(jnp.concatenate([dt, jnp.exp(cs), jnp.exp(cs_last - cs)], axis=0), expand, terms=2)
        dt_x, eo_x, we_x = ex[0:t], ex[t:2 * t], ex[2 * t:3 * t]
        g_x = _sel_right(jnp.broadcast_to(jnp.exp(cs_last), (8, 128)), expand)[0:1]
        xs = xs_ref[...]
        x = xs * dt_x
        x_sc[...] = x.astype(BF16)
        xw_sc[...] = (x * we_x).astype(BF16)
        prev = state[...]
        st_ref[0] = prev
        prev_b = prev.astype(BF16)
        for g in range(2):
            cols = slice(g * 512, (g + 1) * 512)
            b_g = bc_ref[:, g * 128:(g + 1) * 128].astype(BF16)
            c_g = bc_ref[:, 256 + g * 128:256 + (g + 1) * 128].astype(BF16)
            gmat = _dot(c_g, b_g, NT)
            y_off = _dot(c_g, prev_b[:, cols]) * eo_x[:, cols]
            s_loc = _dot(b_g, xw_sc[:, cols], TN)
            state[:, cols] = g_x[:, cols] * prev[:, cols] + s_loc
            for e in range(HG):
                h = g * HG + e
                m = gmat * _decay_mask(cs_sc, cst_sc, h)
                yd_sc[:, h * HD:(h + 1) * HD] = _dot(m.astype(BF16), x_sc[:, h * HD:(h + 1) * HD])
            y_ref[:, cols] = yd_sc[:, cols] + y_off + dsk_ref[:, cols] * xs[:, cols]

    return pl.pallas_call(
        body, name="ssd_fwd", grid=(nc,),
        in_specs=[pl.BlockSpec((t, D), lambda c: (c, 0)),
                  pl.BlockSpec((t, 512), lambda c: (c, 0)),
                  pl.BlockSpec((t, 128), lambda c: (c, OFF_DTF // 128)),
                  pl.BlockSpec((1, 128), lambda c: (0, 0)),
                  pl.BlockSpec((1, 128), lambda c: (0, 0)),
                  pl.BlockSpec((1, D), lambda c: (0, 0))],
        out_specs=[pl.BlockSpec((t, D), lambda c: (c, 0)),
                   pl.BlockSpec((1, NSTATE, D), lambda c: (c, 0, 0))],
        out_shape=[jax.ShapeDtypeStruct((s, D), F32), jax.ShapeDtypeStruct((nc, NSTATE, D), F32)],
        scratch_shapes=[pltpu.VMEM((NSTATE, D), F32), pltpu.VMEM((t, D), BF16), pltpu.VMEM((t, D), BF16),
                        pltpu.VMEM((t, 128), F32), pltpu.VMEM((128, t), F32), pltpu.VMEM((t, D), F32)],
        compiler_params=_params(("arbitrary",)),
    )(xs_a, bc_a, p, bias128, alog128, dskip_x)


def _ssd_bwd(dy, xs_a, bc_a, p, states, bias128, alog128, dskip_x, s):
    nc = s // CHUNK
    t = CHUNK

    def body(dy_ref, xs_ref, bc_ref, dtf_ref, st_ref, bias_ref, alog_ref, dsk_ref,
             dxs_ref, dbc_ref, ddt_ref, dalog_ref, dskip_ref,
             dstate, x_sc, dy_sc, dx_sc, deo_sc, dwe_sc, cs_sc, cst_sc, dcol_sc, drow_sc):
        i = pl.program_id(0)

        @pl.when(i == 0)
        def _():
            dstate[...] = jnp.zeros(dstate.shape, F32)
            dalog_ref[...] = jnp.zeros(dalog_ref.shape, F32)
            dskip_ref[...] = jnp.zeros(dskip_ref.shape, F32)

        dtf = dtf_ref[...]
        dt, a_neg, cs = _ssd_gates(dtf, bias_ref[...], alog_ref[...])
        cs_sc[...] = cs
        cst_sc[...] = cs.T
        cs_last = cs[t - 1:t, :]
        eo, we, g_end = jnp.exp(cs), jnp.exp(cs_last - cs), jnp.exp(cs_last)
        expand, reduce = _head_expand(), _head_reduce()
        ex = _sel_right(jnp.concatenate([dt, eo, we], axis=0), expand, terms=2)
        dt_x, eo_x, we_x = ex[0:t], ex[t:2 * t], ex[2 * t:3 * t]
        g_x = _sel_right(jnp.broadcast_to(g_end, (8, 128)), expand)[0:1]
        xs = xs_ref[...]
        dyv = dy_ref[...]
        x = xs * dt_x
        x_sc[...] = x.astype(BF16)
        dy_sc[...] = dyv.astype(BF16)
        dyo_b = (dyv * eo_x).astype(BF16)
        xw_b = (x * we_x).astype(BF16)
        prev = st_ref[0]
        prev_b = prev.astype(BF16)
        dnext = dstate[...]
        dnext_b = dnext.astype(BF16)
        dcol_sc[...] = jnp.zeros(dcol_sc.shape, F32)
        drow_sc[...] = jnp.zeros(drow_sc.shape, F32)
        lane_row = _iota((1, 128), 1)
        sub_col = _iota((128, 1), 0)
        for g in range(2):
            cols = slice(g * 512, (g + 1) * 512)
            b_g = bc_ref[:, g * 128:(g + 1) * 128].astype(BF16)
            c_g = bc_ref[:, 256 + g * 128:256 + (g + 1) * 128].astype(BF16)
            gmat = _dot(c_g, b_g, NT)
            b_ds = _dot(b_g, dnext_b[:, cols])
            c_s = _dot(c_g, prev_b[:, cols])
            dx_sc[:, cols] = b_ds * we_x[:, cols]
            deo_sc[:, cols] = dyv[:, cols] * c_s
            dwe_sc[:, cols] = b_ds * x[:, cols]
            db = _dot(xw_b[:, cols], dnext_b[:, cols], NT)
            dc = _dot(dyo_b[:, cols], prev_b[:, cols], NT)
            dstate[:, cols] = g_x[:, cols] * dnext[:, cols] + _dot(c_g, dyo_b[:, cols], TN)
            dg = jnp.zeros((t, t), F32)
            for e in range(HG):
                h = g * HG + e
                hc = slice(h * HD, (h + 1) * HD)
                lmat = _decay_mask(cs_sc, cst_sc, h)
                m = gmat * lmat
                dx_sc[:, hc] += _dot(m.astype(BF16), dy_sc[:, hc], TN)
                dm = _dot(dy_sc[:, hc], x_sc[:, hc], NT)
                dg = dg + dm * lmat
                qm = dm * m
                dcol_sc[...] += jnp.sum(qm, axis=1, keepdims=True) * (lane_row == h).astype(F32)
                drow_sc[...] += (sub_col == h).astype(F32) * jnp.sum(qm, axis=0, keepdims=True)
            dg_b = dg.astype(BF16)
            dbc_ref[:, g * 128:(g + 1) * 128] = db + _dot(dg_b, c_g, TN)
            dbc_ref[:, 256 + g * 128:256 + (g + 1) * 128] = dc + _dot(dg_b, b_g)
        d_eo = _sel_right(deo_sc[...], reduce, terms=2)
        d_we = _sel_right(dwe_sc[...], reduce, terms=2)
        d_gend = _sel_right(jnp.broadcast_to(_colsum(dnext * prev), (8, D)), reduce)[0:1]
        d_cs = dcol_sc[...] - drow_sc[...].T + d_eo * eo - d_we * we
        extra = _colsum(d_we * we) + d_gend * g_end
        d_cs = d_cs + jnp.where(_iota((t, 128), 0) == t - 1, extra, 0.0)
        da = _sel_left(_tri_upper(t), d_cs)
        dx = dx_sc[...]
        ddt = _sel_right(dx * xs, reduce, terms=2) + da * a_neg
        dxs_ref[...] = dx * dt_x + dsk_ref[...] * dyv
        ddt_ref[...] = jnp.where(_iota((t, 128), 1) < NH, ddt * _sigmoid(dtf + bias_ref[...]), 0.0)
        dalog_ref[...] += _colsum(da * dt) * a_neg
        dskip_ref[...] += _sel_right(jnp.broadcast_to(_colsum(dyv * xs), (8, D)), reduce)[0:1]

    rev = lambda i: nc - 1 - i
    return pl.pallas_call(
        body, name="ssd_bwd", grid=(nc,),
        in_specs=[pl.BlockSpec((t, D), lambda i: (rev(i), 0)),
                  pl.BlockSpec((t, D), lambda i: (rev(i), 0)),
                  pl.BlockSpec((t, 512), lambda i: (rev(i), 0)),
                  pl.BlockSpec((t, 128), lambda i: (rev(i), OFF_DTF // 128)),
                  pl.BlockSpec((1, NSTATE, D), lambda i: (rev(i), 0, 0)),
                  pl.BlockSpec((1, 128), lambda i: (0, 0)),
                  pl.BlockSpec((1, 128), lambda i: (0, 0)),
                  pl.BlockSpec((1, D), lambda i: (0, 0))],
        out_specs=[pl.BlockSpec((t, D), lambda i: (rev(i), 0)),
                   pl.BlockSpec((t, 512), lambda i: (rev(i), 0)),
                   pl.BlockSpec((t, 128), lambda i: (rev(i), 0)),
                   pl.BlockSpec((1, 128), lambda i: (0, 0)),
                   pl.BlockSpec((1, 128), lambda i: (0, 0))],
        out_shape=[jax.ShapeDtypeStruct((s, D), F32), jax.ShapeDtypeStruct((s, 512), F32),
                   jax.ShapeDtypeStruct((s, 128), F32), jax.ShapeDtypeStruct((1, 128), F32),
                   jax.ShapeDtypeStruct((1, 128), F32)],
        scratch_shapes=[pltpu.VMEM((NSTATE, D), F32), pltpu.VMEM((t, D), BF16), pltpu.VMEM((t, D), BF16),
                        pltpu.VMEM((t, D), F32), pltpu.VMEM((t, D), F32), pltpu.VMEM((t, D), F32),
                        pltpu.VMEM((t, 128), F32), pltpu.VMEM((128, t), F32),
                        pltpu.VMEM((t, 128), F32), pltpu.VMEM((128, t), F32)],
        compiler_params=_params(("arbitrary",)),
    )(dy, xs_a, bc_a, p, states, bias128, alog128, dskip_x)


def _gate_lanes(shape):
    lane = _iota(shape, 1)
    return (lane >= NH) & (lane < 2 * NH)


def _cum_fwd(p, bias128, s):
    tr = min(512, s)

    def body(dtf_ref, bias_ref, o_ref, carry):
        @pl.when(pl.program_id(0) == 0)
        def _():
            carry[...] = jnp.zeros(carry.shape, F32)

        lf = jnp.where(_gate_lanes((tr, 128)), _log_sigmoid(dtf_ref[...] + bias_ref[...]), 0.0)
        cum = _sel_left(_tri_lower(tr), lf) + carry[...]
        carry[...] = cum[tr - 1:tr, :]
        o_ref[...] = cum

    return pl.pallas_call(
        body, name="cum_fwd", grid=(s // tr,),
        in_specs=[pl.BlockSpec((tr, 128), lambda i: (i, OFF_DTF // 128)), pl.BlockSpec((1, 128), lambda i: (0, 0))],
        out_specs=pl.BlockSpec((tr, 128), lambda i: (i, 0)),
        out_shape=jax.ShapeDtypeStruct((s, 128), F32),
        scratch_shapes=[pltpu.VMEM((1, 128), F32)],
        compiler_params=_params(("arbitrary",)),
    )(p, bias128)


def _cum_bwd(dcum, ddt_raw, p, bias128, s):
    tr = min(512, s)

    def fn(pos, dcum, ddt, dtf, bias, carry, acc):
        suffix = _sel_left(_tri_upper(tr), dcum) + carry
        dfr = jnp.where(_gate_lanes((tr, 128)), suffix * _sigmoid(-(dtf + bias)), 0.0)
        out = ddt + dfr
        return out, suffix[0:1, :], acc + _colsum(out)

    return _rowk("cum_bwd", fn, s, tr, [(dcum, 128, 0, 0), (ddt_raw, 128, 0, 0), (p, 128, OFF_DTF // 128, 0)],
                 [bias128], [(128, BF16)], [(1, 128), (1, 128)], reverse=True)


ATT_BLOCK = 512
ATT_STRIP = 32


def _head_part(shape, h, dim):
    i = _iota(shape, dim)
    return (i >= h * HD) & (i < (h + 1) * HD)


def _k_augmented(k_blk, cum_blk, j, h):
    tk = k_blk.shape[0]
    lane = _iota((tk, 128), 1)
    col = jnp.sum(jnp.where(lane == NH + 2 * j + h, cum_blk, 0.0), axis=1, keepdims=True)
    c0, c1, c2 = [c.astype(F32) for c in _split3(-col)]
    k_h = k_blk if h == 0 else pltpu.roll(k_blk, HD, 1)
    aug = jnp.where(lane == HD, c0, jnp.where(lane == HD + 1, c1, jnp.where(lane == HD + 2, c2, 0.0)))
    return jnp.where(lane < HD, k_h, aug).astype(BF16)


def _q_augmented_t(q_blk):
    tq = q_blk.shape[0]
    q_t = (q_blk * ATT_SCALE).T.astype(BF16)
    ones = (_iota((HD, tq), 0) < 3).astype(BF16)
    return [jnp.concatenate([q_t[h * HD:(h + 1) * HD], ones], axis=0) for h in range(2)]


def _rows01(r0, r1):
    sub = _iota((8, r0.shape[1]), 0)
    return jnp.where(sub == 0, r0, jnp.where(sub == 1, r1, 0.0))


def _fold8(x, op, cur):
    for g in range(x.shape[0] // 8):
        cur = op(cur, x[8 * g:8 * (g + 1), :])
    return cur


def _attn_fwd(p, cum, s):
    t = min(ATT_BLOCK, s)
    nq = s // t
    r = ATT_STRIP

    def body(q_ref, k_ref, v_ref, c_ref, o_ref, lse_ref, kaug_sc, vt_sc, s0_sc, s1_sc, p0_sc, p1_sc, m_sc, l_sc, acc_sc):
        j, qi = pl.program_id(0), pl.program_id(1)
        s_sc, p_sc = (s0_sc, s1_sc), (p0_sc, p1_sc)

        @pl.when(qi == 0)
        def _():
            for c in range(nq):
                rows = slice(c * t, (c + 1) * t)
                k_blk, vt = k_ref[rows, :], v_ref[rows, :].T
                for h in range(2):
                    kaug_sc[h, rows, :] = _k_augmented(k_blk, c_ref[rows, :], j, h)
                    vt_sc[h, :, rows] = vt[h * HD:(h + 1) * HD].astype(BF16)

        qaug_t = _q_augmented_t(q_ref[...])
        m_sc[...] = jnp.full(m_sc.shape, -1e30, F32)
        l_sc[...] = jnp.zeros(l_sc.shape, F32)
        acc_sc[...] = jnp.zeros(acc_sc.shape, F32)
        top = _iota((128, t), 0) < HD

        def logits(kb, buf):
            kv = pl.ds(pl.multiple_of(kb * t, t), t)
            for h in range(2):
                s_sc[buf][h] = _dot(kaug_sc[h, kv, :], qaug_t[h])

        def softmax(buf, diagonal):
            alphas = []
            for h in range(2):
                cur = jnp.full((8, t), -1e30, F32)
                for i in range(t // r):
                    rows = slice(i * r, (i + 1) * r)
                    x = s_sc[buf][h, rows, :]
                    if diagonal:
                        x = jnp.where(_iota((r, t), 1) >= i * r + _iota((r, t), 0), x, -1e30)
                        s_sc[buf][h, rows, :] = x
                    cur = _fold8(x, jnp.maximum, cur)
                m_prev = m_sc[h, 0:1, :]
                m_new = jnp.maximum(m_prev, jnp.max(cur, axis=0, keepdims=True))
                alpha = jnp.exp(m_prev - m_new)
                m_sc[h, 0:1, :] = m_new
                alphas.append(alpha)
                tot = jnp.zeros((8, t), F32)
                for i in range(t // r):
                    rows = slice(i * r, (i + 1) * r)
                    pr = jnp.exp(s_sc[buf][h, rows, :] - m_new)
                    p_sc[buf][h, rows, :] = pr.astype(BF16)
                    tot = _fold8(pr, jnp.add, tot)
                l_sc[h, 0:1, :] = alpha * l_sc[h, 0:1, :] + jnp.sum(tot, axis=0, keepdims=True)
            return alphas

        def accumulate(kb, buf, alphas):
            kv = pl.ds(pl.multiple_of(kb * t, t), t)
            for h in range(2):
                part = slice(h * HD, (h + 1) * HD)
                acc_sc[part, :] = acc_sc[part, :] * alphas[h] + _dot(vt_sc[h, :, kv], p_sc[buf][h])

        def first_trip():
            logits(0, 1)
            accumulate(qi, 0, softmax(0, True))
            logits(jnp.minimum(1, qi - 1), 0)
            return tuple(softmax(1, False))

        def only_diagonal():
            accumulate(qi, 0, softmax(0, True))
            return (jnp.ones((1, t), F32),) * 2

        def steady(u, alphas_b):
            accumulate(2 * u - 2, 1, alphas_b)
            logits(2 * u, 1)
            accumulate(2 * u - 1, 0, softmax(0, False))
            logits(jnp.minimum(2 * u + 1, qi - 1), 0)
            return tuple(softmax(1, False))

        logits(qi, 0)
        n_blocks = qi + 1
        alphas_b = lax.cond(qi >= 1, first_trip, only_diagonal)
        alphas_b = lax.fori_loop(1, n_blocks // 2, steady, alphas_b)
        last_b = 2 * (n_blocks // 2) - 2

        @pl.when((qi >= 1) & (n_blocks % 2 == 0))
        def _():
            accumulate(last_b, 1, alphas_b)

        @pl.when((qi >= 2) & (n_blocks % 2 == 1))
        def _():
            accumulate(last_b, 1, alphas_b)
            accumulate(qi - 1, 0, softmax(0, False))

        l0, l1 = l_sc[0, 0:1, :], l_sc[1, 0:1, :]
        o_ref[...] = (acc_sc[...] / jnp.where(top, l0, l1)).T
        lse_ref[0] = _rows01(m_sc[0, 0:1, :] + jnp.log(l0), m_sc[1, 0:1, :] + jnp.log(l1))

    return pl.pallas_call(
        body, name="attn_fwd", grid=(NH // 2, nq),
        in_specs=[pl.BlockSpec((t, 128), lambda j, qi: (qi, OFF_Q // 128 + j)),
                  pl.BlockSpec((s, 128), lambda j, qi: (0, OFF_K // 128 + j)),
                  pl.BlockSpec((s, 128), lambda j, qi: (0, OFF_V // 128 + j)),
                  pl.BlockSpec((s, 128), lambda j, qi: (0, 0))],
        out_specs=[pl.BlockSpec((t, 128), lambda j, qi: (qi, j)),
                   pl.BlockSpec((1, 8, t), lambda j, qi: (j, 0, qi))],
        out_shape=[jax.ShapeDtypeStruct((s, D), F32), jax.ShapeDtypeStruct((NH // 2, 8, s), F32)],
        scratch_shapes=[pltpu.VMEM((2, s, 128), BF16), pltpu.VMEM((2, HD, s), BF16), pltpu.VMEM((2, t, t), F32),
                        pltpu.VMEM((2, t, t), F32), pltpu.VMEM((2, t, t), BF16), pltpu.VMEM((2, t, t), BF16),
                        pltpu.VMEM((2, 8, t), F32), pltpu.VMEM((2, 8, t), F32), pltpu.VMEM((128, t), F32)],
        compiler_params=_params(("parallel", "arbitrary")),
    )(p, p, p, cum)


def _attn_bwd(p, cum, o, lse, do, s):
    t = min(ATT_BLOCK, s)
    nq = s // t
    r = ATT_STRIP

    def body(q_ref, k_ref, v_ref, c_ref, o_ref, lse_ref, do_ref, dq_ref, dk_ref, dv_ref, dc_ref, dr_ref,
             qaugt_sc, qh_sc, dot_sc, doh_sc, delta_sc, dqt_sc, dr_sc, kaug_sc, vh_sc, kt_sc,
             s0_sc, s1_sc, dp0_sc, dp1_sc, p0_sc, p1_sc, ds0_sc, ds1_sc, dk_sc, dv_sc, dc_sc):
        j, ki = pl.program_id(0), pl.program_id(1)
        s_sc, dp_sc, p_sc, ds_sc = (s0_sc, s1_sc), (dp0_sc, dp1_sc), (p0_sc, p1_sc), (ds0_sc, ds1_sc)

        @pl.when(ki == 0)
        def _():
            for c in range(nq):
                rows = slice(c * t, (c + 1) * t)
                q_blk, do_blk = q_ref[rows, :], do_ref[rows, :]
                qaugt_sc[0, :, rows], qaugt_sc[1, :, rows] = _q_augmented_t(q_blk)
                dot_sc[:, rows] = do_blk.T.astype(BF16)
                prod_t = (do_blk * o_ref[rows, :]).T
                delta_sc[:, rows] = _rows01(jnp.sum(prod_t[0:HD], axis=0, keepdims=True),
                                            jnp.sum(prod_t[HD:], axis=0, keepdims=True))
                for h in range(2):
                    head = _head_part((t, 128), h, 1)
                    qh_sc[h, rows, :] = jnp.where(head, q_blk * ATT_SCALE, 0.0).astype(BF16)
                    doh_sc[h, rows, :] = jnp.where(head, do_blk, 0.0).astype(BF16)
            dqt_sc[...] = jnp.zeros(dqt_sc.shape, F32)
            dr_sc[...] = jnp.zeros(dr_sc.shape, F32)

        k_blk, v_blk = k_ref[...], v_ref[...]
        kt = k_blk.T
        for h in range(2):
            kaug_sc[h] = _k_augmented(k_blk, c_ref[...], j, h)
            vh_sc[h] = jnp.where(_head_part((t, 128), h, 1), v_blk, 0.0).astype(BF16)
            kt_sc[h] = kt[h * HD:(h + 1) * HD].astype(BF16)
        dk_sc[...] = jnp.zeros(dk_sc.shape, F32)
        dv_sc[...] = jnp.zeros(dv_sc.shape, F32)
        dc_sc[...] = jnp.zeros(dc_sc.shape, F32)

        def inputs(qb, buf):
            qs = pl.ds(pl.multiple_of(qb * t, t), t)
            for h in range(2):
                s_sc[buf][h] = _dot(kaug_sc[h], qaugt_sc[h, :, qs])
                dp_sc[buf][h] = _dot(vh_sc[h], dot_sc[:, qs])

        def elementwise(qb, buf, diagonal):
            qs = pl.ds(pl.multiple_of(qb * t, t), t)
            for h in range(2):
                lse_row, delta_row = lse_ref[0, h:h + 1, qs], delta_sc[h:h + 1, qs]
                tot = jnp.zeros((8, t), F32)
                for i in range(t // r):
                    rows = slice(i * r, (i + 1) * r)
                    x = s_sc[buf][h, rows, :]
                    if diagonal:
                        x = jnp.where(_iota((r, t), 1) >= i * r + _iota((r, t), 0), x, -1e30)
                    pr = jnp.exp(x - lse_row)
                    ds = pr * (dp_sc[buf][h, rows, :] - delta_row)
                    p_sc[buf][h, rows, :] = pr.astype(BF16)
                    ds_sc[buf][h, rows, :] = ds.astype(BF16)
                    dc_sc[h, rows, :] += sum(ds[:, 128 * g:128 * (g + 1)] for g in range(t // 128))
                    tot = _fold8(ds, jnp.add, tot)
                dr_sc[h, :, qs] += tot

        def outputs(qb, buf):
            qs = pl.ds(pl.multiple_of(qb * t, t), t)
            dv_sc[...] += _dot(p_sc[buf][0], doh_sc[0, qs, :]) + _dot(p_sc[buf][1], doh_sc[1, qs, :])
            dk_sc[...] += _dot(ds_sc[buf][0], qh_sc[0, qs, :]) + _dot(ds_sc[buf][1], qh_sc[1, qs, :])
            for h in range(2):
                dqt_sc[h * HD:(h + 1) * HD, qs] += _dot(kt_sc[h], ds_sc[buf][h])

        def pair(a, b, a_diagonal):
            inputs(a, 0)
            inputs(b, 1)
            elementwise(a, 0, a_diagonal)
            outputs(a, 0)
            elementwise(b, 1, False)
            outputs(b, 1)

        def later(u, carry):
            pair(ki + 1 + 2 * u, ki + 2 + 2 * u, False)
            return carry

        n_later = nq - 1 - ki
        lax.fori_loop(0, n_later // 2, later, 0)

        @pl.when(n_later % 2 == 1)
        def _():
            pair(ki, nq - 1, True)

        @pl.when(n_later % 2 == 0)
        def _():
            inputs(ki, 0)
            elementwise(ki, 0, True)
            outputs(ki, 0)

        dk_ref[...] = dk_sc[...].astype(BF16)
        dv_ref[...] = dv_sc[...].astype(BF16)
        lane = _iota((t, 128), 1)
        cols = jnp.where(lane == 0, jnp.sum(dc_sc[0], axis=1, keepdims=True),
                         jnp.where(lane == 1, jnp.sum(dc_sc[1], axis=1, keepdims=True), 0.0))
        dc_ref[0] = cols.T[0:8, :]

        @pl.when(ki == nq - 1)
        def _():
            for c in range(nq):
                rows = slice(c * t, (c + 1) * t)
                dq_ref[rows, :] = dqt_sc[:, rows].T * ATT_SCALE
            dr_ref[0] = _rows01(jnp.sum(dr_sc[0], axis=0, keepdims=True), jnp.sum(dr_sc[1], axis=0, keepdims=True))

    whole = lambda off: pl.BlockSpec((s, 128), functools.partial(lambda j, ki, off: (0, off + j), off=off))
    return pl.pallas_call(
        body, name="attn_bwd", grid=(NH // 2, nq),
        in_specs=[whole(OFF_Q // 128),
                  pl.BlockSpec((t, 128), lambda j, ki: (ki, OFF_K // 128 + j)),
                  pl.BlockSpec((t, 128), lambda j, ki: (ki, OFF_V // 128 + j)),
                  pl.BlockSpec((t, 128), lambda j, ki: (ki, 0)),
                  whole(0),
                  pl.BlockSpec((1, 8, s), lambda j, ki: (j, 0, 0)),
                  whole(0)],
        out_specs=[whole(0),
                   pl.BlockSpec((t, 128), lambda j, ki: (ki, j)),
                   pl.BlockSpec((t, 128), lambda j, ki: (ki, j)),
                   pl.BlockSpec((1, 8, t), lambda j, ki: (j, 0, ki)),
                   pl.BlockSpec((1, 8, s), lambda j, ki: (j, 0, 0))],
        out_shape=[jax.ShapeDtypeStruct((s, D), F32), jax.ShapeDtypeStruct((s, D), BF16), jax.ShapeDtypeStruct((s, D), BF16),
                   jax.ShapeDtypeStruct((NH // 2, 8, s), F32), jax.ShapeDtypeStruct((NH // 2, 8, s), F32)],
        scratch_shapes=[pltpu.VMEM((2, 128, s), BF16), pltpu.VMEM((2, s, 128), BF16), pltpu.VMEM((128, s), BF16),
                        pltpu.VMEM((2, s, 128), BF16), pltpu.VMEM((8, s), F32), pltpu.VMEM((128, s), F32),
                        pltpu.VMEM((2, 8, s), F32), pltpu.VMEM((2, t, 128), BF16), pltpu.VMEM((2, t, 128), BF16),
                        pltpu.VMEM((2, HD, t), BF16)]
        + [pltpu.VMEM((2, t, t), F32)] * 4 + [pltpu.VMEM((2, t, t), BF16)] * 4
        + [pltpu.VMEM((t, 128), F32), pltpu.VMEM((t, 128), F32), pltpu.VMEM((2, t, 128), F32)],
        compiler_params=_params(("parallel", "arbitrary")),
    )(p, p, p, cum, o, lse, do)


def _ln_stats(u):
    mu = _mean(u)
    d = u - mu
    rstd = lax.rsqrt(_mean(d * d) + EPS)
    return d * rstd, rstd


def _ln_bwd(dx, xh, rstd, gam):
    dxh = dx * gam
    return rstd * (dxh - _mean(dxh) - xh * _mean(dxh * xh))


def _rms_bwd(d, xn, r, w):
    t = d * w
    return r * (t - xn * _mean(t * xn)), _colsum(d * xn)


def _mix_norm(y, p, att, w_ssm, w_att, s):
    def fn(pos, y, z, att, w1, w2):
        g = y * _silu(z)
        n1 = g * lax.rsqrt(_mean(g * g) + EPS) * w1
        n2 = att * lax.rsqrt(_mean(att * att) + EPS) * w2
        return (jnp.concatenate([n1, n2], axis=1),)

    return _rowk("mix_norm", fn, s, 512, [(y, D, 0, 0), (p, D, OFF_Z // D, 0), (att, D, 0, 0)],
                 [w_ssm, w_att], [(2 * D, BF16)], [])[0]


def _mix_norm_bwd(dmix, y, p, att, w_ssm, w_att, s):
    def fn(pos, dmix, y, z, att, w1, w2, a1, a2):
        sz = _silu(z)
        g = y * sz
        r1 = lax.rsqrt(_mean(g * g) + EPS)
        dg, dw1 = _rms_bwd(dmix[:, :D], g * r1, r1, w1)
        r2 = lax.rsqrt(_mean(att * att) + EPS)
        datt, dw2 = _rms_bwd(dmix[:, D:], att * r2, r2, w2)
        return dg * sz, dg * y * _dsilu(z), datt, a1 + dw1, a2 + dw2

    return _rowk("mix_norm_bwd", fn, s, 512,[(dmix, 2 * D, 0, 0), (y, D, 0, 0), (p, D, OFF_Z // D, 0), (att, D, 0, 0)],
                 [w_ssm, w_att], [(D, F32), (D, BF16), (D, F32)], [(1, D), (1, D)])


def _ln1(x0, y, g1, gam, bet, sc2, sh2, s):
    def fn(pos, x0, y, g1, gam, bet, sc2, sh2):
        xh, _ = _ln_stats(ALPHA * x0 + (1.0 + g1) * y)
        x1 = xh * gam + bet
        return x1, _modulate(x1, sc2, sh2)

    return _rowk("ln1", fn, s, 512, [(x0, D, 0, 0), (y, D, 0, 0)], [g1, gam, bet, sc2, sh2], [(D, F32), (D, BF16)], [])


def _ln2_loss(x1, ff, tgt, g2, gam, bet, s):
    def fn(pos, x1, ff, tgt, g2, gam, bet, a_loss, a_dgam, a_dbet, a_dg2):
        xh, rstd = _ln_stats(ALPHA * x1 + (1.0 + g2) * ff)
        err = xh * gam + bet - tgt
        dx2 = err * (1.0 / D)
        du = _ln_bwd(dx2, xh, rstd, gam)
        return (du, du * (1.0 + g2), a_loss + _colsum(err * err), a_dgam + _colsum(dx2 * xh),
                a_dbet + _colsum(dx2), a_dg2 + _colsum(du * ff))

    return _rowk("ln2_loss", fn, s, 512, [(x1, D, 0, 0), (ff, D, 0, 0), (tgt, D, 0, 0)], [g2, gam, bet],
                 [(D, F32), (D, BF16)], [(1, D)] * 4)


def _ln1_bwd(dh2, du2, x0, y, g1, gam, bet, sc2, s):
    def fn(pos, dh2, du2, x0, y, g1, gam, bet, sc2, a_sc, a_sh, a_gam, a_bet, a_g1):
        xh, rstd = _ln_stats(ALPHA * x0 + (1.0 + g1) * y)
        x1 = xh * gam + bet
        dx1 = ALPHA * du2 + dh2 * (1.0 + sc2)
        du1 = _ln_bwd(dx1, xh, rstd, gam)
        return (du1, du1 * (1.0 + g1), a_sc + _colsum(dh2 * x1), a_sh + _colsum(dh2), a_gam + _colsum(dx1 * xh),
                a_bet + _colsum(dx1), a_g1 + _colsum(du1 * y))

    return _rowk("ln1_bwd", fn, s, 512, [(dh2, D, 0, 0), (du2, D, 0, 0), (x0, D, 0, 0), (y, D, 0, 0)],
                 [g1, gam, bet, sc2], [(D, F32), (D, BF16)], [(1, D)] * 5)


def _input_grad(dh1, du1, x0, sc1, s):
    def fn(pos, dh1, du1, x0, sc1, a_sc, a_sh):
        return ALPHA * du1 + dh1 * (1.0 + sc1), a_sc + _colsum(dh1 * x0), a_sh + _colsum(dh1)

    return _rowk("input_grad", fn, s, 512, [(dh1, D, 0, 0), (du1, D, 0, 0), (x0, D, 0, 0)], [sc1],
                 [(D, F32)], [(1, D)] * 2)


def _adamw_math(w, grad, m, v):
    m_new = ADAM_B1 * m + (1.0 - ADAM_B1) * grad
    v_new = ADAM_B2 * v + (1.0 - ADAM_B2) * (grad * grad)
    m_hat = m_new / (1.0 - ADAM_B1 ** ADAM_STEP)
    v_hat = v_new / (1.0 - ADAM_B2 ** ADAM_STEP)
    return -ADAM_LR * (m_hat / (jnp.sqrt(v_hat) + ADAM_EPS) + ADAM_WD * w), m_new, v_new


def _small_update(small_all, layout, w, m, v):
    names = [n for n, _, _ in layout]

    def body(*refs):
        all_ref = refs[0]
        w_refs, m_refs, v_refs = [refs[1 + k * len(names):1 + (k + 1) * len(names)] for k in range(3)]
        sum_ref = refs[1 + 3 * len(names)]
        outs = refs[2 + 3 * len(names):]
        total = all_ref[0]
        for k in range(1, N_DEV):
            total = total + all_ref[k]
        sum_ref[...] = total
        for i, (_, off, size) in enumerate(layout):
            grad = total[:, off:off + size]
            delta, m_new, v_new = _adamw_math(w_refs[i][...], grad, m_refs[i][...], v_refs[i][...])
            for o, val in zip(outs[4 * i:4 * i + 4], (grad, delta, m_new, v_new)):
                o[...] = val

    res = pl.pallas_call(
        body, name="small_update",
        out_shape=[jax.ShapeDtypeStruct(small_all.shape[1:], F32)]
        + [jax.ShapeDtypeStruct(w[n].shape, F32) for n in names for _ in range(4)],
        compiler_params=_params(None),
    )(small_all, *[w[n] for n in names], *[m[n] for n in names], *[v[n] for n in names])
    return res[0], {n: res[1 + 4 * i:5 + 4 * i] for i, n in enumerate(names)}


def _adamw(name, w, g, m, v, *, tr, slots):
    r, c = w.shape

    def body(w_ref, g_ref, m_ref, v_ref, g_out, d_out, m_out, v_out):
        if slots:
            grad = g_ref[0][:, :c].astype(F32)
            for k in range(1, N_DEV):
                grad = grad + g_ref[k][:, :c].astype(F32)
        else:
            grad = g_ref[...]
        g_out[...] = grad
        d_out[...], m_out[...], v_out[...] = _adamw_math(w_ref[...], grad, m_ref[...], v_ref[...])

    tile = pl.BlockSpec((tr, c), lambda i: (i, 0))
    g_spec = pl.BlockSpec((N_DEV, tr, g.shape[-1]), lambda i: (0, i, 0)) if slots else tile
    return pl.pallas_call(
        body, name=name, grid=(r // tr,),
        in_specs=[tile, g_spec, tile, tile], out_specs=[tile] * 4,
        out_shape=[jax.ShapeDtypeStruct((r, c), F32)] * 4,
        compiler_params=_params(("parallel",)),
    )(w, g, m, v)


def _dot_f32(a, b, dims=NN):
    a0, a1, a2 = _split3(a)
    b0, b1, b2 = _split3(b)
    acc = _dot(a0, b0, dims)
    for x, y in ((a0, b1), (a1, b0), (a1, b1), (a0, b2), (a2, b0)):
        acc = acc + _dot(x, y, dims)
    return acc


def _ada_mod(c_all, w_shard, b_shard):
    def body(c_ref, w_ref, b_ref, o_ref):
        act = _silu(c_ref[...])
        act16 = jnp.concatenate([act, jnp.zeros_like(act)], axis=0)
        o_ref[...] = _dot_f32(act16, w_ref[...])[0:N_DEV] + b_ref[...]

    return pl.pallas_call(
        body, name="ada_mod", out_shape=jax.ShapeDtypeStruct((N_DEV, w_shard.shape[1]), F32),
        compiler_params=_params(None),
    )(c_all, w_shard, b_shard)


def _ada_grad(c_all, dmod_cols):
    def body(c_ref, dc_ref, gw_ref):
        act = _silu(c_ref[...])
        act16 = jnp.concatenate([act, jnp.zeros_like(act)], axis=0)
        dm = dc_ref[...]
        dm16 = jnp.concatenate([dm, jnp.zeros_like(dm)], axis=0)
        gw_ref[...] = _dot_f32(act16, dm16, TN)

    return pl.pallas_call(
        body, name="ada_grad", out_shape=jax.ShapeDtypeStruct((D, dmod_cols.shape[1]), F32),
        compiler_params=_params(None),
    )(c_all, dmod_cols)


def _exchange(name, xs, scatter):
    n = len(xs)
    n_peer = N_DEV - 1

    def body(*refs):
        x_refs, o_refs = refs[:n], refs[n:2 * n]
        send_sems, recv_sems, local_sems = refs[2 * n:]
        mx, my, mc = lax.axis_index("x"), lax.axis_index("y"), lax.axis_index("c")
        me = 4 * mx + 2 * my + mc

        def src(a, slot):
            return x_refs[a].at[slot] if scatter else x_refs[a]

        own = [pltpu.make_async_copy(src(a, me), o_refs[a].at[me], local_sems.at[a]) for a in range(n)]
        for cp in own:
            cp.start()
        sends = []
        for d in range(1, N_DEV):
            px = 1 - mx if d & 4 else mx
            py = 1 - my if d & 2 else my
            pc = 1 - mc if d & 1 else mc
            peer = 4 * px + 2 * py + pc
            for a in range(n):
                def copy(src_slot, dst_slot, a=a, d=d, to=(px, py, pc)):
                    return pltpu.make_async_remote_copy(
                        src_ref=src(a, src_slot), dst_ref=o_refs[a].at[dst_slot],
                        send_sem=send_sems.at[a * n_peer + d - 1], recv_sem=recv_sems.at[a * n_peer + d - 1],
                        device_id=to, device_id_type=pl.DeviceIdType.MESH)

                out = copy(peer, me)
                out.start()
                sends.append((out, copy(me, peer)))
        for _, arrival in sends:
            arrival.wait_recv()
        for out, _ in sends:
            out.wait_send()
        for cp in own:
            cp.wait()

    shapes = [tuple(x.shape[1:] if scatter else x.shape) for x in xs]
    return pl.pallas_call(
        body, name=name,
        in_specs=[pl.BlockSpec(memory_space=pl.ANY)] * n, out_specs=[pl.BlockSpec(memory_space=pl.ANY)] * n,
        out_shape=[jax.ShapeDtypeStruct((N_DEV,) + sh, x.dtype) for sh, x in zip(shapes, xs)],
        scratch_shapes=[pltpu.SemaphoreType.DMA((n * n_peer,)), pltpu.SemaphoreType.DMA((n * n_peer,)),
                        pltpu.SemaphoreType.DMA((n,))],
        compiler_params=pltpu.CompilerParams(has_side_effects=True),
    )(*xs)


def _gather_two_level(name, x):
    def body(x_ref, o_ref, send_sems, recv_sems, local_sem):
        mx, my, mc = lax.axis_index("x"), lax.axis_index("y"), lax.axis_index("c")
        me, sibling = (mx, my, mc), (mx, my, 1 - mc)
        chips = [(1 - mx, my), (mx, 1 - my), (1 - mx, 1 - my)]

        def slot(px, py, pc):
            return o_ref.at[4 * px + 2 * py + pc]

        def copy(k, block, to, src=None):
            return pltpu.make_async_remote_copy(
                src_ref=slot(*block) if src is None else src, dst_ref=slot(*block),
                send_sem=send_sems.at[k], recv_sem=recv_sems.at[k], device_id=to, device_id_type=pl.DeviceIdType.MESH)

        mine = pltpu.make_async_copy(x_ref, slot(*me), local_sem)
        mine.start()
        first = [copy(0, me, sibling, src=x_ref)] + [copy(1 + i, me, (*chip, mc), src=x_ref) for i, chip in enumerate(chips)]
        for cp in first:
            cp.start()
        passed = [copy(4 + i, (*chip, mc), sibling) for i, chip in enumerate(chips)]
        for i, chip in enumerate(chips):
            copy(1 + i, (*chip, mc), me).wait_recv()
            passed[i].start()
        copy(0, sibling, me).wait_recv()
        for i, chip in enumerate(chips):
            copy(4 + i, (*chip, 1 - mc), me).wait_recv()
        for cp in first + passed:
            cp.wait_send()
        mine.wait()

    return pl.pallas_call(
        body, name=name,
        in_specs=[pl.BlockSpec(memory_space=pl.ANY)], out_specs=pl.BlockSpec(memory_space=pl.ANY),
        out_shape=jax.ShapeDtypeStruct((N_DEV,) + tuple(x.shape), x.dtype),
        scratch_shapes=[pltpu.SemaphoreType.DMA((7,)), pltpu.SemaphoreType.DMA((7,)), pltpu.SemaphoreType.DMA(())],
        compiler_params=pltpu.CompilerParams(has_side_effects=True),
    )(x)


def _after(x, zero):
    return x if zero is None else x + zero.reshape(-1)[0].astype(x.dtype)


def _exchange_copies(x_refs, land_refs, send_sems, recv_sems, scatter):
    n = len(x_refs)
    n_peer = N_DEV - 1
    mx, my, mc = lax.axis_index("x"), lax.axis_index("y"), lax.axis_index("c")
    me = 4 * mx + 2 * my + mc
    pairs = []
    for d in range(1, N_DEV):
        px = 1 - mx if d & 4 else mx
        py = 1 - my if d & 2 else my
        pc = 1 - mc if d & 1 else mc
        peer = 4 * px + 2 * py + pc
        for a in range(n):
            def copy(src_slot, dst_slot, a=a, d=d, to=(px, py, pc)):
                return pltpu.make_async_remote_copy(
                    src_ref=x_refs[a].at[src_slot] if scatter else x_refs[a], dst_ref=land_refs[a].at[dst_slot],
                    send_sem=send_sems.at[a * n_peer + d - 1], recv_sem=recv_sems.at[a * n_peer + d - 1],
                    device_id=to, device_id_type=pl.DeviceIdType.MESH)

            pairs.append((copy(peer, me), copy(me, peer)))
    return me, pairs


def _exchange_async(name, xs, scatter, collective_id):
    n = len(xs)
    shapes = [tuple(x.shape[1:] if scatter else x.shape) for x in xs]
    x_refs = [jax.new_ref(x, memory_space=pltpu.MemorySpace.HBM) for x in xs]
    land_refs = [jax.empty_ref(jax.ShapeDtypeStruct((N_DEV,) + sh, x.dtype), memory_space=pltpu.MemorySpace.HBM)
                 for sh, x in zip(shapes, xs)]

    @pl.kernel(mesh=plsc.ScalarSubcoreMesh(axis_name="sequencer", num_cores=1), name=name,
               scratch_types=(pltpu.SemaphoreType.DMA((n * (N_DEV - 1),)), pltpu.SemaphoreType.DMA((n * (N_DEV - 1),)),
                              pltpu.SemaphoreType.DMA((n,))),
               compiler_params=pltpu.CompilerParams(collective_id=collective_id))
    def launch(send_sems, recv_sems, own_sems):
        barrier = pltpu.get_barrier_semaphore()
        mx, my, mc = lax.axis_index("x"), lax.axis_index("y"), lax.axis_index("c")
        for d in range(1, N_DEV):
            peer = (1 - mx if d & 4 else mx, 1 - my if d & 2 else my, 1 - mc if d & 1 else mc)
            pl.semaphore_signal(barrier, inc=1, device_id=peer, device_id_type=pl.DeviceIdType.MESH)
        pl.semaphore_wait(barrier, N_DEV - 1)
        me, pairs = _exchange_copies(x_refs, land_refs, send_sems, recv_sems, scatter)
        own = [pltpu.make_async_copy(x_refs[a].at[me] if scatter else x_refs[a], land_refs[a].at[me], own_sems.at[a])
               for a in range(n)]
        for cp in own:
            cp.start()
        for out, _ in pairs:
            out.start()
        for out, arrival in pairs:
            arrival.wait_recv()
            out.wait_send()
        for cp in own:
            cp.wait()

    launch()
    return lambda: [r[...] for r in land_refs]


def _relu2(a):
    r = jnp.maximum(a, 0.0)
    return r * r


def _relu2_grad(acc, r):
    return acc * (2.0 * jnp.sqrt(r.astype(F32)))


def _local_step(x0, tgt, mod, wcat, late_weights, send_grads, conv_w, conv_b, dt_bias, a_log, d_skip, ssm_norm_w, f_bias,
                attn_norm_w, ln1_g, ln1_b, ln2_g, ln2_b):
    ff_w = DFF // N_DEV
    s = x0.shape[0]
    tm = min(1024, s)
    ts = min(1024, s)
    sh1, sc1, g1, sh2, sc2, g2 = [mod[:, i * D:(i + 1) * D] for i in range(6)]
    zero = jnp.zeros((1, 128 - 2 * NH), F32)
    bias128 = jnp.concatenate([dt_bias, f_bias, zero], axis=1)
    alog128 = jnp.concatenate([a_log, jnp.zeros((1, 128 - NH), F32)], axis=1)
    dskip_x = jnp.repeat(d_skip, HD, axis=1)
    w_xs, w_bc, b_xs, b_bc = conv_w[:, :D], conv_w[:, D:], conv_b[:, :D], conv_b[:, D:]

    p = _mm_nn("in_proj", x0, wcat, tm=tm, tn=1152, tk=D, out_dtype=F32, pro=_modulate, aux=(sc1, sh1))
    xs_a, bc_a = _conv_fwd(p, w_xs, b_xs, w_bc, b_bc, s)
    y_ssd, states = _ssd_fwd(xs_a, bc_a, p, bias128, alog128, dskip_x, s)
    cum = _cum_fwd(p, bias128, s)
    att, lse = _attn_fwd(p, cum, s)
    wout, w1s, w2 = late_weights()
    ymix = _mix_norm(y_ssd, p, att, ssm_norm_w, attn_norm_w, s)
    y = _mm_nn("out_proj", ymix, wout, tm=tm, tn=1024, tk=2 * D, out_dtype=F32)
    x1, h2 = _ln1(x0, y, g1, ln1_g, ln1_b, sc2, sh2, s)
    r = _mm_nn("ff_in", h2, w1s, tm=tm, tn=ff_w, tk=D, out_dtype=BF16, epi=_relu2)
    ff = _mm_nn("ff_out", r, w2, tm=tm, tn=1024, tk=1024, out_dtype=F32)
    du2, dff, sq_err, d_ln2_g, d_ln2_b, d_g2 = _ln2_loss(x1, ff, tgt, g2, ln2_g, ln2_b, s)

    da1 = _mm_nt("d_ff_hidden", [(dff, D, 0)], [(w2, D, 0)], n=DFF, tm=tm, tn=1024, out_dtype=BF16, epi=_relu2_grad,
                 epi_aux=(r,))
    d_w2 = _mm_tn("d_w_ff_out", r, dff, tm=1024, tn=1024, ts=ts)
    d_w1s = _mm_tn("d_w_ff_in", h2, da1, tm=1024, tn=ff_w, ts=ts, col_shards=True)
    dh2 = _mm_nt("d_ff_input", [(da1, ff_w, k) for k in range(N_DEV)], [(w1s, ff_w, k) for k in range(N_DEV)], n=D,
                 tm=min(512, s), tn=1024, out_dtype=F32)
    du1, dy, d_sc2, d_sh2, d_ln1_g, d_ln1_b, d_g1 = _ln1_bwd(dh2, du2, x0, y, g1, ln1_g, ln1_b, sc2, s)

    dmix = _mm_nt("d_mix", [(dy, D, 0)], [(wout, D, 0)], n=2 * D, tm=tm, tn=1024, out_dtype=F32)
    d_wout = _mm_tn("d_w_out", ymix, dy, tm=1024, tn=1024, ts=ts)
    sent = send_grads("late", [d_w1s, d_w2.reshape(N_DEV, -1, D), d_wout.reshape(N_DEV, -1, D)])
    dy_ssd, dz, datt, d_ssm_w, d_attn_w = _mix_norm_bwd(dmix, y_ssd, p, att, _after(ssm_norm_w, sent), attn_norm_w, s)
    dq, dk, dv, dcs, drs = _attn_bwd(p, cum, att, lse, datt, s)
    dxs_a, dbc_a, ddt_raw, d_alog, d_dskip = _ssd_bwd(dy_ssd, xs_a, bc_a, p, states, bias128, alog128, dskip_x, s)
    dcum = jnp.pad((drs - dcs)[:, :2, :].reshape(NH, s).T, ((0, 0), (NH, 128 - 2 * NH)))
    ddtf, _, d_bias = _cum_bwd(dcum, ddt_raw, p, bias128, s)
    dxs, dbc, d_wc_xs, d_bc_xs, d_wc_bc, d_bc_bc = _conv_bwd(dxs_a, dbc_a, p, w_xs, b_xs, w_bc, b_bc, s)

    segs = [(dz, OFF_Z, D), (dxs, OFF_XS, D), (dq, OFF_Q, D), (dk, OFF_K, D), (dv, OFF_V, D), (dbc, OFF_BC, 512),
            (ddtf, OFF_DTF, 128)]
    d_z, d_xs, d_q, d_k, d_v, d_bcw, d_dtf = [
        _mm_tn("d_w_in_%d" % i, x0, a, tm=1024, tn=min(w, 1024), ts=ts, pro=_modulate, aux=(sc1, sh1))
        for i, (a, _, w) in enumerate(segs)]
    d_w_in = dict(z=d_z, xs=d_xs, bc=d_bcw, dt=d_dtf[:, :NH], q=d_q, k=d_k, v=d_v, f=d_dtf[:, NH:2 * NH])
    sent = send_grads("in", [_shard_w_in_grad(d_w_in)])
    segs[-1] = (_after(ddtf, sent), OFF_DTF, 128)
    dh1 = _mm_nt("d_h1", [(a, w, 0) for a, _, w in segs], [(wcat, w, off // w) for _, off, w in segs], n=D,
                 tm=min(512, s), tn=1024, out_dtype=F32)
    grad_x, d_sc1, d_sh1 = _input_grad(dh1, du1, x0, sc1, s)

    return dict(
        loss=(0.5 / D) * jnp.sum(sq_err), grad_x=grad_x,
        d_mod=jnp.concatenate([d_sh1, d_sc1, d_g1, d_sh2, d_sc2, d_g2], axis=1),
        d_conv_w=jnp.concatenate([d_wc_xs[:4], d_wc_bc[:4]], axis=1), d_conv_b=jnp.concatenate([d_bc_xs, d_bc_bc], axis=1),
        d_ssm_norm_w=d_ssm_w, d_attn_norm_w=d_attn_w, d_ln1_g=d_ln1_g, d_ln1_b=d_ln1_b, d_ln2_g=d_ln2_g, d_ln2_b=d_ln2_b,
        d_gate_bias=d_bias, d_a_log=d_alog, d_d_skip=d_dskip)


W_IN_SEGS = [('z', W_Z, D), ('xs', W_XS, D), ('bc', W_BC, 512), ('dt', W_DT, NH), ('q', W_Q, D), ('k', W_K, D),
             ('v', W_V, D), ('f', W_F, NH)]
SHARD_W = IN_COLS // N_DEV


def _pack_w_in(shards):
    def cols(lo, hi):
        pieces = []
        while lo < hi:
            dev = lo // SHARD_W
            end = min(hi, (dev + 1) * SHARD_W)
            pieces.append(shards[dev][:, lo - dev * SHARD_W:end - dev * SHARD_W])
            lo = end
        return pieces

    seg = {n: cols(off, off + w) for n, off, w in W_IN_SEGS}
    pieces = seg['z'] + seg['xs'] + seg['q'] + seg['k'] + seg['v'] + seg['bc'] + seg['dt'] + seg['f']
    return jnp.concatenate(pieces + [jnp.zeros((D, 128 - 2 * NH), shards.dtype)], axis=1)


def _shard_w_in_grad(d_w_in):
    blocks = []
    for dev in range(N_DEV):
        lo, hi = dev * SHARD_W, (dev + 1) * SHARD_W
        pieces = [d_w_in[n][:, max(lo, off) - off:min(hi, off + w) - off] for n, off, w in W_IN_SEGS
                  if max(lo, off) < min(hi, off + w)]
        pieces.append(jnp.zeros((D, -SHARD_W % 128), pieces[0].dtype))
        blocks.append(jnp.concatenate(pieces, axis=1))
    return jnp.stack(blocks, axis=0)


WEIGHTS = ['w_ada', 'b_ada', 'w_in', 'conv_w', 'conv_b', 'dt_bias', 'a_log', 'd_skip', 'ssm_norm_w', 'f_bias',
           'attn_norm_w', 'w_out', 'ln1_g', 'ln1_b', 'w_ff_in', 'w_ff_out', 'ln2_g', 'ln2_b']
BIG = ['w_in', 'w_out', 'w_ff_in', 'w_ff_out']
SMALL_LAYOUT = [('b_ada', 0, 6 * D), ('conv_b', 12288, 1536), ('ssm_norm_w', 13824, D), ('attn_norm_w', 14848, D),
                ('ln1_g', 15872, D), ('ln1_b', 16896, D), ('ln2_g', 17920, D), ('ln2_b', 18944, D),
                ('dt_bias', 19968, NH), ('f_bias', 19968 + NH, NH), ('a_log', 20096, NH), ('d_skip', 20224, NH)]
SMALL_LOSS_LANE = 20352


def _pad_lanes(v, n=128):
    return jnp.pad(v, ((0, 0), (0, n - v.shape[1])))


def kernel(x, c, w_ada, b_ada, w_in, conv_w, conv_b, dt_bias, a_log, d_skip, ssm_norm_w, f_bias, attn_norm_w, w_out, ln1_g, ln1_b, w_ff_in, w_ff_out, ln2_g, ln2_b, loss_target, m_w_ada, m_b_ada, m_w_in, m_conv_w, m_conv_b, m_dt_bias, m_a_log, m_d_skip, m_ssm_norm_w, m_f_bias, m_attn_norm_w, m_w_out, m_ln1_g, m_ln1_b, m_w_ff_in, m_w_ff_out, m_ln2_g, m_ln2_b, v_w_ada, v_b_ada, v_w_in, v_conv_w, v_conv_b, v_dt_bias, v_a_log, v_d_skip, v_ssm_norm_w, v_f_bias, v_attn_norm_w, v_w_out, v_ln1_g, v_ln1_b, v_w_ff_in, v_w_ff_out, v_ln2_g, v_ln2_b):
    args = dict(locals())
    w = {n: args[n] for n in WEIGHTS}
    m = {n: args['m_' + n] for n in WEIGHTS}
    v = {n: args['v_' + n] for n in WEIGHTS}
    me = 4 * lax.axis_index("x") + 2 * lax.axis_index("y") + lax.axis_index("c")
    ada_cols = 6 * D // N_DEV
    conv_cols = conv_w.shape[2]

    c_all, conv_all = _exchange("gather_cond", [c, conv_w[0]], False)
    c_all = c_all.reshape(N_DEV, D)
    conv_w_full = conv_all.transpose(1, 0, 2).reshape(4, N_DEV * conv_cols)
    b_shard = lax.dynamic_slice(b_ada, (0, me * ada_cols), (1, ada_cols))
    mod_all, = _exchange("gather_mod", [_ada_mod(c_all, w_ada[0], b_shard)], False)
    mod = lax.dynamic_index_in_dim(mod_all, me, axis=1, keepdims=False).reshape(1, 6 * D)

    win_s = _gather_two_level("gather_w_in", _after(w_in[0].astype(BF16), mod * 0))
    first_done = win_s[0, 0:1, 0:1] * 0
    rest = _exchange_async("gather_rest", [_after(w[n][0].astype(BF16), first_done) for n in BIG[1:]], False, 1)

    def late_weights():
        wout_s, w1s, w2_s = rest()
        return wout_s.reshape(2 * D, D), w1s, w2_s.reshape(DFF, D)

    sends = {}

    def send_grads(tag, blocks):
        sends[tag] = _exchange_async("scatter_" + tag, blocks, True, {'late': 2, 'in': 3}[tag])
        return sum(b.reshape(-1)[0].astype(F32) * 0 for b in blocks)

    out = _local_step(x[0], loss_target[0], mod, _pack_w_in(win_s), late_weights, send_grads,
                      conv_w_full, conv_b, dt_bias, a_log, d_skip, ssm_norm_w, f_bias, attn_norm_w, ln1_g, ln1_b, ln2_g, ln2_b)

    small = jnp.concatenate(
        [out['d_mod'], out['d_conv_w'].reshape(1, -1), out['d_conv_b'], out['d_ssm_norm_w'], out['d_attn_norm_w'],
         out['d_ln1_g'], out['d_ln1_b'], out['d_ln2_g'], out['d_ln2_b'], out['d_gate_bias'], out['d_a_log'],
         out['d_d_skip'], _pad_lanes(out['loss'].reshape(1, 1))], axis=1)
    small_landed = _exchange_async("gather_small", [small], False, 4)
    (g_ff_in, g_ff_out, g_out), (g_in,) = sends['late'](), sends['in']()
    g_parts = dict(w_ff_in=g_ff_in, w_ff_out=g_ff_out, w_out=g_out, w_in=g_in)
    big = {n: _adamw("adamw_" + n, w[n][0], g_parts[n], m[n][0], v[n][0], tr=256, slots=True) for n in BIG}
    big_done = sum(big[n][1][0:1, 0:1] * 0 for n in BIG)
    small_all = _after(small_landed()[0], big_done)
    ssum, small_res = _small_update(small_all, SMALL_LAYOUT, w, m, v)
    dmod_all = small_all[:, 0, :6 * D]
    g_w_ada = _ada_grad(c_all, lax.dynamic_slice(dmod_all, (0, me * ada_cols), (N_DEV, ada_cols)))
    ada = _adamw("adamw_ada", w_ada[0], g_w_ada, m_w_ada[0], v_w_ada[0], tr=256, slots=False)
    g_conv_w = lax.dynamic_slice(ssum[:, 6 * D:6 * D + 4 * N_DEV * conv_cols].reshape(4, N_DEV * conv_cols),
                                 (0, me * conv_cols), (4, conv_cols))
    conv = _adamw("adamw_conv_w", conv_w[0], g_conv_w, m_conv_w[0], v_conv_w[0], tr=4, slots=False)

    results = []
    for k in range(4):
        vals = {n: small_res[n][k] for n in small_res}
        vals['w_ada'], vals['conv_w'] = ada[k][None], conv[k][None]
        for n in BIG:
            vals[n] = big[n][k][None]
        results.append(vals)
    return (ssum[0, SMALL_LOSS_LANE], out['grad_x'][None], *[res[n] for res in results for n in WEIGHTS])
```

```python
import functools

import jax
import jax.numpy as jnp
from jax import lax
from jax.experimental import pallas as pl
from jax.experimental.pallas import tpu as pltpu
from jax.experimental.pallas import tpu_sc as plsc

F32, BF16 = jnp.float32, jnp.bfloat16

N_DEV = 8
D = 1024
NH, HD = 16, 64
NSTATE = 128
CHUNK = 128
HG = 8
DFF = 4096
ALPHA = 2.0 ** 0.25
EPS = 1e-5
ATT_SCALE = HD ** -0.5

OFF_Z, OFF_XS, OFF_Q, OFF_K, OFF_V, OFF_BC, OFF_DTF = 0, 1024, 2048, 3072, 4096, 5120, 5632
PCOLS = 5760
W_Z, W_XS, W_BC, W_DT, W_Q, W_K, W_V, W_F = 0, 1024, 2048, 2560, 2576, 3600, 4624, 5648
IN_COLS = 5664

ADAM_LR, ADAM_B1, ADAM_B2, ADAM_EPS, ADAM_WD, ADAM_STEP = 0.001, 0.9, 0.999, 1e-08, 0.01, 10

VMEM_LIMIT = 56 << 20

NN = (((1,), (0,)), ((), ()))
NT = (((1,), (1,)), ((), ()))
TN = (((0,), (0,)), ((), ()))


def _dot(a, b, dims=NN):
    return lax.dot_general(a, b, dims, preferred_element_type=F32)


def _bdot(a, b, dims=NN):
    return _dot(a.astype(BF16), b.astype(BF16), dims)


def _split3(v, terms=3):
    parts, rest = [], v
    for _ in range(terms):
        p = rest.astype(BF16)
        parts.append(p)
        rest = rest - p.astype(F32)
    return parts


def _sel_left(m01, v):
    return sum(_dot(m01, p) for p in _split3(v))


def _sel_right(v, m01, dims=NN, terms=3):
    return sum(_dot(p, m01, dims) for p in _split3(v, terms))


def _iota(shape, dim):
    return lax.broadcasted_iota(jnp.int32, shape, dim)


def _tri_lower(n):
    return (_iota((n, n), 1) <= _iota((n, n), 0)).astype(BF16)


def _tri_upper(n):
    return (_iota((n, n), 1) >= _iota((n, n), 0)).astype(BF16)


def _head_expand():
    return (lax.shift_right_logical(_iota((128, D), 1), 6) == _iota((128, D), 0)).astype(BF16)


def _head_reduce():
    return (lax.shift_right_logical(_iota((D, 128), 0), 6) == _iota((D, 128), 1)).astype(BF16)


def _sigmoid(x):
    return 1.0 / (1.0 + jnp.exp(-x))


def _silu(x):
    return x * _sigmoid(x)


def _dsilu(x):
    s = _sigmoid(x)
    return s * (1.0 + x * (1.0 - s))


def _softplus(x):
    return jnp.maximum(x, 0.0) + jnp.log(1.0 + jnp.exp(-jnp.abs(x)))


def _log_sigmoid(x):
    return jnp.minimum(x, 0.0) - jnp.log(1.0 + jnp.exp(-jnp.abs(x)))


def _params(sem):
    return pltpu.CompilerParams(dimension_semantics=sem, vmem_limit_bytes=VMEM_LIMIT)


def _mm_nn(name, a, b, *, tm, tn, tk, out_dtype, pro=None, aux=(), epi=None):
    m, k_all = a.shape
    b_sharded = b.ndim == 3
    n = b.shape[0] * b.shape[2] if b_sharded else b.shape[1]
    assert not b_sharded or tn == b.shape[2]
    nk = k_all // tk
    n_aux = len(aux)
    b_spec = (pl.BlockSpec((None, tk, tn), lambda i, j, k: (j, k, 0)) if b_sharded
              else pl.BlockSpec((tk, tn), lambda i, j, k: (k, j)))

    def body(a_ref, b_ref, *rest):
        aux_refs, o_ref = rest[:n_aux], rest[n_aux]
        at = a_ref[...]
        if pro is not None:
            at = pro(at, *[r[...] for r in aux_refs])
        part = _bdot(at, b_ref[...])
        if nk == 1:
            o_ref[...] = (part if epi is None else epi(part)).astype(out_dtype)
            return
        assert epi is None
        acc_ref = rest[n_aux + 1]
        kk = pl.program_id(2)

        @pl.when(kk == 0)
        def _():
            acc_ref[...] = part

        @pl.when(kk > 0)
        def _():
            acc_ref[...] += part

        @pl.when(kk == nk - 1)
        def _():
            o_ref[...] = acc_ref[...].astype(out_dtype)

    return pl.pallas_call(
        body, name=name,
        grid=(m // tm, n // tn, nk),
        in_specs=[pl.BlockSpec((tm, tk), lambda i, j, k: (i, k)), b_spec]
        + [pl.BlockSpec((1, tk), lambda i, j, k: (0, k)) for _ in aux],
        out_specs=pl.BlockSpec((tm, tn), lambda i, j, k: (i, j)),
        out_shape=jax.ShapeDtypeStruct((m, n), out_dtype),
        scratch_shapes=[] if nk == 1 else [pltpu.VMEM((tm, tn), F32)],
        compiler_params=_params(("parallel", "parallel", "arbitrary")),
    )(a, b, *aux)


def _mm_nt(name, a_list, b_list, *, n, tm, tn, out_dtype, epi=None, epi_aux=()):
    m = a_list[0][0].shape[0]
    n_op = len(a_list)
    n_epi = len(epi_aux)

    def body(*refs):
        a_refs, b_refs = refs[:n_op], refs[n_op:2 * n_op]
        e_refs, o_ref = refs[2 * n_op:2 * n_op + n_epi], refs[2 * n_op + n_epi]
        acc = None
        for a_ref, b_ref in zip(a_refs, b_refs):
            part = _bdot(a_ref[...], b_ref[...], NT)
            acc = part if acc is None else acc + part
        if epi is not None:
            acc = epi(acc, *[r[...] for r in e_refs])
        o_ref[...] = acc.astype(out_dtype)

    in_specs = [pl.BlockSpec((tm, w), functools.partial(lambda i, j, cb: (i, cb), cb=cb)) for (_, w, cb) in a_list]
    for (b, w, cb) in b_list:
        if b.ndim == 3:
            in_specs.append(pl.BlockSpec((None, tn, w), functools.partial(lambda i, j, cb: (cb, j, 0), cb=cb)))
        else:
            in_specs.append(pl.BlockSpec((tn, w), functools.partial(lambda i, j, cb: (j, cb), cb=cb)))
    in_specs += [pl.BlockSpec((tm, tn), lambda i, j: (i, j)) for _ in epi_aux]
    return pl.pallas_call(
        body, name=name,
        grid=(m // tm, n // tn),
        in_specs=in_specs,
        out_specs=pl.BlockSpec((tm, tn), lambda i, j: (i, j)),
        out_shape=jax.ShapeDtypeStruct((m, n), out_dtype),
        compiler_params=_params(("parallel", "parallel")),
    )(*[a for (a, _, _) in a_list], *[b for (b, _, _) in b_list], *epi_aux)


def _mm_tn(name, a, b, *, tm, tn, ts, pro=None, aux=(), col_shards=False):
    s_all, ka = a.shape
    nb = b.shape[1]
    n_aux = len(aux)
    ns = s_all // ts
    assert not col_shards or tn == nb // N_DEV

    def body(a_ref, b_ref, *rest):
        aux_refs, o_ref, acc_ref = rest[:n_aux], rest[n_aux], rest[n_aux + 1]
        at = a_ref[...]
        if pro is not None:
            at = pro(at, *[r[...] for r in aux_refs])
        part = _bdot(at, b_ref[...], TN)
        ss = pl.program_id(2)

        @pl.when(ss == 0)
        def _():
            acc_ref[...] = part

        @pl.when(ss > 0)
        def _():
            acc_ref[...] += part

        @pl.when(ss == ns - 1)
        def _():
            o_ref[...] = acc_ref[...].astype(BF16)

    if col_shards:
        out_spec = pl.BlockSpec((None, tm, tn), lambda i, j, s: (j, i, 0))
        out_shape = jax.ShapeDtypeStruct((N_DEV, ka, tn), BF16)
    else:
        out_spec = pl.BlockSpec((tm, tn), lambda i, j, s: (i, j))
        out_shape = jax.ShapeDtypeStruct((ka, nb), BF16)
    return pl.pallas_call(
        body, name=name,
        grid=(ka // tm, nb // tn, ns),
        in_specs=[pl.BlockSpec((ts, tm), lambda i, j, s: (s, i)),
                  pl.BlockSpec((ts, tn), lambda i, j, s: (s, j))]
        + [pl.BlockSpec((1, tm), lambda i, j, s: (0, i)) for _ in aux],
        out_specs=out_spec, out_shape=out_shape,
        scratch_shapes=[pltpu.VMEM((tm, tn), F32)],
        compiler_params=_params(("parallel", "parallel", "arbitrary")),
    )(a, b, *aux)


def _rowk(name, fn, n_rows, tr, rows, fulls, outs, accs, reverse=False):
    n = n_rows // tr
    n_row, n_full, n_out, n_acc = len(rows), len(fulls), len(outs), len(accs)

    def pos(i):
        return (n - 1 - i) if reverse else i

    def body(*refs):
        row_refs = refs[:n_row]
        full_refs = refs[n_row:n_row + n_full]
        out_refs = refs[n_row + n_full:n_row + n_full + n_out]
        acc_refs = refs[n_row + n_full + n_out:]
        i = pl.program_id(0)

        @pl.when(i == 0)
        def _():
            for r in acc_refs:
                r[...] = jnp.zeros(r.shape, r.dtype)

        res = fn(pos(i), *[r[...] for r in row_refs], *[r[...] for r in full_refs], *[r[...] for r in acc_refs])
        for r, v in zip(out_refs + acc_refs, res):
            r[...] = v.astype(r.dtype)

    def row_map(i, cb, shift):
        return (jnp.clip(pos(i) + shift, 0, n - 1), cb)

    def halo_map(i, cb, shift):
        tile = jnp.clip(pos(i) + shift, 0, n - 1)
        return (tile * (tr // 8) + (tr // 8 - 1 if shift < 0 else 0), cb)

    in_specs = [pl.BlockSpec((tr, w), functools.partial(row_map, cb=cb, shift=sh)) if sh == 0 else
                pl.BlockSpec((8, w), functools.partial(halo_map, cb=cb, shift=sh)) for (_, w, cb, sh) in rows]
    in_specs += [pl.BlockSpec(f.shape, functools.partial(lambda i, nd: (0,) * nd, nd=f.ndim)) for f in fulls]
    out_specs = [pl.BlockSpec((tr, w), lambda i: (pos(i), 0)) for (w, _) in outs]
    out_specs += [pl.BlockSpec((r, w), lambda i: (0, 0)) for (r, w) in accs]
    out_shape = [jax.ShapeDtypeStruct((n_rows, w), dt) for (w, dt) in outs]
    out_shape += [jax.ShapeDtypeStruct((r, w), F32) for (r, w) in accs]
    return pl.pallas_call(
        body, name=name, grid=(n,), in_specs=in_specs, out_specs=out_specs, out_shape=out_shape,
        compiler_params=_params(("arbitrary",)),
    )(*[a for (a, _, _, _) in rows], *fulls)


def _colsum(x):
    return jnp.sum(x, axis=0, keepdims=True)


def _mean(x):
    return jnp.mean(x, axis=-1, keepdims=True)


def _modulate(x, sc, sh):
    return x * (1.0 + sc) + sh


def _shift_down(cur, prev8, j):
    tr = cur.shape[0]
    row8 = _iota(prev8.shape, 0)
    head = jnp.where(row8 < j, pltpu.roll(prev8, j, 0), pltpu.roll(cur[0:8], j, 0))
    return head if tr == 8 else jnp.concatenate([head, pltpu.roll(cur, j, 0)[8:]], axis=0)


def _shift_up(cur, next8, j):
    tr = cur.shape[0]
    row8 = _iota(next8.shape, 0)
    tail = jnp.where(row8 < 8 - j, pltpu.roll(cur[tr - 8:], 8 - j, 0), pltpu.roll(next8, 8 - j, 0))
    return jnp.concatenate([pltpu.roll(cur, tr - j, 0)[:tr - 8], tail], axis=0)


def _conv(cur, prev, w, b):
    out = cur * w[3:4] + b
    for j in (1, 2, 3):
        out = out + _shift_down(cur, prev, j) * w[3 - j:4 - j]
    return out


def _conv_fwd(p, w_xs, b_xs, w_bc, b_bc, s):
    def fn(pos, xs, xs_prev, bc, bc_prev, w_xs, b_xs, w_bc, b_bc):
        first = pos == 0
        xs_prev = jnp.where(first, 0.0, xs_prev)
        bc_prev = jnp.where(first, 0.0, bc_prev)
        return _silu(_conv(xs, xs_prev, w_xs, b_xs)), _silu(_conv(bc, bc_prev, w_bc, b_bc))

    return _rowk("conv_fwd", fn, s, 256,
                 [(p, D, OFF_XS // D, 0), (p, D, OFF_XS // D, -1), (p, 512, OFF_BC // 512, 0), (p, 512, OFF_BC // 512, -1)],
                 [w_xs, b_xs, w_bc, b_bc], [(D, F32), (512, F32)], [])


def _conv_bwd(dxs_a, dbc_a, p, w_xs, b_xs, w_bc, b_bc, s):
    tr = 256
    n = s // tr

    def fn(pos, da1, da1n, x1, x1p, x1n, da2, da2n, x2, x2p, x2n, w1, b1, w2, b2, aw1, ab1, aw2, ab2):
        dx1, dw1, db1 = _conv_bwd_fn(pos, n, da1, da1n, x1, x1p, x1n, w1, b1)
        dx2, dw2, db2 = _conv_bwd_fn(pos, n, da2, da2n, x2, x2p, x2n, w2, b2)
        return dx1, dx2, aw1 + dw1, ab1 + db1, aw2 + dw2, ab2 + db2

    cx, cb = OFF_XS // D, OFF_BC // 512
    return _rowk("conv_bwd", fn, s, tr,
                 [(dxs_a, D, 0, 0), (dxs_a, D, 0, 1), (p, D, cx, 0), (p, D, cx, -1), (p, D, cx, 1),
                  (dbc_a, 512, 0, 0), (dbc_a, 512, 0, 1), (p, 512, cb, 0), (p, 512, cb, -1), (p, 512, cb, 1)],
                 [w_xs, b_xs, w_bc, b_bc], [(D, BF16), (512, BF16)], [(8, D), (1, D), (8, 512), (1, 512)])


def _conv_bwd_fn(pos, n, da, da_next, x, x_prev, x_next, w, b):
    first, last = pos == 0, pos == n - 1
    x_prev = jnp.where(first, 0.0, x_prev)
    shifted = {j: _shift_down(x, x_prev, j) for j in (1, 2, 3)}
    conv = x * w[3:4] + b
    for j in (1, 2, 3):
        conv = conv + shifted[j] * w[3 - j:4 - j]
    dc = da * _dsilu(conv)
    dc_next = jnp.where(last, 0.0, da_next * _dsilu(_conv(x_next, x[x.shape[0] - 8:], w, b)))
    dx = dc * w[3:4]
    dws = [None] * 4
    dws[3] = _colsum(dc * x)
    for j in (1, 2, 3):
        dx = dx + _shift_up(dc, dc_next, j) * w[3 - j:4 - j]
        dws[3 - j] = _colsum(dc * shifted[j])
    row = _iota((8, x.shape[1]), 0)
    dw = jnp.zeros((8, x.shape[1]), F32)
    for k in range(4):
        dw = jnp.where(row == k, dws[k], dw)
    return dx, dw, _colsum(dc)


def _ssd_gates(dtf, bias, a_log):
    lane = _iota(dtf.shape, 1)
    head = lane < NH
    dt = jnp.where(head, _softplus(dtf + bias), 0.0)
    a_neg = jnp.where(_iota(a_log.shape, 1) < NH, -jnp.exp(a_log), 0.0)
    a = dt * a_neg
    cs = _sel_left(_tri_lower(CHUNK), a)
    return dt, a_neg, cs


def _decay_mask(cs_ref, cst_ref, h):
    diff = cs_ref[:, h:h + 1] - cst_ref[h:h + 1, :]
    low = _iota((CHUNK, CHUNK), 1) <= _iota((CHUNK, CHUNK), 0)
    return jnp.where(low, jnp.exp(jnp.minimum(diff, 0.0)), 0.0)


def _ssd_fwd(xs_a, bc_a, p, bias128, alog128, dskip_x, s):
    nc = s // CHUNK
    t = CHUNK

    def body(xs_ref, bc_ref, dtf_ref, bias_ref, alog_ref, dsk_ref, y_ref, st_ref,
             state, x_sc, xw_sc, cs_sc, cst_sc, yd_sc):
        c = pl.program_id(0)

        @pl.when(c == 0)
        def _():
            state[...] = jnp.zeros(state.shape, F32)

        dt, _, cs = _ssd_gates(dtf_ref[...], bias_ref[...], alog_ref[...])
        cs_sc[...] = cs
        cst_sc[...] = cs.T
        cs_last = cs[t - 1:t, :]
        expand = _head_expand()
        ex = _sel_right(jnp.concatenate([dt, jnp.exp(cs), jnp.exp(cs_last - cs)], axis=0), expand, terms=2)
        dt_x, eo_x, we_x = ex[0:t], ex[t:2 * t], ex[2 * t:3 * t]
        g_x = _sel_right(jnp.broadcast_to(jnp.exp(cs_last), (8, 128)), expand)[0:1]
        xs = xs_ref[...]
        x = xs * dt_x
        x_sc[...] = x.astype(BF16)
        xw_sc[...] = (x * we_x).astype(BF16)
        prev = state[...]
        st_ref[0] = prev
        prev_b = prev.astype(BF16)
        for g in range(2):
            cols = slice(g * 512, (g + 1) * 512)
            b_g = bc_ref[:, g * 128:(g + 1) * 128].astype(BF16)
            c_g = bc_ref[:, 256 + g * 128:256 + (g + 1) * 128].astype(BF16)
            gmat = _dot(c_g, b_g, NT)
            y_off = _dot(c_g, prev_b[:, cols]) * eo_x[:, cols]
            s_loc = _dot(b_g, xw_sc[:, cols], TN)
            state[:, cols] = g_x[:, cols] * prev[:, cols] + s_loc
            for e in range(HG):
                h = g * HG + e
                m = gmat * _decay_mask(cs_sc, cst_sc, h)
                yd_sc[:, h * HD:(h + 1) * HD] = _dot(m.astype(BF16), x_sc[:, h * HD:(h + 1) * HD])
            y_ref[:, cols] = yd_sc[:, cols] + y_off + dsk_ref[:, cols] * xs[:, cols]

    return pl.pallas_call(
        body, name="ssd_fwd", grid=(nc,),
        in_specs=[pl.BlockSpec((t, D), lambda c: (c, 0)),
                  pl.BlockSpec((t, 512), lambda c: (c, 0)),
                  pl.BlockSpec((t, 128), lambda c: (c, OFF_DTF // 128)),
                  pl.BlockSpec((1, 128), lambda c: (0, 0)),
                  pl.BlockSpec((1, 128), lambda c: (0, 0)),
                  pl.BlockSpec((1, D), lambda c: (0, 0))],
        out_specs=[pl.BlockSpec((t, D), lambda c: (c, 0)),
                   pl.BlockSpec((1, NSTATE, D), lambda c: (c, 0, 0))],
        out_shape=[jax.ShapeDtypeStruct((s, D), F32), jax.ShapeDtypeStruct((nc, NSTATE, D), F32)],
        scratch_shapes=[pltpu.VMEM((NSTATE, D), F32), pltpu.VMEM((t, D), BF16), pltpu.VMEM((t, D), BF16),
                        pltpu.VMEM((t, 128), F32), pltpu.VMEM((128, t), F32), pltpu.VMEM((t, D), F32)],
        compiler_params=_params(("arbitrary",)),
    )(xs_a, bc_a, p, bias128, alog128, dskip_x)


def _ssd_bwd(dy, xs_a, bc_a, p, states, bias128, alog128, dskip_x, s):
    nc = s // CHUNK
    t = CHUNK

    def body(dy_ref, xs_ref, bc_ref, dtf_ref, st_ref, bias_ref, alog_ref, dsk_ref,
             dxs_ref, dbc_ref, ddt_ref, dalog_ref, dskip_ref,
             dstate, x_sc, dy_sc, dx_sc, deo_sc, dwe_sc, cs_sc, cst_sc, dcol_sc, drow_sc):
        i = pl.program_id(0)

        @pl.when(i == 0)
        def _():
            dstate[...] = jnp.zeros(dstate.shape, F32)
            dalog_ref[...] = jnp.zeros(dalog_ref.shape, F32)
            dskip_ref[...] = jnp.zeros(dskip_ref.shape, F32)

        dtf = dtf_ref[...]
        dt, a_neg, cs = _ssd_gates(dtf, bias_ref[...], alog_ref[...])
        cs_sc[...] = cs
        cst_sc[...] = cs.T
        cs_last = cs[t - 1:t, :]
        eo, we, g_end = jnp.exp(cs), jnp.exp(cs_last - cs), jnp.exp(cs_last)
        expand, reduce = _head_expand(), _head_reduce()
        ex = _sel_right(jnp.concatenate([dt, eo, we], axis=0), expand, terms=2)
        dt_x, eo_x, we_x = ex[0:t], ex[t:2 * t], ex[2 * t:3 * t]
        g_x = _sel_right(jnp.broadcast_to(g_end, (8, 128)), expand)[0:1]
        xs = xs_ref[...]
        dyv = dy_ref[...]
        x = xs * dt_x
        x_sc[...] = x.astype(BF16)
        dy_sc[...] = dyv.astype(BF16)
        dyo_b = (dyv * eo_x).astype(BF16)
        xw_b = (x * we_x).astype(BF16)
        prev = st_ref[0]
        prev_b = prev.astype(BF16)
        dnext = dstate[...]
        dnext_b = dnext.astype(BF16)
        dcol_sc[...] = jnp.zeros(dcol_sc.shape, F32)
        drow_sc[...] = jnp.zeros(drow_sc.shape, F32)
        lane_row = _iota((1, 128), 1)
        sub_col = _iota((128, 1), 0)
        for g in range(2):
            cols = slice(g * 512, (g + 1) * 512)
            b_g = bc_ref[:, g * 128:(g + 1) * 128].astype(BF16)
            c_g = bc_ref[:, 256 + g * 128:256 + (g + 1) * 128].astype(BF16)
            gmat = _dot(c_g, b_g, NT)
            b_ds = _dot(b_g, dnext_b[:, cols])
            c_s = _dot(c_g, prev_b[:, cols])
            dx_sc[:, cols] = b_ds * we_x[:, cols]
            deo_sc[:, cols] = dyv[:, cols] * c_s
            dwe_sc[:, cols] = b_ds * x[:, cols]
            db = _dot(xw_b[:, cols], dnext_b[:, cols], NT)
            dc = _dot(dyo_b[:, cols], prev_b[:, cols], NT)
            dstate[:, cols] = g_x[:, cols] * dnext[:, cols] + _dot(c_g, dyo_b[:, cols], TN)
            dg = jnp.zeros((t, t), F32)
            for e in range(HG):
                h = g * HG + e
                hc = slice(h * HD, (h + 1) * HD)
                lmat = _decay_mask(cs_sc, cst_sc, h)
                m = gmat * lmat
                dx_sc[:, hc] += _dot(m.astype(BF16), dy_sc[:, hc], TN)
                dm = _dot(dy_sc[:, hc], x_sc[:, hc], NT)
                dg = dg + dm * lmat
                qm = dm * m
                dcol_sc[...] += jnp.sum(qm, axis=1, keepdims=True) * (lane_row == h).astype(F32)
                drow_sc[...] += (sub_col == h).astype(F32) * jnp.sum(qm, axis=0, keepdims=True)
            dg_b = dg.astype(BF16)
            dbc_ref[:, g * 128:(g + 1) * 128] = db + _dot(dg_b, c_g, TN)
            dbc_ref[:, 256 + g * 128:256 + (g + 1) * 128] = dc + _dot(dg_b, b_g)
        d_eo = _sel_right(deo_sc[...], reduce, terms=2)
        d_we = _sel_right(dwe_sc[...], reduce, terms=2)
        d_gend = _sel_right(jnp.broadcast_to(_colsum(dnext * prev), (8, D)), reduce)[0:1]
        d_cs = dcol_sc[...] - drow_sc[...].T + d_eo * eo - d_we * we
        extra = _colsum(d_we * we) + d_gend * g_end
        d_cs = d_cs + jnp.where(_iota((t, 128), 0) == t - 1, extra, 0.0)
        da = _sel_left(_tri_upper(t), d_cs)
        dx = dx_sc[...]
        ddt = _sel_right(dx * xs, reduce, terms=2) + da * a_neg
        dxs_ref[...] = dx * dt_x + dsk_ref[...] * dyv
        ddt_ref[...] = jnp.where(_iota((t, 128), 1) < NH, ddt * _sigmoid(dtf + bias_ref[...]), 0.0)
        dalog_ref[...] += _colsum(da * dt) * a_neg
        dskip_ref[...] += _sel_right(jnp.broadcast_to(_colsum(dyv * xs), (8, D)), reduce)[0:1]

    rev = lambda i: nc - 1 - i
    return pl.pallas_call(
        body, name="ssd_bwd", grid=(nc,),
        in_specs=[pl.BlockSpec((t, D), lambda i: (rev(i), 0)),
                  pl.BlockSpec((t, D), lambda i: (rev(i), 0)),
                  pl.BlockSpec((t, 512), lambda i: (rev(i), 0)),
                  pl.BlockSpec((t, 128), lambda i: (rev(i), OFF_DTF // 128)),
                  pl.BlockSpec((1, NSTATE, D), lambda i: (rev(i), 0, 0)),
                  pl.BlockSpec((1, 128), lambda i: (0, 0)),
                  pl.BlockSpec((1, 128), lambda i: (0, 0)),
                  pl.BlockSpec((1, D), lambda i: (0, 0))],
        out_specs=[pl.BlockSpec((t, D), lambda i: (rev(i), 0)),
                   pl.BlockSpec((t, 512), lambda i: (rev(i), 0)),
                   pl.BlockSpec((t, 128), lambda i: (rev(i), 0)),
                   pl.BlockSpec((1, 128), lambda i: (0, 0)),
                   pl.BlockSpec((1, 128), lambda i: (0, 0))],
        out_shape=[jax.ShapeDtypeStruct((s, D), F32), jax.ShapeDtypeStruct((s, 512), F32),
                   jax.ShapeDtypeStruct((s, 128), F32), jax.ShapeDtypeStruct((1, 128), F32),
                   jax.ShapeDtypeStruct((1, 128), F32)],
        scratch_shapes=[pltpu.VMEM((NSTATE, D), F32), pltpu.VMEM((t, D), BF16), pltpu.VMEM((t, D), BF16),
                        pltpu.VMEM((t, D), F32), pltpu.VMEM((t, D), F32), pltpu.VMEM((t, D), F32),
                        pltpu.VMEM((t, 128), F32), pltpu.VMEM((128, t), F32),
                        pltpu.VMEM((t, 128), F32), pltpu.VMEM((128, t), F32)],
        compiler_params=_params(("arbitrary",)),
    )(dy, xs_a, bc_a, p, states, bias128, alog128, dskip_x)


def _gate_lanes(shape):
    lane = _iota(shape, 1)
    return (lane >= NH) & (lane < 2 * NH)


def _cum_fwd(p, bias128, s):
    tr = min(512, s)

    def body(dtf_ref, bias_ref, o_ref, carry):
        @pl.when(pl.program_id(0) == 0)
        def _():
            carry[...] = jnp.zeros(carry.shape, F32)

        lf = jnp.where(_gate_lanes((tr, 128)), _log_sigmoid(dtf_ref[...] + bias_ref[...]), 0.0)
        cum = _sel_left(_tri_lower(tr), lf) + carry[...]
        carry[...] = cum[tr - 1:tr, :]
        o_ref[...] = cum

    return pl.pallas_call(
        body, name="cum_fwd", grid=(s // tr,),
        in_specs=[pl.BlockSpec((tr, 128), lambda i: (i, OFF_DTF // 128)), pl.BlockSpec((1, 128), lambda i: (0, 0))],
        out_specs=pl.BlockSpec((tr, 128), lambda i: (i, 0)),
        out_shape=jax.ShapeDtypeStruct((s, 128), F32),
        scratch_shapes=[pltpu.VMEM((1, 128), F32)],
        compiler_params=_params(("arbitrary",)),
    )(p, bias128)


def _cum_bwd(dcum, ddt_raw, p, bias128, s):
    tr = min(512, s)

    def fn(pos, dcum, ddt, dtf, bias, carry, acc):
        suffix = _sel_left(_tri_upper(tr), dcum) + carry
        dfr = jnp.where(_gate_lanes((tr, 128)), suffix * _sigmoid(-(dtf + bias)), 0.0)
        out = ddt + dfr
        return out, suffix[0:1, :], acc + _colsum(out)

    return _rowk("cum_bwd", fn, s, tr, [(dcum, 128, 0, 0), (ddt_raw, 128, 0, 0), (p, 128, OFF_DTF // 128, 0)],
                 [bias128], [(128, BF16)], [(1, 128), (1, 128)], reverse=True)


ATT_BLOCK = 512
ATT_STRIP = 32


def _head_part(shape, h, dim):
    i = _iota(shape, dim)
    return (i >= h * HD) & (i < (h + 1) * HD)


def _k_augmented(k_blk, cum_blk, j, h):
    tk = k_blk.shape[0]
    lane = _iota((tk, 128), 1)
    col = jnp.sum(jnp.where(lane == NH + 2 * j + h, cum_blk, 0.0), axis=1, keepdims=True)
    c0, c1, c2 = [c.astype(F32) for c in _split3(-col)]
    k_h = k_blk if h == 0 else pltpu.roll(k_blk, HD, 1)
    aug = jnp.where(lane == HD, c0, jnp.where(lane == HD + 1, c1, jnp.where(lane == HD + 2, c2, 0.0)))
    return jnp.where(lane < HD, k_h, aug).astype(BF16)


def _q_augmented_t(q_blk):
    tq = q_blk.shape[0]
    q_t = (q_blk * ATT_SCALE).T.astype(BF16)
    ones = (_iota((HD, tq), 0) < 3).astype(BF16)
    return [jnp.concatenate([q_t[h * HD:(h + 1) * HD], ones], axis=0) for h in range(2)]


def _rows01(r0, r1):
    sub = _iota((8, r0.shape[1]), 0)
    return jnp.where(sub == 0, r0, jnp.where(sub == 1, r1, 0.0))


def _fold8(x, op, cur):
    for g in range(x.shape[0] // 8):
        cur = op(cur, x[8 * g:8 * (g + 1), :])
    return cur


def _attn_fwd(p, cum, s):
    t = min(ATT_BLOCK, s)
    nq = s // t
    r = ATT_STRIP

    def body(q_ref, k_ref, v_ref, c_ref, o_ref, lse_ref, kaug_sc, vt_sc, s0_sc, s1_sc, p0_sc, p1_sc, m_sc, l_sc, acc_sc):
        j, qi = pl.program_id(0), pl.program_id(1)
        s_sc, p_sc = (s0_sc, s1_sc), (p0_sc, p1_sc)

        @pl.when(qi == 0)
        def _():
            for c in range(nq):
                rows = slice(c * t, (c + 1) * t)
                k_blk, vt = k_ref[rows, :], v_ref[rows, :].T
                for h in range(2):
                    kaug_sc[h, rows, :] = _k_augmented(k_blk, c_ref[rows, :], j, h)
                    vt_sc[h, :, rows] = vt[h * HD:(h + 1) * HD].astype(BF16)

        qaug_t = _q_augmented_t(q_ref[...])
        m_sc[...] = jnp.full(m_sc.shape, -1e30, F32)
        l_sc[...] = jnp.zeros(l_sc.shape, F32)
        acc_sc[...] = jnp.zeros(acc_sc.shape, F32)
        top = _iota((128, t), 0) < HD

        def logits(kb, buf):
            kv = pl.ds(pl.multiple_of(kb * t, t), t)
            for h in range(2):
                s_sc[buf][h] = _dot(kaug_sc[h, kv, :], qaug_t[h])

        def softmax(buf, diagonal):
            alphas = []
            for h in range(2):
                cur = jnp.full((8, t), -1e30, F32)
                for i in range(t // r):
                    rows = slice(i * r, (i + 1) * r)
                    x = s_sc[buf][h, rows, :]
                    if diagonal:
                        x = jnp.where(_iota((r, t), 1) >= i * r + _iota((r, t), 0), x, -1e30)
                        s_sc[buf][h, rows, :] = x
                    cur = _fold8(x, jnp.maximum, cur)
                m_prev = m_sc[h, 0:1, :]
                m_new = jnp.maximum(m_prev, jnp.max(cur, axis=0, keepdims=True))
                alpha = jnp.exp(m_prev - m_new)
                m_sc[h, 0:1, :] = m_new
                alphas.append(alpha)
                tot = jnp.zeros((8, t), F32)
                for i in range(t // r):
                    rows = slice(i * r, (i + 1) * r)
                    pr = jnp.exp(s_sc[buf][h, rows, :] - m_new)
                    p_sc[buf][h, rows, :] = pr.astype(BF16)
                    tot = _fold8(pr, jnp.add, tot)
                l_sc[h, 0:1, :] = alpha * l_sc[h, 0:1, :] + jnp.sum(tot, axis=0, keepdims=True)
            return alphas

        def accumulate(kb, buf, alphas):
            kv = pl.ds(pl.multiple_of(kb * t, t), t)
            for h in range(2):
                part = slice(h * HD, (h + 1) * HD)
                acc_sc[part, :] = acc_sc[part, :] * alphas[h] + _dot(vt_sc[h, :, kv], p_sc[buf][h])

        def first_trip():
            logits(0, 1)
            accumulate(qi, 0, softmax(0, True))
            logits(jnp.minimum(1, qi - 1), 0)
            return tuple(softmax(1, False))

        def only_diagonal():
            accumulate(qi, 0, softmax(0, True))
            return (jnp.ones((1, t), F32),) * 2

        def steady(u, alphas_b):
            accumulate(2 * u - 2, 1, alphas_b)
            logits(2 * u, 1)
            accumulate(2 * u - 1, 0, softmax(0, False))
            logits(jnp.minimum(2 * u + 1, qi - 1), 0)
            return tuple(softmax(1, False))

        logits(qi, 0)
        n_blocks = qi + 1
        alphas_b = lax.cond(qi >= 1, first_trip, only_diagonal)
        alphas_b = lax.fori_loop(1, n_blocks // 2, steady, alphas_b)
        last_b = 2 * (n_blocks // 2) - 2

        @pl.when((qi >= 1) & (n_blocks % 2 == 0))
        def _():
            accumulate(last_b, 1, alphas_b)

        @pl.when((qi >= 2) & (n_blocks % 2 == 1))
        def _():
            accumulate(last_b, 1, alphas_b)
            accumulate(qi - 1, 0, softmax(0, False))

        l0, l1 = l_sc[0, 0:1, :], l_sc[1, 0:1, :]
        o_ref[...] = (acc_sc[...] / jnp.where(top, l0, l1)).T
        lse_ref[0] = _rows01(m_sc[0, 0:1, :] + jnp.log(l0), m_sc[1, 0:1, :] + jnp.log(l1))

    return pl.pallas_call(
        body, name="attn_fwd", grid=(NH // 2, nq),
        in_specs=[pl.BlockSpec((t, 128), lambda j, qi: (qi, OFF_Q // 128 + j)),
                  pl.BlockSpec((s, 128), lambda j, qi: (0, OFF_K // 128 + j)),
                  pl.BlockSpec((s, 128), lambda j, qi: (0, OFF_V // 128 + j)),
                  pl.BlockSpec((s, 128), lambda j, qi: (0, 0))],
        out_specs=[pl.BlockSpec((t, 128), lambda j, qi: (qi, j)),
                   pl.BlockSpec((1, 8, t), lambda j, qi: (j, 0, qi))],
        out_shape=[jax.ShapeDtypeStruct((s, D), F32), jax.ShapeDtypeStruct((NH // 2, 8, s), F32)],
        scratch_shapes=[pltpu.VMEM((2, s, 128), BF16), pltpu.VMEM((2, HD, s), BF16), pltpu.VMEM((2, t, t), F32),
                        pltpu.VMEM((2, t, t), F32), pltpu.VMEM((2, t, t), BF16), pltpu.VMEM((2, t, t), BF16),
                        pltpu.VMEM((2, 8, t), F32), pltpu.VMEM((2, 8, t), F32), pltpu.VMEM((128, t), F32)],
        compiler_params=_params(("parallel", "arbitrary")),
    )(p, p, p, cum)


def _attn_bwd(p, cum, o, lse, do, s):
    t = min(ATT_BLOCK, s)
    nq = s // t
    r = ATT_STRIP

    def body(q_ref, k_ref, v_ref, c_ref, o_ref, lse_ref, do_ref, dq_ref, dk_ref, dv_ref, dc_ref, dr_ref,
             qaugt_sc, qh_sc, dot_sc, doh_sc, delta_sc, dqt_sc, dr_sc, kaug_sc, vh_sc, kt_sc,
             s0_sc, s1_sc, dp0_sc, dp1_sc, p0_sc, p1_sc, ds0_sc, ds1_sc, dk_sc, dv_sc, dc_sc):
        j, ki = pl.program_id(0), pl.program_id(1)
        s_sc, dp_sc, p_sc, ds_sc = (s0_sc, s1_sc), (dp0_sc, dp1_sc), (p0_sc, p1_sc), (ds0_sc, ds1_sc)

        @pl.when(ki == 0)
        def _():
            for c in range(nq):
                rows = slice(c * t, (c + 1) * t)
                q_blk, do_blk = q_ref[rows, :], do_ref[rows, :]
                qaugt_sc[0, :, rows], qaugt_sc[1, :, rows] = _q_augmented_t(q_blk)
                dot_sc[:, rows] = do_blk.T.astype(BF16)
                prod_t = (do_blk * o_ref[rows, :]).T
                delta_sc[:, rows] = _rows01(jnp.sum(prod_t[0:HD], axis=0, keepdims=True),
                                            jnp.sum(prod_t[HD:], axis=0, keepdims=True))
                for h in range(2):
                    head = _head_part((t, 128), h, 1)
                    qh_sc[h, rows, :] = jnp.where(head, q_blk * ATT_SCALE, 0.0).astype(BF16)
                    doh_sc[h, rows, :] = jnp.where(head, do_blk, 0.0).astype(BF16)
            dqt_sc[...] = jnp.zeros(dqt_sc.shape, F32)
            dr_sc[...] = jnp.zeros(dr_sc.shape, F32)

        k_blk, v_blk = k_ref[...], v_ref[...]
        kt = k_blk.T
        for h in range(2):
            kaug_sc[h] = _k_augmented(k_blk, c_ref[...], j, h)
            vh_sc[h] = jnp.where(_head_part((t, 128), h, 1), v_blk, 0.0).astype(BF16)
            kt_sc[h] = kt[h * HD:(h + 1) * HD].astype(BF16)
        dk_sc[...] = jnp.zeros(dk_sc.shape, F32)
        dv_sc[...] = jnp.zeros(dv_sc.shape, F32)
        dc_sc[...] = jnp.zeros(dc_sc.shape, F32)

        def inputs(qb, buf):
            qs = pl.ds(pl.multiple_of(qb * t, t), t)
            for h in range(2):
                s_sc[buf][h] = _dot(kaug_sc[h], qaugt_sc[h, :, qs])
                dp_sc[buf][h] = _dot(vh_sc[h], dot_sc[:, qs])

        def elementwise(qb, buf, diagonal):
            qs = pl.ds(pl.multiple_of(qb * t, t), t)
            for h in range(2):
                lse_row, delta_row = lse_ref[0, h:h + 1, qs], delta_sc[h:h + 1, qs]
                tot = jnp.zeros((8, t), F32)
                for i in range(t // r):
                    rows = slice(i * r, (i + 1) * r)
                    x = s_sc[buf][h, rows, :]
                    if diagonal:
                        x = jnp.where(_iota((r, t), 1) >= i * r + _iota((r, t), 0), x, -1e30)
                    pr = jnp.exp(x - lse_row)
                    ds = pr * (dp_sc[buf][h, rows, :] - delta_row)
                    p_sc[buf][h, rows, :] = pr.astype(BF16)
                    ds_sc[buf][h, rows, :] = ds.astype(BF16)
                    dc_sc[h, rows, :] += sum(ds[:, 128 * g:128 * (g + 1)] for g in range(t // 128))
                    tot = _fold8(ds, jnp.add, tot)
                dr_sc[h, :, qs] += tot

        def outputs(qb, buf):
            qs = pl.ds(pl.multiple_of(qb * t, t), t)
            dv_sc[...] += _dot(p_sc[buf][0], doh_sc[0, qs, :]) + _dot(p_sc[buf][1], doh_sc[1, qs, :])
            dk_sc[...] += _dot(ds_sc[buf][0], qh_sc[0, qs, :]) + _dot(ds_sc[buf][1], qh_sc[1, qs, :])
            for h in range(2):
                dqt_sc[h * HD:(h + 1) * HD, qs] += _dot(kt_sc[h], ds_sc[buf][h])

        def pair(a, b, a_diagonal):
            inputs(a, 0)
            inputs(b, 1)
            elementwise(a, 0, a_diagonal)
            outputs(a, 0)
            elementwise(b, 1, False)
            outputs(b, 1)

        def later(u, carry):
            pair(ki + 1 + 2 * u, ki + 2 + 2 * u, False)
            return carry

        n_later = nq - 1 - ki
        lax.fori_loop(0, n_later // 2, later, 0)

        @pl.when(n_later % 2 == 1)
        def _():
            pair(ki, nq - 1, True)

        @pl.when(n_later % 2 == 0)
        def _():
            inputs(ki, 0)
            elementwise(ki, 0, True)
            outputs(ki, 0)

        dk_ref[...] = dk_sc[...].astype(BF16)
        dv_ref[...] = dv_sc[...].astype(BF16)
        lane = _iota((t, 128), 1)
        cols = jnp.where(lane == 0, jnp.sum(dc_sc[0], axis=1, keepdims=True),
                         jnp.where(lane == 1, jnp.sum(dc_sc[1], axis=1, keepdims=True), 0.0))
        dc_ref[0] = cols.T[0:8, :]

        @pl.when(ki == nq - 1)
        def _():
            for c in range(nq):
                rows = slice(c * t, (c + 1) * t)
                dq_ref[rows, :] = dqt_sc[:, rows].T * ATT_SCALE
            dr_ref[0] = _rows01(jnp.sum(dr_sc[0], axis=0, keepdims=True), jnp.sum(dr_sc[1], axis=0, keepdims=True))

    whole = lambda off: pl.BlockSpec((s, 128), functools.partial(lambda j, ki, off: (0, off + j), off=off))
    return pl.pallas_call(
        body, name="attn_bwd", grid=(NH // 2, nq),
        in_specs=[whole(OFF_Q // 128),
                  pl.BlockSpec((t, 128), lambda j, ki: (ki, OFF_K // 128 + j)),
                  pl.BlockSpec((t, 128), lambda j, ki: (ki, OFF_V // 128 + j)),
                  pl.BlockSpec((t, 128), lambda j, ki: (ki, 0)),
                  whole(0),
                  pl.BlockSpec((1, 8, s), lambda j, ki: (j, 0, 0)),
                  whole(0)],
        out_specs=[whole(0),
                   pl.BlockSpec((t, 128), lambda j, ki: (ki, j)),
                   pl.BlockSpec((t, 128), lambda j, ki: (ki, j)),
                   pl.BlockSpec((1, 8, t), lambda j, ki: (j, 0, ki)),
                   pl.BlockSpec((1, 8, s), lambda j, ki: (j, 0, 0))],
        out_shape=[jax.ShapeDtypeStruct((s, D), F32), jax.ShapeDtypeStruct((s, D), BF16), jax.ShapeDtypeStruct((s, D), BF16),
                   jax.ShapeDtypeStruct((NH // 2, 8, s), F32), jax.ShapeDtypeStruct((NH // 2, 8, s), F32)],
        scratch_shapes=[pltpu.VMEM((2, 128, s), BF16), pltpu.VMEM((2, s, 128), BF16), pltpu.VMEM((128, s), BF16),
                        pltpu.VMEM((2, s, 128), BF16), pltpu.VMEM((8, s), F32), pltpu.VMEM((128, s), F32),
                        pltpu.VMEM((2, 8, s), F32), pltpu.VMEM((2, t, 128), BF16), pltpu.VMEM((2, t, 128), BF16),
                        pltpu.VMEM((2, HD, t), BF16)]
        + [pltpu.VMEM((2, t, t), F32)] * 4 + [pltpu.VMEM((2, t, t), BF16)] * 4
        + [pltpu.VMEM((t, 128), F32), pltpu.VMEM((t, 128), F32), pltpu.VMEM((2, t, 128), F32)],
        compiler_params=_params(("parallel", "arbitrary")),
    )(p, p, p, cum, o, lse, do)


def _ln_stats(u):
    mu = _mean(u)
    d = u - mu
    rstd = lax.rsqrt(_mean(d * d) + EPS)
    return d * rstd, rstd


def _ln_bwd(dx, xh, rstd, gam):
    dxh = dx * gam
    return rstd * (dxh - _mean(dxh) - xh * _mean(dxh * xh))


def _rms_bwd(d, xn, r, w):
    t = d * w
    return r * (t - xn * _mean(t * xn)), _colsum(d * xn)


def _mix_norm(y, p, att, w_ssm, w_att, s):
    def fn(pos, y, z, att, w1, w2):
        g = y * _silu(z)
        n1 = g * lax.rsqrt(_mean(g * g) + EPS) * w1
        n2 = att * lax.rsqrt(_mean(att * att) + EPS) * w2
        return (jnp.concatenate([n1, n2], axis=1),)

    return _rowk("mix_norm", fn, s, 512, [(y, D, 0, 0), (p, D, OFF_Z // D, 0), (att, D, 0, 0)],
                 [w_ssm, w_att], [(2 * D, BF16)], [])[0]


def _mix_norm_bwd(dmix, y, p, att, w_ssm, w_att, s):
    def fn(pos, dmix, y, z, att, w1, w2, a1, a2):
        sz = _silu(z)
        g = y * sz
        r1 = lax.rsqrt(_mean(g * g) + EPS)
        dg, dw1 = _rms_bwd(dmix[:, :D], g * r1, r1, w1)
        r2 = lax.rsqrt(_mean(att * att) + EPS)
        datt, dw2 = _rms_bwd(dmix[:, D:], att * r2, r2, w2)
        return dg * sz, dg * y * _dsilu(z), datt, a1 + dw1, a2 + dw2

    return _rowk("mix_norm_bwd", fn, s, 256, [(dmix, 2 * D, 0, 0), (y, D, 0, 0), (p, D, OFF_Z // D, 0), (att, D, 0, 0)],
                 [w_ssm, w_att], [(D, F32), (D, BF16), (D, F32)], [(1, D), (1, D)])


def _ln1(x0, y, g1, gam, bet, sc2, sh2, s):
    def fn(pos, x0, y, g1, gam, bet, sc2, sh2):
        xh, _ = _ln_stats(ALPHA * x0 + (1.0 + g1) * y)
        x1 = xh * gam + bet
        return x1, _modulate(x1, sc2, sh2)

    return _rowk("ln1", fn, s, 512, [(x0, D, 0, 0), (y, D, 0, 0)], [g1, gam, bet, sc2, sh2], [(D, F32), (D, BF16)], [])


def _ln2_loss(x1, ff, tgt, g2, gam, bet, s):
    def fn(pos, x1, ff, tgt, g2, gam, bet, a_loss, a_dgam, a_dbet, a_dg2):
        xh, rstd = _ln_stats(ALPHA * x1 + (1.0 + g2) * ff)
        err = xh * gam + bet - tgt
        dx2 = err * (1.0 / D)
        du = _ln_bwd(dx2, xh, rstd, gam)
        return (du, du * (1.0 + g2), a_loss + _colsum(err * err), a_dgam + _colsum(dx2 * xh),
                a_dbet + _colsum(dx2), a_dg2 + _colsum(du * ff))

    return _rowk("ln2_loss", fn, s, 512, [(x1, D, 0, 0), (ff, D, 0, 0), (tgt, D, 0, 0)], [g2, gam, bet],
                 [(D, F32), (D, BF16)], [(1, D)] * 4)


def _ln1_bwd(dh2, du2, x0, y, g1, gam, bet, sc2, s):
    def fn(pos, dh2, du2, x0, y, g1, gam, bet, sc2, a_sc, a_sh, a_gam, a_bet, a_g1):
        xh, rstd = _ln_stats(ALPHA * x0 + (1.0 + g1) * y)
        x1 = xh * gam + bet
        dx1 = ALPHA * du2 + dh2 * (1.0 + sc2)
        du1 = _ln_bwd(dx1, xh, rstd, gam)
        return (du1, du1 * (1.0 + g1), a_sc + _colsum(dh2 * x1), a_sh + _colsum(dh2), a_gam + _colsum(dx1 * xh),
                a_bet + _colsum(dx1), a_g1 + _colsum(du1 * y))

    return _rowk("ln1_bwd", fn, s, 512, [(dh2, D, 0, 0), (du2, D, 0, 0), (x0, D, 0, 0), (y, D, 0, 0)],
                 [g1, gam, bet, sc2], [(D, F32), (D, BF16)], [(1, D)] * 5)


def _input_grad(dh1, du1, x0, sc1, s):
    def fn(pos, dh1, du1, x0, sc1, a_sc, a_sh):
        return ALPHA * du1 + dh1 * (1.0 + sc1), a_sc + _colsum(dh1 * x0), a_sh + _colsum(dh1)

    return _rowk("input_grad", fn, s, 512, [(dh1, D, 0, 0), (du1, D, 0, 0), (x0, D, 0, 0)], [sc1],
                 [(D, F32)], [(1, D)] * 2)


def _adamw_math(w, grad, m, v):
    m_new = ADAM_B1 * m + (1.0 - ADAM_B1) * grad
    v_new = ADAM_B2 * v + (1.0 - ADAM_B2) * (grad * grad)
    m_hat = m_new / (1.0 - ADAM_B1 ** ADAM_STEP)
    v_hat = v_new / (1.0 - ADAM_B2 ** ADAM_STEP)
    return -ADAM_LR * (m_hat / (jnp.sqrt(v_hat) + ADAM_EPS) + ADAM_WD * w), m_new, v_new


def _small_update(small_all, layout, w, m, v):
    names = [n for n, _, _ in layout]

    def body(*refs):
        all_ref = refs[0]
        w_refs, m_refs, v_refs = [refs[1 + k * len(names):1 + (k + 1) * len(names)] for k in range(3)]
        sum_ref = refs[1 + 3 * len(names)]
        outs = refs[2 + 3 * len(names):]
        total = all_ref[0]
        for k in range(1, N_DEV):
            total = total + all_ref[k]
        sum_ref[...] = total
        for i, (_, off, size) in enumerate(layout):
            grad = total[:, off:off + size]
            delta, m_new, v_new = _adamw_math(w_refs[i][...], grad, m_refs[i][...], v_refs[i][...])
            for o, val in zip(outs[4 * i:4 * i + 4], (grad, delta, m_new, v_new)):
                o[...] = val

    res = pl.pallas_call(
        body, name="small_update",
        out_shape=[jax.ShapeDtypeStruct(small_all.shape[1:], F32)]
        + [jax.ShapeDtypeStruct(w[n].shape, F32) for n in names for _ in range(4)],
        compiler_params=_params(None),
    )(small_all, *[w[n] for n in names], *[m[n] for n in names], *[v[n] for n in names])
    return res[0], {n: res[1 + 4 * i:5 + 4 * i] for i, n in enumerate(names)}


def _adamw(name, w, g, m, v, *, tr, slots):
    r, c = w.shape

    def body(w_ref, g_ref, m_ref, v_ref, g_out, d_out, m_out, v_out):
        if slots:
            grad = g_ref[0][:, :c].astype(F32)
            for k in range(1, N_DEV):
                grad = grad + g_ref[k][:, :c].astype(F32)
        else:
            grad = g_ref[...]
        g_out[...] = grad
        d_out[...], m_out[...], v_out[...] = _adamw_math(w_ref[...], grad, m_ref[...], v_ref[...])

    tile = pl.BlockSpec((tr, c), lambda i: (i, 0))
    g_spec = pl.BlockSpec((N_DEV, tr, g.shape[-1]), lambda i: (0, i, 0)) if slots else tile
    return pl.pallas_call(
        body, name=name, grid=(r // tr,),
        in_specs=[tile, g_spec, tile, tile], out_specs=[tile] * 4,
        out_shape=[jax.ShapeDtypeStruct((r, c), F32)] * 4,
        compiler_params=_params(("parallel",)),
    )(w, g, m, v)


def _dot_f32(a, b, dims=NN):
    a0, a1, a2 = _split3(a)
    b0, b1, b2 = _split3(b)
    acc = _dot(a0, b0, dims)
    for x, y in ((a0, b1), (a1, b0), (a1, b1), (a0, b2), (a2, b0)):
        acc = acc + _dot(x, y, dims)
    return acc


def _ada_mod(c_all, w_shard, b_shard):
    def body(c_ref, w_ref, b_ref, o_ref):
        act = _silu(c_ref[...])
        act16 = jnp.concatenate([act, jnp.zeros_like(act)], axis=0)
        o_ref[...] = _dot_f32(act16, w_ref[...])[0:N_DEV] + b_ref[...]

    return pl.pallas_call(
        body, name="ada_mod", out_shape=jax.ShapeDtypeStruct((N_DEV, w_shard.shape[1]), F32),
        compiler_params=_params(None),
    )(c_all, w_shard, b_shard)


def _ada_grad(c_all, dmod_cols):
    def body(c_ref, dc_ref, gw_ref):
        act = _silu(c_ref[...])
        act16 = jnp.concatenate([act, jnp.zeros_like(act)], axis=0)
        dm = dc_ref[...]
        dm16 = jnp.concatenate([dm, jnp.zeros_like(dm)], axis=0)
        gw_ref[...] = _dot_f32(act16, dm16, TN)

    return pl.pallas_call(
        body, name="ada_grad", out_shape=jax.ShapeDtypeStruct((D, dmod_cols.shape[1]), F32),
        compiler_params=_params(None),
    )(c_all, dmod_cols)


def _exchange(name, xs, scatter):
    n = len(xs)
    n_peer = N_DEV - 1

    def body(*refs):
        x_refs, o_refs = refs[:n], refs[n:2 * n]
        send_sems, recv_sems, local_sems = refs[2 * n:]
        mx, my, mc = lax.axis_index("x"), lax.axis_index("y"), lax.axis_index("c")
        me = 4 * mx + 2 * my + mc

        def src(a, slot):
            return x_refs[a].at[slot] if scatter else x_refs[a]

        own = [pltpu.make_async_copy(src(a, me), o_refs[a].at[me], local_sems.at[a]) for a in range(n)]
        for cp in own:
            cp.start()
        sends = []
        for d in range(1, N_DEV):
            px = 1 - mx if d & 4 else mx
            py = 1 - my if d & 2 else my
            pc = 1 - mc if d & 1 else mc
            peer = 4 * px + 2 * py + pc
            for a in range(n):
                def copy(src_slot, dst_slot, a=a, d=d, to=(px, py, pc)):
                    return pltpu.make_async_remote_copy(
                        src_ref=src(a, src_slot), dst_ref=o_refs[a].at[dst_slot],
                        send_sem=send_sems.at[a * n_peer + d - 1], recv_sem=recv_sems.at[a * n_peer + d - 1],
                        device_id=to, device_id_type=pl.DeviceIdType.MESH)

                out = copy(peer, me)
                out.start()
                sends.append((out, copy(me, peer)))
        for _, arrival in sends:
            arrival.wait_recv()
        for out, _ in sends:
            out.wait_send()
        for cp in own:
            cp.wait()

    shapes = [tuple(x.shape[1:] if scatter else x.shape) for x in xs]
    return pl.pallas_call(
        body, name=name,
        in_specs=[pl.BlockSpec(memory_space=pl.ANY)] * n, out_specs=[pl.BlockSpec(memory_space=pl.ANY)] * n,
        out_shape=[jax.ShapeDtypeStruct((N_DEV,) + sh, x.dtype) for sh, x in zip(shapes, xs)],
        scratch_shapes=[pltpu.SemaphoreType.DMA((n * n_peer,)), pltpu.SemaphoreType.DMA((n * n_peer,)),
                        pltpu.SemaphoreType.DMA((n,))],
        compiler_params=pltpu.CompilerParams(has_side_effects=True),
    )(*xs)


def _gather_two_level(name, x):
    def body(x_ref, o_ref, send_sems, recv_sems, local_sem):
        mx, my, mc = lax.axis_index("x"), lax.axis_index("y"), lax.axis_index("c")
        me, sibling = (mx, my, mc), (mx, my, 1 - mc)
        chips = [(1 - mx, my), (mx, 1 - my), (1 - mx, 1 - my)]

        def slot(px, py, pc):
            return o_ref.at[4 * px + 2 * py + pc]

        def copy(k, block, to, src=None):
            return pltpu.make_async_remote_copy(
                src_ref=slot(*block) if src is None else src, dst_ref=slot(*block),
                send_sem=send_sems.at[k], recv_sem=recv_sems.at[k], device_id=to, device_id_type=pl.DeviceIdType.MESH)

        mine = pltpu.make_async_copy(x_ref, slot(*me), local_sem)
        mine.start()
        first = [copy(0, me, sibling, src=x_ref)] + [copy(1 + i, me, (*chip, mc), src=x_ref) for i, chip in enumerate(chips)]
        for cp in first:
            cp.start()
        passed = [copy(4 + i, (*chip, mc), sibling) for i, chip in enumerate(chips)]
        for i, chip in enumerate(chips):
            copy(1 + i, (*chip, mc), me).wait_recv()
            passed[i].start()
        copy(0, sibling, me).wait_recv()
        for i, chip in enumerate(chips):
            copy(4 + i, (*chip, 1 - mc), me).wait_recv()
        for cp in first + passed:
            cp.wait_send()
        mine.wait()

    return pl.pallas_call(
        body, name=name,
        in_specs=[pl.BlockSpec(memory_space=pl.ANY)], out_specs=pl.BlockSpec(memory_space=pl.ANY),
        out_shape=jax.ShapeDtypeStruct((N_DEV,) + tuple(x.shape), x.dtype),
        scratch_shapes=[pltpu.SemaphoreType.DMA((7,)), pltpu.SemaphoreType.DMA((7,)), pltpu.SemaphoreType.DMA(())],
        compiler_params=pltpu.CompilerParams(has_side_effects=True),
    )(x)


def _after(x, zero):
    return x if zero is None else x + zero.reshape(-1)[0].astype(x.dtype)


def _exchange_copies(x_refs, land_refs, send_sems, recv_sems, scatter):
    n = len(x_refs)
    n_peer = N_DEV - 1
    mx, my, mc = lax.axis_index("x"), lax.axis_index("y"), lax.axis_index("c")
    me = 4 * mx + 2 * my + mc
    pairs = []
    for d in range(1, N_DEV):
        px = 1 - mx if d & 4 else mx
        py = 1 - my if d & 2 else my
        pc = 1 - mc if d & 1 else mc
        peer = 4 * px + 2 * py + pc
        for a in range(n):
            def copy(src_slot, dst_slot, a=a, d=d, to=(px, py, pc)):
                return pltpu.make_async_remote_copy(
                    src_ref=x_refs[a].at[src_slot] if scatter else x_refs[a], dst_ref=land_refs[a].at[dst_slot],
                    send_sem=send_sems.at[a * n_peer + d - 1], recv_sem=recv_sems.at[a * n_peer + d - 1],
                    device_id=to, device_id_type=pl.DeviceIdType.MESH)

            pairs.append((copy(peer, me), copy(me, peer)))
    return me, pairs


def _exchange_async(name, xs, scatter, collective_id):
    n = len(xs)
    shapes = [tuple(x.shape[1:] if scatter else x.shape) for x in xs]
    x_refs = [jax.new_ref(x, memory_space=pltpu.MemorySpace.HBM) for x in xs]
    land_refs = [jax.empty_ref(jax.ShapeDtypeStruct((N_DEV,) + sh, x.dtype), memory_space=pltpu.MemorySpace.HBM)
                 for sh, x in zip(shapes, xs)]

    @pl.kernel(mesh=plsc.ScalarSubcoreMesh(axis_name="sequencer", num_cores=1), name=name,
               scratch_types=(pltpu.SemaphoreType.DMA((n * (N_DEV - 1),)), pltpu.SemaphoreType.DMA((n * (N_DEV - 1),)),
                              pltpu.SemaphoreType.DMA((n,))),
               compiler_params=pltpu.CompilerParams(collective_id=collective_id))
    def launch(send_sems, recv_sems, own_sems):
        barrier = pltpu.get_barrier_semaphore()
        mx, my, mc = lax.axis_index("x"), lax.axis_index("y"), lax.axis_index("c")
        for d in range(1, N_DEV):
            peer = (1 - mx if d & 4 else mx, 1 - my if d & 2 else my, 1 - mc if d & 1 else mc)
            pl.semaphore_signal(barrier, inc=1, device_id=peer, device_id_type=pl.DeviceIdType.MESH)
        pl.semaphore_wait(barrier, N_DEV - 1)
        me, pairs = _exchange_copies(x_refs, land_refs, send_sems, recv_sems, scatter)
        own = [pltpu.make_async_copy(x_refs[a].at[me] if scatter else x_refs[a], land_refs[a].at[me], own_sems.at[a])
               for a in range(n)]
        for cp in own:
            cp.start()
        for out, _ in pairs:
            out.start()
        for out, arrival in pairs:
            arrival.wait_recv()
            out.wait_send()
        for cp in own:
            cp.wait()

    launch()
    return lambda: [r[...] for r in land_refs]


def _relu2(a):
    r = jnp.maximum(a, 0.0)
    return r * r


def _relu2_grad(acc, r):
    return acc * (2.0 * jnp.sqrt(r.astype(F32)))


def _local_step(x0, tgt, mod, wcat, late_weights, send_grads, conv_w, conv_b, dt_bias, a_log, d_skip, ssm_norm_w, f_bias,
                attn_norm_w, ln1_g, ln1_b, ln2_g, ln2_b):
    ff_w = DFF // N_DEV
    s = x0.shape[0]
    tm = min(1024, s)
    ts = min(1024, s)
    sh1, sc1, g1, sh2, sc2, g2 = [mod[:, i * D:(i + 1) * D] for i in range(6)]
    zero = jnp.zeros((1, 128 - 2 * NH), F32)
    bias128 = jnp.concatenate([dt_bias, f_bias, zero], axis=1)
    alog128 = jnp.concatenate([a_log, jnp.zeros((1, 128 - NH), F32)], axis=1)
    dskip_x = jnp.repeat(d_skip, HD, axis=1)
    w_xs, w_bc, b_xs, b_bc = conv_w[:, :D], conv_w[:, D:], conv_b[:, :D], conv_b[:, D:]

    h1, = _rowk("modulate", lambda pos, x, sc, sh: (_modulate(x, sc, sh),), s, 512, [(x0, D, 0, 0)], [sc1, sh1], [(D, BF16)], [])
    p = _mm_nn("in_proj", h1, wcat, tm=tm, tn=1152, tk=D, out_dtype=F32)
    xs_a, bc_a = _conv_fwd(p, w_xs, b_xs, w_bc, b_bc, s)
    y_ssd, states = _ssd_fwd(xs_a, bc_a, p, bias128, alog128, dskip_x, s)
    cum = _cum_fwd(p, bias128, s)
    att, lse = _attn_fwd(p, cum, s)
    wout, w1s, w2 = late_weights()
    ymix = _mix_norm(y_ssd, p, att, ssm_norm_w, attn_norm_w, s)
    y = _mm_nn("out_proj", ymix, wout, tm=tm, tn=1024, tk=2 * D, out_dtype=F32)
    x1, h2 = _ln1(x0, y, g1, ln1_g, ln1_b, sc2, sh2, s)
    tall = min(2048, s)
    r = _mm_nn("ff_in", h2, w1s, tm=tall, tn=ff_w, tk=D, out_dtype=BF16, epi=_relu2)
    ff = _mm_nn("ff_out", r, w2, tm=tall, tn=1024, tk=1024, out_dtype=F32)
    du2, dff, sq_err, d_ln2_g, d_ln2_b, d_g2 = _ln2_loss(x1, ff, tgt, g2, ln2_g, ln2_b, s)

    da1 = _mm_nt("d_ff_hidden", [(dff, D, 0)], [(w2, D, 0)], n=DFF, tm=tall, tn=1024, out_dtype=BF16, epi=_relu2_grad,
                 epi_aux=(r,))
    d_w2 = _mm_tn("d_w_ff_out", r, dff, tm=1024, tn=1024, ts=ts)
    d_w1s = _mm_tn("d_w_ff_in", h2, da1, tm=1024, tn=ff_w, ts=ts, col_shards=True)
    dh2 = _mm_nt("d_ff_input", [(da1, ff_w, k) for k in range(N_DEV)], [(w1s, ff_w, k) for k in range(N_DEV)], n=D,
                 tm=min(512, s), tn=1024, out_dtype=F32)
    du1, dy, d_sc2, d_sh2, d_ln1_g, d_ln1_b, d_g1 = _ln1_bwd(dh2, du2, x0, y, g1, ln1_g, ln1_b, sc2, s)

    dmix = _mm_nt("d_mix", [(dy, D, 0)], [(wout, D, 0)], n=2 * D, tm=tm, tn=1024, out_dtype=F32)
    d_wout = _mm_tn("d_w_out", ymix, dy, tm=1024, tn=1024, ts=ts)
    sent = send_grads("late", [d_w1s, d_w2.reshape(N_DEV, -1, D), d_wout.reshape(N_DEV, -1, D)])
    dy_ssd, dz, datt, d_ssm_w, d_attn_w = _mix_norm_bwd(dmix, y_ssd, p, att, _after(ssm_norm_w, sent), attn_norm_w, s)
    dq, dk, dv, dcs, drs = _attn_bwd(p, cum, att, lse, datt, s)
    dxs_a, dbc_a, ddt_raw, d_alog, d_dskip = _ssd_bwd(dy_ssd, xs_a, bc_a, p, states, bias128, alog128, dskip_x, s)
    dcum = jnp.pad((drs - dcs)[:, :2, :].reshape(NH, s).T, ((0, 0), (NH, 128 - 2 * NH)))
    ddtf, _, d_bias = _cum_bwd(dcum, ddt_raw, p, bias128, s)
    dxs, dbc, d_wc_xs, d_bc_xs, d_wc_bc, d_bc_bc = _conv_bwd(dxs_a, dbc_a, p, w_xs, b_xs, w_bc, b_bc, s)

    segs = [(dz, OFF_Z, D), (dxs, OFF_XS, D), (dq, OFF_Q, D), (dk, OFF_K, D), (dv, OFF_V, D), (dbc, OFF_BC, 512),
            (ddtf, OFF_DTF, 128)]
    d_z, d_xs, d_q, d_k, d_v, d_bcw, d_dtf = [
        _mm_tn("d_w_in_%d" % i, h1, a, tm=1024, tn=min(w, 1024), ts=ts)
        for i, (a, _, w) in enumerate(segs)]
    d_w_in = dict(z=d_z, xs=d_xs, bc=d_bcw, dt=d_dtf[:, :NH], q=d_q, k=d_k, v=d_v, f=d_dtf[:, NH:2 * NH])
    sent = send_grads("in", [_shard_w_in_grad(d_w_in)])
    segs[-1] = (_after(ddtf, sent), OFF_DTF, 128)
    dh1 = _mm_nt("d_h1", [(a, w, 0) for a, _, w in segs], [(wcat, w, off // w) for _, off, w in segs], n=D,
                 tm=min(512, s), tn=1024, out_dtype=F32)
    grad_x, d_sc1, d_sh1 = _input_grad(dh1, du1, x0, sc1, s)

    return dict(
        loss=(0.5 / D) * jnp.sum(sq_err), grad_x=grad_x,
        d_mod=jnp.concatenate([d_sh1, d_sc1, d_g1, d_sh2, d_sc2, d_g2], axis=1),
        d_conv_w=jnp.concatenate([d_wc_xs[:4], d_wc_bc[:4]], axis=1), d_conv_b=jnp.concatenate([d_bc_xs, d_bc_bc], axis=1),
        d_ssm_norm_w=d_ssm_w, d_attn_norm_w=d_attn_w, d_ln1_g=d_ln1_g, d_ln1_b=d_ln1_b, d_ln2_g=d_ln2_g, d_ln2_b=d_ln2_b,
        d_gate_bias=d_bias, d_a_log=d_alog, d_d_skip=d_dskip)


W_IN_SEGS = [('z', W_Z, D), ('xs', W_XS, D), ('bc', W_BC, 512), ('dt', W_DT, NH), ('q', W_Q, D), ('k', W_K, D),
             ('v', W_V, D), ('f', W_F, NH)]
SHARD_W = IN_COLS // N_DEV


def _pack_w_in(shards):
    def cols(lo, hi):
        pieces = []
        while lo < hi:
            dev = lo // SHARD_W
            end = min(hi, (dev + 1) * SHARD_W)
            pieces.append(shards[dev][:, lo - dev * SHARD_W:end - dev * SHARD_W])
            lo = end
        return pieces

    seg = {n: cols(off, off + w) for n, off, w in W_IN_SEGS}
    pieces = seg['z'] + seg['xs'] + seg['q'] + seg['k'] + seg['v'] + seg['bc'] + seg['dt'] + seg['f']
    return jnp.concatenate(pieces + [jnp.zeros((D, 128 - 2 * NH), shards.dtype)], axis=1)


def _shard_w_in_grad(d_w_in):
    blocks = []
    for dev in range(N_DEV):
        lo, hi = dev * SHARD_W, (dev + 1) * SHARD_W
        pieces = [d_w_in[n][:, max(lo, off) - off:min(hi, off + w) - off] for n, off, w in W_IN_SEGS
                  if max(lo, off) < min(hi, off + w)]
        pieces.append(jnp.zeros((D, -SHARD_W % 128), pieces[0].dtype))
        blocks.append(jnp.concatenate(pieces, axis=1))
    return jnp.stack(blocks, axis=0)


WEIGHTS = ['w_ada', 'b_ada', 'w_in', 'conv_w', 'conv_b', 'dt_bias', 'a_log', 'd_skip', 'ssm_norm_w', 'f_bias',
           'attn_norm_w', 'w_out', 'ln1_g', 'ln1_b', 'w_ff_in', 'w_ff_out', 'ln2_g', 'ln2_b']
BIG = ['w_in', 'w_out', 'w_ff_in', 'w_ff_out']
SMALL_LAYOUT = [('b_ada', 0, 6 * D), ('conv_b', 12288, 1536), ('ssm_norm_w', 13824, D), ('attn_norm_w', 14848, D),
                ('ln1_g', 15872, D), ('ln1_b', 16896, D), ('ln2_g', 17920, D), ('ln2_b', 18944, D),
                ('dt_bias', 19968, NH), ('f_bias', 19968 + NH, NH), ('a_log', 20096, NH), ('d_skip', 20224, NH)]
SMALL_LOSS_LANE = 20352


def _pad_lanes(v, n=128):
    return jnp.pad(v, ((0, 0), (0, n - v.shape[1])))


def kernel(x, c, w_ada, b_ada, w_in, conv_w, conv_b, dt_bias, a_log, d_skip, ssm_norm_w, f_bias, attn_norm_w, w_out, ln1_g, ln1_b, w_ff_in, w_ff_out, ln2_g, ln2_b, loss_target, m_w_ada, m_b_ada, m_w_in, m_conv_w, m_conv_b, m_dt_bias, m_a_log, m_d_skip, m_ssm_norm_w, m_f_bias, m_attn_norm_w, m_w_out, m_ln1_g, m_ln1_b, m_w_ff_in, m_w_ff_out, m_ln2_g, m_ln2_b, v_w_ada, v_b_ada, v_w_in, v_conv_w, v_conv_b, v_dt_bias, v_a_log, v_d_skip, v_ssm_norm_w, v_f_bias, v_attn_norm_w, v_w_out, v_ln1_g, v_ln1_b, v_w_ff_in, v_w_ff_out, v_ln2_g, v_ln2_b):
    args = dict(locals())
    w = {n: args[n] for n in WEIGHTS}
    m = {n: args['m_' + n] for n in WEIGHTS}
    v = {n: args['v_' + n] for n in WEIGHTS}
    me = 4 * lax.axis_index("x") + 2 * lax.axis_index("y") + lax.axis_index("c")
    ada_cols = 6 * D // N_DEV
    conv_cols = conv_w.shape[2]

    c_all, conv_all = _exchange("gather_cond", [c, conv_w[0]], False)
    c_all = c_all.reshape(N_DEV, D)
    conv_w_full = conv_all.transpose(1, 0, 2).reshape(4, N_DEV * conv_cols)
    b_shard = lax.dynamic_slice(b_ada, (0, me * ada_cols), (1, ada_cols))
    mod_all, = _exchange("gather_mod", [_ada_mod(c_all, w_ada[0], b_shard)], False)
    mod = lax.dynamic_index_in_dim(mod_all, me, axis=1, keepdims=False).reshape(1, 6 * D)

    win_s = _gather_two_level("gather_w_in", _after(w_in[0].astype(BF16), mod * 0))
    first_done = win_s[0, 0:1, 0:1] * 0
    rest = _exchange_async("gather_rest", [_after(w[n][0].astype(BF16), first_done) for n in BIG[1:]], False, 1)

    def late_weights():
        wout_s, w1s, w2_s = rest()
        return wout_s.reshape(2 * D, D), w1s, w2_s.reshape(DFF, D)

    sends = {}

    def send_grads(tag, blocks):
        sends[tag] = _exchange_async("scatter_" + tag, blocks, True, {'late': 2, 'in': 3}[tag])
        return sum(b.reshape(-1)[0].astype(F32) * 0 for b in blocks)

    out = _local_step(x[0], loss_target[0], mod, _pack_w_in(win_s), late_weights, send_grads,
                      conv_w_full, conv_b, dt_bias, a_log, d_skip, ssm_norm_w, f_bias, attn_norm_w, ln1_g, ln1_b, ln2_g, ln2_b)

    small = jnp.concatenate(
        [out['d_mod'], out['d_conv_w'].reshape(1, -1), out['d_conv_b'], out['d_ssm_norm_w'], out['d_attn_norm_w'],
         out['d_ln1_g'], out['d_ln1_b'], out['d_ln2_g'], out['d_ln2_b'], out['d_gate_bias'], out['d_a_log'],
         out['d_d_skip'], _pad_lanes(out['loss'].reshape(1, 1))], axis=1)
    small_landed = _exchange_async("gather_small", [small], False, 4)
    (g_ff_in, g_ff_out, g_out), (g_in,) = sends['late'](), sends['in']()
    g_parts = dict(w_ff_in=g_ff_in, w_ff_out=g_ff_out, w_out=g_out, w_in=g_in)
    big = {n: _adamw("adamw_" + n, w[n][0], g_parts[n], m[n][0], v[n][0], tr=256, slots=True) for n in BIG}
    big_done = sum(big[n][1][0:1, 0:1] * 0 for n in BIG)
    small_all = _after(small_landed()[0], big_done)
    ssum, small_res = _small_update(small_all, SMALL_LAYOUT, w, m, v)
    dmod_all = small_all[:, 0, :6 * D]
    g_w_ada = _ada_grad(c_all, lax.dynamic_slice(dmod_all, (0, me * ada_cols), (N_DEV, ada_cols)))
    ada = _adamw("adamw_ada", w_ada[0], g_w_ada, m_w_ada[0], v_w_ada[0], tr=256, slots=False)
    g_conv_w = lax.dynamic_slice(ssum[:, 6 * D:6 * D + 4 * N_DEV * conv_cols].reshape(4, N_DEV * conv_cols),
                                 (0, me * conv_cols), (4, conv_cols))
    conv = _adamw("adamw_conv_w", conv_w[0], g_conv_w, m_conv_w[0], v_conv_w[0], tr=4, slots=False)

    results = []
    for k in range(4):
        vals = {n: small_res[n][k] for n in small_res}
        vals['w_ada'], vals['conv_w'] = ada[k][None], conv[k][None]
        for n in BIG:
            vals[n] = big[n][k][None]
        results.append(vals)
    return (ssum[0, SMALL_LOSS_LANE], out['grad_x'][None], *[res[n] for res in results for n in WEIGHTS])
```

```python
import functools

import jax
import jax.numpy as jnp
from jax import lax
from jax.experimental import pallas as pl
from jax.experimental.pallas import tpu as pltpu
from jax.experimental.pallas import tpu_sc as plsc

F32, BF16 = jnp.float32, jnp.bfloat16

N_DEV = 8
D = 1024
NH, HD = 16, 64
NSTATE = 128
CHUNK = 128
HG = 8
DFF = 4096
ALPHA = 2.0 ** 0.25
EPS = 1e-5
ATT_SCALE = HD ** -0.5

OFF_Z, OFF_XS, OFF_Q, OFF_K, OFF_V, OFF_BC, OFF_DTF = 0, 1024, 2048, 3072, 4096, 5120, 5632
PCOLS = 5760
W_Z, W_XS, W_BC, W_DT, W_Q, W_K, W_V, W_F = 0, 1024, 2048, 2560, 2576, 3600, 4624, 5648
IN_COLS = 5664

ADAM_LR, ADAM_B1, ADAM_B2, ADAM_EPS, ADAM_WD, ADAM_STEP = 0.001, 0.9, 0.999, 1e-08, 0.01, 10

VMEM_LIMIT = 56 << 20

NN = (((1,), (0,)), ((), ()))
NT = (((1,), (1,)), ((), ()))
TN = (((0,), (0,)), ((), ()))


def _dot(a, b, dims=NN):
    return lax.dot_general(a, b, dims, preferred_element_type=F32)


def _bdot(a, b, dims=NN):
    return _dot(a.astype(BF16), b.astype(BF16), dims)


def _split3(v, terms=3):
    parts, rest = [], v
    for _ in range(terms):
        p = rest.astype(BF16)
        parts.append(p)
        rest = rest - p.astype(F32)
    return parts


def _sel_left(m01, v):
    return sum(_dot(m01, p) for p in _split3(v))


def _sel_right(v, m01, dims=NN, terms=3):
    return sum(_dot(p, m01, dims) for p in _split3(v, terms))


def _iota(shape, dim):
    return lax.broadcasted_iota(jnp.int32, shape, dim)


def _tri_lower(n):
    return (_iota((n, n), 1) <= _iota((n, n), 0)).astype(BF16)


def _tri_upper(n):
    return (_iota((n, n), 1) >= _iota((n, n), 0)).astype(BF16)


def _head_expand():
    return (lax.shift_right_logical(_iota((128, D), 1), 6) == _iota((128, D), 0)).astype(BF16)


def _head_reduce():
    return (lax.shift_right_logical(_iota((D, 128), 0), 6) == _iota((D, 128), 1)).astype(BF16)


def _sigmoid(x):
    return 1.0 / (1.0 + jnp.exp(-x))


def _silu(x):
    return x * _sigmoid(x)


def _dsilu(x):
    s = _sigmoid(x)
    return s * (1.0 + x * (1.0 - s))


def _softplus(x):
    return jnp.maximum(x, 0.0) + jnp.log(1.0 + jnp.exp(-jnp.abs(x)))


def _log_sigmoid(x):
    return jnp.minimum(x, 0.0) - jnp.log(1.0 + jnp.exp(-jnp.abs(x)))


def _params(sem):
    return pltpu.CompilerParams(dimension_semantics=sem, vmem_limit_bytes=VMEM_LIMIT)


def _mm_nn(name, a, b, *, tm, tn, tk, out_dtype, pro=None, aux=(), epi=None):
    m, k_all = a.shape
    b_sharded = b.ndim == 3
    n = b.shape[0] * b.shape[2] if b_sharded else b.shape[1]
    assert not b_sharded or tn == b.shape[2]
    nk = k_all // tk
    n_aux = len(aux)
    b_spec = (pl.BlockSpec((None, tk, tn), lambda i, j, k: (j, k, 0)) if b_sharded
              else pl.BlockSpec((tk, tn), lambda i, j, k: (k, j)))

    def body(a_ref, b_ref, *rest):
        aux_refs, o_ref = rest[:n_aux], rest[n_aux]
        at = a_ref[...]
        if pro is not None:
            at = pro(at, *[r[...] for r in aux_refs])
        part = _bdot(at, b_ref[...])
        if nk == 1:
            o_ref[...] = (part if epi is None else epi(part)).astype(out_dtype)
            return
        assert epi is None
        acc_ref = rest[n_aux + 1]
        kk = pl.program_id(2)

        @pl.when(kk == 0)
        def _():
            acc_ref[...] = part

        @pl.when(kk > 0)
        def _():
            acc_ref[...] += part

        @pl.when(kk == nk - 1)
        def _():
            o_ref[...] = acc_ref[...].astype(out_dtype)

    return pl.pallas_call(
        body, name=name,
        grid=(m // tm, n // tn, nk),
        in_specs=[pl.BlockSpec((tm, tk), lambda i, j, k: (i, k)), b_spec]
        + [pl.BlockSpec((1, tk), lambda i, j, k: (0, k)) for _ in aux],
        out_specs=pl.BlockSpec((tm, tn), lambda i, j, k: (i, j)),
        out_shape=jax.ShapeDtypeStruct((m, n), out_dtype),
        scratch_shapes=[] if nk == 1 else [pltpu.VMEM((tm, tn), F32)],
        compiler_params=_params(("parallel", "parallel", "arbitrary")),
    )(a, b, *aux)


def _mm_nt(name, a_list, b_list, *, n, tm, tn, out_dtype, epi=None, epi_aux=(), b_rows=False):
    m = a_list[0][0].shape[0]
    n_op = len(a_list)
    n_epi = len(epi_aux)
    dims = NN if b_rows else NT

    def body(*refs):
        a_refs, b_refs = refs[:n_op], refs[n_op:2 * n_op]
        e_refs, o_ref = refs[2 * n_op:2 * n_op + n_epi], refs[2 * n_op + n_epi]
        acc = None
        for a_ref, b_ref in zip(a_refs, b_refs):
            part = _bdot(a_ref[...], b_ref[...], dims)
            acc = part if acc is None else acc + part
        if epi is not None:
            acc = epi(acc, *[r[...] for r in e_refs])
        o_ref[...] = acc.astype(out_dtype)

    in_specs = [pl.BlockSpec((tm, w), functools.partial(lambda i, j, cb: (i, cb), cb=cb)) for (_, w, cb) in a_list]
    for (b, w, cb) in b_list:
        if b_rows:
            in_specs.append(pl.BlockSpec((w, tn), functools.partial(lambda i, j, cb: (cb, j), cb=cb)))
        elif b.ndim == 3:
            in_specs.append(pl.BlockSpec((None, tn, w), functools.partial(lambda i, j, cb: (cb, j, 0), cb=cb)))
        else:
            in_specs.append(pl.BlockSpec((tn, w), functools.partial(lambda i, j, cb: (j, cb), cb=cb)))
    in_specs += [pl.BlockSpec((tm, tn), lambda i, j: (i, j)) for _ in epi_aux]
    return pl.pallas_call(
        body, name=name,
        grid=(m // tm, n // tn),
        in_specs=in_specs,
        out_specs=pl.BlockSpec((tm, tn), lambda i, j: (i, j)),
        out_shape=jax.ShapeDtypeStruct((m, n), out_dtype),
        compiler_params=_params(("parallel", "parallel")),
    )(*[a for (a, _, _) in a_list], *[b for (b, _, _) in b_list], *epi_aux)


def _mm_tn(name, a, b, *, tm, tn, ts, pro=None, aux=(), col_shards=False):
    s_all, ka = a.shape
    nb = b.shape[1]
    n_aux = len(aux)
    ns = s_all // ts
    assert not col_shards or tn == nb // N_DEV

    def body(a_ref, b_ref, *rest):
        aux_refs, o_ref, acc_ref = rest[:n_aux], rest[n_aux], rest[n_aux + 1]
        at = a_ref[...]
        if pro is not None:
            at = pro(at, *[r[...] for r in aux_refs])
        part = _bdot(at, b_ref[...], TN)
        ss = pl.program_id(2)

        @pl.when(ss == 0)
        def _():
            acc_ref[...] = part

        @pl.when(ss > 0)
        def _():
            acc_ref[...] += part

        @pl.when(ss == ns - 1)
        def _():
            o_ref[...] = acc_ref[...].astype(BF16)

    if col_shards:
        out_spec = pl.BlockSpec((None, tm, tn), lambda i, j, s: (j, i, 0))
        out_shape = jax.ShapeDtypeStruct((N_DEV, ka, tn), BF16)
    else:
        out_spec = pl.BlockSpec((tm, tn), lambda i, j, s: (i, j))
        out_shape = jax.ShapeDtypeStruct((ka, nb), BF16)
    return pl.pallas_call(
        body, name=name,
        grid=(ka // tm, nb // tn, ns),
        in_specs=[pl.BlockSpec((ts, tm), lambda i, j, s: (s, i)),
                  pl.BlockSpec((ts, tn), lambda i, j, s: (s, j))]
        + [pl.BlockSpec((1, tm), lambda i, j, s: (0, i)) for _ in aux],
        out_specs=out_spec, out_shape=out_shape,
        scratch_shapes=[pltpu.VMEM((tm, tn), F32)],
        compiler_params=_params(("parallel", "parallel", "arbitrary")),
    )(a, b, *aux)


def _rowk(name, fn, n_rows, tr, rows, fulls, outs, accs, reverse=False):
    n = n_rows // tr
    n_row, n_full, n_out, n_acc = len(rows), len(fulls), len(outs), len(accs)

    def pos(i):
        return (n - 1 - i) if reverse else i

    def body(*refs):
        row_refs = refs[:n_row]
        full_refs = refs[n_row:n_row + n_full]
        out_refs = refs[n_row + n_full:n_row + n_full + n_out]
        acc_refs = refs[n_row + n_full + n_out:]
        i = pl.program_id(0)

        @pl.when(i == 0)
        def _():
            for r in acc_refs:
                r[...] = jnp.zeros(r.shape, r.dtype)

        res = fn(pos(i), *[r[...] for r in row_refs], *[r[...] for r in full_refs], *[r[...] for r in acc_refs])
        for r, v in zip(out_refs + acc_refs, res):
            r[...] = v.astype(r.dtype)

    def row_map(i, cb, shift):
        return (jnp.clip(pos(i) + shift, 0, n - 1), cb)

    def halo_map(i, cb, shift):
        tile = jnp.clip(pos(i) + shift, 0, n - 1)
        return (tile * (tr // 8) + (tr // 8 - 1 if shift < 0 else 0), cb)

    in_specs = [pl.BlockSpec((tr, w), functools.partial(row_map, cb=cb, shift=sh)) if sh == 0 else
                pl.BlockSpec((8, w), functools.partial(halo_map, cb=cb, shift=sh)) for (_, w, cb, sh) in rows]
    in_specs += [pl.BlockSpec(f.shape, functools.partial(lambda i, nd: (0,) * nd, nd=f.ndim)) for f in fulls]
    out_specs = [pl.BlockSpec((tr, w), lambda i: (pos(i), 0)) for (w, _) in outs]
    out_specs += [pl.BlockSpec((r, w), lambda i: (0, 0)) for (r, w) in accs]
    out_shape = [jax.ShapeDtypeStruct((n_rows, w), dt) for (w, dt) in outs]
    out_shape += [jax.ShapeDtypeStruct((r, w), F32) for (r, w) in accs]
    return pl.pallas_call(
        body, name=name, grid=(n,), in_specs=in_specs, out_specs=out_specs, out_shape=out_shape,
        compiler_params=_params(("arbitrary",)),
    )(*[a for (a, _, _, _) in rows], *fulls)


def _colsum(x):
    return jnp.sum(x, axis=0, keepdims=True)


def _mean(x):
    return jnp.mean(x, axis=-1, keepdims=True)


def _modulate(x, sc, sh):
    return x * (1.0 + sc) + sh


def _shift_down(cur, prev8, j):
    tr = cur.shape[0]
    row8 = _iota(prev8.shape, 0)
    head = jnp.where(row8 < j, pltpu.roll(prev8, j, 0), pltpu.roll(cur[0:8], j, 0))
    return head if tr == 8 else jnp.concatenate([head, pltpu.roll(cur, j, 0)[8:]], axis=0)


def _shift_up(cur, next8, j):
    tr = cur.shape[0]
    row8 = _iota(next8.shape, 0)
    tail = jnp.where(row8 < 8 - j, pltpu.roll(cur[tr - 8:], 8 - j, 0), pltpu.roll(next8, 8 - j, 0))
    return jnp.concatenate([pltpu.roll(cur, tr - j, 0)[:tr - 8], tail], axis=0)


def _conv(cur, prev, w, b):
    out = cur * w[3:4] + b
    for j in (1, 2, 3):
        out = out + _shift_down(cur, prev, j) * w[3 - j:4 - j]
    return out


def _conv_fwd(p, w_xs, b_xs, w_bc, b_bc, s):
    def fn(pos, xs, xs_prev, bc, bc_prev, w_xs, b_xs, w_bc, b_bc):
        first = pos == 0
        xs_prev = jnp.where(first, 0.0, xs_prev)
        bc_prev = jnp.where(first, 0.0, bc_prev)
        return _silu(_conv(xs, xs_prev, w_xs, b_xs)), _silu(_conv(bc, bc_prev, w_bc, b_bc))

    return _rowk("conv_fwd", fn, s, 256,
                 [(p, D, OFF_XS // D, 0), (p, D, OFF_XS // D, -1), (p, 512, OFF_BC // 512, 0), (p, 512, OFF_BC // 512, -1)],
                 [w_xs, b_xs, w_bc, b_bc], [(D, F32), (512, F32)], [])


def _conv_bwd(dxs_a, dbc_a, p, w_xs, b_xs, w_bc, b_bc, s):
    tr = 256
    n = s // tr

    def fn(pos, da1, da1n, x1, x1p, x1n, da2, da2n, x2, x2p, x2n, w1, b1, w2, b2, aw1, ab1, aw2, ab2):
        dx1, dw1, db1 = _conv_bwd_fn(pos, n, da1, da1n, x1, x1p, x1n, w1, b1)
        dx2, dw2, db2 = _conv_bwd_fn(pos, n, da2, da2n, x2, x2p, x2n, w2, b2)
        return dx1, dx2, aw1 + dw1, ab1 + db1, aw2 + dw2, ab2 + db2

    cx, cb = OFF_XS // D, OFF_BC // 512
    return _rowk("conv_bwd", fn, s, tr,
                 [(dxs_a, D, 0, 0), (dxs_a, D, 0, 1), (p, D, cx, 0), (p, D, cx, -1), (p, D, cx, 1),
                  (dbc_a, 512, 0, 0), (dbc_a, 512, 0, 1), (p, 512, cb, 0), (p, 512, cb, -1), (p, 512, cb, 1)],
                 [w_xs, b_xs, w_bc, b_bc], [(D, BF16), (512, BF16)], [(8, D), (1, D), (8, 512), (1, 512)])


def _conv_bwd_fn(pos, n, da, da_next, x, x_prev, x_next, w, b):
    first, last = pos == 0, pos == n - 1
    x_prev = jnp.where(first, 0.0, x_prev)
    shifted = {j: _shift_down(x, x_prev, j) for j in (1, 2, 3)}
    conv = x * w[3:4] + b
    for j in (1, 2, 3):
        conv = conv + shifted[j] * w[3 - j:4 - j]
    dc = da * _dsilu(conv)
    dc_next = jnp.where(last, 0.0, da_next * _dsilu(_conv(x_next, x[x.shape[0] - 8:], w, b)))
    dx = dc * w[3:4]
    dws = [None] * 4
    dws[3] = _colsum(dc * x)
    for j in (1, 2, 3):
        dx = dx + _shift_up(dc, dc_next, j) * w[3 - j:4 - j]
        dws[3 - j] = _colsum(dc * shifted[j])
    row = _iota((8, x.shape[1]), 0)
    dw = jnp.zeros((8, x.shape[1]), F32)
    for k in range(4):
        dw = jnp.where(row == k, dws[k], dw)
    return dx, dw, _colsum(dc)


def _ssd_gates(dtf, bias, a_log):
    lane = _iota(dtf.shape, 1)
    head = lane < NH
    dt = jnp.where(head, _softplus(dtf + bias), 0.0)
    a_neg = jnp.where(_iota(a_log.shape, 1) < NH, -jnp.exp(a_log), 0.0)
    a = dt * a_neg
    cs = _sel_left(_tri_lower(CHUNK), a)
    return dt, a_neg, cs


def _decay_mask(cs_ref, cst_ref, h):
    diff = cs_ref[:, h:h + 1] - cst_ref[h:h + 1, :]
    low = _iota((CHUNK, CHUNK), 1) <= _iota((CHUNK, CHUNK), 0)
    return jnp.where(low, jnp.exp(jnp.minimum(diff, 0.0)), 0.0)


def _ssd_fwd(xs_a, bc_a, p, bias128, alog128, dskip_x, s):
    nc = s // CHUNK
    t = CHUNK

    def body(xs_ref, bc_ref, dtf_ref, bias_ref, alog_ref, dsk_ref, y_ref, st_ref,
             state, x_sc, xw_sc, cs_sc, cst_sc, yd_sc):
        c = pl.program_id(0)

        @pl.when(c == 0)
        def _():
            state[...] = jnp.zeros(state.shape, F32)

        dt, _, cs = _ssd_gates(dtf_ref[...], bias_ref[...], alog_ref[...])
        cs_sc[...] = cs
        cst_sc[...] = cs.T
        cs_last = cs[t - 1:t, :]
        expand = _head_expand()
        ex = _sel_right(jnp.concatenate([dt, jnp.exp(cs), jnp.exp(cs_last - cs)], axis=0), expand, terms=2)
        dt_x, eo_x, we_x = ex[0:t], ex[t:2 * t], ex[2 * t:3 * t]
        g_x = _sel_right(jnp.broadcast_to(jnp.exp(cs_last), (8, 128)), expand)[0:1]
        xs = xs_ref[...]
        x = xs * dt_x
        x_sc[...] = x.astype(BF16)
        xw_sc[...] = (x * we_x).astype(BF16)
        prev = state[...]
        st_ref[0] = prev
        prev_b = prev.astype(BF16)
        for g in range(2):
            cols = slice(g * 512, (g + 1) * 512)
            b_g = bc_ref[:, g * 128:(g + 1) * 128].astype(BF16)
            c_g = bc_ref[:, 256 + g * 128:256 + (g + 1) * 128].astype(BF16)
            gmat = _dot(c_g, b_g, NT)
            y_off = _dot(c_g, prev_b[:, cols]) * eo_x[:, cols]
            s_loc = _dot(b_g, xw_sc[:, cols], TN)
            state[:, cols] = g_x[:, cols] * prev[:, cols] + s_loc
            for e in range(HG):
                h = g * HG + e
                m = gmat * _decay_mask(cs_sc, cst_sc, h)
                yd_sc[:, h * HD:(h + 1) * HD] = _dot(m.astype(BF16), x_sc[:, h * HD:(h + 1) * HD])
            y_ref[:, cols] = yd_sc[:, cols] + y_off + dsk_ref[:, cols] * xs[:, cols]

    return pl.pallas_call(
        body, name="ssd_fwd", grid=(nc,),
        in_specs=[pl.BlockSpec((t, D), lambda c: (c, 0)),
                  pl.BlockSpec((t, 512), lambda c: (c, 0)),
                  pl.BlockSpec((t, 128), lambda c: (c, OFF_DTF // 128)),
                  pl.BlockSpec((1, 128), lambda c: (0, 0)),
                  pl.BlockSpec((1, 128), lambda c: (0, 0)),
                  pl.BlockSpec((1, D), lambda c: (0, 0))],
        out_specs=[pl.BlockSpec((t, D), lambda c: (c, 0)),
                   pl.BlockSpec((1, NSTATE, D), lambda c: (c, 0, 0))],
        out_shape=[jax.ShapeDtypeStruct((s, D), F32), jax.ShapeDtypeStruct((nc, NSTATE, D), F32)],
        scratch_shapes=[pltpu.VMEM((NSTATE, D), F32), pltpu.VMEM((t, D), BF16), pltpu.VMEM((t, D), BF16),
                        pltpu.VMEM((t, 128), F32), pltpu.VMEM((128, t), F32), pltpu.VMEM((t, D), F32)],
        compiler_params=_params(("arbitrary",)),
    )(xs_a, bc_a, p, bias128, alog128, dskip_x)


def _ssd_bwd(dy, xs_a, bc_a, p, states, bias128, alog128, dskip_x, s):
    nc = s // CHUNK
    t = CHUNK

    def body(dy_ref, xs_ref, bc_ref, dtf_ref, st_ref, bias_ref, alog_ref, dsk_ref,
             dxs_ref, dbc_ref, ddt_ref, dalog_ref, dskip_ref,
             dstate, x_sc, dy_sc, dx_sc, deo_sc, dwe_sc, cs_sc, cst_sc, dcol_sc, drow_sc):
        i = pl.program_id(0)

        @pl.when(i == 0)
        def _():
            dstate[...] = jnp.zeros(dstate.shape, F32)
            dalog_ref[...] = jnp.zeros(dalog_ref.shape, F32)
            dskip_ref[...] = jnp.zeros(dskip_ref.shape, F32)

        dtf = dtf_ref[...]
        dt, a_neg, cs = _ssd_gates(dtf, bias_ref[...], alog_ref[...])
        cs_sc[...] = cs
        cst_sc[...] = cs.T
        cs_last = cs[t - 1:t, :]
        eo, we, g_end = jnp.exp(cs), jnp.exp(cs_last - cs), jnp.exp(cs_last)
        expand, reduce = _head_expand(), _head_reduce()
        ex = _sel_right(jnp.concatenate([dt, eo, we], axis=0), expand, terms=2)
        dt_x, eo_x, we_x = ex[0:t], ex[t:2 * t], ex[2 * t:3 * t]
        g_x = _sel_right(jnp.broadcast_to(g_end, (8, 128)), expand)[0:1]
        xs = xs_ref[...]
        dyv = dy_ref[...]
        x = xs * dt_x
        x_sc[...] = x.astype(BF16)
        dy_sc[...] = dyv.astype(BF16)
        dyo_b = (dyv * eo_x).astype(BF16)
        xw_b = (x * we_x).astype(BF16)
        prev = st_ref[0]
        prev_b = prev.astype(BF16)
        dnext = dstate[...]
        dnext_b = dnext.astype(BF16)
        dcol_sc[...] = jnp.zeros(dcol_sc.shape, F32)
        drow_sc[...] = jnp.zeros(drow_sc.shape, F32)
        lane_row = _iota((1, 128), 1)
        sub_col = _iota((128, 1), 0)
        for g in range(2):
            cols = slice(g * 512, (g + 1) * 512)
            b_g = bc_ref[:, g * 128:(g + 1) * 128].astype(BF16)
            c_g = bc_ref[:, 256 + g * 128:256 + (g + 1) * 128].astype(BF16)
            gmat = _dot(c_g, b_g, NT)
            b_ds = _dot(b_g, dnext_b[:, cols])
            c_s = _dot(c_g, prev_b[:, cols])
            dx_sc[:, cols] = b_ds * we_x[:, cols]
            deo_sc[:, cols] = dyv[:, cols] * c_s
            dwe_sc[:, cols] = b_ds * x[:, cols]
            db = _dot(xw_b[:, cols], dnext_b[:, cols], NT)
            dc = _dot(dyo_b[:, cols], prev_b[:, cols], NT)
            dstate[:, cols] = g_x[:, cols] * dnext[:, cols] + _dot(c_g, dyo_b[:, cols], TN)
            dg = jnp.zeros((t, t), F32)
            for e in range(HG):
                h = g * HG + e
                hc = slice(h * HD, (h + 1) * HD)
                lmat = _decay_mask(cs_sc, cst_sc, h)
                m = gmat * lmat
                dx_sc[:, hc] += _dot(m.astype(BF16), dy_sc[:, hc], TN)
                dm = _dot(dy_sc[:, hc], x_sc[:, hc], NT)
                dg = dg + dm * lmat
                qm = dm * m
                dcol_sc[...] += jnp.sum(qm, axis=1, keepdims=True) * (lane_row == h).astype(F32)
                drow_sc[...] += (sub_col == h).astype(F32) * jnp.sum(qm, axis=0, keepdims=True)
            dg_b = dg.astype(BF16)
            dbc_ref[:, g * 128:(g + 1) * 128] = db + _dot(dg_b, c_g, TN)
            dbc_ref[:, 256 + g * 128:256 + (g + 1) * 128] = dc + _dot(dg_b, b_g)
        d_eo = _sel_right(deo_sc[...], reduce, terms=2)
        d_we = _sel_right(dwe_sc[...], reduce, terms=2)
        d_gend = _sel_right(jnp.broadcast_to(_colsum(dnext * prev), (8, D)), reduce)[0:1]
        d_cs = dcol_sc[...] - drow_sc[...].T + d_eo * eo - d_we * we
        extra = _colsum(d_we * we) + d_gend * g_end
        d_cs = d_cs + jnp.where(_iota((t, 128), 0) == t - 1, extra, 0.0)
        da = _sel_left(_tri_upper(t), d_cs)
        dx = dx_sc[...]
        ddt = _sel_right(dx * xs, reduce, terms=2) + da * a_neg
        dxs_ref[...] = dx * dt_x + dsk_ref[...] * dyv
        ddt_ref[...] = jnp.where(_iota((t, 128), 1) < NH, ddt * _sigmoid(dtf + bias_ref[...]), 0.0)
        dalog_ref[...] += _colsum(da * dt) * a_neg
        dskip_ref[...] += _sel_right(jnp.broadcast_to(_colsum(dyv * xs), (8, D)), reduce)[0:1]

    rev = lambda i: nc - 1 - i
    return pl.pallas_call(
        body, name="ssd_bwd", grid=(nc,),
        in_specs=[pl.BlockSpec((t, D), lambda i: (rev(i), 0)),
                  pl.BlockSpec((t, D), lambda i: (rev(i), 0)),
                  pl.BlockSpec((t, 512), lambda i: (rev(i), 0)),
                  pl.BlockSpec((t, 128), lambda i: (rev(i), OFF_DTF // 128)),
                  pl.BlockSpec((1, NSTATE, D), lambda i: (rev(i), 0, 0)),
                  pl.BlockSpec((1, 128), lambda i: (0, 0)),
                  pl.BlockSpec((1, 128), lambda i: (0, 0)),
                  pl.BlockSpec((1, D), lambda i: (0, 0))],
        out_specs=[pl.BlockSpec((t, D), lambda i: (rev(i), 0)),
                   pl.BlockSpec((t, 512), lambda i: (rev(i), 0)),
                   pl.BlockSpec((t, 128), lambda i: (rev(i), 0)),
                   pl.BlockSpec((1, 128), lambda i: (0, 0)),
                   pl.BlockSpec((1, 128), lambda i: (0, 0))],
        out_shape=[jax.ShapeDtypeStruct((s, D), F32), jax.ShapeDtypeStruct((s, 512), F32),
                   jax.ShapeDtypeStruct((s, 128), F32), jax.ShapeDtypeStruct((1, 128), F32),
                   jax.ShapeDtypeStruct((1, 128), F32)],
        scratch_shapes=[pltpu.VMEM((NSTATE, D), F32), pltpu.VMEM((t, D), BF16), pltpu.VMEM((t, D), BF16),
                        pltpu.VMEM((t, D), F32), pltpu.VMEM((t, D), F32), pltpu.VMEM((t, D), F32),
                        pltpu.VMEM((t, 128), F32), pltpu.VMEM((128, t), F32),
                        pltpu.VMEM((t, 128), F32), pltpu.VMEM((128, t), F32)],
        compiler_params=_params(("arbitrary",)),
    )(dy, xs_a, bc_a, p, states, bias128, alog128, dskip_x)


def _gate_lanes(shape):
    lane = _iota(shape, 1)
    return (lane >= NH) & (lane < 2 * NH)


def _cum_fwd(p, bias128, s):
    tr = min(512, s)

    def body(dtf_ref, bias_ref, o_ref, carry):
        @pl.when(pl.program_id(0) == 0)
        def _():
            carry[...] = jnp.zeros(carry.shape, F32)

        lf = jnp.where(_gate_lanes((tr, 128)), _log_sigmoid(dtf_ref[...] + bias_ref[...]), 0.0)
        cum = _sel_left(_tri_lower(tr), lf) + carry[...]
        carry[...] = cum[tr - 1:tr, :]
        o_ref[...] = cum

    return pl.pallas_call(
        body, name="cum_fwd", grid=(s // tr,),
        in_specs=[pl.BlockSpec((tr, 128), lambda i: (i, OFF_DTF // 128)), pl.BlockSpec((1, 128), lambda i: (0, 0))],
        out_specs=pl.BlockSpec((tr, 128), lambda i: (i, 0)),
        out_shape=jax.ShapeDtypeStruct((s, 128), F32),
        scratch_shapes=[pltpu.VMEM((1, 128), F32)],
        compiler_params=_params(("arbitrary",)),
    )(p, bias128)


def _cum_bwd(dcum, ddt_raw, p, bias128, s):
    tr = min(512, s)

    def fn(pos, dcum, ddt, dtf, bias, carry, acc):
        suffix = _sel_left(_tri_upper(tr), dcum) + carry
        dfr = jnp.where(_gate_lanes((tr, 128)), suffix * _sigmoid(-(dtf + bias)), 0.0)
        out = ddt + dfr
        return out, suffix[0:1, :], acc + _colsum(out)

    return _rowk("cum_bwd", fn, s, tr, [(dcum, 128, 0, 0), (ddt_raw, 128, 0, 0), (p, 128, OFF_DTF // 128, 0)],
                 [bias128], [(128, BF16)], [(1, 128), (1, 128)], reverse=True)


ATT_BLOCK = 512
ATT_STRIP = 32


def _head_part(shape, h, dim):
    i = _iota(shape, dim)
    return (i >= h * HD) & (i < (h + 1) * HD)


def _k_augmented(k_blk, cum_blk, j, h):
    tk = k_blk.shape[0]
    lane = _iota((tk, 128), 1)
    col = jnp.sum(jnp.where(lane == NH + 2 * j + h, cum_blk, 0.0), axis=1, keepdims=True)
    c0, c1, c2 = [c.astype(F32) for c in _split3(-col)]
    k_h = k_blk if h == 0 else pltpu.roll(k_blk, HD, 1)
    aug = jnp.where(lane == HD, c0, jnp.where(lane == HD + 1, c1, jnp.where(lane == HD + 2, c2, 0.0)))
    return jnp.where(lane < HD, k_h, aug).astype(BF16)


def _q_augmented_t(q_blk):
    tq = q_blk.shape[0]
    q_t = (q_blk * ATT_SCALE).T.astype(BF16)
    ones = (_iota((HD, tq), 0) < 3).astype(BF16)
    return [jnp.concatenate([q_t[h * HD:(h + 1) * HD], ones], axis=0) for h in range(2)]


def _rows01(r0, r1):
    sub = _iota((8, r0.shape[1]), 0)
    return jnp.where(sub == 0, r0, jnp.where(sub == 1, r1, 0.0))


def _fold8(x, op, cur):
    for g in range(x.shape[0] // 8):
        cur = op(cur, x[8 * g:8 * (g + 1), :])
    return cur


def _attn_fwd(p, cum, s):
    t = min(ATT_BLOCK, s)
    nq = s // t
    r = ATT_STRIP

    def body(q_ref, k_ref, v_ref, c_ref, o_ref, lse_ref, kaug_sc, vt_sc, s0_sc, s1_sc, p0_sc, p1_sc, m_sc, l_sc, acc_sc):
        j, qi = pl.program_id(0), pl.program_id(1)
        s_sc, p_sc = (s0_sc, s1_sc), (p0_sc, p1_sc)

        @pl.when(qi == 0)
        def _():
            for c in range(nq):
                rows = slice(c * t, (c + 1) * t)
                k_blk, vt = k_ref[rows, :], v_ref[rows, :].T
                for h in range(2):
                    kaug_sc[h, rows, :] = _k_augmented(k_blk, c_ref[rows, :], j, h)
                    vt_sc[h, :, rows] = vt[h * HD:(h + 1) * HD].astype(BF16)

        qaug_t = _q_augmented_t(q_ref[...])
        m_sc[...] = jnp.full(m_sc.shape, -1e30, F32)
        l_sc[...] = jnp.zeros(l_sc.shape, F32)
        acc_sc[...] = jnp.zeros(acc_sc.shape, F32)
        top = _iota((128, t), 0) < HD

        def logits(kb, buf):
            kv = pl.ds(pl.multiple_of(kb * t, t), t)
            for h in range(2):
                s_sc[buf][h] = _dot(kaug_sc[h, kv, :], qaug_t[h])

        def softmax(buf, diagonal):
            alphas = []
            for h in range(2):
                cur = jnp.full((8, t), -1e30, F32)
                for i in range(t // r):
                    rows = slice(i * r, (i + 1) * r)
                    x = s_sc[buf][h, rows, :]
                    if diagonal:
                        x = jnp.where(_iota((r, t), 1) >= i * r + _iota((r, t), 0), x, -1e30)
                        s_sc[buf][h, rows, :] = x
                    cur = _fold8(x, jnp.maximum, cur)
                m_prev = m_sc[h, 0:1, :]
                m_new = jnp.maximum(m_prev, jnp.max(cur, axis=0, keepdims=True))
                alpha = jnp.exp(m_prev - m_new)
                m_sc[h, 0:1, :] = m_new
                alphas.append(alpha)
                tot = jnp.zeros((8, t), F32)
                for i in range(t // r):
                    rows = slice(i * r, (i + 1) * r)
                    pr = jnp.exp(s_sc[buf][h, rows, :] - m_new)
                    p_sc[buf][h, rows, :] = pr.astype(BF16)
                    tot = _fold8(pr, jnp.add, tot)
                l_sc[h, 0:1, :] = alpha * l_sc[h, 0:1, :] + jnp.sum(tot, axis=0, keepdims=True)
            return alphas

        def accumulate(kb, buf, alphas):
            kv = pl.ds(pl.multiple_of(kb * t, t), t)
            for h in range(2):
                part = slice(h * HD, (h + 1) * HD)
                acc_sc[part, :] = acc_sc[part, :] * alphas[h] + _dot(vt_sc[h, :, kv], p_sc[buf][h])

        def first_trip():
            logits(0, 1)
            accumulate(qi, 0, softmax(0, True))
            logits(jnp.minimum(1, qi - 1), 0)
            return tuple(softmax(1, False))

        def only_diagonal():
            accumulate(qi, 0, softmax(0, True))
            return (jnp.ones((1, t), F32),) * 2

        def steady(u, alphas_b):
            accumulate(2 * u - 2, 1, alphas_b)
            logits(2 * u, 1)
            accumulate(2 * u - 1, 0, softmax(0, False))
            logits(jnp.minimum(2 * u + 1, qi - 1), 0)
            return tuple(softmax(1, False))

        logits(qi, 0)
        n_blocks = qi + 1
        alphas_b = lax.cond(qi >= 1, first_trip, only_diagonal)
        alphas_b = lax.fori_loop(1, n_blocks // 2, steady, alphas_b)
        last_b = 2 * (n_blocks // 2) - 2

        @pl.when((qi >= 1) & (n_blocks % 2 == 0))
        def _():
            accumulate(last_b, 1, alphas_b)

        @pl.when((qi >= 2) & (n_blocks % 2 == 1))
        def _():
            accumulate(last_b, 1, alphas_b)
            accumulate(qi - 1, 0, softmax(0, False))

        l0, l1 = l_sc[0, 0:1, :], l_sc[1, 0:1, :]
        o_ref[...] = (acc_sc[...] / jnp.where(top, l0, l1)).T
        lse_ref[0] = _rows01(m_sc[0, 0:1, :] + jnp.log(l0), m_sc[1, 0:1, :] + jnp.log(l1))

    return pl.pallas_call(
        body, name="attn_fwd", grid=(NH // 2, nq),
        in_specs=[pl.BlockSpec((t, 128), lambda j, qi: (qi, OFF_Q // 128 + j)),
                  pl.BlockSpec((s, 128), lambda j, qi: (0, OFF_K // 128 + j)),
                  pl.BlockSpec((s, 128), lambda j, qi: (0, OFF_V // 128 + j)),
                  pl.BlockSpec((s, 128), lambda j, qi: (0, 0))],
        out_specs=[pl.BlockSpec((t, 128), lambda j, qi: (qi, j)),
                   pl.BlockSpec((1, 8, t), lambda j, qi: (j, 0, qi))],
        out_shape=[jax.ShapeDtypeStruct((s, D), F32), jax.ShapeDtypeStruct((NH // 2, 8, s), F32)],
        scratch_shapes=[pltpu.VMEM((2, s, 128), BF16), pltpu.VMEM((2, HD, s), BF16), pltpu.VMEM((2, t, t), F32),
                        pltpu.VMEM((2, t, t), F32), pltpu.VMEM((2, t, t), BF16), pltpu.VMEM((2, t, t), BF16),
                        pltpu.VMEM((2, 8, t), F32), pltpu.VMEM((2, 8, t), F32), pltpu.VMEM((128, t), F32)],
        compiler_params=_params(("parallel", "arbitrary")),
    )(p, p, p, cum)


def _attn_bwd(p, cum, o, lse, do, s):
    t = min(ATT_BLOCK, s)
    nq = s // t
    r = ATT_STRIP

    def body(q_ref, k_ref, v_ref, c_ref, o_ref, lse_ref, do_ref, dq_ref, dk_ref, dv_ref, dc_ref, dr_ref,
             qaugt_sc, qh_sc, dot_sc, doh_sc, delta_sc, dqt_sc, dr_sc, kaug_sc, vh_sc, kt_sc,
             s0_sc, s1_sc, dp0_sc, dp1_sc, p0_sc, p1_sc, ds0_sc, ds1_sc, dk_sc, dv_sc, dc_sc):
        j, ki = pl.program_id(0), pl.program_id(1)
        s_sc, dp_sc, p_sc, ds_sc = (s0_sc, s1_sc), (dp0_sc, dp1_sc), (p0_sc, p1_sc), (ds0_sc, ds1_sc)

        @pl.when(ki == 0)
        def _():
            for c in range(nq):
                rows = slice(c * t, (c + 1) * t)
                q_blk, do_blk = q_ref[rows, :], do_ref[rows, :]
                qaugt_sc[0, :, rows], qaugt_sc[1, :, rows] = _q_augmented_t(q_blk)
                dot_sc[:, rows] = do_blk.T.astype(BF16)
                prod_t = (do_blk * o_ref[rows, :]).T
                delta_sc[:, rows] = _rows01(jnp.sum(prod_t[0:HD], axis=0, keepdims=True),
                                            jnp.sum(prod_t[HD:], axis=0, keepdims=True))
                for h in range(2):
                    head = _head_part((t, 128), h, 1)
                    qh_sc[h, rows, :] = jnp.where(head, q_blk * ATT_SCALE, 0.0).astype(BF16)
                    doh_sc[h, rows, :] = jnp.where(head, do_blk, 0.0).astype(BF16)
            dqt_sc[...] = jnp.zeros(dqt_sc.shape, F32)
            dr_sc[...] = jnp.zeros(dr_sc.shape, F32)

        k_blk, v_blk = k_ref[...], v_ref[...]
        kt = k_blk.T
        for h in range(2):
            kaug_sc[h] = _k_augmented(k_blk, c_ref[...], j, h)
            vh_sc[h] = jnp.where(_head_part((t, 128), h, 1), v_blk, 0.0).astype(BF16)
            kt_sc[h] = kt[h * HD:(h + 1) * HD].astype(BF16)
        dk_sc[...] = jnp.zeros(dk_sc.shape, F32)
        dv_sc[...] = jnp.zeros(dv_sc.shape, F32)
        dc_sc[...] = jnp.zeros(dc_sc.shape, F32)

        def inputs(qb, buf):
            qs = pl.ds(pl.multiple_of(qb * t, t), t)
            for h in range(2):
                s_sc[buf][h] = _dot(kaug_sc[h], qaugt_sc[h, :, qs])
                dp_sc[buf][h] = _dot(vh_sc[h], dot_sc[:, qs])

        def elementwise(qb, buf, diagonal):
            qs = pl.ds(pl.multiple_of(qb * t, t), t)
            for h in range(2):
                lse_row, delta_row = lse_ref[0, h:h + 1, qs], delta_sc[h:h + 1, qs]
                tot = jnp.zeros((8, t), F32)
                for i in range(t // r):
                    rows = slice(i * r, (i + 1) * r)
                    x = s_sc[buf][h, rows, :]
                    if diagonal:
                        x = jnp.where(_iota((r, t), 1) >= i * r + _iota((r, t), 0), x, -1e30)
                    pr = jnp.exp(x - lse_row)
                    ds = pr * (dp_sc[buf][h, rows, :] - delta_row)
                    p_sc[buf][h, rows, :] = pr.astype(BF16)
                    ds_sc[buf][h, rows, :] = ds.astype(BF16)
                    dc_sc[h, rows, :] += sum(ds[:, 128 * g:128 * (g + 1)] for g in range(t // 128))
                    tot = _fold8(ds, jnp.add, tot)
                dr_sc[h, :, qs] += tot

        def outputs(qb, buf):
            qs = pl.ds(pl.multiple_of(qb * t, t), t)
            dv_sc[...] += _dot(p_sc[buf][0], doh_sc[0, qs, :]) + _dot(p_sc[buf][1], doh_sc[1, qs, :])
            dk_sc[...] += _dot(ds_sc[buf][0], qh_sc[0, qs, :]) + _dot(ds_sc[buf][1], qh_sc[1, qs, :])
            for h in range(2):
                dqt_sc[h * HD:(h + 1) * HD, qs] += _dot(kt_sc[h], ds_sc[buf][h])

        def pair(a, b, a_diagonal):
            inputs(a, 0)
            inputs(b, 1)
            elementwise(a, 0, a_diagonal)
            outputs(a, 0)
            elementwise(b, 1, False)
            outputs(b, 1)

        def later(u, carry):
            pair(ki + 1 + 2 * u, ki + 2 + 2 * u, False)
            return carry

        n_later = nq - 1 - ki
        lax.fori_loop(0, n_later // 2, later, 0)

        @pl.when(n_later % 2 == 1)
        def _():
            pair(ki, nq - 1, True)

        @pl.when(n_later % 2 == 0)
        def _():
            inputs(ki, 0)
            elementwise(ki, 0, True)
            outputs(ki, 0)

        dk_ref[...] = dk_sc[...].astype(BF16)
        dv_ref[...] = dv_sc[...].astype(BF16)
        lane = _iota((t, 128), 1)
        cols = jnp.where(lane == 0, jnp.sum(dc_sc[0], axis=1, keepdims=True),
                         jnp.where(lane == 1, jnp.sum(dc_sc[1], axis=1, keepdims=True), 0.0))
        dc_ref[0] = cols.T[0:8, :]

        @pl.when(ki == nq - 1)
        def _():
            for c in range(nq):
                rows = slice(c * t, (c + 1) * t)
                dq_ref[rows, :] = dqt_sc[:, rows].T * ATT_SCALE
            dr_ref[0] = _rows01(jnp.sum(dr_sc[0], axis=0, keepdims=True), jnp.sum(dr_sc[1], axis=0, keepdims=True))

    whole = lambda off: pl.BlockSpec((s, 128), functools.partial(lambda j, ki, off: (0, off + j), off=off))
    return pl.pallas_call(
        body, name="attn_bwd", grid=(NH // 2, nq),
        in_specs=[whole(OFF_Q // 128),
                  pl.BlockSpec((t, 128), lambda j, ki: (ki, OFF_K // 128 + j)),
                  pl.BlockSpec((t, 128), lambda j, ki: (ki, OFF_V // 128 + j)),
                  pl.BlockSpec((t, 128), lambda j, ki: (ki, 0)),
                  whole(0),
                  pl.BlockSpec((1, 8, s), lambda j, ki: (j, 0, 0)),
                  whole(0)],
        out_specs=[whole(0),
                   pl.BlockSpec((t, 128), lambda j, ki: (ki, j)),
                   pl.BlockSpec((t, 128), lambda j, ki: (ki, j)),
                   pl.BlockSpec((1, 8, t), lambda j, ki: (j, 0, ki)),
                   pl.BlockSpec((1, 8, s), lambda j, ki: (j, 0, 0))],
        out_shape=[jax.ShapeDtypeStruct((s, D), F32), jax.ShapeDtypeStruct((s, D), BF16), jax.ShapeDtypeStruct((s, D), BF16),
                   jax.ShapeDtypeStruct((NH // 2, 8, s), F32), jax.ShapeDtypeStruct((NH // 2, 8, s), F32)],
        scratch_shapes=[pltpu.VMEM((2, 128, s), BF16), pltpu.VMEM((2, s, 128), BF16), pltpu.VMEM((128, s), BF16),
                        pltpu.VMEM((2, s, 128), BF16), pltpu.VMEM((8, s), F32), pltpu.VMEM((128, s), F32),
                        pltpu.VMEM((2, 8, s), F32), pltpu.VMEM((2, t, 128), BF16), pltpu.VMEM((2, t, 128), BF16),
                        pltpu.VMEM((2, HD, t), BF16)]
        + [pltpu.VMEM((2, t, t), F32)] * 4 + [pltpu.VMEM((2, t, t), BF16)] * 4
        + [pltpu.VMEM((t, 128), F32), pltpu.VMEM((t, 128), F32), pltpu.VMEM((2, t, 128), F32)],
        compiler_params=_params(("parallel", "arbitrary")),
    )(p, p, p, cum, o, lse, do)


def _ln_stats(u):
    mu = _mean(u)
    d = u - mu
    rstd = lax.rsqrt(_mean(d * d) + EPS)
    return d * rstd, rstd


def _ln_bwd(dx, xh, rstd, gam):
    dxh = dx * gam
    return rstd * (dxh - _mean(dxh) - xh * _mean(dxh * xh))


def _rms_bwd(d, xn, r, w):
    t = d * w
    return r * (t - xn * _mean(t * xn)), _colsum(d * xn)


def _mix_norm(y, p, att, w_ssm, w_att, s):
    def fn(pos, y, z, att, w1, w2):
        g = y * _silu(z)
        n1 = g * lax.rsqrt(_mean(g * g) + EPS) * w1
        n2 = att * lax.rsqrt(_mean(att * att) + EPS) * w2
        return (jnp.concatenate([n1, n2], axis=1),)

    return _rowk("mix_norm", fn, s, 512, [(y, D, 0, 0), (p, D, OFF_Z // D, 0), (att, D, 0, 0)],
                 [w_ssm, w_att], [(2 * D, BF16)], [])[0]


def _mix_norm_bwd(dmix, y, p, att, w_ssm, w_att, s):
    def fn(pos, dmix, y, z, att, w1, w2, a1, a2):
        sz = _silu(z)
        g = y * sz
        r1 = lax.rsqrt(_mean(g * g) + EPS)
        dg, dw1 = _rms_bwd(dmix[:, :D], g * r1, r1, w1)
        r2 = lax.rsqrt(_mean(att * att) + EPS)
        datt, dw2 = _rms_bwd(dmix[:, D:], att * r2, r2, w2)
        return dg * sz, dg * y * _dsilu(z), datt, a1 + dw1, a2 + dw2

    return _rowk("mix_norm_bwd", fn, s, 256, [(dmix, 2 * D, 0, 0), (y, D, 0, 0), (p, D, OFF_Z // D, 0), (att, D, 0, 0)],
                 [w_ssm, w_att], [(D, F32), (D, BF16), (D, F32)], [(1, D), (1, D)])


def _ln1(x0, y, g1, gam, bet, sc2, sh2, s):
    def fn(pos, x0, y, g1, gam, bet, sc2, sh2):
        xh, _ = _ln_stats(ALPHA * x0 + (1.0 + g1) * y)
        x1 = xh * gam + bet
        return x1, _modulate(x1, sc2, sh2)

    return _rowk("ln1", fn, s, 512, [(x0, D, 0, 0), (y, D, 0, 0)], [g1, gam, bet, sc2, sh2], [(D, F32), (D, BF16)], [])


def _ln2_loss(x1, ff, tgt, g2, gam, bet, s):
    def fn(pos, x1, ff, tgt, g2, gam, bet, a_loss, a_dgam, a_dbet, a_dg2):
        xh, rstd = _ln_stats(ALPHA * x1 + (1.0 + g2) * ff)
        err = xh * gam + bet - tgt
        dx2 = err * (1.0 / D)
        du = _ln_bwd(dx2, xh, rstd, gam)
        return (du, du * (1.0 + g2), a_loss + _colsum(err * err), a_dgam + _colsum(dx2 * xh),
                a_dbet + _colsum(dx2), a_dg2 + _colsum(du * ff))

    return _rowk("ln2_loss", fn, s, 512, [(x1, D, 0, 0), (ff, D, 0, 0), (tgt, D, 0, 0)], [g2, gam, bet],
                 [(D, F32), (D, BF16)], [(1, D)] * 4)


def _ln1_bwd(dh2, du2, x0, y, g1, gam, bet, sc2, s):
    def fn(pos, dh2, du2, x0, y, g1, gam, bet, sc2, a_sc, a_sh, a_gam, a_bet, a_g1):
        xh, rstd = _ln_stats(ALPHA * x0 + (1.0 + g1) * y)
        x1 = xh * gam + bet
        dx1 = ALPHA * du2 + dh2 * (1.0 + sc2)
        du1 = _ln_bwd(dx1, xh, rstd, gam)
        return (du1, du1 * (1.0 + g1), a_sc + _colsum(dh2 * x1), a_sh + _colsum(dh2), a_gam + _colsum(dx1 * xh),
                a_bet + _colsum(dx1), a_g1 + _colsum(du1 * y))

    return _rowk("ln1_bwd", fn, s, 512, [(dh2, D, 0, 0), (du2, D, 0, 0), (x0, D, 0, 0), (y, D, 0, 0)],
                 [g1, gam, bet, sc2], [(D, F32), (D, BF16)], [(1, D)] * 5)


def _input_grad(dh1, du1, x0, sc1, s):
    def fn(pos, dh1, du1, x0, sc1, a_sc, a_sh):
        return ALPHA * du1 + dh1 * (1.0 + sc1), a_sc + _colsum(dh1 * x0), a_sh + _colsum(dh1)

    return _rowk("input_grad", fn, s, 512, [(dh1, D, 0, 0), (du1, D, 0, 0), (x0, D, 0, 0)], [sc1],
                 [(D, F32)], [(1, D)] * 2)


def _adamw_math(w, grad, m, v):
    m_new = ADAM_B1 * m + (1.0 - ADAM_B1) * grad
    v_new = ADAM_B2 * v + (1.0 - ADAM_B2) * (grad * grad)
    m_hat = m_new / (1.0 - ADAM_B1 ** ADAM_STEP)
    v_hat = v_new / (1.0 - ADAM_B2 ** ADAM_STEP)
    return -ADAM_LR * (m_hat / (jnp.sqrt(v_hat) + ADAM_EPS) + ADAM_WD * w), m_new, v_new


def _small_update(small_all, layout, w, m, v):
    names = [n for n, _, _ in layout]

    def body(*refs):
        all_ref = refs[0]
        w_refs, m_refs, v_refs = [refs[1 + k * len(names):1 + (k + 1) * len(names)] for k in range(3)]
        sum_ref = refs[1 + 3 * len(names)]
        outs = refs[2 + 3 * len(names):]
        total = all_ref[0]
        for k in range(1, N_DEV):
            total = total + all_ref[k]
        sum_ref[...] = total
        for i, (_, off, size) in enumerate(layout):
            grad = total[:, off:off + size]
            delta, m_new, v_new = _adamw_math(w_refs[i][...], grad, m_refs[i][...], v_refs[i][...])
            for o, val in zip(outs[4 * i:4 * i + 4], (grad, delta, m_new, v_new)):
                o[...] = val

    res = pl.pallas_call(
        body, name="small_update",
        out_shape=[jax.ShapeDtypeStruct(small_all.shape[1:], F32)]
        + [jax.ShapeDtypeStruct(w[n].shape, F32) for n in names for _ in range(4)],
        compiler_params=_params(None),
    )(small_all, *[w[n] for n in names], *[m[n] for n in names], *[v[n] for n in names])
    return res[0], {n: res[1 + 4 * i:5 + 4 * i] for i, n in enumerate(names)}


def _adamw(name, w, g, m, v, *, tr, slots, by_columns=False):
    r, c = w.shape

    def body(w_ref, g_ref, m_ref, v_ref, g_out, d_out, m_out, v_out):
        if slots:
            grad = g_ref[0].astype(F32)
            for k in range(1, N_DEV):
                grad = grad + g_ref[k].astype(F32)
        else:
            grad = g_ref[...]
        g_out[...] = grad
        d_out[...], m_out[...], v_out[...] = _adamw_math(w_ref[...], grad, m_ref[...], v_ref[...])

    if by_columns:
        tile = pl.BlockSpec((r, tr), lambda i: (0, i))
        g_spec = pl.BlockSpec((N_DEV, r, tr), lambda i: (0, 0, i)) if slots else tile
    else:
        tile = pl.BlockSpec((tr, c), lambda i: (i, 0))
        g_spec = pl.BlockSpec((N_DEV, tr, c), lambda i: (0, i, 0)) if slots else tile
    return pl.pallas_call(
        body, name=name, grid=((c if by_columns else r) // tr,),
        in_specs=[tile, g_spec, tile, tile], out_specs=[tile] * 4,
        out_shape=[jax.ShapeDtypeStruct((r, c), F32)] * 4,
        compiler_params=_params(("parallel",)),
    )(w, g, m, v)


def _dot_f32(a, b, dims=NN):
    a0, a1, a2 = _split3(a)
    b0, b1, b2 = _split3(b)
    acc = _dot(a0, b0, dims)
    for x, y in ((a0, b1), (a1, b0), (a1, b1), (a0, b2), (a2, b0)):
        acc = acc + _dot(x, y, dims)
    return acc


def _ada_mod(c_all, w_shard, b_shard):
    def body(c_ref, w_ref, b_ref, o_ref):
        act = _silu(c_ref[...])
        act16 = jnp.concatenate([act, jnp.zeros_like(act)], axis=0)
        o_ref[...] = _dot_f32(act16, w_ref[...])[0:N_DEV] + b_ref[...]

    return pl.pallas_call(
        body, name="ada_mod", out_shape=jax.ShapeDtypeStruct((N_DEV, w_shard.shape[1]), F32),
        compiler_params=_params(None),
    )(c_all, w_shard, b_shard)


def _ada_grad(c_all, dmod_cols):
    def body(c_ref, dc_ref, gw_ref):
        act = _silu(c_ref[...])
        act16 = jnp.concatenate([act, jnp.zeros_like(act)], axis=0)
        dm = dc_ref[...]
        dm16 = jnp.concatenate([dm, jnp.zeros_like(dm)], axis=0)
        gw_ref[...] = _dot_f32(act16, dm16, TN)

    return pl.pallas_call(
        body, name="ada_grad", out_shape=jax.ShapeDtypeStruct((D, dmod_cols.shape[1]), F32),
        compiler_params=_params(None),
    )(c_all, dmod_cols)


def _exchange(name, xs, scatter):
    n = len(xs)
    n_peer = N_DEV - 1

    def body(*refs):
        x_refs, o_refs = refs[:n], refs[n:2 * n]
        send_sems, recv_sems, local_sems = refs[2 * n:]
        mx, my, mc = lax.axis_index("x"), lax.axis_index("y"), lax.axis_index("c")
        me = 4 * mx + 2 * my + mc

        def src(a, slot):
            return x_refs[a].at[slot] if scatter else x_refs[a]

        own = [pltpu.make_async_copy(src(a, me), o_refs[a].at[me], local_sems.at[a]) for a in range(n)]
        for cp in own:
            cp.start()
        sends = []
        for d in range(1, N_DEV):
            px = 1 - mx if d & 4 else mx
            py = 1 - my if d & 2 else my
            pc = 1 - mc if d & 1 else mc
            peer = 4 * px + 2 * py + pc
            for a in range(n):
                def copy(src_slot, dst_slot, a=a, d=d, to=(px, py, pc)):
                    return pltpu.make_async_remote_copy(
                        src_ref=src(a, src_slot), dst_ref=o_refs[a].at[dst_slot],
                        send_sem=send_sems.at[a * n_peer + d - 1], recv_sem=recv_sems.at[a * n_peer + d - 1],
                        device_id=to, device_id_type=pl.DeviceIdType.MESH)

                out = copy(peer, me)
                out.start()
                sends.append((out, copy(me, peer)))
        for _, arrival in sends:
            arrival.wait_recv()
        for out, _ in sends:
            out.wait_send()
        for cp in own:
            cp.wait()

    shapes = [tuple(x.shape[1:] if scatter else x.shape) for x in xs]
    return pl.pallas_call(
        body, name=name,
        in_specs=[pl.BlockSpec(memory_space=pl.ANY)] * n, out_specs=[pl.BlockSpec(memory_space=pl.ANY)] * n,
        out_shape=[jax.ShapeDtypeStruct((N_DEV,) + sh, x.dtype) for sh, x in zip(shapes, xs)],
        scratch_shapes=[pltpu.SemaphoreType.DMA((n * n_peer,)), pltpu.SemaphoreType.DMA((n * n_peer,)),
                        pltpu.SemaphoreType.DMA((n,))],
        compiler_params=pltpu.CompilerParams(has_side_effects=True),
    )(*xs)


def _gather_two_level(name, x):
    def body(x_ref, o_ref, send_sems, recv_sems, local_sem):
        mx, my, mc = lax.axis_index("x"), lax.axis_index("y"), lax.axis_index("c")
        me, sibling = (mx, my, mc), (mx, my, 1 - mc)
        chips = [(1 - mx, my), (mx, 1 - my), (1 - mx, 1 - my)]

        def slot(px, py, pc):
            return o_ref.at[4 * px + 2 * py + pc]

        def copy(k, block, to, src=None):
            return pltpu.make_async_remote_copy(
                src_ref=slot(*block) if src is None else src, dst_ref=slot(*block),
                send_sem=send_sems.at[k], recv_sem=recv_sems.at[k], device_id=to, device_id_type=pl.DeviceIdType.MESH)

        mine = pltpu.make_async_copy(x_ref, slot(*me), local_sem)
        mine.start()
        first = [copy(0, me, sibling, src=x_ref)] + [copy(1 + i, me, (*chip, mc), src=x_ref) for i, chip in enumerate(chips)]
        for cp in first:
            cp.start()
        passed = [copy(4 + i, (*chip, mc), sibling) for i, chip in enumerate(chips)]
        for i, chip in enumerate(chips):
            copy(1 + i, (*chip, mc), me).wait_recv()
            passed[i].start()
        copy(0, sibling, me).wait_recv()
        for i, chip in enumerate(chips):
            copy(4 + i, (*chip, 1 - mc), me).wait_recv()
        for cp in first + passed:
            cp.wait_send()
        mine.wait()

    return pl.pallas_call(
        body, name=name,
        in_specs=[pl.BlockSpec(memory_space=pl.ANY)], out_specs=pl.BlockSpec(memory_space=pl.ANY),
        out_shape=jax.ShapeDtypeStruct((N_DEV,) + tuple(x.shape), x.dtype),
        scratch_shapes=[pltpu.SemaphoreType.DMA((7,)), pltpu.SemaphoreType.DMA((7,)), pltpu.SemaphoreType.DMA(())],
        compiler_params=pltpu.CompilerParams(has_side_effects=True),
    )(x)


def _after(x, zero):
    return x if zero is None else x + zero.reshape(-1)[0].astype(x.dtype)


def _exchange_copies(x_refs, land_refs, send_sems, recv_sems, scatter):
    n = len(x_refs)
    n_peer = N_DEV - 1
    mx, my, mc = lax.axis_index("x"), lax.axis_index("y"), lax.axis_index("c")
    me = 4 * mx + 2 * my + mc
    pairs = []
    for d in range(1, N_DEV):
        px = 1 - mx if d & 4 else mx
        py = 1 - my if d & 2 else my
        pc = 1 - mc if d & 1 else mc
        peer = 4 * px + 2 * py + pc
        for a in range(n):
            def copy(src_slot, dst_slot, a=a, d=d, to=(px, py, pc)):
                return pltpu.make_async_remote_copy(
                    src_ref=x_refs[a].at[src_slot] if scatter else x_refs[a], dst_ref=land_refs[a].at[dst_slot],
                    send_sem=send_sems.at[a * n_peer + d - 1], recv_sem=recv_sems.at[a * n_peer + d - 1],
                    device_id=to, device_id_type=pl.DeviceIdType.MESH)

            pairs.append((copy(peer, me), copy(me, peer)))
    return me, pairs


def _exchange_async(name, xs, scatter, collective_id):
    n = len(xs)
    shapes = [tuple(x.shape[1:] if scatter else x.shape) for x in xs]
    x_refs = [jax.new_ref(x, memory_space=pltpu.MemorySpace.HBM) for x in xs]
    land_refs = [jax.empty_ref(jax.ShapeDtypeStruct((N_DEV,) + sh, x.dtype), memory_space=pltpu.MemorySpace.HBM)
                 for sh, x in zip(shapes, xs)]

    @pl.kernel(mesh=plsc.ScalarSubcoreMesh(axis_name="sequencer", num_cores=1), name=name,
               scratch_types=(pltpu.SemaphoreType.DMA((n * (N_DEV - 1),)), pltpu.SemaphoreType.DMA((n * (N_DEV - 1),)),
                              pltpu.SemaphoreType.DMA((n,))),
               compiler_params=pltpu.CompilerParams(collective_id=collective_id))
    def launch(send_sems, recv_sems, own_sems):
        barrier = pltpu.get_barrier_semaphore()
        mx, my, mc = lax.axis_index("x"), lax.axis_index("y"), lax.axis_index("c")
        for d in range(1, N_DEV):
            peer = (1 - mx if d & 4 else mx, 1 - my if d & 2 else my, 1 - mc if d & 1 else mc)
            pl.semaphore_signal(barrier, inc=1, device_id=peer, device_id_type=pl.DeviceIdType.MESH)
        pl.semaphore_wait(barrier, N_DEV - 1)
        me, pairs = _exchange_copies(x_refs, land_refs, send_sems, recv_sems, scatter)
        own = [pltpu.make_async_copy(x_refs[a].at[me] if scatter else x_refs[a], land_refs[a].at[me], own_sems.at[a])
               for a in range(n)]
        for cp in own:
            cp.start()
        for out, _ in pairs:
            out.start()
        for out, arrival in pairs:
            arrival.wait_recv()
            out.wait_send()
        for cp in own:
            cp.wait()

    launch()
    return lambda: [r[...] for r in land_refs]


def _relu2(a):
    r = jnp.maximum(a, 0.0)
    return r * r


def _relu2_grad(acc, r):
    return acc * (2.0 * jnp.sqrt(r.astype(F32)))


def _local_step(x0, tgt, mod, wcat_t, late_weights, send_grads, conv_w, conv_b, dt_bias, a_log, d_skip, ssm_norm_w, f_bias,
                attn_norm_w, ln1_g, ln1_b, ln2_g, ln2_b):
    ff_w = DFF // N_DEV
    s = x0.shape[0]
    tm = min(1024, s)
    ts = min(1024, s)
    sh1, sc1, g1, sh2, sc2, g2 = [mod[:, i * D:(i + 1) * D] for i in range(6)]
    zero = jnp.zeros((1, 128 - 2 * NH), F32)
    bias128 = jnp.concatenate([dt_bias, f_bias, zero], axis=1)
    alog128 = jnp.concatenate([a_log, jnp.zeros((1, 128 - NH), F32)], axis=1)
    dskip_x = jnp.repeat(d_skip, HD, axis=1)
    w_xs, w_bc, b_xs, b_bc = conv_w[:, :D], conv_w[:, D:], conv_b[:, :D], conv_b[:, D:]

    h1, = _rowk("modulate", lambda pos, x, sc, sh: (_modulate(x, sc, sh),), s, 512, [(x0, D, 0, 0)], [sc1, sh1], [(D, BF16)], [])
    p = _mm_nt("in_proj", [(h1, D, 0)], [(wcat_t, D, 0)], n=PCOLS, tm=tm, tn=1152, out_dtype=F32)
    xs_a, bc_a = _conv_fwd(p, w_xs, b_xs, w_bc, b_bc, s)
    y_ssd, states = _ssd_fwd(xs_a, bc_a, p, bias128, alog128, dskip_x, s)
    cum = _cum_fwd(p, bias128, s)
    att, lse = _attn_fwd(p, cum, s)
    wout, w1s, w2 = late_weights()
    ymix = _mix_norm(y_ssd, p, att, ssm_norm_w, attn_norm_w, s)
    y = _mm_nn("out_proj", ymix, wout, tm=tm, tn=1024, tk=2 * D, out_dtype=F32)
    x1, h2 = _ln1(x0, y, g1, ln1_g, ln1_b, sc2, sh2, s)
    tall = min(2048, s)
    r = _mm_nn("ff_in", h2, w1s, tm=tall, tn=ff_w, tk=D, out_dtype=BF16, epi=_relu2)
    ff = _mm_nn("ff_out", r, w2, tm=tall, tn=1024, tk=1024, out_dtype=F32)
    du2, dff, sq_err, d_ln2_g, d_ln2_b, d_g2 = _ln2_loss(x1, ff, tgt, g2, ln2_g, ln2_b, s)

    da1 = _mm_nt("d_ff_hidden", [(dff, D, 0)], [(w2, D, 0)], n=DFF, tm=tall, tn=1024, out_dtype=BF16, epi=_relu2_grad,
                 epi_aux=(r,))
    d_w2 = _mm_tn("d_w_ff_out", r, dff, tm=1024, tn=1024, ts=ts)
    d_w1s = _mm_tn("d_w_ff_in", h2, da1, tm=1024, tn=ff_w, ts=ts, col_shards=True)
    dh2 = _mm_nt("d_ff_input", [(da1, ff_w, k) for k in range(N_DEV)], [(w1s, ff_w, k) for k in range(N_DEV)], n=D,
                 tm=min(512, s), tn=1024, out_dtype=F32)
    du1, dy, d_sc2, d_sh2, d_ln1_g, d_ln1_b, d_g1 = _ln1_bwd(dh2, du2, x0, y, g1, ln1_g, ln1_b, sc2, s)

    dmix = _mm_nt("d_mix", [(dy, D, 0)], [(wout, D, 0)], n=2 * D, tm=tm, tn=1024, out_dtype=F32)
    d_wout = _mm_tn("d_w_out", ymix, dy, tm=1024, tn=1024, ts=ts)
    sent = send_grads("late", [d_w1s, d_w2.reshape(N_DEV, -1, D), d_wout.reshape(N_DEV, -1, D)])
    dy_ssd, dz, datt, d_ssm_w, d_attn_w = _mix_norm_bwd(dmix, y_ssd, p, att, _after(ssm_norm_w, sent), attn_norm_w, s)
    dq, dk, dv, dcs, drs = _attn_bwd(p, cum, att, lse, datt, s)
    dxs_a, dbc_a, ddt_raw, d_alog, d_dskip = _ssd_bwd(dy_ssd, xs_a, bc_a, p, states, bias128, alog128, dskip_x, s)
    dcum = jnp.pad((drs - dcs)[:, :2, :].reshape(NH, s).T, ((0, 0), (NH, 128 - 2 * NH)))
    ddtf, _, d_bias = _cum_bwd(dcum, ddt_raw, p, bias128, s)
    dxs, dbc, d_wc_xs, d_bc_xs, d_wc_bc, d_bc_bc = _conv_bwd(dxs_a, dbc_a, p, w_xs, b_xs, w_bc, b_bc, s)

    segs = [(dz, OFF_Z, D), (dxs, OFF_XS, D), (dq, OFF_Q, D), (dk, OFF_K, D), (dv, OFF_V, D), (dbc, OFF_BC, 512),
            (ddtf, OFF_DTF, 128)]
    d_z, d_xs, d_q, d_k, d_v, d_bcw, d_dtf = [
        _mm_tn("d_w_in_%d" % i, a, h1, tm=min(w, 1024), tn=1024, ts=ts)
        for i, (a, _, w) in enumerate(segs)]
    d_w_in_t = dict(z=d_z, xs=d_xs, bc=d_bcw, dt=d_dtf[:NH], q=d_q, k=d_k, v=d_v, f=d_dtf[NH:2 * NH])
    sent = send_grads("in", [_shard_w_in_grad_t(d_w_in_t)])
    segs[-1] = (_after(ddtf, sent), OFF_DTF, 128)
    dh1 = _mm_nt("d_h1", [(a, w, 0) for a, _, w in segs], [(wcat_t, w, off // w) for _, off, w in segs], n=D,
                 tm=min(512, s), tn=1024, out_dtype=F32, b_rows=True)
    grad_x, d_sc1, d_sh1 = _input_grad(dh1, du1, x0, sc1, s)

    return dict(
        loss=(0.5 / D) * jnp.sum(sq_err), grad_x=grad_x,
        d_mod=jnp.concatenate([d_sh1, d_sc1, d_g1, d_sh2, d_sc2, d_g2], axis=1),
        d_conv_w=jnp.concatenate([d_wc_xs[:4], d_wc_bc[:4]], axis=1), d_conv_b=jnp.concatenate([d_bc_xs, d_bc_bc], axis=1),
        d_ssm_norm_w=d_ssm_w, d_attn_norm_w=d_attn_w, d_ln1_g=d_ln1_g, d_ln1_b=d_ln1_b, d_ln2_g=d_ln2_g, d_ln2_b=d_ln2_b,
        d_gate_bias=d_bias, d_a_log=d_alog, d_d_skip=d_dskip)


W_IN_SEGS = [('z', W_Z, D), ('xs', W_XS, D), ('bc', W_BC, 512), ('dt', W_DT, NH), ('q', W_Q, D), ('k', W_K, D),
             ('v', W_V, D), ('f', W_F, NH)]
SHARD_W = IN_COLS // N_DEV


def _pack_w_in_t(w_in_t):
    seg = {n: w_in_t[off:off + w] for n, off, w in W_IN_SEGS}
    return jnp.concatenate([seg['z'], seg['xs'], seg['q'], seg['k'], seg['v'], seg['bc'], seg['dt'], seg['f'],
                            jnp.zeros((128 - 2 * NH, D), w_in_t.dtype)], axis=0)


def _shard_w_in_grad_t(d_w_in_t):
    blocks = []
    for dev in range(N_DEV):
        lo, hi = dev * SHARD_W, (dev + 1) * SHARD_W
        pieces = [d_w_in_t[n][max(lo, off) - off:min(hi, off + w) - off] for n, off, w in W_IN_SEGS
                  if max(lo, off) < min(hi, off + w)]
        blocks.append(jnp.concatenate(pieces, axis=0))
    return jnp.stack(blocks, axis=0)


WEIGHTS = ['w_ada', 'b_ada', 'w_in', 'conv_w', 'conv_b', 'dt_bias', 'a_log', 'd_skip', 'ssm_norm_w', 'f_bias',
           'attn_norm_w', 'w_out', 'ln1_g', 'ln1_b', 'w_ff_in', 'w_ff_out', 'ln2_g', 'ln2_b']
BIG = ['w_in', 'w_out', 'w_ff_in', 'w_ff_out']
SMALL_LAYOUT = [('b_ada', 0, 6 * D), ('conv_b', 12288, 1536), ('ssm_norm_w', 13824, D), ('attn_norm_w', 14848, D),
                ('ln1_g', 15872, D), ('ln1_b', 16896, D), ('ln2_g', 17920, D), ('ln2_b', 18944, D),
                ('dt_bias', 19968, NH), ('f_bias', 19968 + NH, NH), ('a_log', 20096, NH), ('d_skip', 20224, NH)]
SMALL_LOSS_LANE = 20352


def _pad_lanes(v, n=128):
    return jnp.pad(v, ((0, 0), (0, n - v.shape[1])))


def kernel(x, c, w_ada, b_ada, w_in, conv_w, conv_b, dt_bias, a_log, d_skip, ssm_norm_w, f_bias, attn_norm_w, w_out, ln1_g, ln1_b, w_ff_in, w_ff_out, ln2_g, ln2_b, loss_target, m_w_ada, m_b_ada, m_w_in, m_conv_w, m_conv_b, m_dt_bias, m_a_log, m_d_skip, m_ssm_norm_w, m_f_bias, m_attn_norm_w, m_w_out, m_ln1_g, m_ln1_b, m_w_ff_in, m_w_ff_out, m_ln2_g, m_ln2_b, v_w_ada, v_b_ada, v_w_in, v_conv_w, v_conv_b, v_dt_bias, v_a_log, v_d_skip, v_ssm_norm_w, v_f_bias, v_attn_norm_w, v_w_out, v_ln1_g, v_ln1_b, v_w_ff_in, v_w_ff_out, v_ln2_g, v_ln2_b):
    args = dict(locals())
    w = {n: args[n] for n in WEIGHTS}
    m = {n: args['m_' + n] for n in WEIGHTS}
    v = {n: args['v_' + n] for n in WEIGHTS}
    me = 4 * lax.axis_index("x") + 2 * lax.axis_index("y") + lax.axis_index("c")
    ada_cols = 6 * D // N_DEV
    conv_cols = conv_w.shape[2]

    c_all, conv_all = _exchange("gather_cond", [c, conv_w[0]], False)
    c_all = c_all.reshape(N_DEV, D)
    conv_w_full = conv_all.transpose(1, 0, 2).reshape(4, N_DEV * conv_cols)
    b_shard = lax.dynamic_slice(b_ada, (0, me * ada_cols), (1, ada_cols))
    mod_all, = _exchange("gather_mod", [_ada_mod(c_all, w_ada[0], b_shard)], False)
    mod = lax.dynamic_index_in_dim(mod_all, me, axis=1, keepdims=False).reshape(1, 6 * D)

    w_in_t = _after(jnp.swapaxes(w_in[0], 0, 1).astype(BF16), mod * 0)
    win_s = _gather_two_level("gather_w_in", w_in_t)
    first_done = win_s[0, 0:1, 0:1] * 0
    rest = _exchange_async("gather_rest", [_after(w[n][0].astype(BF16), first_done) for n in BIG[1:]], False, 1)

    def late_weights():
        wout_s, w1s, w2_s = rest()
        return wout_s.reshape(2 * D, D), w1s, w2_s.reshape(DFF, D)

    sends = {}

    def send_grads(tag, blocks):
        sends[tag] = _exchange_async("scatter_" + tag, blocks, True, {'late': 2, 'in': 3}[tag])
        return sum(b.reshape(-1)[0].astype(F32) * 0 for b in blocks)

    out = _local_step(x[0], loss_target[0], mod, _pack_w_in_t(win_s.reshape(IN_COLS, D)), late_weights, send_grads,
                      conv_w_full, conv_b, dt_bias, a_log, d_skip, ssm_norm_w, f_bias, attn_norm_w, ln1_g, ln1_b, ln2_g, ln2_b)

    small = jnp.concatenate(
        [out['d_mod'], out['d_conv_w'].reshape(1, -1), out['d_conv_b'], out['d_ssm_norm_w'], out['d_attn_norm_w'],
         out['d_ln1_g'], out['d_ln1_b'], out['d_ln2_g'], out['d_ln2_b'], out['d_gate_bias'], out['d_a_log'],
         out['d_d_skip'], _pad_lanes(out['loss'].reshape(1, 1))], axis=1)
    small_landed = _exchange_async("gather_small", [small], False, 4)
    (g_ff_in, g_ff_out, g_out), (g_in,) = sends['late'](), sends['in']()
    g_parts = dict(w_ff_in=g_ff_in, w_ff_out=g_ff_out, w_out=g_out, w_in=g_in)
    big = {n: _adamw("adamw_" + n, w[n][0], g_parts[n], m[n][0], v[n][0], tr=256, slots=True) for n in BIG[1:]}
    t = lambda a: jnp.swapaxes(a[0], 0, 1)
    big['w_in'] = [jnp.swapaxes(r, 0, 1) for r in _adamw("adamw_w_in", t(w_in), g_parts['w_in'], t(m_w_in), t(v_w_in),
                                                         tr=256, slots=True, by_columns=True)]
    big_done = sum(big[n][1][0:1, 0:1] * 0 for n in BIG)
    small_all = _after(small_landed()[0], big_done)
    ssum, small_res = _small_update(small_all, SMALL_LAYOUT, w, m, v)
    dmod_all = small_all[:, 0, :6 * D]
    g_w_ada = _ada_grad(c_all, lax.dynamic_slice(dmod_all, (0, me * ada_cols), (N_DEV, ada_cols)))
    ada = _adamw("adamw_ada", w_ada[0], g_w_ada, m_w_ada[0], v_w_ada[0], tr=256, slots=False)
    g_conv_w = lax.dynamic_slice(ssum[:, 6 * D:6 * D + 4 * N_DEV * conv_cols].reshape(4, N_DEV * conv_cols),
                                 (0, me * conv_cols), (4, conv_cols))
    conv = _adamw("adamw_conv_w", conv_w[0], g_conv_w, m_conv_w[0], v_conv_w[0], tr=4, slots=False)

    results = []
    for k in range(4):
        vals = {n: small_res[n][k] for n in small_res}
        vals['w_ada'], vals['conv_w'] = ada[k][None], conv[k][None]
        for n in BIG:
            vals[n] = big[n][k][None]
        results.append(vals)
    return (ssum[0, SMALL_LOSS_LANE], out['grad_x'][None], *[res[n] for res in results for n in WEIGHTS])
```

```python
import functools

import jax
import jax.numpy as jnp
from jax import lax
from jax.experimental import pallas as pl
from jax.experimental.pallas import tpu as pltpu
from jax.experimental.pallas import tpu_sc as plsc

F32, BF16 = jnp.float32, jnp.bfloat16

N_DEV = 8
D = 1024
NH, HD = 16, 64
NSTATE = 128
CHUNK = 128
HG = 8
DFF = 4096
ALPHA = 2.0 ** 0.25
EPS = 1e-5
ATT_SCALE = HD ** -0.5

OFF_Z, OFF_XS, OFF_Q, OFF_K, OFF_V, OFF_BC, OFF_DTF = 0, 1024, 2048, 3072, 4096, 5120, 5632
PCOLS = 5760
W_Z, W_XS, W_BC, W_DT, W_Q, W_K, W_V, W_F = 0, 1024, 2048, 2560, 2576, 3600, 4624, 5648
IN_COLS = 5664

ADAM_LR, ADAM_B1, ADAM_B2, ADAM_EPS, ADAM_WD, ADAM_STEP = 0.001, 0.9, 0.999, 1e-08, 0.01, 10

VMEM_LIMIT = 56 << 20

NN = (((1,), (0,)), ((), ()))
NT = (((1,), (1,)), ((), ()))
TN = (((0,), (0,)), ((), ()))


def _dot(a, b, dims=NN):
    return lax.dot_general(a, b, dims, preferred_element_type=F32)


def _bdot(a, b, dims=NN):
    return _dot(a.astype(BF16), b.astype(BF16), dims)


def _split3(v, terms=3):
    parts, rest = [], v
    for _ in range(terms):
        p = rest.astype(BF16)
        parts.append(p)
        rest = rest - p.astype(F32)
    return parts


def _sel_left(m01, v):
    return sum(_dot(m01, p) for p in _split3(v))


def _sel_right(v, m01, dims=NN, terms=3):
    return sum(_dot(p, m01, dims) for p in _split3(v, terms))


def _iota(shape, dim):
    return lax.broadcasted_iota(jnp.int32, shape, dim)


def _tri_lower(n):
    return (_iota((n, n), 1) <= _iota((n, n), 0)).astype(BF16)


def _tri_upper(n):
    return (_iota((n, n), 1) >= _iota((n, n), 0)).astype(BF16)


def _head_expand():
    return (lax.shift_right_logical(_iota((128, D), 1), 6) == _iota((128, D), 0)).astype(BF16)


def _head_reduce():
    return (lax.shift_right_logical(_iota((D, 128), 0), 6) == _iota((D, 128), 1)).astype(BF16)


def _sigmoid(x):
    return 1.0 / (1.0 + jnp.exp(-x))


def _silu(x):
    return x * _sigmoid(x)


def _dsilu(x):
    s = _sigmoid(x)
    return s * (1.0 + x * (1.0 - s))


def _softplus(x):
    return jnp.maximum(x, 0.0) + jnp.log(1.0 + jnp.exp(-jnp.abs(x)))


def _log_sigmoid(x):
    return jnp.minimum(x, 0.0) - jnp.log(1.0 + jnp.exp(-jnp.abs(x)))


def _params(sem):
    return pltpu.CompilerParams(dimension_semantics=sem, vmem_limit_bytes=VMEM_LIMIT)


def _mm_nn(name, a, b, *, tm, tn, tk, out_dtype, pro=None, aux=(), epi=None):
    m, k_all = a.shape
    b_sharded = b.ndim == 3
    n = b.shape[0] * b.shape[2] if b_sharded else b.shape[1]
    assert not b_sharded or tn == b.shape[2]
    nk = k_all // tk
    n_aux = len(aux)
    b_spec = (pl.BlockSpec((None, tk, tn), lambda i, j, k: (j, k, 0)) if b_sharded
              else pl.BlockSpec((tk, tn), lambda i, j, k: (k, j)))

    def body(a_ref, b_ref, *rest):
        aux_refs, o_ref = rest[:n_aux], rest[n_aux]
        at = a_ref[...]
        if pro is not None:
            at = pro(at, *[r[...] for r in aux_refs])
        part = _bdot(at, b_ref[...])
        if nk == 1:
            o_ref[...] = (part if epi is None else epi(part)).astype(out_dtype)
            return
        assert epi is None
        acc_ref = rest[n_aux + 1]
        kk = pl.program_id(2)

        @pl.when(kk == 0)
        def _():
            acc_ref[...] = part

        @pl.when(kk > 0)
        def _():
            acc_ref[...] += part

        @pl.when(kk == nk - 1)
        def _():
            o_ref[...] = acc_ref[...].astype(out_dtype)

    return pl.pallas_call(
        body, name=name,
        grid=(m // tm, n // tn, nk),
        in_specs=[pl.BlockSpec((tm, tk), lambda i, j, k: (i, k)), b_spec]
        + [pl.BlockSpec((1, tk), lambda i, j, k: (0, k)) for _ in aux],
        out_specs=pl.BlockSpec((tm, tn), lambda i, j, k: (i, j)),
        out_shape=jax.ShapeDtypeStruct((m, n), out_dtype),
        scratch_shapes=[] if nk == 1 else [pltpu.VMEM((tm, tn), F32)],
        compiler_params=_params(("parallel", "parallel", "arbitrary")),
    )(a, b, *aux)


def _mm_nt(name, a_list, b_list, *, n, tm, tn, out_dtype, epi=None, epi_aux=(), b_rows=False):
    m = a_list[0][0].shape[0]
    n_op = len(a_list)
    n_epi = len(epi_aux)
    dims = NN if b_rows else NT

    def body(*refs):
        a_refs, b_refs = refs[:n_op], refs[n_op:2 * n_op]
        e_refs, o_ref = refs[2 * n_op:2 * n_op + n_epi], refs[2 * n_op + n_epi]
        acc = None
        for a_ref, b_ref in zip(a_refs, b_refs):
            part = _bdot(a_ref[...], b_ref[...], dims)
            acc = part if acc is None else acc + part
        if epi is not None:
            acc = epi(acc, *[r[...] for r in e_refs])
        o_ref[...] = acc.astype(out_dtype)

    in_specs = [pl.BlockSpec((tm, w), functools.partial(lambda i, j, cb: (i, cb), cb=cb)) for (_, w, cb) in a_list]
    for (b, w, cb) in b_list:
        if b_rows:
            in_specs.append(pl.BlockSpec((w, tn), functools.partial(lambda i, j, cb: (cb, j), cb=cb)))
        elif b.ndim == 3:
            in_specs.append(pl.BlockSpec((None, tn, w), functools.partial(lambda i, j, cb: (cb, j, 0), cb=cb)))
        else:
            in_specs.append(pl.BlockSpec((tn, w), functools.partial(lambda i, j, cb: (j, cb), cb=cb)))
    in_specs += [pl.BlockSpec((tm, tn), lambda i, j: (i, j)) for _ in epi_aux]
    return pl.pallas_call(
        body, name=name,
        grid=(m // tm, n // tn),
        in_specs=in_specs,
        out_specs=pl.BlockSpec((tm, tn), lambda i, j: (i, j)),
        out_shape=jax.ShapeDtypeStruct((m, n), out_dtype),
        compiler_params=_params(("parallel", "parallel")),
    )(*[a for (a, _, _) in a_list], *[b for (b, _, _) in b_list], *epi_aux)


def _mm_tn(name, a, b, *, tm, tn, ts, pro=None, aux=(), col_shards=False):
    s_all, ka = a.shape
    nb = b.shape[1]
    n_aux = len(aux)
    ns = s_all // ts
    assert not col_shards or tn == nb // N_DEV

    def body(a_ref, b_ref, *rest):
        aux_refs, o_ref, acc_ref = rest[:n_aux], rest[n_aux], rest[n_aux + 1]
        at = a_ref[...]
        if pro is not None:
            at = pro(at, *[r[...] for r in aux_refs])
        part = _bdot(at, b_ref[...], TN)
        ss = pl.program_id(2)

        @pl.when(ss == 0)
        def _():
            acc_ref[...] = part

        @pl.when(ss > 0)
        def _():
            acc_ref[...] += part

        @pl.when(ss == ns - 1)
        def _():
            o_ref[...] = acc_ref[...].astype(BF16)

    if col_shards:
        out_spec = pl.BlockSpec((None, tm, tn), lambda i, j, s: (j, i, 0))
        out_shape = jax.ShapeDtypeStruct((N_DEV, ka, tn), BF16)
    else:
        out_spec = pl.BlockSpec((tm, tn), lambda i, j, s: (i, j))
        out_shape = jax.ShapeDtypeStruct((ka, nb), BF16)
    return pl.pallas_call(
        body, name=name,
        grid=(ka // tm, nb // tn, ns),
        in_specs=[pl.BlockSpec((ts, tm), lambda i, j, s: (s, i)),
                  pl.BlockSpec((ts, tn), lambda i, j, s: (s, j))]
        + [pl.BlockSpec((1, tm), lambda i, j, s: (0, i)) for _ in aux],
        out_specs=out_spec, out_shape=out_shape,
        scratch_shapes=[pltpu.VMEM((tm, tn), F32)],
        compiler_params=_params(("parallel", "parallel", "arbitrary")),
    )(a, b, *aux)


def _rowk(name, fn, n_rows, tr, rows, fulls, outs, accs, reverse=False):
    n = n_rows // tr
    n_row, n_full, n_out, n_acc = len(rows), len(fulls), len(outs), len(accs)

    def pos(i):
        return (n - 1 - i) if reverse else i

    def body(*refs):
        row_refs = refs[:n_row]
        full_refs = refs[n_row:n_row + n_full]
        out_refs = refs[n_row + n_full:n_row + n_full + n_out]
        acc_refs = refs[n_row + n_full + n_out:]
        i = pl.program_id(0)

        @pl.when(i == 0)
        def _():
            for r in acc_refs:
                r[...] = jnp.zeros(r.shape, r.dtype)

        res = fn(pos(i), *[r[...] for r in row_refs], *[r[...] for r in full_refs], *[r[...] for r in acc_refs])
        for r, v in zip(out_refs + acc_refs, res):
            r[...] = v.astype(r.dtype)

    def row_map(i, cb, shift):
        return (jnp.clip(pos(i) + shift, 0, n - 1), cb)

    def halo_map(i, cb, shift):
        tile = jnp.clip(pos(i) + shift, 0, n - 1)
        return (tile * (tr // 8) + (tr // 8 - 1 if shift < 0 else 0), cb)

    in_specs = [pl.BlockSpec((tr, w), functools.partial(row_map, cb=cb, shift=sh)) if sh == 0 else
                pl.BlockSpec((8, w), functools.partial(halo_map, cb=cb, shift=sh)) for (_, w, cb, sh) in rows]
    in_specs += [pl.BlockSpec(f.shape, functools.partial(lambda i, nd: (0,) * nd, nd=f.ndim)) for f in fulls]
    out_specs = [pl.BlockSpec((tr, w), lambda i: (pos(i), 0)) for (w, _) in outs]
    out_specs += [pl.BlockSpec((r, w), lambda i: (0, 0)) for (r, w) in accs]
    out_shape = [jax.ShapeDtypeStruct((n_rows, w), dt) for (w, dt) in outs]
    out_shape += [jax.ShapeDtypeStruct((r, w), F32) for (r, w) in accs]
    return pl.pallas_call(
        body, name=name, grid=(n,), in_specs=in_specs, out_specs=out_specs, out_shape=out_shape,
        compiler_params=_params(("arbitrary",)),
    )(*[a for (a, _, _, _) in rows], *fulls)


def _colsum(x):
    return jnp.sum(x, axis=0, keepdims=True)


def _mean(x):
    return jnp.mean(x, axis=-1, keepdims=True)


def _modulate(x, sc, sh):
    return x * (1.0 + sc) + sh


def _shift_down(cur, prev8, j):
    tr = cur.shape[0]
    row8 = _iota(prev8.shape, 0)
    head = jnp.where(row8 < j, pltpu.roll(prev8, j, 0), pltpu.roll(cur[0:8], j, 0))
    return head if tr == 8 else jnp.concatenate([head, pltpu.roll(cur, j, 0)[8:]], axis=0)


def _shift_up(cur, next8, j):
    tr = cur.shape[0]
    row8 = _iota(next8.shape, 0)
    tail = jnp.where(row8 < 8 - j, pltpu.roll(cur[tr - 8:], 8 - j, 0), pltpu.roll(next8, 8 - j, 0))
    return jnp.concatenate([pltpu.roll(cur, tr - j, 0)[:tr - 8], tail], axis=0)


def _conv(cur, prev, w, b):
    out = cur * w[3:4] + b
    for j in (1, 2, 3):
        out = out + _shift_down(cur, prev, j) * w[3 - j:4 - j]
    return out


def _conv_fwd(p, w_xs, b_xs, w_bc, b_bc, s):
    def fn(pos, xs, xs_prev, bc, bc_prev, w_xs, b_xs, w_bc, b_bc):
        first = pos == 0
        xs_prev = jnp.where(first, 0.0, xs_prev)
        bc_prev = jnp.where(first, 0.0, bc_prev)
        return _silu(_conv(xs, xs_prev, w_xs, b_xs)), _silu(_conv(bc, bc_prev, w_bc, b_bc))

    return _rowk("conv_fwd", fn, s, 256,
                 [(p, D, OFF_XS // D, 0), (p, D, OFF_XS // D, -1), (p, 512, OFF_BC // 512, 0), (p, 512, OFF_BC // 512, -1)],
                 [w_xs, b_xs, w_bc, b_bc], [(D, F32), (512, F32)], [])


def _conv_bwd(dxs_a, dbc_a, p, w_xs, b_xs, w_bc, b_bc, s):
    tr = 256
    n = s // tr

    def fn(pos, da1, da1n, x1, x1p, x1n, da2, da2n, x2, x2p, x2n, w1, b1, w2, b2, aw1, ab1, aw2, ab2):
        dx1, dw1, db1 = _conv_bwd_fn(pos, n, da1, da1n, x1, x1p, x1n, w1, b1)
        dx2, dw2, db2 = _conv_bwd_fn(pos, n, da2, da2n, x2, x2p, x2n, w2, b2)
        return dx1, dx2, aw1 + dw1, ab1 + db1, aw2 + dw2, ab2 + db2

    cx, cb = OFF_XS // D, OFF_BC // 512
    return _rowk("conv_bwd", fn, s, tr,
                 [(dxs_a, D, 0, 0), (dxs_a, D, 0, 1), (p, D, cx, 0), (p, D, cx, -1), (p, D, cx, 1),
                  (dbc_a, 512, 0, 0), (dbc_a, 512, 0, 1), (p, 512, cb, 0), (p, 512, cb, -1), (p, 512, cb, 1)],
                 [w_xs, b_xs, w_bc, b_bc], [(D, BF16), (512, BF16)], [(8, D), (1, D), (8, 512), (1, 512)])


def _conv_bwd_fn(pos, n, da, da_next, x, x_prev, x_next, w, b):
    first, last = pos == 0, pos == n - 1
    x_prev = jnp.where(first, 0.0, x_prev)
    shifted = {j: _shift_down(x, x_prev, j) for j in (1, 2, 3)}
    conv = x * w[3:4] + b
    for j in (1, 2, 3):
        conv = conv + shifted[j] * w[3 - j:4 - j]
    dc = da * _dsilu(conv)
    dc_next = jnp.where(last, 0.0, da_next * _dsilu(_conv(x_next, x[x.shape[0] - 8:], w, b)))
    dx = dc * w[3:4]
    dws = [None] * 4
    dws[3] = _colsum(dc * x)
    for j in (1, 2, 3):
        dx = dx + _shift_up(dc, dc_next, j) * w[3 - j:4 - j]
        dws[3 - j] = _colsum(dc * shifted[j])
    row = _iota((8, x.shape[1]), 0)
    dw = jnp.zeros((8, x.shape[1]), F32)
    for k in range(4):
        dw = jnp.where(row == k, dws[k], dw)
    return dx, dw, _colsum(dc)


def _ssd_gates(dtf, bias, a_log):
    lane = _iota(dtf.shape, 1)
    head = lane < NH
    dt = jnp.where(head, _softplus(dtf + bias), 0.0)
    a_neg = jnp.where(_iota(a_log.shape, 1) < NH, -jnp.exp(a_log), 0.0)
    a = dt * a_neg
    cs = _sel_left(_tri_lower(CHUNK), a)
    return dt, a_neg, cs


def _decay_mask(cs_ref, cst_ref, h):
    diff = cs_ref[:, h:h + 1] - cst_ref[h:h + 1, :]
    low = _iota((CHUNK, CHUNK), 1) <= _iota((CHUNK, CHUNK), 0)
    return jnp.where(low, jnp.exp(jnp.minimum(diff, 0.0)), 0.0)


def _ssd_fwd(xs_a, bc_a, p, bias128, alog128, dskip_x, s):
    nc = s // CHUNK
    t = CHUNK

    def body(xs_ref, bc_ref, dtf_ref, bias_ref, alog_ref, dsk_ref, y_ref, st_ref,
             state, x_sc, xw_sc, cs_sc, cst_sc, yd_sc):
        c = pl.program_id(0)

        @pl.when(c == 0)
        def _():
            state[...] = jnp.zeros(state.shape, F32)

        dt, _, cs = _ssd_gates(dtf_ref[...], bias_ref[...], alog_ref[...])
        cs_sc[...] = cs
        cst_sc[...] = cs.T
        cs_last = cs[t - 1:t, :]
        expand = _head_expand()
        ex = _sel_right(jnp.concatenate([dt, jnp.exp(cs), jnp.exp(cs_last - cs)], axis=0), expand, terms=2)
        dt_x, eo_x, we_x = ex[0:t], ex[t:2 * t], ex[2 * t:3 * t]
        g_x = _sel_right(jnp.broadcast_to(jnp.exp(cs_last), (8, 128)), expand)[0:1]
        xs = xs_ref[...]
        x = xs * dt_x
        x_sc[...] = x.astype(BF16)
        xw_sc[...] = (x * we_x).astype(BF16)
        prev = state[...]
        st_ref[0] = prev
        prev_b = prev.astype(BF16)
        for g in range(2):
            cols = slice(g * 512, (g + 1) * 512)
            b_g = bc_ref[:, g * 128:(g + 1) * 128].astype(BF16)
            c_g = bc_ref[:, 256 + g * 128:256 + (g + 1) * 128].astype(BF16)
            gmat = _dot(c_g, b_g, NT)
            y_off = _dot(c_g, prev_b[:, cols]) * eo_x[:, cols]
            s_loc = _dot(b_g, xw_sc[:, cols], TN)
            state[:, cols] = g_x[:, cols] * prev[:, cols] + s_loc
            for e in range(HG):
                h = g * HG + e
                m = gmat * _decay_mask(cs_sc, cst_sc, h)
                yd_sc[:, h * HD:(h + 1) * HD] = _dot(m.astype(BF16), x_sc[:, h * HD:(h + 1) * HD])
            y_ref[:, cols] = yd_sc[:, cols] + y_off + dsk_ref[:, cols] * xs[:, cols]

    return pl.pallas_call(
        body, name="ssd_fwd", grid=(nc,),
        in_specs=[pl.BlockSpec((t, D), lambda c: (c, 0)),
                  pl.BlockSpec((t, 512), lambda c: (c, 0)),
                  pl.BlockSpec((t, 128), lambda c: (c, OFF_DTF // 128)),
                  pl.BlockSpec((1, 128), lambda c: (0, 0)),
                  pl.BlockSpec((1, 128), lambda c: (0, 0)),
                  pl.BlockSpec((1, D), lambda c: (0, 0))],
        out_specs=[pl.BlockSpec((t, D), lambda c: (c, 0)),
                   pl.BlockSpec((1, NSTATE, D), lambda c: (c, 0, 0))],
        out_shape=[jax.ShapeDtypeStruct((s, D), F32), jax.ShapeDtypeStruct((nc, NSTATE, D), F32)],
        scratch_shapes=[pltpu.VMEM((NSTATE, D), F32), pltpu.VMEM((t, D), BF16), pltpu.VMEM((t, D), BF16),
                        pltpu.VMEM((t, 128), F32), pltpu.VMEM((128, t), F32), pltpu.VMEM((t, D), F32)],
        compiler_params=_params(("arbitrary",)),
    )(xs_a, bc_a, p, bias128, alog128, dskip_x)


def _ssd_bwd(dy, xs_a, bc_a, p, states, bias128, alog128, dskip_x, s):
    nc = s // CHUNK
    t = CHUNK

    def body(dy_ref, xs_ref, bc_ref, dtf_ref, st_ref, bias_ref, alog_ref, dsk_ref,
             dxs_ref, dbc_ref, ddt_ref, dalog_ref, dskip_ref,
             dstate, x_sc, dy_sc, dx_sc, deo_sc, dwe_sc, cs_sc, cst_sc, dcol_sc, drow_sc):
        i = pl.program_id(0)

        @pl.when(i == 0)
        def _():
            dstate[...] = jnp.zeros(dstate.shape, F32)
            dalog_ref[...] = jnp.zeros(dalog_ref.shape, F32)
            dskip_ref[...] = jnp.zeros(dskip_ref.shape, F32)

        dtf = dtf_ref[...]
        dt, a_neg, cs = _ssd_gates(dtf, bias_ref[...], alog_ref[...])
        cs_sc[...] = cs
        cst_sc[...] = cs.T
        cs_last = cs[t - 1:t, :]
        eo, we, g_end = jnp.exp(cs), jnp.exp(cs_last - cs), jnp.exp(cs_last)
        expand, reduce = _head_expand(), _head_reduce()
        ex = _sel_right(jnp.concatenate([dt, eo, we], axis=0), expand, terms=2)
        dt_x, eo_x, we_x = ex[0:t], ex[t:2 * t], ex[2 * t:3 * t]
        g_x = _sel_right(jnp.broadcast_to(g_end, (8, 128)), expand)[0:1]
        xs = xs_ref[...]
        dyv = dy_ref[...]
        x = xs * dt_x
        x_sc[...] = x.astype(BF16)
        dy_sc[...] = dyv.astype(BF16)
        dyo_b = (dyv * eo_x).astype(BF16)
        xw_b = (x * we_x).astype(BF16)
        prev = st_ref[0]
        prev_b = prev.astype(BF16)
        dnext = dstate[...]
        dnext_b = dnext.astype(BF16)
        dcol_sc[...] = jnp.zeros(dcol_sc.shape, F32)
        drow_sc[...] = jnp.zeros(drow_sc.shape, F32)
        lane_row = _iota((1, 128), 1)
        sub_col = _iota((128, 1), 0)
        for g in range(2):
            cols = slice(g * 512, (g + 1) * 512)
            b_g = bc_ref[:, g * 128:(g + 1) * 128].astype(BF16)
            c_g = bc_ref[:, 256 + g * 128:256 + (g + 1) * 128].astype(BF16)
            gmat = _dot(c_g, b_g, NT)
            b_ds = _dot(b_g, dnext_b[:, cols])
            c_s = _dot(c_g, prev_b[:, cols])
            dx_sc[:, cols] = b_ds * we_x[:, cols]
            deo_sc[:, cols] = dyv[:, cols] * c_s
            dwe_sc[:, cols] = b_ds * x[:, cols]
            db = _dot(xw_b[:, cols], dnext_b[:, cols], NT)
            dc = _dot(dyo_b[:, cols], prev_b[:, cols], NT)
            dstate[:, cols] = g_x[:, cols] * dnext[:, cols] + _dot(c_g, dyo_b[:, cols], TN)
            dg = jnp.zeros((t, t), F32)
            for e in range(HG):
                h = g * HG + e
                hc = slice(h * HD, (h + 1) * HD)
                lmat = _decay_mask(cs_sc, cst_sc, h)
                m = gmat * lmat
                dx_sc[:, hc] += _dot(m.astype(BF16), dy_sc[:, hc], TN)
                dm = _dot(dy_sc[:, hc], x_sc[:, hc], NT)
                dg = dg + dm * lmat
                qm = dm * m
                dcol_sc[...] += jnp.sum(qm, axis=1, keepdims=True) * (lane_row == h).astype(F32)
                drow_sc[...] += (sub_col == h).astype(F32) * jnp.sum(qm, axis=0, keepdims=True)
            dg_b = dg.astype(BF16)
            dbc_ref[:, g * 128:(g + 1) * 128] = db + _dot(dg_b, c_g, TN)
            dbc_ref[:, 256 + g * 128:256 + (g + 1) * 128] = dc + _dot(dg_b, b_g)
        d_eo = _sel_right(deo_sc[...], reduce, terms=2)
        d_we = _sel_right(dwe_sc[...], reduce, terms=2)
        d_gend = _sel_right(jnp.broadcast_to(_colsum(dnext * prev), (8, D)), reduce)[0:1]
        d_cs = dcol_sc[...] - drow_sc[...].T + d_eo * eo - d_we * we
        extra = _colsum(d_we * we) + d_gend * g_end
        d_cs = d_cs + jnp.where(_iota((t, 128), 0) == t - 1, extra, 0.0)
        da = _sel_left(_tri_upper(t), d_cs)
        dx = dx_sc[...]
        ddt = _sel_right(dx * xs, reduce, terms=2) + da * a_neg
        dxs_ref[...] = dx * dt_x + dsk_ref[...] * dyv
        ddt_ref[...] = jnp.where(_iota((t, 128), 1) < NH, ddt * _sigmoid(dtf + bias_ref[...]), 0.0)
        dalog_ref[...] += _colsum(da * dt) * a_neg
        dskip_ref[...] += _sel_right(jnp.broadcast_to(_colsum(dyv * xs), (8, D)), reduce)[0:1]

    rev = lambda i: nc - 1 - i
    return pl.pallas_call(
        body, name="ssd_bwd", grid=(nc,),
        in_specs=[pl.BlockSpec((t, D), lambda i: (rev(i), 0)),
                  pl.BlockSpec((t, D), lambda i: (rev(i), 0)),
                  pl.BlockSpec((t, 512), lambda i: (rev(i), 0)),
                  pl.BlockSpec((t, 128), lambda i: (rev(i), OFF_DTF // 128)),
                  pl.BlockSpec((1, NSTATE, D), lambda i: (rev(i), 0, 0)),
                  pl.BlockSpec((1, 128), lambda i: (0, 0)),
                  pl.BlockSpec((1, 128), lambda i: (0, 0)),
                  pl.BlockSpec((1, D), lambda i: (0, 0))],
        out_specs=[pl.BlockSpec((t, D), lambda i: (rev(i), 0)),
                   pl.BlockSpec((t, 512), lambda i: (rev(i), 0)),
                   pl.BlockSpec((t, 128), lambda i: (rev(i), 0)),
                   pl.BlockSpec((1, 128), lambda i: (0, 0)),
                   pl.BlockSpec((1, 128), lambda i: (0, 0))],
        out_shape=[jax.ShapeDtypeStruct((s, D), F32), jax.ShapeDtypeStruct((s, 512), F32),
                   jax.ShapeDtypeStruct((s, 128), F32), jax.ShapeDtypeStruct((1, 128), F32),
                   jax.ShapeDtypeStruct((1, 128), F32)],
        scratch_shapes=[pltpu.VMEM((NSTATE, D), F32), pltpu.VMEM((t, D), BF16), pltpu.VMEM((t, D), BF16),
                        pltpu.VMEM((t, D), F32), pltpu.VMEM((t, D), F32), pltpu.VMEM((t, D), F32),
                        pltpu.VMEM((t, 128), F32), pltpu.VMEM((128, t), F32),
                        pltpu.VMEM((t, 128), F32), pltpu.VMEM((128, t), F32)],
        compiler_params=_params(("arbitrary",)),
    )(dy, xs_a, bc_a, p, states, bias128, alog128, dskip_x)


def _gate_lanes(shape):
    lane = _iota(shape, 1)
    return (lane >= NH) & (lane < 2 * NH)


def _cum_fwd(p, bias128, s):
    tr = min(512, s)

    def body(dtf_ref, bias_ref, o_ref, carry):
        @pl.when(pl.program_id(0) == 0)
        def _():
            carry[...] = jnp.zeros(carry.shape, F32)

        lf = jnp.where(_gate_lanes((tr, 128)), _log_sigmoid(dtf_ref[...] + bias_ref[...]), 0.0)
        cum = _sel_left(_tri_lower(tr), lf) + carry[...]
        carry[...] = cum[tr - 1:tr, :]
        o_ref[...] = cum

    return pl.pallas_call(
        body, name="cum_fwd", grid=(s // tr,),
        in_specs=[pl.BlockSpec((tr, 128), lambda i: (i, OFF_DTF // 128)), pl.BlockSpec((1, 128), lambda i: (0, 0))],
        out_specs=pl.BlockSpec((tr, 128), lambda i: (i, 0)),
        out_shape=jax.ShapeDtypeStruct((s, 128), F32),
        scratch_shapes=[pltpu.VMEM((1, 128), F32)],
        compiler_params=_params(("arbitrary",)),
    )(p, bias128)


def _cum_bwd(dcum, ddt_raw, p, bias128, s):
    tr = min(512, s)

    def fn(pos, dcum, ddt, dtf, bias, carry, acc):
        suffix = _sel_left(_tri_upper(tr), dcum) + carry
        dfr = jnp.where(_gate_lanes((tr, 128)), suffix * _sigmoid(-(dtf + bias)), 0.0)
        out = ddt + dfr
        return out, suffix[0:1, :], acc + _colsum(out)

    return _rowk("cum_bwd", fn, s, tr, [(dcum, 128, 0, 0), (ddt_raw, 128, 0, 0), (p, 128, OFF_DTF // 128, 0)],
                 [bias128], [(128, BF16)], [(1, 128), (1, 128)], reverse=True)


ATT_BLOCK = 512
ATT_STRIP = 32


def _head_part(shape, h, dim):
    i = _iota(shape, dim)
    return (i >= h * HD) & (i < (h + 1) * HD)


def _k_augmented(k_blk, cum_blk, j, h):
    tk = k_blk.shape[0]
    lane = _iota((tk, 128), 1)
    col = jnp.sum(jnp.where(lane == NH + 2 * j + h, cum_blk, 0.0), axis=1, keepdims=True)
    c0, c1, c2 = [c.astype(F32) for c in _split3(-col)]
    k_h = k_blk if h == 0 else pltpu.roll(k_blk, HD, 1)
    aug = jnp.where(lane == HD, c0, jnp.where(lane == HD + 1, c1, jnp.where(lane == HD + 2, c2, 0.0)))
    return jnp.where(lane < HD, k_h, aug).astype(BF16)


def _q_augmented_t(q_blk):
    tq = q_blk.shape[0]
    q_t = (q_blk * ATT_SCALE).T.astype(BF16)
    ones = (_iota((HD, tq), 0) < 3).astype(BF16)
    return [jnp.concatenate([q_t[h * HD:(h + 1) * HD], ones], axis=0) for h in range(2)]


def _rows01(r0, r1):
    sub = _iota((8, r0.shape[1]), 0)
    return jnp.where(sub == 0, r0, jnp.where(sub == 1, r1, 0.0))


def _fold8(x, op, cur):
    for g in range(x.shape[0] // 8):
        cur = op(cur, x[8 * g:8 * (g + 1), :])
    return cur


def _attn_fwd(p, cum, s):
    t = min(ATT_BLOCK, s)
    nq = s // t
    r = ATT_STRIP

    def body(q_ref, k_ref, v_ref, c_ref, o_ref, lse_ref, kaug_sc, vt_sc, s0_sc, s1_sc, p0_sc, p1_sc, m_sc, l_sc, acc_sc):
        j, qi = pl.program_id(0), pl.program_id(1)
        s_sc, p_sc = (s0_sc, s1_sc), (p0_sc, p1_sc)

        @pl.when(qi == 0)
        def _():
            for c in range(nq):
                rows = slice(c * t, (c + 1) * t)
                k_blk, vt = k_ref[rows, :], v_ref[rows, :].T
                for h in range(2):
                    kaug_sc[h, rows, :] = _k_augmented(k_blk, c_ref[rows, :], j, h)
                    vt_sc[h, :, rows] = vt[h * HD:(h + 1) * HD].astype(BF16)

        qaug_t = _q_augmented_t(q_ref[...])
        m_sc[...] = jnp.full(m_sc.shape, -1e30, F32)
        l_sc[...] = jnp.zeros(l_sc.shape, F32)
        acc_sc[...] = jnp.zeros(acc_sc.shape, F32)
        top = _iota((128, t), 0) < HD

        def logits(kb, buf):
            kv = pl.ds(pl.multiple_of(kb * t, t), t)
            for h in range(2):
                s_sc[buf][h] = _dot(kaug_sc[h, kv, :], qaug_t[h])

        def softmax(buf, diagonal):
            alphas = []
            for h in range(2):
                cur = jnp.full((8, t), -1e30, F32)
                for i in range(t // r):
                    rows = slice(i * r, (i + 1) * r)
                    x = s_sc[buf][h, rows, :]
                    if diagonal:
                        x = jnp.where(_iota((r, t), 1) >= i * r + _iota((r, t), 0), x, -1e30)
                        s_sc[buf][h, rows, :] = x
                    cur = _fold8(x, jnp.maximum, cur)
                m_prev = m_sc[h, 0:1, :]
                m_new = jnp.maximum(m_prev, jnp.max(cur, axis=0, keepdims=True))
                alpha = jnp.exp(m_prev - m_new)
                m_sc[h, 0:1, :] = m_new
                alphas.append(alpha)
                tot = jnp.zeros((8, t), F32)
                for i in range(t // r):
                    rows = slice(i * r, (i + 1) * r)
                    pr = jnp.exp(s_sc[buf][h, rows, :] - m_new)
                    p_sc[buf][h, rows, :] = pr.astype(BF16)
                    tot = _fold8(pr, jnp.add, tot)
                l_sc[h, 0:1, :] = alpha * l_sc[h, 0:1, :] + jnp.sum(tot, axis=0, keepdims=True)
            return alphas

        def accumulate(kb, buf, alphas):
            kv = pl.ds(pl.multiple_of(kb * t, t), t)
            for h in range(2):
                part = slice(h * HD, (h + 1) * HD)
                acc_sc[part, :] = acc_sc[part, :] * alphas[h] + _dot(vt_sc[h, :, kv], p_sc[buf][h])

        def first_trip():
            logits(0, 1)
            accumulate(qi, 0, softmax(0, True))
            logits(jnp.minimum(1, qi - 1), 0)
            return tuple(softmax(1, False))

        def only_diagonal():
            accumulate(qi, 0, softmax(0, True))
            return (jnp.ones((1, t), F32),) * 2

        def steady(u, alphas_b):
            accumulate(2 * u - 2, 1, alphas_b)
            logits(2 * u, 1)
            accumulate(2 * u - 1, 0, softmax(0, False))
            logits(jnp.minimum(2 * u + 1, qi - 1), 0)
            return tuple(softmax(1, False))

        logits(qi, 0)
        n_blocks = qi + 1
        alphas_b = lax.cond(qi >= 1, first_trip, only_diagonal)
        alphas_b = lax.fori_loop(1, n_blocks // 2, steady, alphas_b)
        last_b = 2 * (n_blocks // 2) - 2

        @pl.when((qi >= 1) & (n_blocks % 2 == 0))
        def _():
            accumulate(last_b, 1, alphas_b)

        @pl.when((qi >= 2) & (n_blocks % 2 == 1))
        def _():
            accumulate(last_b, 1, alphas_b)
            accumulate(qi - 1, 0, softmax(0, False))

        l0, l1 = l_sc[0, 0:1, :], l_sc[1, 0:1, :]
        o_ref[...] = (acc_sc[...] / jnp.where(top, l0, l1)).T
        lse_ref[0] = _rows01(m_sc[0, 0:1, :] + jnp.log(l0), m_sc[1, 0:1, :] + jnp.log(l1))

    return pl.pallas_call(
        body, name="attn_fwd", grid=(NH // 2, nq),
        in_specs=[pl.BlockSpec((t, 128), lambda j, qi: (qi, OFF_Q // 128 + j)),
                  pl.BlockSpec((s, 128), lambda j, qi: (0, OFF_K // 128 + j)),
                  pl.BlockSpec((s, 128), lambda j, qi: (0, OFF_V // 128 + j)),
                  pl.BlockSpec((s, 128), lambda j, qi: (0, 0))],
        out_specs=[pl.BlockSpec((t, 128), lambda j, qi: (qi, j)),
                   pl.BlockSpec((1, 8, t), lambda j, qi: (j, 0, qi))],
        out_shape=[jax.ShapeDtypeStruct((s, D), F32), jax.ShapeDtypeStruct((NH // 2, 8, s), F32)],
        scratch_shapes=[pltpu.VMEM((2, s, 128), BF16), pltpu.VMEM((2, HD, s), BF16), pltpu.VMEM((2, t, t), F32),
                        pltpu.VMEM((2, t, t), F32), pltpu.VMEM((2, t, t), BF16), pltpu.VMEM((2, t, t), BF16),
                        pltpu.VMEM((2, 8, t), F32), pltpu.VMEM((2, 8, t), F32), pltpu.VMEM((128, t), F32)],
        compiler_params=_params(("parallel", "arbitrary")),
    )(p, p, p, cum)


def _attn_bwd(p, cum, o, lse, do, s):
    t = min(ATT_BLOCK, s)
    nq = s // t
    r = ATT_STRIP

    def body(q_ref, k_ref, v_ref, c_ref, o_ref, lse_ref, do_ref, dq_ref, dk_ref, dv_ref, dc_ref, dr_ref,
             qaugt_sc, qh_sc, dot_sc, doh_sc, delta_sc, dqt_sc, dr_sc, kaug_sc, vh_sc, kt_sc,
             s0_sc, s1_sc, dp0_sc, dp1_sc, p0_sc, p1_sc, ds0_sc, ds1_sc, dk_sc, dv_sc, dc_sc):
        j, ki = pl.program_id(0), pl.program_id(1)
        s_sc, dp_sc, p_sc, ds_sc = (s0_sc, s1_sc), (dp0_sc, dp1_sc), (p0_sc, p1_sc), (ds0_sc, ds1_sc)

        @pl.when(ki == 0)
        def _():
            for c in range(nq):
                rows = slice(c * t, (c + 1) * t)
                q_blk, do_blk = q_ref[rows, :], do_ref[rows, :]
                qaugt_sc[0, :, rows], qaugt_sc[1, :, rows] = _q_augmented_t(q_blk)
                dot_sc[:, rows] = do_blk.T.astype(BF16)
                prod_t = (do_blk * o_ref[rows, :]).T
                delta_sc[:, rows] = _rows01(jnp.sum(prod_t[0:HD], axis=0, keepdims=True),
                                            jnp.sum(prod_t[HD:], axis=0, keepdims=True))
                for h in range(2):
                    head = _head_part((t, 128), h, 1)
                    qh_sc[h, rows, :] = jnp.where(head, q_blk * ATT_SCALE, 0.0).astype(BF16)
                    doh_sc[h, rows, :] = jnp.where(head, do_blk, 0.0).astype(BF16)
            dqt_sc[...] = jnp.zeros(dqt_sc.shape, F32)
            dr_sc[...] = jnp.zeros(dr_sc.shape, F32)

        k_blk, v_blk = k_ref[...], v_ref[...]
        kt = k_blk.T
        for h in range(2):
            kaug_sc[h] = _k_augmented(k_blk, c_ref[...], j, h)
            vh_sc[h] = jnp.where(_head_part((t, 128), h, 1), v_blk, 0.0).astype(BF16)
            kt_sc[h] = kt[h * HD:(h + 1) * HD].astype(BF16)
        dk_sc[...] = jnp.zeros(dk_sc.shape, F32)
        dv_sc[...] = jnp.zeros(dv_sc.shape, F32)
        dc_sc[...] = jnp.zeros(dc_sc.shape, F32)

        def inputs(qb, buf):
            qs = pl.ds(pl.multiple_of(qb * t, t), t)
            for h in range(2):
                s_sc[buf][h] = _dot(kaug_sc[h], qaugt_sc[h, :, qs])
                dp_sc[buf][h] = _dot(vh_sc[h], dot_sc[:, qs])

        def elementwise(qb, buf, diagonal):
            qs = pl.ds(pl.multiple_of(qb * t, t), t)
            for h in range(2):
                lse_row, delta_row = lse_ref[0, h:h + 1, qs], delta_sc[h:h + 1, qs]
                tot = jnp.zeros((8, t), F32)
                for i in range(t // r):
                    rows = slice(i * r, (i + 1) * r)
                    x = s_sc[buf][h, rows, :]
                    if diagonal:
                        x = jnp.where(_iota((r, t), 1) >= i * r + _iota((r, t), 0), x, -1e30)
                    pr = jnp.exp(x - lse_row)
                    ds = pr * (dp_sc[buf][h, rows, :] - delta_row)
                    p_sc[buf][h, rows, :] = pr.astype(BF16)
                    ds_sc[buf][h, rows, :] = ds.astype(BF16)
                    dc_sc[h, rows, :] += sum(ds[:, 128 * g:128 * (g + 1)] for g in range(t // 128))
                    tot = _fold8(ds, jnp.add, tot)
                dr_sc[h, :, qs] += tot

        def outputs(qb, buf):
            qs = pl.ds(pl.multiple_of(qb * t, t), t)
            dv_sc[...] += _dot(p_sc[buf][0], doh_sc[0, qs, :]) + _dot(p_sc[buf][1], doh_sc[1, qs, :])
            dk_sc[...] += _dot(ds_sc[buf][0], qh_sc[0, qs, :]) + _dot(ds_sc[buf][1], qh_sc[1, qs, :])
            for h in range(2):
                dqt_sc[h * HD:(h + 1) * HD, qs] += _dot(kt_sc[h], ds_sc[buf][h])

        def pair(a, b, a_diagonal):
            inputs(a, 0)
            inputs(b, 1)
            elementwise(a, 0, a_diagonal)
            outputs(a, 0)
            elementwise(b, 1, False)
            outputs(b, 1)

        def later(u, carry):
            pair(ki + 1 + 2 * u, ki + 2 + 2 * u, False)
            return carry

        n_later = nq - 1 - ki
        lax.fori_loop(0, n_later // 2, later, 0)

        @pl.when(n_later % 2 == 1)
        def _():
            pair(ki, nq - 1, True)

        @pl.when(n_later % 2 == 0)
        def _():
            inputs(ki, 0)
            elementwise(ki, 0, True)
            outputs(ki, 0)

        dk_ref[...] = dk_sc[...].astype(BF16)
        dv_ref[...] = dv_sc[...].astype(BF16)
        lane = _iota((t, 128), 1)
        cols = jnp.where(lane == 0, jnp.sum(dc_sc[0], axis=1, keepdims=True),
                         jnp.where(lane == 1, jnp.sum(dc_sc[1], axis=1, keepdims=True), 0.0))
        dc_ref[0] = cols.T[0:8, :]

        @pl.when(ki == nq - 1)
        def _():
            for c in range(nq):
                rows = slice(c * t, (c + 1) * t)
                dq_ref[rows, :] = dqt_sc[:, rows].T * ATT_SCALE
            dr_ref[0] = _rows01(jnp.sum(dr_sc[0], axis=0, keepdims=True), jnp.sum(dr_sc[1], axis=0, keepdims=True))

    whole = lambda off: pl.BlockSpec((s, 128), functools.partial(lambda j, ki, off: (0, off + j), off=off))
    return pl.pallas_call(
        body, name="attn_bwd", grid=(NH // 2, nq),
        in_specs=[whole(OFF_Q // 128),
                  pl.BlockSpec((t, 128), lambda j, ki: (ki, OFF_K // 128 + j)),
                  pl.BlockSpec((t, 128), lambda j, ki: (ki, OFF_V // 128 + j)),
                  pl.BlockSpec((t, 128), lambda j, ki: (ki, 0)),
                  whole(0),
                  pl.BlockSpec((1, 8, s), lambda j, ki: (j, 0, 0)),
                  whole(0)],
        out_specs=[whole(0),
                   pl.BlockSpec((t, 128), lambda j, ki: (ki, j)),
                   pl.BlockSpec((t, 128), lambda j, ki: (ki, j)),
                   pl.BlockSpec((1, 8, t), lambda j, ki: (j, 0, ki)),
                   pl.BlockSpec((1, 8, s), lambda j, ki: (j, 0, 0))],
        out_shape=[jax.ShapeDtypeStruct((s, D), F32), jax.ShapeDtypeStruct((s, D), BF16), jax.ShapeDtypeStruct((s, D), BF16),
                   jax.ShapeDtypeStruct((NH // 2, 8, s), F32), jax.ShapeDtypeStruct((NH // 2, 8, s), F32)],
        scratch_shapes=[pltpu.VMEM((2, 128, s), BF16), pltpu.VMEM((2, s, 128), BF16), pltpu.VMEM((128, s), BF16),
                        pltpu.VMEM((2, s, 128), BF16), pltpu.VMEM((8, s), F32), pltpu.VMEM((128, s), F32),
                        pltpu.VMEM((2, 8, s), F32), pltpu.VMEM((2, t, 128), BF16), pltpu.VMEM((2, t, 128), BF16),
                        pltpu.VMEM((2, HD, t), BF16)]
        + [pltpu.VMEM((2, t, t), F32)] * 4 + [pltpu.VMEM((2, t, t), BF16)] * 4
        + [pltpu.VMEM((t, 128), F32), pltpu.VMEM((t, 128), F32), pltpu.VMEM((2, t, 128), F32)],
        compiler_params=_params(("parallel", "arbitrary")),
    )(p, p, p, cum, o, lse, do)


def _ln_stats(u):
    mu = _mean(u)
    d = u - mu
    rstd = lax.rsqrt(_mean(d * d) + EPS)
    return d * rstd, rstd


def _ln_bwd(dx, xh, rstd, gam):
    dxh = dx * gam
    return rstd * (dxh - _mean(dxh) - xh * _mean(dxh * xh))


def _rms_bwd(d, xn, r, w):
    t = d * w
    return r * (t - xn * _mean(t * xn)), _colsum(d * xn)


def _mix_norm(y, p, att, w_ssm, w_att, s):
    def fn(pos, y, z, att, w1, w2):
        g = y * _silu(z)
        n1 = g * lax.rsqrt(_mean(g * g) + EPS) * w1
        n2 = att * lax.rsqrt(_mean(att * att) + EPS) * w2
        return (jnp.concatenate([n1, n2], axis=1),)

    return _rowk("mix_norm", fn, s, 512, [(y, D, 0, 0), (p, D, OFF_Z // D, 0), (att, D, 0, 0)],
                 [w_ssm, w_att], [(2 * D, BF16)], [])[0]


def _mix_norm_bwd(dmix, y, p, att, w_ssm, w_att, s):
    def fn(pos, dmix, y, z, att, w1, w2, a1, a2):
        sz = _silu(z)
        g = y * sz
        r1 = lax.rsqrt(_mean(g * g) + EPS)
        dg, dw1 = _rms_bwd(dmix[:, :D], g * r1, r1, w1)
        r2 = lax.rsqrt(_mean(att * att) + EPS)
        datt, dw2 = _rms_bwd(dmix[:, D:], att * r2, r2, w2)
        return dg * sz, dg * y * _dsilu(z), datt, a1 + dw1, a2 + dw2

    return _rowk("mix_norm_bwd", fn, s, 256, [(dmix, 2 * D, 0, 0), (y, D, 0, 0), (p, D, OFF_Z // D, 0), (att, D, 0, 0)],
                 [w_ssm, w_att], [(D, F32), (D, BF16), (D, F32)], [(1, D), (1, D)])


def _ln1(x0, y, g1, gam, bet, sc2, sh2, s):
    def fn(pos, x0, y, g1, gam, bet, sc2, sh2):
        xh, _ = _ln_stats(ALPHA * x0 + (1.0 + g1) * y)
        x1 = xh * gam + bet
        return x1, _modulate(x1, sc2, sh2)

    return _rowk("ln1", fn, s, 512, [(x0, D, 0, 0), (y, D, 0, 0)], [g1, gam, bet, sc2, sh2], [(D, F32), (D, BF16)], [])


def _ln2_loss(x1, ff, tgt, g2, gam, bet, s):
    def fn(pos, x1, ff, tgt, g2, gam, bet, a_loss, a_dgam, a_dbet, a_dg2):
        xh, rstd = _ln_stats(ALPHA * x1 + (1.0 + g2) * ff)
        err = xh * gam + bet - tgt
        dx2 = err * (1.0 / D)
        du = _ln_bwd(dx2, xh, rstd, gam)
        return (du, du * (1.0 + g2), a_loss + _colsum(err * err), a_dgam + _colsum(dx2 * xh),
                a_dbet + _colsum(dx2), a_dg2 + _colsum(du * ff))

    return _rowk("ln2_loss", fn, s, 512, [(x1, D, 0, 0), (ff, D, 0, 0), (tgt, D, 0, 0)], [g2, gam, bet],
                 [(D, F32), (D, BF16)], [(1, D)] * 4)


def _ln1_bwd(dh2, du2, x0, y, g1, gam, bet, sc2, s):
    def fn(pos, dh2, du2, x0, y, g1, gam, bet, sc2, a_sc, a_sh, a_gam, a_bet, a_g1):
        xh, rstd = _ln_stats(ALPHA * x0 + (1.0 + g1) * y)
        x1 = xh * gam + bet
        dx1 = ALPHA * du2 + dh2 * (1.0 + sc2)
        du1 = _ln_bwd(dx1, xh, rstd, gam)
        return (du1, du1 * (1.0 + g1), a_sc + _colsum(dh2 * x1), a_sh + _colsum(dh2), a_gam + _colsum(dx1 * xh),
                a_bet + _colsum(dx1), a_g1 + _colsum(du1 * y))

    return _rowk("ln1_bwd", fn, s, 512, [(dh2, D, 0, 0), (du2, D, 0, 0), (x0, D, 0, 0), (y, D, 0, 0)],
                 [g1, gam, bet, sc2], [(D, F32), (D, BF16)], [(1, D)] * 5)


def _input_grad(dh1, du1, x0, sc1, s):
    def fn(pos, dh1, du1, x0, sc1, a_sc, a_sh):
        return ALPHA * du1 + dh1 * (1.0 + sc1), a_sc + _colsum(dh1 * x0), a_sh + _colsum(dh1)

    return _rowk("input_grad", fn, s, 512, [(dh1, D, 0, 0), (du1, D, 0, 0), (x0, D, 0, 0)], [sc1],
                 [(D, F32)], [(1, D)] * 2)


def _adamw_math(w, grad, m, v):
    m_new = ADAM_B1 * m + (1.0 - ADAM_B1) * grad
    v_new = ADAM_B2 * v + (1.0 - ADAM_B2) * (grad * grad)
    m_hat = m_new / (1.0 - ADAM_B1 ** ADAM_STEP)
    v_hat = v_new / (1.0 - ADAM_B2 ** ADAM_STEP)
    return -ADAM_LR * (m_hat / (jnp.sqrt(v_hat) + ADAM_EPS) + ADAM_WD * w), m_new, v_new


def _small_update(small_all, layout, w, m, v):
    names = [n for n, _, _ in layout]

    def body(*refs):
        all_ref = refs[0]
        w_refs, m_refs, v_refs = [refs[1 + k * len(names):1 + (k + 1) * len(names)] for k in range(3)]
        sum_ref = refs[1 + 3 * len(names)]
        outs = refs[2 + 3 * len(names):]
        total = all_ref[0]
        for k in range(1, N_DEV):
            total = total + all_ref[k]
        sum_ref[...] = total
        for i, (_, off, size) in enumerate(layout):
            grad = total[:, off:off + size]
            delta, m_new, v_new = _adamw_math(w_refs[i][...], grad, m_refs[i][...], v_refs[i][...])
            for o, val in zip(outs[4 * i:4 * i + 4], (grad, delta, m_new, v_new)):
                o[...] = val

    res = pl.pallas_call(
        body, name="small_update",
        out_shape=[jax.ShapeDtypeStruct(small_all.shape[1:], F32)]
        + [jax.ShapeDtypeStruct(w[n].shape, F32) for n in names for _ in range(4)],
        compiler_params=_params(None),
    )(small_all, *[w[n] for n in names], *[m[n] for n in names], *[v[n] for n in names])
    return res[0], {n: res[1 + 4 * i:5 + 4 * i] for i, n in enumerate(names)}


def _adamw(name, w, g, m, v, *, tr, slots, by_columns=False):
    r, c = w.shape

    def body(w_ref, g_ref, m_ref, v_ref, g_out, d_out, m_out, v_out):
        if slots:
            grad = g_ref[0][:w_ref.shape[0]].astype(F32)
            for k in range(1, N_DEV):
                grad = grad + g_ref[k][:w_ref.shape[0]].astype(F32)
        else:
            grad = g_ref[...]
        g_out[...] = grad
        d_out[...], m_out[...], v_out[...] = _adamw_math(w_ref[...], grad, m_ref[...], v_ref[...])

    if by_columns:
        tile = pl.BlockSpec((r, tr), lambda i: (0, i))
        g_spec = pl.BlockSpec((N_DEV, g.shape[1], tr), lambda i: (0, 0, i)) if slots else tile
    else:
        tile = pl.BlockSpec((tr, c), lambda i: (i, 0))
        g_spec = pl.BlockSpec((N_DEV, tr, c), lambda i: (0, i, 0)) if slots else tile
    return pl.pallas_call(
        body, name=name, grid=((c if by_columns else r) // tr,),
        in_specs=[tile, g_spec, tile, tile], out_specs=[tile] * 4,
        out_shape=[jax.ShapeDtypeStruct((r, c), F32)] * 4,
        compiler_params=_params(("parallel",)),
    )(w, g, m, v)


def _dot_f32(a, b, dims=NN):
    a0, a1, a2 = _split3(a)
    b0, b1, b2 = _split3(b)
    acc = _dot(a0, b0, dims)
    for x, y in ((a0, b1), (a1, b0), (a1, b1), (a0, b2), (a2, b0)):
        acc = acc + _dot(x, y, dims)
    return acc


def _ada_mod(c_all, w_shard, b_shard):
    def body(c_ref, w_ref, b_ref, o_ref):
        act = _silu(c_ref[...])
        act16 = jnp.concatenate([act, jnp.zeros_like(act)], axis=0)
        o_ref[...] = _dot_f32(act16, w_ref[...])[0:N_DEV] + b_ref[...]

    return pl.pallas_call(
        body, name="ada_mod", out_shape=jax.ShapeDtypeStruct((N_DEV, w_shard.shape[1]), F32),
        compiler_params=_params(None),
    )(c_all, w_shard, b_shard)


def _ada_grad(c_all, dmod_cols):
    def body(c_ref, dc_ref, gw_ref):
        act = _silu(c_ref[...])
        act16 = jnp.concatenate([act, jnp.zeros_like(act)], axis=0)
        dm = dc_ref[...]
        dm16 = jnp.concatenate([dm, jnp.zeros_like(dm)], axis=0)
        gw_ref[...] = _dot_f32(act16, dm16, TN)

    return pl.pallas_call(
        body, name="ada_grad", out_shape=jax.ShapeDtypeStruct((D, dmod_cols.shape[1]), F32),
        compiler_params=_params(None),
    )(c_all, dmod_cols)


def _exchange(name, xs, scatter):
    n = len(xs)
    n_peer = N_DEV - 1

    def body(*refs):
        x_refs, o_refs = refs[:n], refs[n:2 * n]
        send_sems, recv_sems, local_sems = refs[2 * n:]
        mx, my, mc = lax.axis_index("x"), lax.axis_index("y"), lax.axis_index("c")
        me = 4 * mx + 2 * my + mc

        def src(a, slot):
            return x_refs[a].at[slot] if scatter else x_refs[a]

        own = [pltpu.make_async_copy(src(a, me), o_refs[a].at[me], local_sems.at[a]) for a in range(n)]
        for cp in own:
            cp.start()
        sends = []
        for d in range(1, N_DEV):
            px = 1 - mx if d & 4 else mx
            py = 1 - my if d & 2 else my
            pc = 1 - mc if d & 1 else mc
            peer = 4 * px + 2 * py + pc
            for a in range(n):
                def copy(src_slot, dst_slot, a=a, d=d, to=(px, py, pc)):
                    return pltpu.make_async_remote_copy(
                        src_ref=src(a, src_slot), dst_ref=o_refs[a].at[dst_slot],
                        send_sem=send_sems.at[a * n_peer + d - 1], recv_sem=recv_sems.at[a * n_peer + d - 1],
                        device_id=to, device_id_type=pl.DeviceIdType.MESH)

                out = copy(peer, me)
                out.start()
                sends.append((out, copy(me, peer)))
        for _, arrival in sends:
            arrival.wait_recv()
        for out, _ in sends:
            out.wait_send()
        for cp in own:
            cp.wait()

    shapes = [tuple(x.shape[1:] if scatter else x.shape) for x in xs]
    return pl.pallas_call(
        body, name=name,
        in_specs=[pl.BlockSpec(memory_space=pl.ANY)] * n, out_specs=[pl.BlockSpec(memory_space=pl.ANY)] * n,
        out_shape=[jax.ShapeDtypeStruct((N_DEV,) + sh, x.dtype) for sh, x in zip(shapes, xs)],
        scratch_shapes=[pltpu.SemaphoreType.DMA((n * n_peer,)), pltpu.SemaphoreType.DMA((n * n_peer,)),
                        pltpu.SemaphoreType.DMA((n,))],
        compiler_params=pltpu.CompilerParams(has_side_effects=True),
    )(*xs)


def _gather_two_level(name, x):
    def body(x_ref, o_ref, send_sems, recv_sems, local_sem):
        mx, my, mc = lax.axis_index("x"), lax.axis_index("y"), lax.axis_index("c")
        me, sibling = (mx, my, mc), (mx, my, 1 - mc)
        chips = [(1 - mx, my), (mx, 1 - my), (1 - mx, 1 - my)]

        def slot(px, py, pc):
            return o_ref.at[4 * px + 2 * py + pc]

        def copy(k, block, to, src=None):
            return pltpu.make_async_remote_copy(
                src_ref=slot(*block) if src is None else src, dst_ref=slot(*block),
                send_sem=send_sems.at[k], recv_sem=recv_sems.at[k], device_id=to, device_id_type=pl.DeviceIdType.MESH)

        mine = pltpu.make_async_copy(x_ref, slot(*me), local_sem)
        mine.start()
        first = [copy(0, me, sibling, src=x_ref)] + [copy(1 + i, me, (*chip, mc), src=x_ref) for i, chip in enumerate(chips)]
        for cp in first:
            cp.start()
        passed = [copy(4 + i, (*chip, mc), sibling) for i, chip in enumerate(chips)]
        for i, chip in enumerate(chips):
            copy(1 + i, (*chip, mc), me).wait_recv()
            passed[i].start()
        copy(0, sibling, me).wait_recv()
        for i, chip in enumerate(chips):
            copy(4 + i, (*chip, 1 - mc), me).wait_recv()
        for cp in first + passed:
            cp.wait_send()
        mine.wait()

    return pl.pallas_call(
        body, name=name,
        in_specs=[pl.BlockSpec(memory_space=pl.ANY)], out_specs=pl.BlockSpec(memory_space=pl.ANY),
        out_shape=jax.ShapeDtypeStruct((N_DEV,) + tuple(x.shape), x.dtype),
        scratch_shapes=[pltpu.SemaphoreType.DMA((7,)), pltpu.SemaphoreType.DMA((7,)), pltpu.SemaphoreType.DMA(())],
        compiler_params=pltpu.CompilerParams(has_side_effects=True),
    )(x)


def _after(x, zero):
    return x if zero is None else x + zero.reshape(-1)[0].astype(x.dtype)


def _exchange_copies(x_refs, land_refs, send_sems, recv_sems, scatter):
    n = len(x_refs)
    n_peer = N_DEV - 1
    mx, my, mc = lax.axis_index("x"), lax.axis_index("y"), lax.axis_index("c")
    me = 4 * mx + 2 * my + mc
    pairs = []
    for d in range(1, N_DEV):
        px = 1 - mx if d & 4 else mx
        py = 1 - my if d & 2 else my
        pc = 1 - mc if d & 1 else mc
        peer = 4 * px + 2 * py + pc
        for a in range(n):
            def copy(src_slot, dst_slot, a=a, d=d, to=(px, py, pc)):
                return pltpu.make_async_remote_copy(
                    src_ref=x_refs[a].at[src_slot] if scatter else x_refs[a], dst_ref=land_refs[a].at[dst_slot],
                    send_sem=send_sems.at[a * n_peer + d - 1], recv_sem=recv_sems.at[a * n_peer + d - 1],
                    device_id=to, device_id_type=pl.DeviceIdType.MESH)

            pairs.append((copy(peer, me), copy(me, peer)))
    return me, pairs


def _exchange_async(name, xs, scatter, collective_id):
    n = len(xs)
    shapes = [tuple(x.shape[1:] if scatter else x.shape) for x in xs]
    x_refs = [jax.new_ref(x, memory_space=pltpu.MemorySpace.HBM) for x in xs]
    land_refs = [jax.empty_ref(jax.ShapeDtypeStruct((N_DEV,) + sh, x.dtype), memory_space=pltpu.MemorySpace.HBM)
                 for sh, x in zip(shapes, xs)]

    @pl.kernel(mesh=plsc.ScalarSubcoreMesh(axis_name="sequencer", num_cores=1), name=name,
               scratch_types=(pltpu.SemaphoreType.DMA((n * (N_DEV - 1),)), pltpu.SemaphoreType.DMA((n * (N_DEV - 1),)),
                              pltpu.SemaphoreType.DMA((n,))),
               compiler_params=pltpu.CompilerParams(collective_id=collective_id))
    def launch(send_sems, recv_sems, own_sems):
        barrier = pltpu.get_barrier_semaphore()
        mx, my, mc = lax.axis_index("x"), lax.axis_index("y"), lax.axis_index("c")
        for d in range(1, N_DEV):
            peer = (1 - mx if d & 4 else mx, 1 - my if d & 2 else my, 1 - mc if d & 1 else mc)
            pl.semaphore_signal(barrier, inc=1, device_id=peer, device_id_type=pl.DeviceIdType.MESH)
        pl.semaphore_wait(barrier, N_DEV - 1)
        me, pairs = _exchange_copies(x_refs, land_refs, send_sems, recv_sems, scatter)
        own = [pltpu.make_async_copy(x_refs[a].at[me] if scatter else x_refs[a], land_refs[a].at[me], own_sems.at[a])
               for a in range(n)]
        for cp in own:
            cp.start()
        for out, _ in pairs:
            out.start()
        for out, arrival in pairs:
            arrival.wait_recv()
            out.wait_send()
        for cp in own:
            cp.wait()

    launch()
    return lambda: [r[...] for r in land_refs]


def _relu2(a):
    r = jnp.maximum(a, 0.0)
    return r * r


def _relu2_grad(acc, r):
    return acc * (2.0 * jnp.sqrt(r.astype(F32)))


def _local_step(x0, tgt, mod, wcat_t, late_weights, send_grads, conv_w, conv_b, dt_bias, a_log, d_skip, ssm_norm_w, f_bias,
                attn_norm_w, ln1_g, ln1_b, ln2_g, ln2_b):
    ff_w = DFF // N_DEV
    s = x0.shape[0]
    tm = min(1024, s)
    ts = min(1024, s)
    sh1, sc1, g1, sh2, sc2, g2 = [mod[:, i * D:(i + 1) * D] for i in range(6)]
    zero = jnp.zeros((1, 128 - 2 * NH), F32)
    bias128 = jnp.concatenate([dt_bias, f_bias, zero], axis=1)
    alog128 = jnp.concatenate([a_log, jnp.zeros((1, 128 - NH), F32)], axis=1)
    dskip_x = jnp.repeat(d_skip, HD, axis=1)
    w_xs, w_bc, b_xs, b_bc = conv_w[:, :D], conv_w[:, D:], conv_b[:, :D], conv_b[:, D:]

    h1, = _rowk("modulate", lambda pos, x, sc, sh: (_modulate(x, sc, sh),), s, 512, [(x0, D, 0, 0)], [sc1, sh1], [(D, BF16)], [])
    p = _mm_nt("in_proj", [(h1, D, 0)], [(wcat_t, D, 0)], n=PCOLS, tm=tm, tn=1152, out_dtype=F32)
    xs_a, bc_a = _conv_fwd(p, w_xs, b_xs, w_bc, b_bc, s)
    y_ssd, states = _ssd_fwd(xs_a, bc_a, p, bias128, alog128, dskip_x, s)
    cum = _cum_fwd(p, bias128, s)
    att, lse = _attn_fwd(p, cum, s)
    wout, w1s, w2 = late_weights()
    ymix = _mix_norm(y_ssd, p, att, ssm_norm_w, attn_norm_w, s)
    y = _mm_nn("out_proj", ymix, wout, tm=tm, tn=1024, tk=2 * D, out_dtype=F32)
    x1, h2 = _ln1(x0, y, g1, ln1_g, ln1_b, sc2, sh2, s)
    tall = min(2048, s)
    r = _mm_nn("ff_in", h2, w1s, tm=tall, tn=ff_w, tk=D, out_dtype=BF16, epi=_relu2)
    ff = _mm_nn("ff_out", r, w2, tm=tall, tn=1024, tk=1024, out_dtype=F32)
    du2, dff, sq_err, d_ln2_g, d_ln2_b, d_g2 = _ln2_loss(x1, ff, tgt, g2, ln2_g, ln2_b, s)

    da1 = _mm_nt("d_ff_hidden", [(dff, D, 0)], [(w2, D, 0)], n=DFF, tm=tall, tn=1024, out_dtype=BF16, epi=_relu2_grad,
                 epi_aux=(r,))
    d_w2 = _mm_tn("d_w_ff_out", r, dff, tm=1024, tn=1024, ts=ts)
    d_w1s = _mm_tn("d_w_ff_in", h2, da1, tm=1024, tn=ff_w, ts=ts, col_shards=True)
    dh2 = _mm_nt("d_ff_input", [(da1, ff_w, k) for k in range(N_DEV)], [(w1s, ff_w, k) for k in range(N_DEV)], n=D,
                 tm=min(512, s), tn=1024, out_dtype=F32)
    du1, dy, d_sc2, d_sh2, d_ln1_g, d_ln1_b, d_g1 = _ln1_bwd(dh2, du2, x0, y, g1, ln1_g, ln1_b, sc2, s)

    dmix = _mm_nt("d_mix", [(dy, D, 0)], [(wout, D, 0)], n=2 * D, tm=tm, tn=1024, out_dtype=F32)
    d_wout = _mm_tn("d_w_out", ymix, dy, tm=1024, tn=1024, ts=ts)
    sent = send_grads("late", [d_w1s, d_w2.reshape(N_DEV, -1, D), d_wout.reshape(N_DEV, -1, D)])
    dy_ssd, dz, datt, d_ssm_w, d_attn_w = _mix_norm_bwd(dmix, y_ssd, p, att, _after(ssm_norm_w, sent), attn_norm_w, s)
    dq, dk, dv, dcs, drs = _attn_bwd(p, cum, att, lse, datt, s)
    dxs_a, dbc_a, ddt_raw, d_alog, d_dskip = _ssd_bwd(dy_ssd, xs_a, bc_a, p, states, bias128, alog128, dskip_x, s)
    dcum = jnp.pad((drs - dcs)[:, :2, :].reshape(NH, s).T, ((0, 0), (NH, 128 - 2 * NH)))
    ddtf, _, d_bias = _cum_bwd(dcum, ddt_raw, p, bias128, s)
    dxs, dbc, d_wc_xs, d_bc_xs, d_wc_bc, d_bc_bc = _conv_bwd(dxs_a, dbc_a, p, w_xs, b_xs, w_bc, b_bc, s)

    segs = [(dz, OFF_Z, D), (dxs, OFF_XS, D), (dq, OFF_Q, D), (dk, OFF_K, D), (dv, OFF_V, D), (dbc, OFF_BC, 512),
            (ddtf, OFF_DTF, 128)]
    d_z, d_xs, d_q, d_k, d_v, d_bcw, d_dtf = [
        _mm_tn("d_w_in_%d" % i, a, h1, tm=min(w, 1024), tn=1024, ts=ts)
        for i, (a, _, w) in enumerate(segs)]
    d_w_in_t = dict(z=d_z, xs=d_xs, bc=d_bcw, dt=d_dtf[:NH], q=d_q, k=d_k, v=d_v, f=d_dtf[NH:2 * NH])
    sent = send_grads("in", [_shard_w_in_grad_t(d_w_in_t)])
    segs[-1] = (_after(ddtf, sent), OFF_DTF, 128)
    dh1 = _mm_nt("d_h1", [(a, w, 0) for a, _, w in segs], [(wcat_t, w, off // w) for _, off, w in segs], n=D,
                 tm=min(512, s), tn=1024, out_dtype=F32, b_rows=True)
    grad_x, d_sc1, d_sh1 = _input_grad(dh1, du1, x0, sc1, s)

    return dict(
        loss=(0.5 / D) * jnp.sum(sq_err), grad_x=grad_x,
        d_mod=jnp.concatenate([d_sh1, d_sc1, d_g1, d_sh2, d_sc2, d_g2], axis=1),
        d_conv_w=jnp.concatenate([d_wc_xs[:4], d_wc_bc[:4]], axis=1), d_conv_b=jnp.concatenate([d_bc_xs, d_bc_bc], axis=1),
        d_ssm_norm_w=d_ssm_w, d_attn_norm_w=d_attn_w, d_ln1_g=d_ln1_g, d_ln1_b=d_ln1_b, d_ln2_g=d_ln2_g, d_ln2_b=d_ln2_b,
        d_gate_bias=d_bias, d_a_log=d_alog, d_d_skip=d_dskip)


W_IN_SEGS = [('z', W_Z, D), ('xs', W_XS, D), ('bc', W_BC, 512), ('dt', W_DT, NH), ('q', W_Q, D), ('k', W_K, D),
             ('v', W_V, D), ('f', W_F, NH)]
SHARD_W = IN_COLS // N_DEV


def _pack_w_in_t(shards):
    def rows(lo, hi):
        pieces = []
        while lo < hi:
            dev = lo // SHARD_W
            end = min(hi, (dev + 1) * SHARD_W)
            pieces.append(shards[dev][lo - dev * SHARD_W:end - dev * SHARD_W])
            lo = end
        return pieces

    seg = {n: rows(off, off + w) for n, off, w in W_IN_SEGS}
    pieces = seg['z'] + seg['xs'] + seg['q'] + seg['k'] + seg['v'] + seg['bc'] + seg['dt'] + seg['f']
    return jnp.concatenate(pieces + [jnp.zeros((128 - 2 * NH, D), shards.dtype)], axis=0)


SHARD_ROWS = -(-SHARD_W // 16) * 16


def _shard_w_in_grad_t(d_w_in_t):
    blocks = []
    for dev in range(N_DEV):
        lo, hi = dev * SHARD_W, (dev + 1) * SHARD_W
        pieces = [d_w_in_t[n][max(lo, off) - off:min(hi, off + w) - off] for n, off, w in W_IN_SEGS
                  if max(lo, off) < min(hi, off + w)]
        pieces.append(jnp.zeros((SHARD_ROWS - SHARD_W, D), pieces[0].dtype))
        blocks.append(jnp.concatenate(pieces, axis=0))
    return jnp.stack(blocks, axis=0)


WEIGHTS = ['w_ada', 'b_ada', 'w_in', 'conv_w', 'conv_b', 'dt_bias', 'a_log', 'd_skip', 'ssm_norm_w', 'f_bias',
           'attn_norm_w', 'w_out', 'ln1_g', 'ln1_b', 'w_ff_in', 'w_ff_out', 'ln2_g', 'ln2_b']
BIG = ['w_in', 'w_out', 'w_ff_in', 'w_ff_out']
SMALL_LAYOUT = [('b_ada', 0, 6 * D), ('conv_b', 12288, 1536), ('ssm_norm_w', 13824, D), ('attn_norm_w', 14848, D),
                ('ln1_g', 15872, D), ('ln1_b', 16896, D), ('ln2_g', 17920, D), ('ln2_b', 18944, D),
                ('dt_bias', 19968, NH), ('f_bias', 19968 + NH, NH), ('a_log', 20096, NH), ('d_skip', 20224, NH)]
SMALL_LOSS_LANE = 20352


def _pad_lanes(v, n=128):
    return jnp.pad(v, ((0, 0), (0, n - v.shape[1])))


def kernel(x, c, w_ada, b_ada, w_in, conv_w, conv_b, dt_bias, a_log, d_skip, ssm_norm_w, f_bias, attn_norm_w, w_out, ln1_g, ln1_b, w_ff_in, w_ff_out, ln2_g, ln2_b, loss_target, m_w_ada, m_b_ada, m_w_in, m_conv_w, m_conv_b, m_dt_bias, m_a_log, m_d_skip, m_ssm_norm_w, m_f_bias, m_attn_norm_w, m_w_out, m_ln1_g, m_ln1_b, m_w_ff_in, m_w_ff_out, m_ln2_g, m_ln2_b, v_w_ada, v_b_ada, v_w_in, v_conv_w, v_conv_b, v_dt_bias, v_a_log, v_d_skip, v_ssm_norm_w, v_f_bias, v_attn_norm_w, v_w_out, v_ln1_g, v_ln1_b, v_w_ff_in, v_w_ff_out, v_ln2_g, v_ln2_b):
    args = dict(locals())
    w = {n: args[n] for n in WEIGHTS}
    m = {n: args['m_' + n] for n in WEIGHTS}
    v = {n: args['v_' + n] for n in WEIGHTS}
    me = 4 * lax.axis_index("x") + 2 * lax.axis_index("y") + lax.axis_index("c")
    ada_cols = 6 * D // N_DEV
    conv_cols = conv_w.shape[2]

    c_all, conv_all = _exchange("gather_cond", [c, conv_w[0]], False)
    c_all = c_all.reshape(N_DEV, D)
    conv_w_full = conv_all.transpose(1, 0, 2).reshape(4, N_DEV * conv_cols)
    b_shard = lax.dynamic_slice(b_ada, (0, me * ada_cols), (1, ada_cols))
    mod_all, = _exchange("gather_mod", [_ada_mod(c_all, w_ada[0], b_shard)], False)
    mod = lax.dynamic_index_in_dim(mod_all, me, axis=1, keepdims=False).reshape(1, 6 * D)

    w_in_t = _after(jnp.swapaxes(w_in[0], 0, 1).astype(BF16), mod * 0)
    win_s = _gather_two_level("gather_w_in", w_in_t)
    first_done = win_s[0, 0:1, 0:1] * 0
    rest = _exchange_async("gather_rest", [_after(w[n][0].astype(BF16), first_done) for n in BIG[1:]], False, 1)

    def late_weights():
        wout_s, w1s, w2_s = rest()
        return wout_s.reshape(2 * D, D), w1s, w2_s.reshape(DFF, D)

    sends = {}

    def send_grads(tag, blocks):
        sends[tag] = _exchange_async("scatter_" + tag, blocks, True, {'late': 2, 'in': 3}[tag])
        return sum(b.reshape(-1)[0].astype(F32) * 0 for b in blocks)

    out = _local_step(x[0], loss_target[0], mod, _pack_w_in_t(win_s), late_weights, send_grads,
                      conv_w_full, conv_b, dt_bias, a_log, d_skip, ssm_norm_w, f_bias, attn_norm_w, ln1_g, ln1_b, ln2_g, ln2_b)

    small = jnp.concatenate(
        [out['d_mod'], out['d_conv_w'].reshape(1, -1), out['d_conv_b'], out['d_ssm_norm_w'], out['d_attn_norm_w'],
         out['d_ln1_g'], out['d_ln1_b'], out['d_ln2_g'], out['d_ln2_b'], out['d_gate_bias'], out['d_a_log'],
         out['d_d_skip'], _pad_lanes(out['loss'].reshape(1, 1))], axis=1)
    small_landed = _exchange_async("gather_small", [small], False, 4)
    (g_ff_in, g_ff_out, g_out), (g_in,) = sends['late'](), sends['in']()
    g_parts = dict(w_ff_in=g_ff_in, w_ff_out=g_ff_out, w_out=g_out, w_in=g_in)
    big = {n: _adamw("adamw_" + n, w[n][0], g_parts[n], m[n][0], v[n][0], tr=256, slots=True) for n in BIG[1:]}
    t = lambda a: jnp.swapaxes(a[0], 0, 1)
    big['w_in'] = [jnp.swapaxes(r, 0, 1) for r in _adamw("adamw_w_in", t(w_in), g_parts['w_in'], t(m_w_in), t(v_w_in),
                                                         tr=256, slots=True, by_columns=True)]
    big_done = sum(big[n][1][0:1, 0:1] * 0 for n in BIG)
    small_all = _after(small_landed()[0], big_done)
    ssum, small_res = _small_update(small_all, SMALL_LAYOUT, w, m, v)
    dmod_all = small_all[:, 0, :6 * D]
    g_w_ada = _ada_grad(c_all, lax.dynamic_slice(dmod_all, (0, me * ada_cols), (N_DEV, ada_cols)))
    ada = _adamw("adamw_ada", w_ada[0], g_w_ada, m_w_ada[0], v_w_ada[0], tr=256, slots=False)
    g_conv_w = lax.dynamic_slice(ssum[:, 6 * D:6 * D + 4 * N_DEV * conv_cols].reshape(4, N_DEV * conv_cols),
                                 (0, me * conv_cols), (4, conv_cols))
    conv = _adamw("adamw_conv_w", conv_w[0], g_conv_w, m_conv_w[0], v_conv_w[0], tr=4, slots=False)

    results = []
    for k in range(4):
        vals = {n: small_res[n][k] for n in small_res}
        vals['w_ada'], vals['conv_w'] = ada[k][None], conv[k][None]
        for n in BIG:
            vals[n] = big[n][k][None]
        results.append(vals)
    return (ssum[0, SMALL_LOSS_LANE], out['grad_x'][None], *[res[n] for res in results for n in WEIGHTS])
```

```python
import functools

import jax
import jax.numpy as jnp
from jax import lax
from jax.experimental import pallas as pl
from jax.experimental.pallas import tpu as pltpu
from jax.experimental.pallas import tpu_sc as plsc

F32, BF16 = jnp.float32, jnp.bfloat16

N_DEV = 8
D = 1024
NH, HD = 16, 64
NSTATE = 128
CHUNK = 128
HG = 8
DFF = 4096
ALPHA = 2.0 ** 0.25
EPS = 1e-5
ATT_SCALE = HD ** -0.5

OFF_Z, OFF_XS, OFF_Q, OFF_K, OFF_V, OFF_BC, OFF_DTF = 0, 1024, 2048, 3072, 4096, 5120, 5632
PCOLS = 5760
W_Z, W_XS, W_BC, W_DT, W_Q, W_K, W_V, W_F = 0, 1024, 2048, 2560, 2576, 3600, 4624, 5648
IN_COLS = 5664

ADAM_LR, ADAM_B1, ADAM_B2, ADAM_EPS, ADAM_WD, ADAM_STEP = 0.001, 0.9, 0.999, 1e-08, 0.01, 10

VMEM_LIMIT = 56 << 20

NN = (((1,), (0,)), ((), ()))
NT = (((1,), (1,)), ((), ()))
TN = (((0,), (0,)), ((), ()))


def _dot(a, b, dims=NN):
    return lax.dot_general(a, b, dims, preferred_element_type=F32)


def _bdot(a, b, dims=NN):
    return _dot(a.astype(BF16), b.astype(BF16), dims)


def _split3(v, terms=3):
    parts, rest = [], v
    for _ in range(terms):
        p = rest.astype(BF16)
        parts.append(p)
        rest = rest - p.astype(F32)
    return parts


def _sel_left(m01, v):
    return sum(_dot(m01, p) for p in _split3(v))


def _sel_right(v, m01, dims=NN, terms=3):
    return sum(_dot(p, m01, dims) for p in _split3(v, terms))


def _iota(shape, dim):
    return lax.broadcasted_iota(jnp.int32, shape, dim)


def _tri_lower(n):
    return (_iota((n, n), 1) <= _iota((n, n), 0)).astype(BF16)


def _tri_upper(n):
    return (_iota((n, n), 1) >= _iota((n, n), 0)).astype(BF16)


def _head_expand():
    return (lax.shift_right_logical(_iota((128, D), 1), 6) == _iota((128, D), 0)).astype(BF16)


def _head_reduce():
    return (lax.shift_right_logical(_iota((D, 128), 0), 6) == _iota((D, 128), 1)).astype(BF16)


def _sigmoid(x):
    return 1.0 / (1.0 + jnp.exp(-x))


def _silu(x):
    return x * _sigmoid(x)


def _dsilu(x):
    s = _sigmoid(x)
    return s * (1.0 + x * (1.0 - s))


def _softplus(x):
    return jnp.maximum(x, 0.0) + jnp.log(1.0 + jnp.exp(-jnp.abs(x)))


def _log_sigmoid(x):
    return jnp.minimum(x, 0.0) - jnp.log(1.0 + jnp.exp(-jnp.abs(x)))


def _params(sem):
    return pltpu.CompilerParams(dimension_semantics=sem, vmem_limit_bytes=VMEM_LIMIT)


def _mm_nn(name, a, b, *, tm, tn, tk, out_dtype, pro=None, aux=(), epi=None):
    m, k_all = a.shape
    b_sharded = b.ndim == 3
    n = b.shape[0] * b.shape[2] if b_sharded else b.shape[1]
    assert not b_sharded or tn == b.shape[2]
    nk = k_all // tk
    n_aux = len(aux)
    b_spec = (pl.BlockSpec((None, tk, tn), lambda i, j, k: (j, k, 0)) if b_sharded
              else pl.BlockSpec((tk, tn), lambda i, j, k: (k, j)))

    def body(a_ref, b_ref, *rest):
        aux_refs, o_ref = rest[:n_aux], rest[n_aux]
        at = a_ref[...]
        if pro is not None:
            at = pro(at, *[r[...] for r in aux_refs])
        part = _bdot(at, b_ref[...])
        if nk == 1:
            o_ref[...] = (part if epi is None else epi(part)).astype(out_dtype)
            return
        assert epi is None
        acc_ref = rest[n_aux + 1]
        kk = pl.program_id(2)

        @pl.when(kk == 0)
        def _():
            acc_ref[...] = part

        @pl.when(kk > 0)
        def _():
            acc_ref[...] += part

        @pl.when(kk == nk - 1)
        def _():
            o_ref[...] = acc_ref[...].astype(out_dtype)

    return pl.pallas_call(
        body, name=name,
        grid=(m // tm, n // tn, nk),
        in_specs=[pl.BlockSpec((tm, tk), lambda i, j, k: (i, k)), b_spec]
        + [pl.BlockSpec((1, tk), lambda i, j, k: (0, k)) for _ in aux],
        out_specs=pl.BlockSpec((tm, tn), lambda i, j, k: (i, j)),
        out_shape=jax.ShapeDtypeStruct((m, n), out_dtype),
        scratch_shapes=[] if nk == 1 else [pltpu.VMEM((tm, tn), F32)],
        compiler_params=_params(("parallel", "parallel", "arbitrary")),
    )(a, b, *aux)


def _mm_nt(name, a_list, b_list, *, n, tm, tn, out_dtype, epi=None, epi_aux=(), b_rows=False):
    m = a_list[0][0].shape[0]
    n_op = len(a_list)
    n_epi = len(epi_aux)
    dims = NN if b_rows else NT

    def body(*refs):
        a_refs, b_refs = refs[:n_op], refs[n_op:2 * n_op]
        e_refs, o_ref = refs[2 * n_op:2 * n_op + n_epi], refs[2 * n_op + n_epi]
        acc = None
        for a_ref, b_ref in zip(a_refs, b_refs):
            part = _bdot(a_ref[...], b_ref[...], dims)
            acc = part if acc is None else acc + part
        if epi is not None:
            acc = epi(acc, *[r[...] for r in e_refs])
        o_ref[...] = acc.astype(out_dtype)

    in_specs = [pl.BlockSpec((tm, w), functools.partial(lambda i, j, cb: (i, cb), cb=cb)) for (_, w, cb) in a_list]
    for (b, w, cb) in b_list:
        if b_rows:
            in_specs.append(pl.BlockSpec((w, tn), functools.partial(lambda i, j, cb: (cb, j), cb=cb)))
        elif b.ndim == 3:
            in_specs.append(pl.BlockSpec((None, tn, w), functools.partial(lambda i, j, cb: (cb, j, 0), cb=cb)))
        else:
            in_specs.append(pl.BlockSpec((tn, w), functools.partial(lambda i, j, cb: (j, cb), cb=cb)))
    in_specs += [pl.BlockSpec((tm, tn), lambda i, j: (i, j)) for _ in epi_aux]
    return pl.pallas_call(
        body, name=name,
        grid=(m // tm, n // tn),
        in_specs=in_specs,
        out_specs=pl.BlockSpec((tm, tn), lambda i, j: (i, j)),
        out_shape=jax.ShapeDtypeStruct((m, n), out_dtype),
        compiler_params=_params(("parallel", "parallel")),
    )(*[a for (a, _, _) in a_list], *[b for (b, _, _) in b_list], *epi_aux)


def _mm_tn(name, a, b, *, tm, tn, ts, pro=None, aux=(), col_shards=False):
    s_all, ka = a.shape
    nb = b.shape[1]
    n_aux = len(aux)
    ns = s_all // ts
    assert not col_shards or tn == nb // N_DEV

    def body(a_ref, b_ref, *rest):
        aux_refs, o_ref, acc_ref = rest[:n_aux], rest[n_aux], rest[n_aux + 1]
        at = a_ref[...]
        if pro is not None:
            at = pro(at, *[r[...] for r in aux_refs])
        part = _bdot(at, b_ref[...], TN)
        ss = pl.program_id(2)

        @pl.when(ss == 0)
        def _():
            acc_ref[...] = part

        @pl.when(ss > 0)
        def _():
            acc_ref[...] += part

        @pl.when(ss == ns - 1)
        def _():
            o_ref[...] = acc_ref[...].astype(BF16)

    if col_shards:
        out_spec = pl.BlockSpec((None, tm, tn), lambda i, j, s: (j, i, 0))
        out_shape = jax.ShapeDtypeStruct((N_DEV, ka, tn), BF16)
    else:
        out_spec = pl.BlockSpec((tm, tn), lambda i, j, s: (i, j))
        out_shape = jax.ShapeDtypeStruct((ka, nb), BF16)
    return pl.pallas_call(
        body, name=name,
        grid=(ka // tm, nb // tn, ns),
        in_specs=[pl.BlockSpec((ts, tm), lambda i, j, s: (s, i)),
                  pl.BlockSpec((ts, tn), lambda i, j, s: (s, j))]
        + [pl.BlockSpec((1, tm), lambda i, j, s: (0, i)) for _ in aux],
        out_specs=out_spec, out_shape=out_shape,
        scratch_shapes=[pltpu.VMEM((tm, tn), F32)],
        compiler_params=_params(("parallel", "parallel", "arbitrary")),
    )(a, b, *aux)


def _rowk(name, fn, n_rows, tr, rows, fulls, outs, accs, reverse=False):
    n = n_rows // tr
    n_row, n_full, n_out, n_acc = len(rows), len(fulls), len(outs), len(accs)

    def pos(i):
        return (n - 1 - i) if reverse else i

    def body(*refs):
        row_refs = refs[:n_row]
        full_refs = refs[n_row:n_row + n_full]
        out_refs = refs[n_row + n_full:n_row + n_full + n_out]
        acc_refs = refs[n_row + n_full + n_out:]
        i = pl.program_id(0)

        @pl.when(i == 0)
        def _():
            for r in acc_refs:
                r[...] = jnp.zeros(r.shape, r.dtype)

        res = fn(pos(i), *[r[...] for r in row_refs], *[r[...] for r in full_refs], *[r[...] for r in acc_refs])
        for r, v in zip(out_refs + acc_refs, res):
            r[...] = v.astype(r.dtype)

    def row_map(i, cb, shift):
        return (jnp.clip(pos(i) + shift, 0, n - 1), cb)

    def halo_map(i, cb, shift):
        tile = jnp.clip(pos(i) + shift, 0, n - 1)
        return (tile * (tr // 8) + (tr // 8 - 1 if shift < 0 else 0), cb)

    in_specs = [pl.BlockSpec((tr, w), functools.partial(row_map, cb=cb, shift=sh)) if sh == 0 else
                pl.BlockSpec((8, w), functools.partial(halo_map, cb=cb, shift=sh)) for (_, w, cb, sh) in rows]
    in_specs += [pl.BlockSpec(f.shape, functools.partial(lambda i, nd: (0,) * nd, nd=f.ndim)) for f in fulls]
    out_specs = [pl.BlockSpec((tr, w), lambda i: (pos(i), 0)) for (w, _) in outs]
    out_specs += [pl.BlockSpec((r, w), lambda i: (0, 0)) for (r, w) in accs]
    out_shape = [jax.ShapeDtypeStruct((n_rows, w), dt) for (w, dt) in outs]
    out_shape += [jax.ShapeDtypeStruct((r, w), F32) for (r, w) in accs]
    return pl.pallas_call(
        body, name=name, grid=(n,), in_specs=in_specs, out_specs=out_specs, out_shape=out_shape,
        compiler_params=_params(("arbitrary",)),
    )(*[a for (a, _, _, _) in rows], *fulls)


def _colsum(x):
    return jnp.sum(x, axis=0, keepdims=True)


def _mean(x):
    return jnp.mean(x, axis=-1, keepdims=True)


def _modulate(x, sc, sh):
    return x * (1.0 + sc) + sh


def _shift_down(cur, prev8, j):
    tr = cur.shape[0]
    row8 = _iota(prev8.shape, 0)
    head = jnp.where(row8 < j, pltpu.roll(prev8, j, 0), pltpu.roll(cur[0:8], j, 0))
    return head if tr == 8 else jnp.concatenate([head, pltpu.roll(cur, j, 0)[8:]], axis=0)


def _shift_up(cur, next8, j):
    tr = cur.shape[0]
    row8 = _iota(next8.shape, 0)
    tail = jnp.where(row8 < 8 - j, pltpu.roll(cur[tr - 8:], 8 - j, 0), pltpu.roll(next8, 8 - j, 0))
    return jnp.concatenate([pltpu.roll(cur, tr - j, 0)[:tr - 8], tail], axis=0)


def _conv(cur, prev, w, b):
    out = cur * w[3:4] + b
    for j in (1, 2, 3):
        out = out + _shift_down(cur, prev, j) * w[3 - j:4 - j]
    return out


def _conv_fwd(p, w_xs, b_xs, w_bc, b_bc, s):
    def fn(pos, xs, xs_prev, bc, bc_prev, w_xs, b_xs, w_bc, b_bc):
        first = pos == 0
        xs_prev = jnp.where(first, 0.0, xs_prev)
        bc_prev = jnp.where(first, 0.0, bc_prev)
        return _silu(_conv(xs, xs_prev, w_xs, b_xs)), _silu(_conv(bc, bc_prev, w_bc, b_bc))

    return _rowk("conv_fwd", fn, s, 256,
                 [(p, D, OFF_XS // D, 0), (p, D, OFF_XS // D, -1), (p, 512, OFF_BC // 512, 0), (p, 512, OFF_BC // 512, -1)],
                 [w_xs, b_xs, w_bc, b_bc], [(D, F32), (512, F32)], [])


def _conv_bwd(dxs_a, dbc_a, p, w_xs, b_xs, w_bc, b_bc, s):
    tr = 256
    n = s // tr

    def fn(pos, da1, da1n, x1, x1p, x1n, da2, da2n, x2, x2p, x2n, w1, b1, w2, b2, aw1, ab1, aw2, ab2):
        dx1, dw1, db1 = _conv_bwd_fn(pos, n, da1, da1n, x1, x1p, x1n, w1, b1)
        dx2, dw2, db2 = _conv_bwd_fn(pos, n, da2, da2n, x2, x2p, x2n, w2, b2)
        return dx1, dx2, aw1 + dw1, ab1 + db1, aw2 + dw2, ab2 + db2

    cx, cb = OFF_XS // D, OFF_BC // 512
    return _rowk("conv_bwd", fn, s, tr,
                 [(dxs_a, D, 0, 0), (dxs_a, D, 0, 1), (p, D, cx, 0), (p, D, cx, -1), (p, D, cx, 1),
                  (dbc_a, 512, 0, 0), (dbc_a, 512, 0, 1), (p, 512, cb, 0), (p, 512, cb, -1), (p, 512, cb, 1)],
                 [w_xs, b_xs, w_bc, b_bc], [(D, BF16), (512, BF16)], [(8, D), (1, D), (8, 512), (1, 512)])


def _conv_bwd_fn(pos, n, da, da_next, x, x_prev, x_next, w, b):
    first, last = pos == 0, pos == n - 1
    x_prev = jnp.where(first, 0.0, x_prev)
    shifted = {j: _shift_down(x, x_prev, j) for j in (1, 2, 3)}
    conv = x * w[3:4] + b
    for j in (1, 2, 3):
        conv = conv + shifted[j] * w[3 - j:4 - j]
    dc = da * _dsilu(conv)
    dc_next = jnp.where(last, 0.0, da_next * _dsilu(_conv(x_next, x[x.shape[0] - 8:], w, b)))
    dx = dc * w[3:4]
    dws = [None] * 4
    dws[3] = _colsum(dc * x)
    for j in (1, 2, 3):
        dx = dx + _shift_up(dc, dc_next, j) * w[3 - j:4 - j]
        dws[3 - j] = _colsum(dc * shifted[j])
    row = _iota((8, x.shape[1]), 0)
    dw = jnp.zeros((8, x.shape[1]), F32)
    for k in range(4):
        dw = jnp.where(row == k, dws[k], dw)
    return dx, dw, _colsum(dc)


def _ssd_gates(dtf, bias, a_log):
    lane = _iota(dtf.shape, 1)
    head = lane < NH
    dt = jnp.where(head, _softplus(dtf + bias), 0.0)
    a_neg = jnp.where(_iota(a_log.shape, 1) < NH, -jnp.exp(a_log), 0.0)
    a = dt * a_neg
    cs = _sel_left(_tri_lower(CHUNK), a)
    return dt, a_neg, cs


def _decay_mask(cs_ref, cst_ref, h):
    diff = cs_ref[:, h:h + 1] - cst_ref[h:h + 1, :]
    low = _iota((CHUNK, CHUNK), 1) <= _iota((CHUNK, CHUNK), 0)
    return jnp.where(low, jnp.exp(jnp.minimum(diff, 0.0)), 0.0)


def _ssd_fwd(xs_a, bc_a, p, bias128, alog128, dskip_x, s):
    nc = s // CHUNK
    t = CHUNK

    def body(xs_ref, bc_ref, dtf_ref, bias_ref, alog_ref, dsk_ref, y_ref, st_ref,
             state, x_sc, xw_sc, cs_sc, cst_sc, yd_sc):
        c = pl.program_id(0)

        @pl.when(c == 0)
        def _():
            state[...] = jnp.zeros(state.shape, F32)

        dt, _, cs = _ssd_gates(dtf_ref[...], bias_ref[...], alog_ref[...])
        cs_sc[...] = cs
        cst_sc[...] = cs.T
        cs_last = cs[t - 1:t, :]
        expand = _head_expand()
        ex = _sel_right(jnp.concatenate([dt, jnp.exp(cs), jnp.exp(cs_last - cs)], axis=0), expand, terms=2)
        dt_x, eo_x, we_x = ex[0:t], ex[t:2 * t], ex[2 * t:3 * t]
        g_x = _sel_right(jnp.broadcast_to(jnp.exp(cs_last), (8, 128)), expand)[0:1]
        xs = xs_ref[...]
        x = xs * dt_x
        x_sc[...] = x.astype(BF16)
        xw_sc[...] = (x * we_x).astype(BF16)
        prev = state[...]
        st_ref[0] = prev
        prev_b = prev.astype(BF16)
        for g in range(2):
            cols = slice(g * 512, (g + 1) * 512)
            b_g = bc_ref[:, g * 128:(g + 1) * 128].astype(BF16)
            c_g = bc_ref[:, 256 + g * 128:256 + (g + 1) * 128].astype(BF16)
            gmat = _dot(c_g, b_g, NT)
            y_off = _dot(c_g, prev_b[:, cols]) * eo_x[:, cols]
            s_loc = _dot(b_g, xw_sc[:, cols], TN)
            state[:, cols] = g_x[:, cols] * prev[:, cols] + s_loc
            for e in range(HG):
                h = g * HG + e
                m = gmat * _decay_mask(cs_sc, cst_sc, h)
                yd_sc[:, h * HD:(h + 1) * HD] = _dot(m.astype(BF16), x_sc[:, h * HD:(h + 1) * HD])
            y_ref[:, cols] = yd_sc[:, cols] + y_off + dsk_ref[:, cols] * xs[:, cols]

    return pl.pallas_call(
        body, name="ssd_fwd", grid=(nc,),
        in_specs=[pl.BlockSpec((t, D), lambda c: (c, 0)),
                  pl.BlockSpec((t, 512), lambda c: (c, 0)),
                  pl.BlockSpec((t, 128), lambda c: (c, OFF_DTF // 128)),
                  pl.BlockSpec((1, 128), lambda c: (0, 0)),
                  pl.BlockSpec((1, 128), lambda c: (0, 0)),
                  pl.BlockSpec((1, D), lambda c: (0, 0))],
        out_specs=[pl.BlockSpec((t, D), lambda c: (c, 0)),
                   pl.BlockSpec((1, NSTATE, D), lambda c: (c, 0, 0))],
        out_shape=[jax.ShapeDtypeStruct((s, D), F32), jax.ShapeDtypeStruct((nc, NSTATE, D), F32)],
        scratch_shapes=[pltpu.VMEM((NSTATE, D), F32), pltpu.VMEM((t, D), BF16), pltpu.VMEM((t, D), BF16),
                        pltpu.VMEM((t, 128), F32), pltpu.VMEM((128, t), F32), pltpu.VMEM((t, D), F32)],
        compiler_params=_params(("arbitrary",)),
    )(xs_a, bc_a, p, bias128, alog128, dskip_x)


def _ssd_bwd(dy, xs_a, bc_a, p, states, bias128, alog128, dskip_x, s):
    nc = s // CHUNK
    t = CHUNK

    def body(dy_ref, xs_ref, bc_ref, dtf_ref, st_ref, bias_ref, alog_ref, dsk_ref,
             dxs_ref, dbc_ref, ddt_ref, dalog_ref, dskip_ref,
             dstate, x_sc, dy_sc, dx_sc, deo_sc, dwe_sc, cs_sc, cst_sc, dcol_sc, drow_sc):
        i = pl.program_id(0)

        @pl.when(i == 0)
        def _():
            dstate[...] = jnp.zeros(dstate.shape, F32)
            dalog_ref[...] = jnp.zeros(dalog_ref.shape, F32)
            dskip_ref[...] = jnp.zeros(dskip_ref.shape, F32)

        dtf = dtf_ref[...]
        dt, a_neg, cs = _ssd_gates(dtf, bias_ref[...], alog_ref[...])
        cs_sc[...] = cs
        cst_sc[...] = cs.T
        cs_last = cs[t - 1:t, :]
        eo, we, g_end = jnp.exp(cs), jnp.exp(cs_last - cs), jnp.exp(cs_last)
        expand, reduce = _head_expand(), _head_reduce()
        ex = _sel_right(jnp.concatenate([dt, eo, we], axis=0), expand, terms=2)
        dt_x, eo_x, we_x = ex[0:t], ex[t:2 * t], ex[2 * t:3 * t]
        g_x = _sel_right(jnp.broadcast_to(g_end, (8, 128)), expand)[0:1]
        xs = xs_ref[...]
        dyv = dy_ref[...]
        x = xs * dt_x
        x_sc[...] = x.astype(BF16)
        dy_sc[...] = dyv.astype(BF16)
        dyo_b = (dyv * eo_x).astype(BF16)
        xw_b = (x * we_x).astype(BF16)
        prev = st_ref[0]
        prev_b = prev.astype(BF16)
        dnext = dstate[...]
        dnext_b = dnext.astype(BF16)
        dcol_sc[...] = jnp.zeros(dcol_sc.shape, F32)
        drow_sc[...] = jnp.zeros(drow_sc.shape, F32)
        lane_row = _iota((1, 128), 1)
        sub_col = _iota((128, 1), 0)
        for g in range(2):
            cols = slice(g * 512, (g + 1) * 512)
            b_g = bc_ref[:, g * 128:(g + 1) * 128].astype(BF16)
            c_g = bc_ref[:, 256 + g * 128:256 + (g + 1) * 128].astype(BF16)
            gmat = _dot(c_g, b_g, NT)
            b_ds = _dot(b_g, dnext_b[:, cols])
            c_s = _dot(c_g, prev_b[:, cols])
            dx_sc[:, cols] = b_ds * we_x[:, cols]
            deo_sc[:, cols] = dyv[:, cols] * c_s
            dwe_sc[:, cols] = b_ds * x[:, cols]
            db = _dot(xw_b[:, cols], dnext_b[:, cols], NT)
            dc = _dot(dyo_b[:, cols], prev_b[:, cols], NT)
            dstate[:, cols] = g_x[:, cols] * dnext[:, cols] + _dot(c_g, dyo_b[:, cols], TN)
            dg = jnp.zeros((t, t), F32)
            for e in range(HG):
                h = g * HG + e
                hc = slice(h * HD, (h + 1) * HD)
                lmat = _decay_mask(cs_sc, cst_sc, h)
                m = gmat * lmat
                dx_sc[:, hc] += _dot(m.astype(BF16), dy_sc[:, hc], TN)
                dm = _dot(dy_sc[:, hc], x_sc[:, hc], NT)
                dg = dg + dm * lmat
                qm = dm * m
                dcol_sc[...] += jnp.sum(qm, axis=1, keepdims=True) * (lane_row == h).astype(F32)
                drow_sc[...] += (sub_col == h).astype(F32) * jnp.sum(qm, axis=0, keepdims=True)
            dg_b = dg.astype(BF16)
            dbc_ref[:, g * 128:(g + 1) * 128] = db + _dot(dg_b, c_g, TN)
            dbc_ref[:, 256 + g * 128:256 + (g + 1) * 128] = dc + _dot(dg_b, b_g)
        d_eo = _sel_right(deo_sc[...], reduce, terms=2)
        d_we = _sel_right(dwe_sc[...], reduce, terms=2)
        d_gend = _sel_right(jnp.broadcast_to(_colsum(dnext * prev), (8, D)), reduce)[0:1]
        d_cs = dcol_sc[...] - drow_sc[...].T + d_eo * eo - d_we * we
        extra = _colsum(d_we * we) + d_gend * g_end
        d_cs = d_cs + jnp.where(_iota((t, 128), 0) == t - 1, extra, 0.0)
        da = _sel_left(_tri_upper(t), d_cs)
        dx = dx_sc[...]
        ddt = _sel_right(dx * xs, reduce, terms=2) + da * a_neg
        dxs_ref[...] = dx * dt_x + dsk_ref[...] * dyv
        ddt_ref[...] = jnp.where(_iota((t, 128), 1) < NH, ddt * _sigmoid(dtf + bias_ref[...]), 0.0)
        dalog_ref[...] += _colsum(da * dt) * a_neg
        dskip_ref[...] += _sel_right(jnp.broadcast_to(_colsum(dyv * xs), (8, D)), reduce)[0:1]

    rev = lambda i: nc - 1 - i
    return pl.pallas_call(
        body, name="ssd_bwd", grid=(nc,),
        in_specs=[pl.BlockSpec((t, D), lambda i: (rev(i), 0)),
                  pl.BlockSpec((t, D), lambda i: (rev(i), 0)),
                  pl.BlockSpec((t, 512), lambda i: (rev(i), 0)),
                  pl.BlockSpec((t, 128), lambda i: (rev(i), OFF_DTF // 128)),
                  pl.BlockSpec((1, NSTATE, D), lambda i: (rev(i), 0, 0)),
                  pl.BlockSpec((1, 128), lambda i: (0, 0)),
                  pl.BlockSpec((1, 128), lambda i: (0, 0)),
                  pl.BlockSpec((1, D), lambda i: (0, 0))],
        out_specs=[pl.BlockSpec((t, D), lambda i: (rev(i), 0)),
                   pl.BlockSpec((t, 512), lambda i: (rev(i), 0)),
                   pl.BlockSpec((t, 128), lambda i: (rev(i), 0)),
                   pl.BlockSpec((1, 128), lambda i: (0, 0)),
                   pl.BlockSpec((1, 128), lambda i: (0, 0))],
        out_shape=[jax.ShapeDtypeStruct((s, D), F32), jax.ShapeDtypeStruct((s, 512), F32),
                   jax.ShapeDtypeStruct((s, 128), F32), jax.ShapeDtypeStruct((1, 128), F32),
                   jax.ShapeDtypeStruct((1, 128), F32)],
        scratch_shapes=[pltpu.VMEM((NSTATE, D), F32), pltpu.VMEM((t, D), BF16), pltpu.VMEM((t, D), BF16),
                        pltpu.VMEM((t, D), F32), pltpu.VMEM((t, D), F32), pltpu.VMEM((t, D), F32),
                        pltpu.VMEM((t, 128), F32), pltpu.VMEM((128, t), F32),
                        pltpu.VMEM((t, 128), F32), pltpu.VMEM((128, t), F32)],
        compiler_params=_params(("arbitrary",)),
    )(dy, xs_a, bc_a, p, states, bias128, alog128, dskip_x)


def _gate_lanes(shape):
    lane = _iota(shape, 1)
    return (lane >= NH) & (lane < 2 * NH)


def _cum_fwd(p, bias128, s):
    tr = min(512, s)

    def body(dtf_ref, bias_ref, o_ref, carry):
        @pl.when(pl.program_id(0) == 0)
        def _():
            carry[...] = jnp.zeros(carry.shape, F32)

        lf = jnp.where(_gate_lanes((tr, 128)), _log_sigmoid(dtf_ref[...] + bias_ref[...]), 0.0)
        cum = _sel_left(_tri_lower(tr), lf) + carry[...]
        carry[...] = cum[tr - 1:tr, :]
        o_ref[...] = cum

    return pl.pallas_call(
        body, name="cum_fwd", grid=(s // tr,),
        in_specs=[pl.BlockSpec((tr, 128), lambda i: (i, OFF_DTF // 128)), pl.BlockSpec((1, 128), lambda i: (0, 0))],
        out_specs=pl.BlockSpec((tr, 128), lambda i: (i, 0)),
        out_shape=jax.ShapeDtypeStruct((s, 128), F32),
        scratch_shapes=[pltpu.VMEM((1, 128), F32)],
        compiler_params=_params(("arbitrary",)),
    )(p, bias128)


def _cum_bwd(dcum, ddt_raw, p, bias128, s):
    tr = min(512, s)

    def fn(pos, dcum, ddt, dtf, bias, carry, acc):
        suffix = _sel_left(_tri_upper(tr), dcum) + carry
        dfr = jnp.where(_gate_lanes((tr, 128)), suffix * _sigmoid(-(dtf + bias)), 0.0)
        out = ddt + dfr
        return out, suffix[0:1, :], acc + _colsum(out)

    return _rowk("cum_bwd", fn, s, tr, [(dcum, 128, 0, 0), (ddt_raw, 128, 0, 0), (p, 128, OFF_DTF // 128, 0)],
                 [bias128], [(128, BF16)], [(1, 128), (1, 128)], reverse=True)


ATT_BLOCK = 512
ATT_STRIP = 32


def _head_part(shape, h, dim):
    i = _iota(shape, dim)
    return (i >= h * HD) & (i < (h + 1) * HD)


def _k_augmented(k_blk, cum_blk, j, h):
    tk = k_blk.shape[0]
    lane = _iota((tk, 128), 1)
    col = jnp.sum(jnp.where(lane == NH + 2 * j + h, cum_blk, 0.0), axis=1, keepdims=True)
    c0, c1, c2 = [c.astype(F32) for c in _split3(-col)]
    k_h = k_blk if h == 0 else pltpu.roll(k_blk, HD, 1)
    aug = jnp.where(lane == HD, c0, jnp.where(lane == HD + 1, c1, jnp.where(lane == HD + 2, c2, 0.0)))
    return jnp.where(lane < HD, k_h, aug).astype(BF16)


def _q_augmented_t(q_blk):
    tq = q_blk.shape[0]
    q_t = (q_blk * ATT_SCALE).T.astype(BF16)
    ones = (_iota((HD, tq), 0) < 3).astype(BF16)
    return [jnp.concatenate([q_t[h * HD:(h + 1) * HD], ones], axis=0) for h in range(2)]


def _rows01(r0, r1):
    sub = _iota((8, r0.shape[1]), 0)
    return jnp.where(sub == 0, r0, jnp.where(sub == 1, r1, 0.0))


def _fold8(x, op, cur):
    for g in range(x.shape[0] // 8):
        cur = op(cur, x[8 * g:8 * (g + 1), :])
    return cur


def _attn_fwd(p, cum, s):
    t = min(ATT_BLOCK, s)
    nq = s // t
    r = ATT_STRIP

    def body(q_ref, k_ref, v_ref, c_ref, o_ref, lse_ref, kaug_sc, vt_sc, s0_sc, s1_sc, p0_sc, p1_sc, m_sc, l_sc, acc_sc):
        j, qi = pl.program_id(0), pl.program_id(1)
        s_sc, p_sc = (s0_sc, s1_sc), (p0_sc, p1_sc)

        @pl.when(qi == 0)
        def _():
            for c in range(nq):
                rows = slice(c * t, (c + 1) * t)
                k_blk, vt = k_ref[rows, :], v_ref[rows, :].T
                for h in range(2):
                    kaug_sc[h, rows, :] = _k_augmented(k_blk, c_ref[rows, :], j, h)
                    vt_sc[h, :, rows] = vt[h * HD:(h + 1) * HD].astype(BF16)

        qaug_t = _q_augmented_t(q_ref[...])
        m_sc[...] = jnp.full(m_sc.shape, -1e30, F32)
        l_sc[...] = jnp.zeros(l_sc.shape, F32)
        acc_sc[...] = jnp.zeros(acc_sc.shape, F32)
        top = _iota((128, t), 0) < HD

        def logits(kb, buf):
            kv = pl.ds(pl.multiple_of(kb * t, t), t)
            for h in range(2):
                s_sc[buf][h] = _dot(kaug_sc[h, kv, :], qaug_t[h])

        def softmax(buf, diagonal):
            alphas = []
            for h in range(2):
                cur = jnp.full((8, t), -1e30, F32)
                for i in range(t // r):
                    rows = slice(i * r, (i + 1) * r)
                    x = s_sc[buf][h, rows, :]
                    if diagonal:
                        x = jnp.where(_iota((r, t), 1) >= i * r + _iota((r, t), 0), x, -1e30)
                        s_sc[buf][h, rows, :] = x
                    cur = _fold8(x, jnp.maximum, cur)
                m_prev = m_sc[h, 0:1, :]
                m_new = jnp.maximum(m_prev, jnp.max(cur, axis=0, keepdims=True))
                alpha = jnp.exp(m_prev - m_new)
                m_sc[h, 0:1, :] = m_new
                alphas.append(alpha)
                tot = jnp.zeros((8, t), F32)
                for i in range(t // r):
                    rows = slice(i * r, (i + 1) * r)
                    pr = jnp.exp(s_sc[buf][h, rows, :] - m_new)
                    p_sc[buf][h, rows, :] = pr.astype(BF16)
                    tot = _fold8(pr, jnp.add, tot)
                l_sc[h, 0:1, :] = alpha * l_sc[h, 0:1, :] + jnp.sum(tot, axis=0, keepdims=True)
            return alphas

        def accumulate(kb, buf, alphas):
            kv = pl.ds(pl.multiple_of(kb * t, t), t)
            for h in range(2):
                part = slice(h * HD, (h + 1) * HD)
                acc_sc[part, :] = acc_sc[part, :] * alphas[h] + _dot(vt_sc[h, :, kv], p_sc[buf][h])

        def first_trip():
            logits(0, 1)
            accumulate(qi, 0, softmax(0, True))
            logits(jnp.minimum(1, qi - 1), 0)
            return tuple(softmax(1, False))

        def only_diagonal():
            accumulate(qi, 0, softmax(0, True))
            return (jnp.ones((1, t), F32),) * 2

        def steady(u, alphas_b):
            accumulate(2 * u - 2, 1, alphas_b)
            logits(2 * u, 1)
            accumulate(2 * u - 1, 0, softmax(0, False))
            logits(jnp.minimum(2 * u + 1, qi - 1), 0)
            return tuple(softmax(1, False))

        logits(qi, 0)
        n_blocks = qi + 1
        alphas_b = lax.cond(qi >= 1, first_trip, only_diagonal)
        alphas_b = lax.fori_loop(1, n_blocks // 2, steady, alphas_b)
        last_b = 2 * (n_blocks // 2) - 2

        @pl.when((qi >= 1) & (n_blocks % 2 == 0))
        def _():
            accumulate(last_b, 1, alphas_b)

        @pl.when((qi >= 2) & (n_blocks % 2 == 1))
        def _():
            accumulate(last_b, 1, alphas_b)
            accumulate(qi - 1, 0, softmax(0, False))

        l0, l1 = l_sc[0, 0:1, :], l_sc[1, 0:1, :]
        o_ref[...] = (acc_sc[...] / jnp.where(top, l0, l1)).T
        lse_ref[0] = _rows01(m_sc[0, 0:1, :] + jnp.log(l0), m_sc[1, 0:1, :] + jnp.log(l1))

    return pl.pallas_call(
        body, name="attn_fwd", grid=(NH // 2, nq),
        in_specs=[pl.BlockSpec((t, 128), lambda j, qi: (qi, OFF_Q // 128 + j)),
                  pl.BlockSpec((s, 128), lambda j, qi: (0, OFF_K // 128 + j)),
                  pl.BlockSpec((s, 128), lambda j, qi: (0, OFF_V // 128 + j)),
                  pl.BlockSpec((s, 128), lambda j, qi: (0, 0))],
        out_specs=[pl.BlockSpec((t, 128), lambda j, qi: (qi, j)),
                   pl.BlockSpec((1, 8, t), lambda j, qi: (j, 0, qi))],
        out_shape=[jax.ShapeDtypeStruct((s, D), F32), jax.ShapeDtypeStruct((NH // 2, 8, s), F32)],
        scratch_shapes=[pltpu.VMEM((2, s, 128), BF16), pltpu.VMEM((2, HD, s), BF16), pltpu.VMEM((2, t, t), F32),
                        pltpu.VMEM((2, t, t), F32), pltpu.VMEM((2, t, t), BF16), pltpu.VMEM((2, t, t), BF16),
                        pltpu.VMEM((2, 8, t), F32), pltpu.VMEM((2, 8, t), F32), pltpu.VMEM((128, t), F32)],
        compiler_params=_params(("parallel", "arbitrary")),
    )(p, p, p, cum)


def _attn_bwd(p, cum, o, lse, do, s):
    t = min(ATT_BLOCK, s)
    nq = s // t
    r = ATT_STRIP

    def body(q_ref, k_ref, v_ref, c_ref, o_ref, lse_ref, do_ref, dq_ref, dk_ref, dv_ref, dc_ref, dr_ref,
             qaugt_sc, qh_sc, dot_sc, doh_sc, delta_sc, dqt_sc, dr_sc, kaug_sc, vh_sc, kt_sc,
             s0_sc, s1_sc, dp0_sc, dp1_sc, p0_sc, p1_sc, ds0_sc, ds1_sc, dk_sc, dv_sc, dc_sc):
        j, ki = pl.program_id(0), pl.program_id(1)
        s_sc, dp_sc, p_sc, ds_sc = (s0_sc, s1_sc), (dp0_sc, dp1_sc), (p0_sc, p1_sc), (ds0_sc, ds1_sc)

        @pl.when(ki == 0)
        def _():
            for c in range(nq):
                rows = slice(c * t, (c + 1) * t)
                q_blk, do_blk = q_ref[rows, :], do_ref[rows, :]
                qaugt_sc[0, :, rows], qaugt_sc[1, :, rows] = _q_augmented_t(q_blk)
                dot_sc[:, rows] = do_blk.T.astype(BF16)
                prod_t = (do_blk * o_ref[rows, :]).T
                delta_sc[:, rows] = _rows01(jnp.sum(prod_t[0:HD], axis=0, keepdims=True),
                                            jnp.sum(prod_t[HD:], axis=0, keepdims=True))
                for h in range(2):
                    head = _head_part((t, 128), h, 1)
                    qh_sc[h, rows, :] = jnp.where(head, q_blk * ATT_SCALE, 0.0).astype(BF16)
                    doh_sc[h, rows, :] = jnp.where(head, do_blk, 0.0).astype(BF16)
            dqt_sc[...] = jnp.zeros(dqt_sc.shape, F32)
            dr_sc[...] = jnp.zeros(dr_sc.shape, F32)

        k_blk, v_blk = k_ref[...], v_ref[...]
        kt = k_blk.T
        for h in range(2):
            kaug_sc[h] = _k_augmented(k_blk, c_ref[...], j, h)
            vh_sc[h] = jnp.where(_head_part((t, 128), h, 1), v_blk, 0.0).astype(BF16)
            kt_sc[h] = kt[h * HD:(h + 1) * HD].astype(BF16)
        dk_sc[...] = jnp.zeros(dk_sc.shape, F32)
        dv_sc[...] = jnp.zeros(dv_sc.shape, F32)
        dc_sc[...] = jnp.zeros(dc_sc.shape, F32)

        def inputs(qb, buf):
            qs = pl.ds(pl.multiple_of(qb * t, t), t)
            for h in range(2):
                s_sc[buf][h] = _dot(kaug_sc[h], qaugt_sc[h, :, qs])
                dp_sc[buf][h] = _dot(vh_sc[h], dot_sc[:, qs])

        def elementwise(qb, buf, diagonal):
            qs = pl.ds(pl.multiple_of(qb * t, t), t)
            for h in range(2):
                lse_row, delta_row = lse_ref[0, h:h + 1, qs], delta_sc[h:h + 1, qs]
                tot = jnp.zeros((8, t), F32)
                for i in range(t // r):
                    rows = slice(i * r, (i + 1) * r)
                    x = s_sc[buf][h, rows, :]
                    if diagonal:
                        x = jnp.where(_iota((r, t), 1) >= i * r + _iota((r, t), 0), x, -1e30)
                    pr = jnp.exp(x - lse_row)
                    ds = pr * (dp_sc[buf][h, rows, :] - delta_row)
                    p_sc[buf][h, rows, :] = pr.astype(BF16)
                    ds_sc[buf][h, rows, :] = ds.astype(BF16)
                    dc_sc[h, rows, :] += sum(ds[:, 128 * g:128 * (g + 1)] for g in range(t // 128))
                    tot = _fold8(ds, jnp.add, tot)
                dr_sc[h, :, qs] += tot

        def outputs(qb, buf):
            qs = pl.ds(pl.multiple_of(qb * t, t), t)
            dv_sc[...] += _dot(p_sc[buf][0], doh_sc[0, qs, :]) + _dot(p_sc[buf][1], doh_sc[1, qs, :])
            dk_sc[...] += _dot(ds_sc[buf][0], qh_sc[0, qs, :]) + _dot(ds_sc[buf][1], qh_sc[1, qs, :])
            for h in range(2):
                dqt_sc[h * HD:(h + 1) * HD, qs] += _dot(kt_sc[h], ds_sc[buf][h])

        def pair(a, b, a_diagonal):
            inputs(a, 0)
            inputs(b, 1)
            elementwise(a, 0, a_diagonal)
            outputs(a, 0)
            elementwise(b, 1, False)
            outputs(b, 1)

        def later(u, carry):
            pair(ki + 1 + 2 * u, ki + 2 + 2 * u, False)
            return carry

        n_later = nq - 1 - ki
        lax.fori_loop(0, n_later // 2, later, 0)

        @pl.when(n_later % 2 == 1)
        def _():
            pair(ki, nq - 1, True)

        @pl.when(n_later % 2 == 0)
        def _():
            inputs(ki, 0)
            elementwise(ki, 0, True)
            outputs(ki, 0)

        dk_ref[...] = dk_sc[...].astype(BF16)
        dv_ref[...] = dv_sc[...].astype(BF16)
        lane = _iota((t, 128), 1)
        cols = jnp.where(lane == 0, jnp.sum(dc_sc[0], axis=1, keepdims=True),
                         jnp.where(lane == 1, jnp.sum(dc_sc[1], axis=1, keepdims=True), 0.0))
        dc_ref[0] = cols.T[0:8, :]

        @pl.when(ki == nq - 1)
        def _():
            for c in range(nq):
                rows = slice(c * t, (c + 1) * t)
                dq_ref[rows, :] = dqt_sc[:, rows].T * ATT_SCALE
            dr_ref[0] = _rows01(jnp.sum(dr_sc[0], axis=0, keepdims=True), jnp.sum(dr_sc[1], axis=0, keepdims=True))

    whole = lambda off: pl.BlockSpec((s, 128), functools.partial(lambda j, ki, off: (0, off + j), off=off))
    return pl.pallas_call(
        body, name="attn_bwd", grid=(NH // 2, nq),
        in_specs=[whole(OFF_Q // 128),
                  pl.BlockSpec((t, 128), lambda j, ki: (ki, OFF_K // 128 + j)),
                  pl.BlockSpec((t, 128), lambda j, ki: (ki, OFF_V // 128 + j)),
                  pl.BlockSpec((t, 128), lambda j, ki: (ki, 0)),
                  whole(0),
                  pl.BlockSpec((1, 8, s), lambda j, ki: (j, 0, 0)),
                  whole(0)],
        out_specs=[whole(0),
                   pl.BlockSpec((t, 128), lambda j, ki: (ki, j)),
                   pl.BlockSpec((t, 128), lambda j, ki: (ki, j)),
                   pl.BlockSpec((1, 8, t), lambda j, ki: (j, 0, ki)),
                   pl.BlockSpec((1, 8, s), lambda j, ki: (j, 0, 0))],
        out_shape=[jax.ShapeDtypeStruct((s, D), F32), jax.ShapeDtypeStruct((s, D), BF16), jax.ShapeDtypeStruct((s, D), BF16),
                   jax.ShapeDtypeStruct((NH // 2, 8, s), F32), jax.ShapeDtypeStruct((NH // 2, 8, s), F32)],
        scratch_shapes=[pltpu.VMEM((2, 128, s), BF16), pltpu.VMEM((2, s, 128), BF16), pltpu.VMEM((128, s), BF16),
                        pltpu.VMEM((2, s, 128), BF16), pltpu.VMEM((8, s), F32), pltpu.VMEM((128, s), F32),
                        pltpu.VMEM((2, 8, s), F32), pltpu.VMEM((2, t, 128), BF16), pltpu.VMEM((2, t, 128), BF16),
                        pltpu.VMEM((2, HD, t), BF16)]
        + [pltpu.VMEM((2, t, t), F32)] * 4 + [pltpu.VMEM((2, t, t), BF16)] * 4
        + [pltpu.VMEM((t, 128), F32), pltpu.VMEM((t, 128), F32), pltpu.VMEM((2, t, 128), F32)],
        compiler_params=_params(("parallel", "arbitrary")),
    )(p, p, p, cum, o, lse, do)


def _ln_stats(u):
    mu = _mean(u)
    d = u - mu
    rstd = lax.rsqrt(_mean(d * d) + EPS)
    return d * rstd, rstd


def _ln_bwd(dx, xh, rstd, gam):
    dxh = dx * gam
    return rstd * (dxh - _mean(dxh) - xh * _mean(dxh * xh))


def _rms_bwd(d, xn, r, w):
    t = d * w
    return r * (t - xn * _mean(t * xn)), _colsum(d * xn)


def _mix_norm(y, p, att, w_ssm, w_att, s):
    def fn(pos, y, z, att, w1, w2):
        g = y * _silu(z)
        n1 = g * lax.rsqrt(_mean(g * g) + EPS) * w1
        n2 = att * lax.rsqrt(_mean(att * att) + EPS) * w2
        return (jnp.concatenate([n1, n2], axis=1),)

    return _rowk("mix_norm", fn, s, 512, [(y, D, 0, 0), (p, D, OFF_Z // D, 0), (att, D, 0, 0)],
                 [w_ssm, w_att], [(2 * D, BF16)], [])[0]


def _mix_norm_bwd(dmix, y, p, att, w_ssm, w_att, s):
    def fn(pos, dmix, y, z, att, w1, w2, a1, a2):
        sz = _silu(z)
        g = y * sz
        r1 = lax.rsqrt(_mean(g * g) + EPS)
        dg, dw1 = _rms_bwd(dmix[:, :D], g * r1, r1, w1)
        r2 = lax.rsqrt(_mean(att * att) + EPS)
        datt, dw2 = _rms_bwd(dmix[:, D:], att * r2, r2, w2)
        return dg * sz, dg * y * _dsilu(z), datt, a1 + dw1, a2 + dw2

    return _rowk("mix_norm_bwd", fn, s, 256, [(dmix, 2 * D, 0, 0), (y, D, 0, 0), (p, D, OFF_Z // D, 0), (att, D, 0, 0)],
                 [w_ssm, w_att], [(D, F32), (D, BF16), (D, F32)], [(1, D), (1, D)])


def _ln1(x0, y, g1, gam, bet, sc2, sh2, s):
    def fn(pos, x0, y, g1, gam, bet, sc2, sh2):
        xh, _ = _ln_stats(ALPHA * x0 + (1.0 + g1) * y)
        x1 = xh * gam + bet
        return x1, _modulate(x1, sc2, sh2)

    return _rowk("ln1", fn, s, 512, [(x0, D, 0, 0), (y, D, 0, 0)], [g1, gam, bet, sc2, sh2], [(D, F32), (D, BF16)], [])


def _ln2_loss(x1, ff, tgt, g2, gam, bet, s):
    def fn(pos, x1, ff, tgt, g2, gam, bet, a_loss, a_dgam, a_dbet, a_dg2):
        xh, rstd = _ln_stats(ALPHA * x1 + (1.0 + g2) * ff)
        err = xh * gam + bet - tgt
        dx2 = err * (1.0 / D)
        du = _ln_bwd(dx2, xh, rstd, gam)
        return (du, du * (1.0 + g2), a_loss + _colsum(err * err), a_dgam + _colsum(dx2 * xh),
                a_dbet + _colsum(dx2), a_dg2 + _colsum(du * ff))

    return _rowk("ln2_loss", fn, s, 512, [(x1, D, 0, 0), (ff, D, 0, 0), (tgt, D, 0, 0)], [g2, gam, bet],
                 [(D, F32), (D, BF16)], [(1, D)] * 4)


def _ln1_bwd(dh2, du2, x0, y, g1, gam, bet, sc2, s):
    def fn(pos, dh2, du2, x0, y, g1, gam, bet, sc2, a_sc, a_sh, a_gam, a_bet, a_g1):
        xh, rstd = _ln_stats(ALPHA * x0 + (1.0 + g1) * y)
        x1 = xh * gam + bet
        dx1 = ALPHA * du2 + dh2 * (1.0 + sc2)
        du1 = _ln_bwd(dx1, xh, rstd, gam)
        return (du1, du1 * (1.0 + g1), a_sc + _colsum(dh2 * x1), a_sh + _colsum(dh2), a_gam + _colsum(dx1 * xh),
                a_bet + _colsum(dx1), a_g1 + _colsum(du1 * y))

    return _rowk("ln1_bwd", fn, s, 512, [(dh2, D, 0, 0), (du2, D, 0, 0), (x0, D, 0, 0), (y, D, 0, 0)],
                 [g1, gam, bet, sc2], [(D, F32), (D, BF16)], [(1, D)] * 5)


def _input_grad(dh1, du1, x0, sc1, s):
    def fn(pos, dh1, du1, x0, sc1, a_sc, a_sh):
        return ALPHA * du1 + dh1 * (1.0 + sc1), a_sc + _colsum(dh1 * x0), a_sh + _colsum(dh1)

    return _rowk("input_grad", fn, s, 512, [(dh1, D, 0, 0), (du1, D, 0, 0), (x0, D, 0, 0)], [sc1],
                 [(D, F32)], [(1, D)] * 2)


def _adamw_math(w, grad, m, v):
    m_new = ADAM_B1 * m + (1.0 - ADAM_B1) * grad
    v_new = ADAM_B2 * v + (1.0 - ADAM_B2) * (grad * grad)
    m_hat = m_new / (1.0 - ADAM_B1 ** ADAM_STEP)
    v_hat = v_new / (1.0 - ADAM_B2 ** ADAM_STEP)
    return -ADAM_LR * (m_hat / (jnp.sqrt(v_hat) + ADAM_EPS) + ADAM_WD * w), m_new, v_new


def _small_update(small_all, layout, w, m, v):
    names = [n for n, _, _ in layout]

    def body(*refs):
        all_ref = refs[0]
        w_refs, m_refs, v_refs = [refs[1 + k * len(names):1 + (k + 1) * len(names)] for k in range(3)]
        sum_ref = refs[1 + 3 * len(names)]
        outs = refs[2 + 3 * len(names):]
        total = all_ref[0]
        for k in range(1, N_DEV):
            total = total + all_ref[k]
        sum_ref[...] = total
        for i, (_, off, size) in enumerate(layout):
            grad = total[:, off:off + size]
            delta, m_new, v_new = _adamw_math(w_refs[i][...], grad, m_refs[i][...], v_refs[i][...])
            for o, val in zip(outs[4 * i:4 * i + 4], (grad, delta, m_new, v_new)):
                o[...] = val

    res = pl.pallas_call(
        body, name="small_update",
        out_shape=[jax.ShapeDtypeStruct(small_all.shape[1:], F32)]
        + [jax.ShapeDtypeStruct(w[n].shape, F32) for n in names for _ in range(4)],
        compiler_params=_params(None),
    )(small_all, *[w[n] for n in names], *[m[n] for n in names], *[v[n] for n in names])
    return res[0], {n: res[1 + 4 * i:5 + 4 * i] for i, n in enumerate(names)}


def _adamw(name, w, g, m, v, *, tr, slots, by_columns=False):
    r, c = w.shape

    def body(w_ref, g_ref, m_ref, v_ref, g_out, d_out, m_out, v_out):
        if slots:
            grad = g_ref[0].astype(F32)
            for k in range(1, N_DEV):
                grad = grad + g_ref[k].astype(F32)
        else:
            grad = g_ref[...]
        g_out[...] = grad
        d_out[...], m_out[...], v_out[...] = _adamw_math(w_ref[...], grad, m_ref[...], v_ref[...])

    if by_columns:
        tile = pl.BlockSpec((r, tr), lambda i: (0, i))
        g_spec = pl.BlockSpec((N_DEV, r, tr), lambda i: (0, 0, i)) if slots else tile
    else:
        tile = pl.BlockSpec((tr, c), lambda i: (i, 0))
        g_spec = pl.BlockSpec((N_DEV, tr, c), lambda i: (0, i, 0)) if slots else tile
    return pl.pallas_call(
        body, name=name, grid=((c if by_columns else r) // tr,),
        in_specs=[tile, g_spec, tile, tile], out_specs=[tile] * 4,
        out_shape=[jax.ShapeDtypeStruct((r, c), F32)] * 4,
        compiler_params=_params(("parallel",)),
    )(w, g, m, v)


def _dot_f32(a, b, dims=NN):
    a0, a1, a2 = _split3(a)
    b0, b1, b2 = _split3(b)
    acc = _dot(a0, b0, dims)
    for x, y in ((a0, b1), (a1, b0), (a1, b1), (a0, b2), (a2, b0)):
        acc = acc + _dot(x, y, dims)
    return acc


def _ada_mod(c_all, w_shard, b_shard):
    def body(c_ref, w_ref, b_ref, o_ref):
        act = _silu(c_ref[...])
        act16 = jnp.concatenate([act, jnp.zeros_like(act)], axis=0)
        o_ref[...] = _dot_f32(act16, w_ref[...])[0:N_DEV] + b_ref[...]

    return pl.pallas_call(
        body, name="ada_mod", out_shape=jax.ShapeDtypeStruct((N_DEV, w_shard.shape[1]), F32),
        compiler_params=_params(None),
    )(c_all, w_shard, b_shard)


def _ada_grad(c_all, dmod_cols):
    def body(c_ref, dc_ref, gw_ref):
        act = _silu(c_ref[...])
        act16 = jnp.concatenate([act, jnp.zeros_like(act)], axis=0)
        dm = dc_ref[...]
        dm16 = jnp.concatenate([dm, jnp.zeros_like(dm)], axis=0)
        gw_ref[...] = _dot_f32(act16, dm16, TN)

    return pl.pallas_call(
        body, name="ada_grad", out_shape=jax.ShapeDtypeStruct((D, dmod_cols.shape[1]), F32),
        compiler_params=_params(None),
    )(c_all, dmod_cols)


def _exchange(name, xs, scatter):
    n = len(xs)
    n_peer = N_DEV - 1

    def body(*refs):
        x_refs, o_refs = refs[:n], refs[n:2 * n]
        send_sems, recv_sems, local_sems = refs[2 * n:]
        mx, my, mc = lax.axis_index("x"), lax.axis_index("y"), lax.axis_index("c")
        me = 4 * mx + 2 * my + mc

        def src(a, slot):
            return x_refs[a].at[slot] if scatter else x_refs[a]

        own = [pltpu.make_async_copy(src(a, me), o_refs[a].at[me], local_sems.at[a]) for a in range(n)]
        for cp in own:
            cp.start()
        sends = []
        for d in range(1, N_DEV):
            px = 1 - mx if d & 4 else mx
            py = 1 - my if d & 2 else my
            pc = 1 - mc if d & 1 else mc
            peer = 4 * px + 2 * py + pc
            for a in range(n):
                def copy(src_slot, dst_slot, a=a, d=d, to=(px, py, pc)):
                    return pltpu.make_async_remote_copy(
                        src_ref=src(a, src_slot), dst_ref=o_refs[a].at[dst_slot],
                        send_sem=send_sems.at[a * n_peer + d - 1], recv_sem=recv_sems.at[a * n_peer + d - 1],
                        device_id=to, device_id_type=pl.DeviceIdType.MESH)

                out = copy(peer, me)
                out.start()
                sends.append((out, copy(me, peer)))
        for _, arrival in sends:
            arrival.wait_recv()
        for out, _ in sends:
            out.wait_send()
        for cp in own:
            cp.wait()

    shapes = [tuple(x.shape[1:] if scatter else x.shape) for x in xs]
    return pl.pallas_call(
        body, name=name,
        in_specs=[pl.BlockSpec(memory_space=pl.ANY)] * n, out_specs=[pl.BlockSpec(memory_space=pl.ANY)] * n,
        out_shape=[jax.ShapeDtypeStruct((N_DEV,) + sh, x.dtype) for sh, x in zip(shapes, xs)],
        scratch_shapes=[pltpu.SemaphoreType.DMA((n * n_peer,)), pltpu.SemaphoreType.DMA((n * n_peer,)),
                        pltpu.SemaphoreType.DMA((n,))],
        compiler_params=pltpu.CompilerParams(has_side_effects=True),
    )(*xs)


def _gather_two_level(name, x):
    def body(x_ref, o_ref, send_sems, recv_sems, local_sem):
        mx, my, mc = lax.axis_index("x"), lax.axis_index("y"), lax.axis_index("c")
        me, sibling = (mx, my, mc), (mx, my, 1 - mc)
        chips = [(1 - mx, my), (mx, 1 - my), (1 - mx, 1 - my)]

        def slot(px, py, pc):
            return o_ref.at[4 * px + 2 * py + pc]

        def copy(k, block, to, src=None):
            return pltpu.make_async_remote_copy(
                src_ref=slot(*block) if src is None else src, dst_ref=slot(*block),
                send_sem=send_sems.at[k], recv_sem=recv_sems.at[k], device_id=to, device_id_type=pl.DeviceIdType.MESH)

        mine = pltpu.make_async_copy(x_ref, slot(*me), local_sem)
        mine.start()
        first = [copy(0, me, sibling, src=x_ref)] + [copy(1 + i, me, (*chip, mc), src=x_ref) for i, chip in enumerate(chips)]
        for cp in first:
            cp.start()
        passed = [copy(4 + i, (*chip, mc), sibling) for i, chip in enumerate(chips)]
        for i, chip in enumerate(chips):
            copy(1 + i, (*chip, mc), me).wait_recv()
            passed[i].start()
        copy(0, sibling, me).wait_recv()
        for i, chip in enumerate(chips):
            copy(4 + i, (*chip, 1 - mc), me).wait_recv()
        for cp in first + passed:
            cp.wait_send()
        mine.wait()

    return pl.pallas_call(
        body, name=name,
        in_specs=[pl.BlockSpec(memory_space=pl.ANY)], out_specs=pl.BlockSpec(memory_space=pl.ANY),
        out_shape=jax.ShapeDtypeStruct((N_DEV,) + tuple(x.shape), x.dtype),
        scratch_shapes=[pltpu.SemaphoreType.DMA((7,)), pltpu.SemaphoreType.DMA((7,)), pltpu.SemaphoreType.DMA(())],
        compiler_params=pltpu.CompilerParams(has_side_effects=True),
    )(x)


def _after(x, zero):
    return x if zero is None else x + zero.reshape(-1)[0].astype(x.dtype)


def _exchange_copies(x_refs, land_refs, send_sems, recv_sems, scatter):
    n = len(x_refs)
    n_peer = N_DEV - 1
    mx, my, mc = lax.axis_index("x"), lax.axis_index("y"), lax.axis_index("c")
    me = 4 * mx + 2 * my + mc
    pairs = []
    for d in range(1, N_DEV):
        px = 1 - mx if d & 4 else mx
        py = 1 - my if d & 2 else my
        pc = 1 - mc if d & 1 else mc
        peer = 4 * px + 2 * py + pc
        for a in range(n):
            def copy(src_slot, dst_slot, a=a, d=d, to=(px, py, pc)):
                return pltpu.make_async_remote_copy(
                    src_ref=x_refs[a].at[src_slot] if scatter else x_refs[a], dst_ref=land_refs[a].at[dst_slot],
                    send_sem=send_sems.at[a * n_peer + d - 1], recv_sem=recv_sems.at[a * n_peer + d - 1],
                    device_id=to, device_id_type=pl.DeviceIdType.MESH)

            pairs.append((copy(peer, me), copy(me, peer)))
    return me, pairs


def _exchange_async(name, xs, scatter, collective_id):
    n = len(xs)
    shapes = [tuple(x.shape[1:] if scatter else x.shape) for x in xs]
    x_refs = [jax.new_ref(x, memory_space=pltpu.MemorySpace.HBM) for x in xs]
    land_refs = [jax.empty_ref(jax.ShapeDtypeStruct((N_DEV,) + sh, x.dtype), memory_space=pltpu.MemorySpace.HBM)
                 for sh, x in zip(shapes, xs)]

    @pl.kernel(mesh=plsc.ScalarSubcoreMesh(axis_name="sequencer", num_cores=1), name=name,
               scratch_types=(pltpu.SemaphoreType.DMA((n * (N_DEV - 1),)), pltpu.SemaphoreType.DMA((n * (N_DEV - 1),)),
                              pltpu.SemaphoreType.DMA((n,))),
               compiler_params=pltpu.CompilerParams(collective_id=collective_id))
    def launch(send_sems, recv_sems, own_sems):
        barrier = pltpu.get_barrier_semaphore()
        mx, my, mc = lax.axis_index("x"), lax.axis_index("y"), lax.axis_index("c")
        for d in range(1, N_DEV):
            peer = (1 - mx if d & 4 else mx, 1 - my if d & 2 else my, 1 - mc if d & 1 else mc)
            pl.semaphore_signal(barrier, inc=1, device_id=peer, device_id_type=pl.DeviceIdType.MESH)
        pl.semaphore_wait(barrier, N_DEV - 1)
        me, pairs = _exchange_copies(x_refs, land_refs, send_sems, recv_sems, scatter)
        own = [pltpu.make_async_copy(x_refs[a].at[me] if scatter else x_refs[a], land_refs[a].at[me], own_sems.at[a])
               for a in range(n)]
        for cp in own:
            cp.start()
        for out, _ in pairs:
            out.start()
        for out, arrival in pairs:
            arrival.wait_recv()
            out.wait_send()
        for cp in own:
            cp.wait()

    launch()
    return lambda: [r[...] for r in land_refs]


def _relu2(a):
    r = jnp.maximum(a, 0.0)
    return r * r


def _relu2_grad(acc, r):
    return acc * (2.0 * jnp.sqrt(r.astype(F32)))


def _local_step(x0, tgt, mod, wcat_t, late_weights, send_grads, conv_w, conv_b, dt_bias, a_log, d_skip, ssm_norm_w, f_bias,
                attn_norm_w, ln1_g, ln1_b, ln2_g, ln2_b):
    ff_w = DFF // N_DEV
    s = x0.shape[0]
    tm = min(1024, s)
    ts = min(1024, s)
    sh1, sc1, g1, sh2, sc2, g2 = [mod[:, i * D:(i + 1) * D] for i in range(6)]
    zero = jnp.zeros((1, 128 - 2 * NH), F32)
    bias128 = jnp.concatenate([dt_bias, f_bias, zero], axis=1)
    alog128 = jnp.concatenate([a_log, jnp.zeros((1, 128 - NH), F32)], axis=1)
    dskip_x = jnp.repeat(d_skip, HD, axis=1)
    w_xs, w_bc, b_xs, b_bc = conv_w[:, :D], conv_w[:, D:], conv_b[:, :D], conv_b[:, D:]

    h1, = _rowk("modulate", lambda pos, x, sc, sh: (_modulate(x, sc, sh),), s, 512, [(x0, D, 0, 0)], [sc1, sh1], [(D, BF16)], [])
    p = _mm_nt("in_proj", [(h1, D, 0)], [(wcat_t, D, 0)], n=PCOLS, tm=tm, tn=1152, out_dtype=F32)
    xs_a, bc_a = _conv_fwd(p, w_xs, b_xs, w_bc, b_bc, s)
    y_ssd, states = _ssd_fwd(xs_a, bc_a, p, bias128, alog128, dskip_x, s)
    cum = _cum_fwd(p, bias128, s)
    att, lse = _attn_fwd(p, cum, s)
    wout, w1s, w2 = late_weights()
    ymix = _mix_norm(y_ssd, p, att, ssm_norm_w, attn_norm_w, s)
    y = _mm_nn("out_proj", ymix, wout, tm=tm, tn=1024, tk=2 * D, out_dtype=F32)
    x1, h2 = _ln1(x0, y, g1, ln1_g, ln1_b, sc2, sh2, s)
    tall = min(2048, s)
    r = _mm_nn("ff_in", h2, w1s, tm=tall, tn=ff_w, tk=D, out_dtype=BF16, epi=_relu2)
    ff = _mm_nn("ff_out", r, w2, tm=tall, tn=1024, tk=1024, out_dtype=F32)
    du2, dff, sq_err, d_ln2_g, d_ln2_b, d_g2 = _ln2_loss(x1, ff, tgt, g2, ln2_g, ln2_b, s)

    da1 = _mm_nt("d_ff_hidden", [(dff, D, 0)], [(w2, D, 0)], n=DFF, tm=tm, tn=1024, out_dtype=BF16, epi=_relu2_grad,
                 epi_aux=(r,))
    d_w2 = _mm_tn("d_w_ff_out", r, dff, tm=1024, tn=1024, ts=ts)
    d_w1s = _mm_tn("d_w_ff_in", h2, da1, tm=1024, tn=ff_w, ts=ts, col_shards=True)
    dh2 = _mm_nt("d_ff_input", [(da1, ff_w, k) for k in range(N_DEV)], [(w1s, ff_w, k) for k in range(N_DEV)], n=D,
                 tm=min(512, s), tn=1024, out_dtype=F32)
    du1, dy, d_sc2, d_sh2, d_ln1_g, d_ln1_b, d_g1 = _ln1_bwd(dh2, du2, x0, y, g1, ln1_g, ln1_b, sc2, s)

    dmix = _mm_nt("d_mix", [(dy, D, 0)], [(wout, D, 0)], n=2 * D, tm=tm, tn=1024, out_dtype=F32)
    d_wout = _mm_tn("d_w_out", ymix, dy, tm=1024, tn=1024, ts=ts)
    sent = send_grads("late", [d_w1s, d_w2.reshape(N_DEV, -1, D), d_wout.reshape(N_DEV, -1, D)])
    dy_ssd, dz, datt, d_ssm_w, d_attn_w = _mix_norm_bwd(dmix, y_ssd, p, att, _after(ssm_norm_w, sent), attn_norm_w, s)
    dq, dk, dv, dcs, drs = _attn_bwd(p, cum, att, lse, datt, s)
    dxs_a, dbc_a, ddt_raw, d_alog, d_dskip = _ssd_bwd(dy_ssd, xs_a, bc_a, p, states, bias128, alog128, dskip_x, s)
    dcum = jnp.pad((drs - dcs)[:, :2, :].reshape(NH, s).T, ((0, 0), (NH, 128 - 2 * NH)))
    ddtf, _, d_bias = _cum_bwd(dcum, ddt_raw, p, bias128, s)
    dxs, dbc, d_wc_xs, d_bc_xs, d_wc_bc, d_bc_bc = _conv_bwd(dxs_a, dbc_a, p, w_xs, b_xs, w_bc, b_bc, s)

    segs = [(dz, OFF_Z, D), (dxs, OFF_XS, D), (dq, OFF_Q, D), (dk, OFF_K, D), (dv, OFF_V, D), (dbc, OFF_BC, 512),
            (ddtf, OFF_DTF, 128)]
    d_z, d_xs, d_q, d_k, d_v, d_bcw, d_dtf = [
        _mm_tn("d_w_in_%d" % i, a, h1, tm=min(w, 1024), tn=1024, ts=ts)
        for i, (a, _, w) in enumerate(segs)]
    d_w_in_t = dict(z=d_z, xs=d_xs, bc=d_bcw, dt=d_dtf[:NH], q=d_q, k=d_k, v=d_v, f=d_dtf[NH:2 * NH])
    sent = send_grads("in", [_shard_w_in_grad_t(d_w_in_t)])
    segs[-1] = (_after(ddtf, sent), OFF_DTF, 128)
    dh1 = _mm_nt("d_h1", [(a, w, 0) for a, _, w in segs], [(wcat_t, w, off // w) for _, off, w in segs], n=D,
                 tm=min(512, s), tn=1024, out_dtype=F32, b_rows=True)
    grad_x, d_sc1, d_sh1 = _input_grad(dh1, du1, x0, sc1, s)

    return dict(
        loss=(0.5 / D) * jnp.sum(sq_err), grad_x=grad_x,
        d_mod=jnp.concatenate([d_sh1, d_sc1, d_g1, d_sh2, d_sc2, d_g2], axis=1),
        d_conv_w=jnp.concatenate([d_wc_xs[:4], d_wc_bc[:4]], axis=1), d_conv_b=jnp.concatenate([d_bc_xs, d_bc_bc], axis=1),
        d_ssm_norm_w=d_ssm_w, d_attn_norm_w=d_attn_w, d_ln1_g=d_ln1_g, d_ln1_b=d_ln1_b, d_ln2_g=d_ln2_g, d_ln2_b=d_ln2_b,
        d_gate_bias=d_bias, d_a_log=d_alog, d_d_skip=d_dskip)


W_IN_SEGS = [('z', W_Z, D), ('xs', W_XS, D), ('bc', W_BC, 512), ('dt', W_DT, NH), ('q', W_Q, D), ('k', W_K, D),
             ('v', W_V, D), ('f', W_F, NH)]
SHARD_W = IN_COLS // N_DEV


def _pack_w_in_t(w_in_t):
    seg = {n: w_in_t[off:off + w] for n, off, w in W_IN_SEGS}
    return jnp.concatenate([seg['z'], seg['xs'], seg['q'], seg['k'], seg['v'], seg['bc'], seg['dt'], seg['f'],
                            jnp.zeros((128 - 2 * NH, D), w_in_t.dtype)], axis=0)


def _shard_w_in_grad_t(d_w_in_t):
    blocks = []
    for dev in range(N_DEV):
        lo, hi = dev * SHARD_W, (dev + 1) * SHARD_W
        pieces = [d_w_in_t[n][max(lo, off) - off:min(hi, off + w) - off] for n, off, w in W_IN_SEGS
                  if max(lo, off) < min(hi, off + w)]
        blocks.append(jnp.concatenate(pieces, axis=0))
    return jnp.stack(blocks, axis=0)


WEIGHTS = ['w_ada', 'b_ada', 'w_in', 'conv_w', 'conv_b', 'dt_bias', 'a_log', 'd_skip', 'ssm_norm_w', 'f_bias',
           'attn_norm_w', 'w_out', 'ln1_g', 'ln1_b', 'w_ff_in', 'w_ff_out', 'ln2_g', 'ln2_b']
BIG = ['w_in', 'w_out', 'w_ff_in', 'w_ff_out']
SMALL_LAYOUT = [('b_ada', 0, 6 * D), ('conv_b', 12288, 1536), ('ssm_norm_w', 13824, D), ('attn_norm_w', 14848, D),
                ('ln1_g', 15872, D), ('ln1_b', 16896, D), ('ln2_g', 17920, D), ('ln2_b', 18944, D),
                ('dt_bias', 19968, NH), ('f_bias', 19968 + NH, NH), ('a_log', 20096, NH), ('d_skip', 20224, NH)]
SMALL_LOSS_LANE = 20352


def _pad_lanes(v, n=128):
    return jnp.pad(v, ((0, 0), (0, n - v.shape[1])))


def kernel(x, c, w_ada, b_ada, w_in, conv_w, conv_b, dt_bias, a_log, d_skip, ssm_norm_w, f_bias, attn_norm_w, w_out, ln1_g, ln1_b, w_ff_in, w_ff_out, ln2_g, ln2_b, loss_target, m_w_ada, m_b_ada, m_w_in, m_conv_w, m_conv_b, m_dt_bias, m_a_log, m_d_skip, m_ssm_norm_w, m_f_bias, m_attn_norm_w, m_w_out, m_ln1_g, m_ln1_b, m_w_ff_in, m_w_ff_out, m_ln2_g, m_ln2_b, v_w_ada, v_b_ada, v_w_in, v_conv_w, v_conv_b, v_dt_bias, v_a_log, v_d_skip, v_ssm_norm_w, v_f_bias, v_attn_norm_w, v_w_out, v_ln1_g, v_ln1_b, v_w_ff_in, v_w_ff_out, v_ln2_g, v_ln2_b):
    args = dict(locals())
    w = {n: args[n] for n in WEIGHTS}
    m = {n: args['m_' + n] for n in WEIGHTS}
    v = {n: args['v_' + n] for n in WEIGHTS}
    me = 4 * lax.axis_index("x") + 2 * lax.axis_index("y") + lax.axis_index("c")
    ada_cols = 6 * D // N_DEV
    conv_cols = conv_w.shape[2]

    c_all, conv_all = _exchange("gather_cond", [c, conv_w[0]], False)
    c_all = c_all.reshape(N_DEV, D)
    conv_w_full = conv_all.transpose(1, 0, 2).reshape(4, N_DEV * conv_cols)
    b_shard = lax.dynamic_slice(b_ada, (0, me * ada_cols), (1, ada_cols))
    mod_all, = _exchange("gather_mod", [_ada_mod(c_all, w_ada[0], b_shard)], False)
    mod = lax.dynamic_index_in_dim(mod_all, me, axis=1, keepdims=False).reshape(1, 6 * D)

    w_in_t = _after(jnp.swapaxes(w_in[0], 0, 1).astype(BF16), mod * 0)
    win_s = _gather_two_level("gather_w_in", w_in_t)
    first_done = win_s[0, 0:1, 0:1] * 0
    rest = _exchange_async("gather_rest", [_after(w[n][0].astype(BF16), first_done) for n in BIG[1:]], False, 1)

    def late_weights():
        wout_s, w1s, w2_s = rest()
        return wout_s.reshape(2 * D, D), w1s, w2_s.reshape(DFF, D)

    sends = {}

    def send_grads(tag, blocks):
        sends[tag] = _exchange_async("scatter_" + tag, blocks, True, {'late': 2, 'in': 3}[tag])
        return sum(b.reshape(-1)[0].astype(F32) * 0 for b in blocks)

    out = _local_step(x[0], loss_target[0], mod, _pack_w_in_t(win_s.reshape(IN_COLS, D)), late_weights, send_grads,
                      conv_w_full, conv_b, dt_bias, a_log, d_skip, ssm_norm_w, f_bias, attn_norm_w, ln1_g, ln1_b, ln2_g, ln2_b)

    small = jnp.concatenate(
        [out['d_mod'], out['d_conv_w'].reshape(1, -1), out['d_conv_b'], out['d_ssm_norm_w'], out['d_attn_norm_w'],
         out['d_ln1_g'], out['d_ln1_b'], out['d_ln2_g'], out['d_ln2_b'], out['d_gate_bias'], out['d_a_log'],
         out['d_d_skip'], _pad_lanes(out['loss'].reshape(1, 1))], axis=1)
    small_landed = _exchange_async("gather_small", [small], False, 4)
    (g_ff_in, g_ff_out, g_out), (g_in,) = sends['late'](), sends['in']()
    g_parts = dict(w_ff_in=g_ff_in, w_ff_out=g_ff_out, w_out=g_out, w_in=g_in)
    big = {n: _adamw("adamw_" + n, w[n][0], g_parts[n], m[n][0], v[n][0], tr=256, slots=True) for n in BIG[1:]}
    t = lambda a: jnp.swapaxes(a[0], 0, 1)
    big['w_in'] = [jnp.swapaxes(r, 0, 1) for r in _adamw("adamw_w_in", t(w_in), g_parts['w_in'], t(m_w_in), t(v_w_in),
                                                         tr=256, slots=True, by_columns=True)]
    big_done = sum(big[n][1][0:1, 0:1] * 0 for n in BIG)
    small_all = _after(small_landed()[0], big_done)
    ssum, small_res = _small_update(small_all, SMALL_LAYOUT, w, m, v)
    dmod_all = small_all[:, 0, :6 * D]
    g_w_ada = _ada_grad(c_all, lax.dynamic_slice(dmod_all, (0, me * ada_cols), (N_DEV, ada_cols)))
    ada = _adamw("adamw_ada", w_ada[0], g_w_ada, m_w_ada[0], v_w_ada[0], tr=256, slots=False)
    g_conv_w = lax.dynamic_slice(ssum[:, 6 * D:6 * D + 4 * N_DEV * conv_cols].reshape(4, N_DEV * conv_cols),
                                 (0, me * conv_cols), (4, conv_cols))
    conv = _adamw("adamw_conv_w", conv_w[0], g_conv_w, m_conv_w[0], v_conv_w[0], tr=4, slots=False)

    results = []
    for k in range(4):
        vals = {n: small_res[n][k] for n in small_res}
        vals['w_ada'], vals['conv_w'] = ada[k][None], conv[k][None]
        for n in BIG:
            vals[n] = big[n][k][None]
        results.append(vals)
    return (ssum[0, SMALL_LOSS_LANE], out['grad_x'][None], *[res[n] for res in results for n in WEIGHTS])
```

```python
import functools

import jax
import jax.numpy as jnp
from jax import lax
from jax.experimental import pallas as pl
from jax.experimental.pallas import tpu as pltpu
from jax.experimental.pallas import tpu_sc as plsc

F32, BF16 = jnp.float32, jnp.bfloat16

N_DEV = 8
D = 1024
NH, HD = 16, 64
NSTATE = 128
CHUNK = 128
HG = 8
DFF = 4096
ALPHA = 2.0 ** 0.25
EPS = 1e-5
ATT_SCALE = HD ** -0.5

OFF_Z, OFF_XS, OFF_Q, OFF_K, OFF_V, OFF_BC, OFF_DTF = 0, 1024, 2048, 3072, 4096, 5120, 5632
PCOLS = 5760
W_Z, W_XS, W_BC, W_DT, W_Q, W_K, W_V, W_F = 0, 1024, 2048, 2560, 2576, 3600, 4624, 5648
IN_COLS = 5664

ADAM_LR, ADAM_B1, ADAM_B2, ADAM_EPS, ADAM_WD, ADAM_STEP = 0.001, 0.9, 0.999, 1e-08, 0.01, 10

VMEM_LIMIT = 56 << 20

NN = (((1,), (0,)), ((), ()))
NT = (((1,), (1,)), ((), ()))
TN = (((0,), (0,)), ((), ()))


def _dot(a, b, dims=NN):
    return lax.dot_general(a, b, dims, preferred_element_type=F32)


def _bdot(a, b, dims=NN):
    return _dot(a.astype(BF16), b.astype(BF16), dims)


def _split3(v, terms=3):
    parts, rest = [], v
    for _ in range(terms):
        p = rest.astype(BF16)
        parts.append(p)
        rest = rest - p.astype(F32)
    return parts


def _sel_left(m01, v):
    return sum(_dot(m01, p) for p in _split3(v))


def _sel_right(v, m01, dims=NN, terms=3):
    return sum(_dot(p, m01, dims) for p in _split3(v, terms))


def _iota(shape, dim):
    return lax.broadcasted_iota(jnp.int32, shape, dim)


def _tri_lower(n):
    return (_iota((n, n), 1) <= _iota((n, n), 0)).astype(BF16)


def _tri_upper(n):
    return (_iota((n, n), 1) >= _iota((n, n), 0)).astype(BF16)


def _head_expand():
    return (lax.shift_right_logical(_iota((128, D), 1), 6) == _iota((128, D), 0)).astype(BF16)


def _head_reduce():
    return (lax.shift_right_logical(_iota((D, 128), 0), 6) == _iota((D, 128), 1)).astype(BF16)


def _sigmoid(x):
    return 1.0 / (1.0 + jnp.exp(-x))


def _silu(x):
    return x * _sigmoid(x)


def _dsilu(x):
    s = _sigmoid(x)
    return s * (1.0 + x * (1.0 - s))


def _softplus(x):
    return jnp.maximum(x, 0.0) + jnp.log(1.0 + jnp.exp(-jnp.abs(x)))


def _log_sigmoid(x):
    return jnp.minimum(x, 0.0) - jnp.log(1.0 + jnp.exp(-jnp.abs(x)))


def _params(sem):
    return pltpu.CompilerParams(dimension_semantics=sem, vmem_limit_bytes=VMEM_LIMIT)


def _mm_nn(name, a, b, *, tm, tn, tk, out_dtype, pro=None, aux=(), epi=None):
    m, k_all = a.shape
    b_sharded = b.ndim == 3
    n = b.shape[0] * b.shape[2] if b_sharded else b.shape[1]
    assert not b_sharded or tn == b.shape[2]
    nk = k_all // tk
    n_aux = len(aux)
    b_spec = (pl.BlockSpec((None, tk, tn), lambda i, j, k: (j, k, 0)) if b_sharded
              else pl.BlockSpec((tk, tn), lambda i, j, k: (k, j)))

    def body(a_ref, b_ref, *rest):
        aux_refs, o_ref = rest[:n_aux], rest[n_aux]
        at = a_ref[...]
        if pro is not None:
            at = pro(at, *[r[...] for r in aux_refs])
        part = _bdot(at, b_ref[...])
        if nk == 1:
            o_ref[...] = (part if epi is None else epi(part)).astype(out_dtype)
            return
        assert epi is None
        acc_ref = rest[n_aux + 1]
        kk = pl.program_id(2)

        @pl.when(kk == 0)
        def _():
            acc_ref[...] = part

        @pl.when(kk > 0)
        def _():
            acc_ref[...] += part

        @pl.when(kk == nk - 1)
        def _():
            o_ref[...] = acc_ref[...].astype(out_dtype)

    return pl.pallas_call(
        body, name=name,
        grid=(m // tm, n // tn, nk),
        in_specs=[pl.BlockSpec((tm, tk), lambda i, j, k: (i, k)), b_spec]
        + [pl.BlockSpec((1, tk), lambda i, j, k: (0, k)) for _ in aux],
        out_specs=pl.BlockSpec((tm, tn), lambda i, j, k: (i, j)),
        out_shape=jax.ShapeDtypeStruct((m, n), out_dtype),
        scratch_shapes=[] if nk == 1 else [pltpu.VMEM((tm, tn), F32)],
        compiler_params=_params(("parallel", "parallel", "arbitrary")),
    )(a, b, *aux)


def _mm_nt(name, a_list, b_list, *, n, tm, tn, out_dtype, epi=None, epi_aux=(), b_rows=False):
    m = a_list[0][0].shape[0]
    n_op = len(a_list)
    n_epi = len(epi_aux)
    dims = NN if b_rows else NT

    def body(*refs):
        a_refs, b_refs = refs[:n_op], refs[n_op:2 * n_op]
        e_refs, o_ref = refs[2 * n_op:2 * n_op + n_epi], refs[2 * n_op + n_epi]
        acc = None
        for a_ref, b_ref in zip(a_refs, b_refs):
            part = _bdot(a_ref[...], b_ref[...], dims)
            acc = part if acc is None else acc + part
        if epi is not None:
            acc = epi(acc, *[r[...] for r in e_refs])
        o_ref[...] = acc.astype(out_dtype)

    in_specs = [pl.BlockSpec((tm, w), functools.partial(lambda i, j, cb: (i, cb), cb=cb)) for (_, w, cb) in a_list]
    for (b, w, cb) in b_list:
        if b_rows:
            in_specs.append(pl.BlockSpec((w, tn), functools.partial(lambda i, j, cb: (cb, j), cb=cb)))
        elif b.ndim == 3:
            in_specs.append(pl.BlockSpec((None, tn, w), functools.partial(lambda i, j, cb: (cb, j, 0), cb=cb)))
        else:
            in_specs.append(pl.BlockSpec((tn, w), functools.partial(lambda i, j, cb: (j, cb), cb=cb)))
    in_specs += [pl.BlockSpec((tm, tn), lambda i, j: (i, j)) for _ in epi_aux]
    return pl.pallas_call(
        body, name=name,
        grid=(m // tm, n // tn),
        in_specs=in_specs,
        out_specs=pl.BlockSpec((tm, tn), lambda i, j: (i, j)),
        out_shape=jax.ShapeDtypeStruct((m, n), out_dtype),
        compiler_params=_params(("parallel", "parallel")),
    )(*[a for (a, _, _) in a_list], *[b for (b, _, _) in b_list], *epi_aux)


def _mm_tn(name, a, b, *, tm, tn, ts, pro=None, aux=(), col_shards=False):
    s_all, ka = a.shape
    nb = b.shape[1]
    n_aux = len(aux)
    ns = s_all // ts
    assert not col_shards or tn == nb // N_DEV

    def body(a_ref, b_ref, *rest):
        aux_refs, o_ref, acc_ref = rest[:n_aux], rest[n_aux], rest[n_aux + 1]
        at = a_ref[...]
        if pro is not None:
            at = pro(at, *[r[...] for r in aux_refs])
        part = _bdot(at, b_ref[...], TN)
        ss = pl.program_id(2)

        @pl.when(ss == 0)
        def _():
            acc_ref[...] = part

        @pl.when(ss > 0)
        def _():
            acc_ref[...] += part

        @pl.when(ss == ns - 1)
        def _():
            o_ref[...] = acc_ref[...].astype(BF16)

    if col_shards:
        out_spec = pl.BlockSpec((None, tm, tn), lambda i, j, s: (j, i, 0))
        out_shape = jax.ShapeDtypeStruct((N_DEV, ka, tn), BF16)
    else:
        out_spec = pl.BlockSpec((tm, tn), lambda i, j, s: (i, j))
        out_shape = jax.ShapeDtypeStruct((ka, nb), BF16)
    return pl.pallas_call(
        body, name=name,
        grid=(ka // tm, nb // tn, ns),
        in_specs=[pl.BlockSpec((ts, tm), lambda i, j, s: (s, i)),
                  pl.BlockSpec((ts, tn), lambda i, j, s: (s, j))]
        + [pl.BlockSpec((1, tm), lambda i, j, s: (0, i)) for _ in aux],
        out_specs=out_spec, out_shape=out_shape,
        scratch_shapes=[pltpu.VMEM((tm, tn), F32)],
        compiler_params=_params(("parallel", "parallel", "arbitrary")),
    )(a, b, *aux)


def _rowk(name, fn, n_rows, tr, rows, fulls, outs, accs, reverse=False):
    n = n_rows // tr
    n_row, n_full, n_out, n_acc = len(rows), len(fulls), len(outs), len(accs)

    def pos(i):
        return (n - 1 - i) if reverse else i

    def body(*refs):
        row_refs = refs[:n_row]
        full_refs = refs[n_row:n_row + n_full]
        out_refs = refs[n_row + n_full:n_row + n_full + n_out]
        acc_refs = refs[n_row + n_full + n_out:]
        i = pl.program_id(0)

        @pl.when(i == 0)
        def _():
            for r in acc_refs:
                r[...] = jnp.zeros(r.shape, r.dtype)

        res = fn(pos(i), *[r[...] for r in row_refs], *[r[...] for r in full_refs], *[r[...] for r in acc_refs])
        for r, v in zip(out_refs + acc_refs, res):
            r[...] = v.astype(r.dtype)

    def row_map(i, cb, shift):
        return (jnp.clip(pos(i) + shift, 0, n - 1), cb)

    def halo_map(i, cb, shift):
        tile = jnp.clip(pos(i) + shift, 0, n - 1)
        return (tile * (tr // 8) + (tr // 8 - 1 if shift < 0 else 0), cb)

    in_specs = [pl.BlockSpec((tr, w), functools.partial(row_map, cb=cb, shift=sh)) if sh == 0 else
                pl.BlockSpec((8, w), functools.partial(halo_map, cb=cb, shift=sh)) for (_, w, cb, sh) in rows]
    in_specs += [pl.BlockSpec(f.shape, functools.partial(lambda i, nd: (0,) * nd, nd=f.ndim)) for f in fulls]
    out_specs = [pl.BlockSpec((tr, w), lambda i: (pos(i), 0)) for (w, _) in outs]
    out_specs += [pl.BlockSpec((r, w), lambda i: (0, 0)) for (r, w) in accs]
    out_shape = [jax.ShapeDtypeStruct((n_rows, w), dt) for (w, dt) in outs]
    out_shape += [jax.ShapeDtypeStruct((r, w), F32) for (r, w) in accs]
    return pl.pallas_call(
        body, name=name, grid=(n,), in_specs=in_specs, out_specs=out_specs, out_shape=out_shape,
        compiler_params=_params(("arbitrary",)),
    )(*[a for (a, _, _, _) in rows], *fulls)


def _colsum(x):
    return jnp.sum(x, axis=0, keepdims=True)


def _mean(x):
    return jnp.mean(x, axis=-1, keepdims=True)


def _modulate(x, sc, sh):
    return x * (1.0 + sc) + sh


def _shift_down(cur, prev8, j):
    tr = cur.shape[0]
    row8 = _iota(prev8.shape, 0)
    head = jnp.where(row8 < j, pltpu.roll(prev8, j, 0), pltpu.roll(cur[0:8], j, 0))
    return head if tr == 8 else jnp.concatenate([head, pltpu.roll(cur, j, 0)[8:]], axis=0)


def _shift_up(cur, next8, j):
    tr = cur.shape[0]
    row8 = _iota(next8.shape, 0)
    tail = jnp.where(row8 < 8 - j, pltpu.roll(cur[tr - 8:], 8 - j, 0), pltpu.roll(next8, 8 - j, 0))
    return jnp.concatenate([pltpu.roll(cur, tr - j, 0)[:tr - 8], tail], axis=0)


def _conv(cur, prev, w, b):
    out = cur * w[3:4] + b
    for j in (1, 2, 3):
        out = out + _shift_down(cur, prev, j) * w[3 - j:4 - j]
    return out


def _conv_fwd(p, w_xs, b_xs, w_bc, b_bc, s):
    def fn(pos, xs, xs_prev, bc, bc_prev, w_xs, b_xs, w_bc, b_bc):
        first = pos == 0
        xs_prev = jnp.where(first, 0.0, xs_prev)
        bc_prev = jnp.where(first, 0.0, bc_prev)
        return _silu(_conv(xs, xs_prev, w_xs, b_xs)), _silu(_conv(bc, bc_prev, w_bc, b_bc))

    return _rowk("conv_fwd", fn, s, 256,
                 [(p, D, OFF_XS // D, 0), (p, D, OFF_XS // D, -1), (p, 512, OFF_BC // 512, 0), (p, 512, OFF_BC // 512, -1)],
                 [w_xs, b_xs, w_bc, b_bc], [(D, F32), (512, F32)], [])


def _conv_bwd(dxs_a, dbc_a, p, w_xs, b_xs, w_bc, b_bc, s):
    tr = 256
    n = s // tr

    def fn(pos, da1, da1n, x1, x1p, x1n, da2, da2n, x2, x2p, x2n, w1, b1, w2, b2, aw1, ab1, aw2, ab2):
        dx1, dw1, db1 = _conv_bwd_fn(pos, n, da1, da1n, x1, x1p, x1n, w1, b1)
        dx2, dw2, db2 = _conv_bwd_fn(pos, n, da2, da2n, x2, x2p, x2n, w2, b2)
        return dx1, dx2, aw1 + dw1, ab1 + db1, aw2 + dw2, ab2 + db2

    cx, cb = OFF_XS // D, OFF_BC // 512
    return _rowk("conv_bwd", fn, s, tr,
                 [(dxs_a, D, 0, 0), (dxs_a, D, 0, 1), (p, D, cx, 0), (p, D, cx, -1), (p, D, cx, 1),
                  (dbc_a, 512, 0, 0), (dbc_a, 512, 0, 1), (p, 512, cb, 0), (p, 512, cb, -1), (p, 512, cb, 1)],
                 [w_xs, b_xs, w_bc, b_bc], [(D, BF16), (512, BF16)], [(8, D), (1, D), (8, 512), (1, 512)])


def _conv_bwd_fn(pos, n, da, da_next, x, x_prev, x_next, w, b):
    first, last = pos == 0, pos == n - 1
    x_prev = jnp.where(first, 0.0, x_prev)
    shifted = {j: _shift_down(x, x_prev, j) for j in (1, 2, 3)}
    conv = x * w[3:4] + b
    for j in (1, 2, 3):
        conv = conv + shifted[j] * w[3 - j:4 - j]
    dc = da * _dsilu(conv)
    dc_next = jnp.where(last, 0.0, da_next * _dsilu(_conv(x_next, x[x.shape[0] - 8:], w, b)))
    dx = dc * w[3:4]
    dws = [None] * 4
    dws[3] = _colsum(dc * x)
    for j in (1, 2, 3):
        dx = dx + _shift_up(dc, dc_next, j) * w[3 - j:4 - j]
        dws[3 - j] = _colsum(dc * shifted[j])
    row = _iota((8, x.shape[1]), 0)
    dw = jnp.zeros((8, x.shape[1]), F32)
    for k in range(4):
        dw = jnp.where(row == k, dws[k], dw)
    return dx, dw, _colsum(dc)


def _ssd_gates(dtf, bias, a_log):
    lane = _iota(dtf.shape, 1)
    head = lane < NH
    dt = jnp.where(head, _softplus(dtf + bias), 0.0)
    a_neg = jnp.where(_iota(a_log.shape, 1) < NH, -jnp.exp(a_log), 0.0)
    a = dt * a_neg
    cs = _sel_left(_tri_lower(CHUNK), a)
    return dt, a_neg, cs


def _decay_mask(cs_ref, cst_ref, h):
    diff = cs_ref[:, h:h + 1] - cst_ref[h:h + 1, :]
    low = _iota((CHUNK, CHUNK), 1) <= _iota((CHUNK, CHUNK), 0)
    return jnp.where(low, jnp.exp(jnp.minimum(diff, 0.0)), 0.0)


def _ssd_fwd(xs_a, bc_a, p, bias128, alog128, dskip_x, s):
    nc = s // CHUNK
    t = CHUNK

    def body(xs_ref, bc_ref, dtf_ref, bias_ref, alog_ref, dsk_ref, y_ref, st_ref,
             state, x_sc, xw_sc, cs_sc, cst_sc, yd_sc):
        c = pl.program_id(0)

        @pl.when(c == 0)
        def _():
            state[...] = jnp.zeros(state.shape, F32)

        dt, _, cs = _ssd_gates(dtf_ref[...], bias_ref[...], alog_ref[...])
        cs_sc[...] = cs
        cst_sc[...] = cs.T
        cs_last = cs[t - 1:t, :]
        expand = _head_expand()
        ex = _sel_right(jnp.concatenate([dt, jnp.exp(cs), jnp.exp(cs_last - cs)], axis=0), expand, terms=2)
        dt_x, eo_x, we_x = ex[0:t], ex[t:2 * t], ex[2 * t:3 * t]
        g_x = _sel_right(jnp.broadcast_to(jnp.exp(cs_last), (8, 128)), expand)[0:1]
        xs = xs_ref[...]
        x = xs * dt_x
        x_sc[...] = x.astype(BF16)
        xw_sc[...] = (x * we_x).astype(BF16)
        prev = state[...]
        st_ref[0] = prev
        prev_b = prev.astype(BF16)
        for g in range(2):
            cols = slice(g * 512, (g + 1) * 512)
            b_g = bc_ref[:, g * 128:(g + 1) * 128].astype(BF16)
            c_g = bc_ref[:, 256 + g * 128:256 + (g + 1) * 128].astype(BF16)
            gmat = _dot(c_g, b_g, NT)
            y_off = _dot(c_g, prev_b[:, cols]) * eo_x[:, cols]
            s_loc = _dot(b_g, xw_sc[:, cols], TN)
            state[:, cols] = g_x[:, cols] * prev[:, cols] + s_loc
            for e in range(HG):
                h = g * HG + e
                m = gmat * _decay_mask(cs_sc, cst_sc, h)
                yd_sc[:, h * HD:(h + 1) * HD] = _dot(m.astype(BF16), x_sc[:, h * HD:(h + 1) * HD])
            y_ref[:, cols] = yd_sc[:, cols] + y_off + dsk_ref[:, cols] * xs[:, cols]

    return pl.pallas_call(
        body, name="ssd_fwd", grid=(nc,),
        in_specs=[pl.BlockSpec((t, D), lambda c: (c, 0)),
                  pl.BlockSpec((t, 512), lambda c: (c, 0)),
                  pl.BlockSpec((t, 128), lambda c: (c, OFF_DTF // 128)),
                  pl.BlockSpec((1, 128), lambda c: (0, 0)),
                  pl.BlockSpec((1, 128), lambda c: (0, 0)),
                  pl.BlockSpec((1, D), lambda c: (0, 0))],
        out_specs=[pl.BlockSpec((t, D), lambda c: (c, 0)),
                   pl.BlockSpec((1, NSTATE, D), lambda c: (c, 0, 0))],
        out_shape=[jax.ShapeDtypeStruct((s, D), F32), jax.ShapeDtypeStruct((nc, NSTATE, D), F32)],
        scratch_shapes=[pltpu.VMEM((NSTATE, D), F32), pltpu.VMEM((t, D), BF16), pltpu.VMEM((t, D), BF16),
                        pltpu.VMEM((t, 128), F32), pltpu.VMEM((128, t), F32), pltpu.VMEM((t, D), F32)],
        compiler_params=_params(("arbitrary",)),
    )(xs_a, bc_a, p, bias128, alog128, dskip_x)


def _ssd_bwd(dy, xs_a, bc_a, p, states, bias128, alog128, dskip_x, s):
    nc = s // CHUNK
    t = CHUNK

    def body(dy_ref, xs_ref, bc_ref, dtf_ref, st_ref, bias_ref, alog_ref, dsk_ref,
             dxs_ref, dbc_ref, ddt_ref, dalog_ref, dskip_ref,
             dstate, x_sc, dy_sc, dx_sc, deo_sc, dwe_sc, cs_sc, cst_sc, dcol_sc, drow_sc):
        i = pl.program_id(0)

        @pl.when(i == 0)
        def _():
            dstate[...] = jnp.zeros(dstate.shape, F32)
            dalog_ref[...] = jnp.zeros(dalog_ref.shape, F32)
            dskip_ref[...] = jnp.zeros(dskip_ref.shape, F32)

        dtf = dtf_ref[...]
        dt, a_neg, cs = _ssd_gates(dtf, bias_ref[...], alog_ref[...])
        cs_sc[...] = cs
        cst_sc[...] = cs.T
        cs_last = cs[t - 1:t, :]
        eo, we, g_end = jnp.exp(cs), jnp.exp(cs_last - cs), jnp.exp(cs_last)
        expand, reduce = _head_expand(), _head_reduce()
        ex = _sel_right(jnp.concatenate([dt, eo, we], axis=0), expand, terms=2)
        dt_x, eo_x, we_x = ex[0:t], ex[t:2 * t], ex[2 * t:3 * t]
        g_x = _sel_right(jnp.broadcast_to(g_end, (8, 128)), expand)[0:1]
        xs = xs_ref[...]
        dyv = dy_ref[...]
        x = xs * dt_x
        x_sc[...] = x.astype(BF16)
        dy_sc[...] = dyv.astype(BF16)
        dyo_b = (dyv * eo_x).astype(BF16)
        xw_b = (x * we_x).astype(BF16)
        prev = st_ref[0]
        prev_b = prev.astype(BF16)
        dnext = dstate[...]
        dnext_b = dnext.astype(BF16)
        dcol_sc[...] = jnp.zeros(dcol_sc.shape, F32)
        drow_sc[...] = jnp.zeros(drow_sc.shape, F32)
        lane_row = _iota((1, 128), 1)
        sub_col = _iota((128, 1), 0)
        for g in range(2):
            cols = slice(g * 512, (g + 1) * 512)
            b_g = bc_ref[:, g * 128:(g + 1) * 128].astype(BF16)
            c_g = bc_ref[:, 256 + g * 128:256 + (g + 1) * 128].astype(BF16)
            gmat = _dot(c_g, b_g, NT)
            b_ds = _dot(b_g, dnext_b[:, cols])
            c_s = _dot(c_g, prev_b[:, cols])
            dx_sc[:, cols] = b_ds * we_x[:, cols]
            deo_sc[:, cols] = dyv[:, cols] * c_s
            dwe_sc[:, cols] = b_ds * x[:, cols]
            db = _dot(xw_b[:, cols], dnext_b[:, cols], NT)
            dc = _dot(dyo_b[:, cols], prev_b[:, cols], NT)
            dstate[:, cols] = g_x[:, cols] * dnext[:, cols] + _dot(c_g, dyo_b[:, cols], TN)
            dg = jnp.zeros((t, t), F32)
            for e in range(HG):
                h = g * HG + e
                hc = slice(h * HD, (h + 1) * HD)
                lmat = _decay_mask(cs_sc, cst_sc, h)
                m = gmat * lmat
                dx_sc[:, hc] += _dot(m.astype(BF16), dy_sc[:, hc], TN)
                dm = _dot(dy_sc[:, hc], x_sc[:, hc], NT)
                dg = dg + dm * lmat
                qm = dm * m
                dcol_sc[...] += jnp.sum(qm, axis=1, keepdims=True) * (lane_row == h).astype(F32)
                drow_sc[...] += (sub_col == h).astype(F32) * jnp.sum(qm, axis=0, keepdims=True)
            dg_b = dg.astype(BF16)
            dbc_ref[:, g * 128:(g + 1) * 128] = db + _dot(dg_b, c_g, TN)
            dbc_ref[:, 256 + g * 128:256 + (g + 1) * 128] = dc + _dot(dg_b, b_g)
        d_eo = _sel_right(deo_sc[...], reduce, terms=2)
        d_we = _sel_right(dwe_sc[...], reduce, terms=2)
        d_gend = _sel_right(jnp.broadcast_to(_colsum(dnext * prev), (8, D)), reduce)[0:1]
        d_cs = dcol_sc[...] - drow_sc[...].T + d_eo * eo - d_we * we
        extra = _colsum(d_we * we) + d_gend * g_end
        d_cs = d_cs + jnp.where(_iota((t, 128), 0) == t - 1, extra, 0.0)
        da = _sel_left(_tri_upper(t), d_cs)
        dx = dx_sc[...]
        ddt = _sel_right(dx * xs, reduce, terms=2) + da * a_neg
        dxs_ref[...] = dx * dt_x + dsk_ref[...] * dyv
        ddt_ref[...] = jnp.where(_iota((t, 128), 1) < NH, ddt * _sigmoid(dtf + bias_ref[...]), 0.0)
        dalog_ref[...] += _colsum(da * dt) * a_neg
        dskip_ref[...] += _sel_right(jnp.broadcast_to(_colsum(dyv * xs), (8, D)), reduce)[0:1]

    rev = lambda i: nc - 1 - i
    return pl.pallas_call(
        body, name="ssd_bwd", grid=(nc,),
        in_specs=[pl.BlockSpec((t, D), lambda i: (rev(i), 0)),
                  pl.BlockSpec((t, D), lambda i: (rev(i), 0)),
                  pl.BlockSpec((t, 512), lambda i: (rev(i), 0)),
                  pl.BlockSpec((t, 128), lambda i: (rev(i), OFF_DTF // 128)),
                  pl.BlockSpec((1, NSTATE, D), lambda i: (rev(i), 0, 0)),
                  pl.BlockSpec((1, 128), lambda i: (0, 0)),
                  pl.BlockSpec((1, 128), lambda i: (0, 0)),
                  pl.BlockSpec((1, D), lambda i: (0, 0))],
        out_specs=[pl.BlockSpec((t, D), lambda i: (rev(i), 0)),
                   pl.BlockSpec((t, 512), lambda i: (rev(i), 0)),
                   pl.BlockSpec((t, 128), lambda i: (rev(i), 0)),
                   pl.BlockSpec((1, 128), lambda i: (0, 0)),
                   pl.BlockSpec((1, 128), lambda i: (0, 0))],
        out_shape=[jax.ShapeDtypeStruct((s, D), F32), jax.ShapeDtypeStruct((s, 512), F32),
                   jax.ShapeDtypeStruct((s, 128), F32), jax.ShapeDtypeStruct((1, 128), F32),
                   jax.ShapeDtypeStruct((1, 128), F32)],
        scratch_shapes=[pltpu.VMEM((NSTATE, D), F32), pltpu.VMEM((t, D), BF16), pltpu.VMEM((t, D), BF16),
                        pltpu.VMEM((t, D), F32), pltpu.VMEM((t, D), F32), pltpu.VMEM((t, D), F32),
                        pltpu.VMEM((t, 128), F32), pltpu.VMEM((128, t), F32),
                        pltpu.VMEM((t, 128), F32), pltpu.VMEM((128, t), F32)],
        compiler_params=_params(("arbitrary",)),
    )(dy, xs_a, bc_a, p, states, bias128, alog128, dskip_x)


def _gate_lanes(shape):
    lane = _iota(shape, 1)
    return (lane >= NH) & (lane < 2 * NH)


def _cum_fwd(p, bias128, s):
    tr = min(512, s)

    def body(dtf_ref, bias_ref, o_ref, carry):
        @pl.when(pl.program_id(0) == 0)
        def _():
            carry[...] = jnp.zeros(carry.shape, F32)

        lf = jnp.where(_gate_lanes((tr, 128)), _log_sigmoid(dtf_ref[...] + bias_ref[...]), 0.0)
        cum = _sel_left(_tri_lower(tr), lf) + carry[...]
        carry[...] = cum[tr - 1:tr, :]
        o_ref[...] = cum

    return pl.pallas_call(
        body, name="cum_fwd", grid=(s // tr,),
        in_specs=[pl.BlockSpec((tr, 128), lambda i: (i, OFF_DTF // 128)), pl.BlockSpec((1, 128), lambda i: (0, 0))],
        out_specs=pl.BlockSpec((tr, 128), lambda i: (i, 0)),
        out_shape=jax.ShapeDtypeStruct((s, 128), F32),
        scratch_shapes=[pltpu.VMEM((1, 128), F32)],
        compiler_params=_params(("arbitrary",)),
    )(p, bias128)


def _cum_bwd(dcum, ddt_raw, p, bias128, s):
    tr = min(512, s)

    def fn(pos, dcum, ddt, dtf, bias, carry, acc):
        suffix = _sel_left(_tri_upper(tr), dcum) + carry
        dfr = jnp.where(_gate_lanes((tr, 128)), suffix * _sigmoid(-(dtf + bias)), 0.0)
        out = ddt + dfr
        return out, suffix[0:1, :], acc + _colsum(out)

    return _rowk("cum_bwd", fn, s, tr, [(dcum, 128, 0, 0), (ddt_raw, 128, 0, 0), (p, 128, OFF_DTF // 128, 0)],
                 [bias128], [(128, BF16)], [(1, 128), (1, 128)], reverse=True)


ATT_BLOCK = 512
ATT_STRIP = 32


def _head_part(shape, h, dim):
    i = _iota(shape, dim)
    return (i >= h * HD) & (i < (h + 1) * HD)


def _k_augmented(k_blk, cum_blk, j, h):
    tk = k_blk.shape[0]
    lane = _iota((tk, 128), 1)
    col = jnp.sum(jnp.where(lane == NH + 2 * j + h, cum_blk, 0.0), axis=1, keepdims=True)
    c0, c1, c2 = [c.astype(F32) for c in _split3(-col)]
    k_h = k_blk if h == 0 else pltpu.roll(k_blk, HD, 1)
    aug = jnp.where(lane == HD, c0, jnp.where(lane == HD + 1, c1, jnp.where(lane == HD + 2, c2, 0.0)))
    return jnp.where(lane < HD, k_h, aug).astype(BF16)


def _q_augmented_t(q_blk):
    tq = q_blk.shape[0]
    q_t = (q_blk * ATT_SCALE).T.astype(BF16)
    ones = (_iota((HD, tq), 0) < 3).astype(BF16)
    return [jnp.concatenate([q_t[h * HD:(h + 1) * HD], ones], axis=0) for h in range(2)]


def _rows01(r0, r1):
    sub = _iota((8, r0.shape[1]), 0)
    return jnp.where(sub == 0, r0, jnp.where(sub == 1, r1, 0.0))


def _fold8(x, op, cur):
    for g in range(x.shape[0] // 8):
        cur = op(cur, x[8 * g:8 * (g + 1), :])
    return cur


def _attn_fwd(p, cum, s):
    t = min(ATT_BLOCK, s)
    nq = s // t
    r = ATT_STRIP

    def body(q_ref, k_ref, v_ref, c_ref, o_ref, lse_ref, kaug_sc, vt_sc, s0_sc, s1_sc, p0_sc, p1_sc, m_sc, l_sc, acc_sc):
        j, qi = pl.program_id(0), pl.program_id(1)
        s_sc, p_sc = (s0_sc, s1_sc), (p0_sc, p1_sc)

        @pl.when(qi == 0)
        def _():
            for c in range(nq):
                rows = slice(c * t, (c + 1) * t)
                k_blk, vt = k_ref[rows, :], v_ref[rows, :].T
                for h in range(2):
                    kaug_sc[h, rows, :] = _k_augmented(k_blk, c_ref[rows, :], j, h)
                    vt_sc[h, :, rows] = vt[h * HD:(h + 1) * HD].astype(BF16)

        qaug_t = _q_augmented_t(q_ref[...])
        m_sc[...] = jnp.full(m_sc.shape, -1e30, F32)
        l_sc[...] = jnp.zeros(l_sc.shape, F32)
        acc_sc[...] = jnp.zeros(acc_sc.shape, F32)
        top = _iota((128, t), 0) < HD

        def logits(kb, buf):
            kv = pl.ds(pl.multiple_of(kb * t, t), t)
            for h in range(2):
                s_sc[buf][h] = _dot(kaug_sc[h, kv, :], qaug_t[h])

        def softmax(buf, diagonal):
            alphas = []
            for h in range(2):
                cur = jnp.full((8, t), -1e30, F32)
                for i in range(t // r):
                    rows = slice(i * r, (i + 1) * r)
                    x = s_sc[buf][h, rows, :]
                    if diagonal:
                        x = jnp.where(_iota((r, t), 1) >= i * r + _iota((r, t), 0), x, -1e30)
                        s_sc[buf][h, rows, :] = x
                    cur = _fold8(x, jnp.maximum, cur)
                m_prev = m_sc[h, 0:1, :]
                m_new = jnp.maximum(m_prev, jnp.max(cur, axis=0, keepdims=True))
                alpha = jnp.exp(m_prev - m_new)
                m_sc[h, 0:1, :] = m_new
                alphas.append(alpha)
                tot = jnp.zeros((8, t), F32)
                for i in range(t // r):
                    rows = slice(i * r, (i + 1) * r)
                    pr = jnp.exp(s_sc[buf][h, rows, :] - m_new)
                    p_sc[buf][h, rows, :] = pr.astype(BF16)
                    tot = _fold8(pr, jnp.add, tot)
                l_sc[h, 0:1, :] = alpha * l_sc[h, 0:1, :] + jnp.sum(tot, axis=0, keepdims=True)
            return alphas

        def accumulate(kb, buf, alphas):
            kv = pl.ds(pl.multiple_of(kb * t, t), t)
            for h in range(2):
                part = slice(h * HD, (h + 1) * HD)
                acc_sc[part, :] = acc_sc[part, :] * alphas[h] + _dot(vt_sc[h, :, kv], p_sc[buf][h])

        def first_trip():
            logits(0, 1)
            accumulate(qi, 0, softmax(0, True))
            logits(jnp.minimum(1, qi - 1), 0)
            return tuple(softmax(1, False))

        def only_diagonal():
            accumulate(qi, 0, softmax(0, True))
            return (jnp.ones((1, t), F32),) * 2

        def steady(u, alphas_b):
            accumulate(2 * u - 2, 1, alphas_b)
            logits(2 * u, 1)
            accumulate(2 * u - 1, 0, softmax(0, False))
            logits(jnp.minimum(2 * u + 1, qi - 1), 0)
            return tuple(softmax(1, False))

        logits(qi, 0)
        n_blocks = qi + 1
        alphas_b = lax.cond(qi >= 1, first_trip, only_diagonal)
        alphas_b = lax.fori_loop(1, n_blocks // 2, steady, alphas_b)
        last_b = 2 * (n_blocks // 2) - 2

        @pl.when((qi >= 1) & (n_blocks % 2 == 0))
        def _():
            accumulate(last_b, 1, alphas_b)

        @pl.when((qi >= 2) & (n_blocks % 2 == 1))
        def _():
            accumulate(last_b, 1, alphas_b)
            accumulate(qi - 1, 0, softmax(0, False))

        l0, l1 = l_sc[0, 0:1, :], l_sc[1, 0:1, :]
        o_ref[...] = (acc_sc[...] / jnp.where(top, l0, l1)).T
        lse_ref[0] = _rows01(m_sc[0, 0:1, :] + jnp.log(l0), m_sc[1, 0:1, :] + jnp.log(l1))

    return pl.pallas_call(
        body, name="attn_fwd", grid=(NH // 2, nq),
        in_specs=[pl.BlockSpec((t, 128), lambda j, qi: (qi, OFF_Q // 128 + j)),
                  pl.BlockSpec((s, 128), lambda j, qi: (0, OFF_K // 128 + j)),
                  pl.BlockSpec((s, 128), lambda j, qi: (0, OFF_V // 128 + j)),
                  pl.BlockSpec((s, 128), lambda j, qi: (0, 0))],
        out_specs=[pl.BlockSpec((t, 128), lambda j, qi: (qi, j)),
                   pl.BlockSpec((1, 8, t), lambda j, qi: (j, 0, qi))],
        out_shape=[jax.ShapeDtypeStruct((s, D), F32), jax.ShapeDtypeStruct((NH // 2, 8, s), F32)],
        scratch_shapes=[pltpu.VMEM((2, s, 128), BF16), pltpu.VMEM((2, HD, s), BF16), pltpu.VMEM((2, t, t), F32),
                        pltpu.VMEM((2, t, t), F32), pltpu.VMEM((2, t, t), BF16), pltpu.VMEM((2, t, t), BF16),
                        pltpu.VMEM((2, 8, t), F32), pltpu.VMEM((2, 8, t), F32), pltpu.VMEM((128, t), F32)],
        compiler_params=_params(("parallel", "arbitrary")),
    )(p, p, p, cum)


def _attn_bwd(p, cum, o, lse, do, s):
    t = min(ATT_BLOCK, s)
    nq = s // t
    r = ATT_STRIP

    def body(q_ref, k_ref, v_ref, c_ref, o_ref, lse_ref, do_ref, dq_ref, dk_ref, dv_ref, dc_ref, dr_ref,
             qaugt_sc, qh_sc, dot_sc, doh_sc, delta_sc, dqt_sc, dr_sc, kaug_sc, vh_sc, kt_sc,
             s0_sc, s1_sc, dp0_sc, dp1_sc, p0_sc, p1_sc, ds0_sc, ds1_sc, dk_sc, dv_sc, dc_sc):
        j, ki = pl.program_id(0), pl.program_id(1)
        s_sc, dp_sc, p_sc, ds_sc = (s0_sc, s1_sc), (dp0_sc, dp1_sc), (p0_sc, p1_sc), (ds0_sc, ds1_sc)

        @pl.when(ki == 0)
        def _():
            for c in range(nq):
                rows = slice(c * t, (c + 1) * t)
                q_blk, do_blk = q_ref[rows, :], do_ref[rows, :]
                qaugt_sc[0, :, rows], qaugt_sc[1, :, rows] = _q_augmented_t(q_blk)
                dot_sc[:, rows] = do_blk.T.astype(BF16)
                prod_t = (do_blk * o_ref[rows, :]).T
                delta_sc[:, rows] = _rows01(jnp.sum(prod_t[0:HD], axis=0, keepdims=True),
                                            jnp.sum(prod_t[HD:], axis=0, keepdims=True))
                for h in range(2):
                    head = _head_part((t, 128), h, 1)
                    qh_sc[h, rows, :] = jnp.where(head, q_blk * ATT_SCALE, 0.0).astype(BF16)
                    doh_sc[h, rows, :] = jnp.where(head, do_blk, 0.0).astype(BF16)
            dqt_sc[...] = jnp.zeros(dqt_sc.shape, F32)
            dr_sc[...] = jnp.zeros(dr_sc.shape, F32)

        k_blk, v_blk = k_ref[...], v_ref[...]
        kt = k_blk.T
        for h in range(2):
            kaug_sc[h] = _k_augmented(k_blk, c_ref[...], j, h)
            vh_sc[h] = jnp.where(_head_part((t, 128), h, 1), v_blk, 0.0).astype(BF16)
            kt_sc[h] = kt[h * HD:(h + 1) * HD].astype(BF16)
        dk_sc[...] = jnp.zeros(dk_sc.shape, F32)
        dv_sc[...] = jnp.zeros(dv_sc.shape, F32)
        dc_sc[...] = jnp.zeros(dc_sc.shape, F32)

        def inputs(qb, buf):
            qs = pl.ds(pl.multiple_of(qb * t, t), t)
            for h in range(2):
                s_sc[buf][h] = _dot(kaug_sc[h], qaugt_sc[h, :, qs])
                dp_sc[buf][h] = _dot(vh_sc[h], dot_sc[:, qs])

        def elementwise(qb, buf, diagonal):
            qs = pl.ds(pl.multiple_of(qb * t, t), t)
            for h in range(2):
                lse_row, delta_row = lse_ref[0, h:h + 1, qs], delta_sc[h:h + 1, qs]
                tot = jnp.zeros((8, t), F32)
                for i in range(t // r):
                    rows = slice(i * r, (i + 1) * r)
                    x = s_sc[buf][h, rows, :]
                    if diagonal:
                        x = jnp.where(_iota((r, t), 1) >= i * r + _iota((r, t), 0), x, -1e30)
                    pr = jnp.exp(x - lse_row)
                    ds = pr * (dp_sc[buf][h, rows, :] - delta_row)
                    p_sc[buf][h, rows, :] = pr.astype(BF16)
                    ds_sc[buf][h, rows, :] = ds.astype(BF16)
                    dc_sc[h, rows, :] += sum(ds[:, 128 * g:128 * (g + 1)] for g in range(t // 128))
                    tot = _fold8(ds, jnp.add, tot)
                dr_sc[h, :, qs] += tot

        def outputs(qb, buf):
            qs = pl.ds(pl.multiple_of(qb * t, t), t)
            dv_sc[...] += _dot(p_sc[buf][0], doh_sc[0, qs, :]) + _dot(p_sc[buf][1], doh_sc[1, qs, :])
            dk_sc[...] += _dot(ds_sc[buf][0], qh_sc[0, qs, :]) + _dot(ds_sc[buf][1], qh_sc[1, qs, :])
            for h in range(2):
                dqt_sc[h * HD:(h + 1) * HD, qs] += _dot(kt_sc[h], ds_sc[buf][h])

        def pair(a, b, a_diagonal):
            inputs(a, 0)
            inputs(b, 1)
            elementwise(a, 0, a_diagonal)
            outputs(a, 0)
            elementwise(b, 1, False)
            outputs(b, 1)

        def later(u, carry):
            pair(ki + 1 + 2 * u, ki + 2 + 2 * u, False)
            return carry

        n_later = nq - 1 - ki
        lax.fori_loop(0, n_later // 2, later, 0)

        @pl.when(n_later % 2 == 1)
        def _():
            pair(ki, nq - 1, True)

        @pl.when(n_later % 2 == 0)
        def _():
            inputs(ki, 0)
            elementwise(ki, 0, True)
            outputs(ki, 0)

        dk_ref[...] = dk_sc[...].astype(BF16)
        dv_ref[...] = dv_sc[...].astype(BF16)
        lane = _iota((t, 128), 1)
        cols = jnp.where(lane == 0, jnp.sum(dc_sc[0], axis=1, keepdims=True),
                         jnp.where(lane == 1, jnp.sum(dc_sc[1], axis=1, keepdims=True), 0.0))
        dc_ref[0] = cols.T[0:8, :]

        @pl.when(ki == nq - 1)
        def _():
            for c in range(nq):
                rows = slice(c * t, (c + 1) * t)
                dq_ref[rows, :] = dqt_sc[:, rows].T * ATT_SCALE
            dr_ref[0] = _rows01(jnp.sum(dr_sc[0], axis=0, keepdims=True), jnp.sum(dr_sc[1], axis=0, keepdims=True))

    whole = lambda off: pl.BlockSpec((s, 128), functools.partial(lambda j, ki, off: (0, off + j), off=off))
    return pl.pallas_call(
        body, name="attn_bwd", grid=(NH // 2, nq),
        in_specs=[whole(OFF_Q // 128),
                  pl.BlockSpec((t, 128), lambda j, ki: (ki, OFF_K // 128 + j)),
                  pl.BlockSpec((t, 128), lambda j, ki: (ki, OFF_V // 128 + j)),
                  pl.BlockSpec((t, 128), lambda j, ki: (ki, 0)),
                  whole(0),
                  pl.BlockSpec((1, 8, s), lambda j, ki: (j, 0, 0)),
                  whole(0)],
        out_specs=[whole(0),
                   pl.BlockSpec((t, 128), lambda j, ki: (ki, j)),
                   pl.BlockSpec((t, 128), lambda j, ki: (ki, j)),
                   pl.BlockSpec((1, 8, t), lambda j, ki: (j, 0, ki)),
                   pl.BlockSpec((1, 8, s), lambda j, ki: (j, 0, 0))],
        out_shape=[jax.ShapeDtypeStruct((s, D), F32), jax.ShapeDtypeStruct((s, D), BF16), jax.ShapeDtypeStruct((s, D), BF16),
                   jax.ShapeDtypeStruct((NH // 2, 8, s), F32), jax.ShapeDtypeStruct((NH // 2, 8, s), F32)],
        scratch_shapes=[pltpu.VMEM((2, 128, s), BF16), pltpu.VMEM((2, s, 128), BF16), pltpu.VMEM((128, s), BF16),
                        pltpu.VMEM((2, s, 128), BF16), pltpu.VMEM((8, s), F32), pltpu.VMEM((128, s), F32),
                        pltpu.VMEM((2, 8, s), F32), pltpu.VMEM((2, t, 128), BF16), pltpu.VMEM((2, t, 128), BF16),
                        pltpu.VMEM((2, HD, t), BF16)]
        + [pltpu.VMEM((2, t, t), F32)] * 4 + [pltpu.VMEM((2, t, t), BF16)] * 4
        + [pltpu.VMEM((t, 128), F32), pltpu.VMEM((t, 128), F32), pltpu.VMEM((2, t, 128), F32)],
        compiler_params=_params(("parallel", "arbitrary")),
    )(p, p, p, cum, o, lse, do)


def _ln_stats(u):
    mu = _mean(u)
    d = u - mu
    rstd = lax.rsqrt(_mean(d * d) + EPS)
    return d * rstd, rstd


def _ln_bwd(dx, xh, rstd, gam):
    dxh = dx * gam
    return rstd * (dxh - _mean(dxh) - xh * _mean(dxh * xh))


def _rms_bwd(d, xn, r, w):
    t = d * w
    return r * (t - xn * _mean(t * xn)), _colsum(d * xn)


def _mix_norm(y, p, att, w_ssm, w_att, s):
    def fn(pos, y, z, att, w1, w2):
        g = y * _silu(z)
        n1 = g * lax.rsqrt(_mean(g * g) + EPS) * w1
        n2 = att * lax.rsqrt(_mean(att * att) + EPS) * w2
        return (jnp.concatenate([n1, n2], axis=1),)

    return _rowk("mix_norm", fn, s, 512, [(y, D, 0, 0), (p, D, OFF_Z // D, 0), (att, D, 0, 0)],
                 [w_ssm, w_att], [(2 * D, BF16)], [])[0]


def _mix_norm_bwd(dmix, y, p, att, w_ssm, w_att, s):
    def fn(pos, dmix, y, z, att, w1, w2, a1, a2):
        sz = _silu(z)
        g = y * sz
        r1 = lax.rsqrt(_mean(g * g) + EPS)
        dg, dw1 = _rms_bwd(dmix[:, :D], g * r1, r1, w1)
        r2 = lax.rsqrt(_mean(att * att) + EPS)
        datt, dw2 = _rms_bwd(dmix[:, D:], att * r2, r2, w2)
        return dg * sz, dg * y * _dsilu(z), datt, a1 + dw1, a2 + dw2

    return _rowk("mix_norm_bwd", fn, s, 256, [(dmix, 2 * D, 0, 0), (y, D, 0, 0), (p, D, OFF_Z // D, 0), (att, D, 0, 0)],
                 [w_ssm, w_att], [(D, F32), (D, BF16), (D, F32)], [(1, D), (1, D)])


def _ln1(x0, y, g1, gam, bet, sc2, sh2, s):
    def fn(pos, x0, y, g1, gam, bet, sc2, sh2):
        xh, _ = _ln_stats(ALPHA * x0 + (1.0 + g1) * y)
        x1 = xh * gam + bet
        return x1, _modulate(x1, sc2, sh2)

    return _rowk("ln1", fn, s, 512, [(x0, D, 0, 0), (y, D, 0, 0)], [g1, gam, bet, sc2, sh2], [(D, F32), (D, BF16)], [])


def _ln2_loss(x1, ff, tgt, g2, gam, bet, s):
    def fn(pos, x1, ff, tgt, g2, gam, bet, a_loss, a_dgam, a_dbet, a_dg2):
        xh, rstd = _ln_stats(ALPHA * x1 + (1.0 + g2) * ff)
        err = xh * gam + bet - tgt
        dx2 = err * (1.0 / D)
        du = _ln_bwd(dx2, xh, rstd, gam)
        return (du, du * (1.0 + g2), a_loss + _colsum(err * err), a_dgam + _colsum(dx2 * xh),
                a_dbet + _colsum(dx2), a_dg2 + _colsum(du * ff))

    return _rowk("ln2_loss", fn, s, 512, [(x1, D, 0, 0), (ff, D, 0, 0), (tgt, D, 0, 0)], [g2, gam, bet],
                 [(D, F32), (D, BF16)], [(1, D)] * 4)


def _ln1_bwd(dh2, du2, x0, y, g1, gam, bet, sc2, s):
    def fn(pos, dh2, du2, x0, y, g1, gam, bet, sc2, a_sc, a_sh, a_gam, a_bet, a_g1):
        xh, rstd = _ln_stats(ALPHA * x0 + (1.0 + g1) * y)
        x1 = xh * gam + bet
        dx1 = ALPHA * du2 + dh2 * (1.0 + sc2)
        du1 = _ln_bwd(dx1, xh, rstd, gam)
        return (du1, du1 * (1.0 + g1), a_sc + _colsum(dh2 * x1), a_sh + _colsum(dh2), a_gam + _colsum(dx1 * xh),
                a_bet + _colsum(dx1), a_g1 + _colsum(du1 * y))

    return _rowk("ln1_bwd", fn, s, 512, [(dh2, D, 0, 0), (du2, D, 0, 0), (x0, D, 0, 0), (y, D, 0, 0)],
                 [g1, gam, bet, sc2], [(D, F32), (D, BF16)], [(1, D)] * 5)


def _input_grad(dh1, du1, x0, sc1, s):
    def fn(pos, dh1, du1, x0, sc1, a_sc, a_sh):
        return ALPHA * du1 + dh1 * (1.0 + sc1), a_sc + _colsum(dh1 * x0), a_sh + _colsum(dh1)

    return _rowk("input_grad", fn, s, 512, [(dh1, D, 0, 0), (du1, D, 0, 0), (x0, D, 0, 0)], [sc1],
                 [(D, F32)], [(1, D)] * 2)


def _adamw_math(w, grad, m, v):
    m_new = ADAM_B1 * m + (1.0 - ADAM_B1) * grad
    v_new = ADAM_B2 * v + (1.0 - ADAM_B2) * (grad * grad)
    m_hat = m_new / (1.0 - ADAM_B1 ** ADAM_STEP)
    v_hat = v_new / (1.0 - ADAM_B2 ** ADAM_STEP)
    return -ADAM_LR * (m_hat / (jnp.sqrt(v_hat) + ADAM_EPS) + ADAM_WD * w), m_new, v_new


def _small_update(small_all, layout, w, m, v):
    names = [n for n, _, _ in layout]

    def body(*refs):
        all_ref = refs[0]
        w_refs, m_refs, v_refs = [refs[1 + k * len(names):1 + (k + 1) * len(names)] for k in range(3)]
        sum_ref = refs[1 + 3 * len(names)]
        outs = refs[2 + 3 * len(names):]
        total = all_ref[0]
        for k in range(1, N_DEV):
            total = total + all_ref[k]
        sum_ref[...] = total
        for i, (_, off, size) in enumerate(layout):
            grad = total[:, off:off + size]
            delta, m_new, v_new = _adamw_math(w_refs[i][...], grad, m_refs[i][...], v_refs[i][...])
            for o, val in zip(outs[4 * i:4 * i + 4], (grad, delta, m_new, v_new)):
                o[...] = val

    res = pl.pallas_call(
        body, name="small_update",
        out_shape=[jax.ShapeDtypeStruct(small_all.shape[1:], F32)]
        + [jax.ShapeDtypeStruct(w[n].shape, F32) for n in names for _ in range(4)],
        compiler_params=_params(None),
    )(small_all, *[w[n] for n in names], *[m[n] for n in names], *[v[n] for n in names])
    return res[0], {n: res[1 + 4 * i:5 + 4 * i] for i, n in enumerate(names)}


def _adamw(name, w, g, m, v, *, tr, slots, by_columns=False):
    r, c = w.shape

    def body(w_ref, g_ref, m_ref, v_ref, g_out, d_out, m_out, v_out):
        if slots:
            grad = g_ref[0].astype(F32)
            for k in range(1, N_DEV):
                grad = grad + g_ref[k].astype(F32)
        else:
            grad = g_ref[...]
        g_out[...] = grad
        d_out[...], m_out[...], v_out[...] = _adamw_math(w_ref[...], grad, m_ref[...], v_ref[...])

    if by_columns:
        tile = pl.BlockSpec((r, tr), lambda i: (0, i))
        g_spec = pl.BlockSpec((N_DEV, r, tr), lambda i: (0, 0, i)) if slots else tile
    else:
        tile = pl.BlockSpec((tr, c), lambda i: (i, 0))
        g_spec = pl.BlockSpec((N_DEV, tr, c), lambda i: (0, i, 0)) if slots else tile
    return pl.pallas_call(
        body, name=name, grid=((c if by_columns else r) // tr,),
        in_specs=[tile, g_spec, tile, tile], out_specs=[tile] * 4,
        out_shape=[jax.ShapeDtypeStruct((r, c), F32)] * 4,
        compiler_params=_params(("parallel",)),
    )(w, g, m, v)


def _dot_f32(a, b, dims=NN):
    a0, a1, a2 = _split3(a)
    b0, b1, b2 = _split3(b)
    acc = _dot(a0, b0, dims)
    for x, y in ((a0, b1), (a1, b0), (a1, b1), (a0, b2), (a2, b0)):
        acc = acc + _dot(x, y, dims)
    return acc


def _ada_mod(c_all, w_shard, b_shard):
    def body(c_ref, w_ref, b_ref, o_ref):
        act = _silu(c_ref[...])
        act16 = jnp.concatenate([act, jnp.zeros_like(act)], axis=0)
        o_ref[...] = _dot_f32(act16, w_ref[...])[0:N_DEV] + b_ref[...]

    return pl.pallas_call(
        body, name="ada_mod", out_shape=jax.ShapeDtypeStruct((N_DEV, w_shard.shape[1]), F32),
        compiler_params=_params(None),
    )(c_all, w_shard, b_shard)


def _ada_grad(c_all, dmod_cols):
    def body(c_ref, dc_ref, gw_ref):
        act = _silu(c_ref[...])
        act16 = jnp.concatenate([act, jnp.zeros_like(act)], axis=0)
        dm = dc_ref[...]
        dm16 = jnp.concatenate([dm, jnp.zeros_like(dm)], axis=0)
        gw_ref[...] = _dot_f32(act16, dm16, TN)

    return pl.pallas_call(
        body, name="ada_grad", out_shape=jax.ShapeDtypeStruct((D, dmod_cols.shape[1]), F32),
        compiler_params=_params(None),
    )(c_all, dmod_cols)


def _exchange(name, xs, scatter):
    n = len(xs)
    n_peer = N_DEV - 1

    def body(*refs):
        x_refs, o_refs = refs[:n], refs[n:2 * n]
        send_sems, recv_sems, local_sems = refs[2 * n:]
        mx, my, mc = lax.axis_index("x"), lax.axis_index("y"), lax.axis_index("c")
        me = 4 * mx + 2 * my + mc

        def src(a, slot):
            return x_refs[a].at[slot] if scatter else x_refs[a]

        own = [pltpu.make_async_copy(src(a, me), o_refs[a].at[me], local_sems.at[a]) for a in range(n)]
        for cp in own:
            cp.start()
        sends = []
        for d in range(1, N_DEV):
            px = 1 - mx if d & 4 else mx
            py = 1 - my if d & 2 else my
            pc = 1 - mc if d & 1 else mc
            peer = 4 * px + 2 * py + pc
            for a in range(n):
                def copy(src_slot, dst_slot, a=a, d=d, to=(px, py, pc)):
                    return pltpu.make_async_remote_copy(
                        src_ref=src(a, src_slot), dst_ref=o_refs[a].at[dst_slot],
                        send_sem=send_sems.at[a * n_peer + d - 1], recv_sem=recv_sems.at[a * n_peer + d - 1],
                        device_id=to, device_id_type=pl.DeviceIdType.MESH)

                out = copy(peer, me)
                out.start()
                sends.append((out, copy(me, peer)))
        for _, arrival in sends:
            arrival.wait_recv()
        for out, _ in sends:
            out.wait_send()
        for cp in own:
            cp.wait()

    shapes = [tuple(x.shape[1:] if scatter else x.shape) for x in xs]
    return pl.pallas_call(
        body, name=name,
        in_specs=[pl.BlockSpec(memory_space=pl.ANY)] * n, out_specs=[pl.BlockSpec(memory_space=pl.ANY)] * n,
        out_shape=[jax.ShapeDtypeStruct((N_DEV,) + sh, x.dtype) for sh, x in zip(shapes, xs)],
        scratch_shapes=[pltpu.SemaphoreType.DMA((n * n_peer,)), pltpu.SemaphoreType.DMA((n * n_peer,)),
                        pltpu.SemaphoreType.DMA((n,))],
        compiler_params=pltpu.CompilerParams(has_side_effects=True),
    )(*xs)


def _gather_two_level(name, x):
    def body(x_ref, o_ref, send_sems, recv_sems, local_sem):
        mx, my, mc = lax.axis_index("x"), lax.axis_index("y"), lax.axis_index("c")
        me, sibling = (mx, my, mc), (mx, my, 1 - mc)
        chips = [(1 - mx, my), (mx, 1 - my), (1 - mx, 1 - my)]

        def slot(px, py, pc):
            return o_ref.at[4 * px + 2 * py + pc]

        def copy(k, block, to, src=None):
            return pltpu.make_async_remote_copy(
                src_ref=slot(*block) if src is None else src, dst_ref=slot(*block),
                send_sem=send_sems.at[k], recv_sem=recv_sems.at[k], device_id=to, device_id_type=pl.DeviceIdType.MESH)

        mine = pltpu.make_async_copy(x_ref, slot(*me), local_sem)
        mine.start()
        first = [copy(0, me, sibling, src=x_ref)] + [copy(1 + i, me, (*chip, mc), src=x_ref) for i, chip in enumerate(chips)]
        for cp in first:
            cp.start()
        passed = [copy(4 + i, (*chip, mc), sibling) for i, chip in enumerate(chips)]
        for i, chip in enumerate(chips):
            copy(1 + i, (*chip, mc), me).wait_recv()
            passed[i].start()
        copy(0, sibling, me).wait_recv()
        for i, chip in enumerate(chips):
            copy(4 + i, (*chip, 1 - mc), me).wait_recv()
        for cp in first + passed:
            cp.wait_send()
        mine.wait()

    return pl.pallas_call(
        body, name=name,
        in_specs=[pl.BlockSpec(memory_space=pl.ANY)], out_specs=pl.BlockSpec(memory_space=pl.ANY),
        out_shape=jax.ShapeDtypeStruct((N_DEV,) + tuple(x.shape), x.dtype),
        scratch_shapes=[pltpu.SemaphoreType.DMA((7,)), pltpu.SemaphoreType.DMA((7,)), pltpu.SemaphoreType.DMA(())],
        compiler_params=pltpu.CompilerParams(has_side_effects=True),
    )(x)


def _after(x, zero):
    return x if zero is None else x + zero.reshape(-1)[0].astype(x.dtype)


def _exchange_copies(x_refs, land_refs, send_sems, recv_sems, scatter):
    n = len(x_refs)
    n_peer = N_DEV - 1
    mx, my, mc = lax.axis_index("x"), lax.axis_index("y"), lax.axis_index("c")
    me = 4 * mx + 2 * my + mc
    pairs = []
    for d in range(1, N_DEV):
        px = 1 - mx if d & 4 else mx
        py = 1 - my if d & 2 else my
        pc = 1 - mc if d & 1 else mc
        peer = 4 * px + 2 * py + pc
        for a in range(n):
            def copy(src_slot, dst_slot, a=a, d=d, to=(px, py, pc)):
                return pltpu.make_async_remote_copy(
                    src_ref=x_refs[a].at[src_slot] if scatter else x_refs[a], dst_ref=land_refs[a].at[dst_slot],
                    send_sem=send_sems.at[a * n_peer + d - 1], recv_sem=recv_sems.at[a * n_peer + d - 1],
                    device_id=to, device_id_type=pl.DeviceIdType.MESH)

            pairs.append((copy(peer, me), copy(me, peer)))
    return me, pairs


def _exchange_async(name, xs, scatter, collective_id):
    n = len(xs)
    shapes = [tuple(x.shape[1:] if scatter else x.shape) for x in xs]
    x_refs = [jax.new_ref(x, memory_space=pltpu.MemorySpace.HBM) for x in xs]
    land_refs = [jax.empty_ref(jax.ShapeDtypeStruct((N_DEV,) + sh, x.dtype), memory_space=pltpu.MemorySpace.HBM)
                 for sh, x in zip(shapes, xs)]

    @pl.kernel(mesh=plsc.ScalarSubcoreMesh(axis_name="sequencer", num_cores=1), name=name,
               scratch_types=(pltpu.SemaphoreType.DMA((n * (N_DEV - 1),)), pltpu.SemaphoreType.DMA((n * (N_DEV - 1),)),
                              pltpu.SemaphoreType.DMA((n,))),
               compiler_params=pltpu.CompilerParams(collective_id=collective_id))
    def launch(send_sems, recv_sems, own_sems):
        barrier = pltpu.get_barrier_semaphore()
        mx, my, mc = lax.axis_index("x"), lax.axis_index("y"), lax.axis_index("c")
        for d in range(1, N_DEV):
            peer = (1 - mx if d & 4 else mx, 1 - my if d & 2 else my, 1 - mc if d & 1 else mc)
            pl.semaphore_signal(barrier, inc=1, device_id=peer, device_id_type=pl.DeviceIdType.MESH)
        pl.semaphore_wait(barrier, N_DEV - 1)
        me, pairs = _exchange_copies(x_refs, land_refs, send_sems, recv_sems, scatter)
        own = [pltpu.make_async_copy(x_refs[a].at[me] if scatter else x_refs[a], land_refs[a].at[me], own_sems.at[a])
               for a in range(n)]
        for cp in own:
            cp.start()
        for out, _ in pairs:
            out.start()
        for out, arrival in pairs:
            arrival.wait_recv()
            out.wait_send()
        for cp in own:
            cp.wait()

    launch()
    return lambda: [r[...] for r in land_refs]


def _relu2(a):
    r = jnp.maximum(a, 0.0)
    return r * r


def _relu2_grad(acc, r):
    return acc * (2.0 * jnp.sqrt(r.astype(F32)))


def _local_step(x0, tgt, mod, wcat_t, late_weights, send_grads, conv_w, conv_b, dt_bias, a_log, d_skip, ssm_norm_w, f_bias,
                attn_norm_w, ln1_g, ln1_b, ln2_g, ln2_b):
    ff_w = DFF // N_DEV
    s = x0.shape[0]
    tm = min(1024, s)
    ts = min(2048, s)
    sh1, sc1, g1, sh2, sc2, g2 = [mod[:, i * D:(i + 1) * D] for i in range(6)]
    zero = jnp.zeros((1, 128 - 2 * NH), F32)
    bias128 = jnp.concatenate([dt_bias, f_bias, zero], axis=1)
    alog128 = jnp.concatenate([a_log, jnp.zeros((1, 128 - NH), F32)], axis=1)
    dskip_x = jnp.repeat(d_skip, HD, axis=1)
    w_xs, w_bc, b_xs, b_bc = conv_w[:, :D], conv_w[:, D:], conv_b[:, :D], conv_b[:, D:]

    h1, = _rowk("modulate", lambda pos, x, sc, sh: (_modulate(x, sc, sh),), s, 512, [(x0, D, 0, 0)], [sc1, sh1], [(D, BF16)], [])
    p = _mm_nt("in_proj", [(h1, D, 0)], [(wcat_t, D, 0)], n=PCOLS, tm=tm, tn=1152, out_dtype=F32)
    xs_a, bc_a = _conv_fwd(p, w_xs, b_xs, w_bc, b_bc, s)
    y_ssd, states = _ssd_fwd(xs_a, bc_a, p, bias128, alog128, dskip_x, s)
    cum = _cum_fwd(p, bias128, s)
    att, lse = _attn_fwd(p, cum, s)
    wout, w1s, w2 = late_weights()
    ymix = _mix_norm(y_ssd, p, att, ssm_norm_w, attn_norm_w, s)
    y = _mm_nn("out_proj", ymix, wout, tm=tm, tn=1024, tk=2 * D, out_dtype=F32)
    x1, h2 = _ln1(x0, y, g1, ln1_g, ln1_b, sc2, sh2, s)
    tall = min(2048, s)
    r = _mm_nn("ff_in", h2, w1s, tm=tall, tn=ff_w, tk=D, out_dtype=BF16, epi=_relu2)
    ff = _mm_nn("ff_out", r, w2, tm=tall, tn=1024, tk=1024, out_dtype=F32)
    du2, dff, sq_err, d_ln2_g, d_ln2_b, d_g2 = _ln2_loss(x1, ff, tgt, g2, ln2_g, ln2_b, s)

    da1 = _mm_nt("d_ff_hidden", [(dff, D, 0)], [(w2, D, 0)], n=DFF, tm=tm, tn=1024, out_dtype=BF16, epi=_relu2_grad,
                 epi_aux=(r,))
    d_w2 = _mm_tn("d_w_ff_out", r, dff, tm=1024, tn=1024, ts=ts)
    d_w1s = _mm_tn("d_w_ff_in", h2, da1, tm=1024, tn=ff_w, ts=ts, col_shards=True)
    dh2 = _mm_nt("d_ff_input", [(da1, ff_w, k) for k in range(N_DEV)], [(w1s, ff_w, k) for k in range(N_DEV)], n=D,
                 tm=min(512, s), tn=1024, out_dtype=F32)
    du1, dy, d_sc2, d_sh2, d_ln1_g, d_ln1_b, d_g1 = _ln1_bwd(dh2, du2, x0, y, g1, ln1_g, ln1_b, sc2, s)

    dmix = _mm_nt("d_mix", [(dy, D, 0)], [(wout, D, 0)], n=2 * D, tm=tm, tn=1024, out_dtype=F32)
    d_wout = _mm_tn("d_w_out", ymix, dy, tm=1024, tn=1024, ts=ts)
    sent = send_grads("late", [d_w1s, d_w2.reshape(N_DEV, -1, D), d_wout.reshape(N_DEV, -1, D)])
    dy_ssd, dz, datt, d_ssm_w, d_attn_w = _mix_norm_bwd(dmix, y_ssd, p, att, _after(ssm_norm_w, sent), attn_norm_w, s)
    dq, dk, dv, dcs, drs = _attn_bwd(p, cum, att, lse, datt, s)
    dxs_a, dbc_a, ddt_raw, d_alog, d_dskip = _ssd_bwd(dy_ssd, xs_a, bc_a, p, states, bias128, alog128, dskip_x, s)
    dcum = jnp.pad((drs - dcs)[:, :2, :].reshape(NH, s).T, ((0, 0), (NH, 128 - 2 * NH)))
    ddtf, _, d_bias = _cum_bwd(dcum, ddt_raw, p, bias128, s)
    dxs, dbc, d_wc_xs, d_bc_xs, d_wc_bc, d_bc_bc = _conv_bwd(dxs_a, dbc_a, p, w_xs, b_xs, w_bc, b_bc, s)

    segs = [(dz, OFF_Z, D), (dxs, OFF_XS, D), (dq, OFF_Q, D), (dk, OFF_K, D), (dv, OFF_V, D), (dbc, OFF_BC, 512),
            (ddtf, OFF_DTF, 128)]
    d_z, d_xs, d_q, d_k, d_v, d_bcw, d_dtf = [
        _mm_tn("d_w_in_%d" % i, a, h1, tm=min(w, 1024), tn=1024, ts=ts)
        for i, (a, _, w) in enumerate(segs)]
    d_w_in_t = dict(z=d_z, xs=d_xs, bc=d_bcw, dt=d_dtf[:NH], q=d_q, k=d_k, v=d_v, f=d_dtf[NH:2 * NH])
    sent = send_grads("in", [_shard_w_in_grad_t(d_w_in_t)])
    segs[-1] = (_after(ddtf, sent), OFF_DTF, 128)
    dh1 = _mm_nt("d_h1", [(a, w, 0) for a, _, w in segs], [(wcat_t, w, off // w) for _, off, w in segs], n=D,
                 tm=min(512, s), tn=1024, out_dtype=F32, b_rows=True)
    grad_x, d_sc1, d_sh1 = _input_grad(dh1, du1, x0, sc1, s)

    return dict(
        loss=(0.5 / D) * jnp.sum(sq_err), grad_x=grad_x,
        d_mod=jnp.concatenate([d_sh1, d_sc1, d_g1, d_sh2, d_sc2, d_g2], axis=1),
        d_conv_w=jnp.concatenate([d_wc_xs[:4], d_wc_bc[:4]], axis=1), d_conv_b=jnp.concatenate([d_bc_xs, d_bc_bc], axis=1),
        d_ssm_norm_w=d_ssm_w, d_attn_norm_w=d_attn_w, d_ln1_g=d_ln1_g, d_ln1_b=d_ln1_b, d_ln2_g=d_ln2_g, d_ln2_b=d_ln2_b,
        d_gate_bias=d_bias, d_a_log=d_alog, d_d_skip=d_dskip)


W_IN_SEGS = [('z', W_Z, D), ('xs', W_XS, D), ('bc', W_BC, 512), ('dt', W_DT, NH), ('q', W_Q, D), ('k', W_K, D),
             ('v', W_V, D), ('f', W_F, NH)]
SHARD_W = IN_COLS // N_DEV


def _pack_w_in_t(w_in_t):
    seg = {n: w_in_t[off:off + w] for n, off, w in W_IN_SEGS}
    return jnp.concatenate([seg['z'], seg['xs'], seg['q'], seg['k'], seg['v'], seg['bc'], seg['dt'], seg['f'],
                            jnp.zeros((128 - 2 * NH, D), w_in_t.dtype)], axis=0)


def _shard_w_in_grad_t(d_w_in_t):
    blocks = []
    for dev in range(N_DEV):
        lo, hi = dev * SHARD_W, (dev + 1) * SHARD_W
        pieces = [d_w_in_t[n][max(lo, off) - off:min(hi, off + w) - off] for n, off, w in W_IN_SEGS
                  if max(lo, off) < min(hi, off + w)]
        blocks.append(jnp.concatenate(pieces, axis=0))
    return jnp.stack(blocks, axis=0)


WEIGHTS = ['w_ada', 'b_ada', 'w_in', 'conv_w', 'conv_b', 'dt_bias', 'a_log', 'd_skip', 'ssm_norm_w', 'f_bias',
           'attn_norm_w', 'w_out', 'ln1_g', 'ln1_b', 'w_ff_in', 'w_ff_out', 'ln2_g', 'ln2_b']
BIG = ['w_in', 'w_out', 'w_ff_in', 'w_ff_out']
SMALL_LAYOUT = [('b_ada', 0, 6 * D), ('conv_b', 12288, 1536), ('ssm_norm_w', 13824, D), ('attn_norm_w', 14848, D),
                ('ln1_g', 15872, D), ('ln1_b', 16896, D), ('ln2_g', 17920, D), ('ln2_b', 18944, D),
                ('dt_bias', 19968, NH), ('f_bias', 19968 + NH, NH), ('a_log', 20096, NH), ('d_skip', 20224, NH)]
SMALL_LOSS_LANE = 20352


def _pad_lanes(v, n=128):
    return jnp.pad(v, ((0, 0), (0, n - v.shape[1])))


def kernel(x, c, w_ada, b_ada, w_in, conv_w, conv_b, dt_bias, a_log, d_skip, ssm_norm_w, f_bias, attn_norm_w, w_out, ln1_g, ln1_b, w_ff_in, w_ff_out, ln2_g, ln2_b, loss_target, m_w_ada, m_b_ada, m_w_in, m_conv_w, m_conv_b, m_dt_bias, m_a_log, m_d_skip, m_ssm_norm_w, m_f_bias, m_attn_norm_w, m_w_out, m_ln1_g, m_ln1_b, m_w_ff_in, m_w_ff_out, m_ln2_g, m_ln2_b, v_w_ada, v_b_ada, v_w_in, v_conv_w, v_conv_b, v_dt_bias, v_a_log, v_d_skip, v_ssm_norm_w, v_f_bias, v_attn_norm_w, v_w_out, v_ln1_g, v_ln1_b, v_w_ff_in, v_w_ff_out, v_ln2_g, v_ln2_b):
    args = dict(locals())
    w = {n: args[n] for n in WEIGHTS}
    m = {n: args['m_' + n] for n in WEIGHTS}
    v = {n: args['v_' + n] for n in WEIGHTS}
    me = 4 * lax.axis_index("x") + 2 * lax.axis_index("y") + lax.axis_index("c")
    ada_cols = 6 * D // N_DEV
    conv_cols = conv_w.shape[2]

    c_all, conv_all = _exchange("gather_cond", [c, conv_w[0]], False)
    c_all = c_all.reshape(N_DEV, D)
    conv_w_full = conv_all.transpose(1, 0, 2).reshape(4, N_DEV * conv_cols)
    b_shard = lax.dynamic_slice(b_ada, (0, me * ada_cols), (1, ada_cols))
    mod_all, = _exchange("gather_mod", [_ada_mod(c_all, w_ada[0], b_shard)], False)
    mod = lax.dynamic_index_in_dim(mod_all, me, axis=1, keepdims=False).reshape(1, 6 * D)

    w_in_t = _after(jnp.swapaxes(w_in[0], 0, 1).astype(BF16), mod * 0)
    win_s = _gather_two_level("gather_w_in", w_in_t)
    first_done = win_s[0, 0:1, 0:1] * 0
    rest = _exchange_async("gather_rest", [_after(w[n][0].astype(BF16), first_done) for n in BIG[1:]], False, 1)

    def late_weights():
        wout_s, w1s, w2_s = rest()
        return wout_s.reshape(2 * D, D), w1s, w2_s.reshape(DFF, D)

    sends = {}

    def send_grads(tag, blocks):
        sends[tag] = _exchange_async("scatter_" + tag, blocks, True, {'late': 2, 'in': 3}[tag])
        return sum(b.reshape(-1)[0].astype(F32) * 0 for b in blocks)

    out = _local_step(x[0], loss_target[0], mod, _pack_w_in_t(win_s.reshape(IN_COLS, D)), late_weights, send_grads,
                      conv_w_full, conv_b, dt_bias, a_log, d_skip, ssm_norm_w, f_bias, attn_norm_w, ln1_g, ln1_b, ln2_g, ln2_b)

    small = jnp.concatenate(
        [out['d_mod'], out['d_conv_w'].reshape(1, -1), out['d_conv_b'], out['d_ssm_norm_w'], out['d_attn_norm_w'],
         out['d_ln1_g'], out['d_ln1_b'], out['d_ln2_g'], out['d_ln2_b'], out['d_gate_bias'], out['d_a_log'],
         out['d_d_skip'], _pad_lanes(out['loss'].reshape(1, 1))], axis=1)
    small_landed = _exchange_async("gather_small", [small], False, 4)
    (g_ff_in, g_ff_out, g_out), (g_in,) = sends['late'](), sends['in']()
    g_parts = dict(w_ff_in=g_ff_in, w_ff_out=g_ff_out, w_out=g_out, w_in=g_in)
    big = {n: _adamw("adamw_" + n, w[n][0], g_parts[n], m[n][0], v[n][0], tr=256, slots=True) for n in BIG[1:]}
    t = lambda a: jnp.swapaxes(a[0], 0, 1)
    big['w_in'] = [jnp.swapaxes(r, 0, 1) for r in _adamw("adamw_w_in", t(w_in), g_parts['w_in'], t(m_w_in), t(v_w_in),
                                                         tr=256, slots=True, by_columns=True)]
    big_done = sum(big[n][1][0:1, 0:1] * 0 for n in BIG)
    small_all = _after(small_landed()[0], big_done)
    ssum, small_res = _small_update(small_all, SMALL_LAYOUT, w, m, v)
    dmod_all = small_all[:, 0, :6 * D]
    g_w_ada = _ada_grad(c_all, lax.dynamic_slice(dmod_all, (0, me * ada_cols), (N_DEV, ada_cols)))
    ada = _adamw("adamw_ada", w_ada[0], g_w_ada, m_w_ada[0], v_w_ada[0], tr=256, slots=False)
    g_conv_w = lax.dynamic_slice(ssum[:, 6 * D:6 * D + 4 * N_DEV * conv_cols].reshape(4, N_DEV * conv_cols),
                                 (0, me * conv_cols), (4, conv_cols))
    conv = _adamw("adamw_conv_w", conv_w[0], g_conv_w, m_conv_w[0], v_conv_w[0], tr=4, slots=False)

    results = []
    for k in range(4):
        vals = {n: small_res[n][k] for n in small_res}
        vals['w_ada'], vals['conv_w'] = ada[k][None], conv[k][None]
        for n in BIG:
            vals[n] = big[n][k][None]
        results.append(vals)
    return (ssum[0, SMALL_LOSS_LANE], out['grad_x'][None], *[res[n] for res in results for n in WEIGHTS])
```

```python
import functools

import jax
import jax.numpy as jnp
from jax import lax
from jax.experimental import pallas as pl
from jax.experimental.pallas import tpu as pltpu
from jax.experimental.pallas import tpu_sc as plsc

F32, BF16 = jnp.float32, jnp.bfloat16

N_DEV = 8
D = 1024
NH, HD = 16, 64
NSTATE = 128
CHUNK = 128
HG = 8
DFF = 4096
ALPHA = 2.0 ** 0.25
EPS = 1e-5
ATT_SCALE = HD ** -0.5

OFF_Z, OFF_XS, OFF_Q, OFF_K, OFF_V, OFF_BC, OFF_DTF = 0, 1024, 2048, 3072, 4096, 5120, 5632
PCOLS = 5760
W_Z, W_XS, W_BC, W_DT, W_Q, W_K, W_V, W_F = 0, 1024, 2048, 2560, 2576, 3600, 4624, 5648
IN_COLS = 5664

ADAM_LR, ADAM_B1, ADAM_B2, ADAM_EPS, ADAM_WD, ADAM_STEP = 0.001, 0.9, 0.999, 1e-08, 0.01, 10

VMEM_LIMIT = 56 << 20

NN = (((1,), (0,)), ((), ()))
NT = (((1,), (1,)), ((), ()))
TN = (((0,), (0,)), ((), ()))


def _dot(a, b, dims=NN):
    return lax.dot_general(a, b, dims, preferred_element_type=F32)


def _bdot(a, b, dims=NN):
    return _dot(a.astype(BF16), b.astype(BF16), dims)


def _split3(v, terms=3):
    parts, rest = [], v
    for _ in range(terms):
        p = rest.astype(BF16)
        parts.append(p)
        rest = rest - p.astype(F32)
    return parts


def _sel_left(m01, v):
    return sum(_dot(m01, p) for p in _split3(v))


def _sel_right(v, m01, dims=NN, terms=3):
    return sum(_dot(p, m01, dims) for p in _split3(v, terms))


def _iota(shape, dim):
    return lax.broadcasted_iota(jnp.int32, shape, dim)


def _tri_lower(n):
    return (_iota((n, n), 1) <= _iota((n, n), 0)).astype(BF16)


def _tri_upper(n):
    return (_iota((n, n), 1) >= _iota((n, n), 0)).astype(BF16)


def _head_expand():
    return (lax.shift_right_logical(_iota((128, D), 1), 6) == _iota((128, D), 0)).astype(BF16)


def _head_reduce():
    return (lax.shift_right_logical(_iota((D, 128), 0), 6) == _iota((D, 128), 1)).astype(BF16)


def _sigmoid(x):
    return 1.0 / (1.0 + jnp.exp(-x))


def _silu(x):
    return x * _sigmoid(x)


def _dsilu(x):
    s = _sigmoid(x)
    return s * (1.0 + x * (1.0 - s))


def _softplus(x):
    return jnp.maximum(x, 0.0) + jnp.log(1.0 + jnp.exp(-jnp.abs(x)))


def _log_sigmoid(x):
    return jnp.minimum(x, 0.0) - jnp.log(1.0 + jnp.exp(-jnp.abs(x)))


def _params(sem):
    return pltpu.CompilerParams(dimension_semantics=sem, vmem_limit_bytes=VMEM_LIMIT)


def _mm_nn(name, a, b, *, tm, tn, tk, out_dtype, pro=None, aux=(), epi=None):
    m, k_all = a.shape
    b_sharded = b.ndim == 3
    n = b.shape[0] * b.shape[2] if b_sharded else b.shape[1]
    assert not b_sharded or tn == b.shape[2]
    nk = k_all // tk
    n_aux = len(aux)
    b_spec = (pl.BlockSpec((None, tk, tn), lambda i, j, k: (j, k, 0)) if b_sharded
              else pl.BlockSpec((tk, tn), lambda i, j, k: (k, j)))

    def body(a_ref, b_ref, *rest):
        aux_refs, o_ref = rest[:n_aux], rest[n_aux]
        at = a_ref[...]
        if pro is not None:
            at = pro(at, *[r[...] for r in aux_refs])
        part = _bdot(at, b_ref[...])
        if nk == 1:
            o_ref[...] = (part if epi is None else epi(part)).astype(out_dtype)
            return
        assert epi is None
        acc_ref = rest[n_aux + 1]
        kk = pl.program_id(2)

        @pl.when(kk == 0)
        def _():
            acc_ref[...] = part

        @pl.when(kk > 0)
        def _():
            acc_ref[...] += part

        @pl.when(kk == nk - 1)
        def _():
            o_ref[...] = acc_ref[...].astype(out_dtype)

    return pl.pallas_call(
        body, name=name,
        grid=(m // tm, n // tn, nk),
        in_specs=[pl.BlockSpec((tm, tk), lambda i, j, k: (i, k)), b_spec]
        + [pl.BlockSpec((1, tk), lambda i, j, k: (0, k)) for _ in aux],
        out_specs=pl.BlockSpec((tm, tn), lambda i, j, k: (i, j)),
        out_shape=jax.ShapeDtypeStruct((m, n), out_dtype),
        scratch_shapes=[] if nk == 1 else [pltpu.VMEM((tm, tn), F32)],
        compiler_params=_params(("parallel", "parallel", "arbitrary")),
    )(a, b, *aux)


def _mm_nt(name, a_list, b_list, *, n, tm, tn, out_dtype, epi=None, epi_aux=(), b_rows=False):
    m = a_list[0][0].shape[0]
    n_op = len(a_list)
    n_epi = len(epi_aux)
    dims = NN if b_rows else NT

    def body(*refs):
        a_refs, b_refs = refs[:n_op], refs[n_op:2 * n_op]
        e_refs, o_ref = refs[2 * n_op:2 * n_op + n_epi], refs[2 * n_op + n_epi]
        acc = None
        for a_ref, b_ref in zip(a_refs, b_refs):
            part = _bdot(a_ref[...], b_ref[...], dims)
            acc = part if acc is None else acc + part
        if epi is not None:
            acc = epi(acc, *[r[...] for r in e_refs])
        o_ref[...] = acc.astype(out_dtype)

    in_specs = [pl.BlockSpec((tm, w), functools.partial(lambda i, j, cb: (i, cb), cb=cb)) for (_, w, cb) in a_list]
    for (b, w, cb) in b_list:
        if b_rows:
            in_specs.append(pl.BlockSpec((w, tn), functools.partial(lambda i, j, cb: (cb, j), cb=cb)))
        elif b.ndim == 3:
            in_specs.append(pl.BlockSpec((None, tn, w), functools.partial(lambda i, j, cb: (cb, j, 0), cb=cb)))
        else:
            in_specs.append(pl.BlockSpec((tn, w), functools.partial(lambda i, j, cb: (j, cb), cb=cb)))
    in_specs += [pl.BlockSpec((tm, tn), lambda i, j: (i, j)) for _ in epi_aux]
    return pl.pallas_call(
        body, name=name,
        grid=(m // tm, n // tn),
        in_specs=in_specs,
        out_specs=pl.BlockSpec((tm, tn), lambda i, j: (i, j)),
        out_shape=jax.ShapeDtypeStruct((m, n), out_dtype),
        compiler_params=_params(("parallel", "parallel")),
    )(*[a for (a, _, _) in a_list], *[b for (b, _, _) in b_list], *epi_aux)


def _mm_tn(name, a, b, *, tm, tn, ts, pro=None, aux=(), col_shards=False):
    s_all, ka = a.shape
    nb = b.shape[1]
    n_aux = len(aux)
    ns = s_all // ts
    shard_w = nb // N_DEV
    per_tile = tn // shard_w
    assert not col_shards or tn == per_tile * shard_w

    def body(a_ref, b_ref, *rest):
        aux_refs, o_ref, acc_ref = rest[:n_aux], rest[n_aux], rest[n_aux + 1]
        at = a_ref[...]
        if pro is not None:
            at = pro(at, *[r[...] for r in aux_refs])
        part = _bdot(at, b_ref[...], TN)
        ss = pl.program_id(2)

        @pl.when(ss == 0)
        def _():
            acc_ref[...] = part

        @pl.when(ss > 0)
        def _():
            acc_ref[...] += part

        @pl.when(ss == ns - 1)
        def _():
            if col_shards:
                for q in range(per_tile):
                    o_ref[q] = acc_ref[:, q * shard_w:(q + 1) * shard_w].astype(BF16)
            else:
                o_ref[...] = acc_ref[...].astype(BF16)

    if col_shards:
        out_spec = pl.BlockSpec((per_tile, tm, shard_w), lambda i, j, s: (j, i, 0))
        out_shape = jax.ShapeDtypeStruct((N_DEV, ka, shard_w), BF16)
    else:
        out_spec = pl.BlockSpec((tm, tn), lambda i, j, s: (i, j))
        out_shape = jax.ShapeDtypeStruct((ka, nb), BF16)
    return pl.pallas_call(
        body, name=name,
        grid=(ka // tm, nb // tn, ns),
        in_specs=[pl.BlockSpec((ts, tm), lambda i, j, s: (s, i)),
                  pl.BlockSpec((ts, tn), lambda i, j, s: (s, j))]
        + [pl.BlockSpec((1, tm), lambda i, j, s: (0, i)) for _ in aux],
        out_specs=out_spec, out_shape=out_shape,
        scratch_shapes=[pltpu.VMEM((tm, tn), F32)],
        compiler_params=_params(("parallel", "parallel", "arbitrary")),
    )(a, b, *aux)


def _rowk(name, fn, n_rows, tr, rows, fulls, outs, accs, reverse=False):
    n = n_rows // tr
    n_row, n_full, n_out, n_acc = len(rows), len(fulls), len(outs), len(accs)

    def pos(i):
        return (n - 1 - i) if reverse else i

    def body(*refs):
        row_refs = refs[:n_row]
        full_refs = refs[n_row:n_row + n_full]
        out_refs = refs[n_row + n_full:n_row + n_full + n_out]
        acc_refs = refs[n_row + n_full + n_out:]
        i = pl.program_id(0)

        @pl.when(i == 0)
        def _():
            for r in acc_refs:
                r[...] = jnp.zeros(r.shape, r.dtype)

        res = fn(pos(i), *[r[...] for r in row_refs], *[r[...] for r in full_refs], *[r[...] for r in acc_refs])
        for r, v in zip(out_refs + acc_refs, res):
            r[...] = v.astype(r.dtype)

    def row_map(i, cb, shift):
        return (jnp.clip(pos(i) + shift, 0, n - 1), cb)

    def halo_map(i, cb, shift):
        tile = jnp.clip(pos(i) + shift, 0, n - 1)
        return (tile * (tr // 8) + (tr // 8 - 1 if shift < 0 else 0), cb)

    in_specs = [pl.BlockSpec((tr, w), functools.partial(row_map, cb=cb, shift=sh)) if sh == 0 else
                pl.BlockSpec((8, w), functools.partial(halo_map, cb=cb, shift=sh)) for (_, w, cb, sh) in rows]
    in_specs += [pl.BlockSpec(f.shape, functools.partial(lambda i, nd: (0,) * nd, nd=f.ndim)) for f in fulls]
    out_specs = [pl.BlockSpec((tr, w), lambda i: (pos(i), 0)) for (w, _) in outs]
    out_specs += [pl.BlockSpec((r, w), lambda i: (0, 0)) for (r, w) in accs]
    out_shape = [jax.ShapeDtypeStruct((n_rows, w), dt) for (w, dt) in outs]
    out_shape += [jax.ShapeDtypeStruct((r, w), F32) for (r, w) in accs]
    return pl.pallas_call(
        body, name=name, grid=(n,), in_specs=in_specs, out_specs=out_specs, out_shape=out_shape,
        compiler_params=_params(("arbitrary",)),
    )(*[a for (a, _, _, _) in rows], *fulls)


def _colsum(x):
    return jnp.sum(x, axis=0, keepdims=True)


def _mean(x):
    return jnp.mean(x, axis=-1, keepdims=True)


def _modulate(x, sc, sh):
    return x * (1.0 + sc) + sh


def _shift_down(cur, prev8, j):
    tr = cur.shape[0]
    row8 = _iota(prev8.shape, 0)
    head = jnp.where(row8 < j, pltpu.roll(prev8, j, 0), pltpu.roll(cur[0:8], j, 0))
    return head if tr == 8 else jnp.concatenate([head, pltpu.roll(cur, j, 0)[8:]], axis=0)


def _shift_up(cur, next8, j):
    tr = cur.shape[0]
    row8 = _iota(next8.shape, 0)
    tail = jnp.where(row8 < 8 - j, pltpu.roll(cur[tr - 8:], 8 - j, 0), pltpu.roll(next8, 8 - j, 0))
    return jnp.concatenate([pltpu.roll(cur, tr - j, 0)[:tr - 8], tail], axis=0)


def _conv(cur, prev, w, b):
    out = cur * w[3:4] + b
    for j in (1, 2, 3):
        out = out + _shift_down(cur, prev, j) * w[3 - j:4 - j]
    return out


def _conv_fwd(p, w_xs, b_xs, w_bc, b_bc, s):
    def fn(pos, xs, xs_prev, bc, bc_prev, w_xs, b_xs, w_bc, b_bc):
        first = pos == 0
        xs_prev = jnp.where(first, 0.0, xs_prev)
        bc_prev = jnp.where(first, 0.0, bc_prev)
        return _silu(_conv(xs, xs_prev, w_xs, b_xs)), _silu(_conv(bc, bc_prev, w_bc, b_bc))

    return _rowk("conv_fwd", fn, s, 256,
                 [(p, D, OFF_XS // D, 0), (p, D, OFF_XS // D, -1), (p, 512, OFF_BC // 512, 0), (p, 512, OFF_BC // 512, -1)],
                 [w_xs, b_xs, w_bc, b_bc], [(D, F32), (512, F32)], [])


def _conv_bwd(dxs_a, dbc_a, p, w_xs, b_xs, w_bc, b_bc, s):
    tr = 256
    n = s // tr

    def fn(pos, da1, da1n, x1, x1p, x1n, da2, da2n, x2, x2p, x2n, w1, b1, w2, b2, aw1, ab1, aw2, ab2):
        dx1, dw1, db1 = _conv_bwd_fn(pos, n, da1, da1n, x1, x1p, x1n, w1, b1)
        dx2, dw2, db2 = _conv_bwd_fn(pos, n, da2, da2n, x2, x2p, x2n, w2, b2)
        return dx1, dx2, aw1 + dw1, ab1 + db1, aw2 + dw2, ab2 + db2

    cx, cb = OFF_XS // D, OFF_BC // 512
    return _rowk("conv_bwd", fn, s, tr,
                 [(dxs_a, D, 0, 0), (dxs_a, D, 0, 1), (p, D, cx, 0), (p, D, cx, -1), (p, D, cx, 1),
                  (dbc_a, 512, 0, 0), (dbc_a, 512, 0, 1), (p, 512, cb, 0), (p, 512, cb, -1), (p, 512, cb, 1)],
                 [w_xs, b_xs, w_bc, b_bc], [(D, BF16), (512, BF16)], [(8, D), (1, D), (8, 512), (1, 512)])


def _conv_bwd_fn(pos, n, da, da_next, x, x_prev, x_next, w, b):
    first, last = pos == 0, pos == n - 1
    x_prev = jnp.where(first, 0.0, x_prev)
    shifted = {j: _shift_down(x, x_prev, j) for j in (1, 2, 3)}
    conv = x * w[3:4] + b
    for j in (1, 2, 3):
        conv = conv + shifted[j] * w[3 - j:4 - j]
    dc = da * _dsilu(conv)
    dc_next = jnp.where(last, 0.0, da_next * _dsilu(_conv(x_next, x[x.shape[0] - 8:], w, b)))
    dx = dc * w[3:4]
    dws = [None] * 4
    dws[3] = _colsum(dc * x)
    for j in (1, 2, 3):
        dx = dx + _shift_up(dc, dc_next, j) * w[3 - j:4 - j]
        dws[3 - j] = _colsum(dc * shifted[j])
    row = _iota((8, x.shape[1]), 0)
    dw = jnp.zeros((8, x.shape[1]), F32)
    for k in range(4):
        dw = jnp.where(row == k, dws[k], dw)
    return dx, dw, _colsum(dc)


def _ssd_gates(dtf, bias, a_log):
    lane = _iota(dtf.shape, 1)
    head = lane < NH
    dt = jnp.where(head, _softplus(dtf + bias), 0.0)
    a_neg = jnp.where(_iota(a_log.shape, 1) < NH, -jnp.exp(a_log), 0.0)
    a = dt * a_neg
    cs = _sel_left(_tri_lower(CHUNK), a)
    return dt, a_neg, cs


def _decay_mask(cs_ref, cst_ref, h):
    diff = cs_ref[:, h:h + 1] - cst_ref[h:h + 1, :]
    low = _iota((CHUNK, CHUNK), 1) <= _iota((CHUNK, CHUNK), 0)
    return jnp.where(low, jnp.exp(jnp.minimum(diff, 0.0)), 0.0)


def _ssd_fwd(xs_a, bc_a, p, bias128, alog128, dskip_x, s):
    nc = s // CHUNK
    t = CHUNK

    def body(xs_ref, bc_ref, dtf_ref, bias_ref, alog_ref, dsk_ref, y_ref, st_ref,
             state, x_sc, xw_sc, cs_sc, cst_sc, yd_sc):
        c = pl.program_id(0)

        @pl.when(c == 0)
        def _():
            state[...] = jnp.zeros(state.shape, F32)

        dt, _, cs = _ssd_gates(dtf_ref[...], bias_ref[...], alog_ref[...])
        cs_sc[...] = cs
        cst_sc[...] = cs.T
        cs_last = cs[t - 1:t, :]
        expand = _head_expand()
        ex = _sel_right(jnp.concatenate([dt, jnp.exp(cs), jnp.exp(cs_last - cs)], axis=0), expand, terms=2)
        dt_x, eo_x, we_x = ex[0:t], ex[t:2 * t], ex[2 * t:3 * t]
        g_x = _sel_right(jnp.broadcast_to(jnp.exp(cs_last), (8, 128)), expand)[0:1]
        xs = xs_ref[...]
        x = xs * dt_x
        x_sc[...] = x.astype(BF16)
        xw_sc[...] = (x * we_x).astype(BF16)
        prev = state[...]
        st_ref[0] = prev
        prev_b = prev.astype(BF16)
        for g in range(2):
            cols = slice(g * 512, (g + 1) * 512)
            b_g = bc_ref[:, g * 128:(g + 1) * 128].astype(BF16)
            c_g = bc_ref[:, 256 + g * 128:256 + (g + 1) * 128].astype(BF16)
            gmat = _dot(c_g, b_g, NT)
            y_off = _dot(c_g, prev_b[:, cols]) * eo_x[:, cols]
            s_loc = _dot(b_g, xw_sc[:, cols], TN)
            state[:, cols] = g_x[:, cols] * prev[:, cols] + s_loc
            for e in range(HG):
                h = g * HG + e
                m = gmat * _decay_mask(cs_sc, cst_sc, h)
                yd_sc[:, h * HD:(h + 1) * HD] = _dot(m.astype(BF16), x_sc[:, h * HD:(h + 1) * HD])
            y_ref[:, cols] = yd_sc[:, cols] + y_off + dsk_ref[:, cols] * xs[:, cols]

    return pl.pallas_call(
        body, name="ssd_fwd", grid=(nc,),
        in_specs=[pl.BlockSpec((t, D), lambda c: (c, 0)),
                  pl.BlockSpec((t, 512), lambda c: (c, 0)),
                  pl.BlockSpec((t, 128), lambda c: (c, OFF_DTF // 128)),
                  pl.BlockSpec((1, 128), lambda c: (0, 0)),
                  pl.BlockSpec((1, 128), lambda c: (0, 0)),
                  pl.BlockSpec((1, D), lambda c: (0, 0))],
        out_specs=[pl.BlockSpec((t, D), lambda c: (c, 0)),
                   pl.BlockSpec((1, NSTATE, D), lambda c: (c, 0, 0))],
        out_shape=[jax.ShapeDtypeStruct((s, D), F32), jax.ShapeDtypeStruct((nc, NSTATE, D), F32)],
        scratch_shapes=[pltpu.VMEM((NSTATE, D), F32), pltpu.VMEM((t, D), BF16), pltpu.VMEM((t, D), BF16),
                        pltpu.VMEM((t, 128), F32), pltpu.VMEM((128, t), F32), pltpu.VMEM((t, D), F32)],
        compiler_params=_params(("arbitrary",)),
    )(xs_a, bc_a, p, bias128, alog128, dskip_x)


def _ssd_bwd(dy, xs_a, bc_a, p, states, bias128, alog128, dskip_x, s):
    nc = s // CHUNK
    t = CHUNK

    def body(dy_ref, xs_ref, bc_ref, dtf_ref, st_ref, bias_ref, alog_ref, dsk_ref,
             dxs_ref, dbc_ref, ddt_ref, dalog_ref, dskip_ref,
             dstate, x_sc, dy_sc, dx_sc, deo_sc, dwe_sc, cs_sc, cst_sc, dcol_sc, drow_sc):
        i = pl.program_id(0)

        @pl.when(i == 0)
        def _():
            dstate[...] = jnp.zeros(dstate.shape, F32)
            dalog_ref[...] = jnp.zeros(dalog_ref.shape, F32)
            dskip_ref[...] = jnp.zeros(dskip_ref.shape, F32)

        dtf = dtf_ref[...]
        dt, a_neg, cs = _ssd_gates(dtf, bias_ref[...], alog_ref[...])
        cs_sc[...] = cs
        cst_sc[...] = cs.T
        cs_last = cs[t - 1:t, :]
        eo, we, g_end = jnp.exp(cs), jnp.exp(cs_last - cs), jnp.exp(cs_last)
        expand, reduce = _head_expand(), _head_reduce()
        ex = _sel_right(jnp.concatenate([dt, eo, we], axis=0), expand, terms=2)
        dt_x, eo_x, we_x = ex[0:t], ex[t:2 * t], ex[2 * t:3 * t]
        g_x = _sel_right(jnp.broadcast_to(g_end, (8, 128)), expand)[0:1]
        xs = xs_ref[...]
        dyv = dy_ref[...]
        x = xs * dt_x
        x_sc[...] = x.astype(BF16)
        dy_sc[...] = dyv.astype(BF16)
        dyo_b = (dyv * eo_x).astype(BF16)
        xw_b = (x * we_x).astype(BF16)
        prev = st_ref[0]
        prev_b = prev.astype(BF16)
        dnext = dstate[...]
        dnext_b = dnext.astype(BF16)
        dcol_sc[...] = jnp.zeros(dcol_sc.shape, F32)
        drow_sc[...] = jnp.zeros(drow_sc.shape, F32)
        lane_row = _iota((1, 128), 1)
        sub_col = _iota((128, 1), 0)
        for g in range(2):
            cols = slice(g * 512, (g + 1) * 512)
            b_g = bc_ref[:, g * 128:(g + 1) * 128].astype(BF16)
            c_g = bc_ref[:, 256 + g * 128:256 + (g + 1) * 128].astype(BF16)
            gmat = _dot(c_g, b_g, NT)
            b_ds = _dot(b_g, dnext_b[:, cols])
            c_s = _dot(c_g, prev_b[:, cols])
            dx_sc[:, cols] = b_ds * we_x[:, cols]
            deo_sc[:, cols] = dyv[:, cols] * c_s
            dwe_sc[:, cols] = b_ds * x[:, cols]
            db = _dot(xw_b[:, cols], dnext_b[:, cols], NT)
            dc = _dot(dyo_b[:, cols], prev_b[:, cols], NT)
            dstate[:, cols] = g_x[:, cols] * dnext[:, cols] + _dot(c_g, dyo_b[:, cols], TN)
            dg = jnp.zeros((t, t), F32)
            for e in range(HG):
                h = g * HG + e
                hc = slice(h * HD, (h + 1) * HD)
                lmat = _decay_mask(cs_sc, cst_sc, h)
                m = gmat * lmat
                dx_sc[:, hc] += _dot(m.astype(BF16), dy_sc[:, hc], TN)
                dm = _dot(dy_sc[:, hc], x_sc[:, hc], NT)
                dg = dg + dm * lmat
                qm = dm * m
                dcol_sc[...] += jnp.sum(qm, axis=1, keepdims=True) * (lane_row == h).astype(F32)
                drow_sc[...] += (sub_col == h).astype(F32) * jnp.sum(qm, axis=0, keepdims=True)
            dg_b = dg.astype(BF16)
            dbc_ref[:, g * 128:(g + 1) * 128] = db + _dot(dg_b, c_g, TN)
            dbc_ref[:, 256 + g * 128:256 + (g + 1) * 128] = dc + _dot(dg_b, b_g)
        d_eo = _sel_right(deo_sc[...], reduce, terms=2)
        d_we = _sel_right(dwe_sc[...], reduce, terms=2)
        d_gend = _sel_right(jnp.broadcast_to(_colsum(dnext * prev), (8, D)), reduce)[0:1]
        d_cs = dcol_sc[...] - drow_sc[...].T + d_eo * eo - d_we * we
        extra = _colsum(d_we * we) + d_gend * g_end
        d_cs = d_cs + jnp.where(_iota((t, 128), 0) == t - 1, extra, 0.0)
        da = _sel_left(_tri_upper(t), d_cs)
        dx = dx_sc[...]
        ddt = _sel_right(dx * xs, reduce, terms=2) + da * a_neg
        dxs_ref[...] = dx * dt_x + dsk_ref[...] * dyv
        ddt_ref[...] = jnp.where(_iota((t, 128), 1) < NH, ddt * _sigmoid(dtf + bias_ref[...]), 0.0)
        dalog_ref[...] += _colsum(da * dt) * a_neg
        dskip_ref[...] += _sel_right(jnp.broadcast_to(_colsum(dyv * xs), (8, D)), reduce)[0:1]

    rev = lambda i: nc - 1 - i
    return pl.pallas_call(
        body, name="ssd_bwd", grid=(nc,),
        in_specs=[pl.BlockSpec((t, D), lambda i: (rev(i), 0)),
                  pl.BlockSpec((t, D), lambda i: (rev(i), 0)),
                  pl.BlockSpec((t, 512), lambda i: (rev(i), 0)),
                  pl.BlockSpec((t, 128), lambda i: (rev(i), OFF_DTF // 128)),
                  pl.BlockSpec((1, NSTATE, D), lambda i: (rev(i), 0, 0)),
                  pl.BlockSpec((1, 128), lambda i: (0, 0)),
                  pl.BlockSpec((1, 128), lambda i: (0, 0)),
                  pl.BlockSpec((1, D), lambda i: (0, 0))],
        out_specs=[pl.BlockSpec((t, D), lambda i: (rev(i), 0)),
                   pl.BlockSpec((t, 512), lambda i: (rev(i), 0)),
                   pl.BlockSpec((t, 128), lambda i: (rev(i), 0)),
                   pl.BlockSpec((1, 128), lambda i: (0, 0)),
                   pl.BlockSpec((1, 128), lambda i: (0, 0))],
        out_shape=[jax.ShapeDtypeStruct((s, D), F32), jax.ShapeDtypeStruct((s, 512), F32),
                   jax.ShapeDtypeStruct((s, 128), F32), jax.ShapeDtypeStruct((1, 128), F32),
                   jax.ShapeDtypeStruct((1, 128), F32)],
        scratch_shapes=[pltpu.VMEM((NSTATE, D), F32), pltpu.VMEM((t, D), BF16), pltpu.VMEM((t, D), BF16),
                        pltpu.VMEM((t, D), F32), pltpu.VMEM((t, D), F32), pltpu.VMEM((t, D), F32),
                        pltpu.VMEM((t, 128), F32), pltpu.VMEM((128, t), F32),
                        pltpu.VMEM((t, 128), F32), pltpu.VMEM((128, t), F32)],
        compiler_params=_params(("arbitrary",)),
    )(dy, xs_a, bc_a, p, states, bias128, alog128, dskip_x)


def _gate_lanes(shape):
    lane = _iota(shape, 1)
    return (lane >= NH) & (lane < 2 * NH)


def _cum_fwd(p, bias128, s):
    tr = min(512, s)

    def body(dtf_ref, bias_ref, o_ref, carry):
        @pl.when(pl.program_id(0) == 0)
        def _():
            carry[...] = jnp.zeros(carry.shape, F32)

        lf = jnp.where(_gate_lanes((tr, 128)), _log_sigmoid(dtf_ref[...] + bias_ref[...]), 0.0)
        cum = _sel_left(_tri_lower(tr), lf) + carry[...]
        carry[...] = cum[tr - 1:tr, :]
        o_ref[...] = cum

    return pl.pallas_call(
        body, name="cum_fwd", grid=(s // tr,),
        in_specs=[pl.BlockSpec((tr, 128), lambda i: (i, OFF_DTF // 128)), pl.BlockSpec((1, 128), lambda i: (0, 0))],
        out_specs=pl.BlockSpec((tr, 128), lambda i: (i, 0)),
        out_shape=jax.ShapeDtypeStruct((s, 128), F32),
        scratch_shapes=[pltpu.VMEM((1, 128), F32)],
        compiler_params=_params(("arbitrary",)),
    )(p, bias128)


def _cum_bwd(dcum, ddt_raw, p, bias128, s):
    tr = min(512, s)

    def fn(pos, dcum, ddt, dtf, bias, carry, acc):
        suffix = _sel_left(_tri_upper(tr), dcum) + carry
        dfr = jnp.where(_gate_lanes((tr, 128)), suffix * _sigmoid(-(dtf + bias)), 0.0)
        out = ddt + dfr
        return out, suffix[0:1, :], acc + _colsum(out)

    return _rowk("cum_bwd", fn, s, tr, [(dcum, 128, 0, 0), (ddt_raw, 128, 0, 0), (p, 128, OFF_DTF // 128, 0)],
                 [bias128], [(128, BF16)], [(1, 128), (1, 128)], reverse=True)


ATT_BLOCK = 512
ATT_STRIP = 32


def _head_part(shape, h, dim):
    i = _iota(shape, dim)
    return (i >= h * HD) & (i < (h + 1) * HD)


def _k_augmented(k_blk, cum_blk, j, h):
    tk = k_blk.shape[0]
    lane = _iota((tk, 128), 1)
    col = jnp.sum(jnp.where(lane == NH + 2 * j + h, cum_blk, 0.0), axis=1, keepdims=True)
    c0, c1, c2 = [c.astype(F32) for c in _split3(-col)]
    k_h = k_blk if h == 0 else pltpu.roll(k_blk, HD, 1)
    aug = jnp.where(lane == HD, c0, jnp.where(lane == HD + 1, c1, jnp.where(lane == HD + 2, c2, 0.0)))
    return jnp.where(lane < HD, k_h, aug).astype(BF16)


def _q_augmented_t(q_blk):
    tq = q_blk.shape[0]
    q_t = (q_blk * ATT_SCALE).T.astype(BF16)
    ones = (_iota((HD, tq), 0) < 3).astype(BF16)
    return [jnp.concatenate([q_t[h * HD:(h + 1) * HD], ones], axis=0) for h in range(2)]


def _rows01(r0, r1):
    sub = _iota((8, r0.shape[1]), 0)
    return jnp.where(sub == 0, r0, jnp.where(sub == 1, r1, 0.0))


def _fold8(x, op, cur):
    for g in range(x.shape[0] // 8):
        cur = op(cur, x[8 * g:8 * (g + 1), :])
    return cur


def _attn_fwd(p, cum, s):
    t = min(ATT_BLOCK, s)
    nq = s // t
    r = ATT_STRIP

    def body(q_ref, k_ref, v_ref, c_ref, o_ref, lse_ref, kaug_sc, vt_sc, s0_sc, s1_sc, p0_sc, p1_sc, m_sc, l_sc, acc_sc):
        j, qi = pl.program_id(0), pl.program_id(1)
        s_sc, p_sc = (s0_sc, s1_sc), (p0_sc, p1_sc)

        @pl.when(qi == 0)
        def _():
            for c in range(nq):
                rows = slice(c * t, (c + 1) * t)
                k_blk, vt = k_ref[rows, :], v_ref[rows, :].T
                for h in range(2):
                    kaug_sc[h, rows, :] = _k_augmented(k_blk, c_ref[rows, :], j, h)
                    vt_sc[h, :, rows] = vt[h * HD:(h + 1) * HD].astype(BF16)

        qaug_t = _q_augmented_t(q_ref[...])
        m_sc[...] = jnp.full(m_sc.shape, -1e30, F32)
        l_sc[...] = jnp.zeros(l_sc.shape, F32)
        acc_sc[...] = jnp.zeros(acc_sc.shape, F32)
        top = _iota((128, t), 0) < HD

        def logits(kb, buf):
            kv = pl.ds(pl.multiple_of(kb * t, t), t)
            for h in range(2):
                s_sc[buf][h] = _dot(kaug_sc[h, kv, :], qaug_t[h])

        def softmax(buf, diagonal):
            alphas = []
            for h in range(2):
                cur = jnp.full((8, t), -1e30, F32)
                for i in range(t // r):
                    rows = slice(i * r, (i + 1) * r)
                    x = s_sc[buf][h, rows, :]
                    if diagonal:
                        x = jnp.where(_iota((r, t), 1) >= i * r + _iota((r, t), 0), x, -1e30)
                        s_sc[buf][h, rows, :] = x
                    cur = _fold8(x, jnp.maximum, cur)
                m_prev = m_sc[h, 0:1, :]
                m_new = jnp.maximum(m_prev, jnp.max(cur, axis=0, keepdims=True))
                alpha = jnp.exp(m_prev - m_new)
                m_sc[h, 0:1, :] = m_new
                alphas.append(alpha)
                tot = jnp.zeros((8, t), F32)
                for i in range(t // r):
                    rows = slice(i * r, (i + 1) * r)
                    pr = jnp.exp(s_sc[buf][h, rows, :] - m_new)
                    p_sc[buf][h, rows, :] = pr.astype(BF16)
                    tot = _fold8(pr, jnp.add, tot)
                l_sc[h, 0:1, :] = alpha * l_sc[h, 0:1, :] + jnp.sum(tot, axis=0, keepdims=True)
            return alphas

        def accumulate(kb, buf, alphas):
            kv = pl.ds(pl.multiple_of(kb * t, t), t)
            for h in range(2):
                part = slice(h * HD, (h + 1) * HD)
                acc_sc[part, :] = acc_sc[part, :] * alphas[h] + _dot(vt_sc[h, :, kv], p_sc[buf][h])

        def first_trip():
            logits(0, 1)
            accumulate(qi, 0, softmax(0, True))
            logits(jnp.minimum(1, qi - 1), 0)
            return tuple(softmax(1, False))

        def only_diagonal():
            accumulate(qi, 0, softmax(0, True))
            return (jnp.ones((1, t), F32),) * 2

        def steady(u, alphas_b):
            accumulate(2 * u - 2, 1, alphas_b)
            logits(2 * u, 1)
            accumulate(2 * u - 1, 0, softmax(0, False))
            logits(jnp.minimum(2 * u + 1, qi - 1), 0)
            return tuple(softmax(1, False))

        logits(qi, 0)
        n_blocks = qi + 1
        alphas_b = lax.cond(qi >= 1, first_trip, only_diagonal)
        alphas_b = lax.fori_loop(1, n_blocks // 2, steady, alphas_b)
        last_b = 2 * (n_blocks // 2) - 2

        @pl.when((qi >= 1) & (n_blocks % 2 == 0))
        def _():
            accumulate(last_b, 1, alphas_b)

        @pl.when((qi >= 2) & (n_blocks % 2 == 1))
        def _():
            accumulate(last_b, 1, alphas_b)
            accumulate(qi - 1, 0, softmax(0, False))

        l0, l1 = l_sc[0, 0:1, :], l_sc[1, 0:1, :]
        o_ref[...] = (acc_sc[...] / jnp.where(top, l0, l1)).T
        lse_ref[0] = _rows01(m_sc[0, 0:1, :] + jnp.log(l0), m_sc[1, 0:1, :] + jnp.log(l1))

    return pl.pallas_call(
        body, name="attn_fwd", grid=(NH // 2, nq),
        in_specs=[pl.BlockSpec((t, 128), lambda j, qi: (qi, OFF_Q // 128 + j)),
                  pl.BlockSpec((s, 128), lambda j, qi: (0, OFF_K // 128 + j)),
                  pl.BlockSpec((s, 128), lambda j, qi: (0, OFF_V // 128 + j)),
                  pl.BlockSpec((s, 128), lambda j, qi: (0, 0))],
        out_specs=[pl.BlockSpec((t, 128), lambda j, qi: (qi, j)),
                   pl.BlockSpec((1, 8, t), lambda j, qi: (j, 0, qi))],
        out_shape=[jax.ShapeDtypeStruct((s, D), F32), jax.ShapeDtypeStruct((NH // 2, 8, s), F32)],
        scratch_shapes=[pltpu.VMEM((2, s, 128), BF16), pltpu.VMEM((2, HD, s), BF16), pltpu.VMEM((2, t, t), F32),
                        pltpu.VMEM((2, t, t), F32), pltpu.VMEM((2, t, t), BF16), pltpu.VMEM((2, t, t), BF16),
                        pltpu.VMEM((2, 8, t), F32), pltpu.VMEM((2, 8, t), F32), pltpu.VMEM((128, t), F32)],
        compiler_params=_params(("parallel", "arbitrary")),
    )(p, p, p, cum)


def _attn_bwd(p, cum, o, lse, do, s):
    t = min(ATT_BLOCK, s)
    nq = s // t
    r = ATT_STRIP

    def body(q_ref, k_ref, v_ref, c_ref, o_ref, lse_ref, do_ref, dq_ref, dk_ref, dv_ref, dc_ref, dr_ref,
             qaugt_sc, qh_sc, dot_sc, doh_sc, delta_sc, dqt_sc, dr_sc, kaug_sc, vh_sc, kt_sc,
             s0_sc, s1_sc, dp0_sc, dp1_sc, p0_sc, p1_sc, ds0_sc, ds1_sc, dk_sc, dv_sc, dc_sc):
        j, ki = pl.program_id(0), pl.program_id(1)
        s_sc, dp_sc, p_sc, ds_sc = (s0_sc, s1_sc), (dp0_sc, dp1_sc), (p0_sc, p1_sc), (ds0_sc, ds1_sc)

        @pl.when(ki == 0)
        def _():
            for c in range(nq):
                rows = slice(c * t, (c + 1) * t)
                q_blk, do_blk = q_ref[rows, :], do_ref[rows, :]
                qaugt_sc[0, :, rows], qaugt_sc[1, :, rows] = _q_augmented_t(q_blk)
                dot_sc[:, rows] = do_blk.T.astype(BF16)
                prod_t = (do_blk * o_ref[rows, :]).T
                delta_sc[:, rows] = _rows01(jnp.sum(prod_t[0:HD], axis=0, keepdims=True),
                                            jnp.sum(prod_t[HD:], axis=0, keepdims=True))
                for h in range(2):
                    head = _head_part((t, 128), h, 1)
                    qh_sc[h, rows, :] = jnp.where(head, q_blk * ATT_SCALE, 0.0).astype(BF16)
                    doh_sc[h, rows, :] = jnp.where(head, do_blk, 0.0).astype(BF16)
            dqt_sc[...] = jnp.zeros(dqt_sc.shape, F32)
            dr_sc[...] = jnp.zeros(dr_sc.shape, F32)

        k_blk, v_blk = k_ref[...], v_ref[...]
        kt = k_blk.T
        for h in range(2):
            kaug_sc[h] = _k_augmented(k_blk, c_ref[...], j, h)
            vh_sc[h] = jnp.where(_head_part((t, 128), h, 1), v_blk, 0.0).astype(BF16)
            kt_sc[h] = kt[h * HD:(h + 1) * HD].astype(BF16)
        dk_sc[...] = jnp.zeros(dk_sc.shape, F32)
        dv_sc[...] = jnp.zeros(dv_sc.shape, F32)
        dc_sc[...] = jnp.zeros(dc_sc.shape, F32)

        def inputs(qb, buf):
            qs = pl.ds(pl.multiple_of(qb * t, t), t)
            for h in range(2):
                s_sc[buf][h] = _dot(kaug_sc[h], qaugt_sc[h, :, qs])
                dp_sc[buf][h] = _dot(vh_sc[h], dot_sc[:, qs])

        def elementwise(qb, buf, diagonal):
            qs = pl.ds(pl.multiple_of(qb * t, t), t)
            for h in range(2):
                lse_row, delta_row = lse_ref[0, h:h + 1, qs], delta_sc[h:h + 1, qs]
                tot = jnp.zeros((8, t), F32)
                for i in range(t // r):
                    rows = slice(i * r, (i + 1) * r)
                    x = s_sc[buf][h, rows, :]
                    if diagonal:
                        x = jnp.where(_iota((r, t), 1) >= i * r + _iota((r, t), 0), x, -1e30)
                    pr = jnp.exp(x - lse_row)
                    ds = pr * (dp_sc[buf][h, rows, :] - delta_row)
                    p_sc[buf][h, rows, :] = pr.astype(BF16)
                    ds_sc[buf][h, rows, :] = ds.astype(BF16)
                    dc_sc[h, rows, :] += sum(ds[:, 128 * g:128 * (g + 1)] for g in range(t // 128))
                    tot = _fold8(ds, jnp.add, tot)
                dr_sc[h, :, qs] += tot

        def outputs(qb, buf):
            qs = pl.ds(pl.multiple_of(qb * t, t), t)
            dv_sc[...] += _dot(p_sc[buf][0], doh_sc[0, qs, :]) + _dot(p_sc[buf][1], doh_sc[1, qs, :])
            dk_sc[...] += _dot(ds_sc[buf][0], qh_sc[0, qs, :]) + _dot(ds_sc[buf][1], qh_sc[1, qs, :])
            for h in range(2):
                dqt_sc[h * HD:(h + 1) * HD, qs] += _dot(kt_sc[h], ds_sc[buf][h])

        def pair(a, b, a_diagonal):
            inputs(a, 0)
            inputs(b, 1)
            elementwise(a, 0, a_diagonal)
            outputs(a, 0)
            elementwise(b, 1, False)
            outputs(b, 1)

        def later(u, carry):
            pair(ki + 1 + 2 * u, ki + 2 + 2 * u, False)
            return carry

        n_later = nq - 1 - ki
        lax.fori_loop(0, n_later // 2, later, 0)

        @pl.when(n_later % 2 == 1)
        def _():
            pair(ki, nq - 1, True)

        @pl.when(n_later % 2 == 0)
        def _():
            inputs(ki, 0)
            elementwise(ki, 0, True)
            outputs(ki, 0)

        dk_ref[...] = dk_sc[...].astype(BF16)
        dv_ref[...] = dv_sc[...].astype(BF16)
        lane = _iota((t, 128), 1)
        cols = jnp.where(lane == 0, jnp.sum(dc_sc[0], axis=1, keepdims=True),
                         jnp.where(lane == 1, jnp.sum(dc_sc[1], axis=1, keepdims=True), 0.0))
        dc_ref[0] = cols.T[0:8, :]

        @pl.when(ki == nq - 1)
        def _():
            for c in range(nq):
                rows = slice(c * t, (c + 1) * t)
                dq_ref[rows, :] = dqt_sc[:, rows].T * ATT_SCALE
            dr_ref[0] = _rows01(jnp.sum(dr_sc[0], axis=0, keepdims=True), jnp.sum(dr_sc[1], axis=0, keepdims=True))

    whole = lambda off: pl.BlockSpec((s, 128), functools.partial(lambda j, ki, off: (0, off + j), off=off))
    return pl.pallas_call(
        body, name="attn_bwd", grid=(NH // 2, nq),
        in_specs=[whole(OFF_Q // 128),
                  pl.BlockSpec((t, 128), lambda j, ki: (ki, OFF_K // 128 + j)),
                  pl.BlockSpec((t, 128), lambda j, ki: (ki, OFF_V // 128 + j)),
                  pl.BlockSpec((t, 128), lambda j, ki: (ki, 0)),
                  whole(0),
                  pl.BlockSpec((1, 8, s), lambda j, ki: (j, 0, 0)),
                  whole(0)],
        out_specs=[whole(0),
                   pl.BlockSpec((t, 128), lambda j, ki: (ki, j)),
                   pl.BlockSpec((t, 128), lambda j, ki: (ki, j)),
                   pl.BlockSpec((1, 8, t), lambda j, ki: (j, 0, ki)),
                   pl.BlockSpec((1, 8, s), lambda j, ki: (j, 0, 0))],
        out_shape=[jax.ShapeDtypeStruct((s, D), F32), jax.ShapeDtypeStruct((s, D), BF16), jax.ShapeDtypeStruct((s, D), BF16),
                   jax.ShapeDtypeStruct((NH // 2, 8, s), F32), jax.ShapeDtypeStruct((NH // 2, 8, s), F32)],
        scratch_shapes=[pltpu.VMEM((2, 128, s), BF16), pltpu.VMEM((2, s, 128), BF16), pltpu.VMEM((128, s), BF16),
                        pltpu.VMEM((2, s, 128), BF16), pltpu.VMEM((8, s), F32), pltpu.VMEM((128, s), F32),
                        pltpu.VMEM((2, 8, s), F32), pltpu.VMEM((2, t, 128), BF16), pltpu.VMEM((2, t, 128), BF16),
                        pltpu.VMEM((2, HD, t), BF16)]
        + [pltpu.VMEM((2, t, t), F32)] * 4 + [pltpu.VMEM((2, t, t), BF16)] * 4
        + [pltpu.VMEM((t, 128), F32), pltpu.VMEM((t, 128), F32), pltpu.VMEM((2, t, 128), F32)],
        compiler_params=_params(("parallel", "arbitrary")),
    )(p, p, p, cum, o, lse, do)


def _ln_stats(u):
    mu = _mean(u)
    d = u - mu
    rstd = lax.rsqrt(_mean(d * d) + EPS)
    return d * rstd, rstd


def _ln_bwd(dx, xh, rstd, gam):
    dxh = dx * gam
    return rstd * (dxh - _mean(dxh) - xh * _mean(dxh * xh))


def _rms_bwd(d, xn, r, w):
    t = d * w
    return r * (t - xn * _mean(t * xn)), _colsum(d * xn)


def _mix_norm(y, p, att, w_ssm, w_att, s):
    def fn(pos, y, z, att, w1, w2):
        g = y * _silu(z)
        n1 = g * lax.rsqrt(_mean(g * g) + EPS) * w1
        n2 = att * lax.rsqrt(_mean(att * att) + EPS) * w2
        return (jnp.concatenate([n1, n2], axis=1),)

    return _rowk("mix_norm", fn, s, 512, [(y, D, 0, 0), (p, D, OFF_Z // D, 0), (att, D, 0, 0)],
                 [w_ssm, w_att], [(2 * D, BF16)], [])[0]


def _mix_norm_bwd(dmix, y, p, att, w_ssm, w_att, s):
    def fn(pos, dmix, y, z, att, w1, w2, a1, a2):
        sz = _silu(z)
        g = y * sz
        r1 = lax.rsqrt(_mean(g * g) + EPS)
        dg, dw1 = _rms_bwd(dmix[:, :D], g * r1, r1, w1)
        r2 = lax.rsqrt(_mean(att * att) + EPS)
        datt, dw2 = _rms_bwd(dmix[:, D:], att * r2, r2, w2)
        return dg * sz, dg * y * _dsilu(z), datt, a1 + dw1, a2 + dw2

    return _rowk("mix_norm_bwd", fn, s, 256, [(dmix, 2 * D, 0, 0), (y, D, 0, 0), (p, D, OFF_Z // D, 0), (att, D, 0, 0)],
                 [w_ssm, w_att], [(D, F32), (D, BF16), (D, F32)], [(1, D), (1, D)])


def _ln1(x0, y, g1, gam, bet, sc2, sh2, s):
    def fn(pos, x0, y, g1, gam, bet, sc2, sh2):
        xh, _ = _ln_stats(ALPHA * x0 + (1.0 + g1) * y)
        x1 = xh * gam + bet
        return x1, _modulate(x1, sc2, sh2)

    return _rowk("ln1", fn, s, 512, [(x0, D, 0, 0), (y, D, 0, 0)], [g1, gam, bet, sc2, sh2], [(D, F32), (D, BF16)], [])


def _ln2_loss(x1, ff, tgt, g2, gam, bet, s):
    def fn(pos, x1, ff, tgt, g2, gam, bet, a_loss, a_dgam, a_dbet, a_dg2):
        xh, rstd = _ln_stats(ALPHA * x1 + (1.0 + g2) * ff)
        err = xh * gam + bet - tgt
        dx2 = err * (1.0 / D)
        du = _ln_bwd(dx2, xh, rstd, gam)
        return (du, du * (1.0 + g2), a_loss + _colsum(err * err), a_dgam + _colsum(dx2 * xh),
                a_dbet + _colsum(dx2), a_dg2 + _colsum(du * ff))

    return _rowk("ln2_loss", fn, s, 512, [(x1, D, 0, 0), (ff, D, 0, 0), (tgt, D, 0, 0)], [g2, gam, bet],
                 [(D, F32), (D, BF16)], [(1, D)] * 4)


def _ln1_bwd(dh2, du2, x0, y, g1, gam, bet, sc2, s):
    def fn(pos, dh2, du2, x0, y, g1, gam, bet, sc2, a_sc, a_sh, a_gam, a_bet, a_g1):
        xh, rstd = _ln_stats(ALPHA * x0 + (1.0 + g1) * y)
        x1 = xh * gam + bet
        dx1 = ALPHA * du2 + dh2 * (1.0 + sc2)
        du1 = _ln_bwd(dx1, xh, rstd, gam)
        return (du1, du1 * (1.0 + g1), a_sc + _colsum(dh2 * x1), a_sh + _colsum(dh2), a_gam + _colsum(dx1 * xh),
                a_bet + _colsum(dx1), a_g1 + _colsum(du1 * y))

    return _rowk("ln1_bwd", fn, s, 512, [(dh2, D, 0, 0), (du2, D, 0, 0), (x0, D, 0, 0), (y, D, 0, 0)],
                 [g1, gam, bet, sc2], [(D, F32), (D, BF16)], [(1, D)] * 5)


def _input_grad(dh1, du1, x0, sc1, s):
    def fn(pos, dh1, du1, x0, sc1, a_sc, a_sh):
        return ALPHA * du1 + dh1 * (1.0 + sc1), a_sc + _colsum(dh1 * x0), a_sh + _colsum(dh1)

    return _rowk("input_grad", fn, s, 512, [(dh1, D, 0, 0), (du1, D, 0, 0), (x0, D, 0, 0)], [sc1],
                 [(D, F32)], [(1, D)] * 2)


def _adamw_math(w, grad, m, v):
    m_new = ADAM_B1 * m + (1.0 - ADAM_B1) * grad
    v_new = ADAM_B2 * v + (1.0 - ADAM_B2) * (grad * grad)
    m_hat = m_new / (1.0 - ADAM_B1 ** ADAM_STEP)
    v_hat = v_new / (1.0 - ADAM_B2 ** ADAM_STEP)
    return -ADAM_LR * (m_hat / (jnp.sqrt(v_hat) + ADAM_EPS) + ADAM_WD * w), m_new, v_new


def _small_update(small_all, layout, w, m, v):
    names = [n for n, _, _ in layout]

    def body(*refs):
        all_ref = refs[0]
        w_refs, m_refs, v_refs = [refs[1 + k * len(names):1 + (k + 1) * len(names)] for k in range(3)]
        sum_ref = refs[1 + 3 * len(names)]
        outs = refs[2 + 3 * len(names):]
        total = all_ref[0]
        for k in range(1, N_DEV):
            total = total + all_ref[k]
        sum_ref[...] = total
        for i, (_, off, size) in enumerate(layout):
            grad = total[:, off:off + size]
            delta, m_new, v_new = _adamw_math(w_refs[i][...], grad, m_refs[i][...], v_refs[i][...])
            for o, val in zip(outs[4 * i:4 * i + 4], (grad, delta, m_new, v_new)):
                o[...] = val

    res = pl.pallas_call(
        body, name="small_update",
        out_shape=[jax.ShapeDtypeStruct(small_all.shape[1:], F32)]
        + [jax.ShapeDtypeStruct(w[n].shape, F32) for n in names for _ in range(4)],
        compiler_params=_params(None),
    )(small_all, *[w[n] for n in names], *[m[n] for n in names], *[v[n] for n in names])
    return res[0], {n: res[1 + 4 * i:5 + 4 * i] for i, n in enumerate(names)}


def _adamw(name, w, g, m, v, *, tr, slots, by_columns=False):
    r, c = w.shape

    def body(w_ref, g_ref, m_ref, v_ref, g_out, d_out, m_out, v_out):
        if slots:
            grad = g_ref[0].astype(F32)
            for k in range(1, N_DEV):
                grad = grad + g_ref[k].astype(F32)
        else:
            grad = g_ref[...]
        g_out[...] = grad
        d_out[...], m_out[...], v_out[...] = _adamw_math(w_ref[...], grad, m_ref[...], v_ref[...])

    if by_columns:
        tile = pl.BlockSpec((r, tr), lambda i: (0, i))
        g_spec = pl.BlockSpec((N_DEV, r, tr), lambda i: (0, 0, i)) if slots else tile
    else:
        tile = pl.BlockSpec((tr, c), lambda i: (i, 0))
        g_spec = pl.BlockSpec((N_DEV, tr, c), lambda i: (0, i, 0)) if slots else tile
    return pl.pallas_call(
        body, name=name, grid=((c if by_columns else r) // tr,),
        in_specs=[tile, g_spec, tile, tile], out_specs=[tile] * 4,
        out_shape=[jax.ShapeDtypeStruct((r, c), F32)] * 4,
        compiler_params=_params(("parallel",)),
    )(w, g, m, v)


def _dot_f32(a, b, dims=NN):
    a0, a1, a2 = _split3(a)
    b0, b1, b2 = _split3(b)
    acc = _dot(a0, b0, dims)
    for x, y in ((a0, b1), (a1, b0), (a1, b1), (a0, b2), (a2, b0)):
        acc = acc + _dot(x, y, dims)
    return acc


def _ada_mod(c_all, w_shard, b_shard):
    def body(c_ref, w_ref, b_ref, o_ref):
        act = _silu(c_ref[...])
        act16 = jnp.concatenate([act, jnp.zeros_like(act)], axis=0)
        o_ref[...] = _dot_f32(act16, w_ref[...])[0:N_DEV] + b_ref[...]

    return pl.pallas_call(
        body, name="ada_mod", out_shape=jax.ShapeDtypeStruct((N_DEV, w_shard.shape[1]), F32),
        compiler_params=_params(None),
    )(c_all, w_shard, b_shard)


def _ada_grad(c_all, dmod_cols):
    def body(c_ref, dc_ref, gw_ref):
        act = _silu(c_ref[...])
        act16 = jnp.concatenate([act, jnp.zeros_like(act)], axis=0)
        dm = dc_ref[...]
        dm16 = jnp.concatenate([dm, jnp.zeros_like(dm)], axis=0)
        gw_ref[...] = _dot_f32(act16, dm16, TN)

    return pl.pallas_call(
        body, name="ada_grad", out_shape=jax.ShapeDtypeStruct((D, dmod_cols.shape[1]), F32),
        compiler_params=_params(None),
    )(c_all, dmod_cols)


def _exchange(name, xs, scatter):
    n = len(xs)
    n_peer = N_DEV - 1

    def body(*refs):
        x_refs, o_refs = refs[:n], refs[n:2 * n]
        send_sems, recv_sems, local_sems = refs[2 * n:]
        mx, my, mc = lax.axis_index("x"), lax.axis_index("y"), lax.axis_index("c")
        me = 4 * mx + 2 * my + mc

        def src(a, slot):
            return x_refs[a].at[slot] if scatter else x_refs[a]

        own = [pltpu.make_async_copy(src(a, me), o_refs[a].at[me], local_sems.at[a]) for a in range(n)]
        for cp in own:
            cp.start()
        sends = []
        for d in range(1, N_DEV):
            px = 1 - mx if d & 4 else mx
            py = 1 - my if d & 2 else my
            pc = 1 - mc if d & 1 else mc
            peer = 4 * px + 2 * py + pc
            for a in range(n):
                def copy(src_slot, dst_slot, a=a, d=d, to=(px, py, pc)):
                    return pltpu.make_async_remote_copy(
                        src_ref=src(a, src_slot), dst_ref=o_refs[a].at[dst_slot],
                        send_sem=send_sems.at[a * n_peer + d - 1], recv_sem=recv_sems.at[a * n_peer + d - 1],
                        device_id=to, device_id_type=pl.DeviceIdType.MESH)

                out = copy(peer, me)
                out.start()
                sends.append((out, copy(me, peer)))
        for _, arrival in sends:
            arrival.wait_recv()
        for out, _ in sends:
            out.wait_send()
        for cp in own:
            cp.wait()

    shapes = [tuple(x.shape[1:] if scatter else x.shape) for x in xs]
    return pl.pallas_call(
        body, name=name,
        in_specs=[pl.BlockSpec(memory_space=pl.ANY)] * n, out_specs=[pl.BlockSpec(memory_space=pl.ANY)] * n,
        out_shape=[jax.ShapeDtypeStruct((N_DEV,) + sh, x.dtype) for sh, x in zip(shapes, xs)],
        scratch_shapes=[pltpu.SemaphoreType.DMA((n * n_peer,)), pltpu.SemaphoreType.DMA((n * n_peer,)),
                        pltpu.SemaphoreType.DMA((n,))],
        compiler_params=pltpu.CompilerParams(has_side_effects=True),
    )(*xs)


def _gather_two_level(name, x):
    def body(x_ref, o_ref, send_sems, recv_sems, local_sem):
        mx, my, mc = lax.axis_index("x"), lax.axis_index("y"), lax.axis_index("c")
        me, sibling = (mx, my, mc), (mx, my, 1 - mc)
        chips = [(1 - mx, my), (mx, 1 - my), (1 - mx, 1 - my)]

        def slot(px, py, pc):
            return o_ref.at[4 * px + 2 * py + pc]

        def copy(k, block, to, src=None):
            return pltpu.make_async_remote_copy(
                src_ref=slot(*block) if src is None else src, dst_ref=slot(*block),
                send_sem=send_sems.at[k], recv_sem=recv_sems.at[k], device_id=to, device_id_type=pl.DeviceIdType.MESH)

        mine = pltpu.make_async_copy(x_ref, slot(*me), local_sem)
        mine.start()
        first = [copy(0, me, sibling, src=x_ref)] + [copy(1 + i, me, (*chip, mc), src=x_ref) for i, chip in enumerate(chips)]
        for cp in first:
            cp.start()
        passed = [copy(4 + i, (*chip, mc), sibling) for i, chip in enumerate(chips)]
        for i, chip in enumerate(chips):
            copy(1 + i, (*chip, mc), me).wait_recv()
            passed[i].start()
        copy(0, sibling, me).wait_recv()
        for i, chip in enumerate(chips):
            copy(4 + i, (*chip, 1 - mc), me).wait_recv()
        for cp in first + passed:
            cp.wait_send()
        mine.wait()

    return pl.pallas_call(
        body, name=name,
        in_specs=[pl.BlockSpec(memory_space=pl.ANY)], out_specs=pl.BlockSpec(memory_space=pl.ANY),
        out_shape=jax.ShapeDtypeStruct((N_DEV,) + tuple(x.shape), x.dtype),
        scratch_shapes=[pltpu.SemaphoreType.DMA((7,)), pltpu.SemaphoreType.DMA((7,)), pltpu.SemaphoreType.DMA(())],
        compiler_params=pltpu.CompilerParams(has_side_effects=True),
    )(x)


def _after(x, zero):
    return x if zero is None else x + zero.reshape(-1)[0].astype(x.dtype)


def _exchange_copies(x_refs, land_refs, send_sems, recv_sems, scatter):
    n = len(x_refs)
    n_peer = N_DEV - 1
    mx, my, mc = lax.axis_index("x"), lax.axis_index("y"), lax.axis_index("c")
    me = 4 * mx + 2 * my + mc
    pairs = []
    for d in range(1, N_DEV):
        px = 1 - mx if d & 4 else mx
        py = 1 - my if d & 2 else my
        pc = 1 - mc if d & 1 else mc
        peer = 4 * px + 2 * py + pc
        for a in range(n):
            def copy(src_slot, dst_slot, a=a, d=d, to=(px, py, pc)):
                return pltpu.make_async_remote_copy(
                    src_ref=x_refs[a].at[src_slot] if scatter else x_refs[a], dst_ref=land_refs[a].at[dst_slot],
                    send_sem=send_sems.at[a * n_peer + d - 1], recv_sem=recv_sems.at[a * n_peer + d - 1],
                    device_id=to, device_id_type=pl.DeviceIdType.MESH)

            pairs.append((copy(peer, me), copy(me, peer)))
    return me, pairs


def _exchange_async(name, xs, scatter, collective_id):
    n = len(xs)
    shapes = [tuple(x.shape[1:] if scatter else x.shape) for x in xs]
    x_refs = [jax.new_ref(x, memory_space=pltpu.MemorySpace.HBM) for x in xs]
    land_refs = [jax.empty_ref(jax.ShapeDtypeStruct((N_DEV,) + sh, x.dtype), memory_space=pltpu.MemorySpace.HBM)
                 for sh, x in zip(shapes, xs)]

    @pl.kernel(mesh=plsc.ScalarSubcoreMesh(axis_name="sequencer", num_cores=1), name=name,
               scratch_types=(pltpu.SemaphoreType.DMA((n * (N_DEV - 1),)), pltpu.SemaphoreType.DMA((n * (N_DEV - 1),)),
                              pltpu.SemaphoreType.DMA((n,))),
               compiler_params=pltpu.CompilerParams(collective_id=collective_id))
    def launch(send_sems, recv_sems, own_sems):
        barrier = pltpu.get_barrier_semaphore()
        mx, my, mc = lax.axis_index("x"), lax.axis_index("y"), lax.axis_index("c")
        for d in range(1, N_DEV):
            peer = (1 - mx if d & 4 else mx, 1 - my if d & 2 else my, 1 - mc if d & 1 else mc)
            pl.semaphore_signal(barrier, inc=1, device_id=peer, device_id_type=pl.DeviceIdType.MESH)
        pl.semaphore_wait(barrier, N_DEV - 1)
        me, pairs = _exchange_copies(x_refs, land_refs, send_sems, recv_sems, scatter)
        own = [pltpu.make_async_copy(x_refs[a].at[me] if scatter else x_refs[a], land_refs[a].at[me], own_sems.at[a])
               for a in range(n)]
        for cp in own:
            cp.start()
        for out, _ in pairs:
            out.start()
        for out, arrival in pairs:
            arrival.wait_recv()
            out.wait_send()
        for cp in own:
            cp.wait()

    launch()
    return lambda: [r[...] for r in land_refs]


def _relu2(a):
    r = jnp.maximum(a, 0.0)
    return r * r


def _relu2_grad(acc, r):
    return acc * (2.0 * jnp.sqrt(r.astype(F32)))


def _local_step(x0, tgt, mod, wcat_t, late_weights, send_grads, conv_w, conv_b, dt_bias, a_log, d_skip, ssm_norm_w, f_bias,
                attn_norm_w, ln1_g, ln1_b, ln2_g, ln2_b):
    ff_w = DFF // N_DEV
    s = x0.shape[0]
    tm = min(1024, s)
    ts = min(2048, s)
    sh1, sc1, g1, sh2, sc2, g2 = [mod[:, i * D:(i + 1) * D] for i in range(6)]
    zero = jnp.zeros((1, 128 - 2 * NH), F32)
    bias128 = jnp.concatenate([dt_bias, f_bias, zero], axis=1)
    alog128 = jnp.concatenate([a_log, jnp.zeros((1, 128 - NH), F32)], axis=1)
    dskip_x = jnp.repeat(d_skip, HD, axis=1)
    w_xs, w_bc, b_xs, b_bc = conv_w[:, :D], conv_w[:, D:], conv_b[:, :D], conv_b[:, D:]

    h1, = _rowk("modulate", lambda pos, x, sc, sh: (_modulate(x, sc, sh),), s, 512, [(x0, D, 0, 0)], [sc1, sh1], [(D, BF16)], [])
    p = _mm_nt("in_proj", [(h1, D, 0)], [(wcat_t, D, 0)], n=PCOLS, tm=tm, tn=1152, out_dtype=F32)
    xs_a, bc_a = _conv_fwd(p, w_xs, b_xs, w_bc, b_bc, s)
    y_ssd, states = _ssd_fwd(xs_a, bc_a, p, bias128, alog128, dskip_x, s)
    cum = _cum_fwd(p, bias128, s)
    att, lse = _attn_fwd(p, cum, s)
    wout, w1s, w2 = late_weights()
    ymix = _mix_norm(y_ssd, p, att, ssm_norm_w, attn_norm_w, s)
    y = _mm_nn("out_proj", ymix, wout, tm=tm, tn=1024, tk=2 * D, out_dtype=F32)
    x1, h2 = _ln1(x0, y, g1, ln1_g, ln1_b, sc2, sh2, s)
    tall = min(2048, s)
    r = _mm_nn("ff_in", h2, w1s, tm=tall, tn=ff_w, tk=D, out_dtype=BF16, epi=_relu2)
    ff = _mm_nn("ff_out", r, w2, tm=tall, tn=1024, tk=1024, out_dtype=F32)
    du2, dff, sq_err, d_ln2_g, d_ln2_b, d_g2 = _ln2_loss(x1, ff, tgt, g2, ln2_g, ln2_b, s)

    da1 = _mm_nt("d_ff_hidden", [(dff, D, 0)], [(w2, D, 0)], n=DFF, tm=tm, tn=1024, out_dtype=BF16, epi=_relu2_grad,
                 epi_aux=(r,))
    d_w2 = _mm_tn("d_w_ff_out", r, dff, tm=1024, tn=1024, ts=ts)
    d_w1s = _mm_tn("d_w_ff_in", h2, da1, tm=1024, tn=2 * ff_w, ts=ts, col_shards=True)
    dh2 = _mm_nt("d_ff_input", [(da1, ff_w, k) for k in range(N_DEV)], [(w1s, ff_w, k) for k in range(N_DEV)], n=D,
                 tm=min(512, s), tn=1024, out_dtype=F32)
    du1, dy, d_sc2, d_sh2, d_ln1_g, d_ln1_b, d_g1 = _ln1_bwd(dh2, du2, x0, y, g1, ln1_g, ln1_b, sc2, s)

    dmix = _mm_nt("d_mix", [(dy, D, 0)], [(wout, D, 0)], n=2 * D, tm=tm, tn=1024, out_dtype=F32)
    d_wout = _mm_tn("d_w_out", ymix, dy, tm=1024, tn=1024, ts=ts)
    sent = send_grads("late", [d_w1s, d_w2.reshape(N_DEV, -1, D), d_wout.reshape(N_DEV, -1, D)])
    dy_ssd, dz, datt, d_ssm_w, d_attn_w = _mix_norm_bwd(dmix, y_ssd, p, att, _after(ssm_norm_w, sent), attn_norm_w, s)
    dq, dk, dv, dcs, drs = _attn_bwd(p, cum, att, lse, datt, s)
    dxs_a, dbc_a, ddt_raw, d_alog, d_dskip = _ssd_bwd(dy_ssd, xs_a, bc_a, p, states, bias128, alog128, dskip_x, s)
    dcum = jnp.pad((drs - dcs)[:, :2, :].reshape(NH, s).T, ((0, 0), (NH, 128 - 2 * NH)))
    ddtf, _, d_bias = _cum_bwd(dcum, ddt_raw, p, bias128, s)
    dxs, dbc, d_wc_xs, d_bc_xs, d_wc_bc, d_bc_bc = _conv_bwd(dxs_a, dbc_a, p, w_xs, b_xs, w_bc, b_bc, s)

    segs = [(dz, OFF_Z, D), (dxs, OFF_XS, D), (dq, OFF_Q, D), (dk, OFF_K, D), (dv, OFF_V, D), (dbc, OFF_BC, 512),
            (ddtf, OFF_DTF, 128)]
    d_z, d_xs, d_q, d_k, d_v, d_bcw, d_dtf = [
        _mm_tn("d_w_in_%d" % i, a, h1, tm=min(w, 1024), tn=1024, ts=ts)
        for i, (a, _, w) in enumerate(segs)]
    d_w_in_t = dict(z=d_z, xs=d_xs, bc=d_bcw, dt=d_dtf[:NH], q=d_q, k=d_k, v=d_v, f=d_dtf[NH:2 * NH])
    sent = send_grads("in", [_shard_w_in_grad_t(d_w_in_t)])
    segs[-1] = (_after(ddtf, sent), OFF_DTF, 128)
    dh1 = _mm_nt("d_h1", [(a, w, 0) for a, _, w in segs], [(wcat_t, w, off // w) for _, off, w in segs], n=D,
                 tm=min(512, s), tn=1024, out_dtype=F32, b_rows=True)
    grad_x, d_sc1, d_sh1 = _input_grad(dh1, du1, x0, sc1, s)

    return dict(
        loss=(0.5 / D) * jnp.sum(sq_err), grad_x=grad_x,
        d_mod=jnp.concatenate([d_sh1, d_sc1, d_g1, d_sh2, d_sc2, d_g2], axis=1),
        d_conv_w=jnp.concatenate([d_wc_xs[:4], d_wc_bc[:4]], axis=1), d_conv_b=jnp.concatenate([d_bc_xs, d_bc_bc], axis=1),
        d_ssm_norm_w=d_ssm_w, d_attn_norm_w=d_attn_w, d_ln1_g=d_ln1_g, d_ln1_b=d_ln1_b, d_ln2_g=d_ln2_g, d_ln2_b=d_ln2_b,
        d_gate_bias=d_bias, d_a_log=d_alog, d_d_skip=d_dskip)


W_IN_SEGS = [('z', W_Z, D), ('xs', W_XS, D), ('bc', W_BC, 512), ('dt', W_DT, NH), ('q', W_Q, D), ('k', W_K, D),
             ('v', W_V, D), ('f', W_F, NH)]
SHARD_W = IN_COLS // N_DEV


def _pack_w_in_t(w_in_t):
    seg = {n: w_in_t[off:off + w] for n, off, w in W_IN_SEGS}
    return jnp.concatenate([seg['z'], seg['xs'], seg['q'], seg['k'], seg['v'], seg['bc'], seg['dt'], seg['f'],
                            jnp.zeros((128 - 2 * NH, D), w_in_t.dtype)], axis=0)


def _shard_w_in_grad_t(d_w_in_t):
    blocks = []
    for dev in range(N_DEV):
        lo, hi = dev * SHARD_W, (dev + 1) * SHARD_W
        pieces = [d_w_in_t[n][max(lo, off) - off:min(hi, off + w) - off] for n, off, w in W_IN_SEGS
                  if max(lo, off) < min(hi, off + w)]
        blocks.append(jnp.concatenate(pieces, axis=0))
    return jnp.stack(blocks, axis=0)


WEIGHTS = ['w_ada', 'b_ada', 'w_in', 'conv_w', 'conv_b', 'dt_bias', 'a_log', 'd_skip', 'ssm_norm_w', 'f_bias',
           'attn_norm_w', 'w_out', 'ln1_g', 'ln1_b', 'w_ff_in', 'w_ff_out', 'ln2_g', 'ln2_b']
BIG = ['w_in', 'w_out', 'w_ff_in', 'w_ff_out']
SMALL_LAYOUT = [('b_ada', 0, 6 * D), ('conv_b', 12288, 1536), ('ssm_norm_w', 13824, D), ('attn_norm_w', 14848, D),
                ('ln1_g', 15872, D), ('ln1_b', 16896, D), ('ln2_g', 17920, D), ('ln2_b', 18944, D),
                ('dt_bias', 19968, NH), ('f_bias', 19968 + NH, NH), ('a_log', 20096, NH), ('d_skip', 20224, NH)]
SMALL_LOSS_LANE = 20352


def _pad_lanes(v, n=128):
    return jnp.pad(v, ((0, 0), (0, n - v.shape[1])))


def kernel(x, c, w_ada, b_ada, w_in, conv_w, conv_b, dt_bias, a_log, d_skip, ssm_norm_w, f_bias, attn_norm_w, w_out, ln1_g, ln1_b, w_ff_in, w_ff_out, ln2_g, ln2_b, loss_target, m_w_ada, m_b_ada, m_w_in, m_conv_w, m_conv_b, m_dt_bias, m_a_log, m_d_skip, m_ssm_norm_w, m_f_bias, m_attn_norm_w, m_w_out, m_ln1_g, m_ln1_b, m_w_ff_in, m_w_ff_out, m_ln2_g, m_ln2_b, v_w_ada, v_b_ada, v_w_in, v_conv_w, v_conv_b, v_dt_bias, v_a_log, v_d_skip, v_ssm_norm_w, v_f_bias, v_attn_norm_w, v_w_out, v_ln1_g, v_ln1_b, v_w_ff_in, v_w_ff_out, v_ln2_g, v_ln2_b):
    args = dict(locals())
    w = {n: args[n] for n in WEIGHTS}
    m = {n: args['m_' + n] for n in WEIGHTS}
    v = {n: args['v_' + n] for n in WEIGHTS}
    me = 4 * lax.axis_index("x") + 2 * lax.axis_index("y") + lax.axis_index("c")
    ada_cols = 6 * D // N_DEV
    conv_cols = conv_w.shape[2]

    c_all, conv_all = _exchange("gather_cond", [c, conv_w[0]], False)
    c_all = c_all.reshape(N_DEV, D)
    conv_w_full = conv_all.transpose(1, 0, 2).reshape(4, N_DEV * conv_cols)
    b_shard = lax.dynamic_slice(b_ada, (0, me * ada_cols), (1, ada_cols))
    mod_all, = _exchange("gather_mod", [_ada_mod(c_all, w_ada[0], b_shard)], False)
    mod = lax.dynamic_index_in_dim(mod_all, me, axis=1, keepdims=False).reshape(1, 6 * D)

    w_in_t = _after(jnp.swapaxes(w_in[0], 0, 1).astype(BF16), mod * 0)
    win_s = _gather_two_level("gather_w_in", w_in_t)
    first_done = win_s[0, 0:1, 0:1] * 0
    rest = _exchange_async("gather_rest", [_after(w[n][0].astype(BF16), first_done) for n in BIG[1:]], False, 1)

    def late_weights():
        wout_s, w1s, w2_s = rest()
        return wout_s.reshape(2 * D, D), w1s, w2_s.reshape(DFF, D)

    sends = {}

    def send_grads(tag, blocks):
        sends[tag] = _exchange_async("scatter_" + tag, blocks, True, {'late': 2, 'in': 3}[tag])
        return sum(b.reshape(-1)[0].astype(F32) * 0 for b in blocks)

    out = _local_step(x[0], loss_target[0], mod, _pack_w_in_t(win_s.reshape(IN_COLS, D)), late_weights, send_grads,
                      conv_w_full, conv_b, dt_bias, a_log, d_skip, ssm_norm_w, f_bias, attn_norm_w, ln1_g, ln1_b, ln2_g, ln2_b)

    small = jnp.concatenate(
        [out['d_mod'], out['d_conv_w'].reshape(1, -1), out['d_conv_b'], out['d_ssm_norm_w'], out['d_attn_norm_w'],
         out['d_ln1_g'], out['d_ln1_b'], out['d_ln2_g'], out['d_ln2_b'], out['d_gate_bias'], out['d_a_log'],
         out['d_d_skip'], _pad_lanes(out['loss'].reshape(1, 1))], axis=1)
    small_landed = _exchange_async("gather_small", [small], False, 4)
    (g_ff_in, g_ff_out, g_out), (g_in,) = sends['late'](), sends['in']()
    g_parts = dict(w_ff_in=g_ff_in, w_ff_out=g_ff_out, w_out=g_out, w_in=g_in)
    big = {n: _adamw("adamw_" + n, w[n][0], g_parts[n], m[n][0], v[n][0], tr=256, slots=True) for n in BIG[1:]}
    t = lambda a: jnp.swapaxes(a[0], 0, 1)
    big['w_in'] = [jnp.swapaxes(r, 0, 1) for r in _adamw("adamw_w_in", t(w_in), g_parts['w_in'], t(m_w_in), t(v_w_in),
                                                         tr=256, slots=True, by_columns=True)]
    big_done = sum(big[n][1][0:1, 0:1] * 0 for n in BIG)
    small_all = _after(small_landed()[0], big_done)
    ssum, small_res = _small_update(small_all, SMALL_LAYOUT, w, m, v)
    dmod_all = small_all[:, 0, :6 * D]
    g_w_ada = _ada_grad(c_all, lax.dynamic_slice(dmod_all, (0, me * ada_cols), (N_DEV, ada_cols)))
    ada = _adamw("adamw_ada", w_ada[0], g_w_ada, m_w_ada[0], v_w_ada[0], tr=256, slots=False)
    g_conv_w = lax.dynamic_slice(ssum[:, 6 * D:6 * D + 4 * N_DEV * conv_cols].reshape(4, N_DEV * conv_cols),
                                 (0, me * conv_cols), (4, conv_cols))
    conv = _adamw("adamw_conv_w", conv_w[0], g_conv_w, m_conv_w[0], v_conv_w[0], tr=4, slots=False)

    results = []
    for k in range(4):
        vals = {n: small_res[n][k] for n in small_res}
        vals['w_ada'], vals['conv_w'] = ada[k][None], conv[k][None]
        for n in BIG:
            vals[n] = big[n][k][None]
        results.append(vals)
    return (ssum[0, SMALL_LOSS_LANE], out['grad_x'][None], *[res[n] for res in results for n in WEIGHTS])
```

```python
import functools

import jax
import jax.numpy as jnp
from jax import lax
from jax.experimental import pallas as pl
from jax.experimental.pallas import tpu as pltpu
from jax.experimental.pallas import tpu_sc as plsc

F32, BF16 = jnp.float32, jnp.bfloat16

N_DEV = 8
D = 1024
NH, HD = 16, 64
NSTATE = 128
CHUNK = 128
HG = 8
DFF = 4096
ALPHA = 2.0 ** 0.25
EPS = 1e-5
ATT_SCALE = HD ** -0.5

OFF_Z, OFF_XS, OFF_Q, OFF_K, OFF_V, OFF_BC, OFF_DTF = 0, 1024, 2048, 3072, 4096, 5120, 5632
PCOLS = 5760
W_Z, W_XS, W_BC, W_DT, W_Q, W_K, W_V, W_F = 0, 1024, 2048, 2560, 2576, 3600, 4624, 5648
IN_COLS = 5664

ADAM_LR, ADAM_B1, ADAM_B2, ADAM_EPS, ADAM_WD, ADAM_STEP = 0.001, 0.9, 0.999, 1e-08, 0.01, 10

VMEM_LIMIT = 56 << 20

NN = (((1,), (0,)), ((), ()))
NT = (((1,), (1,)), ((), ()))
TN = (((0,), (0,)), ((), ()))


def _dot(a, b, dims=NN):
    return lax.dot_general(a, b, dims, preferred_element_type=F32)


def _bdot(a, b, dims=NN):
    return _dot(a.astype(BF16), b.astype(BF16), dims)


def _split3(v, terms=3):
    parts, rest = [], v
    for _ in range(terms):
        p = rest.astype(BF16)
        parts.append(p)
        rest = rest - p.astype(F32)
    return parts


def _sel_left(m01, v):
    return sum(_dot(m01, p) for p in _split3(v))


def _sel_right(v, m01, dims=NN, terms=3):
    return sum(_dot(p, m01, dims) for p in _split3(v, terms))


def _iota(shape, dim):
    return lax.broadcasted_iota(jnp.int32, shape, dim)


def _tri_lower(n):
    return (_iota((n, n), 1) <= _iota((n, n), 0)).astype(BF16)


def _tri_upper(n):
    return (_iota((n, n), 1) >= _iota((n, n), 0)).astype(BF16)


def _head_expand():
    return (lax.shift_right_logical(_iota((128, D), 1), 6) == _iota((128, D), 0)).astype(BF16)


def _head_reduce():
    return (lax.shift_right_logical(_iota((D, 128), 0), 6) == _iota((D, 128), 1)).astype(BF16)


def _sigmoid(x):
    return 1.0 / (1.0 + jnp.exp(-x))


def _silu(x):
    return x * _sigmoid(x)


def _dsilu(x):
    s = _sigmoid(x)
    return s * (1.0 + x * (1.0 - s))


def _softplus(x):
    return jnp.maximum(x, 0.0) + jnp.log(1.0 + jnp.exp(-jnp.abs(x)))


def _log_sigmoid(x):
    return jnp.minimum(x, 0.0) - jnp.log(1.0 + jnp.exp(-jnp.abs(x)))


def _params(sem):
    return pltpu.CompilerParams(dimension_semantics=sem, vmem_limit_bytes=VMEM_LIMIT)


def _mm_nn(name, a, b, *, tm, tn, tk, out_dtype, pro=None, aux=(), epi=None):
    m, k_all = a.shape
    b_sharded = b.ndim == 3
    n = b.shape[0] * b.shape[2] if b_sharded else b.shape[1]
    per_tile = tn // b.shape[2] if b_sharded else 1
    assert not b_sharded or tn == per_tile * b.shape[2]
    nk = k_all // tk
    n_aux = len(aux)
    b_spec = (pl.BlockSpec((per_tile, tk, b.shape[2]), lambda i, j, k: (j, k, 0)) if b_sharded
              else pl.BlockSpec((tk, tn), lambda i, j, k: (k, j)))

    def body(a_ref, b_ref, *rest):
        aux_refs, o_ref = rest[:n_aux], rest[n_aux]
        at = a_ref[...]
        if pro is not None:
            at = pro(at, *[r[...] for r in aux_refs])
        if b_sharded:
            part = jnp.concatenate([_bdot(at, b_ref[q]) for q in range(per_tile)], axis=1)
        else:
            part = _bdot(at, b_ref[...])
        if nk == 1:
            o_ref[...] = (part if epi is None else epi(part)).astype(out_dtype)
            return
        assert epi is None
        acc_ref = rest[n_aux + 1]
        kk = pl.program_id(2)

        @pl.when(kk == 0)
        def _():
            acc_ref[...] = part

        @pl.when(kk > 0)
        def _():
            acc_ref[...] += part

        @pl.when(kk == nk - 1)
        def _():
            o_ref[...] = acc_ref[...].astype(out_dtype)

    return pl.pallas_call(
        body, name=name,
        grid=(m // tm, n // tn, nk),
        in_specs=[pl.BlockSpec((tm, tk), lambda i, j, k: (i, k)), b_spec]
        + [pl.BlockSpec((1, tk), lambda i, j, k: (0, k)) for _ in aux],
        out_specs=pl.BlockSpec((tm, tn), lambda i, j, k: (i, j)),
        out_shape=jax.ShapeDtypeStruct((m, n), out_dtype),
        scratch_shapes=[] if nk == 1 else [pltpu.VMEM((tm, tn), F32)],
        compiler_params=_params(("parallel", "parallel", "arbitrary")),
    )(a, b, *aux)


def _mm_nt(name, a_list, b_list, *, n, tm, tn, out_dtype, epi=None, epi_aux=(), b_rows=False):
    m = a_list[0][0].shape[0]
    n_op = len(a_list)
    n_epi = len(epi_aux)
    dims = NN if b_rows else NT

    def body(*refs):
        a_refs, b_refs = refs[:n_op], refs[n_op:2 * n_op]
        e_refs, o_ref = refs[2 * n_op:2 * n_op + n_epi], refs[2 * n_op + n_epi]
        acc = None
        for a_ref, b_ref in zip(a_refs, b_refs):
            part = _bdot(a_ref[...], b_ref[...], dims)
            acc = part if acc is None else acc + part
        if epi is not None:
            acc = epi(acc, *[r[...] for r in e_refs])
        o_ref[...] = acc.astype(out_dtype)

    in_specs = [pl.BlockSpec((tm, w), functools.partial(lambda i, j, cb: (i, cb), cb=cb)) for (_, w, cb) in a_list]
    for (b, w, cb) in b_list:
        if b_rows:
            in_specs.append(pl.BlockSpec((w, tn), functools.partial(lambda i, j, cb: (cb, j), cb=cb)))
        elif b.ndim == 3:
            in_specs.append(pl.BlockSpec((None, tn, w), functools.partial(lambda i, j, cb: (cb, j, 0), cb=cb)))
        else:
            in_specs.append(pl.BlockSpec((tn, w), functools.partial(lambda i, j, cb: (j, cb), cb=cb)))
    in_specs += [pl.BlockSpec((tm, tn), lambda i, j: (i, j)) for _ in epi_aux]
    return pl.pallas_call(
        body, name=name,
        grid=(m // tm, n // tn),
        in_specs=in_specs,
        out_specs=pl.BlockSpec((tm, tn), lambda i, j: (i, j)),
        out_shape=jax.ShapeDtypeStruct((m, n), out_dtype),
        compiler_params=_params(("parallel", "parallel")),
    )(*[a for (a, _, _) in a_list], *[b for (b, _, _) in b_list], *epi_aux)


def _mm_tn(name, a, b, *, tm, tn, ts, pro=None, aux=(), col_shards=False):
    s_all, ka = a.shape
    nb = b.shape[1]
    n_aux = len(aux)
    ns = s_all // ts
    shard_w = nb // N_DEV
    per_tile = tn // shard_w
    assert not col_shards or tn == per_tile * shard_w

    def body(a_ref, b_ref, *rest):
        aux_refs, o_ref, acc_ref = rest[:n_aux], rest[n_aux], rest[n_aux + 1]
        at = a_ref[...]
        if pro is not None:
            at = pro(at, *[r[...] for r in aux_refs])
        part = _bdot(at, b_ref[...], TN)
        ss = pl.program_id(2)

        @pl.when(ss == 0)
        def _():
            acc_ref[...] = part

        @pl.when(ss > 0)
        def _():
            acc_ref[...] += part

        @pl.when(ss == ns - 1)
        def _():
            if col_shards:
                for q in range(per_tile):
                    o_ref[q] = acc_ref[:, q * shard_w:(q + 1) * shard_w].astype(BF16)
            else:
                o_ref[...] = acc_ref[...].astype(BF16)

    if col_shards:
        out_spec = pl.BlockSpec((per_tile, tm, shard_w), lambda i, j, s: (j, i, 0))
        out_shape = jax.ShapeDtypeStruct((N_DEV, ka, shard_w), BF16)
    else:
        out_spec = pl.BlockSpec((tm, tn), lambda i, j, s: (i, j))
        out_shape = jax.ShapeDtypeStruct((ka, nb), BF16)
    return pl.pallas_call(
        body, name=name,
        grid=(ka // tm, nb // tn, ns),
        in_specs=[pl.BlockSpec((ts, tm), lambda i, j, s: (s, i)),
                  pl.BlockSpec((ts, tn), lambda i, j, s: (s, j))]
        + [pl.BlockSpec((1, tm), lambda i, j, s: (0, i)) for _ in aux],
        out_specs=out_spec, out_shape=out_shape,
        scratch_shapes=[pltpu.VMEM((tm, tn), F32)],
        compiler_params=_params(("parallel", "parallel", "arbitrary")),
    )(a, b, *aux)


def _rowk(name, fn, n_rows, tr, rows, fulls, outs, accs, reverse=False):
    n = n_rows // tr
    n_row, n_full, n_out, n_acc = len(rows), len(fulls), len(outs), len(accs)

    def pos(i):
        return (n - 1 - i) if reverse else i

    def body(*refs):
        row_refs = refs[:n_row]
        full_refs = refs[n_row:n_row + n_full]
        out_refs = refs[n_row + n_full:n_row + n_full + n_out]
        acc_refs = refs[n_row + n_full + n_out:]
        i = pl.program_id(0)

        @pl.when(i == 0)
        def _():
            for r in acc_refs:
                r[...] = jnp.zeros(r.shape, r.dtype)

        res = fn(pos(i), *[r[...] for r in row_refs], *[r[...] for r in full_refs], *[r[...] for r in acc_refs])
        for r, v in zip(out_refs + acc_refs, res):
            r[...] = v.astype(r.dtype)

    def row_map(i, cb, shift):
        return (jnp.clip(pos(i) + shift, 0, n - 1), cb)

    def halo_map(i, cb, shift):
        tile = jnp.clip(pos(i) + shift, 0, n - 1)
        return (tile * (tr // 8) + (tr // 8 - 1 if shift < 0 else 0), cb)

    in_specs = [pl.BlockSpec((tr, w), functools.partial(row_map, cb=cb, shift=sh)) if sh == 0 else
                pl.BlockSpec((8, w), functools.partial(halo_map, cb=cb, shift=sh)) for (_, w, cb, sh) in rows]
    in_specs += [pl.BlockSpec(f.shape, functools.partial(lambda i, nd: (0,) * nd, nd=f.ndim)) for f in fulls]
    out_specs = [pl.BlockSpec((tr, w), lambda i: (pos(i), 0)) for (w, _) in outs]
    out_specs += [pl.BlockSpec((r, w), lambda i: (0, 0)) for (r, w) in accs]
    out_shape = [jax.ShapeDtypeStruct((n_rows, w), dt) for (w, dt) in outs]
    out_shape += [jax.ShapeDtypeStruct((r, w), F32) for (r, w) in accs]
    return pl.pallas_call(
        body, name=name, grid=(n,), in_specs=in_specs, out_specs=out_specs, out_shape=out_shape,
        compiler_params=_params(("arbitrary",)),
    )(*[a for (a, _, _, _) in rows], *fulls)


def _colsum(x):
    return jnp.sum(x, axis=0, keepdims=True)


def _mean(x):
    return jnp.mean(x, axis=-1, keepdims=True)


def _modulate(x, sc, sh):
    return x * (1.0 + sc) + sh


def _shift_down(cur, prev8, j):
    tr = cur.shape[0]
    row8 = _iota(prev8.shape, 0)
    head = jnp.where(row8 < j, pltpu.roll(prev8, j, 0), pltpu.roll(cur[0:8], j, 0))
    return head if tr == 8 else jnp.concatenate([head, pltpu.roll(cur, j, 0)[8:]], axis=0)


def _shift_up(cur, next8, j):
    tr = cur.shape[0]
    row8 = _iota(next8.shape, 0)
    tail = jnp.where(row8 < 8 - j, pltpu.roll(cur[tr - 8:], 8 - j, 0), pltpu.roll(next8, 8 - j, 0))
    return jnp.concatenate([pltpu.roll(cur, tr - j, 0)[:tr - 8], tail], axis=0)


def _conv(cur, prev, w, b):
    out = cur * w[3:4] + b
    for j in (1, 2, 3):
        out = out + _shift_down(cur, prev, j) * w[3 - j:4 - j]
    return out


def _conv_fwd(p, w_xs, b_xs, w_bc, b_bc, s):
    def fn(pos, xs, xs_prev, bc, bc_prev, w_xs, b_xs, w_bc, b_bc):
        first = pos == 0
        xs_prev = jnp.where(first, 0.0, xs_prev)
        bc_prev = jnp.where(first, 0.0, bc_prev)
        return _silu(_conv(xs, xs_prev, w_xs, b_xs)), _silu(_conv(bc, bc_prev, w_bc, b_bc))

    return _rowk("conv_fwd", fn, s, 256,
                 [(p, D, OFF_XS // D, 0), (p, D, OFF_XS // D, -1), (p, 512, OFF_BC // 512, 0), (p, 512, OFF_BC // 512, -1)],
                 [w_xs, b_xs, w_bc, b_bc], [(D, F32), (512, F32)], [])


def _conv_bwd(dxs_a, dbc_a, p, w_xs, b_xs, w_bc, b_bc, s):
    tr = 256
    n = s // tr

    def fn(pos, da1, da1n, x1, x1p, x1n, da2, da2n, x2, x2p, x2n, w1, b1, w2, b2, aw1, ab1, aw2, ab2):
        dx1, dw1, db1 = _conv_bwd_fn(pos, n, da1, da1n, x1, x1p, x1n, w1, b1)
        dx2, dw2, db2 = _conv_bwd_fn(pos, n, da2, da2n, x2, x2p, x2n, w2, b2)
        return dx1, dx2, aw1 + dw1, ab1 + db1, aw2 + dw2, ab2 + db2

    cx, cb = OFF_XS // D, OFF_BC // 512
    return _rowk("conv_bwd", fn, s, tr,
                 [(dxs_a, D, 0, 0), (dxs_a, D, 0, 1), (p, D, cx, 0), (p, D, cx, -1), (p, D, cx, 1),
                  (dbc_a, 512, 0, 0), (dbc_a, 512, 0, 1), (p, 512, cb, 0), (p, 512, cb, -1), (p, 512, cb, 1)],
                 [w_xs, b_xs, w_bc, b_bc], [(D, BF16), (512, BF16)], [(8, D), (1, D), (8, 512), (1, 512)])


def _conv_bwd_fn(pos, n, da, da_next, x, x_prev, x_next, w, b):
    first, last = pos == 0, pos == n - 1
    x_prev = jnp.where(first, 0.0, x_prev)
    shifted = {j: _shift_down(x, x_prev, j) for j in (1, 2, 3)}
    conv = x * w[3:4] + b
    for j in (1, 2, 3):
        conv = conv + shifted[j] * w[3 - j:4 - j]
    dc = da * _dsilu(conv)
    dc_next = jnp.where(last, 0.0, da_next * _dsilu(_conv(x_next, x[x.shape[0] - 8:], w, b)))
    dx = dc * w[3:4]
    dws = [None] * 4
    dws[3] = _colsum(dc * x)
    for j in (1, 2, 3):
        dx = dx + _shift_up(dc, dc_next, j) * w[3 - j:4 - j]
        dws[3 - j] = _colsum(dc * shifted[j])
    row = _iota((8, x.shape[1]), 0)
    dw = jnp.zeros((8, x.shape[1]), F32)
    for k in range(4):
        dw = jnp.where(row == k, dws[k], dw)
    return dx, dw, _colsum(dc)


def _ssd_gates(dtf, bias, a_log):
    lane = _iota(dtf.shape, 1)
    head = lane < NH
    dt = jnp.where(head, _softplus(dtf + bias), 0.0)
    a_neg = jnp.where(_iota(a_log.shape, 1) < NH, -jnp.exp(a_log), 0.0)
    a = dt * a_neg
    cs = _sel_left(_tri_lower(CHUNK), a)
    return dt, a_neg, cs


def _decay_mask(cs_ref, cst_ref, h):
    diff = cs_ref[:, h:h + 1] - cst_ref[h:h + 1, :]
    low = _iota((CHUNK, CHUNK), 1) <= _iota((CHUNK, CHUNK), 0)
    return jnp.where(low, jnp.exp(jnp.minimum(diff, 0.0)), 0.0)


def _ssd_fwd(xs_a, bc_a, p, bias128, alog128, dskip_x, s):
    nc = s // CHUNK
    t = CHUNK

    def body(xs_ref, bc_ref, dtf_ref, bias_ref, alog_ref, dsk_ref, y_ref, st_ref,
             state, x_sc, xw_sc, cs_sc, cst_sc, yd_sc):
        c = pl.program_id(0)

        @pl.when(c == 0)
        def _():
            state[...] = jnp.zeros(state.shape, F32)

        dt, _, cs = _ssd_gates(dtf_ref[...], bias_ref[...], alog_ref[...])
        cs_sc[...] = cs
        cst_sc[...] = cs.T
        cs_last = cs[t - 1:t, :]
        expand = _head_expand()
        ex = _sel_right(jnp.concatenate([dt, jnp.exp(cs), jnp.exp(cs_last - cs)], axis=0), expand, terms=2)
        dt_x, eo_x, we_x = ex[0:t], ex[t:2 * t], ex[2 * t:3 * t]
        g_x = _sel_right(jnp.broadcast_to(jnp.exp(cs_last), (8, 128)), expand)[0:1]
        xs = xs_ref[...]
        x = xs * dt_x
        x_sc[...] = x.astype(BF16)
        xw_sc[...] = (x * we_x).astype(BF16)
        prev = state[...]
        st_ref[0] = prev
        prev_b = prev.astype(BF16)
        for g in range(2):
            cols = slice(g * 512, (g + 1) * 512)
            b_g = bc_ref[:, g * 128:(g + 1) * 128].astype(BF16)
            c_g = bc_ref[:, 256 + g * 128:256 + (g + 1) * 128].astype(BF16)
            gmat = _dot(c_g, b_g, NT)
            y_off = _dot(c_g, prev_b[:, cols]) * eo_x[:, cols]
            s_loc = _dot(b_g, xw_sc[:, cols], TN)
            state[:, cols] = g_x[:, cols] * prev[:, cols] + s_loc
            for e in range(HG):
                h = g * HG + e
                m = gmat * _decay_mask(cs_sc, cst_sc, h)
                yd_sc[:, h * HD:(h + 1) * HD] = _dot(m.astype(BF16), x_sc[:, h * HD:(h + 1) * HD])
            y_ref[:, cols] = yd_sc[:, cols] + y_off + dsk_ref[:, cols] * xs[:, cols]

    return pl.pallas_call(
        body, name="ssd_fwd", grid=(nc,),
        in_specs=[pl.BlockSpec((t, D), lambda c: (c, 0)),
                  pl.BlockSpec((t, 512), lambda c: (c, 0)),
                  pl.BlockSpec((t, 128), lambda c: (c, OFF_DTF // 128)),
                  pl.BlockSpec((1, 128), lambda c: (0, 0)),
                  pl.BlockSpec((1, 128), lambda c: (0, 0)),
                  pl.BlockSpec((1, D), lambda c: (0, 0))],
        out_specs=[pl.BlockSpec((t, D), lambda c: (c, 0)),
                   pl.BlockSpec((1, NSTATE, D), lambda c: (c, 0, 0))],
        out_shape=[jax.ShapeDtypeStruct((s, D), F32), jax.ShapeDtypeStruct((nc, NSTATE, D), F32)],
        scratch_shapes=[pltpu.VMEM((NSTATE, D), F32), pltpu.VMEM((t, D), BF16), pltpu.VMEM((t, D), BF16),
                        pltpu.VMEM((t, 128), F32), pltpu.VMEM((128, t), F32), pltpu.VMEM((t, D), F32)],
        compiler_params=_params(("arbitrary",)),
    )(xs_a, bc_a, p, bias128, alog128, dskip_x)


def _ssd_bwd(dy, xs_a, bc_a, p, states, bias128, alog128, dskip_x, s):
    nc = s // CHUNK
    t = CHUNK

    def body(dy_ref, xs_ref, bc_ref, dtf_ref, st_ref, bias_ref, alog_ref, dsk_ref,
             dxs_ref, dbc_ref, ddt_ref, dalog_ref, dskip_ref,
             dstate, x_sc, dy_sc, dx_sc, deo_sc, dwe_sc, cs_sc, cst_sc, dcol_sc, drow_sc):
        i = pl.program_id(0)

        @pl.when(i == 0)
        def _():
            dstate[...] = jnp.zeros(dstate.shape, F32)
            dalog_ref[...] = jnp.zeros(dalog_ref.shape, F32)
            dskip_ref[...] = jnp.zeros(dskip_ref.shape, F32)

        dtf = dtf_ref[...]
        dt, a_neg, cs = _ssd_gates(dtf, bias_ref[...], alog_ref[...])
        cs_sc[...] = cs
        cst_sc[...] = cs.T
        cs_last = cs[t - 1:t, :]
        eo, we, g_end = jnp.exp(cs), jnp.exp(cs_last - cs), jnp.exp(cs_last)
        expand, reduce = _head_expand(), _head_reduce()
        ex = _sel_right(jnp.concatenate([dt, eo, we], axis=0), expand, terms=2)
        dt_x, eo_x, we_x = ex[0:t], ex[t:2 * t], ex[2 * t:3 * t]
        g_x = _sel_right(jnp.broadcast_to(g_end, (8, 128)), expand)[0:1]
        xs = xs_ref[...]
        dyv = dy_ref[...]
        x = xs * dt_x
        x_sc[...] = x.astype(BF16)
        dy_sc[...] = dyv.astype(BF16)
        dyo_b = (dyv * eo_x).astype(BF16)
        xw_b = (x * we_x).astype(BF16)
        prev = st_ref[0]
        prev_b = prev.astype(BF16)
        dnext = dstate[...]
        dnext_b = dnext.astype(BF16)
        dcol_sc[...] = jnp.zeros(dcol_sc.shape, F32)
        drow_sc[...] = jnp.zeros(drow_sc.shape, F32)
        lane_row = _iota((1, 128), 1)
        sub_col = _iota((128, 1), 0)
        for g in range(2):
            cols = slice(g * 512, (g + 1) * 512)
            b_g = bc_ref[:, g * 128:(g + 1) * 128].astype(BF16)
            c_g = bc_ref[:, 256 + g * 128:256 + (g + 1) * 128].astype(BF16)
            gmat = _dot(c_g, b_g, NT)
            b_ds = _dot(b_g, dnext_b[:, cols])
            c_s = _dot(c_g, prev_b[:, cols])
            dx_sc[:, cols] = b_ds * we_x[:, cols]
            deo_sc[:, cols] = dyv[:, cols] * c_s
            dwe_sc[:, cols] = b_ds * x[:, cols]
            db = _dot(xw_b[:, cols], dnext_b[:, cols], NT)
            dc = _dot(dyo_b[:, cols], prev_b[:, cols], NT)
            dstate[:, cols] = g_x[:, cols] * dnext[:, cols] + _dot(c_g, dyo_b[:, cols], TN)
            dg = jnp.zeros((t, t), F32)
            for e in range(HG):
                h = g * HG + e
                hc = slice(h * HD, (h + 1) * HD)
                lmat = _decay_mask(cs_sc, cst_sc, h)
                m = gmat * lmat
                dx_sc[:, hc] += _dot(m.astype(BF16), dy_sc[:, hc], TN)
                dm = _dot(dy_sc[:, hc], x_sc[:, hc], NT)
                dg = dg + dm * lmat
                qm = dm * m
                dcol_sc[...] += jnp.sum(qm, axis=1, keepdims=True) * (lane_row == h).astype(F32)
                drow_sc[...] += (sub_col == h).astype(F32) * jnp.sum(qm, axis=0, keepdims=True)
            dg_b = dg.astype(BF16)
            dbc_ref[:, g * 128:(g + 1) * 128] = db + _dot(dg_b, c_g, TN)
            dbc_ref[:, 256 + g * 128:256 + (g + 1) * 128] = dc + _dot(dg_b, b_g)
        d_eo = _sel_right(deo_sc[...], reduce, terms=2)
        d_we = _sel_right(dwe_sc[...], reduce, terms=2)
        d_gend = _sel_right(jnp.broadcast_to(_colsum(dnext * prev), (8, D)), reduce)[0:1]
        d_cs = dcol_sc[...] - drow_sc[...].T + d_eo * eo - d_we * we
        extra = _colsum(d_we * we) + d_gend * g_end
        d_cs = d_cs + jnp.where(_iota((t, 128), 0) == t - 1, extra, 0.0)
        da = _sel_left(_tri_upper(t), d_cs)
        dx = dx_sc[...]
        ddt = _sel_right(dx * xs, reduce, terms=2) + da * a_neg
        dxs_ref[...] = dx * dt_x + dsk_ref[...] * dyv
        ddt_ref[...] = jnp.where(_iota((t, 128), 1) < NH, ddt * _sigmoid(dtf + bias_ref[...]), 0.0)
        dalog_ref[...] += _colsum(da * dt) * a_neg
        dskip_ref[...] += _sel_right(jnp.broadcast_to(_colsum(dyv * xs), (8, D)), reduce)[0:1]

    rev = lambda i: nc - 1 - i
    return pl.pallas_call(
        body, name="ssd_bwd", grid=(nc,),
        in_specs=[pl.BlockSpec((t, D), lambda i: (rev(i), 0)),
                  pl.BlockSpec((t, D), lambda i: (rev(i), 0)),
                  pl.BlockSpec((t, 512), lambda i: (rev(i), 0)),
                  pl.BlockSpec((t, 128), lambda i: (rev(i), OFF_DTF // 128)),
                  pl.BlockSpec((1, NSTATE, D), lambda i: (rev(i), 0, 0)),
                  pl.BlockSpec((1, 128), lambda i: (0, 0)),
                  pl.BlockSpec((1, 128), lambda i: (0, 0)),
                  pl.BlockSpec((1, D), lambda i: (0, 0))],
        out_specs=[pl.BlockSpec((t, D), lambda i: (rev(i), 0)),
                   pl.BlockSpec((t, 512), lambda i: (rev(i), 0)),
                   pl.BlockSpec((t, 128), lambda i: (rev(i), 0)),
                   pl.BlockSpec((1, 128), lambda i: (0, 0)),
                   pl.BlockSpec((1, 128), lambda i: (0, 0))],
        out_shape=[jax.ShapeDtypeStruct((s, D), F32), jax.ShapeDtypeStruct((s, 512), F32),
                   jax.ShapeDtypeStruct((s, 128), F32), jax.ShapeDtypeStruct((1, 128), F32),
                   jax.ShapeDtypeStruct((1, 128), F32)],
        scratch_shapes=[pltpu.VMEM((NSTATE, D), F32), pltpu.VMEM((t, D), BF16), pltpu.VMEM((t, D), BF16),
                        pltpu.VMEM((t, D), F32), pltpu.VMEM((t, D), F32), pltpu.VMEM((t, D), F32),
                        pltpu.VMEM((t, 128), F32), pltpu.VMEM((128, t), F32),
                        pltpu.VMEM((t, 128), F32), pltpu.VMEM((128, t), F32)],
        compiler_params=_params(("arbitrary",)),
    )(dy, xs_a, bc_a, p, states, bias128, alog128, dskip_x)


def _gate_lanes(shape):
    lane = _iota(shape, 1)
    return (lane >= NH) & (lane < 2 * NH)


def _cum_fwd(p, bias128, s):
    tr = min(512, s)

    def body(dtf_ref, bias_ref, o_ref, carry):
        @pl.when(pl.program_id(0) == 0)
        def _():
            carry[...] = jnp.zeros(carry.shape, F32)

        lf = jnp.where(_gate_lanes((tr, 128)), _log_sigmoid(dtf_ref[...] + bias_ref[...]), 0.0)
        cum = _sel_left(_tri_lower(tr), lf) + carry[...]
        carry[...] = cum[tr - 1:tr, :]
        o_ref[...] = cum

    return pl.pallas_call(
        body, name="cum_fwd", grid=(s // tr,),
        in_specs=[pl.BlockSpec((tr, 128), lambda i: (i, OFF_DTF // 128)), pl.BlockSpec((1, 128), lambda i: (0, 0))],
        out_specs=pl.BlockSpec((tr, 128), lambda i: (i, 0)),
        out_shape=jax.ShapeDtypeStruct((s, 128), F32),
        scratch_shapes=[pltpu.VMEM((1, 128), F32)],
        compiler_params=_params(("arbitrary",)),
    )(p, bias128)


def _cum_bwd(dcum, ddt_raw, p, bias128, s):
    tr = min(512, s)

    def fn(pos, dcum, ddt, dtf, bias, carry, acc):
        suffix = _sel_left(_tri_upper(tr), dcum) + carry
        dfr = jnp.where(_gate_lanes((tr, 128)), suffix * _sigmoid(-(dtf + bias)), 0.0)
        out = ddt + dfr
        return out, suffix[0:1, :], acc + _colsum(out)

    return _rowk("cum_bwd", fn, s, tr, [(dcum, 128, 0, 0), (ddt_raw, 128, 0, 0), (p, 128, OFF_DTF // 128, 0)],
                 [bias128], [(128, BF16)], [(1, 128), (1, 128)], reverse=True)


ATT_BLOCK = 512
ATT_STRIP = 32


def _head_part(shape, h, dim):
    i = _iota(shape, dim)
    return (i >= h * HD) & (i < (h + 1) * HD)


def _k_augmented(k_blk, cum_blk, j, h):
    tk = k_blk.shape[0]
    lane = _iota((tk, 128), 1)
    col = jnp.sum(jnp.where(lane == NH + 2 * j + h, cum_blk, 0.0), axis=1, keepdims=True)
    c0, c1, c2 = [c.astype(F32) for c in _split3(-col)]
    k_h = k_blk if h == 0 else pltpu.roll(k_blk, HD, 1)
    aug = jnp.where(lane == HD, c0, jnp.where(lane == HD + 1, c1, jnp.where(lane == HD + 2, c2, 0.0)))
    return jnp.where(lane < HD, k_h, aug).astype(BF16)


def _q_augmented_t(q_blk):
    tq = q_blk.shape[0]
    q_t = (q_blk * ATT_SCALE).T.astype(BF16)
    ones = (_iota((HD, tq), 0) < 3).astype(BF16)
    return [jnp.concatenate([q_t[h * HD:(h + 1) * HD], ones], axis=0) for h in range(2)]


def _rows01(r0, r1):
    sub = _iota((8, r0.shape[1]), 0)
    return jnp.where(sub == 0, r0, jnp.where(sub == 1, r1, 0.0))


def _fold8(x, op, cur):
    for g in range(x.shape[0] // 8):
        cur = op(cur, x[8 * g:8 * (g + 1), :])
    return cur


def _attn_fwd(p, cum, s):
    t = min(ATT_BLOCK, s)
    nq = s // t
    r = ATT_STRIP

    def body(q_ref, k_ref, v_ref, c_ref, o_ref, lse_ref, kaug_sc, vt_sc, s0_sc, s1_sc, p0_sc, p1_sc, m_sc, l_sc, acc_sc):
        j, qi = pl.program_id(0), pl.program_id(1)
        s_sc, p_sc = (s0_sc, s1_sc), (p0_sc, p1_sc)

        @pl.when(qi == 0)
        def _():
            for c in range(nq):
                rows = slice(c * t, (c + 1) * t)
                k_blk, vt = k_ref[rows, :], v_ref[rows, :].T
                for h in range(2):
                    kaug_sc[h, rows, :] = _k_augmented(k_blk, c_ref[rows, :], j, h)
                    vt_sc[h, :, rows] = vt[h * HD:(h + 1) * HD].astype(BF16)

        qaug_t = _q_augmented_t(q_ref[...])
        m_sc[...] = jnp.full(m_sc.shape, -1e30, F32)
        l_sc[...] = jnp.zeros(l_sc.shape, F32)
        acc_sc[...] = jnp.zeros(acc_sc.shape, F32)
        top = _iota((128, t), 0) < HD

        def logits(kb, buf):
            kv = pl.ds(pl.multiple_of(kb * t, t), t)
            for h in range(2):
                s_sc[buf][h] = _dot(kaug_sc[h, kv, :], qaug_t[h])

        def softmax(buf, diagonal):
            alphas = []
            for h in range(2):
                cur = jnp.full((8, t), -1e30, F32)
                for i in range(t // r):
                    rows = slice(i * r, (i + 1) * r)
                    x = s_sc[buf][h, rows, :]
                    if diagonal:
                        x = jnp.where(_iota((r, t), 1) >= i * r + _iota((r, t), 0), x, -1e30)
                        s_sc[buf][h, rows, :] = x
                    cur = _fold8(x, jnp.maximum, cur)
                m_prev = m_sc[h, 0:1, :]
                m_new = jnp.maximum(m_prev, jnp.max(cur, axis=0, keepdims=True))
                alpha = jnp.exp(m_prev - m_new)
                m_sc[h, 0:1, :] = m_new
                alphas.append(alpha)
                tot = jnp.zeros((8, t), F32)
                for i in range(t // r):
                    rows = slice(i * r, (i + 1) * r)
                    pr = jnp.exp(s_sc[buf][h, rows, :] - m_new)
                    p_sc[buf][h, rows, :] = pr.astype(BF16)
                    tot = _fold8(pr, jnp.add, tot)
                l_sc[h, 0:1, :] = alpha * l_sc[h, 0:1, :] + jnp.sum(tot, axis=0, keepdims=True)
            return alphas

        def accumulate(kb, buf, alphas):
            kv = pl.ds(pl.multiple_of(kb * t, t), t)
            for h in range(2):
                part = slice(h * HD, (h + 1) * HD)
                acc_sc[part, :] = acc_sc[part, :] * alphas[h] + _dot(vt_sc[h, :, kv], p_sc[buf][h])

        def first_trip():
            logits(0, 1)
            accumulate(qi, 0, softmax(0, True))
            logits(jnp.minimum(1, qi - 1), 0)
            return tuple(softmax(1, False))

        def only_diagonal():
            accumulate(qi, 0, softmax(0, True))
            return (jnp.ones((1, t), F32),) * 2

        def steady(u, alphas_b):
            accumulate(2 * u - 2, 1, alphas_b)
            logits(2 * u, 1)
            accumulate(2 * u - 1, 0, softmax(0, False))
            logits(jnp.minimum(2 * u + 1, qi - 1), 0)
            return tuple(softmax(1, False))

        logits(qi, 0)
        n_blocks = qi + 1
        alphas_b = lax.cond(qi >= 1, first_trip, only_diagonal)
        alphas_b = lax.fori_loop(1, n_blocks // 2, steady, alphas_b)
        last_b = 2 * (n_blocks // 2) - 2

        @pl.when((qi >= 1) & (n_blocks % 2 == 0))
        def _():
            accumulate(last_b, 1, alphas_b)

        @pl.when((qi >= 2) & (n_blocks % 2 == 1))
        def _():
            accumulate(last_b, 1, alphas_b)
            accumulate(qi - 1, 0, softmax(0, False))

        l0, l1 = l_sc[0, 0:1, :], l_sc[1, 0:1, :]
        o_ref[...] = (acc_sc[...] / jnp.where(top, l0, l1)).T
        lse_ref[0] = _rows01(m_sc[0, 0:1, :] + jnp.log(l0), m_sc[1, 0:1, :] + jnp.log(l1))

    return pl.pallas_call(
        body, name="attn_fwd", grid=(NH // 2, nq),
        in_specs=[pl.BlockSpec((t, 128), lambda j, qi: (qi, OFF_Q // 128 + j)),
                  pl.BlockSpec((s, 128), lambda j, qi: (0, OFF_K // 128 + j)),
                  pl.BlockSpec((s, 128), lambda j, qi: (0, OFF_V // 128 + j)),
                  pl.BlockSpec((s, 128), lambda j, qi: (0, 0))],
        out_specs=[pl.BlockSpec((t, 128), lambda j, qi: (qi, j)),
                   pl.BlockSpec((1, 8, t), lambda j, qi: (j, 0, qi))],
        out_shape=[jax.ShapeDtypeStruct((s, D), F32), jax.ShapeDtypeStruct((NH // 2, 8, s), F32)],
        scratch_shapes=[pltpu.VMEM((2, s, 128), BF16), pltpu.VMEM((2, HD, s), BF16), pltpu.VMEM((2, t, t), F32),
                        pltpu.VMEM((2, t, t), F32), pltpu.VMEM((2, t, t), BF16), pltpu.VMEM((2, t, t), BF16),
                        pltpu.VMEM((2, 8, t), F32), pltpu.VMEM((2, 8, t), F32), pltpu.VMEM((128, t), F32)],
        compiler_params=_params(("parallel", "arbitrary")),
    )(p, p, p, cum)


def _attn_bwd(p, cum, o, lse, do, s):
    t = min(ATT_BLOCK, s)
    nq = s // t
    r = ATT_STRIP

    def body(q_ref, k_ref, v_ref, c_ref, o_ref, lse_ref, do_ref, dq_ref, dk_ref, dv_ref, dc_ref, dr_ref,
             qaugt_sc, qh_sc, dot_sc, doh_sc, delta_sc, dqt_sc, dr_sc, kaug_sc, vh_sc, kt_sc,
             s0_sc, s1_sc, dp0_sc, dp1_sc, p0_sc, p1_sc, ds0_sc, ds1_sc, dk_sc, dv_sc, dc_sc):
        j, ki = pl.program_id(0), pl.program_id(1)
        s_sc, dp_sc, p_sc, ds_sc = (s0_sc, s1_sc), (dp0_sc, dp1_sc), (p0_sc, p1_sc), (ds0_sc, ds1_sc)

        @pl.when(ki == 0)
        def _():
            for c in range(nq):
                rows = slice(c * t, (c + 1) * t)
                q_blk, do_blk = q_ref[rows, :], do_ref[rows, :]
                qaugt_sc[0, :, rows], qaugt_sc[1, :, rows] = _q_augmented_t(q_blk)
                dot_sc[:, rows] = do_blk.T.astype(BF16)
                prod_t = (do_blk * o_ref[rows, :]).T
                delta_sc[:, rows] = _rows01(jnp.sum(prod_t[0:HD], axis=0, keepdims=True),
                                            jnp.sum(prod_t[HD:], axis=0, keepdims=True))
                for h in range(2):
                    head = _head_part((t, 128), h, 1)
                    qh_sc[h, rows, :] = jnp.where(head, q_blk * ATT_SCALE, 0.0).astype(BF16)
                    doh_sc[h, rows, :] = jnp.where(head, do_blk, 0.0).astype(BF16)
            dqt_sc[...] = jnp.zeros(dqt_sc.shape, F32)
            dr_sc[...] = jnp.zeros(dr_sc.shape, F32)

        k_blk, v_blk = k_ref[...], v_ref[...]
        kt = k_blk.T
        for h in range(2):
            kaug_sc[h] = _k_augmented(k_blk, c_ref[...], j, h)
            vh_sc[h] = jnp.where(_head_part((t, 128), h, 1), v_blk, 0.0).astype(BF16)
            kt_sc[h] = kt[h * HD:(h + 1) * HD].astype(BF16)
        dk_sc[...] = jnp.zeros(dk_sc.shape, F32)
        dv_sc[...] = jnp.zeros(dv_sc.shape, F32)
        dc_sc[...] = jnp.zeros(dc_sc.shape, F32)

        def inputs(qb, buf):
            qs = pl.ds(pl.multiple_of(qb * t, t), t)
            for h in range(2):
                s_sc[buf][h] = _dot(kaug_sc[h], qaugt_sc[h, :, qs])
                dp_sc[buf][h] = _dot(vh_sc[h], dot_sc[:, qs])

        def elementwise(qb, buf, diagonal):
            qs = pl.ds(pl.multiple_of(qb * t, t), t)
            for h in range(2):
                lse_row, delta_row = lse_ref[0, h:h + 1, qs], delta_sc[h:h + 1, qs]
                tot = jnp.zeros((8, t), F32)
                for i in range(t // r):
                    rows = slice(i * r, (i + 1) * r)
                    x = s_sc[buf][h, rows, :]
                    if diagonal:
                        x = jnp.where(_iota((r, t), 1) >= i * r + _iota((r, t), 0), x, -1e30)
                    pr = jnp.exp(x - lse_row)
                    ds = pr * (dp_sc[buf][h, rows, :] - delta_row)
                    p_sc[buf][h, rows, :] = pr.astype(BF16)
                    ds_sc[buf][h, rows, :] = ds.astype(BF16)
                    dc_sc[h, rows, :] += sum(ds[:, 128 * g:128 * (g + 1)] for g in range(t // 128))
                    tot = _fold8(ds, jnp.add, tot)
                dr_sc[h, :, qs] += tot

        def outputs(qb, buf):
            qs = pl.ds(pl.multiple_of(qb * t, t), t)
            dv_sc[...] += _dot(p_sc[buf][0], doh_sc[0, qs, :]) + _dot(p_sc[buf][1], doh_sc[1, qs, :])
            dk_sc[...] += _dot(ds_sc[buf][0], qh_sc[0, qs, :]) + _dot(ds_sc[buf][1], qh_sc[1, qs, :])
            for h in range(2):
                dqt_sc[h * HD:(h + 1) * HD, qs] += _dot(kt_sc[h], ds_sc[buf][h])

        def pair(a, b, a_diagonal):
            inputs(a, 0)
            inputs(b, 1)
            elementwise(a, 0, a_diagonal)
            outputs(a, 0)
            elementwise(b, 1, False)
            outputs(b, 1)

        def later(u, carry):
            pair(ki + 1 + 2 * u, ki + 2 + 2 * u, False)
            return carry

        n_later = nq - 1 - ki
        lax.fori_loop(0, n_later // 2, later, 0)

        @pl.when(n_later % 2 == 1)
        def _():
            pair(ki, nq - 1, True)

        @pl.when(n_later % 2 == 0)
        def _():
            inputs(ki, 0)
            elementwise(ki, 0, True)
            outputs(ki, 0)

        dk_ref[...] = dk_sc[...].astype(BF16)
        dv_ref[...] = dv_sc[...].astype(BF16)
        lane = _iota((t, 128), 1)
        cols = jnp.where(lane == 0, jnp.sum(dc_sc[0], axis=1, keepdims=True),
                         jnp.where(lane == 1, jnp.sum(dc_sc[1], axis=1, keepdims=True), 0.0))
        dc_ref[0] = cols.T[0:8, :]

        @pl.when(ki == nq - 1)
        def _():
            for c in range(nq):
                rows = slice(c * t, (c + 1) * t)
                dq_ref[rows, :] = dqt_sc[:, rows].T * ATT_SCALE
            dr_ref[0] = _rows01(jnp.sum(dr_sc[0], axis=0, keepdims=True), jnp.sum(dr_sc[1], axis=0, keepdims=True))

    whole = lambda off: pl.BlockSpec((s, 128), functools.partial(lambda j, ki, off: (0, off + j), off=off))
    return pl.pallas_call(
        body, name="attn_bwd", grid=(NH // 2, nq),
        in_specs=[whole(OFF_Q // 128),
                  pl.BlockSpec((t, 128), lambda j, ki: (ki, OFF_K // 128 + j)),
                  pl.BlockSpec((t, 128), lambda j, ki: (ki, OFF_V // 128 + j)),
                  pl.BlockSpec((t, 128), lambda j, ki: (ki, 0)),
                  whole(0),
                  pl.BlockSpec((1, 8, s), lambda j, ki: (j, 0, 0)),
                  whole(0)],
        out_specs=[whole(0),
                   pl.BlockSpec((t, 128), lambda j, ki: (ki, j)),
                   pl.BlockSpec((t, 128), lambda j, ki: (ki, j)),
                   pl.BlockSpec((1, 8, t), lambda j, ki: (j, 0, ki)),
                   pl.BlockSpec((1, 8, s), lambda j, ki: (j, 0, 0))],
        out_shape=[jax.ShapeDtypeStruct((s, D), F32), jax.ShapeDtypeStruct((s, D), BF16), jax.ShapeDtypeStruct((s, D), BF16),
                   jax.ShapeDtypeStruct((NH // 2, 8, s), F32), jax.ShapeDtypeStruct((NH // 2, 8, s), F32)],
        scratch_shapes=[pltpu.VMEM((2, 128, s), BF16), pltpu.VMEM((2, s, 128), BF16), pltpu.VMEM((128, s), BF16),
                        pltpu.VMEM((2, s, 128), BF16), pltpu.VMEM((8, s), F32), pltpu.VMEM((128, s), F32),
                        pltpu.VMEM((2, 8, s), F32), pltpu.VMEM((2, t, 128), BF16), pltpu.VMEM((2, t, 128), BF16),
                        pltpu.VMEM((2, HD, t), BF16)]
        + [pltpu.VMEM((2, t, t), F32)] * 4 + [pltpu.VMEM((2, t, t), BF16)] * 4
        + [pltpu.VMEM((t, 128), F32), pltpu.VMEM((t, 128), F32), pltpu.VMEM((2, t, 128), F32)],
        compiler_params=_params(("parallel", "arbitrary")),
    )(p, p, p, cum, o, lse, do)


def _ln_stats(u):
    mu = _mean(u)
    d = u - mu
    rstd = lax.rsqrt(_mean(d * d) + EPS)
    return d * rstd, rstd


def _ln_bwd(dx, xh, rstd, gam):
    dxh = dx * gam
    return rstd * (dxh - _mean(dxh) - xh * _mean(dxh * xh))


def _rms_bwd(d, xn, r, w):
    t = d * w
    return r * (t - xn * _mean(t * xn)), _colsum(d * xn)


def _mix_norm(y, p, att, w_ssm, w_att, s):
    def fn(pos, y, z, att, w1, w2):
        g = y * _silu(z)
        n1 = g * lax.rsqrt(_mean(g * g) + EPS) * w1
        n2 = att * lax.rsqrt(_mean(att * att) + EPS) * w2
        return (jnp.concatenate([n1, n2], axis=1),)

    return _rowk("mix_norm", fn, s, 512, [(y, D, 0, 0), (p, D, OFF_Z // D, 0), (att, D, 0, 0)],
                 [w_ssm, w_att], [(2 * D, BF16)], [])[0]


def _mix_norm_bwd(dmix, y, p, att, w_ssm, w_att, s):
    def fn(pos, dmix, y, z, att, w1, w2, a1, a2):
        sz = _silu(z)
        g = y * sz
        r1 = lax.rsqrt(_mean(g * g) + EPS)
        dg, dw1 = _rms_bwd(dmix[:, :D], g * r1, r1, w1)
        r2 = lax.rsqrt(_mean(att * att) + EPS)
        datt, dw2 = _rms_bwd(dmix[:, D:], att * r2, r2, w2)
        return dg * sz, dg * y * _dsilu(z), datt, a1 + dw1, a2 + dw2

    return _rowk("mix_norm_bwd", fn, s, 256, [(dmix, 2 * D, 0, 0), (y, D, 0, 0), (p, D, OFF_Z // D, 0), (att, D, 0, 0)],
                 [w_ssm, w_att], [(D, F32), (D, BF16), (D, F32)], [(1, D), (1, D)])


def _ln1(x0, y, g1, gam, bet, sc2, sh2, s):
    def fn(pos, x0, y, g1, gam, bet, sc2, sh2):
        xh, _ = _ln_stats(ALPHA * x0 + (1.0 + g1) * y)
        x1 = xh * gam + bet
        return x1, _modulate(x1, sc2, sh2)

    return _rowk("ln1", fn, s, 512, [(x0, D, 0, 0), (y, D, 0, 0)], [g1, gam, bet, sc2, sh2], [(D, F32), (D, BF16)], [])


def _ln2_loss(x1, ff, tgt, g2, gam, bet, s):
    def fn(pos, x1, ff, tgt, g2, gam, bet, a_loss, a_dgam, a_dbet, a_dg2):
        xh, rstd = _ln_stats(ALPHA * x1 + (1.0 + g2) * ff)
        err = xh * gam + bet - tgt
        dx2 = err * (1.0 / D)
        du = _ln_bwd(dx2, xh, rstd, gam)
        return (du, du * (1.0 + g2), a_loss + _colsum(err * err), a_dgam + _colsum(dx2 * xh),
                a_dbet + _colsum(dx2), a_dg2 + _colsum(du * ff))

    return _rowk("ln2_loss", fn, s, 512, [(x1, D, 0, 0), (ff, D, 0, 0), (tgt, D, 0, 0)], [g2, gam, bet],
                 [(D, F32), (D, BF16)], [(1, D)] * 4)


def _ln1_bwd(dh2, du2, x0, y, g1, gam, bet, sc2, s):
    def fn(pos, dh2, du2, x0, y, g1, gam, bet, sc2, a_sc, a_sh, a_gam, a_bet, a_g1):
        xh, rstd = _ln_stats(ALPHA * x0 + (1.0 + g1) * y)
        x1 = xh * gam + bet
        dx1 = ALPHA * du2 + dh2 * (1.0 + sc2)
        du1 = _ln_bwd(dx1, xh, rstd, gam)
        return (du1, du1 * (1.0 + g1), a_sc + _colsum(dh2 * x1), a_sh + _colsum(dh2), a_gam + _colsum(dx1 * xh),
                a_bet + _colsum(dx1), a_g1 + _colsum(du1 * y))

    return _rowk("ln1_bwd", fn, s, 512, [(dh2, D, 0, 0), (du2, D, 0, 0), (x0, D, 0, 0), (y, D, 0, 0)],
                 [g1, gam, bet, sc2], [(D, F32), (D, BF16)], [(1, D)] * 5)


def _input_grad(dh1, du1, x0, sc1, s):
    def fn(pos, dh1, du1, x0, sc1, a_sc, a_sh):
        return ALPHA * du1 + dh1 * (1.0 + sc1), a_sc + _colsum(dh1 * x0), a_sh + _colsum(dh1)

    return _rowk("input_grad", fn, s, 512, [(dh1, D, 0, 0), (du1, D, 0, 0), (x0, D, 0, 0)], [sc1],
                 [(D, F32)], [(1, D)] * 2)


def _adamw_math(w, grad, m, v):
    m_new = ADAM_B1 * m + (1.0 - ADAM_B1) * grad
    v_new = ADAM_B2 * v + (1.0 - ADAM_B2) * (grad * grad)
    m_hat = m_new / (1.0 - ADAM_B1 ** ADAM_STEP)
    v_hat = v_new / (1.0 - ADAM_B2 ** ADAM_STEP)
    return -ADAM_LR * (m_hat / (jnp.sqrt(v_hat) + ADAM_EPS) + ADAM_WD * w), m_new, v_new


def _small_update(small_all, layout, w, m, v):
    names = [n for n, _, _ in layout]

    def body(*refs):
        all_ref = refs[0]
        w_refs, m_refs, v_refs = [refs[1 + k * len(names):1 + (k + 1) * len(names)] for k in range(3)]
        sum_ref = refs[1 + 3 * len(names)]
        outs = refs[2 + 3 * len(names):]
        total = all_ref[0]
        for k in range(1, N_DEV):
            total = total + all_ref[k]
        sum_ref[...] = total
        for i, (_, off, size) in enumerate(layout):
            grad = total[:, off:off + size]
            delta, m_new, v_new = _adamw_math(w_refs[i][...], grad, m_refs[i][...], v_refs[i][...])
            for o, val in zip(outs[4 * i:4 * i + 4], (grad, delta, m_new, v_new)):
                o[...] = val

    res = pl.pallas_call(
        body, name="small_update",
        out_shape=[jax.ShapeDtypeStruct(small_all.shape[1:], F32)]
        + [jax.ShapeDtypeStruct(w[n].shape, F32) for n in names for _ in range(4)],
        compiler_params=_params(None),
    )(small_all, *[w[n] for n in names], *[m[n] for n in names], *[v[n] for n in names])
    return res[0], {n: res[1 + 4 * i:5 + 4 * i] for i, n in enumerate(names)}


def _adamw(name, w, g, m, v, *, tr, slots, by_columns=False):
    r, c = w.shape

    def body(w_ref, g_ref, m_ref, v_ref, g_out, d_out, m_out, v_out):
        if slots:
            grad = g_ref[0].astype(F32)
            for k in range(1, N_DEV):
                grad = grad + g_ref[k].astype(F32)
        else:
            grad = g_ref[...]
        g_out[...] = grad
        d_out[...], m_out[...], v_out[...] = _adamw_math(w_ref[...], grad, m_ref[...], v_ref[...])

    if by_columns:
        tile = pl.BlockSpec((r, tr), lambda i: (0, i))
        g_spec = pl.BlockSpec((N_DEV, r, tr), lambda i: (0, 0, i)) if slots else tile
    else:
        tile = pl.BlockSpec((tr, c), lambda i: (i, 0))
        g_spec = pl.BlockSpec((N_DEV, tr, c), lambda i: (0, i, 0)) if slots else tile
    return pl.pallas_call(
        body, name=name, grid=((c if by_columns else r) // tr,),
        in_specs=[tile, g_spec, tile, tile], out_specs=[tile] * 4,
        out_shape=[jax.ShapeDtypeStruct((r, c), F32)] * 4,
        compiler_params=_params(("parallel",)),
    )(w, g, m, v)


def _dot_f32(a, b, dims=NN):
    a0, a1, a2 = _split3(a)
    b0, b1, b2 = _split3(b)
    acc = _dot(a0, b0, dims)
    for x, y in ((a0, b1), (a1, b0), (a1, b1), (a0, b2), (a2, b0)):
        acc = acc + _dot(x, y, dims)
    return acc


def _ada_mod(c_all, w_shard, b_shard):
    def body(c_ref, w_ref, b_ref, o_ref):
        act = _silu(c_ref[...])
        act16 = jnp.concatenate([act, jnp.zeros_like(act)], axis=0)
        o_ref[...] = _dot_f32(act16, w_ref[...])[0:N_DEV] + b_ref[...]

    return pl.pallas_call(
        body, name="ada_mod", out_shape=jax.ShapeDtypeStruct((N_DEV, w_shard.shape[1]), F32),
        compiler_params=_params(None),
    )(c_all, w_shard, b_shard)


def _ada_grad(c_all, dmod_cols):
    def body(c_ref, dc_ref, gw_ref):
        act = _silu(c_ref[...])
        act16 = jnp.concatenate([act, jnp.zeros_like(act)], axis=0)
        dm = dc_ref[...]
        dm16 = jnp.concatenate([dm, jnp.zeros_like(dm)], axis=0)
        gw_ref[...] = _dot_f32(act16, dm16, TN)

    return pl.pallas_call(
        body, name="ada_grad", out_shape=jax.ShapeDtypeStruct((D, dmod_cols.shape[1]), F32),
        compiler_params=_params(None),
    )(c_all, dmod_cols)


def _exchange(name, xs, scatter):
    n = len(xs)
    n_peer = N_DEV - 1

    def body(*refs):
        x_refs, o_refs = refs[:n], refs[n:2 * n]
        send_sems, recv_sems, local_sems = refs[2 * n:]
        mx, my, mc = lax.axis_index("x"), lax.axis_index("y"), lax.axis_index("c")
        me = 4 * mx + 2 * my + mc

        def src(a, slot):
            return x_refs[a].at[slot] if scatter else x_refs[a]

        own = [pltpu.make_async_copy(src(a, me), o_refs[a].at[me], local_sems.at[a]) for a in range(n)]
        for cp in own:
            cp.start()
        sends = []
        for d in range(1, N_DEV):
            px = 1 - mx if d & 4 else mx
            py = 1 - my if d & 2 else my
            pc = 1 - mc if d & 1 else mc
            peer = 4 * px + 2 * py + pc
            for a in range(n):
                def copy(src_slot, dst_slot, a=a, d=d, to=(px, py, pc)):
                    return pltpu.make_async_remote_copy(
                        src_ref=src(a, src_slot), dst_ref=o_refs[a].at[dst_slot],
                        send_sem=send_sems.at[a * n_peer + d - 1], recv_sem=recv_sems.at[a * n_peer + d - 1],
                        device_id=to, device_id_type=pl.DeviceIdType.MESH)

                out = copy(peer, me)
                out.start()
                sends.append((out, copy(me, peer)))
        for _, arrival in sends:
            arrival.wait_recv()
        for out, _ in sends:
            out.wait_send()
        for cp in own:
            cp.wait()

    shapes = [tuple(x.shape[1:] if scatter else x.shape) for x in xs]
    return pl.pallas_call(
        body, name=name,
        in_specs=[pl.BlockSpec(memory_space=pl.ANY)] * n, out_specs=[pl.BlockSpec(memory_space=pl.ANY)] * n,
        out_shape=[jax.ShapeDtypeStruct((N_DEV,) + sh, x.dtype) for sh, x in zip(shapes, xs)],
        scratch_shapes=[pltpu.SemaphoreType.DMA((n * n_peer,)), pltpu.SemaphoreType.DMA((n * n_peer,)),
                        pltpu.SemaphoreType.DMA((n,))],
        compiler_params=pltpu.CompilerParams(has_side_effects=True),
    )(*xs)


def _gather_two_level(name, x):
    def body(x_ref, o_ref, send_sems, recv_sems, local_sem):
        mx, my, mc = lax.axis_index("x"), lax.axis_index("y"), lax.axis_index("c")
        me, sibling = (mx, my, mc), (mx, my, 1 - mc)
        chips = [(1 - mx, my), (mx, 1 - my), (1 - mx, 1 - my)]

        def slot(px, py, pc):
            return o_ref.at[4 * px + 2 * py + pc]

        def copy(k, block, to, src=None):
            return pltpu.make_async_remote_copy(
                src_ref=slot(*block) if src is None else src, dst_ref=slot(*block),
                send_sem=send_sems.at[k], recv_sem=recv_sems.at[k], device_id=to, device_id_type=pl.DeviceIdType.MESH)

        mine = pltpu.make_async_copy(x_ref, slot(*me), local_sem)
        mine.start()
        first = [copy(0, me, sibling, src=x_ref)] + [copy(1 + i, me, (*chip, mc), src=x_ref) for i, chip in enumerate(chips)]
        for cp in first:
            cp.start()
        passed = [copy(4 + i, (*chip, mc), sibling) for i, chip in enumerate(chips)]
        for i, chip in enumerate(chips):
            copy(1 + i, (*chip, mc), me).wait_recv()
            passed[i].start()
        copy(0, sibling, me).wait_recv()
        for i, chip in enumerate(chips):
            copy(4 + i, (*chip, 1 - mc), me).wait_recv()
        for cp in first + passed:
            cp.wait_send()
        mine.wait()

    return pl.pallas_call(
        body, name=name,
        in_specs=[pl.BlockSpec(memory_space=pl.ANY)], out_specs=pl.BlockSpec(memory_space=pl.ANY),
        out_shape=jax.ShapeDtypeStruct((N_DEV,) + tuple(x.shape), x.dtype),
        scratch_shapes=[pltpu.SemaphoreType.DMA((7,)), pltpu.SemaphoreType.DMA((7,)), pltpu.SemaphoreType.DMA(())],
        compiler_params=pltpu.CompilerParams(has_side_effects=True),
    )(x)


def _after(x, zero):
    return x if zero is None else x + zero.reshape(-1)[0].astype(x.dtype)


def _exchange_copies(x_refs, land_refs, send_sems, recv_sems, scatter):
    n = len(x_refs)
    n_peer = N_DEV - 1
    mx, my, mc = lax.axis_index("x"), lax.axis_index("y"), lax.axis_index("c")
    me = 4 * mx + 2 * my + mc
    pairs = []
    for d in range(1, N_DEV):
        px = 1 - mx if d & 4 else mx
        py = 1 - my if d & 2 else my
        pc = 1 - mc if d & 1 else mc
        peer = 4 * px + 2 * py + pc
        for a in range(n):
            def copy(src_slot, dst_slot, a=a, d=d, to=(px, py, pc)):
                return pltpu.make_async_remote_copy(
                    src_ref=x_refs[a].at[src_slot] if scatter else x_refs[a], dst_ref=land_refs[a].at[dst_slot],
                    send_sem=send_sems.at[a * n_peer + d - 1], recv_sem=recv_sems.at[a * n_peer + d - 1],
                    device_id=to, device_id_type=pl.DeviceIdType.MESH)

            pairs.append((copy(peer, me), copy(me, peer)))
    return me, pairs


def _exchange_async(name, xs, scatter, collective_id):
    n = len(xs)
    shapes = [tuple(x.shape[1:] if scatter else x.shape) for x in xs]
    x_refs = [jax.new_ref(x, memory_space=pltpu.MemorySpace.HBM) for x in xs]
    land_refs = [jax.empty_ref(jax.ShapeDtypeStruct((N_DEV,) + sh, x.dtype), memory_space=pltpu.MemorySpace.HBM)
                 for sh, x in zip(shapes, xs)]

    @pl.kernel(mesh=plsc.ScalarSubcoreMesh(axis_name="sequencer", num_cores=1), name=name,
               scratch_types=(pltpu.SemaphoreType.DMA((n * (N_DEV - 1),)), pltpu.SemaphoreType.DMA((n * (N_DEV - 1),)),
                              pltpu.SemaphoreType.DMA((n,))),
               compiler_params=pltpu.CompilerParams(collective_id=collective_id))
    def launch(send_sems, recv_sems, own_sems):
        barrier = pltpu.get_barrier_semaphore()
        mx, my, mc = lax.axis_index("x"), lax.axis_index("y"), lax.axis_index("c")
        for d in range(1, N_DEV):
            peer = (1 - mx if d & 4 else mx, 1 - my if d & 2 else my, 1 - mc if d & 1 else mc)
            pl.semaphore_signal(barrier, inc=1, device_id=peer, device_id_type=pl.DeviceIdType.MESH)
        pl.semaphore_wait(barrier, N_DEV - 1)
        me, pairs = _exchange_copies(x_refs, land_refs, send_sems, recv_sems, scatter)
        own = [pltpu.make_async_copy(x_refs[a].at[me] if scatter else x_refs[a], land_refs[a].at[me], own_sems.at[a])
               for a in range(n)]
        for cp in own:
            cp.start()
        for out, _ in pairs:
            out.start()
        for out, arrival in pairs:
            arrival.wait_recv()
            out.wait_send()
        for cp in own:
            cp.wait()

    launch()
    return lambda: [r[...] for r in land_refs]


def _relu2(a):
    r = jnp.maximum(a, 0.0)
    return r * r


def _relu2_grad(acc, r):
    return acc * (2.0 * jnp.sqrt(r.astype(F32)))


def _local_step(x0, tgt, mod, wcat_t, late_weights, send_grads, conv_w, conv_b, dt_bias, a_log, d_skip, ssm_norm_w, f_bias,
                attn_norm_w, ln1_g, ln1_b, ln2_g, ln2_b):
    ff_w = DFF // N_DEV
    s = x0.shape[0]
    tm = min(1024, s)
    ts = min(2048, s)
    sh1, sc1, g1, sh2, sc2, g2 = [mod[:, i * D:(i + 1) * D] for i in range(6)]
    zero = jnp.zeros((1, 128 - 2 * NH), F32)
    bias128 = jnp.concatenate([dt_bias, f_bias, zero], axis=1)
    alog128 = jnp.concatenate([a_log, jnp.zeros((1, 128 - NH), F32)], axis=1)
    dskip_x = jnp.repeat(d_skip, HD, axis=1)
    w_xs, w_bc, b_xs, b_bc = conv_w[:, :D], conv_w[:, D:], conv_b[:, :D], conv_b[:, D:]

    h1, = _rowk("modulate", lambda pos, x, sc, sh: (_modulate(x, sc, sh),), s, 512, [(x0, D, 0, 0)], [sc1, sh1], [(D, BF16)], [])
    p = _mm_nt("in_proj", [(h1, D, 0)], [(wcat_t, D, 0)], n=PCOLS, tm=tm, tn=1152, out_dtype=F32)
    xs_a, bc_a = _conv_fwd(p, w_xs, b_xs, w_bc, b_bc, s)
    y_ssd, states = _ssd_fwd(xs_a, bc_a, p, bias128, alog128, dskip_x, s)
    cum = _cum_fwd(p, bias128, s)
    att, lse = _attn_fwd(p, cum, s)
    wout, w1s, w2 = late_weights()
    ymix = _mix_norm(y_ssd, p, att, ssm_norm_w, attn_norm_w, s)
    y = _mm_nn("out_proj", ymix, wout, tm=tm, tn=1024, tk=2 * D, out_dtype=F32)
    x1, h2 = _ln1(x0, y, g1, ln1_g, ln1_b, sc2, sh2, s)
    tall = min(2048, s)
    r = _mm_nn("ff_in", h2, w1s, tm=tall, tn=2 * ff_w, tk=D, out_dtype=BF16, epi=_relu2)
    ff = _mm_nn("ff_out", r, w2, tm=tall, tn=1024, tk=1024, out_dtype=F32)
    du2, dff, sq_err, d_ln2_g, d_ln2_b, d_g2 = _ln2_loss(x1, ff, tgt, g2, ln2_g, ln2_b, s)

    da1 = _mm_nt("d_ff_hidden", [(dff, D, 0)], [(w2, D, 0)], n=DFF, tm=tm, tn=1024, out_dtype=BF16, epi=_relu2_grad,
                 epi_aux=(r,))
    d_w2 = _mm_tn("d_w_ff_out", r, dff, tm=1024, tn=1024, ts=ts)
    d_w1s = _mm_tn("d_w_ff_in", h2, da1, tm=1024, tn=2 * ff_w, ts=ts, col_shards=True)
    dh2 = _mm_nt("d_ff_input", [(da1, ff_w, k) for k in range(N_DEV)], [(w1s, ff_w, k) for k in range(N_DEV)], n=D,
                 tm=min(512, s), tn=1024, out_dtype=F32)
    du1, dy, d_sc2, d_sh2, d_ln1_g, d_ln1_b, d_g1 = _ln1_bwd(dh2, du2, x0, y, g1, ln1_g, ln1_b, sc2, s)

    dmix = _mm_nt("d_mix", [(dy, D, 0)], [(wout, D, 0)], n=2 * D, tm=tm, tn=1024, out_dtype=F32)
    d_wout = _mm_tn("d_w_out", ymix, dy, tm=1024, tn=1024, ts=ts)
    sent = send_grads("late", [d_w1s, d_w2.reshape(N_DEV, -1, D), d_wout.reshape(N_DEV, -1, D)])
    dy_ssd, dz, datt, d_ssm_w, d_attn_w = _mix_norm_bwd(dmix, y_ssd, p, att, _after(ssm_norm_w, sent), attn_norm_w, s)
    dq, dk, dv, dcs, drs = _attn_bwd(p, cum, att, lse, datt, s)
    dxs_a, dbc_a, ddt_raw, d_alog, d_dskip = _ssd_bwd(dy_ssd, xs_a, bc_a, p, states, bias128, alog128, dskip_x, s)
    dcum = jnp.pad((drs - dcs)[:, :2, :].reshape(NH, s).T, ((0, 0), (NH, 128 - 2 * NH)))
    ddtf, _, d_bias = _cum_bwd(dcum, ddt_raw, p, bias128, s)
    dxs, dbc, d_wc_xs, d_bc_xs, d_wc_bc, d_bc_bc = _conv_bwd(dxs_a, dbc_a, p, w_xs, b_xs, w_bc, b_bc, s)

    segs = [(dz, OFF_Z, D), (dxs, OFF_XS, D), (dq, OFF_Q, D), (dk, OFF_K, D), (dv, OFF_V, D), (dbc, OFF_BC, 512),
            (ddtf, OFF_DTF, 128)]
    d_z, d_xs, d_q, d_k, d_v, d_bcw, d_dtf = [
        _mm_tn("d_w_in_%d" % i, a, h1, tm=min(w, 1024), tn=1024, ts=ts)
        for i, (a, _, w) in enumerate(segs)]
    d_w_in_t = dict(z=d_z, xs=d_xs, bc=d_bcw, dt=d_dtf[:NH], q=d_q, k=d_k, v=d_v, f=d_dtf[NH:2 * NH])
    sent = send_grads("in", [_shard_w_in_grad_t(d_w_in_t)])
    segs[-1] = (_after(ddtf, sent), OFF_DTF, 128)
    dh1 = _mm_nt("d_h1", [(a, w, 0) for a, _, w in segs], [(wcat_t, w, off // w) for _, off, w in segs], n=D,
                 tm=min(512, s), tn=1024, out_dtype=F32, b_rows=True)
    grad_x, d_sc1, d_sh1 = _input_grad(dh1, du1, x0, sc1, s)

    return dict(
        loss=(0.5 / D) * jnp.sum(sq_err), grad_x=grad_x,
        d_mod=jnp.concatenate([d_sh1, d_sc1, d_g1, d_sh2, d_sc2, d_g2], axis=1),
        d_conv_w=jnp.concatenate([d_wc_xs[:4], d_wc_bc[:4]], axis=1), d_conv_b=jnp.concatenate([d_bc_xs, d_bc_bc], axis=1),
        d_ssm_norm_w=d_ssm_w, d_attn_norm_w=d_attn_w, d_ln1_g=d_ln1_g, d_ln1_b=d_ln1_b, d_ln2_g=d_ln2_g, d_ln2_b=d_ln2_b,
        d_gate_bias=d_bias, d_a_log=d_alog, d_d_skip=d_dskip)


W_IN_SEGS = [('z', W_Z, D), ('xs', W_XS, D), ('bc', W_BC, 512), ('dt', W_DT, NH), ('q', W_Q, D), ('k', W_K, D),
             ('v', W_V, D), ('f', W_F, NH)]
SHARD_W = IN_COLS // N_DEV


def _pack_w_in_t(w_in_t):
    seg = {n: w_in_t[off:off + w] for n, off, w in W_IN_SEGS}
    return jnp.concatenate([seg['z'], seg['xs'], seg['q'], seg['k'], seg['v'], seg['bc'], seg['dt'], seg['f'],
                            jnp.zeros((128 - 2 * NH, D), w_in_t.dtype)], axis=0)


def _shard_w_in_grad_t(d_w_in_t):
    blocks = []
    for dev in range(N_DEV):
        lo, hi = dev * SHARD_W, (dev + 1) * SHARD_W
        pieces = [d_w_in_t[n][max(lo, off) - off:min(hi, off + w) - off] for n, off, w in W_IN_SEGS
                  if max(lo, off) < min(hi, off + w)]
        blocks.append(jnp.concatenate(pieces, axis=0))
    return jnp.stack(blocks, axis=0)


WEIGHTS = ['w_ada', 'b_ada', 'w_in', 'conv_w', 'conv_b', 'dt_bias', 'a_log', 'd_skip', 'ssm_norm_w', 'f_bias',
           'attn_norm_w', 'w_out', 'ln1_g', 'ln1_b', 'w_ff_in', 'w_ff_out', 'ln2_g', 'ln2_b']
BIG = ['w_in', 'w_out', 'w_ff_in', 'w_ff_out']
SMALL_LAYOUT = [('b_ada', 0, 6 * D), ('conv_b', 12288, 1536), ('ssm_norm_w', 13824, D), ('attn_norm_w', 14848, D),
                ('ln1_g', 15872, D), ('ln1_b', 16896, D), ('ln2_g', 17920, D), ('ln2_b', 18944, D),
                ('dt_bias', 19968, NH), ('f_bias', 19968 + NH, NH), ('a_log', 20096, NH), ('d_skip', 20224, NH)]
SMALL_LOSS_LANE = 20352


def _pad_lanes(v, n=128):
    return jnp.pad(v, ((0, 0), (0, n - v.shape[1])))


def kernel(x, c, w_ada, b_ada, w_in, conv_w, conv_b, dt_bias, a_log, d_skip, ssm_norm_w, f_bias, attn_norm_w, w_out, ln1_g, ln1_b, w_ff_in, w_ff_out, ln2_g, ln2_b, loss_target, m_w_ada, m_b_ada, m_w_in, m_conv_w, m_conv_b, m_dt_bias, m_a_log, m_d_skip, m_ssm_norm_w, m_f_bias, m_attn_norm_w, m_w_out, m_ln1_g, m_ln1_b, m_w_ff_in, m_w_ff_out, m_ln2_g, m_ln2_b, v_w_ada, v_b_ada, v_w_in, v_conv_w, v_conv_b, v_dt_bias, v_a_log, v_d_skip, v_ssm_norm_w, v_f_bias, v_attn_norm_w, v_w_out, v_ln1_g, v_ln1_b, v_w_ff_in, v_w_ff_out, v_ln2_g, v_ln2_b):
    args = dict(locals())
    w = {n: args[n] for n in WEIGHTS}
    m = {n: args['m_' + n] for n in WEIGHTS}
    v = {n: args['v_' + n] for n in WEIGHTS}
    me = 4 * lax.axis_index("x") + 2 * lax.axis_index("y") + lax.axis_index("c")
    ada_cols = 6 * D // N_DEV
    conv_cols = conv_w.shape[2]

    c_all, conv_all = _exchange("gather_cond", [c, conv_w[0]], False)
    c_all = c_all.reshape(N_DEV, D)
    conv_w_full = conv_all.transpose(1, 0, 2).reshape(4, N_DEV * conv_cols)
    b_shard = lax.dynamic_slice(b_ada, (0, me * ada_cols), (1, ada_cols))
    mod_all, = _exchange("gather_mod", [_ada_mod(c_all, w_ada[0], b_shard)], False)
    mod = lax.dynamic_index_in_dim(mod_all, me, axis=1, keepdims=False).reshape(1, 6 * D)

    w_in_t = _after(jnp.swapaxes(w_in[0], 0, 1).astype(BF16), mod * 0)
    win_s = _gather_two_level("gather_w_in", w_in_t)
    first_done = win_s[0, 0:1, 0:1] * 0
    rest = _exchange_async("gather_rest", [_after(w[n][0].astype(BF16), first_done) for n in BIG[1:]], False, 1)

    def late_weights():
        wout_s, w1s, w2_s = rest()
        return wout_s.reshape(2 * D, D), w1s, w2_s.reshape(DFF, D)

    sends = {}

    def send_grads(tag, blocks):
        sends[tag] = _exchange_async("scatter_" + tag, blocks, True, {'late': 2, 'in': 3}[tag])
        return sum(b.reshape(-1)[0].astype(F32) * 0 for b in blocks)

    out = _local_step(x[0], loss_target[0], mod, _pack_w_in_t(win_s.reshape(IN_COLS, D)), late_weights, send_grads,
                      conv_w_full, conv_b, dt_bias, a_log, d_skip, ssm_norm_w, f_bias, attn_norm_w, ln1_g, ln1_b, ln2_g, ln2_b)

    small = jnp.concatenate(
        [out['d_mod'], out['d_conv_w'].reshape(1, -1), out['d_conv_b'], out['d_ssm_norm_w'], out['d_attn_norm_w'],
         out['d_ln1_g'], out['d_ln1_b'], out['d_ln2_g'], out['d_ln2_b'], out['d_gate_bias'], out['d_a_log'],
         out['d_d_skip'], _pad_lanes(out['loss'].reshape(1, 1))], axis=1)
    small_landed = _exchange_async("gather_small", [small], False, 4)
    (g_ff_in, g_ff_out, g_out), (g_in,) = sends['late'](), sends['in']()
    g_parts = dict(w_ff_in=g_ff_in, w_ff_out=g_ff_out, w_out=g_out, w_in=g_in)
    big = {n: _adamw("adamw_" + n, w[n][0], g_parts[n], m[n][0], v[n][0], tr=256, slots=True) for n in BIG[1:]}
    t = lambda a: jnp.swapaxes(a[0], 0, 1)
    big['w_in'] = [jnp.swapaxes(r, 0, 1) for r in _adamw("adamw_w_in", t(w_in), g_parts['w_in'], t(m_w_in), t(v_w_in),
                                                         tr=256, slots=True, by_columns=True)]
    big_done = sum(big[n][1][0:1, 0:1] * 0 for n in BIG)
    small_all = _after(small_landed()[0], big_done)
    ssum, small_res = _small_update(small_all, SMALL_LAYOUT, w, m, v)
    dmod_all = small_all[:, 0, :6 * D]
    g_w_ada = _ada_grad(c_all, lax.dynamic_slice(dmod_all, (0, me * ada_cols), (N_DEV, ada_cols)))
    ada = _adamw("adamw_ada", w_ada[0], g_w_ada, m_w_ada[0], v_w_ada[0], tr=256, slots=False)
    g_conv_w = lax.dynamic_slice(ssum[:, 6 * D:6 * D + 4 * N_DEV * conv_cols].reshape(4, N_DEV * conv_cols),
                                 (0, me * conv_cols), (4, conv_cols))
    conv = _adamw("adamw_conv_w", conv_w[0], g_conv_w, m_conv_w[0], v_conv_w[0], tr=4, slots=False)

    results = []
    for k in range(4):
        vals = {n: small_res[n][k] for n in small_res}
        vals['w_ada'], vals['conv_w'] = ada[k][None], conv[k][None]
        for n in BIG:
            vals[n] = big[n][k][None]
        results.append(vals)
    return (ssum[0, SMALL_LOSS_LANE], out['grad_x'][None], *[res[n] for res in results for n in WEIGHTS])
```

```python
import functools

import jax
import jax.numpy as jnp
from jax import lax
from jax.experimental import pallas as pl
from jax.experimental.pallas import tpu as pltpu
from jax.experimental.pallas import tpu_sc as plsc

F32, BF16 = jnp.float32, jnp.bfloat16

N_DEV = 8
D = 1024
NH, HD = 16, 64
NSTATE = 128
CHUNK = 128
HG = 8
DFF = 4096
ALPHA = 2.0 ** 0.25
EPS = 1e-5
ATT_SCALE = HD ** -0.5

OFF_Z, OFF_XS, OFF_Q, OFF_K, OFF_V, OFF_BC, OFF_DTF = 0, 1024, 2048, 3072, 4096, 5120, 5632
PCOLS = 5760
W_Z, W_XS, W_BC, W_DT, W_Q, W_K, W_V, W_F = 0, 1024, 2048, 2560, 2576, 3600, 4624, 5648
IN_COLS = 5664

ADAM_LR, ADAM_B1, ADAM_B2, ADAM_EPS, ADAM_WD, ADAM_STEP = 0.001, 0.9, 0.999, 1e-08, 0.01, 10

VMEM_LIMIT = 56 << 20

NN = (((1,), (0,)), ((), ()))
NT = (((1,), (1,)), ((), ()))
TN = (((0,), (0,)), ((), ()))


def _dot(a, b, dims=NN):
    return lax.dot_general(a, b, dims, preferred_element_type=F32)


def _bdot(a, b, dims=NN):
    return _dot(a.astype(BF16), b.astype(BF16), dims)


def _split3(v, terms=3):
    parts, rest = [], v
    for _ in range(terms):
        p = rest.astype(BF16)
        parts.append(p)
        rest = rest - p.astype(F32)
    return parts


def _sel_left(m01, v):
    return sum(_dot(m01, p) for p in _split3(v))


def _sel_right(v, m01, dims=NN, terms=3):
    return sum(_dot(p, m01, dims) for p in _split3(v, terms))


def _iota(shape, dim):
    return lax.broadcasted_iota(jnp.int32, shape, dim)


def _tri_lower(n):
    return (_iota((n, n), 1) <= _iota((n, n), 0)).astype(BF16)


def _tri_upper(n):
    return (_iota((n, n), 1) >= _iota((n, n), 0)).astype(BF16)


def _head_expand():
    return (lax.shift_right_logical(_iota((128, D), 1), 6) == _iota((128, D), 0)).astype(BF16)


def _head_reduce():
    return (lax.shift_right_logical(_iota((D, 128), 0), 6) == _iota((D, 128), 1)).astype(BF16)


def _sigmoid(x):
    return 1.0 / (1.0 + jnp.exp(-x))


def _silu(x):
    return x * _sigmoid(x)


def _dsilu(x):
    s = _sigmoid(x)
    return s * (1.0 + x * (1.0 - s))


def _softplus(x):
    return jnp.maximum(x, 0.0) + jnp.log(1.0 + jnp.exp(-jnp.abs(x)))


def _log_sigmoid(x):
    return jnp.minimum(x, 0.0) - jnp.log(1.0 + jnp.exp(-jnp.abs(x)))


def _params(sem):
    return pltpu.CompilerParams(dimension_semantics=sem, vmem_limit_bytes=VMEM_LIMIT)


def _mm_nn(name, a, b, *, tm, tn, tk, out_dtype, pro=None, aux=(), epi=None):
    m, k_all = a.shape
    b_sharded = b.ndim == 3
    n = b.shape[0] * b.shape[2] if b_sharded else b.shape[1]
    per_tile = tn // b.shape[2] if b_sharded else 1
    assert not b_sharded or tn == per_tile * b.shape[2]
    nk = k_all // tk
    n_aux = len(aux)
    b_spec = (pl.BlockSpec((per_tile, tk, b.shape[2]), lambda i, j, k: (j, k, 0)) if b_sharded
              else pl.BlockSpec((tk, tn), lambda i, j, k: (k, j)))

    def body(a_ref, b_ref, *rest):
        aux_refs, o_ref = rest[:n_aux], rest[n_aux]
        at = a_ref[...]
        if pro is not None:
            at = pro(at, *[r[...] for r in aux_refs])
        if b_sharded:
            part = jnp.concatenate([_bdot(at, b_ref[q]) for q in range(per_tile)], axis=1)
        else:
            part = _bdot(at, b_ref[...])
        if nk == 1:
            o_ref[...] = (part if epi is None else epi(part)).astype(out_dtype)
            return
        assert epi is None
        acc_ref = rest[n_aux + 1]
        kk = pl.program_id(2)

        @pl.when(kk == 0)
        def _():
            acc_ref[...] = part

        @pl.when(kk > 0)
        def _():
            acc_ref[...] += part

        @pl.when(kk == nk - 1)
        def _():
            o_ref[...] = acc_ref[...].astype(out_dtype)

    return pl.pallas_call(
        body, name=name,
        grid=(m // tm, n // tn, nk),
        in_specs=[pl.BlockSpec((tm, tk), lambda i, j, k: (i, k)), b_spec]
        + [pl.BlockSpec((1, tk), lambda i, j, k: (0, k)) for _ in aux],
        out_specs=pl.BlockSpec((tm, tn), lambda i, j, k: (i, j)),
        out_shape=jax.ShapeDtypeStruct((m, n), out_dtype),
        scratch_shapes=[] if nk == 1 else [pltpu.VMEM((tm, tn), F32)],
        compiler_params=_params(("parallel", "parallel", "arbitrary")),
    )(a, b, *aux)


def _mm_nt(name, a_list, b_list, *, n, tm, tn, out_dtype, epi=None, epi_aux=(), b_rows=False):
    m = a_list[0][0].shape[0]
    n_op = len(a_list)
    n_epi = len(epi_aux)
    dims = NN if b_rows else NT

    def body(*refs):
        a_refs, b_refs = refs[:n_op], refs[n_op:2 * n_op]
        e_refs, o_ref = refs[2 * n_op:2 * n_op + n_epi], refs[2 * n_op + n_epi]
        acc = None
        for a_ref, b_ref in zip(a_refs, b_refs):
            part = _bdot(a_ref[...], b_ref[...], dims)
            acc = part if acc is None else acc + part
        if epi is not None:
            acc = epi(acc, *[r[...] for r in e_refs])
        o_ref[...] = acc.astype(out_dtype)

    in_specs = [pl.BlockSpec((tm, w), functools.partial(lambda i, j, cb: (i, cb), cb=cb)) for (_, w, cb) in a_list]
    for (b, w, cb) in b_list:
        if b_rows:
            in_specs.append(pl.BlockSpec((w, tn), functools.partial(lambda i, j, cb: (cb, j), cb=cb)))
        elif b.ndim == 3:
            in_specs.append(pl.BlockSpec((None, tn, w), functools.partial(lambda i, j, cb: (cb, j, 0), cb=cb)))
        else:
            in_specs.append(pl.BlockSpec((tn, w), functools.partial(lambda i, j, cb: (j, cb), cb=cb)))
    in_specs += [pl.BlockSpec((tm, tn), lambda i, j: (i, j)) for _ in epi_aux]
    return pl.pallas_call(
        body, name=name,
        grid=(m // tm, n // tn),
        in_specs=in_specs,
        out_specs=pl.BlockSpec((tm, tn), lambda i, j: (i, j)),
        out_shape=jax.ShapeDtypeStruct((m, n), out_dtype),
        compiler_params=_params(("parallel", "parallel")),
    )(*[a for (a, _, _) in a_list], *[b for (b, _, _) in b_list], *epi_aux)


def _mm_tn(name, a, b, *, tm, tn, ts, pro=None, aux=(), col_shards=False):
    s_all, ka = a.shape
    nb = b.shape[1]
    n_aux = len(aux)
    ns = s_all // ts
    shard_w = nb // N_DEV
    per_tile = tn // shard_w
    assert not col_shards or tn == per_tile * shard_w

    def body(a_ref, b_ref, *rest):
        aux_refs, o_ref, acc_ref = rest[:n_aux], rest[n_aux], rest[n_aux + 1]
        at = a_ref[...]
        if pro is not None:
            at = pro(at, *[r[...] for r in aux_refs])
        part = _bdot(at, b_ref[...], TN)
        ss = pl.program_id(2)

        @pl.when(ss == 0)
        def _():
            acc_ref[...] = part

        @pl.when(ss > 0)
        def _():
            acc_ref[...] += part

        @pl.when(ss == ns - 1)
        def _():
            if col_shards:
                for q in range(per_tile):
                    o_ref[q] = acc_ref[:, q * shard_w:(q + 1) * shard_w].astype(BF16)
            else:
                o_ref[...] = acc_ref[...].astype(BF16)

    if col_shards:
        out_spec = pl.BlockSpec((per_tile, tm, shard_w), lambda i, j, s: (j, i, 0))
        out_shape = jax.ShapeDtypeStruct((N_DEV, ka, shard_w), BF16)
    else:
        out_spec = pl.BlockSpec((tm, tn), lambda i, j, s: (i, j))
        out_shape = jax.ShapeDtypeStruct((ka, nb), BF16)
    return pl.pallas_call(
        body, name=name,
        grid=(ka // tm, nb // tn, ns),
        in_specs=[pl.BlockSpec((ts, tm), lambda i, j, s: (s, i)),
                  pl.BlockSpec((ts, tn), lambda i, j, s: (s, j))]
        + [pl.BlockSpec((1, tm), lambda i, j, s: (0, i)) for _ in aux],
        out_specs=out_spec, out_shape=out_shape,
        scratch_shapes=[pltpu.VMEM((tm, tn), F32)],
        compiler_params=_params(("parallel", "parallel", "arbitrary")),
    )(a, b, *aux)


def _rowk(name, fn, n_rows, tr, rows, fulls, outs, accs, reverse=False):
    n = n_rows // tr
    n_row, n_full, n_out, n_acc = len(rows), len(fulls), len(outs), len(accs)

    def pos(i):
        return (n - 1 - i) if reverse else i

    def body(*refs):
        row_refs = refs[:n_row]
        full_refs = refs[n_row:n_row + n_full]
        out_refs = refs[n_row + n_full:n_row + n_full + n_out]
        acc_refs = refs[n_row + n_full + n_out:]
        i = pl.program_id(0)

        @pl.when(i == 0)
        def _():
            for r in acc_refs:
                r[...] = jnp.zeros(r.shape, r.dtype)

        res = fn(pos(i), *[r[...] for r in row_refs], *[r[...] for r in full_refs], *[r[...] for r in acc_refs])
        for r, v in zip(out_refs + acc_refs, res):
            r[...] = v.astype(r.dtype)

    def row_map(i, cb, shift):
        return (jnp.clip(pos(i) + shift, 0, n - 1), cb)

    def halo_map(i, cb, shift):
        tile = jnp.clip(pos(i) + shift, 0, n - 1)
        return (tile * (tr // 8) + (tr // 8 - 1 if shift < 0 else 0), cb)

    in_specs = [pl.BlockSpec((tr, w), functools.partial(row_map, cb=cb, shift=sh)) if sh == 0 else
                pl.BlockSpec((8, w), functools.partial(halo_map, cb=cb, shift=sh)) for (_, w, cb, sh) in rows]
    in_specs += [pl.BlockSpec(f.shape, functools.partial(lambda i, nd: (0,) * nd, nd=f.ndim)) for f in fulls]
    out_specs = [pl.BlockSpec((tr, w), lambda i: (pos(i), 0)) for (w, _) in outs]
    out_specs += [pl.BlockSpec((r, w), lambda i: (0, 0)) for (r, w) in accs]
    out_shape = [jax.ShapeDtypeStruct((n_rows, w), dt) for (w, dt) in outs]
    out_shape += [jax.ShapeDtypeStruct((r, w), F32) for (r, w) in accs]
    return pl.pallas_call(
        body, name=name, grid=(n,), in_specs=in_specs, out_specs=out_specs, out_shape=out_shape,
        compiler_params=_params(("arbitrary",)),
    )(*[a for (a, _, _, _) in rows], *fulls)


def _colsum(x):
    return jnp.sum(x, axis=0, keepdims=True)


def _mean(x):
    return jnp.mean(x, axis=-1, keepdims=True)


def _modulate(x, sc, sh):
    return x * (1.0 + sc) + sh


def _shift_down(cur, prev8, j):
    tr = cur.shape[0]
    row8 = _iota(prev8.shape, 0)
    head = jnp.where(row8 < j, pltpu.roll(prev8, j, 0), pltpu.roll(cur[0:8], j, 0))
    return head if tr == 8 else jnp.concatenate([head, pltpu.roll(cur, j, 0)[8:]], axis=0)


def _shift_up(cur, next8, j):
    tr = cur.shape[0]
    row8 = _iota(next8.shape, 0)
    tail = jnp.where(row8 < 8 - j, pltpu.roll(cur[tr - 8:], 8 - j, 0), pltpu.roll(next8, 8 - j, 0))
    return jnp.concatenate([pltpu.roll(cur, tr - j, 0)[:tr - 8], tail], axis=0)


def _conv(cur, prev, w, b):
    out = cur * w[3:4] + b
    for j in (1, 2, 3):
        out = out + _shift_down(cur, prev, j) * w[3 - j:4 - j]
    return out


def _conv_fwd(p, w_xs, b_xs, w_bc, b_bc, s):
    def fn(pos, xs, xs_prev, bc, bc_prev, w_xs, b_xs, w_bc, b_bc):
        first = pos == 0
        xs_prev = jnp.where(first, 0.0, xs_prev)
        bc_prev = jnp.where(first, 0.0, bc_prev)
        return _silu(_conv(xs, xs_prev, w_xs, b_xs)), _silu(_conv(bc, bc_prev, w_bc, b_bc))

    return _rowk("conv_fwd", fn, s, 256,
                 [(p, D, OFF_XS // D, 0), (p, D, OFF_XS // D, -1), (p, 512, OFF_BC // 512, 0), (p, 512, OFF_BC // 512, -1)],
                 [w_xs, b_xs, w_bc, b_bc], [(D, F32), (512, F32)], [])


def _conv_bwd(dxs_a, dbc_a, p, w_xs, b_xs, w_bc, b_bc, s):
    tr = 256
    n = s // tr

    def fn(pos, da1, da1n, x1, x1p, x1n, da2, da2n, x2, x2p, x2n, w1, b1, w2, b2, aw1, ab1, aw2, ab2):
        dx1, dw1, db1 = _conv_bwd_fn(pos, n, da1, da1n, x1, x1p, x1n, w1, b1)
        dx2, dw2, db2 = _conv_bwd_fn(pos, n, da2, da2n, x2, x2p, x2n, w2, b2)
        return dx1, dx2, aw1 + dw1, ab1 + db1, aw2 + dw2, ab2 + db2

    cx, cb = OFF_XS // D, OFF_BC // 512
    return _rowk("conv_bwd", fn, s, tr,
                 [(dxs_a, D, 0, 0), (dxs_a, D, 0, 1), (p, D, cx, 0), (p, D, cx, -1), (p, D, cx, 1),
                  (dbc_a, 512, 0, 0), (dbc_a, 512, 0, 1), (p, 512, cb, 0), (p, 512, cb, -1), (p, 512, cb, 1)],
                 [w_xs, b_xs, w_bc, b_bc], [(D, BF16), (512, BF16)], [(8, D), (1, D), (8, 512), (1, 512)])


def _conv_bwd_fn(pos, n, da, da_next, x, x_prev, x_next, w, b):
    first, last = pos == 0, pos == n - 1
    x_prev = jnp.where(first, 0.0, x_prev)
    shifted = {j: _shift_down(x, x_prev, j) for j in (1, 2, 3)}
    conv = x * w[3:4] + b
    for j in (1, 2, 3):
        conv = conv + shifted[j] * w[3 - j:4 - j]
    dc = da * _dsilu(conv)
    dc_next = jnp.where(last, 0.0, da_next * _dsilu(_conv(x_next, x[x.shape[0] - 8:], w, b)))
    dx = dc * w[3:4]
    dws = [None] * 4
    dws[3] = _colsum(dc * x)
    for j in (1, 2, 3):
        dx = dx + _shift_up(dc, dc_next, j) * w[3 - j:4 - j]
        dws[3 - j] = _colsum(dc * shifted[j])
    row = _iota((8, x.shape[1]), 0)
    dw = jnp.zeros((8, x.shape[1]), F32)
    for k in range(4):
        dw = jnp.where(row == k, dws[k], dw)
    return dx, dw, _colsum(dc)


def _ssd_gates(dtf, bias, a_log):
    lane = _iota(dtf.shape, 1)
    head = lane < NH
    dt = jnp.where(head, _softplus(dtf + bias), 0.0)
    a_neg = jnp.where(_iota(a_log.shape, 1) < NH, -jnp.exp(a_log), 0.0)
    a = dt * a_neg
    cs = _sel_left(_tri_lower(CHUNK), a)
    return dt, a_neg, cs


def _decay_mask(cs_ref, cst_ref, h):
    diff = cs_ref[:, h:h + 1] - cst_ref[h:h + 1, :]
    low = _iota((CHUNK, CHUNK), 1) <= _iota((CHUNK, CHUNK), 0)
    return jnp.where(low, jnp.exp(jnp.minimum(diff, 0.0)), 0.0)


def _ssd_fwd(xs_a, bc_a, p, bias128, alog128, dskip_x, s):
    nc = s // CHUNK
    t = CHUNK

    def body(xs_ref, bc_ref, dtf_ref, bias_ref, alog_ref, dsk_ref, y_ref, st_ref,
             state, x_sc, xw_sc, cs_sc, cst_sc, yd_sc):
        c = pl.program_id(0)

        @pl.when(c == 0)
        def _():
            state[...] = jnp.zeros(state.shape, F32)

        dt, _, cs = _ssd_gates(dtf_ref[...], bias_ref[...], alog_ref[...])
        cs_sc[...] = cs
        cst_sc[...] = cs.T
        cs_last = cs[t - 1:t, :]
        expand = _head_expand()
        ex = _sel_right(jnp.concatenate([dt, jnp.exp(cs), jnp.exp(cs_last - cs)], axis=0), expand, terms=2)
        dt_x, eo_x, we_x = ex[0:t], ex[t:2 * t], ex[2 * t:3 * t]
        g_x = _sel_right(jnp.broadcast_to(jnp.exp(cs_last), (8, 128)), expand)[0:1]
        xs = xs_ref[...]
        x = xs * dt_x
        x_sc[...] = x.astype(BF16)
        xw_sc[...] = (x * we_x).astype(BF16)
        prev = state[...]
        st_ref[0] = prev
        prev_b = prev.astype(BF16)
        for g in range(2):
            cols = slice(g * 512, (g + 1) * 512)
            b_g = bc_ref[:, g * 128:(g + 1) * 128].astype(BF16)
            c_g = bc_ref[:, 256 + g * 128:256 + (g + 1) * 128].astype(BF16)
            gmat = _dot(c_g, b_g, NT)
            y_off = _dot(c_g, prev_b[:, cols]) * eo_x[:, cols]
            s_loc = _dot(b_g, xw_sc[:, cols], TN)
            state[:, cols] = g_x[:, cols] * prev[:, cols] + s_loc
            for e in range(HG):
                h = g * HG + e
                m = gmat * _decay_mask(cs_sc, cst_sc, h)
                yd_sc[:, h * HD:(h + 1) * HD] = _dot(m.astype(BF16), x_sc[:, h * HD:(h + 1) * HD])
            y_ref[:, cols] = yd_sc[:, cols] + y_off + dsk_ref[:, cols] * xs[:, cols]

    return pl.pallas_call(
        body, name="ssd_fwd", grid=(nc,),
        in_specs=[pl.BlockSpec((t, D), lambda c: (c, 0)),
                  pl.BlockSpec((t, 512), lambda c: (c, 0)),
                  pl.BlockSpec((t, 128), lambda c: (c, OFF_DTF // 128)),
                  pl.BlockSpec((1, 128), lambda c: (0, 0)),
                  pl.BlockSpec((1, 128), lambda c: (0, 0)),
                  pl.BlockSpec((1, D), lambda c: (0, 0))],
        out_specs=[pl.BlockSpec((t, D), lambda c: (c, 0)),
                   pl.BlockSpec((1, NSTATE, D), lambda c: (c, 0, 0))],
        out_shape=[jax.ShapeDtypeStruct((s, D), F32), jax.ShapeDtypeStruct((nc, NSTATE, D), F32)],
        scratch_shapes=[pltpu.VMEM((NSTATE, D), F32), pltpu.VMEM((t, D), BF16), pltpu.VMEM((t, D), BF16),
                        pltpu.VMEM((t, 128), F32), pltpu.VMEM((128, t), F32), pltpu.VMEM((t, D), F32)],
        compiler_params=_params(("arbitrary",)),
    )(xs_a, bc_a, p, bias128, alog128, dskip_x)


def _ssd_bwd(dy, xs_a, bc_a, p, states, bias128, alog128, dskip_x, s):
    nc = s // CHUNK
    t = CHUNK

    def body(dy_ref, xs_ref, bc_ref, dtf_ref, st_ref, bias_ref, alog_ref, dsk_ref,
             dxs_ref, dbc_ref, ddt_ref, dalog_ref, dskip_ref,
             dstate, x_sc, dy_sc, dx_sc, deo_sc, dwe_sc, cs_sc, cst_sc, dcol_sc, drow_sc):
        i = pl.program_id(0)

        @pl.when(i == 0)
        def _():
            dstate[...] = jnp.zeros(dstate.shape, F32)
            dalog_ref[...] = jnp.zeros(dalog_ref.shape, F32)
            dskip_ref[...] = jnp.zeros(dskip_ref.shape, F32)

        dtf = dtf_ref[...]
        dt, a_neg, cs = _ssd_gates(dtf, bias_ref[...], alog_ref[...])
        cs_sc[...] = cs
        cst_sc[...] = cs.T
        cs_last = cs[t - 1:t, :]
        eo, we, g_end = jnp.exp(cs), jnp.exp(cs_last - cs), jnp.exp(cs_last)
        expand, reduce = _head_expand(), _head_reduce()
        ex = _sel_right(jnp.concatenate([dt, eo, we], axis=0), expand, terms=2)
        dt_x, eo_x, we_x = ex[0:t], ex[t:2 * t], ex[2 * t:3 * t]
        g_x = _sel_right(jnp.broadcast_to(g_end, (8, 128)), expand)[0:1]
        xs = xs_ref[...]
        dyv = dy_ref[...]
        x = xs * dt_x
        x_sc[...] = x.astype(BF16)
        dy_sc[...] = dyv.astype(BF16)
        dyo_b = (dyv * eo_x).astype(BF16)
        xw_b = (x * we_x).astype(BF16)
        prev = st_ref[0]
        prev_b = prev.astype(BF16)
        dnext = dstate[...]
        dnext_b = dnext.astype(BF16)
        dcol_sc[...] = jnp.zeros(dcol_sc.shape, F32)
        drow_sc[...] = jnp.zeros(drow_sc.shape, F32)
        lane_row = _iota((1, 128), 1)
        sub_col = _iota((128, 1), 0)
        for g in range(2):
            cols = slice(g * 512, (g + 1) * 512)
            b_g = bc_ref[:, g * 128:(g + 1) * 128].astype(BF16)
            c_g = bc_ref[:, 256 + g * 128:256 + (g + 1) * 128].astype(BF16)
            gmat = _dot(c_g, b_g, NT)
            b_ds = _dot(b_g, dnext_b[:, cols])
            c_s = _dot(c_g, prev_b[:, cols])
            dx_sc[:, cols] = b_ds * we_x[:, cols]
            deo_sc[:, cols] = dyv[:, cols] * c_s
            dwe_sc[:, cols] = b_ds * x[:, cols]
            db = _dot(xw_b[:, cols], dnext_b[:, cols], NT)
            dc = _dot(dyo_b[:, cols], prev_b[:, cols], NT)
            dstate[:, cols] = g_x[:, cols] * dnext[:, cols] + _dot(c_g, dyo_b[:, cols], TN)
            dg = jnp.zeros((t, t), F32)
            for e in range(HG):
                h = g * HG + e
                hc = slice(h * HD, (h + 1) * HD)
                lmat = _decay_mask(cs_sc, cst_sc, h)
                m = gmat * lmat
                dx_sc[:, hc] += _dot(m.astype(BF16), dy_sc[:, hc], TN)
                dm = _dot(dy_sc[:, hc], x_sc[:, hc], NT)
                dg = dg + dm * lmat
                qm = dm * m
                dcol_sc[...] += jnp.sum(qm, axis=1, keepdims=True) * (lane_row == h).astype(F32)
                drow_sc[...] += (sub_col == h).astype(F32) * jnp.sum(qm, axis=0, keepdims=True)
            dg_b = dg.astype(BF16)
            dbc_ref[:, g * 128:(g + 1) * 128] = db + _dot(dg_b, c_g, TN)
            dbc_ref[:, 256 + g * 128:256 + (g + 1) * 128] = dc + _dot(dg_b, b_g)
        d_eo = _sel_right(deo_sc[...], reduce, terms=2)
        d_we = _sel_right(dwe_sc[...], reduce, terms=2)
        d_gend = _sel_right(jnp.broadcast_to(_colsum(dnext * prev), (8, D)), reduce)[0:1]
        d_cs = dcol_sc[...] - drow_sc[...].T + d_eo * eo - d_we * we
        extra = _colsum(d_we * we) + d_gend * g_end
        d_cs = d_cs + jnp.where(_iota((t, 128), 0) == t - 1, extra, 0.0)
        da = _sel_left(_tri_upper(t), d_cs)
        dx = dx_sc[...]
        ddt = _sel_right(dx * xs, reduce, terms=2) + da * a_neg
        dxs_ref[...] = dx * dt_x + dsk_ref[...] * dyv
        ddt_ref[...] = jnp.where(_iota((t, 128), 1) < NH, ddt * _sigmoid(dtf + bias_ref[...]), 0.0)
        dalog_ref[...] += _colsum(da * dt) * a_neg
        dskip_ref[...] += _sel_right(jnp.broadcast_to(_colsum(dyv * xs), (8, D)), reduce)[0:1]

    rev = lambda i: nc - 1 - i
    return pl.pallas_call(
        body, name="ssd_bwd", grid=(nc,),
        in_specs=[pl.BlockSpec((t, D), lambda i: (rev(i), 0)),
                  pl.BlockSpec((t, D), lambda i: (rev(i), 0)),
                  pl.BlockSpec((t, 512), lambda i: (rev(i), 0)),
                  pl.BlockSpec((t, 128), lambda i: (rev(i), OFF_DTF // 128)),
                  pl.BlockSpec((1, NSTATE, D), lambda i: (rev(i), 0, 0)),
                  pl.BlockSpec((1, 128), lambda i: (0, 0)),
                  pl.BlockSpec((1, 128), lambda i: (0, 0)),
                  pl.BlockSpec((1, D), lambda i: (0, 0))],
        out_specs=[pl.BlockSpec((t, D), lambda i: (rev(i), 0)),
                   pl.BlockSpec((t, 512), lambda i: (rev(i), 0)),
                   pl.BlockSpec((t, 128), lambda i: (rev(i), 0)),
                   pl.BlockSpec((1, 128), lambda i: (0, 0)),
                   pl.BlockSpec((1, 128), lambda i: (0, 0))],
        out_shape=[jax.ShapeDtypeStruct((s, D), F32), jax.ShapeDtypeStruct((s, 512), F32),
                   jax.ShapeDtypeStruct((s, 128), F32), jax.ShapeDtypeStruct((1, 128), F32),
                   jax.ShapeDtypeStruct((1, 128), F32)],
        scratch_shapes=[pltpu.VMEM((NSTATE, D), F32), pltpu.VMEM((t, D), BF16), pltpu.VMEM((t, D), BF16),
                        pltpu.VMEM((t, D), F32), pltpu.VMEM((t, D), F32), pltpu.VMEM((t, D), F32),
                        pltpu.VMEM((t, 128), F32), pltpu.VMEM((128, t), F32),
                        pltpu.VMEM((t, 128), F32), pltpu.VMEM((128, t), F32)],
        compiler_params=_params(("arbitrary",)),
    )(dy, xs_a, bc_a, p, states, bias128, alog128, dskip_x)


def _gate_lanes(shape):
    lane = _iota(shape, 1)
    return (lane >= NH) & (lane < 2 * NH)


def _cum_fwd(p, bias128, s):
    tr = min(512, s)

    def body(dtf_ref, bias_ref, o_ref, carry):
        @pl.when(pl.program_id(0) == 0)
        def _():
            carry[...] = jnp.zeros(carry.shape, F32)

        lf = jnp.where(_gate_lanes((tr, 128)), _log_sigmoid(dtf_ref[...] + bias_ref[...]), 0.0)
        cum = _sel_left(_tri_lower(tr), lf) + carry[...]
        carry[...] = cum[tr - 1:tr, :]
        o_ref[...] = cum

    return pl.pallas_call(
        body, name="cum_fwd", grid=(s // tr,),
        in_specs=[pl.BlockSpec((tr, 128), lambda i: (i, OFF_DTF // 128)), pl.BlockSpec((1, 128), lambda i: (0, 0))],
        out_specs=pl.BlockSpec((tr, 128), lambda i: (i, 0)),
        out_shape=jax.ShapeDtypeStruct((s, 128), F32),
        scratch_shapes=[pltpu.VMEM((1, 128), F32)],
        compiler_params=_params(("arbitrary",)),
    )(p, bias128)


def _cum_bwd(dcum, ddt_raw, p, bias128, s):
    tr = min(512, s)

    def fn(pos, dcum, ddt, dtf, bias, carry, acc):
        suffix = _sel_left(_tri_upper(tr), dcum) + carry
        dfr = jnp.where(_gate_lanes((tr, 128)), suffix * _sigmoid(-(dtf + bias)), 0.0)
        out = ddt + dfr
        return out, suffix[0:1, :], acc + _colsum(out)

    return _rowk("cum_bwd", fn, s, tr, [(dcum, 128, 0, 0), (ddt_raw, 128, 0, 0), (p, 128, OFF_DTF // 128, 0)],
                 [bias128], [(128, BF16)], [(1, 128), (1, 128)], reverse=True)


ATT_BLOCK = 512
ATT_STRIP = 32


def _head_part(shape, h, dim):
    i = _iota(shape, dim)
    return (i >= h * HD) & (i < (h + 1) * HD)


def _k_augmented(k_blk, cum_blk, j, h):
    tk = k_blk.shape[0]
    lane = _iota((tk, 128), 1)
    col = jnp.sum(jnp.where(lane == NH + 2 * j + h, cum_blk, 0.0), axis=1, keepdims=True)
    c0, c1, c2 = [c.astype(F32) for c in _split3(-col)]
    k_h = k_blk if h == 0 else pltpu.roll(k_blk, HD, 1)
    aug = jnp.where(lane == HD, c0, jnp.where(lane == HD + 1, c1, jnp.where(lane == HD + 2, c2, 0.0)))
    return jnp.where(lane < HD, k_h, aug).astype(BF16)


def _q_augmented_t(q_blk):
    tq = q_blk.shape[0]
    q_t = (q_blk * ATT_SCALE).T.astype(BF16)
    ones = (_iota((HD, tq), 0) < 3).astype(BF16)
    return [jnp.concatenate([q_t[h * HD:(h + 1) * HD], ones], axis=0) for h in range(2)]


def _rows01(r0, r1):
    sub = _iota((8, r0.shape[1]), 0)
    return jnp.where(sub == 0, r0, jnp.where(sub == 1, r1, 0.0))


def _fold8(x, op, cur):
    for g in range(x.shape[0] // 8):
        cur = op(cur, x[8 * g:8 * (g + 1), :])
    return cur


def _attn_fwd(p, cum, s):
    t = min(ATT_BLOCK, s)
    nq = s // t
    r = ATT_STRIP

    def body(q_ref, k_ref, v_ref, c_ref, o_ref, lse_ref, kaug_sc, vt_sc, s0_sc, s1_sc, p0_sc, p1_sc, m_sc, l_sc, acc_sc):
        j, qi = pl.program_id(0), pl.program_id(1)
        s_sc, p_sc = (s0_sc, s1_sc), (p0_sc, p1_sc)

        @pl.when(qi == 0)
        def _():
            for c in range(nq):
                rows = slice(c * t, (c + 1) * t)
                k_blk, vt = k_ref[rows, :], v_ref[rows, :].T
                for h in range(2):
                    kaug_sc[h, rows, :] = _k_augmented(k_blk, c_ref[rows, :], j, h)
                    vt_sc[h, :, rows] = vt[h * HD:(h + 1) * HD].astype(BF16)

        qaug_t = _q_augmented_t(q_ref[...])
        m_sc[...] = jnp.full(m_sc.shape, -1e30, F32)
        l_sc[...] = jnp.zeros(l_sc.shape, F32)
        acc_sc[...] = jnp.zeros(acc_sc.shape, F32)
        top = _iota((128, t), 0) < HD

        def logits(kb, buf):
            kv = pl.ds(pl.multiple_of(kb * t, t), t)
            for h in range(2):
                s_sc[buf][h] = _dot(kaug_sc[h, kv, :], qaug_t[h])

        def softmax(buf, diagonal):
            alphas = []
            for h in range(2):
                cur = jnp.full((8, t), -1e30, F32)
                for i in range(t // r):
                    rows = slice(i * r, (i + 1) * r)
                    x = s_sc[buf][h, rows, :]
                    if diagonal:
                        x = jnp.where(_iota((r, t), 1) >= i * r + _iota((r, t), 0), x, -1e30)
                        s_sc[buf][h, rows, :] = x
                    cur = _fold8(x, jnp.maximum, cur)
                m_prev = m_sc[h, 0:1, :]
                m_new = jnp.maximum(m_prev, jnp.max(cur, axis=0, keepdims=True))
                alpha = jnp.exp(m_prev - m_new)
                m_sc[h, 0:1, :] = m_new
                alphas.append(alpha)
                tot = jnp.zeros((8, t), F32)
                for i in range(t // r):
                    rows = slice(i * r, (i + 1) * r)
                    pr = jnp.exp(s_sc[buf][h, rows, :] - m_new)
                    p_sc[buf][h, rows, :] = pr.astype(BF16)
                    tot = _fold8(pr, jnp.add, tot)
                l_sc[h, 0:1, :] = alpha * l_sc[h, 0:1, :] + jnp.sum(tot, axis=0, keepdims=True)
            return alphas

        def accumulate(kb, buf, alphas):
            kv = pl.ds(pl.multiple_of(kb * t, t), t)
            for h in range(2):
                part = slice(h * HD, (h + 1) * HD)
                acc_sc[part, :] = acc_sc[part, :] * alphas[h] + _dot(vt_sc[h, :, kv], p_sc[buf][h])

        def first_trip():
            logits(0, 1)
            accumulate(qi, 0, softmax(0, True))
            logits(jnp.minimum(1, qi - 1), 0)
            return tuple(softmax(1, False))

        def only_diagonal():
            accumulate(qi, 0, softmax(0, True))
            return (jnp.ones((1, t), F32),) * 2

        def steady(u, alphas_b):
            accumulate(2 * u - 2, 1, alphas_b)
            logits(2 * u, 1)
            accumulate(2 * u - 1, 0, softmax(0, False))
            logits(jnp.minimum(2 * u + 1, qi - 1), 0)
            return tuple(softmax(1, False))

        logits(qi, 0)
        n_blocks = qi + 1
        alphas_b = lax.cond(qi >= 1, first_trip, only_diagonal)
        alphas_b = lax.fori_loop(1, n_blocks // 2, steady, alphas_b)
        last_b = 2 * (n_blocks // 2) - 2

        @pl.when((qi >= 1) & (n_blocks % 2 == 0))
        def _():
            accumulate(last_b, 1, alphas_b)

        @pl.when((qi >= 2) & (n_blocks % 2 == 1))
        def _():
            accumulate(last_b, 1, alphas_b)
            accumulate(qi - 1, 0, softmax(0, False))

        l0, l1 = l_sc[0, 0:1, :], l_sc[1, 0:1, :]
        o_ref[...] = (acc_sc[...] / jnp.where(top, l0, l1)).T
        lse_ref[0] = _rows01(m_sc[0, 0:1, :] + jnp.log(l0), m_sc[1, 0:1, :] + jnp.log(l1))

    return pl.pallas_call(
        body, name="attn_fwd", grid=(NH // 2, nq),
        in_specs=[pl.BlockSpec((t, 128), lambda j, qi: (qi, OFF_Q // 128 + j)),
                  pl.BlockSpec((s, 128), lambda j, qi: (0, OFF_K // 128 + j)),
                  pl.BlockSpec((s, 128), lambda j, qi: (0, OFF_V // 128 + j)),
                  pl.BlockSpec((s, 128), lambda j, qi: (0, 0))],
        out_specs=[pl.BlockSpec((t, 128), lambda j, qi: (qi, j)),
                   pl.BlockSpec((1, 8, t), lambda j, qi: (j, 0, qi))],
        out_shape=[jax.ShapeDtypeStruct((s, D), F32), jax.ShapeDtypeStruct((NH // 2, 8, s), F32)],
        scratch_shapes=[pltpu.VMEM((2, s, 128), BF16), pltpu.VMEM((2, HD, s), BF16), pltpu.VMEM((2, t, t), F32),
                        pltpu.VMEM((2, t, t), F32), pltpu.VMEM((2, t, t), BF16), pltpu.VMEM((2, t, t), BF16),
                        pltpu.VMEM((2, 8, t), F32), pltpu.VMEM((2, 8, t), F32), pltpu.VMEM((128, t), F32)],
        compiler_params=_params(("parallel", "arbitrary")),
    )(p, p, p, cum)


def _attn_bwd(p, cum, o, lse, do, s):
    t = min(ATT_BLOCK, s)
    nq = s // t
    r = ATT_STRIP

    def body(q_ref, k_ref, v_ref, c_ref, o_ref, lse_ref, do_ref, dq_ref, dk_ref, dv_ref, dc_ref, dr_ref,
             qaugt_sc, qh_sc, dot_sc, doh_sc, delta_sc, dqt_sc, dr_sc, kaug_sc, vh_sc, kt_sc,
             s0_sc, s1_sc, dp0_sc, dp1_sc, p0_sc, p1_sc, ds0_sc, ds1_sc, dk_sc, dv_sc, dc_sc):
        j, ki = pl.program_id(0), pl.program_id(1)
        s_sc, dp_sc, p_sc, ds_sc = (s0_sc, s1_sc), (dp0_sc, dp1_sc), (p0_sc, p1_sc), (ds0_sc, ds1_sc)

        @pl.when(ki == 0)
        def _():
            for c in range(nq):
                rows = slice(c * t, (c + 1) * t)
                q_blk, do_blk = q_ref[rows, :], do_ref[rows, :]
                qaugt_sc[0, :, rows], qaugt_sc[1, :, rows] = _q_augmented_t(q_blk)
                dot_sc[:, rows] = do_blk.T.astype(BF16)
                prod_t = (do_blk * o_ref[rows, :]).T
                delta_sc[:, rows] = _rows01(jnp.sum(prod_t[0:HD], axis=0, keepdims=True),
                                            jnp.sum(prod_t[HD:], axis=0, keepdims=True))
                for h in range(2):
                    head = _head_part((t, 128), h, 1)
                    qh_sc[h, rows, :] = jnp.where(head, q_blk * ATT_SCALE, 0.0).astype(BF16)
                    doh_sc[h, rows, :] = jnp.where(head, do_blk, 0.0).astype(BF16)
            dqt_sc[...] = jnp.zeros(dqt_sc.shape, F32)
            dr_sc[...] = jnp.zeros(dr_sc.shape, F32)

        k_blk, v_blk = k_ref[...], v_ref[...]
        kt = k_blk.T
        for h in range(2):
            kaug_sc[h] = _k_augmented(k_blk, c_ref[...], j, h)
            vh_sc[h] = jnp.where(_head_part((t, 128), h, 1), v_blk, 0.0).astype(BF16)
            kt_sc[h] = kt[h * HD:(h + 1) * HD].astype(BF16)
        dk_sc[...] = jnp.zeros(dk_sc.shape, F32)
        dv_sc[...] = jnp.zeros(dv_sc.shape, F32)
        dc_sc[...] = jnp.zeros(dc_sc.shape, F32)

        def inputs(qb, buf):
            qs = pl.ds(pl.multiple_of(qb * t, t), t)
            for h in range(2):
                s_sc[buf][h] = _dot(kaug_sc[h], qaugt_sc[h, :, qs])
                dp_sc[buf][h] = _dot(vh_sc[h], dot_sc[:, qs])

        def elementwise(qb, buf, diagonal):
            qs = pl.ds(pl.multiple_of(qb * t, t), t)
            for h in range(2):
                lse_row, delta_row = lse_ref[0, h:h + 1, qs], delta_sc[h:h + 1, qs]
                tot = jnp.zeros((8, t), F32)
                for i in range(t // r):
                    rows = slice(i * r, (i + 1) * r)
                    x = s_sc[buf][h, rows, :]
                    if diagonal:
                        x = jnp.where(_iota((r, t), 1) >= i * r + _iota((r, t), 0), x, -1e30)
                    pr = jnp.exp(x - lse_row)
                    ds = pr * (dp_sc[buf][h, rows, :] - delta_row)
                    p_sc[buf][h, rows, :] = pr.astype(BF16)
                    ds_sc[buf][h, rows, :] = ds.astype(BF16)
                    dc_sc[h, rows, :] += sum(ds[:, 128 * g:128 * (g + 1)] for g in range(t // 128))
                    tot = _fold8(ds, jnp.add, tot)
                dr_sc[h, :, qs] += tot

        def outputs(qb, buf):
            qs = pl.ds(pl.multiple_of(qb * t, t), t)
            dv_sc[...] += _dot(p_sc[buf][0], doh_sc[0, qs, :]) + _dot(p_sc[buf][1], doh_sc[1, qs, :])
            dk_sc[...] += _dot(ds_sc[buf][0], qh_sc[0, qs, :]) + _dot(ds_sc[buf][1], qh_sc[1, qs, :])
            for h in range(2):
                dqt_sc[h * HD:(h + 1) * HD, qs] += _dot(kt_sc[h], ds_sc[buf][h])

        def pair(a, b, a_diagonal):
            inputs(a, 0)
            inputs(b, 1)
            elementwise(a, 0, a_diagonal)
            outputs(a, 0)
            elementwise(b, 1, False)
            outputs(b, 1)

        def later(u, carry):
            pair(ki + 1 + 2 * u, ki + 2 + 2 * u, False)
            return carry

        n_later = nq - 1 - ki
        lax.fori_loop(0, n_later // 2, later, 0)

        @pl.when(n_later % 2 == 1)
        def _():
            pair(ki, nq - 1, True)

        @pl.when(n_later % 2 == 0)
        def _():
            inputs(ki, 0)
            elementwise(ki, 0, True)
            outputs(ki, 0)

        dk_ref[...] = dk_sc[...].astype(BF16)
        dv_ref[...] = dv_sc[...].astype(BF16)
        lane = _iota((t, 128), 1)
        cols = jnp.where(lane == 0, jnp.sum(dc_sc[0], axis=1, keepdims=True),
                         jnp.where(lane == 1, jnp.sum(dc_sc[1], axis=1, keepdims=True), 0.0))
        dc_ref[0] = cols.T[0:8, :]

        @pl.when(ki == nq - 1)
        def _():
            for c in range(nq):
                rows = slice(c * t, (c + 1) * t)
                dq_ref[rows, :] = dqt_sc[:, rows].T * ATT_SCALE
            dr_ref[0] = _rows01(jnp.sum(dr_sc[0], axis=0, keepdims=True), jnp.sum(dr_sc[1], axis=0, keepdims=True))

    whole = lambda off: pl.BlockSpec((s, 128), functools.partial(lambda j, ki, off: (0, off + j), off=off))
    return pl.pallas_call(
        body, name="attn_bwd", grid=(NH // 2, nq),
        in_specs=[whole(OFF_Q // 128),
                  pl.BlockSpec((t, 128), lambda j, ki: (ki, OFF_K // 128 + j)),
                  pl.BlockSpec((t, 128), lambda j, ki: (ki, OFF_V // 128 + j)),
                  pl.BlockSpec((t, 128), lambda j, ki: (ki, 0)),
                  whole(0),
                  pl.BlockSpec((1, 8, s), lambda j, ki: (j, 0, 0)),
                  whole(0)],
        out_specs=[whole(0),
                   pl.BlockSpec((t, 128), lambda j, ki: (ki, j)),
                   pl.BlockSpec((t, 128), lambda j, ki: (ki, j)),
                   pl.BlockSpec((1, 8, t), lambda j, ki: (j, 0, ki)),
                   pl.BlockSpec((1, 8, s), lambda j, ki: (j, 0, 0))],
        out_shape=[jax.ShapeDtypeStruct((s, D), F32), jax.ShapeDtypeStruct((s, D), BF16), jax.ShapeDtypeStruct((s, D), BF16),
                   jax.ShapeDtypeStruct((NH // 2, 8, s), F32), jax.ShapeDtypeStruct((NH // 2, 8, s), F32)],
        scratch_shapes=[pltpu.VMEM((2, 128, s), BF16), pltpu.VMEM((2, s, 128), BF16), pltpu.VMEM((128, s), BF16),
                        pltpu.VMEM((2, s, 128), BF16), pltpu.VMEM((8, s), F32), pltpu.VMEM((128, s), F32),
                        pltpu.VMEM((2, 8, s), F32), pltpu.VMEM((2, t, 128), BF16), pltpu.VMEM((2, t, 128), BF16),
                        pltpu.VMEM((2, HD, t), BF16)]
        + [pltpu.VMEM((2, t, t), F32)] * 4 + [pltpu.VMEM((2, t, t), BF16)] * 4
        + [pltpu.VMEM((t, 128), F32), pltpu.VMEM((t, 128), F32), pltpu.VMEM((2, t, 128), F32)],
        compiler_params=_params(("parallel", "arbitrary")),
    )(p, p, p, cum, o, lse, do)


def _ln_stats(u):
    mu = _mean(u)
    d = u - mu
    rstd = lax.rsqrt(_mean(d * d) + EPS)
    return d * rstd, rstd


def _ln_bwd(dx, xh, rstd, gam):
    dxh = dx * gam
    return rstd * (dxh - _mean(dxh) - xh * _mean(dxh * xh))


def _rms_bwd(d, xn, r, w):
    t = d * w
    return r * (t - xn * _mean(t * xn)), _colsum(d * xn)


def _mix_norm(y, p, att, w_ssm, w_att, s):
    def fn(pos, y, z, att, w1, w2):
        g = y * _silu(z)
        n1 = g * lax.rsqrt(_mean(g * g) + EPS) * w1
        n2 = att * lax.rsqrt(_mean(att * att) + EPS) * w2
        return (jnp.concatenate([n1, n2], axis=1),)

    return _rowk("mix_norm", fn, s, 512, [(y, D, 0, 0), (p, D, OFF_Z // D, 0), (att, D, 0, 0)],
                 [w_ssm, w_att], [(2 * D, BF16)], [])[0]


def _mix_norm_bwd(dmix, y, p, att, w_ssm, w_att, s):
    def fn(pos, dmix, y, z, att, w1, w2, a1, a2):
        sz = _silu(z)
        g = y * sz
        r1 = lax.rsqrt(_mean(g * g) + EPS)
        dg, dw1 = _rms_bwd(dmix[:, :D], g * r1, r1, w1)
        r2 = lax.rsqrt(_mean(att * att) + EPS)
        datt, dw2 = _rms_bwd(dmix[:, D:], att * r2, r2, w2)
        return dg * sz, dg * y * _dsilu(z), datt, a1 + dw1, a2 + dw2

    return _rowk("mix_norm_bwd", fn, s, 256, [(dmix, 2 * D, 0, 0), (y, D, 0, 0), (p, D, OFF_Z // D, 0), (att, D, 0, 0)],
                 [w_ssm, w_att], [(D, F32), (D, BF16), (D, F32)], [(1, D), (1, D)])


def _ln1(x0, y, g1, gam, bet, sc2, sh2, s):
    def fn(pos, x0, y, g1, gam, bet, sc2, sh2):
        xh, _ = _ln_stats(ALPHA * x0 + (1.0 + g1) * y)
        x1 = xh * gam + bet
        return x1, _modulate(x1, sc2, sh2)

    return _rowk("ln1", fn, s, 512, [(x0, D, 0, 0), (y, D, 0, 0)], [g1, gam, bet, sc2, sh2], [(D, F32), (D, BF16)], [])


def _ln2_loss(x1, ff, tgt, g2, gam, bet, s):
    def fn(pos, x1, ff, tgt, g2, gam, bet, a_loss, a_dgam, a_dbet, a_dg2):
        xh, rstd = _ln_stats(ALPHA * x1 + (1.0 + g2) * ff)
        err = xh * gam + bet - tgt
        dx2 = err * (1.0 / D)
        du = _ln_bwd(dx2, xh, rstd, gam)
        return (du, du * (1.0 + g2), a_loss + _colsum(err * err), a_dgam + _colsum(dx2 * xh),
                a_dbet + _colsum(dx2), a_dg2 + _colsum(du * ff))

    return _rowk("ln2_loss", fn, s, 512, [(x1, D, 0, 0), (ff, D, 0, 0), (tgt, D, 0, 0)], [g2, gam, bet],
                 [(D, F32), (D, BF16)], [(1, D)] * 4)


def _ln1_bwd(dh2, du2, x0, y, g1, gam, bet, sc2, s):
    def fn(pos, dh2, du2, x0, y, g1, gam, bet, sc2, a_sc, a_sh, a_gam, a_bet, a_g1):
        xh, rstd = _ln_stats(ALPHA * x0 + (1.0 + g1) * y)
        x1 = xh * gam + bet
        dx1 = ALPHA * du2 + dh2 * (1.0 + sc2)
        du1 = _ln_bwd(dx1, xh, rstd, gam)
        return (du1, du1 * (1.0 + g1), a_sc + _colsum(dh2 * x1), a_sh + _colsum(dh2), a_gam + _colsum(dx1 * xh),
                a_bet + _colsum(dx1), a_g1 + _colsum(du1 * y))

    return _rowk("ln1_bwd", fn, s, 512, [(dh2, D, 0, 0), (du2, D, 0, 0), (x0, D, 0, 0), (y, D, 0, 0)],
                 [g1, gam, bet, sc2], [(D, F32), (D, BF16)], [(1, D)] * 5)


def _input_grad(dh1, du1, x0, sc1, s):
    def fn(pos, dh1, du1, x0, sc1, a_sc, a_sh):
        return ALPHA * du1 + dh1 * (1.0 + sc1), a_sc + _colsum(dh1 * x0), a_sh + _colsum(dh1)

    return _rowk("input_grad", fn, s, 512, [(dh1, D, 0, 0), (du1, D, 0, 0), (x0, D, 0, 0)], [sc1],
                 [(D, F32)], [(1, D)] * 2)


def _adamw_math(w, grad, m, v):
    m_new = ADAM_B1 * m + (1.0 - ADAM_B1) * grad
    v_new = ADAM_B2 * v + (1.0 - ADAM_B2) * (grad * grad)
    m_hat = m_new / (1.0 - ADAM_B1 ** ADAM_STEP)
    v_hat = v_new / (1.0 - ADAM_B2 ** ADAM_STEP)
    return -ADAM_LR * (m_hat / (jnp.sqrt(v_hat) + ADAM_EPS) + ADAM_WD * w), m_new, v_new


def _small_update(small_all, layout, w, m, v):
    names = [n for n, _, _ in layout]

    def body(*refs):
        all_ref = refs[0]
        w_refs, m_refs, v_refs = [refs[1 + k * len(names):1 + (k + 1) * len(names)] for k in range(3)]
        sum_ref = refs[1 + 3 * len(names)]
        outs = refs[2 + 3 * len(names):]
        total = all_ref[0]
        for k in range(1, N_DEV):
            total = total + all_ref[k]
        sum_ref[...] = total
        for i, (_, off, size) in enumerate(layout):
            grad = total[:, off:off + size]
            delta, m_new, v_new = _adamw_math(w_refs[i][...], grad, m_refs[i][...], v_refs[i][...])
            for o, val in zip(outs[4 * i:4 * i + 4], (grad, delta, m_new, v_new)):
                o[...] = val

    res = pl.pallas_call(
        body, name="small_update",
        out_shape=[jax.ShapeDtypeStruct(small_all.shape[1:], F32)]
        + [jax.ShapeDtypeStruct(w[n].shape, F32) for n in names for _ in range(4)],
        compiler_params=_params(None),
    )(small_all, *[w[n] for n in names], *[m[n] for n in names], *[v[n] for n in names])
    return res[0], {n: res[1 + 4 * i:5 + 4 * i] for i, n in enumerate(names)}


def _adamw(name, w, g, m, v, *, tr, slots, by_columns=False):
    r, c = w.shape

    def body(w_ref, g_ref, m_ref, v_ref, g_out, d_out, m_out, v_out):
        if slots:
            grad = g_ref[0].astype(F32)
            for k in range(1, N_DEV):
                grad = grad + g_ref[k].astype(F32)
        else:
            grad = g_ref[...]
        g_out[...] = grad
        d_out[...], m_out[...], v_out[...] = _adamw_math(w_ref[...], grad, m_ref[...], v_ref[...])

    if by_columns:
        tile = pl.BlockSpec((r, tr), lambda i: (0, i))
        g_spec = pl.BlockSpec((N_DEV, r, tr), lambda i: (0, 0, i)) if slots else tile
    else:
        tile = pl.BlockSpec((tr, c), lambda i: (i, 0))
        g_spec = pl.BlockSpec((N_DEV, tr, c), lambda i: (0, i, 0)) if slots else tile
    return pl.pallas_call(
        body, name=name, grid=((c if by_columns else r) // tr,),
        in_specs=[tile, g_spec, tile, tile], out_specs=[tile] * 4,
        out_shape=[jax.ShapeDtypeStruct((r, c), F32)] * 4,
        compiler_params=_params(("parallel",)),
    )(w, g, m, v)


def _dot_f32(a, b, dims=NN):
    a0, a1, a2 = _split3(a)
    b0, b1, b2 = _split3(b)
    acc = _dot(a0, b0, dims)
    for x, y in ((a0, b1), (a1, b0), (a1, b1), (a0, b2), (a2, b0)):
        acc = acc + _dot(x, y, dims)
    return acc


def _ada_mod(c_all, w_shard, b_shard):
    def body(c_ref, w_ref, b_ref, o_ref):
        act = _silu(c_ref[...])
        act16 = jnp.concatenate([act, jnp.zeros_like(act)], axis=0)
        o_ref[...] = _dot_f32(act16, w_ref[...])[0:N_DEV] + b_ref[...]

    return pl.pallas_call(
        body, name="ada_mod", out_shape=jax.ShapeDtypeStruct((N_DEV, w_shard.shape[1]), F32),
        compiler_params=_params(None),
    )(c_all, w_shard, b_shard)


def _ada_grad(c_all, dmod_cols):
    def body(c_ref, dc_ref, gw_ref):
        act = _silu(c_ref[...])
        act16 = jnp.concatenate([act, jnp.zeros_like(act)], axis=0)
        dm = dc_ref[...]
        dm16 = jnp.concatenate([dm, jnp.zeros_like(dm)], axis=0)
        gw_ref[...] = _dot_f32(act16, dm16, TN)

    return pl.pallas_call(
        body, name="ada_grad", out_shape=jax.ShapeDtypeStruct((D, dmod_cols.shape[1]), F32),
        compiler_params=_params(None),
    )(c_all, dmod_cols)


def _exchange(name, xs, scatter):
    n = len(xs)
    n_peer = N_DEV - 1

    def body(*refs):
        x_refs, o_refs = refs[:n], refs[n:2 * n]
        send_sems, recv_sems, local_sems = refs[2 * n:]
        mx, my, mc = lax.axis_index("x"), lax.axis_index("y"), lax.axis_index("c")
        me = 4 * mx + 2 * my + mc

        def src(a, slot):
            return x_refs[a].at[slot] if scatter else x_refs[a]

        own = [pltpu.make_async_copy(src(a, me), o_refs[a].at[me], local_sems.at[a]) for a in range(n)]
        for cp in own:
            cp.start()
        sends = []
        for d in range(1, N_DEV):
            px = 1 - mx if d & 4 else mx
            py = 1 - my if d & 2 else my
            pc = 1 - mc if d & 1 else mc
            peer = 4 * px + 2 * py + pc
            for a in range(n):
                def copy(src_slot, dst_slot, a=a, d=d, to=(px, py, pc)):
                    return pltpu.make_async_remote_copy(
                        src_ref=src(a, src_slot), dst_ref=o_refs[a].at[dst_slot],
                        send_sem=send_sems.at[a * n_peer + d - 1], recv_sem=recv_sems.at[a * n_peer + d - 1],
                        device_id=to, device_id_type=pl.DeviceIdType.MESH)

                out = copy(peer, me)
                out.start()
                sends.append((out, copy(me, peer)))
        for _, arrival in sends:
            arrival.wait_recv()
        for out, _ in sends:
            out.wait_send()
        for cp in own:
            cp.wait()

    shapes = [tuple(x.shape[1:] if scatter else x.shape) for x in xs]
    return pl.pallas_call(
        body, name=name,
        in_specs=[pl.BlockSpec(memory_space=pl.ANY)] * n, out_specs=[pl.BlockSpec(memory_space=pl.ANY)] * n,
        out_shape=[jax.ShapeDtypeStruct((N_DEV,) + sh, x.dtype) for sh, x in zip(shapes, xs)],
        scratch_shapes=[pltpu.SemaphoreType.DMA((n * n_peer,)), pltpu.SemaphoreType.DMA((n * n_peer,)),
                        pltpu.SemaphoreType.DMA((n,))],
        compiler_params=pltpu.CompilerParams(has_side_effects=True),
    )(*xs)


def _gather_two_level(name, x):
    def body(x_ref, o_ref, send_sems, recv_sems, local_sem):
        mx, my, mc = lax.axis_index("x"), lax.axis_index("y"), lax.axis_index("c")
        me, sibling = (mx, my, mc), (mx, my, 1 - mc)
        chips = [(1 - mx, my), (mx, 1 - my), (1 - mx, 1 - my)]

        def slot(px, py, pc):
            return o_ref.at[4 * px + 2 * py + pc]

        def copy(k, block, to, src=None):
            return pltpu.make_async_remote_copy(
                src_ref=slot(*block) if src is None else src, dst_ref=slot(*block),
                send_sem=send_sems.at[k], recv_sem=recv_sems.at[k], device_id=to, device_id_type=pl.DeviceIdType.MESH)

        mine = pltpu.make_async_copy(x_ref, slot(*me), local_sem)
        mine.start()
        first = [copy(0, me, sibling, src=x_ref)] + [copy(1 + i, me, (*chip, mc), src=x_ref) for i, chip in enumerate(chips)]
        for cp in first:
            cp.start()
        passed = [copy(4 + i, (*chip, mc), sibling) for i, chip in enumerate(chips)]
        for i, chip in enumerate(chips):
            copy(1 + i, (*chip, mc), me).wait_recv()
            passed[i].start()
        copy(0, sibling, me).wait_recv()
        for i, chip in enumerate(chips):
            copy(4 + i, (*chip, 1 - mc), me).wait_recv()
        for cp in first + passed:
            cp.wait_send()
        mine.wait()

    return pl.pallas_call(
        body, name=name,
        in_specs=[pl.BlockSpec(memory_space=pl.ANY)], out_specs=pl.BlockSpec(memory_space=pl.ANY),
        out_shape=jax.ShapeDtypeStruct((N_DEV,) + tuple(x.shape), x.dtype),
        scratch_shapes=[pltpu.SemaphoreType.DMA((7,)), pltpu.SemaphoreType.DMA((7,)), pltpu.SemaphoreType.DMA(())],
        compiler_params=pltpu.CompilerParams(has_side_effects=True),
    )(x)


def _after(x, zero):
    return x if zero is None else x + zero.reshape(-1)[0].astype(x.dtype)


def _exchange_copies(x_refs, land_refs, send_sems, recv_sems, scatter):
    n = len(x_refs)
    n_peer = N_DEV - 1
    mx, my, mc = lax.axis_index("x"), lax.axis_index("y"), lax.axis_index("c")
    me = 4 * mx + 2 * my + mc
    pairs = []
    for d in range(1, N_DEV):
        px = 1 - mx if d & 4 else mx
        py = 1 - my if d & 2 else my
        pc = 1 - mc if d & 1 else mc
        peer = 4 * px + 2 * py + pc
        for a in range(n):
            def copy(src_slot, dst_slot, a=a, d=d, to=(px, py, pc)):
                return pltpu.make_async_remote_copy(
                    src_ref=x_refs[a].at[src_slot] if scatter else x_refs[a], dst_ref=land_refs[a].at[dst_slot],
                    send_sem=send_sems.at[a * n_peer + d - 1], recv_sem=recv_sems.at[a * n_peer + d - 1],
                    device_id=to, device_id_type=pl.DeviceIdType.MESH)

            pairs.append((copy(peer, me), copy(me, peer)))
    return me, pairs


def _exchange_async(name, xs, scatter, collective_id):
    n = len(xs)
    shapes = [tuple(x.shape[1:] if scatter else x.shape) for x in xs]
    x_refs = [jax.new_ref(x, memory_space=pltpu.MemorySpace.HBM) for x in xs]
    land_refs = [jax.empty_ref(jax.ShapeDtypeStruct((N_DEV,) + sh, x.dtype), memory_space=pltpu.MemorySpace.HBM)
                 for sh, x in zip(shapes, xs)]

    @pl.kernel(mesh=plsc.ScalarSubcoreMesh(axis_name="sequencer", num_cores=1), name=name,
               scratch_types=(pltpu.SemaphoreType.DMA((n * (N_DEV - 1),)), pltpu.SemaphoreType.DMA((n * (N_DEV - 1),)),
                              pltpu.SemaphoreType.DMA((n,))),
               compiler_params=pltpu.CompilerParams(collective_id=collective_id))
    def launch(send_sems, recv_sems, own_sems):
        barrier = pltpu.get_barrier_semaphore()
        mx, my, mc = lax.axis_index("x"), lax.axis_index("y"), lax.axis_index("c")
        for d in range(1, N_DEV):
            peer = (1 - mx if d & 4 else mx, 1 - my if d & 2 else my, 1 - mc if d & 1 else mc)
            pl.semaphore_signal(barrier, inc=1, device_id=peer, device_id_type=pl.DeviceIdType.MESH)
        pl.semaphore_wait(barrier, N_DEV - 1)
        me, pairs = _exchange_copies(x_refs, land_refs, send_sems, recv_sems, scatter)
        own = [pltpu.make_async_copy(x_refs[a].at[me] if scatter else x_refs[a], land_refs[a].at[me], own_sems.at[a])
               for a in range(n)]
        for cp in own:
            cp.start()
        for out, _ in pairs:
            out.start()
        for out, arrival in pairs:
            arrival.wait_recv()
            out.wait_send()
        for cp in own:
            cp.wait()

    launch()
    return lambda: [r[...] for r in land_refs]


def _relu2(a):
    r = jnp.maximum(a, 0.0)
    return r * r


def _relu2_grad(acc, r):
    return acc * (2.0 * jnp.sqrt(r.astype(F32)))


def _local_step(x0, tgt, mod, wcat_t, late_weights, send_grads, conv_w, conv_b, dt_bias, a_log, d_skip, ssm_norm_w, f_bias,
                attn_norm_w, ln1_g, ln1_b, ln2_g, ln2_b):
    ff_w = DFF // N_DEV
    s = x0.shape[0]
    tm = min(1024, s)
    ts = min(2048, s)
    sh1, sc1, g1, sh2, sc2, g2 = [mod[:, i * D:(i + 1) * D] for i in range(6)]
    zero = jnp.zeros((1, 128 - 2 * NH), F32)
    bias128 = jnp.concatenate([dt_bias, f_bias, zero], axis=1)
    alog128 = jnp.concatenate([a_log, jnp.zeros((1, 128 - NH), F32)], axis=1)
    dskip_x = jnp.repeat(d_skip, HD, axis=1)
    w_xs, w_bc, b_xs, b_bc = conv_w[:, :D], conv_w[:, D:], conv_b[:, :D], conv_b[:, D:]

    h1, = _rowk("modulate", lambda pos, x, sc, sh: (_modulate(x, sc, sh),), s, 512, [(x0, D, 0, 0)], [sc1, sh1], [(D, BF16)], [])
    p = _mm_nt("in_proj", [(h1, D, 0)], [(wcat_t, D, 0)], n=PCOLS, tm=tm, tn=1152, out_dtype=F32)
    xs_a, bc_a = _conv_fwd(p, w_xs, b_xs, w_bc, b_bc, s)
    y_ssd, states = _ssd_fwd(xs_a, bc_a, p, bias128, alog128, dskip_x, s)
    cum = _cum_fwd(p, bias128, s)
    att, lse = _attn_fwd(p, cum, s)
    wout, w1s, w2 = late_weights()
    ymix = _mix_norm(y_ssd, p, att, ssm_norm_w, attn_norm_w, s)
    y = _mm_nn("out_proj", ymix, wout, tm=tm, tn=1024, tk=2 * D, out_dtype=F32)
    x1, h2 = _ln1(x0, y, g1, ln1_g, ln1_b, sc2, sh2, s)
    tall = min(2048, s)
    r = _mm_nn("ff_in", h2, w1s, tm=tall, tn=2 * ff_w, tk=D, out_dtype=BF16, epi=_relu2)
    ff = _mm_nn("ff_out", r, w2, tm=tall, tn=1024, tk=1024, out_dtype=F32)
    du2, dff, sq_err, d_ln2_g, d_ln2_b, d_g2 = _ln2_loss(x1, ff, tgt, g2, ln2_g, ln2_b, s)

    da1 = _mm_nt("d_ff_hidden", [(dff, D, 0)], [(w2, D, 0)], n=DFF, tm=tm, tn=2048, out_dtype=BF16, epi=_relu2_grad,
                 epi_aux=(r,))
    d_w2 = _mm_tn("d_w_ff_out", r, dff, tm=1024, tn=1024, ts=ts)
    d_w1s = _mm_tn("d_w_ff_in", h2, da1, tm=1024, tn=2 * ff_w, ts=ts, col_shards=True)
    dh2 = _mm_nt("d_ff_input", [(da1, ff_w, k) for k in range(N_DEV)], [(w1s, ff_w, k) for k in range(N_DEV)], n=D,
                 tm=min(512, s), tn=1024, out_dtype=F32)
    du1, dy, d_sc2, d_sh2, d_ln1_g, d_ln1_b, d_g1 = _ln1_bwd(dh2, du2, x0, y, g1, ln1_g, ln1_b, sc2, s)

    dmix = _mm_nt("d_mix", [(dy, D, 0)], [(wout, D, 0)], n=2 * D, tm=tm, tn=1024, out_dtype=F32)
    d_wout = _mm_tn("d_w_out", ymix, dy, tm=1024, tn=1024, ts=ts)
    sent = send_grads("late", [d_w1s, d_w2.reshape(N_DEV, -1, D), d_wout.reshape(N_DEV, -1, D)])
    dy_ssd, dz, datt, d_ssm_w, d_attn_w = _mix_norm_bwd(dmix, y_ssd, p, att, _after(ssm_norm_w, sent), attn_norm_w, s)
    dq, dk, dv, dcs, drs = _attn_bwd(p, cum, att, lse, datt, s)
    dxs_a, dbc_a, ddt_raw, d_alog, d_dskip = _ssd_bwd(dy_ssd, xs_a, bc_a, p, states, bias128, alog128, dskip_x, s)
    dcum = jnp.pad((drs - dcs)[:, :2, :].reshape(NH, s).T, ((0, 0), (NH, 128 - 2 * NH)))
    ddtf, _, d_bias = _cum_bwd(dcum, ddt_raw, p, bias128, s)
    dxs, dbc, d_wc_xs, d_bc_xs, d_wc_bc, d_bc_bc = _conv_bwd(dxs_a, dbc_a, p, w_xs, b_xs, w_bc, b_bc, s)

    segs = [(dz, OFF_Z, D), (dxs, OFF_XS, D), (dq, OFF_Q, D), (dk, OFF_K, D), (dv, OFF_V, D), (dbc, OFF_BC, 512),
            (ddtf, OFF_DTF, 128)]
    d_z, d_xs, d_q, d_k, d_v, d_bcw, d_dtf = [
        _mm_tn("d_w_in_%d" % i, a, h1, tm=min(w, 1024), tn=1024, ts=ts)
        for i, (a, _, w) in enumerate(segs)]
    d_w_in_t = dict(z=d_z, xs=d_xs, bc=d_bcw, dt=d_dtf[:NH], q=d_q, k=d_k, v=d_v, f=d_dtf[NH:2 * NH])
    sent = send_grads("in", [_shard_w_in_grad_t(d_w_in_t)])
    segs[-1] = (_after(ddtf, sent), OFF_DTF, 128)
    dh1 = _mm_nt("d_h1", [(a, w, 0) for a, _, w in segs], [(wcat_t, w, off // w) for _, off, w in segs], n=D,
                 tm=min(512, s), tn=1024, out_dtype=F32, b_rows=True)
    grad_x, d_sc1, d_sh1 = _input_grad(dh1, du1, x0, sc1, s)

    return dict(
        loss=(0.5 / D) * jnp.sum(sq_err), grad_x=grad_x,
        d_mod=jnp.concatenate([d_sh1, d_sc1, d_g1, d_sh2, d_sc2, d_g2], axis=1),
        d_conv_w=jnp.concatenate([d_wc_xs[:4], d_wc_bc[:4]], axis=1), d_conv_b=jnp.concatenate([d_bc_xs, d_bc_bc], axis=1),
        d_ssm_norm_w=d_ssm_w, d_attn_norm_w=d_attn_w, d_ln1_g=d_ln1_g, d_ln1_b=d_ln1_b, d_ln2_g=d_ln2_g, d_ln2_b=d_ln2_b,
        d_gate_bias=d_bias, d_a_log=d_alog, d_d_skip=d_dskip)


W_IN_SEGS = [('z', W_Z, D), ('xs', W_XS, D), ('bc', W_BC, 512), ('dt', W_DT, NH), ('q', W_Q, D), ('k', W_K, D),
             ('v', W_V, D), ('f', W_F, NH)]
SHARD_W = IN_COLS // N_DEV


def _pack_w_in_t(w_in_t):
    seg = {n: w_in_t[off:off + w] for n, off, w in W_IN_SEGS}
    return jnp.concatenate([seg['z'], seg['xs'], seg['q'], seg['k'], seg['v'], seg['bc'], seg['dt'], seg['f'],
                            jnp.zeros((128 - 2 * NH, D), w_in_t.dtype)], axis=0)


def _shard_w_in_grad_t(d_w_in_t):
    blocks = []
    for dev in range(N_DEV):
        lo, hi = dev * SHARD_W, (dev + 1) * SHARD_W
        pieces = [d_w_in_t[n][max(lo, off) - off:min(hi, off + w) - off] for n, off, w in W_IN_SEGS
                  if max(lo, off) < min(hi, off + w)]
        blocks.append(jnp.concatenate(pieces, axis=0))
    return jnp.stack(blocks, axis=0)


WEIGHTS = ['w_ada', 'b_ada', 'w_in', 'conv_w', 'conv_b', 'dt_bias', 'a_log', 'd_skip', 'ssm_norm_w', 'f_bias',
           'attn_norm_w', 'w_out', 'ln1_g', 'ln1_b', 'w_ff_in', 'w_ff_out', 'ln2_g', 'ln2_b']
BIG = ['w_in', 'w_out', 'w_ff_in', 'w_ff_out']
SMALL_LAYOUT = [('b_ada', 0, 6 * D), ('conv_b', 12288, 1536), ('ssm_norm_w', 13824, D), ('attn_norm_w', 14848, D),
                ('ln1_g', 15872, D), ('ln1_b', 16896, D), ('ln2_g', 17920, D), ('ln2_b', 18944, D),
                ('dt_bias', 19968, NH), ('f_bias', 19968 + NH, NH), ('a_log', 20096, NH), ('d_skip', 20224, NH)]
SMALL_LOSS_LANE = 20352


def _pad_lanes(v, n=128):
    return jnp.pad(v, ((0, 0), (0, n - v.shape[1])))


def kernel(x, c, w_ada, b_ada, w_in, conv_w, conv_b, dt_bias, a_log, d_skip, ssm_norm_w, f_bias, attn_norm_w, w_out, ln1_g, ln1_b, w_ff_in, w_ff_out, ln2_g, ln2_b, loss_target, m_w_ada, m_b_ada, m_w_in, m_conv_w, m_conv_b, m_dt_bias, m_a_log, m_d_skip, m_ssm_norm_w, m_f_bias, m_attn_norm_w, m_w_out, m_ln1_g, m_ln1_b, m_w_ff_in, m_w_ff_out, m_ln2_g, m_ln2_b, v_w_ada, v_b_ada, v_w_in, v_conv_w, v_conv_b, v_dt_bias, v_a_log, v_d_skip, v_ssm_norm_w, v_f_bias, v_attn_norm_w, v_w_out, v_ln1_g, v_ln1_b, v_w_ff_in, v_w_ff_out, v_ln2_g, v_ln2_b):
    args = dict(locals())
    w = {n: args[n] for n in WEIGHTS}
    m = {n: args['m_' + n] for n in WEIGHTS}
    v = {n: args['v_' + n] for n in WEIGHTS}
    me = 4 * lax.axis_index("x") + 2 * lax.axis_index("y") + lax.axis_index("c")
    ada_cols = 6 * D // N_DEV
    conv_cols = conv_w.shape[2]

    c_all, conv_all = _exchange("gather_cond", [c, conv_w[0]], False)
    c_all = c_all.reshape(N_DEV, D)
    conv_w_full = conv_all.transpose(1, 0, 2).reshape(4, N_DEV * conv_cols)
    b_shard = lax.dynamic_slice(b_ada, (0, me * ada_cols), (1, ada_cols))
    mod_all, = _exchange("gather_mod", [_ada_mod(c_all, w_ada[0], b_shard)], False)
    mod = lax.dynamic_index_in_dim(mod_all, me, axis=1, keepdims=False).reshape(1, 6 * D)

    w_in_t = _after(jnp.swapaxes(w_in[0], 0, 1).astype(BF16), mod * 0)
    win_s = _gather_two_level("gather_w_in", w_in_t)
    first_done = win_s[0, 0:1, 0:1] * 0
    rest = _exchange_async("gather_rest", [_after(w[n][0].astype(BF16), first_done) for n in BIG[1:]], False, 1)

    def late_weights():
        wout_s, w1s, w2_s = rest()
        return wout_s.reshape(2 * D, D), w1s, w2_s.reshape(DFF, D)

    sends = {}

    def send_grads(tag, blocks):
        sends[tag] = _exchange_async("scatter_" + tag, blocks, True, {'late': 2, 'in': 3}[tag])
        return sum(b.reshape(-1)[0].astype(F32) * 0 for b in blocks)

    out = _local_step(x[0], loss_target[0], mod, _pack_w_in_t(win_s.reshape(IN_COLS, D)), late_weights, send_grads,
                      conv_w_full, conv_b, dt_bias, a_log, d_skip, ssm_norm_w, f_bias, attn_norm_w, ln1_g, ln1_b, ln2_g, ln2_b)

    small = jnp.concatenate(
        [out['d_mod'], out['d_conv_w'].reshape(1, -1), out['d_conv_b'], out['d_ssm_norm_w'], out['d_attn_norm_w'],
         out['d_ln1_g'], out['d_ln1_b'], out['d_ln2_g'], out['d_ln2_b'], out['d_gate_bias'], out['d_a_log'],
         out['d_d_skip'], _pad_lanes(out['loss'].reshape(1, 1))], axis=1)
    small_landed = _exchange_async("gather_small", [small], False, 4)
    (g_ff_in, g_ff_out, g_out), (g_in,) = sends['late'](), sends['in']()
    g_parts = dict(w_ff_in=g_ff_in, w_ff_out=g_ff_out, w_out=g_out, w_in=g_in)
    big = {n: _adamw("adamw_" + n, w[n][0], g_parts[n], m[n][0], v[n][0], tr=256, slots=True) for n in BIG[1:]}
    t = lambda a: jnp.swapaxes(a[0], 0, 1)
    big['w_in'] = [jnp.swapaxes(r, 0, 1) for r in _adamw("adamw_w_in", t(w_in), g_parts['w_in'], t(m_w_in), t(v_w_in),
                                                         tr=256, slots=True, by_columns=True)]
    big_done = sum(big[n][1][0:1, 0:1] * 0 for n in BIG)
    small_all = _after(small_landed()[0], big_done)
    ssum, small_res = _small_update(small_all, SMALL_LAYOUT, w, m, v)
    dmod_all = small_all[:, 0, :6 * D]
    g_w_ada = _ada_grad(c_all, lax.dynamic_slice(dmod_all, (0, me * ada_cols), (N_DEV, ada_cols)))
    ada = _adamw("adamw_ada", w_ada[0], g_w_ada, m_w_ada[0], v_w_ada[0], tr=256, slots=False)
    g_conv_w = lax.dynamic_slice(ssum[:, 6 * D:6 * D + 4 * N_DEV * conv_cols].reshape(4, N_DEV * conv_cols),
                                 (0, me * conv_cols), (4, conv_cols))
    conv = _adamw("adamw_conv_w", conv_w[0], g_conv_w, m_conv_w[0], v_conv_w[0], tr=4, slots=False)

    results = []
    for k in range(4):
        vals = {n: small_res[n][k] for n in small_res}
        vals['w_ada'], vals['conv_w'] = ada[k][None], conv[k][None]
        for n in BIG:
            vals[n] = big[n][k][None]
        results.append(vals)
    return (ssum[0, SMALL_LOSS_LANE], out['grad_x'][None], *[res[n] for res in results for n in WEIGHTS])
```

```python
import functools

import jax
import jax.numpy as jnp
from jax import lax
from jax.experimental import pallas as pl
from jax.experimental.pallas import tpu as pltpu
from jax.experimental.pallas import tpu_sc as plsc

F32, BF16 = jnp.float32, jnp.bfloat16

N_DEV = 8
D = 1024
NH, HD = 16, 64
NSTATE = 128
CHUNK = 128
HG = 8
DFF = 4096
ALPHA = 2.0 ** 0.25
EPS = 1e-5
ATT_SCALE = HD ** -0.5

OFF_Z, OFF_XS, OFF_Q, OFF_K, OFF_V, OFF_BC, OFF_DTF = 0, 1024, 2048, 3072, 4096, 5120, 5632
PCOLS = 5760
W_Z, W_XS, W_BC, W_DT, W_Q, W_K, W_V, W_F = 0, 1024, 2048, 2560, 2576, 3600, 4624, 5648
IN_COLS = 5664

ADAM_LR, ADAM_B1, ADAM_B2, ADAM_EPS, ADAM_WD, ADAM_STEP = 0.001, 0.9, 0.999, 1e-08, 0.01, 10

VMEM_LIMIT = 56 << 20

NN = (((1,), (0,)), ((), ()))
NT = (((1,), (1,)), ((), ()))
TN = (((0,), (0,)), ((), ()))


def _dot(a, b, dims=NN):
    return lax.dot_general(a, b, dims, preferred_element_type=F32)


def _bdot(a, b, dims=NN):
    return _dot(a.astype(BF16), b.astype(BF16), dims)


def _split3(v, terms=3):
    parts, rest = [], v
    for _ in range(terms):
        p = rest.astype(BF16)
        parts.append(p)
        rest = rest - p.astype(F32)
    return parts


def _sel_left(m01, v):
    return sum(_dot(m01, p) for p in _split3(v))


def _sel_right(v, m01, dims=NN, terms=3):
    return sum(_dot(p, m01, dims) for p in _split3(v, terms))


def _iota(shape, dim):
    return lax.broadcasted_iota(jnp.int32, shape, dim)


def _tri_lower(n):
    return (_iota((n, n), 1) <= _iota((n, n), 0)).astype(BF16)


def _tri_upper(n):
    return (_iota((n, n), 1) >= _iota((n, n), 0)).astype(BF16)


def _head_expand():
    return (lax.shift_right_logical(_iota((128, D), 1), 6) == _iota((128, D), 0)).astype(BF16)


def _head_reduce():
    return (lax.shift_right_logical(_iota((D, 128), 0), 6) == _iota((D, 128), 1)).astype(BF16)


def _sigmoid(x):
    return 1.0 / (1.0 + jnp.exp(-x))


def _silu(x):
    return x * _sigmoid(x)


def _dsilu(x):
    s = _sigmoid(x)
    return s * (1.0 + x * (1.0 - s))


def _softplus(x):
    return jnp.maximum(x, 0.0) + jnp.log(1.0 + jnp.exp(-jnp.abs(x)))


def _log_sigmoid(x):
    return jnp.minimum(x, 0.0) - jnp.log(1.0 + jnp.exp(-jnp.abs(x)))


def _params(sem):
    return pltpu.CompilerParams(dimension_semantics=sem, vmem_limit_bytes=VMEM_LIMIT)


def _mm_nn(name, a, b, *, tm, tn, tk, out_dtype, pro=None, aux=(), epi=None):
    m, k_all = a.shape
    b_sharded = b.ndim == 3
    n = b.shape[0] * b.shape[2] if b_sharded else b.shape[1]
    per_tile = tn // b.shape[2] if b_sharded else 1
    assert not b_sharded or tn == per_tile * b.shape[2]
    nk = k_all // tk
    n_aux = len(aux)
    b_spec = (pl.BlockSpec((per_tile, tk, b.shape[2]), lambda i, j, k: (j, k, 0)) if b_sharded
              else pl.BlockSpec((tk, tn), lambda i, j, k: (k, j)))

    def body(a_ref, b_ref, *rest):
        aux_refs, o_ref = rest[:n_aux], rest[n_aux]
        at = a_ref[...]
        if pro is not None:
            at = pro(at, *[r[...] for r in aux_refs])
        if b_sharded:
            part = jnp.concatenate([_bdot(at, b_ref[q]) for q in range(per_tile)], axis=1)
        else:
            part = _bdot(at, b_ref[...])
        if nk == 1:
            o_ref[...] = (part if epi is None else epi(part)).astype(out_dtype)
            return
        assert epi is None
        acc_ref = rest[n_aux + 1]
        kk = pl.program_id(2)

        @pl.when(kk == 0)
        def _():
            acc_ref[...] = part

        @pl.when(kk > 0)
        def _():
            acc_ref[...] += part

        @pl.when(kk == nk - 1)
        def _():
            o_ref[...] = acc_ref[...].astype(out_dtype)

    return pl.pallas_call(
        body, name=name,
        grid=(m // tm, n // tn, nk),
        in_specs=[pl.BlockSpec((tm, tk), lambda i, j, k: (i, k)), b_spec]
        + [pl.BlockSpec((1, tk), lambda i, j, k: (0, k)) for _ in aux],
        out_specs=pl.BlockSpec((tm, tn), lambda i, j, k: (i, j)),
        out_shape=jax.ShapeDtypeStruct((m, n), out_dtype),
        scratch_shapes=[] if nk == 1 else [pltpu.VMEM((tm, tn), F32)],
        compiler_params=_params(("parallel", "parallel", "arbitrary")),
    )(a, b, *aux)


def _mm_nt(name, a_list, b_list, *, n, tm, tn, out_dtype, epi=None, epi_aux=(), b_rows=False):
    m = a_list[0][0].shape[0]
    n_op = len(a_list)
    n_epi = len(epi_aux)
    dims = NN if b_rows else NT

    def body(*refs):
        a_refs, b_refs = refs[:n_op], refs[n_op:2 * n_op]
        e_refs, o_ref = refs[2 * n_op:2 * n_op + n_epi], refs[2 * n_op + n_epi]
        acc = None
        for a_ref, b_ref in zip(a_refs, b_refs):
            part = _bdot(a_ref[...], b_ref[...], dims)
            acc = part if acc is None else acc + part
        if epi is not None:
            acc = epi(acc, *[r[...] for r in e_refs])
        o_ref[...] = acc.astype(out_dtype)

    in_specs = [pl.BlockSpec((tm, w), functools.partial(lambda i, j, cb: (i, cb), cb=cb)) for (_, w, cb) in a_list]
    for (b, w, cb) in b_list:
        if b_rows:
            in_specs.append(pl.BlockSpec((w, tn), functools.partial(lambda i, j, cb: (cb, j), cb=cb)))
        elif b.ndim == 3:
            in_specs.append(pl.BlockSpec((None, tn, w), functools.partial(lambda i, j, cb: (cb, j, 0), cb=cb)))
        else:
            in_specs.append(pl.BlockSpec((tn, w), functools.partial(lambda i, j, cb: (j, cb), cb=cb)))
    in_specs += [pl.BlockSpec((tm, tn), lambda i, j: (i, j)) for _ in epi_aux]
    return pl.pallas_call(
        body, name=name,
        grid=(m // tm, n // tn),
        in_specs=in_specs,
        out_specs=pl.BlockSpec((tm, tn), lambda i, j: (i, j)),
        out_shape=jax.ShapeDtypeStruct((m, n), out_dtype),
        compiler_params=_params(("parallel", "parallel")),
    )(*[a for (a, _, _) in a_list], *[b for (b, _, _) in b_list], *epi_aux)


def _mm_tn(name, a, b, *, tm, tn, ts, pro=None, aux=(), col_shards=False):
    s_all, ka = a.shape
    nb = b.shape[1]
    n_aux = len(aux)
    ns = s_all // ts
    shard_w = nb // N_DEV
    per_tile = tn // shard_w
    assert not col_shards or tn == per_tile * shard_w

    def body(a_ref, b_ref, *rest):
        aux_refs, o_ref, acc_ref = rest[:n_aux], rest[n_aux], rest[n_aux + 1]
        at = a_ref[...]
        if pro is not None:
            at = pro(at, *[r[...] for r in aux_refs])
        part = _bdot(at, b_ref[...], TN)
        ss = pl.program_id(2)

        @pl.when(ss == 0)
        def _():
            acc_ref[...] = part

        @pl.when(ss > 0)
        def _():
            acc_ref[...] += part

        @pl.when(ss == ns - 1)
        def _():
            if col_shards:
                for q in range(per_tile):
                    o_ref[q] = acc_ref[:, q * shard_w:(q + 1) * shard_w].astype(BF16)
            else:
                o_ref[...] = acc_ref[...].astype(BF16)

    if col_shards:
        out_spec = pl.BlockSpec((per_tile, tm, shard_w), lambda i, j, s: (j, i, 0))
        out_shape = jax.ShapeDtypeStruct((N_DEV, ka, shard_w), BF16)
    else:
        out_spec = pl.BlockSpec((tm, tn), lambda i, j, s: (i, j))
        out_shape = jax.ShapeDtypeStruct((ka, nb), BF16)
    return pl.pallas_call(
        body, name=name,
        grid=(ka // tm, nb // tn, ns),
        in_specs=[pl.BlockSpec((ts, tm), lambda i, j, s: (s, i)),
                  pl.BlockSpec((ts, tn), lambda i, j, s: (s, j))]
        + [pl.BlockSpec((1, tm), lambda i, j, s: (0, i)) for _ in aux],
        out_specs=out_spec, out_shape=out_shape,
        scratch_shapes=[pltpu.VMEM((tm, tn), F32)],
        compiler_params=_params(("parallel", "parallel", "arbitrary")),
    )(a, b, *aux)


def _rowk(name, fn, n_rows, tr, rows, fulls, outs, accs, reverse=False):
    n = n_rows // tr
    n_row, n_full, n_out, n_acc = len(rows), len(fulls), len(outs), len(accs)

    def pos(i):
        return (n - 1 - i) if reverse else i

    def body(*refs):
        row_refs = refs[:n_row]
        full_refs = refs[n_row:n_row + n_full]
        out_refs = refs[n_row + n_full:n_row + n_full + n_out]
        acc_refs = refs[n_row + n_full + n_out:]
        i = pl.program_id(0)

        @pl.when(i == 0)
        def _():
            for r in acc_refs:
                r[...] = jnp.zeros(r.shape, r.dtype)

        res = fn(pos(i), *[r[...] for r in row_refs], *[r[...] for r in full_refs], *[r[...] for r in acc_refs])
        for r, v in zip(out_refs + acc_refs, res):
            r[...] = v.astype(r.dtype)

    def row_map(i, cb, shift):
        return (jnp.clip(pos(i) + shift, 0, n - 1), cb)

    def halo_map(i, cb, shift):
        tile = jnp.clip(pos(i) + shift, 0, n - 1)
        return (tile * (tr // 8) + (tr // 8 - 1 if shift < 0 else 0), cb)

    in_specs = [pl.BlockSpec((tr, w), functools.partial(row_map, cb=cb, shift=sh)) if sh == 0 else
                pl.BlockSpec((8, w), functools.partial(halo_map, cb=cb, shift=sh)) for (_, w, cb, sh) in rows]
    in_specs += [pl.BlockSpec(f.shape, functools.partial(lambda i, nd: (0,) * nd, nd=f.ndim)) for f in fulls]
    out_specs = [pl.BlockSpec((tr, w), lambda i: (pos(i), 0)) for (w, _) in outs]
    out_specs += [pl.BlockSpec((r, w), lambda i: (0, 0)) for (r, w) in accs]
    out_shape = [jax.ShapeDtypeStruct((n_rows, w), dt) for (w, dt) in outs]
    out_shape += [jax.ShapeDtypeStruct((r, w), F32) for (r, w) in accs]
    return pl.pallas_call(
        body, name=name, grid=(n,), in_specs=in_specs, out_specs=out_specs, out_shape=out_shape,
        compiler_params=_params(("arbitrary",)),
    )(*[a for (a, _, _, _) in rows], *fulls)


def _colsum(x):
    return jnp.sum(x, axis=0, keepdims=True)


def _mean(x):
    return jnp.mean(x, axis=-1, keepdims=True)


def _modulate(x, sc, sh):
    return x * (1.0 + sc) + sh


def _shift_down(cur, prev8, j):
    tr = cur.shape[0]
    row8 = _iota(prev8.shape, 0)
    head = jnp.where(row8 < j, pltpu.roll(prev8, j, 0), pltpu.roll(cur[0:8], j, 0))
    return head if tr == 8 else jnp.concatenate([head, pltpu.roll(cur, j, 0)[8:]], axis=0)


def _shift_up(cur, next8, j):
    tr = cur.shape[0]
    row8 = _iota(next8.shape, 0)
    tail = jnp.where(row8 < 8 - j, pltpu.roll(cur[tr - 8:], 8 - j, 0), pltpu.roll(next8, 8 - j, 0))
    return jnp.concatenate([pltpu.roll(cur, tr - j, 0)[:tr - 8], tail], axis=0)


def _conv(cur, prev, w, b):
    out = cur * w[3:4] + b
    for j in (1, 2, 3):
        out = out + _shift_down(cur, prev, j) * w[3 - j:4 - j]
    return out


def _conv_fwd(p, w_xs, b_xs, w_bc, b_bc, s):
    def fn(pos, xs, xs_prev, bc, bc_prev, w_xs, b_xs, w_bc, b_bc):
        first = pos == 0
        xs_prev = jnp.where(first, 0.0, xs_prev)
        bc_prev = jnp.where(first, 0.0, bc_prev)
        return _silu(_conv(xs, xs_prev, w_xs, b_xs)), _silu(_conv(bc, bc_prev, w_bc, b_bc))

    return _rowk("conv_fwd", fn, s, 256,
                 [(p, D, OFF_XS // D, 0), (p, D, OFF_XS // D, -1), (p, 512, OFF_BC // 512, 0), (p, 512, OFF_BC // 512, -1)],
                 [w_xs, b_xs, w_bc, b_bc], [(D, F32), (512, F32)], [])


def _conv_bwd(dxs_a, dbc_a, p, w_xs, b_xs, w_bc, b_bc, s):
    tr = 256
    n = s // tr

    def fn(pos, da1, da1n, x1, x1p, x1n, da2, da2n, x2, x2p, x2n, w1, b1, w2, b2, aw1, ab1, aw2, ab2):
        dx1, dw1, db1 = _conv_bwd_fn(pos, n, da1, da1n, x1, x1p, x1n, w1, b1)
        dx2, dw2, db2 = _conv_bwd_fn(pos, n, da2, da2n, x2, x2p, x2n, w2, b2)
        return dx1, dx2, aw1 + dw1, ab1 + db1, aw2 + dw2, ab2 + db2

    cx, cb = OFF_XS // D, OFF_BC // 512
    return _rowk("conv_bwd", fn, s, tr,
                 [(dxs_a, D, 0, 0), (dxs_a, D, 0, 1), (p, D, cx, 0), (p, D, cx, -1), (p, D, cx, 1),
                  (dbc_a, 512, 0, 0), (dbc_a, 512, 0, 1), (p, 512, cb, 0), (p, 512, cb, -1), (p, 512, cb, 1)],
                 [w_xs, b_xs, w_bc, b_bc], [(D, BF16), (512, BF16)], [(8, D), (1, D), (8, 512), (1, 512)])


def _conv_bwd_fn(pos, n, da, da_next, x, x_prev, x_next, w, b):
    first, last = pos == 0, pos == n - 1
    x_prev = jnp.where(first, 0.0, x_prev)
    shifted = {j: _shift_down(x, x_prev, j) for j in (1, 2, 3)}
    conv = x * w[3:4] + b
    for j in (1, 2, 3):
        conv = conv + shifted[j] * w[3 - j:4 - j]
    dc = da * _dsilu(conv)
    dc_next = jnp.where(last, 0.0, da_next * _dsilu(_conv(x_next, x[x.shape[0] - 8:], w, b)))
    dx = dc * w[3:4]
    dws = [None] * 4
    dws[3] = _colsum(dc * x)
    for j in (1, 2, 3):
        dx = dx + _shift_up(dc, dc_next, j) * w[3 - j:4 - j]
        dws[3 - j] = _colsum(dc * shifted[j])
    row = _iota((8, x.shape[1]), 0)
    dw = jnp.zeros((8, x.shape[1]), F32)
    for k in range(4):
        dw = jnp.where(row == k, dws[k], dw)
    return dx, dw, _colsum(dc)


def _ssd_gates(dtf, bias, a_log):
    lane = _iota(dtf.shape, 1)
    head = lane < NH
    dt = jnp.where(head, _softplus(dtf + bias), 0.0)
    a_neg = jnp.where(_iota(a_log.shape, 1) < NH, -jnp.exp(a_log), 0.0)
    a = dt * a_neg
    cs = _sel_left(_tri_lower(CHUNK), a)
    return dt, a_neg, cs


def _decay_mask(cs_ref, cst_ref, h):
    diff = cs_ref[:, h:h + 1] - cst_ref[h:h + 1, :]
    low = _iota((CHUNK, CHUNK), 1) <= _iota((CHUNK, CHUNK), 0)
    return jnp.where(low, jnp.exp(jnp.minimum(diff, 0.0)), 0.0)


def _ssd_fwd(xs_a, bc_a, p, bias128, alog128, dskip_x, s):
    nc = s // CHUNK
    t = CHUNK

    def body(xs_ref, bc_ref, dtf_ref, bias_ref, alog_ref, dsk_ref, y_ref, st_ref,
             state, x_sc, xw_sc, cs_sc, cst_sc, yd_sc):
        c = pl.program_id(0)

        @pl.when(c == 0)
        def _():
            state[...] = jnp.zeros(state.shape, F32)

        dt, _, cs = _ssd_gates(dtf_ref[...], bias_ref[...], alog_ref[...])
        cs_sc[...] = cs
        cst_sc[...] = cs.T
        cs_last = cs[t - 1:t, :]
        expand = _head_expand()
        ex = _sel_right(jnp.concatenate([dt, jnp.exp(cs), jnp.exp(cs_last - cs)], axis=0), expand, terms=2)
        dt_x, eo_x, we_x = ex[0:t], ex[t:2 * t], ex[2 * t:3 * t]
        g_x = _sel_right(jnp.broadcast_to(jnp.exp(cs_last), (8, 128)), expand)[0:1]
        xs = xs_ref[...]
        x = xs * dt_x
        x_sc[...] = x.astype(BF16)
        xw_sc[...] = (x * we_x).astype(BF16)
        prev = state[...]
        st_ref[0] = prev
        prev_b = prev.astype(BF16)
        for g in range(2):
            cols = slice(g * 512, (g + 1) * 512)
            b_g = bc_ref[:, g * 128:(g + 1) * 128].astype(BF16)
            c_g = bc_ref[:, 256 + g * 128:256 + (g + 1) * 128].astype(BF16)
            gmat = _dot(c_g, b_g, NT)
            y_off = _dot(c_g, prev_b[:, cols]) * eo_x[:, cols]
            s_loc = _dot(b_g, xw_sc[:, cols], TN)
            state[:, cols] = g_x[:, cols] * prev[:, cols] + s_loc
            for e in range(HG):
                h = g * HG + e
                m = gmat * _decay_mask(cs_sc, cst_sc, h)
                yd_sc[:, h * HD:(h + 1) * HD] = _dot(m.astype(BF16), x_sc[:, h * HD:(h + 1) * HD])
            y_ref[:, cols] = yd_sc[:, cols] + y_off + dsk_ref[:, cols] * xs[:, cols]

    return pl.pallas_call(
        body, name="ssd_fwd", grid=(nc,),
        in_specs=[pl.BlockSpec((t, D), lambda c: (c, 0)),
                  pl.BlockSpec((t, 512), lambda c: (c, 0)),
                  pl.BlockSpec((t, 128), lambda c: (c, OFF_DTF // 128)),
                  pl.BlockSpec((1, 128), lambda c: (0, 0)),
                  pl.BlockSpec((1, 128), lambda c: (0, 0)),
                  pl.BlockSpec((1, D), lambda c: (0, 0))],
        out_specs=[pl.BlockSpec((t, D), lambda c: (c, 0)),
                   pl.BlockSpec((1, NSTATE, D), lambda c: (c, 0, 0))],
        out_shape=[jax.ShapeDtypeStruct((s, D), F32), jax.ShapeDtypeStruct((nc, NSTATE, D), F32)],
        scratch_shapes=[pltpu.VMEM((NSTATE, D), F32), pltpu.VMEM((t, D), BF16), pltpu.VMEM((t, D), BF16),
                        pltpu.VMEM((t, 128), F32), pltpu.VMEM((128, t), F32), pltpu.VMEM((t, D), F32)],
        compiler_params=_params(("arbitrary",)),
    )(xs_a, bc_a, p, bias128, alog128, dskip_x)


def _ssd_bwd(dy, xs_a, bc_a, p, states, bias128, alog128, dskip_x, s):
    nc = s // CHUNK
    t = CHUNK

    def body(dy_ref, xs_ref, bc_ref, dtf_ref, st_ref, bias_ref, alog_ref, dsk_ref,
             dxs_ref, dbc_ref, ddt_ref, dalog_ref, dskip_ref,
             dstate, x_sc, dy_sc, dx_sc, deo_sc, dwe_sc, cs_sc, cst_sc, dcol_sc, drow_sc):
        i = pl.program_id(0)

        @pl.when(i == 0)
        def _():
            dstate[...] = jnp.zeros(dstate.shape, F32)
            dalog_ref[...] = jnp.zeros(dalog_ref.shape, F32)
            dskip_ref[...] = jnp.zeros(dskip_ref.shape, F32)

        dtf = dtf_ref[...]
        dt, a_neg, cs = _ssd_gates(dtf, bias_ref[...], alog_ref[...])
        cs_sc[...] = cs
        cst_sc[...] = cs.T
        cs_last = cs[t - 1:t, :]
        eo, we, g_end = jnp.exp(cs), jnp.exp(cs_last - cs), jnp.exp(cs_last)
        expand, reduce = _head_expand(), _head_reduce()
        ex = _sel_right(jnp.concatenate([dt, eo, we], axis=0), expand, terms=2)
        dt_x, eo_x, we_x = ex[0:t], ex[t:2 * t], ex[2 * t:3 * t]
        g_x = _sel_right(jnp.broadcast_to(g_end, (8, 128)), expand)[0:1]
        xs = xs_ref[...]
        dyv = dy_ref[...]
        x = xs * dt_x
        x_sc[...] = x.astype(BF16)
        dy_sc[...] = dyv.astype(BF16)
        dyo_b = (dyv * eo_x).astype(BF16)
        xw_b = (x * we_x).astype(BF16)
        prev = st_ref[0]
        prev_b = prev.astype(BF16)
        dnext = dstate[...]
        dnext_b = dnext.astype(BF16)
        dcol_sc[...] = jnp.zeros(dcol_sc.shape, F32)
        drow_sc[...] = jnp.zeros(drow_sc.shape, F32)
        lane_row = _iota((1, 128), 1)
        sub_col = _iota((128, 1), 0)
        for g in range(2):
            cols = slice(g * 512, (g + 1) * 512)
            b_g = bc_ref[:, g * 128:(g + 1) * 128].astype(BF16)
            c_g = bc_ref[:, 256 + g * 128:256 + (g + 1) * 128].astype(BF16)
            gmat = _dot(c_g, b_g, NT)
            b_ds = _dot(b_g, dnext_b[:, cols])
            c_s = _dot(c_g, prev_b[:, cols])
            dx_sc[:, cols] = b_ds * we_x[:, cols]
            deo_sc[:, cols] = dyv[:, cols] * c_s
            dwe_sc[:, cols] = b_ds * x[:, cols]
            db = _dot(xw_b[:, cols], dnext_b[:, cols], NT)
            dc = _dot(dyo_b[:, cols], prev_b[:, cols], NT)
            dstate[:, cols] = g_x[:, cols] * dnext[:, cols] + _dot(c_g, dyo_b[:, cols], TN)
            dg = jnp.zeros((t, t), F32)
            for e in range(HG):
                h = g * HG + e
                hc = slice(h * HD, (h + 1) * HD)
                lmat = _decay_mask(cs_sc, cst_sc, h)
                m = gmat * lmat
                dx_sc[:, hc] += _dot(m.astype(BF16), dy_sc[:, hc], TN)
                dm = _dot(dy_sc[:, hc], x_sc[:, hc], NT)
                dg = dg + dm * lmat
                qm = dm * m
                dcol_sc[...] += jnp.sum(qm, axis=1, keepdims=True) * (lane_row == h).astype(F32)
                drow_sc[...] += (sub_col == h).astype(F32) * jnp.sum(qm, axis=0, keepdims=True)
            dg_b = dg.astype(BF16)
            dbc_ref[:, g * 128:(g + 1) * 128] = db + _dot(dg_b, c_g, TN)
            dbc_ref[:, 256 + g * 128:256 + (g + 1) * 128] = dc + _dot(dg_b, b_g)
        d_eo = _sel_right(deo_sc[...], reduce, terms=2)
        d_we = _sel_right(dwe_sc[...], reduce, terms=2)
        d_gend = _sel_right(jnp.broadcast_to(_colsum(dnext * prev), (8, D)), reduce)[0:1]
        d_cs = dcol_sc[...] - drow_sc[...].T + d_eo * eo - d_we * we
        extra = _colsum(d_we * we) + d_gend * g_end
        d_cs = d_cs + jnp.where(_iota((t, 128), 0) == t - 1, extra, 0.0)
        da = _sel_left(_tri_upper(t), d_cs)
        dx = dx_sc[...]
        ddt = _sel_right(dx * xs, reduce, terms=2) + da * a_neg
        dxs_ref[...] = dx * dt_x + dsk_ref[...] * dyv
        ddt_ref[...] = jnp.where(_iota((t, 128), 1) < NH, ddt * _sigmoid(dtf + bias_ref[...]), 0.0)
        dalog_ref[...] += _colsum(da * dt) * a_neg
        dskip_ref[...] += _sel_right(jnp.broadcast_to(_colsum(dyv * xs), (8, D)), reduce)[0:1]

    rev = lambda i: nc - 1 - i
    return pl.pallas_call(
        body, name="ssd_bwd", grid=(nc,),
        in_specs=[pl.BlockSpec((t, D), lambda i: (rev(i), 0)),
                  pl.BlockSpec((t, D), lambda i: (rev(i), 0)),
                  pl.BlockSpec((t, 512), lambda i: (rev(i), 0)),
                  pl.BlockSpec((t, 128), lambda i: (rev(i), OFF_DTF // 128)),
                  pl.BlockSpec((1, NSTATE, D), lambda i: (rev(i), 0, 0)),
                  pl.BlockSpec((1, 128), lambda i: (0, 0)),
                  pl.BlockSpec((1, 128), lambda i: (0, 0)),
                  pl.BlockSpec((1, D), lambda i: (0, 0))],
        out_specs=[pl.BlockSpec((t, D), lambda i: (rev(i), 0)),
                   pl.BlockSpec((t, 512), lambda i: (rev(i), 0)),
                   pl.BlockSpec((t, 128), lambda i: (rev(i), 0)),
                   pl.BlockSpec((1, 128), lambda i: (0, 0)),
                   pl.BlockSpec((1, 128), lambda i: (0, 0))],
        out_shape=[jax.ShapeDtypeStruct((s, D), F32), jax.ShapeDtypeStruct((s, 512), F32),
                   jax.ShapeDtypeStruct((s, 128), F32), jax.ShapeDtypeStruct((1, 128), F32),
                   jax.ShapeDtypeStruct((1, 128), F32)],
        scratch_shapes=[pltpu.VMEM((NSTATE, D), F32), pltpu.VMEM((t, D), BF16), pltpu.VMEM((t, D), BF16),
                        pltpu.VMEM((t, D), F32), pltpu.VMEM((t, D), F32), pltpu.VMEM((t, D), F32),
                        pltpu.VMEM((t, 128), F32), pltpu.VMEM((128, t), F32),
                        pltpu.VMEM((t, 128), F32), pltpu.VMEM((128, t), F32)],
        compiler_params=_params(("arbitrary",)),
    )(dy, xs_a, bc_a, p, states, bias128, alog128, dskip_x)


def _gate_lanes(shape):
    lane = _iota(shape, 1)
    return (lane >= NH) & (lane < 2 * NH)


def _cum_fwd(p, bias128, s):
    tr = min(512, s)

    def body(dtf_ref, bias_ref, o_ref, carry):
        @pl.when(pl.program_id(0) == 0)
        def _():
            carry[...] = jnp.zeros(carry.shape, F32)

        lf = jnp.where(_gate_lanes((tr, 128)), _log_sigmoid(dtf_ref[...] + bias_ref[...]), 0.0)
        cum = _sel_left(_tri_lower(tr), lf) + carry[...]
        carry[...] = cum[tr - 1:tr, :]
        o_ref[...] = cum

    return pl.pallas_call(
        body, name="cum_fwd", grid=(s // tr,),
        in_specs=[pl.BlockSpec((tr, 128), lambda i: (i, OFF_DTF // 128)), pl.BlockSpec((1, 128), lambda i: (0, 0))],
        out_specs=pl.BlockSpec((tr, 128), lambda i: (i, 0)),
        out_shape=jax.ShapeDtypeStruct((s, 128), F32),
        scratch_shapes=[pltpu.VMEM((1, 128), F32)],
        compiler_params=_params(("arbitrary",)),
    )(p, bias128)


def _cum_bwd(dcum, ddt_raw, p, bias128, s):
    tr = min(512, s)

    def fn(pos, dcum, ddt, dtf, bias, carry, acc):
        suffix = _sel_left(_tri_upper(tr), dcum) + carry
        dfr = jnp.where(_gate_lanes((tr, 128)), suffix * _sigmoid(-(dtf + bias)), 0.0)
        out = ddt + dfr
        return out, suffix[0:1, :], acc + _colsum(out)

    return _rowk("cum_bwd", fn, s, tr, [(dcum, 128, 0, 0), (ddt_raw, 128, 0, 0), (p, 128, OFF_DTF // 128, 0)],
                 [bias128], [(128, BF16)], [(1, 128), (1, 128)], reverse=True)


ATT_BLOCK = 512
ATT_STRIP = 32


def _head_part(shape, h, dim):
    i = _iota(shape, dim)
    return (i >= h * HD) & (i < (h + 1) * HD)


def _k_augmented(k_blk, cum_blk, j, h):
    tk = k_blk.shape[0]
    lane = _iota((tk, 128), 1)
    col = jnp.sum(jnp.where(lane == NH + 2 * j + h, cum_blk, 0.0), axis=1, keepdims=True)
    c0, c1, c2 = [c.astype(F32) for c in _split3(-col)]
    k_h = k_blk if h == 0 else pltpu.roll(k_blk, HD, 1)
    aug = jnp.where(lane == HD, c0, jnp.where(lane == HD + 1, c1, jnp.where(lane == HD + 2, c2, 0.0)))
    return jnp.where(lane < HD, k_h, aug).astype(BF16)


def _q_augmented_t(q_blk):
    tq = q_blk.shape[0]
    q_t = (q_blk * ATT_SCALE).T.astype(BF16)
    ones = (_iota((HD, tq), 0) < 3).astype(BF16)
    return [jnp.concatenate([q_t[h * HD:(h + 1) * HD], ones], axis=0) for h in range(2)]


def _rows01(r0, r1):
    sub = _iota((8, r0.shape[1]), 0)
    return jnp.where(sub == 0, r0, jnp.where(sub == 1, r1, 0.0))


def _fold8(x, op, cur):
    for g in range(x.shape[0] // 8):
        cur = op(cur, x[8 * g:8 * (g + 1), :])
    return cur


def _attn_fwd(p, cum, s):
    t = min(ATT_BLOCK, s)
    nq = s // t
    r = ATT_STRIP

    def body(q_ref, k_ref, v_ref, c_ref, o_ref, lse_ref, kaug_sc, vt_sc, s0_sc, s1_sc, p0_sc, p1_sc, m_sc, l_sc, acc_sc):
        j, qi = pl.program_id(0), pl.program_id(1)
        s_sc, p_sc = (s0_sc, s1_sc), (p0_sc, p1_sc)

        @pl.when(qi == 0)
        def _():
            for c in range(nq):
                rows = slice(c * t, (c + 1) * t)
                k_blk, vt = k_ref[rows, :], v_ref[rows, :].T
                for h in range(2):
                    kaug_sc[h, rows, :] = _k_augmented(k_blk, c_ref[rows, :], j, h)
                    vt_sc[h, :, rows] = vt[h * HD:(h + 1) * HD].astype(BF16)

        qaug_t = _q_augmented_t(q_ref[...])
        m_sc[...] = jnp.full(m_sc.shape, -1e30, F32)
        l_sc[...] = jnp.zeros(l_sc.shape, F32)
        acc_sc[...] = jnp.zeros(acc_sc.shape, F32)
        top = _iota((128, t), 0) < HD

        def logits(kb, buf):
            kv = pl.ds(pl.multiple_of(kb * t, t), t)
            for h in range(2):
                s_sc[buf][h] = _dot(kaug_sc[h, kv, :], qaug_t[h])

        def softmax(buf, diagonal):
            alphas = []
            for h in range(2):
                cur = jnp.full((8, t), -1e30, F32)
                for i in range(t // r):
                    rows = slice(i * r, (i + 1) * r)
                    x = s_sc[buf][h, rows, :]
                    if diagonal:
                        x = jnp.where(_iota((r, t), 1) >= i * r + _iota((r, t), 0), x, -1e30)
                        s_sc[buf][h, rows, :] = x
                    cur = _fold8(x, jnp.maximum, cur)
                m_prev = m_sc[h, 0:1, :]
                m_new = jnp.maximum(m_prev, jnp.max(cur, axis=0, keepdims=True))
                alpha = jnp.exp(m_prev - m_new)
                m_sc[h, 0:1, :] = m_new
                alphas.append(alpha)
                tot = jnp.zeros((8, t), F32)
                for i in range(t // r):
                    rows = slice(i * r, (i + 1) * r)
                    pr = jnp.exp(s_sc[buf][h, rows, :] - m_new)
                    p_sc[buf][h, rows, :] = pr.astype(BF16)
                    tot = _fold8(pr, jnp.add, tot)
                l_sc[h, 0:1, :] = alpha * l_sc[h, 0:1, :] + jnp.sum(tot, axis=0, keepdims=True)
            return alphas

        def accumulate(kb, buf, alphas):
            kv = pl.ds(pl.multiple_of(kb * t, t), t)
            for h in range(2):
                part = slice(h * HD, (h + 1) * HD)
                acc_sc[part, :] = acc_sc[part, :] * alphas[h] + _dot(vt_sc[h, :, kv], p_sc[buf][h])

        def first_trip():
            logits(0, 1)
            accumulate(qi, 0, softmax(0, True))
            logits(jnp.minimum(1, qi - 1), 0)
            return tuple(softmax(1, False))

        def only_diagonal():
            accumulate(qi, 0, softmax(0, True))
            return (jnp.ones((1, t), F32),) * 2

        def steady(u, alphas_b):
            accumulate(2 * u - 2, 1, alphas_b)
            logits(2 * u, 1)
            accumulate(2 * u - 1, 0, softmax(0, False))
            logits(jnp.minimum(2 * u + 1, qi - 1), 0)
            return tuple(softmax(1, False))

        logits(qi, 0)
        n_blocks = qi + 1
        alphas_b = lax.cond(qi >= 1, first_trip, only_diagonal)
        alphas_b = lax.fori_loop(1, n_blocks // 2, steady, alphas_b)
        last_b = 2 * (n_blocks // 2) - 2

        @pl.when((qi >= 1) & (n_blocks % 2 == 0))
        def _():
            accumulate(last_b, 1, alphas_b)

        @pl.when((qi >= 2) & (n_blocks % 2 == 1))
        def _():
            accumulate(last_b, 1, alphas_b)
            accumulate(qi - 1, 0, softmax(0, False))

        l0, l1 = l_sc[0, 0:1, :], l_sc[1, 0:1, :]
        o_ref[...] = (acc_sc[...] / jnp.where(top, l0, l1)).T
        lse_ref[0] = _rows01(m_sc[0, 0:1, :] + jnp.log(l0), m_sc[1, 0:1, :] + jnp.log(l1))

    return pl.pallas_call(
        body, name="attn_fwd", grid=(NH // 2, nq),
        in_specs=[pl.BlockSpec((t, 128), lambda j, qi: (qi, OFF_Q // 128 + j)),
                  pl.BlockSpec((s, 128), lambda j, qi: (0, OFF_K // 128 + j)),
                  pl.BlockSpec((s, 128), lambda j, qi: (0, OFF_V // 128 + j)),
                  pl.BlockSpec((s, 128), lambda j, qi: (0, 0))],
        out_specs=[pl.BlockSpec((t, 128), lambda j, qi: (qi, j)),
                   pl.BlockSpec((1, 8, t), lambda j, qi: (j, 0, qi))],
        out_shape=[jax.ShapeDtypeStruct((s, D), F32), jax.ShapeDtypeStruct((NH // 2, 8, s), F32)],
        scratch_shapes=[pltpu.VMEM((2, s, 128), BF16), pltpu.VMEM((2, HD, s), BF16), pltpu.VMEM((2, t, t), F32),
                        pltpu.VMEM((2, t, t), F32), pltpu.VMEM((2, t, t), BF16), pltpu.VMEM((2, t, t), BF16),
                        pltpu.VMEM((2, 8, t), F32), pltpu.VMEM((2, 8, t), F32), pltpu.VMEM((128, t), F32)],
        compiler_params=_params(("parallel", "arbitrary")),
    )(p, p, p, cum)


def _attn_bwd(p, cum, o, lse, do, s):
    t = min(ATT_BLOCK, s)
    nq = s // t
    r = ATT_STRIP

    def body(q_ref, k_ref, v_ref, c_ref, o_ref, lse_ref, do_ref, dq_ref, dk_ref, dv_ref, dc_ref, dr_ref,
             qaugt_sc, qh_sc, dot_sc, doh_sc, delta_sc, dqt_sc, dr_sc, kaug_sc, vh_sc, kt_sc,
             s0_sc, s1_sc, dp0_sc, dp1_sc, p0_sc, p1_sc, ds0_sc, ds1_sc, dk_sc, dv_sc, dc_sc):
        j, ki = pl.program_id(0), pl.program_id(1)
        s_sc, dp_sc, p_sc, ds_sc = (s0_sc, s1_sc), (dp0_sc, dp1_sc), (p0_sc, p1_sc), (ds0_sc, ds1_sc)

        @pl.when(ki == 0)
        def _():
            for c in range(nq):
                rows = slice(c * t, (c + 1) * t)
                q_blk, do_blk = q_ref[rows, :], do_ref[rows, :]
                qaugt_sc[0, :, rows], qaugt_sc[1, :, rows] = _q_augmented_t(q_blk)
                dot_sc[:, rows] = do_blk.T.astype(BF16)
                prod_t = (do_blk * o_ref[rows, :]).T
                delta_sc[:, rows] = _rows01(jnp.sum(prod_t[0:HD], axis=0, keepdims=True),
                                            jnp.sum(prod_t[HD:], axis=0, keepdims=True))
                for h in range(2):
                    head = _head_part((t, 128), h, 1)
                    qh_sc[h, rows, :] = jnp.where(head, q_blk * ATT_SCALE, 0.0).astype(BF16)
                    doh_sc[h, rows, :] = jnp.where(head, do_blk, 0.0).astype(BF16)
            dqt_sc[...] = jnp.zeros(dqt_sc.shape, F32)
            dr_sc[...] = jnp.zeros(dr_sc.shape, F32)

        k_blk, v_blk = k_ref[...], v_ref[...]
        kt = k_blk.T
        for h in range(2):
            kaug_sc[h] = _k_augmented(k_blk, c_ref[...], j, h)
            vh_sc[h] = jnp.where(_head_part((t, 128), h, 1), v_blk, 0.0).astype(BF16)
            kt_sc[h] = kt[h * HD:(h + 1) * HD].astype(BF16)
        dk_sc[...] = jnp.zeros(dk_sc.shape, F32)
        dv_sc[...] = jnp.zeros(dv_sc.shape, F32)
        dc_sc[...] = jnp.zeros(dc_sc.shape, F32)

        def inputs(qb, buf):
            qs = pl.ds(pl.multiple_of(qb * t, t), t)
            for h in range(2):
                s_sc[buf][h] = _dot(kaug_sc[h], qaugt_sc[h, :, qs])
                dp_sc[buf][h] = _dot(vh_sc[h], dot_sc[:, qs])

        def elementwise(qb, buf, diagonal):
            qs = pl.ds(pl.multiple_of(qb * t, t), t)
            for h in range(2):
                lse_row, delta_row = lse_ref[0, h:h + 1, qs], delta_sc[h:h + 1, qs]
                tot = jnp.zeros((8, t), F32)
                for i in range(t // r):
                    rows = slice(i * r, (i + 1) * r)
                    x = s_sc[buf][h, rows, :]
                    if diagonal:
                        x = jnp.where(_iota((r, t), 1) >= i * r + _iota((r, t), 0), x, -1e30)
                    pr = jnp.exp(x - lse_row)
                    ds = pr * (dp_sc[buf][h, rows, :] - delta_row)
                    p_sc[buf][h, rows, :] = pr.astype(BF16)
                    ds_sc[buf][h, rows, :] = ds.astype(BF16)
                    dc_sc[h, rows, :] += sum(ds[:, 128 * g:128 * (g + 1)] for g in range(t // 128))
                    tot = _fold8(ds, jnp.add, tot)
                dr_sc[h, :, qs] += tot

        def outputs(qb, buf):
            qs = pl.ds(pl.multiple_of(qb * t, t), t)
            dv_sc[...] += _dot(p_sc[buf][0], doh_sc[0, qs, :]) + _dot(p_sc[buf][1], doh_sc[1, qs, :])
            dk_sc[...] += _dot(ds_sc[buf][0], qh_sc[0, qs, :]) + _dot(ds_sc[buf][1], qh_sc[1, qs, :])
            for h in range(2):
                dqt_sc[h * HD:(h + 1) * HD, qs] += _dot(kt_sc[h], ds_sc[buf][h])

        def pair(a, b, a_diagonal):
            inputs(a, 0)
            inputs(b, 1)
            elementwise(a, 0, a_diagonal)
            outputs(a, 0)
            elementwise(b, 1, False)
            outputs(b, 1)

        def later(u, carry):
            pair(ki + 1 + 2 * u, ki + 2 + 2 * u, False)
            return carry

        n_later = nq - 1 - ki
        lax.fori_loop(0, n_later // 2, later, 0)

        @pl.when(n_later % 2 == 1)
        def _():
            pair(ki, nq - 1, True)

        @pl.when(n_later % 2 == 0)
        def _():
            inputs(ki, 0)
            elementwise(ki, 0, True)
            outputs(ki, 0)

        dk_ref[...] = dk_sc[...].astype(BF16)
        dv_ref[...] = dv_sc[...].astype(BF16)
        lane = _iota((t, 128), 1)
        cols = jnp.where(lane == 0, jnp.sum(dc_sc[0], axis=1, keepdims=True),
                         jnp.where(lane == 1, jnp.sum(dc_sc[1], axis=1, keepdims=True), 0.0))
        dc_ref[0] = cols.T[0:8, :]

        @pl.when(ki == nq - 1)
        def _():
            for c in range(nq):
                rows = slice(c * t, (c + 1) * t)
                dq_ref[rows, :] = dqt_sc[:, rows].T * ATT_SCALE
            dr_ref[0] = _rows01(jnp.sum(dr_sc[0], axis=0, keepdims=True), jnp.sum(dr_sc[1], axis=0, keepdims=True))

    whole = lambda off: pl.BlockSpec((s, 128), functools.partial(lambda j, ki, off: (0, off + j), off=off))
    return pl.pallas_call(
        body, name="attn_bwd", grid=(NH // 2, nq),
        in_specs=[whole(OFF_Q // 128),
                  pl.BlockSpec((t, 128), lambda j, ki: (ki, OFF_K // 128 + j)),
                  pl.BlockSpec((t, 128), lambda j, ki: (ki, OFF_V // 128 + j)),
                  pl.BlockSpec((t, 128), lambda j, ki: (ki, 0)),
                  whole(0),
                  pl.BlockSpec((1, 8, s), lambda j, ki: (j, 0, 0)),
                  whole(0)],
        out_specs=[whole(0),
                   pl.BlockSpec((t, 128), lambda j, ki: (ki, j)),
                   pl.BlockSpec((t, 128), lambda j, ki: (ki, j)),
                   pl.BlockSpec((1, 8, t), lambda j, ki: (j, 0, ki)),
                   pl.BlockSpec((1, 8, s), lambda j, ki: (j, 0, 0))],
        out_shape=[jax.ShapeDtypeStruct((s, D), F32), jax.ShapeDtypeStruct((s, D), BF16), jax.ShapeDtypeStruct((s, D), BF16),
                   jax.ShapeDtypeStruct((NH // 2, 8, s), F32), jax.ShapeDtypeStruct((NH // 2, 8, s), F32)],
        scratch_shapes=[pltpu.VMEM((2, 128, s), BF16), pltpu.VMEM((2, s, 128), BF16), pltpu.VMEM((128, s), BF16),
                        pltpu.VMEM((2, s, 128), BF16), pltpu.VMEM((8, s), F32), pltpu.VMEM((128, s), F32),
                        pltpu.VMEM((2, 8, s), F32), pltpu.VMEM((2, t, 128), BF16), pltpu.VMEM((2, t, 128), BF16),
                        pltpu.VMEM((2, HD, t), BF16)]
        + [pltpu.VMEM((2, t, t), F32)] * 4 + [pltpu.VMEM((2, t, t), BF16)] * 4
        + [pltpu.VMEM((t, 128), F32), pltpu.VMEM((t, 128), F32), pltpu.VMEM((2, t, 128), F32)],
        compiler_params=_params(("parallel", "arbitrary")),
    )(p, p, p, cum, o, lse, do)


def _ln_stats(u):
    mu = _mean(u)
    d = u - mu
    rstd = lax.rsqrt(_mean(d * d) + EPS)
    return d * rstd, rstd


def _ln_bwd(dx, xh, rstd, gam):
    dxh = dx * gam
    return rstd * (dxh - _mean(dxh) - xh * _mean(dxh * xh))


def _rms_bwd(d, xn, r, w):
    t = d * w
    return r * (t - xn * _mean(t * xn)), _colsum(d * xn)


def _mix_norm(y, p, att, w_ssm, w_att, s):
    def fn(pos, y, z, att, w1, w2):
        g = y * _silu(z)
        n1 = g * lax.rsqrt(_mean(g * g) + EPS) * w1
        n2 = att * lax.rsqrt(_mean(att * att) + EPS) * w2
        return (jnp.concatenate([n1, n2], axis=1),)

    return _rowk("mix_norm", fn, s, 512, [(y, D, 0, 0), (p, D, OFF_Z // D, 0), (att, D, 0, 0)],
                 [w_ssm, w_att], [(2 * D, BF16)], [])[0]


def _mix_norm_bwd(dmix, y, p, att, w_ssm, w_att, s):
    def fn(pos, dmix, y, z, att, w1, w2, a1, a2):
        sz = _silu(z)
        g = y * sz
        r1 = lax.rsqrt(_mean(g * g) + EPS)
        dg, dw1 = _rms_bwd(dmix[:, :D], g * r1, r1, w1)
        r2 = lax.rsqrt(_mean(att * att) + EPS)
        datt, dw2 = _rms_bwd(dmix[:, D:], att * r2, r2, w2)
        return dg * sz, dg * y * _dsilu(z), datt, a1 + dw1, a2 + dw2

    return _rowk("mix_norm_bwd", fn, s, 256, [(dmix, 2 * D, 0, 0), (y, D, 0, 0), (p, D, OFF_Z // D, 0), (att, D, 0, 0)],
                 [w_ssm, w_att], [(D, F32), (D, BF16), (D, F32)], [(1, D), (1, D)])


def _ln1(x0, y, g1, gam, bet, sc2, sh2, s):
    def fn(pos, x0, y, g1, gam, bet, sc2, sh2):
        xh, _ = _ln_stats(ALPHA * x0 + (1.0 + g1) * y)
        x1 = xh * gam + bet
        return x1, _modulate(x1, sc2, sh2)

    return _rowk("ln1", fn, s, 512, [(x0, D, 0, 0), (y, D, 0, 0)], [g1, gam, bet, sc2, sh2], [(D, F32), (D, BF16)], [])


def _ln2_loss(x1, ff, tgt, g2, gam, bet, s):
    def fn(pos, x1, ff, tgt, g2, gam, bet, a_loss, a_dgam, a_dbet, a_dg2):
        xh, rstd = _ln_stats(ALPHA * x1 + (1.0 + g2) * ff)
        err = xh * gam + bet - tgt
        dx2 = err * (1.0 / D)
        du = _ln_bwd(dx2, xh, rstd, gam)
        return (du, du * (1.0 + g2), a_loss + _colsum(err * err), a_dgam + _colsum(dx2 * xh),
                a_dbet + _colsum(dx2), a_dg2 + _colsum(du * ff))

    return _rowk("ln2_loss", fn, s, 512, [(x1, D, 0, 0), (ff, D, 0, 0), (tgt, D, 0, 0)], [g2, gam, bet],
                 [(D, F32), (D, BF16)], [(1, D)] * 4)


def _ln1_bwd(dh2, du2, x0, y, g1, gam, bet, sc2, s):
    def fn(pos, dh2, du2, x0, y, g1, gam, bet, sc2, a_sc, a_sh, a_gam, a_bet, a_g1):
        xh, rstd = _ln_stats(ALPHA * x0 + (1.0 + g1) * y)
        x1 = xh * gam + bet
        dx1 = ALPHA * du2 + dh2 * (1.0 + sc2)
        du1 = _ln_bwd(dx1, xh, rstd, gam)
        return (du1, du1 * (1.0 + g1), a_sc + _colsum(dh2 * x1), a_sh + _colsum(dh2), a_gam + _colsum(dx1 * xh),
                a_bet + _colsum(dx1), a_g1 + _colsum(du1 * y))

    return _rowk("ln1_bwd", fn, s, 512, [(dh2, D, 0, 0), (du2, D, 0, 0), (x0, D, 0, 0), (y, D, 0, 0)],
                 [g1, gam, bet, sc2], [(D, F32), (D, BF16)], [(1, D)] * 5)


def _input_grad(dh1, du1, x0, sc1, s):
    def fn(pos, dh1, du1, x0, sc1, a_sc, a_sh):
        return ALPHA * du1 + dh1 * (1.0 + sc1), a_sc + _colsum(dh1 * x0), a_sh + _colsum(dh1)

    return _rowk("input_grad", fn, s, 512, [(dh1, D, 0, 0), (du1, D, 0, 0), (x0, D, 0, 0)], [sc1],
                 [(D, F32)], [(1, D)] * 2)


def _adamw_math(w, grad, m, v):
    m_new = ADAM_B1 * m + (1.0 - ADAM_B1) * grad
    v_new = ADAM_B2 * v + (1.0 - ADAM_B2) * (grad * grad)
    m_hat = m_new / (1.0 - ADAM_B1 ** ADAM_STEP)
    v_hat = v_new / (1.0 - ADAM_B2 ** ADAM_STEP)
    return -ADAM_LR * (m_hat / (jnp.sqrt(v_hat) + ADAM_EPS) + ADAM_WD * w), m_new, v_new


def _small_update(small_all, layout, w, m, v):
    names = [n for n, _, _ in layout]

    def body(*refs):
        all_ref = refs[0]
        w_refs, m_refs, v_refs = [refs[1 + k * len(names):1 + (k + 1) * len(names)] for k in range(3)]
        sum_ref = refs[1 + 3 * len(names)]
        outs = refs[2 + 3 * len(names):]
        total = all_ref[0]
        for k in range(1, N_DEV):
            total = total + all_ref[k]
        sum_ref[...] = total
        for i, (_, off, size) in enumerate(layout):
            grad = total[:, off:off + size]
            delta, m_new, v_new = _adamw_math(w_refs[i][...], grad, m_refs[i][...], v_refs[i][...])
            for o, val in zip(outs[4 * i:4 * i + 4], (grad, delta, m_new, v_new)):
                o[...] = val

    res = pl.pallas_call(
        body, name="small_update",
        out_shape=[jax.ShapeDtypeStruct(small_all.shape[1:], F32)]
        + [jax.ShapeDtypeStruct(w[n].shape, F32) for n in names for _ in range(4)],
        compiler_params=_params(None),
    )(small_all, *[w[n] for n in names], *[m[n] for n in names], *[v[n] for n in names])
    return res[0], {n: res[1 + 4 * i:5 + 4 * i] for i, n in enumerate(names)}


def _adamw(name, w, g, m, v, *, tr, slots, by_columns=False):
    r, c = w.shape

    def body(w_ref, g_ref, m_ref, v_ref, g_out, d_out, m_out, v_out):
        if slots:
            grad = g_ref[0].astype(F32)
            for k in range(1, N_DEV):
                grad = grad + g_ref[k].astype(F32)
        else:
            grad = g_ref[...]
        g_out[...] = grad
        d_out[...], m_out[...], v_out[...] = _adamw_math(w_ref[...], grad, m_ref[...], v_ref[...])

    if by_columns:
        tile = pl.BlockSpec((r, tr), lambda i: (0, i))
        g_spec = pl.BlockSpec((N_DEV, r, tr), lambda i: (0, 0, i)) if slots else tile
    else:
        tile = pl.BlockSpec((tr, c), lambda i: (i, 0))
        g_spec = pl.BlockSpec((N_DEV, tr, c), lambda i: (0, i, 0)) if slots else tile
    return pl.pallas_call(
        body, name=name, grid=((c if by_columns else r) // tr,),
        in_specs=[tile, g_spec, tile, tile], out_specs=[tile] * 4,
        out_shape=[jax.ShapeDtypeStruct((r, c), F32)] * 4,
        compiler_params=_params(("parallel",)),
    )(w, g, m, v)


def _dot_f32(a, b, dims=NN):
    a0, a1, a2 = _split3(a)
    b0, b1, b2 = _split3(b)
    acc = _dot(a0, b0, dims)
    for x, y in ((a0, b1), (a1, b0), (a1, b1), (a0, b2), (a2, b0)):
        acc = acc + _dot(x, y, dims)
    return acc


def _ada_mod(c_all, w_shard, b_shard):
    def body(c_ref, w_ref, b_ref, o_ref):
        act = _silu(c_ref[...])
        act16 = jnp.concatenate([act, jnp.zeros_like(act)], axis=0)
        o_ref[...] = _dot_f32(act16, w_ref[...])[0:N_DEV] + b_ref[...]

    return pl.pallas_call(
        body, name="ada_mod", out_shape=jax.ShapeDtypeStruct((N_DEV, w_shard.shape[1]), F32),
        compiler_params=_params(None),
    )(c_all, w_shard, b_shard)


def _ada_grad(c_all, dmod_cols):
    def body(c_ref, dc_ref, gw_ref):
        act = _silu(c_ref[...])
        act16 = jnp.concatenate([act, jnp.zeros_like(act)], axis=0)
        dm = dc_ref[...]
        dm16 = jnp.concatenate([dm, jnp.zeros_like(dm)], axis=0)
        gw_ref[...] = _dot_f32(act16, dm16, TN)

    return pl.pallas_call(
        body, name="ada_grad", out_shape=jax.ShapeDtypeStruct((D, dmod_cols.shape[1]), F32),
        compiler_params=_params(None),
    )(c_all, dmod_cols)


def _exchange(name, xs, scatter):
    n = len(xs)
    n_peer = N_DEV - 1

    def body(*refs):
        x_refs, o_refs = refs[:n], refs[n:2 * n]
        send_sems, recv_sems, local_sems = refs[2 * n:]
        mx, my, mc = lax.axis_index("x"), lax.axis_index("y"), lax.axis_index("c")
        me = 4 * mx + 2 * my + mc

        def src(a, slot):
            return x_refs[a].at[slot] if scatter else x_refs[a]

        own = [pltpu.make_async_copy(src(a, me), o_refs[a].at[me], local_sems.at[a]) for a in range(n)]
        for cp in own:
            cp.start()
        sends = []
        for d in range(1, N_DEV):
            px = 1 - mx if d & 4 else mx
            py = 1 - my if d & 2 else my
            pc = 1 - mc if d & 1 else mc
            peer = 4 * px + 2 * py + pc
            for a in range(n):
                def copy(src_slot, dst_slot, a=a, d=d, to=(px, py, pc)):
                    return pltpu.make_async_remote_copy(
                        src_ref=src(a, src_slot), dst_ref=o_refs[a].at[dst_slot],
                        send_sem=send_sems.at[a * n_peer + d - 1], recv_sem=recv_sems.at[a * n_peer + d - 1],
                        device_id=to, device_id_type=pl.DeviceIdType.MESH)

                out = copy(peer, me)
                out.start()
                sends.append((out, copy(me, peer)))
        for _, arrival in sends:
            arrival.wait_recv()
        for out, _ in sends:
            out.wait_send()
        for cp in own:
            cp.wait()

    shapes = [tuple(x.shape[1:] if scatter else x.shape) for x in xs]
    return pl.pallas_call(
        body, name=name,
        in_specs=[pl.BlockSpec(memory_space=pl.ANY)] * n, out_specs=[pl.BlockSpec(memory_space=pl.ANY)] * n,
        out_shape=[jax.ShapeDtypeStruct((N_DEV,) + sh, x.dtype) for sh, x in zip(shapes, xs)],
        scratch_shapes=[pltpu.SemaphoreType.DMA((n * n_peer,)), pltpu.SemaphoreType.DMA((n * n_peer,)),
                        pltpu.SemaphoreType.DMA((n,))],
        compiler_params=pltpu.CompilerParams(has_side_effects=True),
    )(*xs)


def _gather_two_level(name, x):
    def body(x_ref, o_ref, send_sems, recv_sems, local_sem):
        mx, my, mc = lax.axis_index("x"), lax.axis_index("y"), lax.axis_index("c")
        me, sibling = (mx, my, mc), (mx, my, 1 - mc)
        chips = [(1 - mx, my), (mx, 1 - my), (1 - mx, 1 - my)]

        def slot(px, py, pc):
            return o_ref.at[4 * px + 2 * py + pc]

        def copy(k, block, to, src=None):
            return pltpu.make_async_remote_copy(
                src_ref=slot(*block) if src is None else src, dst_ref=slot(*block),
                send_sem=send_sems.at[k], recv_sem=recv_sems.at[k], device_id=to, device_id_type=pl.DeviceIdType.MESH)

        mine = pltpu.make_async_copy(x_ref, slot(*me), local_sem)
        mine.start()
        first = [copy(0, me, sibling, src=x_ref)] + [copy(1 + i, me, (*chip, mc), src=x_ref) for i, chip in enumerate(chips)]
        for cp in first:
            cp.start()
        passed = [copy(4 + i, (*chip, mc), sibling) for i, chip in enumerate(chips)]
        for i, chip in enumerate(chips):
            copy(1 + i, (*chip, mc), me).wait_recv()
            passed[i].start()
        copy(0, sibling, me).wait_recv()
        for i, chip in enumerate(chips):
            copy(4 + i, (*chip, 1 - mc), me).wait_recv()
        for cp in first + passed:
            cp.wait_send()
        mine.wait()

    return pl.pallas_call(
        body, name=name,
        in_specs=[pl.BlockSpec(memory_space=pl.ANY)], out_specs=pl.BlockSpec(memory_space=pl.ANY),
        out_shape=jax.ShapeDtypeStruct((N_DEV,) + tuple(x.shape), x.dtype),
        scratch_shapes=[pltpu.SemaphoreType.DMA((7,)), pltpu.SemaphoreType.DMA((7,)), pltpu.SemaphoreType.DMA(())],
        compiler_params=pltpu.CompilerParams(has_side_effects=True),
    )(x)


def _after(x, zero):
    return x if zero is None else x + zero.reshape(-1)[0].astype(x.dtype)


def _exchange_copies(x_refs, land_refs, send_sems, recv_sems, scatter):
    n = len(x_refs)
    n_peer = N_DEV - 1
    mx, my, mc = lax.axis_index("x"), lax.axis_index("y"), lax.axis_index("c")
    me = 4 * mx + 2 * my + mc
    pairs = []
    for d in range(1, N_DEV):
        px = 1 - mx if d & 4 else mx
        py = 1 - my if d & 2 else my
        pc = 1 - mc if d & 1 else mc
        peer = 4 * px + 2 * py + pc
        for a in range(n):
            def copy(src_slot, dst_slot, a=a, d=d, to=(px, py, pc)):
                return pltpu.make_async_remote_copy(
                    src_ref=x_refs[a].at[src_slot] if scatter else x_refs[a], dst_ref=land_refs[a].at[dst_slot],
                    send_sem=send_sems.at[a * n_peer + d - 1], recv_sem=recv_sems.at[a * n_peer + d - 1],
                    device_id=to, device_id_type=pl.DeviceIdType.MESH)

            pairs.append((copy(peer, me), copy(me, peer)))
    return me, pairs


def _exchange_async(name, xs, scatter, collective_id):
    n = len(xs)
    shapes = [tuple(x.shape[1:] if scatter else x.shape) for x in xs]
    x_refs = [jax.new_ref(x, memory_space=pltpu.MemorySpace.HBM) for x in xs]
    land_refs = [jax.empty_ref(jax.ShapeDtypeStruct((N_DEV,) + sh, x.dtype), memory_space=pltpu.MemorySpace.HBM)
                 for sh, x in zip(shapes, xs)]

    @pl.kernel(mesh=plsc.ScalarSubcoreMesh(axis_name="sequencer", num_cores=1), name=name,
               scratch_types=(pltpu.SemaphoreType.DMA((n * (N_DEV - 1),)), pltpu.SemaphoreType.DMA((n * (N_DEV - 1),)),
                              pltpu.SemaphoreType.DMA((n,))),
               compiler_params=pltpu.CompilerParams(collective_id=collective_id))
    def launch(send_sems, recv_sems, own_sems):
        barrier = pltpu.get_barrier_semaphore()
        mx, my, mc = lax.axis_index("x"), lax.axis_index("y"), lax.axis_index("c")
        for d in range(1, N_DEV):
            peer = (1 - mx if d & 4 else mx, 1 - my if d & 2 else my, 1 - mc if d & 1 else mc)
            pl.semaphore_signal(barrier, inc=1, device_id=peer, device_id_type=pl.DeviceIdType.MESH)
        pl.semaphore_wait(barrier, N_DEV - 1)
        me, pairs = _exchange_copies(x_refs, land_refs, send_sems, recv_sems, scatter)
        own = [pltpu.make_async_copy(x_refs[a].at[me] if scatter else x_refs[a], land_refs[a].at[me], own_sems.at[a])
               for a in range(n)]
        for cp in own:
            cp.start()
        for out, _ in pairs:
            out.start()
        for out, arrival in pairs:
            arrival.wait_recv()
            out.wait_send()
        for cp in own:
            cp.wait()

    launch()
    return lambda: [r[...] for r in land_refs]


def _relu2(a):
    r = jnp.maximum(a, 0.0)
    return r * r


def _relu2_grad(acc, r):
    return acc * (2.0 * jnp.sqrt(r.astype(F32)))


def _local_step(x0, tgt, mod, wcat_t, late_weights, send_grads, conv_w, conv_b, dt_bias, a_log, d_skip, ssm_norm_w, f_bias,
                attn_norm_w, ln1_g, ln1_b, ln2_g, ln2_b):
    ff_w = DFF // N_DEV
    s = x0.shape[0]
    tm = min(1024, s)
    ts = min(2048, s)
    sh1, sc1, g1, sh2, sc2, g2 = [mod[:, i * D:(i + 1) * D] for i in range(6)]
    zero = jnp.zeros((1, 128 - 2 * NH), F32)
    bias128 = jnp.concatenate([dt_bias, f_bias, zero], axis=1)
    alog128 = jnp.concatenate([a_log, jnp.zeros((1, 128 - NH), F32)], axis=1)
    dskip_x = jnp.repeat(d_skip, HD, axis=1)
    w_xs, w_bc, b_xs, b_bc = conv_w[:, :D], conv_w[:, D:], conv_b[:, :D], conv_b[:, D:]

    h1, = _rowk("modulate", lambda pos, x, sc, sh: (_modulate(x, sc, sh),), s, 512, [(x0, D, 0, 0)], [sc1, sh1], [(D, BF16)], [])
    p = _mm_nt("in_proj", [(h1, D, 0)], [(wcat_t, D, 0)], n=PCOLS, tm=tm, tn=1152, out_dtype=F32)
    xs_a, bc_a = _conv_fwd(p, w_xs, b_xs, w_bc, b_bc, s)
    y_ssd, states = _ssd_fwd(xs_a, bc_a, p, bias128, alog128, dskip_x, s)
    cum = _cum_fwd(p, bias128, s)
    att, lse = _attn_fwd(p, cum, s)
    wout, w1s, w2 = late_weights()
    ymix = _mix_norm(y_ssd, p, att, ssm_norm_w, attn_norm_w, s)
    y = _mm_nn("out_proj", ymix, wout, tm=tm, tn=1024, tk=2 * D, out_dtype=F32)
    x1, h2 = _ln1(x0, y, g1, ln1_g, ln1_b, sc2, sh2, s)
    tall = min(2048, s)
    r = _mm_nn("ff_in", h2, w1s, tm=tall, tn=2 * ff_w, tk=D, out_dtype=BF16, epi=_relu2)
    ff = _mm_nn("ff_out", r, w2, tm=tall, tn=1024, tk=1024, out_dtype=F32)
    du2, dff, sq_err, d_ln2_g, d_ln2_b, d_g2 = _ln2_loss(x1, ff, tgt, g2, ln2_g, ln2_b, s)

    da1 = _mm_nt("d_ff_hidden", [(dff, D, 0)], [(w2, D, 0)], n=DFF, tm=tm, tn=1024, out_dtype=BF16, epi=_relu2_grad,
                 epi_aux=(r,))
    d_w2 = _mm_tn("d_w_ff_out", r, dff, tm=1024, tn=1024, ts=ts)
    d_w1s = _mm_tn("d_w_ff_in", h2, da1, tm=1024, tn=2 * ff_w, ts=ts, col_shards=True)
    dh2 = _mm_nt("d_ff_input", [(da1, ff_w, k) for k in range(N_DEV)], [(w1s, ff_w, k) for k in range(N_DEV)], n=D,
                 tm=tm, tn=1024, out_dtype=F32)
    du1, dy, d_sc2, d_sh2, d_ln1_g, d_ln1_b, d_g1 = _ln1_bwd(dh2, du2, x0, y, g1, ln1_g, ln1_b, sc2, s)

    dmix = _mm_nt("d_mix", [(dy, D, 0)], [(wout, D, 0)], n=2 * D, tm=tm, tn=1024, out_dtype=F32)
    d_wout = _mm_tn("d_w_out", ymix, dy, tm=1024, tn=1024, ts=ts)
    sent = send_grads("late", [d_w1s, d_w2.reshape(N_DEV, -1, D), d_wout.reshape(N_DEV, -1, D)])
    dy_ssd, dz, datt, d_ssm_w, d_attn_w = _mix_norm_bwd(dmix, y_ssd, p, att, _after(ssm_norm_w, sent), attn_norm_w, s)
    dq, dk, dv, dcs, drs = _attn_bwd(p, cum, att, lse, datt, s)
    dxs_a, dbc_a, ddt_raw, d_alog, d_dskip = _ssd_bwd(dy_ssd, xs_a, bc_a, p, states, bias128, alog128, dskip_x, s)
    dcum = jnp.pad((drs - dcs)[:, :2, :].reshape(NH, s).T, ((0, 0), (NH, 128 - 2 * NH)))
    ddtf, _, d_bias = _cum_bwd(dcum, ddt_raw, p, bias128, s)
    dxs, dbc, d_wc_xs, d_bc_xs, d_wc_bc, d_bc_bc = _conv_bwd(dxs_a, dbc_a, p, w_xs, b_xs, w_bc, b_bc, s)

    segs = [(dz, OFF_Z, D), (dxs, OFF_XS, D), (dq, OFF_Q, D), (dk, OFF_K, D), (dv, OFF_V, D), (dbc, OFF_BC, 512),
            (ddtf, OFF_DTF, 128)]
    d_z, d_xs, d_q, d_k, d_v, d_bcw, d_dtf = [
        _mm_tn("d_w_in_%d" % i, a, h1, tm=min(w, 1024), tn=1024, ts=ts)
        for i, (a, _, w) in enumerate(segs)]
    d_w_in_t = dict(z=d_z, xs=d_xs, bc=d_bcw, dt=d_dtf[:NH], q=d_q, k=d_k, v=d_v, f=d_dtf[NH:2 * NH])
    sent = send_grads("in", [_shard_w_in_grad_t(d_w_in_t)])
    segs[-1] = (_after(ddtf, sent), OFF_DTF, 128)
    dh1 = _mm_nt("d_h1", [(a, w, 0) for a, _, w in segs], [(wcat_t, w, off // w) for _, off, w in segs], n=D,
                 tm=min(512, s), tn=1024, out_dtype=F32, b_rows=True)
    grad_x, d_sc1, d_sh1 = _input_grad(dh1, du1, x0, sc1, s)

    return dict(
        loss=(0.5 / D) * jnp.sum(sq_err), grad_x=grad_x,
        d_mod=jnp.concatenate([d_sh1, d_sc1, d_g1, d_sh2, d_sc2, d_g2], axis=1),
        d_conv_w=jnp.concatenate([d_wc_xs[:4], d_wc_bc[:4]], axis=1), d_conv_b=jnp.concatenate([d_bc_xs, d_bc_bc], axis=1),
        d_ssm_norm_w=d_ssm_w, d_attn_norm_w=d_attn_w, d_ln1_g=d_ln1_g, d_ln1_b=d_ln1_b, d_ln2_g=d_ln2_g, d_ln2_b=d_ln2_b,
        d_gate_bias=d_bias, d_a_log=d_alog, d_d_skip=d_dskip)


W_IN_SEGS = [('z', W_Z, D), ('xs', W_XS, D), ('bc', W_BC, 512), ('dt', W_DT, NH), ('q', W_Q, D), ('k', W_K, D),
             ('v', W_V, D), ('f', W_F, NH)]
SHARD_W = IN_COLS // N_DEV


def _pack_w_in_t(w_in_t):
    seg = {n: w_in_t[off:off + w] for n, off, w in W_IN_SEGS}
    return jnp.concatenate([seg['z'], seg['xs'], seg['q'], seg['k'], seg['v'], seg['bc'], seg['dt'], seg['f'],
                            jnp.zeros((128 - 2 * NH, D), w_in_t.dtype)], axis=0)


def _shard_w_in_grad_t(d_w_in_t):
    blocks = []
    for dev in range(N_DEV):
        lo, hi = dev * SHARD_W, (dev + 1) * SHARD_W
        pieces = [d_w_in_t[n][max(lo, off) - off:min(hi, off + w) - off] for n, off, w in W_IN_SEGS
                  if max(lo, off) < min(hi, off + w)]
        blocks.append(jnp.concatenate(pieces, axis=0))
    return jnp.stack(blocks, axis=0)


WEIGHTS = ['w_ada', 'b_ada', 'w_in', 'conv_w', 'conv_b', 'dt_bias', 'a_log', 'd_skip', 'ssm_norm_w', 'f_bias',
           'attn_norm_w', 'w_out', 'ln1_g', 'ln1_b', 'w_ff_in', 'w_ff_out', 'ln2_g', 'ln2_b']
BIG = ['w_in', 'w_out', 'w_ff_in', 'w_ff_out']
SMALL_LAYOUT = [('b_ada', 0, 6 * D), ('conv_b', 12288, 1536), ('ssm_norm_w', 13824, D), ('attn_norm_w', 14848, D),
                ('ln1_g', 15872, D), ('ln1_b', 16896, D), ('ln2_g', 17920, D), ('ln2_b', 18944, D),
                ('dt_bias', 19968, NH), ('f_bias', 19968 + NH, NH), ('a_log', 20096, NH), ('d_skip', 20224, NH)]
SMALL_LOSS_LANE = 20352


def _pad_lanes(v, n=128):
    return jnp.pad(v, ((0, 0), (0, n - v.shape[1])))


def kernel(x, c, w_ada, b_ada, w_in, conv_w, conv_b, dt_bias, a_log, d_skip, ssm_norm_w, f_bias, attn_norm_w, w_out, ln1_g, ln1_b, w_ff_in, w_ff_out, ln2_g, ln2_b, loss_target, m_w_ada, m_b_ada, m_w_in, m_conv_w, m_conv_b, m_dt_bias, m_a_log, m_d_skip, m_ssm_norm_w, m_f_bias, m_attn_norm_w, m_w_out, m_ln1_g, m_ln1_b, m_w_ff_in, m_w_ff_out, m_ln2_g, m_ln2_b, v_w_ada, v_b_ada, v_w_in, v_conv_w, v_conv_b, v_dt_bias, v_a_log, v_d_skip, v_ssm_norm_w, v_f_bias, v_attn_norm_w, v_w_out, v_ln1_g, v_ln1_b, v_w_ff_in, v_w_ff_out, v_ln2_g, v_ln2_b):
    args = dict(locals())
    w = {n: args[n] for n in WEIGHTS}
    m = {n: args['m_' + n] for n in WEIGHTS}
    v = {n: args['v_' + n] for n in WEIGHTS}
    me = 4 * lax.axis_index("x") + 2 * lax.axis_index("y") + lax.axis_index("c")
    ada_cols = 6 * D // N_DEV
    conv_cols = conv_w.shape[2]

    c_all, conv_all = _exchange("gather_cond", [c, conv_w[0]], False)
    c_all = c_all.reshape(N_DEV, D)
    conv_w_full = conv_all.transpose(1, 0, 2).reshape(4, N_DEV * conv_cols)
    b_shard = lax.dynamic_slice(b_ada, (0, me * ada_cols), (1, ada_cols))
    mod_all, = _exchange("gather_mod", [_ada_mod(c_all, w_ada[0], b_shard)], False)
    mod = lax.dynamic_index_in_dim(mod_all, me, axis=1, keepdims=False).reshape(1, 6 * D)

    w_in_t = _after(jnp.swapaxes(w_in[0], 0, 1).astype(BF16), mod * 0)
    win_s = _gather_two_level("gather_w_in", w_in_t)
    first_done = win_s[0, 0:1, 0:1] * 0
    rest = _exchange_async("gather_rest", [_after(w[n][0].astype(BF16), first_done) for n in BIG[1:]], False, 1)

    def late_weights():
        wout_s, w1s, w2_s = rest()
        return wout_s.reshape(2 * D, D), w1s, w2_s.reshape(DFF, D)

    sends = {}

    def send_grads(tag, blocks):
        sends[tag] = _exchange_async("scatter_" + tag, blocks, True, {'late': 2, 'in': 3}[tag])
        return sum(b.reshape(-1)[0].astype(F32) * 0 for b in blocks)

    out = _local_step(x[0], loss_target[0], mod, _pack_w_in_t(win_s.reshape(IN_COLS, D)), late_weights, send_grads,
                      conv_w_full, conv_b, dt_bias, a_log, d_skip, ssm_norm_w, f_bias, attn_norm_w, ln1_g, ln1_b, ln2_g, ln2_b)

    small = jnp.concatenate(
        [out['d_mod'], out['d_conv_w'].reshape(1, -1), out['d_conv_b'], out['d_ssm_norm_w'], out['d_attn_norm_w'],
         out['d_ln1_g'], out['d_ln1_b'], out['d_ln2_g'], out['d_ln2_b'], out['d_gate_bias'], out['d_a_log'],
         out['d_d_skip'], _pad_lanes(out['loss'].reshape(1, 1))], axis=1)
    small_landed = _exchange_async("gather_small", [small], False, 4)
    (g_ff_in, g_ff_out, g_out), (g_in,) = sends['late'](), sends['in']()
    g_parts = dict(w_ff_in=g_ff_in, w_ff_out=g_ff_out, w_out=g_out, w_in=g_in)
    big = {n: _adamw("adamw_" + n, w[n][0], g_parts[n], m[n][0], v[n][0], tr=256, slots=True) for n in BIG[1:]}
    t = lambda a: jnp.swapaxes(a[0], 0, 1)
    big['w_in'] = [jnp.swapaxes(r, 0, 1) for r in _adamw("adamw_w_in", t(w_in), g_parts['w_in'], t(m_w_in), t(v_w_in),
                                                         tr=256, slots=True, by_columns=True)]
    big_done = sum(big[n][1][0:1, 0:1] * 0 for n in BIG)
    small_all = _after(small_landed()[0], big_done)
    ssum, small_res = _small_update(small_all, SMALL_LAYOUT, w, m, v)
    dmod_all = small_all[:, 0, :6 * D]
    g_w_ada = _ada_grad(c_all, lax.dynamic_slice(dmod_all, (0, me * ada_cols), (N_DEV, ada_cols)))
    ada = _adamw("adamw_ada", w_ada[0], g_w_ada, m_w_ada[0], v_w_ada[0], tr=256, slots=False)
    g_conv_w = lax.dynamic_slice(ssum[:, 6 * D:6 * D + 4 * N_DEV * conv_cols].reshape(4, N_DEV * conv_cols),
                                 (0, me * conv_cols), (4, conv_cols))
    conv = _adamw("adamw_conv_w", conv_w[0], g_conv_w, m_conv_w[0], v_conv_w[0], tr=4, slots=False)

    results = []
    for k in range(4):
        vals = {n: small_res[n][k] for n in small_res}
        vals['w_ada'], vals['conv_w'] = ada[k][None], conv[k][None]
        for n in BIG:
            vals[n] = big[n][k][None]
        results.append(vals)
    return (ssum[0, SMALL_LOSS_LANE], out['grad_x'][None], *[res[n] for res in results for n in WEIGHTS])
```
